```python
import math
import jax, jax.numpy as jnp
from jax import lax
import numpy as np

D_MODEL = 1024
BATCH = 8
SEQ = 4096
DEPTH = 2

N_A = DEPTH // 2
N_B = DEPTH - N_A
CONV_W = 3
HEAD_DIM = 64
N_HEADS = D_MODEL // HEAD_DIM
N_KV_HEADS = 4
GROUP = N_HEADS // N_KV_HEADS
WINDOW = 128
BLOCK = 128
ROT_DIM = HEAD_DIM // 4
ROPE_THETA = 500000.0
D_FF = ((8 * D_MODEL // 3 + 255) // 256) * 256
EPS = 1e-6
NEG = -1e30

kernel_name = "yoco_shortconv_swa_sink_hybrid"


def rmsnorm(x, g):
    xf = x.astype(jnp.float32)
    r = lax.rsqrt(jnp.mean(xf * xf, axis=-1, keepdims=True) + EPS)
    return (xf * r).astype(x.dtype) * g


def rope_tables(seq_len, dtype):
    inv_freq = ROPE_THETA ** (-jnp.arange(0, ROT_DIM, 2, dtype=jnp.float32) / ROT_DIM)
    ang = jnp.arange(seq_len, dtype=jnp.float32)[:, None] * inv_freq[None, :]
    return jnp.cos(ang)[:, None, :].astype(dtype), jnp.sin(ang)[:, None, :].astype(dtype)


def partial_rotary(t, cos, sin):
    half = ROT_DIM // 2
    t1, t2, rest = t[..., :half], t[..., half:ROT_DIM], t[..., ROT_DIM:]
    return jnp.concatenate([t1 * cos - t2 * sin, t2 * cos + t1 * sin, rest], axis=-1)


def causal_depthwise_conv3(u, w):
    s = u.shape[1]
    up = jnp.pad(u, ((0, 0), (CONV_W - 1, 0), (0, 0)))
    return up[:, 0:s] * w[0] + up[:, 1:s + 1] * w[1] + up[:, 2:s + 2] * w[2]


def short_conv_mixer(h, w_in, conv_w, w_out):
    bcx = h @ w_in
    b_gate, c_gate, u = jnp.split(bcx, 3, axis=-1)
    y = b_gate * causal_depthwise_conv3(c_gate * u, conv_w)
    return y @ w_out


def swiglu(h, w_gate_up, w_down):
    g, u = jnp.split(h @ w_gate_up, 2, axis=-1)
    return (jax.nn.silu(g) * u) @ w_down


def sliding_window_sink_attention(q, k, v, sinks):
    bsz, s = q.shape[0], q.shape[1]
    nb = s // BLOCK
    qb = q.reshape(bsz, nb, BLOCK, N_KV_HEADS, GROUP, HEAD_DIM)

    def with_prev(t):
        tb = t.reshape(bsz, nb, BLOCK, N_KV_HEADS, HEAD_DIM)
        prev = jnp.concatenate([jnp.zeros_like(tb[:, :1]), tb[:, :-1]], axis=1)
        return jnp.concatenate([prev, tb], axis=2)

    kk, vv = with_prev(k), with_prev(v)
    scale = 1.0 / math.sqrt(HEAD_DIM)
    scores = jnp.einsum('bnqhgd,bnkhd->bnhgqk', qb, kk).astype(jnp.float32) * scale

    qi = jnp.arange(BLOCK)[:, None]
    kj = jnp.arange(2 * BLOCK)[None, :]
    diff = BLOCK + qi - kj
    band = (diff >= 0) & (diff < WINDOW)
    not_pad = (jnp.arange(nb)[:, None, None] > 0) | (kj[None] >= BLOCK)
    valid = band[None] & not_pad
    scores = jnp.where(valid[None, :, None, None], scores, NEG)

    sink = jnp.broadcast_to(
        sinks.astype(jnp.float32).reshape(N_KV_HEADS, GROUP)[None, None, :, :, None, None],
        scores.shape[:-1] + (1,))
    probs = jax.nn.softmax(jnp.concatenate([scores, sink], axis=-1), axis=-1)[..., :-1]
    out = jnp.einsum('bnhgqk,bnkhd->bnqhgd', probs.astype(v.dtype), vv)
    return out.reshape(bsz, s, N_HEADS * HEAD_DIM)


def _fwd_setup_inputs(seed: int = 0) -> dict:
    key = jax.random.key(seed)
    ks = jax.random.split(key, 24)
    f32 = jnp.float32
    D, F = D_MODEL, D_FF
    QD = N_HEADS * HEAD_DIM
    KVD = N_KV_HEADS * HEAD_DIM

    def nrm(k, shape, fan_in):
        return jax.random.normal(k, shape, f32) * (fan_in ** -0.5)

    def gain(k, shape):
        return 1.0 + 0.05 * jax.random.normal(k, shape, f32)

    return {
        "x": jax.random.normal(ks[0], (BATCH, SEQ, D), f32),
        "a_pre_norm": gain(ks[1], (N_A, D)),
        "a_w_in": nrm(ks[2], (N_A, D, 3 * D), D),
        "a_conv_w": nrm(ks[3], (N_A, CONV_W, D), CONV_W),
        "a_w_out": nrm(ks[4], (N_A, D, D), D),
        "a_post_norm": gain(ks[5], (N_A, D)),
        "ffn_pre_norm": gain(ks[6], (DEPTH, D)),
        "ffn_w_gate_up": nrm(ks[7], (DEPTH, D, 2 * F), D),
        "ffn_w_down": nrm(ks[8], (DEPTH, F, D), F),
        "ffn_post_norm": gain(ks[9], (DEPTH, D)),
        "kv_norm": gain(ks[10], (D,)),
        "w_kv": nrm(ks[11], (D, 2 * KVD), D),
        "b_pre_norm": gain(ks[12], (N_B, D)),
        "b_w_q": nrm(ks[13], (N_B, D, QD), D),
        "b_sinks": 0.5 * jax.random.normal(ks[14], (N_B, N_HEADS), f32),
        "b_w_o": nrm(ks[15], (N_B, QD, D), QD),
        "b_post_norm": gain(ks[16], (N_B, D)),
    }


def _fwd_reference(x, a_pre_norm, a_w_in, a_conv_w, a_w_out, a_post_norm,
              ffn_pre_norm, ffn_w_gate_up, ffn_w_down, ffn_post_norm,
              kv_norm, w_kv,
              b_pre_norm, b_w_q, b_sinks, b_w_o, b_post_norm):
    bsz, s, _ = x.shape
    cos, sin = rope_tables(s, x.dtype)
    h = x
    for l in range(DEPTH):
        if l < N_A:
            mix = short_conv_mixer(rmsnorm(h, a_pre_norm[l]), a_w_in[l], a_conv_w[l], a_w_out[l])
            h = h + rmsnorm(mix, a_post_norm[l])
        else:
            j = l - N_A
            if j == 0:
                kv = rmsnorm(h, kv_norm) @ w_kv
                k_sh, v_sh = jnp.split(kv, 2, axis=-1)
                k_sh = partial_rotary(k_sh.reshape(bsz, s, N_KV_HEADS, HEAD_DIM), cos, sin)
                v_sh = v_sh.reshape(bsz, s, N_KV_HEADS, HEAD_DIM)
            q = (rmsnorm(h, b_pre_norm[j]) @ b_w_q[j]).reshape(bsz, s, N_HEADS, HEAD_DIM)
            q = partial_rotary(q, cos, sin)
            attn = sliding_window_sink_attention(q, k_sh, v_sh, b_sinks[j]) @ b_w_o[j]
            h = h + rmsnorm(attn, b_post_norm[j])
        ff = swiglu(rmsnorm(h, ffn_pre_norm[l]), ffn_w_gate_up[l], ffn_w_down[l])
        h = h + rmsnorm(ff, ffn_post_norm[l])
    return h


import jax as _jax
import jax.numpy as _jnp

TWIN_FORMAT = 'train_step'
FWD_PARAMS = ['x', 'a_pre_norm', 'a_w_in', 'a_conv_w', 'a_w_out', 'a_post_norm', 'ffn_pre_norm', 'ffn_w_gate_up', 'ffn_w_down', 'ffn_post_norm', 'kv_norm', 'w_kv', 'b_pre_norm', 'b_w_q', 'b_sinks', 'b_w_o', 'b_post_norm']
TWIN_WEIGHTS = ['a_pre_norm', 'a_w_in', 'a_conv_w', 'a_w_out', 'a_post_norm', 'ffn_pre_norm', 'ffn_w_gate_up', 'ffn_w_down', 'ffn_post_norm', 'kv_norm', 'w_kv', 'b_pre_norm', 'b_w_q', 'b_sinks', 'b_w_o', 'b_post_norm']
TWIN_DIFF_INPUT = 'x'
TWIN_INPUTS = ['x', 'a_pre_norm', 'a_w_in', 'a_conv_w', 'a_w_out', 'a_post_norm', 'ffn_pre_norm', 'ffn_w_gate_up', 'ffn_w_down', 'ffn_post_norm', 'kv_norm', 'w_kv', 'b_pre_norm', 'b_w_q', 'b_sinks', 'b_w_o', 'b_post_norm', 'loss_target', 'm_a_pre_norm', 'm_a_w_in', 'm_a_conv_w', 'm_a_w_out', 'm_a_post_norm', 'm_ffn_pre_norm', 'm_ffn_w_gate_up', 'm_ffn_w_down', 'm_ffn_post_norm', 'm_kv_norm', 'm_w_kv', 'm_b_pre_norm', 'm_b_w_q', 'm_b_sinks', 'm_b_w_o', 'm_b_post_norm', 'v_a_pre_norm', 'v_a_w_in', 'v_a_conv_w', 'v_a_w_out', 'v_a_post_norm', 'v_ffn_pre_norm', 'v_ffn_w_gate_up', 'v_ffn_w_down', 'v_ffn_post_norm', 'v_kv_norm', 'v_w_kv', 'v_b_pre_norm', 'v_b_w_q', 'v_b_sinks', 'v_b_w_o', 'v_b_post_norm']
TWIN_OUTPUTS = ['loss', 'grad_x', 'grad_a_pre_norm', 'grad_a_w_in', 'grad_a_conv_w', 'grad_a_w_out', 'grad_a_post_norm', 'grad_ffn_pre_norm', 'grad_ffn_w_gate_up', 'grad_ffn_w_down', 'grad_ffn_post_norm', 'grad_kv_norm', 'grad_w_kv', 'grad_b_pre_norm', 'grad_b_w_q', 'grad_b_sinks', 'grad_b_w_o', 'grad_b_post_norm', 'delta_a_pre_norm', 'delta_a_w_in', 'delta_a_conv_w', 'delta_a_w_out', 'delta_a_post_norm', 'delta_ffn_pre_norm', 'delta_ffn_w_gate_up', 'delta_ffn_w_down', 'delta_ffn_post_norm', 'delta_kv_norm', 'delta_w_kv', 'delta_b_pre_norm', 'delta_b_w_q', 'delta_b_sinks', 'delta_b_w_o', 'delta_b_post_norm', 'new_m_a_pre_norm', 'new_m_a_w_in', 'new_m_a_conv_w', 'new_m_a_w_out', 'new_m_a_post_norm', 'new_m_ffn_pre_norm', 'new_m_ffn_w_gate_up', 'new_m_ffn_w_down', 'new_m_ffn_post_norm', 'new_m_kv_norm', 'new_m_w_kv', 'new_m_b_pre_norm', 'new_m_b_w_q', 'new_m_b_sinks', 'new_m_b_w_o', 'new_m_b_post_norm', 'new_v_a_pre_norm', 'new_v_a_w_in', 'new_v_a_conv_w', 'new_v_a_w_out', 'new_v_a_post_norm', 'new_v_ffn_pre_norm', 'new_v_ffn_w_gate_up', 'new_v_ffn_w_down', 'new_v_ffn_post_norm', 'new_v_kv_norm', 'new_v_w_kv', 'new_v_b_pre_norm', 'new_v_b_w_q', 'new_v_b_sinks', 'new_v_b_w_o', 'new_v_b_post_norm']
TWIN_LEAF_KINDS = {'loss': 'loss', 'grad_x': 'grad_x', 'grad_a_pre_norm': 'grad_w', 'grad_a_w_in': 'grad_w', 'grad_a_conv_w': 'grad_w', 'grad_a_w_out': 'grad_w', 'grad_a_post_norm': 'grad_w', 'grad_ffn_pre_norm': 'grad_w', 'grad_ffn_w_gate_up': 'grad_w', 'grad_ffn_w_down': 'grad_w', 'grad_ffn_post_norm': 'grad_w', 'grad_kv_norm': 'grad_w', 'grad_w_kv': 'grad_w', 'grad_b_pre_norm': 'grad_w', 'grad_b_w_q': 'grad_w', 'grad_b_sinks': 'grad_w', 'grad_b_w_o': 'grad_w', 'grad_b_post_norm': 'grad_w', 'delta_a_pre_norm': 'delta_w', 'delta_a_w_in': 'delta_w', 'delta_a_conv_w': 'delta_w', 'delta_a_w_out': 'delta_w', 'delta_a_post_norm': 'delta_w', 'delta_ffn_pre_norm': 'delta_w', 'delta_ffn_w_gate_up': 'delta_w', 'delta_ffn_w_down': 'delta_w', 'delta_ffn_post_norm': 'delta_w', 'delta_kv_norm': 'delta_w', 'delta_w_kv': 'delta_w', 'delta_b_pre_norm': 'delta_w', 'delta_b_w_q': 'delta_w', 'delta_b_sinks': 'delta_w', 'delta_b_w_o': 'delta_w', 'delta_b_post_norm': 'delta_w', 'new_m_a_pre_norm': 'new_m', 'new_m_a_w_in': 'new_m', 'new_m_a_conv_w': 'new_m', 'new_m_a_w_out': 'new_m', 'new_m_a_post_norm': 'new_m', 'new_m_ffn_pre_norm': 'new_m', 'new_m_ffn_w_gate_up': 'new_m', 'new_m_ffn_w_down': 'new_m', 'new_m_ffn_post_norm': 'new_m', 'new_m_kv_norm': 'new_m', 'new_m_w_kv': 'new_m', 'new_m_b_pre_norm': 'new_m', 'new_m_b_w_q': 'new_m', 'new_m_b_sinks': 'new_m', 'new_m_b_w_o': 'new_m', 'new_m_b_post_norm': 'new_m', 'new_v_a_pre_norm': 'new_v', 'new_v_a_w_in': 'new_v', 'new_v_a_conv_w': 'new_v', 'new_v_a_w_out': 'new_v', 'new_v_a_post_norm': 'new_v', 'new_v_ffn_pre_norm': 'new_v', 'new_v_ffn_w_gate_up': 'new_v', 'new_v_ffn_w_down': 'new_v', 'new_v_ffn_post_norm': 'new_v', 'new_v_kv_norm': 'new_v', 'new_v_w_kv': 'new_v', 'new_v_b_pre_norm': 'new_v', 'new_v_b_w_q': 'new_v', 'new_v_b_sinks': 'new_v', 'new_v_b_w_o': 'new_v', 'new_v_b_post_norm': 'new_v'}


def _forward(args):
    return _fwd_reference(*[args[k] for k in FWD_PARAMS])


def _output_shape():
    def fwd():
        inp = _fwd_setup_inputs(0)
        return _fwd_reference(*[inp[k] for k in FWD_PARAMS])
    out = _jax.eval_shape(fwd)
    return out.shape, out.dtype

N_MICROBATCH = 1
ADAM_LR = 0.001
ADAM_B1 = 0.9
ADAM_B2 = 0.999
ADAM_EPS = 1e-08
ADAM_WD = 0.01
ADAM_STEP = 10
PER_EXAMPLE_BATCH_AXIS = {'x': 0, 'loss_target': 0}
SHARED_INPUTS = []
_WEIGHT_DTYPES = {'a_pre_norm': _jnp.float32, 'a_w_in': _jnp.float32, 'a_conv_w': _jnp.float32, 'a_w_out': _jnp.float32, 'a_post_norm': _jnp.float32, 'ffn_pre_norm': _jnp.float32, 'ffn_w_gate_up': _jnp.float32, 'ffn_w_down': _jnp.float32, 'ffn_post_norm': _jnp.float32, 'kv_norm': _jnp.float32, 'w_kv': _jnp.float32, 'b_pre_norm': _jnp.float32, 'b_w_q': _jnp.float32, 'b_sinks': _jnp.float32, 'b_w_o': _jnp.float32, 'b_post_norm': _jnp.float32}
MOMENT_SCALE = {'a_pre_norm': 1.734169e+00, 'a_w_in': 9.944421e-01, 'a_conv_w': 1.035325e+00, 'a_w_out': 1.109507e+00, 'a_post_norm': 3.185348e+01, 'ffn_pre_norm': 9.705332e-01, 'ffn_w_gate_up': 4.238090e-01, 'ffn_w_down': 7.984962e-01, 'ffn_post_norm': 3.202899e+01, 'kv_norm': 9.830059e-01, 'w_kv': 1.241108e+00, 'b_pre_norm': 5.719580e-01, 'b_w_q': 5.754443e-01, 'b_sinks': 1.195416e-01, 'b_w_o': 6.932718e-01, 'b_post_norm': 3.206223e+01}


def _to_microbatches(a, axis):
    t = _jnp.moveaxis(a, axis, 0)
    t = t.reshape((N_MICROBATCH, t.shape[0] // N_MICROBATCH) + t.shape[1:])
    return _jnp.moveaxis(t, 1, axis + 1)


def setup_inputs(seed: int = 0) -> dict:
    inp = _fwd_setup_inputs(seed)
    key = _jax.random.fold_in(_jax.random.key(seed), 7919)
    shape, _ = _output_shape()
    out = dict(inp)
    out["loss_target"] = _jax.random.normal(_jax.random.fold_in(key, 0), shape, _jnp.float32)
    for i, name in enumerate(TWIN_WEIGHTS):
        w = inp[name].astype(_jnp.float32)
        if MOMENT_SCALE is None:
            s = _jnp.sqrt(_jnp.mean(_jnp.square(w)) + 1e-30)
        else:
            s = MOMENT_SCALE[name]
        km, kv = _jax.random.split(_jax.random.fold_in(key, i + 1))
        out[name] = w
        out["m_" + name] = s * _jax.random.normal(km, w.shape, _jnp.float32)
        out["v_" + name] = (s * s) * _jax.random.uniform(kv, w.shape, _jnp.float32, 0.5, 1.5)
    if N_MICROBATCH > 1:
        for name, axis in PER_EXAMPLE_BATCH_AXIS.items():
            out[name] = _to_microbatches(out[name], axis)
    return {'x': out['x'], 'a_pre_norm': out['a_pre_norm'], 'a_w_in': out['a_w_in'], 'a_conv_w': out['a_conv_w'], 'a_w_out': out['a_w_out'], 'a_post_norm': out['a_post_norm'], 'ffn_pre_norm': out['ffn_pre_norm'], 'ffn_w_gate_up': out['ffn_w_gate_up'], 'ffn_w_down': out['ffn_w_down'], 'ffn_post_norm': out['ffn_post_norm'], 'kv_norm': out['kv_norm'], 'w_kv': out['w_kv'], 'b_pre_norm': out['b_pre_norm'], 'b_w_q': out['b_w_q'], 'b_sinks': out['b_sinks'], 'b_w_o': out['b_w_o'], 'b_post_norm': out['b_post_norm'], 'loss_target': out['loss_target'], 'm_a_pre_norm': out['m_a_pre_norm'], 'm_a_w_in': out['m_a_w_in'], 'm_a_conv_w': out['m_a_conv_w'], 'm_a_w_out': out['m_a_w_out'], 'm_a_post_norm': out['m_a_post_norm'], 'm_ffn_pre_norm': out['m_ffn_pre_norm'], 'm_ffn_w_gate_up': out['m_ffn_w_gate_up'], 'm_ffn_w_down': out['m_ffn_w_down'], 'm_ffn_post_norm': out['m_ffn_post_norm'], 'm_kv_norm': out['m_kv_norm'], 'm_w_kv': out['m_w_kv'], 'm_b_pre_norm': out['m_b_pre_norm'], 'm_b_w_q': out['m_b_w_q'], 'm_b_sinks': out['m_b_sinks'], 'm_b_w_o': out['m_b_w_o'], 'm_b_post_norm': out['m_b_post_norm'], 'v_a_pre_norm': out['v_a_pre_norm'], 'v_a_w_in': out['v_a_w_in'], 'v_a_conv_w': out['v_a_conv_w'], 'v_a_w_out': out['v_a_w_out'], 'v_a_post_norm': out['v_a_post_norm'], 'v_ffn_pre_norm': out['v_ffn_pre_norm'], 'v_ffn_w_gate_up': out['v_ffn_w_gate_up'], 'v_ffn_w_down': out['v_ffn_w_down'], 'v_ffn_post_norm': out['v_ffn_post_norm'], 'v_kv_norm': out['v_kv_norm'], 'v_w_kv': out['v_w_kv'], 'v_b_pre_norm': out['v_b_pre_norm'], 'v_b_w_q': out['v_b_w_q'], 'v_b_sinks': out['v_b_sinks'], 'v_b_w_o': out['v_b_w_o'], 'v_b_post_norm': out['v_b_post_norm']}


def _loss(weights, diff, rest, loss_target):
    with _jax.named_scope("forward"):
        args = {**rest, TWIN_DIFF_INPUT: diff, **{k: w.astype(_WEIGHT_DTYPES[k]) for k, w in weights.items()}}
        y = _forward(args)
    with _jax.named_scope("loss_head"):
        err = _jnp.square(y.astype(_jnp.float32) - loss_target)
        return 0.5 * _jnp.sum(_jnp.mean(err, axis=-1)) if err.ndim else 0.5 * err


def _adamw(w, g, m, v):
    m = ADAM_B1 * m + (1.0 - ADAM_B1) * g
    v = ADAM_B2 * v + (1.0 - ADAM_B2) * _jnp.square(g)
    m_hat = m / (1.0 - ADAM_B1 ** ADAM_STEP)
    v_hat = v / (1.0 - ADAM_B2 ** ADAM_STEP)
    delta = -ADAM_LR * (m_hat / (_jnp.sqrt(v_hat) + ADAM_EPS) + ADAM_WD * w)
    return delta, m, v


def reference(x, a_pre_norm, a_w_in, a_conv_w, a_w_out, a_post_norm, ffn_pre_norm, ffn_w_gate_up, ffn_w_down, ffn_post_norm, kv_norm, w_kv, b_pre_norm, b_w_q, b_sinks, b_w_o, b_post_norm, loss_target, m_a_pre_norm, m_a_w_in, m_a_conv_w, m_a_w_out, m_a_post_norm, m_ffn_pre_norm, m_ffn_w_gate_up, m_ffn_w_down, m_ffn_post_norm, m_kv_norm, m_w_kv, m_b_pre_norm, m_b_w_q, m_b_sinks, m_b_w_o, m_b_post_norm, v_a_pre_norm, v_a_w_in, v_a_conv_w, v_a_w_out, v_a_post_norm, v_ffn_pre_norm, v_ffn_w_gate_up, v_ffn_w_down, v_ffn_post_norm, v_kv_norm, v_w_kv, v_b_pre_norm, v_b_w_q, v_b_sinks, v_b_w_o, v_b_post_norm):
    given = dict(x=x, a_pre_norm=a_pre_norm, a_w_in=a_w_in, a_conv_w=a_conv_w, a_w_out=a_w_out, a_post_norm=a_post_norm, ffn_pre_norm=ffn_pre_norm, ffn_w_gate_up=ffn_w_gate_up, ffn_w_down=ffn_w_down, ffn_post_norm=ffn_post_norm, kv_norm=kv_norm, w_kv=w_kv, b_pre_norm=b_pre_norm, b_w_q=b_w_q, b_sinks=b_sinks, b_w_o=b_w_o, b_post_norm=b_post_norm, loss_target=loss_target, m_a_pre_norm=m_a_pre_norm, m_a_w_in=m_a_w_in, m_a_conv_w=m_a_conv_w, m_a_w_out=m_a_w_out, m_a_post_norm=m_a_post_norm, m_ffn_pre_norm=m_ffn_pre_norm, m_ffn_w_gate_up=m_ffn_w_gate_up, m_ffn_w_down=m_ffn_w_down, m_ffn_post_norm=m_ffn_post_norm, m_kv_norm=m_kv_norm, m_w_kv=m_w_kv, m_b_pre_norm=m_b_pre_norm, m_b_w_q=m_b_w_q, m_b_sinks=m_b_sinks, m_b_w_o=m_b_w_o, m_b_post_norm=m_b_post_norm, v_a_pre_norm=v_a_pre_norm, v_a_w_in=v_a_w_in, v_a_conv_w=v_a_conv_w, v_a_w_out=v_a_w_out, v_a_post_norm=v_a_post_norm, v_ffn_pre_norm=v_ffn_pre_norm, v_ffn_w_gate_up=v_ffn_w_gate_up, v_ffn_w_down=v_ffn_w_down, v_ffn_post_norm=v_ffn_post_norm, v_kv_norm=v_kv_norm, v_w_kv=v_w_kv, v_b_pre_norm=v_b_pre_norm, v_b_w_q=v_b_w_q, v_b_sinks=v_b_sinks, v_b_w_o=v_b_w_o, v_b_post_norm=v_b_post_norm)
    weights = {n: given[n] for n in TWIN_WEIGHTS}
    shared = {n: given[n] for n in SHARED_INPUTS}
    per_example = {n: given[n] for n in ['x']}
    grad_fn = _jax.value_and_grad(_loss, argnums=(0, 1))

    def one_microbatch(ex, loss_target):
        ex = dict(ex)
        diff = ex.pop(TWIN_DIFF_INPUT)
        return grad_fn(weights, diff, {**shared, **ex}, loss_target)

    if N_MICROBATCH == 1:
        loss, (grad_w, grad_x) = one_microbatch(per_example, given["loss_target"])
    else:
        def body(carry, xs):
            loss_sum, grad_sum = carry
            l_k, (gw_k, gx_k) = one_microbatch(xs[0], xs[1])
            with _jax.named_scope("update"):
                return (loss_sum + l_k, _jax.tree.map(_jnp.add, grad_sum, gw_k)), gx_k

        init = (_jnp.zeros((), _jnp.float32), _jax.tree.map(_jnp.zeros_like, weights))
        (loss, grad_w), grad_x = _jax.lax.scan(body, init, (per_example, given["loss_target"]))
    with _jax.named_scope("update"):
        delta_w, new_m, new_v = {}, {}, {}
        for n in TWIN_WEIGHTS:
            delta_w[n], new_m[n], new_v[n] = _adamw(weights[n], grad_w[n], given["m_" + n], given["v_" + n])
    return (loss, grad_x, *[grad_w[n] for n in TWIN_WEIGHTS], *[delta_w[n] for n in TWIN_WEIGHTS],
            *[new_m[n] for n in TWIN_WEIGHTS], *[new_v[n] for n in TWIN_WEIGHTS])
```

```python
import math

import jax
import jax.numpy as jnp
from jax import lax
from jax.experimental import pallas as pl
from jax.experimental.pallas import tpu as pltpu

F32 = jnp.float32
BF16 = jnp.bfloat16
SDS = jax.ShapeDtypeStruct
MESH = pl.DeviceIdType.MESH

EPS = 1e-6
NEG = -1e30
HEAD_DIM = 64
N_KV_HEADS = 4
BLOCK = 128
ROT_DIM = HEAD_DIM // 4
ROPE_THETA = 500000.0
N_CHIPS = 4

ADAM_LR = 0.001
ADAM_B1 = 0.9
ADAM_B2 = 0.999
ADAM_EPS = 1e-08
ADAM_WD = 0.01
ADAM_STEP = 10

VMEM_LIMIT_BYTES = 52 * 1024 * 1024
ROW_TILE = 512
BF16_ROWS = 16


def _params(*semantics):
    return pltpu.CompilerParams(dimension_semantics=semantics, vmem_limit_bytes=VMEM_LIMIT_BYTES)


def _rms_r(xf):
    return lax.rsqrt(jnp.mean(xf * xf, axis=-1, keepdims=True) + EPS)


def _rmsnorm_bwd(xf, g, dy):
    r = _rms_r(xf)
    xh = xf * r
    gd = g * dy
    return r * (gd - xh * jnp.mean(xh * gd, axis=-1, keepdims=True)), xh


def _dot(a, b):
    return jnp.dot(a, b, preferred_element_type=F32)


def _dot_nt(a, b):
    return lax.dot_general(a, b, (((1,), (1,)), ((), ())), preferred_element_type=F32)


def _dot_tn(a, b):
    return lax.dot_general(a, b, (((0,), (0,)), ((), ())), preferred_element_type=F32)


def _accumulate(ref, first, value):
    @pl.when(first)
    def _():
        ref[...] = value

    @pl.when(jnp.logical_not(first))
    def _():
        ref[...] += value


def norm_matmul(x, g, w, *, split, name, tm=ROW_TILE):
    T, D = x.shape
    P, _, ws = w.shape
    per = P // split

    def body(x_ref, g_ref, w_ref, o_ref, xn_ref):
        @pl.when(pl.program_id(1) == 0)
        def _():
            xf = x_ref[...]
            xn_ref[...] = (xf * _rms_r(xf) * g_ref[...]).astype(BF16)

        o_ref[...] = _dot(xn_ref[...], w_ref[...]).astype(BF16)

    return pl.pallas_call(
        body, name=name, grid=(T // tm, P),
        in_specs=[pl.BlockSpec((tm, D), lambda i, j: (i, 0)),
                  pl.BlockSpec((1, D), lambda i, j: (0, 0)),
                  pl.BlockSpec((None, D, ws), lambda i, j: (j, 0, 0))],
        out_specs=[pl.BlockSpec((None, tm, ws), lambda i, j: (j // per, i, j % per)),
                   pl.BlockSpec((tm, D), lambda i, j: (i, 0))],
        out_shape=[SDS((split, T, per * ws), BF16), SDS((T, D), BF16)],
        compiler_params=_params("parallel", "arbitrary"),
    )(x, g, w)


def _shift_down(prev, cur, by):
    big = jnp.concatenate([prev, cur], axis=0)
    return pltpu.roll(big, by, 0)[prev.shape[0]:]


def _shift_up(cur, nxt, by):
    big = jnp.concatenate([cur, nxt], axis=0)
    return pltpu.roll(big, big.shape[0] - by, 0)[:cur.shape[0]]


def conv_mix_out(bcx, conv_w, w_out, g_post, res, *, name, tm=ROW_TILE):
    T, D = res.shape
    hb = tm // BF16_ROWS

    def body(b_ref, c_ref, u_ref, cp_ref, up_ref, cw_ref, w_ref, g_ref, r_ref, h_ref, z_ref, y_ref):
        i = pl.program_id(0)
        cu = c_ref[...].astype(F32) * u_ref[...].astype(F32)
        cup = cp_ref[...].astype(F32) * up_ref[...].astype(F32)
        cup = jnp.where(i == 0, 0.0, cup)
        cv = (cw_ref[0:1, :] * _shift_down(cup, cu, 2) + cw_ref[1:2, :] * _shift_down(cup, cu, 1)
              + cw_ref[2:3, :] * cu)
        y = (b_ref[...].astype(F32) * cv).astype(BF16)
        y_ref[...] = y
        z = _dot(y, w_ref[...])
        z_ref[...] = z.astype(BF16)
        h_ref[...] = r_ref[...] + z * _rms_r(z) * g_ref[...]

    tile = lambda col: pl.BlockSpec((tm, D), lambda i: (i, col))
    halo = lambda col: pl.BlockSpec((BF16_ROWS, D), lambda i: (jnp.maximum(i * hb - 1, 0), col))
    row = pl.BlockSpec((tm, D), lambda i: (i, 0))
    return pl.pallas_call(
        body, name=name, grid=(T // tm,),
        in_specs=[tile(0), tile(1), tile(2), halo(1), halo(2),
                  pl.BlockSpec((3, D), lambda i: (0, 0)),
                  pl.BlockSpec((D, D), lambda i: (0, 0)),
                  pl.BlockSpec((1, D), lambda i: (0, 0)), row],
        out_specs=[row, row, row],
        out_shape=[SDS((T, D), F32), SDS((T, D), BF16), SDS((T, D), BF16)],
        compiler_params=_params("parallel"),
    )(bcx, bcx, bcx, bcx, bcx, conv_w, w_out, g_post, res)


def plain_mix_out(a, w, g_post, res, *, name, tm=ROW_TILE):
    T, D = res.shape
    K = a.shape[1]

    def body(a_ref, w_ref, g_ref, r_ref, h_ref, z_ref):
        z = _dot(a_ref[...], w_ref[...])
        z_ref[...] = z.astype(BF16)
        h_ref[...] = r_ref[...] + z * _rms_r(z) * g_ref[...]

    row = pl.BlockSpec((tm, D), lambda i: (i, 0))
    return pl.pallas_call(
        body, name=name, grid=(T // tm,),
        in_specs=[pl.BlockSpec((tm, K), lambda i: (i, 0)),
                  pl.BlockSpec((K, D), lambda i: (0, 0)),
                  pl.BlockSpec((1, D), lambda i: (0, 0)), row],
        out_specs=[row, row],
        out_shape=[SDS((T, D), F32), SDS((T, D), BF16)],
        compiler_params=_params("parallel"),
    )(a, w, g_post, res)


def swiglu_mix_out(gu, w_down, g_post, res, *, tk, name, target=None, tm=ROW_TILE):
    T, D = res.shape
    F = gu.shape[2]
    nk = F // tk
    with_loss = target is not None

    def body(*refs):
        if with_loss:
            g_ref, u_ref, w_ref, gp_ref, r_ref, t_ref, h_ref, z_ref, a_ref, loss_ref, acc = refs
        else:
            g_ref, u_ref, w_ref, gp_ref, r_ref, h_ref, z_ref, a_ref, acc = refs
        i, k = pl.program_id(0), pl.program_id(1)
        g = g_ref[...].astype(F32)
        a = (g * jax.nn.sigmoid(g) * u_ref[...].astype(F32)).astype(BF16)
        a_ref[...] = a
        _accumulate(acc, k == 0, _dot(a, w_ref[...]))

        @pl.when(k == nk - 1)
        def _():
            z = acc[...]
            z_ref[...] = z.astype(BF16)
            h = r_ref[...] + z * _rms_r(z) * gp_ref[...]
            if with_loss:
                diff = h - t_ref[...]
                h_ref[...] = diff * (1.0 / D)
                part = jnp.full(loss_ref.shape, 0.5 / D, F32) * jnp.sum(diff * diff)
                _accumulate(loss_ref, i == 0, part)
            else:
                h_ref[...] = h

    row = pl.BlockSpec((tm, D), lambda i, k: (i, 0))
    in_specs = [pl.BlockSpec((None, tm, tk), lambda i, k: (0, i, k)),
                pl.BlockSpec((None, tm, tk), lambda i, k: (1, i, k)),
                pl.BlockSpec((tk, D), lambda i, k: (k, 0)),
                pl.BlockSpec((1, D), lambda i, k: (0, 0)), row]
    out_specs = [row, row, pl.BlockSpec((tm, tk), lambda i, k: (i, k))]
    out_shape = [SDS((T, D), F32), SDS((T, D), BF16), SDS((T, F), BF16)]
    args = [gu, gu, w_down, g_post, res]
    if with_loss:
        in_specs.append(row)
        args.append(target)
        out_specs.append(pl.BlockSpec((8, 128), lambda i, k: (0, 0)))
        out_shape.append(SDS((8, 128), F32))
    return pl.pallas_call(
        body, name=name, grid=(T // tm, nk), in_specs=in_specs, out_specs=out_specs, out_shape=out_shape,
        scratch_shapes=[pltpu.VMEM((tm, D), F32)],
        compiler_params=_params("arbitrary", "arbitrary"),
    )(*args)


def rope_tables(T):
    half = ROT_DIM // 2
    inv_freq = ROPE_THETA ** (-jnp.arange(0, ROT_DIM, 2, dtype=F32) / ROT_DIM)
    ang = jnp.arange(T, dtype=F32)[:, None] * inv_freq[None, :]
    cos, sin = jnp.cos(ang), jnp.sin(ang)
    rest = HEAD_DIM - ROT_DIM
    one, zero = jnp.ones((T, rest), F32), jnp.zeros((T, rest), F32)
    zh = jnp.zeros((T, half), F32)
    fac = jnp.concatenate([cos, cos, one], axis=1)
    up = jnp.concatenate([-sin, zh, zero], axis=1)
    down = jnp.concatenate([zh, sin, zero], axis=1)
    return jnp.stack([jnp.tile(t, (1, 128 // HEAD_DIM)) for t in (fac, up, down)])


def _rope(t, tab):
    w = t.shape[1]
    reps = w // 128
    fac, up, down = (jnp.tile(tab[k], (1, reps)) for k in range(3))
    return t * fac + pltpu.roll(t, w - ROT_DIM // 2, 1) * up + pltpu.roll(t, ROT_DIM // 2, 1) * down


def _rope_t(d, tab):
    w = d.shape[1]
    reps = w // 128
    fac, up, down = (jnp.tile(tab[k], (1, reps)) for k in range(3))
    return d * fac + pltpu.roll(d * up, ROT_DIM // 2, 1) + pltpu.roll(d * down, w - ROT_DIM // 2, 1)


def _band(n):
    qi = lax.broadcasted_iota(jnp.int32, (BLOCK, 2 * BLOCK), 0)
    kj = lax.broadcasted_iota(jnp.int32, (BLOCK, 2 * BLOCK), 1)
    return (kj > qi) & (kj <= qi + BLOCK) & ((n > 0) | (kj >= BLOCK))


def _attn_specs(D, kvd):
    prev = lambda n: jnp.maximum(n - 1, 0)
    return [pl.BlockSpec((BLOCK, D), lambda n: (n, 0)),
            pl.BlockSpec((BLOCK, kvd), lambda n: (prev(n), 0)),
            pl.BlockSpec((BLOCK, kvd), lambda n: (n, 0)),
            pl.BlockSpec((BLOCK, kvd), lambda n: (prev(n), 1)),
            pl.BlockSpec((BLOCK, kvd), lambda n: (n, 1)),
            pl.BlockSpec((3, BLOCK, 128), lambda n: (0, prev(n), 0)),
            pl.BlockSpec((3, BLOCK, 128), lambda n: (0, n, 0)),
            pl.BlockSpec(memory_space=pltpu.SMEM)]


def _softmax_block(q_h, k_j, sink, mask, scale):
    s = jnp.where(mask, _dot_nt(q_h, k_j) * scale, NEG)
    m = jnp.maximum(jnp.max(s, axis=-1, keepdims=True), sink)
    e = jnp.exp(s - m)
    es = jnp.exp(sink - m)
    inv = 1.0 / (jnp.sum(e, axis=-1, keepdims=True) + es)
    return e * inv, es * inv


def attention_fwd(q, kv, tabs, sinks, *, name):
    T, D = q.shape
    kvd = kv.shape[1] // 2
    group = D // HEAD_DIM // N_KV_HEADS
    scale = 1.0 / math.sqrt(HEAD_DIM)

    def body(q_ref, kp_ref, k_ref, vp_ref, v_ref, tp_ref, t_ref, s_ref, o_ref):
        n = pl.program_id(0)
        mask = _band(n)
        qr = _rope(q_ref[...].astype(F32), t_ref[...]).astype(BF16)
        kr = jnp.concatenate([_rope(kp_ref[...].astype(F32), tp_ref[...]),
                              _rope(k_ref[...].astype(F32), t_ref[...])], axis=0).astype(BF16)
        vv = jnp.concatenate([vp_ref[...], v_ref[...]], axis=0)
        outs = []
        for j in range(N_KV_HEADS):
            k_j = kr[:, j * HEAD_DIM:(j + 1) * HEAD_DIM]
            v_j = vv[:, j * HEAD_DIM:(j + 1) * HEAD_DIM]
            for gi in range(group):
                h = j * group + gi
                p, _ = _softmax_block(qr[:, h * HEAD_DIM:(h + 1) * HEAD_DIM], k_j, s_ref[0, h], mask, scale)
                outs.append(_dot(p.astype(BF16), v_j))
        o_ref[...] = jnp.concatenate(outs, axis=1).astype(BF16)

    return pl.pallas_call(
        body, name=name, grid=(T // BLOCK,),
        in_specs=_attn_specs(D, kvd),
        out_specs=pl.BlockSpec((BLOCK, D), lambda n: (n, 0)),
        out_shape=SDS((T, D), BF16),
        compiler_params=_params("parallel"),
    )(q, kv, kv, kv, kv, tabs, tabs, sinks)


def attention_bwd(q, kv, tabs, sinks, do, *, name):
    T, D = q.shape
    kvd = kv.shape[1] // 2
    heads = D // HEAD_DIM
    group = heads // N_KV_HEADS
    scale = 1.0 / math.sqrt(HEAD_DIM)

    def body(q_ref, kp_ref, k_ref, vp_ref, v_ref, tp_ref, t_ref, s_ref, do_ref, dq_ref, dc_ref, dp_ref, ds_ref):
        n = pl.program_id(0)
        mask = _band(n)
        tab, tabp = t_ref[...], tp_ref[...]
        qr = _rope(q_ref[...].astype(F32), tab).astype(BF16)
        kr = jnp.concatenate([_rope(kp_ref[...].astype(F32), tabp),
                              _rope(k_ref[...].astype(F32), tab)], axis=0).astype(BF16)
        vv = jnp.concatenate([vp_ref[...], v_ref[...]], axis=0)
        do_all = do_ref[...]
        lane = lax.broadcasted_iota(jnp.int32, (8, 128), 1)
        dsink = jnp.zeros((8, 128), F32)
        dqs, dks, dvs = [], [], []
        for j in range(N_KV_HEADS):
            k_j = kr[:, j * HEAD_DIM:(j + 1) * HEAD_DIM]
            v_j = vv[:, j * HEAD_DIM:(j + 1) * HEAD_DIM]
            dk_j = jnp.zeros((2 * BLOCK, HEAD_DIM), F32)
            dv_j = jnp.zeros((2 * BLOCK, HEAD_DIM), F32)
            for gi in range(group):
                h = j * group + gi
                q_h = qr[:, h * HEAD_DIM:(h + 1) * HEAD_DIM]
                do_h = do_all[:, h * HEAD_DIM:(h + 1) * HEAD_DIM]
                p, p_sink = _softmax_block(q_h, k_j, s_ref[0, h], mask, scale)
                dp = _dot_nt(do_h, v_j)
                dl = jnp.sum(p * dp, axis=-1, keepdims=True)
                dsc = (p * (dp - dl) * scale).astype(BF16)
                dqs.append(_dot(dsc, k_j))
                dk_j += _dot_tn(dsc, q_h)
                dv_j += _dot_tn(p.astype(BF16), do_h)
                dsink = dsink - jnp.where(lane == h, jnp.sum(p_sink * dl), 0.0)
            dks.append(dk_j)
            dvs.append(dv_j)
        dq_ref[...] = _rope_t(jnp.concatenate(dqs, axis=1), tab).astype(BF16)
        dk = jnp.concatenate(dks, axis=1)
        dv = jnp.concatenate(dvs, axis=1)
        dp_ref[...] = jnp.concatenate([_rope_t(dk[:BLOCK], tabp), dv[:BLOCK]], axis=1)
        dc_ref[...] = jnp.concatenate([_rope_t(dk[BLOCK:], tab), dv[BLOCK:]], axis=1)
        _accumulate(ds_ref, n == 0, dsink)

    blk = lambda w: pl.BlockSpec((BLOCK, w), lambda n: (n, 0))
    return pl.pallas_call(
        body, name=name, grid=(T // BLOCK,),
        in_specs=_attn_specs(D, kvd) + [blk(D)],
        out_specs=[blk(D), blk(2 * kvd), blk(2 * kvd), pl.BlockSpec((8, 128), lambda n: (0, 0))],
        out_shape=[SDS((T, D), BF16), SDS((T, 2 * kvd), F32), SDS((T, 2 * kvd), F32), SDS((8, 128), F32)],
        compiler_params=_params("arbitrary"),
    )(q, kv, kv, kv, kv, tabs, tabs, sinks, do)


def combine_dkv(d_cur, d_prev, *, name):
    T, W = d_cur.shape
    nb = T // BLOCK

    def body(c_ref, p_ref, o_ref):
        nxt = jnp.where(pl.program_id(0) == nb - 1, 0.0, p_ref[...])
        o_ref[...] = (c_ref[...] + nxt).astype(BF16)

    return pl.pallas_call(
        body, name=name, grid=(nb,),
        in_specs=[pl.BlockSpec((BLOCK, W), lambda n: (n, 0)),
                  pl.BlockSpec((BLOCK, W), lambda n: (jnp.minimum(n + 1, nb - 1), 0))],
        out_specs=pl.BlockSpec((BLOCK, W), lambda n: (n, 0)),
        out_shape=SDS((T, W), BF16),
        compiler_params=_params("parallel"),
    )(d_cur, d_prev)


def normbwd_matmul_nt(z, g, dh, w, *, tn, name, gu=None, tm=ROW_TILE):
    T, D = z.shape
    K = w.shape[0]
    swiglu = gu is not None

    def body(*refs):
        if swiglu:
            z_ref, g_ref, dh_ref, w_ref, gg_ref, uu_ref, dz_ref, dg_ref, o_ref = refs
        else:
            z_ref, g_ref, dh_ref, w_ref, dz_ref, dg_ref, o_ref = refs
        i, j = pl.program_id(0), pl.program_id(1)

        @pl.when(j == 0)
        def _():
            dh_ = dh_ref[...]
            dz, zh = _rmsnorm_bwd(z_ref[...].astype(F32), g_ref[...], dh_)
            dz_ref[...] = dz.astype(BF16)
            _accumulate(dg_ref, i == 0, jnp.sum(dh_ * zh, axis=0, keepdims=True))

        d = _dot_nt(dz_ref[...], w_ref[...])
        if swiglu:
            g_ = gg_ref[...].astype(F32)
            sg = jax.nn.sigmoid(g_)
            o_ref[0] = (d * uu_ref[...].astype(F32) * (sg * (1.0 + g_ * (1.0 - sg)))).astype(BF16)
            o_ref[1] = (d * (g_ * sg)).astype(BF16)
        else:
            o_ref[...] = d.astype(BF16)

    row = pl.BlockSpec((tm, D), lambda i, j: (i, 0))
    in_specs = [row, pl.BlockSpec((1, D), lambda i, j: (0, 0)), row, pl.BlockSpec((tn, D), lambda i, j: (j, 0))]
    args = [z, g, dh, w]
    if swiglu:
        in_specs += [pl.BlockSpec((None, tm, tn), lambda i, j: (0, i, j)),
                     pl.BlockSpec((None, tm, tn), lambda i, j: (1, i, j))]
        args += [gu, gu]
        o_spec, o_shape = pl.BlockSpec((2, tm, tn), lambda i, j: (0, i, j)), SDS((2, T, K), BF16)
    else:
        o_spec, o_shape = pl.BlockSpec((tm, tn), lambda i, j: (i, j)), SDS((T, K), BF16)
    return pl.pallas_call(
        body, name=name, grid=(T // tm, K // tn), in_specs=in_specs,
        out_specs=[row, pl.BlockSpec((1, D), lambda i, j: (0, 0)), o_spec],
        out_shape=[SDS((T, D), BF16), SDS((1, D), F32), o_shape],
        compiler_params=_params("arbitrary", "arbitrary"),
    )(*args)


def matmul_nt_normbwd(da, w, h_in, g, dh_out, *, name, tm=ROW_TILE):
    T, D = h_in.shape
    P, _, ws = w.shape
    per = da.shape[2] // ws

    def body(da_ref, w_ref, h_ref, g_ref, dh_ref, o_ref, dg_ref, acc):
        i, k = pl.program_id(0), pl.program_id(1)
        _accumulate(acc, k == 0, _dot_nt(da_ref[...], w_ref[...]))

        @pl.when(k == P - 1)
        def _():
            dn = acc[...]
            dx, hh = _rmsnorm_bwd(h_ref[...], g_ref[...], dn)
            o_ref[...] = dh_ref[...] + dx
            _accumulate(dg_ref, i == 0, jnp.sum(dn * hh, axis=0, keepdims=True))

    row = pl.BlockSpec((tm, D), lambda i, k: (i, 0))
    vec = pl.BlockSpec((1, D), lambda i, k: (0, 0))
    return pl.pallas_call(
        body, name=name, grid=(T // tm, P),
        in_specs=[pl.BlockSpec((None, tm, ws), lambda i, k: (k // per, i, k % per)),
                  pl.BlockSpec((None, D, ws), lambda i, k: (k, 0, 0)), row, vec, row],
        out_specs=[row, vec],
        out_shape=[SDS((T, D), F32), SDS((1, D), F32)],
        scratch_shapes=[pltpu.VMEM((tm, D), F32)],
        compiler_params=_params("arbitrary", "arbitrary"),
    )(da, w, h_in, g, dh_out)


def matmul_tn(a, b, *, ta, tb, tt, name):
    T, Ka = a.shape
    S, _, Nb = b.shape
    per = Nb // tb
    nk = T // tt

    def body(a_ref, b_ref, o_ref, acc):
        k = pl.program_id(2)
        _accumulate(acc, k == 0, _dot_tn(a_ref[...], b_ref[...]))

        @pl.when(k == nk - 1)
        def _():
            o_ref[...] = acc[...].astype(BF16)

    return pl.pallas_call(
        body, name=name, grid=(Ka // ta, S * per, nk),
        in_specs=[pl.BlockSpec((tt, ta), lambda i, j, k: (k, i)),
                  pl.BlockSpec((None, tt, tb), lambda i, j, k: (j // per, k, j % per))],
        out_specs=pl.BlockSpec((None, ta, tb), lambda i, j, k: (j, i, 0)),
        out_shape=SDS((S * per, Ka, tb), BF16),
        scratch_shapes=[pltpu.VMEM((ta, tb), F32)],
        compiler_params=_params("parallel", "parallel", "arbitrary"),
    )(a, b)


def conv_bwd(dy, bcx, conv_w, *, name, tm=ROW_TILE):
    T, D = dy.shape
    nt = T // tm
    hb = tm // BF16_ROWS
    last = T // BF16_ROWS - 1

    def body(dy_ref, dyn_ref, b_ref, bn_ref, c_ref, u_ref, cp_ref, up_ref, cw_ref, o_ref, dw_ref):
        i = pl.program_id(0)
        c, u = c_ref[...].astype(F32), u_ref[...].astype(F32)
        cu = c * u
        cup = jnp.where(i == 0, 0.0, cp_ref[...].astype(F32) * up_ref[...].astype(F32))
        cu1, cu2 = _shift_down(cup, cu, 1), _shift_down(cup, cu, 2)
        w0, w1, w2 = cw_ref[0:1, :], cw_ref[1:2, :], cw_ref[2:3, :]
        dyf = dy_ref[...].astype(F32)
        o_ref[:, 0:D] = (dyf * (w0 * cu2 + w1 * cu1 + w2 * cu)).astype(BF16)
        dcv = dyf * b_ref[...].astype(F32)
        dcvn = jnp.where(i == nt - 1, 0.0, dyn_ref[...].astype(F32) * bn_ref[...].astype(F32))
        dcu = w2 * dcv + w1 * _shift_up(dcv, dcvn, 1) + w0 * _shift_up(dcv, dcvn, 2)
        o_ref[:, D:2 * D] = (dcu * u).astype(BF16)
        o_ref[:, 2 * D:3 * D] = (dcu * c).astype(BF16)
        row = lax.broadcasted_iota(jnp.int32, (8, D), 0)
        dw = jnp.zeros((8, D), F32)
        for tap, t in enumerate((cu2, cu1, cu)):
            dw = jnp.where(row == tap, jnp.sum(dcv * t, axis=0, keepdims=True), dw)
        _accumulate(dw_ref, i == 0, dw)

    tile = lambda col: pl.BlockSpec((tm, D), lambda i: (i, col))
    prev = lambda col: pl.BlockSpec((BF16_ROWS, D), lambda i: (jnp.maximum(i * hb - 1, 0), col))
    nxt = lambda col: pl.BlockSpec((BF16_ROWS, D), lambda i: (jnp.minimum((i + 1) * hb, last), col))
    return pl.pallas_call(
        body, name=name, grid=(nt,),
        in_specs=[tile(0), nxt(0), tile(0), nxt(0), tile(1), tile(2), prev(1), prev(2),
                  pl.BlockSpec((3, D), lambda i: (0, 0))],
        out_specs=[pl.BlockSpec((tm, 3 * D), lambda i: (i, 0)), pl.BlockSpec((8, D), lambda i: (0, 0))],
        out_shape=[SDS((T, 3 * D), BF16), SDS((8, D), F32)],
        compiler_params=_params("arbitrary"),
    )(dy, dy, bcx, bcx, bcx, bcx, bcx, bcx, conv_w)


def local_step(x, target, wts, vec):
    T, D = x.shape
    F = wts["wd0"].shape[0]
    tabs = rope_tables(T)
    ffn_chunk = wts["gu0"].shape[2]

    bcx, xn1 = norm_matmul(x, vec["a_pre"], wts["w_in"], split=1, name="a_in")
    bcx = bcx[0]
    h1, z0, y0 = conv_mix_out(bcx, vec["conv_w"], wts["w_out"], vec["a_post"], x, name="a_out")
    gu0, xn2 = norm_matmul(h1, vec["ffn_pre0"], wts["gu0"], split=2, name="ffn0_in")
    h2, z1, act0 = swiglu_mix_out(gu0, wts["wd0"], vec["ffn_post0"], h1, tk=ffn_chunk, name="ffn0_out")
    kvp, xkv = norm_matmul(h2, vec["kv_norm"], wts["w_kv"], split=1, name="kv_in")
    qp, xq = norm_matmul(h2, vec["b_pre"], wts["w_q"], split=1, name="q_in")
    kvp, qp = kvp[0], qp[0]
    attn = attention_fwd(qp, kvp, tabs, vec["sinks"], name="attn_fwd")
    h3, z2 = plain_mix_out(attn, wts["w_o"], vec["b_post"], h2, name="attn_out")
    gu1, xn3 = norm_matmul(h3, vec["ffn_pre1"], wts["gu1"], split=2, name="ffn1_in")
    dy, z3, act1, loss = swiglu_mix_out(gu1, wts["wd1"], vec["ffn_post1"], h3, tk=ffn_chunk, name="ffn1_out",
                                        target=target)

    grads, small = {}, {}
    tt = 1024 if T % 1024 == 0 else ROW_TILE

    def ffn_bwd(layer, z, gu, act, xn, h_in, dh, w_down, w_gu):
        dz, small["ffn_post%d" % layer], dgu = normbwd_matmul_nt(
            z, vec["ffn_post%d" % layer], dh, w_down, tn=ffn_chunk, gu=gu, name="ffn%d_out_bwd" % layer)
        grads["wd%d" % layer] = matmul_tn(act, dz[None], ta=ffn_chunk, tb=D, tt=tt, name="ffn%d_dwd" % layer)
        dh_in, small["ffn_pre%d" % layer] = matmul_nt_normbwd(
            dgu, w_gu, h_in, vec["ffn_pre%d" % layer], dh, name="ffn%d_in_bwd" % layer)
        grads["gu%d" % layer] = matmul_tn(xn, dgu, ta=D, tb=ffn_chunk, tt=tt, name="ffn%d_dwgu" % layer)
        return dh_in

    dh3 = ffn_bwd(1, z3, gu1, act1, xn3, h3, dy, wts["wd1"], wts["gu1"])
    dz2, small["b_post"], dattn = normbwd_matmul_nt(z2, vec["b_post"], dh3, wts["w_o"], tn=D, name="attn_out_bwd")
    grads["w_o"] = matmul_tn(attn, dz2[None], ta=D, tb=D, tt=tt, name="attn_dwo")
    dq, dkv_cur, dkv_prev, dsinks = attention_bwd(qp, kvp, tabs, vec["sinks"], dattn, name="attn_bwd")
    small["sinks"] = dsinks
    dkv = combine_dkv(dkv_cur, dkv_prev, name="attn_dkv")
    grads["w_q"] = matmul_tn(xq, dq[None], ta=D, tb=D, tt=tt, name="attn_dwq")
    grads["w_kv"] = matmul_tn(xkv, dkv[None], ta=D, tb=dkv.shape[1], tt=tt, name="attn_dwkv")
    dh2, small["b_pre"] = matmul_nt_normbwd(dq[None], wts["w_q"], h2, vec["b_pre"], dh3, name="q_in_bwd")
    dh2, small["kv_norm"] = matmul_nt_normbwd(dkv[None], wts["w_kv"], h2, vec["kv_norm"], dh2, name="kv_in_bwd")
    dh1 = ffn_bwd(0, z1, gu0, act0, xn2, h1, dh2, wts["wd0"], wts["gu0"])
    dz0, small["a_post"], dyc = normbwd_matmul_nt(z0, vec["a_post"], dh1, wts["w_out"], tn=D, name="a_out_bwd")
    grads["w_out"] = matmul_tn(y0, dz0[None], ta=D, tb=D, tt=tt, name="a_dwout")
    dbcx, small["conv_w"] = conv_bwd(dyc, bcx, vec["conv_w"], name="a_conv_bwd")
    grads["w_in"] = matmul_tn(xn1, dbcx[None], ta=D, tb=wts["w_in"].shape[2], tt=tt, name="a_dwin")
    dx, small["a_pre"] = matmul_nt_normbwd(dbcx[None], wts["w_in"], x, vec["a_pre"], dh1, name="a_in_bwd")
    return loss, dx, grads, small


HBM_SPEC = pl.BlockSpec(memory_space=pltpu.HBM)


def _place():
    return lax.axis_index("x"), lax.axis_index("y"), lax.axis_index("c")


def _other_chips(x, y):
    return [(1 - x, y), (x, 1 - y), (1 - x, 1 - y)]


def _remote(src, dst, send_sem, recv_sem, to):
    return pltpu.make_async_remote_copy(src_ref=src, dst_ref=dst, send_sem=send_sem, recv_sem=recv_sem,
                                        device_id=to, device_id_type=MESH)


def allgather_weights(shards, small):
    n = len(shards)
    halves = [s.shape[0] // 2 for s in shards]

    def body(*refs):
        ins, small_in = refs[:n], refs[n]
        outs, small_out = refs[n + 1:2 * n + 1], refs[2 * n + 1]
        send1, recv1, send2, recv2, ssend, srecv, lsem = refs[2 * n + 2:]
        x, y, c = _place()
        p = 2 * x + y
        chips = _other_chips(x, y)
        me, sibling = (x, y, c), (x, y, 1 - c)

        local = [pltpu.make_async_copy(ins[t], outs[t].at[p], lsem.at[t]) for t in range(n)]
        local.append(pltpu.make_async_copy(small_in, small_out.at[p], lsem.at[n]))
        for cp in local:
            cp.start()
        sends = []
        for j, (qx, qy) in enumerate(chips):
            sends.append(_remote(small_in, small_out.at[p], ssend.at[j], srecv.at[j], (qx, qy, c)))
            for t in range(n):
                mine = pl.ds(c * halves[t], halves[t])
                sends.append(_remote(ins[t].at[mine], outs[t].at[p, mine], send1.at[t, j], recv1.at[t, j],
                                     (qx, qy, c)))
        for cp in sends:
            cp.start()
        passed = []
        for j, (qx, qy) in enumerate(chips):
            q = 2 * qx + qy
            for t in range(n):
                landed = outs[t].at[q, pl.ds(c * halves[t], halves[t])]
                _remote(landed, landed, send1.at[t, j], recv1.at[t, j], me).wait_recv()
                cp = _remote(landed, landed, send2.at[t, j], recv2.at[t, j], sibling)
                cp.start()
                passed.append(cp)
        for j, (qx, qy) in enumerate(chips):
            q = 2 * qx + qy
            _remote(small_out.at[q], small_out.at[q], ssend.at[j], srecv.at[j], me).wait_recv()
            for t in range(n):
                theirs = outs[t].at[q, pl.ds((1 - c) * halves[t], halves[t])]
                _remote(theirs, theirs, send2.at[t, j], recv2.at[t, j], me).wait_recv()
        for cp in sends + passed:
            cp.wait_send()
        for cp in local:
            cp.wait()

    DMA = pltpu.SemaphoreType.DMA
    return pl.pallas_call(
        body, name="allgather_weights",
        in_specs=[HBM_SPEC] * (n + 1), out_specs=[HBM_SPEC] * (n + 1),
        out_shape=[SDS((N_CHIPS,) + s.shape, s.dtype) for s in shards] + [SDS((N_CHIPS,) + small.shape, small.dtype)],
        scratch_shapes=[DMA((n, 3)), DMA((n, 3)), DMA((n, 3)), DMA((n, 3)), DMA((3,)), DMA((3,)), DMA((n + 1,))],
    )(*shards, small)


def pair_exchange(grads):
    n = len(grads)

    def body(*refs):
        ins, outs = refs[:n], refs[n:2 * n]
        send, recv = refs[2 * n:]
        x, y, c = _place()
        cps = [_remote(ins[t].at[:, 1 - c], outs[t], send.at[t], recv.at[t], (x, y, 1 - c)) for t in range(n)]
        for cp in cps:
            cp.start()
        for cp in cps:
            cp.wait()

    DMA = pltpu.SemaphoreType.DMA
    return pl.pallas_call(
        body, name="pair_exchange",
        in_specs=[HBM_SPEC] * n, out_specs=[HBM_SPEC] * n,
        out_shape=[SDS((g.shape[0],) + g.shape[2:], g.dtype) for g in grads],
        scratch_shapes=[DMA((n,)), DMA((n,))],
    )(*grads)


def pair_add(own, got, c_arr, *, name):
    P, _, h, cols = own.shape

    def body(c_ref, a_ref, b_ref, o_ref):
        o_ref[...] = (a_ref[...].astype(F32) + b_ref[...].astype(F32)).astype(BF16)

    return pl.pallas_call(
        body, name=name,
        grid_spec=pltpu.PrefetchScalarGridSpec(
            num_scalar_prefetch=1, grid=(P,),
            in_specs=[pl.BlockSpec((None, None, h, cols), lambda q, c_ref: (q, c_ref[0], 0, 0)),
                      pl.BlockSpec((None, h, cols), lambda q, c_ref: (q, 0, 0))],
            out_specs=pl.BlockSpec((None, h, cols), lambda q, c_ref: (q, 0, 0))),
        out_shape=SDS((P, h, cols), BF16),
        compiler_params=_params("parallel"),
    )(c_arr, own, got)


def chip_exchange(sums, small):
    n = len(sums)

    def body(*refs):
        ins, small_in = refs[:n], refs[n]
        outs, small_out = refs[n + 1:2 * n + 1], refs[2 * n + 1]
        send, recv, ssend, srecv, lsem = refs[2 * n + 2:]
        x, y, c = _place()
        local = pltpu.make_async_copy(small_in, small_out.at[0], lsem)
        local.start()
        cps = []
        for k in range(1, 8):
            peer = (x ^ (k >> 2 & 1), y ^ (k >> 1 & 1), c ^ (k & 1))
            cps.append(_remote(small_in, small_out.at[k], ssend.at[k - 1], srecv.at[k - 1], peer))
        for j, (qx, qy) in enumerate(_other_chips(x, y)):
            for t in range(n):
                cps.append(_remote(ins[t].at[2 * qx + qy], outs[t].at[j], send.at[t, j], recv.at[t, j], (qx, qy, c)))
        for cp in cps:
            cp.start()
        for cp in cps:
            cp.wait()
        local.wait()

    DMA = pltpu.SemaphoreType.DMA
    return pl.pallas_call(
        body, name="chip_exchange",
        in_specs=[HBM_SPEC] * (n + 1), out_specs=[HBM_SPEC] * (n + 1),
        out_shape=[SDS((3,) + s.shape[1:], s.dtype) for s in sums] + [SDS((8,) + small.shape, small.dtype)],
        scratch_shapes=[DMA((n, 3)), DMA((n, 3)), DMA((7,)), DMA((7,)), DMA(())],
    )(*sums, small)


def chip_reduce(sums, got, p_arr, *, name):
    _, h, cols = sums.shape
    th = h // 2

    def body(p_ref, a_ref, b_ref, o_ref):
        o_ref[...] = ((a_ref[...].astype(F32) + b_ref[0].astype(F32)) + b_ref[1].astype(F32)) + b_ref[2].astype(F32)

    return pl.pallas_call(
        body, name=name,
        grid_spec=pltpu.PrefetchScalarGridSpec(
            num_scalar_prefetch=1, grid=(h // th,),
            in_specs=[pl.BlockSpec((None, th, cols), lambda i, p_ref: (p_ref[0], i, 0)),
                      pl.BlockSpec((3, th, cols), lambda i, p_ref: (0, i, 0))],
            out_specs=pl.BlockSpec((th, cols), lambda i, p_ref: (i, 0))),
        out_shape=SDS((h, cols), F32),
        compiler_params=_params("parallel"),
    )(p_arr, sums, got)


def small_reduce(blocks, me_arr):
    _, rows, D = blocks.shape

    def body(me_ref, b_ref, o_ref):
        me = me_ref[0]
        total = b_ref[me]
        for d in range(1, 8):
            total = total + b_ref[d ^ me]
        o_ref[...] = total

    return pl.pallas_call(
        body, name="small_reduce",
        grid_spec=pltpu.PrefetchScalarGridSpec(
            num_scalar_prefetch=1, grid=(1,),
            in_specs=[pl.BlockSpec((8, rows, D), lambda i, me_ref: (0, 0, 0))],
            out_specs=pl.BlockSpec((rows, D), lambda i, me_ref: (0, 0))),
        out_shape=SDS((rows, D), F32),
        compiler_params=_params("arbitrary"),
    )(me_arr, blocks)


def half_exchange(halves):
    n = len(halves)

    def body(*refs):
        ins, outs = refs[:n], refs[n:2 * n]
        send, recv, lsem = refs[2 * n:]
        x, y, c = _place()
        local = [pltpu.make_async_copy(ins[t], outs[t].at[c], lsem.at[t]) for t in range(n)]
        sends = [_remote(ins[t], outs[t].at[c], send.at[t], recv.at[t], (x, y, 1 - c)) for t in range(n)]
        for cp in local + sends:
            cp.start()
        for t in range(n):
            theirs = outs[t].at[1 - c]
            _remote(theirs, theirs, send.at[t], recv.at[t], (x, y, c)).wait_recv()
        for cp in sends:
            cp.wait_send()
        for cp in local:
            cp.wait()

    DMA = pltpu.SemaphoreType.DMA
    return pl.pallas_call(
        body, name="half_exchange",
        in_specs=[HBM_SPEC] * n, out_specs=[HBM_SPEC] * n,
        out_shape=[SDS((2,) + hv.shape, hv.dtype) for hv in halves],
        scratch_shapes=[DMA((n,)), DMA((n,)), DMA((n,))],
    )(*halves)


def _row_tile(rows, limit=256):
    return max(t for t in range(8, limit + 1, 8) if rows % t == 0)


def adamw(w, gs, m, v, *, name):
    L, r, cols = w.shape
    tr = _row_tile(r)
    nt = r // tr

    def body(*refs):
        w_ref, m_ref, v_ref = refs[:3]
        g_refs = refs[3:3 + L]
        g_out, d_out, m_out, v_out = refs[3 + L:]
        layer = pl.program_id(0)
        g = g_refs[0][...]
        for l in range(1, L):
            g = jnp.where(layer == l, g_refs[l][...], g)
        m_new = ADAM_B1 * m_ref[...] + (1.0 - ADAM_B1) * g
        v_new = ADAM_B2 * v_ref[...] + (1.0 - ADAM_B2) * (g * g)
        m_hat = m_new / (1.0 - ADAM_B1 ** ADAM_STEP)
        v_hat = v_new / (1.0 - ADAM_B2 ** ADAM_STEP)
        g_out[...] = g
        m_out[...] = m_new
        v_out[...] = v_new
        d_out[...] = -ADAM_LR * (m_hat / (jnp.sqrt(v_hat) + ADAM_EPS) + ADAM_WD * w_ref[...])

    full = pl.BlockSpec((None, tr, cols), lambda l, i: (l, i, 0))
    g_spec = lambda l0: pl.BlockSpec((tr, cols), lambda l, i: (jnp.where(l == l0, i, jnp.where(l < l0, 0, nt - 1)), 0))
    return pl.pallas_call(
        body, name=name, grid=(L, nt),
        in_specs=[full, full, full] + [g_spec(l0) for l0 in range(L)],
        out_specs=[full] * 4,
        out_shape=[SDS(w.shape, F32)] * 4,
        compiler_params=_params("arbitrary", "arbitrary"),
    )(w, m, v, *gs)


SMALL_ROWS = 16


def kernel(x, a_pre_norm, a_w_in, a_conv_w, a_w_out, a_post_norm, ffn_pre_norm, ffn_w_gate_up, ffn_w_down, ffn_post_norm, kv_norm, w_kv, b_pre_norm, b_w_q, b_sinks, b_w_o, b_post_norm, loss_target, m_a_pre_norm, m_a_w_in, m_a_conv_w, m_a_w_out, m_a_post_norm, m_ffn_pre_norm, m_ffn_w_gate_up, m_ffn_w_down, m_ffn_post_norm, m_kv_norm, m_w_kv, m_b_pre_norm, m_b_w_q, m_b_sinks, m_b_w_o, m_b_post_norm, v_a_pre_norm, v_a_w_in, v_a_conv_w, v_a_w_out, v_a_post_norm, v_ffn_pre_norm, v_ffn_w_gate_up, v_ffn_w_down, v_ffn_post_norm, v_kv_norm, v_w_kv, v_b_pre_norm, v_b_w_q, v_b_sinks, v_b_w_o, v_b_post_norm):
    T, D = x.shape[1], x.shape[2]
    xi, yi, ci = _place()
    p = 2 * xi + yi
    p_arr = jnp.reshape(p, (1,)).astype(jnp.int32)
    c_arr = jnp.reshape(ci, (1,)).astype(jnp.int32)
    me_arr = jnp.reshape(4 * xi + 2 * yi + ci, (1,)).astype(jnp.int32)
    qd = D // N_CHIPS

    big = {"w_in": a_w_in[0], "w_out": a_w_out[0], "gu0": ffn_w_gate_up[0], "gu1": ffn_w_gate_up[1],
           "wd0": ffn_w_down[0], "wd1": ffn_w_down[1], "w_kv": w_kv, "w_q": b_w_q[0], "w_o": b_w_o[0]}
    names = list(big)
    small_shard = jnp.concatenate([a_pre_norm, a_post_norm, a_conv_w[0], jnp.zeros((3, qd), F32)], axis=0)
    *full, small_full = allgather_weights([big[k].astype(BF16) for k in names], small_shard)
    full = dict(zip(names, full))
    rows = lambda k: jnp.transpose(small_full[:, k], (1, 0, 2)).reshape(-1, D)
    wts = {"w_in": full["w_in"], "gu0": full["gu0"], "gu1": full["gu1"],
           "w_out": full["w_out"].reshape(D, D), "w_o": full["w_o"].reshape(D, D),
           "wd0": full["wd0"].reshape(-1, D), "wd1": full["wd1"].reshape(-1, D),
           "w_kv": full["w_kv"].reshape(1, D, -1), "w_q": full["w_q"].reshape(1, D, D)}
    vec = {"a_pre": rows(slice(0, 1)), "a_post": rows(slice(1, 2)), "conv_w": rows(slice(2, 5)),
           "ffn_pre0": ffn_pre_norm[0:1], "ffn_pre1": ffn_pre_norm[1:2],
           "ffn_post0": ffn_post_norm[0:1], "ffn_post1": ffn_post_norm[1:2],
           "kv_norm": kv_norm[None], "b_pre": b_pre_norm, "b_post": b_post_norm, "sinks": b_sinks}

    loss, dx, grads, small = local_step(x[0], loss_target[0], wts, vec)

    pad = lambda a: jnp.pad(a, ((0, 0), (0, D - a.shape[1])))
    small_block = jnp.concatenate(
        [small["a_pre"], small["a_post"], small["conv_w"][0:3], small["ffn_pre0"], small["ffn_pre1"],
         small["ffn_post0"], small["ffn_post1"], small["kv_norm"], small["b_pre"], small["b_post"],
         pad(small["sinks"][0:1]), jnp.zeros((SMALL_ROWS - 13, D), F32)], axis=0)
    own = [grads[k].reshape((N_CHIPS, 2, big[k].shape[0] // 2, big[k].shape[1])) for k in names]
    got = pair_exchange(own)
    sums = [pair_add(o, g, c_arr, name="pair_add_" + k) for k, o, g in zip(names, own, got)]
    *got, small_blocks = chip_exchange(sums, small_block)
    halves = [chip_reduce(s, g, p_arr, name="chip_reduce_" + k) for k, s, g in zip(names, sums, got)]
    quarter = dict(zip(names, [q.reshape(big[k].shape) for k, q in zip(names, half_exchange(halves))]))
    small_sum = small_reduce(small_blocks, me_arr)

    out = {}
    out["a_w_in"] = adamw(a_w_in, [quarter["w_in"]], m_a_w_in, v_a_w_in, name="adamw_a_w_in")
    out["a_w_out"] = adamw(a_w_out, [quarter["w_out"]], m_a_w_out, v_a_w_out, name="adamw_a_w_out")
    out["ffn_w_gate_up"] = adamw(ffn_w_gate_up, [quarter["gu0"], quarter["gu1"]], m_ffn_w_gate_up, v_ffn_w_gate_up,
                                 name="adamw_ffn_w_gate_up")
    out["ffn_w_down"] = adamw(ffn_w_down, [quarter["wd0"], quarter["wd1"]], m_ffn_w_down, v_ffn_w_down,
                              name="adamw_ffn_w_down")
    out["w_kv"] = [o[0] for o in adamw(w_kv[None], [quarter["w_kv"]], m_w_kv[None], v_w_kv[None], name="adamw_w_kv")]
    out["b_w_q"] = adamw(b_w_q, [quarter["w_q"]], m_b_w_q, v_b_w_q, name="adamw_b_w_q")
    out["b_w_o"] = adamw(b_w_o, [quarter["w_o"]], m_b_w_o, v_b_w_o, name="adamw_b_w_o")

    def pack(a_pre, a_post, conv, ffn_pre, ffn_post, kvn, b_pre, b_post, sinks):
        return jnp.concatenate([pad(a_pre), pad(a_post), pad(conv[0]), ffn_pre, ffn_post, kvn[None], b_pre, b_post,
                                pad(sinks), jnp.zeros((SMALL_ROWS - 13, D), F32)], axis=0)

    g_small = jnp.concatenate([pad(lax.dynamic_slice(small_sum, (0, p * qd), (5, qd))), small_sum[5:]], axis=0)
    w_small = pack(a_pre_norm, a_post_norm, a_conv_w, ffn_pre_norm, ffn_post_norm, kv_norm, b_pre_norm, b_post_norm,
                   b_sinks)
    m_small = pack(m_a_pre_norm, m_a_post_norm, m_a_conv_w, m_ffn_pre_norm, m_ffn_post_norm, m_kv_norm,
                   m_b_pre_norm, m_b_post_norm, m_b_sinks)
    v_small = pack(v_a_pre_norm, v_a_post_norm, v_a_conv_w, v_ffn_pre_norm, v_ffn_post_norm, v_kv_norm,
                   v_b_pre_norm, v_b_post_norm, v_b_sinks)
    packed = adamw(w_small[None], [g_small], m_small[None], v_small[None], name="adamw_small")
    ns = b_sinks.shape[1]
    unpack = lambda a: {"a_pre_norm": a[0:1, :qd], "a_post_norm": a[1:2, :qd], "a_conv_w": a[None, 2:5, :qd],
                        "ffn_pre_norm": a[5:7], "ffn_post_norm": a[7:9], "kv_norm": a[9], "b_pre_norm": a[10:11],
                        "b_post_norm": a[11:12], "b_sinks": a[12:13, :ns]}
    unpacked = [unpack(a[0]) for a in packed]
    for k in unpacked[0]:
        out[k] = [u[k] for u in unpacked]

    order = ["a_pre_norm", "a_w_in", "a_conv_w", "a_w_out", "a_post_norm", "ffn_pre_norm", "ffn_w_gate_up",
             "ffn_w_down", "ffn_post_norm", "kv_norm", "w_kv", "b_pre_norm", "b_w_q", "b_sinks", "b_w_o",
             "b_post_norm"]
    total_loss = lax.psum(loss[0, 0], ("x", "y", "c"))
    return (total_loss, dx[None], *[out[k][0] for k in order], *[out[k][1] for k in order],
            *[out[k][2] for k in order], *[out[k][3] for k in order])
```

```python
import math

import jax
import jax.numpy as jnp
from jax import lax
from jax.experimental import pallas as pl
from jax.experimental.pallas import tpu as pltpu

F32 = jnp.float32
BF16 = jnp.bfloat16
SDS = jax.ShapeDtypeStruct
MESH = pl.DeviceIdType.MESH

EPS = 1e-6
NEG = -1e30
HEAD_DIM = 64
N_KV_HEADS = 4
BLOCK = 128
ROT_DIM = HEAD_DIM // 4
ROPE_THETA = 500000.0
N_CHIPS = 4

ADAM_LR = 0.001
ADAM_B1 = 0.9
ADAM_B2 = 0.999
ADAM_EPS = 1e-08
ADAM_WD = 0.01
ADAM_STEP = 10

VMEM_LIMIT_BYTES = 52 * 1024 * 1024
ROW_TILE = 512
BF16_ROWS = 16


def _params(*semantics):
    return pltpu.CompilerParams(dimension_semantics=semantics, vmem_limit_bytes=VMEM_LIMIT_BYTES)


def _rms_r(xf):
    return lax.rsqrt(jnp.mean(xf * xf, axis=-1, keepdims=True) + EPS)


def _rmsnorm_bwd(xf, g, dy):
    r = _rms_r(xf)
    xh = xf * r
    gd = g * dy
    return r * (gd - xh * jnp.mean(xh * gd, axis=-1, keepdims=True)), xh


def _dot(a, b):
    return jnp.dot(a, b, preferred_element_type=F32)


def _dot_nt(a, b):
    return lax.dot_general(a, b, (((1,), (1,)), ((), ())), preferred_element_type=F32)


def _dot_tn(a, b):
    return lax.dot_general(a, b, (((0,), (0,)), ((), ())), preferred_element_type=F32)


def _accumulate(ref, first, value):
    @pl.when(first)
    def _():
        ref[...] = value

    @pl.when(jnp.logical_not(first))
    def _():
        ref[...] += value


def norm_matmul(x, g, w, *, split, name, tm=ROW_TILE):
    T, D = x.shape
    P, _, ws = w.shape
    per = P // split

    def body(x_ref, g_ref, w_ref, o_ref, xn_ref):
        @pl.when(pl.program_id(1) == 0)
        def _():
            xf = x_ref[...]
            xn_ref[...] = (xf * _rms_r(xf) * g_ref[...]).astype(BF16)

        o_ref[...] = _dot(xn_ref[...], w_ref[...]).astype(BF16)

    return pl.pallas_call(
        body, name=name, grid=(T // tm, P),
        in_specs=[pl.BlockSpec((tm, D), lambda i, j: (i, 0)),
                  pl.BlockSpec((1, D), lambda i, j: (0, 0)),
                  pl.BlockSpec((None, D, ws), lambda i, j: (j, 0, 0))],
        out_specs=[pl.BlockSpec((None, tm, ws), lambda i, j: (j // per, i, j % per)),
                   pl.BlockSpec((tm, D), lambda i, j: (i, 0))],
        out_shape=[SDS((split, T, per * ws), BF16), SDS((T, D), BF16)],
        compiler_params=_params("parallel", "arbitrary"),
    )(x, g, w)


def _shift_down(prev, cur, by):
    big = jnp.concatenate([prev, cur], axis=0)
    return pltpu.roll(big, by, 0)[prev.shape[0]:]


def _shift_up(cur, nxt, by):
    big = jnp.concatenate([cur, nxt], axis=0)
    return pltpu.roll(big, big.shape[0] - by, 0)[:cur.shape[0]]


def conv_mix_out(bcx, conv_w, w_out, g_post, res, *, name, tm=ROW_TILE):
    T, D = res.shape
    hb = tm // BF16_ROWS

    def body(b_ref, c_ref, u_ref, cp_ref, up_ref, cw_ref, w_ref, g_ref, r_ref, h_ref, z_ref, y_ref):
        i = pl.program_id(0)
        cu = c_ref[...].astype(F32) * u_ref[...].astype(F32)
        cup = cp_ref[...].astype(F32) * up_ref[...].astype(F32)
        cup = jnp.where(i == 0, 0.0, cup)
        cv = (cw_ref[0:1, :] * _shift_down(cup, cu, 2) + cw_ref[1:2, :] * _shift_down(cup, cu, 1)
              + cw_ref[2:3, :] * cu)
        y = (b_ref[...].astype(F32) * cv).astype(BF16)
        y_ref[...] = y
        z = _dot(y, w_ref[...])
        z_ref[...] = z.astype(BF16)
        h_ref[...] = r_ref[...] + z * _rms_r(z) * g_ref[...]

    tile = lambda col: pl.BlockSpec((tm, D), lambda i: (i, col))
    halo = lambda col: pl.BlockSpec((BF16_ROWS, D), lambda i: (jnp.maximum(i * hb - 1, 0), col))
    row = pl.BlockSpec((tm, D), lambda i: (i, 0))
    return pl.pallas_call(
        body, name=name, grid=(T // tm,),
        in_specs=[tile(0), tile(1), tile(2), halo(1), halo(2),
                  pl.BlockSpec((3, D), lambda i: (0, 0)),
                  pl.BlockSpec((D, D), lambda i: (0, 0)),
                  pl.BlockSpec((1, D), lambda i: (0, 0)), row],
        out_specs=[row, row, row],
        out_shape=[SDS((T, D), F32), SDS((T, D), BF16), SDS((T, D), BF16)],
        compiler_params=_params("parallel"),
    )(bcx, bcx, bcx, bcx, bcx, conv_w, w_out, g_post, res)


def plain_mix_out(a, w, g_post, res, *, name, tm=ROW_TILE):
    T, D = res.shape
    K = a.shape[1]

    def body(a_ref, w_ref, g_ref, r_ref, h_ref, z_ref):
        z = _dot(a_ref[...], w_ref[...])
        z_ref[...] = z.astype(BF16)
        h_ref[...] = r_ref[...] + z * _rms_r(z) * g_ref[...]

    row = pl.BlockSpec((tm, D), lambda i: (i, 0))
    return pl.pallas_call(
        body, name=name, grid=(T // tm,),
        in_specs=[pl.BlockSpec((tm, K), lambda i: (i, 0)),
                  pl.BlockSpec((K, D), lambda i: (0, 0)),
                  pl.BlockSpec((1, D), lambda i: (0, 0)), row],
        out_specs=[row, row],
        out_shape=[SDS((T, D), F32), SDS((T, D), BF16)],
        compiler_params=_params("parallel"),
    )(a, w, g_post, res)


def swiglu_mix_out(gu, w_down, g_post, res, *, tk, name, target=None, tm=ROW_TILE):
    T, D = res.shape
    F = gu.shape[2]
    nk = F // tk
    with_loss = target is not None

    def body(*refs):
        if with_loss:
            g_ref, u_ref, w_ref, gp_ref, r_ref, t_ref, h_ref, z_ref, a_ref, loss_ref, acc = refs
        else:
            g_ref, u_ref, w_ref, gp_ref, r_ref, h_ref, z_ref, a_ref, acc = refs
        i, k = pl.program_id(0), pl.program_id(1)
        g = g_ref[...].astype(F32)
        a = (g * jax.nn.sigmoid(g) * u_ref[...].astype(F32)).astype(BF16)
        a_ref[...] = a
        _accumulate(acc, k == 0, _dot(a, w_ref[...]))

        @pl.when(k == nk - 1)
        def _():
            z = acc[...]
            z_ref[...] = z.astype(BF16)
            h = r_ref[...] + z * _rms_r(z) * gp_ref[...]
            if with_loss:
                diff = h - t_ref[...]
                h_ref[...] = diff * (1.0 / D)
                part = jnp.full(loss_ref.shape, 0.5 / D, F32) * jnp.sum(diff * diff)
                _accumulate(loss_ref, i == 0, part)
            else:
                h_ref[...] = h

    row = pl.BlockSpec((tm, D), lambda i, k: (i, 0))
    in_specs = [pl.BlockSpec((None, tm, tk), lambda i, k: (0, i, k)),
                pl.BlockSpec((None, tm, tk), lambda i, k: (1, i, k)),
                pl.BlockSpec((tk, D), lambda i, k: (k, 0)),
                pl.BlockSpec((1, D), lambda i, k: (0, 0)), row]
    out_specs = [row, row, pl.BlockSpec((tm, tk), lambda i, k: (i, k))]
    out_shape = [SDS((T, D), F32), SDS((T, D), BF16), SDS((T, F), BF16)]
    args = [gu, gu, w_down, g_post, res]
    if with_loss:
        in_specs.append(row)
        args.append(target)
        out_specs.append(pl.BlockSpec((8, 128), lambda i, k: (0, 0)))
        out_shape.append(SDS((8, 128), F32))
    return pl.pallas_call(
        body, name=name, grid=(T // tm, nk), in_specs=in_specs, out_specs=out_specs, out_shape=out_shape,
        scratch_shapes=[pltpu.VMEM((tm, D), F32)],
        compiler_params=_params("arbitrary", "arbitrary"),
    )(*args)


def rope_tables(T):
    half = ROT_DIM // 2
    inv_freq = ROPE_THETA ** (-jnp.arange(0, ROT_DIM, 2, dtype=F32) / ROT_DIM)
    ang = (jnp.arange(T, dtype=F32)[:, None] * inv_freq[None, :]).T
    cos, sin = jnp.cos(ang), jnp.sin(ang)
    rest = HEAD_DIM - ROT_DIM
    one, zero = jnp.ones((rest, T), F32), jnp.zeros((rest, T), F32)
    zh = jnp.zeros((half, T), F32)
    fac = jnp.concatenate([cos, cos, one], axis=0)
    up = jnp.concatenate([-sin, zh, zero], axis=0)
    down = jnp.concatenate([zh, sin, zero], axis=0)
    return jnp.stack([fac, up, down])


def _rope(t, tab):
    half = ROT_DIM // 2
    return t * tab[0] + pltpu.roll(t, HEAD_DIM - half, 0) * tab[1] + pltpu.roll(t, half, 0) * tab[2]


def _rope_t(d, tab):
    half = ROT_DIM // 2
    return d * tab[0] + pltpu.roll(d * tab[1], half, 0) + pltpu.roll(d * tab[2], HEAD_DIM - half, 0)


def _head(t, h):
    return t[h * HEAD_DIM:(h + 1) * HEAD_DIM]


def _band(n, group):
    kj = lax.broadcasted_iota(jnp.int32, (2 * BLOCK, BLOCK), 0)
    qi = lax.broadcasted_iota(jnp.int32, (2 * BLOCK, BLOCK), 1)
    mask = (kj > qi) & (kj <= qi + BLOCK) & ((n > 0) | (kj >= BLOCK))
    return jnp.tile(mask, (1, group))


def _attn_specs(D, kvd):
    prev = lambda n: jnp.maximum(n - 1, 0)
    return [pl.BlockSpec((BLOCK, D), lambda n: (n, 0)),
            pl.BlockSpec((BLOCK, kvd), lambda n: (prev(n), 0)),
            pl.BlockSpec((BLOCK, kvd), lambda n: (n, 0)),
            pl.BlockSpec((BLOCK, kvd), lambda n: (prev(n), 1)),
            pl.BlockSpec((BLOCK, kvd), lambda n: (n, 1)),
            pl.BlockSpec((3, HEAD_DIM, BLOCK), lambda n: (0, 0, prev(n))),
            pl.BlockSpec((3, HEAD_DIM, BLOCK), lambda n: (0, 0, n)),
            pl.BlockSpec(memory_space=pltpu.SMEM)]


def _attn_operands(q_ref, kp_ref, k_ref, vp_ref, v_ref, tp_ref, t_ref):
    flip = lambda ref: ref[...].astype(F32).T
    tab = t_ref[...]
    kt = jnp.concatenate([flip(kp_ref), flip(k_ref)], axis=1)
    vt = jnp.concatenate([flip(vp_ref), flip(v_ref)], axis=1)
    return flip(q_ref), kt, vt, tab, jnp.concatenate([tp_ref[...], tab], axis=2)


def _group_heads(t, j, group, tab=None):
    heads = [_head(t, j * group + g) for g in range(group)]
    if tab is not None:
        heads = [_rope(h, tab) for h in heads]
    return jnp.concatenate(heads, axis=1).astype(BF16)


def _sink_row(s_ref, j, group):
    which = lax.broadcasted_iota(jnp.int32, (1, group * BLOCK), 1) // BLOCK
    row = jnp.zeros((1, group * BLOCK), F32)
    for g in range(group):
        row = jnp.where(which == g, s_ref[0, j * group + g], row)
    return row


def _softmax_block(k_j, q_j, sink, mask, scale):
    s = jnp.where(mask, _dot_tn(k_j, q_j) * scale, NEG)
    m = jnp.maximum(jnp.max(s, axis=0, keepdims=True), sink)
    e = jnp.exp(s - m)
    es = jnp.exp(sink - m)
    inv = 1.0 / (jnp.sum(e, axis=0, keepdims=True) + es)
    return e * inv, es * inv


def attention_fwd(q, kv, tabs, sinks, *, name):
    T, D = q.shape
    kvd = kv.shape[1] // 2
    group = D // HEAD_DIM // N_KV_HEADS
    scale = 1.0 / math.sqrt(HEAD_DIM)

    def body(q_ref, kp_ref, k_ref, vp_ref, v_ref, tp_ref, t_ref, s_ref, o_ref):
        mask = _band(pl.program_id(0), group)
        qt, kt, vt, tab, tab2 = _attn_operands(q_ref, kp_ref, k_ref, vp_ref, v_ref, tp_ref, t_ref)
        outs = []
        for j in range(N_KV_HEADS):
            k_j = _rope(_head(kt, j), tab2).astype(BF16)
            p, _ = _softmax_block(k_j, _group_heads(qt, j, group, tab), _sink_row(s_ref, j, group), mask, scale)
            o_j = _dot(_head(vt, j).astype(BF16), p.astype(BF16))
            outs += [o_j[:, g * BLOCK:(g + 1) * BLOCK] for g in range(group)]
        o_ref[...] = jnp.concatenate(outs, axis=0).T.astype(BF16)

    return pl.pallas_call(
        body, name=name, grid=(T // BLOCK,),
        in_specs=_attn_specs(D, kvd),
        out_specs=pl.BlockSpec((BLOCK, D), lambda n: (n, 0)),
        out_shape=SDS((T, D), BF16),
        compiler_params=_params("parallel"),
    )(q, kv, kv, kv, kv, tabs, tabs, sinks)


def attention_bwd(q, kv, tabs, sinks, do, *, name):
    T, D = q.shape
    kvd = kv.shape[1] // 2
    heads = D // HEAD_DIM
    group = heads // N_KV_HEADS
    scale = 1.0 / math.sqrt(HEAD_DIM)

    def body(q_ref, kp_ref, k_ref, vp_ref, v_ref, tp_ref, t_ref, s_ref, do_ref, dq_ref, dc_ref, dp_ref, ds_ref):
        n = pl.program_id(0)
        mask = _band(n, group)
        qt, kt, vt, tab, tab2 = _attn_operands(q_ref, kp_ref, k_ref, vp_ref, v_ref, tp_ref, t_ref)
        dot = do_ref[...].astype(F32).T
        lane = lax.broadcasted_iota(jnp.int32, (8, 128), 1)
        dsink = jnp.zeros((8, 128), F32)
        dqs, dks, dvs = [], [], []
        for j in range(N_KV_HEADS):
            k_j = _rope(_head(kt, j), tab2).astype(BF16)
            v_j = _head(vt, j).astype(BF16)
            q_j = _group_heads(qt, j, group, tab)
            do_j = _group_heads(dot, j, group)
            p, p_sink = _softmax_block(k_j, q_j, _sink_row(s_ref, j, group), mask, scale)
            dp = _dot_tn(v_j, do_j)
            dl = jnp.sum(p * dp, axis=0, keepdims=True)
            dsc = (p * (dp - dl) * scale).astype(BF16)
            dq_j = _dot(k_j, dsc)
            dqs += [_rope_t(dq_j[:, g * BLOCK:(g + 1) * BLOCK], tab) for g in range(group)]
            dks.append(_rope_t(_dot_nt(q_j, dsc), tab2))
            dvs.append(_dot_nt(do_j, p.astype(BF16)))
            weight = p_sink * dl
            for g in range(group):
                dsink = dsink - jnp.where(lane == j * group + g, jnp.sum(weight[:, g * BLOCK:(g + 1) * BLOCK]), 0.0)
        dq_ref[...] = jnp.concatenate(dqs, axis=0).T.astype(BF16)
        dkv = jnp.concatenate(dks + dvs, axis=0)
        dp_ref[...] = dkv[:, :BLOCK].T
        dc_ref[...] = dkv[:, BLOCK:].T
        _accumulate(ds_ref, n == 0, dsink)

    blk = lambda w: pl.BlockSpec((BLOCK, w), lambda n: (n, 0))
    return pl.pallas_call(
        body, name=name, grid=(T // BLOCK,),
        in_specs=_attn_specs(D, kvd) + [blk(D)],
        out_specs=[blk(D), blk(2 * kvd), blk(2 * kvd), pl.BlockSpec((8, 128), lambda n: (0, 0))],
        out_shape=[SDS((T, D), BF16), SDS((T, 2 * kvd), F32), SDS((T, 2 * kvd), F32), SDS((8, 128), F32)],
        compiler_params=_params("arbitrary"),
    )(q, kv, kv, kv, kv, tabs, tabs, sinks, do)


def combine_dkv(d_cur, d_prev, *, name):
    T, W = d_cur.shape
    tm = ROW_TILE
    nt, per, last = T // tm, tm // BLOCK, T // BLOCK - 1

    def body(c_ref, p_ref, pn_ref, o_ref):
        nxt = jnp.where(pl.program_id(0) == nt - 1, 0.0, pn_ref[...])
        o_ref[...] = (c_ref[...] + jnp.concatenate([p_ref[BLOCK:, :], nxt], axis=0)).astype(BF16)

    return pl.pallas_call(
        body, name=name, grid=(nt,),
        in_specs=[pl.BlockSpec((tm, W), lambda i: (i, 0)), pl.BlockSpec((tm, W), lambda i: (i, 0)),
                  pl.BlockSpec((BLOCK, W), lambda i: (jnp.minimum((i + 1) * per, last), 0))],
        out_specs=pl.BlockSpec((tm, W), lambda i: (i, 0)),
        out_shape=SDS((T, W), BF16),
        compiler_params=_params("parallel"),
    )(d_cur, d_prev, d_prev)


def normbwd_matmul_nt(z, g, dh, w, *, tn, name, gu=None, tm=ROW_TILE):
    T, D = z.shape
    K = w.shape[0]
    swiglu = gu is not None

    def body(*refs):
        if swiglu:
            z_ref, g_ref, dh_ref, w_ref, gg_ref, uu_ref, dz_ref, dg_ref, o_ref = refs
        else:
            z_ref, g_ref, dh_ref, w_ref, dz_ref, dg_ref, o_ref = refs
        i, j = pl.program_id(0), pl.program_id(1)

        @pl.when(j == 0)
        def _():
            dh_ = dh_ref[...]
            dz, zh = _rmsnorm_bwd(z_ref[...].astype(F32), g_ref[...], dh_)
            dz_ref[...] = dz.astype(BF16)
            _accumulate(dg_ref, i == 0, jnp.sum(dh_ * zh, axis=0, keepdims=True))

        d = _dot_nt(dz_ref[...], w_ref[...])
        if swiglu:
            g_ = gg_ref[...].astype(F32)
            sg = jax.nn.sigmoid(g_)
            o_ref[0] = (d * uu_ref[...].astype(F32) * (sg * (1.0 + g_ * (1.0 - sg)))).astype(BF16)
            o_ref[1] = (d * (g_ * sg)).astype(BF16)
        else:
            o_ref[...] = d.astype(BF16)

    row = pl.BlockSpec((tm, D), lambda i, j: (i, 0))
    in_specs = [row, pl.BlockSpec((1, D), lambda i, j: (0, 0)), row, pl.BlockSpec((tn, D), lambda i, j: (j, 0))]
    args = [z, g, dh, w]
    if swiglu:
        in_specs += [pl.BlockSpec((None, tm, tn), lambda i, j: (0, i, j)),
                     pl.BlockSpec((None, tm, tn), lambda i, j: (1, i, j))]
        args += [gu, gu]
        o_spec, o_shape = pl.BlockSpec((2, tm, tn), lambda i, j: (0, i, j)), SDS((2, T, K), BF16)
    else:
        o_spec, o_shape = pl.BlockSpec((tm, tn), lambda i, j: (i, j)), SDS((T, K), BF16)
    return pl.pallas_call(
        body, name=name, grid=(T // tm, K // tn), in_specs=in_specs,
        out_specs=[row, pl.BlockSpec((1, D), lambda i, j: (0, 0)), o_spec],
        out_shape=[SDS((T, D), BF16), SDS((1, D), F32), o_shape],
        compiler_params=_params("arbitrary", "arbitrary"),
    )(*args)


def matmul_nt_normbwd(da, w, h_in, g, dh_out, *, name, tm=ROW_TILE):
    T, D = h_in.shape
    P, _, ws = w.shape
    per = da.shape[2] // ws

    def body(da_ref, w_ref, h_ref, g_ref, dh_ref, o_ref, dg_ref, acc):
        i, k = pl.program_id(0), pl.program_id(1)
        _accumulate(acc, k == 0, _dot_nt(da_ref[...], w_ref[...]))

        @pl.when(k == P - 1)
        def _():
            dn = acc[...]
            dx, hh = _rmsnorm_bwd(h_ref[...], g_ref[...], dn)
            o_ref[...] = dh_ref[...] + dx
            _accumulate(dg_ref, i == 0, jnp.sum(dn * hh, axis=0, keepdims=True))

    row = pl.BlockSpec((tm, D), lambda i, k: (i, 0))
    vec = pl.BlockSpec((1, D), lambda i, k: (0, 0))
    return pl.pallas_call(
        body, name=name, grid=(T // tm, P),
        in_specs=[pl.BlockSpec((None, tm, ws), lambda i, k: (k // per, i, k % per)),
                  pl.BlockSpec((None, D, ws), lambda i, k: (k, 0, 0)), row, vec, row],
        out_specs=[row, vec],
        out_shape=[SDS((T, D), F32), SDS((1, D), F32)],
        scratch_shapes=[pltpu.VMEM((tm, D), F32)],
        compiler_params=_params("arbitrary", "arbitrary"),
    )(da, w, h_in, g, dh_out)


def matmul_tn(a, b, *, ta, tb, tt, name):
    T, Ka = a.shape
    S, _, Nb = b.shape
    per = Nb // tb
    nk = T // tt

    def body(a_ref, b_ref, o_ref, acc):
        k = pl.program_id(2)
        _accumulate(acc, k == 0, _dot_tn(a_ref[...], b_ref[...]))

        @pl.when(k == nk - 1)
        def _():
            o_ref[...] = acc[...].astype(BF16)

    return pl.pallas_call(
        body, name=name, grid=(Ka // ta, S * per, nk),
        in_specs=[pl.BlockSpec((tt, ta), lambda i, j, k: (k, i)),
                  pl.BlockSpec((None, tt, tb), lambda i, j, k: (j // per, k, j % per))],
        out_specs=pl.BlockSpec((None, ta, tb), lambda i, j, k: (j, i, 0)),
        out_shape=SDS((S * per, Ka, tb), BF16),
        scratch_shapes=[pltpu.VMEM((ta, tb), F32)],
        compiler_params=_params("parallel", "parallel", "arbitrary"),
    )(a, b)


def conv_bwd(dy, bcx, conv_w, *, name, tm=ROW_TILE):
    T, D = dy.shape
    nt = T // tm
    hb = tm // BF16_ROWS
    last = T // BF16_ROWS - 1

    def body(dy_ref, dyn_ref, b_ref, bn_ref, c_ref, u_ref, cp_ref, up_ref, cw_ref, o_ref, dw_ref):
        i = pl.program_id(0)
        c, u = c_ref[...].astype(F32), u_ref[...].astype(F32)
        cu = c * u
        cup = jnp.where(i == 0, 0.0, cp_ref[...].astype(F32) * up_ref[...].astype(F32))
        cu1, cu2 = _shift_down(cup, cu, 1), _shift_down(cup, cu, 2)
        w0, w1, w2 = cw_ref[0:1, :], cw_ref[1:2, :], cw_ref[2:3, :]
        dyf = dy_ref[...].astype(F32)
        o_ref[:, 0:D] = (dyf * (w0 * cu2 + w1 * cu1 + w2 * cu)).astype(BF16)
        dcv = dyf * b_ref[...].astype(F32)
        dcvn = jnp.where(i == nt - 1, 0.0, dyn_ref[...].astype(F32) * bn_ref[...].astype(F32))
        dcu = w2 * dcv + w1 * _shift_up(dcv, dcvn, 1) + w0 * _shift_up(dcv, dcvn, 2)
        o_ref[:, D:2 * D] = (dcu * u).astype(BF16)
        o_ref[:, 2 * D:3 * D] = (dcu * c).astype(BF16)
        row = lax.broadcasted_iota(jnp.int32, (8, D), 0)
        dw = jnp.zeros((8, D), F32)
        for tap, t in enumerate((cu2, cu1, cu)):
            dw = jnp.where(row == tap, jnp.sum(dcv * t, axis=0, keepdims=True), dw)
        _accumulate(dw_ref, i == 0, dw)

    tile = lambda col: pl.BlockSpec((tm, D), lambda i: (i, col))
    prev = lambda col: pl.BlockSpec((BF16_ROWS, D), lambda i: (jnp.maximum(i * hb - 1, 0), col))
    nxt = lambda col: pl.BlockSpec((BF16_ROWS, D), lambda i: (jnp.minimum((i + 1) * hb, last), col))
    return pl.pallas_call(
        body, name=name, grid=(nt,),
        in_specs=[tile(0), nxt(0), tile(0), nxt(0), tile(1), tile(2), prev(1), prev(2),
                  pl.BlockSpec((3, D), lambda i: (0, 0))],
        out_specs=[pl.BlockSpec((tm, 3 * D), lambda i: (i, 0)), pl.BlockSpec((8, D), lambda i: (0, 0))],
        out_shape=[SDS((T, 3 * D), BF16), SDS((8, D), F32)],
        compiler_params=_params("arbitrary"),
    )(dy, dy, bcx, bcx, bcx, bcx, bcx, bcx, conv_w)


def local_step(x, target, wts, vec):
    T, D = x.shape
    F = wts["wd0"].shape[0]
    tabs = rope_tables(T)
    ffn_chunk = wts["gu0"].shape[2]

    bcx, xn1 = norm_matmul(x, vec["a_pre"], wts["w_in"], split=1, name="a_in")
    bcx = bcx[0]
    h1, z0, y0 = conv_mix_out(bcx, vec["conv_w"], wts["w_out"], vec["a_post"], x, name="a_out")
    gu0, xn2 = norm_matmul(h1, vec["ffn_pre0"], wts["gu0"], split=2, name="ffn0_in")
    h2, z1, act0 = swiglu_mix_out(gu0, wts["wd0"], vec["ffn_post0"], h1, tk=ffn_chunk, name="ffn0_out")
    kvp, xkv = norm_matmul(h2, vec["kv_norm"], wts["w_kv"], split=1, name="kv_in")
    qp, xq = norm_matmul(h2, vec["b_pre"], wts["w_q"], split=1, name="q_in")
    kvp, qp = kvp[0], qp[0]
    attn = attention_fwd(qp, kvp, tabs, vec["sinks"], name="attn_fwd")
    h3, z2 = plain_mix_out(attn, wts["w_o"], vec["b_post"], h2, name="attn_out")
    gu1, xn3 = norm_matmul(h3, vec["ffn_pre1"], wts["gu1"], split=2, name="ffn1_in")
    dy, z3, act1, loss = swiglu_mix_out(gu1, wts["wd1"], vec["ffn_post1"], h3, tk=ffn_chunk, name="ffn1_out",
                                        target=target)

    grads, small = {}, {}
    tt = 1024 if T % 1024 == 0 else ROW_TILE

    def ffn_bwd(layer, z, gu, act, xn, h_in, dh, w_down, w_gu):
        dz, small["ffn_post%d" % layer], dgu = normbwd_matmul_nt(
            z, vec["ffn_post%d" % layer], dh, w_down, tn=ffn_chunk, gu=gu, name="ffn%d_out_bwd" % layer)
        grads["wd%d" % layer] = matmul_tn(act, dz[None], ta=ffn_chunk, tb=D, tt=tt, name="ffn%d_dwd" % layer)
        dh_in, small["ffn_pre%d" % layer] = matmul_nt_normbwd(
            dgu, w_gu, h_in, vec["ffn_pre%d" % layer], dh, name="ffn%d_in_bwd" % layer)
        grads["gu%d" % layer] = matmul_tn(xn, dgu, ta=D, tb=ffn_chunk, tt=tt, name="ffn%d_dwgu" % layer)
        return dh_in

    dh3 = ffn_bwd(1, z3, gu1, act1, xn3, h3, dy, wts["wd1"], wts["gu1"])
    dz2, small["b_post"], dattn = normbwd_matmul_nt(z2, vec["b_post"], dh3, wts["w_o"], tn=D, name="attn_out_bwd")
    grads["w_o"] = matmul_tn(attn, dz2[None], ta=D, tb=D, tt=tt, name="attn_dwo")
    dq, dkv_cur, dkv_prev, dsinks = attention_bwd(qp, kvp, tabs, vec["sinks"], dattn, name="attn_bwd")
    small["sinks"] = dsinks
    dkv = combine_dkv(dkv_cur, dkv_prev, name="attn_dkv")
    grads["w_q"] = matmul_tn(xq, dq[None], ta=D, tb=D, tt=tt, name="attn_dwq")
    grads["w_kv"] = matmul_tn(xkv, dkv[None], ta=D, tb=dkv.shape[1], tt=tt, name="attn_dwkv")
    dh2, small["b_pre"] = matmul_nt_normbwd(dq[None], wts["w_q"], h2, vec["b_pre"], dh3, name="q_in_bwd")
    dh2, small["kv_norm"] = matmul_nt_normbwd(dkv[None], wts["w_kv"], h2, vec["kv_norm"], dh2, name="kv_in_bwd")
    dh1 = ffn_bwd(0, z1, gu0, act0, xn2, h1, dh2, wts["wd0"], wts["gu0"])
    dz0, small["a_post"], dyc = normbwd_matmul_nt(z0, vec["a_post"], dh1, wts["w_out"], tn=D, name="a_out_bwd")
    grads["w_out"] = matmul_tn(y0, dz0[None], ta=D, tb=D, tt=tt, name="a_dwout")
    dbcx, small["conv_w"] = conv_bwd(dyc, bcx, vec["conv_w"], name="a_conv_bwd")
    grads["w_in"] = matmul_tn(xn1, dbcx[None], ta=D, tb=wts["w_in"].shape[2], tt=tt, name="a_dwin")
    dx, small["a_pre"] = matmul_nt_normbwd(dbcx[None], wts["w_in"], x, vec["a_pre"], dh1, name="a_in_bwd")
    return loss, dx, grads, small


HBM_SPEC = pl.BlockSpec(memory_space=pltpu.HBM)


def _place():
    return lax.axis_index("x"), lax.axis_index("y"), lax.axis_index("c")


def _other_chips(x, y):
    return [(1 - x, y), (x, 1 - y), (1 - x, 1 - y)]


def _remote(src, dst, send_sem, recv_sem, to):
    return pltpu.make_async_remote_copy(src_ref=src, dst_ref=dst, send_sem=send_sem, recv_sem=recv_sem,
                                        device_id=to, device_id_type=MESH)


def cast_quarter(w, layer, p_arr, *, name):
    _, r, cols = w.shape
    tr = _row_tile(r, 512)

    def body(p_ref, w_ref, o_ref):
        o_ref[...] = w_ref[...].astype(BF16)

    return pl.pallas_call(
        body, name=name,
        grid_spec=pltpu.PrefetchScalarGridSpec(
            num_scalar_prefetch=1, grid=(r // tr,),
            in_specs=[pl.BlockSpec((None, tr, cols), lambda i, p_ref: (layer, i, 0))],
            out_specs=pl.BlockSpec((None, tr, cols), lambda i, p_ref: (p_ref[0], i, 0))),
        out_shape=SDS((N_CHIPS, r, cols), BF16),
        compiler_params=_params("parallel"),
    )(p_arr, w)


def allgather_weights(quarters, small):
    n = len(quarters)
    halves = [s.shape[1] // 2 for s in quarters]

    def body(*refs):
        small_in = refs[n]
        outs, small_out = refs[n + 1:2 * n + 1], refs[2 * n + 1]
        send1, recv1, send2, recv2, ssend, srecv, lsem = refs[2 * n + 2:]
        x, y, c = _place()
        p = 2 * x + y
        chips = _other_chips(x, y)
        me, sibling = (x, y, c), (x, y, 1 - c)

        local = [pltpu.make_async_copy(small_in, small_out.at[p], lsem)]
        for cp in local:
            cp.start()
        sends = []
        for j, (qx, qy) in enumerate(chips):
            sends.append(_remote(small_in, small_out.at[p], ssend.at[j], srecv.at[j], (qx, qy, c)))
            for t in range(n):
                mine = outs[t].at[p, pl.ds(c * halves[t], halves[t])]
                sends.append(_remote(mine, mine, send1.at[t, j], recv1.at[t, j], (qx, qy, c)))
        for cp in sends:
            cp.start()
        passed = []
        for j, (qx, qy) in enumerate(chips):
            q = 2 * qx + qy
            for t in range(n):
                landed = outs[t].at[q, pl.ds(c * halves[t], halves[t])]
                _remote(landed, landed, send1.at[t, j], recv1.at[t, j], me).wait_recv()
                cp = _remote(landed, landed, send2.at[t, j], recv2.at[t, j], sibling)
                cp.start()
                passed.append(cp)
        for j, (qx, qy) in enumerate(chips):
            q = 2 * qx + qy
            _remote(small_out.at[q], small_out.at[q], ssend.at[j], srecv.at[j], me).wait_recv()
            for t in range(n):
                theirs = outs[t].at[q, pl.ds((1 - c) * halves[t], halves[t])]
                _remote(theirs, theirs, send2.at[t, j], recv2.at[t, j], me).wait_recv()
        for cp in sends + passed:
            cp.wait_send()
        for cp in local:
            cp.wait()

    DMA = pltpu.SemaphoreType.DMA
    return pl.pallas_call(
        body, name="allgather_weights",
        in_specs=[HBM_SPEC] * (n + 1), out_specs=[HBM_SPEC] * (n + 1),
        out_shape=[SDS(s.shape, s.dtype) for s in quarters] + [SDS((N_CHIPS,) + small.shape, small.dtype)],
        input_output_aliases={t: t for t in range(n)},
        scratch_shapes=[DMA((n, 3)), DMA((n, 3)), DMA((n, 3)), DMA((n, 3)), DMA((3,)), DMA((3,)), DMA(())],
    )(*quarters, small)


def pair_exchange(grads):
    n = len(grads)

    def body(*refs):
        ins, outs = refs[:n], refs[n:2 * n]
        send, recv = refs[2 * n:]
        x, y, c = _place()
        cps = [_remote(ins[t].at[:, 1 - c], outs[t], send.at[t], recv.at[t], (x, y, 1 - c)) for t in range(n)]
        for cp in cps:
            cp.start()
        for cp in cps:
            cp.wait()

    DMA = pltpu.SemaphoreType.DMA
    return pl.pallas_call(
        body, name="pair_exchange",
        in_specs=[HBM_SPEC] * n, out_specs=[HBM_SPEC] * n,
        out_shape=[SDS((g.shape[0],) + g.shape[2:], g.dtype) for g in grads],
        scratch_shapes=[DMA((n,)), DMA((n,))],
    )(*grads)


def pair_add(own, got, c_arr, *, name):
    P, _, h, cols = own.shape

    def body(c_ref, a_ref, b_ref, o_ref):
        o_ref[...] = (a_ref[...].astype(F32) + b_ref[...].astype(F32)).astype(BF16)

    return pl.pallas_call(
        body, name=name,
        grid_spec=pltpu.PrefetchScalarGridSpec(
            num_scalar_prefetch=1, grid=(P,),
            in_specs=[pl.BlockSpec((None, None, h, cols), lambda q, c_ref: (q, c_ref[0], 0, 0)),
                      pl.BlockSpec((None, h, cols), lambda q, c_ref: (q, 0, 0))],
            out_specs=pl.BlockSpec((None, h, cols), lambda q, c_ref: (q, 0, 0))),
        out_shape=SDS((P, h, cols), BF16),
        compiler_params=_params("parallel"),
    )(c_arr, own, got)


def chip_exchange(sums, small):
    n = len(sums)

    def body(*refs):
        ins, small_in = refs[:n], refs[n]
        outs, small_out = refs[n + 1:2 * n + 1], refs[2 * n + 1]
        send, recv, ssend, srecv, lsem = refs[2 * n + 2:]
        x, y, c = _place()
        local = pltpu.make_async_copy(small_in, small_out.at[0], lsem)
        local.start()
        cps = []
        for k in range(1, 8):
            peer = (x ^ (k >> 2 & 1), y ^ (k >> 1 & 1), c ^ (k & 1))
            cps.append(_remote(small_in, small_out.at[k], ssend.at[k - 1], srecv.at[k - 1], peer))
        for j, (qx, qy) in enumerate(_other_chips(x, y)):
            for t in range(n):
                cps.append(_remote(ins[t].at[2 * qx + qy], outs[t].at[j], send.at[t, j], recv.at[t, j], (qx, qy, c)))
        for cp in cps:
            cp.start()
        for cp in cps:
            cp.wait()
        local.wait()

    DMA = pltpu.SemaphoreType.DMA
    return pl.pallas_call(
        body, name="chip_exchange",
        in_specs=[HBM_SPEC] * (n + 1), out_specs=[HBM_SPEC] * (n + 1),
        out_shape=[SDS((3,) + s.shape[1:], s.dtype) for s in sums] + [SDS((8,) + small.shape, small.dtype)],
        scratch_shapes=[DMA((n, 3)), DMA((n, 3)), DMA((7,)), DMA((7,)), DMA(())],
    )(*sums, small)


def chip_reduce(sums, got, pc_arr, *, name):
    _, h, cols = sums.shape
    th = h // 2

    def body(pc_ref, a_ref, b_ref, o_ref):
        o_ref[...] = ((a_ref[...].astype(F32) + b_ref[0].astype(F32)) + b_ref[1].astype(F32)) + b_ref[2].astype(F32)

    return pl.pallas_call(
        body, name=name,
        grid_spec=pltpu.PrefetchScalarGridSpec(
            num_scalar_prefetch=1, grid=(h // th,),
            in_specs=[pl.BlockSpec((None, th, cols), lambda i, pc_ref: (pc_ref[0], i, 0)),
                      pl.BlockSpec((3, th, cols), lambda i, pc_ref: (0, i, 0))],
            out_specs=pl.BlockSpec((None, th, cols), lambda i, pc_ref: (pc_ref[1], i, 0))),
        out_shape=SDS((2, h, cols), F32),
        compiler_params=_params("parallel"),
    )(pc_arr, sums, got)


def small_reduce(blocks, me_arr):
    _, rows, D = blocks.shape

    def body(me_ref, b_ref, o_ref):
        me = me_ref[0]
        total = b_ref[me]
        for d in range(1, 8):
            total = total + b_ref[d ^ me]
        o_ref[...] = total

    return pl.pallas_call(
        body, name="small_reduce",
        grid_spec=pltpu.PrefetchScalarGridSpec(
            num_scalar_prefetch=1, grid=(1,),
            in_specs=[pl.BlockSpec((8, rows, D), lambda i, me_ref: (0, 0, 0))],
            out_specs=pl.BlockSpec((rows, D), lambda i, me_ref: (0, 0))),
        out_shape=SDS((rows, D), F32),
        compiler_params=_params("arbitrary"),
    )(me_arr, blocks)


def half_exchange(quarters):
    n = len(quarters)

    def body(*refs):
        outs = refs[n:2 * n]
        send, recv = refs[2 * n:]
        x, y, c = _place()
        sends = [_remote(outs[t].at[c], outs[t].at[c], send.at[t], recv.at[t], (x, y, 1 - c)) for t in range(n)]
        for cp in sends:
            cp.start()
        for t in range(n):
            theirs = outs[t].at[1 - c]
            _remote(theirs, theirs, send.at[t], recv.at[t], (x, y, c)).wait_recv()
        for cp in sends:
            cp.wait_send()

    DMA = pltpu.SemaphoreType.DMA
    return pl.pallas_call(
        body, name="half_exchange",
        in_specs=[HBM_SPEC] * n, out_specs=[HBM_SPEC] * n,
        out_shape=[SDS(q.shape, q.dtype) for q in quarters],
        input_output_aliases={t: t for t in range(n)},
        scratch_shapes=[DMA((n,)), DMA((n,))],
    )(*quarters)


def _row_tile(rows, limit=256):
    return max(t for t in range(8, limit + 1, 8) if rows % t == 0)


def adamw(w, gs, m, v, *, name):
    L, r, cols = w.shape
    tr = _row_tile(r)
    nt = r // tr

    def body(*refs):
        w_ref, m_ref, v_ref = refs[:3]
        g_refs = refs[3:3 + L]
        g_out, d_out, m_out, v_out = refs[3 + L:]
        layer = pl.program_id(0)
        g = g_refs[0][...]
        for l in range(1, L):
            g = jnp.where(layer == l, g_refs[l][...], g)
        m_new = ADAM_B1 * m_ref[...] + (1.0 - ADAM_B1) * g
        v_new = ADAM_B2 * v_ref[...] + (1.0 - ADAM_B2) * (g * g)
        m_hat = m_new / (1.0 - ADAM_B1 ** ADAM_STEP)
        v_hat = v_new / (1.0 - ADAM_B2 ** ADAM_STEP)
        g_out[...] = g
        m_out[...] = m_new
        v_out[...] = v_new
        d_out[...] = -ADAM_LR * (m_hat / (jnp.sqrt(v_hat) + ADAM_EPS) + ADAM_WD * w_ref[...])

    full = pl.BlockSpec((None, tr, cols), lambda l, i: (l, i, 0))
    g_spec = lambda l0: pl.BlockSpec((tr, cols), lambda l, i: (jnp.where(l == l0, i, jnp.where(l < l0, 0, nt - 1)), 0))
    return pl.pallas_call(
        body, name=name, grid=(L, nt),
        in_specs=[full, full, full] + [g_spec(l0) for l0 in range(L)],
        out_specs=[full] * 4,
        out_shape=[SDS(w.shape, F32)] * 4,
        compiler_params=_params("arbitrary", "arbitrary"),
    )(w, m, v, *gs)


SMALL_ROWS = 16


def kernel(x, a_pre_norm, a_w_in, a_conv_w, a_w_out, a_post_norm, ffn_pre_norm, ffn_w_gate_up, ffn_w_down, ffn_post_norm, kv_norm, w_kv, b_pre_norm, b_w_q, b_sinks, b_w_o, b_post_norm, loss_target, m_a_pre_norm, m_a_w_in, m_a_conv_w, m_a_w_out, m_a_post_norm, m_ffn_pre_norm, m_ffn_w_gate_up, m_ffn_w_down, m_ffn_post_norm, m_kv_norm, m_w_kv, m_b_pre_norm, m_b_w_q, m_b_sinks, m_b_w_o, m_b_post_norm, v_a_pre_norm, v_a_w_in, v_a_conv_w, v_a_w_out, v_a_post_norm, v_ffn_pre_norm, v_ffn_w_gate_up, v_ffn_w_down, v_ffn_post_norm, v_kv_norm, v_w_kv, v_b_pre_norm, v_b_w_q, v_b_sinks, v_b_w_o, v_b_post_norm):
    T, D = x.shape[1], x.shape[2]
    xi, yi, ci = _place()
    p = 2 * xi + yi
    p_arr = jnp.reshape(p, (1,)).astype(jnp.int32)
    c_arr = jnp.reshape(ci, (1,)).astype(jnp.int32)
    pc_arr = jnp.stack([p, ci]).astype(jnp.int32)
    me_arr = jnp.reshape(4 * xi + 2 * yi + ci, (1,)).astype(jnp.int32)
    qd = D // N_CHIPS

    big = {"w_in": (a_w_in, 0), "w_out": (a_w_out, 0), "gu0": (ffn_w_gate_up, 0), "gu1": (ffn_w_gate_up, 1),
           "wd0": (ffn_w_down, 0), "wd1": (ffn_w_down, 1), "w_kv": (w_kv[None], 0), "w_q": (b_w_q, 0),
           "w_o": (b_w_o, 0)}
    names = list(big)
    shape = {k: w.shape[1:] for k, (w, _) in big.items()}
    small_shard = jnp.concatenate([a_pre_norm, a_post_norm, a_conv_w[0], jnp.zeros((3, qd), F32)], axis=0)
    *full, small_full = allgather_weights(
        [cast_quarter(w, layer, p_arr, name="cast_" + k) for k, (w, layer) in big.items()], small_shard)
    full = dict(zip(names, full))
    rows = lambda k: jnp.transpose(small_full[:, k], (1, 0, 2)).reshape(-1, D)
    wts = {"w_in": full["w_in"], "gu0": full["gu0"], "gu1": full["gu1"],
           "w_out": full["w_out"].reshape(D, D), "w_o": full["w_o"].reshape(D, D),
           "wd0": full["wd0"].reshape(-1, D), "wd1": full["wd1"].reshape(-1, D),
           "w_kv": full["w_kv"].reshape(1, D, -1), "w_q": full["w_q"].reshape(1, D, D)}
    vec = {"a_pre": rows(slice(0, 1)), "a_post": rows(slice(1, 2)), "conv_w": rows(slice(2, 5)),
           "ffn_pre0": ffn_pre_norm[0:1], "ffn_pre1": ffn_pre_norm[1:2],
           "ffn_post0": ffn_post_norm[0:1], "ffn_post1": ffn_post_norm[1:2],
           "kv_norm": kv_norm[None], "b_pre": b_pre_norm, "b_post": b_post_norm, "sinks": b_sinks}

    loss, dx, grads, small = local_step(x[0], loss_target[0], wts, vec)

    pad = lambda a: jnp.pad(a, ((0, 0), (0, D - a.shape[1])))
    small_block = jnp.concatenate(
        [small["a_pre"], small["a_post"], small["conv_w"][0:3], small["ffn_pre0"], small["ffn_pre1"],
         small["ffn_post0"], small["ffn_post1"], small["kv_norm"], small["b_pre"], small["b_post"],
         pad(small["sinks"][0:1]), jnp.zeros((SMALL_ROWS - 13, D), F32)], axis=0)
    own = [grads[k].reshape((N_CHIPS, 2, shape[k][0] // 2, shape[k][1])) for k in names]
    got = pair_exchange(own)
    sums = [pair_add(o, g, c_arr, name="pair_add_" + k) for k, o, g in zip(names, own, got)]
    *got, small_blocks = chip_exchange(sums, small_block)
    halves = [chip_reduce(s, g, pc_arr, name="chip_reduce_" + k) for k, s, g in zip(names, sums, got)]
    quarter = dict(zip(names, [q.reshape(shape[k]) for k, q in zip(names, half_exchange(halves))]))
    small_sum = small_reduce(small_blocks, me_arr)

    out = {}
    out["a_w_in"] = adamw(a_w_in, [quarter["w_in"]], m_a_w_in, v_a_w_in, name="adamw_a_w_in")
    out["a_w_out"] = adamw(a_w_out, [quarter["w_out"]], m_a_w_out, v_a_w_out, name="adamw_a_w_out")
    out["ffn_w_gate_up"] = adamw(ffn_w_gate_up, [quarter["gu0"], quarter["gu1"]], m_ffn_w_gate_up, v_ffn_w_gate_up,
                                 name="adamw_ffn_w_gate_up")
    out["ffn_w_down"] = adamw(ffn_w_down, [quarter["wd0"], quarter["wd1"]], m_ffn_w_down, v_ffn_w_down,
                              name="adamw_ffn_w_down")
    out["w_kv"] = [o[0] for o in adamw(w_kv[None], [quarter["w_kv"]], m_w_kv[None], v_w_kv[None], name="adamw_w_kv")]
    out["b_w_q"] = adamw(b_w_q, [quarter["w_q"]], m_b_w_q, v_b_w_q, name="adamw_b_w_q")
    out["b_w_o"] = adamw(b_w_o, [quarter["w_o"]], m_b_w_o, v_b_w_o, name="adamw_b_w_o")

    def pack(a_pre, a_post, conv, ffn_pre, ffn_post, kvn, b_pre, b_post, sinks):
        return jnp.concatenate([pad(a_pre), pad(a_post), pad(conv[0]), ffn_pre, ffn_post, kvn[None], b_pre, b_post,
                                pad(sinks), jnp.zeros((SMALL_ROWS - 13, D), F32)], axis=0)

    g_small = jnp.concatenate([pad(lax.dynamic_slice(small_sum, (0, p * qd), (5, qd))), small_sum[5:]], axis=0)
    w_small = pack(a_pre_norm, a_post_norm, a_conv_w, ffn_pre_norm, ffn_post_norm, kv_norm, b_pre_norm, b_post_norm,
                   b_sinks)
    m_small = pack(m_a_pre_norm, m_a_post_norm, m_a_conv_w, m_ffn_pre_norm, m_ffn_post_norm, m_kv_norm,
                   m_b_pre_norm, m_b_post_norm, m_b_sinks)
    v_small = pack(v_a_pre_norm, v_a_post_norm, v_a_conv_w, v_ffn_pre_norm, v_ffn_post_norm, v_kv_norm,
                   v_b_pre_norm, v_b_post_norm, v_b_sinks)
    packed = adamw(w_small[None], [g_small], m_small[None], v_small[None], name="adamw_small")
    ns = b_sinks.shape[1]
    unpack = lambda a: {"a_pre_norm": a[0:1, :qd], "a_post_norm": a[1:2, :qd], "a_conv_w": a[None, 2:5, :qd],
                        "ffn_pre_norm": a[5:7], "ffn_post_norm": a[7:9], "kv_norm": a[9], "b_pre_norm": a[10:11],
                        "b_post_norm": a[11:12], "b_sinks": a[12:13, :ns]}
    unpacked = [unpack(a[0]) for a in packed]
    for k in unpacked[0]:
        out[k] = [u[k] for u in unpacked]

    order = ["a_pre_norm", "a_w_in", "a_conv_w", "a_w_out", "a_post_norm", "ffn_pre_norm", "ffn_w_gate_up",
             "ffn_w_down", "ffn_post_norm", "kv_norm", "w_kv", "b_pre_norm", "b_w_q", "b_sinks", "b_w_o",
             "b_post_norm"]
    total_loss = lax.psum(loss[0, 0], ("x", "y", "c"))
    return (total_loss, dx[None], *[out[k][0] for k in order], *[out[k][1] for k in order],
            *[out[k][2] for k in order], *[out[k][3] for k in order])
```

```python
import math

import jax
import jax.numpy as jnp
from jax import lax
from jax.experimental import pallas as pl
from jax.experimental.pallas import tpu as pltpu

F32 = jnp.float32
BF16 = jnp.bfloat16
SDS = jax.ShapeDtypeStruct
MESH = pl.DeviceIdType.MESH
DMA = pltpu.SemaphoreType.DMA
HBM_SPEC = pl.BlockSpec(memory_space=pltpu.HBM)

EPS = 1e-6
NEG = -1e30
HEAD_DIM = 64
N_KV_HEADS = 4
BLOCK = 128
ROT_DIM = HEAD_DIM // 4
ROPE_THETA = 500000.0
N_CHIPS = 4

ADAM_LR = 0.001
ADAM_B1 = 0.9
ADAM_B2 = 0.999
ADAM_EPS = 1e-08
ADAM_WD = 0.01
ADAM_STEP = 10

VMEM_LIMIT_BYTES = 52 * 1024 * 1024
ROW_TILE = 512
BF16_ROWS = 16


def _params(*semantics):
    return pltpu.CompilerParams(dimension_semantics=semantics, vmem_limit_bytes=VMEM_LIMIT_BYTES)


def _place():
    return lax.axis_index("x"), lax.axis_index("y"), lax.axis_index("c")


def _other_chips(x, y):
    return [(1 - x, y), (x, 1 - y), (1 - x, 1 - y)]


def _remote(src, dst, send_sem, recv_sem, to):
    return pltpu.make_async_remote_copy(src_ref=src, dst_ref=dst, send_sem=send_sem, recv_sem=recv_sem,
                                        device_id=to, device_id_type=MESH)


class Ride:
    def __init__(self, operands, out_shape, aliases, sems, make):
        self.operands, self.out_shape, self.aliases, self.sems, self.make = operands, out_shape, aliases, sems, make


def _call(body, *, name, grid, in_specs, out_specs, out_shape, args, scratch_shapes=(), semantics=None, ride=None):
    if ride is None:
        return pl.pallas_call(body, name=name, grid=grid, in_specs=in_specs, out_specs=out_specs,
                              out_shape=out_shape, scratch_shapes=list(scratch_shapes),
                              compiler_params=_params(*semantics))(*args)
    n_in, n_out, n_scr = len(in_specs), len(out_specs), len(scratch_shapes)
    r_in, r_out = len(ride.operands), len(ride.out_shape)
    a, b = n_in, n_in + r_in
    c, d = b + n_out, b + n_out + r_out
    e = d + n_scr

    def riding(*refs):
        start, finish = ride.make(refs[a:b], refs[c:d], refs[e:])
        ids = [pl.program_id(k) for k in range(len(grid))]
        first, last = ids[0] == 0, ids[0] == grid[0] - 1
        for k in range(1, len(grid)):
            first, last = first & (ids[k] == 0), last & (ids[k] == grid[k] - 1)
        pl.when(first)(start)
        body(*refs[:a], *refs[b:c], *refs[d:e])
        pl.when(last)(finish)

    outs = pl.pallas_call(
        riding, name=name, grid=grid,
        in_specs=list(in_specs) + [HBM_SPEC] * r_in, out_specs=list(out_specs) + [HBM_SPEC] * r_out,
        out_shape=list(out_shape) + list(ride.out_shape),
        input_output_aliases={n_in + i: n_out + o for i, o in ride.aliases.items()},
        scratch_shapes=list(scratch_shapes) + list(ride.sems),
        compiler_params=_params(*(("arbitrary",) * len(grid))),
    )(*args, *ride.operands)
    return outs[:n_out], outs[n_out:]


def alone(ride, *, name):
    def body(*refs):
        n = len(ride.operands)
        start, finish = ride.make(refs[:n], refs[n:n + len(ride.out_shape)], refs[n + len(ride.out_shape):])
        start()
        finish()

    return pl.pallas_call(
        body, name=name, in_specs=[HBM_SPEC] * len(ride.operands), out_specs=[HBM_SPEC] * len(ride.out_shape),
        out_shape=list(ride.out_shape), input_output_aliases=dict(ride.aliases), scratch_shapes=list(ride.sems),
    )(*ride.operands)


def gather_ride(quarters, small=None):
    n = len(quarters)
    halves = [s.shape[1] // 2 for s in quarters]
    operands, out_shape = list(quarters), [SDS(s.shape, s.dtype) for s in quarters]
    sems = [DMA((n, 3)), DMA((n, 3)), DMA((n, 3)), DMA((n, 3))]
    if small is not None:
        operands.append(small)
        out_shape.append(SDS((N_CHIPS,) + small.shape, small.dtype))
        sems += [DMA((3,)), DMA((3,)), DMA(())]

    def make(ins, outs, sem):
        send1, recv1, send2, recv2 = sem[:4]
        x, y, c = _place()
        p = 2 * x + y
        chips = _other_chips(x, y)
        me, sibling = (x, y, c), (x, y, 1 - c)
        first = []
        for j, (qx, qy) in enumerate(chips):
            if small is not None:
                first.append(_remote(ins[n], outs[n].at[p], sem[4].at[j], sem[5].at[j], (qx, qy, c)))
            for t in range(n):
                mine = outs[t].at[p, pl.ds(c * halves[t], halves[t])]
                first.append(_remote(mine, mine, send1.at[t, j], recv1.at[t, j], (qx, qy, c)))
        local = [] if small is None else [pltpu.make_async_copy(ins[n], outs[n].at[p], sem[6])]

        def start():
            for cp in local + first:
                cp.start()

        def finish():
            passed = []
            for j, (qx, qy) in enumerate(chips):
                q = 2 * qx + qy
                for t in range(n):
                    landed = outs[t].at[q, pl.ds(c * halves[t], halves[t])]
                    _remote(landed, landed, send1.at[t, j], recv1.at[t, j], me).wait_recv()
                    cp = _remote(landed, landed, send2.at[t, j], recv2.at[t, j], sibling)
                    cp.start()
                    passed.append(cp)
            for j, (qx, qy) in enumerate(chips):
                q = 2 * qx + qy
                if small is not None:
                    _remote(outs[n].at[q], outs[n].at[q], sem[4].at[j], sem[5].at[j], me).wait_recv()
                for t in range(n):
                    theirs = outs[t].at[q, pl.ds((1 - c) * halves[t], halves[t])]
                    _remote(theirs, theirs, send2.at[t, j], recv2.at[t, j], me).wait_recv()
            for cp in first + passed:
                cp.wait_send()
            for cp in local:
                cp.wait()

        return start, finish

    return Ride(operands, out_shape, {t: t for t in range(n)}, sems, make)


def chip_ride(sums, small=None):
    n = len(sums)
    operands, out_shape = list(sums), [SDS((3,) + s.shape[1:], s.dtype) for s in sums]
    sems = [DMA((n, 3)), DMA((n, 3))] if n else []
    if small is not None:
        operands.append(small)
        out_shape.append(SDS((8,) + small.shape, small.dtype))
        sems += [DMA((7,)), DMA((7,)), DMA(())]

    def make(ins, outs, sem):
        x, y, c = _place()
        cps = []
        for j, (qx, qy) in enumerate(_other_chips(x, y)):
            for t in range(n):
                cps.append(_remote(ins[t].at[2 * qx + qy], outs[t].at[j], sem[0].at[t, j], sem[1].at[t, j], (qx, qy, c)))
        local = []
        if small is not None:
            ssend, srecv, lsem = sem[-3:]
            local.append(pltpu.make_async_copy(ins[n], outs[n].at[0], lsem))
            for k in range(1, 8):
                peer = (x ^ (k >> 2 & 1), y ^ (k >> 1 & 1), c ^ (k & 1))
                cps.append(_remote(ins[n], outs[n].at[k], ssend.at[k - 1], srecv.at[k - 1], peer))

        def start():
            for cp in local + cps:
                cp.start()

        def finish():
            for cp in cps + local:
                cp.wait()

        return start, finish

    return Ride(operands, out_shape, {}, sems, make)


def pair_exchange(grads, *, name):
    n = len(grads)

    def body(*refs):
        ins, outs = refs[:n], refs[n:2 * n]
        send, recv = refs[2 * n:]
        x, y, c = _place()
        cps = [_remote(ins[t].at[:, 1 - c], outs[t], send.at[t], recv.at[t], (x, y, 1 - c)) for t in range(n)]
        for cp in cps:
            cp.start()
        for cp in cps:
            cp.wait()

    return pl.pallas_call(
        body, name=name,
        in_specs=[HBM_SPEC] * n, out_specs=[HBM_SPEC] * n,
        out_shape=[SDS((g.shape[0],) + g.shape[2:], g.dtype) for g in grads],
        scratch_shapes=[DMA((n,)), DMA((n,))],
    )(*grads)


def half_exchange(quarters):
    n = len(quarters)

    def body(*refs):
        outs = refs[n:2 * n]
        send, recv = refs[2 * n:]
        x, y, c = _place()
        sends = [_remote(outs[t].at[c], outs[t].at[c], send.at[t], recv.at[t], (x, y, 1 - c)) for t in range(n)]
        for cp in sends:
            cp.start()
        for t in range(n):
            theirs = outs[t].at[1 - c]
            _remote(theirs, theirs, send.at[t], recv.at[t], (x, y, c)).wait_recv()
        for cp in sends:
            cp.wait_send()

    return pl.pallas_call(
        body, name="half_exchange",
        in_specs=[HBM_SPEC] * n, out_specs=[HBM_SPEC] * n,
        out_shape=[SDS(q.shape, q.dtype) for q in quarters],
        input_output_aliases={t: t for t in range(n)},
        scratch_shapes=[DMA((n,)), DMA((n,))],
    )(*quarters)


def _row_tile(rows, limit=256):
    return max(t for t in range(8, limit + 1, 8) if rows % t == 0)


def cast_quarter(w, layer, p_arr, *, name):
    _, r, cols = w.shape
    tr = _row_tile(r, 512)

    def body(p_ref, w_ref, o_ref):
        o_ref[...] = w_ref[...].astype(BF16)

    return pl.pallas_call(
        body, name=name,
        grid_spec=pltpu.PrefetchScalarGridSpec(
            num_scalar_prefetch=1, grid=(r // tr,),
            in_specs=[pl.BlockSpec((None, tr, cols), lambda i, p_ref: (layer, i, 0))],
            out_specs=pl.BlockSpec((None, tr, cols), lambda i, p_ref: (p_ref[0], i, 0))),
        out_shape=SDS((N_CHIPS, r, cols), BF16),
        compiler_params=_params("parallel"),
    )(p_arr, w)


def pair_add(own, got, c_arr, *, name):
    P, _, h, cols = own.shape

    def body(c_ref, a_ref, b_ref, o_ref):
        o_ref[...] = (a_ref[...].astype(F32) + b_ref[...].astype(F32)).astype(BF16)

    return pl.pallas_call(
        body, name=name,
        grid_spec=pltpu.PrefetchScalarGridSpec(
            num_scalar_prefetch=1, grid=(P,),
            in_specs=[pl.BlockSpec((None, None, h, cols), lambda q, c_ref: (q, c_ref[0], 0, 0)),
                      pl.BlockSpec((None, h, cols), lambda q, c_ref: (q, 0, 0))],
            out_specs=pl.BlockSpec((None, h, cols), lambda q, c_ref: (q, 0, 0))),
        out_shape=SDS((P, h, cols), BF16),
        compiler_params=_params("parallel"),
    )(c_arr, own, got)


def chip_reduce(sums, got, pc_arr, *, name):
    _, h, cols = sums.shape
    th = h // 2

    def body(pc_ref, a_ref, b_ref, o_ref):
        o_ref[...] = ((a_ref[...].astype(F32) + b_ref[0].astype(F32)) + b_ref[1].astype(F32)) + b_ref[2].astype(F32)

    return pl.pallas_call(
        body, name=name,
        grid_spec=pltpu.PrefetchScalarGridSpec(
            num_scalar_prefetch=1, grid=(h // th,),
            in_specs=[pl.BlockSpec((None, th, cols), lambda i, pc_ref: (pc_ref[0], i, 0)),
                      pl.BlockSpec((3, th, cols), lambda i, pc_ref: (0, i, 0))],
            out_specs=pl.BlockSpec((None, th, cols), lambda i, pc_ref: (pc_ref[1], i, 0))),
        out_shape=SDS((2, h, cols), F32),
        compiler_params=_params("parallel"),
    )(pc_arr, sums, got)


def small_reduce(blocks, me_arr):
    _, rows, D = blocks.shape

    def body(me_ref, b_ref, o_ref):
        me = me_ref[0]
        total = b_ref[me]
        for d in range(1, 8):
            total = total + b_ref[d ^ me]
        o_ref[...] = total

    return pl.pallas_call(
        body, name="small_reduce",
        grid_spec=pltpu.PrefetchScalarGridSpec(
            num_scalar_prefetch=1, grid=(1,),
            in_specs=[pl.BlockSpec((8, rows, D), lambda i, me_ref: (0, 0, 0))],
            out_specs=pl.BlockSpec((rows, D), lambda i, me_ref: (0, 0))),
        out_shape=SDS((rows, D), F32),
        compiler_params=_params("arbitrary"),
    )(me_arr, blocks)


def adamw(w, gs, m, v, *, name):
    L, r, cols = w.shape
    tr = _row_tile(r)
    nt = r // tr

    def body(*refs):
        w_ref, m_ref, v_ref = refs[:3]
        g_refs = refs[3:3 + L]
        g_out, d_out, m_out, v_out = refs[3 + L:]
        layer = pl.program_id(0)
        g = g_refs[0][...]
        for l in range(1, L):
            g = jnp.where(layer == l, g_refs[l][...], g)
        m_new = ADAM_B1 * m_ref[...] + (1.0 - ADAM_B1) * g
        v_new = ADAM_B2 * v_ref[...] + (1.0 - ADAM_B2) * (g * g)
        m_hat = m_new / (1.0 - ADAM_B1 ** ADAM_STEP)
        v_hat = v_new / (1.0 - ADAM_B2 ** ADAM_STEP)
        g_out[...] = g
        m_out[...] = m_new
        v_out[...] = v_new
        d_out[...] = -ADAM_LR * (m_hat / (jnp.sqrt(v_hat) + ADAM_EPS) + ADAM_WD * w_ref[...])

    full = pl.BlockSpec((None, tr, cols), lambda l, i: (l, i, 0))
    g_spec = lambda l0: pl.BlockSpec((tr, cols), lambda l, i: (jnp.where(l == l0, i, jnp.where(l < l0, 0, nt - 1)), 0))
    return pl.pallas_call(
        body, name=name, grid=(L, nt),
        in_specs=[full, full, full] + [g_spec(l0) for l0 in range(L)],
        out_specs=[full] * 4,
        out_shape=[SDS(w.shape, F32)] * 4,
        compiler_params=_params("arbitrary", "arbitrary"),
    )(w, m, v, *gs)


def _rms_r(xf):
    return lax.rsqrt(jnp.mean(xf * xf, axis=-1, keepdims=True) + EPS)


def _rmsnorm_bwd(xf, g, dy):
    r = _rms_r(xf)
    xh = xf * r
    gd = g * dy
    return r * (gd - xh * jnp.mean(xh * gd, axis=-1, keepdims=True)), xh


def _dot(a, b):
    return jnp.dot(a, b, preferred_element_type=F32)


def _dot_nt(a, b):
    return lax.dot_general(a, b, (((1,), (1,)), ((), ())), preferred_element_type=F32)


def _dot_tn(a, b):
    return lax.dot_general(a, b, (((0,), (0,)), ((), ())), preferred_element_type=F32)


def _accumulate(ref, first, value):
    @pl.when(first)
    def _():
        ref[...] = value

    @pl.when(jnp.logical_not(first))
    def _():
        ref[...] += value


def norm_matmul(x, g, w, *, split, name, ride=None, tm=ROW_TILE):
    T, D = x.shape
    P, _, ws = w.shape
    per = P // split

    def body(x_ref, g_ref, w_ref, o_ref, xn_ref):
        @pl.when(pl.program_id(1) == 0)
        def _():
            xf = x_ref[...]
            xn_ref[...] = (xf * _rms_r(xf) * g_ref[...]).astype(BF16)

        o_ref[...] = _dot(xn_ref[...], w_ref[...]).astype(BF16)

    return _call(
        body, name=name, grid=(T // tm, P),
        in_specs=[pl.BlockSpec((tm, D), lambda i, j: (i, 0)),
                  pl.BlockSpec((1, D), lambda i, j: (0, 0)),
                  pl.BlockSpec((None, D, ws), lambda i, j: (j, 0, 0))],
        out_specs=[pl.BlockSpec((None, tm, ws), lambda i, j: (j // per, i, j % per)),
                   pl.BlockSpec((tm, D), lambda i, j: (i, 0))],
        out_shape=[SDS((split, T, per * ws), BF16), SDS((T, D), BF16)],
        semantics=("parallel", "arbitrary"), args=(x, g, w), ride=ride)


def _shift_down(prev, cur, by):
    big = jnp.concatenate([prev, cur], axis=0)
    return pltpu.roll(big, by, 0)[prev.shape[0]:]


def _shift_up(cur, nxt, by):
    big = jnp.concatenate([cur, nxt], axis=0)
    return pltpu.roll(big, big.shape[0] - by, 0)[:cur.shape[0]]


def conv_mix_out(bcx, conv_w, w_out, g_post, res, *, name, ride=None, tm=ROW_TILE):
    T, D = res.shape
    hb = tm // BF16_ROWS

    def body(b_ref, c_ref, u_ref, cp_ref, up_ref, cw_ref, w_ref, g_ref, r_ref, h_ref, z_ref, y_ref):
        i = pl.program_id(0)
        cu = c_ref[...].astype(F32) * u_ref[...].astype(F32)
        cup = cp_ref[...].astype(F32) * up_ref[...].astype(F32)
        cup = jnp.where(i == 0, 0.0, cup)
        cv = (cw_ref[0:1, :] * _shift_down(cup, cu, 2) + cw_ref[1:2, :] * _shift_down(cup, cu, 1)
              + cw_ref[2:3, :] * cu)
        y = (b_ref[...].astype(F32) * cv).astype(BF16)
        y_ref[...] = y
        z = _dot(y, w_ref[...])
        z_ref[...] = z.astype(BF16)
        h_ref[...] = r_ref[...] + z * _rms_r(z) * g_ref[...]

    tile = lambda col: pl.BlockSpec((tm, D), lambda i: (i, col))
    halo = lambda col: pl.BlockSpec((BF16_ROWS, D), lambda i: (jnp.maximum(i * hb - 1, 0), col))
    row = pl.BlockSpec((tm, D), lambda i: (i, 0))
    return _call(
        body, name=name, grid=(T // tm,),
        in_specs=[tile(0), tile(1), tile(2), halo(1), halo(2),
                  pl.BlockSpec((3, D), lambda i: (0, 0)),
                  pl.BlockSpec((D, D), lambda i: (0, 0)),
                  pl.BlockSpec((1, D), lambda i: (0, 0)), row],
        out_specs=[row, row, row],
        out_shape=[SDS((T, D), F32), SDS((T, D), BF16), SDS((T, D), BF16)],
        semantics=("parallel",), args=(bcx, bcx, bcx, bcx, bcx, conv_w, w_out, g_post, res), ride=ride)


def plain_mix_out(a, w, g_post, res, *, name, tm=ROW_TILE):
    T, D = res.shape
    K = a.shape[1]

    def body(a_ref, w_ref, g_ref, r_ref, h_ref, z_ref):
        z = _dot(a_ref[...], w_ref[...])
        z_ref[...] = z.astype(BF16)
        h_ref[...] = r_ref[...] + z * _rms_r(z) * g_ref[...]

    row = pl.BlockSpec((tm, D), lambda i: (i, 0))
    return _call(
        body, name=name, grid=(T // tm,),
        in_specs=[pl.BlockSpec((tm, K), lambda i: (i, 0)),
                  pl.BlockSpec((K, D), lambda i: (0, 0)),
                  pl.BlockSpec((1, D), lambda i: (0, 0)), row],
        out_specs=[row, row],
        out_shape=[SDS((T, D), F32), SDS((T, D), BF16)],
        semantics=("parallel",), args=(a, w, g_post, res))


def swiglu_mix_out(gu, w_down, g_post, res, *, tk, name, target=None, ride=None, tm=ROW_TILE):
    T, D = res.shape
    F = gu.shape[2]
    nk = F // tk
    with_loss = target is not None

    def body(*refs):
        if with_loss:
            g_ref, u_ref, w_ref, gp_ref, r_ref, t_ref, h_ref, z_ref, a_ref, loss_ref, acc = refs
        else:
            g_ref, u_ref, w_ref, gp_ref, r_ref, h_ref, z_ref, a_ref, acc = refs
        i, k = pl.program_id(0), pl.program_id(1)
        g = g_ref[...].astype(F32)
        a = (g * jax.nn.sigmoid(g) * u_ref[...].astype(F32)).astype(BF16)
        a_ref[...] = a
        _accumulate(acc, k == 0, _dot(a, w_ref[...]))

        @pl.when(k == nk - 1)
        def _():
            z = acc[...]
            z_ref[...] = z.astype(BF16)
            h = r_ref[...] + z * _rms_r(z) * gp_ref[...]
            if with_loss:
                diff = h - t_ref[...]
                h_ref[...] = diff * (1.0 / D)
                part = jnp.full(loss_ref.shape, 0.5 / D, F32) * jnp.sum(diff * diff)
                _accumulate(loss_ref, i == 0, part)
            else:
                h_ref[...] = h

    row = pl.BlockSpec((tm, D), lambda i, k: (i, 0))
    in_specs = [pl.BlockSpec((None, tm, tk), lambda i, k: (0, i, k)),
                pl.BlockSpec((None, tm, tk), lambda i, k: (1, i, k)),
                pl.BlockSpec((tk, D), lambda i, k: (k, 0)),
                pl.BlockSpec((1, D), lambda i, k: (0, 0)), row]
    out_specs = [row, row, pl.BlockSpec((tm, tk), lambda i, k: (i, k))]
    out_shape = [SDS((T, D), F32), SDS((T, D), BF16), SDS((T, F), BF16)]
    args = [gu, gu, w_down, g_post, res]
    if with_loss:
        in_specs.append(row)
        args.append(target)
        out_specs.append(pl.BlockSpec((8, 128), lambda i, k: (0, 0)))
        out_shape.append(SDS((8, 128), F32))
    return _call(
        body, name=name, grid=(T // tm, nk), in_specs=in_specs, out_specs=out_specs, out_shape=out_shape,
        scratch_shapes=[pltpu.VMEM((tm, D), F32)], semantics=("arbitrary", "arbitrary"), args=args, ride=ride)


def rope_tables(T):
    half = ROT_DIM // 2
    inv_freq = ROPE_THETA ** (-jnp.arange(0, ROT_DIM, 2, dtype=F32) / ROT_DIM)
    ang = (jnp.arange(T, dtype=F32)[:, None] * inv_freq[None, :]).T
    cos, sin = jnp.cos(ang), jnp.sin(ang)
    rest = HEAD_DIM - ROT_DIM
    one, zero = jnp.ones((rest, T), F32), jnp.zeros((rest, T), F32)
    zh = jnp.zeros((half, T), F32)
    fac = jnp.concatenate([cos, cos, one], axis=0)
    up = jnp.concatenate([-sin, zh, zero], axis=0)
    down = jnp.concatenate([zh, sin, zero], axis=0)
    return jnp.stack([fac, up, down])


def _rope(t, tab):
    half = ROT_DIM // 2
    return t * tab[0] + pltpu.roll(t, HEAD_DIM - half, 0) * tab[1] + pltpu.roll(t, half, 0) * tab[2]


def _rope_t(d, tab):
    half = ROT_DIM // 2
    return d * tab[0] + pltpu.roll(d * tab[1], half, 0) + pltpu.roll(d * tab[2], HEAD_DIM - half, 0)


def _head(t, h):
    return t[h * HEAD_DIM:(h + 1) * HEAD_DIM]


def _band(n, group):
    kj = lax.broadcasted_iota(jnp.int32, (2 * BLOCK, BLOCK), 0)
    qi = lax.broadcasted_iota(jnp.int32, (2 * BLOCK, BLOCK), 1)
    mask = (kj > qi) & (kj <= qi + BLOCK) & ((n > 0) | (kj >= BLOCK))
    return jnp.tile(mask, (1, group))


def _attn_specs(D, kvd):
    prev = lambda n: jnp.maximum(n - 1, 0)
    return [pl.BlockSpec((BLOCK, D), lambda n: (n, 0)),
            pl.BlockSpec((BLOCK, kvd), lambda n: (prev(n), 0)),
            pl.BlockSpec((BLOCK, kvd), lambda n: (n, 0)),
            pl.BlockSpec((BLOCK, kvd), lambda n: (prev(n), 1)),
            pl.BlockSpec((BLOCK, kvd), lambda n: (n, 1)),
            pl.BlockSpec((3, HEAD_DIM, BLOCK), lambda n: (0, 0, prev(n))),
            pl.BlockSpec((3, HEAD_DIM, BLOCK), lambda n: (0, 0, n)),
            pl.BlockSpec(memory_space=pltpu.SMEM)]


def _attn_operands(q_ref, kp_ref, k_ref, vp_ref, v_ref, tp_ref, t_ref):
    flip = lambda ref: ref[...].astype(F32).T
    tab = t_ref[...]
    kt = jnp.concatenate([flip(kp_ref), flip(k_ref)], axis=1)
    vt = jnp.concatenate([flip(vp_ref), flip(v_ref)], axis=1)
    return flip(q_ref), kt, vt, tab, jnp.concatenate([tp_ref[...], tab], axis=2)


def _group_heads(t, j, group, tab=None):
    heads = [_head(t, j * group + g) for g in range(group)]
    if tab is not None:
        heads = [_rope(h, tab) for h in heads]
    return jnp.concatenate(heads, axis=1).astype(BF16)


def _sink_row(s_ref, j, group):
    which = lax.broadcasted_iota(jnp.int32, (1, group * BLOCK), 1) // BLOCK
    row = jnp.zeros((1, group * BLOCK), F32)
    for g in range(group):
        row = jnp.where(which == g, s_ref[0, j * group + g], row)
    return row


def _softmax_block(k_j, q_j, sink, mask, scale):
    s = jnp.where(mask, _dot_tn(k_j, q_j) * scale, NEG)
    m = jnp.maximum(jnp.max(s, axis=0, keepdims=True), sink)
    e = jnp.exp(s - m)
    es = jnp.exp(sink - m)
    inv = 1.0 / (jnp.sum(e, axis=0, keepdims=True) + es)
    return e * inv, es * inv


def attention_fwd(q, kv, tabs, sinks, *, name):
    T, D = q.shape
    kvd = kv.shape[1] // 2
    group = D // HEAD_DIM // N_KV_HEADS
    scale = 1.0 / math.sqrt(HEAD_DIM)

    def body(q_ref, kp_ref, k_ref, vp_ref, v_ref, tp_ref, t_ref, s_ref, o_ref):
        mask = _band(pl.program_id(0), group)
        qt, kt, vt, tab, tab2 = _attn_operands(q_ref, kp_ref, k_ref, vp_ref, v_ref, tp_ref, t_ref)
        outs = []
        for j in range(N_KV_HEADS):
            k_j = _rope(_head(kt, j), tab2).astype(BF16)
            p, _ = _softmax_block(k_j, _group_heads(qt, j, group, tab), _sink_row(s_ref, j, group), mask, scale)
            o_j = _dot(_head(vt, j).astype(BF16), p.astype(BF16))
            outs += [o_j[:, g * BLOCK:(g + 1) * BLOCK] for g in range(group)]
        o_ref[...] = jnp.concatenate(outs, axis=0).T.astype(BF16)

    return _call(
        body, name=name, grid=(T // BLOCK,),
        in_specs=_attn_specs(D, kvd),
        out_specs=[pl.BlockSpec((BLOCK, D), lambda n: (n, 0))],
        out_shape=[SDS((T, D), BF16)],
        semantics=("parallel",), args=(q, kv, kv, kv, kv, tabs, tabs, sinks))[0]


def attention_bwd(q, kv, tabs, sinks, do, *, name):
    T, D = q.shape
    kvd = kv.shape[1] // 2
    heads = D // HEAD_DIM
    group = heads // N_KV_HEADS
    scale = 1.0 / math.sqrt(HEAD_DIM)

    def body(q_ref, kp_ref, k_ref, vp_ref, v_ref, tp_ref, t_ref, s_ref, do_ref, dq_ref, dc_ref, dp_ref, ds_ref):
        n = pl.program_id(0)
        mask = _band(n, group)
        qt, kt, vt, tab, tab2 = _attn_operands(q_ref, kp_ref, k_ref, vp_ref, v_ref, tp_ref, t_ref)
        dot = do_ref[...].astype(F32).T
        lane = lax.broadcasted_iota(jnp.int32, (8, 128), 1)
        dsink = jnp.zeros((8, 128), F32)
        dqs, dks, dvs = [], [], []
        for j in range(N_KV_HEADS):
            k_j = _rope(_head(kt, j), tab2).astype(BF16)
            v_j = _head(vt, j).astype(BF16)
            q_j = _group_heads(qt, j, group, tab)
            do_j = _group_heads(dot, j, group)
            p, p_sink = _softmax_block(k_j, q_j, _sink_row(s_ref, j, group), mask, scale)
            dp = _dot_tn(v_j, do_j)
            dl = jnp.sum(p * dp, axis=0, keepdims=True)
            dsc = (p * (dp - dl) * scale).astype(BF16)
            dq_j = _dot(k_j, dsc)
            dqs += [_rope_t(dq_j[:, g * BLOCK:(g + 1) * BLOCK], tab) for g in range(group)]
            dks.append(_rope_t(_dot_nt(q_j, dsc), tab2))
            dvs.append(_dot_nt(do_j, p.astype(BF16)))
            weight = p_sink * dl
            for g in range(group):
                dsink = dsink - jnp.where(lane == j * group + g, jnp.sum(weight[:, g * BLOCK:(g + 1) * BLOCK]), 0.0)
        dq_ref[...] = jnp.concatenate(dqs, axis=0).T.astype(BF16)
        dkv = jnp.concatenate(dks + dvs, axis=0)
        dp_ref[...] = dkv[:, :BLOCK].T
        dc_ref[...] = dkv[:, BLOCK:].T
        _accumulate(ds_ref, n == 0, dsink)

    blk = lambda w: pl.BlockSpec((BLOCK, w), lambda n: (n, 0))
    return _call(
        body, name=name, grid=(T // BLOCK,),
        in_specs=_attn_specs(D, kvd) + [blk(D)],
        out_specs=[blk(D), blk(2 * kvd), blk(2 * kvd), pl.BlockSpec((8, 128), lambda n: (0, 0))],
        out_shape=[SDS((T, D), BF16), SDS((T, 2 * kvd), F32), SDS((T, 2 * kvd), F32), SDS((8, 128), F32)],
        semantics=("arbitrary",), args=(q, kv, kv, kv, kv, tabs, tabs, sinks, do))


def combine_dkv(d_cur, d_prev, *, name):
    T, W = d_cur.shape
    tm = ROW_TILE
    nt, per, last = T // tm, tm // BLOCK, T // BLOCK - 1

    def body(c_ref, p_ref, pn_ref, o_ref):
        nxt = jnp.where(pl.program_id(0) == nt - 1, 0.0, pn_ref[...])
        o_ref[...] = (c_ref[...] + jnp.concatenate([p_ref[BLOCK:, :], nxt], axis=0)).astype(BF16)

    return _call(
        body, name=name, grid=(nt,),
        in_specs=[pl.BlockSpec((tm, W), lambda i: (i, 0)), pl.BlockSpec((tm, W), lambda i: (i, 0)),
                  pl.BlockSpec((BLOCK, W), lambda i: (jnp.minimum((i + 1) * per, last), 0))],
        out_specs=[pl.BlockSpec((tm, W), lambda i: (i, 0))],
        out_shape=[SDS((T, W), BF16)],
        semantics=("parallel",), args=(d_cur, d_prev, d_prev))[0]


def normbwd_matmul_nt(z, g, dh, w, *, tn, name, gu=None, ride=None, tm=ROW_TILE):
    T, D = z.shape
    K = w.shape[0]
    swiglu = gu is not None

    def body(*refs):
        if swiglu:
            z_ref, g_ref, dh_ref, w_ref, gg_ref, uu_ref, dz_ref, dg_ref, o_ref = refs
        else:
            z_ref, g_ref, dh_ref, w_ref, dz_ref, dg_ref, o_ref = refs
        i, j = pl.program_id(0), pl.program_id(1)

        @pl.when(j == 0)
        def _():
            dh_ = dh_ref[...]
            dz, zh = _rmsnorm_bwd(z_ref[...].astype(F32), g_ref[...], dh_)
            dz_ref[...] = dz.astype(BF16)
            _accumulate(dg_ref, i == 0, jnp.sum(dh_ * zh, axis=0, keepdims=True))

        d = _dot_nt(dz_ref[...], w_ref[...])
        if swiglu:
            g_ = gg_ref[...].astype(F32)
            sg = jax.nn.sigmoid(g_)
            o_ref[0] = (d * uu_ref[...].astype(F32) * (sg * (1.0 + g_ * (1.0 - sg)))).astype(BF16)
            o_ref[1] = (d * (g_ * sg)).astype(BF16)
        else:
            o_ref[...] = d.astype(BF16)

    row = pl.BlockSpec((tm, D), lambda i, j: (i, 0))
    in_specs = [row, pl.BlockSpec((1, D), lambda i, j: (0, 0)), row, pl.BlockSpec((tn, D), lambda i, j: (j, 0))]
    args = [z, g, dh, w]
    if swiglu:
        in_specs += [pl.BlockSpec((None, tm, tn), lambda i, j: (0, i, j)),
                     pl.BlockSpec((None, tm, tn), lambda i, j: (1, i, j))]
        args += [gu, gu]
        o_spec, o_shape = pl.BlockSpec((2, tm, tn), lambda i, j: (0, i, j)), SDS((2, T, K), BF16)
    else:
        o_spec, o_shape = pl.BlockSpec((tm, tn), lambda i, j: (i, j)), SDS((T, K), BF16)
    return _call(
        body, name=name, grid=(T // tm, K // tn), in_specs=in_specs,
        out_specs=[row, pl.BlockSpec((1, D), lambda i, j: (0, 0)), o_spec],
        out_shape=[SDS((T, D), BF16), SDS((1, D), F32), o_shape],
        semantics=("arbitrary", "arbitrary"), args=args, ride=ride)


def matmul_nt_normbwd(da, w, h_in, g, dh_out, *, name, ride=None, tm=ROW_TILE):
    T, D = h_in.shape
    P, _, ws = w.shape
    per = da.shape[2] // ws

    def body(da_ref, w_ref, h_ref, g_ref, dh_ref, o_ref, dg_ref, acc):
        i, k = pl.program_id(0), pl.program_id(1)
        _accumulate(acc, k == 0, _dot_nt(da_ref[...], w_ref[...]))

        @pl.when(k == P - 1)
        def _():
            dn = acc[...]
            dx, hh = _rmsnorm_bwd(h_ref[...], g_ref[...], dn)
            o_ref[...] = dh_ref[...] + dx
            _accumulate(dg_ref, i == 0, jnp.sum(dn * hh, axis=0, keepdims=True))

    row = pl.BlockSpec((tm, D), lambda i, k: (i, 0))
    vec = pl.BlockSpec((1, D), lambda i, k: (0, 0))
    return _call(
        body, name=name, grid=(T // tm, P),
        in_specs=[pl.BlockSpec((None, tm, ws), lambda i, k: (k // per, i, k % per)),
                  pl.BlockSpec((None, D, ws), lambda i, k: (k, 0, 0)), row, vec, row],
        out_specs=[row, vec],
        out_shape=[SDS((T, D), F32), SDS((1, D), F32)],
        scratch_shapes=[pltpu.VMEM((tm, D), F32)],
        semantics=("arbitrary", "arbitrary"), args=(da, w, h_in, g, dh_out), ride=ride)


def matmul_tn(a, b, *, ta, tb, tt, name, ride=None):
    T, Ka = a.shape
    S, _, Nb = b.shape
    per = Nb // tb
    nk = T // tt

    def body(a_ref, b_ref, o_ref, acc):
        k = pl.program_id(2)
        _accumulate(acc, k == 0, _dot_tn(a_ref[...], b_ref[...]))

        @pl.when(k == nk - 1)
        def _():
            o_ref[...] = acc[...].astype(BF16)

    out = _call(
        body, name=name, grid=(Ka // ta, S * per, nk),
        in_specs=[pl.BlockSpec((tt, ta), lambda i, j, k: (k, i)),
                  pl.BlockSpec((None, tt, tb), lambda i, j, k: (j // per, k, j % per))],
        out_specs=[pl.BlockSpec((None, ta, tb), lambda i, j, k: (j, i, 0))],
        out_shape=[SDS((S * per, Ka, tb), BF16)],
        scratch_shapes=[pltpu.VMEM((ta, tb), F32)],
        semantics=("parallel", "parallel", "arbitrary"), args=(a, b), ride=ride)
    return out[0] if ride is None else (out[0][0], out[1])


def conv_bwd(dy, bcx, conv_w, *, name, tm=ROW_TILE):
    T, D = dy.shape
    nt = T // tm
    hb = tm // BF16_ROWS
    last = T // BF16_ROWS - 1

    def body(dy_ref, dyn_ref, b_ref, bn_ref, c_ref, u_ref, cp_ref, up_ref, cw_ref, o_ref, dw_ref):
        i = pl.program_id(0)
        c, u = c_ref[...].astype(F32), u_ref[...].astype(F32)
        cu = c * u
        cup = jnp.where(i == 0, 0.0, cp_ref[...].astype(F32) * up_ref[...].astype(F32))
        cu1, cu2 = _shift_down(cup, cu, 1), _shift_down(cup, cu, 2)
        w0, w1, w2 = cw_ref[0:1, :], cw_ref[1:2, :], cw_ref[2:3, :]
        dyf = dy_ref[...].astype(F32)
        o_ref[:, 0:D] = (dyf * (w0 * cu2 + w1 * cu1 + w2 * cu)).astype(BF16)
        dcv = dyf * b_ref[...].astype(F32)
        dcvn = jnp.where(i == nt - 1, 0.0, dyn_ref[...].astype(F32) * bn_ref[...].astype(F32))
        dcu = w2 * dcv + w1 * _shift_up(dcv, dcvn, 1) + w0 * _shift_up(dcv, dcvn, 2)
        o_ref[:, D:2 * D] = (dcu * u).astype(BF16)
        o_ref[:, 2 * D:3 * D] = (dcu * c).astype(BF16)
        row = lax.broadcasted_iota(jnp.int32, (8, D), 0)
        dw = jnp.zeros((8, D), F32)
        for tap, t in enumerate((cu2, cu1, cu)):
            dw = jnp.where(row == tap, jnp.sum(dcv * t, axis=0, keepdims=True), dw)
        _accumulate(dw_ref, i == 0, dw)

    tile = lambda col: pl.BlockSpec((tm, D), lambda i: (i, col))
    prev = lambda col: pl.BlockSpec((BF16_ROWS, D), lambda i: (jnp.maximum(i * hb - 1, 0), col))
    nxt = lambda col: pl.BlockSpec((BF16_ROWS, D), lambda i: (jnp.minimum((i + 1) * hb, last), col))
    return _call(
        body, name=name, grid=(nt,),
        in_specs=[tile(0), nxt(0), tile(0), nxt(0), tile(1), tile(2), prev(1), prev(2),
                  pl.BlockSpec((3, D), lambda i: (0, 0))],
        out_specs=[pl.BlockSpec((tm, 3 * D), lambda i: (i, 0)), pl.BlockSpec((8, D), lambda i: (0, 0))],
        out_shape=[SDS((T, 3 * D), BF16), SDS((8, D), F32)],
        semantics=("arbitrary",), args=(dy, dy, bcx, bcx, bcx, bcx, bcx, bcx, conv_w))


class NoTraffic:
    def ride(self, kernel_name):
        return None

    def landed(self, kernel_name, results, wts):
        pass

    def grad(self, key, value):
        pass


def local_step(x, target, wts, vec, traffic):
    T, D = x.shape
    tabs = rope_tables(T)
    small = {}
    tt = 1024 if T % 1024 == 0 else ROW_TILE

    def run(builder, *args, name, **kw):
        ride = traffic.ride(name)
        if ride is None:
            return builder(*args, name=name, **kw)
        out, extra = builder(*args, name=name, ride=ride, **kw)
        traffic.landed(name, extra, wts)
        return out

    bcx, xn1 = run(norm_matmul, x, vec["a_pre"], wts["w_in"], split=1, name="a_in")
    bcx = bcx[0]
    h1, z0, y0 = run(conv_mix_out, bcx, vec["conv_w"], wts["w_out"], vec["a_post"], x, name="a_out")
    ffn_chunk = wts["gu0"].shape[2]
    gu0, xn2 = run(norm_matmul, h1, vec["ffn_pre0"], wts["gu0"], split=2, name="ffn0_in")
    h2, z1, act0 = run(swiglu_mix_out, gu0, wts["wd0"], vec["ffn_post0"], h1, tk=ffn_chunk, name="ffn0_out")
    kvp, xkv = norm_matmul(h2, vec["kv_norm"], wts["w_kv"], split=1, name="kv_in")
    qp, xq = norm_matmul(h2, vec["b_pre"], wts["w_q"], split=1, name="q_in")
    kvp, qp = kvp[0], qp[0]
    attn = attention_fwd(qp, kvp, tabs, vec["sinks"], name="attn_fwd")
    h3, z2 = plain_mix_out(attn, wts["w_o"], vec["b_post"], h2, name="attn_out")
    gu1, xn3 = norm_matmul(h3, vec["ffn_pre1"], wts["gu1"], split=2, name="ffn1_in")
    dy, z3, act1, loss = swiglu_mix_out(gu1, wts["wd1"], vec["ffn_post1"], h3, tk=ffn_chunk, name="ffn1_out",
                                        target=target)

    def ffn_bwd(layer, z, gu, act, xn, h_in, dh):
        tag = "ffn%d" % layer
        dz, small["ffn_post%d" % layer], dgu = run(
            normbwd_matmul_nt, z, vec["ffn_post%d" % layer], dh, wts["wd%d" % layer], tn=ffn_chunk, gu=gu,
            name=tag + "_out_bwd")
        traffic.grad("wd%d" % layer, matmul_tn(act, dz[None], ta=ffn_chunk, tb=D, tt=tt, name=tag + "_dwd"))
        traffic.grad("gu%d" % layer, run(matmul_tn, xn, dgu, ta=D, tb=ffn_chunk, tt=tt, name=tag + "_dwgu"))
        dh_in, small["ffn_pre%d" % layer] = run(
            matmul_nt_normbwd, dgu, wts["gu%d" % layer], h_in, vec["ffn_pre%d" % layer], dh, name=tag + "_in_bwd")
        return dh_in

    dh3 = ffn_bwd(1, z3, gu1, act1, xn3, h3, dy)
    dz2, small["b_post"], dattn = normbwd_matmul_nt(z2, vec["b_post"], dh3, wts["w_o"], tn=D, name="attn_out_bwd")
    traffic.grad("w_o", matmul_tn(attn, dz2[None], ta=D, tb=D, tt=tt, name="attn_dwo"))
    dq, dkv_cur, dkv_prev, small["sinks"] = attention_bwd(qp, kvp, tabs, vec["sinks"], dattn, name="attn_bwd")
    dkv = combine_dkv(dkv_cur, dkv_prev, name="attn_dkv")
    traffic.grad("w_q", matmul_tn(xq, dq[None], ta=D, tb=D, tt=tt, name="attn_dwq"))
    traffic.grad("w_kv", matmul_tn(xkv, dkv[None], ta=D, tb=dkv.shape[1], tt=tt, name="attn_dwkv"))
    dh2, small["b_pre"] = matmul_nt_normbwd(dq[None], wts["w_q"], h2, vec["b_pre"], dh3, name="q_in_bwd")
    dh2, small["kv_norm"] = matmul_nt_normbwd(dkv[None], wts["w_kv"], h2, vec["kv_norm"], dh2, name="kv_in_bwd")
    dh1 = ffn_bwd(0, z1, gu0, act0, xn2, h1, dh2)
    dz0, small["a_post"], dyc = normbwd_matmul_nt(z0, vec["a_post"], dh1, wts["w_out"], tn=D, name="a_out_bwd")
    traffic.grad("w_out", matmul_tn(y0, dz0[None], ta=D, tb=D, tt=tt, name="a_dwout"))
    dbcx, small["conv_w"] = conv_bwd(dyc, bcx, vec["conv_w"], name="a_conv_bwd")
    traffic.grad("w_in", matmul_tn(xn1, dbcx[None], ta=D, tb=wts["w_in"].shape[2], tt=tt, name="a_dwin"))
    dx, small["a_pre"] = run(matmul_nt_normbwd, dbcx[None], wts["w_in"], x, vec["a_pre"], dh1, name="a_in_bwd")
    return loss, dx, small


SMALL_ROWS = 16

GATHER_PLAN = {"a_in": ["gu0"], "a_out": ["wd0"], "ffn0_in": ["w_kv", "w_q", "w_o", "wd1"], "ffn0_out": ["gu1"]}
REDUCE_PLAN = [(["wd1"], "ffn1_dwgu"), (["gu1"], "ffn1_in_bwd"), (["w_o", "w_q", "w_kv"], "ffn0_out_bwd"),
               (["wd0"], "ffn0_dwgu"), (["gu0"], "ffn0_in_bwd"), (["w_out", "w_in"], "a_in_bwd")]


def _as_operand(key, full, D):
    if key in ("w_out", "w_o"):
        return full.reshape(D, D)
    if key in ("wd0", "wd1"):
        return full.reshape(-1, D)
    if key in ("w_kv", "w_q"):
        return full.reshape(1, D, -1)
    return full


class Traffic:
    def __init__(self, quarters, shape, D, c_arr):
        self.quarters, self.shape, self.D, self.c_arr = quarters, shape, D, c_arr
        self.sums, self.got = {}, {}
        self.ready = {}

    def ride(self, name):
        if name in GATHER_PLAN:
            return gather_ride([self.quarters[k] for k in GATHER_PLAN[name]])
        if name in self.ready:
            return chip_ride([self.sums[k] for k in self.ready[name]])
        return None

    def landed(self, name, results, wts):
        if name in GATHER_PLAN:
            for k, full in zip(GATHER_PLAN[name], results):
                wts[k] = _as_operand(k, full, self.D)
        else:
            for k, got in zip(self.ready[name], results):
                self.got[k] = got

    def grad(self, key, value):
        r, cols = self.shape[key]
        self.sums[key] = value.reshape(N_CHIPS, 2, r // 2, cols)
        for keys, carrier in REDUCE_PLAN:
            if key == keys[-1]:
                own = [self.sums[k] for k in keys]
                got = pair_exchange(own, name="pair_exchange_" + keys[0])
                for k, o, g in zip(keys, own, got):
                    self.sums[k] = pair_add(o, g, self.c_arr, name="pair_add_" + k)
                self.ready[carrier] = keys


def kernel(x, a_pre_norm, a_w_in, a_conv_w, a_w_out, a_post_norm, ffn_pre_norm, ffn_w_gate_up, ffn_w_down, ffn_post_norm, kv_norm, w_kv, b_pre_norm, b_w_q, b_sinks, b_w_o, b_post_norm, loss_target, m_a_pre_norm, m_a_w_in, m_a_conv_w, m_a_w_out, m_a_post_norm, m_ffn_pre_norm, m_ffn_w_gate_up, m_ffn_w_down, m_ffn_post_norm, m_kv_norm, m_w_kv, m_b_pre_norm, m_b_w_q, m_b_sinks, m_b_w_o, m_b_post_norm, v_a_pre_norm, v_a_w_in, v_a_conv_w, v_a_w_out, v_a_post_norm, v_ffn_pre_norm, v_ffn_w_gate_up, v_ffn_w_down, v_ffn_post_norm, v_kv_norm, v_w_kv, v_b_pre_norm, v_b_w_q, v_b_sinks, v_b_w_o, v_b_post_norm):
    T, D = x.shape[1], x.shape[2]
    xi, yi, ci = _place()
    p = 2 * xi + yi
    p_arr = jnp.reshape(p, (1,)).astype(jnp.int32)
    c_arr = jnp.reshape(ci, (1,)).astype(jnp.int32)
    pc_arr = jnp.stack([p, ci]).astype(jnp.int32)
    me_arr = jnp.reshape(4 * xi + 2 * yi + ci, (1,)).astype(jnp.int32)
    qd = D // N_CHIPS

    big = {"w_in": (a_w_in, 0), "w_out": (a_w_out, 0), "gu0": (ffn_w_gate_up, 0), "gu1": (ffn_w_gate_up, 1),
           "wd0": (ffn_w_down, 0), "wd1": (ffn_w_down, 1), "w_kv": (w_kv[None], 0), "w_q": (b_w_q, 0),
           "w_o": (b_w_o, 0)}
    names = list(big)
    shape = {k: w.shape[1:] for k, (w, _) in big.items()}
    quarters = {k: cast_quarter(w, layer, p_arr, name="cast_" + k) for k, (w, layer) in big.items()}
    small_shard = jnp.concatenate([a_pre_norm, a_post_norm, a_conv_w[0], jnp.zeros((3, qd), F32)], axis=0)
    w_in_full, w_out_full, small_full = alone(gather_ride([quarters["w_in"], quarters["w_out"]], small_shard),
                                              name="gather_first")
    wts = {"w_in": w_in_full, "w_out": _as_operand("w_out", w_out_full, D)}
    rows = lambda k: jnp.transpose(small_full[:, k], (1, 0, 2)).reshape(-1, D)
    vec = {"a_pre": rows(slice(0, 1)), "a_post": rows(slice(1, 2)), "conv_w": rows(slice(2, 5)),
           "ffn_pre0": ffn_pre_norm[0:1], "ffn_pre1": ffn_pre_norm[1:2],
           "ffn_post0": ffn_post_norm[0:1], "ffn_post1": ffn_post_norm[1:2],
           "kv_norm": kv_norm[None], "b_pre": b_pre_norm, "b_post": b_post_norm, "sinks": b_sinks}

    traffic = Traffic(quarters, shape, D, c_arr)
    loss, dx, small = local_step(x[0], loss_target[0], wts, vec, traffic)

    pad = lambda a: jnp.pad(a, ((0, 0), (0, D - a.shape[1])))
    small_block = jnp.concatenate(
        [small["a_pre"], small["a_post"], small["conv_w"][0:3], small["ffn_pre0"], small["ffn_pre1"],
         small["ffn_post0"], small["ffn_post1"], small["kv_norm"], small["b_pre"], small["b_post"],
         pad(small["sinks"][0:1]), jnp.zeros((SMALL_ROWS - 13, D), F32)], axis=0)
    (small_blocks,) = alone(chip_ride([], small_block), name="small_exchange")
    halves = [chip_reduce(traffic.sums[k], traffic.got[k], pc_arr, name="chip_reduce_" + k) for k in names]
    quarter = dict(zip(names, [q.reshape(shape[k]) for k, q in zip(names, half_exchange(halves))]))
    small_sum = small_reduce(small_blocks, me_arr)

    out = {}
    out["a_w_in"] = adamw(a_w_in, [quarter["w_in"]], m_a_w_in, v_a_w_in, name="adamw_a_w_in")
    out["a_w_out"] = adamw(a_w_out, [quarter["w_out"]], m_a_w_out, v_a_w_out, name="adamw_a_w_out")
    out["ffn_w_gate_up"] = adamw(ffn_w_gate_up, [quarter["gu0"], quarter["gu1"]], m_ffn_w_gate_up, v_ffn_w_gate_up,
                                 name="adamw_ffn_w_gate_up")
    out["ffn_w_down"] = adamw(ffn_w_down, [quarter["wd0"], quarter["wd1"]], m_ffn_w_down, v_ffn_w_down,
                              name="adamw_ffn_w_down")
    out["w_kv"] = [o[0] for o in adamw(w_kv[None], [quarter["w_kv"]], m_w_kv[None], v_w_kv[None], name="adamw_w_kv")]
    out["b_w_q"] = adamw(b_w_q, [quarter["w_q"]], m_b_w_q, v_b_w_q, name="adamw_b_w_q")
    out["b_w_o"] = adamw(b_w_o, [quarter["w_o"]], m_b_w_o, v_b_w_o, name="adamw_b_w_o")

    def pack(a_pre, a_post, conv, ffn_pre, ffn_post, kvn, b_pre, b_post, sinks):
        return jnp.concatenate([pad(a_pre), pad(a_post), pad(conv[0]), ffn_pre, ffn_post, kvn[None], b_pre, b_post,
                                pad(sinks), jnp.zeros((SMALL_ROWS - 13, D), F32)], axis=0)

    g_small = jnp.concatenate([pad(lax.dynamic_slice(small_sum, (0, p * qd), (5, qd))), small_sum[5:]], axis=0)
    w_small = pack(a_pre_norm, a_post_norm, a_conv_w, ffn_pre_norm, ffn_post_norm, kv_norm, b_pre_norm, b_post_norm,
                   b_sinks)
    m_small = pack(m_a_pre_norm, m_a_post_norm, m_a_conv_w, m_ffn_pre_norm, m_ffn_post_norm, m_kv_norm,
                   m_b_pre_norm, m_b_post_norm, m_b_sinks)
    v_small = pack(v_a_pre_norm, v_a_post_norm, v_a_conv_w, v_ffn_pre_norm, v_ffn_post_norm, v_kv_norm,
                   v_b_pre_norm, v_b_post_norm, v_b_sinks)
    packed = adamw(w_small[None], [g_small], m_small[None], v_small[None], name="adamw_small")
    ns = b_sinks.shape[1]
    unpack = lambda a: {"a_pre_norm": a[0:1, :qd], "a_post_norm": a[1:2, :qd], "a_conv_w": a[None, 2:5, :qd],
                        "ffn_pre_norm": a[5:7], "ffn_post_norm": a[7:9], "kv_norm": a[9], "b_pre_norm": a[10:11],
                        "b_post_norm": a[11:12], "b_sinks": a[12:13, :ns]}
    unpacked = [unpack(a[0]) for a in packed]
    for k in unpacked[0]:
        out[k] = [u[k] for u in unpacked]

    order = ["a_pre_norm", "a_w_in", "a_conv_w", "a_w_out", "a_post_norm", "ffn_pre_norm", "ffn_w_gate_up",
             "ffn_w_down", "ffn_post_norm", "kv_norm", "w_kv", "b_pre_norm", "b_w_q", "b_sinks", "b_w_o",
             "b_post_norm"]
    total_loss = lax.psum(loss[0, 0], ("x", "y", "c"))
    return (total_loss, dx[None], *[out[k][0] for k in order], *[out[k][1] for k in order],
            *[out[k][2] for k in order], *[out[k][3] for k in order])
```

```python
import math

import jax
import jax.numpy as jnp
from jax import lax
from jax.experimental import pallas as pl
from jax.experimental.pallas import tpu as pltpu

F32 = jnp.float32
BF16 = jnp.bfloat16
SDS = jax.ShapeDtypeStruct
MESH = pl.DeviceIdType.MESH
DMA = pltpu.SemaphoreType.DMA
HBM_SPEC = pl.BlockSpec(memory_space=pltpu.HBM)

EPS = 1e-6
NEG = -1e30
HEAD_DIM = 64
N_KV_HEADS = 4
BLOCK = 128
ROT_DIM = HEAD_DIM // 4
ROPE_THETA = 500000.0
N_CHIPS = 4

ADAM_LR = 0.001
ADAM_B1 = 0.9
ADAM_B2 = 0.999
ADAM_EPS = 1e-08
ADAM_WD = 0.01
ADAM_STEP = 10

VMEM_LIMIT_BYTES = 52 * 1024 * 1024
ROW_TILE = 512
BF16_ROWS = 16
MXU_WIDTH = 256

KIND = {"w_in": "col", "gu0": "col", "gu1": "col", "w_out": "row", "wd0": "row", "wd1": "row", "w_kv": "row",
        "w_q": "row", "w_o": "row"}


def _params(*semantics):
    return pltpu.CompilerParams(dimension_semantics=semantics, vmem_limit_bytes=VMEM_LIMIT_BYTES)


def _row_tile(rows, limit, step=8):
    return max(t for t in range(step, limit + 1, step) if rows % t == 0)


def _place():
    return lax.axis_index("x"), lax.axis_index("y"), lax.axis_index("c")


def _other_chips(x, y):
    return [(1 - x, y), (x, 1 - y), (1 - x, 1 - y)]


def _remote(src, dst, send_sem, recv_sem, to):
    return pltpu.make_async_remote_copy(src_ref=src, dst_ref=dst, send_sem=send_sem, recv_sem=recv_sem,
                                        device_id=to, device_id_type=MESH)


def _full_shape(kind, quarter):
    r, ws = quarter
    return (N_CHIPS * r, ws) if kind == "row" else (r, N_CHIPS * ws)


def _half_of_quarter(ref, kind, quarter, q, half):
    r, ws = quarter
    h = r // 2
    if kind == "row":
        return ref.at[pl.ds(pl.multiple_of(q * r + half * h, BF16_ROWS), h)]
    return ref.at[pl.ds(pl.multiple_of(half * h, BF16_ROWS), h), pl.ds(pl.multiple_of(q * ws, 128), ws)]


class Ride:
    def __init__(self, operands, out_shape, aliases, sems, make):
        self.operands, self.out_shape, self.aliases, self.sems, self.make = operands, out_shape, aliases, sems, make


def _call(body, *, name, grid, in_specs, out_specs, out_shape, args, scratch_shapes=(), semantics=None, ride=None):
    if ride is None:
        return pl.pallas_call(body, name=name, grid=grid, in_specs=in_specs, out_specs=out_specs,
                              out_shape=out_shape, scratch_shapes=list(scratch_shapes),
                              compiler_params=_params(*semantics))(*args)
    n_in, n_out, n_scr = len(in_specs), len(out_specs), len(scratch_shapes)
    r_in, r_out = len(ride.operands), len(ride.out_shape)
    a, b = n_in, n_in + r_in
    c, d = b + n_out, b + n_out + r_out
    e = d + n_scr

    def riding(*refs):
        start, finish = ride.make(refs[a:b], refs[c:d], refs[e:])
        ids = [pl.program_id(k) for k in range(len(grid))]
        first, last = ids[0] == 0, ids[0] == grid[0] - 1
        for k in range(1, len(grid)):
            first, last = first & (ids[k] == 0), last & (ids[k] == grid[k] - 1)
        pl.when(first)(start)
        body(*refs[:a], *refs[b:c], *refs[d:e])
        pl.when(last)(finish)

    outs = pl.pallas_call(
        riding, name=name, grid=grid,
        in_specs=list(in_specs) + [HBM_SPEC] * r_in, out_specs=list(out_specs) + [HBM_SPEC] * r_out,
        out_shape=list(out_shape) + list(ride.out_shape),
        input_output_aliases={n_in + i: n_out + o for i, o in ride.aliases.items()},
        scratch_shapes=list(scratch_shapes) + list(ride.sems),
        compiler_params=_params(*(("arbitrary",) * len(grid))),
    )(*args, *ride.operands)
    return outs[:n_out], outs[n_out:]


def alone(ride, *, name):
    def body(*refs):
        n = len(ride.operands)
        start, finish = ride.make(refs[:n], refs[n:n + len(ride.out_shape)], refs[n + len(ride.out_shape):])
        start()
        finish()

    return pl.pallas_call(
        body, name=name, in_specs=[HBM_SPEC] * len(ride.operands), out_specs=[HBM_SPEC] * len(ride.out_shape),
        out_shape=list(ride.out_shape), input_output_aliases=dict(ride.aliases), scratch_shapes=list(ride.sems),
    )(*ride.operands)


def gather_ride(wholes, metas, small=None):
    n = len(wholes)
    operands, out_shape = list(wholes), [SDS(s.shape, s.dtype) for s in wholes]
    sems = [DMA((n, 3)), DMA((n, 3)), DMA((n, 3)), DMA((n, 3))]
    if small is not None:
        operands.append(small)
        out_shape.append(SDS((N_CHIPS,) + small.shape, small.dtype))
        sems += [DMA((3,)), DMA((3,)), DMA(())]

    def make(ins, outs, sem):
        send1, recv1, send2, recv2 = sem[:4]
        x, y, c = _place()
        p = 2 * x + y
        chips = _other_chips(x, y)
        me, sibling = (x, y, c), (x, y, 1 - c)
        part = lambda t, q, half: _half_of_quarter(outs[t], *metas[t], q, half)
        first = []
        for j, (qx, qy) in enumerate(chips):
            if small is not None:
                first.append(_remote(ins[n], outs[n].at[p], sem[4].at[j], sem[5].at[j], (qx, qy, c)))
            for t in range(n):
                first.append(_remote(part(t, p, c), part(t, p, c), send1.at[t, j], recv1.at[t, j], (qx, qy, c)))
        local = [] if small is None else [pltpu.make_async_copy(ins[n], outs[n].at[p], sem[6])]

        def start():
            for cp in local + first:
                cp.start()

        def finish():
            passed = []
            for j, (qx, qy) in enumerate(chips):
                q = 2 * qx + qy
                for t in range(n):
                    landed = part(t, q, c)
                    _remote(landed, landed, send1.at[t, j], recv1.at[t, j], me).wait_recv()
                    cp = _remote(landed, landed, send2.at[t, j], recv2.at[t, j], sibling)
                    cp.start()
                    passed.append(cp)
            for j, (qx, qy) in enumerate(chips):
                q = 2 * qx + qy
                if small is not None:
                    _remote(outs[n].at[q], outs[n].at[q], sem[4].at[j], sem[5].at[j], me).wait_recv()
                for t in range(n):
                    theirs = part(t, q, 1 - c)
                    _remote(theirs, theirs, send2.at[t, j], recv2.at[t, j], me).wait_recv()
            for cp in first + passed:
                cp.wait_send()
            for cp in local:
                cp.wait()

        return start, finish

    return Ride(operands, out_shape, {t: t for t in range(n)}, sems, make)


def chip_ride(sums, metas, small=None):
    n = len(sums)
    operands = list(sums)
    out_shape = [SDS((3, s.shape[1], quarter[1]), s.dtype) for s, (_, quarter) in zip(sums, metas)]
    sems = [DMA((n, 3)), DMA((n, 3))] if n else []
    if small is not None:
        operands.append(small)
        out_shape.append(SDS((8,) + small.shape, small.dtype))
        sems += [DMA((7,)), DMA((7,)), DMA(())]

    def make(ins, outs, sem):
        x, y, c = _place()
        cps = []
        for j, (qx, qy) in enumerate(_other_chips(x, y)):
            q = 2 * qx + qy
            for t in range(n):
                kind, (_, ws) = metas[t]
                src = ins[t].at[q] if kind == "row" else ins[t].at[0, :, pl.ds(pl.multiple_of(q * ws, 128), ws)]
                cps.append(_remote(src, outs[t].at[j], sem[0].at[t, j], sem[1].at[t, j], (qx, qy, c)))
        local = []
        if small is not None:
            ssend, srecv, lsem = sem[-3:]
            local.append(pltpu.make_async_copy(ins[n], outs[n].at[0], lsem))
            for k in range(1, 8):
                peer = (x ^ (k >> 2 & 1), y ^ (k >> 1 & 1), c ^ (k & 1))
                cps.append(_remote(ins[n], outs[n].at[k], ssend.at[k - 1], srecv.at[k - 1], peer))

        def start():
            for cp in local + cps:
                cp.start()

        def finish():
            for cp in cps + local:
                cp.wait()

        return start, finish

    return Ride(operands, out_shape, {}, sems, make)


def pair_exchange(grads, *, name):
    n = len(grads)

    def body(*refs):
        ins, outs = refs[:n], refs[n:2 * n]
        send, recv = refs[2 * n:]
        x, y, c = _place()
        cps = [_remote(ins[t].at[:, 1 - c], outs[t], send.at[t], recv.at[t], (x, y, 1 - c)) for t in range(n)]
        for cp in cps:
            cp.start()
        for cp in cps:
            cp.wait()

    return pl.pallas_call(
        body, name=name,
        in_specs=[HBM_SPEC] * n, out_specs=[HBM_SPEC] * n,
        out_shape=[SDS((g.shape[0],) + g.shape[2:], g.dtype) for g in grads],
        scratch_shapes=[DMA((n,)), DMA((n,))],
    )(*grads)


def half_exchange(quarters):
    n = len(quarters)

    def body(*refs):
        outs = refs[n:2 * n]
        send, recv = refs[2 * n:]
        x, y, c = _place()
        sends = [_remote(outs[t].at[c], outs[t].at[c], send.at[t], recv.at[t], (x, y, 1 - c)) for t in range(n)]
        for cp in sends:
            cp.start()
        for t in range(n):
            theirs = outs[t].at[1 - c]
            _remote(theirs, theirs, send.at[t], recv.at[t], (x, y, c)).wait_recv()
        for cp in sends:
            cp.wait_send()

    return pl.pallas_call(
        body, name="half_exchange",
        in_specs=[HBM_SPEC] * n, out_specs=[HBM_SPEC] * n,
        out_shape=[SDS(q.shape, q.dtype) for q in quarters],
        input_output_aliases={t: t for t in range(n)},
        scratch_shapes=[DMA((n,)), DMA((n,))],
    )(*quarters)


def cast_quarter(w, layer, kind, p_arr, *, name):
    _, r, ws = w.shape
    tr = _row_tile(r, 512)
    per = r // tr
    out_map = (lambda i, p_ref: (p_ref[0] * per + i, 0)) if kind == "row" else (lambda i, p_ref: (i, p_ref[0]))

    def body(p_ref, w_ref, o_ref):
        o_ref[...] = w_ref[...].astype(BF16)

    return pl.pallas_call(
        body, name=name,
        grid_spec=pltpu.PrefetchScalarGridSpec(
            num_scalar_prefetch=1, grid=(per,),
            in_specs=[pl.BlockSpec((None, tr, ws), lambda i, p_ref: (layer, i, 0))],
            out_specs=pl.BlockSpec((tr, ws), out_map)),
        out_shape=SDS(_full_shape(kind, (r, ws)), BF16),
        compiler_params=_params("parallel"),
    )(p_arr, w)


def pair_add(own, got, c_arr, *, name):
    A, _, h, W = own.shape
    th = _row_tile(h, max(BF16_ROWS, (3 << 19) // W), BF16_ROWS)

    def body(c_ref, a_ref, b_ref, o_ref):
        o_ref[...] = (a_ref[...].astype(F32) + b_ref[...].astype(F32)).astype(BF16)

    return pl.pallas_call(
        body, name=name,
        grid_spec=pltpu.PrefetchScalarGridSpec(
            num_scalar_prefetch=1, grid=(A, h // th),
            in_specs=[pl.BlockSpec((None, None, th, W), lambda q, i, c_ref: (q, c_ref[0], i, 0)),
                      pl.BlockSpec((None, th, W), lambda q, i, c_ref: (q, i, 0))],
            out_specs=pl.BlockSpec((None, th, W), lambda q, i, c_ref: (q, i, 0))),
        out_shape=SDS((A, h, W), BF16),
        compiler_params=_params("parallel", "parallel"),
    )(c_arr, own, got)


def chip_reduce(sums, got, kind, pc_arr, *, name):
    _, h, ws = got.shape
    th = h // 2
    mine = (lambda i, pc_ref: (pc_ref[0], i, 0)) if kind == "row" else (lambda i, pc_ref: (0, i, pc_ref[0]))

    def body(pc_ref, a_ref, b_ref, o_ref):
        o_ref[...] = ((a_ref[...].astype(F32) + b_ref[0].astype(F32)) + b_ref[1].astype(F32)) + b_ref[2].astype(F32)

    return pl.pallas_call(
        body, name=name,
        grid_spec=pltpu.PrefetchScalarGridSpec(
            num_scalar_prefetch=1, grid=(h // th,),
            in_specs=[pl.BlockSpec((None, th, ws), mine),
                      pl.BlockSpec((3, th, ws), lambda i, pc_ref: (0, i, 0))],
            out_specs=pl.BlockSpec((None, th, ws), lambda i, pc_ref: (pc_ref[1], i, 0))),
        out_shape=SDS((2, h, ws), F32),
        compiler_params=_params("parallel"),
    )(pc_arr, sums, got)


def small_reduce(blocks, me_arr):
    _, rows, D = blocks.shape

    def body(me_ref, b_ref, o_ref):
        me = me_ref[0]
        total = b_ref[me]
        for d in range(1, 8):
            total = total + b_ref[d ^ me]
        o_ref[...] = total

    return pl.pallas_call(
        body, name="small_reduce",
        grid_spec=pltpu.PrefetchScalarGridSpec(
            num_scalar_prefetch=1, grid=(1,),
            in_specs=[pl.BlockSpec((8, rows, D), lambda i, me_ref: (0, 0, 0))],
            out_specs=pl.BlockSpec((rows, D), lambda i, me_ref: (0, 0))),
        out_shape=SDS((rows, D), F32),
        compiler_params=_params("arbitrary"),
    )(me_arr, blocks)


def adamw(w, gs, m, v, *, name):
    L, r, cols = w.shape
    tr = _row_tile(r, 256)
    nt = r // tr

    def body(*refs):
        w_ref, m_ref, v_ref = refs[:3]
        g_refs = refs[3:3 + L]
        g_out, d_out, m_out, v_out = refs[3 + L:]
        layer = pl.program_id(0)
        g = g_refs[0][...]
        for l in range(1, L):
            g = jnp.where(layer == l, g_refs[l][...], g)
        m_new = ADAM_B1 * m_ref[...] + (1.0 - ADAM_B1) * g
        v_new = ADAM_B2 * v_ref[...] + (1.0 - ADAM_B2) * (g * g)
        m_hat = m_new / (1.0 - ADAM_B1 ** ADAM_STEP)
        v_hat = v_new / (1.0 - ADAM_B2 ** ADAM_STEP)
        g_out[...] = g
        m_out[...] = m_new
        v_out[...] = v_new
        d_out[...] = -ADAM_LR * (m_hat / (jnp.sqrt(v_hat) + ADAM_EPS) + ADAM_WD * w_ref[...])

    full = pl.BlockSpec((None, tr, cols), lambda l, i: (l, i, 0))
    g_spec = lambda l0: pl.BlockSpec((tr, cols), lambda l, i: (jnp.where(l == l0, i, jnp.where(l < l0, 0, nt - 1)), 0))
    return pl.pallas_call(
        body, name=name, grid=(L, nt),
        in_specs=[full, full, full] + [g_spec(l0) for l0 in range(L)],
        out_specs=[full] * 4,
        out_shape=[SDS(w.shape, F32)] * 4,
        compiler_params=_params("arbitrary", "arbitrary"),
    )(w, m, v, *gs)


def _rms_r(xf):
    return lax.rsqrt(jnp.mean(xf * xf, axis=-1, keepdims=True) + EPS)


def _rmsnorm_bwd(xf, g, dy):
    r = _rms_r(xf)
    xh = xf * r
    gd = g * dy
    return r * (gd - xh * jnp.mean(xh * gd, axis=-1, keepdims=True)), xh


def _dot(a, b):
    return jnp.dot(a, b, preferred_element_type=F32)


def _dot_nt(a, b):
    return lax.dot_general(a, b, (((1,), (1,)), ((), ())), preferred_element_type=F32)


def _dot_tn(a, b):
    return lax.dot_general(a, b, (((0,), (0,)), ((), ())), preferred_element_type=F32)


def _accumulate(ref, first, value):
    @pl.when(first)
    def _():
        ref[...] = value

    @pl.when(jnp.logical_not(first))
    def _():
        ref[...] += value


def norm_matmul(x, g, w, *, tn, split, name, ride=None, tm=ROW_TILE):
    T, D = x.shape
    N = w.shape[1]
    per = N // split // tn

    def body(x_ref, g_ref, w_ref, o_ref, xn_ref):
        @pl.when(pl.program_id(1) == 0)
        def _():
            xf = x_ref[...]
            xn_ref[...] = (xf * _rms_r(xf) * g_ref[...]).astype(BF16)

        o_ref[...] = _dot(xn_ref[...], w_ref[...]).astype(BF16)

    return _call(
        body, name=name, grid=(T // tm, N // tn),
        in_specs=[pl.BlockSpec((tm, D), lambda i, j: (i, 0)),
                  pl.BlockSpec((1, D), lambda i, j: (0, 0)),
                  pl.BlockSpec((D, tn), lambda i, j: (0, j))],
        out_specs=[pl.BlockSpec((None, tm, tn), lambda i, j: (j // per, i, j % per)),
                   pl.BlockSpec((tm, D), lambda i, j: (i, 0))],
        out_shape=[SDS((split, T, N // split), BF16), SDS((T, D), BF16)],
        semantics=("parallel", "arbitrary"), args=(x, g, w), ride=ride)


def _shift_down(prev, cur, by):
    big = jnp.concatenate([prev, cur], axis=0)
    return pltpu.roll(big, by, 0)[prev.shape[0]:]


def _shift_up(cur, nxt, by):
    big = jnp.concatenate([cur, nxt], axis=0)
    return pltpu.roll(big, big.shape[0] - by, 0)[:cur.shape[0]]


def conv_mix_out(bcx, conv_w, w_out, g_post, res, *, name, ride=None, tm=ROW_TILE):
    T, D = res.shape
    hb = tm // BF16_ROWS

    def body(b_ref, c_ref, u_ref, cp_ref, up_ref, cw_ref, w_ref, g_ref, r_ref, h_ref, z_ref, y_ref):
        i = pl.program_id(0)
        cu = c_ref[...].astype(F32) * u_ref[...].astype(F32)
        cup = cp_ref[...].astype(F32) * up_ref[...].astype(F32)
        cup = jnp.where(i == 0, 0.0, cup)
        cv = (cw_ref[0:1, :] * _shift_down(cup, cu, 2) + cw_ref[1:2, :] * _shift_down(cup, cu, 1)
              + cw_ref[2:3, :] * cu)
        y = (b_ref[...].astype(F32) * cv).astype(BF16)
        y_ref[...] = y
        z = _dot(y, w_ref[...])
        z_ref[...] = z.astype(BF16)
        h_ref[...] = r_ref[...] + z * _rms_r(z) * g_ref[...]

    tile = lambda col: pl.BlockSpec((tm, D), lambda i: (i, col))
    halo = lambda col: pl.BlockSpec((BF16_ROWS, D), lambda i: (jnp.maximum(i * hb - 1, 0), col))
    row = pl.BlockSpec((tm, D), lambda i: (i, 0))
    return _call(
        body, name=name, grid=(T // tm,),
        in_specs=[tile(0), tile(1), tile(2), halo(1), halo(2),
                  pl.BlockSpec((3, D), lambda i: (0, 0)),
                  pl.BlockSpec((D, D), lambda i: (0, 0)),
                  pl.BlockSpec((1, D), lambda i: (0, 0)), row],
        out_specs=[row, row, row],
        out_shape=[SDS((T, D), F32), SDS((T, D), BF16), SDS((T, D), BF16)],
        semantics=("parallel",), args=(bcx, bcx, bcx, bcx, bcx, conv_w, w_out, g_post, res), ride=ride)


def plain_mix_out(a, w, g_post, res, *, name, tm=ROW_TILE):
    T, D = res.shape
    K = a.shape[1]

    def body(a_ref, w_ref, g_ref, r_ref, h_ref, z_ref):
        z = _dot(a_ref[...], w_ref[...])
        z_ref[...] = z.astype(BF16)
        h_ref[...] = r_ref[...] + z * _rms_r(z) * g_ref[...]

    row = pl.BlockSpec((tm, D), lambda i: (i, 0))
    return _call(
        body, name=name, grid=(T // tm,),
        in_specs=[pl.BlockSpec((tm, K), lambda i: (i, 0)),
                  pl.BlockSpec((K, D), lambda i: (0, 0)),
                  pl.BlockSpec((1, D), lambda i: (0, 0)), row],
        out_specs=[row, row],
        out_shape=[SDS((T, D), F32), SDS((T, D), BF16)],
        semantics=("parallel",), args=(a, w, g_post, res))


def swiglu_mix_out(gu, w_down, g_post, res, *, name, target=None, ride=None, tm=ROW_TILE // 2):
    T, D = res.shape
    F = gu.shape[2]
    with_loss = target is not None

    def body(*refs):
        if with_loss:
            g_ref, u_ref, w_ref, gp_ref, r_ref, t_ref, h_ref, z_ref, a_ref, loss_ref = refs
        else:
            g_ref, u_ref, w_ref, gp_ref, r_ref, h_ref, z_ref, a_ref = refs
        g = g_ref[...]
        a = g * jax.nn.sigmoid(g) * u_ref[...]
        a_ref[...] = a
        z = _dot(a, w_ref[...])
        z_ref[...] = z.astype(BF16)
        h = r_ref[...] + z * _rms_r(z) * gp_ref[...]
        if with_loss:
            diff = h - t_ref[...]
            h_ref[...] = diff * (1.0 / D)
            part = jnp.full(loss_ref.shape, 0.5 / D, F32) * jnp.sum(diff * diff)
            _accumulate(loss_ref, pl.program_id(0) == 0, part)
        else:
            h_ref[...] = h

    row = pl.BlockSpec((tm, D), lambda i: (i, 0))
    in_specs = [pl.BlockSpec((None, tm, F), lambda i: (0, i, 0)),
                pl.BlockSpec((None, tm, F), lambda i: (1, i, 0)),
                pl.BlockSpec((F, D), lambda i: (0, 0)),
                pl.BlockSpec((1, D), lambda i: (0, 0)), row]
    out_specs = [row, row, pl.BlockSpec((tm, F), lambda i: (i, 0))]
    out_shape = [SDS((T, D), F32), SDS((T, D), BF16), SDS((T, F), BF16)]
    args = [gu, gu, w_down, g_post, res]
    if with_loss:
        in_specs.append(row)
        args.append(target)
        out_specs.append(pl.BlockSpec((8, 128), lambda i: (0, 0)))
        out_shape.append(SDS((8, 128), F32))
    return _call(
        body, name=name, grid=(T // tm,), in_specs=in_specs, out_specs=out_specs, out_shape=out_shape,
        semantics=("arbitrary",), args=args, ride=ride)


def rope_tables(T):
    half = ROT_DIM // 2
    inv_freq = ROPE_THETA ** (-jnp.arange(0, ROT_DIM, 2, dtype=F32) / ROT_DIM)
    ang = (jnp.arange(T, dtype=F32)[:, None] * inv_freq[None, :]).T
    cos, sin = jnp.cos(ang), jnp.sin(ang)
    rest = HEAD_DIM - ROT_DIM
    one, zero = jnp.ones((rest, T), F32), jnp.zeros((rest, T), F32)
    zh = jnp.zeros((half, T), F32)
    fac = jnp.concatenate([cos, cos, one], axis=0)
    up = jnp.concatenate([-sin, zh, zero], axis=0)
    down = jnp.concatenate([zh, sin, zero], axis=0)
    return jnp.stack([fac, up, down])


def _rope(t, tab):
    half = ROT_DIM // 2
    return t * tab[0] + pltpu.roll(t, HEAD_DIM - half, 0) * tab[1] + pltpu.roll(t, half, 0) * tab[2]


def _rope_t(d, tab):
    half = ROT_DIM // 2
    return d * tab[0] + pltpu.roll(d * tab[1], half, 0) + pltpu.roll(d * tab[2], HEAD_DIM - half, 0)


def _head(t, h):
    return t[h * HEAD_DIM:(h + 1) * HEAD_DIM]


def _band(n):
    kj = lax.broadcasted_iota(jnp.int32, (2 * BLOCK, BLOCK), 0)
    qi = lax.broadcasted_iota(jnp.int32, (2 * BLOCK, BLOCK), 1)
    return (kj > qi) & (kj <= qi + BLOCK) & ((n > 0) | (kj >= BLOCK))


def _attn_specs(D, kvd):
    prev = lambda n: jnp.maximum(n - 1, 0)
    return [pl.BlockSpec((BLOCK, D), lambda n: (n, 0)),
            pl.BlockSpec((BLOCK, kvd), lambda n: (prev(n), 0)),
            pl.BlockSpec((BLOCK, kvd), lambda n: (n, 0)),
            pl.BlockSpec((BLOCK, kvd), lambda n: (prev(n), 1)),
            pl.BlockSpec((BLOCK, kvd), lambda n: (n, 1)),
            pl.BlockSpec((3, HEAD_DIM, BLOCK), lambda n: (0, 0, prev(n))),
            pl.BlockSpec((3, HEAD_DIM, BLOCK), lambda n: (0, 0, n)),
            pl.BlockSpec(memory_space=pltpu.SMEM)]


def _attn_operands(q_ref, kp_ref, k_ref, vp_ref, v_ref, tp_ref, t_ref):
    flip = lambda ref: ref[...].astype(F32).T
    tab = t_ref[...]
    kt = jnp.concatenate([flip(kp_ref), flip(k_ref)], axis=1)
    vt = jnp.concatenate([flip(vp_ref), flip(v_ref)], axis=1)
    return flip(q_ref), kt, vt, tab, jnp.concatenate([tp_ref[...], tab], axis=2)


def _softmax_block(k_j, q_h, sink, mask, scale):
    s = jnp.where(mask, _dot_tn(k_j, q_h) * scale, NEG)
    m = jnp.maximum(jnp.max(s, axis=0, keepdims=True), sink)
    e = jnp.exp(s - m)
    es = jnp.exp(sink - m)
    inv = 1.0 / (jnp.sum(e, axis=0, keepdims=True) + es)
    return e * inv, es * inv


def attention_fwd(q, kv, tabs, sinks, *, name):
    T, D = q.shape
    kvd = kv.shape[1] // 2
    group = D // HEAD_DIM // N_KV_HEADS
    scale = 1.0 / math.sqrt(HEAD_DIM)

    def body(q_ref, kp_ref, k_ref, vp_ref, v_ref, tp_ref, t_ref, s_ref, o_ref):
        mask = _band(pl.program_id(0))
        qt, kt, vt, tab, tab2 = _attn_operands(q_ref, kp_ref, k_ref, vp_ref, v_ref, tp_ref, t_ref)
        outs = []
        for j in range(N_KV_HEADS):
            k_j = _rope(_head(kt, j), tab2).astype(BF16)
            v_j = _head(vt, j).astype(BF16)
            for g in range(group):
                h = j * group + g
                q_h = _rope(_head(qt, h), tab).astype(BF16)
                p, _ = _softmax_block(k_j, q_h, s_ref[0, h], mask, scale)
                outs.append(_dot(v_j, p.astype(BF16)))
        o_ref[...] = jnp.concatenate(outs, axis=0).T.astype(BF16)

    return _call(
        body, name=name, grid=(T // BLOCK,),
        in_specs=_attn_specs(D, kvd),
        out_specs=[pl.BlockSpec((BLOCK, D), lambda n: (n, 0))],
        out_shape=[SDS((T, D), BF16)],
        semantics=("parallel",), args=(q, kv, kv, kv, kv, tabs, tabs, sinks))[0]


def attention_bwd(q, kv, tabs, sinks, do, *, name):
    T, D = q.shape
    kvd = kv.shape[1] // 2
    heads = D // HEAD_DIM
    group = heads // N_KV_HEADS
    scale = 1.0 / math.sqrt(HEAD_DIM)

    def body(q_ref, kp_ref, k_ref, vp_ref, v_ref, tp_ref, t_ref, s_ref, do_ref, dq_ref, dc_ref, dp_ref, ds_ref):
        n = pl.program_id(0)
        mask = _band(n)
        qt, kt, vt, tab, tab2 = _attn_operands(q_ref, kp_ref, k_ref, vp_ref, v_ref, tp_ref, t_ref)
        dot = do_ref[...].astype(F32).T
        lane = lax.broadcasted_iota(jnp.int32, (8, 128), 1)
        dsink = jnp.zeros((8, 128), F32)
        dqs, dks, dvs = [], [], []
        for j in range(N_KV_HEADS):
            k_j = _rope(_head(kt, j), tab2).astype(BF16)
            v_j = _head(vt, j).astype(BF16)
            dk_j = jnp.zeros((HEAD_DIM, 2 * BLOCK), F32)
            dv_j = jnp.zeros((HEAD_DIM, 2 * BLOCK), F32)
            for g in range(group):
                h = j * group + g
                q_h = _rope(_head(qt, h), tab).astype(BF16)
                do_h = _head(dot, h).astype(BF16)
                p, p_sink = _softmax_block(k_j, q_h, s_ref[0, h], mask, scale)
                dp = _dot_tn(v_j, do_h)
                dl = jnp.sum(p * dp, axis=0, keepdims=True)
                dsc = (p * (dp - dl) * scale).astype(BF16)
                dqs.append(_rope_t(_dot(k_j, dsc), tab))
                dk_j += _dot_nt(q_h, dsc)
                dv_j += _dot_nt(do_h, p.astype(BF16))
                dsink = dsink - jnp.where(lane == h, jnp.sum(p_sink * dl), 0.0)
            dks.append(_rope_t(dk_j, tab2))
            dvs.append(dv_j)
        dq_ref[...] = jnp.concatenate(dqs, axis=0).T.astype(BF16)
        dkv = jnp.concatenate(dks + dvs, axis=0)
        dp_ref[...] = dkv[:, :BLOCK].T
        dc_ref[...] = dkv[:, BLOCK:].T
        _accumulate(ds_ref, n == 0, dsink)

    blk = lambda w: pl.BlockSpec((BLOCK, w), lambda n: (n, 0))
    return _call(
        body, name=name, grid=(T // BLOCK,),
        in_specs=_attn_specs(D, kvd) + [blk(D)],
        out_specs=[blk(D), blk(2 * kvd), blk(2 * kvd), pl.BlockSpec((8, 128), lambda n: (0, 0))],
        out_shape=[SDS((T, D), BF16), SDS((T, 2 * kvd), F32), SDS((T, 2 * kvd), F32), SDS((8, 128), F32)],
        semantics=("arbitrary",), args=(q, kv, kv, kv, kv, tabs, tabs, sinks, do))


def combine_dkv(d_cur, d_prev, *, name):
    T, W = d_cur.shape
    tm = ROW_TILE
    nt, per, last = T // tm, tm // BLOCK, T // BLOCK - 1

    def body(c_ref, p_ref, pn_ref, o_ref):
        nxt = jnp.where(pl.program_id(0) == nt - 1, 0.0, pn_ref[...])
        o_ref[...] = (c_ref[...] + jnp.concatenate([p_ref[BLOCK:, :], nxt], axis=0)).astype(BF16)

    return _call(
        body, name=name, grid=(nt,),
        in_specs=[pl.BlockSpec((tm, W), lambda i: (i, 0)), pl.BlockSpec((tm, W), lambda i: (i, 0)),
                  pl.BlockSpec((BLOCK, W), lambda i: (jnp.minimum((i + 1) * per, last), 0))],
        out_specs=[pl.BlockSpec((tm, W), lambda i: (i, 0))],
        out_shape=[SDS((T, W), BF16)],
        semantics=("parallel",), args=(d_cur, d_prev, d_prev))[0]


def normbwd_matmul_nt(z, g, dh, w, *, tn, name, gu=None, ride=None, tm=ROW_TILE):
    T, D = z.shape
    K = w.shape[0]
    swiglu = gu is not None

    def body(*refs):
        if swiglu:
            z_ref, g_ref, dh_ref, w_ref, gg_ref, uu_ref, dz_ref, dg_ref, o_ref = refs
        else:
            z_ref, g_ref, dh_ref, w_ref, dz_ref, dg_ref, o_ref = refs
        i, j = pl.program_id(0), pl.program_id(1)

        @pl.when(j == 0)
        def _():
            dh_ = dh_ref[...]
            dz, zh = _rmsnorm_bwd(z_ref[...].astype(F32), g_ref[...], dh_)
            dz_ref[...] = dz.astype(BF16)
            _accumulate(dg_ref, i == 0, jnp.sum(dh_ * zh, axis=0, keepdims=True))

        d = _dot_nt(dz_ref[...], w_ref[...])
        if swiglu:
            d, g_ = d.astype(BF16), gg_ref[...]
            sg = jax.nn.sigmoid(g_)
            o_ref[0] = d * uu_ref[...] * (sg * (1.0 + g_ * (1.0 - sg)))
            o_ref[1] = d * (g_ * sg)
        else:
            o_ref[...] = d.astype(BF16)

    row = pl.BlockSpec((tm, D), lambda i, j: (i, 0))
    in_specs = [row, pl.BlockSpec((1, D), lambda i, j: (0, 0)), row, pl.BlockSpec((tn, D), lambda i, j: (j, 0))]
    args = [z, g, dh, w]
    if swiglu:
        in_specs += [pl.BlockSpec((None, tm, tn), lambda i, j: (0, i, j)),
                     pl.BlockSpec((None, tm, tn), lambda i, j: (1, i, j))]
        args += [gu, gu]
        o_spec, o_shape = pl.BlockSpec((2, tm, tn), lambda i, j: (0, i, j)), SDS((2, T, K), BF16)
    else:
        o_spec, o_shape = pl.BlockSpec((tm, tn), lambda i, j: (i, j)), SDS((T, K), BF16)
    return _call(
        body, name=name, grid=(T // tm, K // tn), in_specs=in_specs,
        out_specs=[row, pl.BlockSpec((1, D), lambda i, j: (0, 0)), o_spec],
        out_shape=[SDS((T, D), BF16), SDS((1, D), F32), o_shape],
        semantics=("arbitrary", "arbitrary"), args=args, ride=ride)


def matmul_nt_normbwd(da, w, h_in, g, dh_out, *, name, ride=None, tm=ROW_TILE):
    T, D = h_in.shape
    S, _, K = da.shape

    def body(*refs):
        da_refs, w_refs = refs[:S], refs[S:2 * S]
        h_ref, g_ref, dh_ref, o_ref, dg_ref = refs[2 * S:]
        dn = _dot_nt(da_refs[0][...], w_refs[0][...])
        for s in range(1, S):
            dn = dn + _dot_nt(da_refs[s][...], w_refs[s][...])
        dx, hh = _rmsnorm_bwd(h_ref[...], g_ref[...], dn)
        o_ref[...] = dh_ref[...] + dx
        _accumulate(dg_ref, pl.program_id(0) == 0, jnp.sum(dn * hh, axis=0, keepdims=True))

    row = pl.BlockSpec((tm, D), lambda i: (i, 0))
    vec = pl.BlockSpec((1, D), lambda i: (0, 0))
    part = lambda s: pl.BlockSpec((None, tm, K), lambda i: (s, i, 0))
    cols = lambda s: pl.BlockSpec((D, K), lambda i: (0, s), pipeline_mode=pl.Buffered(1))
    return _call(
        body, name=name, grid=(T // tm,),
        in_specs=[part(s) for s in range(S)] + [cols(s) for s in range(S)] + [row, vec, row],
        out_specs=[row, vec],
        out_shape=[SDS((T, D), F32), SDS((1, D), F32)],
        semantics=("arbitrary",), args=[da] * S + [w] * S + [h_in, g, dh_out], ride=ride)


def matmul_tn(a, b, *, tb, name, ride=None, ta=MXU_WIDTH):
    T, Ka = a.shape
    S, _, Nb = b.shape
    per = Nb // tb

    def body(a_ref, b_ref, o_ref):
        o_ref[...] = _dot_tn(a_ref[...], b_ref[...]).astype(BF16)

    out = _call(
        body, name=name, grid=(S * per, Ka // ta),
        in_specs=[pl.BlockSpec((T, ta), lambda j, i: (0, i)),
                  pl.BlockSpec((None, T, tb), lambda j, i: (j // per, 0, j % per))],
        out_specs=[pl.BlockSpec((ta, tb), lambda j, i: (i, j))],
        out_shape=[SDS((Ka, S * Nb), BF16)],
        semantics=("parallel", "parallel"), args=(a, b), ride=ride)
    return out[0] if ride is None else (out[0][0], out[1])


def conv_bwd(dy, bcx, conv_w, *, name, tm=ROW_TILE):
    T, D = dy.shape
    nt = T // tm
    hb = tm // BF16_ROWS
    last = T // BF16_ROWS - 1

    def body(dy_ref, dyn_ref, b_ref, bn_ref, c_ref, u_ref, cp_ref, up_ref, cw_ref, o_ref, dw_ref):
        i = pl.program_id(0)
        c, u = c_ref[...].astype(F32), u_ref[...].astype(F32)
        cu = c * u
        cup = jnp.where(i == 0, 0.0, cp_ref[...].astype(F32) * up_ref[...].astype(F32))
        cu1, cu2 = _shift_down(cup, cu, 1), _shift_down(cup, cu, 2)
        w0, w1, w2 = cw_ref[0:1, :], cw_ref[1:2, :], cw_ref[2:3, :]
        dyf = dy_ref[...].astype(F32)
        o_ref[:, 0:D] = (dyf * (w0 * cu2 + w1 * cu1 + w2 * cu)).astype(BF16)
        dcv = dyf * b_ref[...].astype(F32)
        dcvn = jnp.where(i == nt - 1, 0.0, dyn_ref[...].astype(F32) * bn_ref[...].astype(F32))
        dcu = w2 * dcv + w1 * _shift_up(dcv, dcvn, 1) + w0 * _shift_up(dcv, dcvn, 2)
        o_ref[:, D:2 * D] = (dcu * u).astype(BF16)
        o_ref[:, 2 * D:3 * D] = (dcu * c).astype(BF16)
        row = lax.broadcasted_iota(jnp.int32, (8, D), 0)
        dw = jnp.zeros((8, D), F32)
        for tap, t in enumerate((cu2, cu1, cu)):
            dw = jnp.where(row == tap, jnp.sum(dcv * t, axis=0, keepdims=True), dw)
        _accumulate(dw_ref, i == 0, dw)

    tile = lambda col: pl.BlockSpec((tm, D), lambda i: (i, col))
    prev = lambda col: pl.BlockSpec((BF16_ROWS, D), lambda i: (jnp.maximum(i * hb - 1, 0), col))
    nxt = lambda col: pl.BlockSpec((BF16_ROWS, D), lambda i: (jnp.minimum((i + 1) * hb, last), col))
    return _call(
        body, name=name, grid=(nt,),
        in_specs=[tile(0), nxt(0), tile(0), nxt(0), tile(1), tile(2), prev(1), prev(2),
                  pl.BlockSpec((3, D), lambda i: (0, 0))],
        out_specs=[pl.BlockSpec((tm, 3 * D), lambda i: (i, 0)), pl.BlockSpec((8, D), lambda i: (0, 0))],
        out_shape=[SDS((T, 3 * D), BF16), SDS((8, D), F32)],
        semantics=("arbitrary",), args=(dy, dy, bcx, bcx, bcx, bcx, bcx, bcx, conv_w))


class NoTraffic:
    def ride(self, kernel_name):
        return None

    def landed(self, kernel_name, results, wts):
        pass

    def grad(self, key, value):
        pass


def local_step(x, target, wts, vec, traffic):
    T, D = x.shape
    tabs = rope_tables(T)
    small = {}

    def run(builder, *args, name, **kw):
        ride = traffic.ride(name)
        if ride is None:
            return builder(*args, name=name, **kw)
        out, extra = builder(*args, name=name, ride=ride, **kw)
        traffic.landed(name, extra, wts)
        return out

    bcx, xn1 = run(norm_matmul, x, vec["a_pre"], wts["w_in"], tn=D, split=1, name="a_in")
    bcx = bcx[0]
    h1, z0, y0 = run(conv_mix_out, bcx, vec["conv_w"], wts["w_out"], vec["a_post"], x, name="a_out")
    F = wts["gu0"].shape[1] // 2
    gu0, xn2 = run(norm_matmul, h1, vec["ffn_pre0"], wts["gu0"], tn=F, split=2, name="ffn0_in")
    h2, z1, act0 = run(swiglu_mix_out, gu0, wts["wd0"], vec["ffn_post0"], h1, name="ffn0_out")
    kvp, xkv = norm_matmul(h2, vec["kv_norm"], wts["w_kv"], tn=wts["w_kv"].shape[1], split=1, name="kv_in")
    qp, xq = norm_matmul(h2, vec["b_pre"], wts["w_q"], tn=D, split=1, name="q_in")
    kvp, qp = kvp[0], qp[0]
    attn = attention_fwd(qp, kvp, tabs, vec["sinks"], name="attn_fwd")
    h3, z2 = plain_mix_out(attn, wts["w_o"], vec["b_post"], h2, name="attn_out")
    gu1, xn3 = norm_matmul(h3, vec["ffn_pre1"], wts["gu1"], tn=F, split=2, name="ffn1_in")
    dy, z3, act1, loss = swiglu_mix_out(gu1, wts["wd1"], vec["ffn_post1"], h3, name="ffn1_out", target=target)

    def ffn_bwd(layer, z, gu, act, xn, h_in, dh):
        tag = "ffn%d" % layer
        dz, small["ffn_post%d" % layer], dgu = run(
            normbwd_matmul_nt, z, vec["ffn_post%d" % layer], dh, wts["wd%d" % layer], tn=F // 2, gu=gu,
            name=tag + "_out_bwd")
        traffic.grad("wd%d" % layer, matmul_tn(act, dz[None], tb=D, name=tag + "_dwd"))
        traffic.grad("gu%d" % layer, run(matmul_tn, xn, dgu, tb=F // 2, name=tag + "_dwgu"))
        dh_in, small["ffn_pre%d" % layer] = run(
            matmul_nt_normbwd, dgu, wts["gu%d" % layer], h_in, vec["ffn_pre%d" % layer], dh, name=tag + "_in_bwd")
        return dh_in

    dh3 = ffn_bwd(1, z3, gu1, act1, xn3, h3, dy)
    dz2, small["b_post"], dattn = normbwd_matmul_nt(z2, vec["b_post"], dh3, wts["w_o"], tn=D, name="attn_out_bwd")
    traffic.grad("w_o", matmul_tn(attn, dz2[None], tb=D, name="attn_dwo"))
    dq, dkv_cur, dkv_prev, small["sinks"] = attention_bwd(qp, kvp, tabs, vec["sinks"], dattn, name="attn_bwd")
    dkv = combine_dkv(dkv_cur, dkv_prev, name="attn_dkv")
    traffic.grad("w_q", matmul_tn(xq, dq[None], tb=D, name="attn_dwq"))
    traffic.grad("w_kv", matmul_tn(xkv, dkv[None], tb=dkv.shape[1], name="attn_dwkv"))
    dh2, small["b_pre"] = matmul_nt_normbwd(dq[None], wts["w_q"], h2, vec["b_pre"], dh3, name="q_in_bwd")
    dh2, small["kv_norm"] = matmul_nt_normbwd(dkv[None], wts["w_kv"], h2, vec["kv_norm"], dh2, name="kv_in_bwd")
    dh1 = ffn_bwd(0, z1, gu0, act0, xn2, h1, dh2)
    dz0, small["a_post"], dyc = normbwd_matmul_nt(z0, vec["a_post"], dh1, wts["w_out"], tn=D, name="a_out_bwd")
    traffic.grad("w_out", matmul_tn(y0, dz0[None], tb=D, name="a_dwout"))
    dbcx, small["conv_w"] = conv_bwd(dyc, bcx, vec["conv_w"], name="a_conv_bwd")
    traffic.grad("w_in", matmul_tn(xn1, dbcx[None], tb=3 * D // 2, name="a_dwin"))
    dx, small["a_pre"] = run(matmul_nt_normbwd, dbcx[None], wts["w_in"], x, vec["a_pre"], dh1, name="a_in_bwd")
    return loss, dx, small


SMALL_ROWS = 16

GATHER_PLAN = {"a_in": ["gu0"], "a_out": ["wd0"], "ffn0_in": ["w_kv", "w_q", "w_o", "wd1"], "ffn0_out": ["gu1"]}
REDUCE_PLAN = [(["wd1"], "ffn1_dwgu"), (["gu1"], "ffn1_in_bwd"), (["w_o", "w_q", "w_kv"], "ffn0_out_bwd"),
               (["wd0"], "ffn0_dwgu"), (["gu0"], "ffn0_in_bwd"), (["w_out", "w_in"], "a_in_bwd")]


class Traffic:
    def __init__(self, wholes, quarter, c_arr):
        self.wholes, self.quarter, self.c_arr = wholes, quarter, c_arr
        self.sums, self.got = {}, {}
        self.ready = {}

    def meta(self, keys):
        return [(KIND[k], self.quarter[k]) for k in keys]

    def ride(self, name):
        if name in GATHER_PLAN:
            keys = GATHER_PLAN[name]
            return gather_ride([self.wholes[k] for k in keys], self.meta(keys))
        if name in self.ready:
            keys = self.ready[name]
            return chip_ride([self.sums[k] for k in keys], self.meta(keys))
        return None

    def landed(self, name, results, wts):
        if name in GATHER_PLAN:
            wts.update(zip(GATHER_PLAN[name], results))
        else:
            self.got.update(zip(self.ready[name], results))

    def grad(self, key, value):
        r, ws = self.quarter[key]
        self.sums[key] = value.reshape((N_CHIPS, 2, r // 2, ws) if KIND[key] == "row" else (1, 2, r // 2, N_CHIPS * ws))
        for keys, carrier in REDUCE_PLAN:
            if key == keys[-1]:
                own = [self.sums[k] for k in keys]
                got = pair_exchange(own, name="pair_exchange_" + keys[0])
                for k, o, g in zip(keys, own, got):
                    self.sums[k] = pair_add(o, g, self.c_arr, name="pair_add_" + k)
                self.ready[carrier] = keys


def kernel(x, a_pre_norm, a_w_in, a_conv_w, a_w_out, a_post_norm, ffn_pre_norm, ffn_w_gate_up, ffn_w_down, ffn_post_norm, kv_norm, w_kv, b_pre_norm, b_w_q, b_sinks, b_w_o, b_post_norm, loss_target, m_a_pre_norm, m_a_w_in, m_a_conv_w, m_a_w_out, m_a_post_norm, m_ffn_pre_norm, m_ffn_w_gate_up, m_ffn_w_down, m_ffn_post_norm, m_kv_norm, m_w_kv, m_b_pre_norm, m_b_w_q, m_b_sinks, m_b_w_o, m_b_post_norm, v_a_pre_norm, v_a_w_in, v_a_conv_w, v_a_w_out, v_a_post_norm, v_ffn_pre_norm, v_ffn_w_gate_up, v_ffn_w_down, v_ffn_post_norm, v_kv_norm, v_w_kv, v_b_pre_norm, v_b_w_q, v_b_sinks, v_b_w_o, v_b_post_norm):
    T, D = x.shape[1], x.shape[2]
    xi, yi, ci = _place()
    p = 2 * xi + yi
    p_arr = jnp.reshape(p, (1,)).astype(jnp.int32)
    c_arr = jnp.reshape(ci, (1,)).astype(jnp.int32)
    pc_arr = jnp.stack([p, ci]).astype(jnp.int32)
    me_arr = jnp.reshape(4 * xi + 2 * yi + ci, (1,)).astype(jnp.int32)
    qd = D // N_CHIPS

    big = {"w_in": (a_w_in, 0), "w_out": (a_w_out, 0), "gu0": (ffn_w_gate_up, 0), "gu1": (ffn_w_gate_up, 1),
           "wd0": (ffn_w_down, 0), "wd1": (ffn_w_down, 1), "w_kv": (w_kv[None], 0), "w_q": (b_w_q, 0),
           "w_o": (b_w_o, 0)}
    names = list(big)
    quarter = {k: w.shape[1:] for k, (w, _) in big.items()}
    wholes = {k: cast_quarter(w, layer, KIND[k], p_arr, name="cast_" + k) for k, (w, layer) in big.items()}
    traffic = Traffic(wholes, quarter, c_arr)
    small_shard = jnp.concatenate([a_pre_norm, a_post_norm, a_conv_w[0], jnp.zeros((3, qd), F32)], axis=0)
    first = ["w_in", "w_out"]
    *landed, small_full = alone(gather_ride([wholes[k] for k in first], traffic.meta(first), small_shard),
                                name="gather_first")
    wts = dict(zip(first, landed))
    rows = lambda k: jnp.transpose(small_full[:, k], (1, 0, 2)).reshape(-1, D)
    vec = {"a_pre": rows(slice(0, 1)), "a_post": rows(slice(1, 2)), "conv_w": rows(slice(2, 5)),
           "ffn_pre0": ffn_pre_norm[0:1], "ffn_pre1": ffn_pre_norm[1:2],
           "ffn_post0": ffn_post_norm[0:1], "ffn_post1": ffn_post_norm[1:2],
           "kv_norm": kv_norm[None], "b_pre": b_pre_norm, "b_post": b_post_norm, "sinks": b_sinks}

    loss, dx, small = local_step(x[0], loss_target[0], wts, vec, traffic)

    pad = lambda a: jnp.pad(a, ((0, 0), (0, D - a.shape[1])))
    small_block = jnp.concatenate(
        [small["a_pre"], small["a_post"], small["conv_w"][0:3], small["ffn_pre0"], small["ffn_pre1"],
         small["ffn_post0"], small["ffn_post1"], small["kv_norm"], small["b_pre"], small["b_post"],
         pad(small["sinks"][0:1]), jnp.zeros((SMALL_ROWS - 13, D), F32)], axis=0)
    (small_blocks,) = alone(chip_ride([], [], small_block), name="small_exchange")
    halves = [chip_reduce(traffic.sums[k], traffic.got[k], KIND[k], pc_arr, name="chip_reduce_" + k) for k in names]
    grad = dict(zip(names, [q.reshape(quarter[k]) for k, q in zip(names, half_exchange(halves))]))
    small_sum = small_reduce(small_blocks, me_arr)

    out = {}
    out["a_w_in"] = adamw(a_w_in, [grad["w_in"]], m_a_w_in, v_a_w_in, name="adamw_a_w_in")
    out["a_w_out"] = adamw(a_w_out, [grad["w_out"]], m_a_w_out, v_a_w_out, name="adamw_a_w_out")
    out["ffn_w_gate_up"] = adamw(ffn_w_gate_up, [grad["gu0"], grad["gu1"]], m_ffn_w_gate_up, v_ffn_w_gate_up,
                                 name="adamw_ffn_w_gate_up")
    out["ffn_w_down"] = adamw(ffn_w_down, [grad["wd0"], grad["wd1"]], m_ffn_w_down, v_ffn_w_down,
                              name="adamw_ffn_w_down")
    out["w_kv"] = [o[0] for o in adamw(w_kv[None], [grad["w_kv"]], m_w_kv[None], v_w_kv[None], name="adamw_w_kv")]
    out["b_w_q"] = adamw(b_w_q, [grad["w_q"]], m_b_w_q, v_b_w_q, name="adamw_b_w_q")
    out["b_w_o"] = adamw(b_w_o, [grad["w_o"]], m_b_w_o, v_b_w_o, name="adamw_b_w_o")

    def pack(a_pre, a_post, conv, ffn_pre, ffn_post, kvn, b_pre, b_post, sinks):
        return jnp.concatenate([pad(a_pre), pad(a_post), pad(conv[0]), ffn_pre, ffn_post, kvn[None], b_pre, b_post,
                                pad(sinks), jnp.zeros((SMALL_ROWS - 13, D), F32)], axis=0)

    g_small = jnp.concatenate([pad(lax.dynamic_slice(small_sum, (0, p * qd), (5, qd))), small_sum[5:]], axis=0)
    w_small = pack(a_pre_norm, a_post_norm, a_conv_w, ffn_pre_norm, ffn_post_norm, kv_norm, b_pre_norm, b_post_norm,
                   b_sinks)
    m_small = pack(m_a_pre_norm, m_a_post_norm, m_a_conv_w, m_ffn_pre_norm, m_ffn_post_norm, m_kv_norm,
                   m_b_pre_norm, m_b_post_norm, m_b_sinks)
    v_small = pack(v_a_pre_norm, v_a_post_norm, v_a_conv_w, v_ffn_pre_norm, v_ffn_post_norm, v_kv_norm,
                   v_b_pre_norm, v_b_post_norm, v_b_sinks)
    packed = adamw(w_small[None], [g_small], m_small[None], v_small[None], name="adamw_small")
    ns = b_sinks.shape[1]
    unpack = lambda a: {"a_pre_norm": a[0:1, :qd], "a_post_norm": a[1:2, :qd], "a_conv_w": a[None, 2:5, :qd],
                        "ffn_pre_norm": a[5:7], "ffn_post_norm": a[7:9], "kv_norm": a[9], "b_pre_norm": a[10:11],
                        "b_post_norm": a[11:12], "b_sinks": a[12:13, :ns]}
    unpacked = [unpack(a[0]) for a in packed]
    for k in unpacked[0]:
        out[k] = [u[k] for u in unpacked]

    order = ["a_pre_norm", "a_w_in", "a_conv_w", "a_w_out", "a_post_norm", "ffn_pre_norm", "ffn_w_gate_up",
             "ffn_w_down", "ffn_post_norm", "kv_norm", "w_kv", "b_pre_norm", "b_w_q", "b_sinks", "b_w_o",
             "b_post_norm"]
    total_loss = lax.psum(loss[0, 0], ("x", "y", "c"))
    return (total_loss, dx[None], *[out[k][0] for k in order], *[out[k][1] for k in order],
            *[out[k][2] for k in order], *[out[k][3] for k in order])
```

```python
import math

import jax
import jax.numpy as jnp
from jax import lax
from jax.experimental import pallas as pl
from jax.experimental.pallas import tpu as pltpu

F32 = jnp.float32
BF16 = jnp.bfloat16
SDS = jax.ShapeDtypeStruct
MESH = pl.DeviceIdType.MESH
DMA = pltpu.SemaphoreType.DMA
HBM_SPEC = pl.BlockSpec(memory_space=pltpu.HBM)

EPS = 1e-6
NEG = -1e30
HEAD_DIM = 64
N_KV_HEADS = 4
BLOCK = 128
ROT_DIM = HEAD_DIM // 4
ROPE_THETA = 500000.0
N_CHIPS = 4

ADAM_LR = 0.001
ADAM_B1 = 0.9
ADAM_B2 = 0.999
ADAM_EPS = 1e-08
ADAM_WD = 0.01
ADAM_STEP = 10

VMEM_LIMIT_BYTES = 52 * 1024 * 1024
ROW_TILE = 512
BF16_ROWS = 16
MXU_WIDTH = 256

KIND = {"w_in": "col", "gu0": "col", "gu1": "col", "w_out": "row", "wd0": "row", "wd1": "row", "w_kv": "row",
        "w_q": "row", "w_o": "row"}


def _params(*semantics):
    return pltpu.CompilerParams(dimension_semantics=semantics, vmem_limit_bytes=VMEM_LIMIT_BYTES)


def _row_tile(rows, limit, step=8):
    return max(t for t in range(step, limit + 1, step) if rows % t == 0)


def _place():
    return lax.axis_index("x"), lax.axis_index("y"), lax.axis_index("c")


def _other_chips(x, y):
    return [(1 - x, y), (x, 1 - y), (1 - x, 1 - y)]


def _remote(src, dst, send_sem, recv_sem, to):
    return pltpu.make_async_remote_copy(src_ref=src, dst_ref=dst, send_sem=send_sem, recv_sem=recv_sem,
                                        device_id=to, device_id_type=MESH)


def _full_shape(kind, quarter):
    r, ws = quarter
    return (N_CHIPS * r, ws) if kind == "row" else (r, N_CHIPS * ws)


def _half_of_quarter(ref, kind, quarter, q, half):
    r, ws = quarter
    h = r // 2
    if kind == "row":
        return ref.at[pl.ds(pl.multiple_of(q * r + half * h, BF16_ROWS), h)]
    return ref.at[pl.ds(pl.multiple_of(half * h, BF16_ROWS), h), pl.ds(pl.multiple_of(q * ws, 128), ws)]


class Ride:
    def __init__(self, operands, out_shape, aliases, sems, make):
        self.operands, self.out_shape, self.aliases, self.sems, self.make = operands, out_shape, aliases, sems, make


def _call(body, *, name, grid, in_specs, out_specs, out_shape, args, scratch_shapes=(), semantics=None, ride=None):
    if ride is None:
        return pl.pallas_call(body, name=name, grid=grid, in_specs=in_specs, out_specs=out_specs,
                              out_shape=out_shape, scratch_shapes=list(scratch_shapes),
                              compiler_params=_params(*semantics))(*args)
    n_in, n_out, n_scr = len(in_specs), len(out_specs), len(scratch_shapes)
    r_in, r_out = len(ride.operands), len(ride.out_shape)
    a, b = n_in, n_in + r_in
    c, d = b + n_out, b + n_out + r_out
    e = d + n_scr

    def riding(*refs):
        start, finish = ride.make(refs[a:b], refs[c:d], refs[e:])
        ids = [pl.program_id(k) for k in range(len(grid))]
        first, last = ids[0] == 0, ids[0] == grid[0] - 1
        for k in range(1, len(grid)):
            first, last = first & (ids[k] == 0), last & (ids[k] == grid[k] - 1)
        pl.when(first)(start)
        body(*refs[:a], *refs[b:c], *refs[d:e])
        pl.when(last)(finish)

    outs = pl.pallas_call(
        riding, name=name, grid=grid,
        in_specs=list(in_specs) + [HBM_SPEC] * r_in, out_specs=list(out_specs) + [HBM_SPEC] * r_out,
        out_shape=list(out_shape) + list(ride.out_shape),
        input_output_aliases={n_in + i: n_out + o for i, o in ride.aliases.items()},
        scratch_shapes=list(scratch_shapes) + list(ride.sems),
        compiler_params=_params(*(("arbitrary",) * len(grid))),
    )(*args, *ride.operands)
    return outs[:n_out], outs[n_out:]


def alone(ride, *, name):
    def body(*refs):
        n = len(ride.operands)
        start, finish = ride.make(refs[:n], refs[n:n + len(ride.out_shape)], refs[n + len(ride.out_shape):])
        start()
        finish()

    return pl.pallas_call(
        body, name=name, in_specs=[HBM_SPEC] * len(ride.operands), out_specs=[HBM_SPEC] * len(ride.out_shape),
        out_shape=list(ride.out_shape), input_output_aliases=dict(ride.aliases), scratch_shapes=list(ride.sems),
    )(*ride.operands)


def gather_ride(wholes, metas, small=None):
    n = len(wholes)
    operands, out_shape = list(wholes), [SDS(s.shape, s.dtype) for s in wholes]
    sems = [DMA((n, 3)), DMA((n, 3)), DMA((n, 3)), DMA((n, 3))]
    if small is not None:
        operands.append(small)
        out_shape.append(SDS((N_CHIPS,) + small.shape, small.dtype))
        sems += [DMA((3,)), DMA((3,)), DMA(())]

    def make(ins, outs, sem):
        send1, recv1, send2, recv2 = sem[:4]
        x, y, c = _place()
        p = 2 * x + y
        chips = _other_chips(x, y)
        me, sibling = (x, y, c), (x, y, 1 - c)
        part = lambda t, q, half: _half_of_quarter(outs[t], *metas[t], q, half)
        first = []
        for j, (qx, qy) in enumerate(chips):
            if small is not None:
                first.append(_remote(ins[n], outs[n].at[p], sem[4].at[j], sem[5].at[j], (qx, qy, c)))
            for t in range(n):
                first.append(_remote(part(t, p, c), part(t, p, c), send1.at[t, j], recv1.at[t, j], (qx, qy, c)))
        local = [] if small is None else [pltpu.make_async_copy(ins[n], outs[n].at[p], sem[6])]

        def start():
            for cp in local + first:
                cp.start()

        def finish():
            passed = []
            for j, (qx, qy) in enumerate(chips):
                q = 2 * qx + qy
                for t in range(n):
                    landed = part(t, q, c)
                    _remote(landed, landed, send1.at[t, j], recv1.at[t, j], me).wait_recv()
                    cp = _remote(landed, landed, send2.at[t, j], recv2.at[t, j], sibling)
                    cp.start()
                    passed.append(cp)
            for j, (qx, qy) in enumerate(chips):
                q = 2 * qx + qy
                if small is not None:
                    _remote(outs[n].at[q], outs[n].at[q], sem[4].at[j], sem[5].at[j], me).wait_recv()
                for t in range(n):
                    theirs = part(t, q, 1 - c)
                    _remote(theirs, theirs, send2.at[t, j], recv2.at[t, j], me).wait_recv()
            for cp in first + passed:
                cp.wait_send()
            for cp in local:
                cp.wait()

        return start, finish

    return Ride(operands, out_shape, {t: t for t in range(n)}, sems, make)


def chip_ride(sums, metas, small=None):
    n = len(sums)
    operands = list(sums)
    out_shape = [SDS((3, s.shape[1], quarter[1]), s.dtype) for s, (_, quarter) in zip(sums, metas)]
    sems = [DMA((n, 3)), DMA((n, 3))] if n else []
    if small is not None:
        operands.append(small)
        out_shape.append(SDS((8,) + small.shape, small.dtype))
        sems += [DMA((7,)), DMA((7,)), DMA(())]

    def make(ins, outs, sem):
        x, y, c = _place()
        cps = []
        for j, (qx, qy) in enumerate(_other_chips(x, y)):
            q = 2 * qx + qy
            for t in range(n):
                kind, (_, ws) = metas[t]
                if kind == "row":
                    src = ins[t].at[q]
                elif kind == "col":
                    src = ins[t].at[0, :, pl.ds(pl.multiple_of(q * ws, 128), ws)]
                else:
                    src = ins[t].at[q // 2, :, pl.ds(pl.multiple_of((q % 2) * ws, 128), ws)]
                cps.append(_remote(src, outs[t].at[j], sem[0].at[t, j], sem[1].at[t, j], (qx, qy, c)))
        local = []
        if small is not None:
            ssend, srecv, lsem = sem[-3:]
            local.append(pltpu.make_async_copy(ins[n], outs[n].at[0], lsem))
            for k in range(1, 8):
                peer = (x ^ (k >> 2 & 1), y ^ (k >> 1 & 1), c ^ (k & 1))
                cps.append(_remote(ins[n], outs[n].at[k], ssend.at[k - 1], srecv.at[k - 1], peer))

        def start():
            for cp in local + cps:
                cp.start()

        def finish():
            for cp in cps + local:
                cp.wait()

        return start, finish

    return Ride(operands, out_shape, {}, sems, make)


def pair_exchange(grads, *, name):
    n = len(grads)

    def body(*refs):
        ins, outs = refs[:n], refs[n:2 * n]
        send, recv = refs[2 * n:]
        x, y, c = _place()
        cps = [_remote(ins[t].at[:, 1 - c], outs[t], send.at[t], recv.at[t], (x, y, 1 - c)) for t in range(n)]
        for cp in cps:
            cp.start()
        for cp in cps:
            cp.wait()

    return pl.pallas_call(
        body, name=name,
        in_specs=[HBM_SPEC] * n, out_specs=[HBM_SPEC] * n,
        out_shape=[SDS((g.shape[0],) + g.shape[2:], g.dtype) for g in grads],
        scratch_shapes=[DMA((n,)), DMA((n,))],
    )(*grads)


def half_exchange(quarters):
    n = len(quarters)

    def body(*refs):
        outs = refs[n:2 * n]
        send, recv = refs[2 * n:]
        x, y, c = _place()
        sends = [_remote(outs[t].at[c], outs[t].at[c], send.at[t], recv.at[t], (x, y, 1 - c)) for t in range(n)]
        for cp in sends:
            cp.start()
        for t in range(n):
            theirs = outs[t].at[1 - c]
            _remote(theirs, theirs, send.at[t], recv.at[t], (x, y, c)).wait_recv()
        for cp in sends:
            cp.wait_send()

    return pl.pallas_call(
        body, name="half_exchange",
        in_specs=[HBM_SPEC] * n, out_specs=[HBM_SPEC] * n,
        out_shape=[SDS(q.shape, q.dtype) for q in quarters],
        input_output_aliases={t: t for t in range(n)},
        scratch_shapes=[DMA((n,)), DMA((n,))],
    )(*quarters)


def cast_quarter(w, layer, kind, p_arr, *, name):
    _, r, ws = w.shape
    tr = _row_tile(r, 512)
    per = r // tr
    out_map = (lambda i, p_ref: (p_ref[0] * per + i, 0)) if kind == "row" else (lambda i, p_ref: (i, p_ref[0]))

    def body(p_ref, w_ref, o_ref):
        o_ref[...] = w_ref[...].astype(BF16)

    return pl.pallas_call(
        body, name=name,
        grid_spec=pltpu.PrefetchScalarGridSpec(
            num_scalar_prefetch=1, grid=(per,),
            in_specs=[pl.BlockSpec((None, tr, ws), lambda i, p_ref: (layer, i, 0))],
            out_specs=pl.BlockSpec((tr, ws), out_map)),
        out_shape=SDS(_full_shape(kind, (r, ws)), BF16),
        compiler_params=_params("parallel"),
    )(p_arr, w)


def pair_add(own, got, c_arr, *, name):
    A, _, h, W = own.shape
    th = _row_tile(h, max(BF16_ROWS, (3 << 19) // W), BF16_ROWS)

    def body(c_ref, a_ref, b_ref, o_ref):
        o_ref[...] = (a_ref[...].astype(F32) + b_ref[...].astype(F32)).astype(BF16)

    return pl.pallas_call(
        body, name=name,
        grid_spec=pltpu.PrefetchScalarGridSpec(
            num_scalar_prefetch=1, grid=(A, h // th),
            in_specs=[pl.BlockSpec((None, None, th, W), lambda q, i, c_ref: (q, c_ref[0], i, 0)),
                      pl.BlockSpec((None, th, W), lambda q, i, c_ref: (q, i, 0))],
            out_specs=pl.BlockSpec((None, th, W), lambda q, i, c_ref: (q, i, 0))),
        out_shape=SDS((A, h, W), BF16),
        compiler_params=_params("parallel", "parallel"),
    )(c_arr, own, got)


def chip_reduce(sums, got, kind, pc_arr, *, name):
    _, h, ws = got.shape
    th = h // 2
    mine = {"row": lambda i, pc_ref: (pc_ref[0], i, 0), "col": lambda i, pc_ref: (0, i, pc_ref[0]),
            "split": lambda i, pc_ref: (pc_ref[0] // 2, i, pc_ref[0] % 2)}[kind]

    def body(pc_ref, a_ref, b_ref, o_ref):
        o_ref[...] = ((a_ref[...].astype(F32) + b_ref[0].astype(F32)) + b_ref[1].astype(F32)) + b_ref[2].astype(F32)

    return pl.pallas_call(
        body, name=name,
        grid_spec=pltpu.PrefetchScalarGridSpec(
            num_scalar_prefetch=1, grid=(h // th,),
            in_specs=[pl.BlockSpec((None, th, ws), mine),
                      pl.BlockSpec((3, th, ws), lambda i, pc_ref: (0, i, 0))],
            out_specs=pl.BlockSpec((None, th, ws), lambda i, pc_ref: (pc_ref[1], i, 0))),
        out_shape=SDS((2, h, ws), F32),
        compiler_params=_params("parallel"),
    )(pc_arr, sums, got)


def small_reduce(blocks, me_arr):
    _, rows, D = blocks.shape

    def body(me_ref, b_ref, o_ref):
        me = me_ref[0]
        total = b_ref[me]
        for d in range(1, 8):
            total = total + b_ref[d ^ me]
        o_ref[...] = total

    return pl.pallas_call(
        body, name="small_reduce",
        grid_spec=pltpu.PrefetchScalarGridSpec(
            num_scalar_prefetch=1, grid=(1,),
            in_specs=[pl.BlockSpec((8, rows, D), lambda i, me_ref: (0, 0, 0))],
            out_specs=pl.BlockSpec((rows, D), lambda i, me_ref: (0, 0))),
        out_shape=SDS((rows, D), F32),
        compiler_params=_params("arbitrary"),
    )(me_arr, blocks)


def adamw(w, gs, m, v, *, name):
    L, r, cols = w.shape
    tr = _row_tile(r, 256)
    nt = r // tr

    def body(*refs):
        w_ref, m_ref, v_ref = refs[:3]
        g_refs = refs[3:3 + L]
        g_out, d_out, m_out, v_out = refs[3 + L:]
        layer = pl.program_id(0)
        g = g_refs[0][...]
        for l in range(1, L):
            g = jnp.where(layer == l, g_refs[l][...], g)
        m_new = ADAM_B1 * m_ref[...] + (1.0 - ADAM_B1) * g
        v_new = ADAM_B2 * v_ref[...] + (1.0 - ADAM_B2) * (g * g)
        m_hat = m_new / (1.0 - ADAM_B1 ** ADAM_STEP)
        v_hat = v_new / (1.0 - ADAM_B2 ** ADAM_STEP)
        g_out[...] = g
        m_out[...] = m_new
        v_out[...] = v_new
        d_out[...] = -ADAM_LR * (m_hat / (jnp.sqrt(v_hat) + ADAM_EPS) + ADAM_WD * w_ref[...])

    full = pl.BlockSpec((None, tr, cols), lambda l, i: (l, i, 0))
    g_spec = lambda l0: pl.BlockSpec((tr, cols), lambda l, i: (jnp.where(l == l0, i, jnp.where(l < l0, 0, nt - 1)), 0))
    return pl.pallas_call(
        body, name=name, grid=(L, nt),
        in_specs=[full, full, full] + [g_spec(l0) for l0 in range(L)],
        out_specs=[full] * 4,
        out_shape=[SDS(w.shape, F32)] * 4,
        compiler_params=_params("arbitrary", "arbitrary"),
    )(w, m, v, *gs)


def _rms_r(xf):
    return lax.rsqrt(jnp.mean(xf * xf, axis=-1, keepdims=True) + EPS)


def _rmsnorm_bwd(xf, g, dy):
    r = _rms_r(xf)
    xh = xf * r
    gd = g * dy
    return r * (gd - xh * jnp.mean(xh * gd, axis=-1, keepdims=True)), xh


def _dot(a, b):
    return jnp.dot(a, b, preferred_element_type=F32)


def _dot_nt(a, b):
    return lax.dot_general(a, b, (((1,), (1,)), ((), ())), preferred_element_type=F32)


def _dot_tn(a, b):
    return lax.dot_general(a, b, (((0,), (0,)), ((), ())), preferred_element_type=F32)


def _accumulate(ref, first, value):
    @pl.when(first)
    def _():
        ref[...] = value

    @pl.when(jnp.logical_not(first))
    def _():
        ref[...] += value


def norm_matmul(x, g, w, *, tn, split, name, ride=None, tm=ROW_TILE):
    T, D = x.shape
    N = w.shape[1]
    per = N // split // tn

    def body(x_ref, g_ref, w_ref, o_ref, xn_ref):
        @pl.when(pl.program_id(1) == 0)
        def _():
            xf = x_ref[...]
            xn_ref[...] = (xf * _rms_r(xf) * g_ref[...]).astype(BF16)

        o_ref[...] = _dot(xn_ref[...], w_ref[...]).astype(BF16)

    return _call(
        body, name=name, grid=(T // tm, N // tn),
        in_specs=[pl.BlockSpec((tm, D), lambda i, j: (i, 0)),
                  pl.BlockSpec((1, D), lambda i, j: (0, 0)),
                  pl.BlockSpec((D, tn), lambda i, j: (0, j))],
        out_specs=[pl.BlockSpec((None, tm, tn), lambda i, j: (j // per, i, j % per)),
                   pl.BlockSpec((tm, D), lambda i, j: (i, 0))],
        out_shape=[SDS((split, T, N // split), BF16), SDS((T, D), BF16)],
        semantics=("parallel", "arbitrary"), args=(x, g, w), ride=ride)


def _shift_down(prev, cur, by):
    big = jnp.concatenate([prev, cur], axis=0)
    return pltpu.roll(big, by, 0)[prev.shape[0]:]


def _shift_up(cur, nxt, by):
    big = jnp.concatenate([cur, nxt], axis=0)
    return pltpu.roll(big, big.shape[0] - by, 0)[:cur.shape[0]]


def conv_mix_out(bcx, conv_w, w_out, g_post, res, *, name, ride=None, tm=ROW_TILE):
    T, D = res.shape
    hb = tm // BF16_ROWS

    def body(b_ref, c_ref, u_ref, cp_ref, up_ref, cw_ref, w_ref, g_ref, r_ref, h_ref, z_ref, y_ref):
        i = pl.program_id(0)
        cu = c_ref[...].astype(F32) * u_ref[...].astype(F32)
        cup = cp_ref[...].astype(F32) * up_ref[...].astype(F32)
        cup = jnp.where(i == 0, 0.0, cup)
        cv = (cw_ref[0:1, :] * _shift_down(cup, cu, 2) + cw_ref[1:2, :] * _shift_down(cup, cu, 1)
              + cw_ref[2:3, :] * cu)
        y = (b_ref[...].astype(F32) * cv).astype(BF16)
        y_ref[...] = y
        z = _dot(y, w_ref[...])
        z_ref[...] = z.astype(BF16)
        h_ref[...] = r_ref[...] + z * _rms_r(z) * g_ref[...]

    tile = lambda col: pl.BlockSpec((tm, D), lambda i: (i, col))
    halo = lambda col: pl.BlockSpec((BF16_ROWS, D), lambda i: (jnp.maximum(i * hb - 1, 0), col))
    row = pl.BlockSpec((tm, D), lambda i: (i, 0))
    return _call(
        body, name=name, grid=(T // tm,),
        in_specs=[tile(0), tile(1), tile(2), halo(1), halo(2),
                  pl.BlockSpec((3, D), lambda i: (0, 0)),
                  pl.BlockSpec((D, D), lambda i: (0, 0)),
                  pl.BlockSpec((1, D), lambda i: (0, 0)), row],
        out_specs=[row, row, row],
        out_shape=[SDS((T, D), F32), SDS((T, D), BF16), SDS((T, D), BF16)],
        semantics=("parallel",), args=(bcx, bcx, bcx, bcx, bcx, conv_w, w_out, g_post, res), ride=ride)


def plain_mix_out(a, w, g_post, res, *, name, target=None, tm=ROW_TILE):
    T, D = res.shape
    K = a.shape[1]
    with_loss = target is not None

    def body(a_ref, w_ref, g_ref, r_ref, *rest):
        z = _dot(a_ref[...], w_ref[...])
        h = r_ref[...] + z * _rms_r(z) * g_ref[...]
        if with_loss:
            t_ref, h_ref, z_ref, loss_ref = rest
            diff = h - t_ref[...]
            h_ref[...] = diff * (1.0 / D)
            part = jnp.full(loss_ref.shape, 0.5 / D, F32) * jnp.sum(diff * diff)
            _accumulate(loss_ref, pl.program_id(0) == 0, part)
        else:
            h_ref, z_ref = rest
            h_ref[...] = h
        z_ref[...] = z.astype(BF16)

    row = pl.BlockSpec((tm, D), lambda i: (i, 0))
    loss_spec, loss_shape = pl.BlockSpec((8, 128), lambda i: (0, 0)), SDS((8, 128), F32)
    return _call(
        body, name=name, grid=(T // tm,),
        in_specs=[pl.BlockSpec((tm, K), lambda i: (i, 0)),
                  pl.BlockSpec((K, D), lambda i: (0, 0)),
                  pl.BlockSpec((1, D), lambda i: (0, 0)), row] + [row] * with_loss,
        out_specs=[row, row] + [loss_spec] * with_loss,
        out_shape=[SDS((T, D), F32), SDS((T, D), BF16)] + [loss_shape] * with_loss,
        semantics=("arbitrary",), args=(a, w, g_post, res) + ((target,) if with_loss else ()))


def _silu_grads(d, g, u):
    sg = jax.nn.sigmoid(g)
    return d * u * (sg * (1.0 + g * (1.0 - sg))), d * (g * sg)


def norm_swiglu_in(x, g, w, *, name, ride=None, tm=ROW_TILE // 2):
    T, D = x.shape
    F = w.shape[1] // 2

    def body(x_ref, g_ref, wg_ref, wu_ref, gu_ref, a_ref, xt_ref):
        xf = x_ref[...]
        xn = xf * _rms_r(xf) * g_ref[...]
        xt_ref[...] = xn.T.astype(BF16)
        xb = xn.astype(BF16)
        gate = _dot(xb, wg_ref[...]).astype(BF16)
        up = _dot(xb, wu_ref[...]).astype(BF16)
        gu_ref[0] = gate
        gu_ref[1] = up
        a_ref[...] = gate * jax.nn.sigmoid(gate) * up

    half = lambda s: pl.BlockSpec((D, F), lambda i: (0, s), pipeline_mode=pl.Buffered(1))
    return _call(
        body, name=name, grid=(T // tm,),
        in_specs=[pl.BlockSpec((tm, D), lambda i: (i, 0)), pl.BlockSpec((1, D), lambda i: (0, 0)), half(0), half(1)],
        out_specs=[pl.BlockSpec((2, tm, F), lambda i: (0, i, 0)), pl.BlockSpec((tm, F), lambda i: (i, 0)),
                   pl.BlockSpec((D, tm), lambda i: (0, i))],
        out_shape=[SDS((2, T, F), BF16), SDS((T, F), BF16), SDS((D, T), BF16)],
        semantics=("parallel",), args=(x, g, w, w), ride=ride)


def swiglu_bwd_tn(xt, dact, gu, *, name, ride=None, tb=MXU_WIDTH):
    D, T = xt.shape
    F = dact.shape[1]

    def body(xt_ref, d_ref, g_ref, u_ref, o_ref):
        dg, du = _silu_grads(d_ref[...], g_ref[...], u_ref[...])
        o_ref[0] = _dot(xt_ref[...], dg).astype(BF16)
        o_ref[1] = _dot(xt_ref[...], du).astype(BF16)

    col = lambda s: pl.BlockSpec((None, T, tb), lambda j: (s, 0, j))
    out = _call(
        body, name=name, grid=(F // tb,),
        in_specs=[pl.BlockSpec((D, T), lambda j: (0, 0), pipeline_mode=pl.Buffered(1)),
                  pl.BlockSpec((T, tb), lambda j: (0, j)), col(0), col(1)],
        out_specs=[pl.BlockSpec((2, D, tb), lambda j: (0, 0, j))],
        out_shape=[SDS((2, D, F), BF16)],
        semantics=("parallel",), args=(xt, dact, gu, gu), ride=ride)
    return out[0] if ride is None else (out[0][0], out[1])


def swiglu_bwd_in(dact, gu, w, h_in, g, dh_out, *, name, ride=None, tm=ROW_TILE // 2):
    T, D = h_in.shape
    F = dact.shape[1]

    def body(d_ref, gg_ref, uu_ref, wg_ref, wu_ref, h_ref, g_ref, dh_ref, o_ref, dg_ref):
        dgate, dup = _silu_grads(d_ref[...], gg_ref[...], uu_ref[...])
        dn = _dot_nt(dgate, wg_ref[...]) + _dot_nt(dup, wu_ref[...])
        dx, hh = _rmsnorm_bwd(h_ref[...], g_ref[...], dn)
        o_ref[...] = dh_ref[...] + dx
        _accumulate(dg_ref, pl.program_id(0) == 0, jnp.sum(dn * hh, axis=0, keepdims=True))

    row = pl.BlockSpec((tm, D), lambda i: (i, 0))
    vec = pl.BlockSpec((1, D), lambda i: (0, 0))
    part = lambda s: pl.BlockSpec((None, tm, F), lambda i: (s, i, 0))
    half = lambda s: pl.BlockSpec((D, F), lambda i: (0, s), pipeline_mode=pl.Buffered(1))
    return _call(
        body, name=name, grid=(T // tm,),
        in_specs=[pl.BlockSpec((tm, F), lambda i: (i, 0)), part(0), part(1), half(0), half(1), row, vec, row],
        out_specs=[row, vec],
        out_shape=[SDS((T, D), F32), SDS((1, D), F32)],
        semantics=("arbitrary",), args=(dact, gu, gu, w, w, h_in, g, dh_out), ride=ride)


def rope_tables(T):
    half = ROT_DIM // 2
    inv_freq = ROPE_THETA ** (-jnp.arange(0, ROT_DIM, 2, dtype=F32) / ROT_DIM)
    ang = (jnp.arange(T, dtype=F32)[:, None] * inv_freq[None, :]).T
    cos, sin = jnp.cos(ang), jnp.sin(ang)
    rest = HEAD_DIM - ROT_DIM
    one, zero = jnp.ones((rest, T), F32), jnp.zeros((rest, T), F32)
    zh = jnp.zeros((half, T), F32)
    fac = jnp.concatenate([cos, cos, one], axis=0)
    up = jnp.concatenate([-sin, zh, zero], axis=0)
    down = jnp.concatenate([zh, sin, zero], axis=0)
    return jnp.stack([fac, up, down])


def _rope(t, tab):
    half = ROT_DIM // 2
    return t * tab[0] + pltpu.roll(t, HEAD_DIM - half, 0) * tab[1] + pltpu.roll(t, half, 0) * tab[2]


def _rope_t(d, tab):
    half = ROT_DIM // 2
    return d * tab[0] + pltpu.roll(d * tab[1], half, 0) + pltpu.roll(d * tab[2], HEAD_DIM - half, 0)


def _head(t, h):
    return t[h * HEAD_DIM:(h + 1) * HEAD_DIM]


def _band(n, group):
    kj = lax.broadcasted_iota(jnp.int32, (2 * BLOCK, BLOCK), 0)
    qi = lax.broadcasted_iota(jnp.int32, (2 * BLOCK, BLOCK), 1)
    mask = (kj > qi) & (kj <= qi + BLOCK) & ((n > 0) | (kj >= BLOCK))
    return jnp.tile(mask, (1, group))


def _attn_specs(D, kvd):
    prev = lambda n: jnp.maximum(n - 1, 0)
    return [pl.BlockSpec((BLOCK, D), lambda n: (n, 0)),
            pl.BlockSpec((BLOCK, kvd), lambda n: (prev(n), 0)),
            pl.BlockSpec((BLOCK, kvd), lambda n: (n, 0)),
            pl.BlockSpec((BLOCK, kvd), lambda n: (prev(n), 1)),
            pl.BlockSpec((BLOCK, kvd), lambda n: (n, 1)),
            pl.BlockSpec((3, HEAD_DIM, BLOCK), lambda n: (0, 0, prev(n))),
            pl.BlockSpec((3, HEAD_DIM, BLOCK), lambda n: (0, 0, n)),
            pl.BlockSpec(memory_space=pltpu.SMEM)]


def _attn_operands(q_ref, kp_ref, k_ref, vp_ref, v_ref, tp_ref, t_ref):
    flip = lambda ref: ref[...].astype(F32).T
    tab = t_ref[...]
    kt = jnp.concatenate([flip(kp_ref), flip(k_ref)], axis=1)
    vt = jnp.concatenate([flip(vp_ref), flip(v_ref)], axis=1)
    return flip(q_ref), kt, vt, tab, jnp.concatenate([tp_ref[...], tab], axis=2)


SCORE_SCALE = 1.0 / math.sqrt(HEAD_DIM)


def _group_heads(t, j, group, tab=None):
    heads = [_head(t, j * group + g) for g in range(group)]
    if tab is not None:
        heads = [_rope(h, tab) * SCORE_SCALE for h in heads]
    return jnp.concatenate(heads, axis=1).astype(BF16)


def _sink_row(s_ref, j, group):
    which = lax.broadcasted_iota(jnp.int32, (1, group * BLOCK), 1) // BLOCK
    row = jnp.zeros((1, group * BLOCK), F32)
    for g in range(group):
        row = jnp.where(which == g, s_ref[0, j * group + g], row)
    return row


def _softmax_block(k_j, q_j, sink, mask):
    s = jnp.where(mask, _dot_tn(k_j, q_j), NEG)
    m = jnp.maximum(jnp.max(s, axis=0, keepdims=True), sink)
    e = jnp.exp(s - m)
    es = jnp.exp(sink - m)
    inv = 1.0 / (jnp.sum(e, axis=0, keepdims=True) + es)
    return e * inv, es * inv


def attention_fwd(q, kv, tabs, sinks, *, name, ride=None):
    T, D = q.shape
    kvd = kv.shape[1] // 2
    group = D // HEAD_DIM // N_KV_HEADS

    def body(q_ref, kp_ref, k_ref, vp_ref, v_ref, tp_ref, t_ref, s_ref, o_ref):
        mask = _band(pl.program_id(0), group)
        qt, kt, vt, tab, tab2 = _attn_operands(q_ref, kp_ref, k_ref, vp_ref, v_ref, tp_ref, t_ref)
        outs = []
        for j in range(N_KV_HEADS):
            k_j = _rope(_head(kt, j), tab2).astype(BF16)
            p, _ = _softmax_block(k_j, _group_heads(qt, j, group, tab), _sink_row(s_ref, j, group), mask)
            o_j = _dot(_head(vt, j).astype(BF16), p.astype(BF16))
            outs += [o_j[:, g * BLOCK:(g + 1) * BLOCK] for g in range(group)]
        o_ref[...] = jnp.concatenate(outs, axis=0).T.astype(BF16)

    return _call(
        body, name=name, grid=(T // BLOCK,),
        in_specs=_attn_specs(D, kvd),
        out_specs=[pl.BlockSpec((BLOCK, D), lambda n: (n, 0))],
        out_shape=[SDS((T, D), BF16)],
        semantics=("parallel",), args=(q, kv, kv, kv, kv, tabs, tabs, sinks), ride=ride)


def attention_bwd(q, kv, tabs, sinks, do, *, name, ride=None):
    T, D = q.shape
    kvd = kv.shape[1] // 2
    heads = D // HEAD_DIM
    group = heads // N_KV_HEADS

    def body(q_ref, kp_ref, k_ref, vp_ref, v_ref, tp_ref, t_ref, s_ref, do_ref, dq_ref, dc_ref, dp_ref, ds_ref):
        n = pl.program_id(0)
        mask = _band(n, group)
        qt, kt, vt, tab, tab2 = _attn_operands(q_ref, kp_ref, k_ref, vp_ref, v_ref, tp_ref, t_ref)
        dot = do_ref[...].astype(F32).T
        lane = lax.broadcasted_iota(jnp.int32, (8, 128), 1)
        dsink = jnp.zeros((8, 128), F32)
        dqs, dks, dvs = [], [], []
        for j in range(N_KV_HEADS):
            k_j = _rope(_head(kt, j), tab2).astype(BF16)
            v_j = _head(vt, j).astype(BF16)
            q_j = _group_heads(qt, j, group, tab)
            do_j = _group_heads(dot, j, group)
            p, p_sink = _softmax_block(k_j, q_j, _sink_row(s_ref, j, group), mask)
            dp = _dot_tn(v_j, do_j)
            dl = jnp.sum(p * dp, axis=0, keepdims=True)
            dsc = (p * (dp - dl)).astype(BF16)
            dq_j = _dot(k_j, dsc) * SCORE_SCALE
            dqs += [_rope_t(dq_j[:, g * BLOCK:(g + 1) * BLOCK], tab) for g in range(group)]
            dks.append(_rope_t(_dot_nt(q_j, dsc), tab2))
            dvs.append(_dot_nt(do_j, p.astype(BF16)))
            weight = p_sink * dl
            for g in range(group):
                dsink = dsink - jnp.where(lane == j * group + g, jnp.sum(weight[:, g * BLOCK:(g + 1) * BLOCK]), 0.0)
        dq_ref[...] = jnp.concatenate(dqs, axis=0).T.astype(BF16)
        dkv = jnp.concatenate(dks + dvs, axis=0)
        dp_ref[...] = dkv[:, :BLOCK].T
        dc_ref[...] = dkv[:, BLOCK:].T
        _accumulate(ds_ref, n == 0, dsink)

    blk = lambda w: pl.BlockSpec((BLOCK, w), lambda n: (n, 0))
    return _call(
        body, name=name, grid=(T // BLOCK,),
        in_specs=_attn_specs(D, kvd) + [blk(D)],
        out_specs=[blk(D), blk(2 * kvd), blk(2 * kvd), pl.BlockSpec((8, 128), lambda n: (0, 0))],
        out_shape=[SDS((T, D), BF16), SDS((T, 2 * kvd), F32), SDS((T, 2 * kvd), F32), SDS((8, 128), F32)],
        semantics=("arbitrary",), args=(q, kv, kv, kv, kv, tabs, tabs, sinks, do), ride=ride)


def combine_dkv(d_cur, d_prev, *, name):
    T, W = d_cur.shape
    tm = ROW_TILE
    nt, per, last = T // tm, tm // BLOCK, T // BLOCK - 1

    def body(c_ref, p_ref, pn_ref, o_ref):
        nxt = jnp.where(pl.program_id(0) == nt - 1, 0.0, pn_ref[...])
        o_ref[...] = (c_ref[...] + jnp.concatenate([p_ref[BLOCK:, :], nxt], axis=0)).astype(BF16)

    return _call(
        body, name=name, grid=(nt,),
        in_specs=[pl.BlockSpec((tm, W), lambda i: (i, 0)), pl.BlockSpec((tm, W), lambda i: (i, 0)),
                  pl.BlockSpec((BLOCK, W), lambda i: (jnp.minimum((i + 1) * per, last), 0))],
        out_specs=[pl.BlockSpec((tm, W), lambda i: (i, 0))],
        out_shape=[SDS((T, W), BF16)],
        semantics=("parallel",), args=(d_cur, d_prev, d_prev))[0]


def normbwd_matmul_nt(z, g, dh, w, *, name, ride=None, tm=ROW_TILE):
    T, D = z.shape
    K = w.shape[0]

    def body(z_ref, g_ref, dh_ref, w_ref, dz_ref, dg_ref, o_ref):
        dh_ = dh_ref[...]
        dz, zh = _rmsnorm_bwd(z_ref[...].astype(F32), g_ref[...], dh_)
        dz = dz.astype(BF16)
        dz_ref[...] = dz
        _accumulate(dg_ref, pl.program_id(0) == 0, jnp.sum(dh_ * zh, axis=0, keepdims=True))
        o_ref[...] = _dot_nt(dz, w_ref[...]).astype(BF16)

    row = pl.BlockSpec((tm, D), lambda i: (i, 0))
    vec = pl.BlockSpec((1, D), lambda i: (0, 0))
    return _call(
        body, name=name, grid=(T // tm,),
        in_specs=[row, vec, row, pl.BlockSpec((K, D), lambda i: (0, 0))],
        out_specs=[row, vec, pl.BlockSpec((tm, K), lambda i: (i, 0))],
        out_shape=[SDS((T, D), BF16), SDS((1, D), F32), SDS((T, K), BF16)],
        semantics=("arbitrary",), args=(z, g, dh, w), ride=ride)


def matmul_nt_normbwd(da, w, h_in, g, dh_out, *, name, ride=None, tm=ROW_TILE):
    T, D = h_in.shape
    S, _, K = da.shape

    def body(*refs):
        da_refs, w_refs = refs[:S], refs[S:2 * S]
        h_ref, g_ref, dh_ref, o_ref, dg_ref = refs[2 * S:]
        dn = _dot_nt(da_refs[0][...], w_refs[0][...])
        for s in range(1, S):
            dn = dn + _dot_nt(da_refs[s][...], w_refs[s][...])
        dx, hh = _rmsnorm_bwd(h_ref[...], g_ref[...], dn)
        o_ref[...] = dh_ref[...] + dx
        _accumulate(dg_ref, pl.program_id(0) == 0, jnp.sum(dn * hh, axis=0, keepdims=True))

    row = pl.BlockSpec((tm, D), lambda i: (i, 0))
    vec = pl.BlockSpec((1, D), lambda i: (0, 0))
    part = lambda s: pl.BlockSpec((None, tm, K), lambda i: (s, i, 0))
    cols = lambda s: pl.BlockSpec((D, K), lambda i: (0, s), pipeline_mode=pl.Buffered(1))
    return _call(
        body, name=name, grid=(T // tm,),
        in_specs=[part(s) for s in range(S)] + [cols(s) for s in range(S)] + [row, vec, row],
        out_specs=[row, vec],
        out_shape=[SDS((T, D), F32), SDS((1, D), F32)],
        semantics=("arbitrary",), args=[da] * S + [w] * S + [h_in, g, dh_out], ride=ride)


def matmul_tn(a, b, *, tb, name, ride=None, ta=MXU_WIDTH):
    T, Ka = a.shape
    S, _, Nb = b.shape
    per = Nb // tb

    def body(a_ref, b_ref, o_ref):
        o_ref[...] = _dot_tn(a_ref[...], b_ref[...]).astype(BF16)

    out = _call(
        body, name=name, grid=(S * per, Ka // ta),
        in_specs=[pl.BlockSpec((T, ta), lambda j, i: (0, i)),
                  pl.BlockSpec((None, T, tb), lambda j, i: (j // per, 0, j % per))],
        out_specs=[pl.BlockSpec((ta, tb), lambda j, i: (i, j))],
        out_shape=[SDS((Ka, S * Nb), BF16)],
        semantics=("parallel", "parallel"), args=(a, b), ride=ride)
    return out[0] if ride is None else (out[0][0], out[1])


def conv_bwd(dy, bcx, conv_w, *, name, tm=ROW_TILE):
    T, D = dy.shape
    nt = T // tm
    hb = tm // BF16_ROWS
    last = T // BF16_ROWS - 1

    def body(dy_ref, dyn_ref, b_ref, bn_ref, c_ref, u_ref, cp_ref, up_ref, cw_ref, o_ref, dw_ref):
        i = pl.program_id(0)
        c, u = c_ref[...].astype(F32), u_ref[...].astype(F32)
        cu = c * u
        cup = jnp.where(i == 0, 0.0, cp_ref[...].astype(F32) * up_ref[...].astype(F32))
        cu1, cu2 = _shift_down(cup, cu, 1), _shift_down(cup, cu, 2)
        w0, w1, w2 = cw_ref[0:1, :], cw_ref[1:2, :], cw_ref[2:3, :]
        dyf = dy_ref[...].astype(F32)
        o_ref[:, 0:D] = (dyf * (w0 * cu2 + w1 * cu1 + w2 * cu)).astype(BF16)
        dcv = dyf * b_ref[...].astype(F32)
        dcvn = jnp.where(i == nt - 1, 0.0, dyn_ref[...].astype(F32) * bn_ref[...].astype(F32))
        dcu = w2 * dcv + w1 * _shift_up(dcv, dcvn, 1) + w0 * _shift_up(dcv, dcvn, 2)
        o_ref[:, D:2 * D] = (dcu * u).astype(BF16)
        o_ref[:, 2 * D:3 * D] = (dcu * c).astype(BF16)
        row = lax.broadcasted_iota(jnp.int32, (8, D), 0)
        dw = jnp.zeros((8, D), F32)
        for tap, t in enumerate((cu2, cu1, cu)):
            dw = jnp.where(row == tap, jnp.sum(dcv * t, axis=0, keepdims=True), dw)
        _accumulate(dw_ref, i == 0, dw)

    tile = lambda col: pl.BlockSpec((tm, D), lambda i: (i, col))
    prev = lambda col: pl.BlockSpec((BF16_ROWS, D), lambda i: (jnp.maximum(i * hb - 1, 0), col))
    nxt = lambda col: pl.BlockSpec((BF16_ROWS, D), lambda i: (jnp.minimum((i + 1) * hb, last), col))
    return _call(
        body, name=name, grid=(nt,),
        in_specs=[tile(0), nxt(0), tile(0), nxt(0), tile(1), tile(2), prev(1), prev(2),
                  pl.BlockSpec((3, D), lambda i: (0, 0))],
        out_specs=[pl.BlockSpec((tm, 3 * D), lambda i: (i, 0)), pl.BlockSpec((8, D), lambda i: (0, 0))],
        out_shape=[SDS((T, 3 * D), BF16), SDS((8, D), F32)],
        semantics=("arbitrary",), args=(dy, dy, bcx, bcx, bcx, bcx, bcx, bcx, conv_w))


class NoTraffic:
    def ride(self, kernel_name):
        return None

    def landed(self, kernel_name, results, wts):
        pass

    def grad(self, key, value):
        pass


def local_step(x, target, wts, vec, traffic):
    T, D = x.shape
    tabs = rope_tables(T)
    small = {}

    def run(builder, *args, name, **kw):
        ride = traffic.ride(name)
        if ride is None:
            return builder(*args, name=name, **kw)
        out, extra = builder(*args, name=name, ride=ride, **kw)
        traffic.landed(name, extra, wts)
        return out

    bcx, xn1 = run(norm_matmul, x, vec["a_pre"], wts["w_in"], tn=D, split=1, name="a_in")
    bcx = bcx[0]
    h1, z0, y0 = run(conv_mix_out, bcx, vec["conv_w"], wts["w_out"], vec["a_post"], x, name="a_out")
    gu0, act0, xt2 = run(norm_swiglu_in, h1, vec["ffn_pre0"], wts["gu0"], name="ffn0_in")
    h2, z1 = plain_mix_out(act0, wts["wd0"], vec["ffn_post0"], h1, name="ffn0_out")
    kvp, xkv = norm_matmul(h2, vec["kv_norm"], wts["w_kv"], tn=wts["w_kv"].shape[1], split=1, name="kv_in")
    qp, xq = norm_matmul(h2, vec["b_pre"], wts["w_q"], tn=D, split=1, name="q_in")
    kvp, qp = kvp[0], qp[0]
    (attn,) = run(attention_fwd, qp, kvp, tabs, vec["sinks"], name="attn_fwd")
    h3, z2 = plain_mix_out(attn, wts["w_o"], vec["b_post"], h2, name="attn_out")
    gu1, act1, xt3 = run(norm_swiglu_in, h3, vec["ffn_pre1"], wts["gu1"], name="ffn1_in")
    dy, z3, loss = plain_mix_out(act1, wts["wd1"], vec["ffn_post1"], h3, name="ffn1_out", target=target)

    def ffn_bwd(layer, z, gu, act, xt, h_in, dh):
        tag = "ffn%d" % layer
        dz, small["ffn_post%d" % layer], dact = run(
            normbwd_matmul_nt, z, vec["ffn_post%d" % layer], dh, wts["wd%d" % layer], name=tag + "_out_bwd")
        traffic.grad("wd%d" % layer, matmul_tn(act, dz[None], tb=D, name=tag + "_dwd"))
        traffic.grad("gu%d" % layer, run(swiglu_bwd_tn, xt, dact, gu, name=tag + "_dwgu"))
        dh_in, small["ffn_pre%d" % layer] = run(
            swiglu_bwd_in, dact, gu, wts["gu%d" % layer], h_in, vec["ffn_pre%d" % layer], dh, name=tag + "_in_bwd")
        return dh_in

    dh3 = ffn_bwd(1, z3, gu1, act1, xt3, h3, dy)
    dz2, small["b_post"], dattn = normbwd_matmul_nt(z2, vec["b_post"], dh3, wts["w_o"], name="attn_out_bwd")
    traffic.grad("w_o", matmul_tn(attn, dz2[None], tb=D, name="attn_dwo"))
    dq, dkv_cur, dkv_prev, small["sinks"] = attention_bwd(qp, kvp, tabs, vec["sinks"], dattn, name="attn_bwd")
    dkv = combine_dkv(dkv_cur, dkv_prev, name="attn_dkv")
    traffic.grad("w_q", matmul_tn(xq, dq[None], tb=D, name="attn_dwq"))
    traffic.grad("w_kv", matmul_tn(xkv, dkv[None], tb=dkv.shape[1], name="attn_dwkv"))
    dh2, small["b_pre"] = matmul_nt_normbwd(dq[None], wts["w_q"], h2, vec["b_pre"], dh3, name="q_in_bwd")
    dh2, small["kv_norm"] = matmul_nt_normbwd(dkv[None], wts["w_kv"], h2, vec["kv_norm"], dh2, name="kv_in_bwd")
    dh1 = ffn_bwd(0, z1, gu0, act0, xt2, h1, dh2)
    dz0, small["a_post"], dyc = normbwd_matmul_nt(z0, vec["a_post"], dh1, wts["w_out"], name="a_out_bwd")
    traffic.grad("w_out", matmul_tn(y0, dz0[None], tb=D, name="a_dwout"))
    dbcx, small["conv_w"] = conv_bwd(dyc, bcx, vec["conv_w"], name="a_conv_bwd")
    traffic.grad("w_in", matmul_tn(xn1, dbcx[None], tb=3 * D // 2, name="a_dwin"))
    dx, small["a_pre"] = run(matmul_nt_normbwd, dbcx[None], wts["w_in"], x, vec["a_pre"], dh1, name="a_in_bwd")
    return loss, dx, small


SMALL_ROWS = 16
LOSS_ROW = 13

GATHER_PLAN = {"a_in": ["gu0"], "a_out": ["wd0"], "ffn0_in": ["w_kv", "w_q", "w_o"], "attn_fwd": ["gu1"],
               "ffn1_in": ["wd1"]}
REDUCE_PLAN = [(["wd1"], "ffn1_dwgu"), (["gu1"], "ffn1_in_bwd"), (["w_o", "w_q", "w_kv"], "ffn0_out_bwd"),
               (["wd0"], "ffn0_dwgu"), (["gu0"], "ffn0_in_bwd"), (["w_out", "w_in"], "a_in_bwd")]
GRAD_KIND = dict(KIND, gu0="split", gu1="split")


class Traffic:
    def __init__(self, wholes, quarter, c_arr):
        self.wholes, self.quarter, self.c_arr = wholes, quarter, c_arr
        self.sums, self.got = {}, {}
        self.ready = {}

    def meta(self, keys, kinds=KIND):
        return [(kinds[k], self.quarter[k]) for k in keys]

    def ride(self, name):
        if name in GATHER_PLAN:
            keys = GATHER_PLAN[name]
            return gather_ride([self.wholes[k] for k in keys], self.meta(keys))
        if name in self.ready:
            keys = self.ready[name]
            return chip_ride([self.sums[k] for k in keys], self.meta(keys, GRAD_KIND))
        return None

    def landed(self, name, results, wts):
        if name in GATHER_PLAN:
            wts.update(zip(GATHER_PLAN[name], results))
        else:
            self.got.update(zip(self.ready[name], results))

    def grad(self, key, value):
        r, ws = self.quarter[key]
        view = {"row": (N_CHIPS, 2, r // 2, ws), "col": (1, 2, r // 2, N_CHIPS * ws), "split": (2, 2, r // 2, 2 * ws)}
        self.sums[key] = value.reshape(view[GRAD_KIND[key]])
        for keys, carrier in REDUCE_PLAN:
            if key == keys[-1]:
                own = [self.sums[k] for k in keys]
                got = pair_exchange(own, name="pair_exchange_" + keys[0])
                for k, o, g in zip(keys, own, got):
                    self.sums[k] = pair_add(o, g, self.c_arr, name="pair_add_" + k)
                self.ready[carrier] = keys


def kernel(x, a_pre_norm, a_w_in, a_conv_w, a_w_out, a_post_norm, ffn_pre_norm, ffn_w_gate_up, ffn_w_down, ffn_post_norm, kv_norm, w_kv, b_pre_norm, b_w_q, b_sinks, b_w_o, b_post_norm, loss_target, m_a_pre_norm, m_a_w_in, m_a_conv_w, m_a_w_out, m_a_post_norm, m_ffn_pre_norm, m_ffn_w_gate_up, m_ffn_w_down, m_ffn_post_norm, m_kv_norm, m_w_kv, m_b_pre_norm, m_b_w_q, m_b_sinks, m_b_w_o, m_b_post_norm, v_a_pre_norm, v_a_w_in, v_a_conv_w, v_a_w_out, v_a_post_norm, v_ffn_pre_norm, v_ffn_w_gate_up, v_ffn_w_down, v_ffn_post_norm, v_kv_norm, v_w_kv, v_b_pre_norm, v_b_w_q, v_b_sinks, v_b_w_o, v_b_post_norm):
    T, D = x.shape[1], x.shape[2]
    xi, yi, ci = _place()
    p = 2 * xi + yi
    p_arr = jnp.reshape(p, (1,)).astype(jnp.int32)
    c_arr = jnp.reshape(ci, (1,)).astype(jnp.int32)
    pc_arr = jnp.stack([p, ci]).astype(jnp.int32)
    me_arr = jnp.reshape(4 * xi + 2 * yi + ci, (1,)).astype(jnp.int32)
    qd = D // N_CHIPS

    big = {"w_in": (a_w_in, 0), "w_out": (a_w_out, 0), "gu0": (ffn_w_gate_up, 0), "gu1": (ffn_w_gate_up, 1),
           "wd0": (ffn_w_down, 0), "wd1": (ffn_w_down, 1), "w_kv": (w_kv[None], 0), "w_q": (b_w_q, 0),
           "w_o": (b_w_o, 0)}
    names = list(big)
    quarter = {k: w.shape[1:] for k, (w, _) in big.items()}
    wholes = {k: cast_quarter(w, layer, KIND[k], p_arr, name="cast_" + k) for k, (w, layer) in big.items()}
    traffic = Traffic(wholes, quarter, c_arr)
    small_shard = jnp.concatenate([a_pre_norm, a_post_norm, a_conv_w[0], jnp.zeros((3, qd), F32)], axis=0)
    first = ["w_in", "w_out"]
    *landed, small_full = alone(gather_ride([wholes[k] for k in first], traffic.meta(first), small_shard),
                                name="gather_first")
    wts = dict(zip(first, landed))
    rows = lambda k: jnp.transpose(small_full[:, k], (1, 0, 2)).reshape(-1, D)
    vec = {"a_pre": rows(slice(0, 1)), "a_post": rows(slice(1, 2)), "conv_w": rows(slice(2, 5)),
           "ffn_pre0": ffn_pre_norm[0:1], "ffn_pre1": ffn_pre_norm[1:2],
           "ffn_post0": ffn_post_norm[0:1], "ffn_post1": ffn_post_norm[1:2],
           "kv_norm": kv_norm[None], "b_pre": b_pre_norm, "b_post": b_post_norm, "sinks": b_sinks}

    loss, dx, small = local_step(x[0], loss_target[0], wts, vec, traffic)

    pad = lambda a: jnp.pad(a, ((0, 0), (0, D - a.shape[1])))
    small_block = jnp.concatenate(
        [small["a_pre"], small["a_post"], small["conv_w"][0:3], small["ffn_pre0"], small["ffn_pre1"],
         small["ffn_post0"], small["ffn_post1"], small["kv_norm"], small["b_pre"], small["b_post"],
         pad(small["sinks"][0:1]), pad(loss[0:1]), jnp.zeros((SMALL_ROWS - LOSS_ROW - 1, D), F32)], axis=0)
    (small_blocks,) = alone(chip_ride([], [], small_block), name="small_exchange")
    halves = [chip_reduce(traffic.sums[k], traffic.got[k], GRAD_KIND[k], pc_arr, name="chip_reduce_" + k)
              for k in names]
    grad = dict(zip(names, [q.reshape(quarter[k]) for k, q in zip(names, half_exchange(halves))]))
    small_sum = small_reduce(small_blocks, me_arr)

    out = {}
    out["a_w_in"] = adamw(a_w_in, [grad["w_in"]], m_a_w_in, v_a_w_in, name="adamw_a_w_in")
    out["a_w_out"] = adamw(a_w_out, [grad["w_out"]], m_a_w_out, v_a_w_out, name="adamw_a_w_out")
    out["ffn_w_gate_up"] = adamw(ffn_w_gate_up, [grad["gu0"], grad["gu1"]], m_ffn_w_gate_up, v_ffn_w_gate_up,
                                 name="adamw_ffn_w_gate_up")
    out["ffn_w_down"] = adamw(ffn_w_down, [grad["wd0"], grad["wd1"]], m_ffn_w_down, v_ffn_w_down,
                              name="adamw_ffn_w_down")
    out["w_kv"] = [o[0] for o in adamw(w_kv[None], [grad["w_kv"]], m_w_kv[None], v_w_kv[None], name="adamw_w_kv")]
    out["b_w_q"] = adamw(b_w_q, [grad["w_q"]], m_b_w_q, v_b_w_q, name="adamw_b_w_q")
    out["b_w_o"] = adamw(b_w_o, [grad["w_o"]], m_b_w_o, v_b_w_o, name="adamw_b_w_o")

    def pack(a_pre, a_post, conv, ffn_pre, ffn_post, kvn, b_pre, b_post, sinks):
        return jnp.concatenate([pad(a_pre), pad(a_post), pad(conv[0]), ffn_pre, ffn_post, kvn[None], b_pre, b_post,
                                pad(sinks), jnp.zeros((SMALL_ROWS - 13, D), F32)], axis=0)

    g_small = jnp.concatenate([pad(lax.dynamic_slice(small_sum, (0, p * qd), (5, qd))), small_sum[5:]], axis=0)
    w_small = pack(a_pre_norm, a_post_norm, a_conv_w, ffn_pre_norm, ffn_post_norm, kv_norm, b_pre_norm, b_post_norm,
                   b_sinks)
    m_small = pack(m_a_pre_norm, m_a_post_norm, m_a_conv_w, m_ffn_pre_norm, m_ffn_post_norm, m_kv_norm,
                   m_b_pre_norm, m_b_post_norm, m_b_sinks)
    v_small = pack(v_a_pre_norm, v_a_post_norm, v_a_conv_w, v_ffn_pre_norm, v_ffn_post_norm, v_kv_norm,
                   v_b_pre_norm, v_b_post_norm, v_b_sinks)
    packed = adamw(w_small[None], [g_small], m_small[None], v_small[None], name="adamw_small")
    ns = b_sinks.shape[1]
    unpack = lambda a: {"a_pre_norm": a[0:1, :qd], "a_post_norm": a[1:2, :qd], "a_conv_w": a[None, 2:5, :qd],
                        "ffn_pre_norm": a[5:7], "ffn_post_norm": a[7:9], "kv_norm": a[9], "b_pre_norm": a[10:11],
                        "b_post_norm": a[11:12], "b_sinks": a[12:13, :ns]}
    unpacked = [unpack(a[0]) for a in packed]
    for k in unpacked[0]:
        out[k] = [u[k] for u in unpacked]

    order = ["a_pre_norm", "a_w_in", "a_conv_w", "a_w_out", "a_post_norm", "ffn_pre_norm", "ffn_w_gate_up",
             "ffn_w_down", "ffn_post_norm", "kv_norm", "w_kv", "b_pre_norm", "b_w_q", "b_sinks", "b_w_o",
             "b_post_norm"]
    return (small_sum[LOSS_ROW, 0], dx[None], *[out[k][0] for k in order], *[out[k][1] for k in order],
            *[out[k][2] for k in order], *[out[k][3] for k in order])
```

```python
import math

import jax
import jax.numpy as jnp
from jax import lax
from jax.experimental import pallas as pl
from jax.experimental.pallas import tpu as pltpu

F32 = jnp.float32
BF16 = jnp.bfloat16
SDS = jax.ShapeDtypeStruct
MESH = pl.DeviceIdType.MESH
DMA = pltpu.SemaphoreType.DMA
HBM_SPEC = pl.BlockSpec(memory_space=pltpu.HBM)

EPS = 1e-6
NEG = -1e30
HEAD_DIM = 64
N_KV_HEADS = 4
BLOCK = 128
ROT_DIM = HEAD_DIM // 4
ROPE_THETA = 500000.0
N_CHIPS = 4

ADAM_LR = 0.001
ADAM_B1 = 0.9
ADAM_B2 = 0.999
ADAM_EPS = 1e-08
ADAM_WD = 0.01
ADAM_STEP = 10

VMEM_LIMIT_BYTES = 52 * 1024 * 1024
ROW_TILE = 512
BF16_ROWS = 16
MXU_WIDTH = 256

KIND = {"w_in": "col", "gu0": "col", "gu1": "col", "w_out": "row", "wd0": "row", "wd1": "row", "w_kv": "row",
        "w_q": "row", "w_o": "row"}


def _params(*semantics):
    return pltpu.CompilerParams(dimension_semantics=semantics, vmem_limit_bytes=VMEM_LIMIT_BYTES)


def _row_tile(rows, limit, step=8):
    return max(t for t in range(step, limit + 1, step) if rows % t == 0)


def _place():
    return lax.axis_index("x"), lax.axis_index("y"), lax.axis_index("c")


def _other_chips(x, y):
    return [(1 - x, y), (x, 1 - y), (1 - x, 1 - y)]


def _remote(src, dst, send_sem, recv_sem, to):
    return pltpu.make_async_remote_copy(src_ref=src, dst_ref=dst, send_sem=send_sem, recv_sem=recv_sem,
                                        device_id=to, device_id_type=MESH)


def _full_shape(kind, quarter):
    r, ws = quarter
    return (N_CHIPS * r, ws) if kind == "row" else (r, N_CHIPS * ws)


def _half_of_quarter(ref, kind, quarter, sixteenths, q, half):
    r, ws = quarter
    h = r // 2
    lo, n = sixteenths[0] * h // 16, sixteenths[1] * h // 16
    assert lo % BF16_ROWS == 0 and n % BF16_ROWS == 0, (quarter, sixteenths)
    if kind == "row":
        return ref.at[pl.ds(pl.multiple_of(q * r + half * h + lo, BF16_ROWS), n)]
    return ref.at[pl.ds(pl.multiple_of(half * h + lo, BF16_ROWS), n), pl.ds(pl.multiple_of(q * ws, 128), ws)]


class Ride:
    def __init__(self, operands, out_shape, aliases, sems, make):
        self.operands, self.out_shape, self.aliases, self.sems, self.make = operands, out_shape, aliases, sems, make


def _call(body, *, name, grid, in_specs, out_specs, out_shape, args, scratch_shapes=(), semantics=None, ride=None):
    if ride is None:
        return pl.pallas_call(body, name=name, grid=grid, in_specs=in_specs, out_specs=out_specs,
                              out_shape=out_shape, scratch_shapes=list(scratch_shapes),
                              compiler_params=_params(*semantics))(*args)
    n_in, n_out, n_scr = len(in_specs), len(out_specs), len(scratch_shapes)
    r_in, r_out = len(ride.operands), len(ride.out_shape)
    a, b = n_in, n_in + r_in
    c, d = b + n_out, b + n_out + r_out
    e = d + n_scr

    def riding(*refs):
        start, finish = ride.make(refs[a:b], refs[c:d], refs[e:])
        ids = [pl.program_id(k) for k in range(len(grid))]
        first, last = ids[0] == 0, ids[0] == grid[0] - 1
        for k in range(1, len(grid)):
            first, last = first & (ids[k] == 0), last & (ids[k] == grid[k] - 1)
        pl.when(first)(start)
        body(*refs[:a], *refs[b:c], *refs[d:e])
        pl.when(last)(finish)

    outs = pl.pallas_call(
        riding, name=name, grid=grid,
        in_specs=list(in_specs) + [HBM_SPEC] * r_in, out_specs=list(out_specs) + [HBM_SPEC] * r_out,
        out_shape=list(out_shape) + list(ride.out_shape),
        input_output_aliases={n_in + i: n_out + o for i, o in ride.aliases.items()},
        scratch_shapes=list(scratch_shapes) + list(ride.sems),
        compiler_params=_params(*(("arbitrary",) * len(grid))),
    )(*args, *ride.operands)
    return outs[:n_out], outs[n_out:]


def alone(ride, *, name):
    def body(*refs):
        n = len(ride.operands)
        start, finish = ride.make(refs[:n], refs[n:n + len(ride.out_shape)], refs[n + len(ride.out_shape):])
        start()
        finish()

    return pl.pallas_call(
        body, name=name, in_specs=[HBM_SPEC] * len(ride.operands), out_specs=[HBM_SPEC] * len(ride.out_shape),
        out_shape=list(ride.out_shape), input_output_aliases=dict(ride.aliases), scratch_shapes=list(ride.sems),
    )(*ride.operands)


def gather_ride(wholes, metas, small=None):
    n = len(wholes)
    operands, out_shape = list(wholes), [SDS(s.shape, s.dtype) for s in wholes]
    sems = [DMA((n, 3)), DMA((n, 3)), DMA((n, 3)), DMA((n, 3))]
    if small is not None:
        operands.append(small)
        out_shape.append(SDS((N_CHIPS,) + small.shape, small.dtype))
        sems += [DMA((3,)), DMA((3,)), DMA(())]

    def make(ins, outs, sem):
        send1, recv1, send2, recv2 = sem[:4]
        x, y, c = _place()
        p = 2 * x + y
        chips = _other_chips(x, y)
        me, sibling = (x, y, c), (x, y, 1 - c)
        part = lambda t, q, half: _half_of_quarter(outs[t], *metas[t], q, half)
        first = []
        for j, (qx, qy) in enumerate(chips):
            if small is not None:
                first.append(_remote(ins[n], outs[n].at[p], sem[4].at[j], sem[5].at[j], (qx, qy, c)))
            for t in range(n):
                first.append(_remote(part(t, p, c), part(t, p, c), send1.at[t, j], recv1.at[t, j], (qx, qy, c)))
        local = [] if small is None else [pltpu.make_async_copy(ins[n], outs[n].at[p], sem[6])]

        def start():
            for cp in local + first:
                cp.start()

        def finish():
            passed = []
            for j, (qx, qy) in enumerate(chips):
                q = 2 * qx + qy
                for t in range(n):
                    landed = part(t, q, c)
                    _remote(landed, landed, send1.at[t, j], recv1.at[t, j], me).wait_recv()
                    cp = _remote(landed, landed, send2.at[t, j], recv2.at[t, j], sibling)
                    cp.start()
                    passed.append(cp)
            for j, (qx, qy) in enumerate(chips):
                q = 2 * qx + qy
                if small is not None:
                    _remote(outs[n].at[q], outs[n].at[q], sem[4].at[j], sem[5].at[j], me).wait_recv()
                for t in range(n):
                    theirs = part(t, q, 1 - c)
                    _remote(theirs, theirs, send2.at[t, j], recv2.at[t, j], me).wait_recv()
            for cp in first + passed:
                cp.wait_send()
            for cp in local:
                cp.wait()

        return start, finish

    return Ride(operands, out_shape, {t: t for t in range(n)}, sems, make)


def chip_ride(sums, metas, small=None):
    n = len(sums)
    operands = list(sums)
    out_shape = [SDS((3, s.shape[1], quarter[1]), s.dtype) for s, (_, quarter) in zip(sums, metas)]
    sems = [DMA((n, 3)), DMA((n, 3))] if n else []
    if small is not None:
        operands.append(small)
        out_shape.append(SDS((8,) + small.shape, small.dtype))
        sems += [DMA((7,)), DMA((7,)), DMA(())]

    def make(ins, outs, sem):
        x, y, c = _place()
        cps = []
        for j, (qx, qy) in enumerate(_other_chips(x, y)):
            q = 2 * qx + qy
            for t in range(n):
                kind, (_, ws) = metas[t]
                if kind == "row":
                    src = ins[t].at[q]
                elif kind == "col":
                    src = ins[t].at[0, :, pl.ds(pl.multiple_of(q * ws, 128), ws)]
                else:
                    src = ins[t].at[q // 2, :, pl.ds(pl.multiple_of((q % 2) * ws, 128), ws)]
                cps.append(_remote(src, outs[t].at[j], sem[0].at[t, j], sem[1].at[t, j], (qx, qy, c)))
        local = []
        if small is not None:
            ssend, srecv, lsem = sem[-3:]
            local.append(pltpu.make_async_copy(ins[n], outs[n].at[0], lsem))
            for k in range(1, 8):
                peer = (x ^ (k >> 2 & 1), y ^ (k >> 1 & 1), c ^ (k & 1))
                cps.append(_remote(ins[n], outs[n].at[k], ssend.at[k - 1], srecv.at[k - 1], peer))

        def start():
            for cp in local + cps:
                cp.start()

        def finish():
            for cp in cps + local:
                cp.wait()

        return start, finish

    return Ride(operands, out_shape, {}, sems, make)


def pair_exchange(grads, *, name):
    n = len(grads)

    def body(*refs):
        ins, outs = refs[:n], refs[n:2 * n]
        send, recv = refs[2 * n:]
        x, y, c = _place()
        cps = [_remote(ins[t].at[:, 1 - c], outs[t], send.at[t], recv.at[t], (x, y, 1 - c)) for t in range(n)]
        for cp in cps:
            cp.start()
        for cp in cps:
            cp.wait()

    return pl.pallas_call(
        body, name=name,
        in_specs=[HBM_SPEC] * n, out_specs=[HBM_SPEC] * n,
        out_shape=[SDS((g.shape[0],) + g.shape[2:], g.dtype) for g in grads],
        scratch_shapes=[DMA((n,)), DMA((n,))],
    )(*grads)


def half_exchange(quarters):
    n = len(quarters)

    def body(*refs):
        outs = refs[n:2 * n]
        send, recv = refs[2 * n:]
        x, y, c = _place()
        sends = [_remote(outs[t].at[c], outs[t].at[c], send.at[t], recv.at[t], (x, y, 1 - c)) for t in range(n)]
        for cp in sends:
            cp.start()
        for t in range(n):
            theirs = outs[t].at[1 - c]
            _remote(theirs, theirs, send.at[t], recv.at[t], (x, y, c)).wait_recv()
        for cp in sends:
            cp.wait_send()

    return pl.pallas_call(
        body, name="half_exchange",
        in_specs=[HBM_SPEC] * n, out_specs=[HBM_SPEC] * n,
        out_shape=[SDS(q.shape, q.dtype) for q in quarters],
        input_output_aliases={t: t for t in range(n)},
        scratch_shapes=[DMA((n,)), DMA((n,))],
    )(*quarters)


def cast_quarter(w, layer, kind, p_arr, *, name):
    _, r, ws = w.shape
    tr = _row_tile(r, 512)
    per = r // tr
    out_map = (lambda i, p_ref: (p_ref[0] * per + i, 0)) if kind == "row" else (lambda i, p_ref: (i, p_ref[0]))

    def body(p_ref, w_ref, o_ref):
        o_ref[...] = w_ref[...].astype(BF16)

    return pl.pallas_call(
        body, name=name,
        grid_spec=pltpu.PrefetchScalarGridSpec(
            num_scalar_prefetch=1, grid=(per,),
            in_specs=[pl.BlockSpec((None, tr, ws), lambda i, p_ref: (layer, i, 0))],
            out_specs=pl.BlockSpec((tr, ws), out_map)),
        out_shape=SDS(_full_shape(kind, (r, ws)), BF16),
        compiler_params=_params("parallel"),
    )(p_arr, w)


def pair_add(own, got, c_arr, *, name):
    A, _, h, W = own.shape
    th = _row_tile(h, max(BF16_ROWS, (3 << 19) // W), BF16_ROWS)

    def body(c_ref, a_ref, b_ref, o_ref):
        o_ref[...] = (a_ref[...].astype(F32) + b_ref[...].astype(F32)).astype(BF16)

    return pl.pallas_call(
        body, name=name,
        grid_spec=pltpu.PrefetchScalarGridSpec(
            num_scalar_prefetch=1, grid=(A, h // th),
            in_specs=[pl.BlockSpec((None, None, th, W), lambda q, i, c_ref: (q, c_ref[0], i, 0)),
                      pl.BlockSpec((None, th, W), lambda q, i, c_ref: (q, i, 0))],
            out_specs=pl.BlockSpec((None, th, W), lambda q, i, c_ref: (q, i, 0))),
        out_shape=SDS((A, h, W), BF16),
        compiler_params=_params("parallel", "parallel"),
    )(c_arr, own, got)


def chip_reduce(sums, got, kind, pc_arr, *, name):
    _, h, ws = got.shape
    th = h // 2
    mine = {"row": lambda i, pc_ref: (pc_ref[0], i, 0), "col": lambda i, pc_ref: (0, i, pc_ref[0]),
            "split": lambda i, pc_ref: (pc_ref[0] // 2, i, pc_ref[0] % 2)}[kind]

    def body(pc_ref, a_ref, b_ref, o_ref):
        o_ref[...] = ((a_ref[...].astype(F32) + b_ref[0].astype(F32)) + b_ref[1].astype(F32)) + b_ref[2].astype(F32)

    return pl.pallas_call(
        body, name=name,
        grid_spec=pltpu.PrefetchScalarGridSpec(
            num_scalar_prefetch=1, grid=(h // th,),
            in_specs=[pl.BlockSpec((None, th, ws), mine),
                      pl.BlockSpec((3, th, ws), lambda i, pc_ref: (0, i, 0))],
            out_specs=pl.BlockSpec((None, th, ws), lambda i, pc_ref: (pc_ref[1], i, 0))),
        out_shape=SDS((2, h, ws), F32),
        compiler_params=_params("parallel"),
    )(pc_arr, sums, got)


def small_reduce(blocks, me_arr):
    _, rows, D = blocks.shape

    def body(me_ref, b_ref, o_ref):
        me = me_ref[0]
        total = b_ref[me]
        for d in range(1, 8):
            total = total + b_ref[d ^ me]
        o_ref[...] = total

    return pl.pallas_call(
        body, name="small_reduce",
        grid_spec=pltpu.PrefetchScalarGridSpec(
            num_scalar_prefetch=1, grid=(1,),
            in_specs=[pl.BlockSpec((8, rows, D), lambda i, me_ref: (0, 0, 0))],
            out_specs=pl.BlockSpec((rows, D), lambda i, me_ref: (0, 0))),
        out_shape=SDS((rows, D), F32),
        compiler_params=_params("arbitrary"),
    )(me_arr, blocks)


def adamw(w, gs, m, v, *, name):
    L, r, cols = w.shape
    tr = _row_tile(r, 256)
    nt = r // tr

    def body(*refs):
        w_ref, m_ref, v_ref = refs[:3]
        g_refs = refs[3:3 + L]
        g_out, d_out, m_out, v_out = refs[3 + L:]
        layer = pl.program_id(0)
        g = g_refs[0][...]
        for l in range(1, L):
            g = jnp.where(layer == l, g_refs[l][...], g)
        m_new = ADAM_B1 * m_ref[...] + (1.0 - ADAM_B1) * g
        v_new = ADAM_B2 * v_ref[...] + (1.0 - ADAM_B2) * (g * g)
        m_hat = m_new / (1.0 - ADAM_B1 ** ADAM_STEP)
        v_hat = v_new / (1.0 - ADAM_B2 ** ADAM_STEP)
        g_out[...] = g
        m_out[...] = m_new
        v_out[...] = v_new
        d_out[...] = -ADAM_LR * (m_hat / (jnp.sqrt(v_hat) + ADAM_EPS) + ADAM_WD * w_ref[...])

    full = pl.BlockSpec((None, tr, cols), lambda l, i: (l, i, 0))
    g_spec = lambda l0: pl.BlockSpec((tr, cols), lambda l, i: (jnp.where(l == l0, i, jnp.where(l < l0, 0, nt - 1)), 0))
    return pl.pallas_call(
        body, name=name, grid=(L, nt),
        in_specs=[full, full, full] + [g_spec(l0) for l0 in range(L)],
        out_specs=[full] * 4,
        out_shape=[SDS(w.shape, F32)] * 4,
        compiler_params=_params("arbitrary", "arbitrary"),
    )(w, m, v, *gs)


def _rms_r(xf):
    return lax.rsqrt(jnp.mean(xf * xf, axis=-1, keepdims=True) + EPS)


def _rmsnorm_bwd(xf, g, dy):
    r = _rms_r(xf)
    xh = xf * r
    gd = g * dy
    return r * (gd - xh * jnp.mean(xh * gd, axis=-1, keepdims=True)), xh


def _dot(a, b):
    return jnp.dot(a, b, preferred_element_type=F32)


def _dot_nt(a, b):
    return lax.dot_general(a, b, (((1,), (1,)), ((), ())), preferred_element_type=F32)


def _dot_tn(a, b):
    return lax.dot_general(a, b, (((0,), (0,)), ((), ())), preferred_element_type=F32)


def _accumulate(ref, first, value):
    @pl.when(first)
    def _():
        ref[...] = value

    @pl.when(jnp.logical_not(first))
    def _():
        ref[...] += value


def norm_matmul(x, g, w, *, tn, split, name, ride=None, tm=ROW_TILE):
    T, D = x.shape
    N = w.shape[1]
    per = N // split // tn

    def body(x_ref, g_ref, w_ref, o_ref, xn_ref):
        @pl.when(pl.program_id(1) == 0)
        def _():
            xf = x_ref[...]
            xn_ref[...] = (xf * _rms_r(xf) * g_ref[...]).astype(BF16)

        o_ref[...] = _dot(xn_ref[...], w_ref[...]).astype(BF16)

    return _call(
        body, name=name, grid=(T // tm, N // tn),
        in_specs=[pl.BlockSpec((tm, D), lambda i, j: (i, 0)),
                  pl.BlockSpec((1, D), lambda i, j: (0, 0)),
                  pl.BlockSpec((D, tn), lambda i, j: (0, j))],
        out_specs=[pl.BlockSpec((None, tm, tn), lambda i, j: (j // per, i, j % per)),
                   pl.BlockSpec((tm, D), lambda i, j: (i, 0))],
        out_shape=[SDS((split, T, N // split), BF16), SDS((T, D), BF16)],
        semantics=("parallel", "arbitrary"), args=(x, g, w), ride=ride)


def _shift_down(prev, cur, by):
    big = jnp.concatenate([prev, cur], axis=0)
    return pltpu.roll(big, by, 0)[prev.shape[0]:]


def _shift_up(cur, nxt, by):
    big = jnp.concatenate([cur, nxt], axis=0)
    return pltpu.roll(big, big.shape[0] - by, 0)[:cur.shape[0]]


def conv_mix_out(bcx, conv_w, w_out, g_post, res, *, name, ride=None, tm=ROW_TILE):
    T, D = res.shape
    hb = tm // BF16_ROWS

    def body(b_ref, c_ref, u_ref, cp_ref, up_ref, cw_ref, w_ref, g_ref, r_ref, h_ref, z_ref, y_ref):
        i = pl.program_id(0)
        cu = c_ref[...].astype(F32) * u_ref[...].astype(F32)
        cup = cp_ref[...].astype(F32) * up_ref[...].astype(F32)
        cup = jnp.where(i == 0, 0.0, cup)
        cv = (cw_ref[0:1, :] * _shift_down(cup, cu, 2) + cw_ref[1:2, :] * _shift_down(cup, cu, 1)
              + cw_ref[2:3, :] * cu)
        y = (b_ref[...].astype(F32) * cv).astype(BF16)
        y_ref[...] = y
        z = _dot(y, w_ref[...])
        z_ref[...] = z.astype(BF16)
        h_ref[...] = r_ref[...] + z * _rms_r(z) * g_ref[...]

    tile = lambda col: pl.BlockSpec((tm, D), lambda i: (i, col))
    halo = lambda col: pl.BlockSpec((BF16_ROWS, D), lambda i: (jnp.maximum(i * hb - 1, 0), col))
    row = pl.BlockSpec((tm, D), lambda i: (i, 0))
    return _call(
        body, name=name, grid=(T // tm,),
        in_specs=[tile(0), tile(1), tile(2), halo(1), halo(2),
                  pl.BlockSpec((3, D), lambda i: (0, 0)),
                  pl.BlockSpec((D, D), lambda i: (0, 0)),
                  pl.BlockSpec((1, D), lambda i: (0, 0)), row],
        out_specs=[row, row, row],
        out_shape=[SDS((T, D), F32), SDS((T, D), BF16), SDS((T, D), BF16)],
        semantics=("parallel",), args=(bcx, bcx, bcx, bcx, bcx, conv_w, w_out, g_post, res), ride=ride)


def plain_mix_out(a, w, g_post, res, *, name, target=None, ride=None, tm=ROW_TILE):
    T, D = res.shape
    K = a.shape[1]
    with_loss = target is not None

    def body(a_ref, w_ref, g_ref, r_ref, *rest):
        z = _dot(a_ref[...], w_ref[...])
        h = r_ref[...] + z * _rms_r(z) * g_ref[...]
        if with_loss:
            t_ref, h_ref, z_ref, loss_ref = rest
            diff = h - t_ref[...]
            h_ref[...] = diff * (1.0 / D)
            part = jnp.full(loss_ref.shape, 0.5 / D, F32) * jnp.sum(diff * diff)
            _accumulate(loss_ref, pl.program_id(0) == 0, part)
        else:
            h_ref, z_ref = rest
            h_ref[...] = h
        z_ref[...] = z.astype(BF16)

    row = pl.BlockSpec((tm, D), lambda i: (i, 0))
    loss_spec, loss_shape = pl.BlockSpec((8, 128), lambda i: (0, 0)), SDS((8, 128), F32)
    return _call(
        body, name=name, grid=(T // tm,),
        in_specs=[pl.BlockSpec((tm, K), lambda i: (i, 0)),
                  pl.BlockSpec((K, D), lambda i: (0, 0)),
                  pl.BlockSpec((1, D), lambda i: (0, 0)), row] + [row] * with_loss,
        out_specs=[row, row] + [loss_spec] * with_loss,
        out_shape=[SDS((T, D), F32), SDS((T, D), BF16)] + [loss_shape] * with_loss,
        semantics=("arbitrary",), args=(a, w, g_post, res) + ((target,) if with_loss else ()), ride=ride)


def _silu_grads(d, g, u):
    sg = jax.nn.sigmoid(g)
    return d * u * (sg * (1.0 + g * (1.0 - sg))), d * (g * sg)


def norm_swiglu_in(x, g, w, *, name, ride=None, tm=ROW_TILE // 2):
    T, D = x.shape
    F = w.shape[1] // 2

    def body(x_ref, g_ref, wg_ref, wu_ref, gu_ref, a_ref, xt_ref):
        xf = x_ref[...]
        xn = xf * _rms_r(xf) * g_ref[...]
        xt_ref[...] = xn.T.astype(BF16)
        xb = xn.astype(BF16)
        gate = _dot(xb, wg_ref[...]).astype(BF16)
        up = _dot(xb, wu_ref[...]).astype(BF16)
        gu_ref[0] = gate
        gu_ref[1] = up
        a_ref[...] = gate * jax.nn.sigmoid(gate) * up

    half = lambda s: pl.BlockSpec((D, F), lambda i: (0, s), pipeline_mode=pl.Buffered(1))
    return _call(
        body, name=name, grid=(T // tm,),
        in_specs=[pl.BlockSpec((tm, D), lambda i: (i, 0)), pl.BlockSpec((1, D), lambda i: (0, 0)), half(0), half(1)],
        out_specs=[pl.BlockSpec((2, tm, F), lambda i: (0, i, 0)), pl.BlockSpec((tm, F), lambda i: (i, 0)),
                   pl.BlockSpec((D, tm), lambda i: (0, i))],
        out_shape=[SDS((2, T, F), BF16), SDS((T, F), BF16), SDS((D, T), BF16)],
        semantics=("parallel",), args=(x, g, w, w), ride=ride)


def swiglu_bwd_tn(xt, dact, gu, *, name, ride=None, tb=MXU_WIDTH):
    D, T = xt.shape
    F = dact.shape[1]

    def body(xt_ref, d_ref, g_ref, u_ref, o_ref):
        dg, du = _silu_grads(d_ref[...], g_ref[...], u_ref[...])
        o_ref[0] = _dot(xt_ref[...], dg).astype(BF16)
        o_ref[1] = _dot(xt_ref[...], du).astype(BF16)

    col = lambda s: pl.BlockSpec((None, T, tb), lambda j: (s, 0, j))
    out = _call(
        body, name=name, grid=(F // tb,),
        in_specs=[pl.BlockSpec((D, T), lambda j: (0, 0), pipeline_mode=pl.Buffered(1)),
                  pl.BlockSpec((T, tb), lambda j: (0, j)), col(0), col(1)],
        out_specs=[pl.BlockSpec((2, D, tb), lambda j: (0, 0, j))],
        out_shape=[SDS((2, D, F), BF16)],
        semantics=("parallel",), args=(xt, dact, gu, gu), ride=ride)
    return out[0] if ride is None else (out[0][0], out[1])


def swiglu_bwd_in(dact, gu, w, h_in, g, dh_out, *, name, ride=None, tm=ROW_TILE // 2):
    T, D = h_in.shape
    F = dact.shape[1]

    def body(d_ref, gg_ref, uu_ref, wg_ref, wu_ref, h_ref, g_ref, dh_ref, o_ref, dg_ref):
        dgate, dup = _silu_grads(d_ref[...], gg_ref[...], uu_ref[...])
        dn = _dot_nt(dgate, wg_ref[...]) + _dot_nt(dup, wu_ref[...])
        dx, hh = _rmsnorm_bwd(h_ref[...], g_ref[...], dn)
        o_ref[...] = dh_ref[...] + dx
        _accumulate(dg_ref, pl.program_id(0) == 0, jnp.sum(dn * hh, axis=0, keepdims=True))

    row = pl.BlockSpec((tm, D), lambda i: (i, 0))
    vec = pl.BlockSpec((1, D), lambda i: (0, 0))
    part = lambda s: pl.BlockSpec((None, tm, F), lambda i: (s, i, 0))
    half = lambda s: pl.BlockSpec((D, F), lambda i: (0, s), pipeline_mode=pl.Buffered(1))
    return _call(
        body, name=name, grid=(T // tm,),
        in_specs=[pl.BlockSpec((tm, F), lambda i: (i, 0)), part(0), part(1), half(0), half(1), row, vec, row],
        out_specs=[row, vec],
        out_shape=[SDS((T, D), F32), SDS((1, D), F32)],
        semantics=("arbitrary",), args=(dact, gu, gu, w, w, h_in, g, dh_out), ride=ride)


def rope_tables(T):
    half = ROT_DIM // 2
    inv_freq = ROPE_THETA ** (-jnp.arange(0, ROT_DIM, 2, dtype=F32) / ROT_DIM)
    ang = (jnp.arange(T, dtype=F32)[:, None] * inv_freq[None, :]).T
    cos, sin = jnp.cos(ang), jnp.sin(ang)
    rest = HEAD_DIM - ROT_DIM
    one, zero = jnp.ones((rest, T), F32), jnp.zeros((rest, T), F32)
    zh = jnp.zeros((half, T), F32)
    fac = jnp.concatenate([cos, cos, one], axis=0)
    up = jnp.concatenate([-sin, zh, zero], axis=0)
    down = jnp.concatenate([zh, sin, zero], axis=0)
    return jnp.stack([fac, up, down])


def _rope(t, tab):
    half = ROT_DIM // 2
    return t * tab[0] + pltpu.roll(t, HEAD_DIM - half, 0) * tab[1] + pltpu.roll(t, half, 0) * tab[2]


def _rope_t(d, tab):
    half = ROT_DIM // 2
    return d * tab[0] + pltpu.roll(d * tab[1], half, 0) + pltpu.roll(d * tab[2], HEAD_DIM - half, 0)


def _head(t, h):
    return t[h * HEAD_DIM:(h + 1) * HEAD_DIM]


def _band(n, group):
    kj = lax.broadcasted_iota(jnp.int32, (2 * BLOCK, BLOCK), 0)
    qi = lax.broadcasted_iota(jnp.int32, (2 * BLOCK, BLOCK), 1)
    mask = (kj > qi) & (kj <= qi + BLOCK) & ((n > 0) | (kj >= BLOCK))
    return jnp.tile(mask, (1, group))


def _attn_specs(D, kvd):
    prev = lambda n: jnp.maximum(n - 1, 0)
    return [pl.BlockSpec((BLOCK, D), lambda n: (n, 0)),
            pl.BlockSpec((BLOCK, kvd), lambda n: (prev(n), 0)),
            pl.BlockSpec((BLOCK, kvd), lambda n: (n, 0)),
            pl.BlockSpec((BLOCK, kvd), lambda n: (prev(n), 1)),
            pl.BlockSpec((BLOCK, kvd), lambda n: (n, 1)),
            pl.BlockSpec((3, HEAD_DIM, BLOCK), lambda n: (0, 0, prev(n))),
            pl.BlockSpec((3, HEAD_DIM, BLOCK), lambda n: (0, 0, n)),
            pl.BlockSpec(memory_space=pltpu.SMEM)]


def _attn_operands(q_ref, kp_ref, k_ref, vp_ref, v_ref, tp_ref, t_ref):
    flip = lambda ref: ref[...].astype(F32).T
    tab = t_ref[...]
    kt = jnp.concatenate([flip(kp_ref), flip(k_ref)], axis=1)
    vt = jnp.concatenate([flip(vp_ref), flip(v_ref)], axis=1)
    return flip(q_ref), kt, vt, tab, jnp.concatenate([tp_ref[...], tab], axis=2)


SCORE_SCALE = 1.0 / math.sqrt(HEAD_DIM)
HEADS_TOGETHER = 4


def _group_heads(t, first, count, tab=None):
    heads = [_head(t, first + g) for g in range(count)]
    if tab is not None:
        heads = [_rope(h, tab) * SCORE_SCALE for h in heads]
    return jnp.concatenate(heads, axis=1).astype(BF16)


def _sink_row(s_ref, first, count):
    which = lax.broadcasted_iota(jnp.int32, (1, count * BLOCK), 1) // BLOCK
    row = jnp.zeros((1, count * BLOCK), F32)
    for g in range(count):
        row = jnp.where(which == g, s_ref[0, first + g], row)
    return row


def _softmax_block(k_j, q_j, sink, mask):
    return _softmax(_dot_tn(k_j, q_j), sink, mask)


def _softmax(scores, sink, mask):
    s = jnp.where(mask, scores, NEG)
    m = jnp.maximum(jnp.max(s, axis=0, keepdims=True), sink)
    e = jnp.exp(s - m)
    es = jnp.exp(sink - m)
    inv = 1.0 / (jnp.sum(e, axis=0, keepdims=True) + es)
    return e * inv, es * inv


def attention_fwd(q, kv, tabs, sinks, *, name, ride=None):
    T, D = q.shape
    kvd = kv.shape[1] // 2
    group = D // HEAD_DIM // N_KV_HEADS

    def body(q_ref, kp_ref, k_ref, vp_ref, v_ref, tp_ref, t_ref, s_ref, o_ref):
        gs = HEADS_TOGETHER
        mask = _band(pl.program_id(0), gs)
        qt, kt, vt, tab, tab2 = _attn_operands(q_ref, kp_ref, k_ref, vp_ref, v_ref, tp_ref, t_ref)
        firsts = [(j, first) for j in range(N_KV_HEADS) for first in range(j * group, (j + 1) * group, gs)]
        ks = [_rope(_head(kt, j), tab2).astype(BF16) for j in range(N_KV_HEADS)]
        scores = [_dot_tn(ks[j], _group_heads(qt, first, gs, tab)) for j, first in firsts]
        probs = [_softmax(s, _sink_row(s_ref, first, gs), mask)[0].astype(BF16)
                 for s, (j, first) in zip(scores, firsts)]
        outs = []
        for p, (j, first) in zip(probs, firsts):
            o = _dot(_head(vt, j).astype(BF16), p)
            outs += [o[:, g * BLOCK:(g + 1) * BLOCK] for g in range(gs)]
        o_ref[...] = jnp.concatenate(outs, axis=0).T.astype(BF16)

    return _call(
        body, name=name, grid=(T // BLOCK,),
        in_specs=_attn_specs(D, kvd),
        out_specs=[pl.BlockSpec((BLOCK, D), lambda n: (n, 0))],
        out_shape=[SDS((T, D), BF16)],
        semantics=("parallel",), args=(q, kv, kv, kv, kv, tabs, tabs, sinks), ride=ride)


def attention_bwd(q, kv, tabs, sinks, do, *, name, ride=None):
    T, D = q.shape
    kvd = kv.shape[1] // 2
    heads = D // HEAD_DIM
    group = heads // N_KV_HEADS

    def body(q_ref, kp_ref, k_ref, vp_ref, v_ref, tp_ref, t_ref, s_ref, do_ref, dq_ref, dc_ref, dp_ref, ds_ref):
        n = pl.program_id(0)
        gs = HEADS_TOGETHER
        mask = _band(n, gs)
        qt, kt, vt, tab, tab2 = _attn_operands(q_ref, kp_ref, k_ref, vp_ref, v_ref, tp_ref, t_ref)
        dot = do_ref[...].astype(F32).T
        lane = lax.broadcasted_iota(jnp.int32, (8, 128), 1)
        dsink = jnp.zeros((8, 128), F32)
        firsts = [(j, first) for j in range(N_KV_HEADS) for first in range(j * group, (j + 1) * group, gs)]
        ks = [_rope(_head(kt, j), tab2).astype(BF16) for j in range(N_KV_HEADS)]
        vs = [_head(vt, j).astype(BF16) for j in range(N_KV_HEADS)]
        qs = [_group_heads(qt, first, gs, tab) for _, first in firsts]
        dos = [_group_heads(dot, first, gs) for _, first in firsts]
        scores = [_dot_tn(ks[j], q) for q, (j, _) in zip(qs, firsts)]
        dps = [_dot_tn(vs[j], do) for do, (j, _) in zip(dos, firsts)]
        ps, dscs = [], []
        for s, dp, (j, first) in zip(scores, dps, firsts):
            p, p_sink = _softmax(s, _sink_row(s_ref, first, gs), mask)
            dl = jnp.sum(p * dp, axis=0, keepdims=True)
            dscs.append((p * (dp - dl)).astype(BF16))
            ps.append(p.astype(BF16))
            weight = p_sink * dl
            for g in range(gs):
                dsink = dsink - jnp.where(lane == first + g, jnp.sum(weight[:, g * BLOCK:(g + 1) * BLOCK]), 0.0)
        dqs = []
        dks = [jnp.zeros((HEAD_DIM, 2 * BLOCK), F32) for _ in range(N_KV_HEADS)]
        dvs = [jnp.zeros((HEAD_DIM, 2 * BLOCK), F32) for _ in range(N_KV_HEADS)]
        for p, dsc, q, do, (j, _) in zip(ps, dscs, qs, dos, firsts):
            dq = _dot(ks[j], dsc) * SCORE_SCALE
            dqs += [_rope_t(dq[:, g * BLOCK:(g + 1) * BLOCK], tab) for g in range(gs)]
            dks[j] = dks[j] + _dot_nt(q, dsc)
            dvs[j] = dvs[j] + _dot_nt(do, p)
        dks = [_rope_t(dk, tab2) for dk in dks]
        dq_ref[...] = jnp.concatenate(dqs, axis=0).T.astype(BF16)
        dkv = jnp.concatenate(dks + dvs, axis=0)
        dp_ref[...] = dkv[:, :BLOCK].T
        dc_ref[...] = dkv[:, BLOCK:].T
        _accumulate(ds_ref, n == 0, dsink)

    blk = lambda w: pl.BlockSpec((BLOCK, w), lambda n: (n, 0))
    return _call(
        body, name=name, grid=(T // BLOCK,),
        in_specs=_attn_specs(D, kvd) + [blk(D)],
        out_specs=[blk(D), blk(2 * kvd), blk(2 * kvd), pl.BlockSpec((8, 128), lambda n: (0, 0))],
        out_shape=[SDS((T, D), BF16), SDS((T, 2 * kvd), F32), SDS((T, 2 * kvd), F32), SDS((8, 128), F32)],
        semantics=("arbitrary",), args=(q, kv, kv, kv, kv, tabs, tabs, sinks, do), ride=ride)


def combine_dkv(d_cur, d_prev, *, name):
    T, W = d_cur.shape
    tm = ROW_TILE
    nt, per, last = T // tm, tm // BLOCK, T // BLOCK - 1

    def body(c_ref, p_ref, pn_ref, o_ref):
        nxt = jnp.where(pl.program_id(0) == nt - 1, 0.0, pn_ref[...])
        o_ref[...] = (c_ref[...] + jnp.concatenate([p_ref[BLOCK:, :], nxt], axis=0)).astype(BF16)

    return _call(
        body, name=name, grid=(nt,),
        in_specs=[pl.BlockSpec((tm, W), lambda i: (i, 0)), pl.BlockSpec((tm, W), lambda i: (i, 0)),
                  pl.BlockSpec((BLOCK, W), lambda i: (jnp.minimum((i + 1) * per, last), 0))],
        out_specs=[pl.BlockSpec((tm, W), lambda i: (i, 0))],
        out_shape=[SDS((T, W), BF16)],
        semantics=("parallel",), args=(d_cur, d_prev, d_prev))[0]


def normbwd_matmul_nt(z, g, dh, w, *, name, ride=None, tm=ROW_TILE):
    T, D = z.shape
    K = w.shape[0]

    def body(z_ref, g_ref, dh_ref, w_ref, dz_ref, dg_ref, o_ref):
        dh_ = dh_ref[...]
        dz, zh = _rmsnorm_bwd(z_ref[...].astype(F32), g_ref[...], dh_)
        dz = dz.astype(BF16)
        dz_ref[...] = dz
        _accumulate(dg_ref, pl.program_id(0) == 0, jnp.sum(dh_ * zh, axis=0, keepdims=True))
        o_ref[...] = _dot_nt(dz, w_ref[...]).astype(BF16)

    row = pl.BlockSpec((tm, D), lambda i: (i, 0))
    vec = pl.BlockSpec((1, D), lambda i: (0, 0))
    return _call(
        body, name=name, grid=(T // tm,),
        in_specs=[row, vec, row, pl.BlockSpec((K, D), lambda i: (0, 0))],
        out_specs=[row, vec, pl.BlockSpec((tm, K), lambda i: (i, 0))],
        out_shape=[SDS((T, D), BF16), SDS((1, D), F32), SDS((T, K), BF16)],
        semantics=("arbitrary",), args=(z, g, dh, w), ride=ride)


def matmul_nt_normbwd(da, w, h_in, g, dh_out, *, name, ride=None, tm=ROW_TILE):
    T, D = h_in.shape
    S, _, K = da.shape

    def body(*refs):
        da_refs, w_refs = refs[:S], refs[S:2 * S]
        h_ref, g_ref, dh_ref, o_ref, dg_ref = refs[2 * S:]
        dn = _dot_nt(da_refs[0][...], w_refs[0][...])
        for s in range(1, S):
            dn = dn + _dot_nt(da_refs[s][...], w_refs[s][...])
        dx, hh = _rmsnorm_bwd(h_ref[...], g_ref[...], dn)
        o_ref[...] = dh_ref[...] + dx
        _accumulate(dg_ref, pl.program_id(0) == 0, jnp.sum(dn * hh, axis=0, keepdims=True))

    row = pl.BlockSpec((tm, D), lambda i: (i, 0))
    vec = pl.BlockSpec((1, D), lambda i: (0, 0))
    part = lambda s: pl.BlockSpec((None, tm, K), lambda i: (s, i, 0))
    cols = lambda s: pl.BlockSpec((D, K), lambda i: (0, s), pipeline_mode=pl.Buffered(1))
    return _call(
        body, name=name, grid=(T // tm,),
        in_specs=[part(s) for s in range(S)] + [cols(s) for s in range(S)] + [row, vec, row],
        out_specs=[row, vec],
        out_shape=[SDS((T, D), F32), SDS((1, D), F32)],
        semantics=("arbitrary",), args=[da] * S + [w] * S + [h_in, g, dh_out], ride=ride)


def matmul_tn(a, b, *, tb, name, ride=None, ta=MXU_WIDTH):
    T, Ka = a.shape
    S, _, Nb = b.shape
    per = Nb // tb

    def body(a_ref, b_ref, o_ref):
        o_ref[...] = _dot_tn(a_ref[...], b_ref[...]).astype(BF16)

    out = _call(
        body, name=name, grid=(S * per, Ka // ta),
        in_specs=[pl.BlockSpec((T, ta), lambda j, i: (0, i)),
                  pl.BlockSpec((None, T, tb), lambda j, i: (j // per, 0, j % per))],
        out_specs=[pl.BlockSpec((ta, tb), lambda j, i: (i, j))],
        out_shape=[SDS((Ka, S * Nb), BF16)],
        semantics=("parallel", "parallel"), args=(a, b), ride=ride)
    return out[0] if ride is None else (out[0][0], out[1])


def conv_bwd(dy, bcx, conv_w, *, name, tm=ROW_TILE):
    T, D = dy.shape
    nt = T // tm
    hb = tm // BF16_ROWS
    last = T // BF16_ROWS - 1

    def body(dy_ref, dyn_ref, b_ref, bn_ref, c_ref, u_ref, cp_ref, up_ref, cw_ref, o_ref, dw_ref):
        i = pl.program_id(0)
        c, u = c_ref[...].astype(F32), u_ref[...].astype(F32)
        cu = c * u
        cup = jnp.where(i == 0, 0.0, cp_ref[...].astype(F32) * up_ref[...].astype(F32))
        cu1, cu2 = _shift_down(cup, cu, 1), _shift_down(cup, cu, 2)
        w0, w1, w2 = cw_ref[0:1, :], cw_ref[1:2, :], cw_ref[2:3, :]
        dyf = dy_ref[...].astype(F32)
        o_ref[:, 0:D] = (dyf * (w0 * cu2 + w1 * cu1 + w2 * cu)).astype(BF16)
        dcv = dyf * b_ref[...].astype(F32)
        dcvn = jnp.where(i == nt - 1, 0.0, dyn_ref[...].astype(F32) * bn_ref[...].astype(F32))
        dcu = w2 * dcv + w1 * _shift_up(dcv, dcvn, 1) + w0 * _shift_up(dcv, dcvn, 2)
        o_ref[:, D:2 * D] = (dcu * u).astype(BF16)
        o_ref[:, 2 * D:3 * D] = (dcu * c).astype(BF16)
        row = lax.broadcasted_iota(jnp.int32, (8, D), 0)
        dw = jnp.zeros((8, D), F32)
        for tap, t in enumerate((cu2, cu1, cu)):
            dw = jnp.where(row == tap, jnp.sum(dcv * t, axis=0, keepdims=True), dw)
        _accumulate(dw_ref, i == 0, dw)

    tile = lambda col: pl.BlockSpec((tm, D), lambda i: (i, col))
    prev = lambda col: pl.BlockSpec((BF16_ROWS, D), lambda i: (jnp.maximum(i * hb - 1, 0), col))
    nxt = lambda col: pl.BlockSpec((BF16_ROWS, D), lambda i: (jnp.minimum((i + 1) * hb, last), col))
    return _call(
        body, name=name, grid=(nt,),
        in_specs=[tile(0), nxt(0), tile(0), nxt(0), tile(1), tile(2), prev(1), prev(2),
                  pl.BlockSpec((3, D), lambda i: (0, 0))],
        out_specs=[pl.BlockSpec((tm, 3 * D), lambda i: (i, 0)), pl.BlockSpec((8, D), lambda i: (0, 0))],
        out_shape=[SDS((T, 3 * D), BF16), SDS((8, D), F32)],
        semantics=("arbitrary",), args=(dy, dy, bcx, bcx, bcx, bcx, bcx, bcx, conv_w))


class NoTraffic:
    def ride(self, kernel_name):
        return None

    def landed(self, kernel_name, results, wts):
        pass

    def grad(self, key, value):
        pass


def local_step(x, target, wts, vec, traffic):
    T, D = x.shape
    tabs = rope_tables(T)
    small = {}

    def run(builder, *args, name, **kw):
        ride = traffic.ride(name)
        if ride is None:
            return builder(*args, name=name, **kw)
        out, extra = builder(*args, name=name, ride=ride, **kw)
        traffic.landed(name, extra, wts)
        return out

    bcx, xn1 = run(norm_matmul, x, vec["a_pre"], wts["w_in"], tn=3 * D, split=1, name="a_in")
    bcx = bcx[0]
    h1, z0, y0 = run(conv_mix_out, bcx, vec["conv_w"], wts["w_out"], vec["a_post"], x, name="a_out")
    gu0, act0, xt2 = run(norm_swiglu_in, h1, vec["ffn_pre0"], wts["gu0"], name="ffn0_in")
    h2, z1 = run(plain_mix_out, act0, wts["wd0"], vec["ffn_post0"], h1, name="ffn0_out")
    kvp, xkv = norm_matmul(h2, vec["kv_norm"], wts["w_kv"], tn=wts["w_kv"].shape[1], split=1, name="kv_in")
    qp, xq = norm_matmul(h2, vec["b_pre"], wts["w_q"], tn=D, split=1, name="q_in")
    kvp, qp = kvp[0], qp[0]
    (attn,) = run(attention_fwd, qp, kvp, tabs, vec["sinks"], name="attn_fwd")
    h3, z2 = plain_mix_out(attn, wts["w_o"], vec["b_post"], h2, name="attn_out")
    gu1, act1, xt3 = run(norm_swiglu_in, h3, vec["ffn_pre1"], wts["gu1"], name="ffn1_in")
    dy, z3, loss = plain_mix_out(act1, wts["wd1"], vec["ffn_post1"], h3, name="ffn1_out", target=target)

    def ffn_bwd(layer, z, gu, act, xt, h_in, dh):
        tag = "ffn%d" % layer
        dz, small["ffn_post%d" % layer], dact = run(
            normbwd_matmul_nt, z, vec["ffn_post%d" % layer], dh, wts["wd%d" % layer], name=tag + "_out_bwd")
        traffic.grad("wd%d" % layer, matmul_tn(act, dz[None], tb=D, name=tag + "_dwd"))
        traffic.grad("gu%d" % layer, run(swiglu_bwd_tn, xt, dact, gu, name=tag + "_dwgu"))
        dh_in, small["ffn_pre%d" % layer] = run(
            swiglu_bwd_in, dact, gu, wts["gu%d" % layer], h_in, vec["ffn_pre%d" % layer], dh, name=tag + "_in_bwd")
        return dh_in

    dh3 = ffn_bwd(1, z3, gu1, act1, xt3, h3, dy)
    dz2, small["b_post"], dattn = normbwd_matmul_nt(z2, vec["b_post"], dh3, wts["w_o"], name="attn_out_bwd")
    traffic.grad("w_o", matmul_tn(attn, dz2[None], tb=D, name="attn_dwo"))
    dq, dkv_cur, dkv_prev, small["sinks"] = attention_bwd(qp, kvp, tabs, vec["sinks"], dattn, name="attn_bwd")
    dkv = combine_dkv(dkv_cur, dkv_prev, name="attn_dkv")
    traffic.grad("w_q", matmul_tn(xq, dq[None], tb=D, name="attn_dwq"))
    traffic.grad("w_kv", matmul_tn(xkv, dkv[None], tb=dkv.shape[1], name="attn_dwkv"))
    dh2, small["b_pre"] = matmul_nt_normbwd(dq[None], wts["w_q"], h2, vec["b_pre"], dh3, name="q_in_bwd")
    dh2, small["kv_norm"] = matmul_nt_normbwd(dkv[None], wts["w_kv"], h2, vec["kv_norm"], dh2, name="kv_in_bwd")
    dh1 = ffn_bwd(0, z1, gu0, act0, xt2, h1, dh2)
    dz0, small["a_post"], dyc = normbwd_matmul_nt(z0, vec["a_post"], dh1, wts["w_out"], name="a_out_bwd")
    traffic.grad("w_out", matmul_tn(y0, dz0[None], tb=D, name="a_dwout"))
    dbcx, small["conv_w"] = conv_bwd(dyc, bcx, vec["conv_w"], name="a_conv_bwd")
    traffic.grad("w_in", matmul_tn(xn1, dbcx[None], tb=3 * D // 2, name="a_dwin"))
    dx, small["a_pre"] = run(matmul_nt_normbwd, dbcx[None], wts["w_in"], x, vec["a_pre"], dh1, name="a_in_bwd")
    return loss, dx, small


SMALL_ROWS = 16
LOSS_ROW = 13

WHOLE = (0, 16)
GATHER_PLAN = {"gather_first": [("w_in", WHOLE)],
               "a_in": [("w_out", WHOLE), ("gu0", (0, 11))],
               "a_out": [("gu0", (11, 5))],
               "ffn0_in": [("wd0", WHOLE), ("w_kv", WHOLE), ("w_q", WHOLE), ("w_o", WHOLE)],
               "ffn0_out": [("gu1", (0, 8))],
               "attn_fwd": [("gu1", (8, 8))],
               "ffn1_in": [("wd1", WHOLE)]}
REDUCE_PLAN = [(["wd1"], "ffn1_dwgu"), (["gu1"], "ffn1_in_bwd"), (["w_o", "w_q", "w_kv"], "ffn0_out_bwd"),
               (["wd0"], "ffn0_dwgu"), (["gu0"], "ffn0_in_bwd"), (["w_out", "w_in"], "a_in_bwd")]
GRAD_KIND = dict(KIND, gu0="split", gu1="split")


class Traffic:
    def __init__(self, wholes, quarter, c_arr):
        self.wholes, self.quarter, self.c_arr = wholes, quarter, c_arr
        self.sums, self.got = {}, {}
        self.ready = {}

    def ride(self, name, small=None):
        if name in GATHER_PLAN:
            return gather_ride([self.wholes[k] for k, _ in GATHER_PLAN[name]],
                               [(KIND[k], self.quarter[k], part) for k, part in GATHER_PLAN[name]], small)
        if name in self.ready:
            keys = self.ready[name]
            return chip_ride([self.sums[k] for k in keys], [(GRAD_KIND[k], self.quarter[k]) for k in keys])
        return None

    def landed(self, name, results, wts):
        if name in GATHER_PLAN:
            for (k, _), whole in zip(GATHER_PLAN[name], results):
                self.wholes[k] = wts[k] = whole
        else:
            self.got.update(zip(self.ready[name], results))

    def grad(self, key, value):
        r, ws = self.quarter[key]
        view = {"row": (N_CHIPS, 2, r // 2, ws), "col": (1, 2, r // 2, N_CHIPS * ws), "split": (2, 2, r // 2, 2 * ws)}
        self.sums[key] = value.reshape(view[GRAD_KIND[key]])
        for keys, carrier in REDUCE_PLAN:
            if key == keys[-1]:
                own = [self.sums[k] for k in keys]
                got = pair_exchange(own, name="pair_exchange_" + keys[0])
                for k, o, g in zip(keys, own, got):
                    self.sums[k] = pair_add(o, g, self.c_arr, name="pair_add_" + k)
                self.ready[carrier] = keys


def kernel(x, a_pre_norm, a_w_in, a_conv_w, a_w_out, a_post_norm, ffn_pre_norm, ffn_w_gate_up, ffn_w_down, ffn_post_norm, kv_norm, w_kv, b_pre_norm, b_w_q, b_sinks, b_w_o, b_post_norm, loss_target, m_a_pre_norm, m_a_w_in, m_a_conv_w, m_a_w_out, m_a_post_norm, m_ffn_pre_norm, m_ffn_w_gate_up, m_ffn_w_down, m_ffn_post_norm, m_kv_norm, m_w_kv, m_b_pre_norm, m_b_w_q, m_b_sinks, m_b_w_o, m_b_post_norm, v_a_pre_norm, v_a_w_in, v_a_conv_w, v_a_w_out, v_a_post_norm, v_ffn_pre_norm, v_ffn_w_gate_up, v_ffn_w_down, v_ffn_post_norm, v_kv_norm, v_w_kv, v_b_pre_norm, v_b_w_q, v_b_sinks, v_b_w_o, v_b_post_norm):
    T, D = x.shape[1], x.shape[2]
    xi, yi, ci = _place()
    p = 2 * xi + yi
    p_arr = jnp.reshape(p, (1,)).astype(jnp.int32)
    c_arr = jnp.reshape(ci, (1,)).astype(jnp.int32)
    pc_arr = jnp.stack([p, ci]).astype(jnp.int32)
    me_arr = jnp.reshape(4 * xi + 2 * yi + ci, (1,)).astype(jnp.int32)
    qd = D // N_CHIPS

    big = {"w_in": (a_w_in, 0), "w_out": (a_w_out, 0), "gu0": (ffn_w_gate_up, 0), "gu1": (ffn_w_gate_up, 1),
           "wd0": (ffn_w_down, 0), "wd1": (ffn_w_down, 1), "w_kv": (w_kv[None], 0), "w_q": (b_w_q, 0),
           "w_o": (b_w_o, 0)}
    names = list(big)
    quarter = {k: w.shape[1:] for k, (w, _) in big.items()}
    wholes = {k: cast_quarter(w, layer, KIND[k], p_arr, name="cast_" + k) for k, (w, layer) in big.items()}
    traffic = Traffic(wholes, quarter, c_arr)
    small_shard = jnp.concatenate([a_pre_norm, a_post_norm, a_conv_w[0], jnp.zeros((3, qd), F32)], axis=0)
    wts = {}
    *landed, small_full = alone(traffic.ride("gather_first", small_shard), name="gather_first")
    traffic.landed("gather_first", landed, wts)
    rows = lambda k: jnp.transpose(small_full[:, k], (1, 0, 2)).reshape(-1, D)
    vec = {"a_pre": rows(slice(0, 1)), "a_post": rows(slice(1, 2)), "conv_w": rows(slice(2, 5)),
           "ffn_pre0": ffn_pre_norm[0:1], "ffn_pre1": ffn_pre_norm[1:2],
           "ffn_post0": ffn_post_norm[0:1], "ffn_post1": ffn_post_norm[1:2],
           "kv_norm": kv_norm[None], "b_pre": b_pre_norm, "b_post": b_post_norm, "sinks": b_sinks}

    loss, dx, small = local_step(x[0], loss_target[0], wts, vec, traffic)

    pad = lambda a: jnp.pad(a, ((0, 0), (0, D - a.shape[1])))
    small_block = jnp.concatenate(
        [small["a_pre"], small["a_post"], small["conv_w"][0:3], small["ffn_pre0"], small["ffn_pre1"],
         small["ffn_post0"], small["ffn_post1"], small["kv_norm"], small["b_pre"], small["b_post"],
         pad(small["sinks"][0:1]), pad(loss[0:1]), jnp.zeros((SMALL_ROWS - LOSS_ROW - 1, D), F32)], axis=0)
    (small_blocks,) = alone(chip_ride([], [], small_block), name="small_exchange")
    halves = [chip_reduce(traffic.sums[k], traffic.got[k], GRAD_KIND[k], pc_arr, name="chip_reduce_" + k)
              for k in names]
    grad = dict(zip(names, [q.reshape(quarter[k]) for k, q in zip(names, half_exchange(halves))]))
    small_sum = small_reduce(small_blocks, me_arr)

    out = {}
    out["a_w_in"] = adamw(a_w_in, [grad["w_in"]], m_a_w_in, v_a_w_in, name="adamw_a_w_in")
    out["a_w_out"] = adamw(a_w_out, [grad["w_out"]], m_a_w_out, v_a_w_out, name="adamw_a_w_out")
    out["ffn_w_gate_up"] = adamw(ffn_w_gate_up, [grad["gu0"], grad["gu1"]], m_ffn_w_gate_up, v_ffn_w_gate_up,
                                 name="adamw_ffn_w_gate_up")
    out["ffn_w_down"] = adamw(ffn_w_down, [grad["wd0"], grad["wd1"]], m_ffn_w_down, v_ffn_w_down,
                              name="adamw_ffn_w_down")
    out["w_kv"] = [o[0] for o in adamw(w_kv[None], [grad["w_kv"]], m_w_kv[None], v_w_kv[None], name="adamw_w_kv")]
    out["b_w_q"] = adamw(b_w_q, [grad["w_q"]], m_b_w_q, v_b_w_q, name="adamw_b_w_q")
    out["b_w_o"] = adamw(b_w_o, [grad["w_o"]], m_b_w_o, v_b_w_o, name="adamw_b_w_o")

    def pack(a_pre, a_post, conv, ffn_pre, ffn_post, kvn, b_pre, b_post, sinks):
        return jnp.concatenate([pad(a_pre), pad(a_post), pad(conv[0]), ffn_pre, ffn_post, kvn[None], b_pre, b_post,
                                pad(sinks), jnp.zeros((SMALL_ROWS - 13, D), F32)], axis=0)

    g_small = jnp.concatenate([pad(lax.dynamic_slice(small_sum, (0, p * qd), (5, qd))), small_sum[5:]], axis=0)
    w_small = pack(a_pre_norm, a_post_norm, a_conv_w, ffn_pre_norm, ffn_post_norm, kv_norm, b_pre_norm, b_post_norm,
                   b_sinks)
    m_small = pack(m_a_pre_norm, m_a_post_norm, m_a_conv_w, m_ffn_pre_norm, m_ffn_post_norm, m_kv_norm,
                   m_b_pre_norm, m_b_post_norm, m_b_sinks)
    v_small = pack(v_a_pre_norm, v_a_post_norm, v_a_conv_w, v_ffn_pre_norm, v_ffn_post_norm, v_kv_norm,
                   v_b_pre_norm, v_b_post_norm, v_b_sinks)
    packed = adamw(w_small[None], [g_small], m_small[None], v_small[None], name="adamw_small")
    ns = b_sinks.shape[1]
    unpack = lambda a: {"a_pre_norm": a[0:1, :qd], "a_post_norm": a[1:2, :qd], "a_conv_w": a[None, 2:5, :qd],
                        "ffn_pre_norm": a[5:7], "ffn_post_norm": a[7:9], "kv_norm": a[9], "b_pre_norm": a[10:11],
                        "b_post_norm": a[11:12], "b_sinks": a[12:13, :ns]}
    unpacked = [unpack(a[0]) for a in packed]
    for k in unpacked[0]:
        out[k] = [u[k] for u in unpacked]

    order = ["a_pre_norm", "a_w_in", "a_conv_w", "a_w_out", "a_post_norm", "ffn_pre_norm", "ffn_w_gate_up",
             "ffn_w_down", "ffn_post_norm", "kv_norm", "w_kv", "b_pre_norm", "b_w_q", "b_sinks", "b_w_o",
             "b_post_norm"]
    return (small_sum[LOSS_ROW, 0], dx[None], *[out[k][0] for k in order], *[out[k][1] for k in order],
            *[out[k][2] for k in order], *[out[k][3] for k in order])
```

```python
import math

import jax
import jax.numpy as jnp
from jax import lax
from jax.experimental import pallas as pl
from jax.experimental.pallas import tpu as pltpu

F32 = jnp.float32
BF16 = jnp.bfloat16
SDS = jax.ShapeDtypeStruct
MESH = pl.DeviceIdType.MESH
DMA = pltpu.SemaphoreType.DMA
HBM_SPEC = pl.BlockSpec(memory_space=pltpu.HBM)

EPS = 1e-6
NEG = -1e30
HEAD_DIM = 64
N_KV_HEADS = 4
BLOCK = 128
ROT_DIM = HEAD_DIM // 4
ROPE_THETA = 500000.0
N_CHIPS = 4

ADAM_LR = 0.001
ADAM_B1 = 0.9
ADAM_B2 = 0.999
ADAM_EPS = 1e-08
ADAM_WD = 0.01
ADAM_STEP = 10

VMEM_LIMIT_BYTES = 52 * 1024 * 1024
ROW_TILE = 512
BF16_ROWS = 16
MXU_WIDTH = 256

KIND = {"w_in": "col", "gu0": "col", "gu1": "col", "w_out": "row", "wd0": "row", "wd1": "row", "w_kv": "row",
        "w_q": "row", "w_o": "row"}


def _params(*semantics):
    return pltpu.CompilerParams(dimension_semantics=semantics, vmem_limit_bytes=VMEM_LIMIT_BYTES)


def _row_tile(rows, limit, step=8):
    return max(t for t in range(step, limit + 1, step) if rows % t == 0)


def _place():
    return lax.axis_index("x"), lax.axis_index("y"), lax.axis_index("c")


def _other_chips(x, y):
    return [(1 - x, y), (x, 1 - y), (1 - x, 1 - y)]


def _remote(src, dst, send_sem, recv_sem, to):
    return pltpu.make_async_remote_copy(src_ref=src, dst_ref=dst, send_sem=send_sem, recv_sem=recv_sem,
                                        device_id=to, device_id_type=MESH)


def _full_shape(kind, quarter):
    r, ws = quarter
    return (N_CHIPS * r, ws) if kind == "row" else (r, N_CHIPS * ws)


def _half_of_quarter(ref, kind, quarter, sixteenths, q, half):
    r, ws = quarter
    h = r // 2
    lo, n = sixteenths[0] * h // 16, sixteenths[1] * h // 16
    assert lo % BF16_ROWS == 0 and n % BF16_ROWS == 0, (quarter, sixteenths)
    if kind == "row":
        return ref.at[pl.ds(pl.multiple_of(q * r + half * h + lo, BF16_ROWS), n)]
    return ref.at[pl.ds(pl.multiple_of(half * h + lo, BF16_ROWS), n), pl.ds(pl.multiple_of(q * ws, 128), ws)]


class Ride:
    def __init__(self, operands, out_shape, aliases, sems, make):
        self.operands, self.out_shape, self.aliases, self.sems, self.make = operands, out_shape, aliases, sems, make


def join(rides):
    rides = [r for r in rides if r is not None]
    if len(rides) < 2:
        return rides[0] if rides else None
    aliases, at = {}, [0, 0, 0]
    cuts = []
    for r in rides:
        aliases.update({at[0] + i: at[1] + o for i, o in r.aliases.items()})
        cuts.append(tuple(at))
        at = [at[0] + len(r.operands), at[1] + len(r.out_shape), at[2] + len(r.sems)]
    cuts.append(tuple(at))

    def make(ins, outs, sem):
        made = [r.make(ins[lo[0]:hi[0]], outs[lo[1]:hi[1]], sem[lo[2]:hi[2]]) for r, lo, hi in zip(rides, cuts, cuts[1:])]

        def start():
            for s, _ in made:
                s()

        def finish():
            for _, f in made:
                f()

        return start, finish

    return Ride(sum((list(r.operands) for r in rides), []), sum((list(r.out_shape) for r in rides), []), aliases,
                sum((list(r.sems) for r in rides), []), make)


def _call(body, *, name, grid, in_specs, out_specs, out_shape, args, scratch_shapes=(), semantics=None, ride=None,
          prefetch=None):
    pre = 0 if prefetch is None else 1
    n_in, n_out, n_scr = len(in_specs), len(out_specs), len(scratch_shapes)
    r_in, r_out = (len(ride.operands), len(ride.out_shape)) if ride is not None else (0, 0)
    a, b = pre + n_in, pre + n_in + r_in
    c, d = b + n_out, b + n_out + r_out
    e = d + n_scr

    def riding(*refs):
        start, finish = ride.make(refs[a:b], refs[c:d], refs[e:])
        ids = [pl.program_id(k) for k in range(len(grid))]
        first, last = ids[0] == 0, ids[0] == grid[0] - 1
        for k in range(1, len(grid)):
            first, last = first & (ids[k] == 0), last & (ids[k] == grid[k] - 1)
        pl.when(first)(start)
        body(*refs[:a], *refs[b:c], *refs[d:e])
        pl.when(last)(finish)

    if ride is None:
        kernel_body, extra_in, extra_out, extra_shape, extra_scr, aliases = body, [], [], [], [], {}
        params = _params(*semantics)
    else:
        kernel_body, extra_in, extra_out = riding, [HBM_SPEC] * r_in, [HBM_SPEC] * r_out
        extra_shape, extra_scr = list(ride.out_shape), list(ride.sems)
        aliases = {pre + n_in + i: n_out + o for i, o in ride.aliases.items()}
        params = _params(*(("arbitrary",) * len(grid)))
    specs = dict(grid=grid, in_specs=list(in_specs) + extra_in, out_specs=list(out_specs) + extra_out,
                 scratch_shapes=list(scratch_shapes) + extra_scr)
    if prefetch is not None:
        specs = dict(grid_spec=pltpu.PrefetchScalarGridSpec(num_scalar_prefetch=1, **specs))
        args = (prefetch,) + tuple(args)
    outs = pl.pallas_call(kernel_body, name=name, out_shape=list(out_shape) + extra_shape,
                          input_output_aliases=aliases, compiler_params=params, **specs,
                          )(*args, *(ride.operands if ride is not None else ()))
    return outs if ride is None else (outs[:n_out], outs[n_out:])


def alone(ride, *, name):
    def body(*refs):
        n = len(ride.operands)
        start, finish = ride.make(refs[:n], refs[n:n + len(ride.out_shape)], refs[n + len(ride.out_shape):])
        start()
        finish()

    return pl.pallas_call(
        body, name=name, in_specs=[HBM_SPEC] * len(ride.operands), out_specs=[HBM_SPEC] * len(ride.out_shape),
        out_shape=list(ride.out_shape), input_output_aliases=dict(ride.aliases), scratch_shapes=list(ride.sems),
    )(*ride.operands)


def gather_ride(wholes, metas, small=None):
    n = len(wholes)
    operands, out_shape = list(wholes), [SDS(s.shape, s.dtype) for s in wholes]
    sems = [DMA((n, 3)), DMA((n, 3)), DMA((n, 3)), DMA((n, 3))]
    if small is not None:
        operands.append(small)
        out_shape.append(SDS((N_CHIPS,) + small.shape, small.dtype))
        sems += [DMA((3,)), DMA((3,)), DMA(())]

    def make(ins, outs, sem):
        send1, recv1, send2, recv2 = sem[:4]
        x, y, c = _place()
        p = 2 * x + y
        chips = _other_chips(x, y)
        me, sibling = (x, y, c), (x, y, 1 - c)
        part = lambda t, q, half: _half_of_quarter(outs[t], *metas[t], q, half)
        first = []
        for j, (qx, qy) in enumerate(chips):
            if small is not None:
                first.append(_remote(ins[n], outs[n].at[p], sem[4].at[j], sem[5].at[j], (qx, qy, c)))
            for t in range(n):
                first.append(_remote(part(t, p, c), part(t, p, c), send1.at[t, j], recv1.at[t, j], (qx, qy, c)))
        local = [] if small is None else [pltpu.make_async_copy(ins[n], outs[n].at[p], sem[6])]

        def start():
            for cp in local + first:
                cp.start()

        def finish():
            passed = []
            for j, (qx, qy) in enumerate(chips):
                q = 2 * qx + qy
                for t in range(n):
                    landed = part(t, q, c)
                    _remote(landed, landed, send1.at[t, j], recv1.at[t, j], me).wait_recv()
                    cp = _remote(landed, landed, send2.at[t, j], recv2.at[t, j], sibling)
                    cp.start()
                    passed.append(cp)
            for j, (qx, qy) in enumerate(chips):
                q = 2 * qx + qy
                if small is not None:
                    _remote(outs[n].at[q], outs[n].at[q], sem[4].at[j], sem[5].at[j], me).wait_recv()
                for t in range(n):
                    theirs = part(t, q, 1 - c)
                    _remote(theirs, theirs, send2.at[t, j], recv2.at[t, j], me).wait_recv()
            for cp in first + passed:
                cp.wait_send()
            for cp in local:
                cp.wait()

        return start, finish

    return Ride(operands, out_shape, {t: t for t in range(n)}, sems, make)


def chip_ride(sums, metas, small=None):
    n = len(sums)
    operands = list(sums)
    out_shape = [SDS((3, s.shape[1], quarter[1]), s.dtype) for s, (_, quarter) in zip(sums, metas)]
    sems = [DMA((n, 3)), DMA((n, 3))] if n else []
    if small is not None:
        operands.append(small)
        out_shape.append(SDS((8,) + small.shape, small.dtype))
        sems += [DMA((7,)), DMA((7,)), DMA(())]

    def make(ins, outs, sem):
        x, y, c = _place()
        cps = []
        for j, (qx, qy) in enumerate(_other_chips(x, y)):
            q = 2 * qx + qy
            for t in range(n):
                kind, (_, ws) = metas[t]
                if kind == "row":
                    src = ins[t].at[q]
                elif kind == "col":
                    src = ins[t].at[0, :, pl.ds(pl.multiple_of(q * ws, 128), ws)]
                else:
                    src = ins[t].at[q // 2, :, pl.ds(pl.multiple_of((q % 2) * ws, 128), ws)]
                cps.append(_remote(src, outs[t].at[j], sem[0].at[t, j], sem[1].at[t, j], (qx, qy, c)))
        local = []
        if small is not None:
            ssend, srecv, lsem = sem[-3:]
            local.append(pltpu.make_async_copy(ins[n], outs[n].at[0], lsem))
            for k in range(1, 8):
                peer = (x ^ (k >> 2 & 1), y ^ (k >> 1 & 1), c ^ (k & 1))
                cps.append(_remote(ins[n], outs[n].at[k], ssend.at[k - 1], srecv.at[k - 1], peer))

        def start():
            for cp in local + cps:
                cp.start()

        def finish():
            for cp in cps + local:
                cp.wait()

        return start, finish

    return Ride(operands, out_shape, {}, sems, make)


def pair_ride(grads):
    n = len(grads)

    def make(ins, outs, sem):
        x, y, c = _place()
        cps = [_remote(ins[t].at[:, 1 - c], outs[t], sem[0].at[t], sem[1].at[t], (x, y, 1 - c)) for t in range(n)]

        def start():
            for cp in cps:
                cp.start()

        def finish():
            for cp in cps:
                cp.wait()

        return start, finish

    return Ride(list(grads), [SDS((g.shape[0],) + g.shape[2:], g.dtype) for g in grads], {}, [DMA((n,)), DMA((n,))],
                make)


def half_exchange(quarters):
    n = len(quarters)

    def body(*refs):
        outs = refs[n:2 * n]
        send, recv = refs[2 * n:]
        x, y, c = _place()
        sends = [_remote(outs[t].at[c], outs[t].at[c], send.at[t], recv.at[t], (x, y, 1 - c)) for t in range(n)]
        for cp in sends:
            cp.start()
        for t in range(n):
            theirs = outs[t].at[1 - c]
            _remote(theirs, theirs, send.at[t], recv.at[t], (x, y, c)).wait_recv()
        for cp in sends:
            cp.wait_send()

    return pl.pallas_call(
        body, name="half_exchange",
        in_specs=[HBM_SPEC] * n, out_specs=[HBM_SPEC] * n,
        out_shape=[SDS(q.shape, q.dtype) for q in quarters],
        input_output_aliases={t: t for t in range(n)},
        scratch_shapes=[DMA((n,)), DMA((n,))],
    )(*quarters)


CAST_STEPS = 4


def cast_quarters(sources, p_arr, *, name, ride=None):
    n = len(sources)
    in_specs, out_specs, out_shape = [], [], []
    for w, layer, kind in sources:
        _, r, ws = w.shape
        tr = r // CAST_STEPS
        assert tr % BF16_ROWS == 0, w.shape
        in_specs.append(pl.BlockSpec((None, tr, ws), lambda i, p_ref, layer=layer: (layer, i, 0)))
        out_specs.append(pl.BlockSpec((tr, ws), (lambda i, p_ref: (p_ref[0] * CAST_STEPS + i, 0)) if kind == "row"
                                      else (lambda i, p_ref: (i, p_ref[0]))))
        out_shape.append(SDS(_full_shape(kind, (r, ws)), BF16))

    def body(p_ref, *refs):
        for w_ref, o_ref in zip(refs[:n], refs[n:]):
            o_ref[...] = w_ref[...].astype(BF16)

    return _call(body, name=name, grid=(CAST_STEPS,), in_specs=in_specs, out_specs=out_specs, out_shape=out_shape,
                 semantics=("parallel",), args=[w for w, _, _ in sources], ride=ride, prefetch=p_arr)


def pair_add(own, got, c_arr, *, name):
    A, _, h, W = own.shape
    th = _row_tile(h, max(BF16_ROWS, (3 << 19) // W), BF16_ROWS)

    def body(c_ref, a_ref, b_ref, o_ref):
        o_ref[...] = (a_ref[...].astype(F32) + b_ref[...].astype(F32)).astype(BF16)

    return pl.pallas_call(
        body, name=name,
        grid_spec=pltpu.PrefetchScalarGridSpec(
            num_scalar_prefetch=1, grid=(A, h // th),
            in_specs=[pl.BlockSpec((None, None, th, W), lambda q, i, c_ref: (q, c_ref[0], i, 0)),
                      pl.BlockSpec((None, th, W), lambda q, i, c_ref: (q, i, 0))],
            out_specs=pl.BlockSpec((None, th, W), lambda q, i, c_ref: (q, i, 0))),
        out_shape=SDS((A, h, W), BF16),
        compiler_params=_params("parallel", "parallel"),
    )(c_arr, own, got)


REDUCE_STEPS = 2


def chip_reduce(sums, got, kinds, pc_arr, *, name):
    n = len(sums)
    mine = {"row": lambda i, pc_ref: (pc_ref[0], i, 0), "col": lambda i, pc_ref: (0, i, pc_ref[0]),
            "split": lambda i, pc_ref: (pc_ref[0] // 2, i, pc_ref[0] % 2)}
    a_specs, b_specs, o_specs, out_shape = [], [], [], []
    for g, kind in zip(got, kinds):
        _, h, ws = g.shape
        th = h // REDUCE_STEPS
        assert th % BF16_ROWS == 0, g.shape
        a_specs.append(pl.BlockSpec((None, th, ws), mine[kind]))
        b_specs.append(pl.BlockSpec((3, th, ws), lambda i, pc_ref: (0, i, 0)))
        o_specs.append(pl.BlockSpec((None, th, ws), lambda i, pc_ref: (pc_ref[1], i, 0)))
        out_shape.append(SDS((2, h, ws), F32))

    def body(pc_ref, *refs):
        for a_ref, b_ref, o_ref in zip(refs[:n], refs[n:2 * n], refs[2 * n:]):
            o_ref[...] = ((a_ref[...].astype(F32) + b_ref[0].astype(F32)) + b_ref[1].astype(F32)) + b_ref[2].astype(F32)

    return _call(body, name=name, grid=(REDUCE_STEPS,), in_specs=a_specs + b_specs, out_specs=o_specs,
                 out_shape=out_shape, semantics=("parallel",), args=list(sums) + list(got), prefetch=pc_arr)


def small_reduce(blocks, me_arr):
    _, rows, D = blocks.shape

    def body(me_ref, b_ref, o_ref):
        me = me_ref[0]
        total = b_ref[me]
        for d in range(1, 8):
            total = total + b_ref[d ^ me]
        o_ref[...] = total

    return pl.pallas_call(
        body, name="small_reduce",
        grid_spec=pltpu.PrefetchScalarGridSpec(
            num_scalar_prefetch=1, grid=(1,),
            in_specs=[pl.BlockSpec((8, rows, D), lambda i, me_ref: (0, 0, 0))],
            out_specs=pl.BlockSpec((rows, D), lambda i, me_ref: (0, 0))),
        out_shape=SDS((rows, D), F32),
        compiler_params=_params("arbitrary"),
    )(me_arr, blocks)


def adamw(w, gs, m, v, *, name):
    L, r, cols = w.shape
    tr = _row_tile(r, 256)
    nt = r // tr

    def body(*refs):
        w_ref, m_ref, v_ref = refs[:3]
        g_refs = refs[3:3 + L]
        g_out, d_out, m_out, v_out = refs[3 + L:]
        layer = pl.program_id(0)
        g = g_refs[0][...]
        for l in range(1, L):
            g = jnp.where(layer == l, g_refs[l][...], g)
        m_new = ADAM_B1 * m_ref[...] + (1.0 - ADAM_B1) * g
        v_new = ADAM_B2 * v_ref[...] + (1.0 - ADAM_B2) * (g * g)
        m_hat = m_new / (1.0 - ADAM_B1 ** ADAM_STEP)
        v_hat = v_new / (1.0 - ADAM_B2 ** ADAM_STEP)
        g_out[...] = g
        m_out[...] = m_new
        v_out[...] = v_new
        d_out[...] = -ADAM_LR * (m_hat / (jnp.sqrt(v_hat) + ADAM_EPS) + ADAM_WD * w_ref[...])

    full = pl.BlockSpec((None, tr, cols), lambda l, i: (l, i, 0))
    g_spec = lambda l0: pl.BlockSpec((tr, cols), lambda l, i: (jnp.where(l == l0, i, jnp.where(l < l0, 0, nt - 1)), 0))
    return pl.pallas_call(
        body, name=name, grid=(L, nt),
        in_specs=[full, full, full] + [g_spec(l0) for l0 in range(L)],
        out_specs=[full] * 4,
        out_shape=[SDS(w.shape, F32)] * 4,
        compiler_params=_params("arbitrary", "arbitrary"),
    )(w, m, v, *gs)


def _rms_r(xf):
    return lax.rsqrt(jnp.mean(xf * xf, axis=-1, keepdims=True) + EPS)


def _rmsnorm_bwd(xf, g, dy):
    r = _rms_r(xf)
    xh = xf * r
    gd = g * dy
    return r * (gd - xh * jnp.mean(xh * gd, axis=-1, keepdims=True)), xh


def _dot(a, b):
    return jnp.dot(a, b, preferred_element_type=F32)


def _dot_nt(a, b):
    return lax.dot_general(a, b, (((1,), (1,)), ((), ())), preferred_element_type=F32)


def _dot_tn(a, b):
    return lax.dot_general(a, b, (((0,), (0,)), ((), ())), preferred_element_type=F32)


def _accumulate(ref, first, value):
    @pl.when(first)
    def _():
        ref[...] = value

    @pl.when(jnp.logical_not(first))
    def _():
        ref[...] += value


def norm_matmul(x, g, w, *, tn, split, name, ride=None, tm=ROW_TILE):
    T, D = x.shape
    N = w.shape[1]
    per = N // split // tn

    def body(x_ref, g_ref, w_ref, o_ref, xn_ref):
        @pl.when(pl.program_id(1) == 0)
        def _():
            xf = x_ref[...]
            xn_ref[...] = (xf * _rms_r(xf) * g_ref[...]).astype(BF16)

        o_ref[...] = _dot(xn_ref[...], w_ref[...]).astype(BF16)

    return _call(
        body, name=name, grid=(T // tm, N // tn),
        in_specs=[pl.BlockSpec((tm, D), lambda i, j: (i, 0)),
                  pl.BlockSpec((1, D), lambda i, j: (0, 0)),
                  pl.BlockSpec((D, tn), lambda i, j: (0, j))],
        out_specs=[pl.BlockSpec((None, tm, tn), lambda i, j: (j // per, i, j % per)),
                   pl.BlockSpec((tm, D), lambda i, j: (i, 0))],
        out_shape=[SDS((split, T, N // split), BF16), SDS((T, D), BF16)],
        semantics=("parallel", "arbitrary"), args=(x, g, w), ride=ride)


def _shift_down(prev, cur, by):
    big = jnp.concatenate([prev, cur], axis=0)
    return pltpu.roll(big, by, 0)[prev.shape[0]:]


def _shift_up(cur, nxt, by):
    big = jnp.concatenate([cur, nxt], axis=0)
    return pltpu.roll(big, big.shape[0] - by, 0)[:cur.shape[0]]


def conv_mix_out(bcx, conv_w, w_out, g_post, res, *, name, ride=None, tm=ROW_TILE):
    T, D = res.shape
    hb = tm // BF16_ROWS

    def body(b_ref, c_ref, u_ref, cp_ref, up_ref, cw_ref, w_ref, g_ref, r_ref, h_ref, z_ref, y_ref):
        i = pl.program_id(0)
        cu = c_ref[...].astype(F32) * u_ref[...].astype(F32)
        cup = cp_ref[...].astype(F32) * up_ref[...].astype(F32)
        cup = jnp.where(i == 0, 0.0, cup)
        cv = (cw_ref[0:1, :] * _shift_down(cup, cu, 2) + cw_ref[1:2, :] * _shift_down(cup, cu, 1)
              + cw_ref[2:3, :] * cu)
        y = (b_ref[...].astype(F32) * cv).astype(BF16)
        y_ref[...] = y
        z = _dot(y, w_ref[...])
        z_ref[...] = z.astype(BF16)
        h_ref[...] = r_ref[...] + z * _rms_r(z) * g_ref[...]

    tile = lambda col: pl.BlockSpec((tm, D), lambda i: (i, col))
    halo = lambda col: pl.BlockSpec((BF16_ROWS, D), lambda i: (jnp.maximum(i * hb - 1, 0), col))
    row = pl.BlockSpec((tm, D), lambda i: (i, 0))
    return _call(
        body, name=name, grid=(T // tm,),
        in_specs=[tile(0), tile(1), tile(2), halo(1), halo(2),
                  pl.BlockSpec((3, D), lambda i: (0, 0)),
                  pl.BlockSpec((D, D), lambda i: (0, 0)),
                  pl.BlockSpec((1, D), lambda i: (0, 0)), row],
        out_specs=[row, row, row],
        out_shape=[SDS((T, D), F32), SDS((T, D), BF16), SDS((T, D), BF16)],
        semantics=("parallel",), args=(bcx, bcx, bcx, bcx, bcx, conv_w, w_out, g_post, res), ride=ride)


def plain_mix_out(a, w, g_post, res, *, name, target=None, ride=None, tm=ROW_TILE):
    T, D = res.shape
    K = a.shape[1]
    with_loss = target is not None

    def body(a_ref, w_ref, g_ref, r_ref, *rest):
        z = _dot(a_ref[...], w_ref[...])
        h = r_ref[...] + z * _rms_r(z) * g_ref[...]
        if with_loss:
            t_ref, h_ref, z_ref, loss_ref = rest
            diff = h - t_ref[...]
            h_ref[...] = diff * (1.0 / D)
            part = jnp.full(loss_ref.shape, 0.5 / D, F32) * jnp.sum(diff * diff)
            _accumulate(loss_ref, pl.program_id(0) == 0, part)
        else:
            h_ref, z_ref = rest
            h_ref[...] = h
        z_ref[...] = z.astype(BF16)

    row = pl.BlockSpec((tm, D), lambda i: (i, 0))
    loss_spec, loss_shape = pl.BlockSpec((8, 128), lambda i: (0, 0)), SDS((8, 128), F32)
    return _call(
        body, name=name, grid=(T // tm,),
        in_specs=[pl.BlockSpec((tm, K), lambda i: (i, 0)),
                  pl.BlockSpec((K, D), lambda i: (0, 0)),
                  pl.BlockSpec((1, D), lambda i: (0, 0)), row] + [row] * with_loss,
        out_specs=[row, row] + [loss_spec] * with_loss,
        out_shape=[SDS((T, D), F32), SDS((T, D), BF16)] + [loss_shape] * with_loss,
        semantics=("arbitrary",), args=(a, w, g_post, res) + ((target,) if with_loss else ()), ride=ride)


def _silu_grads(d, g, u):
    sg = jax.nn.sigmoid(g)
    return d * u * (sg * (1.0 + g * (1.0 - sg))), d * (g * sg)


def norm_swiglu_in(x, g, w, *, name, ride=None, tm=ROW_TILE // 2):
    T, D = x.shape
    F = w.shape[1] // 2

    def body(x_ref, g_ref, wg_ref, wu_ref, gu_ref, a_ref, xt_ref):
        xf = x_ref[...]
        xn = xf * _rms_r(xf) * g_ref[...]
        xt_ref[...] = xn.T.astype(BF16)
        xb = xn.astype(BF16)
        gate = _dot(xb, wg_ref[...]).astype(BF16)
        up = _dot(xb, wu_ref[...]).astype(BF16)
        gu_ref[0] = gate
        gu_ref[1] = up
        a_ref[...] = gate * jax.nn.sigmoid(gate) * up

    half = lambda s: pl.BlockSpec((D, F), lambda i: (0, s), pipeline_mode=pl.Buffered(1))
    return _call(
        body, name=name, grid=(T // tm,),
        in_specs=[pl.BlockSpec((tm, D), lambda i: (i, 0)), pl.BlockSpec((1, D), lambda i: (0, 0)), half(0), half(1)],
        out_specs=[pl.BlockSpec((2, tm, F), lambda i: (0, i, 0)), pl.BlockSpec((tm, F), lambda i: (i, 0)),
                   pl.BlockSpec((D, tm), lambda i: (0, i))],
        out_shape=[SDS((2, T, F), BF16), SDS((T, F), BF16), SDS((D, T), BF16)],
        semantics=("parallel",), args=(x, g, w, w), ride=ride)


def swiglu_bwd_tn(xt, dact, gu, *, name, ride=None, tb=MXU_WIDTH):
    D, T = xt.shape
    F = dact.shape[1]

    def body(xt_ref, d_ref, g_ref, u_ref, o_ref):
        dg, du = _silu_grads(d_ref[...], g_ref[...], u_ref[...])
        o_ref[0] = _dot(xt_ref[...], dg).astype(BF16)
        o_ref[1] = _dot(xt_ref[...], du).astype(BF16)

    col = lambda s: pl.BlockSpec((None, T, tb), lambda j: (s, 0, j))
    out = _call(
        body, name=name, grid=(F // tb,),
        in_specs=[pl.BlockSpec((D, T), lambda j: (0, 0), pipeline_mode=pl.Buffered(1)),
                  pl.BlockSpec((T, tb), lambda j: (0, j)), col(0), col(1)],
        out_specs=[pl.BlockSpec((2, D, tb), lambda j: (0, 0, j))],
        out_shape=[SDS((2, D, F), BF16)],
        semantics=("parallel",), args=(xt, dact, gu, gu), ride=ride)
    return out[0] if ride is None else (out[0][0], out[1])


def swiglu_bwd_in(dact, gu, w, h_in, g, dh_out, *, name, ride=None, tm=ROW_TILE // 2):
    T, D = h_in.shape
    F = dact.shape[1]

    def body(d_ref, gg_ref, uu_ref, wg_ref, wu_ref, h_ref, g_ref, dh_ref, o_ref, dg_ref):
        dgate, dup = _silu_grads(d_ref[...], gg_ref[...], uu_ref[...])
        dn = _dot_nt(dgate, wg_ref[...]) + _dot_nt(dup, wu_ref[...])
        dx, hh = _rmsnorm_bwd(h_ref[...], g_ref[...], dn)
        o_ref[...] = dh_ref[...] + dx
        _accumulate(dg_ref, pl.program_id(0) == 0, jnp.sum(dn * hh, axis=0, keepdims=True))

    row = pl.BlockSpec((tm, D), lambda i: (i, 0))
    vec = pl.BlockSpec((1, D), lambda i: (0, 0))
    part = lambda s: pl.BlockSpec((None, tm, F), lambda i: (s, i, 0))
    half = lambda s: pl.BlockSpec((D, F), lambda i: (0, s), pipeline_mode=pl.Buffered(1))
    return _call(
        body, name=name, grid=(T // tm,),
        in_specs=[pl.BlockSpec((tm, F), lambda i: (i, 0)), part(0), part(1), half(0), half(1), row, vec, row],
        out_specs=[row, vec],
        out_shape=[SDS((T, D), F32), SDS((1, D), F32)],
        semantics=("arbitrary",), args=(dact, gu, gu, w, w, h_in, g, dh_out), ride=ride)


def rope_tables(T):
    half = ROT_DIM // 2
    inv_freq = ROPE_THETA ** (-jnp.arange(0, ROT_DIM, 2, dtype=F32) / ROT_DIM)
    ang = (jnp.arange(T, dtype=F32)[:, None] * inv_freq[None, :]).T
    cos, sin = jnp.cos(ang), jnp.sin(ang)
    rest = HEAD_DIM - ROT_DIM
    one, zero = jnp.ones((rest, T), F32), jnp.zeros((rest, T), F32)
    zh = jnp.zeros((half, T), F32)
    fac = jnp.concatenate([cos, cos, one], axis=0)
    up = jnp.concatenate([-sin, zh, zero], axis=0)
    down = jnp.concatenate([zh, sin, zero], axis=0)
    return jnp.stack([fac, up, down])


def _rope(t, tab):
    half = ROT_DIM // 2
    return t * tab[0] + pltpu.roll(t, HEAD_DIM - half, 0) * tab[1] + pltpu.roll(t, half, 0) * tab[2]


def _rope_t(d, tab):
    half = ROT_DIM // 2
    return d * tab[0] + pltpu.roll(d * tab[1], half, 0) + pltpu.roll(d * tab[2], HEAD_DIM - half, 0)


def _head(t, h):
    return t[h * HEAD_DIM:(h + 1) * HEAD_DIM]


def _band(n, group):
    kj = lax.broadcasted_iota(jnp.int32, (2 * BLOCK, BLOCK), 0)
    qi = lax.broadcasted_iota(jnp.int32, (2 * BLOCK, BLOCK), 1)
    mask = (kj > qi) & (kj <= qi + BLOCK) & ((n > 0) | (kj >= BLOCK))
    return jnp.tile(mask, (1, group))


def _attn_specs(D, kvd):
    prev = lambda n: jnp.maximum(n - 1, 0)
    return [pl.BlockSpec((BLOCK, D), lambda n: (n, 0)),
            pl.BlockSpec((BLOCK, kvd), lambda n: (prev(n), 0)),
            pl.BlockSpec((BLOCK, kvd), lambda n: (n, 0)),
            pl.BlockSpec((BLOCK, kvd), lambda n: (prev(n), 1)),
            pl.BlockSpec((BLOCK, kvd), lambda n: (n, 1)),
            pl.BlockSpec((3, HEAD_DIM, BLOCK), lambda n: (0, 0, prev(n))),
            pl.BlockSpec((3, HEAD_DIM, BLOCK), lambda n: (0, 0, n)),
            pl.BlockSpec(memory_space=pltpu.SMEM)]


def _attn_operands(q_ref, kp_ref, k_ref, vp_ref, v_ref, tp_ref, t_ref):
    flip = lambda ref: ref[...].astype(F32).T
    tab = t_ref[...]
    kt = jnp.concatenate([flip(kp_ref), flip(k_ref)], axis=1)
    vt = jnp.concatenate([flip(vp_ref), flip(v_ref)], axis=1)
    return flip(q_ref), kt, vt, tab, jnp.concatenate([tp_ref[...], tab], axis=2)


SCORE_SCALE = 1.0 / math.sqrt(HEAD_DIM)
HEADS_TOGETHER = 4


def _group_heads(t, first, count, tab=None):
    heads = [_head(t, first + g) for g in range(count)]
    if tab is not None:
        heads = [_rope(h, tab) * SCORE_SCALE for h in heads]
    return jnp.concatenate(heads, axis=1).astype(BF16)


def _sink_row(s_ref, first, count):
    which = lax.broadcasted_iota(jnp.int32, (1, count * BLOCK), 1) // BLOCK
    row = jnp.zeros((1, count * BLOCK), F32)
    for g in range(count):
        row = jnp.where(which == g, s_ref[0, first + g], row)
    return row


def _softmax_block(k_j, q_j, sink, mask):
    return _softmax(_dot_tn(k_j, q_j), sink, mask)


def _softmax(scores, sink, mask):
    s = jnp.where(mask, scores, NEG)
    m = jnp.maximum(jnp.max(s, axis=0, keepdims=True), sink)
    e = jnp.exp(s - m)
    es = jnp.exp(sink - m)
    inv = 1.0 / (jnp.sum(e, axis=0, keepdims=True) + es)
    return e * inv, es * inv


def attention_fwd(q, kv, tabs, sinks, *, name, ride=None):
    T, D = q.shape
    kvd = kv.shape[1] // 2
    group = D // HEAD_DIM // N_KV_HEADS

    def body(q_ref, kp_ref, k_ref, vp_ref, v_ref, tp_ref, t_ref, s_ref, o_ref):
        gs = HEADS_TOGETHER
        mask = _band(pl.program_id(0), gs)
        qt, kt, vt, tab, tab2 = _attn_operands(q_ref, kp_ref, k_ref, vp_ref, v_ref, tp_ref, t_ref)
        firsts = [(j, first) for j in range(N_KV_HEADS) for first in range(j * group, (j + 1) * group, gs)]
        ks = [_rope(_head(kt, j), tab2).astype(BF16) for j in range(N_KV_HEADS)]
        scores = [_dot_tn(ks[j], _group_heads(qt, first, gs, tab)) for j, first in firsts]
        probs = [_softmax(s, _sink_row(s_ref, first, gs), mask)[0].astype(BF16)
                 for s, (j, first) in zip(scores, firsts)]
        outs = []
        for p, (j, first) in zip(probs, firsts):
            o = _dot(_head(vt, j).astype(BF16), p)
            outs += [o[:, g * BLOCK:(g + 1) * BLOCK] for g in range(gs)]
        o_ref[...] = jnp.concatenate(outs, axis=0).T.astype(BF16)

    return _call(
        body, name=name, grid=(T // BLOCK,),
        in_specs=_attn_specs(D, kvd),
        out_specs=[pl.BlockSpec((BLOCK, D), lambda n: (n, 0))],
        out_shape=[SDS((T, D), BF16)],
        semantics=("parallel",), args=(q, kv, kv, kv, kv, tabs, tabs, sinks), ride=ride)


def attention_bwd(q, kv, tabs, sinks, do, *, name, ride=None):
    T, D = q.shape
    kvd = kv.shape[1] // 2
    heads = D // HEAD_DIM
    group = heads // N_KV_HEADS

    def body(q_ref, kp_ref, k_ref, vp_ref, v_ref, tp_ref, t_ref, s_ref, do_ref, dq_ref, dc_ref, dp_ref, ds_ref):
        n = pl.program_id(0)
        gs = HEADS_TOGETHER
        mask = _band(n, gs)
        qt, kt, vt, tab, tab2 = _attn_operands(q_ref, kp_ref, k_ref, vp_ref, v_ref, tp_ref, t_ref)
        dot = do_ref[...].astype(F32).T
        lane = lax.broadcasted_iota(jnp.int32, (8, 128), 1)
        dsink = jnp.zeros((8, 128), F32)
        firsts = [(j, first) for j in range(N_KV_HEADS) for first in range(j * group, (j + 1) * group, gs)]
        ks = [_rope(_head(kt, j), tab2).astype(BF16) for j in range(N_KV_HEADS)]
        vs = [_head(vt, j).astype(BF16) for j in range(N_KV_HEADS)]
        qs = [_group_heads(qt, first, gs, tab) for _, first in firsts]
        dos = [_group_heads(dot, first, gs) for _, first in firsts]
        scores = [_dot_tn(ks[j], q) for q, (j, _) in zip(qs, firsts)]
        dps = [_dot_tn(vs[j], do) for do, (j, _) in zip(dos, firsts)]
        ps, dscs = [], []
        for s, dp, (j, first) in zip(scores, dps, firsts):
            p, p_sink = _softmax(s, _sink_row(s_ref, first, gs), mask)
            dl = jnp.sum(p * dp, axis=0, keepdims=True)
            dscs.append((p * (dp - dl)).astype(BF16))
            ps.append(p.astype(BF16))
            weight = p_sink * dl
            for g in range(gs):
                dsink = dsink - jnp.where(lane == first + g, jnp.sum(weight[:, g * BLOCK:(g + 1) * BLOCK]), 0.0)
        dqs = []
        dks = [jnp.zeros((HEAD_DIM, 2 * BLOCK), F32) for _ in range(N_KV_HEADS)]
        dvs = [jnp.zeros((HEAD_DIM, 2 * BLOCK), F32) for _ in range(N_KV_HEADS)]
        for p, dsc, q, do, (j, _) in zip(ps, dscs, qs, dos, firsts):
            dq = _dot(ks[j], dsc) * SCORE_SCALE
            dqs += [_rope_t(dq[:, g * BLOCK:(g + 1) * BLOCK], tab) for g in range(gs)]
            dks[j] = dks[j] + _dot_nt(q, dsc)
            dvs[j] = dvs[j] + _dot_nt(do, p)
        dks = [_rope_t(dk, tab2) for dk in dks]
        dq_ref[...] = jnp.concatenate(dqs, axis=0).T.astype(BF16)
        dkv = jnp.concatenate(dks + dvs, axis=0)
        dp_ref[...] = dkv[:, :BLOCK].T
        dc_ref[...] = dkv[:, BLOCK:].T
        _accumulate(ds_ref, n == 0, dsink)

    blk = lambda w: pl.BlockSpec((BLOCK, w), lambda n: (n, 0))
    return _call(
        body, name=name, grid=(T // BLOCK,),
        in_specs=_attn_specs(D, kvd) + [blk(D)],
        out_specs=[blk(D), blk(2 * kvd), blk(2 * kvd), pl.BlockSpec((8, 128), lambda n: (0, 0))],
        out_shape=[SDS((T, D), BF16), SDS((T, 2 * kvd), F32), SDS((T, 2 * kvd), F32), SDS((8, 128), F32)],
        semantics=("arbitrary",), args=(q, kv, kv, kv, kv, tabs, tabs, sinks, do), ride=ride)


def combine_dkv(d_cur, d_prev, *, name):
    T, W = d_cur.shape
    tm = ROW_TILE
    nt, per, last = T // tm, tm // BLOCK, T // BLOCK - 1

    def body(c_ref, p_ref, pn_ref, o_ref):
        nxt = jnp.where(pl.program_id(0) == nt - 1, 0.0, pn_ref[...])
        o_ref[...] = (c_ref[...] + jnp.concatenate([p_ref[BLOCK:, :], nxt], axis=0)).astype(BF16)

    return _call(
        body, name=name, grid=(nt,),
        in_specs=[pl.BlockSpec((tm, W), lambda i: (i, 0)), pl.BlockSpec((tm, W), lambda i: (i, 0)),
                  pl.BlockSpec((BLOCK, W), lambda i: (jnp.minimum((i + 1) * per, last), 0))],
        out_specs=[pl.BlockSpec((tm, W), lambda i: (i, 0))],
        out_shape=[SDS((T, W), BF16)],
        semantics=("parallel",), args=(d_cur, d_prev, d_prev))[0]


def normbwd_matmul_nt(z, g, dh, w, *, name, ride=None, tm=ROW_TILE):
    T, D = z.shape
    K = w.shape[0]

    def body(z_ref, g_ref, dh_ref, w_ref, dz_ref, dg_ref, o_ref):
        dh_ = dh_ref[...]
        dz, zh = _rmsnorm_bwd(z_ref[...].astype(F32), g_ref[...], dh_)
        dz = dz.astype(BF16)
        dz_ref[...] = dz
        _accumulate(dg_ref, pl.program_id(0) == 0, jnp.sum(dh_ * zh, axis=0, keepdims=True))
        o_ref[...] = _dot_nt(dz, w_ref[...]).astype(BF16)

    row = pl.BlockSpec((tm, D), lambda i: (i, 0))
    vec = pl.BlockSpec((1, D), lambda i: (0, 0))
    return _call(
        body, name=name, grid=(T // tm,),
        in_specs=[row, vec, row, pl.BlockSpec((K, D), lambda i: (0, 0))],
        out_specs=[row, vec, pl.BlockSpec((tm, K), lambda i: (i, 0))],
        out_shape=[SDS((T, D), BF16), SDS((1, D), F32), SDS((T, K), BF16)],
        semantics=("arbitrary",), args=(z, g, dh, w), ride=ride)


def matmul_nt_normbwd(da, w, h_in, g, dh_out, *, name, ride=None, tm=ROW_TILE):
    T, D = h_in.shape
    S, _, K = da.shape

    def body(*refs):
        da_refs, w_refs = refs[:S], refs[S:2 * S]
        h_ref, g_ref, dh_ref, o_ref, dg_ref = refs[2 * S:]
        dn = _dot_nt(da_refs[0][...], w_refs[0][...])
        for s in range(1, S):
            dn = dn + _dot_nt(da_refs[s][...], w_refs[s][...])
        dx, hh = _rmsnorm_bwd(h_ref[...], g_ref[...], dn)
        o_ref[...] = dh_ref[...] + dx
        _accumulate(dg_ref, pl.program_id(0) == 0, jnp.sum(dn * hh, axis=0, keepdims=True))

    row = pl.BlockSpec((tm, D), lambda i: (i, 0))
    vec = pl.BlockSpec((1, D), lambda i: (0, 0))
    part = lambda s: pl.BlockSpec((None, tm, K), lambda i: (s, i, 0))
    cols = lambda s: pl.BlockSpec((D, K), lambda i: (0, s), pipeline_mode=pl.Buffered(1))
    return _call(
        body, name=name, grid=(T // tm,),
        in_specs=[part(s) for s in range(S)] + [cols(s) for s in range(S)] + [row, vec, row],
        out_specs=[row, vec],
        out_shape=[SDS((T, D), F32), SDS((1, D), F32)],
        semantics=("arbitrary",), args=[da] * S + [w] * S + [h_in, g, dh_out], ride=ride)


def matmul_tn(a, b, *, tb, name, ride=None, ta=MXU_WIDTH):
    T, Ka = a.shape
    S, _, Nb = b.shape
    per = Nb // tb

    def body(a_ref, b_ref, o_ref):
        o_ref[...] = _dot_tn(a_ref[...], b_ref[...]).astype(BF16)

    out = _call(
        body, name=name, grid=(S * per, Ka // ta),
        in_specs=[pl.BlockSpec((T, ta), lambda j, i: (0, i)),
                  pl.BlockSpec((None, T, tb), lambda j, i: (j // per, 0, j % per))],
        out_specs=[pl.BlockSpec((ta, tb), lambda j, i: (i, j))],
        out_shape=[SDS((Ka, S * Nb), BF16)],
        semantics=("parallel", "parallel"), args=(a, b), ride=ride)
    return out[0] if ride is None else (out[0][0], out[1])


def conv_bwd(dy, bcx, conv_w, *, name, ride=None, tm=ROW_TILE):
    T, D = dy.shape
    nt = T // tm
    hb = tm // BF16_ROWS
    last = T // BF16_ROWS - 1

    def body(dy_ref, dyn_ref, b_ref, bn_ref, c_ref, u_ref, cp_ref, up_ref, cw_ref, o_ref, dw_ref):
        i = pl.program_id(0)
        c, u = c_ref[...].astype(F32), u_ref[...].astype(F32)
        cu = c * u
        cup = jnp.where(i == 0, 0.0, cp_ref[...].astype(F32) * up_ref[...].astype(F32))
        cu1, cu2 = _shift_down(cup, cu, 1), _shift_down(cup, cu, 2)
        w0, w1, w2 = cw_ref[0:1, :], cw_ref[1:2, :], cw_ref[2:3, :]
        dyf = dy_ref[...].astype(F32)
        o_ref[:, 0:D] = (dyf * (w0 * cu2 + w1 * cu1 + w2 * cu)).astype(BF16)
        dcv = dyf * b_ref[...].astype(F32)
        dcvn = jnp.where(i == nt - 1, 0.0, dyn_ref[...].astype(F32) * bn_ref[...].astype(F32))
        dcu = w2 * dcv + w1 * _shift_up(dcv, dcvn, 1) + w0 * _shift_up(dcv, dcvn, 2)
        o_ref[:, D:2 * D] = (dcu * u).astype(BF16)
        o_ref[:, 2 * D:3 * D] = (dcu * c).astype(BF16)
        row = lax.broadcasted_iota(jnp.int32, (8, D), 0)
        dw = jnp.zeros((8, D), F32)
        for tap, t in enumerate((cu2, cu1, cu)):
            dw = jnp.where(row == tap, jnp.sum(dcv * t, axis=0, keepdims=True), dw)
        _accumulate(dw_ref, i == 0, dw)

    tile = lambda col: pl.BlockSpec((tm, D), lambda i: (i, col))
    prev = lambda col: pl.BlockSpec((BF16_ROWS, D), lambda i: (jnp.maximum(i * hb - 1, 0), col))
    nxt = lambda col: pl.BlockSpec((BF16_ROWS, D), lambda i: (jnp.minimum((i + 1) * hb, last), col))
    return _call(
        body, name=name, grid=(nt,),
        in_specs=[tile(0), nxt(0), tile(0), nxt(0), tile(1), tile(2), prev(1), prev(2),
                  pl.BlockSpec((3, D), lambda i: (0, 0))],
        out_specs=[pl.BlockSpec((tm, 3 * D), lambda i: (i, 0)), pl.BlockSpec((8, D), lambda i: (0, 0))],
        out_shape=[SDS((T, 3 * D), BF16), SDS((8, D), F32)],
        semantics=("arbitrary",), args=(dy, dy, bcx, bcx, bcx, bcx, bcx, bcx, conv_w), ride=ride)


class NoTraffic:
    def ride(self, kernel_name):
        return None

    def landed(self, kernel_name, results, wts):
        pass

    def grad(self, key, value):
        pass


def local_step(x, target, wts, vec, traffic):
    T, D = x.shape
    tabs = rope_tables(T)
    small = {}

    def run(builder, *args, name, **kw):
        ride = traffic.ride(name)
        if ride is None:
            return builder(*args, name=name, **kw)
        out, extra = builder(*args, name=name, ride=ride, **kw)
        traffic.landed(name, extra, wts)
        return out

    bcx, xn1 = run(norm_matmul, x, vec["a_pre"], wts["w_in"], tn=3 * D, split=1, name="a_in")
    bcx = bcx[0]
    h1, z0, y0 = run(conv_mix_out, bcx, vec["conv_w"], wts["w_out"], vec["a_post"], x, name="a_out")
    gu0, act0, xt2 = run(norm_swiglu_in, h1, vec["ffn_pre0"], wts["gu0"], name="ffn0_in")
    h2, z1 = run(plain_mix_out, act0, wts["wd0"], vec["ffn_post0"], h1, name="ffn0_out")
    kvp, xkv = norm_matmul(h2, vec["kv_norm"], wts["w_kv"], tn=wts["w_kv"].shape[1], split=1, name="kv_in")
    qp, xq = norm_matmul(h2, vec["b_pre"], wts["w_q"], tn=D, split=1, name="q_in")
    kvp, qp = kvp[0], qp[0]
    (attn,) = run(attention_fwd, qp, kvp, tabs, vec["sinks"], name="attn_fwd")
    h3, z2 = plain_mix_out(attn, wts["w_o"], vec["b_post"], h2, name="attn_out")
    gu1, act1, xt3 = run(norm_swiglu_in, h3, vec["ffn_pre1"], wts["gu1"], name="ffn1_in")
    dy, z3, loss = plain_mix_out(act1, wts["wd1"], vec["ffn_post1"], h3, name="ffn1_out", target=target)

    def ffn_bwd(layer, z, gu, act, xt, h_in, dh, gu_first):
        tag = "ffn%d" % layer
        dz, small["ffn_post%d" % layer], dact = run(
            normbwd_matmul_nt, z, vec["ffn_post%d" % layer], dh, wts["wd%d" % layer], name=tag + "_out_bwd")
        dwd = lambda: traffic.grad("wd%d" % layer, run(matmul_tn, act, dz[None], tb=D, name=tag + "_dwd"))
        dwgu = lambda: traffic.grad("gu%d" % layer, run(swiglu_bwd_tn, xt, dact, gu, name=tag + "_dwgu"))
        for step in ((dwgu, dwd) if gu_first else (dwd, dwgu)):
            step()
        dh_in, small["ffn_pre%d" % layer] = run(
            swiglu_bwd_in, dact, gu, wts["gu%d" % layer], h_in, vec["ffn_pre%d" % layer], dh, name=tag + "_in_bwd")
        return dh_in

    dh3 = ffn_bwd(1, z3, gu1, act1, xt3, h3, dy, gu_first=False)
    dz2, small["b_post"], dattn = normbwd_matmul_nt(z2, vec["b_post"], dh3, wts["w_o"], name="attn_out_bwd")
    traffic.grad("w_o", matmul_tn(attn, dz2[None], tb=D, name="attn_dwo"))
    dq, dkv_cur, dkv_prev, small["sinks"] = run(attention_bwd, qp, kvp, tabs, vec["sinks"], dattn, name="attn_bwd")
    dkv = combine_dkv(dkv_cur, dkv_prev, name="attn_dkv")
    traffic.grad("w_q", matmul_tn(xq, dq[None], tb=D, name="attn_dwq"))
    traffic.grad("w_kv", matmul_tn(xkv, dkv[None], tb=dkv.shape[1], name="attn_dwkv"))
    dh2, small["b_pre"] = run(matmul_nt_normbwd, dq[None], wts["w_q"], h2, vec["b_pre"], dh3, name="q_in_bwd")
    dh2, small["kv_norm"] = matmul_nt_normbwd(dkv[None], wts["w_kv"], h2, vec["kv_norm"], dh2, name="kv_in_bwd")
    dh1 = ffn_bwd(0, z1, gu0, act0, xt2, h1, dh2, gu_first=True)
    dz0, small["a_post"], dyc = normbwd_matmul_nt(z0, vec["a_post"], dh1, wts["w_out"], name="a_out_bwd")
    traffic.grad("w_out", matmul_tn(y0, dz0[None], tb=D, name="a_dwout"))
    dbcx, small["conv_w"] = run(conv_bwd, dyc, bcx, vec["conv_w"], name="a_conv_bwd")
    traffic.grad("w_in", matmul_tn(xn1, dbcx[None], tb=3 * D // 2, name="a_dwin"))
    dx, small["a_pre"] = run(matmul_nt_normbwd, dbcx[None], wts["w_in"], x, vec["a_pre"], dh1, name="a_in_bwd")
    return loss, dx, small


SMALL_ROWS = 16
LOSS_ROW = 13

WHOLE = (0, 16)
GATHER_PLAN = {"cast_rest": [("w_in", WHOLE)],
               "a_in": [("w_out", WHOLE), ("gu0", (0, 9))],
               "a_out": [("gu0", (9, 7))],
               "ffn0_in": [("wd0", WHOLE), ("w_kv", WHOLE), ("w_q", WHOLE), ("w_o", WHOLE)],
               "ffn0_out": [("gu1", (0, 8))],
               "attn_fwd": [("gu1", (8, 8))],
               "ffn1_in": [("wd1", WHOLE)]}
PAIR_PLAN = {"ffn1_dwgu": ["wd1"], "ffn1_in_bwd": ["gu1"], "attn_bwd": ["w_o"], "q_in_bwd": ["w_q", "w_kv"],
             "ffn0_dwd": ["gu0"], "ffn0_in_bwd": ["wd0"], "a_conv_bwd": ["w_out"]}
PAIR_ALONE = ["w_in"]
CHIP_PLAN = {"ffn1_in_bwd": ["wd1"], "attn_bwd": ["gu1"], "ffn0_out_bwd": ["w_o", "w_q", "w_kv"],
             "ffn0_in_bwd": ["gu0"], "a_conv_bwd": ["wd0"], "a_in_bwd": ["w_out", "w_in"]}
GRAD_KIND = dict(KIND, gu0="split", gu1="split")


class Traffic:
    def __init__(self, wholes, quarter, c_arr):
        self.wholes, self.quarter, self.c_arr = wholes, quarter, c_arr
        self.views, self.sums, self.got = {}, {}, {}
        self.stages = {}

    def ride(self, name, small=None):
        rides, stages = [], []
        if name in GATHER_PLAN:
            plan = GATHER_PLAN[name]
            rides.append(gather_ride([self.wholes[k] for k, _ in plan],
                                     [(KIND[k], self.quarter[k], part) for k, part in plan], small))
            stages.append(("gather", [k for k, _ in plan]))
        if name in CHIP_PLAN:
            keys = CHIP_PLAN[name]
            rides.append(chip_ride([self.sums[k] for k in keys], [(GRAD_KIND[k], self.quarter[k]) for k in keys]))
            stages.append(("chip", keys))
        if name in PAIR_PLAN:
            keys = PAIR_PLAN[name]
            rides.append(pair_ride([self.views[k] for k in keys]))
            stages.append(("pair", keys))
        self.stages[name] = stages
        return join(rides)

    def landed(self, name, results, wts):
        results = list(results)
        for stage, keys in self.stages[name]:
            mine, results = results[:len(keys)], results[len(keys):]
            if stage == "gather":
                for k, whole in zip(keys, mine):
                    self.wholes[k] = wts[k] = whole
            elif stage == "chip":
                self.got.update(zip(keys, mine))
            else:
                for k, got in zip(keys, mine):
                    self.sums[k] = pair_add(self.views[k], got, self.c_arr, name="pair_add_" + k)

    def grad(self, key, value):
        r, ws = self.quarter[key]
        view = {"row": (N_CHIPS, 2, r // 2, ws), "col": (1, 2, r // 2, N_CHIPS * ws), "split": (2, 2, r // 2, 2 * ws)}
        self.views[key] = value.reshape(view[GRAD_KIND[key]])
        if key in PAIR_ALONE:
            (got,) = alone(pair_ride([self.views[key]]), name="pair_exchange_" + key)
            self.sums[key] = pair_add(self.views[key], got, self.c_arr, name="pair_add_" + key)


def kernel(x, a_pre_norm, a_w_in, a_conv_w, a_w_out, a_post_norm, ffn_pre_norm, ffn_w_gate_up, ffn_w_down, ffn_post_norm, kv_norm, w_kv, b_pre_norm, b_w_q, b_sinks, b_w_o, b_post_norm, loss_target, m_a_pre_norm, m_a_w_in, m_a_conv_w, m_a_w_out, m_a_post_norm, m_ffn_pre_norm, m_ffn_w_gate_up, m_ffn_w_down, m_ffn_post_norm, m_kv_norm, m_w_kv, m_b_pre_norm, m_b_w_q, m_b_sinks, m_b_w_o, m_b_post_norm, v_a_pre_norm, v_a_w_in, v_a_conv_w, v_a_w_out, v_a_post_norm, v_ffn_pre_norm, v_ffn_w_gate_up, v_ffn_w_down, v_ffn_post_norm, v_kv_norm, v_w_kv, v_b_pre_norm, v_b_w_q, v_b_sinks, v_b_w_o, v_b_post_norm):
    T, D = x.shape[1], x.shape[2]
    xi, yi, ci = _place()
    p = 2 * xi + yi
    p_arr = jnp.reshape(p, (1,)).astype(jnp.int32)
    c_arr = jnp.reshape(ci, (1,)).astype(jnp.int32)
    pc_arr = jnp.stack([p, ci]).astype(jnp.int32)
    me_arr = jnp.reshape(4 * xi + 2 * yi + ci, (1,)).astype(jnp.int32)
    qd = D // N_CHIPS

    big = {"w_in": (a_w_in, 0), "w_out": (a_w_out, 0), "gu0": (ffn_w_gate_up, 0), "gu1": (ffn_w_gate_up, 1),
           "wd0": (ffn_w_down, 0), "wd1": (ffn_w_down, 1), "w_kv": (w_kv[None], 0), "w_q": (b_w_q, 0),
           "w_o": (b_w_o, 0)}
    names = list(big)
    quarter = {k: w.shape[1:] for k, (w, _) in big.items()}
    source = lambda k: big[k] + (KIND[k],)
    traffic = Traffic(dict(zip(names[:1], cast_quarters([source(names[0])], p_arr, name="cast_first"))), quarter, c_arr)
    small_shard = jnp.concatenate([a_pre_norm, a_post_norm, a_conv_w[0], jnp.zeros((3, qd), F32)], axis=0)
    wts = {}
    rest, (*landed, small_full) = cast_quarters([source(k) for k in names[1:]], p_arr, name="cast_rest",
                                                ride=traffic.ride("cast_rest", small_shard))
    traffic.wholes.update(zip(names[1:], rest))
    traffic.landed("cast_rest", landed, wts)
    rows = lambda k: jnp.transpose(small_full[:, k], (1, 0, 2)).reshape(-1, D)
    vec = {"a_pre": rows(slice(0, 1)), "a_post": rows(slice(1, 2)), "conv_w": rows(slice(2, 5)),
           "ffn_pre0": ffn_pre_norm[0:1], "ffn_pre1": ffn_pre_norm[1:2],
           "ffn_post0": ffn_post_norm[0:1], "ffn_post1": ffn_post_norm[1:2],
           "kv_norm": kv_norm[None], "b_pre": b_pre_norm, "b_post": b_post_norm, "sinks": b_sinks}

    loss, dx, small = local_step(x[0], loss_target[0], wts, vec, traffic)

    pad = lambda a: jnp.pad(a, ((0, 0), (0, D - a.shape[1])))
    small_block = jnp.concatenate(
        [small["a_pre"], small["a_post"], small["conv_w"][0:3], small["ffn_pre0"], small["ffn_pre1"],
         small["ffn_post0"], small["ffn_post1"], small["kv_norm"], small["b_pre"], small["b_post"],
         pad(small["sinks"][0:1]), pad(loss[0:1]), jnp.zeros((SMALL_ROWS - LOSS_ROW - 1, D), F32)], axis=0)
    (small_blocks,) = alone(chip_ride([], [], small_block), name="small_exchange")
    halves = chip_reduce([traffic.sums[k] for k in names], [traffic.got[k] for k in names],
                         [GRAD_KIND[k] for k in names], pc_arr, name="chip_reduce")
    grad = dict(zip(names, [q.reshape(quarter[k]) for k, q in zip(names, half_exchange(halves))]))
    small_sum = small_reduce(small_blocks, me_arr)

    out = {}
    out["a_w_in"] = adamw(a_w_in, [grad["w_in"]], m_a_w_in, v_a_w_in, name="adamw_a_w_in")
    out["a_w_out"] = adamw(a_w_out, [grad["w_out"]], m_a_w_out, v_a_w_out, name="adamw_a_w_out")
    out["ffn_w_gate_up"] = adamw(ffn_w_gate_up, [grad["gu0"], grad["gu1"]], m_ffn_w_gate_up, v_ffn_w_gate_up,
                                 name="adamw_ffn_w_gate_up")
    out["ffn_w_down"] = adamw(ffn_w_down, [grad["wd0"], grad["wd1"]], m_ffn_w_down, v_ffn_w_down,
                              name="adamw_ffn_w_down")
    out["w_kv"] = [o[0] for o in adamw(w_kv[None], [grad["w_kv"]], m_w_kv[None], v_w_kv[None], name="adamw_w_kv")]
    out["b_w_q"] = adamw(b_w_q, [grad["w_q"]], m_b_w_q, v_b_w_q, name="adamw_b_w_q")
    out["b_w_o"] = adamw(b_w_o, [grad["w_o"]], m_b_w_o, v_b_w_o, name="adamw_b_w_o")

    def pack(a_pre, a_post, conv, ffn_pre, ffn_post, kvn, b_pre, b_post, sinks):
        return jnp.concatenate([pad(a_pre), pad(a_post), pad(conv[0]), ffn_pre, ffn_post, kvn[None], b_pre, b_post,
                                pad(sinks), jnp.zeros((SMALL_ROWS - 13, D), F32)], axis=0)

    g_small = jnp.concatenate([pad(lax.dynamic_slice(small_sum, (0, p * qd), (5, qd))), small_sum[5:]], axis=0)
    w_small = pack(a_pre_norm, a_post_norm, a_conv_w, ffn_pre_norm, ffn_post_norm, kv_norm, b_pre_norm, b_post_norm,
                   b_sinks)
    m_small = pack(m_a_pre_norm, m_a_post_norm, m_a_conv_w, m_ffn_pre_norm, m_ffn_post_norm, m_kv_norm,
                   m_b_pre_norm, m_b_post_norm, m_b_sinks)
    v_small = pack(v_a_pre_norm, v_a_post_norm, v_a_conv_w, v_ffn_pre_norm, v_ffn_post_norm, v_kv_norm,
                   v_b_pre_norm, v_b_post_norm, v_b_sinks)
    packed = adamw(w_small[None], [g_small], m_small[None], v_small[None], name="adamw_small")
    ns = b_sinks.shape[1]
    unpack = lambda a: {"a_pre_norm": a[0:1, :qd], "a_post_norm": a[1:2, :qd], "a_conv_w": a[None, 2:5, :qd],
                        "ffn_pre_norm": a[5:7], "ffn_post_norm": a[7:9], "kv_norm": a[9], "b_pre_norm": a[10:11],
                        "b_post_norm": a[11:12], "b_sinks": a[12:13, :ns]}
    unpacked = [unpack(a[0]) for a in packed]
    for k in unpacked[0]:
        out[k] = [u[k] for u in unpacked]

    order = ["a_pre_norm", "a_w_in", "a_conv_w", "a_w_out", "a_post_norm", "ffn_pre_norm", "ffn_w_gate_up",
             "ffn_w_down", "ffn_post_norm", "kv_norm", "w_kv", "b_pre_norm", "b_w_q", "b_sinks", "b_w_o",
             "b_post_norm"]
    return (small_sum[LOSS_ROW, 0], dx[None], *[out[k][0] for k in order], *[out[k][1] for k in order],
            *[out[k][2] for k in order], *[out[k][3] for k in order])
```

```python
import math

import jax
import jax.numpy as jnp
from jax import lax
from jax.experimental import pallas as pl
from jax.experimental.pallas import tpu as pltpu

F32 = jnp.float32
BF16 = jnp.bfloat16
SDS = jax.ShapeDtypeStruct
MESH = pl.DeviceIdType.MESH
DMA = pltpu.SemaphoreType.DMA
HBM_SPEC = pl.BlockSpec(memory_space=pltpu.HBM)

EPS = 1e-6
NEG = -1e30
HEAD_DIM = 64
N_KV_HEADS = 4
BLOCK = 128
ROT_DIM = HEAD_DIM // 4
ROPE_THETA = 500000.0
N_CHIPS = 4

ADAM_LR = 0.001
ADAM_B1 = 0.9
ADAM_B2 = 0.999
ADAM_EPS = 1e-08
ADAM_WD = 0.01
ADAM_STEP = 10

VMEM_LIMIT_BYTES = 52 * 1024 * 1024
ROW_TILE = 512
BF16_ROWS = 16
MXU_WIDTH = 256

KIND = {"w_in": "col", "gu0": "col", "gu1": "col", "w_out": "row", "wd0": "row", "wd1": "row", "w_kv": "row",
        "w_q": "row", "w_o": "row"}


def _params(*semantics):
    return pltpu.CompilerParams(dimension_semantics=semantics, vmem_limit_bytes=VMEM_LIMIT_BYTES)


def _row_tile(rows, limit, step=8):
    return max(t for t in range(step, limit + 1, step) if rows % t == 0)


def _place():
    return lax.axis_index("x"), lax.axis_index("y"), lax.axis_index("c")


def _other_chips(x, y):
    return [(1 - x, y), (x, 1 - y), (1 - x, 1 - y)]


def _remote(src, dst, send_sem, recv_sem, to):
    return pltpu.make_async_remote_copy(src_ref=src, dst_ref=dst, send_sem=send_sem, recv_sem=recv_sem,
                                        device_id=to, device_id_type=MESH)


def _full_shape(kind, quarter):
    r, ws = quarter
    return (N_CHIPS * r, ws) if kind == "row" else (r, N_CHIPS * ws)


def _half_of_quarter(ref, kind, quarter, sixteenths, q, half):
    r, ws = quarter
    h = r // 2
    lo, n = sixteenths[0] * h // 16, sixteenths[1] * h // 16
    assert lo % BF16_ROWS == 0 and n % BF16_ROWS == 0, (quarter, sixteenths)
    if kind == "row":
        return ref.at[pl.ds(pl.multiple_of(q * r + half * h + lo, BF16_ROWS), n)]
    return ref.at[pl.ds(pl.multiple_of(half * h + lo, BF16_ROWS), n), pl.ds(pl.multiple_of(q * ws, 128), ws)]


class Ride:
    def __init__(self, operands, out_shape, aliases, sems, make):
        self.operands, self.out_shape, self.aliases, self.sems, self.make = operands, out_shape, aliases, sems, make


def join(rides):
    rides = [r for r in rides if r is not None]
    if len(rides) < 2:
        return rides[0] if rides else None
    aliases, at = {}, [0, 0, 0]
    cuts = []
    for r in rides:
        aliases.update({at[0] + i: at[1] + o for i, o in r.aliases.items()})
        cuts.append(tuple(at))
        at = [at[0] + len(r.operands), at[1] + len(r.out_shape), at[2] + len(r.sems)]
    cuts.append(tuple(at))

    def make(ins, outs, sem):
        made = [r.make(ins[lo[0]:hi[0]], outs[lo[1]:hi[1]], sem[lo[2]:hi[2]]) for r, lo, hi in zip(rides, cuts, cuts[1:])]

        def start():
            for s, _ in made:
                s()

        def finish():
            for _, f in made:
                f()

        return start, finish

    return Ride(sum((list(r.operands) for r in rides), []), sum((list(r.out_shape) for r in rides), []), aliases,
                sum((list(r.sems) for r in rides), []), make)


def _call(body, *, name, grid, in_specs, out_specs, out_shape, args, scratch_shapes=(), semantics=None, ride=None,
          prefetch=None):
    pre = 0 if prefetch is None else 1
    n_in, n_out, n_scr = len(in_specs), len(out_specs), len(scratch_shapes)
    r_in, r_out = (len(ride.operands), len(ride.out_shape)) if ride is not None else (0, 0)
    a, b = pre + n_in, pre + n_in + r_in
    c, d = b + n_out, b + n_out + r_out
    e = d + n_scr

    def riding(*refs):
        start, finish = ride.make(refs[a:b], refs[c:d], refs[e:])
        ids = [pl.program_id(k) for k in range(len(grid))]
        first, last = ids[0] == 0, ids[0] == grid[0] - 1
        for k in range(1, len(grid)):
            first, last = first & (ids[k] == 0), last & (ids[k] == grid[k] - 1)
        pl.when(first)(start)
        body(*refs[:a], *refs[b:c], *refs[d:e])
        pl.when(last)(finish)

    if ride is None:
        kernel_body, extra_in, extra_out, extra_shape, extra_scr, aliases = body, [], [], [], [], {}
        params = _params(*semantics)
    else:
        kernel_body, extra_in, extra_out = riding, [HBM_SPEC] * r_in, [HBM_SPEC] * r_out
        extra_shape, extra_scr = list(ride.out_shape), list(ride.sems)
        aliases = {pre + n_in + i: n_out + o for i, o in ride.aliases.items()}
        params = _params(*(("arbitrary",) * len(grid)))
    specs = dict(grid=grid, in_specs=list(in_specs) + extra_in, out_specs=list(out_specs) + extra_out,
                 scratch_shapes=list(scratch_shapes) + extra_scr)
    if prefetch is not None:
        specs = dict(grid_spec=pltpu.PrefetchScalarGridSpec(num_scalar_prefetch=1, **specs))
        args = (prefetch,) + tuple(args)
    outs = pl.pallas_call(kernel_body, name=name, out_shape=list(out_shape) + extra_shape,
                          input_output_aliases=aliases, compiler_params=params, **specs,
                          )(*args, *(ride.operands if ride is not None else ()))
    return outs if ride is None else (outs[:n_out], outs[n_out:])


def alone(ride, *, name):
    def body(*refs):
        n = len(ride.operands)
        start, finish = ride.make(refs[:n], refs[n:n + len(ride.out_shape)], refs[n + len(ride.out_shape):])
        start()
        finish()

    return pl.pallas_call(
        body, name=name, in_specs=[HBM_SPEC] * len(ride.operands), out_specs=[HBM_SPEC] * len(ride.out_shape),
        out_shape=list(ride.out_shape), input_output_aliases=dict(ride.aliases), scratch_shapes=list(ride.sems),
    )(*ride.operands)


def gather_ride(wholes, metas, small=None):
    n = len(wholes)
    operands, out_shape = list(wholes), [SDS(s.shape, s.dtype) for s in wholes]
    sems = [DMA((n, 3)), DMA((n, 3)), DMA((n, 3)), DMA((n, 3))]
    if small is not None:
        operands.append(small)
        out_shape.append(SDS((N_CHIPS,) + small.shape, small.dtype))
        sems += [DMA((3,)), DMA((3,)), DMA(())]

    def make(ins, outs, sem):
        send1, recv1, send2, recv2 = sem[:4]
        x, y, c = _place()
        p = 2 * x + y
        chips = _other_chips(x, y)
        me, sibling = (x, y, c), (x, y, 1 - c)
        part = lambda t, q, half: _half_of_quarter(outs[t], *metas[t], q, half)
        first = []
        for j, (qx, qy) in enumerate(chips):
            if small is not None:
                first.append(_remote(ins[n], outs[n].at[p], sem[4].at[j], sem[5].at[j], (qx, qy, c)))
            for t in range(n):
                first.append(_remote(part(t, p, c), part(t, p, c), send1.at[t, j], recv1.at[t, j], (qx, qy, c)))
        local = [] if small is None else [pltpu.make_async_copy(ins[n], outs[n].at[p], sem[6])]

        def start():
            for cp in local + first:
                cp.start()

        def finish():
            passed = []
            for j, (qx, qy) in enumerate(chips):
                q = 2 * qx + qy
                for t in range(n):
                    landed = part(t, q, c)
                    _remote(landed, landed, send1.at[t, j], recv1.at[t, j], me).wait_recv()
                    cp = _remote(landed, landed, send2.at[t, j], recv2.at[t, j], sibling)
                    cp.start()
                    passed.append(cp)
            for j, (qx, qy) in enumerate(chips):
                q = 2 * qx + qy
                if small is not None:
                    _remote(outs[n].at[q], outs[n].at[q], sem[4].at[j], sem[5].at[j], me).wait_recv()
                for t in range(n):
                    theirs = part(t, q, 1 - c)
                    _remote(theirs, theirs, send2.at[t, j], recv2.at[t, j], me).wait_recv()
            for cp in first + passed:
                cp.wait_send()
            for cp in local:
                cp.wait()

        return start, finish

    return Ride(operands, out_shape, {t: t for t in range(n)}, sems, make)


def chip_ride(sums, metas, small=None):
    n = len(sums)
    operands = list(sums)
    out_shape = [SDS((3, s.shape[1], quarter[1]), s.dtype) for s, (_, quarter) in zip(sums, metas)]
    sems = [DMA((n, 3)), DMA((n, 3))] if n else []
    if small is not None:
        operands.append(small)
        out_shape.append(SDS((8,) + small.shape, small.dtype))
        sems += [DMA((7,)), DMA((7,)), DMA(())]

    def make(ins, outs, sem):
        x, y, c = _place()
        cps = []
        for j, (qx, qy) in enumerate(_other_chips(x, y)):
            q = 2 * qx + qy
            for t in range(n):
                kind, (_, ws) = metas[t]
                if kind == "row":
                    src = ins[t].at[q]
                elif kind == "col":
                    src = ins[t].at[0, :, pl.ds(pl.multiple_of(q * ws, 128), ws)]
                else:
                    src = ins[t].at[q // 2, :, pl.ds(pl.multiple_of((q % 2) * ws, 128), ws)]
                cps.append(_remote(src, outs[t].at[j], sem[0].at[t, j], sem[1].at[t, j], (qx, qy, c)))
        local = []
        if small is not None:
            ssend, srecv, lsem = sem[-3:]
            local.append(pltpu.make_async_copy(ins[n], outs[n].at[0], lsem))
            for k in range(1, 8):
                peer = (x ^ (k >> 2 & 1), y ^ (k >> 1 & 1), c ^ (k & 1))
                cps.append(_remote(ins[n], outs[n].at[k], ssend.at[k - 1], srecv.at[k - 1], peer))

        def start():
            for cp in local + cps:
                cp.start()

        def finish():
            for cp in cps + local:
                cp.wait()

        return start, finish

    return Ride(operands, out_shape, {}, sems, make)


def pair_ride(grads):
    n = len(grads)

    def make(ins, outs, sem):
        x, y, c = _place()
        cps = [_remote(ins[t].at[:, 1 - c], outs[t], sem[0].at[t], sem[1].at[t], (x, y, 1 - c)) for t in range(n)]

        def start():
            for cp in cps:
                cp.start()

        def finish():
            for cp in cps:
                cp.wait()

        return start, finish

    return Ride(list(grads), [SDS((g.shape[0],) + g.shape[2:], g.dtype) for g in grads], {}, [DMA((n,)), DMA((n,))],
                make)


def half_ride(quarters):
    n = len(quarters)

    def make(ins, outs, sem):
        x, y, c = _place()
        sends = [_remote(outs[t].at[c], outs[t].at[c], sem[0].at[t], sem[1].at[t], (x, y, 1 - c)) for t in range(n)]

        def start():
            for cp in sends:
                cp.start()

        def finish():
            for t in range(n):
                theirs = outs[t].at[1 - c]
                _remote(theirs, theirs, sem[0].at[t], sem[1].at[t], (x, y, c)).wait_recv()
            for cp in sends:
                cp.wait_send()

        return start, finish

    return Ride(list(quarters), [SDS(q.shape, q.dtype) for q in quarters], {t: t for t in range(n)},
                [DMA((n,)), DMA((n,))], make)


CAST_STEPS = 4


def cast_quarters(sources, p_arr, *, name, ride=None):
    n = len(sources)
    in_specs, out_specs, out_shape = [], [], []
    for w, layer, kind in sources:
        _, r, ws = w.shape
        tr = r // CAST_STEPS
        assert tr % BF16_ROWS == 0, w.shape
        in_specs.append(pl.BlockSpec((None, tr, ws), lambda i, p_ref, layer=layer: (layer, i, 0)))
        out_specs.append(pl.BlockSpec((tr, ws), (lambda i, p_ref: (p_ref[0] * CAST_STEPS + i, 0)) if kind == "row"
                                      else (lambda i, p_ref: (i, p_ref[0]))))
        out_shape.append(SDS(_full_shape(kind, (r, ws)), BF16))

    def body(p_ref, *refs):
        for w_ref, o_ref in zip(refs[:n], refs[n:]):
            o_ref[...] = w_ref[...].astype(BF16)

    return _call(body, name=name, grid=(CAST_STEPS,), in_specs=in_specs, out_specs=out_specs, out_shape=out_shape,
                 semantics=("parallel",), args=[w for w, _, _ in sources], ride=ride, prefetch=p_arr)


def pair_add(own, got, c_arr, *, name):
    A, _, h, W = own.shape
    th = _row_tile(h, max(BF16_ROWS, (3 << 19) // W), BF16_ROWS)

    def body(c_ref, a_ref, b_ref, o_ref):
        o_ref[...] = (a_ref[...].astype(F32) + b_ref[...].astype(F32)).astype(BF16)

    return pl.pallas_call(
        body, name=name,
        grid_spec=pltpu.PrefetchScalarGridSpec(
            num_scalar_prefetch=1, grid=(A, h // th),
            in_specs=[pl.BlockSpec((None, None, th, W), lambda q, i, c_ref: (q, c_ref[0], i, 0)),
                      pl.BlockSpec((None, th, W), lambda q, i, c_ref: (q, i, 0))],
            out_specs=pl.BlockSpec((None, th, W), lambda q, i, c_ref: (q, i, 0))),
        out_shape=SDS((A, h, W), BF16),
        compiler_params=_params("parallel", "parallel"),
    )(c_arr, own, got)


REDUCE_STEPS = 2


def chip_reduce(sums, got, kinds, pc_arr, *, name):
    n = len(sums)
    mine = {"row": lambda i, pc_ref: (pc_ref[0], i, 0), "col": lambda i, pc_ref: (0, i, pc_ref[0]),
            "split": lambda i, pc_ref: (pc_ref[0] // 2, i, pc_ref[0] % 2)}
    a_specs, b_specs, o_specs, out_shape = [], [], [], []
    for g, kind in zip(got, kinds):
        _, h, ws = g.shape
        th = h // REDUCE_STEPS
        assert th % BF16_ROWS == 0, g.shape
        a_specs.append(pl.BlockSpec((None, th, ws), mine[kind]))
        b_specs.append(pl.BlockSpec((3, th, ws), lambda i, pc_ref: (0, i, 0)))
        o_specs.append(pl.BlockSpec((None, th, ws), lambda i, pc_ref: (pc_ref[1], i, 0)))
        out_shape.append(SDS((2, h, ws), F32))

    def body(pc_ref, *refs):
        for a_ref, b_ref, o_ref in zip(refs[:n], refs[n:2 * n], refs[2 * n:]):
            o_ref[...] = ((a_ref[...].astype(F32) + b_ref[0].astype(F32)) + b_ref[1].astype(F32)) + b_ref[2].astype(F32)

    return _call(body, name=name, grid=(REDUCE_STEPS,), in_specs=a_specs + b_specs, out_specs=o_specs,
                 out_shape=out_shape, semantics=("parallel",), args=list(sums) + list(got), prefetch=pc_arr)


def small_reduce(blocks, me_arr):
    _, rows, D = blocks.shape

    def body(me_ref, b_ref, o_ref):
        me = me_ref[0]
        total = b_ref[me]
        for d in range(1, 8):
            total = total + b_ref[d ^ me]
        o_ref[...] = total

    return pl.pallas_call(
        body, name="small_reduce",
        grid_spec=pltpu.PrefetchScalarGridSpec(
            num_scalar_prefetch=1, grid=(1,),
            in_specs=[pl.BlockSpec((8, rows, D), lambda i, me_ref: (0, 0, 0))],
            out_specs=pl.BlockSpec((rows, D), lambda i, me_ref: (0, 0))),
        out_shape=SDS((rows, D), F32),
        compiler_params=_params("arbitrary"),
    )(me_arr, blocks)


def adamw(w, gs, m, v, *, name):
    L, r, cols = w.shape
    tr = _row_tile(r, 256)
    nt = r // tr

    def body(*refs):
        w_ref, m_ref, v_ref = refs[:3]
        g_refs = refs[3:3 + L]
        g_out, d_out, m_out, v_out = refs[3 + L:]
        layer = pl.program_id(0)
        g = g_refs[0][...]
        for l in range(1, L):
            g = jnp.where(layer == l, g_refs[l][...], g)
        m_new = ADAM_B1 * m_ref[...] + (1.0 - ADAM_B1) * g
        v_new = ADAM_B2 * v_ref[...] + (1.0 - ADAM_B2) * (g * g)
        m_hat = m_new / (1.0 - ADAM_B1 ** ADAM_STEP)
        v_hat = v_new / (1.0 - ADAM_B2 ** ADAM_STEP)
        g_out[...] = g
        m_out[...] = m_new
        v_out[...] = v_new
        d_out[...] = -ADAM_LR * (m_hat / (jnp.sqrt(v_hat) + ADAM_EPS) + ADAM_WD * w_ref[...])

    full = pl.BlockSpec((None, tr, cols), lambda l, i: (l, i, 0))
    g_spec = lambda l0: pl.BlockSpec((tr, cols), lambda l, i: (jnp.where(l == l0, i, jnp.where(l < l0, 0, nt - 1)), 0))
    return pl.pallas_call(
        body, name=name, grid=(L, nt),
        in_specs=[full, full, full] + [g_spec(l0) for l0 in range(L)],
        out_specs=[full] * 4,
        out_shape=[SDS(w.shape, F32)] * 4,
        compiler_params=_params("arbitrary", "arbitrary"),
    )(w, m, v, *gs)


def _rms_r(xf):
    return lax.rsqrt(jnp.mean(xf * xf, axis=-1, keepdims=True) + EPS)


def _rmsnorm_bwd(xf, g, dy):
    r = _rms_r(xf)
    xh = xf * r
    gd = g * dy
    return r * (gd - xh * jnp.mean(xh * gd, axis=-1, keepdims=True)), xh


def _dot(a, b):
    return jnp.dot(a, b, preferred_element_type=F32)


def _dot_nt(a, b):
    return lax.dot_general(a, b, (((1,), (1,)), ((), ())), preferred_element_type=F32)


def _dot_tn(a, b):
    return lax.dot_general(a, b, (((0,), (0,)), ((), ())), preferred_element_type=F32)


def _accumulate(ref, first, value):
    @pl.when(first)
    def _():
        ref[...] = value

    @pl.when(jnp.logical_not(first))
    def _():
        ref[...] += value


def norm_matmul(x, g, w, *, tn, split, name, ride=None, tm=ROW_TILE):
    T, D = x.shape
    N = w.shape[1]
    per = N // split // tn

    def body(x_ref, g_ref, w_ref, o_ref, xn_ref):
        @pl.when(pl.program_id(1) == 0)
        def _():
            xf = x_ref[...]
            xn_ref[...] = (xf * _rms_r(xf) * g_ref[...]).astype(BF16)

        o_ref[...] = _dot(xn_ref[...], w_ref[...]).astype(BF16)

    return _call(
        body, name=name, grid=(T // tm, N // tn),
        in_specs=[pl.BlockSpec((tm, D), lambda i, j: (i, 0)),
                  pl.BlockSpec((1, D), lambda i, j: (0, 0)),
                  pl.BlockSpec((D, tn), lambda i, j: (0, j))],
        out_specs=[pl.BlockSpec((None, tm, tn), lambda i, j: (j // per, i, j % per)),
                   pl.BlockSpec((tm, D), lambda i, j: (i, 0))],
        out_shape=[SDS((split, T, N // split), BF16), SDS((T, D), BF16)],
        semantics=("parallel", "arbitrary"), args=(x, g, w), ride=ride)


BIG_ROW_TILE = 1024


def norm2_matmul(x, gains, weights, *, name, tm=BIG_ROW_TILE):
    T, D = x.shape
    tm = min(tm, T)
    n = len(gains)

    def body(x_ref, *refs):
        xf = x_ref[...]
        xh = xf * _rms_r(xf)
        for g_ref, w_ref, o_ref, xn_ref in zip(refs[:n], refs[n:2 * n], refs[2 * n::2], refs[2 * n + 1::2]):
            xn = (xh * g_ref[...]).astype(BF16)
            xn_ref[...] = xn
            o_ref[...] = _dot(xn, w_ref[...]).astype(BF16)

    row = pl.BlockSpec((tm, D), lambda i: (i, 0))
    vec = pl.BlockSpec((1, D), lambda i: (0, 0))
    out_specs, out_shape = [], []
    for w in weights:
        out_specs += [pl.BlockSpec((tm, w.shape[1]), lambda i: (i, 0)), row]
        out_shape += [SDS((T, w.shape[1]), BF16), SDS((T, D), BF16)]
    return _call(
        body, name=name, grid=(T // tm,),
        in_specs=[row] + [vec] * n + [pl.BlockSpec(w.shape, lambda i: (0, 0)) for w in weights],
        out_specs=out_specs, out_shape=out_shape, semantics=("parallel",), args=[x] + list(gains) + list(weights))


def _shift_down(prev, cur, by):
    big = jnp.concatenate([prev, cur], axis=0)
    return pltpu.roll(big, by, 0)[prev.shape[0]:]


def _shift_up(cur, nxt, by):
    big = jnp.concatenate([cur, nxt], axis=0)
    return pltpu.roll(big, big.shape[0] - by, 0)[:cur.shape[0]]


def conv_mix_out(bcx, conv_w, w_out, g_post, res, *, name, ride=None, tm=ROW_TILE):
    T, D = res.shape
    hb = tm // BF16_ROWS

    def body(b_ref, c_ref, u_ref, cp_ref, up_ref, cw_ref, w_ref, g_ref, r_ref, h_ref, z_ref, y_ref):
        i = pl.program_id(0)
        cu = c_ref[...].astype(F32) * u_ref[...].astype(F32)
        cup = cp_ref[...].astype(F32) * up_ref[...].astype(F32)
        cup = jnp.where(i == 0, 0.0, cup)
        cv = (cw_ref[0:1, :] * _shift_down(cup, cu, 2) + cw_ref[1:2, :] * _shift_down(cup, cu, 1)
              + cw_ref[2:3, :] * cu)
        y = (b_ref[...].astype(F32) * cv).astype(BF16)
        y_ref[...] = y
        z = _dot(y, w_ref[...])
        z_ref[...] = z.astype(BF16)
        h_ref[...] = r_ref[...] + z * _rms_r(z) * g_ref[...]

    tile = lambda col: pl.BlockSpec((tm, D), lambda i: (i, col))
    halo = lambda col: pl.BlockSpec((BF16_ROWS, D), lambda i: (jnp.maximum(i * hb - 1, 0), col))
    row = pl.BlockSpec((tm, D), lambda i: (i, 0))
    return _call(
        body, name=name, grid=(T // tm,),
        in_specs=[tile(0), tile(1), tile(2), halo(1), halo(2),
                  pl.BlockSpec((3, D), lambda i: (0, 0)),
                  pl.BlockSpec((D, D), lambda i: (0, 0)),
                  pl.BlockSpec((1, D), lambda i: (0, 0)), row],
        out_specs=[row, row, row],
        out_shape=[SDS((T, D), F32), SDS((T, D), BF16), SDS((T, D), BF16)],
        semantics=("parallel",), args=(bcx, bcx, bcx, bcx, bcx, conv_w, w_out, g_post, res), ride=ride)


def plain_mix_out(a, w, g_post, res, *, name, target=None, ride=None, tm=ROW_TILE):
    T, D = res.shape
    tm = min(tm, T)
    K = a.shape[1]
    with_loss = target is not None

    def body(a_ref, w_ref, g_ref, r_ref, *rest):
        z = _dot(a_ref[...], w_ref[...])
        h = r_ref[...] + z * _rms_r(z) * g_ref[...]
        if with_loss:
            t_ref, h_ref, z_ref, loss_ref = rest
            diff = h - t_ref[...]
            h_ref[...] = diff * (1.0 / D)
            part = jnp.full(loss_ref.shape, 0.5 / D, F32) * jnp.sum(diff * diff)
            _accumulate(loss_ref, pl.program_id(0) == 0, part)
        else:
            h_ref, z_ref = rest
            h_ref[...] = h
        z_ref[...] = z.astype(BF16)

    row = pl.BlockSpec((tm, D), lambda i: (i, 0))
    loss_spec, loss_shape = pl.BlockSpec((8, 128), lambda i: (0, 0)), SDS((8, 128), F32)
    return _call(
        body, name=name, grid=(T // tm,),
        in_specs=[pl.BlockSpec((tm, K), lambda i: (i, 0)),
                  pl.BlockSpec((K, D), lambda i: (0, 0)),
                  pl.BlockSpec((1, D), lambda i: (0, 0)), row] + [row] * with_loss,
        out_specs=[row, row] + [loss_spec] * with_loss,
        out_shape=[SDS((T, D), F32), SDS((T, D), BF16)] + [loss_shape] * with_loss,
        semantics=("arbitrary",), args=(a, w, g_post, res) + ((target,) if with_loss else ()), ride=ride)


def _silu_grads(d, g, u):
    sg = jax.nn.sigmoid(g)
    return d * u * (sg * (1.0 + g * (1.0 - sg))), d * (g * sg)


def norm_swiglu_in(x, g, w, *, name, ride=None, tm=ROW_TILE // 2):
    T, D = x.shape
    F = w.shape[1] // 2

    def body(x_ref, g_ref, wg_ref, wu_ref, gu_ref, a_ref, xt_ref):
        xf = x_ref[...]
        xn = xf * _rms_r(xf) * g_ref[...]
        xt_ref[...] = xn.T.astype(BF16)
        xb = xn.astype(BF16)
        gate = _dot(xb, wg_ref[...]).astype(BF16)
        up = _dot(xb, wu_ref[...]).astype(BF16)
        gu_ref[0] = gate
        gu_ref[1] = up
        a_ref[...] = gate * jax.nn.sigmoid(gate) * up

    half = lambda s: pl.BlockSpec((D, F), lambda i: (0, s), pipeline_mode=pl.Buffered(1))
    return _call(
        body, name=name, grid=(T // tm,),
        in_specs=[pl.BlockSpec((tm, D), lambda i: (i, 0)), pl.BlockSpec((1, D), lambda i: (0, 0)), half(0), half(1)],
        out_specs=[pl.BlockSpec((2, tm, F), lambda i: (0, i, 0)), pl.BlockSpec((tm, F), lambda i: (i, 0)),
                   pl.BlockSpec((D, tm), lambda i: (0, i))],
        out_shape=[SDS((2, T, F), BF16), SDS((T, F), BF16), SDS((D, T), BF16)],
        semantics=("parallel",), args=(x, g, w, w), ride=ride)


def swiglu_bwd_tn(xt, dact, gu, *, name, ride=None, tb=MXU_WIDTH):
    D, T = xt.shape
    F = dact.shape[1]

    def body(xt_ref, d_ref, g_ref, u_ref, o_ref):
        dg, du = _silu_grads(d_ref[...], g_ref[...], u_ref[...])
        o_ref[0] = _dot(xt_ref[...], dg).astype(BF16)
        o_ref[1] = _dot(xt_ref[...], du).astype(BF16)

    col = lambda s: pl.BlockSpec((None, T, tb), lambda j: (s, 0, j))
    out = _call(
        body, name=name, grid=(F // tb,),
        in_specs=[pl.BlockSpec((D, T), lambda j: (0, 0), pipeline_mode=pl.Buffered(1)),
                  pl.BlockSpec((T, tb), lambda j: (0, j)), col(0), col(1)],
        out_specs=[pl.BlockSpec((2, D, tb), lambda j: (0, 0, j))],
        out_shape=[SDS((2, D, F), BF16)],
        semantics=("parallel",), args=(xt, dact, gu, gu), ride=ride)
    return out[0] if ride is None else (out[0][0], out[1])


def swiglu_bwd_in(dact, gu, w, h_in, g, dh_out, *, name, ride=None, tm=ROW_TILE // 2):
    T, D = h_in.shape
    F = dact.shape[1]

    def body(d_ref, gg_ref, uu_ref, wg_ref, wu_ref, h_ref, g_ref, dh_ref, o_ref, dg_ref):
        dgate, dup = _silu_grads(d_ref[...], gg_ref[...], uu_ref[...])
        dn = _dot_nt(dgate, wg_ref[...]) + _dot_nt(dup, wu_ref[...])
        dx, hh = _rmsnorm_bwd(h_ref[...], g_ref[...], dn)
        o_ref[...] = dh_ref[...] + dx
        _accumulate(dg_ref, pl.program_id(0) == 0, jnp.sum(dn * hh, axis=0, keepdims=True))

    row = pl.BlockSpec((tm, D), lambda i: (i, 0))
    vec = pl.BlockSpec((1, D), lambda i: (0, 0))
    part = lambda s: pl.BlockSpec((None, tm, F), lambda i: (s, i, 0))
    half = lambda s: pl.BlockSpec((D, F), lambda i: (0, s), pipeline_mode=pl.Buffered(1))
    return _call(
        body, name=name, grid=(T // tm,),
        in_specs=[pl.BlockSpec((tm, F), lambda i: (i, 0)), part(0), part(1), half(0), half(1), row, vec, row],
        out_specs=[row, vec],
        out_shape=[SDS((T, D), F32), SDS((1, D), F32)],
        semantics=("arbitrary",), args=(dact, gu, gu, w, w, h_in, g, dh_out), ride=ride)


def rope_tables(T):
    half = ROT_DIM // 2
    inv_freq = ROPE_THETA ** (-jnp.arange(0, ROT_DIM, 2, dtype=F32) / ROT_DIM)
    ang = (jnp.arange(T, dtype=F32)[:, None] * inv_freq[None, :]).T
    cos, sin = jnp.cos(ang), jnp.sin(ang)
    rest = HEAD_DIM - ROT_DIM
    one, zero = jnp.ones((rest, T), F32), jnp.zeros((rest, T), F32)
    zh = jnp.zeros((half, T), F32)
    fac = jnp.concatenate([cos, cos, one], axis=0)
    up = jnp.concatenate([-sin, zh, zero], axis=0)
    down = jnp.concatenate([zh, sin, zero], axis=0)
    return jnp.stack([fac, up, down])


def _rope(t, tab):
    half = ROT_DIM // 2
    return t * tab[0] + pltpu.roll(t, HEAD_DIM - half, 0) * tab[1] + pltpu.roll(t, half, 0) * tab[2]


def _rope_t(d, tab):
    half = ROT_DIM // 2
    return d * tab[0] + pltpu.roll(d * tab[1], half, 0) + pltpu.roll(d * tab[2], HEAD_DIM - half, 0)


def _head(t, h):
    return t[h * HEAD_DIM:(h + 1) * HEAD_DIM]


def _band(n, group):
    kj = lax.broadcasted_iota(jnp.int32, (2 * BLOCK, BLOCK), 0)
    qi = lax.broadcasted_iota(jnp.int32, (2 * BLOCK, BLOCK), 1)
    mask = (kj > qi) & (kj <= qi + BLOCK) & ((n > 0) | (kj >= BLOCK))
    return jnp.tile(mask, (1, group))


def _attn_specs(D, kvd):
    prev = lambda n: jnp.maximum(n - 1, 0)
    return [pl.BlockSpec((BLOCK, D), lambda n: (n, 0)),
            pl.BlockSpec((BLOCK, kvd), lambda n: (prev(n), 0)),
            pl.BlockSpec((BLOCK, kvd), lambda n: (n, 0)),
            pl.BlockSpec((BLOCK, kvd), lambda n: (prev(n), 1)),
            pl.BlockSpec((BLOCK, kvd), lambda n: (n, 1)),
            pl.BlockSpec((3, HEAD_DIM, BLOCK), lambda n: (0, 0, prev(n))),
            pl.BlockSpec((3, HEAD_DIM, BLOCK), lambda n: (0, 0, n)),
            pl.BlockSpec(memory_space=pltpu.SMEM)]


def _attn_operands(q_ref, kp_ref, k_ref, vp_ref, v_ref, tp_ref, t_ref):
    flip = lambda ref: ref[...].astype(F32).T
    tab = t_ref[...]
    kt = jnp.concatenate([flip(kp_ref), flip(k_ref)], axis=1)
    vt = jnp.concatenate([flip(vp_ref), flip(v_ref)], axis=1)
    return flip(q_ref), kt, vt, tab, jnp.concatenate([tp_ref[...], tab], axis=2)


SCORE_SCALE = 1.0 / math.sqrt(HEAD_DIM)
HEADS_TOGETHER = 4


def _group_heads(t, first, count, tab=None):
    heads = [_head(t, first + g) for g in range(count)]
    if tab is not None:
        heads = [_rope(h, tab) * SCORE_SCALE for h in heads]
    return jnp.concatenate(heads, axis=1).astype(BF16)


def _sink_row(s_ref, first, count):
    which = lax.broadcasted_iota(jnp.int32, (1, count * BLOCK), 1) // BLOCK
    row = jnp.zeros((1, count * BLOCK), F32)
    for g in range(count):
        row = jnp.where(which == g, s_ref[0, first + g], row)
    return row


def _softmax_block(k_j, q_j, sink, mask):
    return _softmax(_dot_tn(k_j, q_j), sink, mask)


def _softmax(scores, sink, mask):
    s = jnp.where(mask, scores, NEG)
    m = jnp.maximum(jnp.max(s, axis=0, keepdims=True), sink)
    e = jnp.exp(s - m)
    es = jnp.exp(sink - m)
    inv = 1.0 / (jnp.sum(e, axis=0, keepdims=True) + es)
    return e * inv, es * inv


def attention_fwd(q, kv, tabs, sinks, *, name, ride=None):
    T, D = q.shape
    kvd = kv.shape[1] // 2
    group = D // HEAD_DIM // N_KV_HEADS

    def body(q_ref, kp_ref, k_ref, vp_ref, v_ref, tp_ref, t_ref, s_ref, o_ref):
        gs = HEADS_TOGETHER
        mask = _band(pl.program_id(0), gs)
        qt, kt, vt, tab, tab2 = _attn_operands(q_ref, kp_ref, k_ref, vp_ref, v_ref, tp_ref, t_ref)
        firsts = [(j, first) for j in range(N_KV_HEADS) for first in range(j * group, (j + 1) * group, gs)]
        ks = [_rope(_head(kt, j), tab2).astype(BF16) for j in range(N_KV_HEADS)]
        scores = [_dot_tn(ks[j], _group_heads(qt, first, gs, tab)) for j, first in firsts]
        probs = [_softmax(s, _sink_row(s_ref, first, gs), mask)[0].astype(BF16)
                 for s, (j, first) in zip(scores, firsts)]
        outs = []
        for p, (j, first) in zip(probs, firsts):
            o = _dot(_head(vt, j).astype(BF16), p)
            outs += [o[:, g * BLOCK:(g + 1) * BLOCK] for g in range(gs)]
        o_ref[...] = jnp.concatenate(outs, axis=0).T.astype(BF16)

    return _call(
        body, name=name, grid=(T // BLOCK,),
        in_specs=_attn_specs(D, kvd),
        out_specs=[pl.BlockSpec((BLOCK, D), lambda n: (n, 0))],
        out_shape=[SDS((T, D), BF16)],
        semantics=("parallel",), args=(q, kv, kv, kv, kv, tabs, tabs, sinks), ride=ride)


def attention_bwd(q, kv, tabs, sinks, do, *, name, ride=None):
    T, D = q.shape
    kvd = kv.shape[1] // 2
    heads = D // HEAD_DIM
    group = heads // N_KV_HEADS

    def body(q_ref, kp_ref, k_ref, vp_ref, v_ref, tp_ref, t_ref, s_ref, do_ref, dq_ref, dc_ref, dp_ref, ds_ref):
        n = pl.program_id(0)
        gs = HEADS_TOGETHER
        mask = _band(n, gs)
        qt, kt, vt, tab, tab2 = _attn_operands(q_ref, kp_ref, k_ref, vp_ref, v_ref, tp_ref, t_ref)
        dot = do_ref[...].astype(F32).T
        lane = lax.broadcasted_iota(jnp.int32, (8, 128), 1)
        dsink = jnp.zeros((8, 128), F32)
        firsts = [(j, first) for j in range(N_KV_HEADS) for first in range(j * group, (j + 1) * group, gs)]
        ks = [_rope(_head(kt, j), tab2).astype(BF16) for j in range(N_KV_HEADS)]
        vs = [_head(vt, j).astype(BF16) for j in range(N_KV_HEADS)]
        qs = [_group_heads(qt, first, gs, tab) for _, first in firsts]
        dos = [_group_heads(dot, first, gs) for _, first in firsts]
        scores = [_dot_tn(ks[j], q) for q, (j, _) in zip(qs, firsts)]
        dps = [_dot_tn(vs[j], do) for do, (j, _) in zip(dos, firsts)]
        ps, dscs = [], []
        for s, dp, (j, first) in zip(scores, dps, firsts):
            p, p_sink = _softmax(s, _sink_row(s_ref, first, gs), mask)
            dl = jnp.sum(p * dp, axis=0, keepdims=True)
            dscs.append((p * (dp - dl)).astype(BF16))
            ps.append(p.astype(BF16))
            weight = p_sink * dl
            for g in range(gs):
                dsink = dsink - jnp.where(lane == first + g, jnp.sum(weight[:, g * BLOCK:(g + 1) * BLOCK]), 0.0)
        dqs = []
        dks = [jnp.zeros((HEAD_DIM, 2 * BLOCK), F32) for _ in range(N_KV_HEADS)]
        dvs = [jnp.zeros((HEAD_DIM, 2 * BLOCK), F32) for _ in range(N_KV_HEADS)]
        for p, dsc, q, do, (j, _) in zip(ps, dscs, qs, dos, firsts):
            dq = _dot(ks[j], dsc) * SCORE_SCALE
            dqs += [_rope_t(dq[:, g * BLOCK:(g + 1) * BLOCK], tab) for g in range(gs)]
            dks[j] = dks[j] + _dot_nt(q, dsc)
            dvs[j] = dvs[j] + _dot_nt(do, p)
        dks = [_rope_t(dk, tab2) for dk in dks]
        dq_ref[...] = jnp.concatenate(dqs, axis=0).T.astype(BF16)
        dkv = jnp.concatenate(dks + dvs, axis=0)
        dp_ref[...] = dkv[:, :BLOCK].T
        dc_ref[...] = dkv[:, BLOCK:].T
        _accumulate(ds_ref, n == 0, dsink)

    blk = lambda w: pl.BlockSpec((BLOCK, w), lambda n: (n, 0))
    return _call(
        body, name=name, grid=(T // BLOCK,),
        in_specs=_attn_specs(D, kvd) + [blk(D)],
        out_specs=[blk(D), blk(2 * kvd), blk(2 * kvd), pl.BlockSpec((8, 128), lambda n: (0, 0))],
        out_shape=[SDS((T, D), BF16), SDS((T, 2 * kvd), F32), SDS((T, 2 * kvd), F32), SDS((8, 128), F32)],
        semantics=("arbitrary",), args=(q, kv, kv, kv, kv, tabs, tabs, sinks, do), ride=ride)


def combine_dkv(d_cur, d_prev, *, name):
    T, W = d_cur.shape
    tm = ROW_TILE
    nt, per, last = T // tm, tm // BLOCK, T // BLOCK - 1

    def body(c_ref, p_ref, pn_ref, o_ref):
        nxt = jnp.where(pl.program_id(0) == nt - 1, 0.0, pn_ref[...])
        o_ref[...] = (c_ref[...] + jnp.concatenate([p_ref[BLOCK:, :], nxt], axis=0)).astype(BF16)

    return _call(
        body, name=name, grid=(nt,),
        in_specs=[pl.BlockSpec((tm, W), lambda i: (i, 0)), pl.BlockSpec((tm, W), lambda i: (i, 0)),
                  pl.BlockSpec((BLOCK, W), lambda i: (jnp.minimum((i + 1) * per, last), 0))],
        out_specs=[pl.BlockSpec((tm, W), lambda i: (i, 0))],
        out_shape=[SDS((T, W), BF16)],
        semantics=("parallel",), args=(d_cur, d_prev, d_prev))[0]


def normbwd_matmul_nt(z, g, dh, w, *, name, ride=None, tm=ROW_TILE):
    T, D = z.shape
    tm = min(tm, T)
    K = w.shape[0]

    def body(z_ref, g_ref, dh_ref, w_ref, dz_ref, dg_ref, o_ref):
        dh_ = dh_ref[...]
        dz, zh = _rmsnorm_bwd(z_ref[...].astype(F32), g_ref[...], dh_)
        dz = dz.astype(BF16)
        dz_ref[...] = dz
        _accumulate(dg_ref, pl.program_id(0) == 0, jnp.sum(dh_ * zh, axis=0, keepdims=True))
        o_ref[...] = _dot_nt(dz, w_ref[...]).astype(BF16)

    row = pl.BlockSpec((tm, D), lambda i: (i, 0))
    vec = pl.BlockSpec((1, D), lambda i: (0, 0))
    return _call(
        body, name=name, grid=(T // tm,),
        in_specs=[row, vec, row, pl.BlockSpec((K, D), lambda i: (0, 0))],
        out_specs=[row, vec, pl.BlockSpec((tm, K), lambda i: (i, 0))],
        out_shape=[SDS((T, D), BF16), SDS((1, D), F32), SDS((T, K), BF16)],
        semantics=("arbitrary",), args=(z, g, dh, w), ride=ride)


def matmul_nt_normbwd(da, w, h_in, g, dh_out, *, name, ride=None, tm=ROW_TILE):
    T, D = h_in.shape
    S, _, K = da.shape

    def body(*refs):
        da_refs, w_refs = refs[:S], refs[S:2 * S]
        h_ref, g_ref, dh_ref, o_ref, dg_ref = refs[2 * S:]
        dn = _dot_nt(da_refs[0][...], w_refs[0][...])
        for s in range(1, S):
            dn = dn + _dot_nt(da_refs[s][...], w_refs[s][...])
        dx, hh = _rmsnorm_bwd(h_ref[...], g_ref[...], dn)
        o_ref[...] = dh_ref[...] + dx
        _accumulate(dg_ref, pl.program_id(0) == 0, jnp.sum(dn * hh, axis=0, keepdims=True))

    row = pl.BlockSpec((tm, D), lambda i: (i, 0))
    vec = pl.BlockSpec((1, D), lambda i: (0, 0))
    part = lambda s: pl.BlockSpec((None, tm, K), lambda i: (s, i, 0))
    cols = lambda s: pl.BlockSpec((D, K), lambda i: (0, s), pipeline_mode=pl.Buffered(1))
    return _call(
        body, name=name, grid=(T // tm,),
        in_specs=[part(s) for s in range(S)] + [cols(s) for s in range(S)] + [row, vec, row],
        out_specs=[row, vec],
        out_shape=[SDS((T, D), F32), SDS((1, D), F32)],
        semantics=("arbitrary",), args=[da] * S + [w] * S + [h_in, g, dh_out], ride=ride)


def matmuls_nt_normbwd(das, ws, h_in, gs, dh_out, *, name, ride=None, tm=BIG_ROW_TILE):
    T, D = h_in.shape
    tm = min(tm, T)
    n = len(das)

    def body(*refs):
        da_refs, w_refs, g_refs = refs[:n], refs[n:2 * n], refs[2 * n:3 * n]
        h_ref, dh_ref, o_ref = refs[3 * n:3 * n + 3]
        hf = h_ref[...]
        r = _rms_r(hf)
        hh = hf * r
        total = dh_ref[...]
        for da_ref, w_ref, g_ref, dg_ref in zip(da_refs, w_refs, g_refs, refs[3 * n + 3:]):
            dn = _dot_nt(da_ref[...], w_ref[...])
            gd = g_ref[...] * dn
            total = total + r * (gd - hh * jnp.mean(hh * gd, axis=-1, keepdims=True))
            _accumulate(dg_ref, pl.program_id(0) == 0, jnp.sum(dn * hh, axis=0, keepdims=True))
        o_ref[...] = total

    row = pl.BlockSpec((tm, D), lambda i: (i, 0))
    vec = pl.BlockSpec((1, D), lambda i: (0, 0))
    return _call(
        body, name=name, grid=(T // tm,),
        in_specs=[pl.BlockSpec((tm, da.shape[1]), lambda i: (i, 0)) for da in das]
        + [pl.BlockSpec(w.shape, lambda i: (0, 0)) for w in ws] + [vec] * n + [row, row],
        out_specs=[row] + [vec] * n,
        out_shape=[SDS((T, D), F32)] + [SDS((1, D), F32)] * n,
        semantics=("arbitrary",), args=list(das) + list(ws) + list(gs) + [h_in, dh_out], ride=ride)


def matmul_tn(a, b, *, tb, name, ride=None, ta=MXU_WIDTH):
    T, Ka = a.shape
    S, _, Nb = b.shape
    per = Nb // tb

    def body(a_ref, b_ref, o_ref):
        o_ref[...] = _dot_tn(a_ref[...], b_ref[...]).astype(BF16)

    out = _call(
        body, name=name, grid=(S * per, Ka // ta),
        in_specs=[pl.BlockSpec((T, ta), lambda j, i: (0, i)),
                  pl.BlockSpec((None, T, tb), lambda j, i: (j // per, 0, j % per))],
        out_specs=[pl.BlockSpec((ta, tb), lambda j, i: (i, j))],
        out_shape=[SDS((Ka, S * Nb), BF16)],
        semantics=("parallel", "parallel"), args=(a, b), ride=ride)
    return out[0] if ride is None else (out[0][0], out[1])


def conv_bwd(dy, bcx, conv_w, *, name, ride=None, tm=ROW_TILE):
    T, D = dy.shape
    nt = T // tm
    hb = tm // BF16_ROWS
    last = T // BF16_ROWS - 1

    def body(dy_ref, dyn_ref, b_ref, bn_ref, c_ref, u_ref, cp_ref, up_ref, cw_ref, o_ref, dw_ref):
        i = pl.program_id(0)
        c, u = c_ref[...].astype(F32), u_ref[...].astype(F32)
        cu = c * u
        cup = jnp.where(i == 0, 0.0, cp_ref[...].astype(F32) * up_ref[...].astype(F32))
        cu1, cu2 = _shift_down(cup, cu, 1), _shift_down(cup, cu, 2)
        w0, w1, w2 = cw_ref[0:1, :], cw_ref[1:2, :], cw_ref[2:3, :]
        dyf = dy_ref[...].astype(F32)
        o_ref[:, 0:D] = (dyf * (w0 * cu2 + w1 * cu1 + w2 * cu)).astype(BF16)
        dcv = dyf * b_ref[...].astype(F32)
        dcvn = jnp.where(i == nt - 1, 0.0, dyn_ref[...].astype(F32) * bn_ref[...].astype(F32))
        dcu = w2 * dcv + w1 * _shift_up(dcv, dcvn, 1) + w0 * _shift_up(dcv, dcvn, 2)
        o_ref[:, D:2 * D] = (dcu * u).astype(BF16)
        o_ref[:, 2 * D:3 * D] = (dcu * c).astype(BF16)
        row = lax.broadcasted_iota(jnp.int32, (8, D), 0)
        dw = jnp.zeros((8, D), F32)
        for tap, t in enumerate((cu2, cu1, cu)):
            dw = jnp.where(row == tap, jnp.sum(dcv * t, axis=0, keepdims=True), dw)
        _accumulate(dw_ref, i == 0, dw)

    tile = lambda col: pl.BlockSpec((tm, D), lambda i: (i, col))
    prev = lambda col: pl.BlockSpec((BF16_ROWS, D), lambda i: (jnp.maximum(i * hb - 1, 0), col))
    nxt = lambda col: pl.BlockSpec((BF16_ROWS, D), lambda i: (jnp.minimum((i + 1) * hb, last), col))
    return _call(
        body, name=name, grid=(nt,),
        in_specs=[tile(0), nxt(0), tile(0), nxt(0), tile(1), tile(2), prev(1), prev(2),
                  pl.BlockSpec((3, D), lambda i: (0, 0))],
        out_specs=[pl.BlockSpec((tm, 3 * D), lambda i: (i, 0)), pl.BlockSpec((8, D), lambda i: (0, 0))],
        out_shape=[SDS((T, 3 * D), BF16), SDS((8, D), F32)],
        semantics=("arbitrary",), args=(dy, dy, bcx, bcx, bcx, bcx, bcx, bcx, conv_w), ride=ride)


class NoTraffic:
    def ride(self, kernel_name):
        return None

    def landed(self, kernel_name, results, wts):
        pass

    def grad(self, key, value):
        pass


def local_step(x, target, wts, vec, traffic):
    T, D = x.shape
    tabs = rope_tables(T)
    small = {}

    def run(builder, *args, name, **kw):
        ride = traffic.ride(name)
        if ride is None:
            return builder(*args, name=name, **kw)
        out, extra = builder(*args, name=name, ride=ride, **kw)
        traffic.landed(name, extra, wts)
        return out

    bcx, xn1 = run(norm_matmul, x, vec["a_pre"], wts["w_in"], tn=3 * D, split=1, name="a_in")
    bcx = bcx[0]
    h1, z0, y0 = run(conv_mix_out, bcx, vec["conv_w"], wts["w_out"], vec["a_post"], x, name="a_out")
    gu0, act0, xt2 = run(norm_swiglu_in, h1, vec["ffn_pre0"], wts["gu0"], name="ffn0_in")
    h2, z1 = run(plain_mix_out, act0, wts["wd0"], vec["ffn_post0"], h1, name="ffn0_out")
    kvp, xkv, qp, xq = norm2_matmul(h2, [vec["kv_norm"], vec["b_pre"]], [wts["w_kv"], wts["w_q"]], name="kvq_in")
    (attn,) = run(attention_fwd, qp, kvp, tabs, vec["sinks"], name="attn_fwd")
    h3, z2 = plain_mix_out(attn, wts["w_o"], vec["b_post"], h2, name="attn_out", tm=BIG_ROW_TILE)
    gu1, act1, xt3 = run(norm_swiglu_in, h3, vec["ffn_pre1"], wts["gu1"], name="ffn1_in")
    dy, z3, loss = plain_mix_out(act1, wts["wd1"], vec["ffn_post1"], h3, name="ffn1_out", target=target)

    def ffn_bwd(layer, z, gu, act, xt, h_in, dh, gu_first):
        tag = "ffn%d" % layer
        dz, small["ffn_post%d" % layer], dact = run(
            normbwd_matmul_nt, z, vec["ffn_post%d" % layer], dh, wts["wd%d" % layer], name=tag + "_out_bwd")
        dwd = lambda: traffic.grad("wd%d" % layer, run(matmul_tn, act, dz[None], tb=D, name=tag + "_dwd"))
        dwgu = lambda: traffic.grad("gu%d" % layer, run(swiglu_bwd_tn, xt, dact, gu, name=tag + "_dwgu"))
        for step in ((dwgu, dwd) if gu_first else (dwd, dwgu)):
            step()
        dh_in, small["ffn_pre%d" % layer] = run(
            swiglu_bwd_in, dact, gu, wts["gu%d" % layer], h_in, vec["ffn_pre%d" % layer], dh, name=tag + "_in_bwd")
        return dh_in

    dh3 = ffn_bwd(1, z3, gu1, act1, xt3, h3, dy, gu_first=False)
    dz2, small["b_post"], dattn = normbwd_matmul_nt(z2, vec["b_post"], dh3, wts["w_o"], name="attn_out_bwd",
                                                    tm=BIG_ROW_TILE)
    traffic.grad("w_o", matmul_tn(attn, dz2[None], tb=D, name="attn_dwo"))
    dq, dkv_cur, dkv_prev, small["sinks"] = run(attention_bwd, qp, kvp, tabs, vec["sinks"], dattn, name="attn_bwd")
    dkv = combine_dkv(dkv_cur, dkv_prev, name="attn_dkv")
    traffic.grad("w_q", matmul_tn(xq, dq[None], tb=D, name="attn_dwq"))
    traffic.grad("w_kv", matmul_tn(xkv, dkv[None], tb=dkv.shape[1], name="attn_dwkv"))
    dh2, small["b_pre"], small["kv_norm"] = run(
        matmuls_nt_normbwd, [dq, dkv], [wts["w_q"], wts["w_kv"]], h2, [vec["b_pre"], vec["kv_norm"]], dh3,
        name="qkv_in_bwd")
    dh1 = ffn_bwd(0, z1, gu0, act0, xt2, h1, dh2, gu_first=True)
    dz0, small["a_post"], dyc = normbwd_matmul_nt(z0, vec["a_post"], dh1, wts["w_out"], name="a_out_bwd",
                                                  tm=BIG_ROW_TILE)
    traffic.grad("w_out", matmul_tn(y0, dz0[None], tb=D, name="a_dwout"))
    dbcx, small["conv_w"] = run(conv_bwd, dyc, bcx, vec["conv_w"], name="a_conv_bwd")
    traffic.grad("w_in", matmul_tn(xn1, dbcx[None], tb=3 * D // 2, name="a_dwin"))
    dx, small["a_pre"] = run(matmul_nt_normbwd, dbcx[None], wts["w_in"], x, vec["a_pre"], dh1, name="a_in_bwd")
    return loss, dx, small


SMALL_ROWS = 16
LOSS_ROW = 13

WHOLE = (0, 16)
GATHER_PLAN = {"cast_rest": [("w_in", WHOLE)],
               "a_in": [("w_out", WHOLE), ("gu0", (0, 9))],
               "a_out": [("gu0", (9, 7))],
               "ffn0_in": [("wd0", WHOLE), ("w_kv", WHOLE), ("w_q", WHOLE), ("w_o", WHOLE)],
               "ffn0_out": [("gu1", (0, 8))],
               "attn_fwd": [("gu1", (8, 8))],
               "ffn1_in": [("wd1", WHOLE)]}
PAIR_PLAN = {"ffn1_dwgu": ["wd1"], "ffn1_in_bwd": ["gu1"], "attn_bwd": ["w_o"], "qkv_in_bwd": ["w_q", "w_kv"],
             "ffn0_dwd": ["gu0"], "ffn0_in_bwd": ["wd0"], "a_conv_bwd": ["w_out"]}
PAIR_ALONE = ["w_in"]
CHIP_PLAN = {"ffn1_in_bwd": ["wd1"], "attn_bwd": ["gu1"], "ffn0_out_bwd": ["w_o", "w_q", "w_kv"],
             "ffn0_in_bwd": ["gu0"], "a_conv_bwd": ["wd0"], "a_in_bwd": ["w_out", "w_in"]}
HALF_PLAN = {"a_in_bwd": ["gu0", "gu1", "wd0", "wd1", "w_kv", "w_q", "w_o"]}
GRAD_KIND = dict(KIND, gu0="split", gu1="split")


class Traffic:
    def __init__(self, wholes, quarter, c_arr, pc_arr):
        self.wholes, self.quarter, self.c_arr, self.pc_arr = wholes, quarter, c_arr, pc_arr
        self.views, self.sums, self.got = {}, {}, {}
        self.reduced = {}
        self.stages = {}

    def reduce(self, keys, name):
        return chip_reduce([self.sums[k] for k in keys], [self.got[k] for k in keys], [GRAD_KIND[k] for k in keys],
                           self.pc_arr, name=name)

    def ride(self, name, small=None):
        rides, stages = [], []
        if name in GATHER_PLAN:
            plan = GATHER_PLAN[name]
            rides.append(gather_ride([self.wholes[k] for k, _ in plan],
                                     [(KIND[k], self.quarter[k], part) for k, part in plan], small))
            stages.append(("gather", [k for k, _ in plan]))
        if name in CHIP_PLAN:
            keys = CHIP_PLAN[name]
            rides.append(chip_ride([self.sums[k] for k in keys], [(GRAD_KIND[k], self.quarter[k]) for k in keys]))
            stages.append(("chip", keys))
        if name in PAIR_PLAN:
            keys = PAIR_PLAN[name]
            rides.append(pair_ride([self.views[k] for k in keys]))
            stages.append(("pair", keys))
        if name in HALF_PLAN:
            keys = HALF_PLAN[name]
            rides.append(half_ride(self.reduce(keys, "chip_reduce_early")))
            stages.append(("half", keys))
        self.stages[name] = stages
        return join(rides)

    def landed(self, name, results, wts):
        results = list(results)
        for stage, keys in self.stages[name]:
            mine, results = results[:len(keys)], results[len(keys):]
            if stage == "gather":
                for k, whole in zip(keys, mine):
                    self.wholes[k] = wts[k] = whole
            elif stage == "chip":
                self.got.update(zip(keys, mine))
            elif stage == "half":
                self.reduced.update(zip(keys, mine))
            else:
                for k, got in zip(keys, mine):
                    self.sums[k] = pair_add(self.views[k], got, self.c_arr, name="pair_add_" + k)

    def grad(self, key, value):
        r, ws = self.quarter[key]
        view = {"row": (N_CHIPS, 2, r // 2, ws), "col": (1, 2, r // 2, N_CHIPS * ws), "split": (2, 2, r // 2, 2 * ws)}
        self.views[key] = value.reshape(view[GRAD_KIND[key]])
        if key in PAIR_ALONE:
            (got,) = alone(pair_ride([self.views[key]]), name="pair_exchange_" + key)
            self.sums[key] = pair_add(self.views[key], got, self.c_arr, name="pair_add_" + key)


def kernel(x, a_pre_norm, a_w_in, a_conv_w, a_w_out, a_post_norm, ffn_pre_norm, ffn_w_gate_up, ffn_w_down, ffn_post_norm, kv_norm, w_kv, b_pre_norm, b_w_q, b_sinks, b_w_o, b_post_norm, loss_target, m_a_pre_norm, m_a_w_in, m_a_conv_w, m_a_w_out, m_a_post_norm, m_ffn_pre_norm, m_ffn_w_gate_up, m_ffn_w_down, m_ffn_post_norm, m_kv_norm, m_w_kv, m_b_pre_norm, m_b_w_q, m_b_sinks, m_b_w_o, m_b_post_norm, v_a_pre_norm, v_a_w_in, v_a_conv_w, v_a_w_out, v_a_post_norm, v_ffn_pre_norm, v_ffn_w_gate_up, v_ffn_w_down, v_ffn_post_norm, v_kv_norm, v_w_kv, v_b_pre_norm, v_b_w_q, v_b_sinks, v_b_w_o, v_b_post_norm):
    T, D = x.shape[1], x.shape[2]
    xi, yi, ci = _place()
    p = 2 * xi + yi
    p_arr = jnp.reshape(p, (1,)).astype(jnp.int32)
    c_arr = jnp.reshape(ci, (1,)).astype(jnp.int32)
    pc_arr = jnp.stack([p, ci]).astype(jnp.int32)
    me_arr = jnp.reshape(4 * xi + 2 * yi + ci, (1,)).astype(jnp.int32)
    qd = D // N_CHIPS

    big = {"w_in": (a_w_in, 0), "w_out": (a_w_out, 0), "gu0": (ffn_w_gate_up, 0), "gu1": (ffn_w_gate_up, 1),
           "wd0": (ffn_w_down, 0), "wd1": (ffn_w_down, 1), "w_kv": (w_kv[None], 0), "w_q": (b_w_q, 0),
           "w_o": (b_w_o, 0)}
    names = list(big)
    quarter = {k: w.shape[1:] for k, (w, _) in big.items()}
    source = lambda k: big[k] + (KIND[k],)
    traffic = Traffic(dict(zip(names[:1], cast_quarters([source(names[0])], p_arr, name="cast_first"))), quarter,
                      c_arr, pc_arr)
    small_shard = jnp.concatenate([a_pre_norm, a_post_norm, a_conv_w[0], jnp.zeros((3, qd), F32)], axis=0)
    wts = {}
    rest, (*landed, small_full) = cast_quarters([source(k) for k in names[1:]], p_arr, name="cast_rest",
                                                ride=traffic.ride("cast_rest", small_shard))
    traffic.wholes.update(zip(names[1:], rest))
    traffic.landed("cast_rest", landed, wts)
    rows = lambda k: jnp.transpose(small_full[:, k], (1, 0, 2)).reshape(-1, D)
    vec = {"a_pre": rows(slice(0, 1)), "a_post": rows(slice(1, 2)), "conv_w": rows(slice(2, 5)),
           "ffn_pre0": ffn_pre_norm[0:1], "ffn_pre1": ffn_pre_norm[1:2],
           "ffn_post0": ffn_post_norm[0:1], "ffn_post1": ffn_post_norm[1:2],
           "kv_norm": kv_norm[None], "b_pre": b_pre_norm, "b_post": b_post_norm, "sinks": b_sinks}

    loss, dx, small = local_step(x[0], loss_target[0], wts, vec, traffic)

    pad = lambda a: jnp.pad(a, ((0, 0), (0, D - a.shape[1])))
    small_block = jnp.concatenate(
        [small["a_pre"], small["a_post"], small["conv_w"][0:3], small["ffn_pre0"], small["ffn_pre1"],
         small["ffn_post0"], small["ffn_post1"], small["kv_norm"], small["b_pre"], small["b_post"],
         pad(small["sinks"][0:1]), pad(loss[0:1]), jnp.zeros((SMALL_ROWS - LOSS_ROW - 1, D), F32)], axis=0)
    late = [k for k in names if k not in traffic.reduced]
    *swapped, small_blocks = alone(join([half_ride(traffic.reduce(late, "chip_reduce_late")),
                                         chip_ride([], [], small_block)]), name="last_exchange")
    traffic.reduced.update(zip(late, swapped))
    grad = {k: traffic.reduced[k].reshape(quarter[k]) for k in names}
    small_sum = small_reduce(small_blocks, me_arr)

    out = {}
    out["a_w_in"] = adamw(a_w_in, [grad["w_in"]], m_a_w_in, v_a_w_in, name="adamw_a_w_in")
    out["a_w_out"] = adamw(a_w_out, [grad["w_out"]], m_a_w_out, v_a_w_out, name="adamw_a_w_out")
    out["ffn_w_gate_up"] = adamw(ffn_w_gate_up, [grad["gu0"], grad["gu1"]], m_ffn_w_gate_up, v_ffn_w_gate_up,
                                 name="adamw_ffn_w_gate_up")
    out["ffn_w_down"] = adamw(ffn_w_down, [grad["wd0"], grad["wd1"]], m_ffn_w_down, v_ffn_w_down,
                              name="adamw_ffn_w_down")
    out["w_kv"] = [o[0] for o in adamw(w_kv[None], [grad["w_kv"]], m_w_kv[None], v_w_kv[None], name="adamw_w_kv")]
    out["b_w_q"] = adamw(b_w_q, [grad["w_q"]], m_b_w_q, v_b_w_q, name="adamw_b_w_q")
    out["b_w_o"] = adamw(b_w_o, [grad["w_o"]], m_b_w_o, v_b_w_o, name="adamw_b_w_o")

    def pack(a_pre, a_post, conv, ffn_pre, ffn_post, kvn, b_pre, b_post, sinks):
        return jnp.concatenate([pad(a_pre), pad(a_post), pad(conv[0]), ffn_pre, ffn_post, kvn[None], b_pre, b_post,
                                pad(sinks), jnp.zeros((SMALL_ROWS - 13, D), F32)], axis=0)

    g_small = jnp.concatenate([pad(lax.dynamic_slice(small_sum, (0, p * qd), (5, qd))), small_sum[5:]], axis=0)
    w_small = pack(a_pre_norm, a_post_norm, a_conv_w, ffn_pre_norm, ffn_post_norm, kv_norm, b_pre_norm, b_post_norm,
                   b_sinks)
    m_small = pack(m_a_pre_norm, m_a_post_norm, m_a_conv_w, m_ffn_pre_norm, m_ffn_post_norm, m_kv_norm,
                   m_b_pre_norm, m_b_post_norm, m_b_sinks)
    v_small = pack(v_a_pre_norm, v_a_post_norm, v_a_conv_w, v_ffn_pre_norm, v_ffn_post_norm, v_kv_norm,
                   v_b_pre_norm, v_b_post_norm, v_b_sinks)
    packed = adamw(w_small[None], [g_small], m_small[None], v_small[None], name="adamw_small")
    ns = b_sinks.shape[1]
    unpack = lambda a: {"a_pre_norm": a[0:1, :qd], "a_post_norm": a[1:2, :qd], "a_conv_w": a[None, 2:5, :qd],
                        "ffn_pre_norm": a[5:7], "ffn_post_norm": a[7:9], "kv_norm": a[9], "b_pre_norm": a[10:11],
                        "b_post_norm": a[11:12], "b_sinks": a[12:13, :ns]}
    unpacked = [unpack(a[0]) for a in packed]
    for k in unpacked[0]:
        out[k] = [u[k] for u in unpacked]

    order = ["a_pre_norm", "a_w_in", "a_conv_w", "a_w_out", "a_post_norm", "ffn_pre_norm", "ffn_w_gate_up",
             "ffn_w_down", "ffn_post_norm", "kv_norm", "w_kv", "b_pre_norm", "b_w_q", "b_sinks", "b_w_o",
             "b_post_norm"]
    return (small_sum[LOSS_ROW, 0], dx[None], *[out[k][0] for k in order], *[out[k][1] for k in order],
            *[out[k][2] for k in order], *[out[k][3] for k in order])
```

```python
import math

import jax
import jax.numpy as jnp
from jax import lax
from jax.experimental import pallas as pl
from jax.experimental.pallas import tpu as pltpu

F32 = jnp.float32
BF16 = jnp.bfloat16
SDS = jax.ShapeDtypeStruct
MESH = pl.DeviceIdType.MESH
DMA = pltpu.SemaphoreType.DMA
HBM_SPEC = pl.BlockSpec(memory_space=pltpu.HBM)

EPS = 1e-6
NEG = -1e30
HEAD_DIM = 64
N_KV_HEADS = 4
BLOCK = 128
ROT_DIM = HEAD_DIM // 4
ROPE_THETA = 500000.0
N_CHIPS = 4

ADAM_LR = 0.001
ADAM_B1 = 0.9
ADAM_B2 = 0.999
ADAM_EPS = 1e-08
ADAM_WD = 0.01
ADAM_STEP = 10

VMEM_LIMIT_BYTES = 52 * 1024 * 1024
ROW_TILE = 512
BF16_ROWS = 16
MXU_WIDTH = 256

KIND = {"w_in": "col", "gu0": "col", "gu1": "col", "w_out": "row", "wd0": "row", "wd1": "row", "w_kv": "row",
        "w_q": "row", "w_o": "row"}


def _params(*semantics):
    return pltpu.CompilerParams(dimension_semantics=semantics, vmem_limit_bytes=VMEM_LIMIT_BYTES)


def _row_tile(rows, limit, step=8):
    return max(t for t in range(step, limit + 1, step) if rows % t == 0)


def _place():
    return lax.axis_index("x"), lax.axis_index("y"), lax.axis_index("c")


def _other_chips(x, y):
    return [(1 - x, y), (x, 1 - y), (1 - x, 1 - y)]


def _remote(src, dst, send_sem, recv_sem, to):
    return pltpu.make_async_remote_copy(src_ref=src, dst_ref=dst, send_sem=send_sem, recv_sem=recv_sem,
                                        device_id=to, device_id_type=MESH)


def _full_shape(kind, quarter):
    r, ws = quarter
    return (N_CHIPS * r, ws) if kind == "row" else (r, N_CHIPS * ws)


def _rows_of(h, part):
    lo, n = (0, h) if part is None else (part[0] * BF16_ROWS, part[1] * BF16_ROWS)
    assert lo + n <= h, (h, part)
    return lo, n


def _half_of_quarter(ref, kind, quarter, part, q, half):
    r, ws = quarter
    h = r // 2
    lo, n = _rows_of(h, part)
    if kind == "row":
        return ref.at[pl.ds(pl.multiple_of(q * r + half * h + lo, BF16_ROWS), n)]
    return ref.at[pl.ds(pl.multiple_of(half * h + lo, BF16_ROWS), n), pl.ds(pl.multiple_of(q * ws, 128), ws)]


class Ride:
    def __init__(self, operands, out_shape, aliases, sems, make):
        self.operands, self.out_shape, self.aliases, self.sems, self.make = operands, out_shape, aliases, sems, make


def join(rides):
    rides = [r for r in rides if r is not None]
    if len(rides) < 2:
        return rides[0] if rides else None
    aliases, at = {}, [0, 0, 0]
    cuts = []
    for r in rides:
        aliases.update({at[0] + i: at[1] + o for i, o in r.aliases.items()})
        cuts.append(tuple(at))
        at = [at[0] + len(r.operands), at[1] + len(r.out_shape), at[2] + len(r.sems)]
    cuts.append(tuple(at))

    def make(ins, outs, sem):
        made = [r.make(ins[lo[0]:hi[0]], outs[lo[1]:hi[1]], sem[lo[2]:hi[2]]) for r, lo, hi in zip(rides, cuts, cuts[1:])]

        def start():
            for s, _ in made:
                s()

        def finish():
            for _, f in made:
                f()

        return start, finish

    return Ride(sum((list(r.operands) for r in rides), []), sum((list(r.out_shape) for r in rides), []), aliases,
                sum((list(r.sems) for r in rides), []), make)


def _call(body, *, name, grid, in_specs, out_specs, out_shape, args, scratch_shapes=(), semantics=None, ride=None,
          prefetch=None):
    pre = 0 if prefetch is None else 1
    n_in, n_out, n_scr = len(in_specs), len(out_specs), len(scratch_shapes)
    r_in, r_out = (len(ride.operands), len(ride.out_shape)) if ride is not None else (0, 0)
    a, b = pre + n_in, pre + n_in + r_in
    c, d = b + n_out, b + n_out + r_out
    e = d + n_scr

    def riding(*refs):
        start, finish = ride.make(refs[a:b], refs[c:d], refs[e:])
        ids = [pl.program_id(k) for k in range(len(grid))]
        first, last = ids[0] == 0, ids[0] == grid[0] - 1
        for k in range(1, len(grid)):
            first, last = first & (ids[k] == 0), last & (ids[k] == grid[k] - 1)
        pl.when(first)(start)
        body(*refs[:a], *refs[b:c], *refs[d:e])
        pl.when(last)(finish)

    if ride is None:
        kernel_body, extra_in, extra_out, extra_shape, extra_scr, aliases = body, [], [], [], [], {}
        params = _params(*semantics)
    else:
        kernel_body, extra_in, extra_out = riding, [HBM_SPEC] * r_in, [HBM_SPEC] * r_out
        extra_shape, extra_scr = list(ride.out_shape), list(ride.sems)
        aliases = {pre + n_in + i: n_out + o for i, o in ride.aliases.items()}
        params = _params(*(("arbitrary",) * len(grid)))
    specs = dict(grid=grid, in_specs=list(in_specs) + extra_in, out_specs=list(out_specs) + extra_out,
                 scratch_shapes=list(scratch_shapes) + extra_scr)
    if prefetch is not None:
        specs = dict(grid_spec=pltpu.PrefetchScalarGridSpec(num_scalar_prefetch=1, **specs))
        args = (prefetch,) + tuple(args)
    outs = pl.pallas_call(kernel_body, name=name, out_shape=list(out_shape) + extra_shape,
                          input_output_aliases=aliases, compiler_params=params, **specs,
                          )(*args, *(ride.operands if ride is not None else ()))
    return outs if ride is None else (outs[:n_out], outs[n_out:])


def alone(ride, *, name):
    def body(*refs):
        n = len(ride.operands)
        start, finish = ride.make(refs[:n], refs[n:n + len(ride.out_shape)], refs[n + len(ride.out_shape):])
        start()
        finish()

    return pl.pallas_call(
        body, name=name, in_specs=[HBM_SPEC] * len(ride.operands), out_specs=[HBM_SPEC] * len(ride.out_shape),
        out_shape=list(ride.out_shape), input_output_aliases=dict(ride.aliases), scratch_shapes=list(ride.sems),
    )(*ride.operands)


def gather_ride(wholes, metas, small=None):
    n = len(wholes)
    operands, out_shape = list(wholes), [SDS(s.shape, s.dtype) for s in wholes]
    sems = [DMA((n, 3)), DMA((n, 3)), DMA((n, 3)), DMA((n, 3))]
    if small is not None:
        operands.append(small)
        out_shape.append(SDS((N_CHIPS,) + small.shape, small.dtype))
        sems += [DMA((3,)), DMA((3,)), DMA(())]

    def make(ins, outs, sem):
        send1, recv1, send2, recv2 = sem[:4]
        x, y, c = _place()
        p = 2 * x + y
        chips = _other_chips(x, y)
        me, sibling = (x, y, c), (x, y, 1 - c)
        part = lambda t, q, half: _half_of_quarter(outs[t], *metas[t], q, half)
        first = []
        for j, (qx, qy) in enumerate(chips):
            if small is not None:
                first.append(_remote(ins[n], outs[n].at[p], sem[4].at[j], sem[5].at[j], (qx, qy, c)))
            for t in range(n):
                first.append(_remote(part(t, p, c), part(t, p, c), send1.at[t, j], recv1.at[t, j], (qx, qy, c)))
        local = [] if small is None else [pltpu.make_async_copy(ins[n], outs[n].at[p], sem[6])]

        def start():
            for cp in local + first:
                cp.start()

        def finish():
            passed = []
            for j, (qx, qy) in enumerate(chips):
                q = 2 * qx + qy
                for t in range(n):
                    landed = part(t, q, c)
                    _remote(landed, landed, send1.at[t, j], recv1.at[t, j], me).wait_recv()
                    cp = _remote(landed, landed, send2.at[t, j], recv2.at[t, j], sibling)
                    cp.start()
                    passed.append(cp)
            for j, (qx, qy) in enumerate(chips):
                q = 2 * qx + qy
                if small is not None:
                    _remote(outs[n].at[q], outs[n].at[q], sem[4].at[j], sem[5].at[j], me).wait_recv()
                for t in range(n):
                    theirs = part(t, q, 1 - c)
                    _remote(theirs, theirs, send2.at[t, j], recv2.at[t, j], me).wait_recv()
            for cp in first + passed:
                cp.wait_send()
            for cp in local:
                cp.wait()

        return start, finish

    return Ride(operands, out_shape, {t: t for t in range(n)}, sems, make)


def chip_ride(sums, metas, small=None, earlier=None):
    n = len(sums)
    operands = list(sums)
    out_shape = [SDS((3, s.shape[1], quarter[1]), s.dtype) for s, (_, quarter, _) in zip(sums, metas)]
    sems = [DMA((n, 3)), DMA((n, 3))] if n else []
    if small is not None:
        operands.append(small)
        out_shape.append(SDS((8,) + small.shape, small.dtype))
        sems += [DMA((7,)), DMA((7,)), DMA(())]
    aliases = {}
    for t, buffer in enumerate(earlier or [None] * n):
        if buffer is not None:
            aliases[len(operands)] = t
            operands.append(buffer)

    def make(ins, outs, sem):
        x, y, c = _place()
        cps = []
        for j, (qx, qy) in enumerate(_other_chips(x, y)):
            q = 2 * qx + qy
            for t in range(n):
                kind, (_, ws), part = metas[t]
                rows = pl.ds(*_rows_of(ins[t].shape[1], part))
                if kind == "row":
                    src = ins[t].at[q, rows]
                elif kind == "col":
                    src = ins[t].at[0, rows, pl.ds(pl.multiple_of(q * ws, 128), ws)]
                else:
                    src = ins[t].at[q // 2, rows, pl.ds(pl.multiple_of((q % 2) * ws, 128), ws)]
                cps.append(_remote(src, outs[t].at[j, rows], sem[0].at[t, j], sem[1].at[t, j], (qx, qy, c)))
        local = []
        if small is not None:
            ssend, srecv, lsem = sem[2 * bool(n):2 * bool(n) + 3]
            local.append(pltpu.make_async_copy(ins[n], outs[n].at[0], lsem))
            for k in range(1, 8):
                peer = (x ^ (k >> 2 & 1), y ^ (k >> 1 & 1), c ^ (k & 1))
                cps.append(_remote(ins[n], outs[n].at[k], ssend.at[k - 1], srecv.at[k - 1], peer))

        def start():
            for cp in local + cps:
                cp.start()

        def finish():
            for cp in cps + local:
                cp.wait()

        return start, finish

    return Ride(operands, out_shape, aliases, sems, make)


def pair_ride(grads):
    n = len(grads)

    def make(ins, outs, sem):
        x, y, c = _place()
        cps = [_remote(ins[t].at[:, 1 - c], outs[t], sem[0].at[t], sem[1].at[t], (x, y, 1 - c)) for t in range(n)]

        def start():
            for cp in cps:
                cp.start()

        def finish():
            for cp in cps:
                cp.wait()

        return start, finish

    return Ride(list(grads), [SDS((g.shape[0],) + g.shape[2:], g.dtype) for g in grads], {}, [DMA((n,)), DMA((n,))],
                make)


def half_ride(quarters):
    n = len(quarters)

    def make(ins, outs, sem):
        x, y, c = _place()
        sends = [_remote(outs[t].at[c], outs[t].at[c], sem[0].at[t], sem[1].at[t], (x, y, 1 - c)) for t in range(n)]

        def start():
            for cp in sends:
                cp.start()

        def finish():
            for t in range(n):
                theirs = outs[t].at[1 - c]
                _remote(theirs, theirs, sem[0].at[t], sem[1].at[t], (x, y, c)).wait_recv()
            for cp in sends:
                cp.wait_send()

        return start, finish

    return Ride(list(quarters), [SDS(q.shape, q.dtype) for q in quarters], {t: t for t in range(n)},
                [DMA((n,)), DMA((n,))], make)


CAST_STEPS = 4


def cast_quarters(sources, p_arr, *, name, ride=None):
    n = len(sources)
    in_specs, out_specs, out_shape = [], [], []
    for w, layer, kind in sources:
        _, r, ws = w.shape
        tr = r // CAST_STEPS
        assert tr % BF16_ROWS == 0, w.shape
        in_specs.append(pl.BlockSpec((None, tr, ws), lambda i, p_ref, layer=layer: (layer, i, 0)))
        out_specs.append(pl.BlockSpec((tr, ws), (lambda i, p_ref: (p_ref[0] * CAST_STEPS + i, 0)) if kind == "row"
                                      else (lambda i, p_ref: (i, p_ref[0]))))
        out_shape.append(SDS(_full_shape(kind, (r, ws)), BF16))

    def body(p_ref, *refs):
        for w_ref, o_ref in zip(refs[:n], refs[n:]):
            o_ref[...] = w_ref[...].astype(BF16)

    return _call(body, name=name, grid=(CAST_STEPS,), in_specs=in_specs, out_specs=out_specs, out_shape=out_shape,
                 semantics=("parallel",), args=[w for w, _, _ in sources], ride=ride, prefetch=p_arr)


def pair_add(own, got, c_arr, *, name):
    A, _, h, W = own.shape
    th = _row_tile(h, max(BF16_ROWS, (3 << 19) // W), BF16_ROWS)

    def body(c_ref, a_ref, b_ref, o_ref):
        o_ref[...] = (a_ref[...].astype(F32) + b_ref[...].astype(F32)).astype(BF16)

    return pl.pallas_call(
        body, name=name,
        grid_spec=pltpu.PrefetchScalarGridSpec(
            num_scalar_prefetch=1, grid=(A, h // th),
            in_specs=[pl.BlockSpec((None, None, th, W), lambda q, i, c_ref: (q, c_ref[0], i, 0)),
                      pl.BlockSpec((None, th, W), lambda q, i, c_ref: (q, i, 0))],
            out_specs=pl.BlockSpec((None, th, W), lambda q, i, c_ref: (q, i, 0))),
        out_shape=SDS((A, h, W), BF16),
        compiler_params=_params("parallel", "parallel"),
    )(c_arr, own, got)


REDUCE_STEPS = 2


def chip_reduce(sums, got, kinds, pc_arr, *, name):
    n = len(sums)
    mine = {"row": lambda i, pc_ref: (pc_ref[0], i, 0), "col": lambda i, pc_ref: (0, i, pc_ref[0]),
            "split": lambda i, pc_ref: (pc_ref[0] // 2, i, pc_ref[0] % 2)}
    a_specs, b_specs, o_specs, out_shape = [], [], [], []
    for g, kind in zip(got, kinds):
        _, h, ws = g.shape
        th = h // REDUCE_STEPS
        assert th % BF16_ROWS == 0, g.shape
        a_specs.append(pl.BlockSpec((None, th, ws), mine[kind]))
        b_specs.append(pl.BlockSpec((3, th, ws), lambda i, pc_ref: (0, i, 0)))
        o_specs.append(pl.BlockSpec((None, th, ws), lambda i, pc_ref: (pc_ref[1], i, 0)))
        out_shape.append(SDS((2, h, ws), F32))

    def body(pc_ref, *refs):
        for a_ref, b_ref, o_ref in zip(refs[:n], refs[n:2 * n], refs[2 * n:]):
            o_ref[...] = ((a_ref[...].astype(F32) + b_ref[0].astype(F32)) + b_ref[1].astype(F32)) + b_ref[2].astype(F32)

    return _call(body, name=name, grid=(REDUCE_STEPS,), in_specs=a_specs + b_specs, out_specs=o_specs,
                 out_shape=out_shape, semantics=("parallel",), args=list(sums) + list(got), prefetch=pc_arr)


def small_reduce(blocks, me_arr):
    _, rows, D = blocks.shape

    def body(me_ref, b_ref, o_ref):
        me = me_ref[0]
        total = b_ref[me]
        for d in range(1, 8):
            total = total + b_ref[d ^ me]
        o_ref[...] = total

    return pl.pallas_call(
        body, name="small_reduce",
        grid_spec=pltpu.PrefetchScalarGridSpec(
            num_scalar_prefetch=1, grid=(1,),
            in_specs=[pl.BlockSpec((8, rows, D), lambda i, me_ref: (0, 0, 0))],
            out_specs=pl.BlockSpec((rows, D), lambda i, me_ref: (0, 0))),
        out_shape=SDS((rows, D), F32),
        compiler_params=_params("arbitrary"),
    )(me_arr, blocks)


def adamw(w, gs, m, v, *, name):
    L, r, cols = w.shape
    tr = _row_tile(r, 256)
    nt = r // tr

    def body(*refs):
        w_ref, m_ref, v_ref = refs[:3]
        g_refs = refs[3:3 + L]
        g_out, d_out, m_out, v_out = refs[3 + L:]
        layer = pl.program_id(0)
        g = g_refs[0][...]
        for l in range(1, L):
            g = jnp.where(layer == l, g_refs[l][...], g)
        m_new = ADAM_B1 * m_ref[...] + (1.0 - ADAM_B1) * g
        v_new = ADAM_B2 * v_ref[...] + (1.0 - ADAM_B2) * (g * g)
        m_hat = m_new / (1.0 - ADAM_B1 ** ADAM_STEP)
        v_hat = v_new / (1.0 - ADAM_B2 ** ADAM_STEP)
        g_out[...] = g
        m_out[...] = m_new
        v_out[...] = v_new
        d_out[...] = -ADAM_LR * (m_hat / (jnp.sqrt(v_hat) + ADAM_EPS) + ADAM_WD * w_ref[...])

    full = pl.BlockSpec((None, tr, cols), lambda l, i: (l, i, 0))
    g_spec = lambda l0: pl.BlockSpec((tr, cols), lambda l, i: (jnp.where(l == l0, i, jnp.where(l < l0, 0, nt - 1)), 0))
    return pl.pallas_call(
        body, name=name, grid=(L, nt),
        in_specs=[full, full, full] + [g_spec(l0) for l0 in range(L)],
        out_specs=[full] * 4,
        out_shape=[SDS(w.shape, F32)] * 4,
        compiler_params=_params("arbitrary", "arbitrary"),
    )(w, m, v, *gs)


def _rms_r(xf):
    return lax.rsqrt(jnp.mean(xf * xf, axis=-1, keepdims=True) + EPS)


def _rmsnorm_bwd(xf, g, dy):
    r = _rms_r(xf)
    xh = xf * r
    gd = g * dy
    return r * (gd - xh * jnp.mean(xh * gd, axis=-1, keepdims=True)), xh


def _dot(a, b):
    return jnp.dot(a, b, preferred_element_type=F32)


def _dot_nt(a, b):
    return lax.dot_general(a, b, (((1,), (1,)), ((), ())), preferred_element_type=F32)


def _dot_tn(a, b):
    return lax.dot_general(a, b, (((0,), (0,)), ((), ())), preferred_element_type=F32)


def _accumulate(ref, first, value):
    @pl.when(first)
    def _():
        ref[...] = value

    @pl.when(jnp.logical_not(first))
    def _():
        ref[...] += value


def norm_matmul(x, g, w, *, tn, split, name, ride=None, tm=ROW_TILE):
    T, D = x.shape
    N = w.shape[1]
    per = N // split // tn

    def body(x_ref, g_ref, w_ref, o_ref, xn_ref):
        @pl.when(pl.program_id(1) == 0)
        def _():
            xf = x_ref[...]
            xn_ref[...] = (xf * _rms_r(xf) * g_ref[...]).astype(BF16)

        o_ref[...] = _dot(xn_ref[...], w_ref[...]).astype(BF16)

    return _call(
        body, name=name, grid=(T // tm, N // tn),
        in_specs=[pl.BlockSpec((tm, D), lambda i, j: (i, 0)),
                  pl.BlockSpec((1, D), lambda i, j: (0, 0)),
                  pl.BlockSpec((D, tn), lambda i, j: (0, j))],
        out_specs=[pl.BlockSpec((None, tm, tn), lambda i, j: (j // per, i, j % per)),
                   pl.BlockSpec((tm, D), lambda i, j: (i, 0))],
        out_shape=[SDS((split, T, N // split), BF16), SDS((T, D), BF16)],
        semantics=("parallel", "arbitrary"), args=(x, g, w), ride=ride)


BIG_ROW_TILE = 1024


def norm2_matmul(x, gains, weights, *, name, tm=BIG_ROW_TILE):
    T, D = x.shape
    tm = min(tm, T)
    n = len(gains)

    def body(x_ref, *refs):
        xf = x_ref[...]
        xh = xf * _rms_r(xf)
        for g_ref, w_ref, o_ref, xn_ref in zip(refs[:n], refs[n:2 * n], refs[2 * n::2], refs[2 * n + 1::2]):
            xn = (xh * g_ref[...]).astype(BF16)
            xn_ref[...] = xn
            o_ref[...] = _dot(xn, w_ref[...]).astype(BF16)

    row = pl.BlockSpec((tm, D), lambda i: (i, 0))
    vec = pl.BlockSpec((1, D), lambda i: (0, 0))
    out_specs, out_shape = [], []
    for w in weights:
        out_specs += [pl.BlockSpec((tm, w.shape[1]), lambda i: (i, 0)), row]
        out_shape += [SDS((T, w.shape[1]), BF16), SDS((T, D), BF16)]
    return _call(
        body, name=name, grid=(T // tm,),
        in_specs=[row] + [vec] * n + [pl.BlockSpec(w.shape, lambda i: (0, 0)) for w in weights],
        out_specs=out_specs, out_shape=out_shape, semantics=("parallel",), args=[x] + list(gains) + list(weights))


def _shift_down(prev, cur, by):
    big = jnp.concatenate([prev, cur], axis=0)
    return pltpu.roll(big, by, 0)[prev.shape[0]:]


def _shift_up(cur, nxt, by):
    big = jnp.concatenate([cur, nxt], axis=0)
    return pltpu.roll(big, big.shape[0] - by, 0)[:cur.shape[0]]


def conv_mix_out(bcx, conv_w, w_out, g_post, res, *, name, ride=None, tm=ROW_TILE):
    T, D = res.shape
    hb = tm // BF16_ROWS

    def body(b_ref, c_ref, u_ref, cp_ref, up_ref, cw_ref, w_ref, g_ref, r_ref, h_ref, z_ref, y_ref):
        i = pl.program_id(0)
        cu = c_ref[...].astype(F32) * u_ref[...].astype(F32)
        cup = cp_ref[...].astype(F32) * up_ref[...].astype(F32)
        cup = jnp.where(i == 0, 0.0, cup)
        cv = (cw_ref[0:1, :] * _shift_down(cup, cu, 2) + cw_ref[1:2, :] * _shift_down(cup, cu, 1)
              + cw_ref[2:3, :] * cu)
        y = (b_ref[...].astype(F32) * cv).astype(BF16)
        y_ref[...] = y
        z = _dot(y, w_ref[...])
        z_ref[...] = z.astype(BF16)
        h_ref[...] = r_ref[...] + z * _rms_r(z) * g_ref[...]

    tile = lambda col: pl.BlockSpec((tm, D), lambda i: (i, col))
    halo = lambda col: pl.BlockSpec((BF16_ROWS, D), lambda i: (jnp.maximum(i * hb - 1, 0), col))
    row = pl.BlockSpec((tm, D), lambda i: (i, 0))
    return _call(
        body, name=name, grid=(T // tm,),
        in_specs=[tile(0), tile(1), tile(2), halo(1), halo(2),
                  pl.BlockSpec((3, D), lambda i: (0, 0)),
                  pl.BlockSpec((D, D), lambda i: (0, 0)),
                  pl.BlockSpec((1, D), lambda i: (0, 0)), row],
        out_specs=[row, row, row],
        out_shape=[SDS((T, D), F32), SDS((T, D), BF16), SDS((T, D), BF16)],
        semantics=("parallel",), args=(bcx, bcx, bcx, bcx, bcx, conv_w, w_out, g_post, res), ride=ride)


def plain_mix_out(a, w, g_post, res, *, name, target=None, ride=None, tm=ROW_TILE):
    T, D = res.shape
    tm = min(tm, T)
    K = a.shape[1]
    with_loss = target is not None

    def body(a_ref, w_ref, g_ref, r_ref, *rest):
        z = _dot(a_ref[...], w_ref[...])
        h = r_ref[...] + z * _rms_r(z) * g_ref[...]
        if with_loss:
            t_ref, h_ref, z_ref, loss_ref = rest
            diff = h - t_ref[...]
            h_ref[...] = diff * (1.0 / D)
            part = jnp.full(loss_ref.shape, 0.5 / D, F32) * jnp.sum(diff * diff)
            _accumulate(loss_ref, pl.program_id(0) == 0, part)
        else:
            h_ref, z_ref = rest
            h_ref[...] = h
        z_ref[...] = z.astype(BF16)

    row = pl.BlockSpec((tm, D), lambda i: (i, 0))
    loss_spec, loss_shape = pl.BlockSpec((8, 128), lambda i: (0, 0)), SDS((8, 128), F32)
    return _call(
        body, name=name, grid=(T // tm,),
        in_specs=[pl.BlockSpec((tm, K), lambda i: (i, 0)),
                  pl.BlockSpec((K, D), lambda i: (0, 0)),
                  pl.BlockSpec((1, D), lambda i: (0, 0)), row] + [row] * with_loss,
        out_specs=[row, row] + [loss_spec] * with_loss,
        out_shape=[SDS((T, D), F32), SDS((T, D), BF16)] + [loss_shape] * with_loss,
        semantics=("arbitrary",), args=(a, w, g_post, res) + ((target,) if with_loss else ()), ride=ride)


def _silu_grads(d, g, u):
    sg = jax.nn.sigmoid(g)
    return d * u * (sg * (1.0 + g * (1.0 - sg))), d * (g * sg)


def norm_swiglu_in(x, g, w, *, name, ride=None, tm=ROW_TILE // 2):
    T, D = x.shape
    F = w.shape[1] // 2

    def body(x_ref, g_ref, wg_ref, wu_ref, gu_ref, a_ref, xt_ref):
        xf = x_ref[...]
        xn = xf * _rms_r(xf) * g_ref[...]
        xt_ref[...] = xn.T.astype(BF16)
        xb = xn.astype(BF16)
        gate = _dot(xb, wg_ref[...]).astype(BF16)
        up = _dot(xb, wu_ref[...]).astype(BF16)
        gu_ref[0] = gate
        gu_ref[1] = up
        a_ref[...] = gate * jax.nn.sigmoid(gate) * up

    half = lambda s: pl.BlockSpec((D, F), lambda i: (0, s), pipeline_mode=pl.Buffered(1))
    return _call(
        body, name=name, grid=(T // tm,),
        in_specs=[pl.BlockSpec((tm, D), lambda i: (i, 0)), pl.BlockSpec((1, D), lambda i: (0, 0)), half(0), half(1)],
        out_specs=[pl.BlockSpec((2, tm, F), lambda i: (0, i, 0)), pl.BlockSpec((tm, F), lambda i: (i, 0)),
                   pl.BlockSpec((D, tm), lambda i: (0, i))],
        out_shape=[SDS((2, T, F), BF16), SDS((T, F), BF16), SDS((D, T), BF16)],
        semantics=("parallel",), args=(x, g, w, w), ride=ride)


def swiglu_bwd_tn(xt, dact, gu, *, name, ride=None, tb=MXU_WIDTH):
    D, T = xt.shape
    F = dact.shape[1]

    def body(xt_ref, d_ref, g_ref, u_ref, o_ref):
        dg, du = _silu_grads(d_ref[...], g_ref[...], u_ref[...])
        o_ref[0] = _dot(xt_ref[...], dg).astype(BF16)
        o_ref[1] = _dot(xt_ref[...], du).astype(BF16)

    col = lambda s: pl.BlockSpec((None, T, tb), lambda j: (s, 0, j))
    out = _call(
        body, name=name, grid=(F // tb,),
        in_specs=[pl.BlockSpec((D, T), lambda j: (0, 0), pipeline_mode=pl.Buffered(1)),
                  pl.BlockSpec((T, tb), lambda j: (0, j)), col(0), col(1)],
        out_specs=[pl.BlockSpec((2, D, tb), lambda j: (0, 0, j))],
        out_shape=[SDS((2, D, F), BF16)],
        semantics=("parallel",), args=(xt, dact, gu, gu), ride=ride)
    return out[0] if ride is None else (out[0][0], out[1])


def swiglu_bwd_in(dact, gu, w, h_in, g, dh_out, *, name, ride=None, tm=ROW_TILE // 2):
    T, D = h_in.shape
    F = dact.shape[1]

    def body(d_ref, gg_ref, uu_ref, wg_ref, wu_ref, h_ref, g_ref, dh_ref, o_ref, dg_ref):
        dgate, dup = _silu_grads(d_ref[...], gg_ref[...], uu_ref[...])
        dn = _dot_nt(dgate, wg_ref[...]) + _dot_nt(dup, wu_ref[...])
        dx, hh = _rmsnorm_bwd(h_ref[...], g_ref[...], dn)
        o_ref[...] = dh_ref[...] + dx
        _accumulate(dg_ref, pl.program_id(0) == 0, jnp.sum(dn * hh, axis=0, keepdims=True))

    row = pl.BlockSpec((tm, D), lambda i: (i, 0))
    vec = pl.BlockSpec((1, D), lambda i: (0, 0))
    part = lambda s: pl.BlockSpec((None, tm, F), lambda i: (s, i, 0))
    half = lambda s: pl.BlockSpec((D, F), lambda i: (0, s), pipeline_mode=pl.Buffered(1))
    return _call(
        body, name=name, grid=(T // tm,),
        in_specs=[pl.BlockSpec((tm, F), lambda i: (i, 0)), part(0), part(1), half(0), half(1), row, vec, row],
        out_specs=[row, vec],
        out_shape=[SDS((T, D), F32), SDS((1, D), F32)],
        semantics=("arbitrary",), args=(dact, gu, gu, w, w, h_in, g, dh_out), ride=ride)


def rope_tables(T):
    half = ROT_DIM // 2
    inv_freq = ROPE_THETA ** (-jnp.arange(0, ROT_DIM, 2, dtype=F32) / ROT_DIM)
    ang = (jnp.arange(T, dtype=F32)[:, None] * inv_freq[None, :]).T
    cos, sin = jnp.cos(ang), jnp.sin(ang)
    rest = HEAD_DIM - ROT_DIM
    one, zero = jnp.ones((rest, T), F32), jnp.zeros((rest, T), F32)
    zh = jnp.zeros((half, T), F32)
    fac = jnp.concatenate([cos, cos, one], axis=0)
    up = jnp.concatenate([-sin, zh, zero], axis=0)
    down = jnp.concatenate([zh, sin, zero], axis=0)
    return jnp.stack([fac, up, down])


def _rope(t, tab):
    half = ROT_DIM // 2
    return t * tab[0] + pltpu.roll(t, HEAD_DIM - half, 0) * tab[1] + pltpu.roll(t, half, 0) * tab[2]


def _rope_t(d, tab):
    half = ROT_DIM // 2
    return d * tab[0] + pltpu.roll(d * tab[1], half, 0) + pltpu.roll(d * tab[2], HEAD_DIM - half, 0)


def _head(t, h):
    return t[h * HEAD_DIM:(h + 1) * HEAD_DIM]


def _band(n, group):
    kj = lax.broadcasted_iota(jnp.int32, (2 * BLOCK, BLOCK), 0)
    qi = lax.broadcasted_iota(jnp.int32, (2 * BLOCK, BLOCK), 1)
    mask = (kj > qi) & (kj <= qi + BLOCK) & ((n > 0) | (kj >= BLOCK))
    return jnp.tile(mask, (1, group))


def _attn_specs(D, kvd):
    prev = lambda n: jnp.maximum(n - 1, 0)
    return [pl.BlockSpec((BLOCK, D), lambda n: (n, 0)),
            pl.BlockSpec((BLOCK, kvd), lambda n: (prev(n), 0)),
            pl.BlockSpec((BLOCK, kvd), lambda n: (n, 0)),
            pl.BlockSpec((BLOCK, kvd), lambda n: (prev(n), 1)),
            pl.BlockSpec((BLOCK, kvd), lambda n: (n, 1)),
            pl.BlockSpec((3, HEAD_DIM, BLOCK), lambda n: (0, 0, prev(n))),
            pl.BlockSpec((3, HEAD_DIM, BLOCK), lambda n: (0, 0, n)),
            pl.BlockSpec(memory_space=pltpu.SMEM)]


def _attn_operands(q_ref, kp_ref, k_ref, vp_ref, v_ref, tp_ref, t_ref):
    flip = lambda ref: ref[...].astype(F32).T
    tab = t_ref[...]
    kt = jnp.concatenate([flip(kp_ref), flip(k_ref)], axis=1)
    vt = jnp.concatenate([flip(vp_ref), flip(v_ref)], axis=1)
    return flip(q_ref), kt, vt, tab, jnp.concatenate([tp_ref[...], tab], axis=2)


SCORE_SCALE = 1.0 / math.sqrt(HEAD_DIM)
HEADS_TOGETHER = 4


def _group_heads(t, first, count, tab=None):
    heads = [_head(t, first + g) for g in range(count)]
    if tab is not None:
        heads = [_rope(h, tab) * SCORE_SCALE for h in heads]
    return jnp.concatenate(heads, axis=1).astype(BF16)


def _sink_row(s_ref, first, count):
    which = lax.broadcasted_iota(jnp.int32, (1, count * BLOCK), 1) // BLOCK
    row = jnp.zeros((1, count * BLOCK), F32)
    for g in range(count):
        row = jnp.where(which == g, s_ref[0, first + g], row)
    return row


def _softmax(scores, sink, mask):
    s = jnp.where(mask, scores, NEG)
    m = jnp.maximum(jnp.max(s, axis=0, keepdims=True), sink)
    e = jnp.exp(s - m)
    es = jnp.exp(sink - m)
    return e, es, 1.0 / (jnp.sum(e, axis=0, keepdims=True) + es)


def attention_fwd(q, kv, tabs, sinks, *, name, ride=None):
    T, D = q.shape
    kvd = kv.shape[1] // 2
    group = D // HEAD_DIM // N_KV_HEADS

    def body(q_ref, kp_ref, k_ref, vp_ref, v_ref, tp_ref, t_ref, s_ref, o_ref):
        gs = HEADS_TOGETHER
        mask = _band(pl.program_id(0), gs)
        qt, kt, vt, tab, tab2 = _attn_operands(q_ref, kp_ref, k_ref, vp_ref, v_ref, tp_ref, t_ref)
        firsts = [(j, first) for j in range(N_KV_HEADS) for first in range(j * group, (j + 1) * group, gs)]
        ks = [_rope(_head(kt, j), tab2).astype(BF16) for j in range(N_KV_HEADS)]
        scores = [_dot_tn(ks[j], _group_heads(qt, first, gs, tab)) for j, first in firsts]
        soft = [_softmax(s, _sink_row(s_ref, first, gs), mask) for s, (j, first) in zip(scores, firsts)]
        outs = []
        for (e, _, inv), (j, first) in zip(soft, firsts):
            o = _dot(_head(vt, j).astype(BF16), e.astype(BF16)) * inv
            outs += [o[:, g * BLOCK:(g + 1) * BLOCK] for g in range(gs)]
        o_ref[...] = jnp.concatenate(outs, axis=0).T.astype(BF16)

    return _call(
        body, name=name, grid=(T // BLOCK,),
        in_specs=_attn_specs(D, kvd),
        out_specs=[pl.BlockSpec((BLOCK, D), lambda n: (n, 0))],
        out_shape=[SDS((T, D), BF16)],
        semantics=("parallel",), args=(q, kv, kv, kv, kv, tabs, tabs, sinks), ride=ride)


def attention_bwd(q, kv, tabs, sinks, do, *, name, ride=None):
    T, D = q.shape
    kvd = kv.shape[1] // 2
    heads = D // HEAD_DIM
    group = heads // N_KV_HEADS

    def body(q_ref, kp_ref, k_ref, vp_ref, v_ref, tp_ref, t_ref, s_ref, do_ref, dq_ref, dc_ref, dp_ref, ds_ref):
        n = pl.program_id(0)
        gs = HEADS_TOGETHER
        mask = _band(n, gs)
        qt, kt, vt, tab, tab2 = _attn_operands(q_ref, kp_ref, k_ref, vp_ref, v_ref, tp_ref, t_ref)
        dot = do_ref[...].astype(F32).T
        lane = lax.broadcasted_iota(jnp.int32, (8, 128), 1)
        dsink = jnp.zeros((8, 128), F32)
        firsts = [(j, first) for j in range(N_KV_HEADS) for first in range(j * group, (j + 1) * group, gs)]
        ks = [_rope(_head(kt, j), tab2).astype(BF16) for j in range(N_KV_HEADS)]
        vs = [_head(vt, j).astype(BF16) for j in range(N_KV_HEADS)]
        qs = [_group_heads(qt, first, gs, tab) for _, first in firsts]
        dos = [_group_heads(dot, first, gs) for _, first in firsts]
        scores = [_dot_tn(ks[j], q) for q, (j, _) in zip(qs, firsts)]
        dps = [_dot_tn(vs[j], do) for do, (j, _) in zip(dos, firsts)]
        es, des, invs = [], [], []
        for s, dp, (j, first) in zip(scores, dps, firsts):
            e, e_sink, inv = _softmax(s, _sink_row(s_ref, first, gs), mask)
            dl = jnp.sum(e * dp, axis=0, keepdims=True) * inv
            des.append((e * (dp - dl)).astype(BF16))
            es.append(e.astype(BF16))
            invs.append(inv)
            weight = e_sink * inv * dl
            for g in range(gs):
                dsink = dsink - jnp.where(lane == first + g, jnp.sum(weight[:, g * BLOCK:(g + 1) * BLOCK]), 0.0)
        dqs = []
        dks = [jnp.zeros((HEAD_DIM, 2 * BLOCK), F32) for _ in range(N_KV_HEADS)]
        dvs = [jnp.zeros((HEAD_DIM, 2 * BLOCK), F32) for _ in range(N_KV_HEADS)]
        for e, de, inv, q, do, (j, _) in zip(es, des, invs, qs, dos, firsts):
            dq = _dot(ks[j], de) * (inv * SCORE_SCALE)
            dqs += [_rope_t(dq[:, g * BLOCK:(g + 1) * BLOCK], tab) for g in range(gs)]
            dks[j] = dks[j] + _dot_nt((q * inv).astype(BF16), de)
            dvs[j] = dvs[j] + _dot_nt((do * inv).astype(BF16), e)
        dks = [_rope_t(dk, tab2) for dk in dks]
        dq_ref[...] = jnp.concatenate(dqs, axis=0).T.astype(BF16)
        dkv = jnp.concatenate(dks + dvs, axis=0)
        dp_ref[...] = dkv[:, :BLOCK].T
        dc_ref[...] = dkv[:, BLOCK:].T
        _accumulate(ds_ref, n == 0, dsink)

    blk = lambda w: pl.BlockSpec((BLOCK, w), lambda n: (n, 0))
    return _call(
        body, name=name, grid=(T // BLOCK,),
        in_specs=_attn_specs(D, kvd) + [blk(D)],
        out_specs=[blk(D), blk(2 * kvd), blk(2 * kvd), pl.BlockSpec((8, 128), lambda n: (0, 0))],
        out_shape=[SDS((T, D), BF16), SDS((T, 2 * kvd), F32), SDS((T, 2 * kvd), F32), SDS((8, 128), F32)],
        semantics=("arbitrary",), args=(q, kv, kv, kv, kv, tabs, tabs, sinks, do), ride=ride)


def combine_dkv(d_cur, d_prev, *, name):
    T, W = d_cur.shape
    tm = ROW_TILE
    nt, per, last = T // tm, tm // BLOCK, T // BLOCK - 1

    def body(c_ref, p_ref, pn_ref, o_ref):
        nxt = jnp.where(pl.program_id(0) == nt - 1, 0.0, pn_ref[...])
        o_ref[...] = (c_ref[...] + jnp.concatenate([p_ref[BLOCK:, :], nxt], axis=0)).astype(BF16)

    return _call(
        body, name=name, grid=(nt,),
        in_specs=[pl.BlockSpec((tm, W), lambda i: (i, 0)), pl.BlockSpec((tm, W), lambda i: (i, 0)),
                  pl.BlockSpec((BLOCK, W), lambda i: (jnp.minimum((i + 1) * per, last), 0))],
        out_specs=[pl.BlockSpec((tm, W), lambda i: (i, 0))],
        out_shape=[SDS((T, W), BF16)],
        semantics=("parallel",), args=(d_cur, d_prev, d_prev))[0]


def normbwd_matmul_nt(z, g, dh, w, *, name, ride=None, tm=ROW_TILE):
    T, D = z.shape
    tm = min(tm, T)
    K = w.shape[0]

    def body(z_ref, g_ref, dh_ref, w_ref, dz_ref, dg_ref, o_ref):
        dh_ = dh_ref[...]
        dz, zh = _rmsnorm_bwd(z_ref[...].astype(F32), g_ref[...], dh_)
        dz = dz.astype(BF16)
        dz_ref[...] = dz
        _accumulate(dg_ref, pl.program_id(0) == 0, jnp.sum(dh_ * zh, axis=0, keepdims=True))
        o_ref[...] = _dot_nt(dz, w_ref[...]).astype(BF16)

    row = pl.BlockSpec((tm, D), lambda i: (i, 0))
    vec = pl.BlockSpec((1, D), lambda i: (0, 0))
    return _call(
        body, name=name, grid=(T // tm,),
        in_specs=[row, vec, row, pl.BlockSpec((K, D), lambda i: (0, 0))],
        out_specs=[row, vec, pl.BlockSpec((tm, K), lambda i: (i, 0))],
        out_shape=[SDS((T, D), BF16), SDS((1, D), F32), SDS((T, K), BF16)],
        semantics=("arbitrary",), args=(z, g, dh, w), ride=ride)


def matmul_nt_normbwd(da, w, h_in, g, dh_out, *, name, ride=None, tm=ROW_TILE):
    T, D = h_in.shape
    S, _, K = da.shape

    def body(*refs):
        da_refs, w_refs = refs[:S], refs[S:2 * S]
        h_ref, g_ref, dh_ref, o_ref, dg_ref = refs[2 * S:]
        dn = _dot_nt(da_refs[0][...], w_refs[0][...])
        for s in range(1, S):
            dn = dn + _dot_nt(da_refs[s][...], w_refs[s][...])
        dx, hh = _rmsnorm_bwd(h_ref[...], g_ref[...], dn)
        o_ref[...] = dh_ref[...] + dx
        _accumulate(dg_ref, pl.program_id(0) == 0, jnp.sum(dn * hh, axis=0, keepdims=True))

    row = pl.BlockSpec((tm, D), lambda i: (i, 0))
    vec = pl.BlockSpec((1, D), lambda i: (0, 0))
    part = lambda s: pl.BlockSpec((None, tm, K), lambda i: (s, i, 0))
    cols = lambda s: pl.BlockSpec((D, K), lambda i: (0, s), pipeline_mode=pl.Buffered(1))
    return _call(
        body, name=name, grid=(T // tm,),
        in_specs=[part(s) for s in range(S)] + [cols(s) for s in range(S)] + [row, vec, row],
        out_specs=[row, vec],
        out_shape=[SDS((T, D), F32), SDS((1, D), F32)],
        semantics=("arbitrary",), args=[da] * S + [w] * S + [h_in, g, dh_out], ride=ride)


def matmuls_nt_normbwd(das, ws, h_in, gs, dh_out, *, name, ride=None, tm=BIG_ROW_TILE):
    T, D = h_in.shape
    tm = min(tm, T)
    n = len(das)

    def body(*refs):
        da_refs, w_refs, g_refs = refs[:n], refs[n:2 * n], refs[2 * n:3 * n]
        h_ref, dh_ref, o_ref = refs[3 * n:3 * n + 3]
        hf = h_ref[...]
        r = _rms_r(hf)
        hh = hf * r
        total = dh_ref[...]
        for da_ref, w_ref, g_ref, dg_ref in zip(da_refs, w_refs, g_refs, refs[3 * n + 3:]):
            dn = _dot_nt(da_ref[...], w_ref[...])
            gd = g_ref[...] * dn
            total = total + r * (gd - hh * jnp.mean(hh * gd, axis=-1, keepdims=True))
            _accumulate(dg_ref, pl.program_id(0) == 0, jnp.sum(dn * hh, axis=0, keepdims=True))
        o_ref[...] = total

    row = pl.BlockSpec((tm, D), lambda i: (i, 0))
    vec = pl.BlockSpec((1, D), lambda i: (0, 0))
    return _call(
        body, name=name, grid=(T // tm,),
        in_specs=[pl.BlockSpec((tm, da.shape[1]), lambda i: (i, 0)) for da in das]
        + [pl.BlockSpec(w.shape, lambda i: (0, 0)) for w in ws] + [vec] * n + [row, row],
        out_specs=[row] + [vec] * n,
        out_shape=[SDS((T, D), F32)] + [SDS((1, D), F32)] * n,
        semantics=("arbitrary",), args=list(das) + list(ws) + list(gs) + [h_in, dh_out], ride=ride)


def matmul_tn(a, b, *, tb, name, ride=None, ta=MXU_WIDTH):
    T, Ka = a.shape
    S, _, Nb = b.shape
    per = Nb // tb

    def body(a_ref, b_ref, o_ref):
        o_ref[...] = _dot_tn(a_ref[...], b_ref[...]).astype(BF16)

    out = _call(
        body, name=name, grid=(S * per, Ka // ta),
        in_specs=[pl.BlockSpec((T, ta), lambda j, i: (0, i)),
                  pl.BlockSpec((None, T, tb), lambda j, i: (j // per, 0, j % per))],
        out_specs=[pl.BlockSpec((ta, tb), lambda j, i: (i, j))],
        out_shape=[SDS((Ka, S * Nb), BF16)],
        semantics=("parallel", "parallel"), args=(a, b), ride=ride)
    return out[0] if ride is None else (out[0][0], out[1])


def conv_bwd(dy, bcx, conv_w, *, name, ride=None, tm=ROW_TILE):
    T, D = dy.shape
    nt = T // tm
    hb = tm // BF16_ROWS
    last = T // BF16_ROWS - 1

    def body(dy_ref, dyn_ref, b_ref, bn_ref, c_ref, u_ref, cp_ref, up_ref, cw_ref, o_ref, dw_ref):
        i = pl.program_id(0)
        c, u = c_ref[...].astype(F32), u_ref[...].astype(F32)
        cu = c * u
        cup = jnp.where(i == 0, 0.0, cp_ref[...].astype(F32) * up_ref[...].astype(F32))
        cu1, cu2 = _shift_down(cup, cu, 1), _shift_down(cup, cu, 2)
        w0, w1, w2 = cw_ref[0:1, :], cw_ref[1:2, :], cw_ref[2:3, :]
        dyf = dy_ref[...].astype(F32)
        o_ref[:, 0:D] = (dyf * (w0 * cu2 + w1 * cu1 + w2 * cu)).astype(BF16)
        dcv = dyf * b_ref[...].astype(F32)
        dcvn = jnp.where(i == nt - 1, 0.0, dyn_ref[...].astype(F32) * bn_ref[...].astype(F32))
        dcu = w2 * dcv + w1 * _shift_up(dcv, dcvn, 1) + w0 * _shift_up(dcv, dcvn, 2)
        o_ref[:, D:2 * D] = (dcu * u).astype(BF16)
        o_ref[:, 2 * D:3 * D] = (dcu * c).astype(BF16)
        row = lax.broadcasted_iota(jnp.int32, (8, D), 0)
        dw = jnp.zeros((8, D), F32)
        for tap, t in enumerate((cu2, cu1, cu)):
            dw = jnp.where(row == tap, jnp.sum(dcv * t, axis=0, keepdims=True), dw)
        _accumulate(dw_ref, i == 0, dw)

    tile = lambda col: pl.BlockSpec((tm, D), lambda i: (i, col))
    prev = lambda col: pl.BlockSpec((BF16_ROWS, D), lambda i: (jnp.maximum(i * hb - 1, 0), col))
    nxt = lambda col: pl.BlockSpec((BF16_ROWS, D), lambda i: (jnp.minimum((i + 1) * hb, last), col))
    return _call(
        body, name=name, grid=(nt,),
        in_specs=[tile(0), nxt(0), tile(0), nxt(0), tile(1), tile(2), prev(1), prev(2),
                  pl.BlockSpec((3, D), lambda i: (0, 0))],
        out_specs=[pl.BlockSpec((tm, 3 * D), lambda i: (i, 0)), pl.BlockSpec((8, D), lambda i: (0, 0))],
        out_shape=[SDS((T, 3 * D), BF16), SDS((8, D), F32)],
        semantics=("arbitrary",), args=(dy, dy, bcx, bcx, bcx, bcx, bcx, bcx, conv_w), ride=ride)


class NoTraffic:
    def ride(self, kernel_name):
        return None

    def landed(self, kernel_name, results, wts):
        pass

    def grad(self, key, value):
        pass


def local_step(x, target, wts, vec, traffic):
    T, D = x.shape
    tabs = rope_tables(T)
    small = {}

    def run(builder, *args, name, **kw):
        ride = traffic.ride(name)
        if ride is None:
            return builder(*args, name=name, **kw)
        out, extra = builder(*args, name=name, ride=ride, **kw)
        traffic.landed(name, extra, wts)
        return out

    bcx, xn1 = run(norm_matmul, x, vec["a_pre"], wts["w_in"], tn=3 * D, split=1, name="a_in")
    bcx = bcx[0]
    h1, z0, y0 = run(conv_mix_out, bcx, vec["conv_w"], wts["w_out"], vec["a_post"], x, name="a_out")
    gu0, act0, xt2 = run(norm_swiglu_in, h1, vec["ffn_pre0"], wts["gu0"], name="ffn0_in")
    h2, z1 = run(plain_mix_out, act0, wts["wd0"], vec["ffn_post0"], h1, name="ffn0_out")
    kvp, xkv, qp, xq = norm2_matmul(h2, [vec["kv_norm"], vec["b_pre"]], [wts["w_kv"], wts["w_q"]], name="kvq_in")
    (attn,) = run(attention_fwd, qp, kvp, tabs, vec["sinks"], name="attn_fwd")
    h3, z2 = plain_mix_out(attn, wts["w_o"], vec["b_post"], h2, name="attn_out", tm=BIG_ROW_TILE)
    gu1, act1, xt3 = run(norm_swiglu_in, h3, vec["ffn_pre1"], wts["gu1"], name="ffn1_in")
    dy, z3, loss = plain_mix_out(act1, wts["wd1"], vec["ffn_post1"], h3, name="ffn1_out", target=target)

    def ffn_bwd(layer, z, gu, act, xt, h_in, dh, gu_first):
        tag = "ffn%d" % layer
        dz, small["ffn_post%d" % layer], dact = run(
            normbwd_matmul_nt, z, vec["ffn_post%d" % layer], dh, wts["wd%d" % layer], name=tag + "_out_bwd")
        dwd = lambda: traffic.grad("wd%d" % layer, run(matmul_tn, act, dz[None], tb=D, name=tag + "_dwd"))
        dwgu = lambda: traffic.grad("gu%d" % layer, run(swiglu_bwd_tn, xt, dact, gu, name=tag + "_dwgu"))
        for step in ((dwgu, dwd) if gu_first else (dwd, dwgu)):
            step()
        dh_in, small["ffn_pre%d" % layer] = run(
            swiglu_bwd_in, dact, gu, wts["gu%d" % layer], h_in, vec["ffn_pre%d" % layer], dh, name=tag + "_in_bwd")
        return dh_in

    dh3 = ffn_bwd(1, z3, gu1, act1, xt3, h3, dy, gu_first=False)
    dz2, small["b_post"], dattn = normbwd_matmul_nt(z2, vec["b_post"], dh3, wts["w_o"], name="attn_out_bwd",
                                                    tm=BIG_ROW_TILE)
    traffic.grad("w_o", matmul_tn(attn, dz2[None], tb=D, name="attn_dwo"))
    dq, dkv_cur, dkv_prev, small["sinks"] = run(attention_bwd, qp, kvp, tabs, vec["sinks"], dattn, name="attn_bwd")
    dkv = combine_dkv(dkv_cur, dkv_prev, name="attn_dkv")
    traffic.grad("w_q", matmul_tn(xq, dq[None], tb=D, name="attn_dwq"))
    traffic.grad("w_kv", matmul_tn(xkv, dkv[None], tb=dkv.shape[1], name="attn_dwkv"))
    dh2, small["b_pre"], small["kv_norm"] = run(
        matmuls_nt_normbwd, [dq, dkv], [wts["w_q"], wts["w_kv"]], h2, [vec["b_pre"], vec["kv_norm"]], dh3,
        name="qkv_in_bwd")
    dh1 = ffn_bwd(0, z1, gu0, act0, xt2, h1, dh2, gu_first=True)
    dz0, small["a_post"], dyc = run(normbwd_matmul_nt, z0, vec["a_post"], dh1, wts["w_out"], name="a_out_bwd",
                                    tm=BIG_ROW_TILE)
    traffic.grad("w_out", matmul_tn(y0, dz0[None], tb=D, name="a_dwout"))
    dbcx, small["conv_w"] = run(conv_bwd, dyc, bcx, vec["conv_w"], name="a_conv_bwd")
    traffic.grad("w_in", matmul_tn(xn1, dbcx[None], tb=3 * D // 2, name="a_dwin"))
    dx, small["a_pre"] = run(matmul_nt_normbwd, dbcx[None], wts["w_in"], x, vec["a_pre"], dh1, name="a_in_bwd",
                             tm=ROW_TILE // 2)
    return loss, dx, small


SMALL_ROWS = 16
LOSS_ROW = 13

WHOLE = None
GATHER_PLAN = {"cast_rest": [("w_in", WHOLE)],
               "a_in": [("w_out", WHOLE), ("gu0", (0, 18))],
               "a_out": [("gu0", (18, 14))],
               "ffn0_in": [("wd0", WHOLE), ("w_kv", WHOLE), ("w_q", WHOLE), ("w_o", WHOLE)],
               "ffn0_out": [("gu1", (0, 16))],
               "attn_fwd": [("gu1", (16, 16))],
               "ffn1_in": [("wd1", WHOLE)]}
PAIR_PLAN = {"ffn1_dwgu": ["wd1"], "ffn1_in_bwd": ["gu1"], "attn_bwd": ["w_o"], "qkv_in_bwd": ["w_q", "w_kv"],
             "ffn0_dwd": ["gu0"], "ffn0_in_bwd": ["wd0"], "a_conv_bwd": ["w_out"]}
PAIR_ALONE = ["w_in"]
CHIP_PLAN = {"ffn1_in_bwd": [("wd1", WHOLE)], "attn_bwd": [("gu1", WHOLE)],
             "ffn0_out_bwd": [("w_o", WHOLE), ("w_q", WHOLE), ("w_kv", WHOLE)],
             "ffn0_in_bwd": [("gu0", WHOLE)], "a_out_bwd": [("wd0", (0, 14))], "a_conv_bwd": [("wd0", (14, 8))],
             "a_in_bwd": [("w_out", WHOLE), ("w_in", WHOLE)]}
HALF_PLAN = {"a_in_bwd": ["gu0", "gu1", "wd0", "wd1", "w_kv", "w_q", "w_o"]}
GRAD_KIND = dict(KIND, gu0="split", gu1="split")


class Traffic:
    def __init__(self, wholes, quarter, c_arr, pc_arr):
        self.wholes, self.quarter, self.c_arr, self.pc_arr = wholes, quarter, c_arr, pc_arr
        self.views, self.sums, self.got = {}, {}, {}
        self.reduced = {}
        self.stages = {}

    def reduce(self, keys, name):
        return chip_reduce([self.sums[k] for k in keys], [self.got[k] for k in keys], [GRAD_KIND[k] for k in keys],
                           self.pc_arr, name=name)

    def ride(self, name, small=None):
        rides, stages = [], []
        if name in GATHER_PLAN:
            plan = GATHER_PLAN[name]
            rides.append(gather_ride([self.wholes[k] for k, _ in plan],
                                     [(KIND[k], self.quarter[k], part) for k, part in plan], small))
            stages.append(("gather", [k for k, _ in plan]))
        if name in CHIP_PLAN:
            plan = CHIP_PLAN[name]
            rides.append(chip_ride([self.sums[k] for k, _ in plan],
                                   [(GRAD_KIND[k], self.quarter[k], part) for k, part in plan],
                                   earlier=[self.got.get(k) for k, _ in plan]))
            stages.append(("chip", [k for k, _ in plan]))
        if name in PAIR_PLAN:
            keys = PAIR_PLAN[name]
            rides.append(pair_ride([self.views[k] for k in keys]))
            stages.append(("pair", keys))
        if name in HALF_PLAN:
            keys = HALF_PLAN[name]
            rides.append(half_ride(self.reduce(keys, "chip_reduce_early")))
            stages.append(("half", keys))
        self.stages[name] = stages
        return join(rides)

    def landed(self, name, results, wts):
        results = list(results)
        for stage, keys in self.stages[name]:
            mine, results = results[:len(keys)], results[len(keys):]
            if stage == "gather":
                for k, whole in zip(keys, mine):
                    self.wholes[k] = wts[k] = whole
            elif stage == "chip":
                self.got.update(zip(keys, mine))
            elif stage == "half":
                self.reduced.update(zip(keys, mine))
            else:
                for k, got in zip(keys, mine):
                    self.sums[k] = pair_add(self.views[k], got, self.c_arr, name="pair_add_" + k)

    def grad(self, key, value):
        r, ws = self.quarter[key]
        view = {"row": (N_CHIPS, 2, r // 2, ws), "col": (1, 2, r // 2, N_CHIPS * ws), "split": (2, 2, r // 2, 2 * ws)}
        self.views[key] = value.reshape(view[GRAD_KIND[key]])
        if key in PAIR_ALONE:
            (got,) = alone(pair_ride([self.views[key]]), name="pair_exchange_" + key)
            self.sums[key] = pair_add(self.views[key], got, self.c_arr, name="pair_add_" + key)


def kernel(x, a_pre_norm, a_w_in, a_conv_w, a_w_out, a_post_norm, ffn_pre_norm, ffn_w_gate_up, ffn_w_down, ffn_post_norm, kv_norm, w_kv, b_pre_norm, b_w_q, b_sinks, b_w_o, b_post_norm, loss_target, m_a_pre_norm, m_a_w_in, m_a_conv_w, m_a_w_out, m_a_post_norm, m_ffn_pre_norm, m_ffn_w_gate_up, m_ffn_w_down, m_ffn_post_norm, m_kv_norm, m_w_kv, m_b_pre_norm, m_b_w_q, m_b_sinks, m_b_w_o, m_b_post_norm, v_a_pre_norm, v_a_w_in, v_a_conv_w, v_a_w_out, v_a_post_norm, v_ffn_pre_norm, v_ffn_w_gate_up, v_ffn_w_down, v_ffn_post_norm, v_kv_norm, v_w_kv, v_b_pre_norm, v_b_w_q, v_b_sinks, v_b_w_o, v_b_post_norm):
    T, D = x.shape[1], x.shape[2]
    xi, yi, ci = _place()
    p = 2 * xi + yi
    p_arr = jnp.reshape(p, (1,)).astype(jnp.int32)
    c_arr = jnp.reshape(ci, (1,)).astype(jnp.int32)
    pc_arr = jnp.stack([p, ci]).astype(jnp.int32)
    me_arr = jnp.reshape(4 * xi + 2 * yi + ci, (1,)).astype(jnp.int32)
    qd = D // N_CHIPS

    big = {"w_in": (a_w_in, 0), "w_out": (a_w_out, 0), "gu0": (ffn_w_gate_up, 0), "gu1": (ffn_w_gate_up, 1),
           "wd0": (ffn_w_down, 0), "wd1": (ffn_w_down, 1), "w_kv": (w_kv[None], 0), "w_q": (b_w_q, 0),
           "w_o": (b_w_o, 0)}
    names = list(big)
    quarter = {k: w.shape[1:] for k, (w, _) in big.items()}
    source = lambda k: big[k] + (KIND[k],)
    traffic = Traffic(dict(zip(names[:1], cast_quarters([source(names[0])], p_arr, name="cast_first"))), quarter,
                      c_arr, pc_arr)
    small_shard = jnp.concatenate([a_pre_norm, a_post_norm, a_conv_w[0], jnp.zeros((3, qd), F32)], axis=0)
    wts = {}
    rest, (*landed, small_full) = cast_quarters([source(k) for k in names[1:]], p_arr, name="cast_rest",
                                                ride=traffic.ride("cast_rest", small_shard))
    traffic.wholes.update(zip(names[1:], rest))
    traffic.landed("cast_rest", landed, wts)
    rows = lambda k: jnp.transpose(small_full[:, k], (1, 0, 2)).reshape(-1, D)
    vec = {"a_pre": rows(slice(0, 1)), "a_post": rows(slice(1, 2)), "conv_w": rows(slice(2, 5)),
           "ffn_pre0": ffn_pre_norm[0:1], "ffn_pre1": ffn_pre_norm[1:2],
           "ffn_post0": ffn_post_norm[0:1], "ffn_post1": ffn_post_norm[1:2],
           "kv_norm": kv_norm[None], "b_pre": b_pre_norm, "b_post": b_post_norm, "sinks": b_sinks}

    loss, dx, small = local_step(x[0], loss_target[0], wts, vec, traffic)

    pad = lambda a: jnp.pad(a, ((0, 0), (0, D - a.shape[1])))
    small_block = jnp.concatenate(
        [small["a_pre"], small["a_post"], small["conv_w"][0:3], small["ffn_pre0"], small["ffn_pre1"],
         small["ffn_post0"], small["ffn_post1"], small["kv_norm"], small["b_pre"], small["b_post"],
         pad(small["sinks"][0:1]), pad(loss[0:1]), jnp.zeros((SMALL_ROWS - LOSS_ROW - 1, D), F32)], axis=0)
    late = [k for k in names if k not in traffic.reduced]
    *swapped, small_blocks = alone(join([half_ride(traffic.reduce(late, "chip_reduce_late")),
                                         chip_ride([], [], small_block)]), name="last_exchange")
    traffic.reduced.update(zip(late, swapped))
    grad = {k: traffic.reduced[k].reshape(quarter[k]) for k in names}
    small_sum = small_reduce(small_blocks, me_arr)

    out = {}
    out["a_w_in"] = adamw(a_w_in, [grad["w_in"]], m_a_w_in, v_a_w_in, name="adamw_a_w_in")
    out["a_w_out"] = adamw(a_w_out, [grad["w_out"]], m_a_w_out, v_a_w_out, name="adamw_a_w_out")
    out["ffn_w_gate_up"] = adamw(ffn_w_gate_up, [grad["gu0"], grad["gu1"]], m_ffn_w_gate_up, v_ffn_w_gate_up,
                                 name="adamw_ffn_w_gate_up")
    out["ffn_w_down"] = adamw(ffn_w_down, [grad["wd0"], grad["wd1"]], m_ffn_w_down, v_ffn_w_down,
                              name="adamw_ffn_w_down")
    out["w_kv"] = [o[0] for o in adamw(w_kv[None], [grad["w_kv"]], m_w_kv[None], v_w_kv[None], name="adamw_w_kv")]
    out["b_w_q"] = adamw(b_w_q, [grad["w_q"]], m_b_w_q, v_b_w_q, name="adamw_b_w_q")
    out["b_w_o"] = adamw(b_w_o, [grad["w_o"]], m_b_w_o, v_b_w_o, name="adamw_b_w_o")

    def pack(a_pre, a_post, conv, ffn_pre, ffn_post, kvn, b_pre, b_post, sinks):
        return jnp.concatenate([pad(a_pre), pad(a_post), pad(conv[0]), ffn_pre, ffn_post, kvn[None], b_pre, b_post,
                                pad(sinks), jnp.zeros((SMALL_ROWS - 13, D), F32)], axis=0)

    g_small = jnp.concatenate([pad(lax.dynamic_slice(small_sum, (0, p * qd), (5, qd))), small_sum[5:]], axis=0)
    w_small = pack(a_pre_norm, a_post_norm, a_conv_w, ffn_pre_norm, ffn_post_norm, kv_norm, b_pre_norm, b_post_norm,
                   b_sinks)
    m_small = pack(m_a_pre_norm, m_a_post_norm, m_a_conv_w, m_ffn_pre_norm, m_ffn_post_norm, m_kv_norm,
                   m_b_pre_norm, m_b_post_norm, m_b_sinks)
    v_small = pack(v_a_pre_norm, v_a_post_norm, v_a_conv_w, v_ffn_pre_norm, v_ffn_post_norm, v_kv_norm,
                   v_b_pre_norm, v_b_post_norm, v_b_sinks)
    packed = adamw(w_small[None], [g_small], m_small[None], v_small[None], name="adamw_small")
    ns = b_sinks.shape[1]
    unpack = lambda a: {"a_pre_norm": a[0:1, :qd], "a_post_norm": a[1:2, :qd], "a_conv_w": a[None, 2:5, :qd],
                        "ffn_pre_norm": a[5:7], "ffn_post_norm": a[7:9], "kv_norm": a[9], "b_pre_norm": a[10:11],
                        "b_post_norm": a[11:12], "b_sinks": a[12:13, :ns]}
    unpacked = [unpack(a[0]) for a in packed]
    for k in unpacked[0]:
        out[k] = [u[k] for u in unpacked]

    order = ["a_pre_norm", "a_w_in", "a_conv_w", "a_w_out", "a_post_norm", "ffn_pre_norm", "ffn_w_gate_up",
             "ffn_w_down", "ffn_post_norm", "kv_norm", "w_kv", "b_pre_norm", "b_w_q", "b_sinks", "b_w_o",
             "b_post_norm"]
    return (small_sum[LOSS_ROW, 0], dx[None], *[out[k][0] for k in order], *[out[k][1] for k in order],
            *[out[k][2] for k in order], *[out[k][3] for k in order])
```

```python
import math

import jax
import jax.numpy as jnp
from jax import lax
from jax.experimental import pallas as pl
from jax.experimental.pallas import tpu as pltpu

F32 = jnp.float32
BF16 = jnp.bfloat16
SDS = jax.ShapeDtypeStruct
MESH = pl.DeviceIdType.MESH
DMA = pltpu.SemaphoreType.DMA
HBM_SPEC = pl.BlockSpec(memory_space=pltpu.HBM)

EPS = 1e-6
NEG = -1e30
HEAD_DIM = 64
N_KV_HEADS = 4
BLOCK = 128
ROT_DIM = HEAD_DIM // 4
ROPE_THETA = 500000.0
N_CHIPS = 4

ADAM_LR = 0.001
ADAM_B1 = 0.9
ADAM_B2 = 0.999
ADAM_EPS = 1e-08
ADAM_WD = 0.01
ADAM_STEP = 10

VMEM_LIMIT_BYTES = 52 * 1024 * 1024
ROW_TILE = 512
BF16_ROWS = 16
STREAM = BF16
MXU_WIDTH = 256

KIND = {"w_in": "col", "gu0": "col", "gu1": "col", "w_out": "row", "wd0": "row", "wd1": "row", "w_kv": "row",
        "w_q": "row", "w_o": "row"}


def _params(*semantics):
    return pltpu.CompilerParams(dimension_semantics=semantics, vmem_limit_bytes=VMEM_LIMIT_BYTES)


def _row_tile(rows, limit, step=8):
    return max(t for t in range(step, limit + 1, step) if rows % t == 0)


def _place():
    return lax.axis_index("x"), lax.axis_index("y"), lax.axis_index("c")


def _other_chips(x, y):
    return [(1 - x, y), (x, 1 - y), (1 - x, 1 - y)]


def _remote(src, dst, send_sem, recv_sem, to):
    return pltpu.make_async_remote_copy(src_ref=src, dst_ref=dst, send_sem=send_sem, recv_sem=recv_sem,
                                        device_id=to, device_id_type=MESH)


def _full_shape(kind, quarter):
    r, ws = quarter
    return (N_CHIPS * r, ws) if kind == "row" else (r, N_CHIPS * ws)


def _rows_of(h, part):
    lo, n = (0, h) if part is None else (part[0] * BF16_ROWS, part[1] * BF16_ROWS)
    assert lo + n <= h, (h, part)
    return lo, n


def _half_of_quarter(ref, kind, quarter, part, q, half):
    r, ws = quarter
    h = r // 2
    lo, n = _rows_of(h, part)
    if kind == "row":
        return ref.at[pl.ds(pl.multiple_of(q * r + half * h + lo, BF16_ROWS), n)]
    return ref.at[pl.ds(pl.multiple_of(half * h + lo, BF16_ROWS), n), pl.ds(pl.multiple_of(q * ws, 128), ws)]


class Ride:
    def __init__(self, operands, out_shape, aliases, sems, make):
        self.operands, self.out_shape, self.aliases, self.sems, self.make = operands, out_shape, aliases, sems, make


def join(rides):
    rides = [r for r in rides if r is not None]
    if len(rides) < 2:
        return rides[0] if rides else None
    aliases, at = {}, [0, 0, 0]
    cuts = []
    for r in rides:
        aliases.update({at[0] + i: at[1] + o for i, o in r.aliases.items()})
        cuts.append(tuple(at))
        at = [at[0] + len(r.operands), at[1] + len(r.out_shape), at[2] + len(r.sems)]
    cuts.append(tuple(at))

    def make(ins, outs, sem):
        made = [r.make(ins[lo[0]:hi[0]], outs[lo[1]:hi[1]], sem[lo[2]:hi[2]]) for r, lo, hi in zip(rides, cuts, cuts[1:])]

        def start():
            for s, _ in made:
                s()

        def finish():
            for _, f in made:
                f()

        return start, finish

    return Ride(sum((list(r.operands) for r in rides), []), sum((list(r.out_shape) for r in rides), []), aliases,
                sum((list(r.sems) for r in rides), []), make)


def _call(body, *, name, grid, in_specs, out_specs, out_shape, args, scratch_shapes=(), semantics=None, ride=None,
          prefetch=None):
    pre = 0 if prefetch is None else 1
    n_in, n_out, n_scr = len(in_specs), len(out_specs), len(scratch_shapes)
    r_in, r_out = (len(ride.operands), len(ride.out_shape)) if ride is not None else (0, 0)
    a, b = pre + n_in, pre + n_in + r_in
    c, d = b + n_out, b + n_out + r_out
    e = d + n_scr

    def riding(*refs):
        start, finish = ride.make(refs[a:b], refs[c:d], refs[e:])
        ids = [pl.program_id(k) for k in range(len(grid))]
        first, last = ids[0] == 0, ids[0] == grid[0] - 1
        for k in range(1, len(grid)):
            first, last = first & (ids[k] == 0), last & (ids[k] == grid[k] - 1)
        pl.when(first)(start)
        body(*refs[:a], *refs[b:c], *refs[d:e])
        pl.when(last)(finish)

    if ride is None:
        kernel_body, extra_in, extra_out, extra_shape, extra_scr, aliases = body, [], [], [], [], {}
        params = _params(*semantics)
    else:
        kernel_body, extra_in, extra_out = riding, [HBM_SPEC] * r_in, [HBM_SPEC] * r_out
        extra_shape, extra_scr = list(ride.out_shape), list(ride.sems)
        aliases = {pre + n_in + i: n_out + o for i, o in ride.aliases.items()}
        params = _params(*(("arbitrary",) * len(grid)))
    specs = dict(grid=grid, in_specs=list(in_specs) + extra_in, out_specs=list(out_specs) + extra_out,
                 scratch_shapes=list(scratch_shapes) + extra_scr)
    if prefetch is not None:
        specs = dict(grid_spec=pltpu.PrefetchScalarGridSpec(num_scalar_prefetch=1, **specs))
        args = (prefetch,) + tuple(args)
    outs = pl.pallas_call(kernel_body, name=name, out_shape=list(out_shape) + extra_shape,
                          input_output_aliases=aliases, compiler_params=params, **specs,
                          )(*args, *(ride.operands if ride is not None else ()))
    return outs if ride is None else (outs[:n_out], outs[n_out:])


def alone(ride, *, name):
    def body(*refs):
        n = len(ride.operands)
        start, finish = ride.make(refs[:n], refs[n:n + len(ride.out_shape)], refs[n + len(ride.out_shape):])
        start()
        finish()

    return pl.pallas_call(
        body, name=name, in_specs=[HBM_SPEC] * len(ride.operands), out_specs=[HBM_SPEC] * len(ride.out_shape),
        out_shape=list(ride.out_shape), input_output_aliases=dict(ride.aliases), scratch_shapes=list(ride.sems),
    )(*ride.operands)


def gather_ride(wholes, metas, small=None):
    n = len(wholes)
    operands, out_shape = list(wholes), [SDS(s.shape, s.dtype) for s in wholes]
    sems = [DMA((n, 3)), DMA((n, 3)), DMA((n, 3)), DMA((n, 3))]
    if small is not None:
        operands.append(small)
        out_shape.append(SDS((N_CHIPS,) + small.shape, small.dtype))
        sems += [DMA((3,)), DMA((3,)), DMA(())]

    def make(ins, outs, sem):
        send1, recv1, send2, recv2 = sem[:4]
        x, y, c = _place()
        p = 2 * x + y
        chips = _other_chips(x, y)
        me, sibling = (x, y, c), (x, y, 1 - c)
        part = lambda t, q, half: _half_of_quarter(outs[t], *metas[t], q, half)
        first = []
        for j, (qx, qy) in enumerate(chips):
            if small is not None:
                first.append(_remote(ins[n], outs[n].at[p], sem[4].at[j], sem[5].at[j], (qx, qy, c)))
            for t in range(n):
                first.append(_remote(part(t, p, c), part(t, p, c), send1.at[t, j], recv1.at[t, j], (qx, qy, c)))
        local = [] if small is None else [pltpu.make_async_copy(ins[n], outs[n].at[p], sem[6])]

        def start():
            for cp in local + first:
                cp.start()

        def finish():
            passed = []
            for j, (qx, qy) in enumerate(chips):
                q = 2 * qx + qy
                for t in range(n):
                    landed = part(t, q, c)
                    _remote(landed, landed, send1.at[t, j], recv1.at[t, j], me).wait_recv()
                    cp = _remote(landed, landed, send2.at[t, j], recv2.at[t, j], sibling)
                    cp.start()
                    passed.append(cp)
            for j, (qx, qy) in enumerate(chips):
                q = 2 * qx + qy
                if small is not None:
                    _remote(outs[n].at[q], outs[n].at[q], sem[4].at[j], sem[5].at[j], me).wait_recv()
                for t in range(n):
                    theirs = part(t, q, 1 - c)
                    _remote(theirs, theirs, send2.at[t, j], recv2.at[t, j], me).wait_recv()
            for cp in first + passed:
                cp.wait_send()
            for cp in local:
                cp.wait()

        return start, finish

    return Ride(operands, out_shape, {t: t for t in range(n)}, sems, make)


def chip_ride(sums, metas, small=None, earlier=None):
    n = len(sums)
    operands = list(sums)
    out_shape = [SDS((3, s.shape[1], quarter[1]), s.dtype) for s, (_, quarter, _) in zip(sums, metas)]
    sems = [DMA((n, 3)), DMA((n, 3))] if n else []
    if small is not None:
        operands.append(small)
        out_shape.append(SDS((8,) + small.shape, small.dtype))
        sems += [DMA((7,)), DMA((7,)), DMA(())]
    aliases = {}
    for t, buffer in enumerate(earlier or [None] * n):
        if buffer is not None:
            aliases[len(operands)] = t
            operands.append(buffer)

    def make(ins, outs, sem):
        x, y, c = _place()
        cps = []
        for j, (qx, qy) in enumerate(_other_chips(x, y)):
            q = 2 * qx + qy
            for t in range(n):
                kind, (_, ws), part = metas[t]
                rows = pl.ds(*_rows_of(ins[t].shape[1], part))
                if kind == "row":
                    src = ins[t].at[q, rows]
                elif kind == "col":
                    src = ins[t].at[0, rows, pl.ds(pl.multiple_of(q * ws, 128), ws)]
                else:
                    src = ins[t].at[q // 2, rows, pl.ds(pl.multiple_of((q % 2) * ws, 128), ws)]
                cps.append(_remote(src, outs[t].at[j, rows], sem[0].at[t, j], sem[1].at[t, j], (qx, qy, c)))
        local = []
        if small is not None:
            ssend, srecv, lsem = sem[2 * bool(n):2 * bool(n) + 3]
            local.append(pltpu.make_async_copy(ins[n], outs[n].at[0], lsem))
            for k in range(1, 8):
                peer = (x ^ (k >> 2 & 1), y ^ (k >> 1 & 1), c ^ (k & 1))
                cps.append(_remote(ins[n], outs[n].at[k], ssend.at[k - 1], srecv.at[k - 1], peer))

        def start():
            for cp in local + cps:
                cp.start()

        def finish():
            for cp in cps + local:
                cp.wait()

        return start, finish

    return Ride(operands, out_shape, aliases, sems, make)


def pair_ride(grads):
    n = len(grads)

    def make(ins, outs, sem):
        x, y, c = _place()
        cps = [_remote(ins[t].at[:, 1 - c], outs[t], sem[0].at[t], sem[1].at[t], (x, y, 1 - c)) for t in range(n)]

        def start():
            for cp in cps:
                cp.start()

        def finish():
            for cp in cps:
                cp.wait()

        return start, finish

    return Ride(list(grads), [SDS((g.shape[0],) + g.shape[2:], g.dtype) for g in grads], {}, [DMA((n,)), DMA((n,))],
                make)


def half_ride(quarters):
    n = len(quarters)

    def make(ins, outs, sem):
        x, y, c = _place()
        sends = [_remote(outs[t].at[c], outs[t].at[c], sem[0].at[t], sem[1].at[t], (x, y, 1 - c)) for t in range(n)]

        def start():
            for cp in sends:
                cp.start()

        def finish():
            for t in range(n):
                theirs = outs[t].at[1 - c]
                _remote(theirs, theirs, sem[0].at[t], sem[1].at[t], (x, y, c)).wait_recv()
            for cp in sends:
                cp.wait_send()

        return start, finish

    return Ride(list(quarters), [SDS(q.shape, q.dtype) for q in quarters], {t: t for t in range(n)},
                [DMA((n,)), DMA((n,))], make)


CAST_STEPS = 4


def cast_quarters(sources, p_arr, *, name, ride=None):
    n = len(sources)
    in_specs, out_specs, out_shape = [], [], []
    for w, layer, kind in sources:
        _, r, ws = w.shape
        tr = r // CAST_STEPS
        assert tr % BF16_ROWS == 0, w.shape
        in_specs.append(pl.BlockSpec((None, tr, ws), lambda i, p_ref, layer=layer: (layer, i, 0)))
        out_specs.append(pl.BlockSpec((tr, ws), (lambda i, p_ref: (p_ref[0] * CAST_STEPS + i, 0)) if kind == "row"
                                      else (lambda i, p_ref: (i, p_ref[0]))))
        out_shape.append(SDS(_full_shape(kind, (r, ws)), BF16))

    def body(p_ref, *refs):
        for w_ref, o_ref in zip(refs[:n], refs[n:]):
            o_ref[...] = w_ref[...].astype(BF16)

    return _call(body, name=name, grid=(CAST_STEPS,), in_specs=in_specs, out_specs=out_specs, out_shape=out_shape,
                 semantics=("parallel",), args=[w for w, _, _ in sources], ride=ride, prefetch=p_arr)


def pair_add(own, got, c_arr, *, name):
    A, _, h, W = own.shape
    th = _row_tile(h, max(BF16_ROWS, (3 << 19) // W), BF16_ROWS)

    def body(c_ref, a_ref, b_ref, o_ref):
        o_ref[...] = (a_ref[...].astype(F32) + b_ref[...].astype(F32)).astype(BF16)

    return pl.pallas_call(
        body, name=name,
        grid_spec=pltpu.PrefetchScalarGridSpec(
            num_scalar_prefetch=1, grid=(A, h // th),
            in_specs=[pl.BlockSpec((None, None, th, W), lambda q, i, c_ref: (q, c_ref[0], i, 0)),
                      pl.BlockSpec((None, th, W), lambda q, i, c_ref: (q, i, 0))],
            out_specs=pl.BlockSpec((None, th, W), lambda q, i, c_ref: (q, i, 0))),
        out_shape=SDS((A, h, W), BF16),
        compiler_params=_params("parallel", "parallel"),
    )(c_arr, own, got)


REDUCE_STEPS = 2


def chip_reduce(sums, got, kinds, pc_arr, *, name):
    n = len(sums)
    mine = {"row": lambda i, pc_ref: (pc_ref[0], i, 0), "col": lambda i, pc_ref: (0, i, pc_ref[0]),
            "split": lambda i, pc_ref: (pc_ref[0] // 2, i, pc_ref[0] % 2)}
    a_specs, b_specs, o_specs, out_shape = [], [], [], []
    for g, kind in zip(got, kinds):
        _, h, ws = g.shape
        th = h // REDUCE_STEPS
        assert th % BF16_ROWS == 0, g.shape
        a_specs.append(pl.BlockSpec((None, th, ws), mine[kind]))
        b_specs.append(pl.BlockSpec((3, th, ws), lambda i, pc_ref: (0, i, 0)))
        o_specs.append(pl.BlockSpec((None, th, ws), lambda i, pc_ref: (pc_ref[1], i, 0)))
        out_shape.append(SDS((2, h, ws), F32))

    def body(pc_ref, *refs):
        for a_ref, b_ref, o_ref in zip(refs[:n], refs[n:2 * n], refs[2 * n:]):
            o_ref[...] = ((a_ref[...].astype(F32) + b_ref[0].astype(F32)) + b_ref[1].astype(F32)) + b_ref[2].astype(F32)

    return _call(body, name=name, grid=(REDUCE_STEPS,), in_specs=a_specs + b_specs, out_specs=o_specs,
                 out_shape=out_shape, semantics=("parallel",), args=list(sums) + list(got), prefetch=pc_arr)


def small_reduce(blocks, me_arr):
    _, rows, D = blocks.shape

    def body(me_ref, b_ref, o_ref):
        me = me_ref[0]
        total = b_ref[me]
        for d in range(1, 8):
            total = total + b_ref[d ^ me]
        o_ref[...] = total

    return pl.pallas_call(
        body, name="small_reduce",
        grid_spec=pltpu.PrefetchScalarGridSpec(
            num_scalar_prefetch=1, grid=(1,),
            in_specs=[pl.BlockSpec((8, rows, D), lambda i, me_ref: (0, 0, 0))],
            out_specs=pl.BlockSpec((rows, D), lambda i, me_ref: (0, 0))),
        out_shape=SDS((rows, D), F32),
        compiler_params=_params("arbitrary"),
    )(me_arr, blocks)


def adamw(w, gs, m, v, *, name):
    L, r, cols = w.shape
    tr = _row_tile(r, 256)
    nt = r // tr

    def body(*refs):
        w_ref, m_ref, v_ref = refs[:3]
        g_refs = refs[3:3 + L]
        g_out, d_out, m_out, v_out = refs[3 + L:]
        layer = pl.program_id(0)
        g = g_refs[0][...]
        for l in range(1, L):
            g = jnp.where(layer == l, g_refs[l][...], g)
        m_new = ADAM_B1 * m_ref[...] + (1.0 - ADAM_B1) * g
        v_new = ADAM_B2 * v_ref[...] + (1.0 - ADAM_B2) * (g * g)
        m_hat = m_new / (1.0 - ADAM_B1 ** ADAM_STEP)
        v_hat = v_new / (1.0 - ADAM_B2 ** ADAM_STEP)
        g_out[...] = g
        m_out[...] = m_new
        v_out[...] = v_new
        d_out[...] = -ADAM_LR * (m_hat / (jnp.sqrt(v_hat) + ADAM_EPS) + ADAM_WD * w_ref[...])

    full = pl.BlockSpec((None, tr, cols), lambda l, i: (l, i, 0))
    g_spec = lambda l0: pl.BlockSpec((tr, cols), lambda l, i: (jnp.where(l == l0, i, jnp.where(l < l0, 0, nt - 1)), 0))
    return pl.pallas_call(
        body, name=name, grid=(L, nt),
        in_specs=[full, full, full] + [g_spec(l0) for l0 in range(L)],
        out_specs=[full] * 4,
        out_shape=[SDS(w.shape, F32)] * 4,
        compiler_params=_params("arbitrary", "arbitrary"),
    )(w, m, v, *gs)


def _rms_r(xf):
    return lax.rsqrt(jnp.mean(xf * xf, axis=-1, keepdims=True) + EPS)


def _rmsnorm_bwd(xf, g, dy):
    r = _rms_r(xf)
    xh = xf * r
    gd = g * dy
    return r * (gd - xh * jnp.mean(xh * gd, axis=-1, keepdims=True)), xh


def _dot(a, b):
    return jnp.dot(a, b, preferred_element_type=F32)


def _dot_nt(a, b):
    return lax.dot_general(a, b, (((1,), (1,)), ((), ())), preferred_element_type=F32)


def _dot_tn(a, b):
    return lax.dot_general(a, b, (((0,), (0,)), ((), ())), preferred_element_type=F32)


def _accumulate(ref, first, value):
    @pl.when(first)
    def _():
        ref[...] = value

    @pl.when(jnp.logical_not(first))
    def _():
        ref[...] += value


def norm_matmul(x, g, w, *, tn, split, name, ride=None, tm=ROW_TILE):
    T, D = x.shape
    N = w.shape[1]
    per = N // split // tn

    def body(x_ref, g_ref, w_ref, o_ref, xn_ref):
        @pl.when(pl.program_id(1) == 0)
        def _():
            xf = x_ref[...].astype(F32)
            xn_ref[...] = (xf * _rms_r(xf) * g_ref[...]).astype(BF16)

        o_ref[...] = _dot(xn_ref[...], w_ref[...]).astype(BF16)

    return _call(
        body, name=name, grid=(T // tm, N // tn),
        in_specs=[pl.BlockSpec((tm, D), lambda i, j: (i, 0)),
                  pl.BlockSpec((1, D), lambda i, j: (0, 0)),
                  pl.BlockSpec((D, tn), lambda i, j: (0, j))],
        out_specs=[pl.BlockSpec((None, tm, tn), lambda i, j: (j // per, i, j % per)),
                   pl.BlockSpec((tm, D), lambda i, j: (i, 0))],
        out_shape=[SDS((split, T, N // split), BF16), SDS((T, D), BF16)],
        semantics=("parallel", "arbitrary"), args=(x, g, w), ride=ride)


BIG_ROW_TILE = 1024


def norm2_matmul(x, gains, weights, *, name, tm=BIG_ROW_TILE):
    T, D = x.shape
    tm = min(tm, T)
    n = len(gains)

    def body(x_ref, *refs):
        xf = x_ref[...].astype(F32)
        xh = xf * _rms_r(xf)
        for g_ref, w_ref, o_ref, xn_ref in zip(refs[:n], refs[n:2 * n], refs[2 * n::2], refs[2 * n + 1::2]):
            xn = (xh * g_ref[...]).astype(BF16)
            xn_ref[...] = xn
            o_ref[...] = _dot(xn, w_ref[...]).astype(BF16)

    row = pl.BlockSpec((tm, D), lambda i: (i, 0))
    vec = pl.BlockSpec((1, D), lambda i: (0, 0))
    out_specs, out_shape = [], []
    for w in weights:
        out_specs += [pl.BlockSpec((tm, w.shape[1]), lambda i: (i, 0)), row]
        out_shape += [SDS((T, w.shape[1]), BF16), SDS((T, D), BF16)]
    return _call(
        body, name=name, grid=(T // tm,),
        in_specs=[row] + [vec] * n + [pl.BlockSpec(w.shape, lambda i: (0, 0)) for w in weights],
        out_specs=out_specs, out_shape=out_shape, semantics=("parallel",), args=[x] + list(gains) + list(weights))


def _shift_down(prev, cur, by):
    big = jnp.concatenate([prev, cur], axis=0)
    return pltpu.roll(big, by, 0)[prev.shape[0]:]


def _shift_up(cur, nxt, by):
    big = jnp.concatenate([cur, nxt], axis=0)
    return pltpu.roll(big, big.shape[0] - by, 0)[:cur.shape[0]]


def conv_mix_out(bcx, conv_w, w_out, g_post, res, *, name, ride=None, tm=ROW_TILE):
    T, D = res.shape
    hb = tm // BF16_ROWS

    def body(b_ref, c_ref, u_ref, cp_ref, up_ref, cw_ref, w_ref, g_ref, r_ref, h_ref, z_ref, y_ref):
        i = pl.program_id(0)
        cu = c_ref[...].astype(F32) * u_ref[...].astype(F32)
        cup = cp_ref[...].astype(F32) * up_ref[...].astype(F32)
        cup = jnp.where(i == 0, 0.0, cup)
        cv = (cw_ref[0:1, :] * _shift_down(cup, cu, 2) + cw_ref[1:2, :] * _shift_down(cup, cu, 1)
              + cw_ref[2:3, :] * cu)
        y = (b_ref[...].astype(F32) * cv).astype(BF16)
        y_ref[...] = y
        z = _dot(y, w_ref[...])
        z_ref[...] = z.astype(BF16)
        h_ref[...] = (r_ref[...] + z * _rms_r(z) * g_ref[...]).astype(STREAM)

    tile = lambda col: pl.BlockSpec((tm, D), lambda i: (i, col))
    halo = lambda col: pl.BlockSpec((BF16_ROWS, D), lambda i: (jnp.maximum(i * hb - 1, 0), col))
    row = pl.BlockSpec((tm, D), lambda i: (i, 0))
    return _call(
        body, name=name, grid=(T // tm,),
        in_specs=[tile(0), tile(1), tile(2), halo(1), halo(2),
                  pl.BlockSpec((3, D), lambda i: (0, 0)),
                  pl.BlockSpec((D, D), lambda i: (0, 0)),
                  pl.BlockSpec((1, D), lambda i: (0, 0)), row],
        out_specs=[row, row, row],
        out_shape=[SDS((T, D), STREAM), SDS((T, D), BF16), SDS((T, D), BF16)],
        semantics=("parallel",), args=(bcx, bcx, bcx, bcx, bcx, conv_w, w_out, g_post, res), ride=ride)


def plain_mix_out(a, w, g_post, res, *, name, target=None, ride=None, tm=ROW_TILE):
    T, D = res.shape
    tm = min(tm, T)
    K = a.shape[1]
    with_loss = target is not None

    def body(a_ref, w_ref, g_ref, r_ref, *rest):
        z = _dot(a_ref[...], w_ref[...])
        h = r_ref[...].astype(F32) + z * _rms_r(z) * g_ref[...]
        if with_loss:
            t_ref, h_ref, z_ref, loss_ref = rest
            diff = h - t_ref[...]
            h_ref[...] = (diff * (1.0 / D)).astype(STREAM)
            part = jnp.full(loss_ref.shape, 0.5 / D, F32) * jnp.sum(diff * diff)
            _accumulate(loss_ref, pl.program_id(0) == 0, part)
        else:
            h_ref, z_ref = rest
            h_ref[...] = h.astype(STREAM)
        z_ref[...] = z.astype(BF16)

    row = pl.BlockSpec((tm, D), lambda i: (i, 0))
    loss_spec, loss_shape = pl.BlockSpec((8, 128), lambda i: (0, 0)), SDS((8, 128), F32)
    return _call(
        body, name=name, grid=(T // tm,),
        in_specs=[pl.BlockSpec((tm, K), lambda i: (i, 0)),
                  pl.BlockSpec((K, D), lambda i: (0, 0)),
                  pl.BlockSpec((1, D), lambda i: (0, 0)), row] + [row] * with_loss,
        out_specs=[row, row] + [loss_spec] * with_loss,
        out_shape=[SDS((T, D), STREAM), SDS((T, D), BF16)] + [loss_shape] * with_loss,
        semantics=("arbitrary",), args=(a, w, g_post, res) + ((target,) if with_loss else ()), ride=ride)


def _silu_grads(d, g, u):
    sg = jax.nn.sigmoid(g)
    return d * u * (sg * (1.0 + g * (1.0 - sg))), d * (g * sg)


def norm_swiglu_in(x, g, w, *, name, ride=None, tm=ROW_TILE // 2):
    T, D = x.shape
    F = w.shape[1] // 2

    def body(x_ref, g_ref, wg_ref, wu_ref, gu_ref, a_ref, xt_ref):
        xf = x_ref[...].astype(F32)
        xn = xf * _rms_r(xf) * g_ref[...]
        xt_ref[...] = xn.T.astype(BF16)
        xb = xn.astype(BF16)
        gate = _dot(xb, wg_ref[...]).astype(BF16)
        up = _dot(xb, wu_ref[...]).astype(BF16)
        gu_ref[0] = gate
        gu_ref[1] = up
        a_ref[...] = gate * jax.nn.sigmoid(gate) * up

    half = lambda s: pl.BlockSpec((D, F), lambda i: (0, s), pipeline_mode=pl.Buffered(1))
    return _call(
        body, name=name, grid=(T // tm,),
        in_specs=[pl.BlockSpec((tm, D), lambda i: (i, 0)), pl.BlockSpec((1, D), lambda i: (0, 0)), half(0), half(1)],
        out_specs=[pl.BlockSpec((2, tm, F), lambda i: (0, i, 0)), pl.BlockSpec((tm, F), lambda i: (i, 0)),
                   pl.BlockSpec((D, tm), lambda i: (0, i))],
        out_shape=[SDS((2, T, F), BF16), SDS((T, F), BF16), SDS((D, T), BF16)],
        semantics=("parallel",), args=(x, g, w, w), ride=ride)


def swiglu_bwd_tn(xt, dact, gu, *, name, ride=None, tb=MXU_WIDTH):
    D, T = xt.shape
    F = dact.shape[1]

    def body(xt_ref, d_ref, g_ref, u_ref, o_ref):
        dg, du = _silu_grads(d_ref[...], g_ref[...], u_ref[...])
        o_ref[0] = _dot(xt_ref[...], dg).astype(BF16)
        o_ref[1] = _dot(xt_ref[...], du).astype(BF16)

    col = lambda s: pl.BlockSpec((None, T, tb), lambda j: (s, 0, j))
    out = _call(
        body, name=name, grid=(F // tb,),
        in_specs=[pl.BlockSpec((D, T), lambda j: (0, 0), pipeline_mode=pl.Buffered(1)),
                  pl.BlockSpec((T, tb), lambda j: (0, j)), col(0), col(1)],
        out_specs=[pl.BlockSpec((2, D, tb), lambda j: (0, 0, j))],
        out_shape=[SDS((2, D, F), BF16)],
        semantics=("parallel",), args=(xt, dact, gu, gu), ride=ride)
    return out[0] if ride is None else (out[0][0], out[1])


def swiglu_bwd_in(dact, gu, w, h_in, g, dh_out, *, name, ride=None, tm=ROW_TILE // 2):
    T, D = h_in.shape
    F = dact.shape[1]

    def body(d_ref, gg_ref, uu_ref, wg_ref, wu_ref, h_ref, g_ref, dh_ref, o_ref, dg_ref):
        dgate, dup = _silu_grads(d_ref[...], gg_ref[...], uu_ref[...])
        dn = _dot_nt(dgate, wg_ref[...]) + _dot_nt(dup, wu_ref[...])
        dx, hh = _rmsnorm_bwd(h_ref[...].astype(F32), g_ref[...], dn)
        o_ref[...] = (dh_ref[...] + dx).astype(STREAM)
        _accumulate(dg_ref, pl.program_id(0) == 0, jnp.sum(dn * hh, axis=0, keepdims=True))

    row = pl.BlockSpec((tm, D), lambda i: (i, 0))
    vec = pl.BlockSpec((1, D), lambda i: (0, 0))
    part = lambda s: pl.BlockSpec((None, tm, F), lambda i: (s, i, 0))
    half = lambda s: pl.BlockSpec((D, F), lambda i: (0, s), pipeline_mode=pl.Buffered(1))
    return _call(
        body, name=name, grid=(T // tm,),
        in_specs=[pl.BlockSpec((tm, F), lambda i: (i, 0)), part(0), part(1), half(0), half(1), row, vec, row],
        out_specs=[row, vec],
        out_shape=[SDS((T, D), STREAM), SDS((1, D), F32)],
        semantics=("arbitrary",), args=(dact, gu, gu, w, w, h_in, g, dh_out), ride=ride)


def rope_tables(T):
    half = ROT_DIM // 2
    inv_freq = ROPE_THETA ** (-jnp.arange(0, ROT_DIM, 2, dtype=F32) / ROT_DIM)
    ang = (jnp.arange(T, dtype=F32)[:, None] * inv_freq[None, :]).T
    cos, sin = jnp.cos(ang), jnp.sin(ang)
    rest = HEAD_DIM - ROT_DIM
    one, zero = jnp.ones((rest, T), F32), jnp.zeros((rest, T), F32)
    zh = jnp.zeros((half, T), F32)
    fac = jnp.concatenate([cos, cos, one], axis=0)
    up = jnp.concatenate([-sin, zh, zero], axis=0)
    down = jnp.concatenate([zh, sin, zero], axis=0)
    return jnp.stack([fac, up, down])


def _rope(t, tab):
    half = ROT_DIM // 2
    return t * tab[0] + pltpu.roll(t, HEAD_DIM - half, 0) * tab[1] + pltpu.roll(t, half, 0) * tab[2]


def _rope_t(d, tab):
    half = ROT_DIM // 2
    return d * tab[0] + pltpu.roll(d * tab[1], half, 0) + pltpu.roll(d * tab[2], HEAD_DIM - half, 0)


def _head(t, h):
    return t[h * HEAD_DIM:(h + 1) * HEAD_DIM]


def _band(n, group):
    kj = lax.broadcasted_iota(jnp.int32, (2 * BLOCK, BLOCK), 0)
    qi = lax.broadcasted_iota(jnp.int32, (2 * BLOCK, BLOCK), 1)
    mask = (kj > qi) & (kj <= qi + BLOCK) & ((n > 0) | (kj >= BLOCK))
    return jnp.tile(mask, (1, group))


def _attn_specs(D, kvd):
    prev = lambda n: jnp.maximum(n - 1, 0)
    return [pl.BlockSpec((BLOCK, D), lambda n: (n, 0)),
            pl.BlockSpec((BLOCK, kvd), lambda n: (prev(n), 0)),
            pl.BlockSpec((BLOCK, kvd), lambda n: (n, 0)),
            pl.BlockSpec((BLOCK, kvd), lambda n: (prev(n), 1)),
            pl.BlockSpec((BLOCK, kvd), lambda n: (n, 1)),
            pl.BlockSpec((3, HEAD_DIM, BLOCK), lambda n: (0, 0, prev(n))),
            pl.BlockSpec((3, HEAD_DIM, BLOCK), lambda n: (0, 0, n)),
            pl.BlockSpec(memory_space=pltpu.SMEM)]


def _attn_operands(q_ref, kp_ref, k_ref, vp_ref, v_ref, tp_ref, t_ref):
    flip = lambda ref: ref[...].astype(F32).T
    tab = t_ref[...]
    kt = jnp.concatenate([flip(kp_ref), flip(k_ref)], axis=1)
    vt = jnp.concatenate([flip(vp_ref), flip(v_ref)], axis=1)
    return flip(q_ref), kt, vt, tab, jnp.concatenate([tp_ref[...], tab], axis=2)


SCORE_SCALE = 1.0 / math.sqrt(HEAD_DIM)
HEADS_TOGETHER = 4


def _group_heads(t, first, count, tab=None):
    heads = [_head(t, first + g) for g in range(count)]
    if tab is not None:
        heads = [_rope(h, tab) * SCORE_SCALE for h in heads]
    return jnp.concatenate(heads, axis=1).astype(BF16)


def _sink_row(s_ref, first, count):
    which = lax.broadcasted_iota(jnp.int32, (1, count * BLOCK), 1) // BLOCK
    row = jnp.zeros((1, count * BLOCK), F32)
    for g in range(count):
        row = jnp.where(which == g, s_ref[0, first + g], row)
    return row


def _softmax(scores, sink, mask):
    s = jnp.where(mask, scores, NEG)
    m = jnp.maximum(jnp.max(s, axis=0, keepdims=True), sink)
    e = jnp.exp(s - m)
    es = jnp.exp(sink - m)
    return e, es, 1.0 / (jnp.sum(e, axis=0, keepdims=True) + es)


def attention_fwd(q, kv, tabs, sinks, *, name, ride=None):
    T, D = q.shape
    kvd = kv.shape[1] // 2
    group = D // HEAD_DIM // N_KV_HEADS

    def body(q_ref, kp_ref, k_ref, vp_ref, v_ref, tp_ref, t_ref, s_ref, o_ref):
        gs = HEADS_TOGETHER
        mask = _band(pl.program_id(0), gs)
        qt, kt, vt, tab, tab2 = _attn_operands(q_ref, kp_ref, k_ref, vp_ref, v_ref, tp_ref, t_ref)
        firsts = [(j, first) for j in range(N_KV_HEADS) for first in range(j * group, (j + 1) * group, gs)]
        ks = [_rope(_head(kt, j), tab2).astype(BF16) for j in range(N_KV_HEADS)]
        scores = [_dot_tn(ks[j], _group_heads(qt, first, gs, tab)) for j, first in firsts]
        soft = [_softmax(s, _sink_row(s_ref, first, gs), mask) for s, (j, first) in zip(scores, firsts)]
        outs = []
        for (e, _, inv), (j, first) in zip(soft, firsts):
            o = _dot(_head(vt, j).astype(BF16), e.astype(BF16)) * inv
            outs += [o[:, g * BLOCK:(g + 1) * BLOCK] for g in range(gs)]
        o_ref[...] = jnp.concatenate(outs, axis=0).T.astype(BF16)

    return _call(
        body, name=name, grid=(T // BLOCK,),
        in_specs=_attn_specs(D, kvd),
        out_specs=[pl.BlockSpec((BLOCK, D), lambda n: (n, 0))],
        out_shape=[SDS((T, D), BF16)],
        semantics=("parallel",), args=(q, kv, kv, kv, kv, tabs, tabs, sinks), ride=ride)


def attention_bwd(q, kv, tabs, sinks, do, *, name, ride=None):
    T, D = q.shape
    kvd = kv.shape[1] // 2
    heads = D // HEAD_DIM
    group = heads // N_KV_HEADS

    def body(q_ref, kp_ref, k_ref, vp_ref, v_ref, tp_ref, t_ref, s_ref, do_ref, dq_ref, dc_ref, dp_ref, ds_ref):
        n = pl.program_id(0)
        gs = HEADS_TOGETHER
        mask = _band(n, gs)
        qt, kt, vt, tab, tab2 = _attn_operands(q_ref, kp_ref, k_ref, vp_ref, v_ref, tp_ref, t_ref)
        dot = do_ref[...].astype(F32).T
        lane = lax.broadcasted_iota(jnp.int32, (8, 128), 1)
        dsink = jnp.zeros((8, 128), F32)
        firsts = [(j, first) for j in range(N_KV_HEADS) for first in range(j * group, (j + 1) * group, gs)]
        ks = [_rope(_head(kt, j), tab2).astype(BF16) for j in range(N_KV_HEADS)]
        vs = [_head(vt, j).astype(BF16) for j in range(N_KV_HEADS)]
        qs = [_group_heads(qt, first, gs, tab) for _, first in firsts]
        dos = [_group_heads(dot, first, gs) for _, first in firsts]
        scores = [_dot_tn(ks[j], q) for q, (j, _) in zip(qs, firsts)]
        dps = [_dot_tn(vs[j], do) for do, (j, _) in zip(dos, firsts)]
        ps, dscs = [], []
        for s, dp, (j, first) in zip(scores, dps, firsts):
            e, e_sink, inv = _softmax(s, _sink_row(s_ref, first, gs), mask)
            p = e * inv
            dl = jnp.sum(p * dp, axis=0, keepdims=True)
            dscs.append((p * (dp - dl)).astype(BF16))
            ps.append(p.astype(BF16))
            weight = e_sink * inv * dl
            for g in range(gs):
                dsink = dsink - jnp.where(lane == first + g, jnp.sum(weight[:, g * BLOCK:(g + 1) * BLOCK]), 0.0)
        dqs = []
        dks = [jnp.zeros((HEAD_DIM, 2 * BLOCK), F32) for _ in range(N_KV_HEADS)]
        dvs = [jnp.zeros((HEAD_DIM, 2 * BLOCK), F32) for _ in range(N_KV_HEADS)]
        for p, dsc, q, do, (j, _) in zip(ps, dscs, qs, dos, firsts):
            dq = _dot(ks[j], dsc) * SCORE_SCALE
            dqs += [_rope_t(dq[:, g * BLOCK:(g + 1) * BLOCK], tab) for g in range(gs)]
            dks[j] = dks[j] + _dot_nt(q, dsc)
            dvs[j] = dvs[j] + _dot_nt(do, p)
        dks = [_rope_t(dk, tab2) for dk in dks]
        dq_ref[...] = jnp.concatenate(dqs, axis=0).T.astype(BF16)
        dkv = jnp.concatenate(dks + dvs, axis=0)
        dp_ref[...] = dkv[:, :BLOCK].T
        dc_ref[...] = dkv[:, BLOCK:].T
        _accumulate(ds_ref, n == 0, dsink)

    blk = lambda w: pl.BlockSpec((BLOCK, w), lambda n: (n, 0))
    return _call(
        body, name=name, grid=(T // BLOCK,),
        in_specs=_attn_specs(D, kvd) + [blk(D)],
        out_specs=[blk(D), blk(2 * kvd), blk(2 * kvd), pl.BlockSpec((8, 128), lambda n: (0, 0))],
        out_shape=[SDS((T, D), BF16), SDS((T, 2 * kvd), F32), SDS((T, 2 * kvd), F32), SDS((8, 128), F32)],
        semantics=("arbitrary",), args=(q, kv, kv, kv, kv, tabs, tabs, sinks, do), ride=ride)


def combine_dkv(d_cur, d_prev, *, name):
    T, W = d_cur.shape
    tm = ROW_TILE
    nt, per, last = T // tm, tm // BLOCK, T // BLOCK - 1

    def body(c_ref, p_ref, pn_ref, o_ref):
        nxt = jnp.where(pl.program_id(0) == nt - 1, 0.0, pn_ref[...])
        o_ref[...] = (c_ref[...] + jnp.concatenate([p_ref[BLOCK:, :], nxt], axis=0)).astype(BF16)

    return _call(
        body, name=name, grid=(nt,),
        in_specs=[pl.BlockSpec((tm, W), lambda i: (i, 0)), pl.BlockSpec((tm, W), lambda i: (i, 0)),
                  pl.BlockSpec((BLOCK, W), lambda i: (jnp.minimum((i + 1) * per, last), 0))],
        out_specs=[pl.BlockSpec((tm, W), lambda i: (i, 0))],
        out_shape=[SDS((T, W), BF16)],
        semantics=("parallel",), args=(d_cur, d_prev, d_prev))[0]


def normbwd_matmul_nt(z, g, dh, w, *, name, ride=None, tm=ROW_TILE):
    T, D = z.shape
    tm = min(tm, T)
    K = w.shape[0]

    def body(z_ref, g_ref, dh_ref, w_ref, dz_ref, dg_ref, o_ref):
        dh_ = dh_ref[...].astype(F32)
        dz, zh = _rmsnorm_bwd(z_ref[...].astype(F32), g_ref[...], dh_)
        dz = dz.astype(BF16)
        dz_ref[...] = dz
        _accumulate(dg_ref, pl.program_id(0) == 0, jnp.sum(dh_ * zh, axis=0, keepdims=True))
        o_ref[...] = _dot_nt(dz, w_ref[...]).astype(BF16)

    row = pl.BlockSpec((tm, D), lambda i: (i, 0))
    vec = pl.BlockSpec((1, D), lambda i: (0, 0))
    return _call(
        body, name=name, grid=(T // tm,),
        in_specs=[row, vec, row, pl.BlockSpec((K, D), lambda i: (0, 0))],
        out_specs=[row, vec, pl.BlockSpec((tm, K), lambda i: (i, 0))],
        out_shape=[SDS((T, D), BF16), SDS((1, D), F32), SDS((T, K), BF16)],
        semantics=("arbitrary",), args=(z, g, dh, w), ride=ride)


def matmul_nt_normbwd(da, w, h_in, g, dh_out, *, name, ride=None, tm=ROW_TILE):
    T, D = h_in.shape
    S, _, K = da.shape

    def body(*refs):
        da_refs, w_refs = refs[:S], refs[S:2 * S]
        h_ref, g_ref, dh_ref, o_ref, dg_ref = refs[2 * S:]
        dn = _dot_nt(da_refs[0][...], w_refs[0][...])
        for s in range(1, S):
            dn = dn + _dot_nt(da_refs[s][...], w_refs[s][...])
        dx, hh = _rmsnorm_bwd(h_ref[...].astype(F32), g_ref[...], dn)
        o_ref[...] = dh_ref[...] + dx
        _accumulate(dg_ref, pl.program_id(0) == 0, jnp.sum(dn * hh, axis=0, keepdims=True))

    row = pl.BlockSpec((tm, D), lambda i: (i, 0))
    vec = pl.BlockSpec((1, D), lambda i: (0, 0))
    part = lambda s: pl.BlockSpec((None, tm, K), lambda i: (s, i, 0))
    cols = lambda s: pl.BlockSpec((D, K), lambda i: (0, s), pipeline_mode=pl.Buffered(1))
    return _call(
        body, name=name, grid=(T // tm,),
        in_specs=[part(s) for s in range(S)] + [cols(s) for s in range(S)] + [row, vec, row],
        out_specs=[row, vec],
        out_shape=[SDS((T, D), F32), SDS((1, D), F32)],
        semantics=("arbitrary",), args=[da] * S + [w] * S + [h_in, g, dh_out], ride=ride)


def matmuls_nt_normbwd(das, ws, h_in, gs, dh_out, *, name, ride=None, tm=BIG_ROW_TILE):
    T, D = h_in.shape
    tm = min(tm, T)
    n = len(das)

    def body(*refs):
        da_refs, w_refs, g_refs = refs[:n], refs[n:2 * n], refs[2 * n:3 * n]
        h_ref, dh_ref, o_ref = refs[3 * n:3 * n + 3]
        hf = h_ref[...].astype(F32)
        r = _rms_r(hf)
        hh = hf * r
        total = dh_ref[...].astype(F32)
        for da_ref, w_ref, g_ref, dg_ref in zip(da_refs, w_refs, g_refs, refs[3 * n + 3:]):
            dn = _dot_nt(da_ref[...], w_ref[...])
            gd = g_ref[...] * dn
            total = total + r * (gd - hh * jnp.mean(hh * gd, axis=-1, keepdims=True))
            _accumulate(dg_ref, pl.program_id(0) == 0, jnp.sum(dn * hh, axis=0, keepdims=True))
        o_ref[...] = total.astype(STREAM)

    row = pl.BlockSpec((tm, D), lambda i: (i, 0))
    vec = pl.BlockSpec((1, D), lambda i: (0, 0))
    return _call(
        body, name=name, grid=(T // tm,),
        in_specs=[pl.BlockSpec((tm, da.shape[1]), lambda i: (i, 0)) for da in das]
        + [pl.BlockSpec(w.shape, lambda i: (0, 0)) for w in ws] + [vec] * n + [row, row],
        out_specs=[row] + [vec] * n,
        out_shape=[SDS((T, D), STREAM)] + [SDS((1, D), F32)] * n,
        semantics=("arbitrary",), args=list(das) + list(ws) + list(gs) + [h_in, dh_out], ride=ride)


def matmul_tn(a, b, *, tb, name, ride=None, ta=MXU_WIDTH):
    T, Ka = a.shape
    S, _, Nb = b.shape
    per = Nb // tb

    def body(a_ref, b_ref, o_ref):
        o_ref[...] = _dot_tn(a_ref[...], b_ref[...]).astype(BF16)

    out = _call(
        body, name=name, grid=(S * per, Ka // ta),
        in_specs=[pl.BlockSpec((T, ta), lambda j, i: (0, i)),
                  pl.BlockSpec((None, T, tb), lambda j, i: (j // per, 0, j % per))],
        out_specs=[pl.BlockSpec((ta, tb), lambda j, i: (i, j))],
        out_shape=[SDS((Ka, S * Nb), BF16)],
        semantics=("parallel", "parallel"), args=(a, b), ride=ride)
    return out[0] if ride is None else (out[0][0], out[1])


def conv_bwd(dy, bcx, conv_w, *, name, ride=None, tm=ROW_TILE):
    T, D = dy.shape
    nt = T // tm
    hb = tm // BF16_ROWS
    last = T // BF16_ROWS - 1

    def body(dy_ref, dyn_ref, b_ref, bn_ref, c_ref, u_ref, cp_ref, up_ref, cw_ref, o_ref, dw_ref):
        i = pl.program_id(0)
        c, u = c_ref[...].astype(F32), u_ref[...].astype(F32)
        cu = c * u
        cup = jnp.where(i == 0, 0.0, cp_ref[...].astype(F32) * up_ref[...].astype(F32))
        cu1, cu2 = _shift_down(cup, cu, 1), _shift_down(cup, cu, 2)
        w0, w1, w2 = cw_ref[0:1, :], cw_ref[1:2, :], cw_ref[2:3, :]
        dyf = dy_ref[...].astype(F32)
        o_ref[:, 0:D] = (dyf * (w0 * cu2 + w1 * cu1 + w2 * cu)).astype(BF16)
        dcv = dyf * b_ref[...].astype(F32)
        dcvn = jnp.where(i == nt - 1, 0.0, dyn_ref[...].astype(F32) * bn_ref[...].astype(F32))
        dcu = w2 * dcv + w1 * _shift_up(dcv, dcvn, 1) + w0 * _shift_up(dcv, dcvn, 2)
        o_ref[:, D:2 * D] = (dcu * u).astype(BF16)
        o_ref[:, 2 * D:3 * D] = (dcu * c).astype(BF16)
        row = lax.broadcasted_iota(jnp.int32, (8, D), 0)
        dw = jnp.zeros((8, D), F32)
        for tap, t in enumerate((cu2, cu1, cu)):
            dw = jnp.where(row == tap, jnp.sum(dcv * t, axis=0, keepdims=True), dw)
        _accumulate(dw_ref, i == 0, dw)

    tile = lambda col: pl.BlockSpec((tm, D), lambda i: (i, col))
    prev = lambda col: pl.BlockSpec((BF16_ROWS, D), lambda i: (jnp.maximum(i * hb - 1, 0), col))
    nxt = lambda col: pl.BlockSpec((BF16_ROWS, D), lambda i: (jnp.minimum((i + 1) * hb, last), col))
    return _call(
        body, name=name, grid=(nt,),
        in_specs=[tile(0), nxt(0), tile(0), nxt(0), tile(1), tile(2), prev(1), prev(2),
                  pl.BlockSpec((3, D), lambda i: (0, 0))],
        out_specs=[pl.BlockSpec((tm, 3 * D), lambda i: (i, 0)), pl.BlockSpec((8, D), lambda i: (0, 0))],
        out_shape=[SDS((T, 3 * D), BF16), SDS((8, D), F32)],
        semantics=("arbitrary",), args=(dy, dy, bcx, bcx, bcx, bcx, bcx, bcx, conv_w), ride=ride)


class NoTraffic:
    def ride(self, kernel_name):
        return None

    def landed(self, kernel_name, results, wts):
        pass

    def grad(self, key, value):
        pass


def local_step(x, target, wts, vec, traffic):
    T, D = x.shape
    tabs = rope_tables(T)
    small = {}

    def run(builder, *args, name, **kw):
        ride = traffic.ride(name)
        if ride is None:
            return builder(*args, name=name, **kw)
        out, extra = builder(*args, name=name, ride=ride, **kw)
        traffic.landed(name, extra, wts)
        return out

    bcx, xn1 = run(norm_matmul, x, vec["a_pre"], wts["w_in"], tn=3 * D, split=1, name="a_in")
    bcx = bcx[0]
    h1, z0, y0 = run(conv_mix_out, bcx, vec["conv_w"], wts["w_out"], vec["a_post"], x, name="a_out")
    gu0, act0, xt2 = run(norm_swiglu_in, h1, vec["ffn_pre0"], wts["gu0"], name="ffn0_in")
    h2, z1 = run(plain_mix_out, act0, wts["wd0"], vec["ffn_post0"], h1, name="ffn0_out")
    kvp, xkv, qp, xq = norm2_matmul(h2, [vec["kv_norm"], vec["b_pre"]], [wts["w_kv"], wts["w_q"]], name="kvq_in")
    (attn,) = run(attention_fwd, qp, kvp, tabs, vec["sinks"], name="attn_fwd")
    h3, z2 = plain_mix_out(attn, wts["w_o"], vec["b_post"], h2, name="attn_out", tm=BIG_ROW_TILE)
    gu1, act1, xt3 = run(norm_swiglu_in, h3, vec["ffn_pre1"], wts["gu1"], name="ffn1_in")
    dy, z3, loss = plain_mix_out(act1, wts["wd1"], vec["ffn_post1"], h3, name="ffn1_out", target=target)

    def ffn_bwd(layer, z, gu, act, xt, h_in, dh, gu_first):
        tag = "ffn%d" % layer
        dz, small["ffn_post%d" % layer], dact = run(
            normbwd_matmul_nt, z, vec["ffn_post%d" % layer], dh, wts["wd%d" % layer], name=tag + "_out_bwd")
        dwd = lambda: traffic.grad("wd%d" % layer, run(matmul_tn, act, dz[None], tb=D, name=tag + "_dwd"))
        dwgu = lambda: traffic.grad("gu%d" % layer, run(swiglu_bwd_tn, xt, dact, gu, name=tag + "_dwgu"))
        for step in ((dwgu, dwd) if gu_first else (dwd, dwgu)):
            step()
        dh_in, small["ffn_pre%d" % layer] = run(
            swiglu_bwd_in, dact, gu, wts["gu%d" % layer], h_in, vec["ffn_pre%d" % layer], dh, name=tag + "_in_bwd")
        return dh_in

    dh3 = ffn_bwd(1, z3, gu1, act1, xt3, h3, dy, gu_first=False)
    dz2, small["b_post"], dattn = normbwd_matmul_nt(z2, vec["b_post"], dh3, wts["w_o"], name="attn_out_bwd",
                                                    tm=BIG_ROW_TILE)
    traffic.grad("w_o", matmul_tn(attn, dz2[None], tb=D, name="attn_dwo"))
    dq, dkv_cur, dkv_prev, small["sinks"] = run(attention_bwd, qp, kvp, tabs, vec["sinks"], dattn, name="attn_bwd")
    dkv = combine_dkv(dkv_cur, dkv_prev, name="attn_dkv")
    traffic.grad("w_q", matmul_tn(xq, dq[None], tb=D, name="attn_dwq"))
    traffic.grad("w_kv", matmul_tn(xkv, dkv[None], tb=dkv.shape[1], name="attn_dwkv"))
    dh2, small["b_pre"], small["kv_norm"] = run(
        matmuls_nt_normbwd, [dq, dkv], [wts["w_q"], wts["w_kv"]], h2, [vec["b_pre"], vec["kv_norm"]], dh3,
        name="qkv_in_bwd")
    dh1 = ffn_bwd(0, z1, gu0, act0, xt2, h1, dh2, gu_first=True)
    dz0, small["a_post"], dyc = run(normbwd_matmul_nt, z0, vec["a_post"], dh1, wts["w_out"], name="a_out_bwd",
                                    tm=BIG_ROW_TILE)
    traffic.grad("w_out", matmul_tn(y0, dz0[None], tb=D, name="a_dwout"))
    dbcx, small["conv_w"] = run(conv_bwd, dyc, bcx, vec["conv_w"], name="a_conv_bwd")
    traffic.grad("w_in", matmul_tn(xn1, dbcx[None], tb=3 * D // 2, name="a_dwin"))
    dx, small["a_pre"] = run(matmul_nt_normbwd, dbcx[None], wts["w_in"], x, vec["a_pre"], dh1, name="a_in_bwd",
                             tm=ROW_TILE // 2)
    return loss, dx, small


SMALL_ROWS = 16
LOSS_ROW = 13

WHOLE = None
GATHER_PLAN = {"cast_rest": [("w_in", WHOLE)],
               "a_in": [("w_out", WHOLE), ("gu0", (0, 18))],
               "a_out": [("gu0", (18, 14))],
               "ffn0_in": [("wd0", WHOLE), ("w_kv", WHOLE), ("w_q", WHOLE), ("w_o", WHOLE)],
               "ffn0_out": [("gu1", (0, 16))],
               "attn_fwd": [("gu1", (16, 16))],
               "ffn1_in": [("wd1", WHOLE)]}
PAIR_PLAN = {"ffn1_dwgu": ["wd1"], "ffn1_in_bwd": ["gu1"], "attn_bwd": ["w_o"], "qkv_in_bwd": ["w_q", "w_kv"],
             "ffn0_dwd": ["gu0"], "ffn0_in_bwd": ["wd0"], "a_conv_bwd": ["w_out"]}
PAIR_ALONE = ["w_in"]
CHIP_PLAN = {"ffn1_in_bwd": [("wd1", WHOLE)], "attn_bwd": [("gu1", WHOLE)],
             "ffn0_out_bwd": [("w_o", WHOLE), ("w_q", WHOLE), ("w_kv", WHOLE)],
             "ffn0_in_bwd": [("gu0", WHOLE)], "a_out_bwd": [("wd0", (0, 14))], "a_conv_bwd": [("wd0", (14, 8))],
             "a_in_bwd": [("w_out", WHOLE), ("w_in", WHOLE)]}
HALF_PLAN = {"a_in_bwd": ["gu0", "gu1", "wd0", "wd1", "w_kv", "w_q", "w_o"]}
GRAD_KIND = dict(KIND, gu0="split", gu1="split")


class Traffic:
    def __init__(self, wholes, quarter, c_arr, pc_arr):
        self.wholes, self.quarter, self.c_arr, self.pc_arr = wholes, quarter, c_arr, pc_arr
        self.views, self.sums, self.got = {}, {}, {}
        self.reduced = {}
        self.stages = {}

    def reduce(self, keys, name):
        return chip_reduce([self.sums[k] for k in keys], [self.got[k] for k in keys], [GRAD_KIND[k] for k in keys],
                           self.pc_arr, name=name)

    def ride(self, name, small=None):
        rides, stages = [], []
        if name in GATHER_PLAN:
            plan = GATHER_PLAN[name]
            rides.append(gather_ride([self.wholes[k] for k, _ in plan],
                                     [(KIND[k], self.quarter[k], part) for k, part in plan], small))
            stages.append(("gather", [k for k, _ in plan]))
        if name in CHIP_PLAN:
            plan = CHIP_PLAN[name]
            rides.append(chip_ride([self.sums[k] for k, _ in plan],
                                   [(GRAD_KIND[k], self.quarter[k], part) for k, part in plan],
                                   earlier=[self.got.get(k) for k, _ in plan]))
            stages.append(("chip", [k for k, _ in plan]))
        if name in PAIR_PLAN:
            keys = PAIR_PLAN[name]
            rides.append(pair_ride([self.views[k] for k in keys]))
            stages.append(("pair", keys))
        if name in HALF_PLAN:
            keys = HALF_PLAN[name]
            rides.append(half_ride(self.reduce(keys, "chip_reduce_early")))
            stages.append(("half", keys))
        self.stages[name] = stages
        return join(rides)

    def landed(self, name, results, wts):
        results = list(results)
        for stage, keys in self.stages[name]:
            mine, results = results[:len(keys)], results[len(keys):]
            if stage == "gather":
                for k, whole in zip(keys, mine):
                    self.wholes[k] = wts[k] = whole
            elif stage == "chip":
                self.got.update(zip(keys, mine))
            elif stage == "half":
                self.reduced.update(zip(keys, mine))
            else:
                for k, got in zip(keys, mine):
                    self.sums[k] = pair_add(self.views[k], got, self.c_arr, name="pair_add_" + k)

    def grad(self, key, value):
        r, ws = self.quarter[key]
        view = {"row": (N_CHIPS, 2, r // 2, ws), "col": (1, 2, r // 2, N_CHIPS * ws), "split": (2, 2, r // 2, 2 * ws)}
        self.views[key] = value.reshape(view[GRAD_KIND[key]])
        if key in PAIR_ALONE:
            (got,) = alone(pair_ride([self.views[key]]), name="pair_exchange_" + key)
            self.sums[key] = pair_add(self.views[key], got, self.c_arr, name="pair_add_" + key)


def kernel(x, a_pre_norm, a_w_in, a_conv_w, a_w_out, a_post_norm, ffn_pre_norm, ffn_w_gate_up, ffn_w_down, ffn_post_norm, kv_norm, w_kv, b_pre_norm, b_w_q, b_sinks, b_w_o, b_post_norm, loss_target, m_a_pre_norm, m_a_w_in, m_a_conv_w, m_a_w_out, m_a_post_norm, m_ffn_pre_norm, m_ffn_w_gate_up, m_ffn_w_down, m_ffn_post_norm, m_kv_norm, m_w_kv, m_b_pre_norm, m_b_w_q, m_b_sinks, m_b_w_o, m_b_post_norm, v_a_pre_norm, v_a_w_in, v_a_conv_w, v_a_w_out, v_a_post_norm, v_ffn_pre_norm, v_ffn_w_gate_up, v_ffn_w_down, v_ffn_post_norm, v_kv_norm, v_w_kv, v_b_pre_norm, v_b_w_q, v_b_sinks, v_b_w_o, v_b_post_norm):
    T, D = x.shape[1], x.shape[2]
    xi, yi, ci = _place()
    p = 2 * xi + yi
    p_arr = jnp.reshape(p, (1,)).astype(jnp.int32)
    c_arr = jnp.reshape(ci, (1,)).astype(jnp.int32)
    pc_arr = jnp.stack([p, ci]).astype(jnp.int32)
    me_arr = jnp.reshape(4 * xi + 2 * yi + ci, (1,)).astype(jnp.int32)
    qd = D // N_CHIPS

    big = {"w_in": (a_w_in, 0), "w_out": (a_w_out, 0), "gu0": (ffn_w_gate_up, 0), "gu1": (ffn_w_gate_up, 1),
           "wd0": (ffn_w_down, 0), "wd1": (ffn_w_down, 1), "w_kv": (w_kv[None], 0), "w_q": (b_w_q, 0),
           "w_o": (b_w_o, 0)}
    names = list(big)
    quarter = {k: w.shape[1:] for k, (w, _) in big.items()}
    source = lambda k: big[k] + (KIND[k],)
    traffic = Traffic(dict(zip(names[:1], cast_quarters([source(names[0])], p_arr, name="cast_first"))), quarter,
                      c_arr, pc_arr)
    small_shard = jnp.concatenate([a_pre_norm, a_post_norm, a_conv_w[0], jnp.zeros((3, qd), F32)], axis=0)
    wts = {}
    rest, (*landed, small_full) = cast_quarters([source(k) for k in names[1:]], p_arr, name="cast_rest",
                                                ride=traffic.ride("cast_rest", small_shard))
    traffic.wholes.update(zip(names[1:], rest))
    traffic.landed("cast_rest", landed, wts)
    rows = lambda k: jnp.transpose(small_full[:, k], (1, 0, 2)).reshape(-1, D)
    vec = {"a_pre": rows(slice(0, 1)), "a_post": rows(slice(1, 2)), "conv_w": rows(slice(2, 5)),
           "ffn_pre0": ffn_pre_norm[0:1], "ffn_pre1": ffn_pre_norm[1:2],
           "ffn_post0": ffn_post_norm[0:1], "ffn_post1": ffn_post_norm[1:2],
           "kv_norm": kv_norm[None], "b_pre": b_pre_norm, "b_post": b_post_norm, "sinks": b_sinks}

    loss, dx, small = local_step(x[0], loss_target[0], wts, vec, traffic)

    pad = lambda a: jnp.pad(a, ((0, 0), (0, D - a.shape[1])))
    small_block = jnp.concatenate(
        [small["a_pre"], small["a_post"], small["conv_w"][0:3], small["ffn_pre0"], small["ffn_pre1"],
         small["ffn_post0"], small["ffn_post1"], small["kv_norm"], small["b_pre"], small["b_post"],
         pad(small["sinks"][0:1]), pad(loss[0:1]), jnp.zeros((SMALL_ROWS - LOSS_ROW - 1, D), F32)], axis=0)
    late = [k for k in names if k not in traffic.reduced]
    *swapped, small_blocks = alone(join([half_ride(traffic.reduce(late, "chip_reduce_late")),
                                         chip_ride([], [], small_block)]), name="last_exchange")
    traffic.reduced.update(zip(late, swapped))
    grad = {k: traffic.reduced[k].reshape(quarter[k]) for k in names}
    small_sum = small_reduce(small_blocks, me_arr)

    out = {}
    out["a_w_in"] = adamw(a_w_in, [grad["w_in"]], m_a_w_in, v_a_w_in, name="adamw_a_w_in")
    out["a_w_out"] = adamw(a_w_out, [grad["w_out"]], m_a_w_out, v_a_w_out, name="adamw_a_w_out")
    out["ffn_w_gate_up"] = adamw(ffn_w_gate_up, [grad["gu0"], grad["gu1"]], m_ffn_w_gate_up, v_ffn_w_gate_up,
                                 name="adamw_ffn_w_gate_up")
    out["ffn_w_down"] = adamw(ffn_w_down, [grad["wd0"], grad["wd1"]], m_ffn_w_down, v_ffn_w_down,
                              name="adamw_ffn_w_down")
    out["w_kv"] = [o[0] for o in adamw(w_kv[None], [grad["w_kv"]], m_w_kv[None], v_w_kv[None], name="adamw_w_kv")]
    out["b_w_q"] = adamw(b_w_q, [grad["w_q"]], m_b_w_q, v_b_w_q, name="adamw_b_w_q")
    out["b_w_o"] = adamw(b_w_o, [grad["w_o"]], m_b_w_o, v_b_w_o, name="adamw_b_w_o")

    def pack(a_pre, a_post, conv, ffn_pre, ffn_post, kvn, b_pre, b_post, sinks):
        return jnp.concatenate([pad(a_pre), pad(a_post), pad(conv[0]), ffn_pre, ffn_post, kvn[None], b_pre, b_post,
                                pad(sinks), jnp.zeros((SMALL_ROWS - 13, D), F32)], axis=0)

    g_small = jnp.concatenate([pad(lax.dynamic_slice(small_sum, (0, p * qd), (5, qd))), small_sum[5:]], axis=0)
    w_small = pack(a_pre_norm, a_post_norm, a_conv_w, ffn_pre_norm, ffn_post_norm, kv_norm, b_pre_norm, b_post_norm,
                   b_sinks)
    m_small = pack(m_a_pre_norm, m_a_post_norm, m_a_conv_w, m_ffn_pre_norm, m_ffn_post_norm, m_kv_norm,
                   m_b_pre_norm, m_b_post_norm, m_b_sinks)
    v_small = pack(v_a_pre_norm, v_a_post_norm, v_a_conv_w, v_ffn_pre_norm, v_ffn_post_norm, v_kv_norm,
                   v_b_pre_norm, v_b_post_norm, v_b_sinks)
    packed = adamw(w_small[None], [g_small], m_small[None], v_small[None], name="adamw_small")
    ns = b_sinks.shape[1]
    unpack = lambda a: {"a_pre_norm": a[0:1, :qd], "a_post_norm": a[1:2, :qd], "a_conv_w": a[None, 2:5, :qd],
                        "ffn_pre_norm": a[5:7], "ffn_post_norm": a[7:9], "kv_norm": a[9], "b_pre_norm": a[10:11],
                        "b_post_norm": a[11:12], "b_sinks": a[12:13, :ns]}
    unpacked = [unpack(a[0]) for a in packed]
    for k in unpacked[0]:
        out[k] = [u[k] for u in unpacked]

    order = ["a_pre_norm", "a_w_in", "a_conv_w", "a_w_out", "a_post_norm", "ffn_pre_norm", "ffn_w_gate_up",
             "ffn_w_down", "ffn_post_norm", "kv_norm", "w_kv", "b_pre_norm", "b_w_q", "b_sinks", "b_w_o",
             "b_post_norm"]
    return (small_sum[LOSS_ROW, 0], dx[None], *[out[k][0] for k in order], *[out[k][1] for k in order],
            *[out[k][2] for k in order], *[out[k][3] for k in order])
```

```python
import math

import jax
import jax.numpy as jnp
from jax import lax
from jax.experimental import pallas as pl
from jax.experimental.pallas import tpu as pltpu
from jax.experimental.pallas import tpu_sc as plsc

F32 = jnp.float32
BF16 = jnp.bfloat16
SDS = jax.ShapeDtypeStruct
MESH = pl.DeviceIdType.MESH
DMA = pltpu.SemaphoreType.DMA
HBM_SPEC = pl.BlockSpec(memory_space=pltpu.HBM)

EPS = 1e-6
NEG = -1e30
HEAD_DIM = 64
N_KV_HEADS = 4
BLOCK = 128
ROT_DIM = HEAD_DIM // 4
ROPE_THETA = 500000.0
N_CHIPS = 4

ADAM_LR = 0.001
ADAM_B1 = 0.9
ADAM_B2 = 0.999
ADAM_EPS = 1e-08
ADAM_WD = 0.01
ADAM_STEP = 10

VMEM_LIMIT_BYTES = 52 * 1024 * 1024
ROW_TILE = 512
BF16_ROWS = 16
STREAM = BF16
MXU_WIDTH = 256

KIND = {"w_in": "col", "gu0": "col", "gu1": "col", "w_out": "row", "wd0": "row", "wd1": "row", "w_kv": "row",
        "w_q": "row", "w_o": "row"}


def _params(*semantics):
    return pltpu.CompilerParams(dimension_semantics=semantics, vmem_limit_bytes=VMEM_LIMIT_BYTES)


def _row_tile(rows, limit, step=8):
    return max(t for t in range(step, limit + 1, step) if rows % t == 0)


def _place():
    return lax.axis_index("x"), lax.axis_index("y"), lax.axis_index("c")


def _other_chips(x, y):
    return [(1 - x, y), (x, 1 - y), (1 - x, 1 - y)]


def _remote(src, dst, send_sem, recv_sem, to):
    return pltpu.make_async_remote_copy(src_ref=src, dst_ref=dst, send_sem=send_sem, recv_sem=recv_sem,
                                        device_id=to, device_id_type=MESH)


def _full_shape(kind, quarter):
    r, ws = quarter
    return (N_CHIPS * r, ws) if kind == "row" else (r, N_CHIPS * ws)


def _rows_of(h, part):
    lo, n = (0, h) if part is None else (part[0] * BF16_ROWS, part[1] * BF16_ROWS)
    assert lo + n <= h, (h, part)
    return lo, n


def _half_of_quarter(ref, kind, quarter, part, q, half):
    r, ws = quarter
    h = r // 2
    lo, n = _rows_of(h, part)
    if kind == "row":
        return ref.at[pl.ds(pl.multiple_of(q * r + half * h + lo, BF16_ROWS), n)]
    return ref.at[pl.ds(pl.multiple_of(half * h + lo, BF16_ROWS), n), pl.ds(pl.multiple_of(q * ws, 128), ws)]


class Ride:
    def __init__(self, operands, out_shape, aliases, sems, make):
        self.operands, self.out_shape, self.aliases, self.sems, self.make = operands, out_shape, aliases, sems, make


def join(rides):
    rides = [r for r in rides if r is not None]
    if len(rides) < 2:
        return rides[0] if rides else None
    aliases, at = {}, [0, 0, 0]
    cuts = []
    for r in rides:
        aliases.update({at[0] + i: at[1] + o for i, o in r.aliases.items()})
        cuts.append(tuple(at))
        at = [at[0] + len(r.operands), at[1] + len(r.out_shape), at[2] + len(r.sems)]
    cuts.append(tuple(at))

    def make(ins, outs, sem):
        made = [r.make(ins[lo[0]:hi[0]], outs[lo[1]:hi[1]], sem[lo[2]:hi[2]]) for r, lo, hi in zip(rides, cuts, cuts[1:])]

        def start():
            for s, _ in made:
                s()

        def finish():
            for _, f in made:
                f()

        return start, finish

    return Ride(sum((list(r.operands) for r in rides), []), sum((list(r.out_shape) for r in rides), []), aliases,
                sum((list(r.sems) for r in rides), []), make)


def _call(body, *, name, grid, in_specs, out_specs, out_shape, args, scratch_shapes=(), semantics=None, ride=None,
          prefetch=None):
    pre = 0 if prefetch is None else 1
    n_in, n_out, n_scr = len(in_specs), len(out_specs), len(scratch_shapes)
    r_in, r_out = (len(ride.operands), len(ride.out_shape)) if ride is not None else (0, 0)
    a, b = pre + n_in, pre + n_in + r_in
    c, d = b + n_out, b + n_out + r_out
    e = d + n_scr

    def riding(*refs):
        start, finish = ride.make(refs[a:b], refs[c:d], refs[e:])
        ids = [pl.program_id(k) for k in range(len(grid))]
        first, last = ids[0] == 0, ids[0] == grid[0] - 1
        for k in range(1, len(grid)):
            first, last = first & (ids[k] == 0), last & (ids[k] == grid[k] - 1)
        pl.when(first)(start)
        body(*refs[:a], *refs[b:c], *refs[d:e])
        pl.when(last)(finish)

    if ride is None:
        kernel_body, extra_in, extra_out, extra_shape, extra_scr, aliases = body, [], [], [], [], {}
        params = _params(*semantics)
    else:
        kernel_body, extra_in, extra_out = riding, [HBM_SPEC] * r_in, [HBM_SPEC] * r_out
        extra_shape, extra_scr = list(ride.out_shape), list(ride.sems)
        aliases = {pre + n_in + i: n_out + o for i, o in ride.aliases.items()}
        params = _params(*(("arbitrary",) * len(grid)))
    specs = dict(grid=grid, in_specs=list(in_specs) + extra_in, out_specs=list(out_specs) + extra_out,
                 scratch_shapes=list(scratch_shapes) + extra_scr)
    if prefetch is not None:
        specs = dict(grid_spec=pltpu.PrefetchScalarGridSpec(num_scalar_prefetch=1, **specs))
        args = (prefetch,) + tuple(args)
    outs = pl.pallas_call(kernel_body, name=name, out_shape=list(out_shape) + extra_shape,
                          input_output_aliases=aliases, compiler_params=params, **specs,
                          )(*args, *(ride.operands if ride is not None else ()))
    return outs if ride is None else (outs[:n_out], outs[n_out:])


def alone(ride, *, name):
    def body(*refs):
        n = len(ride.operands)
        start, finish = ride.make(refs[:n], refs[n:n + len(ride.out_shape)], refs[n + len(ride.out_shape):])
        start()
        finish()

    return pl.pallas_call(
        body, name=name, in_specs=[HBM_SPEC] * len(ride.operands), out_specs=[HBM_SPEC] * len(ride.out_shape),
        out_shape=list(ride.out_shape), input_output_aliases=dict(ride.aliases), scratch_shapes=list(ride.sems),
    )(*ride.operands)


def gather_ride(wholes, metas, small=None):
    n = len(wholes)
    operands, out_shape = list(wholes), [SDS(s.shape, s.dtype) for s in wholes]
    sems = [DMA((n, 3)), DMA((n, 3)), DMA((n, 3)), DMA((n, 3))]
    if small is not None:
        operands.append(small)
        out_shape.append(SDS((N_CHIPS,) + small.shape, small.dtype))
        sems += [DMA((3,)), DMA((3,)), DMA(())]

    def make(ins, outs, sem):
        send1, recv1, send2, recv2 = sem[:4]
        x, y, c = _place()
        p = 2 * x + y
        chips = _other_chips(x, y)
        me, sibling = (x, y, c), (x, y, 1 - c)
        part = lambda t, q, half: _half_of_quarter(outs[t], *metas[t], q, half)
        first = []
        for j, (qx, qy) in enumerate(chips):
            if small is not None:
                first.append(_remote(ins[n], outs[n].at[p], sem[4].at[j], sem[5].at[j], (qx, qy, c)))
            for t in range(n):
                first.append(_remote(part(t, p, c), part(t, p, c), send1.at[t, j], recv1.at[t, j], (qx, qy, c)))
        local = [] if small is None else [pltpu.make_async_copy(ins[n], outs[n].at[p], sem[6])]

        def start():
            for cp in local + first:
                cp.start()

        def finish():
            passed = []
            for j, (qx, qy) in enumerate(chips):
                q = 2 * qx + qy
                for t in range(n):
                    landed = part(t, q, c)
                    _remote(landed, landed, send1.at[t, j], recv1.at[t, j], me).wait_recv()
                    cp = _remote(landed, landed, send2.at[t, j], recv2.at[t, j], sibling)
                    cp.start()
                    passed.append(cp)
            for j, (qx, qy) in enumerate(chips):
                q = 2 * qx + qy
                if small is not None:
                    _remote(outs[n].at[q], outs[n].at[q], sem[4].at[j], sem[5].at[j], me).wait_recv()
                for t in range(n):
                    theirs = part(t, q, 1 - c)
                    _remote(theirs, theirs, send2.at[t, j], recv2.at[t, j], me).wait_recv()
            for cp in first + passed:
                cp.wait_send()
            for cp in local:
                cp.wait()

        return start, finish

    return Ride(operands, out_shape, {t: t for t in range(n)}, sems, make)


def chip_ride(sums, metas, small=None, earlier=None):
    n = len(sums)
    operands = list(sums)
    out_shape = [SDS((3, s.shape[1], quarter[1]), s.dtype) for s, (_, quarter, _) in zip(sums, metas)]
    sems = [DMA((n, 3)), DMA((n, 3))] if n else []
    if small is not None:
        operands.append(small)
        out_shape.append(SDS((8,) + small.shape, small.dtype))
        sems += [DMA((7,)), DMA((7,)), DMA(())]
    aliases = {}
    for t, buffer in enumerate(earlier or [None] * n):
        if buffer is not None:
            aliases[len(operands)] = t
            operands.append(buffer)

    def make(ins, outs, sem):
        x, y, c = _place()
        cps = []
        for j, (qx, qy) in enumerate(_other_chips(x, y)):
            q = 2 * qx + qy
            for t in range(n):
                kind, (_, ws), part = metas[t]
                rows = pl.ds(*_rows_of(ins[t].shape[1], part))
                if kind == "row":
                    src = ins[t].at[q, rows]
                elif kind == "col":
                    src = ins[t].at[0, rows, pl.ds(pl.multiple_of(q * ws, 128), ws)]
                else:
                    src = ins[t].at[q // 2, rows, pl.ds(pl.multiple_of((q % 2) * ws, 128), ws)]
                cps.append(_remote(src, outs[t].at[j, rows], sem[0].at[t, j], sem[1].at[t, j], (qx, qy, c)))
        local = []
        if small is not None:
            ssend, srecv, lsem = sem[2 * bool(n):2 * bool(n) + 3]
            local.append(pltpu.make_async_copy(ins[n], outs[n].at[0], lsem))
            for k in range(1, 8):
                peer = (x ^ (k >> 2 & 1), y ^ (k >> 1 & 1), c ^ (k & 1))
                cps.append(_remote(ins[n], outs[n].at[k], ssend.at[k - 1], srecv.at[k - 1], peer))

        def start():
            for cp in local + cps:
                cp.start()

        def finish():
            for cp in cps + local:
                cp.wait()

        return start, finish

    return Ride(operands, out_shape, aliases, sems, make)


def pair_ride(grads):
    n = len(grads)

    def make(ins, outs, sem):
        x, y, c = _place()
        cps = [_remote(ins[t].at[:, 1 - c], outs[t], sem[0].at[t], sem[1].at[t], (x, y, 1 - c)) for t in range(n)]

        def start():
            for cp in cps:
                cp.start()

        def finish():
            for cp in cps:
                cp.wait()

        return start, finish

    return Ride(list(grads), [SDS((g.shape[0],) + g.shape[2:], g.dtype) for g in grads], {}, [DMA((n,)), DMA((n,))],
                make)


def half_ride(quarters):
    n = len(quarters)

    def make(ins, outs, sem):
        x, y, c = _place()
        sends = [_remote(outs[t].at[c], outs[t].at[c], sem[0].at[t], sem[1].at[t], (x, y, 1 - c)) for t in range(n)]

        def start():
            for cp in sends:
                cp.start()

        def finish():
            for t in range(n):
                theirs = outs[t].at[1 - c]
                _remote(theirs, theirs, sem[0].at[t], sem[1].at[t], (x, y, c)).wait_recv()
            for cp in sends:
                cp.wait_send()

        return start, finish

    return Ride(list(quarters), [SDS(q.shape, q.dtype) for q in quarters], {t: t for t in range(n)},
                [DMA((n,)), DMA((n,))], make)


CAST_STEPS = 4


def cast_quarters(sources, p_arr, *, name, ride=None):
    n = len(sources)
    in_specs, out_specs, out_shape = [], [], []
    for w, layer, kind in sources:
        _, r, ws = w.shape
        tr = r // CAST_STEPS
        assert tr % BF16_ROWS == 0, w.shape
        in_specs.append(pl.BlockSpec((None, tr, ws), lambda i, p_ref, layer=layer: (layer, i, 0)))
        out_specs.append(pl.BlockSpec((tr, ws), (lambda i, p_ref: (p_ref[0] * CAST_STEPS + i, 0)) if kind == "row"
                                      else (lambda i, p_ref: (i, p_ref[0]))))
        out_shape.append(SDS(_full_shape(kind, (r, ws)), BF16))

    def body(p_ref, *refs):
        for w_ref, o_ref in zip(refs[:n], refs[n:]):
            o_ref[...] = w_ref[...].astype(BF16)

    return _call(body, name=name, grid=(CAST_STEPS,), in_specs=in_specs, out_specs=out_specs, out_shape=out_shape,
                 semantics=("parallel",), args=[w for w, _, _ in sources], ride=ride, prefetch=p_arr)


def pair_add(own, got, c_arr, *, name):
    A, _, h, W = own.shape
    th = _row_tile(h, max(BF16_ROWS, (3 << 19) // W), BF16_ROWS)

    def body(c_ref, a_ref, b_ref, o_ref):
        o_ref[...] = (a_ref[...].astype(F32) + b_ref[...].astype(F32)).astype(BF16)

    return pl.pallas_call(
        body, name=name,
        grid_spec=pltpu.PrefetchScalarGridSpec(
            num_scalar_prefetch=1, grid=(A, h // th),
            in_specs=[pl.BlockSpec((None, None, th, W), lambda q, i, c_ref: (q, c_ref[0], i, 0)),
                      pl.BlockSpec((None, th, W), lambda q, i, c_ref: (q, i, 0))],
            out_specs=pl.BlockSpec((None, th, W), lambda q, i, c_ref: (q, i, 0))),
        out_shape=SDS((A, h, W), BF16),
        compiler_params=_params("parallel", "parallel"),
    )(c_arr, own, got)


REDUCE_STEPS = 2


def chip_reduce(sums, got, kinds, pc_arr, *, name):
    n = len(sums)
    mine = {"row": lambda i, pc_ref: (pc_ref[0], i, 0), "col": lambda i, pc_ref: (0, i, pc_ref[0]),
            "split": lambda i, pc_ref: (pc_ref[0] // 2, i, pc_ref[0] % 2)}
    a_specs, b_specs, o_specs, out_shape = [], [], [], []
    for g, kind in zip(got, kinds):
        _, h, ws = g.shape
        th = h // REDUCE_STEPS
        assert th % BF16_ROWS == 0, g.shape
        a_specs.append(pl.BlockSpec((None, th, ws), mine[kind]))
        b_specs.append(pl.BlockSpec((3, th, ws), lambda i, pc_ref: (0, i, 0)))
        o_specs.append(pl.BlockSpec((None, th, ws), lambda i, pc_ref: (pc_ref[1], i, 0)))
        out_shape.append(SDS((2, h, ws), F32))

    def body(pc_ref, *refs):
        for a_ref, b_ref, o_ref in zip(refs[:n], refs[n:2 * n], refs[2 * n:]):
            o_ref[...] = ((a_ref[...].astype(F32) + b_ref[0].astype(F32)) + b_ref[1].astype(F32)) + b_ref[2].astype(F32)

    return _call(body, name=name, grid=(REDUCE_STEPS,), in_specs=a_specs + b_specs, out_specs=o_specs,
                 out_shape=out_shape, semantics=("parallel",), args=list(sums) + list(got), prefetch=pc_arr)


def small_reduce(blocks, me_arr):
    _, rows, D = blocks.shape

    def body(me_ref, b_ref, o_ref):
        me = me_ref[0]
        total = b_ref[me]
        for d in range(1, 8):
            total = total + b_ref[d ^ me]
        o_ref[...] = total

    return pl.pallas_call(
        body, name="small_reduce",
        grid_spec=pltpu.PrefetchScalarGridSpec(
            num_scalar_prefetch=1, grid=(1,),
            in_specs=[pl.BlockSpec((8, rows, D), lambda i, me_ref: (0, 0, 0))],
            out_specs=pl.BlockSpec((rows, D), lambda i, me_ref: (0, 0))),
        out_shape=SDS((rows, D), F32),
        compiler_params=_params("arbitrary"),
    )(me_arr, blocks)


def adamw(w, gs, m, v, *, name):
    L, r, cols = w.shape
    tr = _row_tile(r, 256)
    nt = r // tr

    def body(*refs):
        w_ref, m_ref, v_ref = refs[:3]
        g_refs = refs[3:3 + L]
        g_out, d_out, m_out, v_out = refs[3 + L:]
        layer = pl.program_id(0)
        g = g_refs[0][...]
        for l in range(1, L):
            g = jnp.where(layer == l, g_refs[l][...], g)
        m_new = ADAM_B1 * m_ref[...] + (1.0 - ADAM_B1) * g
        v_new = ADAM_B2 * v_ref[...] + (1.0 - ADAM_B2) * (g * g)
        m_hat = m_new / (1.0 - ADAM_B1 ** ADAM_STEP)
        v_hat = v_new / (1.0 - ADAM_B2 ** ADAM_STEP)
        g_out[...] = g
        m_out[...] = m_new
        v_out[...] = v_new
        d_out[...] = -ADAM_LR * (m_hat / (jnp.sqrt(v_hat) + ADAM_EPS) + ADAM_WD * w_ref[...])

    full = pl.BlockSpec((None, tr, cols), lambda l, i: (l, i, 0))
    g_spec = lambda l0: pl.BlockSpec((tr, cols), lambda l, i: (jnp.where(l == l0, i, jnp.where(l < l0, 0, nt - 1)), 0))
    return pl.pallas_call(
        body, name=name, grid=(L, nt),
        in_specs=[full, full, full] + [g_spec(l0) for l0 in range(L)],
        out_specs=[full] * 4,
        out_shape=[SDS(w.shape, F32)] * 4,
        compiler_params=_params("arbitrary", "arbitrary"),
    )(w, m, v, *gs)


SC_TILES = 32
SC_CHUNKS = 11


def sc_adamw(w, g, m, v, outs, layer, *, name):
    n = g.shape[0]
    per = n // SC_TILES
    chunk = per // SC_CHUNKS
    assert per * SC_TILES == n and chunk * SC_CHUNKS == per and chunk % 16 == 0, n
    g_out, d_out, m_out, v_out = outs

    def body(w_hbm, g_hbm, m_hbm, v_hbm, wb, gb, mb, vb, db):
        tile = lax.axis_index("subcore") * 2 + lax.axis_index("core")

        @pl.loop(0, SC_CHUNKS)
        def _(k):
            at = tile * per + k * chunk
            here = pl.ds(layer * n + at, chunk)
            pltpu.sync_copy(w_hbm.at[here], wb)
            pltpu.sync_copy(g_hbm.at[pl.ds(at, chunk)], gb)
            pltpu.sync_copy(m_hbm.at[here], mb)
            pltpu.sync_copy(v_hbm.at[here], vb)

            @pl.loop(0, chunk, step=16)
            def _(i):
                lanes = pl.ds(i, 16)
                g_ = gb[lanes]
                m_new = ADAM_B1 * mb[lanes] + (1.0 - ADAM_B1) * g_
                v_new = ADAM_B2 * vb[lanes] + (1.0 - ADAM_B2) * (g_ * g_)
                m_hat = m_new / (1.0 - ADAM_B1 ** ADAM_STEP)
                v_hat = v_new / (1.0 - ADAM_B2 ** ADAM_STEP)
                mb[lanes] = m_new
                vb[lanes] = v_new
                db[lanes] = -ADAM_LR * (m_hat / (jnp.sqrt(v_hat) + ADAM_EPS) + ADAM_WD * wb[lanes])

            pltpu.sync_copy(gb, g_out.at[here])
            pltpu.sync_copy(db, d_out.at[here])
            pltpu.sync_copy(mb, m_out.at[here])
            pltpu.sync_copy(vb, v_out.at[here])

    pl.kernel(body, out_type=(), name=name,
              mesh=plsc.VectorSubcoreMesh(core_axis_name="core", subcore_axis_name="subcore"),
              scratch_types=[pltpu.VMEM((chunk,), F32)] * 5)(w, g, m, v)


def _rms_r(xf):
    return lax.rsqrt(jnp.mean(xf * xf, axis=-1, keepdims=True) + EPS)


def _rmsnorm_bwd(xf, g, dy):
    r = _rms_r(xf)
    xh = xf * r
    gd = g * dy
    return r * (gd - xh * jnp.mean(xh * gd, axis=-1, keepdims=True)), xh


def _dot(a, b):
    return jnp.dot(a, b, preferred_element_type=F32)


def _dot_nt(a, b):
    return lax.dot_general(a, b, (((1,), (1,)), ((), ())), preferred_element_type=F32)


def _dot_tn(a, b):
    return lax.dot_general(a, b, (((0,), (0,)), ((), ())), preferred_element_type=F32)


def _accumulate(ref, first, value):
    @pl.when(first)
    def _():
        ref[...] = value

    @pl.when(jnp.logical_not(first))
    def _():
        ref[...] += value


def norm_matmul(x, g, w, *, tn, split, name, ride=None, tm=ROW_TILE):
    T, D = x.shape
    N = w.shape[1]
    per = N // split // tn

    def body(x_ref, g_ref, w_ref, o_ref, xn_ref):
        @pl.when(pl.program_id(1) == 0)
        def _():
            xf = x_ref[...].astype(F32)
            xn_ref[...] = (xf * _rms_r(xf) * g_ref[...]).astype(BF16)

        o_ref[...] = _dot(xn_ref[...], w_ref[...]).astype(BF16)

    return _call(
        body, name=name, grid=(T // tm, N // tn),
        in_specs=[pl.BlockSpec((tm, D), lambda i, j: (i, 0)),
                  pl.BlockSpec((1, D), lambda i, j: (0, 0)),
                  pl.BlockSpec((D, tn), lambda i, j: (0, j))],
        out_specs=[pl.BlockSpec((None, tm, tn), lambda i, j: (j // per, i, j % per)),
                   pl.BlockSpec((tm, D), lambda i, j: (i, 0))],
        out_shape=[SDS((split, T, N // split), BF16), SDS((T, D), BF16)],
        semantics=("parallel", "arbitrary"), args=(x, g, w), ride=ride)


BIG_ROW_TILE = 1024


def norm2_matmul(x, gains, weights, *, name, tm=BIG_ROW_TILE):
    T, D = x.shape
    tm = min(tm, T)
    n = len(gains)

    def body(x_ref, *refs):
        xf = x_ref[...].astype(F32)
        xh = xf * _rms_r(xf)
        for g_ref, w_ref, o_ref, xn_ref in zip(refs[:n], refs[n:2 * n], refs[2 * n::2], refs[2 * n + 1::2]):
            xn = (xh * g_ref[...]).astype(BF16)
            xn_ref[...] = xn
            o_ref[...] = _dot(xn, w_ref[...]).astype(BF16)

    row = pl.BlockSpec((tm, D), lambda i: (i, 0))
    vec = pl.BlockSpec((1, D), lambda i: (0, 0))
    out_specs, out_shape = [], []
    for w in weights:
        out_specs += [pl.BlockSpec((tm, w.shape[1]), lambda i: (i, 0)), row]
        out_shape += [SDS((T, w.shape[1]), BF16), SDS((T, D), BF16)]
    return _call(
        body, name=name, grid=(T // tm,),
        in_specs=[row] + [vec] * n + [pl.BlockSpec(w.shape, lambda i: (0, 0)) for w in weights],
        out_specs=out_specs, out_shape=out_shape, semantics=("parallel",), args=[x] + list(gains) + list(weights))


def _shift_down(prev, cur, by):
    big = jnp.concatenate([prev, cur], axis=0)
    return pltpu.roll(big, by, 0)[prev.shape[0]:]


def _shift_up(cur, nxt, by):
    big = jnp.concatenate([cur, nxt], axis=0)
    return pltpu.roll(big, big.shape[0] - by, 0)[:cur.shape[0]]


def conv_mix_out(bcx, conv_w, w_out, g_post, res, *, name, ride=None, tm=ROW_TILE):
    T, D = res.shape
    hb = tm // BF16_ROWS

    def body(b_ref, c_ref, u_ref, cp_ref, up_ref, cw_ref, w_ref, g_ref, r_ref, h_ref, z_ref, y_ref):
        i = pl.program_id(0)
        cu = c_ref[...].astype(F32) * u_ref[...].astype(F32)
        cup = cp_ref[...].astype(F32) * up_ref[...].astype(F32)
        cup = jnp.where(i == 0, 0.0, cup)
        cv = (cw_ref[0:1, :] * _shift_down(cup, cu, 2) + cw_ref[1:2, :] * _shift_down(cup, cu, 1)
              + cw_ref[2:3, :] * cu)
        y = (b_ref[...].astype(F32) * cv).astype(BF16)
        y_ref[...] = y
        z = _dot(y, w_ref[...])
        z_ref[...] = z.astype(BF16)
        h_ref[...] = (r_ref[...] + z * _rms_r(z) * g_ref[...]).astype(STREAM)

    tile = lambda col: pl.BlockSpec((tm, D), lambda i: (i, col))
    halo = lambda col: pl.BlockSpec((BF16_ROWS, D), lambda i: (jnp.maximum(i * hb - 1, 0), col))
    row = pl.BlockSpec((tm, D), lambda i: (i, 0))
    return _call(
        body, name=name, grid=(T // tm,),
        in_specs=[tile(0), tile(1), tile(2), halo(1), halo(2),
                  pl.BlockSpec((3, D), lambda i: (0, 0)),
                  pl.BlockSpec((D, D), lambda i: (0, 0)),
                  pl.BlockSpec((1, D), lambda i: (0, 0)), row],
        out_specs=[row, row, row],
        out_shape=[SDS((T, D), STREAM), SDS((T, D), BF16), SDS((T, D), BF16)],
        semantics=("parallel",), args=(bcx, bcx, bcx, bcx, bcx, conv_w, w_out, g_post, res), ride=ride)


def plain_mix_out(a, w, g_post, res, *, name, target=None, ride=None, tm=ROW_TILE):
    T, D = res.shape
    tm = min(tm, T)
    K = a.shape[1]
    with_loss = target is not None

    def body(a_ref, w_ref, g_ref, r_ref, *rest):
        z = _dot(a_ref[...], w_ref[...])
        h = r_ref[...].astype(F32) + z * _rms_r(z) * g_ref[...]
        if with_loss:
            t_ref, h_ref, z_ref, loss_ref = rest
            diff = h - t_ref[...]
            h_ref[...] = (diff * (1.0 / D)).astype(STREAM)
            part = jnp.full(loss_ref.shape, 0.5 / D, F32) * jnp.sum(diff * diff)
            _accumulate(loss_ref, pl.program_id(0) == 0, part)
        else:
            h_ref, z_ref = rest
            h_ref[...] = h.astype(STREAM)
        z_ref[...] = z.astype(BF16)

    row = pl.BlockSpec((tm, D), lambda i: (i, 0))
    loss_spec, loss_shape = pl.BlockSpec((8, 128), lambda i: (0, 0)), SDS((8, 128), F32)
    return _call(
        body, name=name, grid=(T // tm,),
        in_specs=[pl.BlockSpec((tm, K), lambda i: (i, 0)),
                  pl.BlockSpec((K, D), lambda i: (0, 0)),
                  pl.BlockSpec((1, D), lambda i: (0, 0)), row] + [row] * with_loss,
        out_specs=[row, row] + [loss_spec] * with_loss,
        out_shape=[SDS((T, D), STREAM), SDS((T, D), BF16)] + [loss_shape] * with_loss,
        semantics=("arbitrary",), args=(a, w, g_post, res) + ((target,) if with_loss else ()), ride=ride)


def _silu_grads(d, g, u):
    sg = jax.nn.sigmoid(g)
    return d * u * (sg * (1.0 + g * (1.0 - sg))), d * (g * sg)


def norm_swiglu_in(x, g, w, *, name, ride=None, tm=ROW_TILE // 2):
    T, D = x.shape
    F = w.shape[1] // 2

    def body(x_ref, g_ref, wg_ref, wu_ref, gu_ref, a_ref, xt_ref):
        xf = x_ref[...].astype(F32)
        xn = xf * _rms_r(xf) * g_ref[...]
        xt_ref[...] = xn.T.astype(BF16)
        xb = xn.astype(BF16)
        gate = _dot(xb, wg_ref[...]).astype(BF16)
        up = _dot(xb, wu_ref[...]).astype(BF16)
        gu_ref[0] = gate
        gu_ref[1] = up
        a_ref[...] = gate * jax.nn.sigmoid(gate) * up

    half = lambda s: pl.BlockSpec((D, F), lambda i: (0, s), pipeline_mode=pl.Buffered(1))
    return _call(
        body, name=name, grid=(T // tm,),
        in_specs=[pl.BlockSpec((tm, D), lambda i: (i, 0)), pl.BlockSpec((1, D), lambda i: (0, 0)), half(0), half(1)],
        out_specs=[pl.BlockSpec((2, tm, F), lambda i: (0, i, 0)), pl.BlockSpec((tm, F), lambda i: (i, 0)),
                   pl.BlockSpec((D, tm), lambda i: (0, i))],
        out_shape=[SDS((2, T, F), BF16), SDS((T, F), BF16), SDS((D, T), BF16)],
        semantics=("parallel",), args=(x, g, w, w), ride=ride)


def swiglu_bwd_tn(xt, dact, gu, *, name, ride=None, tb=MXU_WIDTH):
    D, T = xt.shape
    F = dact.shape[1]

    def body(xt_ref, d_ref, g_ref, u_ref, o_ref):
        dg, du = _silu_grads(d_ref[...], g_ref[...], u_ref[...])
        o_ref[0] = _dot(xt_ref[...], dg).astype(BF16)
        o_ref[1] = _dot(xt_ref[...], du).astype(BF16)

    col = lambda s: pl.BlockSpec((None, T, tb), lambda j: (s, 0, j))
    out = _call(
        body, name=name, grid=(F // tb,),
        in_specs=[pl.BlockSpec((D, T), lambda j: (0, 0), pipeline_mode=pl.Buffered(1)),
                  pl.BlockSpec((T, tb), lambda j: (0, j)), col(0), col(1)],
        out_specs=[pl.BlockSpec((2, D, tb), lambda j: (0, 0, j))],
        out_shape=[SDS((2, D, F), BF16)],
        semantics=("parallel",), args=(xt, dact, gu, gu), ride=ride)
    return out[0] if ride is None else (out[0][0], out[1])


def swiglu_bwd_in(dact, gu, w, h_in, g, dh_out, *, name, ride=None, tm=ROW_TILE // 2):
    T, D = h_in.shape
    F = dact.shape[1]

    def body(d_ref, gg_ref, uu_ref, wg_ref, wu_ref, h_ref, g_ref, dh_ref, o_ref, dg_ref):
        dgate, dup = _silu_grads(d_ref[...], gg_ref[...], uu_ref[...])
        dn = _dot_nt(dgate, wg_ref[...]) + _dot_nt(dup, wu_ref[...])
        dx, hh = _rmsnorm_bwd(h_ref[...].astype(F32), g_ref[...], dn)
        o_ref[...] = (dh_ref[...] + dx).astype(STREAM)
        _accumulate(dg_ref, pl.program_id(0) == 0, jnp.sum(dn * hh, axis=0, keepdims=True))

    row = pl.BlockSpec((tm, D), lambda i: (i, 0))
    vec = pl.BlockSpec((1, D), lambda i: (0, 0))
    part = lambda s: pl.BlockSpec((None, tm, F), lambda i: (s, i, 0))
    half = lambda s: pl.BlockSpec((D, F), lambda i: (0, s), pipeline_mode=pl.Buffered(1))
    return _call(
        body, name=name, grid=(T // tm,),
        in_specs=[pl.BlockSpec((tm, F), lambda i: (i, 0)), part(0), part(1), half(0), half(1), row, vec, row],
        out_specs=[row, vec],
        out_shape=[SDS((T, D), STREAM), SDS((1, D), F32)],
        semantics=("arbitrary",), args=(dact, gu, gu, w, w, h_in, g, dh_out), ride=ride)


def rope_tables(T):
    half = ROT_DIM // 2
    inv_freq = ROPE_THETA ** (-jnp.arange(0, ROT_DIM, 2, dtype=F32) / ROT_DIM)
    ang = (jnp.arange(T, dtype=F32)[:, None] * inv_freq[None, :]).T
    cos, sin = jnp.cos(ang), jnp.sin(ang)
    rest = HEAD_DIM - ROT_DIM
    one, zero = jnp.ones((rest, T), F32), jnp.zeros((rest, T), F32)
    zh = jnp.zeros((half, T), F32)
    fac = jnp.concatenate([cos, cos, one], axis=0)
    up = jnp.concatenate([-sin, zh, zero], axis=0)
    down = jnp.concatenate([zh, sin, zero], axis=0)
    return jnp.stack([fac, up, down])


def _rope(t, tab):
    half = ROT_DIM // 2
    return t * tab[0] + pltpu.roll(t, HEAD_DIM - half, 0) * tab[1] + pltpu.roll(t, half, 0) * tab[2]


def _rope_t(d, tab):
    half = ROT_DIM // 2
    return d * tab[0] + pltpu.roll(d * tab[1], half, 0) + pltpu.roll(d * tab[2], HEAD_DIM - half, 0)


def _head(t, h):
    return t[h * HEAD_DIM:(h + 1) * HEAD_DIM]


def _band(n, group):
    kj = lax.broadcasted_iota(jnp.int32, (2 * BLOCK, BLOCK), 0)
    qi = lax.broadcasted_iota(jnp.int32, (2 * BLOCK, BLOCK), 1)
    mask = (kj > qi) & (kj <= qi + BLOCK) & ((n > 0) | (kj >= BLOCK))
    return jnp.tile(mask, (1, group))


def _attn_specs(D, kvd):
    prev = lambda n: jnp.maximum(n - 1, 0)
    return [pl.BlockSpec((BLOCK, D), lambda n: (n, 0)),
            pl.BlockSpec((BLOCK, kvd), lambda n: (prev(n), 0)),
            pl.BlockSpec((BLOCK, kvd), lambda n: (n, 0)),
            pl.BlockSpec((BLOCK, kvd), lambda n: (prev(n), 1)),
            pl.BlockSpec((BLOCK, kvd), lambda n: (n, 1)),
            pl.BlockSpec((3, HEAD_DIM, BLOCK), lambda n: (0, 0, prev(n))),
            pl.BlockSpec((3, HEAD_DIM, BLOCK), lambda n: (0, 0, n)),
            pl.BlockSpec(memory_space=pltpu.SMEM)]


def _attn_operands(q_ref, kp_ref, k_ref, vp_ref, v_ref, tp_ref, t_ref):
    flip = lambda ref: ref[...].astype(F32).T
    tab = t_ref[...]
    kt = jnp.concatenate([flip(kp_ref), flip(k_ref)], axis=1)
    vt = jnp.concatenate([flip(vp_ref), flip(v_ref)], axis=1)
    return flip(q_ref), kt, vt, tab, jnp.concatenate([tp_ref[...], tab], axis=2)


SCORE_SCALE = 1.0 / math.sqrt(HEAD_DIM)
HEADS_TOGETHER = 4


def _group_heads(t, first, count, tab=None):
    heads = [_head(t, first + g) for g in range(count)]
    if tab is not None:
        heads = [_rope(h, tab) * SCORE_SCALE for h in heads]
    return jnp.concatenate(heads, axis=1).astype(BF16)


def _sink_row(s_ref, first, count):
    which = lax.broadcasted_iota(jnp.int32, (1, count * BLOCK), 1) // BLOCK
    row = jnp.zeros((1, count * BLOCK), F32)
    for g in range(count):
        row = jnp.where(which == g, s_ref[0, first + g], row)
    return row


def _softmax(scores, sink, mask):
    s = jnp.where(mask, scores, NEG)
    m = jnp.maximum(jnp.max(s, axis=0, keepdims=True), sink)
    e = jnp.exp(s - m)
    es = jnp.exp(sink - m)
    return e, es, 1.0 / (jnp.sum(e, axis=0, keepdims=True) + es)


def attention_fwd(q, kv, tabs, sinks, *, name, ride=None):
    T, D = q.shape
    kvd = kv.shape[1] // 2
    group = D // HEAD_DIM // N_KV_HEADS

    def body(q_ref, kp_ref, k_ref, vp_ref, v_ref, tp_ref, t_ref, s_ref, o_ref):
        gs = HEADS_TOGETHER
        mask = _band(pl.program_id(0), gs)
        qt, kt, vt, tab, tab2 = _attn_operands(q_ref, kp_ref, k_ref, vp_ref, v_ref, tp_ref, t_ref)
        firsts = [(j, first) for j in range(N_KV_HEADS) for first in range(j * group, (j + 1) * group, gs)]
        ks = [_rope(_head(kt, j), tab2).astype(BF16) for j in range(N_KV_HEADS)]
        scores = [_dot_tn(ks[j], _group_heads(qt, first, gs, tab)) for j, first in firsts]
        soft = [_softmax(s, _sink_row(s_ref, first, gs), mask) for s, (j, first) in zip(scores, firsts)]
        outs = []
        for (e, _, inv), (j, first) in zip(soft, firsts):
            o = _dot(_head(vt, j).astype(BF16), e.astype(BF16)) * inv
            outs += [o[:, g * BLOCK:(g + 1) * BLOCK] for g in range(gs)]
        o_ref[...] = jnp.concatenate(outs, axis=0).T.astype(BF16)

    return _call(
        body, name=name, grid=(T // BLOCK,),
        in_specs=_attn_specs(D, kvd),
        out_specs=[pl.BlockSpec((BLOCK, D), lambda n: (n, 0))],
        out_shape=[SDS((T, D), BF16)],
        semantics=("parallel",), args=(q, kv, kv, kv, kv, tabs, tabs, sinks), ride=ride)


def attention_bwd(q, kv, tabs, sinks, do, *, name, ride=None):
    T, D = q.shape
    kvd = kv.shape[1] // 2
    heads = D // HEAD_DIM
    group = heads // N_KV_HEADS

    def body(q_ref, kp_ref, k_ref, vp_ref, v_ref, tp_ref, t_ref, s_ref, do_ref, dq_ref, dc_ref, dp_ref, ds_ref):
        n = pl.program_id(0)
        gs = HEADS_TOGETHER
        mask = _band(n, gs)
        qt, kt, vt, tab, tab2 = _attn_operands(q_ref, kp_ref, k_ref, vp_ref, v_ref, tp_ref, t_ref)
        dot = do_ref[...].astype(F32).T
        lane = lax.broadcasted_iota(jnp.int32, (8, 128), 1)
        dsink = jnp.zeros((8, 128), F32)
        firsts = [(j, first) for j in range(N_KV_HEADS) for first in range(j * group, (j + 1) * group, gs)]
        ks = [_rope(_head(kt, j), tab2).astype(BF16) for j in range(N_KV_HEADS)]
        vs = [_head(vt, j).astype(BF16) for j in range(N_KV_HEADS)]
        qs = [_group_heads(qt, first, gs, tab) for _, first in firsts]
        dos = [_group_heads(dot, first, gs) for _, first in firsts]
        scores = [_dot_tn(ks[j], q) for q, (j, _) in zip(qs, firsts)]
        dps = [_dot_tn(vs[j], do) for do, (j, _) in zip(dos, firsts)]
        ps, dscs = [], []
        for s, dp, (j, first) in zip(scores, dps, firsts):
            e, e_sink, inv = _softmax(s, _sink_row(s_ref, first, gs), mask)
            p = e * inv
            dl = jnp.sum(p * dp, axis=0, keepdims=True)
            dscs.append((p * (dp - dl)).astype(BF16))
            ps.append(p.astype(BF16))
            weight = e_sink * inv * dl
            for g in range(gs):
                dsink = dsink - jnp.where(lane == first + g, jnp.sum(weight[:, g * BLOCK:(g + 1) * BLOCK]), 0.0)
        dqs = []
        dks = [jnp.zeros((HEAD_DIM, 2 * BLOCK), F32) for _ in range(N_KV_HEADS)]
        dvs = [jnp.zeros((HEAD_DIM, 2 * BLOCK), F32) for _ in range(N_KV_HEADS)]
        for p, dsc, q, do, (j, _) in zip(ps, dscs, qs, dos, firsts):
            dq = _dot(ks[j], dsc) * SCORE_SCALE
            dqs += [_rope_t(dq[:, g * BLOCK:(g + 1) * BLOCK], tab) for g in range(gs)]
            dks[j] = dks[j] + _dot_nt(q, dsc)
            dvs[j] = dvs[j] + _dot_nt(do, p)
        dks = [_rope_t(dk, tab2) for dk in dks]
        dq_ref[...] = jnp.concatenate(dqs, axis=0).T.astype(BF16)
        dkv = jnp.concatenate(dks + dvs, axis=0)
        dp_ref[...] = dkv[:, :BLOCK].T
        dc_ref[...] = dkv[:, BLOCK:].T
        _accumulate(ds_ref, n == 0, dsink)

    blk = lambda w: pl.BlockSpec((BLOCK, w), lambda n: (n, 0))
    return _call(
        body, name=name, grid=(T // BLOCK,),
        in_specs=_attn_specs(D, kvd) + [blk(D)],
        out_specs=[blk(D), blk(2 * kvd), blk(2 * kvd), pl.BlockSpec((8, 128), lambda n: (0, 0))],
        out_shape=[SDS((T, D), BF16), SDS((T, 2 * kvd), F32), SDS((T, 2 * kvd), F32), SDS((8, 128), F32)],
        semantics=("arbitrary",), args=(q, kv, kv, kv, kv, tabs, tabs, sinks, do), ride=ride)


def combine_dkv(d_cur, d_prev, *, name):
    T, W = d_cur.shape
    tm = ROW_TILE
    nt, per, last = T // tm, tm // BLOCK, T // BLOCK - 1

    def body(c_ref, p_ref, pn_ref, o_ref):
        nxt = jnp.where(pl.program_id(0) == nt - 1, 0.0, pn_ref[...])
        o_ref[...] = (c_ref[...] + jnp.concatenate([p_ref[BLOCK:, :], nxt], axis=0)).astype(BF16)

    return _call(
        body, name=name, grid=(nt,),
        in_specs=[pl.BlockSpec((tm, W), lambda i: (i, 0)), pl.BlockSpec((tm, W), lambda i: (i, 0)),
                  pl.BlockSpec((BLOCK, W), lambda i: (jnp.minimum((i + 1) * per, last), 0))],
        out_specs=[pl.BlockSpec((tm, W), lambda i: (i, 0))],
        out_shape=[SDS((T, W), BF16)],
        semantics=("parallel",), args=(d_cur, d_prev, d_prev))[0]


def normbwd_matmul_nt(z, g, dh, w, *, name, ride=None, tm=ROW_TILE):
    T, D = z.shape
    tm = min(tm, T)
    K = w.shape[0]

    def body(z_ref, g_ref, dh_ref, w_ref, dz_ref, dg_ref, o_ref):
        dh_ = dh_ref[...].astype(F32)
        dz, zh = _rmsnorm_bwd(z_ref[...].astype(F32), g_ref[...], dh_)
        dz = dz.astype(BF16)
        dz_ref[...] = dz
        _accumulate(dg_ref, pl.program_id(0) == 0, jnp.sum(dh_ * zh, axis=0, keepdims=True))
        o_ref[...] = _dot_nt(dz, w_ref[...]).astype(BF16)

    row = pl.BlockSpec((tm, D), lambda i: (i, 0))
    vec = pl.BlockSpec((1, D), lambda i: (0, 0))
    return _call(
        body, name=name, grid=(T // tm,),
        in_specs=[row, vec, row, pl.BlockSpec((K, D), lambda i: (0, 0))],
        out_specs=[row, vec, pl.BlockSpec((tm, K), lambda i: (i, 0))],
        out_shape=[SDS((T, D), BF16), SDS((1, D), F32), SDS((T, K), BF16)],
        semantics=("arbitrary",), args=(z, g, dh, w), ride=ride)


def matmul_nt_normbwd(da, w, h_in, g, dh_out, *, name, ride=None, tm=ROW_TILE):
    T, D = h_in.shape
    S, _, K = da.shape

    def body(*refs):
        da_refs, w_refs = refs[:S], refs[S:2 * S]
        h_ref, g_ref, dh_ref, o_ref, dg_ref = refs[2 * S:]
        dn = _dot_nt(da_refs[0][...], w_refs[0][...])
        for s in range(1, S):
            dn = dn + _dot_nt(da_refs[s][...], w_refs[s][...])
        dx, hh = _rmsnorm_bwd(h_ref[...].astype(F32), g_ref[...], dn)
        o_ref[...] = dh_ref[...] + dx
        _accumulate(dg_ref, pl.program_id(0) == 0, jnp.sum(dn * hh, axis=0, keepdims=True))

    row = pl.BlockSpec((tm, D), lambda i: (i, 0))
    vec = pl.BlockSpec((1, D), lambda i: (0, 0))
    part = lambda s: pl.BlockSpec((None, tm, K), lambda i: (s, i, 0))
    cols = lambda s: pl.BlockSpec((D, K), lambda i: (0, s), pipeline_mode=pl.Buffered(1))
    return _call(
        body, name=name, grid=(T // tm,),
        in_specs=[part(s) for s in range(S)] + [cols(s) for s in range(S)] + [row, vec, row],
        out_specs=[row, vec],
        out_shape=[SDS((T, D), F32), SDS((1, D), F32)],
        semantics=("arbitrary",), args=[da] * S + [w] * S + [h_in, g, dh_out], ride=ride)


def matmuls_nt_normbwd(das, ws, h_in, gs, dh_out, *, name, ride=None, tm=BIG_ROW_TILE):
    T, D = h_in.shape
    tm = min(tm, T)
    n = len(das)

    def body(*refs):
        da_refs, w_refs, g_refs = refs[:n], refs[n:2 * n], refs[2 * n:3 * n]
        h_ref, dh_ref, o_ref = refs[3 * n:3 * n + 3]
        hf = h_ref[...].astype(F32)
        r = _rms_r(hf)
        hh = hf * r
        total = dh_ref[...].astype(F32)
        for da_ref, w_ref, g_ref, dg_ref in zip(da_refs, w_refs, g_refs, refs[3 * n + 3:]):
            dn = _dot_nt(da_ref[...], w_ref[...])
            gd = g_ref[...] * dn
            total = total + r * (gd - hh * jnp.mean(hh * gd, axis=-1, keepdims=True))
            _accumulate(dg_ref, pl.program_id(0) == 0, jnp.sum(dn * hh, axis=0, keepdims=True))
        o_ref[...] = total.astype(STREAM)

    row = pl.BlockSpec((tm, D), lambda i: (i, 0))
    vec = pl.BlockSpec((1, D), lambda i: (0, 0))
    return _call(
        body, name=name, grid=(T // tm,),
        in_specs=[pl.BlockSpec((tm, da.shape[1]), lambda i: (i, 0)) for da in das]
        + [pl.BlockSpec(w.shape, lambda i: (0, 0)) for w in ws] + [vec] * n + [row, row],
        out_specs=[row] + [vec] * n,
        out_shape=[SDS((T, D), STREAM)] + [SDS((1, D), F32)] * n,
        semantics=("arbitrary",), args=list(das) + list(ws) + list(gs) + [h_in, dh_out], ride=ride)


def matmul_tn(a, b, *, tb, name, ride=None, ta=MXU_WIDTH):
    T, Ka = a.shape
    S, _, Nb = b.shape
    per = Nb // tb

    def body(a_ref, b_ref, o_ref):
        o_ref[...] = _dot_tn(a_ref[...], b_ref[...]).astype(BF16)

    out = _call(
        body, name=name, grid=(S * per, Ka // ta),
        in_specs=[pl.BlockSpec((T, ta), lambda j, i: (0, i)),
                  pl.BlockSpec((None, T, tb), lambda j, i: (j // per, 0, j % per))],
        out_specs=[pl.BlockSpec((ta, tb), lambda j, i: (i, j))],
        out_shape=[SDS((Ka, S * Nb), BF16)],
        semantics=("parallel", "parallel"), args=(a, b), ride=ride)
    return out[0] if ride is None else (out[0][0], out[1])


def conv_bwd(dy, bcx, conv_w, *, name, ride=None, tm=ROW_TILE):
    T, D = dy.shape
    nt = T // tm
    hb = tm // BF16_ROWS
    last = T // BF16_ROWS - 1

    def body(dy_ref, dyn_ref, b_ref, bn_ref, c_ref, u_ref, cp_ref, up_ref, cw_ref, o_ref, dw_ref):
        i = pl.program_id(0)
        c, u = c_ref[...].astype(F32), u_ref[...].astype(F32)
        cu = c * u
        cup = jnp.where(i == 0, 0.0, cp_ref[...].astype(F32) * up_ref[...].astype(F32))
        cu1, cu2 = _shift_down(cup, cu, 1), _shift_down(cup, cu, 2)
        w0, w1, w2 = cw_ref[0:1, :], cw_ref[1:2, :], cw_ref[2:3, :]
        dyf = dy_ref[...].astype(F32)
        o_ref[:, 0:D] = (dyf * (w0 * cu2 + w1 * cu1 + w2 * cu)).astype(BF16)
        dcv = dyf * b_ref[...].astype(F32)
        dcvn = jnp.where(i == nt - 1, 0.0, dyn_ref[...].astype(F32) * bn_ref[...].astype(F32))
        dcu = w2 * dcv + w1 * _shift_up(dcv, dcvn, 1) + w0 * _shift_up(dcv, dcvn, 2)
        o_ref[:, D:2 * D] = (dcu * u).astype(BF16)
        o_ref[:, 2 * D:3 * D] = (dcu * c).astype(BF16)
        row = lax.broadcasted_iota(jnp.int32, (8, D), 0)
        dw = jnp.zeros((8, D), F32)
        for tap, t in enumerate((cu2, cu1, cu)):
            dw = jnp.where(row == tap, jnp.sum(dcv * t, axis=0, keepdims=True), dw)
        _accumulate(dw_ref, i == 0, dw)

    tile = lambda col: pl.BlockSpec((tm, D), lambda i: (i, col))
    prev = lambda col: pl.BlockSpec((BF16_ROWS, D), lambda i: (jnp.maximum(i * hb - 1, 0), col))
    nxt = lambda col: pl.BlockSpec((BF16_ROWS, D), lambda i: (jnp.minimum((i + 1) * hb, last), col))
    return _call(
        body, name=name, grid=(nt,),
        in_specs=[tile(0), nxt(0), tile(0), nxt(0), tile(1), tile(2), prev(1), prev(2),
                  pl.BlockSpec((3, D), lambda i: (0, 0))],
        out_specs=[pl.BlockSpec((tm, 3 * D), lambda i: (i, 0)), pl.BlockSpec((8, D), lambda i: (0, 0))],
        out_shape=[SDS((T, 3 * D), BF16), SDS((8, D), F32)],
        semantics=("arbitrary",), args=(dy, dy, bcx, bcx, bcx, bcx, bcx, bcx, conv_w), ride=ride)


class NoTraffic:
    def ride(self, kernel_name):
        return None

    def landed(self, kernel_name, results, wts):
        pass

    def grad(self, key, value):
        pass


def local_step(x, target, wts, vec, traffic):
    T, D = x.shape
    tabs = rope_tables(T)
    small = {}

    def run(builder, *args, name, **kw):
        ride = traffic.ride(name)
        if ride is None:
            return builder(*args, name=name, **kw)
        out, extra = builder(*args, name=name, ride=ride, **kw)
        traffic.landed(name, extra, wts)
        return out

    bcx, xn1 = run(norm_matmul, x, vec["a_pre"], wts["w_in"], tn=3 * D, split=1, name="a_in")
    bcx = bcx[0]
    h1, z0, y0 = run(conv_mix_out, bcx, vec["conv_w"], wts["w_out"], vec["a_post"], x, name="a_out")
    gu0, act0, xt2 = run(norm_swiglu_in, h1, vec["ffn_pre0"], wts["gu0"], name="ffn0_in")
    h2, z1 = run(plain_mix_out, act0, wts["wd0"], vec["ffn_post0"], h1, name="ffn0_out")
    kvp, xkv, qp, xq = norm2_matmul(h2, [vec["kv_norm"], vec["b_pre"]], [wts["w_kv"], wts["w_q"]], name="kvq_in")
    (attn,) = run(attention_fwd, qp, kvp, tabs, vec["sinks"], name="attn_fwd")
    h3, z2 = plain_mix_out(attn, wts["w_o"], vec["b_post"], h2, name="attn_out", tm=BIG_ROW_TILE)
    gu1, act1, xt3 = run(norm_swiglu_in, h3, vec["ffn_pre1"], wts["gu1"], name="ffn1_in")
    dy, z3, loss = plain_mix_out(act1, wts["wd1"], vec["ffn_post1"], h3, name="ffn1_out", target=target)

    def ffn_bwd(layer, z, gu, act, xt, h_in, dh, gu_first):
        tag = "ffn%d" % layer
        dz, small["ffn_post%d" % layer], dact = run(
            normbwd_matmul_nt, z, vec["ffn_post%d" % layer], dh, wts["wd%d" % layer], name=tag + "_out_bwd")
        dwd = lambda: traffic.grad("wd%d" % layer, run(matmul_tn, act, dz[None], tb=D, name=tag + "_dwd"))
        dwgu = lambda: traffic.grad("gu%d" % layer, run(swiglu_bwd_tn, xt, dact, gu, name=tag + "_dwgu"))
        for step in ((dwgu, dwd) if gu_first else (dwd, dwgu)):
            step()
        dh_in, small["ffn_pre%d" % layer] = run(
            swiglu_bwd_in, dact, gu, wts["gu%d" % layer], h_in, vec["ffn_pre%d" % layer], dh, name=tag + "_in_bwd")
        return dh_in

    dh3 = ffn_bwd(1, z3, gu1, act1, xt3, h3, dy, gu_first=False)
    dz2, small["b_post"], dattn = normbwd_matmul_nt(z2, vec["b_post"], dh3, wts["w_o"], name="attn_out_bwd",
                                                    tm=BIG_ROW_TILE)
    traffic.grad("w_o", matmul_tn(attn, dz2[None], tb=D, name="attn_dwo"))
    dq, dkv_cur, dkv_prev, small["sinks"] = run(attention_bwd, qp, kvp, tabs, vec["sinks"], dattn, name="attn_bwd")
    dkv = combine_dkv(dkv_cur, dkv_prev, name="attn_dkv")
    traffic.grad("w_q", matmul_tn(xq, dq[None], tb=D, name="attn_dwq"))
    traffic.grad("w_kv", matmul_tn(xkv, dkv[None], tb=dkv.shape[1], name="attn_dwkv"))
    dh2, small["b_pre"], small["kv_norm"] = run(
        matmuls_nt_normbwd, [dq, dkv], [wts["w_q"], wts["w_kv"]], h2, [vec["b_pre"], vec["kv_norm"]], dh3,
        name="qkv_in_bwd")
    dh1 = ffn_bwd(0, z1, gu0, act0, xt2, h1, dh2, gu_first=True)
    dz0, small["a_post"], dyc = run(normbwd_matmul_nt, z0, vec["a_post"], dh1, wts["w_out"], name="a_out_bwd",
                                    tm=BIG_ROW_TILE)
    traffic.grad("w_out", matmul_tn(y0, dz0[None], tb=D, name="a_dwout"))
    dbcx, small["conv_w"] = run(conv_bwd, dyc, bcx, vec["conv_w"], name="a_conv_bwd")
    traffic.grad("w_in", matmul_tn(xn1, dbcx[None], tb=3 * D // 2, name="a_dwin"))
    dx, small["a_pre"] = run(matmul_nt_normbwd, dbcx[None], wts["w_in"], x, vec["a_pre"], dh1, name="a_in_bwd",
                             tm=ROW_TILE // 2)
    return loss, dx, small


SMALL_ROWS = 16
LOSS_ROW = 13

WHOLE = None
GATHER_PLAN = {"cast_rest": [("w_in", WHOLE)],
               "a_in": [("w_out", WHOLE), ("gu0", (0, 18))],
               "a_out": [("gu0", (18, 14))],
               "ffn0_in": [("wd0", WHOLE), ("w_kv", WHOLE), ("w_q", WHOLE), ("w_o", WHOLE)],
               "ffn0_out": [("gu1", (0, 16))],
               "attn_fwd": [("gu1", (16, 16))],
               "ffn1_in": [("wd1", WHOLE)]}
PAIR_PLAN = {"ffn1_dwgu": ["wd1"], "ffn1_in_bwd": ["gu1"], "attn_bwd": ["w_o"], "qkv_in_bwd": ["w_q", "w_kv"],
             "ffn0_dwd": ["gu0"], "ffn0_in_bwd": ["wd0"], "a_conv_bwd": ["w_out"]}
PAIR_ALONE = ["w_in"]
CHIP_PLAN = {"ffn1_in_bwd": [("wd1", WHOLE)], "attn_bwd": [("gu1", WHOLE)],
             "ffn0_out_bwd": [("w_o", WHOLE), ("w_q", WHOLE), ("w_kv", WHOLE)],
             "ffn0_in_bwd": [("gu0", WHOLE)], "a_out_bwd": [("wd0", (0, 14))], "a_conv_bwd": [("wd0", (14, 8))],
             "a_in_bwd": [("w_out", WHOLE), ("w_in", WHOLE)]}
HALF_PLAN = {"a_in_bwd": ["gu0", "gu1", "wd0", "wd1", "w_kv", "w_q", "w_o"]}
GRAD_KIND = dict(KIND, gu0="split", gu1="split")


class Traffic:
    def __init__(self, wholes, quarter, c_arr, pc_arr):
        self.wholes, self.quarter, self.c_arr, self.pc_arr = wholes, quarter, c_arr, pc_arr
        self.views, self.sums, self.got = {}, {}, {}
        self.reduced = {}
        self.stages = {}

    def reduce(self, keys, name):
        return chip_reduce([self.sums[k] for k in keys], [self.got[k] for k in keys], [GRAD_KIND[k] for k in keys],
                           self.pc_arr, name=name)

    def ride(self, name, small=None):
        rides, stages = [], []
        if name in GATHER_PLAN:
            plan = GATHER_PLAN[name]
            rides.append(gather_ride([self.wholes[k] for k, _ in plan],
                                     [(KIND[k], self.quarter[k], part) for k, part in plan], small))
            stages.append(("gather", [k for k, _ in plan]))
        if name in CHIP_PLAN:
            plan = CHIP_PLAN[name]
            rides.append(chip_ride([self.sums[k] for k, _ in plan],
                                   [(GRAD_KIND[k], self.quarter[k], part) for k, part in plan],
                                   earlier=[self.got.get(k) for k, _ in plan]))
            stages.append(("chip", [k for k, _ in plan]))
        if name in PAIR_PLAN:
            keys = PAIR_PLAN[name]
            rides.append(pair_ride([self.views[k] for k in keys]))
            stages.append(("pair", keys))
        if name in HALF_PLAN:
            keys = HALF_PLAN[name]
            rides.append(half_ride(self.reduce(keys, "chip_reduce_early")))
            stages.append(("half", keys))
        self.stages[name] = stages
        return join(rides)

    def landed(self, name, results, wts):
        results = list(results)
        for stage, keys in self.stages[name]:
            mine, results = results[:len(keys)], results[len(keys):]
            if stage == "gather":
                for k, whole in zip(keys, mine):
                    self.wholes[k] = wts[k] = whole
            elif stage == "chip":
                self.got.update(zip(keys, mine))
            elif stage == "half":
                self.reduced.update(zip(keys, mine))
            else:
                for k, got in zip(keys, mine):
                    self.sums[k] = pair_add(self.views[k], got, self.c_arr, name="pair_add_" + k)

    def grad(self, key, value):
        r, ws = self.quarter[key]
        view = {"row": (N_CHIPS, 2, r // 2, ws), "col": (1, 2, r // 2, N_CHIPS * ws), "split": (2, 2, r // 2, 2 * ws)}
        self.views[key] = value.reshape(view[GRAD_KIND[key]])
        if key in PAIR_ALONE:
            (got,) = alone(pair_ride([self.views[key]]), name="pair_exchange_" + key)
            self.sums[key] = pair_add(self.views[key], got, self.c_arr, name="pair_add_" + key)


def kernel(x, a_pre_norm, a_w_in, a_conv_w, a_w_out, a_post_norm, ffn_pre_norm, ffn_w_gate_up, ffn_w_down, ffn_post_norm, kv_norm, w_kv, b_pre_norm, b_w_q, b_sinks, b_w_o, b_post_norm, loss_target, m_a_pre_norm, m_a_w_in, m_a_conv_w, m_a_w_out, m_a_post_norm, m_ffn_pre_norm, m_ffn_w_gate_up, m_ffn_w_down, m_ffn_post_norm, m_kv_norm, m_w_kv, m_b_pre_norm, m_b_w_q, m_b_sinks, m_b_w_o, m_b_post_norm, v_a_pre_norm, v_a_w_in, v_a_conv_w, v_a_w_out, v_a_post_norm, v_ffn_pre_norm, v_ffn_w_gate_up, v_ffn_w_down, v_ffn_post_norm, v_kv_norm, v_w_kv, v_b_pre_norm, v_b_w_q, v_b_sinks, v_b_w_o, v_b_post_norm):
    T, D = x.shape[1], x.shape[2]
    xi, yi, ci = _place()
    p = 2 * xi + yi
    p_arr = jnp.reshape(p, (1,)).astype(jnp.int32)
    c_arr = jnp.reshape(ci, (1,)).astype(jnp.int32)
    pc_arr = jnp.stack([p, ci]).astype(jnp.int32)
    me_arr = jnp.reshape(4 * xi + 2 * yi + ci, (1,)).astype(jnp.int32)
    qd = D // N_CHIPS

    big = {"w_in": (a_w_in, 0), "w_out": (a_w_out, 0), "gu0": (ffn_w_gate_up, 0), "gu1": (ffn_w_gate_up, 1),
           "wd0": (ffn_w_down, 0), "wd1": (ffn_w_down, 1), "w_kv": (w_kv[None], 0), "w_q": (b_w_q, 0),
           "w_o": (b_w_o, 0)}
    names = list(big)
    quarter = {k: w.shape[1:] for k, (w, _) in big.items()}
    source = lambda k: big[k] + (KIND[k],)
    traffic = Traffic(dict(zip(names[:1], cast_quarters([source(names[0])], p_arr, name="cast_first"))), quarter,
                      c_arr, pc_arr)
    small_shard = jnp.concatenate([a_pre_norm, a_post_norm, a_conv_w[0], jnp.zeros((3, qd), F32)], axis=0)
    wts = {}
    rest, (*landed, small_full) = cast_quarters([source(k) for k in names[1:]], p_arr, name="cast_rest",
                                                ride=traffic.ride("cast_rest", small_shard))
    traffic.wholes.update(zip(names[1:], rest))
    traffic.landed("cast_rest", landed, wts)
    rows = lambda k: jnp.transpose(small_full[:, k], (1, 0, 2)).reshape(-1, D)
    vec = {"a_pre": rows(slice(0, 1)), "a_post": rows(slice(1, 2)), "conv_w": rows(slice(2, 5)),
           "ffn_pre0": ffn_pre_norm[0:1], "ffn_pre1": ffn_pre_norm[1:2],
           "ffn_post0": ffn_post_norm[0:1], "ffn_post1": ffn_post_norm[1:2],
           "kv_norm": kv_norm[None], "b_pre": b_pre_norm, "b_post": b_post_norm, "sinks": b_sinks}

    loss, dx, small = local_step(x[0], loss_target[0], wts, vec, traffic)

    pad = lambda a: jnp.pad(a, ((0, 0), (0, D - a.shape[1])))
    small_block = jnp.concatenate(
        [small["a_pre"], small["a_post"], small["conv_w"][0:3], small["ffn_pre0"], small["ffn_pre1"],
         small["ffn_post0"], small["ffn_post1"], small["kv_norm"], small["b_pre"], small["b_post"],
         pad(small["sinks"][0:1]), pad(loss[0:1]), jnp.zeros((SMALL_ROWS - LOSS_ROW - 1, D), F32)], axis=0)
    late = [k for k in names if k not in traffic.reduced]
    *swapped, small_blocks = alone(join([half_ride(traffic.reduce(late, "chip_reduce_late")),
                                         chip_ride([], [], small_block)]), name="last_exchange")
    traffic.reduced.update(zip(late, swapped))
    grad = {k: traffic.reduced[k].reshape(quarter[k]) for k in names}
    small_sum = small_reduce(small_blocks, me_arr)

    out = {}
    out["a_w_in"] = adamw(a_w_in, [grad["w_in"]], m_a_w_in, v_a_w_in, name="adamw_a_w_in")
    out["a_w_out"] = adamw(a_w_out, [grad["w_out"]], m_a_w_out, v_a_w_out, name="adamw_a_w_out")
    for key, stem, w, m, v in (("ffn_w_gate_up", "gu", ffn_w_gate_up, m_ffn_w_gate_up, v_ffn_w_gate_up),
                               ("ffn_w_down", "wd", ffn_w_down, m_ffn_w_down, v_ffn_w_down)):
        outs = [jax.empty_ref(SDS((w.size,), F32), memory_space=pltpu.MemorySpace.HBM) for _ in range(4)]
        for layer in (1, 0):
            sc_adamw(w.reshape(-1), grad["%s%d" % (stem, layer)].reshape(-1), m.reshape(-1), v.reshape(-1), outs,
                     layer, name="adamw_%s%d" % (stem, layer))
        out[key] = [o[...].reshape(w.shape) for o in outs]
    out["w_kv"] = [o[0] for o in adamw(w_kv[None], [grad["w_kv"]], m_w_kv[None], v_w_kv[None], name="adamw_w_kv")]
    out["b_w_q"] = adamw(b_w_q, [grad["w_q"]], m_b_w_q, v_b_w_q, name="adamw_b_w_q")
    out["b_w_o"] = adamw(b_w_o, [grad["w_o"]], m_b_w_o, v_b_w_o, name="adamw_b_w_o")

    def pack(a_pre, a_post, conv, ffn_pre, ffn_post, kvn, b_pre, b_post, sinks):
        return jnp.concatenate([pad(a_pre), pad(a_post), pad(conv[0]), ffn_pre, ffn_post, kvn[None], b_pre, b_post,
                                pad(sinks), jnp.zeros((SMALL_ROWS - 13, D), F32)], axis=0)

    g_small = jnp.concatenate([pad(lax.dynamic_slice(small_sum, (0, p * qd), (5, qd))), small_sum[5:]], axis=0)
    w_small = pack(a_pre_norm, a_post_norm, a_conv_w, ffn_pre_norm, ffn_post_norm, kv_norm, b_pre_norm, b_post_norm,
                   b_sinks)
    m_small = pack(m_a_pre_norm, m_a_post_norm, m_a_conv_w, m_ffn_pre_norm, m_ffn_post_norm, m_kv_norm,
                   m_b_pre_norm, m_b_post_norm, m_b_sinks)
    v_small = pack(v_a_pre_norm, v_a_post_norm, v_a_conv_w, v_ffn_pre_norm, v_ffn_post_norm, v_kv_norm,
                   v_b_pre_norm, v_b_post_norm, v_b_sinks)
    packed = adamw(w_small[None], [g_small], m_small[None], v_small[None], name="adamw_small")
    ns = b_sinks.shape[1]
    unpack = lambda a: {"a_pre_norm": a[0:1, :qd], "a_post_norm": a[1:2, :qd], "a_conv_w": a[None, 2:5, :qd],
                        "ffn_pre_norm": a[5:7], "ffn_post_norm": a[7:9], "kv_norm": a[9], "b_pre_norm": a[10:11],
                        "b_post_norm": a[11:12], "b_sinks": a[12:13, :ns]}
    unpacked = [unpack(a[0]) for a in packed]
    for k in unpacked[0]:
        out[k] = [u[k] for u in unpacked]

    order = ["a_pre_norm", "a_w_in", "a_conv_w", "a_w_out", "a_post_norm", "ffn_pre_norm", "ffn_w_gate_up",
             "ffn_w_down", "ffn_post_norm", "kv_norm", "w_kv", "b_pre_norm", "b_w_q", "b_sinks", "b_w_o",
             "b_post_norm"]
    return (small_sum[LOSS_ROW, 0], dx[None], *[out[k][0] for k in order], *[out[k][1] for k in order],
            *[out[k][2] for k in order], *[out[k][3] for k in order])
```

```python
import math

import jax
import jax.numpy as jnp
from jax import lax
from jax.experimental import pallas as pl
from jax.experimental.pallas import tpu as pltpu

F32 = jnp.float32
BF16 = jnp.bfloat16
SDS = jax.ShapeDtypeStruct
MESH = pl.DeviceIdType.MESH
DMA = pltpu.SemaphoreType.DMA
HBM_SPEC = pl.BlockSpec(memory_space=pltpu.HBM)

EPS = 1e-6
NEG = -1e30
HEAD_DIM = 64
N_KV_HEADS = 4
BLOCK = 128
ROT_DIM = HEAD_DIM // 4
ROPE_THETA = 500000.0
N_CHIPS = 4

ADAM_LR = 0.001
ADAM_B1 = 0.9
ADAM_B2 = 0.999
ADAM_EPS = 1e-08
ADAM_WD = 0.01
ADAM_STEP = 10

VMEM_LIMIT_BYTES = 52 * 1024 * 1024
ROW_TILE = 512
BF16_ROWS = 16
STREAM = BF16
MXU_WIDTH = 256

KIND = {"w_in": "col", "gu0": "col", "gu1": "col", "w_out": "row", "wd0": "row", "wd1": "row", "w_kv": "row",
        "w_q": "row", "w_o": "row"}


def _params(*semantics):
    return pltpu.CompilerParams(dimension_semantics=semantics, vmem_limit_bytes=VMEM_LIMIT_BYTES)


def _row_tile(rows, limit, step=8):
    return max(t for t in range(step, limit + 1, step) if rows % t == 0)


def _place():
    return lax.axis_index("x"), lax.axis_index("y"), lax.axis_index("c")


def _other_chips(x, y):
    return [(1 - x, y), (x, 1 - y), (1 - x, 1 - y)]


def _remote(src, dst, send_sem, recv_sem, to):
    return pltpu.make_async_remote_copy(src_ref=src, dst_ref=dst, send_sem=send_sem, recv_sem=recv_sem,
                                        device_id=to, device_id_type=MESH)


def _full_shape(kind, quarter):
    r, ws = quarter
    return (N_CHIPS * r, ws) if kind == "row" else (r, N_CHIPS * ws)


def _rows_of(h, part):
    lo, n = (0, h) if part is None else (part[0] * BF16_ROWS, part[1] * BF16_ROWS)
    assert lo + n <= h, (h, part)
    return lo, n


def _half_of_quarter(ref, kind, quarter, part, q, half):
    r, ws = quarter
    h = r // 2
    lo, n = _rows_of(h, part)
    if kind == "row":
        return ref.at[pl.ds(pl.multiple_of(q * r + half * h + lo, BF16_ROWS), n)]
    return ref.at[pl.ds(pl.multiple_of(half * h + lo, BF16_ROWS), n), pl.ds(pl.multiple_of(q * ws, 128), ws)]


class Ride:
    def __init__(self, operands, out_shape, aliases, sems, make):
        self.operands, self.out_shape, self.aliases, self.sems, self.make = operands, out_shape, aliases, sems, make


def join(rides):
    rides = [r for r in rides if r is not None]
    if len(rides) < 2:
        return rides[0] if rides else None
    aliases, at = {}, [0, 0, 0]
    cuts = []
    for r in rides:
        aliases.update({at[0] + i: at[1] + o for i, o in r.aliases.items()})
        cuts.append(tuple(at))
        at = [at[0] + len(r.operands), at[1] + len(r.out_shape), at[2] + len(r.sems)]
    cuts.append(tuple(at))

    def make(ins, outs, sem):
        made = [r.make(ins[lo[0]:hi[0]], outs[lo[1]:hi[1]], sem[lo[2]:hi[2]]) for r, lo, hi in zip(rides, cuts, cuts[1:])]

        def start():
            for s, _ in made:
                s()

        def finish():
            for _, f in made:
                f()

        return start, finish

    return Ride(sum((list(r.operands) for r in rides), []), sum((list(r.out_shape) for r in rides), []), aliases,
                sum((list(r.sems) for r in rides), []), make)


def _call(body, *, name, grid, in_specs, out_specs, out_shape, args, scratch_shapes=(), semantics=None, ride=None,
          prefetch=None):
    pre = 0 if prefetch is None else 1
    n_in, n_out, n_scr = len(in_specs), len(out_specs), len(scratch_shapes)
    r_in, r_out = (len(ride.operands), len(ride.out_shape)) if ride is not None else (0, 0)
    a, b = pre + n_in, pre + n_in + r_in
    c, d = b + n_out, b + n_out + r_out
    e = d + n_scr

    def riding(*refs):
        start, finish = ride.make(refs[a:b], refs[c:d], refs[e:])
        ids = [pl.program_id(k) for k in range(len(grid))]
        first, last = ids[0] == 0, ids[0] == grid[0] - 1
        for k in range(1, len(grid)):
            first, last = first & (ids[k] == 0), last & (ids[k] == grid[k] - 1)
        pl.when(first)(start)
        body(*refs[:a], *refs[b:c], *refs[d:e])
        pl.when(last)(finish)

    if ride is None:
        kernel_body, extra_in, extra_out, extra_shape, extra_scr, aliases = body, [], [], [], [], {}
        params = _params(*semantics)
    else:
        kernel_body, extra_in, extra_out = riding, [HBM_SPEC] * r_in, [HBM_SPEC] * r_out
        extra_shape, extra_scr = list(ride.out_shape), list(ride.sems)
        aliases = {pre + n_in + i: n_out + o for i, o in ride.aliases.items()}
        params = _params(*(("arbitrary",) * len(grid)))
    specs = dict(grid=grid, in_specs=list(in_specs) + extra_in, out_specs=list(out_specs) + extra_out,
                 scratch_shapes=list(scratch_shapes) + extra_scr)
    if prefetch is not None:
        specs = dict(grid_spec=pltpu.PrefetchScalarGridSpec(num_scalar_prefetch=1, **specs))
        args = (prefetch,) + tuple(args)
    outs = pl.pallas_call(kernel_body, name=name, out_shape=list(out_shape) + extra_shape,
                          input_output_aliases=aliases, compiler_params=params, **specs,
                          )(*args, *(ride.operands if ride is not None else ()))
    return outs if ride is None else (outs[:n_out], outs[n_out:])


def alone(ride, *, name):
    def body(*refs):
        n = len(ride.operands)
        start, finish = ride.make(refs[:n], refs[n:n + len(ride.out_shape)], refs[n + len(ride.out_shape):])
        start()
        finish()

    return pl.pallas_call(
        body, name=name, in_specs=[HBM_SPEC] * len(ride.operands), out_specs=[HBM_SPEC] * len(ride.out_shape),
        out_shape=list(ride.out_shape), input_output_aliases=dict(ride.aliases), scratch_shapes=list(ride.sems),
    )(*ride.operands)


def gather_ride(wholes, metas, small=None):
    n = len(wholes)
    operands, out_shape = list(wholes), [SDS(s.shape, s.dtype) for s in wholes]
    sems = [DMA((n, 3)), DMA((n, 3)), DMA((n, 3)), DMA((n, 3))]
    if small is not None:
        operands.append(small)
        out_shape.append(SDS((N_CHIPS,) + small.shape, small.dtype))
        sems += [DMA((3,)), DMA((3,)), DMA(())]

    def make(ins, outs, sem):
        send1, recv1, send2, recv2 = sem[:4]
        x, y, c = _place()
        p = 2 * x + y
        chips = _other_chips(x, y)
        me, sibling = (x, y, c), (x, y, 1 - c)
        part = lambda t, q, half: _half_of_quarter(outs[t], *metas[t], q, half)
        first = []
        for j, (qx, qy) in enumerate(chips):
            if small is not None:
                first.append(_remote(ins[n], outs[n].at[p], sem[4].at[j], sem[5].at[j], (qx, qy, c)))
            for t in range(n):
                first.append(_remote(part(t, p, c), part(t, p, c), send1.at[t, j], recv1.at[t, j], (qx, qy, c)))
        local = [] if small is None else [pltpu.make_async_copy(ins[n], outs[n].at[p], sem[6])]

        def start():
            for cp in local + first:
                cp.start()

        def finish():
            passed = []
            for j, (qx, qy) in enumerate(chips):
                q = 2 * qx + qy
                for t in range(n):
                    landed = part(t, q, c)
                    _remote(landed, landed, send1.at[t, j], recv1.at[t, j], me).wait_recv()
                    cp = _remote(landed, landed, send2.at[t, j], recv2.at[t, j], sibling)
                    cp.start()
                    passed.append(cp)
            for j, (qx, qy) in enumerate(chips):
                q = 2 * qx + qy
                if small is not None:
                    _remote(outs[n].at[q], outs[n].at[q], sem[4].at[j], sem[5].at[j], me).wait_recv()
                for t in range(n):
                    theirs = part(t, q, 1 - c)
                    _remote(theirs, theirs, send2.at[t, j], recv2.at[t, j], me).wait_recv()
            for cp in first + passed:
                cp.wait_send()
            for cp in local:
                cp.wait()

        return start, finish

    return Ride(operands, out_shape, {t: t for t in range(n)}, sems, make)


def chip_ride(sums, metas, small=None, earlier=None):
    n = len(sums)
    operands = list(sums)
    out_shape = [SDS((3, s.shape[1], quarter[1]), s.dtype) for s, (_, quarter, _) in zip(sums, metas)]
    sems = [DMA((n, 3)), DMA((n, 3))] if n else []
    if small is not None:
        operands.append(small)
        out_shape.append(SDS((8,) + small.shape, small.dtype))
        sems += [DMA((7,)), DMA((7,)), DMA(())]
    aliases = {}
    for t, buffer in enumerate(earlier or [None] * n):
        if buffer is not None:
            aliases[len(operands)] = t
            operands.append(buffer)

    def make(ins, outs, sem):
        x, y, c = _place()
        cps = []
        for j, (qx, qy) in enumerate(_other_chips(x, y)):
            q = 2 * qx + qy
            for t in range(n):
                kind, (_, ws), part = metas[t]
                rows = pl.ds(*_rows_of(ins[t].shape[1], part))
                if kind == "row":
                    src = ins[t].at[q, rows]
                elif kind == "col":
                    src = ins[t].at[0, rows, pl.ds(pl.multiple_of(q * ws, 128), ws)]
                else:
                    src = ins[t].at[q // 2, rows, pl.ds(pl.multiple_of((q % 2) * ws, 128), ws)]
                cps.append(_remote(src, outs[t].at[j, rows], sem[0].at[t, j], sem[1].at[t, j], (qx, qy, c)))
        local = []
        if small is not None:
            ssend, srecv, lsem = sem[2 * bool(n):2 * bool(n) + 3]
            local.append(pltpu.make_async_copy(ins[n], outs[n].at[0], lsem))
            for k in range(1, 8):
                peer = (x ^ (k >> 2 & 1), y ^ (k >> 1 & 1), c ^ (k & 1))
                cps.append(_remote(ins[n], outs[n].at[k], ssend.at[k - 1], srecv.at[k - 1], peer))

        def start():
            for cp in local + cps:
                cp.start()

        def finish():
            for cp in cps + local:
                cp.wait()

        return start, finish

    return Ride(operands, out_shape, aliases, sems, make)


def pair_ride(grads):
    n = len(grads)

    def make(ins, outs, sem):
        x, y, c = _place()
        cps = [_remote(ins[t].at[:, 1 - c], outs[t], sem[0].at[t], sem[1].at[t], (x, y, 1 - c)) for t in range(n)]

        def start():
            for cp in cps:
                cp.start()

        def finish():
            for cp in cps:
                cp.wait()

        return start, finish

    return Ride(list(grads), [SDS((g.shape[0],) + g.shape[2:], g.dtype) for g in grads], {}, [DMA((n,)), DMA((n,))],
                make)


def half_ride(quarters):
    n = len(quarters)

    def make(ins, outs, sem):
        x, y, c = _place()
        sends = [_remote(outs[t].at[c], outs[t].at[c], sem[0].at[t], sem[1].at[t], (x, y, 1 - c)) for t in range(n)]

        def start():
            for cp in sends:
                cp.start()

        def finish():
            for t in range(n):
                theirs = outs[t].at[1 - c]
                _remote(theirs, theirs, sem[0].at[t], sem[1].at[t], (x, y, c)).wait_recv()
            for cp in sends:
                cp.wait_send()

        return start, finish

    return Ride(list(quarters), [SDS(q.shape, q.dtype) for q in quarters], {t: t for t in range(n)},
                [DMA((n,)), DMA((n,))], make)


CAST_STEPS = 4


def cast_quarters(sources, p_arr, *, name, ride=None):
    n = len(sources)
    in_specs, out_specs, out_shape = [], [], []
    for w, layer, kind in sources:
        _, r, ws = w.shape
        tr = r // CAST_STEPS
        assert tr % BF16_ROWS == 0, w.shape
        in_specs.append(pl.BlockSpec((None, tr, ws), lambda i, p_ref, layer=layer: (layer, i, 0)))
        out_specs.append(pl.BlockSpec((tr, ws), (lambda i, p_ref: (p_ref[0] * CAST_STEPS + i, 0)) if kind == "row"
                                      else (lambda i, p_ref: (i, p_ref[0]))))
        out_shape.append(SDS(_full_shape(kind, (r, ws)), BF16))

    def body(p_ref, *refs):
        for w_ref, o_ref in zip(refs[:n], refs[n:]):
            o_ref[...] = w_ref[...].astype(BF16)

    return _call(body, name=name, grid=(CAST_STEPS,), in_specs=in_specs, out_specs=out_specs, out_shape=out_shape,
                 semantics=("parallel",), args=[w for w, _, _ in sources], ride=ride, prefetch=p_arr)


def pair_add(own, got, c_arr, *, name):
    A, _, h, W = own.shape
    th = _row_tile(h, max(BF16_ROWS, (3 << 19) // W), BF16_ROWS)

    def body(c_ref, a_ref, b_ref, o_ref):
        o_ref[...] = (a_ref[...].astype(F32) + b_ref[...].astype(F32)).astype(BF16)

    return pl.pallas_call(
        body, name=name,
        grid_spec=pltpu.PrefetchScalarGridSpec(
            num_scalar_prefetch=1, grid=(A, h // th),
            in_specs=[pl.BlockSpec((None, None, th, W), lambda q, i, c_ref: (q, c_ref[0], i, 0)),
                      pl.BlockSpec((None, th, W), lambda q, i, c_ref: (q, i, 0))],
            out_specs=pl.BlockSpec((None, th, W), lambda q, i, c_ref: (q, i, 0))),
        out_shape=SDS((A, h, W), BF16),
        compiler_params=_params("parallel", "parallel"),
    )(c_arr, own, got)


REDUCE_STEPS = 2


def chip_reduce(sums, got, kinds, pc_arr, *, name):
    n = len(sums)
    mine = {"row": lambda i, pc_ref: (pc_ref[0], i, 0), "col": lambda i, pc_ref: (0, i, pc_ref[0]),
            "split": lambda i, pc_ref: (pc_ref[0] // 2, i, pc_ref[0] % 2)}
    a_specs, b_specs, o_specs, out_shape = [], [], [], []
    for g, kind in zip(got, kinds):
        _, h, ws = g.shape
        th = h // REDUCE_STEPS
        assert th % BF16_ROWS == 0, g.shape
        a_specs.append(pl.BlockSpec((None, th, ws), mine[kind]))
        b_specs.append(pl.BlockSpec((3, th, ws), lambda i, pc_ref: (0, i, 0)))
        o_specs.append(pl.BlockSpec((None, th, ws), lambda i, pc_ref: (pc_ref[1], i, 0)))
        out_shape.append(SDS((2, h, ws), F32))

    def body(pc_ref, *refs):
        for a_ref, b_ref, o_ref in zip(refs[:n], refs[n:2 * n], refs[2 * n:]):
            o_ref[...] = ((a_ref[...].astype(F32) + b_ref[0].astype(F32)) + b_ref[1].astype(F32)) + b_ref[2].astype(F32)

    return _call(body, name=name, grid=(REDUCE_STEPS,), in_specs=a_specs + b_specs, out_specs=o_specs,
                 out_shape=out_shape, semantics=("parallel",), args=list(sums) + list(got), prefetch=pc_arr)


def small_reduce(blocks, me_arr):
    _, rows, D = blocks.shape

    def body(me_ref, b_ref, o_ref):
        me = me_ref[0]
        total = b_ref[me]
        for d in range(1, 8):
            total = total + b_ref[d ^ me]
        o_ref[...] = total

    return pl.pallas_call(
        body, name="small_reduce",
        grid_spec=pltpu.PrefetchScalarGridSpec(
            num_scalar_prefetch=1, grid=(1,),
            in_specs=[pl.BlockSpec((8, rows, D), lambda i, me_ref: (0, 0, 0))],
            out_specs=pl.BlockSpec((rows, D), lambda i, me_ref: (0, 0))),
        out_shape=SDS((rows, D), F32),
        compiler_params=_params("arbitrary"),
    )(me_arr, blocks)


def adamw(w, gs, m, v, *, name):
    L, r, cols = w.shape
    tr = _row_tile(r, 256)
    nt = r // tr

    def body(*refs):
        w_ref, m_ref, v_ref = refs[:3]
        g_refs = refs[3:3 + L]
        g_out, d_out, m_out, v_out = refs[3 + L:]
        layer = pl.program_id(0)
        g = g_refs[0][...]
        for l in range(1, L):
            g = jnp.where(layer == l, g_refs[l][...], g)
        m_new = ADAM_B1 * m_ref[...] + (1.0 - ADAM_B1) * g
        v_new = ADAM_B2 * v_ref[...] + (1.0 - ADAM_B2) * (g * g)
        m_hat = m_new / (1.0 - ADAM_B1 ** ADAM_STEP)
        v_hat = v_new / (1.0 - ADAM_B2 ** ADAM_STEP)
        g_out[...] = g
        m_out[...] = m_new
        v_out[...] = v_new
        d_out[...] = -ADAM_LR * (m_hat / (jnp.sqrt(v_hat) + ADAM_EPS) + ADAM_WD * w_ref[...])

    full = pl.BlockSpec((None, tr, cols), lambda l, i: (l, i, 0))
    g_spec = lambda l0: pl.BlockSpec((tr, cols), lambda l, i: (jnp.where(l == l0, i, jnp.where(l < l0, 0, nt - 1)), 0))
    return pl.pallas_call(
        body, name=name, grid=(L, nt),
        in_specs=[full, full, full] + [g_spec(l0) for l0 in range(L)],
        out_specs=[full] * 4,
        out_shape=[SDS(w.shape, F32)] * 4,
        compiler_params=_params("arbitrary", "arbitrary"),
    )(w, m, v, *gs)


def _rms_r(xf):
    return lax.rsqrt(jnp.mean(xf * xf, axis=-1, keepdims=True) + EPS)


def _rmsnorm_bwd(xf, g, dy):
    r = _rms_r(xf)
    xh = xf * r
    gd = g * dy
    return r * (gd - xh * jnp.mean(xh * gd, axis=-1, keepdims=True)), xh


def _dot(a, b):
    return jnp.dot(a, b, preferred_element_type=F32)


def _dot_nt(a, b):
    return lax.dot_general(a, b, (((1,), (1,)), ((), ())), preferred_element_type=F32)


def _dot_tn(a, b):
    return lax.dot_general(a, b, (((0,), (0,)), ((), ())), preferred_element_type=F32)


def _accumulate(ref, first, value):
    @pl.when(first)
    def _():
        ref[...] = value

    @pl.when(jnp.logical_not(first))
    def _():
        ref[...] += value


def norm_matmul(x, g, w, *, tn, split, name, ride=None, tm=ROW_TILE):
    T, D = x.shape
    N = w.shape[1]
    per = N // split // tn

    def body(x_ref, g_ref, w_ref, o_ref, xn_ref):
        @pl.when(pl.program_id(1) == 0)
        def _():
            xf = x_ref[...].astype(F32)
            xn_ref[...] = (xf * _rms_r(xf) * g_ref[...]).astype(BF16)

        o_ref[...] = _dot(xn_ref[...], w_ref[...]).astype(BF16)

    return _call(
        body, name=name, grid=(T // tm, N // tn),
        in_specs=[pl.BlockSpec((tm, D), lambda i, j: (i, 0)),
                  pl.BlockSpec((1, D), lambda i, j: (0, 0)),
                  pl.BlockSpec((D, tn), lambda i, j: (0, j))],
        out_specs=[pl.BlockSpec((None, tm, tn), lambda i, j: (j // per, i, j % per)),
                   pl.BlockSpec((tm, D), lambda i, j: (i, 0))],
        out_shape=[SDS((split, T, N // split), BF16), SDS((T, D), BF16)],
        semantics=("parallel", "arbitrary"), args=(x, g, w), ride=ride)


BIG_ROW_TILE = 1024


def norm2_matmul(x, gains, weights, *, name, tm=BIG_ROW_TILE):
    T, D = x.shape
    tm = min(tm, T)
    n = len(gains)

    def body(x_ref, *refs):
        xf = x_ref[...].astype(F32)
        xh = xf * _rms_r(xf)
        for g_ref, w_ref, o_ref, xn_ref in zip(refs[:n], refs[n:2 * n], refs[2 * n::2], refs[2 * n + 1::2]):
            xn = (xh * g_ref[...]).astype(BF16)
            xn_ref[...] = xn
            o_ref[...] = _dot(xn, w_ref[...]).astype(BF16)

    row = pl.BlockSpec((tm, D), lambda i: (i, 0))
    vec = pl.BlockSpec((1, D), lambda i: (0, 0))
    out_specs, out_shape = [], []
    for w in weights:
        out_specs += [pl.BlockSpec((tm, w.shape[1]), lambda i: (i, 0)), row]
        out_shape += [SDS((T, w.shape[1]), BF16), SDS((T, D), BF16)]
    return _call(
        body, name=name, grid=(T // tm,),
        in_specs=[row] + [vec] * n + [pl.BlockSpec(w.shape, lambda i: (0, 0)) for w in weights],
        out_specs=out_specs, out_shape=out_shape, semantics=("parallel",), args=[x] + list(gains) + list(weights))


def _shift_down(prev, cur, by):
    big = jnp.concatenate([prev, cur], axis=0)
    return pltpu.roll(big, by, 0)[prev.shape[0]:]


def _shift_up(cur, nxt, by):
    big = jnp.concatenate([cur, nxt], axis=0)
    return pltpu.roll(big, big.shape[0] - by, 0)[:cur.shape[0]]


def conv_mix_out(bcx, conv_w, w_out, g_post, res, *, name, ride=None, tm=ROW_TILE):
    T, D = res.shape
    hb = tm // BF16_ROWS

    def body(b_ref, c_ref, u_ref, cp_ref, up_ref, cw_ref, w_ref, g_ref, r_ref, h_ref, z_ref, y_ref):
        i = pl.program_id(0)
        cu = c_ref[...].astype(F32) * u_ref[...].astype(F32)
        cup = cp_ref[...].astype(F32) * up_ref[...].astype(F32)
        cup = jnp.where(i == 0, 0.0, cup)
        cv = (cw_ref[0:1, :] * _shift_down(cup, cu, 2) + cw_ref[1:2, :] * _shift_down(cup, cu, 1)
              + cw_ref[2:3, :] * cu)
        y = (b_ref[...].astype(F32) * cv).astype(BF16)
        y_ref[...] = y
        z = _dot(y, w_ref[...])
        z_ref[...] = z.astype(BF16)
        h_ref[...] = (r_ref[...] + z * _rms_r(z) * g_ref[...]).astype(STREAM)

    tile = lambda col: pl.BlockSpec((tm, D), lambda i: (i, col))
    halo = lambda col: pl.BlockSpec((BF16_ROWS, D), lambda i: (jnp.maximum(i * hb - 1, 0), col))
    row = pl.BlockSpec((tm, D), lambda i: (i, 0))
    return _call(
        body, name=name, grid=(T // tm,),
        in_specs=[tile(0), tile(1), tile(2), halo(1), halo(2),
                  pl.BlockSpec((3, D), lambda i: (0, 0)),
                  pl.BlockSpec((D, D), lambda i: (0, 0)),
                  pl.BlockSpec((1, D), lambda i: (0, 0)), row],
        out_specs=[row, row, row],
        out_shape=[SDS((T, D), STREAM), SDS((T, D), BF16), SDS((T, D), BF16)],
        semantics=("parallel",), args=(bcx, bcx, bcx, bcx, bcx, conv_w, w_out, g_post, res), ride=ride)


def _normbwd_then_nt(dh, zf, g_ref, w_ref, dz_ref, dg_ref, o_ref, first):
    dz, zh = _rmsnorm_bwd(zf, g_ref[...], dh)
    dz = dz.astype(BF16)
    dz_ref[...] = dz
    _accumulate(dg_ref, first, jnp.sum(dh * zh, axis=0, keepdims=True))
    o_ref[...] = _dot_nt(dz, w_ref[...]).astype(BF16)


def _then_specs(then, tm, T, D):
    z, g, w = then
    K = w.shape[0]
    row = pl.BlockSpec((tm, D), lambda i: (i, 0))
    vec = pl.BlockSpec((1, D), lambda i: (0, 0))
    in_specs = [row, vec, pl.BlockSpec((K, D), lambda i: (0, 0), pipeline_mode=pl.Buffered(1))]
    out_specs = [row, vec, pl.BlockSpec((tm, K), lambda i: (i, 0))]
    out_shape = [SDS((T, D), BF16), SDS((1, D), F32), SDS((T, K), BF16)]
    return in_specs, out_specs, out_shape


def plain_mix_out(a, w, g_post, res, *, name, target=None, ride=None, tm=ROW_TILE):
    T, D = res.shape
    tm = min(tm, T)
    K = a.shape[1]
    with_loss = target is not None

    def body(a_ref, w_ref, g_ref, r_ref, *rest):
        z = _dot(a_ref[...], w_ref[...])
        h = r_ref[...].astype(F32) + z * _rms_r(z) * g_ref[...]
        if with_loss:
            t_ref, h_ref, dz_ref, dg_ref, da_ref, loss_ref = rest
            first = pl.program_id(0) == 0
            diff = h - t_ref[...]
            dh = diff * (1.0 / D)
            h_ref[...] = dh.astype(STREAM)
            part = jnp.full(loss_ref.shape, 0.5 / D, F32) * jnp.sum(diff * diff)
            _accumulate(loss_ref, first, part)
            _normbwd_then_nt(dh, z, g_ref, w_ref, dz_ref, dg_ref, da_ref, first)
        else:
            h_ref, z_ref = rest
            h_ref[...] = h.astype(STREAM)
            z_ref[...] = z.astype(BF16)

    row = pl.BlockSpec((tm, D), lambda i: (i, 0))
    vec = pl.BlockSpec((1, D), lambda i: (0, 0))
    in_specs = [pl.BlockSpec((tm, K), lambda i: (i, 0)), pl.BlockSpec((K, D), lambda i: (0, 0)), vec, row]
    if with_loss:
        in_specs.append(row)
        out_specs = [row, row, vec, pl.BlockSpec((tm, K), lambda i: (i, 0)), pl.BlockSpec((8, 128), lambda i: (0, 0))]
        out_shape = [SDS((T, D), STREAM), SDS((T, D), BF16), SDS((1, D), F32), SDS((T, K), BF16), SDS((8, 128), F32)]
    else:
        out_specs, out_shape = [row, row], [SDS((T, D), STREAM), SDS((T, D), BF16)]
    return _call(
        body, name=name, grid=(T // tm,), in_specs=in_specs, out_specs=out_specs, out_shape=out_shape,
        semantics=("arbitrary",), args=(a, w, g_post, res) + ((target,) if with_loss else ()), ride=ride)


def _silu_grads(d, g, u):
    sg = jax.nn.sigmoid(g)
    return d * u * (sg * (1.0 + g * (1.0 - sg))), d * (g * sg)


def norm_swiglu_in(x, g, w, *, name, ride=None, tm=ROW_TILE // 2):
    T, D = x.shape
    F = w.shape[1] // 2

    def body(x_ref, g_ref, wg_ref, wu_ref, gu_ref, a_ref, xt_ref):
        xf = x_ref[...].astype(F32)
        xn = xf * _rms_r(xf) * g_ref[...]
        xt_ref[...] = xn.T.astype(BF16)
        xb = xn.astype(BF16)
        gate = _dot(xb, wg_ref[...]).astype(BF16)
        up = _dot(xb, wu_ref[...]).astype(BF16)
        gu_ref[0] = gate
        gu_ref[1] = up
        a_ref[...] = gate * jax.nn.sigmoid(gate) * up

    half = lambda s: pl.BlockSpec((D, F), lambda i: (0, s), pipeline_mode=pl.Buffered(1))
    return _call(
        body, name=name, grid=(T // tm,),
        in_specs=[pl.BlockSpec((tm, D), lambda i: (i, 0)), pl.BlockSpec((1, D), lambda i: (0, 0)), half(0), half(1)],
        out_specs=[pl.BlockSpec((2, tm, F), lambda i: (0, i, 0)), pl.BlockSpec((tm, F), lambda i: (i, 0)),
                   pl.BlockSpec((D, tm), lambda i: (0, i))],
        out_shape=[SDS((2, T, F), BF16), SDS((T, F), BF16), SDS((D, T), BF16)],
        semantics=("parallel",), args=(x, g, w, w), ride=ride)


def swiglu_bwd_tn(xt, dact, gu, *, name, ride=None, tb=MXU_WIDTH):
    D, T = xt.shape
    F = dact.shape[1]

    def body(xt_ref, d_ref, g_ref, u_ref, o_ref):
        dg, du = _silu_grads(d_ref[...], g_ref[...], u_ref[...])
        o_ref[0] = _dot(xt_ref[...], dg).astype(BF16)
        o_ref[1] = _dot(xt_ref[...], du).astype(BF16)

    col = lambda s: pl.BlockSpec((None, T, tb), lambda j: (s, 0, j))
    out = _call(
        body, name=name, grid=(F // tb,),
        in_specs=[pl.BlockSpec((D, T), lambda j: (0, 0), pipeline_mode=pl.Buffered(1)),
                  pl.BlockSpec((T, tb), lambda j: (0, j)), col(0), col(1)],
        out_specs=[pl.BlockSpec((2, D, tb), lambda j: (0, 0, j))],
        out_shape=[SDS((2, D, F), BF16)],
        semantics=("parallel",), args=(xt, dact, gu, gu), ride=ride)
    return out[0] if ride is None else (out[0][0], out[1])


def swiglu_bwd_in(dact, gu, w, h_in, g, dh_out, then, *, name, ride=None, tm=ROW_TILE // 2):
    T, D = h_in.shape
    F = dact.shape[1]

    def body(d_ref, gg_ref, uu_ref, wg_ref, wu_ref, h_ref, g_ref, dh_ref, z_ref, g2_ref, w2_ref,
             o_ref, dg_ref, dz_ref, dg2_ref, da_ref):
        first = pl.program_id(0) == 0
        dgate, dup = _silu_grads(d_ref[...], gg_ref[...], uu_ref[...])
        dn = _dot_nt(dgate, wg_ref[...]) + _dot_nt(dup, wu_ref[...])
        dx, hh = _rmsnorm_bwd(h_ref[...].astype(F32), g_ref[...], dn)
        dh_in = dh_ref[...] + dx
        o_ref[...] = dh_in.astype(STREAM)
        _accumulate(dg_ref, first, jnp.sum(dn * hh, axis=0, keepdims=True))
        _normbwd_then_nt(dh_in, z_ref[...].astype(F32), g2_ref, w2_ref, dz_ref, dg2_ref, da_ref, first)

    row = pl.BlockSpec((tm, D), lambda i: (i, 0))
    vec = pl.BlockSpec((1, D), lambda i: (0, 0))
    part = lambda s: pl.BlockSpec((None, tm, F), lambda i: (s, i, 0))
    half = lambda s: pl.BlockSpec((D, F), lambda i: (0, s), pipeline_mode=pl.Buffered(1))
    then_in, then_out, then_shape = _then_specs(then, tm, T, D)
    return _call(
        body, name=name, grid=(T // tm,),
        in_specs=[pl.BlockSpec((tm, F), lambda i: (i, 0)), part(0), part(1), half(0), half(1), row, vec, row] + then_in,
        out_specs=[row, vec] + then_out,
        out_shape=[SDS((T, D), STREAM), SDS((1, D), F32)] + then_shape,
        semantics=("arbitrary",), args=(dact, gu, gu, w, w, h_in, g, dh_out) + tuple(then), ride=ride)


def rope_tables(T):
    half = ROT_DIM // 2
    inv_freq = ROPE_THETA ** (-jnp.arange(0, ROT_DIM, 2, dtype=F32) / ROT_DIM)
    ang = (jnp.arange(T, dtype=F32)[:, None] * inv_freq[None, :]).T
    cos, sin = jnp.cos(ang), jnp.sin(ang)
    rest = HEAD_DIM - ROT_DIM
    one, zero = jnp.ones((rest, T), F32), jnp.zeros((rest, T), F32)
    zh = jnp.zeros((half, T), F32)
    fac = jnp.concatenate([cos, cos, one], axis=0)
    up = jnp.concatenate([-sin, zh, zero], axis=0)
    down = jnp.concatenate([zh, sin, zero], axis=0)
    return jnp.stack([fac, up, down])


def _rope(t, tab):
    half = ROT_DIM // 2
    return t * tab[0] + pltpu.roll(t, HEAD_DIM - half, 0) * tab[1] + pltpu.roll(t, half, 0) * tab[2]


def _rope_t(d, tab):
    half = ROT_DIM // 2
    return d * tab[0] + pltpu.roll(d * tab[1], half, 0) + pltpu.roll(d * tab[2], HEAD_DIM - half, 0)


def _head(t, h):
    return t[h * HEAD_DIM:(h + 1) * HEAD_DIM]


def _band(n, group):
    kj = lax.broadcasted_iota(jnp.int32, (2 * BLOCK, BLOCK), 0)
    qi = lax.broadcasted_iota(jnp.int32, (2 * BLOCK, BLOCK), 1)
    mask = (kj > qi) & (kj <= qi + BLOCK) & ((n > 0) | (kj >= BLOCK))
    return jnp.tile(mask, (1, group))


def _attn_specs(D, kvd):
    prev = lambda n: jnp.maximum(n - 1, 0)
    return [pl.BlockSpec((BLOCK, D), lambda n: (n, 0)),
            pl.BlockSpec((BLOCK, kvd), lambda n: (prev(n), 0)),
            pl.BlockSpec((BLOCK, kvd), lambda n: (n, 0)),
            pl.BlockSpec((BLOCK, kvd), lambda n: (prev(n), 1)),
            pl.BlockSpec((BLOCK, kvd), lambda n: (n, 1)),
            pl.BlockSpec((3, HEAD_DIM, BLOCK), lambda n: (0, 0, prev(n))),
            pl.BlockSpec((3, HEAD_DIM, BLOCK), lambda n: (0, 0, n)),
            pl.BlockSpec(memory_space=pltpu.SMEM)]


def _attn_operands(q_ref, kp_ref, k_ref, vp_ref, v_ref, tp_ref, t_ref):
    flip = lambda ref: ref[...].astype(F32).T
    tab = t_ref[...]
    kt = jnp.concatenate([flip(kp_ref), flip(k_ref)], axis=1)
    vt = jnp.concatenate([flip(vp_ref), flip(v_ref)], axis=1)
    return flip(q_ref), kt, vt, tab, jnp.concatenate([tp_ref[...], tab], axis=2)


SCORE_SCALE = 1.0 / math.sqrt(HEAD_DIM)
HEADS_TOGETHER = 4


def _group_heads(t, first, count, tab=None):
    heads = [_head(t, first + g) for g in range(count)]
    if tab is not None:
        heads = [_rope(h, tab) * SCORE_SCALE for h in heads]
    return jnp.concatenate(heads, axis=1).astype(BF16)


def _sink_row(s_ref, first, count):
    which = lax.broadcasted_iota(jnp.int32, (1, count * BLOCK), 1) // BLOCK
    row = jnp.zeros((1, count * BLOCK), F32)
    for g in range(count):
        row = jnp.where(which == g, s_ref[0, first + g], row)
    return row


def _softmax(scores, sink, mask):
    s = jnp.where(mask, scores, NEG)
    m = jnp.maximum(jnp.max(s, axis=0, keepdims=True), sink)
    e = jnp.exp(s - m)
    es = jnp.exp(sink - m)
    return e, es, 1.0 / (jnp.sum(e, axis=0, keepdims=True) + es)


def attention_fwd(q, kv, tabs, sinks, *, name, ride=None):
    T, D = q.shape
    kvd = kv.shape[1] // 2
    group = D // HEAD_DIM // N_KV_HEADS

    def body(q_ref, kp_ref, k_ref, vp_ref, v_ref, tp_ref, t_ref, s_ref, o_ref):
        gs = HEADS_TOGETHER
        mask = _band(pl.program_id(0), gs)
        qt, kt, vt, tab, tab2 = _attn_operands(q_ref, kp_ref, k_ref, vp_ref, v_ref, tp_ref, t_ref)
        firsts = [(j, first) for j in range(N_KV_HEADS) for first in range(j * group, (j + 1) * group, gs)]
        ks = [_rope(_head(kt, j), tab2).astype(BF16) for j in range(N_KV_HEADS)]
        scores = [_dot_tn(ks[j], _group_heads(qt, first, gs, tab)) for j, first in firsts]
        soft = [_softmax(s, _sink_row(s_ref, first, gs), mask) for s, (j, first) in zip(scores, firsts)]
        outs = []
        for (e, _, inv), (j, first) in zip(soft, firsts):
            o = _dot(_head(vt, j).astype(BF16), e.astype(BF16)) * inv
            outs += [o[:, g * BLOCK:(g + 1) * BLOCK] for g in range(gs)]
        o_ref[...] = jnp.concatenate(outs, axis=0).T.astype(BF16)

    return _call(
        body, name=name, grid=(T // BLOCK,),
        in_specs=_attn_specs(D, kvd),
        out_specs=[pl.BlockSpec((BLOCK, D), lambda n: (n, 0))],
        out_shape=[SDS((T, D), BF16)],
        semantics=("parallel",), args=(q, kv, kv, kv, kv, tabs, tabs, sinks), ride=ride)


def attention_bwd(q, kv, tabs, sinks, do, *, name, ride=None):
    T, D = q.shape
    kvd = kv.shape[1] // 2
    heads = D // HEAD_DIM
    group = heads // N_KV_HEADS

    def body(q_ref, kp_ref, k_ref, vp_ref, v_ref, tp_ref, t_ref, s_ref, do_ref, dq_ref, dc_ref, dp_ref, ds_ref):
        n = pl.program_id(0)
        gs = HEADS_TOGETHER
        mask = _band(n, gs)
        qt, kt, vt, tab, tab2 = _attn_operands(q_ref, kp_ref, k_ref, vp_ref, v_ref, tp_ref, t_ref)
        dot = do_ref[...].astype(F32).T
        lane = lax.broadcasted_iota(jnp.int32, (8, 128), 1)
        dsink = jnp.zeros((8, 128), F32)
        firsts = [(j, first) for j in range(N_KV_HEADS) for first in range(j * group, (j + 1) * group, gs)]
        ks = [_rope(_head(kt, j), tab2).astype(BF16) for j in range(N_KV_HEADS)]
        vs = [_head(vt, j).astype(BF16) for j in range(N_KV_HEADS)]
        qs = [_group_heads(qt, first, gs, tab) for _, first in firsts]
        dos = [_group_heads(dot, first, gs) for _, first in firsts]
        scores = [_dot_tn(ks[j], q) for q, (j, _) in zip(qs, firsts)]
        dps = [_dot_tn(vs[j], do) for do, (j, _) in zip(dos, firsts)]
        ps, dscs = [], []
        for s, dp, (j, first) in zip(scores, dps, firsts):
            e, e_sink, inv = _softmax(s, _sink_row(s_ref, first, gs), mask)
            p = e * inv
            dl = jnp.sum(p * dp, axis=0, keepdims=True)
            dscs.append((p * (dp - dl)).astype(BF16))
            ps.append(p.astype(BF16))
            weight = e_sink * inv * dl
            for g in range(gs):
                dsink = dsink - jnp.where(lane == first + g, jnp.sum(weight[:, g * BLOCK:(g + 1) * BLOCK]), 0.0)
        dqs = []
        dks = [jnp.zeros((HEAD_DIM, 2 * BLOCK), F32) for _ in range(N_KV_HEADS)]
        dvs = [jnp.zeros((HEAD_DIM, 2 * BLOCK), F32) for _ in range(N_KV_HEADS)]
        for p, dsc, q, do, (j, _) in zip(ps, dscs, qs, dos, firsts):
            dq = _dot(ks[j], dsc) * SCORE_SCALE
            dqs += [_rope_t(dq[:, g * BLOCK:(g + 1) * BLOCK], tab) for g in range(gs)]
            dks[j] = dks[j] + _dot_nt(q, dsc)
            dvs[j] = dvs[j] + _dot_nt(do, p)
        dks = [_rope_t(dk, tab2) for dk in dks]
        dq_ref[...] = jnp.concatenate(dqs, axis=0).T.astype(BF16)
        dkv = jnp.concatenate(dks + dvs, axis=0)
        dp_ref[...] = dkv[:, :BLOCK].T
        dc_ref[...] = dkv[:, BLOCK:].T
        _accumulate(ds_ref, n == 0, dsink)

    blk = lambda w: pl.BlockSpec((BLOCK, w), lambda n: (n, 0))
    return _call(
        body, name=name, grid=(T // BLOCK,),
        in_specs=_attn_specs(D, kvd) + [blk(D)],
        out_specs=[blk(D), blk(2 * kvd), blk(2 * kvd), pl.BlockSpec((8, 128), lambda n: (0, 0))],
        out_shape=[SDS((T, D), BF16), SDS((T, 2 * kvd), F32), SDS((T, 2 * kvd), F32), SDS((8, 128), F32)],
        semantics=("arbitrary",), args=(q, kv, kv, kv, kv, tabs, tabs, sinks, do), ride=ride)


def combine_dkv(d_cur, d_prev, *, name):
    T, W = d_cur.shape
    tm = ROW_TILE
    nt, per, last = T // tm, tm // BLOCK, T // BLOCK - 1

    def body(c_ref, p_ref, pn_ref, o_ref):
        nxt = jnp.where(pl.program_id(0) == nt - 1, 0.0, pn_ref[...])
        o_ref[...] = (c_ref[...] + jnp.concatenate([p_ref[BLOCK:, :], nxt], axis=0)).astype(BF16)

    return _call(
        body, name=name, grid=(nt,),
        in_specs=[pl.BlockSpec((tm, W), lambda i: (i, 0)), pl.BlockSpec((tm, W), lambda i: (i, 0)),
                  pl.BlockSpec((BLOCK, W), lambda i: (jnp.minimum((i + 1) * per, last), 0))],
        out_specs=[pl.BlockSpec((tm, W), lambda i: (i, 0))],
        out_shape=[SDS((T, W), BF16)],
        semantics=("parallel",), args=(d_cur, d_prev, d_prev))[0]


def matmul_nt_normbwd(da, w, h_in, g, dh_out, *, name, ride=None, tm=ROW_TILE):
    T, D = h_in.shape
    S, _, K = da.shape

    def body(*refs):
        da_refs, w_refs = refs[:S], refs[S:2 * S]
        h_ref, g_ref, dh_ref, o_ref, dg_ref = refs[2 * S:]
        dn = _dot_nt(da_refs[0][...], w_refs[0][...])
        for s in range(1, S):
            dn = dn + _dot_nt(da_refs[s][...], w_refs[s][...])
        dx, hh = _rmsnorm_bwd(h_ref[...].astype(F32), g_ref[...], dn)
        o_ref[...] = dh_ref[...] + dx
        _accumulate(dg_ref, pl.program_id(0) == 0, jnp.sum(dn * hh, axis=0, keepdims=True))

    row = pl.BlockSpec((tm, D), lambda i: (i, 0))
    vec = pl.BlockSpec((1, D), lambda i: (0, 0))
    part = lambda s: pl.BlockSpec((None, tm, K), lambda i: (s, i, 0))
    cols = lambda s: pl.BlockSpec((D, K), lambda i: (0, s), pipeline_mode=pl.Buffered(1))
    return _call(
        body, name=name, grid=(T // tm,),
        in_specs=[part(s) for s in range(S)] + [cols(s) for s in range(S)] + [row, vec, row],
        out_specs=[row, vec],
        out_shape=[SDS((T, D), F32), SDS((1, D), F32)],
        semantics=("arbitrary",), args=[da] * S + [w] * S + [h_in, g, dh_out], ride=ride)


def matmuls_nt_normbwd(das, ws, h_in, gs, dh_out, then, *, name, ride=None, tm=ROW_TILE):
    T, D = h_in.shape
    tm = min(tm, T)
    n = len(das)

    def body(*refs):
        da_refs, w_refs, g_refs = refs[:n], refs[n:2 * n], refs[2 * n:3 * n]
        h_ref, dh_ref, z_ref, g2_ref, w2_ref, o_ref = refs[3 * n:3 * n + 6]
        dg_refs, (dz_ref, dg2_ref, da_ref) = refs[3 * n + 6:4 * n + 6], refs[4 * n + 6:]
        first = pl.program_id(0) == 0
        hf = h_ref[...].astype(F32)
        r = _rms_r(hf)
        hh = hf * r
        total = dh_ref[...].astype(F32)
        for da_ref_, w_ref, g_ref, dg_ref in zip(da_refs, w_refs, g_refs, dg_refs):
            dn = _dot_nt(da_ref_[...], w_ref[...])
            gd = g_ref[...] * dn
            total = total + r * (gd - hh * jnp.mean(hh * gd, axis=-1, keepdims=True))
            _accumulate(dg_ref, first, jnp.sum(dn * hh, axis=0, keepdims=True))
        o_ref[...] = total.astype(STREAM)
        _normbwd_then_nt(total, z_ref[...].astype(F32), g2_ref, w2_ref, dz_ref, dg2_ref, da_ref, first)

    row = pl.BlockSpec((tm, D), lambda i: (i, 0))
    vec = pl.BlockSpec((1, D), lambda i: (0, 0))
    then_in, then_out, then_shape = _then_specs(then, tm, T, D)
    return _call(
        body, name=name, grid=(T // tm,),
        in_specs=[pl.BlockSpec((tm, da.shape[1]), lambda i: (i, 0)) for da in das]
        + [pl.BlockSpec(w.shape, lambda i: (0, 0)) for w in ws] + [vec] * n + [row, row] + then_in,
        out_specs=[row] + [vec] * n + then_out,
        out_shape=[SDS((T, D), STREAM)] + [SDS((1, D), F32)] * n + then_shape,
        semantics=("arbitrary",), args=list(das) + list(ws) + list(gs) + [h_in, dh_out] + list(then), ride=ride)


def matmul_tn(a, b, *, tb, name, ride=None, ta=MXU_WIDTH):
    T, Ka = a.shape
    S, _, Nb = b.shape
    per = Nb // tb

    def body(a_ref, b_ref, o_ref):
        o_ref[...] = _dot_tn(a_ref[...], b_ref[...]).astype(BF16)

    out = _call(
        body, name=name, grid=(S * per, Ka // ta),
        in_specs=[pl.BlockSpec((T, ta), lambda j, i: (0, i)),
                  pl.BlockSpec((None, T, tb), lambda j, i: (j // per, 0, j % per))],
        out_specs=[pl.BlockSpec((ta, tb), lambda j, i: (i, j))],
        out_shape=[SDS((Ka, S * Nb), BF16)],
        semantics=("parallel", "parallel"), args=(a, b), ride=ride)
    return out[0] if ride is None else (out[0][0], out[1])


def conv_bwd(dy, bcx, conv_w, *, name, ride=None, tm=ROW_TILE):
    T, D = dy.shape
    nt = T // tm
    hb = tm // BF16_ROWS
    last = T // BF16_ROWS - 1

    def body(dy_ref, dyn_ref, b_ref, bn_ref, c_ref, u_ref, cp_ref, up_ref, cw_ref, o_ref, dw_ref):
        i = pl.program_id(0)
        c, u = c_ref[...].astype(F32), u_ref[...].astype(F32)
        cu = c * u
        cup = jnp.where(i == 0, 0.0, cp_ref[...].astype(F32) * up_ref[...].astype(F32))
        cu1, cu2 = _shift_down(cup, cu, 1), _shift_down(cup, cu, 2)
        w0, w1, w2 = cw_ref[0:1, :], cw_ref[1:2, :], cw_ref[2:3, :]
        dyf = dy_ref[...].astype(F32)
        o_ref[:, 0:D] = (dyf * (w0 * cu2 + w1 * cu1 + w2 * cu)).astype(BF16)
        dcv = dyf * b_ref[...].astype(F32)
        dcvn = jnp.where(i == nt - 1, 0.0, dyn_ref[...].astype(F32) * bn_ref[...].astype(F32))
        dcu = w2 * dcv + w1 * _shift_up(dcv, dcvn, 1) + w0 * _shift_up(dcv, dcvn, 2)
        o_ref[:, D:2 * D] = (dcu * u).astype(BF16)
        o_ref[:, 2 * D:3 * D] = (dcu * c).astype(BF16)
        row = lax.broadcasted_iota(jnp.int32, (8, D), 0)
        dw = jnp.zeros((8, D), F32)
        for tap, t in enumerate((cu2, cu1, cu)):
            dw = jnp.where(row == tap, jnp.sum(dcv * t, axis=0, keepdims=True), dw)
        _accumulate(dw_ref, i == 0, dw)

    tile = lambda col: pl.BlockSpec((tm, D), lambda i: (i, col))
    prev = lambda col: pl.BlockSpec((BF16_ROWS, D), lambda i: (jnp.maximum(i * hb - 1, 0), col))
    nxt = lambda col: pl.BlockSpec((BF16_ROWS, D), lambda i: (jnp.minimum((i + 1) * hb, last), col))
    return _call(
        body, name=name, grid=(nt,),
        in_specs=[tile(0), nxt(0), tile(0), nxt(0), tile(1), tile(2), prev(1), prev(2),
                  pl.BlockSpec((3, D), lambda i: (0, 0))],
        out_specs=[pl.BlockSpec((tm, 3 * D), lambda i: (i, 0)), pl.BlockSpec((8, D), lambda i: (0, 0))],
        out_shape=[SDS((T, 3 * D), BF16), SDS((8, D), F32)],
        semantics=("arbitrary",), args=(dy, dy, bcx, bcx, bcx, bcx, bcx, bcx, conv_w), ride=ride)


class NoTraffic:
    def ride(self, kernel_name):
        return None

    def landed(self, kernel_name, results, wts):
        pass

    def grad(self, key, value):
        pass


def local_step(x, target, wts, vec, traffic):
    T, D = x.shape
    tabs = rope_tables(T)
    small = {}

    def run(builder, *args, name, **kw):
        ride = traffic.ride(name)
        if ride is None:
            return builder(*args, name=name, **kw)
        out, extra = builder(*args, name=name, ride=ride, **kw)
        traffic.landed(name, extra, wts)
        return out

    bcx, xn1 = run(norm_matmul, x, vec["a_pre"], wts["w_in"], tn=3 * D, split=1, name="a_in")
    bcx = bcx[0]
    h1, z0, y0 = run(conv_mix_out, bcx, vec["conv_w"], wts["w_out"], vec["a_post"], x, name="a_out")
    gu0, act0, xt2 = run(norm_swiglu_in, h1, vec["ffn_pre0"], wts["gu0"], name="ffn0_in")
    h2, z1 = run(plain_mix_out, act0, wts["wd0"], vec["ffn_post0"], h1, name="ffn0_out")
    kvp, xkv, qp, xq = norm2_matmul(h2, [vec["kv_norm"], vec["b_pre"]], [wts["w_kv"], wts["w_q"]], name="kvq_in")
    (attn,) = run(attention_fwd, qp, kvp, tabs, vec["sinks"], name="attn_fwd")
    h3, z2 = plain_mix_out(attn, wts["w_o"], vec["b_post"], h2, name="attn_out", tm=BIG_ROW_TILE)
    gu1, act1, xt3 = run(norm_swiglu_in, h3, vec["ffn_pre1"], wts["gu1"], name="ffn1_in")
    dy, dz3, small["ffn_post1"], dact1, loss = plain_mix_out(act1, wts["wd1"], vec["ffn_post1"], h3, name="ffn1_out",
                                                             target=target)

    def ffn_bwd(layer, dz, dact, gu, act, xt, h_in, dh, then, gu_first):
        tag = "ffn%d" % layer
        dwd = lambda: traffic.grad("wd%d" % layer, run(matmul_tn, act, dz[None], tb=D, name=tag + "_dwd"))
        dwgu = lambda: traffic.grad("gu%d" % layer, run(swiglu_bwd_tn, xt, dact, gu, name=tag + "_dwgu"))
        for step in ((dwgu, dwd) if gu_first else (dwd, dwgu)):
            step()
        dh_in, small["ffn_pre%d" % layer], dz_, dg_, da_ = run(
            swiglu_bwd_in, dact, gu, wts["gu%d" % layer], h_in, vec["ffn_pre%d" % layer], dh, then,
            name=tag + "_in_bwd")
        return dh_in, dz_, dg_, da_

    dh3, dz2, small["b_post"], dattn = ffn_bwd(1, dz3, dact1, gu1, act1, xt3, h3, dy,
                                               (z2, vec["b_post"], wts["w_o"]), gu_first=False)
    traffic.grad("w_o", matmul_tn(attn, dz2[None], tb=D, name="attn_dwo"))
    dq, dkv_cur, dkv_prev, small["sinks"] = run(attention_bwd, qp, kvp, tabs, vec["sinks"], dattn, name="attn_bwd")
    dkv = combine_dkv(dkv_cur, dkv_prev, name="attn_dkv")
    traffic.grad("w_q", matmul_tn(xq, dq[None], tb=D, name="attn_dwq"))
    traffic.grad("w_kv", matmul_tn(xkv, dkv[None], tb=dkv.shape[1], name="attn_dwkv"))
    dh2, small["b_pre"], small["kv_norm"], dz1, small["ffn_post0"], dact0 = run(
        matmuls_nt_normbwd, [dq, dkv], [wts["w_q"], wts["w_kv"]], h2, [vec["b_pre"], vec["kv_norm"]], dh3,
        (z1, vec["ffn_post0"], wts["wd0"]), name="qkv_in_bwd")
    dh1, dz0, small["a_post"], dyc = ffn_bwd(0, dz1, dact0, gu0, act0, xt2, h1, dh2,
                                             (z0, vec["a_post"], wts["w_out"]), gu_first=True)
    traffic.grad("w_out", matmul_tn(y0, dz0[None], tb=D, name="a_dwout"))
    dbcx, small["conv_w"] = run(conv_bwd, dyc, bcx, vec["conv_w"], name="a_conv_bwd")
    traffic.grad("w_in", matmul_tn(xn1, dbcx[None], tb=3 * D // 2, name="a_dwin"))
    dx, small["a_pre"] = run(matmul_nt_normbwd, dbcx[None], wts["w_in"], x, vec["a_pre"], dh1, name="a_in_bwd",
                             tm=ROW_TILE // 2)
    return loss, dx, small


SMALL_ROWS = 16
LOSS_ROW = 13

WHOLE = None
GATHER_PLAN = {"cast_rest": [("w_in", WHOLE)],
               "a_in": [("w_out", WHOLE), ("gu0", (0, 18))],
               "a_out": [("gu0", (18, 14))],
               "ffn0_in": [("wd0", WHOLE), ("w_kv", WHOLE), ("w_q", WHOLE), ("w_o", WHOLE)],
               "ffn0_out": [("gu1", (0, 16))],
               "attn_fwd": [("gu1", (16, 16))],
               "ffn1_in": [("wd1", WHOLE)]}
PAIR_PLAN = {"ffn1_dwgu": ["wd1"], "ffn1_in_bwd": ["gu1"], "attn_bwd": ["w_o"], "qkv_in_bwd": ["w_q", "w_kv"],
             "ffn0_dwd": ["gu0"], "ffn0_in_bwd": ["wd0"], "a_conv_bwd": ["w_out"]}
PAIR_ALONE = ["w_in"]
CHIP_PLAN = {"ffn1_in_bwd": [("wd1", WHOLE)], "attn_bwd": [("gu1", WHOLE)],
             "ffn0_dwgu": [("w_o", WHOLE), ("w_q", WHOLE), ("w_kv", WHOLE)],
             "ffn0_in_bwd": [("gu0", WHOLE)], "a_conv_bwd": [("wd0", WHOLE)],
             "a_in_bwd": [("w_out", WHOLE), ("w_in", WHOLE)]}
HALF_PLAN = {"a_in_bwd": ["gu0", "gu1", "wd0", "wd1", "w_kv", "w_q", "w_o"]}
GRAD_KIND = dict(KIND, gu0="split", gu1="split")


class Traffic:
    def __init__(self, wholes, quarter, c_arr, pc_arr):
        self.wholes, self.quarter, self.c_arr, self.pc_arr = wholes, quarter, c_arr, pc_arr
        self.views, self.sums, self.got = {}, {}, {}
        self.reduced = {}
        self.stages = {}

    def reduce(self, keys, name):
        return chip_reduce([self.sums[k] for k in keys], [self.got[k] for k in keys], [GRAD_KIND[k] for k in keys],
                           self.pc_arr, name=name)

    def ride(self, name, small=None):
        rides, stages = [], []
        if name in GATHER_PLAN:
            plan = GATHER_PLAN[name]
            rides.append(gather_ride([self.wholes[k] for k, _ in plan],
                                     [(KIND[k], self.quarter[k], part) for k, part in plan], small))
            stages.append(("gather", [k for k, _ in plan]))
        if name in CHIP_PLAN:
            plan = CHIP_PLAN[name]
            rides.append(chip_ride([self.sums[k] for k, _ in plan],
                                   [(GRAD_KIND[k], self.quarter[k], part) for k, part in plan],
                                   earlier=[self.got.get(k) for k, _ in plan]))
            stages.append(("chip", [k for k, _ in plan]))
        if name in PAIR_PLAN:
            keys = PAIR_PLAN[name]
            rides.append(pair_ride([self.views[k] for k in keys]))
            stages.append(("pair", keys))
        if name in HALF_PLAN:
            keys = HALF_PLAN[name]
            rides.append(half_ride(self.reduce(keys, "chip_reduce_early")))
            stages.append(("half", keys))
        self.stages[name] = stages
        return join(rides)

    def landed(self, name, results, wts):
        results = list(results)
        for stage, keys in self.stages[name]:
            mine, results = results[:len(keys)], results[len(keys):]
            if stage == "gather":
                for k, whole in zip(keys, mine):
                    self.wholes[k] = wts[k] = whole
            elif stage == "chip":
                self.got.update(zip(keys, mine))
            elif stage == "half":
                self.reduced.update(zip(keys, mine))
            else:
                for k, got in zip(keys, mine):
                    self.sums[k] = pair_add(self.views[k], got, self.c_arr, name="pair_add_" + k)

    def grad(self, key, value):
        r, ws = self.quarter[key]
        view = {"row": (N_CHIPS, 2, r // 2, ws), "col": (1, 2, r // 2, N_CHIPS * ws), "split": (2, 2, r // 2, 2 * ws)}
        self.views[key] = value.reshape(view[GRAD_KIND[key]])
        if key in PAIR_ALONE:
            (got,) = alone(pair_ride([self.views[key]]), name="pair_exchange_" + key)
            self.sums[key] = pair_add(self.views[key], got, self.c_arr, name="pair_add_" + key)


def kernel(x, a_pre_norm, a_w_in, a_conv_w, a_w_out, a_post_norm, ffn_pre_norm, ffn_w_gate_up, ffn_w_down, ffn_post_norm, kv_norm, w_kv, b_pre_norm, b_w_q, b_sinks, b_w_o, b_post_norm, loss_target, m_a_pre_norm, m_a_w_in, m_a_conv_w, m_a_w_out, m_a_post_norm, m_ffn_pre_norm, m_ffn_w_gate_up, m_ffn_w_down, m_ffn_post_norm, m_kv_norm, m_w_kv, m_b_pre_norm, m_b_w_q, m_b_sinks, m_b_w_o, m_b_post_norm, v_a_pre_norm, v_a_w_in, v_a_conv_w, v_a_w_out, v_a_post_norm, v_ffn_pre_norm, v_ffn_w_gate_up, v_ffn_w_down, v_ffn_post_norm, v_kv_norm, v_w_kv, v_b_pre_norm, v_b_w_q, v_b_sinks, v_b_w_o, v_b_post_norm):
    T, D = x.shape[1], x.shape[2]
    xi, yi, ci = _place()
    p = 2 * xi + yi
    p_arr = jnp.reshape(p, (1,)).astype(jnp.int32)
    c_arr = jnp.reshape(ci, (1,)).astype(jnp.int32)
    pc_arr = jnp.stack([p, ci]).astype(jnp.int32)
    me_arr = jnp.reshape(4 * xi + 2 * yi + ci, (1,)).astype(jnp.int32)
    qd = D // N_CHIPS

    big = {"w_in": (a_w_in, 0), "w_out": (a_w_out, 0), "gu0": (ffn_w_gate_up, 0), "gu1": (ffn_w_gate_up, 1),
           "wd0": (ffn_w_down, 0), "wd1": (ffn_w_down, 1), "w_kv": (w_kv[None], 0), "w_q": (b_w_q, 0),
           "w_o": (b_w_o, 0)}
    names = list(big)
    quarter = {k: w.shape[1:] for k, (w, _) in big.items()}
    source = lambda k: big[k] + (KIND[k],)
    traffic = Traffic(dict(zip(names[:1], cast_quarters([source(names[0])], p_arr, name="cast_first"))), quarter,
                      c_arr, pc_arr)
    small_shard = jnp.concatenate([a_pre_norm, a_post_norm, a_conv_w[0], jnp.zeros((3, qd), F32)], axis=0)
    wts = {}
    rest, (*landed, small_full) = cast_quarters([source(k) for k in names[1:]], p_arr, name="cast_rest",
                                                ride=traffic.ride("cast_rest", small_shard))
    traffic.wholes.update(zip(names[1:], rest))
    traffic.landed("cast_rest", landed, wts)
    rows = lambda k: jnp.transpose(small_full[:, k], (1, 0, 2)).reshape(-1, D)
    vec = {"a_pre": rows(slice(0, 1)), "a_post": rows(slice(1, 2)), "conv_w": rows(slice(2, 5)),
           "ffn_pre0": ffn_pre_norm[0:1], "ffn_pre1": ffn_pre_norm[1:2],
           "ffn_post0": ffn_post_norm[0:1], "ffn_post1": ffn_post_norm[1:2],
           "kv_norm": kv_norm[None], "b_pre": b_pre_norm, "b_post": b_post_norm, "sinks": b_sinks}

    loss, dx, small = local_step(x[0], loss_target[0], wts, vec, traffic)

    pad = lambda a: jnp.pad(a, ((0, 0), (0, D - a.shape[1])))
    small_block = jnp.concatenate(
        [small["a_pre"], small["a_post"], small["conv_w"][0:3], small["ffn_pre0"], small["ffn_pre1"],
         small["ffn_post0"], small["ffn_post1"], small["kv_norm"], small["b_pre"], small["b_post"],
         pad(small["sinks"][0:1]), pad(loss[0:1]), jnp.zeros((SMALL_ROWS - LOSS_ROW - 1, D), F32)], axis=0)
    late = [k for k in names if k not in traffic.reduced]
    *swapped, small_blocks = alone(join([half_ride(traffic.reduce(late, "chip_reduce_late")),
                                         chip_ride([], [], small_block)]), name="last_exchange")
    traffic.reduced.update(zip(late, swapped))
    grad = {k: traffic.reduced[k].reshape(quarter[k]) for k in names}
    small_sum = small_reduce(small_blocks, me_arr)

    out = {}
    out["a_w_in"] = adamw(a_w_in, [grad["w_in"]], m_a_w_in, v_a_w_in, name="adamw_a_w_in")
    out["a_w_out"] = adamw(a_w_out, [grad["w_out"]], m_a_w_out, v_a_w_out, name="adamw_a_w_out")
    out["ffn_w_gate_up"] = adamw(ffn_w_gate_up, [grad["gu0"], grad["gu1"]], m_ffn_w_gate_up, v_ffn_w_gate_up,
                                 name="adamw_ffn_w_gate_up")
    out["ffn_w_down"] = adamw(ffn_w_down, [grad["wd0"], grad["wd1"]], m_ffn_w_down, v_ffn_w_down,
                              name="adamw_ffn_w_down")
    out["w_kv"] = [o[0] for o in adamw(w_kv[None], [grad["w_kv"]], m_w_kv[None], v_w_kv[None], name="adamw_w_kv")]
    out["b_w_q"] = adamw(b_w_q, [grad["w_q"]], m_b_w_q, v_b_w_q, name="adamw_b_w_q")
    out["b_w_o"] = adamw(b_w_o, [grad["w_o"]], m_b_w_o, v_b_w_o, name="adamw_b_w_o")

    def pack(a_pre, a_post, conv, ffn_pre, ffn_post, kvn, b_pre, b_post, sinks):
        return jnp.concatenate([pad(a_pre), pad(a_post), pad(conv[0]), ffn_pre, ffn_post, kvn[None], b_pre, b_post,
                                pad(sinks), jnp.zeros((SMALL_ROWS - 13, D), F32)], axis=0)

    g_small = jnp.concatenate([pad(lax.dynamic_slice(small_sum, (0, p * qd), (5, qd))), small_sum[5:]], axis=0)
    w_small = pack(a_pre_norm, a_post_norm, a_conv_w, ffn_pre_norm, ffn_post_norm, kv_norm, b_pre_norm, b_post_norm,
                   b_sinks)
    m_small = pack(m_a_pre_norm, m_a_post_norm, m_a_conv_w, m_ffn_pre_norm, m_ffn_post_norm, m_kv_norm,
                   m_b_pre_norm, m_b_post_norm, m_b_sinks)
    v_small = pack(v_a_pre_norm, v_a_post_norm, v_a_conv_w, v_ffn_pre_norm, v_ffn_post_norm, v_kv_norm,
                   v_b_pre_norm, v_b_post_norm, v_b_sinks)
    packed = adamw(w_small[None], [g_small], m_small[None], v_small[None], name="adamw_small")
    ns = b_sinks.shape[1]
    unpack = lambda a: {"a_pre_norm": a[0:1, :qd], "a_post_norm": a[1:2, :qd], "a_conv_w": a[None, 2:5, :qd],
                        "ffn_pre_norm": a[5:7], "ffn_post_norm": a[7:9], "kv_norm": a[9], "b_pre_norm": a[10:11],
                        "b_post_norm": a[11:12], "b_sinks": a[12:13, :ns]}
    unpacked = [unpack(a[0]) for a in packed]
    for k in unpacked[0]:
        out[k] = [u[k] for u in unpacked]

    order = ["a_pre_norm", "a_w_in", "a_conv_w", "a_w_out", "a_post_norm", "ffn_pre_norm", "ffn_w_gate_up",
             "ffn_w_down", "ffn_post_norm", "kv_norm", "w_kv", "b_pre_norm", "b_w_q", "b_sinks", "b_w_o",
             "b_post_norm"]
    return (small_sum[LOSS_ROW, 0], dx[None], *[out[k][0] for k in order], *[out[k][1] for k in order],
            *[out[k][2] for k in order], *[out[k][3] for k in order])
```

```python
import math

import jax
import jax.numpy as jnp
from jax import lax
from jax.experimental import pallas as pl
from jax.experimental.pallas import tpu as pltpu

F32 = jnp.float32
BF16 = jnp.bfloat16
SDS = jax.ShapeDtypeStruct
MESH = pl.DeviceIdType.MESH
DMA = pltpu.SemaphoreType.DMA
HBM_SPEC = pl.BlockSpec(memory_space=pltpu.HBM)

EPS = 1e-6
NEG = -1e30
HEAD_DIM = 64
N_KV_HEADS = 4
BLOCK = 128
ROT_DIM = HEAD_DIM // 4
ROPE_THETA = 500000.0
N_CHIPS = 4

ADAM_LR = 0.001
ADAM_B1 = 0.9
ADAM_B2 = 0.999
ADAM_EPS = 1e-08
ADAM_WD = 0.01
ADAM_STEP = 10

VMEM_LIMIT_BYTES = 52 * 1024 * 1024
ROW_TILE = 512
BF16_ROWS = 16
STREAM = BF16
MXU_WIDTH = 256

KIND = {"w_in": "col", "gu0": "col", "gu1": "col", "w_out": "row", "wd0": "row", "wd1": "row", "w_kv": "row",
        "w_q": "row", "w_o": "row"}


def _params(*semantics):
    return pltpu.CompilerParams(dimension_semantics=semantics, vmem_limit_bytes=VMEM_LIMIT_BYTES)


def _row_tile(rows, limit, step=8):
    return max(t for t in range(step, limit + 1, step) if rows % t == 0)


def _place():
    return lax.axis_index("x"), lax.axis_index("y"), lax.axis_index("c")


def _other_chips(x, y):
    return [(1 - x, y), (x, 1 - y), (1 - x, 1 - y)]


def _remote(src, dst, send_sem, recv_sem, to):
    return pltpu.make_async_remote_copy(src_ref=src, dst_ref=dst, send_sem=send_sem, recv_sem=recv_sem,
                                        device_id=to, device_id_type=MESH)


def _full_shape(kind, quarter):
    r, ws = quarter
    return (N_CHIPS * r, ws) if kind == "row" else (r, N_CHIPS * ws)


def _rows_of(h, part):
    lo, n = (0, h) if part is None else (part[0] * BF16_ROWS, part[1] * BF16_ROWS)
    assert lo + n <= h, (h, part)
    return lo, n


def _half_of_quarter(ref, kind, quarter, part, q, half):
    r, ws = quarter
    h = r // 2
    lo, n = _rows_of(h, part)
    if kind == "row":
        return ref.at[pl.ds(pl.multiple_of(q * r + half * h + lo, BF16_ROWS), n)]
    return ref.at[pl.ds(pl.multiple_of(half * h + lo, BF16_ROWS), n), pl.ds(pl.multiple_of(q * ws, 128), ws)]


class Ride:
    def __init__(self, operands, out_shape, aliases, sems, make):
        self.operands, self.out_shape, self.aliases, self.sems, self.make = operands, out_shape, aliases, sems, make


def join(rides):
    rides = [r for r in rides if r is not None]
    if len(rides) < 2:
        return rides[0] if rides else None
    aliases, at = {}, [0, 0, 0]
    cuts = []
    for r in rides:
        aliases.update({at[0] + i: at[1] + o for i, o in r.aliases.items()})
        cuts.append(tuple(at))
        at = [at[0] + len(r.operands), at[1] + len(r.out_shape), at[2] + len(r.sems)]
    cuts.append(tuple(at))

    def make(ins, outs, sem):
        made = [r.make(ins[lo[0]:hi[0]], outs[lo[1]:hi[1]], sem[lo[2]:hi[2]]) for r, lo, hi in zip(rides, cuts, cuts[1:])]

        def start():
            for s, _ in made:
                s()

        def finish():
            for _, f in made:
                f()

        return start, finish

    return Ride(sum((list(r.operands) for r in rides), []), sum((list(r.out_shape) for r in rides), []), aliases,
                sum((list(r.sems) for r in rides), []), make)


def _call(body, *, name, grid, in_specs, out_specs, out_shape, args, scratch_shapes=(), semantics=None, ride=None,
          prefetch=None):
    pre = 0 if prefetch is None else 1
    n_in, n_out, n_scr = len(in_specs), len(out_specs), len(scratch_shapes)
    r_in, r_out = (len(ride.operands), len(ride.out_shape)) if ride is not None else (0, 0)
    a, b = pre + n_in, pre + n_in + r_in
    c, d = b + n_out, b + n_out + r_out
    e = d + n_scr

    def riding(*refs):
        start, finish = ride.make(refs[a:b], refs[c:d], refs[e:])
        ids = [pl.program_id(k) for k in range(len(grid))]
        first, last = ids[0] == 0, ids[0] == grid[0] - 1
        for k in range(1, len(grid)):
            first, last = first & (ids[k] == 0), last & (ids[k] == grid[k] - 1)
        pl.when(first)(start)
        body(*refs[:a], *refs[b:c], *refs[d:e])
        pl.when(last)(finish)

    if ride is None:
        kernel_body, extra_in, extra_out, extra_shape, extra_scr, aliases = body, [], [], [], [], {}
        params = _params(*semantics)
    else:
        kernel_body, extra_in, extra_out = riding, [HBM_SPEC] * r_in, [HBM_SPEC] * r_out
        extra_shape, extra_scr = list(ride.out_shape), list(ride.sems)
        aliases = {pre + n_in + i: n_out + o for i, o in ride.aliases.items()}
        params = _params(*(("arbitrary",) * len(grid)))
    specs = dict(grid=grid, in_specs=list(in_specs) + extra_in, out_specs=list(out_specs) + extra_out,
                 scratch_shapes=list(scratch_shapes) + extra_scr)
    if prefetch is not None:
        specs = dict(grid_spec=pltpu.PrefetchScalarGridSpec(num_scalar_prefetch=1, **specs))
        args = (prefetch,) + tuple(args)
    outs = pl.pallas_call(kernel_body, name=name, out_shape=list(out_shape) + extra_shape,
                          input_output_aliases=aliases, compiler_params=params, **specs,
                          )(*args, *(ride.operands if ride is not None else ()))
    return outs if ride is None else (outs[:n_out], outs[n_out:])


def alone(ride, *, name):
    def body(*refs):
        n = len(ride.operands)
        start, finish = ride.make(refs[:n], refs[n:n + len(ride.out_shape)], refs[n + len(ride.out_shape):])
        start()
        finish()

    return pl.pallas_call(
        body, name=name, in_specs=[HBM_SPEC] * len(ride.operands), out_specs=[HBM_SPEC] * len(ride.out_shape),
        out_shape=list(ride.out_shape), input_output_aliases=dict(ride.aliases), scratch_shapes=list(ride.sems),
    )(*ride.operands)


def gather_ride(wholes, metas, small=None):
    n = len(wholes)
    operands, out_shape = list(wholes), [SDS(s.shape, s.dtype) for s in wholes]
    sems = [DMA((n, 3)), DMA((n, 3)), DMA((n, 3)), DMA((n, 3))]
    if small is not None:
        operands.append(small)
        out_shape.append(SDS((N_CHIPS,) + small.shape, small.dtype))
        sems += [DMA((3,)), DMA((3,)), DMA(())]

    def make(ins, outs, sem):
        send1, recv1, send2, recv2 = sem[:4]
        x, y, c = _place()
        p = 2 * x + y
        chips = _other_chips(x, y)
        me, sibling = (x, y, c), (x, y, 1 - c)
        part = lambda t, q, half: _half_of_quarter(outs[t], *metas[t], q, half)
        first = []
        for j, (qx, qy) in enumerate(chips):
            if small is not None:
                first.append(_remote(ins[n], outs[n].at[p], sem[4].at[j], sem[5].at[j], (qx, qy, c)))
            for t in range(n):
                first.append(_remote(part(t, p, c), part(t, p, c), send1.at[t, j], recv1.at[t, j], (qx, qy, c)))
        local = [] if small is None else [pltpu.make_async_copy(ins[n], outs[n].at[p], sem[6])]

        def start():
            for cp in local + first:
                cp.start()

        def finish():
            passed = []
            for j, (qx, qy) in enumerate(chips):
                q = 2 * qx + qy
                for t in range(n):
                    landed = part(t, q, c)
                    _remote(landed, landed, send1.at[t, j], recv1.at[t, j], me).wait_recv()
                    cp = _remote(landed, landed, send2.at[t, j], recv2.at[t, j], sibling)
                    cp.start()
                    passed.append(cp)
            for j, (qx, qy) in enumerate(chips):
                q = 2 * qx + qy
                if small is not None:
                    _remote(outs[n].at[q], outs[n].at[q], sem[4].at[j], sem[5].at[j], me).wait_recv()
                for t in range(n):
                    theirs = part(t, q, 1 - c)
                    _remote(theirs, theirs, send2.at[t, j], recv2.at[t, j], me).wait_recv()
            for cp in first + passed:
                cp.wait_send()
            for cp in local:
                cp.wait()

        return start, finish

    return Ride(operands, out_shape, {t: t for t in range(n)}, sems, make)


def chip_ride(sums, metas, small=None, earlier=None):
    n = len(sums)
    operands = list(sums)
    out_shape = [SDS((3, s.shape[1], quarter[1]), s.dtype) for s, (_, quarter, _) in zip(sums, metas)]
    sems = [DMA((n, 3)), DMA((n, 3))] if n else []
    if small is not None:
        operands.append(small)
        out_shape.append(SDS((8,) + small.shape, small.dtype))
        sems += [DMA((7,)), DMA((7,)), DMA(())]
    aliases = {}
    for t, buffer in enumerate(earlier or [None] * n):
        if buffer is not None:
            aliases[len(operands)] = t
            operands.append(buffer)

    def make(ins, outs, sem):
        x, y, c = _place()
        cps = []
        for j, (qx, qy) in enumerate(_other_chips(x, y)):
            q = 2 * qx + qy
            for t in range(n):
                kind, (_, ws), part = metas[t]
                rows = pl.ds(*_rows_of(ins[t].shape[1], part))
                if kind == "row":
                    src = ins[t].at[q, rows]
                elif kind == "col":
                    src = ins[t].at[0, rows, pl.ds(pl.multiple_of(q * ws, 128), ws)]
                else:
                    src = ins[t].at[q // 2, rows, pl.ds(pl.multiple_of((q % 2) * ws, 128), ws)]
                cps.append(_remote(src, outs[t].at[j, rows], sem[0].at[t, j], sem[1].at[t, j], (qx, qy, c)))
        local = []
        if small is not None:
            ssend, srecv, lsem = sem[2 * bool(n):2 * bool(n) + 3]
            local.append(pltpu.make_async_copy(ins[n], outs[n].at[0], lsem))
            for k in range(1, 8):
                peer = (x ^ (k >> 2 & 1), y ^ (k >> 1 & 1), c ^ (k & 1))
                cps.append(_remote(ins[n], outs[n].at[k], ssend.at[k - 1], srecv.at[k - 1], peer))

        def start():
            for cp in local + cps:
                cp.start()

        def finish():
            for cp in cps + local:
                cp.wait()

        return start, finish

    return Ride(operands, out_shape, aliases, sems, make)


def pair_ride(grads):
    n = len(grads)

    def make(ins, outs, sem):
        x, y, c = _place()
        cps = [_remote(ins[t].at[:, 1 - c], outs[t], sem[0].at[t], sem[1].at[t], (x, y, 1 - c)) for t in range(n)]

        def start():
            for cp in cps:
                cp.start()

        def finish():
            for cp in cps:
                cp.wait()

        return start, finish

    return Ride(list(grads), [SDS((g.shape[0],) + g.shape[2:], g.dtype) for g in grads], {}, [DMA((n,)), DMA((n,))],
                make)


def half_ride(quarters):
    n = len(quarters)

    def make(ins, outs, sem):
        x, y, c = _place()
        sends = [_remote(outs[t].at[c], outs[t].at[c], sem[0].at[t], sem[1].at[t], (x, y, 1 - c)) for t in range(n)]

        def start():
            for cp in sends:
                cp.start()

        def finish():
            for t in range(n):
                theirs = outs[t].at[1 - c]
                _remote(theirs, theirs, sem[0].at[t], sem[1].at[t], (x, y, c)).wait_recv()
            for cp in sends:
                cp.wait_send()

        return start, finish

    return Ride(list(quarters), [SDS(q.shape, q.dtype) for q in quarters], {t: t for t in range(n)},
                [DMA((n,)), DMA((n,))], make)


CAST_STEPS = 4


def cast_quarters(sources, p_arr, *, name, ride=None):
    n = len(sources)
    in_specs, out_specs, out_shape = [], [], []
    for w, layer, kind in sources:
        _, r, ws = w.shape
        tr = r // CAST_STEPS
        assert tr % BF16_ROWS == 0, w.shape
        in_specs.append(pl.BlockSpec((None, tr, ws), lambda i, p_ref, layer=layer: (layer, i, 0)))
        out_specs.append(pl.BlockSpec((tr, ws), (lambda i, p_ref: (p_ref[0] * CAST_STEPS + i, 0)) if kind == "row"
                                      else (lambda i, p_ref: (i, p_ref[0]))))
        out_shape.append(SDS(_full_shape(kind, (r, ws)), BF16))

    def body(p_ref, *refs):
        for w_ref, o_ref in zip(refs[:n], refs[n:]):
            o_ref[...] = w_ref[...].astype(BF16)

    return _call(body, name=name, grid=(CAST_STEPS,), in_specs=in_specs, out_specs=out_specs, out_shape=out_shape,
                 semantics=("parallel",), args=[w for w, _, _ in sources], ride=ride, prefetch=p_arr)


def pair_add(own, got, c_arr, *, name):
    A, _, h, W = own.shape
    th = _row_tile(h, max(BF16_ROWS, (3 << 19) // W), BF16_ROWS)

    def body(c_ref, a_ref, b_ref, o_ref):
        o_ref[...] = (a_ref[...].astype(F32) + b_ref[...].astype(F32)).astype(BF16)

    return pl.pallas_call(
        body, name=name,
        grid_spec=pltpu.PrefetchScalarGridSpec(
            num_scalar_prefetch=1, grid=(A, h // th),
            in_specs=[pl.BlockSpec((None, None, th, W), lambda q, i, c_ref: (q, c_ref[0], i, 0)),
                      pl.BlockSpec((None, th, W), lambda q, i, c_ref: (q, i, 0))],
            out_specs=pl.BlockSpec((None, th, W), lambda q, i, c_ref: (q, i, 0))),
        out_shape=SDS((A, h, W), BF16),
        compiler_params=_params("parallel", "parallel"),
    )(c_arr, own, got)


REDUCE_STEPS = 2


def chip_reduce(sums, got, kinds, pc_arr, *, name):
    n = len(sums)
    mine = {"row": lambda i, pc_ref: (pc_ref[0], i, 0), "col": lambda i, pc_ref: (0, i, pc_ref[0]),
            "split": lambda i, pc_ref: (pc_ref[0] // 2, i, pc_ref[0] % 2)}
    a_specs, b_specs, o_specs, out_shape = [], [], [], []
    for g, kind in zip(got, kinds):
        _, h, ws = g.shape
        th = h // REDUCE_STEPS
        assert th % BF16_ROWS == 0, g.shape
        a_specs.append(pl.BlockSpec((None, th, ws), mine[kind]))
        b_specs.append(pl.BlockSpec((3, th, ws), lambda i, pc_ref: (0, i, 0)))
        o_specs.append(pl.BlockSpec((None, th, ws), lambda i, pc_ref: (pc_ref[1], i, 0)))
        out_shape.append(SDS((2, h, ws), F32))

    def body(pc_ref, *refs):
        for a_ref, b_ref, o_ref in zip(refs[:n], refs[n:2 * n], refs[2 * n:]):
            o_ref[...] = ((a_ref[...].astype(F32) + b_ref[0].astype(F32)) + b_ref[1].astype(F32)) + b_ref[2].astype(F32)

    return _call(body, name=name, grid=(REDUCE_STEPS,), in_specs=a_specs + b_specs, out_specs=o_specs,
                 out_shape=out_shape, semantics=("parallel",), args=list(sums) + list(got), prefetch=pc_arr)


def small_reduce(blocks, me_arr):
    _, rows, D = blocks.shape

    def body(me_ref, b_ref, o_ref):
        me = me_ref[0]
        total = b_ref[me]
        for d in range(1, 8):
            total = total + b_ref[d ^ me]
        o_ref[...] = total

    return pl.pallas_call(
        body, name="small_reduce",
        grid_spec=pltpu.PrefetchScalarGridSpec(
            num_scalar_prefetch=1, grid=(1,),
            in_specs=[pl.BlockSpec((8, rows, D), lambda i, me_ref: (0, 0, 0))],
            out_specs=pl.BlockSpec((rows, D), lambda i, me_ref: (0, 0))),
        out_shape=SDS((rows, D), F32),
        compiler_params=_params("arbitrary"),
    )(me_arr, blocks)


def adamw(w, gs, m, v, *, name):
    L, r, cols = w.shape
    tr = _row_tile(r, 256)
    nt = r // tr

    def body(*refs):
        w_ref, m_ref, v_ref = refs[:3]
        g_refs = refs[3:3 + L]
        g_out, d_out, m_out, v_out = refs[3 + L:]
        layer = pl.program_id(0)
        g = g_refs[0][...]
        for l in range(1, L):
            g = jnp.where(layer == l, g_refs[l][...], g)
        m_new = ADAM_B1 * m_ref[...] + (1.0 - ADAM_B1) * g
        v_new = ADAM_B2 * v_ref[...] + (1.0 - ADAM_B2) * (g * g)
        m_hat = m_new / (1.0 - ADAM_B1 ** ADAM_STEP)
        v_hat = v_new / (1.0 - ADAM_B2 ** ADAM_STEP)
        g_out[...] = g
        m_out[...] = m_new
        v_out[...] = v_new
        d_out[...] = -ADAM_LR * (m_hat / (jnp.sqrt(v_hat) + ADAM_EPS) + ADAM_WD * w_ref[...])

    full = pl.BlockSpec((None, tr, cols), lambda l, i: (l, i, 0))
    g_spec = lambda l0: pl.BlockSpec((tr, cols), lambda l, i: (jnp.where(l == l0, i, jnp.where(l < l0, 0, nt - 1)), 0))
    return pl.pallas_call(
        body, name=name, grid=(L, nt),
        in_specs=[full, full, full] + [g_spec(l0) for l0 in range(L)],
        out_specs=[full] * 4,
        out_shape=[SDS(w.shape, F32)] * 4,
        compiler_params=_params("arbitrary", "arbitrary"),
    )(w, m, v, *gs)


def _rms_r(xf):
    return lax.rsqrt(jnp.mean(xf * xf, axis=-1, keepdims=True) + EPS)


def _rmsnorm_bwd(xf, g, dy):
    r = _rms_r(xf)
    xh = xf * r
    gd = g * dy
    return r * (gd - xh * jnp.mean(xh * gd, axis=-1, keepdims=True)), xh


def _dot(a, b):
    return jnp.dot(a, b, preferred_element_type=F32)


def _dot_nt(a, b):
    return lax.dot_general(a, b, (((1,), (1,)), ((), ())), preferred_element_type=F32)


def _dot_tn(a, b):
    return lax.dot_general(a, b, (((0,), (0,)), ((), ())), preferred_element_type=F32)


def _accumulate(ref, first, value):
    @pl.when(first)
    def _():
        ref[...] = value

    @pl.when(jnp.logical_not(first))
    def _():
        ref[...] += value


def norm_matmul(x, g, w, *, tn, split, name, ride=None, tm=ROW_TILE):
    T, D = x.shape
    N = w.shape[1]
    per = N // split // tn

    def body(x_ref, g_ref, w_ref, o_ref, xn_ref):
        @pl.when(pl.program_id(1) == 0)
        def _():
            xf = x_ref[...].astype(F32)
            xn_ref[...] = (xf * _rms_r(xf) * g_ref[...]).astype(BF16)

        o_ref[...] = _dot(xn_ref[...], w_ref[...]).astype(BF16)

    return _call(
        body, name=name, grid=(T // tm, N // tn),
        in_specs=[pl.BlockSpec((tm, D), lambda i, j: (i, 0)),
                  pl.BlockSpec((1, D), lambda i, j: (0, 0)),
                  pl.BlockSpec((D, tn), lambda i, j: (0, j))],
        out_specs=[pl.BlockSpec((None, tm, tn), lambda i, j: (j // per, i, j % per)),
                   pl.BlockSpec((tm, D), lambda i, j: (i, 0))],
        out_shape=[SDS((split, T, N // split), BF16), SDS((T, D), BF16)],
        semantics=("parallel", "arbitrary"), args=(x, g, w), ride=ride)


BIG_ROW_TILE = 1024


def norm2_matmul(x, gains, weights, *, name, tm=BIG_ROW_TILE):
    T, D = x.shape
    tm = min(tm, T)
    n = len(gains)

    def body(x_ref, *refs):
        xf = x_ref[...].astype(F32)
        xh = xf * _rms_r(xf)
        for g_ref, w_ref, o_ref, xn_ref in zip(refs[:n], refs[n:2 * n], refs[2 * n::2], refs[2 * n + 1::2]):
            xn = (xh * g_ref[...]).astype(BF16)
            xn_ref[...] = xn
            o_ref[...] = _dot(xn, w_ref[...]).astype(BF16)

    row = pl.BlockSpec((tm, D), lambda i: (i, 0))
    vec = pl.BlockSpec((1, D), lambda i: (0, 0))
    out_specs, out_shape = [], []
    for w in weights:
        out_specs += [pl.BlockSpec((tm, w.shape[1]), lambda i: (i, 0)), row]
        out_shape += [SDS((T, w.shape[1]), BF16), SDS((T, D), BF16)]
    return _call(
        body, name=name, grid=(T // tm,),
        in_specs=[row] + [vec] * n + [pl.BlockSpec(w.shape, lambda i: (0, 0)) for w in weights],
        out_specs=out_specs, out_shape=out_shape, semantics=("parallel",), args=[x] + list(gains) + list(weights))


def _shift_down(prev, cur, by):
    big = jnp.concatenate([prev, cur], axis=0)
    return pltpu.roll(big, by, 0)[prev.shape[0]:]


def _shift_up(cur, nxt, by):
    big = jnp.concatenate([cur, nxt], axis=0)
    return pltpu.roll(big, big.shape[0] - by, 0)[:cur.shape[0]]


def conv_mix_out(bcx, conv_w, w_out, g_post, res, *, name, ride=None, tm=ROW_TILE):
    T, D = res.shape
    hb = tm // BF16_ROWS

    def body(b_ref, c_ref, u_ref, cp_ref, up_ref, cw_ref, w_ref, g_ref, r_ref, h_ref, z_ref, y_ref):
        i = pl.program_id(0)
        cu = c_ref[...].astype(F32) * u_ref[...].astype(F32)
        cup = cp_ref[...].astype(F32) * up_ref[...].astype(F32)
        cup = jnp.where(i == 0, 0.0, cup)
        cv = (cw_ref[0:1, :] * _shift_down(cup, cu, 2) + cw_ref[1:2, :] * _shift_down(cup, cu, 1)
              + cw_ref[2:3, :] * cu)
        y = (b_ref[...].astype(F32) * cv).astype(BF16)
        y_ref[...] = y
        z = _dot(y, w_ref[...])
        z_ref[...] = z.astype(BF16)
        h_ref[...] = (r_ref[...] + z * _rms_r(z) * g_ref[...]).astype(STREAM)

    tile = lambda col: pl.BlockSpec((tm, D), lambda i: (i, col))
    halo = lambda col: pl.BlockSpec((BF16_ROWS, D), lambda i: (jnp.maximum(i * hb - 1, 0), col))
    row = pl.BlockSpec((tm, D), lambda i: (i, 0))
    return _call(
        body, name=name, grid=(T // tm,),
        in_specs=[tile(0), tile(1), tile(2), halo(1), halo(2),
                  pl.BlockSpec((3, D), lambda i: (0, 0)),
                  pl.BlockSpec((D, D), lambda i: (0, 0)),
                  pl.BlockSpec((1, D), lambda i: (0, 0)), row],
        out_specs=[row, row, row],
        out_shape=[SDS((T, D), STREAM), SDS((T, D), BF16), SDS((T, D), BF16)],
        semantics=("parallel",), args=(bcx, bcx, bcx, bcx, bcx, conv_w, w_out, g_post, res), ride=ride)


def _normbwd_then_nt(dh, zf, g_ref, w_ref, dz_ref, dg_ref, o_ref, first):
    dz, zh = _rmsnorm_bwd(zf, g_ref[...], dh)
    dz = dz.astype(BF16)
    dz_ref[...] = dz
    _accumulate(dg_ref, first, jnp.sum(dh * zh, axis=0, keepdims=True))
    o_ref[...] = _dot_nt(dz, w_ref[...]).astype(BF16)


def _then_specs(then, tm, T, D):
    z, g, w = then
    K = w.shape[0]
    row = pl.BlockSpec((tm, D), lambda i: (i, 0))
    vec = pl.BlockSpec((1, D), lambda i: (0, 0))
    in_specs = [row, vec, pl.BlockSpec((K, D), lambda i: (0, 0), pipeline_mode=pl.Buffered(1))]
    out_specs = [row, vec, pl.BlockSpec((tm, K), lambda i: (i, 0))]
    out_shape = [SDS((T, D), BF16), SDS((1, D), F32), SDS((T, K), BF16)]
    return in_specs, out_specs, out_shape


def plain_mix_out(a, w, g_post, res, *, name, target=None, ride=None, tm=ROW_TILE):
    T, D = res.shape
    tm = min(tm, T)
    K = a.shape[1]
    with_loss = target is not None

    def body(a_ref, w_ref, g_ref, r_ref, *rest):
        z = _dot(a_ref[...], w_ref[...])
        h = r_ref[...].astype(F32) + z * _rms_r(z) * g_ref[...]
        if with_loss:
            t_ref, h_ref, dz_ref, dg_ref, da_ref, loss_ref = rest
            first = pl.program_id(0) == 0
            diff = h - t_ref[...]
            dh = diff * (1.0 / D)
            h_ref[...] = dh.astype(STREAM)
            part = jnp.full(loss_ref.shape, 0.5 / D, F32) * jnp.sum(diff * diff)
            _accumulate(loss_ref, first, part)
            _normbwd_then_nt(dh, z, g_ref, w_ref, dz_ref, dg_ref, da_ref, first)
        else:
            h_ref, z_ref = rest
            h_ref[...] = h.astype(STREAM)
            z_ref[...] = z.astype(BF16)

    row = pl.BlockSpec((tm, D), lambda i: (i, 0))
    vec = pl.BlockSpec((1, D), lambda i: (0, 0))
    in_specs = [pl.BlockSpec((tm, K), lambda i: (i, 0)), pl.BlockSpec((K, D), lambda i: (0, 0)), vec, row]
    if with_loss:
        in_specs.append(row)
        out_specs = [row, row, vec, pl.BlockSpec((tm, K), lambda i: (i, 0)), pl.BlockSpec((8, 128), lambda i: (0, 0))]
        out_shape = [SDS((T, D), STREAM), SDS((T, D), BF16), SDS((1, D), F32), SDS((T, K), BF16), SDS((8, 128), F32)]
    else:
        out_specs, out_shape = [row, row], [SDS((T, D), STREAM), SDS((T, D), BF16)]
    return _call(
        body, name=name, grid=(T // tm,), in_specs=in_specs, out_specs=out_specs, out_shape=out_shape,
        semantics=("arbitrary",), args=(a, w, g_post, res) + ((target,) if with_loss else ()), ride=ride)


def _silu_grads(d, g, u):
    sg = jax.nn.sigmoid(g)
    return d * u * (sg * (1.0 + g * (1.0 - sg))), d * (g * sg)


def norm_swiglu_in(x, g, w, *, name, ride=None, tm=ROW_TILE // 2):
    T, D = x.shape
    F = w.shape[1] // 2

    def body(x_ref, g_ref, wg_ref, wu_ref, gu_ref, a_ref, xt_ref):
        xf = x_ref[...].astype(F32)
        xn = xf * _rms_r(xf) * g_ref[...]
        xt_ref[...] = xn.T.astype(BF16)
        xb = xn.astype(BF16)
        gate = _dot(xb, wg_ref[...]).astype(BF16)
        up = _dot(xb, wu_ref[...]).astype(BF16)
        gu_ref[0] = gate
        gu_ref[1] = up
        a_ref[...] = gate * jax.nn.sigmoid(gate) * up

    half = lambda s: pl.BlockSpec((D, F), lambda i: (0, s), pipeline_mode=pl.Buffered(1))
    return _call(
        body, name=name, grid=(T // tm,),
        in_specs=[pl.BlockSpec((tm, D), lambda i: (i, 0)), pl.BlockSpec((1, D), lambda i: (0, 0)), half(0), half(1)],
        out_specs=[pl.BlockSpec((2, tm, F), lambda i: (0, i, 0)), pl.BlockSpec((tm, F), lambda i: (i, 0)),
                   pl.BlockSpec((D, tm), lambda i: (0, i))],
        out_shape=[SDS((2, T, F), BF16), SDS((T, F), BF16), SDS((D, T), BF16)],
        semantics=("parallel",), args=(x, g, w, w), ride=ride)


def swiglu_bwd_tn(xt, dact, gu, *, name, ride=None, tb=MXU_WIDTH):
    D, T = xt.shape
    F = dact.shape[1]

    def body(xt_ref, d_ref, g_ref, u_ref, o_ref):
        dg, du = _silu_grads(d_ref[...], g_ref[...], u_ref[...])
        o_ref[0] = _dot(xt_ref[...], dg).astype(BF16)
        o_ref[1] = _dot(xt_ref[...], du).astype(BF16)

    col = lambda s: pl.BlockSpec((None, T, tb), lambda j: (s, 0, j))
    out = _call(
        body, name=name, grid=(F // tb,),
        in_specs=[pl.BlockSpec((D, T), lambda j: (0, 0), pipeline_mode=pl.Buffered(1)),
                  pl.BlockSpec((T, tb), lambda j: (0, j)), col(0), col(1)],
        out_specs=[pl.BlockSpec((2, D, tb), lambda j: (0, 0, j))],
        out_shape=[SDS((2, D, F), BF16)],
        semantics=("parallel",), args=(xt, dact, gu, gu), ride=ride)
    return out[0] if ride is None else (out[0][0], out[1])


def swiglu_bwd_in(dact, gu, w, h_in, g, dh_out, then, *, name, ride=None, tm=ROW_TILE // 2):
    T, D = h_in.shape
    F = dact.shape[1]

    def body(d_ref, gg_ref, uu_ref, wg_ref, wu_ref, h_ref, g_ref, dh_ref, z_ref, g2_ref, w2_ref,
             o_ref, dg_ref, dz_ref, dg2_ref, da_ref):
        first = pl.program_id(0) == 0
        dgate, dup = _silu_grads(d_ref[...], gg_ref[...], uu_ref[...])
        dn = _dot_nt(dgate, wg_ref[...]) + _dot_nt(dup, wu_ref[...])
        dx, hh = _rmsnorm_bwd(h_ref[...].astype(F32), g_ref[...], dn)
        dh_in = dh_ref[...] + dx
        o_ref[...] = dh_in.astype(STREAM)
        _accumulate(dg_ref, first, jnp.sum(dn * hh, axis=0, keepdims=True))
        _normbwd_then_nt(dh_in, z_ref[...].astype(F32), g2_ref, w2_ref, dz_ref, dg2_ref, da_ref, first)

    row = pl.BlockSpec((tm, D), lambda i: (i, 0))
    vec = pl.BlockSpec((1, D), lambda i: (0, 0))
    part = lambda s: pl.BlockSpec((None, tm, F), lambda i: (s, i, 0))
    half = lambda s: pl.BlockSpec((D, F), lambda i: (0, s), pipeline_mode=pl.Buffered(1))
    then_in, then_out, then_shape = _then_specs(then, tm, T, D)
    return _call(
        body, name=name, grid=(T // tm,),
        in_specs=[pl.BlockSpec((tm, F), lambda i: (i, 0)), part(0), part(1), half(0), half(1), row, vec, row] + then_in,
        out_specs=[row, vec] + then_out,
        out_shape=[SDS((T, D), STREAM), SDS((1, D), F32)] + then_shape,
        semantics=("arbitrary",), args=(dact, gu, gu, w, w, h_in, g, dh_out) + tuple(then), ride=ride)


def rope_tables(T):
    half = ROT_DIM // 2
    inv_freq = ROPE_THETA ** (-jnp.arange(0, ROT_DIM, 2, dtype=F32) / ROT_DIM)
    ang = (jnp.arange(T, dtype=F32)[:, None] * inv_freq[None, :]).T
    cos, sin = jnp.cos(ang), jnp.sin(ang)
    rest = HEAD_DIM - ROT_DIM
    one, zero = jnp.ones((rest, T), F32), jnp.zeros((rest, T), F32)
    zh = jnp.zeros((half, T), F32)
    fac = jnp.concatenate([cos, cos, one], axis=0)
    up = jnp.concatenate([-sin, zh, zero], axis=0)
    down = jnp.concatenate([zh, sin, zero], axis=0)
    return jnp.stack([fac, up, down])


def _rope(t, tab):
    half = ROT_DIM // 2
    return t * tab[0] + pltpu.roll(t, HEAD_DIM - half, 0) * tab[1] + pltpu.roll(t, half, 0) * tab[2]


def _rope_t(d, tab):
    half = ROT_DIM // 2
    return d * tab[0] + pltpu.roll(d * tab[1], half, 0) + pltpu.roll(d * tab[2], HEAD_DIM - half, 0)


def _head(t, h):
    return t[h * HEAD_DIM:(h + 1) * HEAD_DIM]


def _band(n, group):
    kj = lax.broadcasted_iota(jnp.int32, (2 * BLOCK, BLOCK), 0)
    qi = lax.broadcasted_iota(jnp.int32, (2 * BLOCK, BLOCK), 1)
    mask = (kj > qi) & (kj <= qi + BLOCK) & ((n > 0) | (kj >= BLOCK))
    return jnp.tile(mask, (1, group))


def _attn_specs(D, kvd):
    prev = lambda n: jnp.maximum(n - 1, 0)
    return [pl.BlockSpec((BLOCK, D), lambda n: (n, 0)),
            pl.BlockSpec((BLOCK, kvd), lambda n: (prev(n), 0)),
            pl.BlockSpec((BLOCK, kvd), lambda n: (n, 0)),
            pl.BlockSpec((BLOCK, kvd), lambda n: (prev(n), 1)),
            pl.BlockSpec((BLOCK, kvd), lambda n: (n, 1)),
            pl.BlockSpec((3, HEAD_DIM, BLOCK), lambda n: (0, 0, prev(n))),
            pl.BlockSpec((3, HEAD_DIM, BLOCK), lambda n: (0, 0, n)),
            pl.BlockSpec(memory_space=pltpu.SMEM)]


def _attn_operands(q_ref, kp_ref, k_ref, vp_ref, v_ref, tp_ref, t_ref):
    flip = lambda ref: ref[...].astype(F32).T
    tab = t_ref[...]
    kt = jnp.concatenate([flip(kp_ref), flip(k_ref)], axis=1)
    vt = jnp.concatenate([flip(vp_ref), flip(v_ref)], axis=1)
    return flip(q_ref), kt, vt, tab, jnp.concatenate([tp_ref[...], tab], axis=2)


SCORE_SCALE = 1.0 / math.sqrt(HEAD_DIM)
HEADS_TOGETHER = 4


def _group_heads(t, first, count, tab=None):
    heads = [_head(t, first + g) for g in range(count)]
    if tab is not None:
        heads = [_rope(h, tab) * SCORE_SCALE for h in heads]
    return jnp.concatenate(heads, axis=1).astype(BF16)


def _sink_row(s_ref, first, count):
    which = lax.broadcasted_iota(jnp.int32, (1, count * BLOCK), 1) // BLOCK
    row = jnp.zeros((1, count * BLOCK), F32)
    for g in range(count):
        row = jnp.where(which == g, s_ref[0, first + g], row)
    return row


def _softmax(scores, sink, mask):
    s = jnp.where(mask, scores, NEG)
    m = jnp.maximum(jnp.max(s, axis=0, keepdims=True), sink)
    e = jnp.exp(s - m)
    es = jnp.exp(sink - m)
    return e, es, 1.0 / (jnp.sum(e, axis=0, keepdims=True) + es)


def attention_fwd(q, kv, tabs, sinks, *, name, ride=None):
    T, D = q.shape
    kvd = kv.shape[1] // 2
    group = D // HEAD_DIM // N_KV_HEADS

    def body(q_ref, kp_ref, k_ref, vp_ref, v_ref, tp_ref, t_ref, s_ref, o_ref):
        gs = HEADS_TOGETHER
        mask = _band(pl.program_id(0), gs)
        qt, kt, vt, tab, tab2 = _attn_operands(q_ref, kp_ref, k_ref, vp_ref, v_ref, tp_ref, t_ref)
        firsts = [(j, first) for j in range(N_KV_HEADS) for first in range(j * group, (j + 1) * group, gs)]
        ks = [_rope(_head(kt, j), tab2).astype(BF16) for j in range(N_KV_HEADS)]
        scores = [_dot_tn(ks[j], _group_heads(qt, first, gs, tab)) for j, first in firsts]
        soft = [_softmax(s, _sink_row(s_ref, first, gs), mask) for s, (j, first) in zip(scores, firsts)]
        outs = []
        for (e, _, inv), (j, first) in zip(soft, firsts):
            o = _dot(_head(vt, j).astype(BF16), e.astype(BF16)) * inv
            outs += [o[:, g * BLOCK:(g + 1) * BLOCK] for g in range(gs)]
        o_ref[...] = jnp.concatenate(outs, axis=0).T.astype(BF16)

    return _call(
        body, name=name, grid=(T // BLOCK,),
        in_specs=_attn_specs(D, kvd),
        out_specs=[pl.BlockSpec((BLOCK, D), lambda n: (n, 0))],
        out_shape=[SDS((T, D), BF16)],
        semantics=("parallel",), args=(q, kv, kv, kv, kv, tabs, tabs, sinks), ride=ride)


def attention_bwd(q, kv, tabs, sinks, do, *, name, ride=None):
    T, D = q.shape
    kvd = kv.shape[1] // 2
    heads = D // HEAD_DIM
    group = heads // N_KV_HEADS

    def body(q_ref, kp_ref, k_ref, vp_ref, v_ref, tp_ref, t_ref, s_ref, do_ref, dq_ref, dc_ref, dp_ref, ds_ref):
        n = pl.program_id(0)
        gs = HEADS_TOGETHER
        mask = _band(n, gs)
        qt, kt, vt, tab, tab2 = _attn_operands(q_ref, kp_ref, k_ref, vp_ref, v_ref, tp_ref, t_ref)
        dot = do_ref[...].astype(F32).T
        lane = lax.broadcasted_iota(jnp.int32, (8, 128), 1)
        dsink = jnp.zeros((8, 128), F32)
        firsts = [(j, first) for j in range(N_KV_HEADS) for first in range(j * group, (j + 1) * group, gs)]
        ks = [_rope(_head(kt, j), tab2).astype(BF16) for j in range(N_KV_HEADS)]
        vs = [_head(vt, j).astype(BF16) for j in range(N_KV_HEADS)]
        qs = [_group_heads(qt, first, gs, tab) for _, first in firsts]
        dos = [_group_heads(dot, first, gs) for _, first in firsts]
        scores = [_dot_tn(ks[j], q) for q, (j, _) in zip(qs, firsts)]
        dps = [_dot_tn(vs[j], do) for do, (j, _) in zip(dos, firsts)]
        ps, dscs = [], []
        for s, dp, (j, first) in zip(scores, dps, firsts):
            e, e_sink, inv = _softmax(s, _sink_row(s_ref, first, gs), mask)
            p = e * inv
            dl = jnp.sum(p * dp, axis=0, keepdims=True)
            dscs.append((p * (dp - dl)).astype(BF16))
            ps.append(p.astype(BF16))
            weight = e_sink * inv * dl
            for g in range(gs):
                dsink = dsink - jnp.where(lane == first + g, jnp.sum(weight[:, g * BLOCK:(g + 1) * BLOCK]), 0.0)
        dqs = []
        dks = [jnp.zeros((HEAD_DIM, 2 * BLOCK), F32) for _ in range(N_KV_HEADS)]
        dvs = [jnp.zeros((HEAD_DIM, 2 * BLOCK), F32) for _ in range(N_KV_HEADS)]
        for p, dsc, q, do, (j, _) in zip(ps, dscs, qs, dos, firsts):
            dq = _dot(ks[j], dsc) * SCORE_SCALE
            dqs += [_rope_t(dq[:, g * BLOCK:(g + 1) * BLOCK], tab) for g in range(gs)]
            dks[j] = dks[j] + _dot_nt(q, dsc)
            dvs[j] = dvs[j] + _dot_nt(do, p)
        dks = [_rope_t(dk, tab2) for dk in dks]
        dq_ref[...] = jnp.concatenate(dqs, axis=0).T.astype(BF16)
        dkv = jnp.concatenate(dks + dvs, axis=0)
        dp_ref[...] = dkv[:, :BLOCK].T
        dc_ref[...] = dkv[:, BLOCK:].T
        _accumulate(ds_ref, n == 0, dsink)

    blk = lambda w: pl.BlockSpec((BLOCK, w), lambda n: (n, 0))
    return _call(
        body, name=name, grid=(T // BLOCK,),
        in_specs=_attn_specs(D, kvd) + [blk(D)],
        out_specs=[blk(D), blk(2 * kvd), blk(2 * kvd), pl.BlockSpec((8, 128), lambda n: (0, 0))],
        out_shape=[SDS((T, D), BF16), SDS((T, 2 * kvd), F32), SDS((T, 2 * kvd), F32), SDS((8, 128), F32)],
        semantics=("arbitrary",), args=(q, kv, kv, kv, kv, tabs, tabs, sinks, do), ride=ride)


def combine_dkv(d_cur, d_prev, *, name):
    T, W = d_cur.shape
    tm = ROW_TILE
    nt, per, last = T // tm, tm // BLOCK, T // BLOCK - 1

    def body(c_ref, p_ref, pn_ref, o_ref):
        nxt = jnp.where(pl.program_id(0) == nt - 1, 0.0, pn_ref[...])
        o_ref[...] = (c_ref[...] + jnp.concatenate([p_ref[BLOCK:, :], nxt], axis=0)).astype(BF16)

    return _call(
        body, name=name, grid=(nt,),
        in_specs=[pl.BlockSpec((tm, W), lambda i: (i, 0)), pl.BlockSpec((tm, W), lambda i: (i, 0)),
                  pl.BlockSpec((BLOCK, W), lambda i: (jnp.minimum((i + 1) * per, last), 0))],
        out_specs=[pl.BlockSpec((tm, W), lambda i: (i, 0))],
        out_shape=[SDS((T, W), BF16)],
        semantics=("parallel",), args=(d_cur, d_prev, d_prev))[0]


def matmul_nt_normbwd(da, w, h_in, g, dh_out, *, name, ride=None, tm=ROW_TILE):
    T, D = h_in.shape
    S, _, K = da.shape

    def body(*refs):
        da_refs, w_refs = refs[:S], refs[S:2 * S]
        h_ref, g_ref, dh_ref, o_ref, dg_ref = refs[2 * S:]
        dn = _dot_nt(da_refs[0][...], w_refs[0][...])
        for s in range(1, S):
            dn = dn + _dot_nt(da_refs[s][...], w_refs[s][...])
        dx, hh = _rmsnorm_bwd(h_ref[...].astype(F32), g_ref[...], dn)
        o_ref[...] = dh_ref[...] + dx
        _accumulate(dg_ref, pl.program_id(0) == 0, jnp.sum(dn * hh, axis=0, keepdims=True))

    row = pl.BlockSpec((tm, D), lambda i: (i, 0))
    vec = pl.BlockSpec((1, D), lambda i: (0, 0))
    part = lambda s: pl.BlockSpec((None, tm, K), lambda i: (s, i, 0))
    cols = lambda s: pl.BlockSpec((D, K), lambda i: (0, s), pipeline_mode=pl.Buffered(1))
    return _call(
        body, name=name, grid=(T // tm,),
        in_specs=[part(s) for s in range(S)] + [cols(s) for s in range(S)] + [row, vec, row],
        out_specs=[row, vec],
        out_shape=[SDS((T, D), F32), SDS((1, D), F32)],
        semantics=("arbitrary",), args=[da] * S + [w] * S + [h_in, g, dh_out], ride=ride)


def matmuls_nt_normbwd(das, ws, h_in, gs, dh_out, then, *, name, ride=None, tm=ROW_TILE):
    T, D = h_in.shape
    tm = min(tm, T)
    n = len(das)

    def body(*refs):
        da_refs, w_refs, g_refs = refs[:n], refs[n:2 * n], refs[2 * n:3 * n]
        h_ref, dh_ref, z_ref, g2_ref, w2_ref, o_ref = refs[3 * n:3 * n + 6]
        dg_refs, (dz_ref, dg2_ref, da_ref) = refs[3 * n + 6:4 * n + 6], refs[4 * n + 6:]
        first = pl.program_id(0) == 0
        hf = h_ref[...].astype(F32)
        r = _rms_r(hf)
        hh = hf * r
        total = dh_ref[...].astype(F32)
        for da_ref_, w_ref, g_ref, dg_ref in zip(da_refs, w_refs, g_refs, dg_refs):
            dn = _dot_nt(da_ref_[...], w_ref[...])
            gd = g_ref[...] * dn
            total = total + r * (gd - hh * jnp.mean(hh * gd, axis=-1, keepdims=True))
            _accumulate(dg_ref, first, jnp.sum(dn * hh, axis=0, keepdims=True))
        o_ref[...] = total.astype(STREAM)
        _normbwd_then_nt(total, z_ref[...].astype(F32), g2_ref, w2_ref, dz_ref, dg2_ref, da_ref, first)

    row = pl.BlockSpec((tm, D), lambda i: (i, 0))
    vec = pl.BlockSpec((1, D), lambda i: (0, 0))
    then_in, then_out, then_shape = _then_specs(then, tm, T, D)
    return _call(
        body, name=name, grid=(T // tm,),
        in_specs=[pl.BlockSpec((tm, da.shape[1]), lambda i: (i, 0)) for da in das]
        + [pl.BlockSpec(w.shape, lambda i: (0, 0)) for w in ws] + [vec] * n + [row, row] + then_in,
        out_specs=[row] + [vec] * n + then_out,
        out_shape=[SDS((T, D), STREAM)] + [SDS((1, D), F32)] * n + then_shape,
        semantics=("arbitrary",), args=list(das) + list(ws) + list(gs) + [h_in, dh_out] + list(then), ride=ride)


def matmul_tn(a, b, *, tb, name, ride=None, ta=MXU_WIDTH):
    T, Ka = a.shape
    S, _, Nb = b.shape
    per = Nb // tb

    def body(a_ref, b_ref, o_ref):
        o_ref[...] = _dot_tn(a_ref[...], b_ref[...]).astype(BF16)

    out = _call(
        body, name=name, grid=(S * per, Ka // ta),
        in_specs=[pl.BlockSpec((T, ta), lambda j, i: (0, i)),
                  pl.BlockSpec((None, T, tb), lambda j, i: (j // per, 0, j % per))],
        out_specs=[pl.BlockSpec((ta, tb), lambda j, i: (i, j))],
        out_shape=[SDS((Ka, S * Nb), BF16)],
        semantics=("parallel", "parallel"), args=(a, b), ride=ride)
    return out[0] if ride is None else (out[0][0], out[1])


def conv_bwd(dy, bcx, conv_w, *, name, ride=None, tm=ROW_TILE):
    T, D = dy.shape
    nt = T // tm
    hb = tm // BF16_ROWS
    last = T // BF16_ROWS - 1

    def body(dy_ref, dyn_ref, b_ref, bn_ref, c_ref, u_ref, cp_ref, up_ref, cw_ref, o_ref, dw_ref):
        i = pl.program_id(0)
        c, u = c_ref[...].astype(F32), u_ref[...].astype(F32)
        cu = c * u
        cup = jnp.where(i == 0, 0.0, cp_ref[...].astype(F32) * up_ref[...].astype(F32))
        cu1, cu2 = _shift_down(cup, cu, 1), _shift_down(cup, cu, 2)
        w0, w1, w2 = cw_ref[0:1, :], cw_ref[1:2, :], cw_ref[2:3, :]
        dyf = dy_ref[...].astype(F32)
        o_ref[:, 0:D] = (dyf * (w0 * cu2 + w1 * cu1 + w2 * cu)).astype(BF16)
        dcv = dyf * b_ref[...].astype(F32)
        dcvn = jnp.where(i == nt - 1, 0.0, dyn_ref[...].astype(F32) * bn_ref[...].astype(F32))
        dcu = w2 * dcv + w1 * _shift_up(dcv, dcvn, 1) + w0 * _shift_up(dcv, dcvn, 2)
        o_ref[:, D:2 * D] = (dcu * u).astype(BF16)
        o_ref[:, 2 * D:3 * D] = (dcu * c).astype(BF16)
        row = lax.broadcasted_iota(jnp.int32, (8, D), 0)
        dw = jnp.zeros((8, D), F32)
        for tap, t in enumerate((cu2, cu1, cu)):
            dw = jnp.where(row == tap, jnp.sum(dcv * t, axis=0, keepdims=True), dw)
        _accumulate(dw_ref, i == 0, dw)

    tile = lambda col: pl.BlockSpec((tm, D), lambda i: (i, col))
    prev = lambda col: pl.BlockSpec((BF16_ROWS, D), lambda i: (jnp.maximum(i * hb - 1, 0), col))
    nxt = lambda col: pl.BlockSpec((BF16_ROWS, D), lambda i: (jnp.minimum((i + 1) * hb, last), col))
    return _call(
        body, name=name, grid=(nt,),
        in_specs=[tile(0), nxt(0), tile(0), nxt(0), tile(1), tile(2), prev(1), prev(2),
                  pl.BlockSpec((3, D), lambda i: (0, 0))],
        out_specs=[pl.BlockSpec((tm, 3 * D), lambda i: (i, 0)), pl.BlockSpec((8, D), lambda i: (0, 0))],
        out_shape=[SDS((T, 3 * D), BF16), SDS((8, D), F32)],
        semantics=("arbitrary",), args=(dy, dy, bcx, bcx, bcx, bcx, bcx, bcx, conv_w), ride=ride)


class NoTraffic:
    def ride(self, kernel_name):
        return None

    def landed(self, kernel_name, results, wts):
        pass

    def grad(self, key, value):
        pass


def local_step(x, target, wts, vec, traffic):
    T, D = x.shape
    tabs = rope_tables(T)
    small = {}

    def run(builder, *args, name, **kw):
        ride = traffic.ride(name)
        if ride is None:
            return builder(*args, name=name, **kw)
        out, extra = builder(*args, name=name, ride=ride, **kw)
        traffic.landed(name, extra, wts)
        return out

    bcx, xn1 = run(norm_matmul, x, vec["a_pre"], wts["w_in"], tn=3 * D, split=1, name="a_in")
    bcx = bcx[0]
    h1, z0, y0 = run(conv_mix_out, bcx, vec["conv_w"], wts["w_out"], vec["a_post"], x, name="a_out")
    gu0, act0, xt2 = run(norm_swiglu_in, h1, vec["ffn_pre0"], wts["gu0"], name="ffn0_in")
    h2, z1 = run(plain_mix_out, act0, wts["wd0"], vec["ffn_post0"], h1, name="ffn0_out")
    kvp, xkv, qp, xq = norm2_matmul(h2, [vec["kv_norm"], vec["b_pre"]], [wts["w_kv"], wts["w_q"]], name="kvq_in")
    (attn,) = run(attention_fwd, qp, kvp, tabs, vec["sinks"], name="attn_fwd")
    h3, z2 = plain_mix_out(attn, wts["w_o"], vec["b_post"], h2, name="attn_out", tm=BIG_ROW_TILE)
    gu1, act1, xt3 = run(norm_swiglu_in, h3, vec["ffn_pre1"], wts["gu1"], name="ffn1_in")
    dy, dz3, small["ffn_post1"], dact1, loss = plain_mix_out(act1, wts["wd1"], vec["ffn_post1"], h3, name="ffn1_out",
                                                             target=target)

    def ffn_bwd(layer, dz, dact, gu, act, xt, h_in, dh, then, gu_first):
        tag = "ffn%d" % layer
        dwd = lambda: traffic.grad("wd%d" % layer, run(matmul_tn, act, dz[None], tb=D, name=tag + "_dwd"))
        dwgu = lambda: traffic.grad("gu%d" % layer, run(swiglu_bwd_tn, xt, dact, gu, name=tag + "_dwgu"))
        for step in ((dwgu, dwd) if gu_first else (dwd, dwgu)):
            step()
        dh_in, small["ffn_pre%d" % layer], dz_, dg_, da_ = run(
            swiglu_bwd_in, dact, gu, wts["gu%d" % layer], h_in, vec["ffn_pre%d" % layer], dh, then,
            name=tag + "_in_bwd")
        return dh_in, dz_, dg_, da_

    dh3, dz2, small["b_post"], dattn = ffn_bwd(1, dz3, dact1, gu1, act1, xt3, h3, dy,
                                               (z2, vec["b_post"], wts["w_o"]), gu_first=False)
    traffic.grad("w_o", matmul_tn(attn, dz2[None], tb=D, name="attn_dwo"))
    dq, dkv_cur, dkv_prev, small["sinks"] = run(attention_bwd, qp, kvp, tabs, vec["sinks"], dattn, name="attn_bwd")
    dkv = combine_dkv(dkv_cur, dkv_prev, name="attn_dkv")
    traffic.grad("w_q", matmul_tn(xq, dq[None], tb=D, name="attn_dwq"))
    traffic.grad("w_kv", matmul_tn(xkv, dkv[None], tb=dkv.shape[1], name="attn_dwkv"))
    dh2, small["b_pre"], small["kv_norm"], dz1, small["ffn_post0"], dact0 = run(
        matmuls_nt_normbwd, [dq, dkv], [wts["w_q"], wts["w_kv"]], h2, [vec["b_pre"], vec["kv_norm"]], dh3,
        (z1, vec["ffn_post0"], wts["wd0"]), name="qkv_in_bwd")
    dh1, dz0, small["a_post"], dyc = ffn_bwd(0, dz1, dact0, gu0, act0, xt2, h1, dh2,
                                             (z0, vec["a_post"], wts["w_out"]), gu_first=True)
    traffic.grad("w_out", run(matmul_tn, y0, dz0[None], tb=D, name="a_dwout"))
    dbcx, small["conv_w"] = run(conv_bwd, dyc, bcx, vec["conv_w"], name="a_conv_bwd")
    traffic.grad("w_in", matmul_tn(xn1, dbcx[None], tb=3 * D // 2, name="a_dwin"))
    dx, small["a_pre"] = run(matmul_nt_normbwd, dbcx[None], wts["w_in"], x, vec["a_pre"], dh1, name="a_in_bwd",
                             tm=ROW_TILE // 2)
    return loss, dx, small


SMALL_ROWS = 16
LOSS_ROW = 13

WHOLE = None
GATHER_PLAN = {"cast_rest": [("w_in", WHOLE)],
               "a_in": [("w_out", WHOLE), ("gu0", (0, 18))],
               "a_out": [("gu0", (18, 14))],
               "ffn0_in": [("wd0", WHOLE), ("w_kv", WHOLE), ("w_q", WHOLE), ("w_o", WHOLE)],
               "ffn0_out": [("gu1", (0, 16))],
               "attn_fwd": [("gu1", (16, 16))],
               "ffn1_in": [("wd1", WHOLE)]}
PAIR_PLAN = {"ffn1_dwgu": ["wd1"], "ffn1_in_bwd": ["gu1"], "attn_bwd": ["w_o"], "qkv_in_bwd": ["w_q", "w_kv"],
             "ffn0_dwd": ["gu0"], "ffn0_in_bwd": ["wd0"], "a_conv_bwd": ["w_out"]}
PAIR_ALONE = ["w_in"]
CHIP_PLAN = {"ffn1_in_bwd": [("wd1", WHOLE)], "attn_bwd": [("gu1", WHOLE)],
             "ffn0_dwgu": [("w_o", WHOLE), ("w_q", WHOLE), ("w_kv", WHOLE)],
             "ffn0_in_bwd": [("gu0", WHOLE)], "a_dwout": [("wd0", (0, 8))], "a_conv_bwd": [("wd0", (8, 14))],
             "a_in_bwd": [("w_out", WHOLE), ("w_in", WHOLE)]}
HALF_PLAN = {"a_in_bwd": ["gu0", "gu1", "wd0", "wd1", "w_kv", "w_q", "w_o"]}
GRAD_KIND = dict(KIND, gu0="split", gu1="split")


class Traffic:
    def __init__(self, wholes, quarter, c_arr, pc_arr):
        self.wholes, self.quarter, self.c_arr, self.pc_arr = wholes, quarter, c_arr, pc_arr
        self.views, self.sums, self.got = {}, {}, {}
        self.reduced = {}
        self.stages = {}

    def reduce(self, keys, name):
        return chip_reduce([self.sums[k] for k in keys], [self.got[k] for k in keys], [GRAD_KIND[k] for k in keys],
                           self.pc_arr, name=name)

    def ride(self, name, small=None):
        rides, stages = [], []
        if name in GATHER_PLAN:
            plan = GATHER_PLAN[name]
            rides.append(gather_ride([self.wholes[k] for k, _ in plan],
                                     [(KIND[k], self.quarter[k], part) for k, part in plan], small))
            stages.append(("gather", [k for k, _ in plan]))
        if name in CHIP_PLAN:
            plan = CHIP_PLAN[name]
            rides.append(chip_ride([self.sums[k] for k, _ in plan],
                                   [(GRAD_KIND[k], self.quarter[k], part) for k, part in plan],
                                   earlier=[self.got.get(k) for k, _ in plan]))
            stages.append(("chip", [k for k, _ in plan]))
        if name in PAIR_PLAN:
            keys = PAIR_PLAN[name]
            rides.append(pair_ride([self.views[k] for k in keys]))
            stages.append(("pair", keys))
        if name in HALF_PLAN:
            keys = HALF_PLAN[name]
            rides.append(half_ride(self.reduce(keys, "chip_reduce_early")))
            stages.append(("half", keys))
        self.stages[name] = stages
        return join(rides)

    def landed(self, name, results, wts):
        results = list(results)
        for stage, keys in self.stages[name]:
            mine, results = results[:len(keys)], results[len(keys):]
            if stage == "gather":
                for k, whole in zip(keys, mine):
                    self.wholes[k] = wts[k] = whole
            elif stage == "chip":
                self.got.update(zip(keys, mine))
            elif stage == "half":
                self.reduced.update(zip(keys, mine))
            else:
                for k, got in zip(keys, mine):
                    self.sums[k] = pair_add(self.views[k], got, self.c_arr, name="pair_add_" + k)

    def grad(self, key, value):
        r, ws = self.quarter[key]
        view = {"row": (N_CHIPS, 2, r // 2, ws), "col": (1, 2, r // 2, N_CHIPS * ws), "split": (2, 2, r // 2, 2 * ws)}
        self.views[key] = value.reshape(view[GRAD_KIND[key]])
        if key in PAIR_ALONE:
            (got,) = alone(pair_ride([self.views[key]]), name="pair_exchange_" + key)
            self.sums[key] = pair_add(self.views[key], got, self.c_arr, name="pair_add_" + key)


def kernel(x, a_pre_norm, a_w_in, a_conv_w, a_w_out, a_post_norm, ffn_pre_norm, ffn_w_gate_up, ffn_w_down, ffn_post_norm, kv_norm, w_kv, b_pre_norm, b_w_q, b_sinks, b_w_o, b_post_norm, loss_target, m_a_pre_norm, m_a_w_in, m_a_conv_w, m_a_w_out, m_a_post_norm, m_ffn_pre_norm, m_ffn_w_gate_up, m_ffn_w_down, m_ffn_post_norm, m_kv_norm, m_w_kv, m_b_pre_norm, m_b_w_q, m_b_sinks, m_b_w_o, m_b_post_norm, v_a_pre_norm, v_a_w_in, v_a_conv_w, v_a_w_out, v_a_post_norm, v_ffn_pre_norm, v_ffn_w_gate_up, v_ffn_w_down, v_ffn_post_norm, v_kv_norm, v_w_kv, v_b_pre_norm, v_b_w_q, v_b_sinks, v_b_w_o, v_b_post_norm):
    T, D = x.shape[1], x.shape[2]
    xi, yi, ci = _place()
    p = 2 * xi + yi
    p_arr = jnp.reshape(p, (1,)).astype(jnp.int32)
    c_arr = jnp.reshape(ci, (1,)).astype(jnp.int32)
    pc_arr = jnp.stack([p, ci]).astype(jnp.int32)
    me_arr = jnp.reshape(4 * xi + 2 * yi + ci, (1,)).astype(jnp.int32)
    qd = D // N_CHIPS

    big = {"w_in": (a_w_in, 0), "w_out": (a_w_out, 0), "gu0": (ffn_w_gate_up, 0), "gu1": (ffn_w_gate_up, 1),
           "wd0": (ffn_w_down, 0), "wd1": (ffn_w_down, 1), "w_kv": (w_kv[None], 0), "w_q": (b_w_q, 0),
           "w_o": (b_w_o, 0)}
    names = list(big)
    quarter = {k: w.shape[1:] for k, (w, _) in big.items()}
    source = lambda k: big[k] + (KIND[k],)
    traffic = Traffic(dict(zip(names[:1], cast_quarters([source(names[0])], p_arr, name="cast_first"))), quarter,
                      c_arr, pc_arr)
    small_shard = jnp.concatenate([a_pre_norm, a_post_norm, a_conv_w[0], jnp.zeros((3, qd), F32)], axis=0)
    wts = {}
    rest, (*landed, small_full) = cast_quarters([source(k) for k in names[1:]], p_arr, name="cast_rest",
                                                ride=traffic.ride("cast_rest", small_shard))
    traffic.wholes.update(zip(names[1:], rest))
    traffic.landed("cast_rest", landed, wts)
    rows = lambda k: jnp.transpose(small_full[:, k], (1, 0, 2)).reshape(-1, D)
    vec = {"a_pre": rows(slice(0, 1)), "a_post": rows(slice(1, 2)), "conv_w": rows(slice(2, 5)),
           "ffn_pre0": ffn_pre_norm[0:1], "ffn_pre1": ffn_pre_norm[1:2],
           "ffn_post0": ffn_post_norm[0:1], "ffn_post1": ffn_post_norm[1:2],
           "kv_norm": kv_norm[None], "b_pre": b_pre_norm, "b_post": b_post_norm, "sinks": b_sinks}

    loss, dx, small = local_step(x[0], loss_target[0], wts, vec, traffic)

    pad = lambda a: jnp.pad(a, ((0, 0), (0, D - a.shape[1])))
    small_block = jnp.concatenate(
        [small["a_pre"], small["a_post"], small["conv_w"][0:3], small["ffn_pre0"], small["ffn_pre1"],
         small["ffn_post0"], small["ffn_post1"], small["kv_norm"], small["b_pre"], small["b_post"],
         pad(small["sinks"][0:1]), pad(loss[0:1]), jnp.zeros((SMALL_ROWS - LOSS_ROW - 1, D), F32)], axis=0)
    late = [k for k in names if k not in traffic.reduced]
    *swapped, small_blocks = alone(join([half_ride(traffic.reduce(late, "chip_reduce_late")),
                                         chip_ride([], [], small_block)]), name="last_exchange")
    traffic.reduced.update(zip(late, swapped))
    grad = {k: traffic.reduced[k].reshape(quarter[k]) for k in names}
    small_sum = small_reduce(small_blocks, me_arr)

    out = {}
    out["a_w_in"] = adamw(a_w_in, [grad["w_in"]], m_a_w_in, v_a_w_in, name="adamw_a_w_in")
    out["a_w_out"] = adamw(a_w_out, [grad["w_out"]], m_a_w_out, v_a_w_out, name="adamw_a_w_out")
    out["ffn_w_gate_up"] = adamw(ffn_w_gate_up, [grad["gu0"], grad["gu1"]], m_ffn_w_gate_up, v_ffn_w_gate_up,
                                 name="adamw_ffn_w_gate_up")
    out["ffn_w_down"] = adamw(ffn_w_down, [grad["wd0"], grad["wd1"]], m_ffn_w_down, v_ffn_w_down,
                              name="adamw_ffn_w_down")
    out["w_kv"] = [o[0] for o in adamw(w_kv[None], [grad["w_kv"]], m_w_kv[None], v_w_kv[None], name="adamw_w_kv")]
    out["b_w_q"] = adamw(b_w_q, [grad["w_q"]], m_b_w_q, v_b_w_q, name="adamw_b_w_q")
    out["b_w_o"] = adamw(b_w_o, [grad["w_o"]], m_b_w_o, v_b_w_o, name="adamw_b_w_o")

    def pack(a_pre, a_post, conv, ffn_pre, ffn_post, kvn, b_pre, b_post, sinks):
        return jnp.concatenate([pad(a_pre), pad(a_post), pad(conv[0]), ffn_pre, ffn_post, kvn[None], b_pre, b_post,
                                pad(sinks), jnp.zeros((SMALL_ROWS - 13, D), F32)], axis=0)

    g_small = jnp.concatenate([pad(lax.dynamic_slice(small_sum, (0, p * qd), (5, qd))), small_sum[5:]], axis=0)
    w_small = pack(a_pre_norm, a_post_norm, a_conv_w, ffn_pre_norm, ffn_post_norm, kv_norm, b_pre_norm, b_post_norm,
                   b_sinks)
    m_small = pack(m_a_pre_norm, m_a_post_norm, m_a_conv_w, m_ffn_pre_norm, m_ffn_post_norm, m_kv_norm,
                   m_b_pre_norm, m_b_post_norm, m_b_sinks)
    v_small = pack(v_a_pre_norm, v_a_post_norm, v_a_conv_w, v_ffn_pre_norm, v_ffn_post_norm, v_kv_norm,
                   v_b_pre_norm, v_b_post_norm, v_b_sinks)
    packed = adamw(w_small[None], [g_small], m_small[None], v_small[None], name="adamw_small")
    ns = b_sinks.shape[1]
    unpack = lambda a: {"a_pre_norm": a[0:1, :qd], "a_post_norm": a[1:2, :qd], "a_conv_w": a[None, 2:5, :qd],
                        "ffn_pre_norm": a[5:7], "ffn_post_norm": a[7:9], "kv_norm": a[9], "b_pre_norm": a[10:11],
                        "b_post_norm": a[11:12], "b_sinks": a[12:13, :ns]}
    unpacked = [unpack(a[0]) for a in packed]
    for k in unpacked[0]:
        out[k] = [u[k] for u in unpacked]

    order = ["a_pre_norm", "a_w_in", "a_conv_w", "a_w_out", "a_post_norm", "ffn_pre_norm", "ffn_w_gate_up",
             "ffn_w_down", "ffn_post_norm", "kv_norm", "w_kv", "b_pre_norm", "b_w_q", "b_sinks", "b_w_o",
             "b_post_norm"]
    return (small_sum[LOSS_ROW, 0], dx[None], *[out[k][0] for k in order], *[out[k][1] for k in order],
            *[out[k][2] for k in order], *[out[k][3] for k in order])
```

```python
import math

import jax
import jax.numpy as jnp
from jax import lax
from jax.experimental import pallas as pl
from jax.experimental.pallas import tpu as pltpu

F32 = jnp.float32
BF16 = jnp.bfloat16
SDS = jax.ShapeDtypeStruct
MESH = pl.DeviceIdType.MESH
DMA = pltpu.SemaphoreType.DMA
HBM_SPEC = pl.BlockSpec(memory_space=pltpu.HBM)

EPS = 1e-6
NEG = -1e30
HEAD_DIM = 64
N_KV_HEADS = 4
BLOCK = 128
ROT_DIM = HEAD_DIM // 4
ROPE_THETA = 500000.0
N_CHIPS = 4

ADAM_LR = 0.001
ADAM_B1 = 0.9
ADAM_B2 = 0.999
ADAM_EPS = 1e-08
ADAM_WD = 0.01
ADAM_STEP = 10

VMEM_LIMIT_BYTES = 52 * 1024 * 1024
ROW_TILE = 512
BF16_ROWS = 16
STREAM = BF16
MXU_WIDTH = 256

KIND = {"w_in": "col", "gu0": "col", "gu1": "col", "w_out": "row", "wd0": "row", "wd1": "row", "w_kv": "row",
        "w_q": "row", "w_o": "row"}


def _params(*semantics):
    return pltpu.CompilerParams(dimension_semantics=semantics, vmem_limit_bytes=VMEM_LIMIT_BYTES)


def _row_tile(rows, limit, step=8):
    return max(t for t in range(step, limit + 1, step) if rows % t == 0)


def _place():
    return lax.axis_index("x"), lax.axis_index("y"), lax.axis_index("c")


def _other_chips(x, y):
    return [(1 - x, y), (x, 1 - y), (1 - x, 1 - y)]


def _remote(src, dst, send_sem, recv_sem, to):
    return pltpu.make_async_remote_copy(src_ref=src, dst_ref=dst, send_sem=send_sem, recv_sem=recv_sem,
                                        device_id=to, device_id_type=MESH)


def _full_shape(kind, quarter):
    r, ws = quarter
    return (N_CHIPS * r, ws) if kind == "row" else (r, N_CHIPS * ws)


def _rows_of(h, part):
    lo, n = (0, h) if part is None else (part[0] * BF16_ROWS, part[1] * BF16_ROWS)
    assert lo + n <= h, (h, part)
    return lo, n


def _half_of_quarter(ref, kind, quarter, part, q, half):
    r, ws = quarter
    h = r // 2
    lo, n = _rows_of(h, part)
    if kind == "row":
        return ref.at[pl.ds(pl.multiple_of(q * r + half * h + lo, BF16_ROWS), n)]
    return ref.at[pl.ds(pl.multiple_of(half * h + lo, BF16_ROWS), n), pl.ds(pl.multiple_of(q * ws, 128), ws)]


class Ride:
    def __init__(self, operands, out_shape, aliases, sems, make):
        self.operands, self.out_shape, self.aliases, self.sems, self.make = operands, out_shape, aliases, sems, make


def join(rides):
    rides = [r for r in rides if r is not None]
    if len(rides) < 2:
        return rides[0] if rides else None
    aliases, at = {}, [0, 0, 0]
    cuts = []
    for r in rides:
        aliases.update({at[0] + i: at[1] + o for i, o in r.aliases.items()})
        cuts.append(tuple(at))
        at = [at[0] + len(r.operands), at[1] + len(r.out_shape), at[2] + len(r.sems)]
    cuts.append(tuple(at))

    def make(ins, outs, sem):
        made = [r.make(ins[lo[0]:hi[0]], outs[lo[1]:hi[1]], sem[lo[2]:hi[2]]) for r, lo, hi in zip(rides, cuts, cuts[1:])]

        def start():
            for s, _ in made:
                s()

        def finish():
            for _, f in made:
                f()

        return start, finish

    return Ride(sum((list(r.operands) for r in rides), []), sum((list(r.out_shape) for r in rides), []), aliases,
                sum((list(r.sems) for r in rides), []), make)


def _call(body, *, name, grid, in_specs, out_specs, out_shape, args, scratch_shapes=(), semantics=None, ride=None,
          prefetch=None):
    pre = 0 if prefetch is None else 1
    n_in, n_out, n_scr = len(in_specs), len(out_specs), len(scratch_shapes)
    r_in, r_out = (len(ride.operands), len(ride.out_shape)) if ride is not None else (0, 0)
    a, b = pre + n_in, pre + n_in + r_in
    c, d = b + n_out, b + n_out + r_out
    e = d + n_scr

    def riding(*refs):
        start, finish = ride.make(refs[a:b], refs[c:d], refs[e:])
        ids = [pl.program_id(k) for k in range(len(grid))]
        first, last = ids[0] == 0, ids[0] == grid[0] - 1
        for k in range(1, len(grid)):
            first, last = first & (ids[k] == 0), last & (ids[k] == grid[k] - 1)
        pl.when(first)(start)
        body(*refs[:a], *refs[b:c], *refs[d:e])
        pl.when(last)(finish)

    if ride is None:
        kernel_body, extra_in, extra_out, extra_shape, extra_scr, aliases = body, [], [], [], [], {}
        params = _params(*semantics)
    else:
        kernel_body, extra_in, extra_out = riding, [HBM_SPEC] * r_in, [HBM_SPEC] * r_out
        extra_shape, extra_scr = list(ride.out_shape), list(ride.sems)
        aliases = {pre + n_in + i: n_out + o for i, o in ride.aliases.items()}
        params = _params(*(("arbitrary",) * len(grid)))
    specs = dict(grid=grid, in_specs=list(in_specs) + extra_in, out_specs=list(out_specs) + extra_out,
                 scratch_shapes=list(scratch_shapes) + extra_scr)
    if prefetch is not None:
        specs = dict(grid_spec=pltpu.PrefetchScalarGridSpec(num_scalar_prefetch=1, **specs))
        args = (prefetch,) + tuple(args)
    outs = pl.pallas_call(kernel_body, name=name, out_shape=list(out_shape) + extra_shape,
                          input_output_aliases=aliases, compiler_params=params, **specs,
                          )(*args, *(ride.operands if ride is not None else ()))
    return outs if ride is None else (outs[:n_out], outs[n_out:])


def alone(ride, *, name):
    def body(*refs):
        n = len(ride.operands)
        start, finish = ride.make(refs[:n], refs[n:n + len(ride.out_shape)], refs[n + len(ride.out_shape):])
        start()
        finish()

    return pl.pallas_call(
        body, name=name, in_specs=[HBM_SPEC] * len(ride.operands), out_specs=[HBM_SPEC] * len(ride.out_shape),
        out_shape=list(ride.out_shape), input_output_aliases=dict(ride.aliases), scratch_shapes=list(ride.sems),
    )(*ride.operands)


def gather_ride(wholes, metas, small=None):
    n = len(wholes)
    operands, out_shape = list(wholes), [SDS(s.shape, s.dtype) for s in wholes]
    sems = [DMA((n, 3)), DMA((n, 3)), DMA((n, 3)), DMA((n, 3))]
    if small is not None:
        operands.append(small)
        out_shape.append(SDS((N_CHIPS,) + small.shape, small.dtype))
        sems += [DMA((3,)), DMA((3,)), DMA(())]

    def make(ins, outs, sem):
        send1, recv1, send2, recv2 = sem[:4]
        x, y, c = _place()
        p = 2 * x + y
        chips = _other_chips(x, y)
        me, sibling = (x, y, c), (x, y, 1 - c)
        part = lambda t, q, half: _half_of_quarter(outs[t], *metas[t], q, half)
        first = []
        for j, (qx, qy) in enumerate(chips):
            if small is not None:
                first.append(_remote(ins[n], outs[n].at[p], sem[4].at[j], sem[5].at[j], (qx, qy, c)))
            for t in range(n):
                first.append(_remote(part(t, p, c), part(t, p, c), send1.at[t, j], recv1.at[t, j], (qx, qy, c)))
        local = [] if small is None else [pltpu.make_async_copy(ins[n], outs[n].at[p], sem[6])]

        def start():
            for cp in local + first:
                cp.start()

        def finish():
            passed = []
            for j, (qx, qy) in enumerate(chips):
                q = 2 * qx + qy
                for t in range(n):
                    landed = part(t, q, c)
                    _remote(landed, landed, send1.at[t, j], recv1.at[t, j], me).wait_recv()
                    cp = _remote(landed, landed, send2.at[t, j], recv2.at[t, j], sibling)
                    cp.start()
                    passed.append(cp)
            for j, (qx, qy) in enumerate(chips):
                q = 2 * qx + qy
                if small is not None:
                    _remote(outs[n].at[q], outs[n].at[q], sem[4].at[j], sem[5].at[j], me).wait_recv()
                for t in range(n):
                    theirs = part(t, q, 1 - c)
                    _remote(theirs, theirs, send2.at[t, j], recv2.at[t, j], me).wait_recv()
            for cp in first + passed:
                cp.wait_send()
            for cp in local:
                cp.wait()

        return start, finish

    return Ride(operands, out_shape, {t: t for t in range(n)}, sems, make)


def chip_ride(sums, metas, small=None, earlier=None):
    n = len(sums)
    operands = list(sums)
    out_shape = [SDS((3, s.shape[1], quarter[1]), s.dtype) for s, (_, quarter, _) in zip(sums, metas)]
    sems = [DMA((n, 3)), DMA((n, 3))] if n else []
    if small is not None:
        operands.append(small)
        out_shape.append(SDS((8,) + small.shape, small.dtype))
        sems += [DMA((7,)), DMA((7,)), DMA(())]
    aliases = {}
    for t, buffer in enumerate(earlier or [None] * n):
        if buffer is not None:
            aliases[len(operands)] = t
            operands.append(buffer)

    def make(ins, outs, sem):
        x, y, c = _place()
        cps = []
        for j, (qx, qy) in enumerate(_other_chips(x, y)):
            q = 2 * qx + qy
            for t in range(n):
                kind, (_, ws), part = metas[t]
                rows = pl.ds(*_rows_of(ins[t].shape[1], part))
                if kind == "row":
                    src = ins[t].at[q, rows]
                elif kind == "col":
                    src = ins[t].at[0, rows, pl.ds(pl.multiple_of(q * ws, 128), ws)]
                else:
                    src = ins[t].at[q // 2, rows, pl.ds(pl.multiple_of((q % 2) * ws, 128), ws)]
                cps.append(_remote(src, outs[t].at[j, rows], sem[0].at[t, j], sem[1].at[t, j], (qx, qy, c)))
        local = []
        if small is not None:
            ssend, srecv, lsem = sem[2 * bool(n):2 * bool(n) + 3]
            local.append(pltpu.make_async_copy(ins[n], outs[n].at[0], lsem))
            for k in range(1, 8):
                peer = (x ^ (k >> 2 & 1), y ^ (k >> 1 & 1), c ^ (k & 1))
                cps.append(_remote(ins[n], outs[n].at[k], ssend.at[k - 1], srecv.at[k - 1], peer))

        def start():
            for cp in local + cps:
                cp.start()

        def finish():
            for cp in cps + local:
                cp.wait()

        return start, finish

    return Ride(operands, out_shape, aliases, sems, make)


def pair_ride(grads):
    n = len(grads)

    def make(ins, outs, sem):
        x, y, c = _place()
        cps = [_remote(ins[t].at[:, 1 - c], outs[t], sem[0].at[t], sem[1].at[t], (x, y, 1 - c)) for t in range(n)]

        def start():
            for cp in cps:
                cp.start()

        def finish():
            for cp in cps:
                cp.wait()

        return start, finish

    return Ride(list(grads), [SDS((g.shape[0],) + g.shape[2:], g.dtype) for g in grads], {}, [DMA((n,)), DMA((n,))],
                make)


def half_ride(quarters):
    n = len(quarters)

    def make(ins, outs, sem):
        x, y, c = _place()
        sends = [_remote(outs[t].at[c], outs[t].at[c], sem[0].at[t], sem[1].at[t], (x, y, 1 - c)) for t in range(n)]

        def start():
            for cp in sends:
                cp.start()

        def finish():
            for t in range(n):
                theirs = outs[t].at[1 - c]
                _remote(theirs, theirs, sem[0].at[t], sem[1].at[t], (x, y, c)).wait_recv()
            for cp in sends:
                cp.wait_send()

        return start, finish

    return Ride(list(quarters), [SDS(q.shape, q.dtype) for q in quarters], {t: t for t in range(n)},
                [DMA((n,)), DMA((n,))], make)


CAST_STEPS = 4


def cast_quarters(sources, p_arr, *, name, ride=None):
    n = len(sources)
    in_specs, out_specs, out_shape = [], [], []
    for w, layer, kind in sources:
        _, r, ws = w.shape
        tr = r // CAST_STEPS
        assert tr % BF16_ROWS == 0, w.shape
        in_specs.append(pl.BlockSpec((None, tr, ws), lambda i, p_ref, layer=layer: (layer, i, 0)))
        out_specs.append(pl.BlockSpec((tr, ws), (lambda i, p_ref: (p_ref[0] * CAST_STEPS + i, 0)) if kind == "row"
                                      else (lambda i, p_ref: (i, p_ref[0]))))
        out_shape.append(SDS(_full_shape(kind, (r, ws)), BF16))

    def body(p_ref, *refs):
        for w_ref, o_ref in zip(refs[:n], refs[n:]):
            o_ref[...] = w_ref[...].astype(BF16)

    return _call(body, name=name, grid=(CAST_STEPS,), in_specs=in_specs, out_specs=out_specs, out_shape=out_shape,
                 semantics=("parallel",), args=[w for w, _, _ in sources], ride=ride, prefetch=p_arr)


def pair_add(own, got, c_arr, *, name):
    A, _, h, W = own.shape
    th = _row_tile(h, max(BF16_ROWS, (3 << 19) // W), BF16_ROWS)

    def body(c_ref, a_ref, b_ref, o_ref):
        o_ref[...] = (a_ref[...].astype(F32) + b_ref[...].astype(F32)).astype(BF16)

    return pl.pallas_call(
        body, name=name,
        grid_spec=pltpu.PrefetchScalarGridSpec(
            num_scalar_prefetch=1, grid=(A, h // th),
            in_specs=[pl.BlockSpec((None, None, th, W), lambda q, i, c_ref: (q, c_ref[0], i, 0)),
                      pl.BlockSpec((None, th, W), lambda q, i, c_ref: (q, i, 0))],
            out_specs=pl.BlockSpec((None, th, W), lambda q, i, c_ref: (q, i, 0))),
        out_shape=SDS((A, h, W), BF16),
        compiler_params=_params("parallel", "parallel"),
    )(c_arr, own, got)


REDUCE_STEPS = 2


def chip_reduce(sums, got, kinds, pc_arr, *, name):
    n = len(sums)
    mine = {"row": lambda i, pc_ref: (pc_ref[0], i, 0), "col": lambda i, pc_ref: (0, i, pc_ref[0]),
            "split": lambda i, pc_ref: (pc_ref[0] // 2, i, pc_ref[0] % 2)}
    a_specs, b_specs, o_specs, out_shape = [], [], [], []
    for g, kind in zip(got, kinds):
        _, h, ws = g.shape
        th = h // REDUCE_STEPS
        assert th % BF16_ROWS == 0, g.shape
        a_specs.append(pl.BlockSpec((None, th, ws), mine[kind]))
        b_specs.append(pl.BlockSpec((3, th, ws), lambda i, pc_ref: (0, i, 0)))
        o_specs.append(pl.BlockSpec((None, th, ws), lambda i, pc_ref: (pc_ref[1], i, 0)))
        out_shape.append(SDS((2, h, ws), F32))

    def body(pc_ref, *refs):
        for a_ref, b_ref, o_ref in zip(refs[:n], refs[n:2 * n], refs[2 * n:]):
            o_ref[...] = ((a_ref[...].astype(F32) + b_ref[0].astype(F32)) + b_ref[1].astype(F32)) + b_ref[2].astype(F32)

    return _call(body, name=name, grid=(REDUCE_STEPS,), in_specs=a_specs + b_specs, out_specs=o_specs,
                 out_shape=out_shape, semantics=("parallel",), args=list(sums) + list(got), prefetch=pc_arr)


def small_reduce(blocks, me_arr):
    _, rows, D = blocks.shape

    def body(me_ref, b_ref, o_ref):
        me = me_ref[0]
        total = b_ref[me]
        for d in range(1, 8):
            total = total + b_ref[d ^ me]
        o_ref[...] = total

    return pl.pallas_call(
        body, name="small_reduce",
        grid_spec=pltpu.PrefetchScalarGridSpec(
            num_scalar_prefetch=1, grid=(1,),
            in_specs=[pl.BlockSpec((8, rows, D), lambda i, me_ref: (0, 0, 0))],
            out_specs=pl.BlockSpec((rows, D), lambda i, me_ref: (0, 0))),
        out_shape=SDS((rows, D), F32),
        compiler_params=_params("arbitrary"),
    )(me_arr, blocks)


def adamw(w, gs, m, v, *, name):
    L, r, cols = w.shape
    tr = _row_tile(r, 256)
    nt = r // tr

    def body(*refs):
        w_ref, m_ref, v_ref = refs[:3]
        g_refs = refs[3:3 + L]
        g_out, d_out, m_out, v_out = refs[3 + L:]
        layer = pl.program_id(0)
        g = g_refs[0][...]
        for l in range(1, L):
            g = jnp.where(layer == l, g_refs[l][...], g)
        m_new = ADAM_B1 * m_ref[...] + (1.0 - ADAM_B1) * g
        v_new = ADAM_B2 * v_ref[...] + (1.0 - ADAM_B2) * (g * g)
        m_hat = m_new / (1.0 - ADAM_B1 ** ADAM_STEP)
        v_hat = v_new / (1.0 - ADAM_B2 ** ADAM_STEP)
        g_out[...] = g
        m_out[...] = m_new
        v_out[...] = v_new
        d_out[...] = -ADAM_LR * (m_hat / (jnp.sqrt(v_hat) + ADAM_EPS) + ADAM_WD * w_ref[...])

    full = pl.BlockSpec((None, tr, cols), lambda l, i: (l, i, 0))
    g_spec = lambda l0: pl.BlockSpec((tr, cols), lambda l, i: (jnp.where(l == l0, i, jnp.where(l < l0, 0, nt - 1)), 0))
    return pl.pallas_call(
        body, name=name, grid=(L, nt),
        in_specs=[full, full, full] + [g_spec(l0) for l0 in range(L)],
        out_specs=[full] * 4,
        out_shape=[SDS(w.shape, F32)] * 4,
        compiler_params=_params("arbitrary", "arbitrary"),
    )(w, m, v, *gs)


def _rms_r(xf):
    return lax.rsqrt(jnp.mean(xf * xf, axis=-1, keepdims=True) + EPS)


def _rmsnorm_bwd(xf, g, dy):
    r = _rms_r(xf)
    xh = xf * r
    gd = g * dy
    return r * (gd - xh * jnp.mean(xh * gd, axis=-1, keepdims=True)), xh


def _dot(a, b):
    return jnp.dot(a, b, preferred_element_type=F32)


def _dot_nt(a, b):
    return lax.dot_general(a, b, (((1,), (1,)), ((), ())), preferred_element_type=F32)


def _dot_tn(a, b):
    return lax.dot_general(a, b, (((0,), (0,)), ((), ())), preferred_element_type=F32)


def _accumulate(ref, first, value):
    @pl.when(first)
    def _():
        ref[...] = value

    @pl.when(jnp.logical_not(first))
    def _():
        ref[...] += value


def norm_matmul(x, g, w, *, tn, split, name, ride=None, tm=ROW_TILE):
    T, D = x.shape
    N = w.shape[1]
    per = N // split // tn

    def body(x_ref, g_ref, w_ref, o_ref, xn_ref):
        @pl.when(pl.program_id(1) == 0)
        def _():
            xf = x_ref[...].astype(F32)
            xn_ref[...] = (xf * _rms_r(xf) * g_ref[...]).astype(BF16)

        o_ref[...] = _dot(xn_ref[...], w_ref[...]).astype(BF16)

    return _call(
        body, name=name, grid=(T // tm, N // tn),
        in_specs=[pl.BlockSpec((tm, D), lambda i, j: (i, 0)),
                  pl.BlockSpec((1, D), lambda i, j: (0, 0)),
                  pl.BlockSpec((D, tn), lambda i, j: (0, j))],
        out_specs=[pl.BlockSpec((None, tm, tn), lambda i, j: (j // per, i, j % per)),
                   pl.BlockSpec((tm, D), lambda i, j: (i, 0))],
        out_shape=[SDS((split, T, N // split), BF16), SDS((T, D), BF16)],
        semantics=("parallel", "arbitrary"), args=(x, g, w), ride=ride)


BIG_ROW_TILE = 1024


def norm2_matmul(x, gains, weights, *, name, tm=BIG_ROW_TILE):
    T, D = x.shape
    tm = min(tm, T)
    n = len(gains)

    def body(x_ref, *refs):
        xf = x_ref[...].astype(F32)
        xh = xf * _rms_r(xf)
        for g_ref, w_ref, o_ref, xn_ref in zip(refs[:n], refs[n:2 * n], refs[2 * n::2], refs[2 * n + 1::2]):
            xn = (xh * g_ref[...]).astype(BF16)
            xn_ref[...] = xn
            o_ref[...] = _dot(xn, w_ref[...]).astype(BF16)

    row = pl.BlockSpec((tm, D), lambda i: (i, 0))
    vec = pl.BlockSpec((1, D), lambda i: (0, 0))
    out_specs, out_shape = [], []
    for w in weights:
        out_specs += [pl.BlockSpec((tm, w.shape[1]), lambda i: (i, 0)), row]
        out_shape += [SDS((T, w.shape[1]), BF16), SDS((T, D), BF16)]
    return _call(
        body, name=name, grid=(T // tm,),
        in_specs=[row] + [vec] * n + [pl.BlockSpec(w.shape, lambda i: (0, 0)) for w in weights],
        out_specs=out_specs, out_shape=out_shape, semantics=("parallel",), args=[x] + list(gains) + list(weights))


def _shift_down(prev, cur, by):
    big = jnp.concatenate([prev, cur], axis=0)
    return pltpu.roll(big, by, 0)[prev.shape[0]:]


def _shift_up(cur, nxt, by):
    big = jnp.concatenate([cur, nxt], axis=0)
    return pltpu.roll(big, big.shape[0] - by, 0)[:cur.shape[0]]


def conv_mix_out(bcx, conv_w, w_out, g_post, res, *, name, ride=None, tm=ROW_TILE):
    T, D = res.shape
    hb = tm // BF16_ROWS

    def body(b_ref, c_ref, u_ref, cp_ref, up_ref, cw_ref, w_ref, g_ref, r_ref, h_ref, z_ref, y_ref):
        i = pl.program_id(0)
        cu = c_ref[...].astype(F32) * u_ref[...].astype(F32)
        cup = cp_ref[...].astype(F32) * up_ref[...].astype(F32)
        cup = jnp.where(i == 0, 0.0, cup)
        cv = (cw_ref[0:1, :] * _shift_down(cup, cu, 2) + cw_ref[1:2, :] * _shift_down(cup, cu, 1)
              + cw_ref[2:3, :] * cu)
        y = (b_ref[...].astype(F32) * cv).astype(BF16)
        y_ref[...] = y
        z = _dot(y, w_ref[...])
        z_ref[...] = z.astype(BF16)
        h_ref[...] = (r_ref[...] + z * _rms_r(z) * g_ref[...]).astype(STREAM)

    tile = lambda col: pl.BlockSpec((tm, D), lambda i: (i, col))
    halo = lambda col: pl.BlockSpec((BF16_ROWS, D), lambda i: (jnp.maximum(i * hb - 1, 0), col))
    row = pl.BlockSpec((tm, D), lambda i: (i, 0))
    return _call(
        body, name=name, grid=(T // tm,),
        in_specs=[tile(0), tile(1), tile(2), halo(1), halo(2),
                  pl.BlockSpec((3, D), lambda i: (0, 0)),
                  pl.BlockSpec((D, D), lambda i: (0, 0)),
                  pl.BlockSpec((1, D), lambda i: (0, 0)), row],
        out_specs=[row, row, row],
        out_shape=[SDS((T, D), STREAM), SDS((T, D), BF16), SDS((T, D), BF16)],
        semantics=("parallel",), args=(bcx, bcx, bcx, bcx, bcx, conv_w, w_out, g_post, res), ride=ride)


def _normbwd_then_nt(dh, zf, g_ref, w_ref, dz_ref, dg_ref, o_ref, first):
    dz, zh = _rmsnorm_bwd(zf, g_ref[...], dh)
    dz = dz.astype(BF16)
    dz_ref[...] = dz
    _accumulate(dg_ref, first, jnp.sum(dh * zh, axis=0, keepdims=True))
    o_ref[...] = _dot_nt(dz, w_ref[...]).astype(BF16)


def _then_specs(then, tm, T, D):
    z, g, w = then
    K = w.shape[0]
    row = pl.BlockSpec((tm, D), lambda i: (i, 0))
    vec = pl.BlockSpec((1, D), lambda i: (0, 0))
    in_specs = [row, vec, pl.BlockSpec((K, D), lambda i: (0, 0), pipeline_mode=pl.Buffered(1))]
    out_specs = [row, vec, pl.BlockSpec((tm, K), lambda i: (i, 0))]
    out_shape = [SDS((T, D), BF16), SDS((1, D), F32), SDS((T, K), BF16)]
    return in_specs, out_specs, out_shape


def plain_mix_out(a, w, g_post, res, *, name, target=None, ride=None, tm=ROW_TILE):
    T, D = res.shape
    tm = min(tm, T)
    K = a.shape[1]
    with_loss = target is not None

    def body(a_ref, w_ref, g_ref, r_ref, *rest):
        z = _dot(a_ref[...], w_ref[...])
        h = r_ref[...].astype(F32) + z * _rms_r(z) * g_ref[...]
        if with_loss:
            t_ref, h_ref, dz_ref, dg_ref, da_ref, loss_ref = rest
            first = pl.program_id(0) == 0
            diff = h - t_ref[...]
            dh = diff * (1.0 / D)
            h_ref[...] = dh.astype(STREAM)
            part = jnp.full(loss_ref.shape, 0.5 / D, F32) * jnp.sum(diff * diff)
            _accumulate(loss_ref, first, part)
            _normbwd_then_nt(dh, z, g_ref, w_ref, dz_ref, dg_ref, da_ref, first)
        else:
            h_ref, z_ref = rest
            h_ref[...] = h.astype(STREAM)
            z_ref[...] = z.astype(BF16)

    row = pl.BlockSpec((tm, D), lambda i: (i, 0))
    vec = pl.BlockSpec((1, D), lambda i: (0, 0))
    in_specs = [pl.BlockSpec((tm, K), lambda i: (i, 0)), pl.BlockSpec((K, D), lambda i: (0, 0)), vec, row]
    if with_loss:
        in_specs.append(row)
        out_specs = [row, row, vec, pl.BlockSpec((tm, K), lambda i: (i, 0)), pl.BlockSpec((8, 128), lambda i: (0, 0))]
        out_shape = [SDS((T, D), STREAM), SDS((T, D), BF16), SDS((1, D), F32), SDS((T, K), BF16), SDS((8, 128), F32)]
    else:
        out_specs, out_shape = [row, row], [SDS((T, D), STREAM), SDS((T, D), BF16)]
    return _call(
        body, name=name, grid=(T // tm,), in_specs=in_specs, out_specs=out_specs, out_shape=out_shape,
        semantics=("arbitrary",), args=(a, w, g_post, res) + ((target,) if with_loss else ()), ride=ride)


def _silu_grads(d, g, u):
    sg = jax.nn.sigmoid(g)
    return d * u * (sg * (1.0 + g * (1.0 - sg))), d * (g * sg)


def norm_swiglu_in(x, g, w, *, name, ride=None, tm=ROW_TILE // 2):
    T, D = x.shape
    F = w.shape[1] // 2

    def body(x_ref, g_ref, wg_ref, wu_ref, gu_ref, a_ref, xt_ref):
        xf = x_ref[...].astype(F32)
        xn = xf * _rms_r(xf) * g_ref[...]
        xt_ref[...] = xn.T.astype(BF16)
        xb = xn.astype(BF16)
        gate = _dot(xb, wg_ref[...]).astype(BF16)
        up = _dot(xb, wu_ref[...]).astype(BF16)
        gu_ref[0] = gate
        gu_ref[1] = up
        a_ref[...] = gate * jax.nn.sigmoid(gate) * up

    half = lambda s: pl.BlockSpec((D, F), lambda i: (0, s), pipeline_mode=pl.Buffered(1))
    return _call(
        body, name=name, grid=(T // tm,),
        in_specs=[pl.BlockSpec((tm, D), lambda i: (i, 0)), pl.BlockSpec((1, D), lambda i: (0, 0)), half(0), half(1)],
        out_specs=[pl.BlockSpec((2, tm, F), lambda i: (0, i, 0)), pl.BlockSpec((tm, F), lambda i: (i, 0)),
                   pl.BlockSpec((D, tm), lambda i: (0, i))],
        out_shape=[SDS((2, T, F), BF16), SDS((T, F), BF16), SDS((D, T), BF16)],
        semantics=("parallel",), args=(x, g, w, w), ride=ride)


def swiglu_bwd_tn(xt, dact, gu, *, name, ride=None, tb=MXU_WIDTH):
    D, T = xt.shape
    F = dact.shape[1]

    def body(xt_ref, d_ref, g_ref, u_ref, o_ref):
        dg, du = _silu_grads(d_ref[...], g_ref[...], u_ref[...])
        o_ref[0] = _dot(xt_ref[...], dg).astype(BF16)
        o_ref[1] = _dot(xt_ref[...], du).astype(BF16)

    col = lambda s: pl.BlockSpec((None, T, tb), lambda j: (s, 0, j))
    out = _call(
        body, name=name, grid=(F // tb,),
        in_specs=[pl.BlockSpec((D, T), lambda j: (0, 0), pipeline_mode=pl.Buffered(1)),
                  pl.BlockSpec((T, tb), lambda j: (0, j)), col(0), col(1)],
        out_specs=[pl.BlockSpec((2, D, tb), lambda j: (0, 0, j))],
        out_shape=[SDS((2, D, F), BF16)],
        semantics=("parallel",), args=(xt, dact, gu, gu), ride=ride)
    return out[0] if ride is None else (out[0][0], out[1])


def swiglu_bwd_in(dact, gu, w, h_in, g, dh_out, then, *, name, ride=None, tm=ROW_TILE // 2):
    T, D = h_in.shape
    F = dact.shape[1]

    def body(d_ref, gg_ref, uu_ref, wg_ref, wu_ref, h_ref, g_ref, dh_ref, z_ref, g2_ref, w2_ref,
             o_ref, dg_ref, dz_ref, dg2_ref, da_ref):
        first = pl.program_id(0) == 0
        dgate, dup = _silu_grads(d_ref[...], gg_ref[...], uu_ref[...])
        dn = _dot_nt(dgate, wg_ref[...]) + _dot_nt(dup, wu_ref[...])
        dx, hh = _rmsnorm_bwd(h_ref[...].astype(F32), g_ref[...], dn)
        dh_in = dh_ref[...] + dx
        o_ref[...] = dh_in.astype(STREAM)
        _accumulate(dg_ref, first, jnp.sum(dn * hh, axis=0, keepdims=True))
        _normbwd_then_nt(dh_in, z_ref[...].astype(F32), g2_ref, w2_ref, dz_ref, dg2_ref, da_ref, first)

    row = pl.BlockSpec((tm, D), lambda i: (i, 0))
    vec = pl.BlockSpec((1, D), lambda i: (0, 0))
    part = lambda s: pl.BlockSpec((None, tm, F), lambda i: (s, i, 0))
    half = lambda s: pl.BlockSpec((D, F), lambda i: (0, s), pipeline_mode=pl.Buffered(1))
    then_in, then_out, then_shape = _then_specs(then, tm, T, D)
    return _call(
        body, name=name, grid=(T // tm,),
        in_specs=[pl.BlockSpec((tm, F), lambda i: (i, 0)), part(0), part(1), half(0), half(1), row, vec, row] + then_in,
        out_specs=[row, vec] + then_out,
        out_shape=[SDS((T, D), STREAM), SDS((1, D), F32)] + then_shape,
        semantics=("arbitrary",), args=(dact, gu, gu, w, w, h_in, g, dh_out) + tuple(then), ride=ride)


def rope_tables(T):
    half = ROT_DIM // 2
    inv_freq = ROPE_THETA ** (-jnp.arange(0, ROT_DIM, 2, dtype=F32) / ROT_DIM)
    ang = (jnp.arange(T, dtype=F32)[:, None] * inv_freq[None, :]).T
    cos, sin = jnp.cos(ang), jnp.sin(ang)
    rest = HEAD_DIM - ROT_DIM
    one, zero = jnp.ones((rest, T), F32), jnp.zeros((rest, T), F32)
    zh = jnp.zeros((half, T), F32)
    fac = jnp.concatenate([cos, cos, one], axis=0)
    up = jnp.concatenate([-sin, zh, zero], axis=0)
    down = jnp.concatenate([zh, sin, zero], axis=0)
    return jnp.stack([fac, up, down])


def _rope(t, tab):
    half = ROT_DIM // 2
    return t * tab[0] + pltpu.roll(t, HEAD_DIM - half, 0) * tab[1] + pltpu.roll(t, half, 0) * tab[2]


def _rope_t(d, tab):
    half = ROT_DIM // 2
    return d * tab[0] + pltpu.roll(d * tab[1], half, 0) + pltpu.roll(d * tab[2], HEAD_DIM - half, 0)


def _head(t, h):
    return t[h * HEAD_DIM:(h + 1) * HEAD_DIM]


def _band(n, group):
    kj = lax.broadcasted_iota(jnp.int32, (2 * BLOCK, BLOCK), 0)
    qi = lax.broadcasted_iota(jnp.int32, (2 * BLOCK, BLOCK), 1)
    mask = (kj > qi) & (kj <= qi + BLOCK) & ((n > 0) | (kj >= BLOCK))
    return jnp.tile(mask, (1, group))


def _attn_specs(D, kvd):
    prev = lambda n: jnp.maximum(n - 1, 0)
    return [pl.BlockSpec((BLOCK, D), lambda n: (n, 0)),
            pl.BlockSpec((BLOCK, kvd), lambda n: (prev(n), 0)),
            pl.BlockSpec((BLOCK, kvd), lambda n: (n, 0)),
            pl.BlockSpec((BLOCK, kvd), lambda n: (prev(n), 1)),
            pl.BlockSpec((BLOCK, kvd), lambda n: (n, 1)),
            pl.BlockSpec((3, HEAD_DIM, BLOCK), lambda n: (0, 0, prev(n))),
            pl.BlockSpec((3, HEAD_DIM, BLOCK), lambda n: (0, 0, n)),
            pl.BlockSpec(memory_space=pltpu.SMEM)]


def _attn_operands(q_ref, kp_ref, k_ref, vp_ref, v_ref, tp_ref, t_ref):
    flip = lambda ref: ref[...].astype(F32).T
    tab = t_ref[...]
    kt = jnp.concatenate([flip(kp_ref), flip(k_ref)], axis=1)
    vt = jnp.concatenate([flip(vp_ref), flip(v_ref)], axis=1)
    return flip(q_ref), kt, vt, tab, jnp.concatenate([tp_ref[...], tab], axis=2)


SCORE_SCALE = 1.0 / math.sqrt(HEAD_DIM)
HEADS_TOGETHER = 4


def _group_heads(t, first, count, tab=None):
    heads = [_head(t, first + g) for g in range(count)]
    if tab is not None:
        heads = [_rope(h, tab) * SCORE_SCALE for h in heads]
    return jnp.concatenate(heads, axis=1).astype(BF16)


def _sink_row(s_ref, first, count):
    which = lax.broadcasted_iota(jnp.int32, (1, count * BLOCK), 1) // BLOCK
    row = jnp.zeros((1, count * BLOCK), F32)
    for g in range(count):
        row = jnp.where(which == g, s_ref[0, first + g], row)
    return row


def _softmax(scores, sink, mask):
    s = jnp.where(mask, scores, NEG)
    m = jnp.maximum(jnp.max(s, axis=0, keepdims=True), sink)
    e = jnp.exp(s - m)
    es = jnp.exp(sink - m)
    return e, es, 1.0 / (jnp.sum(e, axis=0, keepdims=True) + es)


def attention_fwd(q, kv, tabs, sinks, *, name, ride=None):
    T, D = q.shape
    kvd = kv.shape[1] // 2
    group = D // HEAD_DIM // N_KV_HEADS

    def body(q_ref, kp_ref, k_ref, vp_ref, v_ref, tp_ref, t_ref, s_ref, o_ref):
        gs = HEADS_TOGETHER
        mask = _band(pl.program_id(0), gs)
        qt, kt, vt, tab, tab2 = _attn_operands(q_ref, kp_ref, k_ref, vp_ref, v_ref, tp_ref, t_ref)
        firsts = [(j, first) for j in range(N_KV_HEADS) for first in range(j * group, (j + 1) * group, gs)]
        ks = [_rope(_head(kt, j), tab2).astype(BF16) for j in range(N_KV_HEADS)]
        scores = [_dot_tn(ks[j], _group_heads(qt, first, gs, tab)) for j, first in firsts]
        soft = [_softmax(s, _sink_row(s_ref, first, gs), mask) for s, (j, first) in zip(scores, firsts)]
        outs = []
        for (e, _, inv), (j, first) in zip(soft, firsts):
            o = _dot(_head(vt, j).astype(BF16), e.astype(BF16)) * inv
            outs += [o[:, g * BLOCK:(g + 1) * BLOCK] for g in range(gs)]
        o_ref[...] = jnp.concatenate(outs, axis=0).T.astype(BF16)

    return _call(
        body, name=name, grid=(T // BLOCK,),
        in_specs=_attn_specs(D, kvd),
        out_specs=[pl.BlockSpec((BLOCK, D), lambda n: (n, 0))],
        out_shape=[SDS((T, D), BF16)],
        semantics=("parallel",), args=(q, kv, kv, kv, kv, tabs, tabs, sinks), ride=ride)


def attention_bwd(q, kv, tabs, sinks, do, *, name, ride=None):
    T, D = q.shape
    kvd = kv.shape[1] // 2
    heads = D // HEAD_DIM
    group = heads // N_KV_HEADS

    def body(q_ref, kp_ref, k_ref, vp_ref, v_ref, tp_ref, t_ref, s_ref, do_ref, dq_ref, dc_ref, dp_ref, ds_ref):
        n = pl.program_id(0)
        gs = HEADS_TOGETHER
        mask = _band(n, gs)
        qt, kt, vt, tab, tab2 = _attn_operands(q_ref, kp_ref, k_ref, vp_ref, v_ref, tp_ref, t_ref)
        dot = do_ref[...].astype(F32).T
        lane = lax.broadcasted_iota(jnp.int32, (8, 128), 1)
        dsink = jnp.zeros((8, 128), F32)
        firsts = [(j, first) for j in range(N_KV_HEADS) for first in range(j * group, (j + 1) * group, gs)]
        ks = [_rope(_head(kt, j), tab2).astype(BF16) for j in range(N_KV_HEADS)]
        vs = [_head(vt, j).astype(BF16) for j in range(N_KV_HEADS)]
        qs = [_group_heads(qt, first, gs, tab) for _, first in firsts]
        dos = [_group_heads(dot, first, gs) for _, first in firsts]
        scores = [_dot_tn(ks[j], q) for q, (j, _) in zip(qs, firsts)]
        dps = [_dot_tn(vs[j], do) for do, (j, _) in zip(dos, firsts)]
        ps, dscs = [], []
        for s, dp, (j, first) in zip(scores, dps, firsts):
            e, e_sink, inv = _softmax(s, _sink_row(s_ref, first, gs), mask)
            p = e * inv
            dl = jnp.sum(p * dp, axis=0, keepdims=True)
            dscs.append((p * (dp - dl)).astype(BF16))
            ps.append(p.astype(BF16))
            weight = e_sink * inv * dl
            for g in range(gs):
                dsink = dsink - jnp.where(lane == first + g, jnp.sum(weight[:, g * BLOCK:(g + 1) * BLOCK]), 0.0)
        dqs = []
        dks = [jnp.zeros((HEAD_DIM, 2 * BLOCK), F32) for _ in range(N_KV_HEADS)]
        dvs = [jnp.zeros((HEAD_DIM, 2 * BLOCK), F32) for _ in range(N_KV_HEADS)]
        for p, dsc, q, do, (j, _) in zip(ps, dscs, qs, dos, firsts):
            dq = _dot(ks[j], dsc) * SCORE_SCALE
            dqs += [_rope_t(dq[:, g * BLOCK:(g + 1) * BLOCK], tab) for g in range(gs)]
            dks[j] = dks[j] + _dot_nt(q, dsc)
            dvs[j] = dvs[j] + _dot_nt(do, p)
        dks = [_rope_t(dk, tab2) for dk in dks]
        dq_ref[...] = jnp.concatenate(dqs, axis=0).T.astype(BF16)
        dkv = jnp.concatenate(dks + dvs, axis=0)
        dp_ref[...] = dkv[:, :BLOCK].T
        dc_ref[...] = dkv[:, BLOCK:].T
        _accumulate(ds_ref, n == 0, dsink)

    blk = lambda w: pl.BlockSpec((BLOCK, w), lambda n: (n, 0))
    return _call(
        body, name=name, grid=(T // BLOCK,),
        in_specs=_attn_specs(D, kvd) + [blk(D)],
        out_specs=[blk(D), blk(2 * kvd), blk(2 * kvd), pl.BlockSpec((8, 128), lambda n: (0, 0))],
        out_shape=[SDS((T, D), BF16), SDS((T, 2 * kvd), F32), SDS((T, 2 * kvd), F32), SDS((8, 128), F32)],
        semantics=("arbitrary",), args=(q, kv, kv, kv, kv, tabs, tabs, sinks, do), ride=ride)


def combine_dkv(d_cur, d_prev, *, name):
    T, W = d_cur.shape
    tm = ROW_TILE
    nt, per, last = T // tm, tm // BLOCK, T // BLOCK - 1

    def body(c_ref, p_ref, pn_ref, o_ref):
        nxt = jnp.where(pl.program_id(0) == nt - 1, 0.0, pn_ref[...])
        o_ref[...] = (c_ref[...] + jnp.concatenate([p_ref[BLOCK:, :], nxt], axis=0)).astype(BF16)

    return _call(
        body, name=name, grid=(nt,),
        in_specs=[pl.BlockSpec((tm, W), lambda i: (i, 0)), pl.BlockSpec((tm, W), lambda i: (i, 0)),
                  pl.BlockSpec((BLOCK, W), lambda i: (jnp.minimum((i + 1) * per, last), 0))],
        out_specs=[pl.BlockSpec((tm, W), lambda i: (i, 0))],
        out_shape=[SDS((T, W), BF16)],
        semantics=("parallel",), args=(d_cur, d_prev, d_prev))[0]


def matmul_nt_normbwd(da, w, h_in, g, dh_out, *, name, ride=None, tm=ROW_TILE):
    T, D = h_in.shape
    S, _, K = da.shape

    def body(*refs):
        da_refs, w_refs = refs[:S], refs[S:2 * S]
        h_ref, g_ref, dh_ref, o_ref, dg_ref = refs[2 * S:]
        dn = _dot_nt(da_refs[0][...], w_refs[0][...])
        for s in range(1, S):
            dn = dn + _dot_nt(da_refs[s][...], w_refs[s][...])
        dx, hh = _rmsnorm_bwd(h_ref[...].astype(F32), g_ref[...], dn)
        o_ref[...] = dh_ref[...] + dx
        _accumulate(dg_ref, pl.program_id(0) == 0, jnp.sum(dn * hh, axis=0, keepdims=True))

    row = pl.BlockSpec((tm, D), lambda i: (i, 0))
    vec = pl.BlockSpec((1, D), lambda i: (0, 0))
    part = lambda s: pl.BlockSpec((None, tm, K), lambda i: (s, i, 0))
    cols = lambda s: pl.BlockSpec((D, K), lambda i: (0, s), pipeline_mode=pl.Buffered(1))
    return _call(
        body, name=name, grid=(T // tm,),
        in_specs=[part(s) for s in range(S)] + [cols(s) for s in range(S)] + [row, vec, row],
        out_specs=[row, vec],
        out_shape=[SDS((T, D), F32), SDS((1, D), F32)],
        semantics=("arbitrary",), args=[da] * S + [w] * S + [h_in, g, dh_out], ride=ride)


def matmuls_nt_normbwd(das, ws, h_in, gs, dh_out, then, *, name, ride=None, tm=ROW_TILE):
    T, D = h_in.shape
    tm = min(tm, T)
    n = len(das)

    def body(*refs):
        da_refs, w_refs, g_refs = refs[:n], refs[n:2 * n], refs[2 * n:3 * n]
        h_ref, dh_ref, z_ref, g2_ref, w2_ref, o_ref = refs[3 * n:3 * n + 6]
        dg_refs, (dz_ref, dg2_ref, da_ref) = refs[3 * n + 6:4 * n + 6], refs[4 * n + 6:]
        first = pl.program_id(0) == 0
        hf = h_ref[...].astype(F32)
        r = _rms_r(hf)
        hh = hf * r
        total = dh_ref[...].astype(F32)
        for da_ref_, w_ref, g_ref, dg_ref in zip(da_refs, w_refs, g_refs, dg_refs):
            dn = _dot_nt(da_ref_[...], w_ref[...])
            gd = g_ref[...] * dn
            total = total + r * (gd - hh * jnp.mean(hh * gd, axis=-1, keepdims=True))
            _accumulate(dg_ref, first, jnp.sum(dn * hh, axis=0, keepdims=True))
        o_ref[...] = total.astype(STREAM)
        _normbwd_then_nt(total, z_ref[...].astype(F32), g2_ref, w2_ref, dz_ref, dg2_ref, da_ref, first)

    row = pl.BlockSpec((tm, D), lambda i: (i, 0))
    vec = pl.BlockSpec((1, D), lambda i: (0, 0))
    then_in, then_out, then_shape = _then_specs(then, tm, T, D)
    return _call(
        body, name=name, grid=(T // tm,),
        in_specs=[pl.BlockSpec((tm, da.shape[1]), lambda i: (i, 0)) for da in das]
        + [pl.BlockSpec(w.shape, lambda i: (0, 0)) for w in ws] + [vec] * n + [row, row] + then_in,
        out_specs=[row] + [vec] * n + then_out,
        out_shape=[SDS((T, D), STREAM)] + [SDS((1, D), F32)] * n + then_shape,
        semantics=("arbitrary",), args=list(das) + list(ws) + list(gs) + [h_in, dh_out] + list(then), ride=ride)


def matmul_tn(a, b, *, tb, name, ride=None, ta=MXU_WIDTH):
    T, Ka = a.shape
    S, _, Nb = b.shape
    per = Nb // tb

    def body(a_ref, b_ref, o_ref):
        o_ref[...] = _dot_tn(a_ref[...], b_ref[...]).astype(BF16)

    out = _call(
        body, name=name, grid=(S * per, Ka // ta),
        in_specs=[pl.BlockSpec((T, ta), lambda j, i: (0, i)),
                  pl.BlockSpec((None, T, tb), lambda j, i: (j // per, 0, j % per))],
        out_specs=[pl.BlockSpec((ta, tb), lambda j, i: (i, j))],
        out_shape=[SDS((Ka, S * Nb), BF16)],
        semantics=("parallel", "parallel"), args=(a, b), ride=ride)
    return out[0] if ride is None else (out[0][0], out[1])


def conv_bwd(dy, bcx, conv_w, *, name, ride=None, tm=ROW_TILE):
    T, D = dy.shape
    nt = T // tm
    hb = tm // BF16_ROWS
    last = T // BF16_ROWS - 1

    def body(dy_ref, dyn_ref, b_ref, bn_ref, c_ref, u_ref, cp_ref, up_ref, cw_ref, o_ref, dw_ref):
        i = pl.program_id(0)
        c, u = c_ref[...].astype(F32), u_ref[...].astype(F32)
        cu = c * u
        cup = jnp.where(i == 0, 0.0, cp_ref[...].astype(F32) * up_ref[...].astype(F32))
        cu1, cu2 = _shift_down(cup, cu, 1), _shift_down(cup, cu, 2)
        w0, w1, w2 = cw_ref[0:1, :], cw_ref[1:2, :], cw_ref[2:3, :]
        dyf = dy_ref[...].astype(F32)
        o_ref[:, 0:D] = (dyf * (w0 * cu2 + w1 * cu1 + w2 * cu)).astype(BF16)
        dcv = dyf * b_ref[...].astype(F32)
        dcvn = jnp.where(i == nt - 1, 0.0, dyn_ref[...].astype(F32) * bn_ref[...].astype(F32))
        dcu = w2 * dcv + w1 * _shift_up(dcv, dcvn, 1) + w0 * _shift_up(dcv, dcvn, 2)
        o_ref[:, D:2 * D] = (dcu * u).astype(BF16)
        o_ref[:, 2 * D:3 * D] = (dcu * c).astype(BF16)
        row = lax.broadcasted_iota(jnp.int32, (8, D), 0)
        dw = jnp.zeros((8, D), F32)
        for tap, t in enumerate((cu2, cu1, cu)):
            dw = jnp.where(row == tap, jnp.sum(dcv * t, axis=0, keepdims=True), dw)
        _accumulate(dw_ref, i == 0, dw)

    tile = lambda col: pl.BlockSpec((tm, D), lambda i: (i, col))
    prev = lambda col: pl.BlockSpec((BF16_ROWS, D), lambda i: (jnp.maximum(i * hb - 1, 0), col))
    nxt = lambda col: pl.BlockSpec((BF16_ROWS, D), lambda i: (jnp.minimum((i + 1) * hb, last), col))
    return _call(
        body, name=name, grid=(nt,),
        in_specs=[tile(0), nxt(0), tile(0), nxt(0), tile(1), tile(2), prev(1), prev(2),
                  pl.BlockSpec((3, D), lambda i: (0, 0))],
        out_specs=[pl.BlockSpec((tm, 3 * D), lambda i: (i, 0)), pl.BlockSpec((8, D), lambda i: (0, 0))],
        out_shape=[SDS((T, 3 * D), BF16), SDS((8, D), F32)],
        semantics=("arbitrary",), args=(dy, dy, bcx, bcx, bcx, bcx, bcx, bcx, conv_w), ride=ride)


class NoTraffic:
    def ride(self, kernel_name):
        return None

    def landed(self, kernel_name, results, wts):
        pass

    def grad(self, key, value):
        pass


def local_step(x, target, wts, vec, traffic):
    T, D = x.shape
    tabs = rope_tables(T)
    small = {}

    def run(builder, *args, name, **kw):
        ride = traffic.ride(name)
        if ride is None:
            return builder(*args, name=name, **kw)
        out, extra = builder(*args, name=name, ride=ride, **kw)
        traffic.landed(name, extra, wts)
        return out

    bcx, xn1 = run(norm_matmul, x, vec["a_pre"], wts["w_in"], tn=3 * D, split=1, name="a_in")
    bcx = bcx[0]
    h1, z0, y0 = run(conv_mix_out, bcx, vec["conv_w"], wts["w_out"], vec["a_post"], x, name="a_out")
    gu0, act0, xt2 = run(norm_swiglu_in, h1, vec["ffn_pre0"], wts["gu0"], name="ffn0_in")
    h2, z1 = run(plain_mix_out, act0, wts["wd0"], vec["ffn_post0"], h1, name="ffn0_out")
    kvp, xkv, qp, xq = norm2_matmul(h2, [vec["kv_norm"], vec["b_pre"]], [wts["w_kv"], wts["w_q"]], name="kvq_in")
    (attn,) = run(attention_fwd, qp, kvp, tabs, vec["sinks"], name="attn_fwd")
    h3, z2 = plain_mix_out(attn, wts["w_o"], vec["b_post"], h2, name="attn_out", tm=BIG_ROW_TILE)
    gu1, act1, xt3 = run(norm_swiglu_in, h3, vec["ffn_pre1"], wts["gu1"], name="ffn1_in")
    dy, dz3, small["ffn_post1"], dact1, loss = plain_mix_out(act1, wts["wd1"], vec["ffn_post1"], h3, name="ffn1_out",
                                                             target=target)

    def ffn_bwd(layer, dz, dact, gu, act, xt, h_in, dh, then, gu_first):
        tag = "ffn%d" % layer
        dwd = lambda: traffic.grad("wd%d" % layer, run(matmul_tn, act, dz[None], tb=D, name=tag + "_dwd"))
        dwgu = lambda: traffic.grad("gu%d" % layer, run(swiglu_bwd_tn, xt, dact, gu, name=tag + "_dwgu"))
        for step in ((dwgu, dwd) if gu_first else (dwd, dwgu)):
            step()
        dh_in, small["ffn_pre%d" % layer], dz_, dg_, da_ = run(
            swiglu_bwd_in, dact, gu, wts["gu%d" % layer], h_in, vec["ffn_pre%d" % layer], dh, then,
            name=tag + "_in_bwd")
        return dh_in, dz_, dg_, da_

    dh3, dz2, small["b_post"], dattn = ffn_bwd(1, dz3, dact1, gu1, act1, xt3, h3, dy,
                                               (z2, vec["b_post"], wts["w_o"]), gu_first=False)
    traffic.grad("w_o", matmul_tn(attn, dz2[None], tb=D, name="attn_dwo"))
    dq, dkv_cur, dkv_prev, small["sinks"] = run(attention_bwd, qp, kvp, tabs, vec["sinks"], dattn, name="attn_bwd")
    dkv = combine_dkv(dkv_cur, dkv_prev, name="attn_dkv")
    traffic.grad("w_q", matmul_tn(xq, dq[None], tb=D, name="attn_dwq"))
    traffic.grad("w_kv", matmul_tn(xkv, dkv[None], tb=dkv.shape[1], name="attn_dwkv"))
    dh2, small["b_pre"], small["kv_norm"], dz1, small["ffn_post0"], dact0 = run(
        matmuls_nt_normbwd, [dq, dkv], [wts["w_q"], wts["w_kv"]], h2, [vec["b_pre"], vec["kv_norm"]], dh3,
        (z1, vec["ffn_post0"], wts["wd0"]), name="qkv_in_bwd")
    dh1, dz0, small["a_post"], dyc = ffn_bwd(0, dz1, dact0, gu0, act0, xt2, h1, dh2,
                                             (z0, vec["a_post"], wts["w_out"]), gu_first=True)
    traffic.grad("w_out", run(matmul_tn, y0, dz0[None], tb=D, name="a_dwout"))
    dbcx, small["conv_w"] = run(conv_bwd, dyc, bcx, vec["conv_w"], name="a_conv_bwd")
    traffic.grad("w_in", run(matmul_tn, xn1, dbcx[None], tb=3 * D // 2, name="a_dwin"))
    dx, small["a_pre"] = run(matmul_nt_normbwd, dbcx[None], wts["w_in"], x, vec["a_pre"], dh1, name="a_in_bwd",
                             tm=ROW_TILE // 2)
    return loss, dx, small


SMALL_ROWS = 16
LOSS_ROW = 13

WHOLE = None
GATHER_PLAN = {"cast_rest": [("w_in", WHOLE)],
               "a_in": [("w_out", WHOLE), ("gu0", (0, 18))],
               "a_out": [("gu0", (18, 14))],
               "ffn0_in": [("wd0", WHOLE), ("w_kv", WHOLE), ("w_q", WHOLE), ("w_o", WHOLE)],
               "ffn0_out": [("gu1", (0, 16))],
               "attn_fwd": [("gu1", (16, 16))],
               "ffn1_in": [("wd1", WHOLE)]}
PAIR_PLAN = {"ffn1_dwgu": ["wd1"], "ffn1_in_bwd": ["gu1"], "attn_bwd": ["w_o"], "qkv_in_bwd": ["w_q", "w_kv"],
             "ffn0_dwd": ["gu0"], "ffn0_in_bwd": ["wd0"], "a_conv_bwd": ["w_out"]}
PAIR_ALONE = ["w_in"]
CHIP_PLAN = {"ffn1_in_bwd": [("wd1", WHOLE)], "attn_bwd": [("gu1", WHOLE)],
             "ffn0_dwgu": [("w_o", WHOLE), ("w_q", WHOLE), ("w_kv", WHOLE)],
             "ffn0_in_bwd": [("gu0", WHOLE)], "a_dwout": [("wd0", (0, 8))], "a_conv_bwd": [("wd0", (8, 14))],
             "a_in_bwd": [("w_out", WHOLE), ("w_in", WHOLE)]}
HALF_PLAN = {"a_dwin": ["gu0", "gu1", "wd0", "wd1", "w_kv", "w_q", "w_o"]}
GRAD_KIND = dict(KIND, gu0="split", gu1="split")


class Traffic:
    def __init__(self, wholes, quarter, c_arr, pc_arr):
        self.wholes, self.quarter, self.c_arr, self.pc_arr = wholes, quarter, c_arr, pc_arr
        self.views, self.sums, self.got = {}, {}, {}
        self.reduced = {}
        self.stages = {}

    def reduce(self, keys, name):
        return chip_reduce([self.sums[k] for k in keys], [self.got[k] for k in keys], [GRAD_KIND[k] for k in keys],
                           self.pc_arr, name=name)

    def ride(self, name, small=None):
        rides, stages = [], []
        if name in GATHER_PLAN:
            plan = GATHER_PLAN[name]
            rides.append(gather_ride([self.wholes[k] for k, _ in plan],
                                     [(KIND[k], self.quarter[k], part) for k, part in plan], small))
            stages.append(("gather", [k for k, _ in plan]))
        if name in CHIP_PLAN:
            plan = CHIP_PLAN[name]
            rides.append(chip_ride([self.sums[k] for k, _ in plan],
                                   [(GRAD_KIND[k], self.quarter[k], part) for k, part in plan],
                                   earlier=[self.got.get(k) for k, _ in plan]))
            stages.append(("chip", [k for k, _ in plan]))
        if name in PAIR_PLAN:
            keys = PAIR_PLAN[name]
            rides.append(pair_ride([self.views[k] for k in keys]))
            stages.append(("pair", keys))
        if name in HALF_PLAN:
            keys = HALF_PLAN[name]
            rides.append(half_ride(self.reduce(keys, "chip_reduce_early")))
            stages.append(("half", keys))
        self.stages[name] = stages
        return join(rides)

    def landed(self, name, results, wts):
        results = list(results)
        for stage, keys in self.stages[name]:
            mine, results = results[:len(keys)], results[len(keys):]
            if stage == "gather":
                for k, whole in zip(keys, mine):
                    self.wholes[k] = wts[k] = whole
            elif stage == "chip":
                self.got.update(zip(keys, mine))
            elif stage == "half":
                self.reduced.update(zip(keys, mine))
            else:
                for k, got in zip(keys, mine):
                    self.sums[k] = pair_add(self.views[k], got, self.c_arr, name="pair_add_" + k)

    def grad(self, key, value):
        r, ws = self.quarter[key]
        view = {"row": (N_CHIPS, 2, r // 2, ws), "col": (1, 2, r // 2, N_CHIPS * ws), "split": (2, 2, r // 2, 2 * ws)}
        self.views[key] = value.reshape(view[GRAD_KIND[key]])
        if key in PAIR_ALONE:
            (got,) = alone(pair_ride([self.views[key]]), name="pair_exchange_" + key)
            self.sums[key] = pair_add(self.views[key], got, self.c_arr, name="pair_add_" + key)


def kernel(x, a_pre_norm, a_w_in, a_conv_w, a_w_out, a_post_norm, ffn_pre_norm, ffn_w_gate_up, ffn_w_down, ffn_post_norm, kv_norm, w_kv, b_pre_norm, b_w_q, b_sinks, b_w_o, b_post_norm, loss_target, m_a_pre_norm, m_a_w_in, m_a_conv_w, m_a_w_out, m_a_post_norm, m_ffn_pre_norm, m_ffn_w_gate_up, m_ffn_w_down, m_ffn_post_norm, m_kv_norm, m_w_kv, m_b_pre_norm, m_b_w_q, m_b_sinks, m_b_w_o, m_b_post_norm, v_a_pre_norm, v_a_w_in, v_a_conv_w, v_a_w_out, v_a_post_norm, v_ffn_pre_norm, v_ffn_w_gate_up, v_ffn_w_down, v_ffn_post_norm, v_kv_norm, v_w_kv, v_b_pre_norm, v_b_w_q, v_b_sinks, v_b_w_o, v_b_post_norm):
    T, D = x.shape[1], x.shape[2]
    xi, yi, ci = _place()
    p = 2 * xi + yi
    p_arr = jnp.reshape(p, (1,)).astype(jnp.int32)
    c_arr = jnp.reshape(ci, (1,)).astype(jnp.int32)
    pc_arr = jnp.stack([p, ci]).astype(jnp.int32)
    me_arr = jnp.reshape(4 * xi + 2 * yi + ci, (1,)).astype(jnp.int32)
    qd = D // N_CHIPS

    big = {"w_in": (a_w_in, 0), "w_out": (a_w_out, 0), "gu0": (ffn_w_gate_up, 0), "gu1": (ffn_w_gate_up, 1),
           "wd0": (ffn_w_down, 0), "wd1": (ffn_w_down, 1), "w_kv": (w_kv[None], 0), "w_q": (b_w_q, 0),
           "w_o": (b_w_o, 0)}
    names = list(big)
    quarter = {k: w.shape[1:] for k, (w, _) in big.items()}
    source = lambda k: big[k] + (KIND[k],)
    traffic = Traffic(dict(zip(names[:1], cast_quarters([source(names[0])], p_arr, name="cast_first"))), quarter,
                      c_arr, pc_arr)
    small_shard = jnp.concatenate([a_pre_norm, a_post_norm, a_conv_w[0], jnp.zeros((3, qd), F32)], axis=0)
    wts = {}
    rest, (*landed, small_full) = cast_quarters([source(k) for k in names[1:]], p_arr, name="cast_rest",
                                                ride=traffic.ride("cast_rest", small_shard))
    traffic.wholes.update(zip(names[1:], rest))
    traffic.landed("cast_rest", landed, wts)
    rows = lambda k: jnp.transpose(small_full[:, k], (1, 0, 2)).reshape(-1, D)
    vec = {"a_pre": rows(slice(0, 1)), "a_post": rows(slice(1, 2)), "conv_w": rows(slice(2, 5)),
           "ffn_pre0": ffn_pre_norm[0:1], "ffn_pre1": ffn_pre_norm[1:2],
           "ffn_post0": ffn_post_norm[0:1], "ffn_post1": ffn_post_norm[1:2],
           "kv_norm": kv_norm[None], "b_pre": b_pre_norm, "b_post": b_post_norm, "sinks": b_sinks}

    loss, dx, small = local_step(x[0], loss_target[0], wts, vec, traffic)

    pad = lambda a: jnp.pad(a, ((0, 0), (0, D - a.shape[1])))
    small_block = jnp.concatenate(
        [small["a_pre"], small["a_post"], small["conv_w"][0:3], small["ffn_pre0"], small["ffn_pre1"],
         small["ffn_post0"], small["ffn_post1"], small["kv_norm"], small["b_pre"], small["b_post"],
         pad(small["sinks"][0:1]), pad(loss[0:1]), jnp.zeros((SMALL_ROWS - LOSS_ROW - 1, D), F32)], axis=0)
    late = [k for k in names if k not in traffic.reduced]
    *swapped, small_blocks = alone(join([half_ride(traffic.reduce(late, "chip_reduce_late")),
                                         chip_ride([], [], small_block)]), name="last_exchange")
    traffic.reduced.update(zip(late, swapped))
    grad = {k: traffic.reduced[k].reshape(quarter[k]) for k in names}
    small_sum = small_reduce(small_blocks, me_arr)

    out = {}
    out["a_w_in"] = adamw(a_w_in, [grad["w_in"]], m_a_w_in, v_a_w_in, name="adamw_a_w_in")
    out["a_w_out"] = adamw(a_w_out, [grad["w_out"]], m_a_w_out, v_a_w_out, name="adamw_a_w_out")
    out["ffn_w_gate_up"] = adamw(ffn_w_gate_up, [grad["gu0"], grad["gu1"]], m_ffn_w_gate_up, v_ffn_w_gate_up,
                                 name="adamw_ffn_w_gate_up")
    out["ffn_w_down"] = adamw(ffn_w_down, [grad["wd0"], grad["wd1"]], m_ffn_w_down, v_ffn_w_down,
                              name="adamw_ffn_w_down")
    out["w_kv"] = [o[0] for o in adamw(w_kv[None], [grad["w_kv"]], m_w_kv[None], v_w_kv[None], name="adamw_w_kv")]
    out["b_w_q"] = adamw(b_w_q, [grad["w_q"]], m_b_w_q, v_b_w_q, name="adamw_b_w_q")
    out["b_w_o"] = adamw(b_w_o, [grad["w_o"]], m_b_w_o, v_b_w_o, name="adamw_b_w_o")

    def pack(a_pre, a_post, conv, ffn_pre, ffn_post, kvn, b_pre, b_post, sinks):
        return jnp.concatenate([pad(a_pre), pad(a_post), pad(conv[0]), ffn_pre, ffn_post, kvn[None], b_pre, b_post,
                                pad(sinks), jnp.zeros((SMALL_ROWS - 13, D), F32)], axis=0)

    g_small = jnp.concatenate([pad(lax.dynamic_slice(small_sum, (0, p * qd), (5, qd))), small_sum[5:]], axis=0)
    w_small = pack(a_pre_norm, a_post_norm, a_conv_w, ffn_pre_norm, ffn_post_norm, kv_norm, b_pre_norm, b_post_norm,
                   b_sinks)
    m_small = pack(m_a_pre_norm, m_a_post_norm, m_a_conv_w, m_ffn_pre_norm, m_ffn_post_norm, m_kv_norm,
                   m_b_pre_norm, m_b_post_norm, m_b_sinks)
    v_small = pack(v_a_pre_norm, v_a_post_norm, v_a_conv_w, v_ffn_pre_norm, v_ffn_post_norm, v_kv_norm,
                   v_b_pre_norm, v_b_post_norm, v_b_sinks)
    packed = adamw(w_small[None], [g_small], m_small[None], v_small[None], name="adamw_small")
    ns = b_sinks.shape[1]
    unpack = lambda a: {"a_pre_norm": a[0:1, :qd], "a_post_norm": a[1:2, :qd], "a_conv_w": a[None, 2:5, :qd],
                        "ffn_pre_norm": a[5:7], "ffn_post_norm": a[7:9], "kv_norm": a[9], "b_pre_norm": a[10:11],
                        "b_post_norm": a[11:12], "b_sinks": a[12:13, :ns]}
    unpacked = [unpack(a[0]) for a in packed]
    for k in unpacked[0]:
        out[k] = [u[k] for u in unpacked]

    order = ["a_pre_norm", "a_w_in", "a_conv_w", "a_w_out", "a_post_norm", "ffn_pre_norm", "ffn_w_gate_up",
             "ffn_w_down", "ffn_post_norm", "kv_norm", "w_kv", "b_pre_norm", "b_w_q", "b_sinks", "b_w_o",
             "b_post_norm"]
    return (small_sum[LOSS_ROW, 0], dx[None], *[out[k][0] for k in order], *[out[k][1] for k in order],
            *[out[k][2] for k in order], *[out[k][3] for k in order])
```

```python
import math

import jax
import jax.numpy as jnp
from jax import lax
from jax.experimental import pallas as pl
from jax.experimental.pallas import tpu as pltpu

F32 = jnp.float32
BF16 = jnp.bfloat16
SDS = jax.ShapeDtypeStruct
MESH = pl.DeviceIdType.MESH
DMA = pltpu.SemaphoreType.DMA
HBM_SPEC = pl.BlockSpec(memory_space=pltpu.HBM)

EPS = 1e-6
NEG = -1e30
HEAD_DIM = 64
N_KV_HEADS = 4
BLOCK = 128
ROT_DIM = HEAD_DIM // 4
ROPE_THETA = 500000.0
N_CHIPS = 4

ADAM_LR = 0.001
ADAM_B1 = 0.9
ADAM_B2 = 0.999
ADAM_EPS = 1e-08
ADAM_WD = 0.01
ADAM_STEP = 10

VMEM_LIMIT_BYTES = 52 * 1024 * 1024
ROW_TILE = 512
BF16_ROWS = 16
STREAM = BF16
MXU_WIDTH = 256

KIND = {"w_in": "col", "gu0": "col", "gu1": "col", "w_out": "row", "wd0": "row", "wd1": "row", "w_kv": "row",
        "w_q": "row", "w_o": "row"}


def _params(*semantics):
    return pltpu.CompilerParams(dimension_semantics=semantics, vmem_limit_bytes=VMEM_LIMIT_BYTES)


def _row_tile(rows, limit, step=8):
    return max(t for t in range(step, limit + 1, step) if rows % t == 0)


def _place():
    return lax.axis_index("x"), lax.axis_index("y"), lax.axis_index("c")


def _other_chips(x, y):
    return [(1 - x, y), (x, 1 - y), (1 - x, 1 - y)]


def _remote(src, dst, send_sem, recv_sem, to):
    return pltpu.make_async_remote_copy(src_ref=src, dst_ref=dst, send_sem=send_sem, recv_sem=recv_sem,
                                        device_id=to, device_id_type=MESH)


def _full_shape(kind, quarter):
    r, ws = quarter
    return (N_CHIPS * r, ws) if kind == "row" else (r, N_CHIPS * ws)


def _rows_of(h, part):
    lo, n = (0, h) if part is None else (part[0] * BF16_ROWS, part[1] * BF16_ROWS)
    assert lo + n <= h, (h, part)
    return lo, n


def _half_of_quarter(ref, kind, quarter, part, q, half):
    r, ws = quarter
    h = r // 2
    lo, n = _rows_of(h, part)
    if kind == "row":
        return ref.at[pl.ds(pl.multiple_of(q * r + half * h + lo, BF16_ROWS), n)]
    return ref.at[pl.ds(pl.multiple_of(half * h + lo, BF16_ROWS), n), pl.ds(pl.multiple_of(q * ws, 128), ws)]


class Ride:
    def __init__(self, operands, out_shape, aliases, sems, make):
        self.operands, self.out_shape, self.aliases, self.sems, self.make = operands, out_shape, aliases, sems, make


def join(rides):
    rides = [r for r in rides if r is not None]
    if len(rides) < 2:
        return rides[0] if rides else None
    aliases, at = {}, [0, 0, 0]
    cuts = []
    for r in rides:
        aliases.update({at[0] + i: at[1] + o for i, o in r.aliases.items()})
        cuts.append(tuple(at))
        at = [at[0] + len(r.operands), at[1] + len(r.out_shape), at[2] + len(r.sems)]
    cuts.append(tuple(at))

    def make(ins, outs, sem):
        made = [r.make(ins[lo[0]:hi[0]], outs[lo[1]:hi[1]], sem[lo[2]:hi[2]]) for r, lo, hi in zip(rides, cuts, cuts[1:])]

        def start():
            for s, _ in made:
                s()

        def finish():
            for _, f in made:
                f()

        return start, finish

    return Ride(sum((list(r.operands) for r in rides), []), sum((list(r.out_shape) for r in rides), []), aliases,
                sum((list(r.sems) for r in rides), []), make)


def _call(body, *, name, grid, in_specs, out_specs, out_shape, args, scratch_shapes=(), semantics=None, ride=None,
          prefetch=None):
    pre = 0 if prefetch is None else 1
    n_in, n_out, n_scr = len(in_specs), len(out_specs), len(scratch_shapes)
    r_in, r_out = (len(ride.operands), len(ride.out_shape)) if ride is not None else (0, 0)
    a, b = pre + n_in, pre + n_in + r_in
    c, d = b + n_out, b + n_out + r_out
    e = d + n_scr

    def riding(*refs):
        start, finish = ride.make(refs[a:b], refs[c:d], refs[e:])
        ids = [pl.program_id(k) for k in range(len(grid))]
        first, last = ids[0] == 0, ids[0] == grid[0] - 1
        for k in range(1, len(grid)):
            first, last = first & (ids[k] == 0), last & (ids[k] == grid[k] - 1)
        pl.when(first)(start)
        body(*refs[:a], *refs[b:c], *refs[d:e])
        pl.when(last)(finish)

    if ride is None:
        kernel_body, extra_in, extra_out, extra_shape, extra_scr, aliases = body, [], [], [], [], {}
        params = _params(*semantics)
    else:
        kernel_body, extra_in, extra_out = riding, [HBM_SPEC] * r_in, [HBM_SPEC] * r_out
        extra_shape, extra_scr = list(ride.out_shape), list(ride.sems)
        aliases = {pre + n_in + i: n_out + o for i, o in ride.aliases.items()}
        params = _params(*(("arbitrary",) * len(grid)))
    specs = dict(grid=grid, in_specs=list(in_specs) + extra_in, out_specs=list(out_specs) + extra_out,
                 scratch_shapes=list(scratch_shapes) + extra_scr)
    if prefetch is not None:
        specs = dict(grid_spec=pltpu.PrefetchScalarGridSpec(num_scalar_prefetch=1, **specs))
        args = (prefetch,) + tuple(args)
    outs = pl.pallas_call(kernel_body, name=name, out_shape=list(out_shape) + extra_shape,
                          input_output_aliases=aliases, compiler_params=params, **specs,
                          )(*args, *(ride.operands if ride is not None else ()))
    return outs if ride is None else (outs[:n_out], outs[n_out:])


def alone(ride, *, name):
    def body(*refs):
        n = len(ride.operands)
        start, finish = ride.make(refs[:n], refs[n:n + len(ride.out_shape)], refs[n + len(ride.out_shape):])
        start()
        finish()

    return pl.pallas_call(
        body, name=name, in_specs=[HBM_SPEC] * len(ride.operands), out_specs=[HBM_SPEC] * len(ride.out_shape),
        out_shape=list(ride.out_shape), input_output_aliases=dict(ride.aliases), scratch_shapes=list(ride.sems),
    )(*ride.operands)


def gather_ride(wholes, metas, small=None):
    n = len(wholes)
    operands, out_shape = list(wholes), [SDS(s.shape, s.dtype) for s in wholes]
    sems = [DMA((n, 3)), DMA((n, 3)), DMA((n, 3)), DMA((n, 3))]
    if small is not None:
        operands.append(small)
        out_shape.append(SDS((N_CHIPS,) + small.shape, small.dtype))
        sems += [DMA((3,)), DMA((3,)), DMA(())]

    def make(ins, outs, sem):
        send1, recv1, send2, recv2 = sem[:4]
        x, y, c = _place()
        p = 2 * x + y
        chips = _other_chips(x, y)
        me, sibling = (x, y, c), (x, y, 1 - c)
        part = lambda t, q, half: _half_of_quarter(outs[t], *metas[t], q, half)
        first = []
        for j, (qx, qy) in enumerate(chips):
            if small is not None:
                first.append(_remote(ins[n], outs[n].at[p], sem[4].at[j], sem[5].at[j], (qx, qy, c)))
            for t in range(n):
                first.append(_remote(part(t, p, c), part(t, p, c), send1.at[t, j], recv1.at[t, j], (qx, qy, c)))
        local = [] if small is None else [pltpu.make_async_copy(ins[n], outs[n].at[p], sem[6])]

        def start():
            for cp in local + first:
                cp.start()

        def finish():
            passed = []
            for j, (qx, qy) in enumerate(chips):
                q = 2 * qx + qy
                for t in range(n):
                    landed = part(t, q, c)
                    _remote(landed, landed, send1.at[t, j], recv1.at[t, j], me).wait_recv()
                    cp = _remote(landed, landed, send2.at[t, j], recv2.at[t, j], sibling)
                    cp.start()
                    passed.append(cp)
            for j, (qx, qy) in enumerate(chips):
                q = 2 * qx + qy
                if small is not None:
                    _remote(outs[n].at[q], outs[n].at[q], sem[4].at[j], sem[5].at[j], me).wait_recv()
                for t in range(n):
                    theirs = part(t, q, 1 - c)
                    _remote(theirs, theirs, send2.at[t, j], recv2.at[t, j], me).wait_recv()
            for cp in first + passed:
                cp.wait_send()
            for cp in local:
                cp.wait()

        return start, finish

    return Ride(operands, out_shape, {t: t for t in range(n)}, sems, make)


def chip_ride(sums, metas, small=None, earlier=None):
    n = len(sums)
    operands = list(sums)
    out_shape = [SDS((3, s.shape[1], quarter[1]), s.dtype) for s, (_, quarter, _) in zip(sums, metas)]
    sems = [DMA((n, 3)), DMA((n, 3))] if n else []
    if small is not None:
        operands.append(small)
        out_shape.append(SDS((8,) + small.shape, small.dtype))
        sems += [DMA((7,)), DMA((7,)), DMA(())]
    aliases = {}
    for t, buffer in enumerate(earlier or [None] * n):
        if buffer is not None:
            aliases[len(operands)] = t
            operands.append(buffer)

    def make(ins, outs, sem):
        x, y, c = _place()
        cps = []
        for j, (qx, qy) in enumerate(_other_chips(x, y)):
            q = 2 * qx + qy
            for t in range(n):
                kind, (_, ws), part = metas[t]
                rows = pl.ds(*_rows_of(ins[t].shape[1], part))
                if kind == "row":
                    src = ins[t].at[q, rows]
                elif kind == "col":
                    src = ins[t].at[0, rows, pl.ds(pl.multiple_of(q * ws, 128), ws)]
                else:
                    src = ins[t].at[q // 2, rows, pl.ds(pl.multiple_of((q % 2) * ws, 128), ws)]
                cps.append(_remote(src, outs[t].at[j, rows], sem[0].at[t, j], sem[1].at[t, j], (qx, qy, c)))
        local = []
        if small is not None:
            ssend, srecv, lsem = sem[2 * bool(n):2 * bool(n) + 3]
            local.append(pltpu.make_async_copy(ins[n], outs[n].at[0], lsem))
            for k in range(1, 8):
                peer = (x ^ (k >> 2 & 1), y ^ (k >> 1 & 1), c ^ (k & 1))
                cps.append(_remote(ins[n], outs[n].at[k], ssend.at[k - 1], srecv.at[k - 1], peer))

        def start():
            for cp in local + cps:
                cp.start()

        def finish():
            for cp in cps + local:
                cp.wait()

        return start, finish

    return Ride(operands, out_shape, aliases, sems, make)


def pair_ride(grads):
    n = len(grads)

    def make(ins, outs, sem):
        x, y, c = _place()
        cps = [_remote(ins[t].at[:, 1 - c], outs[t], sem[0].at[t], sem[1].at[t], (x, y, 1 - c)) for t in range(n)]

        def start():
            for cp in cps:
                cp.start()

        def finish():
            for cp in cps:
                cp.wait()

        return start, finish

    return Ride(list(grads), [SDS((g.shape[0],) + g.shape[2:], g.dtype) for g in grads], {}, [DMA((n,)), DMA((n,))],
                make)


def half_ride(quarters):
    n = len(quarters)

    def make(ins, outs, sem):
        x, y, c = _place()
        sends = [_remote(outs[t].at[c], outs[t].at[c], sem[0].at[t], sem[1].at[t], (x, y, 1 - c)) for t in range(n)]

        def start():
            for cp in sends:
                cp.start()

        def finish():
            for t in range(n):
                theirs = outs[t].at[1 - c]
                _remote(theirs, theirs, sem[0].at[t], sem[1].at[t], (x, y, c)).wait_recv()
            for cp in sends:
                cp.wait_send()

        return start, finish

    return Ride(list(quarters), [SDS(q.shape, q.dtype) for q in quarters], {t: t for t in range(n)},
                [DMA((n,)), DMA((n,))], make)


CAST_STEPS = 4


def cast_quarters(sources, p_arr, *, name, ride=None):
    n = len(sources)
    in_specs, out_specs, out_shape = [], [], []
    for w, layer, kind in sources:
        _, r, ws = w.shape
        tr = r // CAST_STEPS
        assert tr % BF16_ROWS == 0, w.shape
        in_specs.append(pl.BlockSpec((None, tr, ws), lambda i, p_ref, layer=layer: (layer, i, 0)))
        out_specs.append(pl.BlockSpec((tr, ws), (lambda i, p_ref: (p_ref[0] * CAST_STEPS + i, 0)) if kind == "row"
                                      else (lambda i, p_ref: (i, p_ref[0]))))
        out_shape.append(SDS(_full_shape(kind, (r, ws)), BF16))

    def body(p_ref, *refs):
        for w_ref, o_ref in zip(refs[:n], refs[n:]):
            o_ref[...] = w_ref[...].astype(BF16)

    return _call(body, name=name, grid=(CAST_STEPS,), in_specs=in_specs, out_specs=out_specs, out_shape=out_shape,
                 semantics=("parallel",), args=[w for w, _, _ in sources], ride=ride, prefetch=p_arr)


def pair_add(own, got, c_arr, *, name):
    A, _, h, W = own.shape
    th = _row_tile(h, max(BF16_ROWS, (3 << 19) // W), BF16_ROWS)

    def body(c_ref, a_ref, b_ref, o_ref):
        o_ref[...] = (a_ref[...].astype(F32) + b_ref[...].astype(F32)).astype(BF16)

    return pl.pallas_call(
        body, name=name,
        grid_spec=pltpu.PrefetchScalarGridSpec(
            num_scalar_prefetch=1, grid=(A, h // th),
            in_specs=[pl.BlockSpec((None, None, th, W), lambda q, i, c_ref: (q, c_ref[0], i, 0)),
                      pl.BlockSpec((None, th, W), lambda q, i, c_ref: (q, i, 0))],
            out_specs=pl.BlockSpec((None, th, W), lambda q, i, c_ref: (q, i, 0))),
        out_shape=SDS((A, h, W), BF16),
        compiler_params=_params("parallel", "parallel"),
    )(c_arr, own, got)


REDUCE_STEPS = 2


def chip_reduce(sums, got, kinds, pc_arr, *, name):
    n = len(sums)
    mine = {"row": lambda i, pc_ref: (pc_ref[0], i, 0), "col": lambda i, pc_ref: (0, i, pc_ref[0]),
            "split": lambda i, pc_ref: (pc_ref[0] // 2, i, pc_ref[0] % 2)}
    a_specs, b_specs, o_specs, out_shape = [], [], [], []
    for g, kind in zip(got, kinds):
        _, h, ws = g.shape
        th = h // REDUCE_STEPS
        assert th % BF16_ROWS == 0, g.shape
        a_specs.append(pl.BlockSpec((None, th, ws), mine[kind]))
        b_specs.append(pl.BlockSpec((3, th, ws), lambda i, pc_ref: (0, i, 0)))
        o_specs.append(pl.BlockSpec((None, th, ws), lambda i, pc_ref: (pc_ref[1], i, 0)))
        out_shape.append(SDS((2, h, ws), F32))

    def body(pc_ref, *refs):
        for a_ref, b_ref, o_ref in zip(refs[:n], refs[n:2 * n], refs[2 * n:]):
            o_ref[...] = ((a_ref[...].astype(F32) + b_ref[0].astype(F32)) + b_ref[1].astype(F32)) + b_ref[2].astype(F32)

    return _call(body, name=name, grid=(REDUCE_STEPS,), in_specs=a_specs + b_specs, out_specs=o_specs,
                 out_shape=out_shape, semantics=("parallel",), args=list(sums) + list(got), prefetch=pc_arr)


def small_reduce(blocks, me_arr):
    _, rows, D = blocks.shape

    def body(me_ref, b_ref, o_ref):
        me = me_ref[0]
        total = b_ref[me]
        for d in range(1, 8):
            total = total + b_ref[d ^ me]
        o_ref[...] = total

    return pl.pallas_call(
        body, name="small_reduce",
        grid_spec=pltpu.PrefetchScalarGridSpec(
            num_scalar_prefetch=1, grid=(1,),
            in_specs=[pl.BlockSpec((8, rows, D), lambda i, me_ref: (0, 0, 0))],
            out_specs=pl.BlockSpec((rows, D), lambda i, me_ref: (0, 0))),
        out_shape=SDS((rows, D), F32),
        compiler_params=_params("arbitrary"),
    )(me_arr, blocks)


def adamw(w, gs, m, v, *, name):
    L, r, cols = w.shape
    tr = _row_tile(r, 256)
    nt = r // tr

    def body(*refs):
        w_ref, m_ref, v_ref = refs[:3]
        g_refs = refs[3:3 + L]
        g_out, d_out, m_out, v_out = refs[3 + L:]
        layer = pl.program_id(0)
        g = g_refs[0][...]
        for l in range(1, L):
            g = jnp.where(layer == l, g_refs[l][...], g)
        m_new = ADAM_B1 * m_ref[...] + (1.0 - ADAM_B1) * g
        v_new = ADAM_B2 * v_ref[...] + (1.0 - ADAM_B2) * (g * g)
        m_hat = m_new / (1.0 - ADAM_B1 ** ADAM_STEP)
        v_hat = v_new / (1.0 - ADAM_B2 ** ADAM_STEP)
        g_out[...] = g
        m_out[...] = m_new
        v_out[...] = v_new
        d_out[...] = -ADAM_LR * (m_hat / (jnp.sqrt(v_hat) + ADAM_EPS) + ADAM_WD * w_ref[...])

    full = pl.BlockSpec((None, tr, cols), lambda l, i: (l, i, 0))
    g_spec = lambda l0: pl.BlockSpec((tr, cols), lambda l, i: (jnp.where(l == l0, i, jnp.where(l < l0, 0, nt - 1)), 0))
    return pl.pallas_call(
        body, name=name, grid=(L, nt),
        in_specs=[full, full, full] + [g_spec(l0) for l0 in range(L)],
        out_specs=[full] * 4,
        out_shape=[SDS(w.shape, F32)] * 4,
        compiler_params=_params("arbitrary", "arbitrary"),
    )(w, m, v, *gs)


def _rms_r(xf):
    return lax.rsqrt(jnp.mean(xf * xf, axis=-1, keepdims=True) + EPS)


def _rmsnorm_bwd(xf, g, dy):
    r = _rms_r(xf)
    xh = xf * r
    gd = g * dy
    return r * (gd - xh * jnp.mean(xh * gd, axis=-1, keepdims=True)), xh


def _dot(a, b):
    return jnp.dot(a, b, preferred_element_type=F32)


def _dot_nt(a, b):
    return lax.dot_general(a, b, (((1,), (1,)), ((), ())), preferred_element_type=F32)


def _dot_tn(a, b):
    return lax.dot_general(a, b, (((0,), (0,)), ((), ())), preferred_element_type=F32)


def _accumulate(ref, first, value):
    @pl.when(first)
    def _():
        ref[...] = value

    @pl.when(jnp.logical_not(first))
    def _():
        ref[...] += value


def norm_matmul(x, g, w, *, tn, split, name, ride=None, tm=ROW_TILE):
    T, D = x.shape
    N = w.shape[1]
    per = N // split // tn

    def body(x_ref, g_ref, w_ref, o_ref, xn_ref):
        @pl.when(pl.program_id(1) == 0)
        def _():
            xf = x_ref[...].astype(F32)
            xn_ref[...] = (xf * _rms_r(xf) * g_ref[...]).astype(BF16)

        o_ref[...] = _dot(xn_ref[...], w_ref[...]).astype(BF16)

    return _call(
        body, name=name, grid=(T // tm, N // tn),
        in_specs=[pl.BlockSpec((tm, D), lambda i, j: (i, 0)),
                  pl.BlockSpec((1, D), lambda i, j: (0, 0)),
                  pl.BlockSpec((D, tn), lambda i, j: (0, j))],
        out_specs=[pl.BlockSpec((None, tm, tn), lambda i, j: (j // per, i, j % per)),
                   pl.BlockSpec((tm, D), lambda i, j: (i, 0))],
        out_shape=[SDS((split, T, N // split), BF16), SDS((T, D), BF16)],
        semantics=("parallel", "arbitrary"), args=(x, g, w), ride=ride)


BIG_ROW_TILE = 1024


def norm2_matmul(x, gains, weights, *, name, tm=BIG_ROW_TILE):
    T, D = x.shape
    tm = min(tm, T)
    n = len(gains)

    def body(x_ref, *refs):
        xf = x_ref[...].astype(F32)
        xh = xf * _rms_r(xf)
        for g_ref, w_ref, o_ref, xn_ref in zip(refs[:n], refs[n:2 * n], refs[2 * n::2], refs[2 * n + 1::2]):
            xn = (xh * g_ref[...]).astype(BF16)
            xn_ref[...] = xn
            o_ref[...] = _dot(xn, w_ref[...]).astype(BF16)

    row = pl.BlockSpec((tm, D), lambda i: (i, 0))
    vec = pl.BlockSpec((1, D), lambda i: (0, 0))
    out_specs, out_shape = [], []
    for w in weights:
        out_specs += [pl.BlockSpec((tm, w.shape[1]), lambda i: (i, 0)), row]
        out_shape += [SDS((T, w.shape[1]), BF16), SDS((T, D), BF16)]
    return _call(
        body, name=name, grid=(T // tm,),
        in_specs=[row] + [vec] * n + [pl.BlockSpec(w.shape, lambda i: (0, 0)) for w in weights],
        out_specs=out_specs, out_shape=out_shape, semantics=("parallel",), args=[x] + list(gains) + list(weights))


def _shift_down(prev, cur, by):
    big = jnp.concatenate([prev, cur], axis=0)
    return pltpu.roll(big, by, 0)[prev.shape[0]:]


def _shift_up(cur, nxt, by):
    big = jnp.concatenate([cur, nxt], axis=0)
    return pltpu.roll(big, big.shape[0] - by, 0)[:cur.shape[0]]


def conv_mix_out(bcx, conv_w, w_out, g_post, res, *, name, ride=None, tm=ROW_TILE):
    T, D = res.shape
    hb = tm // BF16_ROWS

    def body(b_ref, c_ref, u_ref, cp_ref, up_ref, cw_ref, w_ref, g_ref, r_ref, h_ref, z_ref, y_ref):
        i = pl.program_id(0)
        cu = c_ref[...].astype(F32) * u_ref[...].astype(F32)
        cup = cp_ref[...].astype(F32) * up_ref[...].astype(F32)
        cup = jnp.where(i == 0, 0.0, cup)
        cv = (cw_ref[0:1, :] * _shift_down(cup, cu, 2) + cw_ref[1:2, :] * _shift_down(cup, cu, 1)
              + cw_ref[2:3, :] * cu)
        y = (b_ref[...].astype(F32) * cv).astype(BF16)
        y_ref[...] = y
        z = _dot(y, w_ref[...])
        z_ref[...] = z.astype(BF16)
        h_ref[...] = (r_ref[...] + z * _rms_r(z) * g_ref[...]).astype(STREAM)

    tile = lambda col: pl.BlockSpec((tm, D), lambda i: (i, col))
    halo = lambda col: pl.BlockSpec((BF16_ROWS, D), lambda i: (jnp.maximum(i * hb - 1, 0), col))
    row = pl.BlockSpec((tm, D), lambda i: (i, 0))
    return _call(
        body, name=name, grid=(T // tm,),
        in_specs=[tile(0), tile(1), tile(2), halo(1), halo(2),
                  pl.BlockSpec((3, D), lambda i: (0, 0)),
                  pl.BlockSpec((D, D), lambda i: (0, 0)),
                  pl.BlockSpec((1, D), lambda i: (0, 0)), row],
        out_specs=[row, row, row],
        out_shape=[SDS((T, D), STREAM), SDS((T, D), BF16), SDS((T, D), BF16)],
        semantics=("parallel",), args=(bcx, bcx, bcx, bcx, bcx, conv_w, w_out, g_post, res), ride=ride)


def _normbwd_then_nt(dh, zf, g_ref, w_ref, dz_ref, dg_ref, o_ref, first):
    dz, zh = _rmsnorm_bwd(zf, g_ref[...], dh)
    dz = dz.astype(BF16)
    dz_ref[...] = dz
    _accumulate(dg_ref, first, jnp.sum(dh * zh, axis=0, keepdims=True))
    o_ref[...] = _dot_nt(dz, w_ref[...]).astype(BF16)


def _then_specs(then, tm, T, D):
    z, g, w = then
    K = w.shape[0]
    row = pl.BlockSpec((tm, D), lambda i: (i, 0))
    vec = pl.BlockSpec((1, D), lambda i: (0, 0))
    in_specs = [row, vec, pl.BlockSpec((K, D), lambda i: (0, 0), pipeline_mode=pl.Buffered(1))]
    out_specs = [row, vec, pl.BlockSpec((tm, K), lambda i: (i, 0))]
    out_shape = [SDS((T, D), BF16), SDS((1, D), F32), SDS((T, K), BF16)]
    return in_specs, out_specs, out_shape


def plain_mix_out(a, w, g_post, res, *, name, target=None, ride=None, tm=ROW_TILE):
    T, D = res.shape
    tm = min(tm, T)
    K = a.shape[1]
    with_loss = target is not None

    def body(a_ref, w_ref, g_ref, r_ref, *rest):
        z = _dot(a_ref[...], w_ref[...])
        h = r_ref[...].astype(F32) + z * _rms_r(z) * g_ref[...]
        if with_loss:
            t_ref, h_ref, dz_ref, dg_ref, da_ref, loss_ref = rest
            first = pl.program_id(0) == 0
            diff = h - t_ref[...]
            dh = diff * (1.0 / D)
            h_ref[...] = dh.astype(STREAM)
            part = jnp.full(loss_ref.shape, 0.5 / D, F32) * jnp.sum(diff * diff)
            _accumulate(loss_ref, first, part)
            _normbwd_then_nt(dh, z, g_ref, w_ref, dz_ref, dg_ref, da_ref, first)
        else:
            h_ref, z_ref = rest
            h_ref[...] = h.astype(STREAM)
            z_ref[...] = z.astype(BF16)

    row = pl.BlockSpec((tm, D), lambda i: (i, 0))
    vec = pl.BlockSpec((1, D), lambda i: (0, 0))
    in_specs = [pl.BlockSpec((tm, K), lambda i: (i, 0)), pl.BlockSpec((K, D), lambda i: (0, 0)), vec, row]
    if with_loss:
        in_specs.append(row)
        out_specs = [row, row, vec, pl.BlockSpec((tm, K), lambda i: (i, 0)), pl.BlockSpec((8, 128), lambda i: (0, 0))]
        out_shape = [SDS((T, D), STREAM), SDS((T, D), BF16), SDS((1, D), F32), SDS((T, K), BF16), SDS((8, 128), F32)]
    else:
        out_specs, out_shape = [row, row], [SDS((T, D), STREAM), SDS((T, D), BF16)]
    return _call(
        body, name=name, grid=(T // tm,), in_specs=in_specs, out_specs=out_specs, out_shape=out_shape,
        semantics=("arbitrary",), args=(a, w, g_post, res) + ((target,) if with_loss else ()), ride=ride)


def _silu_grads(d, g, u):
    sg = jax.nn.sigmoid(g)
    return d * u * (sg * (1.0 + g * (1.0 - sg))), d * (g * sg)


def norm_swiglu_in(x, g, w, *, name, ride=None, tm=ROW_TILE // 2):
    T, D = x.shape
    F = w.shape[1] // 2

    def body(x_ref, g_ref, wg_ref, wu_ref, gu_ref, a_ref, xt_ref):
        xf = x_ref[...].astype(F32)
        xn = xf * _rms_r(xf) * g_ref[...]
        xt_ref[...] = xn.T.astype(BF16)
        xb = xn.astype(BF16)
        gate = _dot(xb, wg_ref[...]).astype(BF16)
        up = _dot(xb, wu_ref[...]).astype(BF16)
        gu_ref[0] = gate
        gu_ref[1] = up
        a_ref[...] = gate * jax.nn.sigmoid(gate) * up

    half = lambda s: pl.BlockSpec((D, F), lambda i: (0, s), pipeline_mode=pl.Buffered(1))
    return _call(
        body, name=name, grid=(T // tm,),
        in_specs=[pl.BlockSpec((tm, D), lambda i: (i, 0)), pl.BlockSpec((1, D), lambda i: (0, 0)), half(0), half(1)],
        out_specs=[pl.BlockSpec((2, tm, F), lambda i: (0, i, 0)), pl.BlockSpec((tm, F), lambda i: (i, 0)),
                   pl.BlockSpec((D, tm), lambda i: (0, i))],
        out_shape=[SDS((2, T, F), BF16), SDS((T, F), BF16), SDS((D, T), BF16)],
        semantics=("parallel",), args=(x, g, w, w), ride=ride)


def swiglu_bwd_tn(xt, dact, gu, *, name, ride=None, tb=MXU_WIDTH):
    D, T = xt.shape
    F = dact.shape[1]

    def body(xt_ref, d_ref, g_ref, u_ref, o_ref):
        dg, du = _silu_grads(d_ref[...], g_ref[...], u_ref[...])
        o_ref[0] = _dot(xt_ref[...], dg).astype(BF16)
        o_ref[1] = _dot(xt_ref[...], du).astype(BF16)

    col = lambda s: pl.BlockSpec((None, T, tb), lambda j: (s, 0, j))
    out = _call(
        body, name=name, grid=(F // tb,),
        in_specs=[pl.BlockSpec((D, T), lambda j: (0, 0), pipeline_mode=pl.Buffered(1)),
                  pl.BlockSpec((T, tb), lambda j: (0, j)), col(0), col(1)],
        out_specs=[pl.BlockSpec((2, D, tb), lambda j: (0, 0, j))],
        out_shape=[SDS((2, D, F), BF16)],
        semantics=("parallel",), args=(xt, dact, gu, gu), ride=ride)
    return out[0] if ride is None else (out[0][0], out[1])


def swiglu_bwd_in(dact, gu, w, h_in, g, dh_out, then, *, name, ride=None, tm=ROW_TILE // 2):
    T, D = h_in.shape
    F = dact.shape[1]

    def body(d_ref, gg_ref, uu_ref, wg_ref, wu_ref, h_ref, g_ref, dh_ref, z_ref, g2_ref, w2_ref,
             o_ref, dg_ref, dz_ref, dg2_ref, da_ref):
        first = pl.program_id(0) == 0
        dgate, dup = _silu_grads(d_ref[...], gg_ref[...], uu_ref[...])
        dn = _dot_nt(dgate, wg_ref[...]) + _dot_nt(dup, wu_ref[...])
        dx, hh = _rmsnorm_bwd(h_ref[...].astype(F32), g_ref[...], dn)
        dh_in = dh_ref[...] + dx
        o_ref[...] = dh_in.astype(STREAM)
        _accumulate(dg_ref, first, jnp.sum(dn * hh, axis=0, keepdims=True))
        _normbwd_then_nt(dh_in, z_ref[...].astype(F32), g2_ref, w2_ref, dz_ref, dg2_ref, da_ref, first)

    row = pl.BlockSpec((tm, D), lambda i: (i, 0))
    vec = pl.BlockSpec((1, D), lambda i: (0, 0))
    part = lambda s: pl.BlockSpec((None, tm, F), lambda i: (s, i, 0))
    half = lambda s: pl.BlockSpec((D, F), lambda i: (0, s), pipeline_mode=pl.Buffered(1))
    then_in, then_out, then_shape = _then_specs(then, tm, T, D)
    return _call(
        body, name=name, grid=(T // tm,),
        in_specs=[pl.BlockSpec((tm, F), lambda i: (i, 0)), part(0), part(1), half(0), half(1), row, vec, row] + then_in,
        out_specs=[row, vec] + then_out,
        out_shape=[SDS((T, D), STREAM), SDS((1, D), F32)] + then_shape,
        semantics=("arbitrary",), args=(dact, gu, gu, w, w, h_in, g, dh_out) + tuple(then), ride=ride)


def rope_tables(T):
    half = ROT_DIM // 2
    inv_freq = ROPE_THETA ** (-jnp.arange(0, ROT_DIM, 2, dtype=F32) / ROT_DIM)
    ang = (jnp.arange(T, dtype=F32)[:, None] * inv_freq[None, :]).T
    cos, sin = jnp.cos(ang), jnp.sin(ang)
    rest = HEAD_DIM - ROT_DIM
    one, zero = jnp.ones((rest, T), F32), jnp.zeros((rest, T), F32)
    zh = jnp.zeros((half, T), F32)
    fac = jnp.concatenate([cos, cos, one], axis=0)
    up = jnp.concatenate([-sin, zh, zero], axis=0)
    down = jnp.concatenate([zh, sin, zero], axis=0)
    return jnp.stack([fac, up, down])


def _rope(t, tab):
    half = ROT_DIM // 2
    return t * tab[0] + pltpu.roll(t, HEAD_DIM - half, 0) * tab[1] + pltpu.roll(t, half, 0) * tab[2]


def _rope_t(d, tab):
    half = ROT_DIM // 2
    return d * tab[0] + pltpu.roll(d * tab[1], half, 0) + pltpu.roll(d * tab[2], HEAD_DIM - half, 0)


def _head(t, h):
    return t[h * HEAD_DIM:(h + 1) * HEAD_DIM]


def _band(n, group):
    kj = lax.broadcasted_iota(jnp.int32, (2 * BLOCK, BLOCK), 0)
    qi = lax.broadcasted_iota(jnp.int32, (2 * BLOCK, BLOCK), 1)
    mask = (kj > qi) & (kj <= qi + BLOCK) & ((n > 0) | (kj >= BLOCK))
    return jnp.tile(mask, (1, group))


def _attn_specs(D, kvd):
    prev = lambda n: jnp.maximum(n - 1, 0)
    return [pl.BlockSpec((BLOCK, D), lambda n: (n, 0)),
            pl.BlockSpec((BLOCK, kvd), lambda n: (prev(n), 0)),
            pl.BlockSpec((BLOCK, kvd), lambda n: (n, 0)),
            pl.BlockSpec((BLOCK, kvd), lambda n: (prev(n), 1)),
            pl.BlockSpec((BLOCK, kvd), lambda n: (n, 1)),
            pl.BlockSpec((3, HEAD_DIM, BLOCK), lambda n: (0, 0, prev(n))),
            pl.BlockSpec((3, HEAD_DIM, BLOCK), lambda n: (0, 0, n)),
            pl.BlockSpec(memory_space=pltpu.SMEM)]


def _attn_operands(q_ref, kp_ref, k_ref, vp_ref, v_ref, tp_ref, t_ref):
    flip = lambda ref: ref[...].astype(F32).T
    tab = t_ref[...]
    kt = jnp.concatenate([flip(kp_ref), flip(k_ref)], axis=1)
    vt = jnp.concatenate([flip(vp_ref), flip(v_ref)], axis=1)
    return flip(q_ref), kt, vt, tab, jnp.concatenate([tp_ref[...], tab], axis=2)


SCORE_SCALE = 1.0 / math.sqrt(HEAD_DIM)
HEADS_TOGETHER = 4


def _group_heads(t, first, count, tab=None):
    heads = [_head(t, first + g) for g in range(count)]
    if tab is not None:
        heads = [_rope(h, tab) * SCORE_SCALE for h in heads]
    return jnp.concatenate(heads, axis=1).astype(BF16)


def _sink_row(s_ref, first, count):
    which = lax.broadcasted_iota(jnp.int32, (1, count * BLOCK), 1) // BLOCK
    row = jnp.zeros((1, count * BLOCK), F32)
    for g in range(count):
        row = jnp.where(which == g, s_ref[0, first + g], row)
    return row


def _sum_keys(t):
    return _dot(jnp.ones((8, t.shape[0]), BF16), t)[0:1]


def _softmax(scores, sink, mask):
    s = jnp.where(mask, scores.astype(BF16), NEG)
    m = jnp.maximum(jnp.max(s, axis=0, keepdims=True).astype(F32), sink).astype(BF16)
    e = jnp.exp(s - m)
    m = m.astype(F32)
    return e, m, 1.0 / (_sum_keys(e) + jnp.exp(sink - m))


def _per_head(row, count):
    return [row[:, g * BLOCK:(g + 1) * BLOCK] for g in range(count)]


def attention_fwd(q, kv, tabs, sinks, *, name, ride=None):
    T, D = q.shape
    kvd = kv.shape[1] // 2
    heads = D // HEAD_DIM
    group = heads // N_KV_HEADS

    def body(q_ref, kp_ref, k_ref, vp_ref, v_ref, tp_ref, t_ref, s_ref, o_ref, stat_ref):
        gs = HEADS_TOGETHER
        mask = _band(pl.program_id(0), gs)
        qt, kt, vt, tab, tab2 = _attn_operands(q_ref, kp_ref, k_ref, vp_ref, v_ref, tp_ref, t_ref)
        firsts = [(j, first) for j in range(N_KV_HEADS) for first in range(j * group, (j + 1) * group, gs)]
        ks = [_rope(_head(kt, j), tab2).astype(BF16) for j in range(N_KV_HEADS)]
        scores = [_dot_tn(ks[j], _group_heads(qt, first, gs, tab)) for j, first in firsts]
        soft = [_softmax(s, _sink_row(s_ref, first, gs), mask) for s, (j, first) in zip(scores, firsts)]
        outs, ms, invs = [], [], []
        for (e, m, inv), (j, first) in zip(soft, firsts):
            o = _dot(_head(vt, j).astype(BF16), e) * inv
            outs += [o[:, g * BLOCK:(g + 1) * BLOCK] for g in range(gs)]
            ms += _per_head(m, gs)
            invs += _per_head(inv, gs)
        o_ref[...] = jnp.concatenate(outs, axis=0).T.astype(BF16)
        stat_ref[0] = jnp.concatenate(ms, axis=0)
        stat_ref[1] = jnp.concatenate(invs, axis=0)

    return _call(
        body, name=name, grid=(T // BLOCK,),
        in_specs=_attn_specs(D, kvd),
        out_specs=[pl.BlockSpec((BLOCK, D), lambda n: (n, 0)), pl.BlockSpec((2, heads, BLOCK), lambda n: (0, 0, n))],
        out_shape=[SDS((T, D), BF16), SDS((2, heads, T), F32)],
        semantics=("parallel",), args=(q, kv, kv, kv, kv, tabs, tabs, sinks), ride=ride)


def attention_bwd(q, kv, tabs, sinks, do, o, stats, *, name, ride=None):
    T, D = q.shape
    kvd = kv.shape[1] // 2
    heads = D // HEAD_DIM
    group = heads // N_KV_HEADS

    def body(q_ref, kp_ref, k_ref, vp_ref, v_ref, tp_ref, t_ref, s_ref, do_ref, o_ref, stat_ref,
             dq_ref, dc_ref, dp_ref, ds_ref):
        n = pl.program_id(0)
        gs = HEADS_TOGETHER
        mask = _band(n, gs)
        qt, kt, vt, tab, tab2 = _attn_operands(q_ref, kp_ref, k_ref, vp_ref, v_ref, tp_ref, t_ref)
        dot = do_ref[...].astype(F32).T
        odo = o_ref[...].astype(F32).T * dot
        dl_all = jnp.concatenate([jnp.sum(_head(odo, h), axis=0, keepdims=True) for h in range(heads)], axis=0)
        m_all, inv_all = stat_ref[0], stat_ref[1]
        row = lambda t, first: jnp.concatenate([t[first + g:first + g + 1] for g in range(gs)], axis=1)
        lane = lax.broadcasted_iota(jnp.int32, (8, 128), 1)
        dsink = jnp.zeros((8, 128), F32)
        firsts = [(j, first) for j in range(N_KV_HEADS) for first in range(j * group, (j + 1) * group, gs)]
        ks = [_rope(_head(kt, j), tab2).astype(BF16) for j in range(N_KV_HEADS)]
        vs = [_head(vt, j).astype(BF16) for j in range(N_KV_HEADS)]
        qs = [_group_heads(qt, first, gs, tab) for _, first in firsts]
        dos = [_group_heads(dot, first, gs) for _, first in firsts]
        scores = [_dot_tn(ks[j], q) for q, (j, _) in zip(qs, firsts)]
        dps = [_dot_tn(vs[j], do) for do, (j, _) in zip(dos, firsts)]
        ps, dscs = [], []
        for s, dp, (j, first) in zip(scores, dps, firsts):
            m, inv, dl = row(m_all, first), row(inv_all, first), row(dl_all, first)
            e = jnp.exp(jnp.where(mask, s.astype(BF16), NEG) - m.astype(BF16))
            p = e * inv.astype(BF16)
            dscs.append(p * (dp.astype(BF16) - dl.astype(BF16)))
            ps.append(p)
            weight = jnp.exp(_sink_row(s_ref, first, gs) - m) * inv * dl
            for g in range(gs):
                dsink = dsink - jnp.where(lane == first + g, jnp.sum(weight[:, g * BLOCK:(g + 1) * BLOCK]), 0.0)
        dqs = []
        dks = [jnp.zeros((HEAD_DIM, 2 * BLOCK), F32) for _ in range(N_KV_HEADS)]
        dvs = [jnp.zeros((HEAD_DIM, 2 * BLOCK), F32) for _ in range(N_KV_HEADS)]
        for p, dsc, q, do, (j, _) in zip(ps, dscs, qs, dos, firsts):
            dq = _dot(ks[j], dsc) * SCORE_SCALE
            dqs += [_rope_t(dq[:, g * BLOCK:(g + 1) * BLOCK], tab) for g in range(gs)]
            dks[j] = dks[j] + _dot_nt(q, dsc)
            dvs[j] = dvs[j] + _dot_nt(do, p)
        dks = [_rope_t(dk, tab2) for dk in dks]
        dq_ref[...] = jnp.concatenate(dqs, axis=0).T.astype(BF16)
        dkv = jnp.concatenate(dks + dvs, axis=0)
        dp_ref[...] = dkv[:, :BLOCK].T
        dc_ref[...] = dkv[:, BLOCK:].T
        _accumulate(ds_ref, n == 0, dsink)

    blk = lambda w: pl.BlockSpec((BLOCK, w), lambda n: (n, 0))
    return _call(
        body, name=name, grid=(T // BLOCK,),
        in_specs=_attn_specs(D, kvd) + [blk(D), blk(D), pl.BlockSpec((2, heads, BLOCK), lambda n: (0, 0, n))],
        out_specs=[blk(D), blk(2 * kvd), blk(2 * kvd), pl.BlockSpec((8, 128), lambda n: (0, 0))],
        out_shape=[SDS((T, D), BF16), SDS((T, 2 * kvd), F32), SDS((T, 2 * kvd), F32), SDS((8, 128), F32)],
        semantics=("arbitrary",), args=(q, kv, kv, kv, kv, tabs, tabs, sinks, do, o, stats), ride=ride)


def combine_dkv(d_cur, d_prev, *, name):
    T, W = d_cur.shape
    tm = ROW_TILE
    nt, per, last = T // tm, tm // BLOCK, T // BLOCK - 1

    def body(c_ref, p_ref, pn_ref, o_ref):
        nxt = jnp.where(pl.program_id(0) == nt - 1, 0.0, pn_ref[...])
        o_ref[...] = (c_ref[...] + jnp.concatenate([p_ref[BLOCK:, :], nxt], axis=0)).astype(BF16)

    return _call(
        body, name=name, grid=(nt,),
        in_specs=[pl.BlockSpec((tm, W), lambda i: (i, 0)), pl.BlockSpec((tm, W), lambda i: (i, 0)),
                  pl.BlockSpec((BLOCK, W), lambda i: (jnp.minimum((i + 1) * per, last), 0))],
        out_specs=[pl.BlockSpec((tm, W), lambda i: (i, 0))],
        out_shape=[SDS((T, W), BF16)],
        semantics=("parallel",), args=(d_cur, d_prev, d_prev))[0]


def matmul_nt_normbwd(da, w, h_in, g, dh_out, *, name, ride=None, tm=ROW_TILE):
    T, D = h_in.shape
    S, _, K = da.shape

    def body(*refs):
        da_refs, w_refs = refs[:S], refs[S:2 * S]
        h_ref, g_ref, dh_ref, o_ref, dg_ref = refs[2 * S:]
        dn = _dot_nt(da_refs[0][...], w_refs[0][...])
        for s in range(1, S):
            dn = dn + _dot_nt(da_refs[s][...], w_refs[s][...])
        dx, hh = _rmsnorm_bwd(h_ref[...].astype(F32), g_ref[...], dn)
        o_ref[...] = dh_ref[...] + dx
        _accumulate(dg_ref, pl.program_id(0) == 0, jnp.sum(dn * hh, axis=0, keepdims=True))

    row = pl.BlockSpec((tm, D), lambda i: (i, 0))
    vec = pl.BlockSpec((1, D), lambda i: (0, 0))
    part = lambda s: pl.BlockSpec((None, tm, K), lambda i: (s, i, 0))
    cols = lambda s: pl.BlockSpec((D, K), lambda i: (0, s), pipeline_mode=pl.Buffered(1))
    return _call(
        body, name=name, grid=(T // tm,),
        in_specs=[part(s) for s in range(S)] + [cols(s) for s in range(S)] + [row, vec, row],
        out_specs=[row, vec],
        out_shape=[SDS((T, D), F32), SDS((1, D), F32)],
        semantics=("arbitrary",), args=[da] * S + [w] * S + [h_in, g, dh_out], ride=ride)


def matmuls_nt_normbwd(das, ws, h_in, gs, dh_out, then, *, name, ride=None, tm=ROW_TILE):
    T, D = h_in.shape
    tm = min(tm, T)
    n = len(das)

    def body(*refs):
        da_refs, w_refs, g_refs = refs[:n], refs[n:2 * n], refs[2 * n:3 * n]
        h_ref, dh_ref, z_ref, g2_ref, w2_ref, o_ref = refs[3 * n:3 * n + 6]
        dg_refs, (dz_ref, dg2_ref, da_ref) = refs[3 * n + 6:4 * n + 6], refs[4 * n + 6:]
        first = pl.program_id(0) == 0
        hf = h_ref[...].astype(F32)
        r = _rms_r(hf)
        hh = hf * r
        total = dh_ref[...].astype(F32)
        for da_ref_, w_ref, g_ref, dg_ref in zip(da_refs, w_refs, g_refs, dg_refs):
            dn = _dot_nt(da_ref_[...], w_ref[...])
            gd = g_ref[...] * dn
            total = total + r * (gd - hh * jnp.mean(hh * gd, axis=-1, keepdims=True))
            _accumulate(dg_ref, first, jnp.sum(dn * hh, axis=0, keepdims=True))
        o_ref[...] = total.astype(STREAM)
        _normbwd_then_nt(total, z_ref[...].astype(F32), g2_ref, w2_ref, dz_ref, dg2_ref, da_ref, first)

    row = pl.BlockSpec((tm, D), lambda i: (i, 0))
    vec = pl.BlockSpec((1, D), lambda i: (0, 0))
    then_in, then_out, then_shape = _then_specs(then, tm, T, D)
    return _call(
        body, name=name, grid=(T // tm,),
        in_specs=[pl.BlockSpec((tm, da.shape[1]), lambda i: (i, 0)) for da in das]
        + [pl.BlockSpec(w.shape, lambda i: (0, 0)) for w in ws] + [vec] * n + [row, row] + then_in,
        out_specs=[row] + [vec] * n + then_out,
        out_shape=[SDS((T, D), STREAM)] + [SDS((1, D), F32)] * n + then_shape,
        semantics=("arbitrary",), args=list(das) + list(ws) + list(gs) + [h_in, dh_out] + list(then), ride=ride)


def matmul_tn(a, b, *, tb, name, ride=None, ta=MXU_WIDTH):
    T, Ka = a.shape
    S, _, Nb = b.shape
    per = Nb // tb

    def body(a_ref, b_ref, o_ref):
        o_ref[...] = _dot_tn(a_ref[...], b_ref[...]).astype(BF16)

    out = _call(
        body, name=name, grid=(S * per, Ka // ta),
        in_specs=[pl.BlockSpec((T, ta), lambda j, i: (0, i)),
                  pl.BlockSpec((None, T, tb), lambda j, i: (j // per, 0, j % per))],
        out_specs=[pl.BlockSpec((ta, tb), lambda j, i: (i, j))],
        out_shape=[SDS((Ka, S * Nb), BF16)],
        semantics=("parallel", "parallel"), args=(a, b), ride=ride)
    return out[0] if ride is None else (out[0][0], out[1])


def conv_bwd(dy, bcx, conv_w, *, name, ride=None, tm=ROW_TILE):
    T, D = dy.shape
    nt = T // tm
    hb = tm // BF16_ROWS
    last = T // BF16_ROWS - 1

    def body(dy_ref, dyn_ref, b_ref, bn_ref, c_ref, u_ref, cp_ref, up_ref, cw_ref, o_ref, dw_ref):
        i = pl.program_id(0)
        c, u = c_ref[...].astype(F32), u_ref[...].astype(F32)
        cu = c * u
        cup = jnp.where(i == 0, 0.0, cp_ref[...].astype(F32) * up_ref[...].astype(F32))
        cu1, cu2 = _shift_down(cup, cu, 1), _shift_down(cup, cu, 2)
        w0, w1, w2 = cw_ref[0:1, :], cw_ref[1:2, :], cw_ref[2:3, :]
        dyf = dy_ref[...].astype(F32)
        o_ref[:, 0:D] = (dyf * (w0 * cu2 + w1 * cu1 + w2 * cu)).astype(BF16)
        dcv = dyf * b_ref[...].astype(F32)
        dcvn = jnp.where(i == nt - 1, 0.0, dyn_ref[...].astype(F32) * bn_ref[...].astype(F32))
        dcu = w2 * dcv + w1 * _shift_up(dcv, dcvn, 1) + w0 * _shift_up(dcv, dcvn, 2)
        o_ref[:, D:2 * D] = (dcu * u).astype(BF16)
        o_ref[:, 2 * D:3 * D] = (dcu * c).astype(BF16)
        row = lax.broadcasted_iota(jnp.int32, (8, D), 0)
        dw = jnp.zeros((8, D), F32)
        for tap, t in enumerate((cu2, cu1, cu)):
            dw = jnp.where(row == tap, jnp.sum(dcv * t, axis=0, keepdims=True), dw)
        _accumulate(dw_ref, i == 0, dw)

    tile = lambda col: pl.BlockSpec((tm, D), lambda i: (i, col))
    prev = lambda col: pl.BlockSpec((BF16_ROWS, D), lambda i: (jnp.maximum(i * hb - 1, 0), col))
    nxt = lambda col: pl.BlockSpec((BF16_ROWS, D), lambda i: (jnp.minimum((i + 1) * hb, last), col))
    return _call(
        body, name=name, grid=(nt,),
        in_specs=[tile(0), nxt(0), tile(0), nxt(0), tile(1), tile(2), prev(1), prev(2),
                  pl.BlockSpec((3, D), lambda i: (0, 0))],
        out_specs=[pl.BlockSpec((tm, 3 * D), lambda i: (i, 0)), pl.BlockSpec((8, D), lambda i: (0, 0))],
        out_shape=[SDS((T, 3 * D), BF16), SDS((8, D), F32)],
        semantics=("arbitrary",), args=(dy, dy, bcx, bcx, bcx, bcx, bcx, bcx, conv_w), ride=ride)


class NoTraffic:
    def ride(self, kernel_name):
        return None

    def landed(self, kernel_name, results, wts):
        pass

    def grad(self, key, value):
        pass


def local_step(x, target, wts, vec, traffic):
    T, D = x.shape
    tabs = rope_tables(T)
    small = {}

    def run(builder, *args, name, **kw):
        ride = traffic.ride(name)
        if ride is None:
            return builder(*args, name=name, **kw)
        out, extra = builder(*args, name=name, ride=ride, **kw)
        traffic.landed(name, extra, wts)
        return out

    bcx, xn1 = run(norm_matmul, x, vec["a_pre"], wts["w_in"], tn=3 * D, split=1, name="a_in")
    bcx = bcx[0]
    h1, z0, y0 = run(conv_mix_out, bcx, vec["conv_w"], wts["w_out"], vec["a_post"], x, name="a_out")
    gu0, act0, xt2 = run(norm_swiglu_in, h1, vec["ffn_pre0"], wts["gu0"], name="ffn0_in")
    h2, z1 = run(plain_mix_out, act0, wts["wd0"], vec["ffn_post0"], h1, name="ffn0_out")
    kvp, xkv, qp, xq = norm2_matmul(h2, [vec["kv_norm"], vec["b_pre"]], [wts["w_kv"], wts["w_q"]], name="kvq_in")
    attn, attn_stats = run(attention_fwd, qp, kvp, tabs, vec["sinks"], name="attn_fwd")
    h3, z2 = plain_mix_out(attn, wts["w_o"], vec["b_post"], h2, name="attn_out", tm=BIG_ROW_TILE)
    gu1, act1, xt3 = run(norm_swiglu_in, h3, vec["ffn_pre1"], wts["gu1"], name="ffn1_in")
    dy, dz3, small["ffn_post1"], dact1, loss = plain_mix_out(act1, wts["wd1"], vec["ffn_post1"], h3, name="ffn1_out",
                                                             target=target)

    def ffn_bwd(layer, dz, dact, gu, act, xt, h_in, dh, then, gu_first):
        tag = "ffn%d" % layer
        dwd = lambda: traffic.grad("wd%d" % layer, run(matmul_tn, act, dz[None], tb=D, name=tag + "_dwd"))
        dwgu = lambda: traffic.grad("gu%d" % layer, run(swiglu_bwd_tn, xt, dact, gu, name=tag + "_dwgu"))
        for step in ((dwgu, dwd) if gu_first else (dwd, dwgu)):
            step()
        dh_in, small["ffn_pre%d" % layer], dz_, dg_, da_ = run(
            swiglu_bwd_in, dact, gu, wts["gu%d" % layer], h_in, vec["ffn_pre%d" % layer], dh, then,
            name=tag + "_in_bwd")
        return dh_in, dz_, dg_, da_

    dh3, dz2, small["b_post"], dattn = ffn_bwd(1, dz3, dact1, gu1, act1, xt3, h3, dy,
                                               (z2, vec["b_post"], wts["w_o"]), gu_first=False)
    traffic.grad("w_o", matmul_tn(attn, dz2[None], tb=D, name="attn_dwo"))
    dq, dkv_cur, dkv_prev, small["sinks"] = run(attention_bwd, qp, kvp, tabs, vec["sinks"], dattn, attn, attn_stats,
                                                name="attn_bwd")
    dkv = combine_dkv(dkv_cur, dkv_prev, name="attn_dkv")
    traffic.grad("w_q", matmul_tn(xq, dq[None], tb=D, name="attn_dwq"))
    traffic.grad("w_kv", matmul_tn(xkv, dkv[None], tb=dkv.shape[1], name="attn_dwkv"))
    dh2, small["b_pre"], small["kv_norm"], dz1, small["ffn_post0"], dact0 = run(
        matmuls_nt_normbwd, [dq, dkv], [wts["w_q"], wts["w_kv"]], h2, [vec["b_pre"], vec["kv_norm"]], dh3,
        (z1, vec["ffn_post0"], wts["wd0"]), name="qkv_in_bwd")
    dh1, dz0, small["a_post"], dyc = ffn_bwd(0, dz1, dact0, gu0, act0, xt2, h1, dh2,
                                             (z0, vec["a_post"], wts["w_out"]), gu_first=True)
    traffic.grad("w_out", run(matmul_tn, y0, dz0[None], tb=D, name="a_dwout"))
    dbcx, small["conv_w"] = run(conv_bwd, dyc, bcx, vec["conv_w"], name="a_conv_bwd")
    traffic.grad("w_in", matmul_tn(xn1, dbcx[None], tb=3 * D // 2, name="a_dwin"))
    dx, small["a_pre"] = run(matmul_nt_normbwd, dbcx[None], wts["w_in"], x, vec["a_pre"], dh1, name="a_in_bwd",
                             tm=ROW_TILE // 2)
    return loss, dx, small


SMALL_ROWS = 16
LOSS_ROW = 13

WHOLE = None
GATHER_PLAN = {"cast_rest": [("w_in", WHOLE)],
               "a_in": [("w_out", WHOLE), ("gu0", (0, 18))],
               "a_out": [("gu0", (18, 14))],
               "ffn0_in": [("wd0", WHOLE), ("w_kv", WHOLE), ("w_q", WHOLE), ("w_o", WHOLE)],
               "ffn0_out": [("gu1", (0, 16))],
               "attn_fwd": [("gu1", (16, 16))],
               "ffn1_in": [("wd1", WHOLE)]}
PAIR_PLAN = {"ffn1_dwgu": ["wd1"], "ffn1_in_bwd": ["gu1"], "attn_bwd": ["w_o"], "qkv_in_bwd": ["w_q", "w_kv"],
             "ffn0_dwd": ["gu0"], "ffn0_in_bwd": ["wd0"], "a_conv_bwd": ["w_out"]}
PAIR_ALONE = ["w_in"]
CHIP_PLAN = {"ffn1_in_bwd": [("wd1", WHOLE)], "attn_bwd": [("gu1", WHOLE)],
             "ffn0_dwgu": [("w_o", WHOLE), ("w_q", WHOLE), ("w_kv", WHOLE)],
             "ffn0_in_bwd": [("gu0", WHOLE)], "a_dwout": [("wd0", (0, 8))], "a_conv_bwd": [("wd0", (8, 14))],
             "a_in_bwd": [("w_out", WHOLE), ("w_in", WHOLE)]}
HALF_PLAN = {"a_in_bwd": ["gu0", "gu1", "wd0", "wd1", "w_kv", "w_q", "w_o"]}
GRAD_KIND = dict(KIND, gu0="split", gu1="split")


class Traffic:
    def __init__(self, wholes, quarter, c_arr, pc_arr):
        self.wholes, self.quarter, self.c_arr, self.pc_arr = wholes, quarter, c_arr, pc_arr
        self.views, self.sums, self.got = {}, {}, {}
        self.reduced = {}
        self.stages = {}

    def reduce(self, keys, name):
        return chip_reduce([self.sums[k] for k in keys], [self.got[k] for k in keys], [GRAD_KIND[k] for k in keys],
                           self.pc_arr, name=name)

    def ride(self, name, small=None):
        rides, stages = [], []
        if name in GATHER_PLAN:
            plan = GATHER_PLAN[name]
            rides.append(gather_ride([self.wholes[k] for k, _ in plan],
                                     [(KIND[k], self.quarter[k], part) for k, part in plan], small))
            stages.append(("gather", [k for k, _ in plan]))
        if name in CHIP_PLAN:
            plan = CHIP_PLAN[name]
            rides.append(chip_ride([self.sums[k] for k, _ in plan],
                                   [(GRAD_KIND[k], self.quarter[k], part) for k, part in plan],
                                   earlier=[self.got.get(k) for k, _ in plan]))
            stages.append(("chip", [k for k, _ in plan]))
        if name in PAIR_PLAN:
            keys = PAIR_PLAN[name]
            rides.append(pair_ride([self.views[k] for k in keys]))
            stages.append(("pair", keys))
        if name in HALF_PLAN:
            keys = HALF_PLAN[name]
            rides.append(half_ride(self.reduce(keys, "chip_reduce_early")))
            stages.append(("half", keys))
        self.stages[name] = stages
        return join(rides)

    def landed(self, name, results, wts):
        results = list(results)
        for stage, keys in self.stages[name]:
            mine, results = results[:len(keys)], results[len(keys):]
            if stage == "gather":
                for k, whole in zip(keys, mine):
                    self.wholes[k] = wts[k] = whole
            elif stage == "chip":
                self.got.update(zip(keys, mine))
            elif stage == "half":
                self.reduced.update(zip(keys, mine))
            else:
                for k, got in zip(keys, mine):
                    self.sums[k] = pair_add(self.views[k], got, self.c_arr, name="pair_add_" + k)

    def grad(self, key, value):
        r, ws = self.quarter[key]
        view = {"row": (N_CHIPS, 2, r // 2, ws), "col": (1, 2, r // 2, N_CHIPS * ws), "split": (2, 2, r // 2, 2 * ws)}
        self.views[key] = value.reshape(view[GRAD_KIND[key]])
        if key in PAIR_ALONE:
            (got,) = alone(pair_ride([self.views[key]]), name="pair_exchange_" + key)
            self.sums[key] = pair_add(self.views[key], got, self.c_arr, name="pair_add_" + key)


def kernel(x, a_pre_norm, a_w_in, a_conv_w, a_w_out, a_post_norm, ffn_pre_norm, ffn_w_gate_up, ffn_w_down, ffn_post_norm, kv_norm, w_kv, b_pre_norm, b_w_q, b_sinks, b_w_o, b_post_norm, loss_target, m_a_pre_norm, m_a_w_in, m_a_conv_w, m_a_w_out, m_a_post_norm, m_ffn_pre_norm, m_ffn_w_gate_up, m_ffn_w_down, m_ffn_post_norm, m_kv_norm, m_w_kv, m_b_pre_norm, m_b_w_q, m_b_sinks, m_b_w_o, m_b_post_norm, v_a_pre_norm, v_a_w_in, v_a_conv_w, v_a_w_out, v_a_post_norm, v_ffn_pre_norm, v_ffn_w_gate_up, v_ffn_w_down, v_ffn_post_norm, v_kv_norm, v_w_kv, v_b_pre_norm, v_b_w_q, v_b_sinks, v_b_w_o, v_b_post_norm):
    T, D = x.shape[1], x.shape[2]
    xi, yi, ci = _place()
    p = 2 * xi + yi
    p_arr = jnp.reshape(p, (1,)).astype(jnp.int32)
    c_arr = jnp.reshape(ci, (1,)).astype(jnp.int32)
    pc_arr = jnp.stack([p, ci]).astype(jnp.int32)
    me_arr = jnp.reshape(4 * xi + 2 * yi + ci, (1,)).astype(jnp.int32)
    qd = D // N_CHIPS

    big = {"w_in": (a_w_in, 0), "w_out": (a_w_out, 0), "gu0": (ffn_w_gate_up, 0), "gu1": (ffn_w_gate_up, 1),
           "wd0": (ffn_w_down, 0), "wd1": (ffn_w_down, 1), "w_kv": (w_kv[None], 0), "w_q": (b_w_q, 0),
           "w_o": (b_w_o, 0)}
    names = list(big)
    quarter = {k: w.shape[1:] for k, (w, _) in big.items()}
    source = lambda k: big[k] + (KIND[k],)
    traffic = Traffic(dict(zip(names[:1], cast_quarters([source(names[0])], p_arr, name="cast_first"))), quarter,
                      c_arr, pc_arr)
    small_shard = jnp.concatenate([a_pre_norm, a_post_norm, a_conv_w[0], jnp.zeros((3, qd), F32)], axis=0)
    wts = {}
    rest, (*landed, small_full) = cast_quarters([source(k) for k in names[1:]], p_arr, name="cast_rest",
                                                ride=traffic.ride("cast_rest", small_shard))
    traffic.wholes.update(zip(names[1:], rest))
    traffic.landed("cast_rest", landed, wts)
    rows = lambda k: jnp.transpose(small_full[:, k], (1, 0, 2)).reshape(-1, D)
    vec = {"a_pre": rows(slice(0, 1)), "a_post": rows(slice(1, 2)), "conv_w": rows(slice(2, 5)),
           "ffn_pre0": ffn_pre_norm[0:1], "ffn_pre1": ffn_pre_norm[1:2],
           "ffn_post0": ffn_post_norm[0:1], "ffn_post1": ffn_post_norm[1:2],
           "kv_norm": kv_norm[None], "b_pre": b_pre_norm, "b_post": b_post_norm, "sinks": b_sinks}

    loss, dx, small = local_step(x[0], loss_target[0], wts, vec, traffic)

    pad = lambda a: jnp.pad(a, ((0, 0), (0, D - a.shape[1])))
    small_block = jnp.concatenate(
        [small["a_pre"], small["a_post"], small["conv_w"][0:3], small["ffn_pre0"], small["ffn_pre1"],
         small["ffn_post0"], small["ffn_post1"], small["kv_norm"], small["b_pre"], small["b_post"],
         pad(small["sinks"][0:1]), pad(loss[0:1]), jnp.zeros((SMALL_ROWS - LOSS_ROW - 1, D), F32)], axis=0)
    late = [k for k in names if k not in traffic.reduced]
    *swapped, small_blocks = alone(join([half_ride(traffic.reduce(late, "chip_reduce_late")),
                                         chip_ride([], [], small_block)]), name="last_exchange")
    traffic.reduced.update(zip(late, swapped))
    grad = {k: traffic.reduced[k].reshape(quarter[k]) for k in names}
    small_sum = small_reduce(small_blocks, me_arr)

    out = {}
    out["a_w_in"] = adamw(a_w_in, [grad["w_in"]], m_a_w_in, v_a_w_in, name="adamw_a_w_in")
    out["a_w_out"] = adamw(a_w_out, [grad["w_out"]], m_a_w_out, v_a_w_out, name="adamw_a_w_out")
    out["ffn_w_gate_up"] = adamw(ffn_w_gate_up, [grad["gu0"], grad["gu1"]], m_ffn_w_gate_up, v_ffn_w_gate_up,
                                 name="adamw_ffn_w_gate_up")
    out["ffn_w_down"] = adamw(ffn_w_down, [grad["wd0"], grad["wd1"]], m_ffn_w_down, v_ffn_w_down,
                              name="adamw_ffn_w_down")
    out["w_kv"] = [o[0] for o in adamw(w_kv[None], [grad["w_kv"]], m_w_kv[None], v_w_kv[None], name="adamw_w_kv")]
    out["b_w_q"] = adamw(b_w_q, [grad["w_q"]], m_b_w_q, v_b_w_q, name="adamw_b_w_q")
    out["b_w_o"] = adamw(b_w_o, [grad["w_o"]], m_b_w_o, v_b_w_o, name="adamw_b_w_o")

    def pack(a_pre, a_post, conv, ffn_pre, ffn_post, kvn, b_pre, b_post, sinks):
        return jnp.concatenate([pad(a_pre), pad(a_post), pad(conv[0]), ffn_pre, ffn_post, kvn[None], b_pre, b_post,
                                pad(sinks), jnp.zeros((SMALL_ROWS - 13, D), F32)], axis=0)

    g_small = jnp.concatenate([pad(lax.dynamic_slice(small_sum, (0, p * qd), (5, qd))), small_sum[5:]], axis=0)
    w_small = pack(a_pre_norm, a_post_norm, a_conv_w, ffn_pre_norm, ffn_post_norm, kv_norm, b_pre_norm, b_post_norm,
                   b_sinks)
    m_small = pack(m_a_pre_norm, m_a_post_norm, m_a_conv_w, m_ffn_pre_norm, m_ffn_post_norm, m_kv_norm,
                   m_b_pre_norm, m_b_post_norm, m_b_sinks)
    v_small = pack(v_a_pre_norm, v_a_post_norm, v_a_conv_w, v_ffn_pre_norm, v_ffn_post_norm, v_kv_norm,
                   v_b_pre_norm, v_b_post_norm, v_b_sinks)
    packed = adamw(w_small[None], [g_small], m_small[None], v_small[None], name="adamw_small")
    ns = b_sinks.shape[1]
    unpack = lambda a: {"a_pre_norm": a[0:1, :qd], "a_post_norm": a[1:2, :qd], "a_conv_w": a[None, 2:5, :qd],
                        "ffn_pre_norm": a[5:7], "ffn_post_norm": a[7:9], "kv_norm": a[9], "b_pre_norm": a[10:11],
                        "b_post_norm": a[11:12], "b_sinks": a[12:13, :ns]}
    unpacked = [unpack(a[0]) for a in packed]
    for k in unpacked[0]:
        out[k] = [u[k] for u in unpacked]

    order = ["a_pre_norm", "a_w_in", "a_conv_w", "a_w_out", "a_post_norm", "ffn_pre_norm", "ffn_w_gate_up",
             "ffn_w_down", "ffn_post_norm", "kv_norm", "w_kv", "b_pre_norm", "b_w_q", "b_sinks", "b_w_o",
             "b_post_norm"]
    return (small_sum[LOSS_ROW, 0], dx[None], *[out[k][0] for k in order], *[out[k][1] for k in order],
            *[out[k][2] for k in order], *[out[k][3] for k in order])
```

```python
import math

import jax
import jax.numpy as jnp
from jax import lax
from jax.experimental import pallas as pl
from jax.experimental.pallas import tpu as pltpu

F32 = jnp.float32
BF16 = jnp.bfloat16
SDS = jax.ShapeDtypeStruct
MESH = pl.DeviceIdType.MESH
DMA = pltpu.SemaphoreType.DMA
HBM_SPEC = pl.BlockSpec(memory_space=pltpu.HBM)

EPS = 1e-6
NEG = -1e30
HEAD_DIM = 64
N_KV_HEADS = 4
BLOCK = 128
ROT_DIM = HEAD_DIM // 4
ROPE_THETA = 500000.0
N_CHIPS = 4

ADAM_LR = 0.001
ADAM_B1 = 0.9
ADAM_B2 = 0.999
ADAM_EPS = 1e-08
ADAM_WD = 0.01
ADAM_STEP = 10

VMEM_LIMIT_BYTES = 52 * 1024 * 1024
ROW_TILE = 512
BF16_ROWS = 16
STREAM = BF16
MXU_WIDTH = 256

KIND = {"w_in": "col", "gu0": "col", "gu1": "col", "w_out": "row", "wd0": "row", "wd1": "row", "w_kv": "row",
        "w_q": "row", "w_o": "row"}


def _params(*semantics):
    return pltpu.CompilerParams(dimension_semantics=semantics, vmem_limit_bytes=VMEM_LIMIT_BYTES)


def _row_tile(rows, limit, step=8):
    return max(t for t in range(step, limit + 1, step) if rows % t == 0)


def _place():
    return lax.axis_index("x"), lax.axis_index("y"), lax.axis_index("c")


def _other_chips(x, y):
    return [(1 - x, y), (x, 1 - y), (1 - x, 1 - y)]


def _remote(src, dst, send_sem, recv_sem, to):
    return pltpu.make_async_remote_copy(src_ref=src, dst_ref=dst, send_sem=send_sem, recv_sem=recv_sem,
                                        device_id=to, device_id_type=MESH)


def _full_shape(kind, quarter):
    r, ws = quarter
    return (N_CHIPS * r, ws) if kind == "row" else (r, N_CHIPS * ws)


def _rows_of(h, part):
    lo, n = (0, h) if part is None else (part[0] * BF16_ROWS, part[1] * BF16_ROWS)
    assert lo + n <= h, (h, part)
    return lo, n


def _half_of_quarter(ref, kind, quarter, part, q, half):
    r, ws = quarter
    h = r // 2
    lo, n = _rows_of(h, part)
    if kind == "row":
        return ref.at[pl.ds(pl.multiple_of(q * r + half * h + lo, BF16_ROWS), n)]
    return ref.at[pl.ds(pl.multiple_of(half * h + lo, BF16_ROWS), n), pl.ds(pl.multiple_of(q * ws, 128), ws)]


class Ride:
    def __init__(self, operands, out_shape, aliases, sems, make):
        self.operands, self.out_shape, self.aliases, self.sems, self.make = operands, out_shape, aliases, sems, make


def join(rides):
    rides = [r for r in rides if r is not None]
    if len(rides) < 2:
        return rides[0] if rides else None
    aliases, at = {}, [0, 0, 0]
    cuts = []
    for r in rides:
        aliases.update({at[0] + i: at[1] + o for i, o in r.aliases.items()})
        cuts.append(tuple(at))
        at = [at[0] + len(r.operands), at[1] + len(r.out_shape), at[2] + len(r.sems)]
    cuts.append(tuple(at))

    def make(ins, outs, sem):
        made = [r.make(ins[lo[0]:hi[0]], outs[lo[1]:hi[1]], sem[lo[2]:hi[2]]) for r, lo, hi in zip(rides, cuts, cuts[1:])]

        def start():
            for s, _ in made:
                s()

        def finish():
            for _, f in made:
                f()

        return start, finish

    return Ride(sum((list(r.operands) for r in rides), []), sum((list(r.out_shape) for r in rides), []), aliases,
                sum((list(r.sems) for r in rides), []), make)


def _call(body, *, name, grid, in_specs, out_specs, out_shape, args, scratch_shapes=(), semantics=None, ride=None,
          prefetch=None):
    pre = 0 if prefetch is None else 1
    n_in, n_out, n_scr = len(in_specs), len(out_specs), len(scratch_shapes)
    r_in, r_out = (len(ride.operands), len(ride.out_shape)) if ride is not None else (0, 0)
    a, b = pre + n_in, pre + n_in + r_in
    c, d = b + n_out, b + n_out + r_out
    e = d + n_scr

    def riding(*refs):
        start, finish = ride.make(refs[a:b], refs[c:d], refs[e:])
        ids = [pl.program_id(k) for k in range(len(grid))]
        first, last = ids[0] == 0, ids[0] == grid[0] - 1
        for k in range(1, len(grid)):
            first, last = first & (ids[k] == 0), last & (ids[k] == grid[k] - 1)
        pl.when(first)(start)
        body(*refs[:a], *refs[b:c], *refs[d:e])
        pl.when(last)(finish)

    if ride is None:
        kernel_body, extra_in, extra_out, extra_shape, extra_scr, aliases = body, [], [], [], [], {}
        params = _params(*semantics)
    else:
        kernel_body, extra_in, extra_out = riding, [HBM_SPEC] * r_in, [HBM_SPEC] * r_out
        extra_shape, extra_scr = list(ride.out_shape), list(ride.sems)
        aliases = {pre + n_in + i: n_out + o for i, o in ride.aliases.items()}
        params = _params(*(("arbitrary",) * len(grid)))
    specs = dict(grid=grid, in_specs=list(in_specs) + extra_in, out_specs=list(out_specs) + extra_out,
                 scratch_shapes=list(scratch_shapes) + extra_scr)
    if prefetch is not None:
        specs = dict(grid_spec=pltpu.PrefetchScalarGridSpec(num_scalar_prefetch=1, **specs))
        args = (prefetch,) + tuple(args)
    outs = pl.pallas_call(kernel_body, name=name, out_shape=list(out_shape) + extra_shape,
                          input_output_aliases=aliases, compiler_params=params, **specs,
                          )(*args, *(ride.operands if ride is not None else ()))
    return outs if ride is None else (outs[:n_out], outs[n_out:])


def alone(ride, *, name):
    def body(*refs):
        n = len(ride.operands)
        start, finish = ride.make(refs[:n], refs[n:n + len(ride.out_shape)], refs[n + len(ride.out_shape):])
        start()
        finish()

    return pl.pallas_call(
        body, name=name, in_specs=[HBM_SPEC] * len(ride.operands), out_specs=[HBM_SPEC] * len(ride.out_shape),
        out_shape=list(ride.out_shape), input_output_aliases=dict(ride.aliases), scratch_shapes=list(ride.sems),
    )(*ride.operands)


def gather_ride(wholes, metas, small=None):
    n = len(wholes)
    operands, out_shape = list(wholes), [SDS(s.shape, s.dtype) for s in wholes]
    sems = [DMA((n, 3)), DMA((n, 3)), DMA((n, 3)), DMA((n, 3))]
    if small is not None:
        operands.append(small)
        out_shape.append(SDS((N_CHIPS,) + small.shape, small.dtype))
        sems += [DMA((3,)), DMA((3,)), DMA(())]

    def make(ins, outs, sem):
        send1, recv1, send2, recv2 = sem[:4]
        x, y, c = _place()
        p = 2 * x + y
        chips = _other_chips(x, y)
        me, sibling = (x, y, c), (x, y, 1 - c)
        part = lambda t, q, half: _half_of_quarter(outs[t], *metas[t], q, half)
        first = []
        for j, (qx, qy) in enumerate(chips):
            if small is not None:
                first.append(_remote(ins[n], outs[n].at[p], sem[4].at[j], sem[5].at[j], (qx, qy, c)))
            for t in range(n):
                first.append(_remote(part(t, p, c), part(t, p, c), send1.at[t, j], recv1.at[t, j], (qx, qy, c)))
        local = [] if small is None else [pltpu.make_async_copy(ins[n], outs[n].at[p], sem[6])]

        def start():
            for cp in local + first:
                cp.start()

        def finish():
            passed = []
            for j, (qx, qy) in enumerate(chips):
                q = 2 * qx + qy
                for t in range(n):
                    landed = part(t, q, c)
                    _remote(landed, landed, send1.at[t, j], recv1.at[t, j], me).wait_recv()
                    cp = _remote(landed, landed, send2.at[t, j], recv2.at[t, j], sibling)
                    cp.start()
                    passed.append(cp)
            for j, (qx, qy) in enumerate(chips):
                q = 2 * qx + qy
                if small is not None:
                    _remote(outs[n].at[q], outs[n].at[q], sem[4].at[j], sem[5].at[j], me).wait_recv()
                for t in range(n):
                    theirs = part(t, q, 1 - c)
                    _remote(theirs, theirs, send2.at[t, j], recv2.at[t, j], me).wait_recv()
            for cp in first + passed:
                cp.wait_send()
            for cp in local:
                cp.wait()

        return start, finish

    return Ride(operands, out_shape, {t: t for t in range(n)}, sems, make)


def chip_ride(sums, metas, small=None, earlier=None):
    n = len(sums)
    operands = list(sums)
    out_shape = [SDS((3, s.shape[1], quarter[1]), s.dtype) for s, (_, quarter, _) in zip(sums, metas)]
    sems = [DMA((n, 3)), DMA((n, 3))] if n else []
    if small is not None:
        operands.append(small)
        out_shape.append(SDS((8,) + small.shape, small.dtype))
        sems += [DMA((7,)), DMA((7,)), DMA(())]
    aliases = {}
    for t, buffer in enumerate(earlier or [None] * n):
        if buffer is not None:
            aliases[len(operands)] = t
            operands.append(buffer)

    def make(ins, outs, sem):
        x, y, c = _place()
        cps = []
        for j, (qx, qy) in enumerate(_other_chips(x, y)):
            q = 2 * qx + qy
            for t in range(n):
                kind, (_, ws), part = metas[t]
                rows = pl.ds(*_rows_of(ins[t].shape[1], part))
                if kind == "row":
                    src = ins[t].at[q, rows]
                elif kind == "col":
                    src = ins[t].at[0, rows, pl.ds(pl.multiple_of(q * ws, 128), ws)]
                else:
                    src = ins[t].at[q // 2, rows, pl.ds(pl.multiple_of((q % 2) * ws, 128), ws)]
                cps.append(_remote(src, outs[t].at[j, rows], sem[0].at[t, j], sem[1].at[t, j], (qx, qy, c)))
        local = []
        if small is not None:
            ssend, srecv, lsem = sem[2 * bool(n):2 * bool(n) + 3]
            local.append(pltpu.make_async_copy(ins[n], outs[n].at[0], lsem))
            for k in range(1, 8):
                peer = (x ^ (k >> 2 & 1), y ^ (k >> 1 & 1), c ^ (k & 1))
                cps.append(_remote(ins[n], outs[n].at[k], ssend.at[k - 1], srecv.at[k - 1], peer))

        def start():
            for cp in local + cps:
                cp.start()

        def finish():
            for cp in cps + local:
                cp.wait()

        return start, finish

    return Ride(operands, out_shape, aliases, sems, make)


def pair_ride(grads):
    n = len(grads)

    def make(ins, outs, sem):
        x, y, c = _place()
        cps = [_remote(ins[t].at[:, 1 - c], outs[t], sem[0].at[t], sem[1].at[t], (x, y, 1 - c)) for t in range(n)]

        def start():
            for cp in cps:
                cp.start()

        def finish():
            for cp in cps:
                cp.wait()

        return start, finish

    return Ride(list(grads), [SDS((g.shape[0],) + g.shape[2:], g.dtype) for g in grads], {}, [DMA((n,)), DMA((n,))],
                make)


def half_ride(quarters):
    n = len(quarters)

    def make(ins, outs, sem):
        x, y, c = _place()
        sends = [_remote(outs[t].at[c], outs[t].at[c], sem[0].at[t], sem[1].at[t], (x, y, 1 - c)) for t in range(n)]

        def start():
            for cp in sends:
                cp.start()

        def finish():
            for t in range(n):
                theirs = outs[t].at[1 - c]
                _remote(theirs, theirs, sem[0].at[t], sem[1].at[t], (x, y, c)).wait_recv()
            for cp in sends:
                cp.wait_send()

        return start, finish

    return Ride(list(quarters), [SDS(q.shape, q.dtype) for q in quarters], {t: t for t in range(n)},
                [DMA((n,)), DMA((n,))], make)


CAST_STEPS = 4


def cast_quarters(sources, p_arr, *, name, ride=None):
    n = len(sources)
    in_specs, out_specs, out_shape = [], [], []
    for w, layer, kind in sources:
        _, r, ws = w.shape
        tr = r // CAST_STEPS
        assert tr % BF16_ROWS == 0, w.shape
        in_specs.append(pl.BlockSpec((None, tr, ws), lambda i, p_ref, layer=layer: (layer, i, 0)))
        out_specs.append(pl.BlockSpec((tr, ws), (lambda i, p_ref: (p_ref[0] * CAST_STEPS + i, 0)) if kind == "row"
                                      else (lambda i, p_ref: (i, p_ref[0]))))
        out_shape.append(SDS(_full_shape(kind, (r, ws)), BF16))

    def body(p_ref, *refs):
        for w_ref, o_ref in zip(refs[:n], refs[n:]):
            o_ref[...] = w_ref[...].astype(BF16)

    return _call(body, name=name, grid=(CAST_STEPS,), in_specs=in_specs, out_specs=out_specs, out_shape=out_shape,
                 semantics=("parallel",), args=[w for w, _, _ in sources], ride=ride, prefetch=p_arr)


def pair_add(own, got, c_arr, *, name):
    A, _, h, W = own.shape
    th = _row_tile(h, max(BF16_ROWS, (3 << 19) // W), BF16_ROWS)

    def body(c_ref, a_ref, b_ref, o_ref):
        o_ref[...] = (a_ref[...].astype(F32) + b_ref[...].astype(F32)).astype(BF16)

    return pl.pallas_call(
        body, name=name,
        grid_spec=pltpu.PrefetchScalarGridSpec(
            num_scalar_prefetch=1, grid=(A, h // th),
            in_specs=[pl.BlockSpec((None, None, th, W), lambda q, i, c_ref: (q, c_ref[0], i, 0)),
                      pl.BlockSpec((None, th, W), lambda q, i, c_ref: (q, i, 0))],
            out_specs=pl.BlockSpec((None, th, W), lambda q, i, c_ref: (q, i, 0))),
        out_shape=SDS((A, h, W), BF16),
        compiler_params=_params("parallel", "parallel"),
    )(c_arr, own, got)


REDUCE_STEPS = 2


def chip_reduce(sums, got, kinds, pc_arr, *, name):
    n = len(sums)
    mine = {"row": lambda i, pc_ref: (pc_ref[0], i, 0), "col": lambda i, pc_ref: (0, i, pc_ref[0]),
            "split": lambda i, pc_ref: (pc_ref[0] // 2, i, pc_ref[0] % 2)}
    a_specs, b_specs, o_specs, out_shape = [], [], [], []
    for g, kind in zip(got, kinds):
        _, h, ws = g.shape
        th = h // REDUCE_STEPS
        assert th % BF16_ROWS == 0, g.shape
        a_specs.append(pl.BlockSpec((None, th, ws), mine[kind]))
        b_specs.append(pl.BlockSpec((3, th, ws), lambda i, pc_ref: (0, i, 0)))
        o_specs.append(pl.BlockSpec((None, th, ws), lambda i, pc_ref: (pc_ref[1], i, 0)))
        out_shape.append(SDS((2, h, ws), F32))

    def body(pc_ref, *refs):
        for a_ref, b_ref, o_ref in zip(refs[:n], refs[n:2 * n], refs[2 * n:]):
            o_ref[...] = ((a_ref[...].astype(F32) + b_ref[0].astype(F32)) + b_ref[1].astype(F32)) + b_ref[2].astype(F32)

    return _call(body, name=name, grid=(REDUCE_STEPS,), in_specs=a_specs + b_specs, out_specs=o_specs,
                 out_shape=out_shape, semantics=("parallel",), args=list(sums) + list(got), prefetch=pc_arr)


def small_reduce(blocks, me_arr):
    _, rows, D = blocks.shape

    def body(me_ref, b_ref, o_ref):
        me = me_ref[0]
        total = b_ref[me]
        for d in range(1, 8):
            total = total + b_ref[d ^ me]
        o_ref[...] = total

    return pl.pallas_call(
        body, name="small_reduce",
        grid_spec=pltpu.PrefetchScalarGridSpec(
            num_scalar_prefetch=1, grid=(1,),
            in_specs=[pl.BlockSpec((8, rows, D), lambda i, me_ref: (0, 0, 0))],
            out_specs=pl.BlockSpec((rows, D), lambda i, me_ref: (0, 0))),
        out_shape=SDS((rows, D), F32),
        compiler_params=_params("arbitrary"),
    )(me_arr, blocks)


def adamw(w, gs, m, v, *, name):
    L, r, cols = w.shape
    tr = _row_tile(r, 256)
    nt = r // tr

    def body(*refs):
        w_ref, m_ref, v_ref = refs[:3]
        g_refs = refs[3:3 + L]
        g_out, d_out, m_out, v_out = refs[3 + L:]
        layer = pl.program_id(0)
        g = g_refs[0][...]
        for l in range(1, L):
            g = jnp.where(layer == l, g_refs[l][...], g)
        m_new = ADAM_B1 * m_ref[...] + (1.0 - ADAM_B1) * g
        v_new = ADAM_B2 * v_ref[...] + (1.0 - ADAM_B2) * (g * g)
        m_hat = m_new / (1.0 - ADAM_B1 ** ADAM_STEP)
        v_hat = v_new / (1.0 - ADAM_B2 ** ADAM_STEP)
        g_out[...] = g
        m_out[...] = m_new
        v_out[...] = v_new
        d_out[...] = -ADAM_LR * (m_hat / (jnp.sqrt(v_hat) + ADAM_EPS) + ADAM_WD * w_ref[...])

    full = pl.BlockSpec((None, tr, cols), lambda l, i: (l, i, 0))
    g_spec = lambda l0: pl.BlockSpec((tr, cols), lambda l, i: (jnp.where(l == l0, i, jnp.where(l < l0, 0, nt - 1)), 0))
    return pl.pallas_call(
        body, name=name, grid=(L, nt),
        in_specs=[full, full, full] + [g_spec(l0) for l0 in range(L)],
        out_specs=[full] * 4,
        out_shape=[SDS(w.shape, F32)] * 4,
        compiler_params=_params("arbitrary", "arbitrary"),
    )(w, m, v, *gs)


def _rms_r(xf):
    return lax.rsqrt(jnp.mean(xf * xf, axis=-1, keepdims=True) + EPS)


def _rmsnorm_bwd(xf, g, dy):
    r = _rms_r(xf)
    xh = xf * r
    gd = g * dy
    return r * (gd - xh * jnp.mean(xh * gd, axis=-1, keepdims=True)), xh


def _dot(a, b):
    return jnp.dot(a, b, preferred_element_type=F32)


def _dot_nt(a, b):
    return lax.dot_general(a, b, (((1,), (1,)), ((), ())), preferred_element_type=F32)


def _dot_tn(a, b):
    return lax.dot_general(a, b, (((0,), (0,)), ((), ())), preferred_element_type=F32)


def _accumulate(ref, first, value):
    @pl.when(first)
    def _():
        ref[...] = value

    @pl.when(jnp.logical_not(first))
    def _():
        ref[...] += value


def norm_matmul(x, g, w, *, tn, split, name, ride=None, tm=ROW_TILE):
    T, D = x.shape
    N = w.shape[1]
    per = N // split // tn

    def body(x_ref, g_ref, w_ref, o_ref, xn_ref):
        @pl.when(pl.program_id(1) == 0)
        def _():
            xf = x_ref[...].astype(F32)
            xn_ref[...] = (xf * _rms_r(xf) * g_ref[...]).astype(BF16)

        o_ref[...] = _dot(xn_ref[...], w_ref[...]).astype(BF16)

    return _call(
        body, name=name, grid=(T // tm, N // tn),
        in_specs=[pl.BlockSpec((tm, D), lambda i, j: (i, 0)),
                  pl.BlockSpec((1, D), lambda i, j: (0, 0)),
                  pl.BlockSpec((D, tn), lambda i, j: (0, j))],
        out_specs=[pl.BlockSpec((None, tm, tn), lambda i, j: (j // per, i, j % per)),
                   pl.BlockSpec((tm, D), lambda i, j: (i, 0))],
        out_shape=[SDS((split, T, N // split), BF16), SDS((T, D), BF16)],
        semantics=("parallel", "arbitrary"), args=(x, g, w), ride=ride)


BIG_ROW_TILE = 1024


def norm2_matmul(x, gains, weights, *, name, tm=BIG_ROW_TILE):
    T, D = x.shape
    tm = min(tm, T)
    n = len(gains)

    def body(x_ref, *refs):
        xf = x_ref[...].astype(F32)
        xh = xf * _rms_r(xf)
        for g_ref, w_ref, o_ref, xn_ref in zip(refs[:n], refs[n:2 * n], refs[2 * n::2], refs[2 * n + 1::2]):
            xn = (xh * g_ref[...]).astype(BF16)
            xn_ref[...] = xn
            o_ref[...] = _dot(xn, w_ref[...]).astype(BF16)

    row = pl.BlockSpec((tm, D), lambda i: (i, 0))
    vec = pl.BlockSpec((1, D), lambda i: (0, 0))
    out_specs, out_shape = [], []
    for w in weights:
        out_specs += [pl.BlockSpec((tm, w.shape[1]), lambda i: (i, 0)), row]
        out_shape += [SDS((T, w.shape[1]), BF16), SDS((T, D), BF16)]
    return _call(
        body, name=name, grid=(T // tm,),
        in_specs=[row] + [vec] * n + [pl.BlockSpec(w.shape, lambda i: (0, 0)) for w in weights],
        out_specs=out_specs, out_shape=out_shape, semantics=("parallel",), args=[x] + list(gains) + list(weights))


def _shift_down(prev, cur, by):
    big = jnp.concatenate([prev, cur], axis=0)
    return pltpu.roll(big, by, 0)[prev.shape[0]:]


def _shift_up(cur, nxt, by):
    big = jnp.concatenate([cur, nxt], axis=0)
    return pltpu.roll(big, big.shape[0] - by, 0)[:cur.shape[0]]


def conv_mix_out(bcx, conv_w, w_out, g_post, res, *, name, ride=None, tm=ROW_TILE):
    T, D = res.shape
    hb = tm // BF16_ROWS

    def body(b_ref, c_ref, u_ref, cp_ref, up_ref, cw_ref, w_ref, g_ref, r_ref, h_ref, z_ref, y_ref):
        i = pl.program_id(0)
        cu = c_ref[...].astype(F32) * u_ref[...].astype(F32)
        cup = cp_ref[...].astype(F32) * up_ref[...].astype(F32)
        cup = jnp.where(i == 0, 0.0, cup)
        cv = (cw_ref[0:1, :] * _shift_down(cup, cu, 2) + cw_ref[1:2, :] * _shift_down(cup, cu, 1)
              + cw_ref[2:3, :] * cu)
        y = (b_ref[...].astype(F32) * cv).astype(BF16)
        y_ref[...] = y
        z = _dot(y, w_ref[...])
        z_ref[...] = z.astype(BF16)
        h_ref[...] = (r_ref[...] + z * _rms_r(z) * g_ref[...]).astype(STREAM)

    tile = lambda col: pl.BlockSpec((tm, D), lambda i: (i, col))
    halo = lambda col: pl.BlockSpec((BF16_ROWS, D), lambda i: (jnp.maximum(i * hb - 1, 0), col))
    row = pl.BlockSpec((tm, D), lambda i: (i, 0))
    return _call(
        body, name=name, grid=(T // tm,),
        in_specs=[tile(0), tile(1), tile(2), halo(1), halo(2),
                  pl.BlockSpec((3, D), lambda i: (0, 0)),
                  pl.BlockSpec((D, D), lambda i: (0, 0)),
                  pl.BlockSpec((1, D), lambda i: (0, 0)), row],
        out_specs=[row, row, row],
        out_shape=[SDS((T, D), STREAM), SDS((T, D), BF16), SDS((T, D), BF16)],
        semantics=("parallel",), args=(bcx, bcx, bcx, bcx, bcx, conv_w, w_out, g_post, res), ride=ride)


def _normbwd_then_nt(dh, zf, g_ref, w_ref, dz_ref, dg_ref, o_ref, first):
    dz, zh = _rmsnorm_bwd(zf, g_ref[...], dh)
    dz = dz.astype(BF16)
    dz_ref[...] = dz
    _accumulate(dg_ref, first, jnp.sum(dh * zh, axis=0, keepdims=True))
    o_ref[...] = _dot_nt(dz, w_ref[...]).astype(BF16)


def _then_specs(then, tm, T, D):
    z, g, w = then
    K = w.shape[0]
    row = pl.BlockSpec((tm, D), lambda i: (i, 0))
    vec = pl.BlockSpec((1, D), lambda i: (0, 0))
    in_specs = [row, vec, pl.BlockSpec((K, D), lambda i: (0, 0), pipeline_mode=pl.Buffered(1))]
    out_specs = [row, vec, pl.BlockSpec((tm, K), lambda i: (i, 0))]
    out_shape = [SDS((T, D), BF16), SDS((1, D), F32), SDS((T, K), BF16)]
    return in_specs, out_specs, out_shape


def plain_mix_out(a, w, g_post, res, *, name, target=None, ride=None, tm=ROW_TILE):
    T, D = res.shape
    tm = min(tm, T)
    K = a.shape[1]
    with_loss = target is not None

    def body(a_ref, w_ref, g_ref, r_ref, *rest):
        z = _dot(a_ref[...], w_ref[...])
        h = r_ref[...].astype(F32) + z * _rms_r(z) * g_ref[...]
        if with_loss:
            t_ref, h_ref, dz_ref, dg_ref, da_ref, loss_ref = rest
            first = pl.program_id(0) == 0
            diff = h - t_ref[...]
            dh = diff * (1.0 / D)
            h_ref[...] = dh.astype(STREAM)
            part = jnp.full(loss_ref.shape, 0.5 / D, F32) * jnp.sum(diff * diff)
            _accumulate(loss_ref, first, part)
            _normbwd_then_nt(dh, z, g_ref, w_ref, dz_ref, dg_ref, da_ref, first)
        else:
            h_ref, z_ref = rest
            h_ref[...] = h.astype(STREAM)
            z_ref[...] = z.astype(BF16)

    row = pl.BlockSpec((tm, D), lambda i: (i, 0))
    vec = pl.BlockSpec((1, D), lambda i: (0, 0))
    in_specs = [pl.BlockSpec((tm, K), lambda i: (i, 0)), pl.BlockSpec((K, D), lambda i: (0, 0)), vec, row]
    if with_loss:
        in_specs.append(row)
        out_specs = [row, row, vec, pl.BlockSpec((tm, K), lambda i: (i, 0)), pl.BlockSpec((8, 128), lambda i: (0, 0))]
        out_shape = [SDS((T, D), STREAM), SDS((T, D), BF16), SDS((1, D), F32), SDS((T, K), BF16), SDS((8, 128), F32)]
    else:
        out_specs, out_shape = [row, row], [SDS((T, D), STREAM), SDS((T, D), BF16)]
    return _call(
        body, name=name, grid=(T // tm,), in_specs=in_specs, out_specs=out_specs, out_shape=out_shape,
        semantics=("arbitrary",), args=(a, w, g_post, res) + ((target,) if with_loss else ()), ride=ride)


def _silu_grads(d, g, u):
    sg = jax.nn.sigmoid(g)
    return d * u * (sg * (1.0 + g * (1.0 - sg))), d * (g * sg)


def norm_swiglu_in(x, g, w, *, name, ride=None, tm=ROW_TILE // 2):
    T, D = x.shape
    F = w.shape[1] // 2

    def body(x_ref, g_ref, wg_ref, wu_ref, gu_ref, a_ref, xt_ref):
        xf = x_ref[...].astype(F32)
        xn = xf * _rms_r(xf) * g_ref[...]
        xt_ref[...] = xn.T.astype(BF16)
        xb = xn.astype(BF16)
        gate = _dot(xb, wg_ref[...]).astype(BF16)
        up = _dot(xb, wu_ref[...]).astype(BF16)
        gu_ref[0] = gate
        gu_ref[1] = up
        a_ref[...] = gate * jax.nn.sigmoid(gate) * up

    half = lambda s: pl.BlockSpec((D, F), lambda i: (0, s), pipeline_mode=pl.Buffered(1))
    return _call(
        body, name=name, grid=(T // tm,),
        in_specs=[pl.BlockSpec((tm, D), lambda i: (i, 0)), pl.BlockSpec((1, D), lambda i: (0, 0)), half(0), half(1)],
        out_specs=[pl.BlockSpec((2, tm, F), lambda i: (0, i, 0)), pl.BlockSpec((tm, F), lambda i: (i, 0)),
                   pl.BlockSpec((D, tm), lambda i: (0, i))],
        out_shape=[SDS((2, T, F), BF16), SDS((T, F), BF16), SDS((D, T), BF16)],
        semantics=("parallel",), args=(x, g, w, w), ride=ride)


def swiglu_bwd_tn(xt, dact, gu, *, name, ride=None, tb=MXU_WIDTH):
    D, T = xt.shape
    F = dact.shape[1]

    def body(xt_ref, d_ref, g_ref, u_ref, o_ref):
        dg, du = _silu_grads(d_ref[...], g_ref[...], u_ref[...])
        o_ref[0] = _dot(xt_ref[...], dg).astype(BF16)
        o_ref[1] = _dot(xt_ref[...], du).astype(BF16)

    col = lambda s: pl.BlockSpec((None, T, tb), lambda j: (s, 0, j))
    out = _call(
        body, name=name, grid=(F // tb,),
        in_specs=[pl.BlockSpec((D, T), lambda j: (0, 0), pipeline_mode=pl.Buffered(1)),
                  pl.BlockSpec((T, tb), lambda j: (0, j)), col(0), col(1)],
        out_specs=[pl.BlockSpec((2, D, tb), lambda j: (0, 0, j))],
        out_shape=[SDS((2, D, F), BF16)],
        semantics=("parallel",), args=(xt, dact, gu, gu), ride=ride)
    return out[0] if ride is None else (out[0][0], out[1])


def swiglu_bwd_in(dact, gu, w, h_in, g, dh_out, then, *, name, ride=None, tm=ROW_TILE // 2):
    T, D = h_in.shape
    F = dact.shape[1]

    def body(d_ref, gg_ref, uu_ref, wg_ref, wu_ref, h_ref, g_ref, dh_ref, z_ref, g2_ref, w2_ref,
             o_ref, dg_ref, dz_ref, dg2_ref, da_ref):
        first = pl.program_id(0) == 0
        dgate, dup = _silu_grads(d_ref[...], gg_ref[...], uu_ref[...])
        dn = _dot_nt(dgate, wg_ref[...]) + _dot_nt(dup, wu_ref[...])
        dx, hh = _rmsnorm_bwd(h_ref[...].astype(F32), g_ref[...], dn)
        dh_in = dh_ref[...] + dx
        o_ref[...] = dh_in.astype(STREAM)
        _accumulate(dg_ref, first, jnp.sum(dn * hh, axis=0, keepdims=True))
        _normbwd_then_nt(dh_in, z_ref[...].astype(F32), g2_ref, w2_ref, dz_ref, dg2_ref, da_ref, first)

    row = pl.BlockSpec((tm, D), lambda i: (i, 0))
    vec = pl.BlockSpec((1, D), lambda i: (0, 0))
    part = lambda s: pl.BlockSpec((None, tm, F), lambda i: (s, i, 0))
    half = lambda s: pl.BlockSpec((D, F), lambda i: (0, s), pipeline_mode=pl.Buffered(1))
    then_in, then_out, then_shape = _then_specs(then, tm, T, D)
    return _call(
        body, name=name, grid=(T // tm,),
        in_specs=[pl.BlockSpec((tm, F), lambda i: (i, 0)), part(0), part(1), half(0), half(1), row, vec, row] + then_in,
        out_specs=[row, vec] + then_out,
        out_shape=[SDS((T, D), STREAM), SDS((1, D), F32)] + then_shape,
        semantics=("arbitrary",), args=(dact, gu, gu, w, w, h_in, g, dh_out) + tuple(then), ride=ride)


def rope_tables(T):
    half = ROT_DIM // 2
    inv_freq = ROPE_THETA ** (-jnp.arange(0, ROT_DIM, 2, dtype=F32) / ROT_DIM)
    ang = (jnp.arange(T, dtype=F32)[:, None] * inv_freq[None, :]).T
    cos, sin = jnp.cos(ang), jnp.sin(ang)
    rest = HEAD_DIM - ROT_DIM
    one, zero = jnp.ones((rest, T), F32), jnp.zeros((rest, T), F32)
    zh = jnp.zeros((half, T), F32)
    fac = jnp.concatenate([cos, cos, one], axis=0)
    up = jnp.concatenate([-sin, zh, zero], axis=0)
    down = jnp.concatenate([zh, sin, zero], axis=0)
    return jnp.stack([fac, up, down])


def _rope(t, tab):
    half = ROT_DIM // 2
    return t * tab[0] + pltpu.roll(t, HEAD_DIM - half, 0) * tab[1] + pltpu.roll(t, half, 0) * tab[2]


def _rope_t(d, tab):
    half = ROT_DIM // 2
    return d * tab[0] + pltpu.roll(d * tab[1], half, 0) + pltpu.roll(d * tab[2], HEAD_DIM - half, 0)


def _head(t, h):
    return t[h * HEAD_DIM:(h + 1) * HEAD_DIM]


def _band(n, group):
    kj = lax.broadcasted_iota(jnp.int32, (2 * BLOCK, BLOCK), 0)
    qi = lax.broadcasted_iota(jnp.int32, (2 * BLOCK, BLOCK), 1)
    mask = (kj > qi) & (kj <= qi + BLOCK) & ((n > 0) | (kj >= BLOCK))
    return jnp.tile(mask, (1, group))


def _attn_specs(D, kvd, nb):
    cur = lambda n: jnp.minimum(n, nb - 1)
    prev = lambda n: jnp.maximum(cur(n) - 1, 0)
    return [pl.BlockSpec((BLOCK, D), lambda n: (cur(n), 0)),
            pl.BlockSpec((BLOCK, kvd), lambda n: (prev(n), 0)),
            pl.BlockSpec((BLOCK, kvd), lambda n: (cur(n), 0)),
            pl.BlockSpec((BLOCK, kvd), lambda n: (prev(n), 1)),
            pl.BlockSpec((BLOCK, kvd), lambda n: (cur(n), 1)),
            pl.BlockSpec((3, HEAD_DIM, BLOCK), lambda n: (0, 0, prev(n))),
            pl.BlockSpec((3, HEAD_DIM, BLOCK), lambda n: (0, 0, cur(n))),
            pl.BlockSpec(memory_space=pltpu.SMEM)]


def _attn_operands(q_ref, kp_ref, k_ref, vp_ref, v_ref, tp_ref, t_ref):
    flip = lambda ref: ref[...].astype(F32).T
    tab = t_ref[...]
    kt = jnp.concatenate([flip(kp_ref), flip(k_ref)], axis=1)
    vt = jnp.concatenate([flip(vp_ref), flip(v_ref)], axis=1)
    return flip(q_ref), kt, vt, tab, jnp.concatenate([tp_ref[...], tab], axis=2)


SCORE_SCALE = 1.0 / math.sqrt(HEAD_DIM)
HEADS_TOGETHER = 4


def _group_heads(t, first, count, tab=None):
    heads = [_head(t, first + g) for g in range(count)]
    if tab is not None:
        heads = [_rope(h, tab) * SCORE_SCALE for h in heads]
    return jnp.concatenate(heads, axis=1).astype(BF16)


def _sink_row(s_ref, first, count):
    which = lax.broadcasted_iota(jnp.int32, (1, count * BLOCK), 1) // BLOCK
    row = jnp.zeros((1, count * BLOCK), F32)
    for g in range(count):
        row = jnp.where(which == g, s_ref[0, first + g], row)
    return row


def _sum_keys(t):
    return _dot(jnp.ones((8, t.shape[0]), BF16), t)[0:1]


def _softmax(scores, sink, mask):
    s = jnp.where(mask, scores.astype(BF16), NEG)
    m = jnp.maximum(jnp.max(s, axis=0, keepdims=True).astype(F32), sink).astype(BF16)
    e = jnp.exp(s - m)
    m = m.astype(F32)
    return e, m, 1.0 / (_sum_keys(e) + jnp.exp(sink - m))


def _per_head(row, count):
    return [row[:, g * BLOCK:(g + 1) * BLOCK] for g in range(count)]


def attention_fwd(q, kv, tabs, sinks, *, name, ride=None):
    T, D = q.shape
    kvd = kv.shape[1] // 2
    heads = D // HEAD_DIM
    group = heads // N_KV_HEADS

    def body(q_ref, kp_ref, k_ref, vp_ref, v_ref, tp_ref, t_ref, s_ref, o_ref, stat_ref):
        gs = HEADS_TOGETHER
        mask = _band(pl.program_id(0), gs)
        qt, kt, vt, tab, tab2 = _attn_operands(q_ref, kp_ref, k_ref, vp_ref, v_ref, tp_ref, t_ref)
        firsts = [(j, first) for j in range(N_KV_HEADS) for first in range(j * group, (j + 1) * group, gs)]
        ks = [_rope(_head(kt, j), tab2).astype(BF16) for j in range(N_KV_HEADS)]
        scores = [_dot_tn(ks[j], _group_heads(qt, first, gs, tab)) for j, first in firsts]
        soft = [_softmax(s, _sink_row(s_ref, first, gs), mask) for s, (j, first) in zip(scores, firsts)]
        outs, ms, invs = [], [], []
        for (e, m, inv), (j, first) in zip(soft, firsts):
            o = _dot(_head(vt, j).astype(BF16), e) * inv
            outs += [o[:, g * BLOCK:(g + 1) * BLOCK] for g in range(gs)]
            ms += _per_head(m, gs)
            invs += _per_head(inv, gs)
        o_ref[...] = jnp.concatenate(outs, axis=0).T.astype(BF16)
        stat_ref[0] = jnp.concatenate(ms, axis=0)
        stat_ref[1] = jnp.concatenate(invs, axis=0)

    return _call(
        body, name=name, grid=(T // BLOCK,),
        in_specs=_attn_specs(D, kvd, T // BLOCK),
        out_specs=[pl.BlockSpec((BLOCK, D), lambda n: (n, 0)), pl.BlockSpec((2, heads, BLOCK), lambda n: (0, 0, n))],
        out_shape=[SDS((T, D), BF16), SDS((2, heads, T), F32)],
        semantics=("parallel",), args=(q, kv, kv, kv, kv, tabs, tabs, sinks), ride=ride)


def attention_bwd(q, kv, tabs, sinks, do, o, stats, *, name, ride=None):
    T, D = q.shape
    kvd = kv.shape[1] // 2
    heads = D // HEAD_DIM
    group = heads // N_KV_HEADS
    nb = T // BLOCK

    def body(q_ref, kp_ref, k_ref, vp_ref, v_ref, tp_ref, t_ref, s_ref, do_ref, o_ref, stat_ref,
             dq_ref, dkv_ref, ds_ref, carry):
        n = pl.program_id(0)

        @pl.when(n == 0)
        def _():
            carry[...] = jnp.zeros_like(carry)

        @pl.when(n < nb)
        def _():
            block(n, q_ref, kp_ref, k_ref, vp_ref, v_ref, tp_ref, t_ref, s_ref, do_ref, o_ref, stat_ref,
                  dq_ref, dkv_ref, ds_ref, carry)

        @pl.when(n == nb)
        def _():
            dkv_ref[...] = carry[...].astype(BF16)

    def block(n, q_ref, kp_ref, k_ref, vp_ref, v_ref, tp_ref, t_ref, s_ref, do_ref, o_ref, stat_ref,
              dq_ref, dkv_ref, ds_ref, carry):
        gs = HEADS_TOGETHER
        mask = _band(n, gs)
        qt, kt, vt, tab, tab2 = _attn_operands(q_ref, kp_ref, k_ref, vp_ref, v_ref, tp_ref, t_ref)
        dot = do_ref[...].astype(F32).T
        odo = o_ref[...].astype(F32).T * dot
        dl_all = jnp.concatenate([jnp.sum(_head(odo, h), axis=0, keepdims=True) for h in range(heads)], axis=0)
        m_all, inv_all = stat_ref[0], stat_ref[1]
        row = lambda t, first: jnp.concatenate([t[first + g:first + g + 1] for g in range(gs)], axis=1)
        lane = lax.broadcasted_iota(jnp.int32, (8, 128), 1)
        dsink = jnp.zeros((8, 128), F32)
        firsts = [(j, first) for j in range(N_KV_HEADS) for first in range(j * group, (j + 1) * group, gs)]
        ks = [_rope(_head(kt, j), tab2).astype(BF16) for j in range(N_KV_HEADS)]
        vs = [_head(vt, j).astype(BF16) for j in range(N_KV_HEADS)]
        qs = [_group_heads(qt, first, gs, tab) for _, first in firsts]
        dos = [_group_heads(dot, first, gs) for _, first in firsts]
        scores = [_dot_tn(ks[j], q) for q, (j, _) in zip(qs, firsts)]
        dps = [_dot_tn(vs[j], do) for do, (j, _) in zip(dos, firsts)]
        ps, dscs = [], []
        for s, dp, (j, first) in zip(scores, dps, firsts):
            m, inv, dl = row(m_all, first), row(inv_all, first), row(dl_all, first)
            e = jnp.exp(jnp.where(mask, s.astype(BF16), NEG) - m.astype(BF16))
            p = e * inv.astype(BF16)
            dscs.append(p * (dp.astype(BF16) - dl.astype(BF16)))
            ps.append(p)
            weight = jnp.exp(_sink_row(s_ref, first, gs) - m) * inv * dl
            for g in range(gs):
                dsink = dsink - jnp.where(lane == first + g, jnp.sum(weight[:, g * BLOCK:(g + 1) * BLOCK]), 0.0)
        dqs = []
        dks = [jnp.zeros((HEAD_DIM, 2 * BLOCK), F32) for _ in range(N_KV_HEADS)]
        dvs = [jnp.zeros((HEAD_DIM, 2 * BLOCK), F32) for _ in range(N_KV_HEADS)]
        for p, dsc, q, do, (j, _) in zip(ps, dscs, qs, dos, firsts):
            dq = _dot(ks[j], dsc) * SCORE_SCALE
            dqs += [_rope_t(dq[:, g * BLOCK:(g + 1) * BLOCK], tab) for g in range(gs)]
            dks[j] = dks[j] + _dot_nt(q, dsc)
            dvs[j] = dvs[j] + _dot_nt(do, p)
        dks = [_rope_t(dk, tab2) for dk in dks]
        dq_ref[...] = jnp.concatenate(dqs, axis=0).T.astype(BF16)
        dkv = jnp.concatenate(dks + dvs, axis=0)
        dkv_ref[...] = (carry[...] + dkv[:, :BLOCK].T).astype(BF16)
        carry[...] = dkv[:, BLOCK:].T
        _accumulate(ds_ref, n == 0, dsink)

    cur = lambda n: jnp.minimum(n, nb - 1)
    blk = lambda w: pl.BlockSpec((BLOCK, w), lambda n: (cur(n), 0))
    return _call(
        body, name=name, grid=(nb + 1,),
        in_specs=_attn_specs(D, kvd, nb) + [blk(D), blk(D), pl.BlockSpec((2, heads, BLOCK), lambda n: (0, 0, cur(n)))],
        out_specs=[blk(D), pl.BlockSpec((BLOCK, 2 * kvd), lambda n: (jnp.maximum(n - 1, 0), 0)),
                   pl.BlockSpec((8, 128), lambda n: (0, 0))],
        out_shape=[SDS((T, D), BF16), SDS((T, 2 * kvd), BF16), SDS((8, 128), F32)],
        scratch_shapes=[pltpu.VMEM((BLOCK, 2 * kvd), F32)],
        semantics=("arbitrary",), args=(q, kv, kv, kv, kv, tabs, tabs, sinks, do, o, stats), ride=ride)


def matmul_nt_normbwd(da, w, h_in, g, dh_out, *, name, ride=None, tm=ROW_TILE):
    T, D = h_in.shape
    S, _, K = da.shape

    def body(*refs):
        da_refs, w_refs = refs[:S], refs[S:2 * S]
        h_ref, g_ref, dh_ref, o_ref, dg_ref = refs[2 * S:]
        dn = _dot_nt(da_refs[0][...], w_refs[0][...])
        for s in range(1, S):
            dn = dn + _dot_nt(da_refs[s][...], w_refs[s][...])
        dx, hh = _rmsnorm_bwd(h_ref[...].astype(F32), g_ref[...], dn)
        o_ref[...] = dh_ref[...] + dx
        _accumulate(dg_ref, pl.program_id(0) == 0, jnp.sum(dn * hh, axis=0, keepdims=True))

    row = pl.BlockSpec((tm, D), lambda i: (i, 0))
    vec = pl.BlockSpec((1, D), lambda i: (0, 0))
    part = lambda s: pl.BlockSpec((None, tm, K), lambda i: (s, i, 0))
    cols = lambda s: pl.BlockSpec((D, K), lambda i: (0, s), pipeline_mode=pl.Buffered(1))
    return _call(
        body, name=name, grid=(T // tm,),
        in_specs=[part(s) for s in range(S)] + [cols(s) for s in range(S)] + [row, vec, row],
        out_specs=[row, vec],
        out_shape=[SDS((T, D), F32), SDS((1, D), F32)],
        semantics=("arbitrary",), args=[da] * S + [w] * S + [h_in, g, dh_out], ride=ride)


def matmuls_nt_normbwd(das, ws, h_in, gs, dh_out, then, *, name, ride=None, tm=ROW_TILE):
    T, D = h_in.shape
    tm = min(tm, T)
    n = len(das)

    def body(*refs):
        da_refs, w_refs, g_refs = refs[:n], refs[n:2 * n], refs[2 * n:3 * n]
        h_ref, dh_ref, z_ref, g2_ref, w2_ref, o_ref = refs[3 * n:3 * n + 6]
        dg_refs, (dz_ref, dg2_ref, da_ref) = refs[3 * n + 6:4 * n + 6], refs[4 * n + 6:]
        first = pl.program_id(0) == 0
        hf = h_ref[...].astype(F32)
        r = _rms_r(hf)
        hh = hf * r
        total = dh_ref[...].astype(F32)
        for da_ref_, w_ref, g_ref, dg_ref in zip(da_refs, w_refs, g_refs, dg_refs):
            dn = _dot_nt(da_ref_[...], w_ref[...])
            gd = g_ref[...] * dn
            total = total + r * (gd - hh * jnp.mean(hh * gd, axis=-1, keepdims=True))
            _accumulate(dg_ref, first, jnp.sum(dn * hh, axis=0, keepdims=True))
        o_ref[...] = total.astype(STREAM)
        _normbwd_then_nt(total, z_ref[...].astype(F32), g2_ref, w2_ref, dz_ref, dg2_ref, da_ref, first)

    row = pl.BlockSpec((tm, D), lambda i: (i, 0))
    vec = pl.BlockSpec((1, D), lambda i: (0, 0))
    then_in, then_out, then_shape = _then_specs(then, tm, T, D)
    return _call(
        body, name=name, grid=(T // tm,),
        in_specs=[pl.BlockSpec((tm, da.shape[1]), lambda i: (i, 0)) for da in das]
        + [pl.BlockSpec(w.shape, lambda i: (0, 0)) for w in ws] + [vec] * n + [row, row] + then_in,
        out_specs=[row] + [vec] * n + then_out,
        out_shape=[SDS((T, D), STREAM)] + [SDS((1, D), F32)] * n + then_shape,
        semantics=("arbitrary",), args=list(das) + list(ws) + list(gs) + [h_in, dh_out] + list(then), ride=ride)


def matmul_tn(a, b, *, tb, name, ride=None, ta=MXU_WIDTH):
    T, Ka = a.shape
    S, _, Nb = b.shape
    per = Nb // tb

    def body(a_ref, b_ref, o_ref):
        o_ref[...] = _dot_tn(a_ref[...], b_ref[...]).astype(BF16)

    out = _call(
        body, name=name, grid=(S * per, Ka // ta),
        in_specs=[pl.BlockSpec((T, ta), lambda j, i: (0, i)),
                  pl.BlockSpec((None, T, tb), lambda j, i: (j // per, 0, j % per))],
        out_specs=[pl.BlockSpec((ta, tb), lambda j, i: (i, j))],
        out_shape=[SDS((Ka, S * Nb), BF16)],
        semantics=("parallel", "parallel"), args=(a, b), ride=ride)
    return out[0] if ride is None else (out[0][0], out[1])


def conv_bwd(dy, bcx, conv_w, *, name, ride=None, tm=ROW_TILE):
    T, D = dy.shape
    nt = T // tm
    hb = tm // BF16_ROWS
    last = T // BF16_ROWS - 1

    def body(dy_ref, dyn_ref, b_ref, bn_ref, c_ref, u_ref, cp_ref, up_ref, cw_ref, o_ref, dw_ref):
        i = pl.program_id(0)
        c, u = c_ref[...].astype(F32), u_ref[...].astype(F32)
        cu = c * u
        cup = jnp.where(i == 0, 0.0, cp_ref[...].astype(F32) * up_ref[...].astype(F32))
        cu1, cu2 = _shift_down(cup, cu, 1), _shift_down(cup, cu, 2)
        w0, w1, w2 = cw_ref[0:1, :], cw_ref[1:2, :], cw_ref[2:3, :]
        dyf = dy_ref[...].astype(F32)
        o_ref[:, 0:D] = (dyf * (w0 * cu2 + w1 * cu1 + w2 * cu)).astype(BF16)
        dcv = dyf * b_ref[...].astype(F32)
        dcvn = jnp.where(i == nt - 1, 0.0, dyn_ref[...].astype(F32) * bn_ref[...].astype(F32))
        dcu = w2 * dcv + w1 * _shift_up(dcv, dcvn, 1) + w0 * _shift_up(dcv, dcvn, 2)
        o_ref[:, D:2 * D] = (dcu * u).astype(BF16)
        o_ref[:, 2 * D:3 * D] = (dcu * c).astype(BF16)
        row = lax.broadcasted_iota(jnp.int32, (8, D), 0)
        dw = jnp.zeros((8, D), F32)
        for tap, t in enumerate((cu2, cu1, cu)):
            dw = jnp.where(row == tap, jnp.sum(dcv * t, axis=0, keepdims=True), dw)
        _accumulate(dw_ref, i == 0, dw)

    tile = lambda col: pl.BlockSpec((tm, D), lambda i: (i, col))
    prev = lambda col: pl.BlockSpec((BF16_ROWS, D), lambda i: (jnp.maximum(i * hb - 1, 0), col))
    nxt = lambda col: pl.BlockSpec((BF16_ROWS, D), lambda i: (jnp.minimum((i + 1) * hb, last), col))
    return _call(
        body, name=name, grid=(nt,),
        in_specs=[tile(0), nxt(0), tile(0), nxt(0), tile(1), tile(2), prev(1), prev(2),
                  pl.BlockSpec((3, D), lambda i: (0, 0))],
        out_specs=[pl.BlockSpec((tm, 3 * D), lambda i: (i, 0)), pl.BlockSpec((8, D), lambda i: (0, 0))],
        out_shape=[SDS((T, 3 * D), BF16), SDS((8, D), F32)],
        semantics=("arbitrary",), args=(dy, dy, bcx, bcx, bcx, bcx, bcx, bcx, conv_w), ride=ride)


class NoTraffic:
    def ride(self, kernel_name):
        return None

    def landed(self, kernel_name, results, wts):
        pass

    def grad(self, key, value):
        pass


def local_step(x, target, wts, vec, traffic):
    T, D = x.shape
    tabs = rope_tables(T)
    small = {}

    def run(builder, *args, name, **kw):
        ride = traffic.ride(name)
        if ride is None:
            return builder(*args, name=name, **kw)
        out, extra = builder(*args, name=name, ride=ride, **kw)
        traffic.landed(name, extra, wts)
        return out

    bcx, xn1 = run(norm_matmul, x, vec["a_pre"], wts["w_in"], tn=3 * D, split=1, name="a_in")
    bcx = bcx[0]
    h1, z0, y0 = run(conv_mix_out, bcx, vec["conv_w"], wts["w_out"], vec["a_post"], x, name="a_out")
    gu0, act0, xt2 = run(norm_swiglu_in, h1, vec["ffn_pre0"], wts["gu0"], name="ffn0_in")
    h2, z1 = run(plain_mix_out, act0, wts["wd0"], vec["ffn_post0"], h1, name="ffn0_out")
    kvp, xkv, qp, xq = norm2_matmul(h2, [vec["kv_norm"], vec["b_pre"]], [wts["w_kv"], wts["w_q"]], name="kvq_in")
    attn, attn_stats = run(attention_fwd, qp, kvp, tabs, vec["sinks"], name="attn_fwd")
    h3, z2 = plain_mix_out(attn, wts["w_o"], vec["b_post"], h2, name="attn_out", tm=BIG_ROW_TILE)
    gu1, act1, xt3 = run(norm_swiglu_in, h3, vec["ffn_pre1"], wts["gu1"], name="ffn1_in")
    dy, dz3, small["ffn_post1"], dact1, loss = plain_mix_out(act1, wts["wd1"], vec["ffn_post1"], h3, name="ffn1_out",
                                                             target=target)

    def ffn_bwd(layer, dz, dact, gu, act, xt, h_in, dh, then, gu_first):
        tag = "ffn%d" % layer
        dwd = lambda: traffic.grad("wd%d" % layer, run(matmul_tn, act, dz[None], tb=D, name=tag + "_dwd"))
        dwgu = lambda: traffic.grad("gu%d" % layer, run(swiglu_bwd_tn, xt, dact, gu, name=tag + "_dwgu"))
        for step in ((dwgu, dwd) if gu_first else (dwd, dwgu)):
            step()
        dh_in, small["ffn_pre%d" % layer], dz_, dg_, da_ = run(
            swiglu_bwd_in, dact, gu, wts["gu%d" % layer], h_in, vec["ffn_pre%d" % layer], dh, then,
            name=tag + "_in_bwd")
        return dh_in, dz_, dg_, da_

    dh3, dz2, small["b_post"], dattn = ffn_bwd(1, dz3, dact1, gu1, act1, xt3, h3, dy,
                                               (z2, vec["b_post"], wts["w_o"]), gu_first=False)
    traffic.grad("w_o", matmul_tn(attn, dz2[None], tb=D, name="attn_dwo"))
    dq, dkv, small["sinks"] = run(attention_bwd, qp, kvp, tabs, vec["sinks"], dattn, attn, attn_stats,
                                  name="attn_bwd")
    traffic.grad("w_q", matmul_tn(xq, dq[None], tb=D, name="attn_dwq"))
    traffic.grad("w_kv", matmul_tn(xkv, dkv[None], tb=dkv.shape[1], name="attn_dwkv"))
    dh2, small["b_pre"], small["kv_norm"], dz1, small["ffn_post0"], dact0 = run(
        matmuls_nt_normbwd, [dq, dkv], [wts["w_q"], wts["w_kv"]], h2, [vec["b_pre"], vec["kv_norm"]], dh3,
        (z1, vec["ffn_post0"], wts["wd0"]), name="qkv_in_bwd")
    dh1, dz0, small["a_post"], dyc = ffn_bwd(0, dz1, dact0, gu0, act0, xt2, h1, dh2,
                                             (z0, vec["a_post"], wts["w_out"]), gu_first=True)
    traffic.grad("w_out", run(matmul_tn, y0, dz0[None], tb=D, name="a_dwout"))
    dbcx, small["conv_w"] = run(conv_bwd, dyc, bcx, vec["conv_w"], name="a_conv_bwd")
    traffic.grad("w_in", matmul_tn(xn1, dbcx[None], tb=3 * D // 2, name="a_dwin"))
    dx, small["a_pre"] = run(matmul_nt_normbwd, dbcx[None], wts["w_in"], x, vec["a_pre"], dh1, name="a_in_bwd",
                             tm=ROW_TILE // 2)
    return loss, dx, small


SMALL_ROWS = 16
LOSS_ROW = 13

WHOLE = None
GATHER_PLAN = {"cast_rest": [("w_in", WHOLE)],
               "a_in": [("w_out", WHOLE), ("gu0", (0, 18))],
               "a_out": [("gu0", (18, 14))],
               "ffn0_in": [("wd0", WHOLE), ("w_kv", WHOLE), ("w_q", WHOLE), ("w_o", WHOLE)],
               "ffn0_out": [("gu1", (0, 16))],
               "attn_fwd": [("gu1", (16, 16))],
               "ffn1_in": [("wd1", WHOLE)]}
PAIR_PLAN = {"ffn1_dwgu": ["wd1"], "ffn1_in_bwd": ["gu1"], "attn_bwd": ["w_o"], "qkv_in_bwd": ["w_q", "w_kv"],
             "ffn0_dwd": ["gu0"], "ffn0_in_bwd": ["wd0"], "a_conv_bwd": ["w_out"]}
PAIR_ALONE = ["w_in"]
CHIP_PLAN = {"ffn1_in_bwd": [("wd1", WHOLE)], "attn_bwd": [("gu1", WHOLE)],
             "ffn0_dwgu": [("w_o", WHOLE), ("w_q", WHOLE), ("w_kv", WHOLE)],
             "ffn0_in_bwd": [("gu0", WHOLE)], "a_dwout": [("wd0", (0, 8))], "a_conv_bwd": [("wd0", (8, 14))],
             "a_in_bwd": [("w_out", WHOLE), ("w_in", WHOLE)]}
HALF_PLAN = {"a_in_bwd": ["gu0", "gu1", "wd0", "wd1", "w_kv", "w_q", "w_o"]}
GRAD_KIND = dict(KIND, gu0="split", gu1="split")


class Traffic:
    def __init__(self, wholes, quarter, c_arr, pc_arr):
        self.wholes, self.quarter, self.c_arr, self.pc_arr = wholes, quarter, c_arr, pc_arr
        self.views, self.sums, self.got = {}, {}, {}
        self.reduced = {}
        self.stages = {}

    def reduce(self, keys, name):
        return chip_reduce([self.sums[k] for k in keys], [self.got[k] for k in keys], [GRAD_KIND[k] for k in keys],
                           self.pc_arr, name=name)

    def ride(self, name, small=None):
        rides, stages = [], []
        if name in GATHER_PLAN:
            plan = GATHER_PLAN[name]
            rides.append(gather_ride([self.wholes[k] for k, _ in plan],
                                     [(KIND[k], self.quarter[k], part) for k, part in plan], small))
            stages.append(("gather", [k for k, _ in plan]))
        if name in CHIP_PLAN:
            plan = CHIP_PLAN[name]
            rides.append(chip_ride([self.sums[k] for k, _ in plan],
                                   [(GRAD_KIND[k], self.quarter[k], part) for k, part in plan],
                                   earlier=[self.got.get(k) for k, _ in plan]))
            stages.append(("chip", [k for k, _ in plan]))
        if name in PAIR_PLAN:
            keys = PAIR_PLAN[name]
            rides.append(pair_ride([self.views[k] for k in keys]))
            stages.append(("pair", keys))
        if name in HALF_PLAN:
            keys = HALF_PLAN[name]
            rides.append(half_ride(self.reduce(keys, "chip_reduce_early")))
            stages.append(("half", keys))
        self.stages[name] = stages
        return join(rides)

    def landed(self, name, results, wts):
        results = list(results)
        for stage, keys in self.stages[name]:
            mine, results = results[:len(keys)], results[len(keys):]
            if stage == "gather":
                for k, whole in zip(keys, mine):
                    self.wholes[k] = wts[k] = whole
            elif stage == "chip":
                self.got.update(zip(keys, mine))
            elif stage == "half":
                self.reduced.update(zip(keys, mine))
            else:
                for k, got in zip(keys, mine):
                    self.sums[k] = pair_add(self.views[k], got, self.c_arr, name="pair_add_" + k)

    def grad(self, key, value):
        r, ws = self.quarter[key]
        view = {"row": (N_CHIPS, 2, r // 2, ws), "col": (1, 2, r // 2, N_CHIPS * ws), "split": (2, 2, r // 2, 2 * ws)}
        self.views[key] = value.reshape(view[GRAD_KIND[key]])
        if key in PAIR_ALONE:
            (got,) = alone(pair_ride([self.views[key]]), name="pair_exchange_" + key)
            self.sums[key] = pair_add(self.views[key], got, self.c_arr, name="pair_add_" + key)


def kernel(x, a_pre_norm, a_w_in, a_conv_w, a_w_out, a_post_norm, ffn_pre_norm, ffn_w_gate_up, ffn_w_down, ffn_post_norm, kv_norm, w_kv, b_pre_norm, b_w_q, b_sinks, b_w_o, b_post_norm, loss_target, m_a_pre_norm, m_a_w_in, m_a_conv_w, m_a_w_out, m_a_post_norm, m_ffn_pre_norm, m_ffn_w_gate_up, m_ffn_w_down, m_ffn_post_norm, m_kv_norm, m_w_kv, m_b_pre_norm, m_b_w_q, m_b_sinks, m_b_w_o, m_b_post_norm, v_a_pre_norm, v_a_w_in, v_a_conv_w, v_a_w_out, v_a_post_norm, v_ffn_pre_norm, v_ffn_w_gate_up, v_ffn_w_down, v_ffn_post_norm, v_kv_norm, v_w_kv, v_b_pre_norm, v_b_w_q, v_b_sinks, v_b_w_o, v_b_post_norm):
    T, D = x.shape[1], x.shape[2]
    xi, yi, ci = _place()
    p = 2 * xi + yi
    p_arr = jnp.reshape(p, (1,)).astype(jnp.int32)
    c_arr = jnp.reshape(ci, (1,)).astype(jnp.int32)
    pc_arr = jnp.stack([p, ci]).astype(jnp.int32)
    me_arr = jnp.reshape(4 * xi + 2 * yi + ci, (1,)).astype(jnp.int32)
    qd = D // N_CHIPS

    big = {"w_in": (a_w_in, 0), "w_out": (a_w_out, 0), "gu0": (ffn_w_gate_up, 0), "gu1": (ffn_w_gate_up, 1),
           "wd0": (ffn_w_down, 0), "wd1": (ffn_w_down, 1), "w_kv": (w_kv[None], 0), "w_q": (b_w_q, 0),
           "w_o": (b_w_o, 0)}
    names = list(big)
    quarter = {k: w.shape[1:] for k, (w, _) in big.items()}
    source = lambda k: big[k] + (KIND[k],)
    traffic = Traffic(dict(zip(names[:1], cast_quarters([source(names[0])], p_arr, name="cast_first"))), quarter,
                      c_arr, pc_arr)
    small_shard = jnp.concatenate([a_pre_norm, a_post_norm, a_conv_w[0], jnp.zeros((3, qd), F32)], axis=0)
    wts = {}
    rest, (*landed, small_full) = cast_quarters([source(k) for k in names[1:]], p_arr, name="cast_rest",
                                                ride=traffic.ride("cast_rest", small_shard))
    traffic.wholes.update(zip(names[1:], rest))
    traffic.landed("cast_rest", landed, wts)
    rows = lambda k: jnp.transpose(small_full[:, k], (1, 0, 2)).reshape(-1, D)
    vec = {"a_pre": rows(slice(0, 1)), "a_post": rows(slice(1, 2)), "conv_w": rows(slice(2, 5)),
           "ffn_pre0": ffn_pre_norm[0:1], "ffn_pre1": ffn_pre_norm[1:2],
           "ffn_post0": ffn_post_norm[0:1], "ffn_post1": ffn_post_norm[1:2],
           "kv_norm": kv_norm[None], "b_pre": b_pre_norm, "b_post": b_post_norm, "sinks": b_sinks}

    loss, dx, small = local_step(x[0], loss_target[0], wts, vec, traffic)

    pad = lambda a: jnp.pad(a, ((0, 0), (0, D - a.shape[1])))
    small_block = jnp.concatenate(
        [small["a_pre"], small["a_post"], small["conv_w"][0:3], small["ffn_pre0"], small["ffn_pre1"],
         small["ffn_post0"], small["ffn_post1"], small["kv_norm"], small["b_pre"], small["b_post"],
         pad(small["sinks"][0:1]), pad(loss[0:1]), jnp.zeros((SMALL_ROWS - LOSS_ROW - 1, D), F32)], axis=0)
    late = [k for k in names if k not in traffic.reduced]
    *swapped, small_blocks = alone(join([half_ride(traffic.reduce(late, "chip_reduce_late")),
                                         chip_ride([], [], small_block)]), name="last_exchange")
    traffic.reduced.update(zip(late, swapped))
    grad = {k: traffic.reduced[k].reshape(quarter[k]) for k in names}
    small_sum = small_reduce(small_blocks, me_arr)

    out = {}
    out["a_w_in"] = adamw(a_w_in, [grad["w_in"]], m_a_w_in, v_a_w_in, name="adamw_a_w_in")
    out["a_w_out"] = adamw(a_w_out, [grad["w_out"]], m_a_w_out, v_a_w_out, name="adamw_a_w_out")
    out["ffn_w_gate_up"] = adamw(ffn_w_gate_up, [grad["gu0"], grad["gu1"]], m_ffn_w_gate_up, v_ffn_w_gate_up,
                                 name="adamw_ffn_w_gate_up")
    out["ffn_w_down"] = adamw(ffn_w_down, [grad["wd0"], grad["wd1"]], m_ffn_w_down, v_ffn_w_down,
                              name="adamw_ffn_w_down")
    out["w_kv"] = [o[0] for o in adamw(w_kv[None], [grad["w_kv"]], m_w_kv[None], v_w_kv[None], name="adamw_w_kv")]
    out["b_w_q"] = adamw(b_w_q, [grad["w_q"]], m_b_w_q, v_b_w_q, name="adamw_b_w_q")
    out["b_w_o"] = adamw(b_w_o, [grad["w_o"]], m_b_w_o, v_b_w_o, name="adamw_b_w_o")

    def pack(a_pre, a_post, conv, ffn_pre, ffn_post, kvn, b_pre, b_post, sinks):
        return jnp.concatenate([pad(a_pre), pad(a_post), pad(conv[0]), ffn_pre, ffn_post, kvn[None], b_pre, b_post,
                                pad(sinks), jnp.zeros((SMALL_ROWS - 13, D), F32)], axis=0)

    g_small = jnp.concatenate([pad(lax.dynamic_slice(small_sum, (0, p * qd), (5, qd))), small_sum[5:]], axis=0)
    w_small = pack(a_pre_norm, a_post_norm, a_conv_w, ffn_pre_norm, ffn_post_norm, kv_norm, b_pre_norm, b_post_norm,
                   b_sinks)
    m_small = pack(m_a_pre_norm, m_a_post_norm, m_a_conv_w, m_ffn_pre_norm, m_ffn_post_norm, m_kv_norm,
                   m_b_pre_norm, m_b_post_norm, m_b_sinks)
    v_small = pack(v_a_pre_norm, v_a_post_norm, v_a_conv_w, v_ffn_pre_norm, v_ffn_post_norm, v_kv_norm,
                   v_b_pre_norm, v_b_post_norm, v_b_sinks)
    packed = adamw(w_small[None], [g_small], m_small[None], v_small[None], name="adamw_small")
    ns = b_sinks.shape[1]
    unpack = lambda a: {"a_pre_norm": a[0:1, :qd], "a_post_norm": a[1:2, :qd], "a_conv_w": a[None, 2:5, :qd],
                        "ffn_pre_norm": a[5:7], "ffn_post_norm": a[7:9], "kv_norm": a[9], "b_pre_norm": a[10:11],
                        "b_post_norm": a[11:12], "b_sinks": a[12:13, :ns]}
    unpacked = [unpack(a[0]) for a in packed]
    for k in unpacked[0]:
        out[k] = [u[k] for u in unpacked]

    order = ["a_pre_norm", "a_w_in", "a_conv_w", "a_w_out", "a_post_norm", "ffn_pre_norm", "ffn_w_gate_up",
             "ffn_w_down", "ffn_post_norm", "kv_norm", "w_kv", "b_pre_norm", "b_w_q", "b_sinks", "b_w_o",
             "b_post_norm"]
    return (small_sum[LOSS_ROW, 0], dx[None], *[out[k][0] for k in order], *[out[k][1] for k in order],
            *[out[k][2] for k in order], *[out[k][3] for k in order])
```

```python
import math

import jax
import jax.numpy as jnp
from jax import lax
from jax.experimental import pallas as pl
from jax.experimental.pallas import tpu as pltpu

F32 = jnp.float32
BF16 = jnp.bfloat16
SDS = jax.ShapeDtypeStruct
MESH = pl.DeviceIdType.MESH
DMA = pltpu.SemaphoreType.DMA
HBM_SPEC = pl.BlockSpec(memory_space=pltpu.HBM)

EPS = 1e-6
NEG = -1e30
HEAD_DIM = 64
N_KV_HEADS = 4
BLOCK = 128
ROT_DIM = HEAD_DIM // 4
ROPE_THETA = 500000.0
N_CHIPS = 4

ADAM_LR = 0.001
ADAM_B1 = 0.9
ADAM_B2 = 0.999
ADAM_EPS = 1e-08
ADAM_WD = 0.01
ADAM_STEP = 10

VMEM_LIMIT_BYTES = 52 * 1024 * 1024
ROW_TILE = 512
BF16_ROWS = 16
STREAM = BF16
MXU_WIDTH = 256

KIND = {"w_in": "col", "gu0": "col", "gu1": "col", "w_out": "row", "wd0": "row", "wd1": "row", "w_kv": "row",
        "w_q": "row", "w_o": "row"}


def _params(*semantics):
    return pltpu.CompilerParams(dimension_semantics=semantics, vmem_limit_bytes=VMEM_LIMIT_BYTES)


def _row_tile(rows, limit, step=8):
    return max(t for t in range(step, limit + 1, step) if rows % t == 0)


def _place():
    return lax.axis_index("x"), lax.axis_index("y"), lax.axis_index("c")


def _other_chips(x, y):
    return [(1 - x, y), (x, 1 - y), (1 - x, 1 - y)]


def _remote(src, dst, send_sem, recv_sem, to):
    return pltpu.make_async_remote_copy(src_ref=src, dst_ref=dst, send_sem=send_sem, recv_sem=recv_sem,
                                        device_id=to, device_id_type=MESH)


def _full_shape(kind, quarter):
    r, ws = quarter
    return (N_CHIPS * r, ws) if kind == "row" else (r, N_CHIPS * ws)


def _rows_of(h, part):
    lo, n = (0, h) if part is None else (part[0] * BF16_ROWS, part[1] * BF16_ROWS)
    assert lo + n <= h, (h, part)
    return lo, n


def _half_of_quarter(ref, kind, quarter, part, q, half):
    r, ws = quarter
    h = r // 2
    lo, n = _rows_of(h, part)
    if kind == "row":
        return ref.at[pl.ds(pl.multiple_of(q * r + half * h + lo, BF16_ROWS), n)]
    return ref.at[pl.ds(pl.multiple_of(half * h + lo, BF16_ROWS), n), pl.ds(pl.multiple_of(q * ws, 128), ws)]


class Ride:
    def __init__(self, operands, out_shape, aliases, sems, make):
        self.operands, self.out_shape, self.aliases, self.sems, self.make = operands, out_shape, aliases, sems, make


def join(rides):
    rides = [r for r in rides if r is not None]
    if len(rides) < 2:
        return rides[0] if rides else None
    aliases, at = {}, [0, 0, 0]
    cuts = []
    for r in rides:
        aliases.update({at[0] + i: at[1] + o for i, o in r.aliases.items()})
        cuts.append(tuple(at))
        at = [at[0] + len(r.operands), at[1] + len(r.out_shape), at[2] + len(r.sems)]
    cuts.append(tuple(at))

    def make(ins, outs, sem):
        made = [r.make(ins[lo[0]:hi[0]], outs[lo[1]:hi[1]], sem[lo[2]:hi[2]]) for r, lo, hi in zip(rides, cuts, cuts[1:])]

        def start():
            for s, _ in made:
                s()

        def finish():
            for _, f in made:
                f()

        return start, finish

    return Ride(sum((list(r.operands) for r in rides), []), sum((list(r.out_shape) for r in rides), []), aliases,
                sum((list(r.sems) for r in rides), []), make)


def _call(body, *, name, grid, in_specs, out_specs, out_shape, args, scratch_shapes=(), semantics=None, ride=None,
          prefetch=None):
    pre = 0 if prefetch is None else 1
    n_in, n_out, n_scr = len(in_specs), len(out_specs), len(scratch_shapes)
    r_in, r_out = (len(ride.operands), len(ride.out_shape)) if ride is not None else (0, 0)
    a, b = pre + n_in, pre + n_in + r_in
    c, d = b + n_out, b + n_out + r_out
    e = d + n_scr

    def riding(*refs):
        start, finish = ride.make(refs[a:b], refs[c:d], refs[e:])
        ids = [pl.program_id(k) for k in range(len(grid))]
        first, last = ids[0] == 0, ids[0] == grid[0] - 1
        for k in range(1, len(grid)):
            first, last = first & (ids[k] == 0), last & (ids[k] == grid[k] - 1)
        pl.when(first)(start)
        body(*refs[:a], *refs[b:c], *refs[d:e])
        pl.when(last)(finish)

    if ride is None:
        kernel_body, extra_in, extra_out, extra_shape, extra_scr, aliases = body, [], [], [], [], {}
        params = _params(*semantics)
    else:
        kernel_body, extra_in, extra_out = riding, [HBM_SPEC] * r_in, [HBM_SPEC] * r_out
        extra_shape, extra_scr = list(ride.out_shape), list(ride.sems)
        aliases = {pre + n_in + i: n_out + o for i, o in ride.aliases.items()}
        params = _params(*(("arbitrary",) * len(grid)))
    specs = dict(grid=grid, in_specs=list(in_specs) + extra_in, out_specs=list(out_specs) + extra_out,
                 scratch_shapes=list(scratch_shapes) + extra_scr)
    if prefetch is not None:
        specs = dict(grid_spec=pltpu.PrefetchScalarGridSpec(num_scalar_prefetch=1, **specs))
        args = (prefetch,) + tuple(args)
    outs = pl.pallas_call(kernel_body, name=name, out_shape=list(out_shape) + extra_shape,
                          input_output_aliases=aliases, compiler_params=params, **specs,
                          )(*args, *(ride.operands if ride is not None else ()))
    return outs if ride is None else (outs[:n_out], outs[n_out:])


def alone(ride, *, name):
    def body(*refs):
        n = len(ride.operands)
        start, finish = ride.make(refs[:n], refs[n:n + len(ride.out_shape)], refs[n + len(ride.out_shape):])
        start()
        finish()

    return pl.pallas_call(
        body, name=name, in_specs=[HBM_SPEC] * len(ride.operands), out_specs=[HBM_SPEC] * len(ride.out_shape),
        out_shape=list(ride.out_shape), input_output_aliases=dict(ride.aliases), scratch_shapes=list(ride.sems),
    )(*ride.operands)


def gather_ride(wholes, metas, small=None):
    n = len(wholes)
    operands, out_shape = list(wholes), [SDS(s.shape, s.dtype) for s in wholes]
    sems = [DMA((n, 3)), DMA((n, 3)), DMA((n, 3)), DMA((n, 3))]
    if small is not None:
        operands.append(small)
        out_shape.append(SDS((N_CHIPS,) + small.shape, small.dtype))
        sems += [DMA((3,)), DMA((3,)), DMA(())]

    def make(ins, outs, sem):
        send1, recv1, send2, recv2 = sem[:4]
        x, y, c = _place()
        p = 2 * x + y
        chips = _other_chips(x, y)
        me, sibling = (x, y, c), (x, y, 1 - c)
        part = lambda t, q, half: _half_of_quarter(outs[t], *metas[t], q, half)
        first = []
        for j, (qx, qy) in enumerate(chips):
            if small is not None:
                first.append(_remote(ins[n], outs[n].at[p], sem[4].at[j], sem[5].at[j], (qx, qy, c)))
            for t in range(n):
                first.append(_remote(part(t, p, c), part(t, p, c), send1.at[t, j], recv1.at[t, j], (qx, qy, c)))
        local = [] if small is None else [pltpu.make_async_copy(ins[n], outs[n].at[p], sem[6])]

        def start():
            for cp in local + first:
                cp.start()

        def finish():
            passed = []
            for j, (qx, qy) in enumerate(chips):
                q = 2 * qx + qy
                for t in range(n):
                    landed = part(t, q, c)
                    _remote(landed, landed, send1.at[t, j], recv1.at[t, j], me).wait_recv()
                    cp = _remote(landed, landed, send2.at[t, j], recv2.at[t, j], sibling)
                    cp.start()
                    passed.append(cp)
            for j, (qx, qy) in enumerate(chips):
                q = 2 * qx + qy
                if small is not None:
                    _remote(outs[n].at[q], outs[n].at[q], sem[4].at[j], sem[5].at[j], me).wait_recv()
                for t in range(n):
                    theirs = part(t, q, 1 - c)
                    _remote(theirs, theirs, send2.at[t, j], recv2.at[t, j], me).wait_recv()
            for cp in first + passed:
                cp.wait_send()
            for cp in local:
                cp.wait()

        return start, finish

    return Ride(operands, out_shape, {t: t for t in range(n)}, sems, make)


def chip_ride(sums, metas, small=None, earlier=None):
    n = len(sums)
    operands = list(sums)
    out_shape = [SDS((3, s.shape[1], quarter[1]), s.dtype) for s, (_, quarter, _) in zip(sums, metas)]
    sems = [DMA((n, 3)), DMA((n, 3))] if n else []
    if small is not None:
        operands.append(small)
        out_shape.append(SDS((8,) + small.shape, small.dtype))
        sems += [DMA((7,)), DMA((7,)), DMA(())]
    aliases = {}
    for t, buffer in enumerate(earlier or [None] * n):
        if buffer is not None:
            aliases[len(operands)] = t
            operands.append(buffer)

    def make(ins, outs, sem):
        x, y, c = _place()
        cps = []
        for j, (qx, qy) in enumerate(_other_chips(x, y)):
            q = 2 * qx + qy
            for t in range(n):
                kind, (_, ws), part = metas[t]
                rows = pl.ds(*_rows_of(ins[t].shape[1], part))
                if kind == "row":
                    src = ins[t].at[q, rows]
                elif kind == "col":
                    src = ins[t].at[0, rows, pl.ds(pl.multiple_of(q * ws, 128), ws)]
                else:
                    src = ins[t].at[q // 2, rows, pl.ds(pl.multiple_of((q % 2) * ws, 128), ws)]
                cps.append(_remote(src, outs[t].at[j, rows], sem[0].at[t, j], sem[1].at[t, j], (qx, qy, c)))
        local = []
        if small is not None:
            ssend, srecv, lsem = sem[2 * bool(n):2 * bool(n) + 3]
            local.append(pltpu.make_async_copy(ins[n], outs[n].at[0], lsem))
            for k in range(1, 8):
                peer = (x ^ (k >> 2 & 1), y ^ (k >> 1 & 1), c ^ (k & 1))
                cps.append(_remote(ins[n], outs[n].at[k], ssend.at[k - 1], srecv.at[k - 1], peer))

        def start():
            for cp in local + cps:
                cp.start()

        def finish():
            for cp in cps + local:
                cp.wait()

        return start, finish

    return Ride(operands, out_shape, aliases, sems, make)


def pair_ride(grads):
    n = len(grads)

    def make(ins, outs, sem):
        x, y, c = _place()
        cps = [_remote(ins[t].at[:, 1 - c], outs[t], sem[0].at[t], sem[1].at[t], (x, y, 1 - c)) for t in range(n)]

        def start():
            for cp in cps:
                cp.start()

        def finish():
            for cp in cps:
                cp.wait()

        return start, finish

    return Ride(list(grads), [SDS((g.shape[0],) + g.shape[2:], g.dtype) for g in grads], {}, [DMA((n,)), DMA((n,))],
                make)


def half_ride(quarters):
    n = len(quarters)

    def make(ins, outs, sem):
        x, y, c = _place()
        sends = [_remote(outs[t].at[c], outs[t].at[c], sem[0].at[t], sem[1].at[t], (x, y, 1 - c)) for t in range(n)]

        def start():
            for cp in sends:
                cp.start()

        def finish():
            for t in range(n):
                theirs = outs[t].at[1 - c]
                _remote(theirs, theirs, sem[0].at[t], sem[1].at[t], (x, y, c)).wait_recv()
            for cp in sends:
                cp.wait_send()

        return start, finish

    return Ride(list(quarters), [SDS(q.shape, q.dtype) for q in quarters], {t: t for t in range(n)},
                [DMA((n,)), DMA((n,))], make)


CAST_STEPS = 4


def cast_quarters(sources, p_arr, *, name, ride=None):
    n = len(sources)
    in_specs, out_specs, out_shape = [], [], []
    for w, layer, kind in sources:
        _, r, ws = w.shape
        tr = r // CAST_STEPS
        assert tr % BF16_ROWS == 0, w.shape
        in_specs.append(pl.BlockSpec((None, tr, ws), lambda i, p_ref, layer=layer: (layer, i, 0)))
        out_specs.append(pl.BlockSpec((tr, ws), (lambda i, p_ref: (p_ref[0] * CAST_STEPS + i, 0)) if kind == "row"
                                      else (lambda i, p_ref: (i, p_ref[0]))))
        out_shape.append(SDS(_full_shape(kind, (r, ws)), BF16))

    def body(p_ref, *refs):
        for w_ref, o_ref in zip(refs[:n], refs[n:]):
            o_ref[...] = w_ref[...].astype(BF16)

    return _call(body, name=name, grid=(CAST_STEPS,), in_specs=in_specs, out_specs=out_specs, out_shape=out_shape,
                 semantics=("parallel",), args=[w for w, _, _ in sources], ride=ride, prefetch=p_arr)


def pair_add(own, got, c_arr, *, name):
    A, _, h, W = own.shape
    th = _row_tile(h, max(BF16_ROWS, (3 << 19) // W), BF16_ROWS)

    def body(c_ref, a_ref, b_ref, o_ref):
        o_ref[...] = (a_ref[...].astype(F32) + b_ref[...].astype(F32)).astype(BF16)

    return pl.pallas_call(
        body, name=name,
        grid_spec=pltpu.PrefetchScalarGridSpec(
            num_scalar_prefetch=1, grid=(A, h // th),
            in_specs=[pl.BlockSpec((None, None, th, W), lambda q, i, c_ref: (q, c_ref[0], i, 0)),
                      pl.BlockSpec((None, th, W), lambda q, i, c_ref: (q, i, 0))],
            out_specs=pl.BlockSpec((None, th, W), lambda q, i, c_ref: (q, i, 0))),
        out_shape=SDS((A, h, W), BF16),
        compiler_params=_params("parallel", "parallel"),
    )(c_arr, own, got)


REDUCE_STEPS = 2


def chip_reduce(sums, got, kinds, pc_arr, *, name):
    n = len(sums)
    mine = {"row": lambda i, pc_ref: (pc_ref[0], i, 0), "col": lambda i, pc_ref: (0, i, pc_ref[0]),
            "split": lambda i, pc_ref: (pc_ref[0] // 2, i, pc_ref[0] % 2)}
    a_specs, b_specs, o_specs, out_shape = [], [], [], []
    for g, kind in zip(got, kinds):
        _, h, ws = g.shape
        th = h // REDUCE_STEPS
        assert th % BF16_ROWS == 0, g.shape
        a_specs.append(pl.BlockSpec((None, th, ws), mine[kind]))
        b_specs.append(pl.BlockSpec((3, th, ws), lambda i, pc_ref: (0, i, 0)))
        o_specs.append(pl.BlockSpec((None, th, ws), lambda i, pc_ref: (pc_ref[1], i, 0)))
        out_shape.append(SDS((2, h, ws), F32))

    def body(pc_ref, *refs):
        for a_ref, b_ref, o_ref in zip(refs[:n], refs[n:2 * n], refs[2 * n:]):
            o_ref[...] = ((a_ref[...].astype(F32) + b_ref[0].astype(F32)) + b_ref[1].astype(F32)) + b_ref[2].astype(F32)

    return _call(body, name=name, grid=(REDUCE_STEPS,), in_specs=a_specs + b_specs, out_specs=o_specs,
                 out_shape=out_shape, semantics=("parallel",), args=list(sums) + list(got), prefetch=pc_arr)


def small_reduce(blocks, me_arr):
    _, rows, D = blocks.shape

    def body(me_ref, b_ref, o_ref):
        me = me_ref[0]
        total = b_ref[me]
        for d in range(1, 8):
            total = total + b_ref[d ^ me]
        o_ref[...] = total

    return pl.pallas_call(
        body, name="small_reduce",
        grid_spec=pltpu.PrefetchScalarGridSpec(
            num_scalar_prefetch=1, grid=(1,),
            in_specs=[pl.BlockSpec((8, rows, D), lambda i, me_ref: (0, 0, 0))],
            out_specs=pl.BlockSpec((rows, D), lambda i, me_ref: (0, 0))),
        out_shape=SDS((rows, D), F32),
        compiler_params=_params("arbitrary"),
    )(me_arr, blocks)


def adamw(w, gs, m, v, *, name):
    L, r, cols = w.shape
    tr = _row_tile(r, 256)
    nt = r // tr

    def body(*refs):
        w_ref, m_ref, v_ref = refs[:3]
        g_refs = refs[3:3 + L]
        g_out, d_out, m_out, v_out = refs[3 + L:]
        layer = pl.program_id(0)
        g = g_refs[0][...]
        for l in range(1, L):
            g = jnp.where(layer == l, g_refs[l][...], g)
        m_new = ADAM_B1 * m_ref[...] + (1.0 - ADAM_B1) * g
        v_new = ADAM_B2 * v_ref[...] + (1.0 - ADAM_B2) * (g * g)
        m_hat = m_new / (1.0 - ADAM_B1 ** ADAM_STEP)
        v_hat = v_new / (1.0 - ADAM_B2 ** ADAM_STEP)
        g_out[...] = g
        m_out[...] = m_new
        v_out[...] = v_new
        d_out[...] = -ADAM_LR * (m_hat / (jnp.sqrt(v_hat) + ADAM_EPS) + ADAM_WD * w_ref[...])

    full = pl.BlockSpec((None, tr, cols), lambda l, i: (l, i, 0))
    g_spec = lambda l0: pl.BlockSpec((tr, cols), lambda l, i: (jnp.where(l == l0, i, jnp.where(l < l0, 0, nt - 1)), 0))
    return pl.pallas_call(
        body, name=name, grid=(L, nt),
        in_specs=[full, full, full] + [g_spec(l0) for l0 in range(L)],
        out_specs=[full] * 4,
        out_shape=[SDS(w.shape, F32)] * 4,
        compiler_params=_params("arbitrary", "arbitrary"),
    )(w, m, v, *gs)


def _rms_r(xf):
    return lax.rsqrt(jnp.mean(xf * xf, axis=-1, keepdims=True) + EPS)


def _rmsnorm_bwd(xf, g, dy):
    r = _rms_r(xf)
    xh = xf * r
    gd = g * dy
    return r * (gd - xh * jnp.mean(xh * gd, axis=-1, keepdims=True)), xh


def _dot(a, b):
    return jnp.dot(a, b, preferred_element_type=F32)


def _dot_nt(a, b):
    return lax.dot_general(a, b, (((1,), (1,)), ((), ())), preferred_element_type=F32)


def _dot_tn(a, b):
    return lax.dot_general(a, b, (((0,), (0,)), ((), ())), preferred_element_type=F32)


def _accumulate(ref, first, value):
    @pl.when(first)
    def _():
        ref[...] = value

    @pl.when(jnp.logical_not(first))
    def _():
        ref[...] += value


def norm_matmul(x, g, w, *, tn, split, name, ride=None, tm=ROW_TILE):
    T, D = x.shape
    N = w.shape[1]
    per = N // split // tn

    def body(x_ref, g_ref, w_ref, o_ref, xn_ref):
        @pl.when(pl.program_id(1) == 0)
        def _():
            xf = x_ref[...].astype(F32)
            xn_ref[...] = (xf * _rms_r(xf) * g_ref[...]).astype(BF16)

        o_ref[...] = _dot(xn_ref[...], w_ref[...]).astype(BF16)

    return _call(
        body, name=name, grid=(T // tm, N // tn),
        in_specs=[pl.BlockSpec((tm, D), lambda i, j: (i, 0)),
                  pl.BlockSpec((1, D), lambda i, j: (0, 0)),
                  pl.BlockSpec((D, tn), lambda i, j: (0, j))],
        out_specs=[pl.BlockSpec((None, tm, tn), lambda i, j: (j // per, i, j % per)),
                   pl.BlockSpec((tm, D), lambda i, j: (i, 0))],
        out_shape=[SDS((split, T, N // split), BF16), SDS((T, D), BF16)],
        semantics=("parallel", "arbitrary"), args=(x, g, w), ride=ride)


BIG_ROW_TILE = 1024


def norm2_matmul(x, gains, weights, *, name, tm=BIG_ROW_TILE):
    T, D = x.shape
    tm = min(tm, T)
    n = len(gains)

    def body(x_ref, *refs):
        xf = x_ref[...].astype(F32)
        xh = xf * _rms_r(xf)
        for g_ref, w_ref, o_ref, xn_ref in zip(refs[:n], refs[n:2 * n], refs[2 * n::2], refs[2 * n + 1::2]):
            xn = (xh * g_ref[...]).astype(BF16)
            xn_ref[...] = xn
            o_ref[...] = _dot(xn, w_ref[...]).astype(BF16)

    row = pl.BlockSpec((tm, D), lambda i: (i, 0))
    vec = pl.BlockSpec((1, D), lambda i: (0, 0))
    out_specs, out_shape = [], []
    for w in weights:
        out_specs += [pl.BlockSpec((tm, w.shape[1]), lambda i: (i, 0)), row]
        out_shape += [SDS((T, w.shape[1]), BF16), SDS((T, D), BF16)]
    return _call(
        body, name=name, grid=(T // tm,),
        in_specs=[row] + [vec] * n + [pl.BlockSpec(w.shape, lambda i: (0, 0)) for w in weights],
        out_specs=out_specs, out_shape=out_shape, semantics=("parallel",), args=[x] + list(gains) + list(weights))


def _shift_down(prev, cur, by):
    big = jnp.concatenate([prev, cur], axis=0)
    return pltpu.roll(big, by, 0)[prev.shape[0]:]


def _shift_up(cur, nxt, by):
    big = jnp.concatenate([cur, nxt], axis=0)
    return pltpu.roll(big, big.shape[0] - by, 0)[:cur.shape[0]]


def conv_mix_out(bcx, conv_w, w_out, g_post, res, *, name, ride=None, tm=ROW_TILE):
    T, D = res.shape
    hb = tm // BF16_ROWS

    def body(b_ref, c_ref, u_ref, cp_ref, up_ref, cw_ref, w_ref, g_ref, r_ref, h_ref, z_ref, y_ref):
        i = pl.program_id(0)
        cu = c_ref[...].astype(F32) * u_ref[...].astype(F32)
        cup = cp_ref[...].astype(F32) * up_ref[...].astype(F32)
        cup = jnp.where(i == 0, 0.0, cup)
        cv = (cw_ref[0:1, :] * _shift_down(cup, cu, 2) + cw_ref[1:2, :] * _shift_down(cup, cu, 1)
              + cw_ref[2:3, :] * cu)
        y = (b_ref[...].astype(F32) * cv).astype(BF16)
        y_ref[...] = y
        z = _dot(y, w_ref[...])
        z_ref[...] = z.astype(BF16)
        h_ref[...] = (r_ref[...] + z * _rms_r(z) * g_ref[...]).astype(STREAM)

    tile = lambda col: pl.BlockSpec((tm, D), lambda i: (i, col))
    halo = lambda col: pl.BlockSpec((BF16_ROWS, D), lambda i: (jnp.maximum(i * hb - 1, 0), col))
    row = pl.BlockSpec((tm, D), lambda i: (i, 0))
    return _call(
        body, name=name, grid=(T // tm,),
        in_specs=[tile(0), tile(1), tile(2), halo(1), halo(2),
                  pl.BlockSpec((3, D), lambda i: (0, 0)),
                  pl.BlockSpec((D, D), lambda i: (0, 0)),
                  pl.BlockSpec((1, D), lambda i: (0, 0)), row],
        out_specs=[row, row, row],
        out_shape=[SDS((T, D), STREAM), SDS((T, D), BF16), SDS((T, D), BF16)],
        semantics=("parallel",), args=(bcx, bcx, bcx, bcx, bcx, conv_w, w_out, g_post, res), ride=ride)


def _normbwd_then_nt(dh, zf, g_ref, w_ref, dz_ref, dg_ref, o_ref, first):
    dz, zh = _rmsnorm_bwd(zf, g_ref[...], dh)
    dz = dz.astype(BF16)
    dz_ref[...] = dz
    _accumulate(dg_ref, first, jnp.sum(dh * zh, axis=0, keepdims=True))
    o_ref[...] = _dot_nt(dz, w_ref[...]).astype(BF16)


def _then_specs(then, tm, T, D):
    z, g, w = then
    K = w.shape[0]
    row = pl.BlockSpec((tm, D), lambda i: (i, 0))
    vec = pl.BlockSpec((1, D), lambda i: (0, 0))
    in_specs = [row, vec, pl.BlockSpec((K, D), lambda i: (0, 0), pipeline_mode=pl.Buffered(1))]
    out_specs = [row, vec, pl.BlockSpec((tm, K), lambda i: (i, 0))]
    out_shape = [SDS((T, D), BF16), SDS((1, D), F32), SDS((T, K), BF16)]
    return in_specs, out_specs, out_shape


def plain_mix_out(a, w, g_post, res, *, name, target=None, ride=None, tm=ROW_TILE):
    T, D = res.shape
    tm = min(tm, T)
    K = a.shape[1]
    with_loss = target is not None

    def body(a_ref, w_ref, g_ref, r_ref, *rest):
        if not with_loss:
            h_ref, z_ref = rest
            z = _dot(a_ref[...], w_ref[...])
            h_ref[...] = (r_ref[...].astype(F32) + z * _rms_r(z) * g_ref[...]).astype(STREAM)
            z_ref[...] = z.astype(BF16)
            return
        t_ref, h_ref, dz_ref, dg_ref, da_ref, loss_ref = rest
        first = pl.program_id(0) == 0
        subs = [pl.ds(k, MXU_WIDTH) for k in range(0, tm, MXU_WIDTH)]
        zs = [_dot(a_ref[rows, :], w_ref[...]) for rows in subs]
        loss, dg = jnp.zeros((), F32), jnp.zeros((1, D), F32)
        for rows, z in zip(subs, zs):
            diff = r_ref[rows, :].astype(F32) + z * _rms_r(z) * g_ref[...] - t_ref[rows, :]
            dh = diff * (1.0 / D)
            h_ref[rows, :] = dh.astype(STREAM)
            loss = loss + jnp.sum(diff * diff)
            dz, zh = _rmsnorm_bwd(z, g_ref[...], dh)
            dz = dz.astype(BF16)
            dz_ref[rows, :] = dz
            dg = dg + jnp.sum(dh * zh, axis=0, keepdims=True)
            da_ref[rows, :] = _dot_nt(dz, w_ref[...]).astype(BF16)
        _accumulate(loss_ref, first, jnp.full(loss_ref.shape, 0.5 / D, F32) * loss)
        _accumulate(dg_ref, first, dg)

    row = pl.BlockSpec((tm, D), lambda i: (i, 0))
    vec = pl.BlockSpec((1, D), lambda i: (0, 0))
    in_specs = [pl.BlockSpec((tm, K), lambda i: (i, 0)), pl.BlockSpec((K, D), lambda i: (0, 0)), vec, row]
    if with_loss:
        in_specs.append(row)
        out_specs = [row, row, vec, pl.BlockSpec((tm, K), lambda i: (i, 0)), pl.BlockSpec((8, 128), lambda i: (0, 0))]
        out_shape = [SDS((T, D), STREAM), SDS((T, D), BF16), SDS((1, D), F32), SDS((T, K), BF16), SDS((8, 128), F32)]
    else:
        out_specs, out_shape = [row, row], [SDS((T, D), STREAM), SDS((T, D), BF16)]
    return _call(
        body, name=name, grid=(T // tm,), in_specs=in_specs, out_specs=out_specs, out_shape=out_shape,
        semantics=("arbitrary",), args=(a, w, g_post, res) + ((target,) if with_loss else ()), ride=ride)


def _silu_grads(d, g, u):
    sg = jax.nn.sigmoid(g)
    return d * u * (sg * (1.0 + g * (1.0 - sg))), d * (g * sg)


def norm_swiglu_in(x, g, w, *, name, ride=None, tm=ROW_TILE // 2):
    T, D = x.shape
    F = w.shape[1] // 2

    def body(x_ref, g_ref, wg_ref, wu_ref, gu_ref, a_ref, xt_ref):
        xf = x_ref[...].astype(F32)
        xn = xf * _rms_r(xf) * g_ref[...]
        xt_ref[...] = xn.T.astype(BF16)
        xb = xn.astype(BF16)
        gate = _dot(xb, wg_ref[...]).astype(BF16)
        up = _dot(xb, wu_ref[...]).astype(BF16)
        gu_ref[0] = gate
        gu_ref[1] = up
        a_ref[...] = gate * jax.nn.sigmoid(gate) * up

    half = lambda s: pl.BlockSpec((D, F), lambda i: (0, s), pipeline_mode=pl.Buffered(1))
    return _call(
        body, name=name, grid=(T // tm,),
        in_specs=[pl.BlockSpec((tm, D), lambda i: (i, 0)), pl.BlockSpec((1, D), lambda i: (0, 0)), half(0), half(1)],
        out_specs=[pl.BlockSpec((2, tm, F), lambda i: (0, i, 0)), pl.BlockSpec((tm, F), lambda i: (i, 0)),
                   pl.BlockSpec((D, tm), lambda i: (0, i))],
        out_shape=[SDS((2, T, F), BF16), SDS((T, F), BF16), SDS((D, T), BF16)],
        semantics=("parallel",), args=(x, g, w, w), ride=ride)


def swiglu_bwd_tn(xt, dact, gu, *, name, ride=None, tb=MXU_WIDTH):
    D, T = xt.shape
    F = dact.shape[1]

    def body(xt_ref, d_ref, g_ref, u_ref, o_ref):
        dg, du = _silu_grads(d_ref[...], g_ref[...], u_ref[...])
        o_ref[0] = _dot(xt_ref[...], dg).astype(BF16)
        o_ref[1] = _dot(xt_ref[...], du).astype(BF16)

    col = lambda s: pl.BlockSpec((None, T, tb), lambda j: (s, 0, j))
    out = _call(
        body, name=name, grid=(F // tb,),
        in_specs=[pl.BlockSpec((D, T), lambda j: (0, 0), pipeline_mode=pl.Buffered(1)),
                  pl.BlockSpec((T, tb), lambda j: (0, j)), col(0), col(1)],
        out_specs=[pl.BlockSpec((2, D, tb), lambda j: (0, 0, j))],
        out_shape=[SDS((2, D, F), BF16)],
        semantics=("parallel",), args=(xt, dact, gu, gu), ride=ride)
    return out[0] if ride is None else (out[0][0], out[1])


def swiglu_bwd_in(dact, gu, w, h_in, g, dh_out, then, *, name, ride=None, tm=ROW_TILE):
    T, D = h_in.shape
    F = dact.shape[1]

    def body(d_ref, gg_ref, uu_ref, wg_ref, wu_ref, h_ref, g_ref, dh_ref, z_ref, g2_ref, w2_ref,
             o_ref, dg_ref, dz_ref, dg2_ref, da_ref):
        first = pl.program_id(0) == 0
        subs = [pl.ds(k, MXU_WIDTH) for k in range(0, tm, MXU_WIDTH)]
        dns = []
        for rows in subs:
            dgate, dup = _silu_grads(d_ref[rows, :], gg_ref[rows, :], uu_ref[rows, :])
            dns.append(_dot_nt(dgate, wg_ref[...]) + _dot_nt(dup, wu_ref[...]))
        dg, dg2 = jnp.zeros((1, D), F32), jnp.zeros((1, D), F32)
        for rows, dn in zip(subs, dns):
            dx, hh = _rmsnorm_bwd(h_ref[rows, :].astype(F32), g_ref[...], dn)
            dh_in = dh_ref[rows, :] + dx
            o_ref[rows, :] = dh_in.astype(STREAM)
            dg = dg + jnp.sum(dn * hh, axis=0, keepdims=True)
            dz, zh = _rmsnorm_bwd(z_ref[rows, :].astype(F32), g2_ref[...], dh_in)
            dz = dz.astype(BF16)
            dz_ref[rows, :] = dz
            dg2 = dg2 + jnp.sum(dh_in * zh, axis=0, keepdims=True)
            da_ref[rows, :] = _dot_nt(dz, w2_ref[...]).astype(BF16)
        _accumulate(dg_ref, first, dg)
        _accumulate(dg2_ref, first, dg2)

    row = pl.BlockSpec((tm, D), lambda i: (i, 0))
    vec = pl.BlockSpec((1, D), lambda i: (0, 0))
    part = lambda s: pl.BlockSpec((None, tm, F), lambda i: (s, i, 0))
    half = lambda s: pl.BlockSpec((D, F), lambda i: (0, s), pipeline_mode=pl.Buffered(1))
    then_in, then_out, then_shape = _then_specs(then, tm, T, D)
    return _call(
        body, name=name, grid=(T // tm,),
        in_specs=[pl.BlockSpec((tm, F), lambda i: (i, 0)), part(0), part(1), half(0), half(1), row, vec, row] + then_in,
        out_specs=[row, vec] + then_out,
        out_shape=[SDS((T, D), STREAM), SDS((1, D), F32)] + then_shape,
        semantics=("arbitrary",), args=(dact, gu, gu, w, w, h_in, g, dh_out) + tuple(then), ride=ride)


def rope_tables(T):
    half = ROT_DIM // 2
    inv_freq = ROPE_THETA ** (-jnp.arange(0, ROT_DIM, 2, dtype=F32) / ROT_DIM)
    ang = (jnp.arange(T, dtype=F32)[:, None] * inv_freq[None, :]).T
    cos, sin = jnp.cos(ang), jnp.sin(ang)
    rest = HEAD_DIM - ROT_DIM
    one, zero = jnp.ones((rest, T), F32), jnp.zeros((rest, T), F32)
    zh = jnp.zeros((half, T), F32)
    fac = jnp.concatenate([cos, cos, one], axis=0)
    up = jnp.concatenate([-sin, zh, zero], axis=0)
    down = jnp.concatenate([zh, sin, zero], axis=0)
    return jnp.stack([fac, up, down])


def _rope(t, tab):
    half = ROT_DIM // 2
    return t * tab[0] + pltpu.roll(t, HEAD_DIM - half, 0) * tab[1] + pltpu.roll(t, half, 0) * tab[2]


def _rope_t(d, tab):
    half = ROT_DIM // 2
    return d * tab[0] + pltpu.roll(d * tab[1], half, 0) + pltpu.roll(d * tab[2], HEAD_DIM - half, 0)


def _head(t, h):
    return t[h * HEAD_DIM:(h + 1) * HEAD_DIM]


def _band(n, group):
    kj = lax.broadcasted_iota(jnp.int32, (2 * BLOCK, BLOCK), 0)
    qi = lax.broadcasted_iota(jnp.int32, (2 * BLOCK, BLOCK), 1)
    mask = (kj > qi) & (kj <= qi + BLOCK) & ((n > 0) | (kj >= BLOCK))
    return jnp.tile(mask, (1, group))


def _attn_specs(D, kvd, nb):
    cur = lambda n: jnp.minimum(n, nb - 1)
    prev = lambda n: jnp.maximum(cur(n) - 1, 0)
    return [pl.BlockSpec((BLOCK, D), lambda n: (cur(n), 0)),
            pl.BlockSpec((BLOCK, kvd), lambda n: (prev(n), 0)),
            pl.BlockSpec((BLOCK, kvd), lambda n: (cur(n), 0)),
            pl.BlockSpec((BLOCK, kvd), lambda n: (prev(n), 1)),
            pl.BlockSpec((BLOCK, kvd), lambda n: (cur(n), 1)),
            pl.BlockSpec((3, HEAD_DIM, BLOCK), lambda n: (0, 0, prev(n))),
            pl.BlockSpec((3, HEAD_DIM, BLOCK), lambda n: (0, 0, cur(n))),
            pl.BlockSpec(memory_space=pltpu.SMEM)]


def _attn_operands(q_ref, kp_ref, k_ref, vp_ref, v_ref, tp_ref, t_ref):
    flip = lambda ref: ref[...].astype(F32).T
    tab = t_ref[...]
    kt = jnp.concatenate([flip(kp_ref), flip(k_ref)], axis=1)
    vt = jnp.concatenate([flip(vp_ref), flip(v_ref)], axis=1)
    return flip(q_ref), kt, vt, tab, jnp.concatenate([tp_ref[...], tab], axis=2)


SCORE_SCALE = 1.0 / math.sqrt(HEAD_DIM)
HEADS_TOGETHER = 4


def _group_heads(t, first, count, tab=None):
    heads = [_head(t, first + g) for g in range(count)]
    if tab is not None:
        heads = [_rope(h, tab) * SCORE_SCALE for h in heads]
    return jnp.concatenate(heads, axis=1).astype(BF16)


def _sink_row(s_ref, first, count):
    which = lax.broadcasted_iota(jnp.int32, (1, count * BLOCK), 1) // BLOCK
    row = jnp.zeros((1, count * BLOCK), F32)
    for g in range(count):
        row = jnp.where(which == g, s_ref[0, first + g], row)
    return row


def _sum_keys(t):
    return _dot(jnp.ones((8, t.shape[0]), BF16), t)[0:1]


def _softmax(scores, sink, mask):
    s = jnp.where(mask, scores.astype(BF16), NEG)
    m = jnp.maximum(jnp.max(s, axis=0, keepdims=True).astype(F32), sink).astype(BF16)
    e = jnp.exp(s - m)
    m = m.astype(F32)
    return e, m, 1.0 / (_sum_keys(e) + jnp.exp(sink - m))


def _per_head(row, count):
    return [row[:, g * BLOCK:(g + 1) * BLOCK] for g in range(count)]


def attention_fwd(q, kv, tabs, sinks, *, name, ride=None):
    T, D = q.shape
    kvd = kv.shape[1] // 2
    heads = D // HEAD_DIM
    group = heads // N_KV_HEADS

    def body(q_ref, kp_ref, k_ref, vp_ref, v_ref, tp_ref, t_ref, s_ref, o_ref, stat_ref):
        gs = HEADS_TOGETHER
        mask = _band(pl.program_id(0), gs)
        qt, kt, vt, tab, tab2 = _attn_operands(q_ref, kp_ref, k_ref, vp_ref, v_ref, tp_ref, t_ref)
        firsts = [(j, first) for j in range(N_KV_HEADS) for first in range(j * group, (j + 1) * group, gs)]
        ks = [_rope(_head(kt, j), tab2).astype(BF16) for j in range(N_KV_HEADS)]
        scores = [_dot_tn(ks[j], _group_heads(qt, first, gs, tab)) for j, first in firsts]
        soft = [_softmax(s, _sink_row(s_ref, first, gs), mask) for s, (j, first) in zip(scores, firsts)]
        outs, ms, invs = [], [], []
        for (e, m, inv), (j, first) in zip(soft, firsts):
            o = _dot(_head(vt, j).astype(BF16), e) * inv
            outs += [o[:, g * BLOCK:(g + 1) * BLOCK] for g in range(gs)]
            ms += _per_head(m, gs)
            invs += _per_head(inv, gs)
        o_ref[...] = jnp.concatenate(outs, axis=0).T.astype(BF16)
        stat_ref[0] = jnp.concatenate(ms, axis=0)
        stat_ref[1] = jnp.concatenate(invs, axis=0)

    return _call(
        body, name=name, grid=(T // BLOCK,),
        in_specs=_attn_specs(D, kvd, T // BLOCK),
        out_specs=[pl.BlockSpec((BLOCK, D), lambda n: (n, 0)), pl.BlockSpec((2, heads, BLOCK), lambda n: (0, 0, n))],
        out_shape=[SDS((T, D), BF16), SDS((2, heads, T), F32)],
        semantics=("parallel",), args=(q, kv, kv, kv, kv, tabs, tabs, sinks), ride=ride)


def attention_bwd(q, kv, tabs, sinks, do, o, stats, *, name, ride=None):
    T, D = q.shape
    kvd = kv.shape[1] // 2
    heads = D // HEAD_DIM
    group = heads // N_KV_HEADS
    nb = T // BLOCK

    def body(q_ref, kp_ref, k_ref, vp_ref, v_ref, tp_ref, t_ref, s_ref, do_ref, o_ref, stat_ref,
             dq_ref, dkv_ref, ds_ref, carry):
        n = pl.program_id(0)

        @pl.when(n == 0)
        def _():
            carry[...] = jnp.zeros_like(carry)

        @pl.when(n < nb)
        def _():
            block(n, q_ref, kp_ref, k_ref, vp_ref, v_ref, tp_ref, t_ref, s_ref, do_ref, o_ref, stat_ref,
                  dq_ref, dkv_ref, ds_ref, carry)

        @pl.when(n == nb)
        def _():
            dkv_ref[...] = carry[...].astype(BF16)

    def block(n, q_ref, kp_ref, k_ref, vp_ref, v_ref, tp_ref, t_ref, s_ref, do_ref, o_ref, stat_ref,
              dq_ref, dkv_ref, ds_ref, carry):
        gs = HEADS_TOGETHER
        mask = _band(n, gs)
        qt, kt, vt, tab, tab2 = _attn_operands(q_ref, kp_ref, k_ref, vp_ref, v_ref, tp_ref, t_ref)
        dot = do_ref[...].astype(F32).T
        odo = o_ref[...].astype(F32).T * dot
        dl_all = jnp.concatenate([jnp.sum(_head(odo, h), axis=0, keepdims=True) for h in range(heads)], axis=0)
        m_all, inv_all = stat_ref[0], stat_ref[1]
        row = lambda t, first: jnp.concatenate([t[first + g:first + g + 1] for g in range(gs)], axis=1)
        lane = lax.broadcasted_iota(jnp.int32, (8, 128), 1)
        dsink = jnp.zeros((8, 128), F32)
        firsts = [(j, first) for j in range(N_KV_HEADS) for first in range(j * group, (j + 1) * group, gs)]
        ks = [_rope(_head(kt, j), tab2).astype(BF16) for j in range(N_KV_HEADS)]
        vs = [_head(vt, j).astype(BF16) for j in range(N_KV_HEADS)]
        qs = [_group_heads(qt, first, gs, tab) for _, first in firsts]
        dos = [_group_heads(dot, first, gs) for _, first in firsts]
        scores = [_dot_tn(ks[j], q) for q, (j, _) in zip(qs, firsts)]
        dps = [_dot_tn(vs[j], do) for do, (j, _) in zip(dos, firsts)]
        ps, dscs = [], []
        for s, dp, (j, first) in zip(scores, dps, firsts):
            m, inv, dl = row(m_all, first), row(inv_all, first), row(dl_all, first)
            e = jnp.exp(jnp.where(mask, s.astype(BF16), NEG) - m.astype(BF16))
            p = e * inv.astype(BF16)
            dscs.append(p * (dp.astype(BF16) - dl.astype(BF16)))
            ps.append(p)
            weight = jnp.exp(_sink_row(s_ref, first, gs) - m) * inv * dl
            for g in range(gs):
                dsink = dsink - jnp.where(lane == first + g, jnp.sum(weight[:, g * BLOCK:(g + 1) * BLOCK]), 0.0)
        dqs = []
        dks = [jnp.zeros((HEAD_DIM, 2 * BLOCK), F32) for _ in range(N_KV_HEADS)]
        dvs = [jnp.zeros((HEAD_DIM, 2 * BLOCK), F32) for _ in range(N_KV_HEADS)]
        for p, dsc, q, do, (j, _) in zip(ps, dscs, qs, dos, firsts):
            dq = _dot(ks[j], dsc) * SCORE_SCALE
            dqs += [_rope_t(dq[:, g * BLOCK:(g + 1) * BLOCK], tab) for g in range(gs)]
            dks[j] = dks[j] + _dot_nt(q, dsc)
            dvs[j] = dvs[j] + _dot_nt(do, p)
        dks = [_rope_t(dk, tab2) for dk in dks]
        dq_ref[...] = jnp.concatenate(dqs, axis=0).T.astype(BF16)
        dkv = jnp.concatenate(dks + dvs, axis=0)
        dkv_ref[...] = (carry[...] + dkv[:, :BLOCK].T).astype(BF16)
        carry[...] = dkv[:, BLOCK:].T
        _accumulate(ds_ref, n == 0, dsink)

    cur = lambda n: jnp.minimum(n, nb - 1)
    blk = lambda w: pl.BlockSpec((BLOCK, w), lambda n: (cur(n), 0))
    return _call(
        body, name=name, grid=(nb + 1,),
        in_specs=_attn_specs(D, kvd, nb) + [blk(D), blk(D), pl.BlockSpec((2, heads, BLOCK), lambda n: (0, 0, cur(n)))],
        out_specs=[blk(D), pl.BlockSpec((BLOCK, 2 * kvd), lambda n: (jnp.maximum(n - 1, 0), 0)),
                   pl.BlockSpec((8, 128), lambda n: (0, 0))],
        out_shape=[SDS((T, D), BF16), SDS((T, 2 * kvd), BF16), SDS((8, 128), F32)],
        scratch_shapes=[pltpu.VMEM((BLOCK, 2 * kvd), F32)],
        semantics=("arbitrary",), args=(q, kv, kv, kv, kv, tabs, tabs, sinks, do, o, stats), ride=ride)


def matmul_nt_normbwd(da, w, h_in, g, dh_out, *, name, ride=None, tm=ROW_TILE):
    T, D = h_in.shape
    S, _, K = da.shape

    def body(*refs):
        da_refs, w_refs = refs[:S], refs[S:2 * S]
        h_ref, g_ref, dh_ref, o_ref, dg_ref = refs[2 * S:]
        dn = _dot_nt(da_refs[0][...], w_refs[0][...])
        for s in range(1, S):
            dn = dn + _dot_nt(da_refs[s][...], w_refs[s][...])
        dx, hh = _rmsnorm_bwd(h_ref[...].astype(F32), g_ref[...], dn)
        o_ref[...] = dh_ref[...] + dx
        _accumulate(dg_ref, pl.program_id(0) == 0, jnp.sum(dn * hh, axis=0, keepdims=True))

    row = pl.BlockSpec((tm, D), lambda i: (i, 0))
    vec = pl.BlockSpec((1, D), lambda i: (0, 0))
    part = lambda s: pl.BlockSpec((None, tm, K), lambda i: (s, i, 0))
    cols = lambda s: pl.BlockSpec((D, K), lambda i: (0, s), pipeline_mode=pl.Buffered(1))
    return _call(
        body, name=name, grid=(T // tm,),
        in_specs=[part(s) for s in range(S)] + [cols(s) for s in range(S)] + [row, vec, row],
        out_specs=[row, vec],
        out_shape=[SDS((T, D), F32), SDS((1, D), F32)],
        semantics=("arbitrary",), args=[da] * S + [w] * S + [h_in, g, dh_out], ride=ride)


def matmuls_nt_normbwd(das, ws, h_in, gs, dh_out, then, *, name, ride=None, tm=ROW_TILE):
    T, D = h_in.shape
    tm = min(tm, T)
    n = len(das)

    def body(*refs):
        da_refs, w_refs, g_refs = refs[:n], refs[n:2 * n], refs[2 * n:3 * n]
        h_ref, dh_ref, z_ref, g2_ref, w2_ref, o_ref = refs[3 * n:3 * n + 6]
        dg_refs, (dz_ref, dg2_ref, da_ref) = refs[3 * n + 6:4 * n + 6], refs[4 * n + 6:]
        first = pl.program_id(0) == 0
        subs = [pl.ds(k, MXU_WIDTH) for k in range(0, tm, MXU_WIDTH)]
        dns = [[_dot_nt(da_ref_[rows, :], w_ref[...]) for da_ref_, w_ref in zip(da_refs, w_refs)] for rows in subs]
        dgs, dg2 = [jnp.zeros((1, D), F32) for _ in range(n)], jnp.zeros((1, D), F32)
        for rows, dn_sub in zip(subs, dns):
            hf = h_ref[rows, :].astype(F32)
            r = _rms_r(hf)
            hh = hf * r
            total = dh_ref[rows, :].astype(F32)
            for b, (dn, g_ref) in enumerate(zip(dn_sub, g_refs)):
                gd = g_ref[...] * dn
                total = total + r * (gd - hh * jnp.mean(hh * gd, axis=-1, keepdims=True))
                dgs[b] = dgs[b] + jnp.sum(dn * hh, axis=0, keepdims=True)
            o_ref[rows, :] = total.astype(STREAM)
            dz, zh = _rmsnorm_bwd(z_ref[rows, :].astype(F32), g2_ref[...], total)
            dz = dz.astype(BF16)
            dz_ref[rows, :] = dz
            dg2 = dg2 + jnp.sum(total * zh, axis=0, keepdims=True)
            da_ref[rows, :] = _dot_nt(dz, w2_ref[...]).astype(BF16)
        for dg_ref, dg in zip(dg_refs + (dg2_ref,), dgs + [dg2]):
            _accumulate(dg_ref, first, dg)

    row = pl.BlockSpec((tm, D), lambda i: (i, 0))
    vec = pl.BlockSpec((1, D), lambda i: (0, 0))
    then_in, then_out, then_shape = _then_specs(then, tm, T, D)
    return _call(
        body, name=name, grid=(T // tm,),
        in_specs=[pl.BlockSpec((tm, da.shape[1]), lambda i: (i, 0)) for da in das]
        + [pl.BlockSpec(w.shape, lambda i: (0, 0)) for w in ws] + [vec] * n + [row, row] + then_in,
        out_specs=[row] + [vec] * n + then_out,
        out_shape=[SDS((T, D), STREAM)] + [SDS((1, D), F32)] * n + then_shape,
        semantics=("arbitrary",), args=list(das) + list(ws) + list(gs) + [h_in, dh_out] + list(then), ride=ride)


def matmul_tn(a, b, *, tb, name, ride=None, ta=MXU_WIDTH):
    T, Ka = a.shape
    S, _, Nb = b.shape
    per = Nb // tb

    def body(a_ref, b_ref, o_ref):
        o_ref[...] = _dot_tn(a_ref[...], b_ref[...]).astype(BF16)

    out = _call(
        body, name=name, grid=(S * per, Ka // ta),
        in_specs=[pl.BlockSpec((T, ta), lambda j, i: (0, i)),
                  pl.BlockSpec((None, T, tb), lambda j, i: (j // per, 0, j % per))],
        out_specs=[pl.BlockSpec((ta, tb), lambda j, i: (i, j))],
        out_shape=[SDS((Ka, S * Nb), BF16)],
        semantics=("parallel", "parallel"), args=(a, b), ride=ride)
    return out[0] if ride is None else (out[0][0], out[1])


def conv_bwd(dy, bcx, conv_w, *, name, ride=None, tm=ROW_TILE):
    T, D = dy.shape
    nt = T // tm
    hb = tm // BF16_ROWS
    last = T // BF16_ROWS - 1

    def body(dy_ref, dyn_ref, b_ref, bn_ref, c_ref, u_ref, cp_ref, up_ref, cw_ref, o_ref, dw_ref):
        i = pl.program_id(0)
        c, u = c_ref[...].astype(F32), u_ref[...].astype(F32)
        cu = c * u
        cup = jnp.where(i == 0, 0.0, cp_ref[...].astype(F32) * up_ref[...].astype(F32))
        cu1, cu2 = _shift_down(cup, cu, 1), _shift_down(cup, cu, 2)
        w0, w1, w2 = cw_ref[0:1, :], cw_ref[1:2, :], cw_ref[2:3, :]
        dyf = dy_ref[...].astype(F32)
        o_ref[:, 0:D] = (dyf * (w0 * cu2 + w1 * cu1 + w2 * cu)).astype(BF16)
        dcv = dyf * b_ref[...].astype(F32)
        dcvn = jnp.where(i == nt - 1, 0.0, dyn_ref[...].astype(F32) * bn_ref[...].astype(F32))
        dcu = w2 * dcv + w1 * _shift_up(dcv, dcvn, 1) + w0 * _shift_up(dcv, dcvn, 2)
        o_ref[:, D:2 * D] = (dcu * u).astype(BF16)
        o_ref[:, 2 * D:3 * D] = (dcu * c).astype(BF16)
        row = lax.broadcasted_iota(jnp.int32, (8, D), 0)
        dw = jnp.zeros((8, D), F32)
        for tap, t in enumerate((cu2, cu1, cu)):
            dw = jnp.where(row == tap, jnp.sum(dcv * t, axis=0, keepdims=True), dw)
        _accumulate(dw_ref, i == 0, dw)

    tile = lambda col: pl.BlockSpec((tm, D), lambda i: (i, col))
    prev = lambda col: pl.BlockSpec((BF16_ROWS, D), lambda i: (jnp.maximum(i * hb - 1, 0), col))
    nxt = lambda col: pl.BlockSpec((BF16_ROWS, D), lambda i: (jnp.minimum((i + 1) * hb, last), col))
    return _call(
        body, name=name, grid=(nt,),
        in_specs=[tile(0), nxt(0), tile(0), nxt(0), tile(1), tile(2), prev(1), prev(2),
                  pl.BlockSpec((3, D), lambda i: (0, 0))],
        out_specs=[pl.BlockSpec((tm, 3 * D), lambda i: (i, 0)), pl.BlockSpec((8, D), lambda i: (0, 0))],
        out_shape=[SDS((T, 3 * D), BF16), SDS((8, D), F32)],
        semantics=("arbitrary",), args=(dy, dy, bcx, bcx, bcx, bcx, bcx, bcx, conv_w), ride=ride)


class NoTraffic:
    def ride(self, kernel_name):
        return None

    def landed(self, kernel_name, results, wts):
        pass

    def grad(self, key, value):
        pass


def local_step(x, target, wts, vec, traffic):
    T, D = x.shape
    tabs = rope_tables(T)
    small = {}

    def run(builder, *args, name, **kw):
        ride = traffic.ride(name)
        if ride is None:
            return builder(*args, name=name, **kw)
        out, extra = builder(*args, name=name, ride=ride, **kw)
        traffic.landed(name, extra, wts)
        return out

    bcx, xn1 = run(norm_matmul, x, vec["a_pre"], wts["w_in"], tn=3 * D, split=1, name="a_in")
    bcx = bcx[0]
    h1, z0, y0 = run(conv_mix_out, bcx, vec["conv_w"], wts["w_out"], vec["a_post"], x, name="a_out")
    gu0, act0, xt2 = run(norm_swiglu_in, h1, vec["ffn_pre0"], wts["gu0"], name="ffn0_in")
    h2, z1 = run(plain_mix_out, act0, wts["wd0"], vec["ffn_post0"], h1, name="ffn0_out")
    kvp, xkv, qp, xq = norm2_matmul(h2, [vec["kv_norm"], vec["b_pre"]], [wts["w_kv"], wts["w_q"]], name="kvq_in")
    attn, attn_stats = run(attention_fwd, qp, kvp, tabs, vec["sinks"], name="attn_fwd")
    h3, z2 = plain_mix_out(attn, wts["w_o"], vec["b_post"], h2, name="attn_out", tm=BIG_ROW_TILE)
    gu1, act1, xt3 = run(norm_swiglu_in, h3, vec["ffn_pre1"], wts["gu1"], name="ffn1_in")
    dy, dz3, small["ffn_post1"], dact1, loss = plain_mix_out(act1, wts["wd1"], vec["ffn_post1"], h3, name="ffn1_out",
                                                             target=target)

    def ffn_bwd(layer, dz, dact, gu, act, xt, h_in, dh, then, gu_first):
        tag = "ffn%d" % layer
        dwd = lambda: traffic.grad("wd%d" % layer, run(matmul_tn, act, dz[None], tb=D, name=tag + "_dwd"))
        dwgu = lambda: traffic.grad("gu%d" % layer, run(swiglu_bwd_tn, xt, dact, gu, name=tag + "_dwgu"))
        for step in ((dwgu, dwd) if gu_first else (dwd, dwgu)):
            step()
        dh_in, small["ffn_pre%d" % layer], dz_, dg_, da_ = run(
            swiglu_bwd_in, dact, gu, wts["gu%d" % layer], h_in, vec["ffn_pre%d" % layer], dh, then,
            name=tag + "_in_bwd")
        return dh_in, dz_, dg_, da_

    dh3, dz2, small["b_post"], dattn = ffn_bwd(1, dz3, dact1, gu1, act1, xt3, h3, dy,
                                               (z2, vec["b_post"], wts["w_o"]), gu_first=False)
    traffic.grad("w_o", matmul_tn(attn, dz2[None], tb=D, name="attn_dwo"))
    dq, dkv, small["sinks"] = run(attention_bwd, qp, kvp, tabs, vec["sinks"], dattn, attn, attn_stats,
                                  name="attn_bwd")
    traffic.grad("w_q", matmul_tn(xq, dq[None], tb=D, name="attn_dwq"))
    traffic.grad("w_kv", matmul_tn(xkv, dkv[None], tb=dkv.shape[1], name="attn_dwkv"))
    dh2, small["b_pre"], small["kv_norm"], dz1, small["ffn_post0"], dact0 = run(
        matmuls_nt_normbwd, [dq, dkv], [wts["w_q"], wts["w_kv"]], h2, [vec["b_pre"], vec["kv_norm"]], dh3,
        (z1, vec["ffn_post0"], wts["wd0"]), name="qkv_in_bwd")
    dh1, dz0, small["a_post"], dyc = ffn_bwd(0, dz1, dact0, gu0, act0, xt2, h1, dh2,
                                             (z0, vec["a_post"], wts["w_out"]), gu_first=True)
    traffic.grad("w_out", run(matmul_tn, y0, dz0[None], tb=D, name="a_dwout"))
    dbcx, small["conv_w"] = run(conv_bwd, dyc, bcx, vec["conv_w"], name="a_conv_bwd")
    traffic.grad("w_in", matmul_tn(xn1, dbcx[None], tb=3 * D // 2, name="a_dwin"))
    dx, small["a_pre"] = run(matmul_nt_normbwd, dbcx[None], wts["w_in"], x, vec["a_pre"], dh1, name="a_in_bwd",
                             tm=ROW_TILE // 2)
    return loss, dx, small


SMALL_ROWS = 16
LOSS_ROW = 13

WHOLE = None
GATHER_PLAN = {"cast_rest": [("w_in", WHOLE)],
               "a_in": [("w_out", WHOLE), ("gu0", (0, 18))],
               "a_out": [("gu0", (18, 14))],
               "ffn0_in": [("wd0", WHOLE), ("w_kv", WHOLE), ("w_q", WHOLE), ("w_o", WHOLE)],
               "ffn0_out": [("gu1", (0, 16))],
               "attn_fwd": [("gu1", (16, 16))],
               "ffn1_in": [("wd1", WHOLE)]}
PAIR_PLAN = {"ffn1_dwgu": ["wd1"], "ffn1_in_bwd": ["gu1"], "attn_bwd": ["w_o"], "qkv_in_bwd": ["w_q", "w_kv"],
             "ffn0_dwd": ["gu0"], "ffn0_in_bwd": ["wd0"], "a_conv_bwd": ["w_out"]}
PAIR_ALONE = ["w_in"]
CHIP_PLAN = {"ffn1_in_bwd": [("wd1", WHOLE)], "attn_bwd": [("gu1", WHOLE)],
             "ffn0_dwgu": [("w_o", WHOLE), ("w_q", WHOLE), ("w_kv", WHOLE)],
             "ffn0_in_bwd": [("gu0", WHOLE)], "a_dwout": [("wd0", (0, 8))], "a_conv_bwd": [("wd0", (8, 14))],
             "a_in_bwd": [("w_out", WHOLE), ("w_in", WHOLE)]}
HALF_PLAN = {"a_in_bwd": ["gu0", "gu1", "wd0", "wd1", "w_kv", "w_q", "w_o"]}
GRAD_KIND = dict(KIND, gu0="split", gu1="split")


class Traffic:
    def __init__(self, wholes, quarter, c_arr, pc_arr):
        self.wholes, self.quarter, self.c_arr, self.pc_arr = wholes, quarter, c_arr, pc_arr
        self.views, self.sums, self.got = {}, {}, {}
        self.reduced = {}
        self.stages = {}

    def reduce(self, keys, name):
        return chip_reduce([self.sums[k] for k in keys], [self.got[k] for k in keys], [GRAD_KIND[k] for k in keys],
                           self.pc_arr, name=name)

    def ride(self, name, small=None):
        rides, stages = [], []
        if name in GATHER_PLAN:
            plan = GATHER_PLAN[name]
            rides.append(gather_ride([self.wholes[k] for k, _ in plan],
                                     [(KIND[k], self.quarter[k], part) for k, part in plan], small))
            stages.append(("gather", [k for k, _ in plan]))
        if name in CHIP_PLAN:
            plan = CHIP_PLAN[name]
            rides.append(chip_ride([self.sums[k] for k, _ in plan],
                                   [(GRAD_KIND[k], self.quarter[k], part) for k, part in plan],
                                   earlier=[self.got.get(k) for k, _ in plan]))
            stages.append(("chip", [k for k, _ in plan]))
        if name in PAIR_PLAN:
            keys = PAIR_PLAN[name]
            rides.append(pair_ride([self.views[k] for k in keys]))
            stages.append(("pair", keys))
        if name in HALF_PLAN:
            keys = HALF_PLAN[name]
            rides.append(half_ride(self.reduce(keys, "chip_reduce_early")))
            stages.append(("half", keys))
        self.stages[name] = stages
        return join(rides)

    def landed(self, name, results, wts):
        results = list(results)
        for stage, keys in self.stages[name]:
            mine, results = results[:len(keys)], results[len(keys):]
            if stage == "gather":
                for k, whole in zip(keys, mine):
                    self.wholes[k] = wts[k] = whole
            elif stage == "chip":
                self.got.update(zip(keys, mine))
            elif stage == "half":
                self.reduced.update(zip(keys, mine))
            else:
                for k, got in zip(keys, mine):
                    self.sums[k] = pair_add(self.views[k], got, self.c_arr, name="pair_add_" + k)

    def grad(self, key, value):
        r, ws = self.quarter[key]
        view = {"row": (N_CHIPS, 2, r // 2, ws), "col": (1, 2, r // 2, N_CHIPS * ws), "split": (2, 2, r // 2, 2 * ws)}
        self.views[key] = value.reshape(view[GRAD_KIND[key]])
        if key in PAIR_ALONE:
            (got,) = alone(pair_ride([self.views[key]]), name="pair_exchange_" + key)
            self.sums[key] = pair_add(self.views[key], got, self.c_arr, name="pair_add_" + key)


def kernel(x, a_pre_norm, a_w_in, a_conv_w, a_w_out, a_post_norm, ffn_pre_norm, ffn_w_gate_up, ffn_w_down, ffn_post_norm, kv_norm, w_kv, b_pre_norm, b_w_q, b_sinks, b_w_o, b_post_norm, loss_target, m_a_pre_norm, m_a_w_in, m_a_conv_w, m_a_w_out, m_a_post_norm, m_ffn_pre_norm, m_ffn_w_gate_up, m_ffn_w_down, m_ffn_post_norm, m_kv_norm, m_w_kv, m_b_pre_norm, m_b_w_q, m_b_sinks, m_b_w_o, m_b_post_norm, v_a_pre_norm, v_a_w_in, v_a_conv_w, v_a_w_out, v_a_post_norm, v_ffn_pre_norm, v_ffn_w_gate_up, v_ffn_w_down, v_ffn_post_norm, v_kv_norm, v_w_kv, v_b_pre_norm, v_b_w_q, v_b_sinks, v_b_w_o, v_b_post_norm):
    T, D = x.shape[1], x.shape[2]
    xi, yi, ci = _place()
    p = 2 * xi + yi
    p_arr = jnp.reshape(p, (1,)).astype(jnp.int32)
    c_arr = jnp.reshape(ci, (1,)).astype(jnp.int32)
    pc_arr = jnp.stack([p, ci]).astype(jnp.int32)
    me_arr = jnp.reshape(4 * xi + 2 * yi + ci, (1,)).astype(jnp.int32)
    qd = D // N_CHIPS

    big = {"w_in": (a_w_in, 0), "w_out": (a_w_out, 0), "gu0": (ffn_w_gate_up, 0), "gu1": (ffn_w_gate_up, 1),
           "wd0": (ffn_w_down, 0), "wd1": (ffn_w_down, 1), "w_kv": (w_kv[None], 0), "w_q": (b_w_q, 0),
           "w_o": (b_w_o, 0)}
    names = list(big)
    quarter = {k: w.shape[1:] for k, (w, _) in big.items()}
    source = lambda k: big[k] + (KIND[k],)
    traffic = Traffic(dict(zip(names[:1], cast_quarters([source(names[0])], p_arr, name="cast_first"))), quarter,
                      c_arr, pc_arr)
    small_shard = jnp.concatenate([a_pre_norm, a_post_norm, a_conv_w[0], jnp.zeros((3, qd), F32)], axis=0)
    wts = {}
    rest, (*landed, small_full) = cast_quarters([source(k) for k in names[1:]], p_arr, name="cast_rest",
                                                ride=traffic.ride("cast_rest", small_shard))
    traffic.wholes.update(zip(names[1:], rest))
    traffic.landed("cast_rest", landed, wts)
    rows = lambda k: jnp.transpose(small_full[:, k], (1, 0, 2)).reshape(-1, D)
    vec = {"a_pre": rows(slice(0, 1)), "a_post": rows(slice(1, 2)), "conv_w": rows(slice(2, 5)),
           "ffn_pre0": ffn_pre_norm[0:1], "ffn_pre1": ffn_pre_norm[1:2],
           "ffn_post0": ffn_post_norm[0:1], "ffn_post1": ffn_post_norm[1:2],
           "kv_norm": kv_norm[None], "b_pre": b_pre_norm, "b_post": b_post_norm, "sinks": b_sinks}

    loss, dx, small = local_step(x[0], loss_target[0], wts, vec, traffic)

    pad = lambda a: jnp.pad(a, ((0, 0), (0, D - a.shape[1])))
    small_block = jnp.concatenate(
        [small["a_pre"], small["a_post"], small["conv_w"][0:3], small["ffn_pre0"], small["ffn_pre1"],
         small["ffn_post0"], small["ffn_post1"], small["kv_norm"], small["b_pre"], small["b_post"],
         pad(small["sinks"][0:1]), pad(loss[0:1]), jnp.zeros((SMALL_ROWS - LOSS_ROW - 1, D), F32)], axis=0)
    late = [k for k in names if k not in traffic.reduced]
    *swapped, small_blocks = alone(join([half_ride(traffic.reduce(late, "chip_reduce_late")),
                                         chip_ride([], [], small_block)]), name="last_exchange")
    traffic.reduced.update(zip(late, swapped))
    grad = {k: traffic.reduced[k].reshape(quarter[k]) for k in names}
    small_sum = small_reduce(small_blocks, me_arr)

    out = {}
    out["a_w_in"] = adamw(a_w_in, [grad["w_in"]], m_a_w_in, v_a_w_in, name="adamw_a_w_in")
    out["a_w_out"] = adamw(a_w_out, [grad["w_out"]], m_a_w_out, v_a_w_out, name="adamw_a_w_out")
    out["ffn_w_gate_up"] = adamw(ffn_w_gate_up, [grad["gu0"], grad["gu1"]], m_ffn_w_gate_up, v_ffn_w_gate_up,
                                 name="adamw_ffn_w_gate_up")
    out["ffn_w_down"] = adamw(ffn_w_down, [grad["wd0"], grad["wd1"]], m_ffn_w_down, v_ffn_w_down,
                              name="adamw_ffn_w_down")
    out["w_kv"] = [o[0] for o in adamw(w_kv[None], [grad["w_kv"]], m_w_kv[None], v_w_kv[None], name="adamw_w_kv")]
    out["b_w_q"] = adamw(b_w_q, [grad["w_q"]], m_b_w_q, v_b_w_q, name="adamw_b_w_q")
    out["b_w_o"] = adamw(b_w_o, [grad["w_o"]], m_b_w_o, v_b_w_o, name="adamw_b_w_o")

    def pack(a_pre, a_post, conv, ffn_pre, ffn_post, kvn, b_pre, b_post, sinks):
        return jnp.concatenate([pad(a_pre), pad(a_post), pad(conv[0]), ffn_pre, ffn_post, kvn[None], b_pre, b_post,
                                pad(sinks), jnp.zeros((SMALL_ROWS - 13, D), F32)], axis=0)

    g_small = jnp.concatenate([pad(lax.dynamic_slice(small_sum, (0, p * qd), (5, qd))), small_sum[5:]], axis=0)
    w_small = pack(a_pre_norm, a_post_norm, a_conv_w, ffn_pre_norm, ffn_post_norm, kv_norm, b_pre_norm, b_post_norm,
                   b_sinks)
    m_small = pack(m_a_pre_norm, m_a_post_norm, m_a_conv_w, m_ffn_pre_norm, m_ffn_post_norm, m_kv_norm,
                   m_b_pre_norm, m_b_post_norm, m_b_sinks)
    v_small = pack(v_a_pre_norm, v_a_post_norm, v_a_conv_w, v_ffn_pre_norm, v_ffn_post_norm, v_kv_norm,
                   v_b_pre_norm, v_b_post_norm, v_b_sinks)
    packed = adamw(w_small[None], [g_small], m_small[None], v_small[None], name="adamw_small")
    ns = b_sinks.shape[1]
    unpack = lambda a: {"a_pre_norm": a[0:1, :qd], "a_post_norm": a[1:2, :qd], "a_conv_w": a[None, 2:5, :qd],
                        "ffn_pre_norm": a[5:7], "ffn_post_norm": a[7:9], "kv_norm": a[9], "b_pre_norm": a[10:11],
                        "b_post_norm": a[11:12], "b_sinks": a[12:13, :ns]}
    unpacked = [unpack(a[0]) for a in packed]
    for k in unpacked[0]:
        out[k] = [u[k] for u in unpacked]

    order = ["a_pre_norm", "a_w_in", "a_conv_w", "a_w_out", "a_post_norm", "ffn_pre_norm", "ffn_w_gate_up",
             "ffn_w_down", "ffn_post_norm", "kv_norm", "w_kv", "b_pre_norm", "b_w_q", "b_sinks", "b_w_o",
             "b_post_norm"]
    return (small_sum[LOSS_ROW, 0], dx[None], *[out[k][0] for k in order], *[out[k][1] for k in order],
            *[out[k][2] for k in order], *[out[k][3] for k in order])
```

```python
import math

import jax
import jax.numpy as jnp
from jax import lax
from jax.experimental import pallas as pl
from jax.experimental.pallas import tpu as pltpu

F32 = jnp.float32
BF16 = jnp.bfloat16
SDS = jax.ShapeDtypeStruct
MESH = pl.DeviceIdType.MESH
DMA = pltpu.SemaphoreType.DMA
HBM_SPEC = pl.BlockSpec(memory_space=pltpu.HBM)

EPS = 1e-6
NEG = -1e30
HEAD_DIM = 64
N_KV_HEADS = 4
BLOCK = 128
ROT_DIM = HEAD_DIM // 4
ROPE_THETA = 500000.0
N_CHIPS = 4

ADAM_LR = 0.001
ADAM_B1 = 0.9
ADAM_B2 = 0.999
ADAM_EPS = 1e-08
ADAM_WD = 0.01
ADAM_STEP = 10

VMEM_LIMIT_BYTES = 52 * 1024 * 1024
ROW_TILE = 512
BF16_ROWS = 16
STREAM = BF16
MXU_WIDTH = 256

KIND = {"w_in": "col", "gu0": "col", "gu1": "col", "w_out": "row", "wd0": "row", "wd1": "row", "w_kv": "row",
        "w_q": "row", "w_o": "row"}


def _params(*semantics):
    return pltpu.CompilerParams(dimension_semantics=semantics, vmem_limit_bytes=VMEM_LIMIT_BYTES)


def _row_tile(rows, limit, step=8):
    return max(t for t in range(step, limit + 1, step) if rows % t == 0)


def _place():
    return lax.axis_index("x"), lax.axis_index("y"), lax.axis_index("c")


def _other_chips(x, y):
    return [(1 - x, y), (x, 1 - y), (1 - x, 1 - y)]


def _remote(src, dst, send_sem, recv_sem, to):
    return pltpu.make_async_remote_copy(src_ref=src, dst_ref=dst, send_sem=send_sem, recv_sem=recv_sem,
                                        device_id=to, device_id_type=MESH)


def _full_shape(kind, quarter):
    r, ws = quarter
    return (N_CHIPS * r, ws) if kind == "row" else (r, N_CHIPS * ws)


def _rows_of(h, part):
    lo, n = (0, h) if part is None else (part[0] * BF16_ROWS, part[1] * BF16_ROWS)
    assert lo + n <= h, (h, part)
    return lo, n


def _half_of_quarter(ref, kind, quarter, part, q, half):
    r, ws = quarter
    h = r // 2
    lo, n = _rows_of(h, part)
    if kind == "row":
        return ref.at[pl.ds(pl.multiple_of(q * r + half * h + lo, BF16_ROWS), n)]
    return ref.at[pl.ds(pl.multiple_of(half * h + lo, BF16_ROWS), n), pl.ds(pl.multiple_of(q * ws, 128), ws)]


class Ride:
    def __init__(self, operands, out_shape, aliases, sems, make):
        self.operands, self.out_shape, self.aliases, self.sems, self.make = operands, out_shape, aliases, sems, make


def join(rides):
    rides = [r for r in rides if r is not None]
    if len(rides) < 2:
        return rides[0] if rides else None
    aliases, at = {}, [0, 0, 0]
    cuts = []
    for r in rides:
        aliases.update({at[0] + i: at[1] + o for i, o in r.aliases.items()})
        cuts.append(tuple(at))
        at = [at[0] + len(r.operands), at[1] + len(r.out_shape), at[2] + len(r.sems)]
    cuts.append(tuple(at))

    def make(ins, outs, sem):
        made = [r.make(ins[lo[0]:hi[0]], outs[lo[1]:hi[1]], sem[lo[2]:hi[2]]) for r, lo, hi in zip(rides, cuts, cuts[1:])]

        def start():
            for s, _ in made:
                s()

        def finish():
            for _, f in made:
                f()

        return start, finish

    return Ride(sum((list(r.operands) for r in rides), []), sum((list(r.out_shape) for r in rides), []), aliases,
                sum((list(r.sems) for r in rides), []), make)


def _call(body, *, name, grid, in_specs, out_specs, out_shape, args, scratch_shapes=(), semantics=None, ride=None,
          prefetch=None):
    pre = 0 if prefetch is None else 1
    n_in, n_out, n_scr = len(in_specs), len(out_specs), len(scratch_shapes)
    r_in, r_out = (len(ride.operands), len(ride.out_shape)) if ride is not None else (0, 0)
    a, b = pre + n_in, pre + n_in + r_in
    c, d = b + n_out, b + n_out + r_out
    e = d + n_scr

    def riding(*refs):
        start, finish = ride.make(refs[a:b], refs[c:d], refs[e:])
        ids = [pl.program_id(k) for k in range(len(grid))]
        first, last = ids[0] == 0, ids[0] == grid[0] - 1
        for k in range(1, len(grid)):
            first, last = first & (ids[k] == 0), last & (ids[k] == grid[k] - 1)
        pl.when(first)(start)
        body(*refs[:a], *refs[b:c], *refs[d:e])
        pl.when(last)(finish)

    if ride is None:
        kernel_body, extra_in, extra_out, extra_shape, extra_scr, aliases = body, [], [], [], [], {}
        params = _params(*semantics)
    else:
        kernel_body, extra_in, extra_out = riding, [HBM_SPEC] * r_in, [HBM_SPEC] * r_out
        extra_shape, extra_scr = list(ride.out_shape), list(ride.sems)
        aliases = {pre + n_in + i: n_out + o for i, o in ride.aliases.items()}
        params = _params(*(("arbitrary",) * len(grid)))
    specs = dict(grid=grid, in_specs=list(in_specs) + extra_in, out_specs=list(out_specs) + extra_out,
                 scratch_shapes=list(scratch_shapes) + extra_scr)
    if prefetch is not None:
        specs = dict(grid_spec=pltpu.PrefetchScalarGridSpec(num_scalar_prefetch=1, **specs))
        args = (prefetch,) + tuple(args)
    outs = pl.pallas_call(kernel_body, name=name, out_shape=list(out_shape) + extra_shape,
                          input_output_aliases=aliases, compiler_params=params, **specs,
                          )(*args, *(ride.operands if ride is not None else ()))
    return outs if ride is None else (outs[:n_out], outs[n_out:])


def alone(ride, *, name):
    def body(*refs):
        n = len(ride.operands)
        start, finish = ride.make(refs[:n], refs[n:n + len(ride.out_shape)], refs[n + len(ride.out_shape):])
        start()
        finish()

    return pl.pallas_call(
        body, name=name, in_specs=[HBM_SPEC] * len(ride.operands), out_specs=[HBM_SPEC] * len(ride.out_shape),
        out_shape=list(ride.out_shape), input_output_aliases=dict(ride.aliases), scratch_shapes=list(ride.sems),
    )(*ride.operands)


def gather_ride(wholes, metas, small=None):
    n = len(wholes)
    operands, out_shape = list(wholes), [SDS(s.shape, s.dtype) for s in wholes]
    sems = [DMA((n, 3)), DMA((n, 3)), DMA((n, 3)), DMA((n, 3))]
    if small is not None:
        operands.append(small)
        out_shape.append(SDS((N_CHIPS,) + small.shape, small.dtype))
        sems += [DMA((3,)), DMA((3,)), DMA(())]

    def make(ins, outs, sem):
        send1, recv1, send2, recv2 = sem[:4]
        x, y, c = _place()
        p = 2 * x + y
        chips = _other_chips(x, y)
        me, sibling = (x, y, c), (x, y, 1 - c)
        part = lambda t, q, half: _half_of_quarter(outs[t], *metas[t], q, half)
        first = []
        for j, (qx, qy) in enumerate(chips):
            if small is not None:
                first.append(_remote(ins[n], outs[n].at[p], sem[4].at[j], sem[5].at[j], (qx, qy, c)))
            for t in range(n):
                first.append(_remote(part(t, p, c), part(t, p, c), send1.at[t, j], recv1.at[t, j], (qx, qy, c)))
        local = [] if small is None else [pltpu.make_async_copy(ins[n], outs[n].at[p], sem[6])]

        def start():
            for cp in local + first:
                cp.start()

        def finish():
            passed = []
            for j, (qx, qy) in enumerate(chips):
                q = 2 * qx + qy
                for t in range(n):
                    landed = part(t, q, c)
                    _remote(landed, landed, send1.at[t, j], recv1.at[t, j], me).wait_recv()
                    cp = _remote(landed, landed, send2.at[t, j], recv2.at[t, j], sibling)
                    cp.start()
                    passed.append(cp)
            for j, (qx, qy) in enumerate(chips):
                q = 2 * qx + qy
                if small is not None:
                    _remote(outs[n].at[q], outs[n].at[q], sem[4].at[j], sem[5].at[j], me).wait_recv()
                for t in range(n):
                    theirs = part(t, q, 1 - c)
                    _remote(theirs, theirs, send2.at[t, j], recv2.at[t, j], me).wait_recv()
            for cp in first + passed:
                cp.wait_send()
            for cp in local:
                cp.wait()

        return start, finish

    return Ride(operands, out_shape, {t: t for t in range(n)}, sems, make)


def chip_ride(sums, metas, small=None, earlier=None):
    n = len(sums)
    operands = list(sums)
    out_shape = [SDS((3, s.shape[1], quarter[1]), s.dtype) for s, (_, quarter, _) in zip(sums, metas)]
    sems = [DMA((n, 3)), DMA((n, 3))] if n else []
    if small is not None:
        operands.append(small)
        out_shape.append(SDS((8,) + small.shape, small.dtype))
        sems += [DMA((7,)), DMA((7,)), DMA(())]
    aliases = {}
    for t, buffer in enumerate(earlier or [None] * n):
        if buffer is not None:
            aliases[len(operands)] = t
            operands.append(buffer)

    def make(ins, outs, sem):
        x, y, c = _place()
        cps = []
        for j, (qx, qy) in enumerate(_other_chips(x, y)):
            q = 2 * qx + qy
            for t in range(n):
                kind, (_, ws), part = metas[t]
                rows = pl.ds(*_rows_of(ins[t].shape[1], part))
                if kind == "row":
                    src = ins[t].at[q, rows]
                elif kind == "col":
                    src = ins[t].at[0, rows, pl.ds(pl.multiple_of(q * ws, 128), ws)]
                else:
                    src = ins[t].at[q // 2, rows, pl.ds(pl.multiple_of((q % 2) * ws, 128), ws)]
                cps.append(_remote(src, outs[t].at[j, rows], sem[0].at[t, j], sem[1].at[t, j], (qx, qy, c)))
        local = []
        if small is not None:
            ssend, srecv, lsem = sem[2 * bool(n):2 * bool(n) + 3]
            local.append(pltpu.make_async_copy(ins[n], outs[n].at[0], lsem))
            for k in range(1, 8):
                peer = (x ^ (k >> 2 & 1), y ^ (k >> 1 & 1), c ^ (k & 1))
                cps.append(_remote(ins[n], outs[n].at[k], ssend.at[k - 1], srecv.at[k - 1], peer))

        def start():
            for cp in local + cps:
                cp.start()

        def finish():
            for cp in cps + local:
                cp.wait()

        return start, finish

    return Ride(operands, out_shape, aliases, sems, make)


def pair_ride(grads):
    n = len(grads)

    def make(ins, outs, sem):
        x, y, c = _place()
        cps = [_remote(ins[t].at[:, 1 - c], outs[t], sem[0].at[t], sem[1].at[t], (x, y, 1 - c)) for t in range(n)]

        def start():
            for cp in cps:
                cp.start()

        def finish():
            for cp in cps:
                cp.wait()

        return start, finish

    return Ride(list(grads), [SDS((g.shape[0],) + g.shape[2:], g.dtype) for g in grads], {}, [DMA((n,)), DMA((n,))],
                make)


def half_ride(quarters):
    n = len(quarters)

    def make(ins, outs, sem):
        x, y, c = _place()
        sends = [_remote(outs[t].at[c], outs[t].at[c], sem[0].at[t], sem[1].at[t], (x, y, 1 - c)) for t in range(n)]

        def start():
            for cp in sends:
                cp.start()

        def finish():
            for t in range(n):
                theirs = outs[t].at[1 - c]
                _remote(theirs, theirs, sem[0].at[t], sem[1].at[t], (x, y, c)).wait_recv()
            for cp in sends:
                cp.wait_send()

        return start, finish

    return Ride(list(quarters), [SDS(q.shape, q.dtype) for q in quarters], {t: t for t in range(n)},
                [DMA((n,)), DMA((n,))], make)


CAST_STEPS = 4


def cast_quarters(sources, p_arr, *, name, ride=None):
    n = len(sources)
    in_specs, out_specs, out_shape = [], [], []
    for w, layer, kind in sources:
        _, r, ws = w.shape
        tr = r // CAST_STEPS
        assert tr % BF16_ROWS == 0, w.shape
        in_specs.append(pl.BlockSpec((None, tr, ws), lambda i, p_ref, layer=layer: (layer, i, 0)))
        out_specs.append(pl.BlockSpec((tr, ws), (lambda i, p_ref: (p_ref[0] * CAST_STEPS + i, 0)) if kind == "row"
                                      else (lambda i, p_ref: (i, p_ref[0]))))
        out_shape.append(SDS(_full_shape(kind, (r, ws)), BF16))

    def body(p_ref, *refs):
        for w_ref, o_ref in zip(refs[:n], refs[n:]):
            o_ref[...] = w_ref[...].astype(BF16)

    return _call(body, name=name, grid=(CAST_STEPS,), in_specs=in_specs, out_specs=out_specs, out_shape=out_shape,
                 semantics=("parallel",), args=[w for w, _, _ in sources], ride=ride, prefetch=p_arr)


def pair_add(own, got, c_arr, *, name):
    A, _, h, W = own.shape
    th = _row_tile(h, max(BF16_ROWS, (3 << 19) // W), BF16_ROWS)

    def body(c_ref, a_ref, b_ref, o_ref):
        o_ref[...] = (a_ref[...].astype(F32) + b_ref[...].astype(F32)).astype(BF16)

    return pl.pallas_call(
        body, name=name,
        grid_spec=pltpu.PrefetchScalarGridSpec(
            num_scalar_prefetch=1, grid=(A, h // th),
            in_specs=[pl.BlockSpec((None, None, th, W), lambda q, i, c_ref: (q, c_ref[0], i, 0)),
                      pl.BlockSpec((None, th, W), lambda q, i, c_ref: (q, i, 0))],
            out_specs=pl.BlockSpec((None, th, W), lambda q, i, c_ref: (q, i, 0))),
        out_shape=SDS((A, h, W), BF16),
        compiler_params=_params("parallel", "parallel"),
    )(c_arr, own, got)


REDUCE_STEPS = 2


def chip_reduce(sums, got, kinds, pc_arr, *, name):
    n = len(sums)
    mine = {"row": lambda i, pc_ref: (pc_ref[0], i, 0), "col": lambda i, pc_ref: (0, i, pc_ref[0]),
            "split": lambda i, pc_ref: (pc_ref[0] // 2, i, pc_ref[0] % 2)}
    a_specs, b_specs, o_specs, out_shape = [], [], [], []
    for g, kind in zip(got, kinds):
        _, h, ws = g.shape
        th = h // REDUCE_STEPS
        assert th % BF16_ROWS == 0, g.shape
        a_specs.append(pl.BlockSpec((None, th, ws), mine[kind]))
        b_specs.append(pl.BlockSpec((3, th, ws), lambda i, pc_ref: (0, i, 0)))
        o_specs.append(pl.BlockSpec((None, th, ws), lambda i, pc_ref: (pc_ref[1], i, 0)))
        out_shape.append(SDS((2, h, ws), F32))

    def body(pc_ref, *refs):
        for a_ref, b_ref, o_ref in zip(refs[:n], refs[n:2 * n], refs[2 * n:]):
            o_ref[...] = ((a_ref[...].astype(F32) + b_ref[0].astype(F32)) + b_ref[1].astype(F32)) + b_ref[2].astype(F32)

    return _call(body, name=name, grid=(REDUCE_STEPS,), in_specs=a_specs + b_specs, out_specs=o_specs,
                 out_shape=out_shape, semantics=("parallel",), args=list(sums) + list(got), prefetch=pc_arr)


def small_reduce(blocks, me_arr):
    _, rows, D = blocks.shape

    def body(me_ref, b_ref, o_ref):
        me = me_ref[0]
        total = b_ref[me]
        for d in range(1, 8):
            total = total + b_ref[d ^ me]
        o_ref[...] = total

    return pl.pallas_call(
        body, name="small_reduce",
        grid_spec=pltpu.PrefetchScalarGridSpec(
            num_scalar_prefetch=1, grid=(1,),
            in_specs=[pl.BlockSpec((8, rows, D), lambda i, me_ref: (0, 0, 0))],
            out_specs=pl.BlockSpec((rows, D), lambda i, me_ref: (0, 0))),
        out_shape=SDS((rows, D), F32),
        compiler_params=_params("arbitrary"),
    )(me_arr, blocks)


def adamw(w, gs, m, v, *, name):
    L, r, cols = w.shape
    tr = _row_tile(r, 256)
    nt = r // tr

    def body(*refs):
        w_ref, m_ref, v_ref = refs[:3]
        g_refs = refs[3:3 + L]
        g_out, d_out, m_out, v_out = refs[3 + L:]
        layer = pl.program_id(0)
        g = g_refs[0][...]
        for l in range(1, L):
            g = jnp.where(layer == l, g_refs[l][...], g)
        m_new = ADAM_B1 * m_ref[...] + (1.0 - ADAM_B1) * g
        v_new = ADAM_B2 * v_ref[...] + (1.0 - ADAM_B2) * (g * g)
        m_hat = m_new / (1.0 - ADAM_B1 ** ADAM_STEP)
        v_hat = v_new / (1.0 - ADAM_B2 ** ADAM_STEP)
        g_out[...] = g
        m_out[...] = m_new
        v_out[...] = v_new
        d_out[...] = -ADAM_LR * (m_hat / (jnp.sqrt(v_hat) + ADAM_EPS) + ADAM_WD * w_ref[...])

    full = pl.BlockSpec((None, tr, cols), lambda l, i: (l, i, 0))
    g_spec = lambda l0: pl.BlockSpec((tr, cols), lambda l, i: (jnp.where(l == l0, i, jnp.where(l < l0, 0, nt - 1)), 0))
    return pl.pallas_call(
        body, name=name, grid=(L, nt),
        in_specs=[full, full, full] + [g_spec(l0) for l0 in range(L)],
        out_specs=[full] * 4,
        out_shape=[SDS(w.shape, F32)] * 4,
        compiler_params=_params("arbitrary", "arbitrary"),
    )(w, m, v, *gs)


def _rms_r(xf):
    return lax.rsqrt(jnp.mean(xf * xf, axis=-1, keepdims=True) + EPS)


def _rmsnorm_bwd(xf, g, dy):
    r = _rms_r(xf)
    xh = xf * r
    gd = g * dy
    return r * (gd - xh * jnp.mean(xh * gd, axis=-1, keepdims=True)), xh


def _dot(a, b):
    return jnp.dot(a, b, preferred_element_type=F32)


def _dot_nt(a, b):
    return lax.dot_general(a, b, (((1,), (1,)), ((), ())), preferred_element_type=F32)


def _dot_tn(a, b):
    return lax.dot_general(a, b, (((0,), (0,)), ((), ())), preferred_element_type=F32)


def _accumulate(ref, first, value):
    @pl.when(first)
    def _():
        ref[...] = value

    @pl.when(jnp.logical_not(first))
    def _():
        ref[...] += value


def norm_matmul(x, g, w, *, tn, split, name, ride=None, tm=ROW_TILE):
    T, D = x.shape
    N = w.shape[1]
    per = N // split // tn

    def body(x_ref, g_ref, w_ref, o_ref, xn_ref):
        @pl.when(pl.program_id(1) == 0)
        def _():
            xf = x_ref[...].astype(F32)
            xn_ref[...] = (xf * _rms_r(xf) * g_ref[...]).astype(BF16)

        o_ref[...] = _dot(xn_ref[...], w_ref[...]).astype(BF16)

    return _call(
        body, name=name, grid=(T // tm, N // tn),
        in_specs=[pl.BlockSpec((tm, D), lambda i, j: (i, 0)),
                  pl.BlockSpec((1, D), lambda i, j: (0, 0)),
                  pl.BlockSpec((D, tn), lambda i, j: (0, j))],
        out_specs=[pl.BlockSpec((None, tm, tn), lambda i, j: (j // per, i, j % per)),
                   pl.BlockSpec((tm, D), lambda i, j: (i, 0))],
        out_shape=[SDS((split, T, N // split), BF16), SDS((T, D), BF16)],
        semantics=("parallel", "arbitrary"), args=(x, g, w), ride=ride)


BIG_ROW_TILE = 1024


def norm2_matmul(x, gains, weights, *, name, tm=BIG_ROW_TILE):
    T, D = x.shape
    tm = min(tm, T)
    n = len(gains)

    def body(x_ref, *refs):
        subs = _sub_tiles(tm)
        xhs = []
        for rows in subs:
            xf = x_ref[rows, :].astype(F32)
            xhs.append(xf * _rms_r(xf))
        for g_ref, w_ref, o_ref, xn_ref in zip(refs[:n], refs[n:2 * n], refs[2 * n::2], refs[2 * n + 1::2]):
            for rows, xh in zip(subs, xhs):
                xn = (xh * g_ref[...]).astype(BF16)
                xn_ref[rows, :] = xn
                o_ref[rows, :] = _dot(xn, w_ref[...]).astype(BF16)

    row = pl.BlockSpec((tm, D), lambda i: (i, 0))
    vec = pl.BlockSpec((1, D), lambda i: (0, 0))
    out_specs, out_shape = [], []
    for w in weights:
        out_specs += [pl.BlockSpec((tm, w.shape[1]), lambda i: (i, 0)), row]
        out_shape += [SDS((T, w.shape[1]), BF16), SDS((T, D), BF16)]
    return _call(
        body, name=name, grid=(T // tm,),
        in_specs=[row] + [vec] * n + [pl.BlockSpec(w.shape, lambda i: (0, 0)) for w in weights],
        out_specs=out_specs, out_shape=out_shape, semantics=("parallel",), args=[x] + list(gains) + list(weights))


def _shift_down(prev, cur, by):
    big = jnp.concatenate([prev, cur], axis=0)
    return pltpu.roll(big, by, 0)[prev.shape[0]:]


def _shift_up(cur, nxt, by):
    big = jnp.concatenate([cur, nxt], axis=0)
    return pltpu.roll(big, big.shape[0] - by, 0)[:cur.shape[0]]


def conv_mix_out(bcx, conv_w, w_out, g_post, res, *, name, ride=None, tm=ROW_TILE):
    T, D = res.shape
    hb = tm // BF16_ROWS

    def body(b_ref, c_ref, u_ref, cp_ref, up_ref, cw_ref, w_ref, g_ref, r_ref, h_ref, z_ref, y_ref):
        i = pl.program_id(0)
        cu = c_ref[...].astype(F32) * u_ref[...].astype(F32)
        cup = cp_ref[...].astype(F32) * up_ref[...].astype(F32)
        cup = jnp.where(i == 0, 0.0, cup)
        cv = (cw_ref[0:1, :] * _shift_down(cup, cu, 2) + cw_ref[1:2, :] * _shift_down(cup, cu, 1)
              + cw_ref[2:3, :] * cu)
        y = (b_ref[...].astype(F32) * cv).astype(BF16)
        y_ref[...] = y
        z = _dot(y, w_ref[...])
        z_ref[...] = z.astype(BF16)
        h_ref[...] = (r_ref[...] + z * _rms_r(z) * g_ref[...]).astype(STREAM)

    tile = lambda col: pl.BlockSpec((tm, D), lambda i: (i, col))
    halo = lambda col: pl.BlockSpec((BF16_ROWS, D), lambda i: (jnp.maximum(i * hb - 1, 0), col))
    row = pl.BlockSpec((tm, D), lambda i: (i, 0))
    return _call(
        body, name=name, grid=(T // tm,),
        in_specs=[tile(0), tile(1), tile(2), halo(1), halo(2),
                  pl.BlockSpec((3, D), lambda i: (0, 0)),
                  pl.BlockSpec((D, D), lambda i: (0, 0)),
                  pl.BlockSpec((1, D), lambda i: (0, 0)), row],
        out_specs=[row, row, row],
        out_shape=[SDS((T, D), STREAM), SDS((T, D), BF16), SDS((T, D), BF16)],
        semantics=("parallel",), args=(bcx, bcx, bcx, bcx, bcx, conv_w, w_out, g_post, res), ride=ride)


def _normbwd_then_nt(dh, zf, g_ref, w_ref, dz_ref, dg_ref, o_ref, first):
    dz, zh = _rmsnorm_bwd(zf, g_ref[...], dh)
    dz = dz.astype(BF16)
    dz_ref[...] = dz
    _accumulate(dg_ref, first, jnp.sum(dh * zh, axis=0, keepdims=True))
    o_ref[...] = _dot_nt(dz, w_ref[...]).astype(BF16)


def _then_specs(then, tm, T, D):
    z, g, w = then
    K = w.shape[0]
    row = pl.BlockSpec((tm, D), lambda i: (i, 0))
    vec = pl.BlockSpec((1, D), lambda i: (0, 0))
    in_specs = [row, vec, pl.BlockSpec((K, D), lambda i: (0, 0), pipeline_mode=pl.Buffered(1))]
    out_specs = [row, vec, pl.BlockSpec((tm, K), lambda i: (i, 0))]
    out_shape = [SDS((T, D), BF16), SDS((1, D), F32), SDS((T, K), BF16)]
    return in_specs, out_specs, out_shape


def plain_mix_out(a, w, g_post, res, *, name, target=None, ride=None, tm=ROW_TILE):
    T, D = res.shape
    tm = min(tm, T)
    K = a.shape[1]
    with_loss = target is not None

    def body(a_ref, w_ref, g_ref, r_ref, *rest):
        subs = _sub_tiles(tm)
        zs = [_dot(a_ref[rows, :], w_ref[...]) for rows in subs]
        if not with_loss:
            h_ref, z_ref = rest
            for rows, z in zip(subs, zs):
                h_ref[rows, :] = (r_ref[rows, :].astype(F32) + z * _rms_r(z) * g_ref[...]).astype(STREAM)
                z_ref[rows, :] = z.astype(BF16)
            return
        t_ref, h_ref, dz_ref, dg_ref, da_ref, loss_ref = rest
        first = pl.program_id(0) == 0
        loss, dg = jnp.zeros((), F32), jnp.zeros((1, D), F32)
        for rows, z in zip(subs, zs):
            diff = r_ref[rows, :].astype(F32) + z * _rms_r(z) * g_ref[...] - t_ref[rows, :]
            dh = diff * (1.0 / D)
            h_ref[rows, :] = dh.astype(STREAM)
            loss = loss + jnp.sum(diff * diff)
            dz, zh = _rmsnorm_bwd(z, g_ref[...], dh)
            dz = dz.astype(BF16)
            dz_ref[rows, :] = dz
            dg = dg + jnp.sum(dh * zh, axis=0, keepdims=True)
            da_ref[rows, :] = _dot_nt(dz, w_ref[...]).astype(BF16)
        _accumulate(loss_ref, first, jnp.full(loss_ref.shape, 0.5 / D, F32) * loss)
        _accumulate(dg_ref, first, dg)

    row = pl.BlockSpec((tm, D), lambda i: (i, 0))
    vec = pl.BlockSpec((1, D), lambda i: (0, 0))
    in_specs = [pl.BlockSpec((tm, K), lambda i: (i, 0)), pl.BlockSpec((K, D), lambda i: (0, 0)), vec, row]
    if with_loss:
        in_specs.append(row)
        out_specs = [row, row, vec, pl.BlockSpec((tm, K), lambda i: (i, 0)), pl.BlockSpec((8, 128), lambda i: (0, 0))]
        out_shape = [SDS((T, D), STREAM), SDS((T, D), BF16), SDS((1, D), F32), SDS((T, K), BF16), SDS((8, 128), F32)]
    else:
        out_specs, out_shape = [row, row], [SDS((T, D), STREAM), SDS((T, D), BF16)]
    return _call(
        body, name=name, grid=(T // tm,), in_specs=in_specs, out_specs=out_specs, out_shape=out_shape,
        semantics=("arbitrary",), args=(a, w, g_post, res) + ((target,) if with_loss else ()), ride=ride)


def _silu_grads(d, g, u):
    sg = jax.nn.sigmoid(g)
    return d * u * (sg * (1.0 + g * (1.0 - sg))), d * (g * sg)


def _sub_tiles(tm):
    return [pl.ds(k, min(MXU_WIDTH, tm)) for k in range(0, tm, MXU_WIDTH)]


def norm_swiglu_in(x, g, w, *, name, ride=None, tm=ROW_TILE):
    T, D = x.shape
    F = w.shape[1] // 2

    def body(x_ref, g_ref, wg_ref, wu_ref, gu_ref, a_ref, xt_ref):
        subs = _sub_tiles(tm)
        xns = []
        for rows in subs:
            xf = x_ref[rows, :].astype(F32)
            xns.append(xf * _rms_r(xf) * g_ref[...])
        xbs = [xn.astype(BF16) for xn in xns]
        gates = [_dot(xb, wg_ref[...]).astype(BF16) for xb in xbs]
        ups = [_dot(xb, wu_ref[...]).astype(BF16) for xb in xbs]
        for rows, gate, up in zip(subs, gates, ups):
            gu_ref[0, rows, :] = gate
            gu_ref[1, rows, :] = up
            a_ref[rows, :] = gate * jax.nn.sigmoid(gate) * up
        for rows, xn in zip(subs, xns):
            xt_ref[:, rows] = xn.T.astype(BF16)

    half = lambda s: pl.BlockSpec((D, F), lambda i: (0, s), pipeline_mode=pl.Buffered(1))
    return _call(
        body, name=name, grid=(T // tm,),
        in_specs=[pl.BlockSpec((tm, D), lambda i: (i, 0)), pl.BlockSpec((1, D), lambda i: (0, 0)), half(0), half(1)],
        out_specs=[pl.BlockSpec((2, tm, F), lambda i: (0, i, 0)), pl.BlockSpec((tm, F), lambda i: (i, 0)),
                   pl.BlockSpec((D, tm), lambda i: (0, i))],
        out_shape=[SDS((2, T, F), BF16), SDS((T, F), BF16), SDS((D, T), BF16)],
        semantics=("parallel",), args=(x, g, w, w), ride=ride)


def swiglu_bwd_tn(xt, dact, gu, *, name, ride=None, tb=MXU_WIDTH):
    D, T = xt.shape
    F = dact.shape[1]

    def body(xt_ref, d_ref, g_ref, u_ref, o_ref):
        dg, du = _silu_grads(d_ref[...], g_ref[...], u_ref[...])
        o_ref[0] = _dot(xt_ref[...], dg).astype(BF16)
        o_ref[1] = _dot(xt_ref[...], du).astype(BF16)

    col = lambda s: pl.BlockSpec((None, T, tb), lambda j: (s, 0, j))
    out = _call(
        body, name=name, grid=(F // tb,),
        in_specs=[pl.BlockSpec((D, T), lambda j: (0, 0), pipeline_mode=pl.Buffered(1)),
                  pl.BlockSpec((T, tb), lambda j: (0, j)), col(0), col(1)],
        out_specs=[pl.BlockSpec((2, D, tb), lambda j: (0, 0, j))],
        out_shape=[SDS((2, D, F), BF16)],
        semantics=("parallel",), args=(xt, dact, gu, gu), ride=ride)
    return out[0] if ride is None else (out[0][0], out[1])


def swiglu_bwd_in(dact, gu, w, h_in, g, dh_out, then, *, name, ride=None, tm=ROW_TILE):
    T, D = h_in.shape
    F = dact.shape[1]

    def body(d_ref, gg_ref, uu_ref, wg_ref, wu_ref, h_ref, g_ref, dh_ref, z_ref, g2_ref, w2_ref,
             o_ref, dg_ref, dz_ref, dg2_ref, da_ref):
        first = pl.program_id(0) == 0
        subs = _sub_tiles(tm)
        dns = []
        for rows in subs:
            dgate, dup = _silu_grads(d_ref[rows, :], gg_ref[rows, :], uu_ref[rows, :])
            dns.append(_dot_nt(dgate, wg_ref[...]) + _dot_nt(dup, wu_ref[...]))
        dg, dg2 = jnp.zeros((1, D), F32), jnp.zeros((1, D), F32)
        for rows, dn in zip(subs, dns):
            dx, hh = _rmsnorm_bwd(h_ref[rows, :].astype(F32), g_ref[...], dn)
            dh_in = dh_ref[rows, :] + dx
            o_ref[rows, :] = dh_in.astype(STREAM)
            dg = dg + jnp.sum(dn * hh, axis=0, keepdims=True)
            dz, zh = _rmsnorm_bwd(z_ref[rows, :].astype(F32), g2_ref[...], dh_in)
            dz = dz.astype(BF16)
            dz_ref[rows, :] = dz
            dg2 = dg2 + jnp.sum(dh_in * zh, axis=0, keepdims=True)
            da_ref[rows, :] = _dot_nt(dz, w2_ref[...]).astype(BF16)
        _accumulate(dg_ref, first, dg)
        _accumulate(dg2_ref, first, dg2)

    row = pl.BlockSpec((tm, D), lambda i: (i, 0))
    vec = pl.BlockSpec((1, D), lambda i: (0, 0))
    part = lambda s: pl.BlockSpec((None, tm, F), lambda i: (s, i, 0))
    half = lambda s: pl.BlockSpec((D, F), lambda i: (0, s), pipeline_mode=pl.Buffered(1))
    then_in, then_out, then_shape = _then_specs(then, tm, T, D)
    return _call(
        body, name=name, grid=(T // tm,),
        in_specs=[pl.BlockSpec((tm, F), lambda i: (i, 0)), part(0), part(1), half(0), half(1), row, vec, row] + then_in,
        out_specs=[row, vec] + then_out,
        out_shape=[SDS((T, D), STREAM), SDS((1, D), F32)] + then_shape,
        semantics=("arbitrary",), args=(dact, gu, gu, w, w, h_in, g, dh_out) + tuple(then), ride=ride)


def rope_tables(T):
    half = ROT_DIM // 2
    inv_freq = ROPE_THETA ** (-jnp.arange(0, ROT_DIM, 2, dtype=F32) / ROT_DIM)
    ang = (jnp.arange(T, dtype=F32)[:, None] * inv_freq[None, :]).T
    cos, sin = jnp.cos(ang), jnp.sin(ang)
    rest = HEAD_DIM - ROT_DIM
    one, zero = jnp.ones((rest, T), F32), jnp.zeros((rest, T), F32)
    zh = jnp.zeros((half, T), F32)
    fac = jnp.concatenate([cos, cos, one], axis=0)
    up = jnp.concatenate([-sin, zh, zero], axis=0)
    down = jnp.concatenate([zh, sin, zero], axis=0)
    return jnp.stack([fac, up, down])


def _rope(t, tab):
    half = ROT_DIM // 2
    return t * tab[0] + pltpu.roll(t, HEAD_DIM - half, 0) * tab[1] + pltpu.roll(t, half, 0) * tab[2]


def _rope_t(d, tab):
    half = ROT_DIM // 2
    return d * tab[0] + pltpu.roll(d * tab[1], half, 0) + pltpu.roll(d * tab[2], HEAD_DIM - half, 0)


def _head(t, h):
    return t[h * HEAD_DIM:(h + 1) * HEAD_DIM]


def _band(n, group):
    kj = lax.broadcasted_iota(jnp.int32, (2 * BLOCK, BLOCK), 0)
    qi = lax.broadcasted_iota(jnp.int32, (2 * BLOCK, BLOCK), 1)
    mask = (kj > qi) & (kj <= qi + BLOCK) & ((n > 0) | (kj >= BLOCK))
    return jnp.tile(mask, (1, group))


def _attn_specs(D, kvd, nb):
    cur = lambda n: jnp.minimum(n, nb - 1)
    prev = lambda n: jnp.maximum(cur(n) - 1, 0)
    return [pl.BlockSpec((BLOCK, D), lambda n: (cur(n), 0)),
            pl.BlockSpec((BLOCK, kvd), lambda n: (prev(n), 0)),
            pl.BlockSpec((BLOCK, kvd), lambda n: (cur(n), 0)),
            pl.BlockSpec((BLOCK, kvd), lambda n: (prev(n), 1)),
            pl.BlockSpec((BLOCK, kvd), lambda n: (cur(n), 1)),
            pl.BlockSpec((3, HEAD_DIM, BLOCK), lambda n: (0, 0, prev(n))),
            pl.BlockSpec((3, HEAD_DIM, BLOCK), lambda n: (0, 0, cur(n))),
            pl.BlockSpec(memory_space=pltpu.SMEM)]


def _attn_operands(q_ref, kp_ref, k_ref, vp_ref, v_ref, tp_ref, t_ref):
    flip = lambda ref: ref[...].astype(F32).T
    tab = t_ref[...]
    kt = jnp.concatenate([flip(kp_ref), flip(k_ref)], axis=1)
    vt = jnp.concatenate([flip(vp_ref), flip(v_ref)], axis=1)
    return flip(q_ref), kt, vt, tab, jnp.concatenate([tp_ref[...], tab], axis=2)


SCORE_SCALE = 1.0 / math.sqrt(HEAD_DIM)
HEADS_TOGETHER = 4


def _group_heads(t, first, count, tab=None):
    heads = [_head(t, first + g) for g in range(count)]
    if tab is not None:
        heads = [_rope(h, tab) * SCORE_SCALE for h in heads]
    return jnp.concatenate(heads, axis=1).astype(BF16)


def _sink_row(s_ref, first, count):
    which = lax.broadcasted_iota(jnp.int32, (1, count * BLOCK), 1) // BLOCK
    row = jnp.zeros((1, count * BLOCK), F32)
    for g in range(count):
        row = jnp.where(which == g, s_ref[0, first + g], row)
    return row


def _sum_keys(t):
    return _dot(jnp.ones((8, t.shape[0]), BF16), t)[0:1]


def _softmax(scores, sink, mask):
    s = jnp.where(mask, scores.astype(BF16), NEG)
    m = jnp.maximum(jnp.max(s, axis=0, keepdims=True).astype(F32), sink).astype(BF16)
    e = jnp.exp(s - m)
    m = m.astype(F32)
    return e, m, 1.0 / (_sum_keys(e) + jnp.exp(sink - m))


def _per_head(row, count):
    return [row[:, g * BLOCK:(g + 1) * BLOCK] for g in range(count)]


def attention_fwd(q, kv, tabs, sinks, *, name, ride=None):
    T, D = q.shape
    kvd = kv.shape[1] // 2
    heads = D // HEAD_DIM
    group = heads // N_KV_HEADS

    def body(q_ref, kp_ref, k_ref, vp_ref, v_ref, tp_ref, t_ref, s_ref, o_ref, stat_ref):
        gs = HEADS_TOGETHER
        mask = _band(pl.program_id(0), gs)
        qt, kt, vt, tab, tab2 = _attn_operands(q_ref, kp_ref, k_ref, vp_ref, v_ref, tp_ref, t_ref)
        firsts = [(j, first) for j in range(N_KV_HEADS) for first in range(j * group, (j + 1) * group, gs)]
        ks = [_rope(_head(kt, j), tab2).astype(BF16) for j in range(N_KV_HEADS)]
        scores = [_dot_tn(ks[j], _group_heads(qt, first, gs, tab)) for j, first in firsts]
        soft = [_softmax(s, _sink_row(s_ref, first, gs), mask) for s, (j, first) in zip(scores, firsts)]
        outs, ms, invs = [], [], []
        for (e, m, inv), (j, first) in zip(soft, firsts):
            o = _dot(_head(vt, j).astype(BF16), e) * inv
            outs += [o[:, g * BLOCK:(g + 1) * BLOCK] for g in range(gs)]
            ms += _per_head(m, gs)
            invs += _per_head(inv, gs)
        o_ref[...] = jnp.concatenate(outs, axis=0).T.astype(BF16)
        stat_ref[0] = jnp.concatenate(ms, axis=0)
        stat_ref[1] = jnp.concatenate(invs, axis=0)

    return _call(
        body, name=name, grid=(T // BLOCK,),
        in_specs=_attn_specs(D, kvd, T // BLOCK),
        out_specs=[pl.BlockSpec((BLOCK, D), lambda n: (n, 0)), pl.BlockSpec((2, heads, BLOCK), lambda n: (0, 0, n))],
        out_shape=[SDS((T, D), BF16), SDS((2, heads, T), F32)],
        semantics=("parallel",), args=(q, kv, kv, kv, kv, tabs, tabs, sinks), ride=ride)


def attention_bwd(q, kv, tabs, sinks, do, o, stats, *, name, ride=None):
    T, D = q.shape
    kvd = kv.shape[1] // 2
    heads = D // HEAD_DIM
    group = heads // N_KV_HEADS
    nb = T // BLOCK

    def body(q_ref, kp_ref, k_ref, vp_ref, v_ref, tp_ref, t_ref, s_ref, do_ref, o_ref, stat_ref,
             dq_ref, dkv_ref, ds_ref, carry):
        n = pl.program_id(0)

        @pl.when(n == 0)
        def _():
            carry[...] = jnp.zeros_like(carry)

        @pl.when(n < nb)
        def _():
            block(n, q_ref, kp_ref, k_ref, vp_ref, v_ref, tp_ref, t_ref, s_ref, do_ref, o_ref, stat_ref,
                  dq_ref, dkv_ref, ds_ref, carry)

        @pl.when(n == nb)
        def _():
            dkv_ref[...] = carry[...].astype(BF16)

    def block(n, q_ref, kp_ref, k_ref, vp_ref, v_ref, tp_ref, t_ref, s_ref, do_ref, o_ref, stat_ref,
              dq_ref, dkv_ref, ds_ref, carry):
        gs = HEADS_TOGETHER
        mask = _band(n, gs)
        qt, kt, vt, tab, tab2 = _attn_operands(q_ref, kp_ref, k_ref, vp_ref, v_ref, tp_ref, t_ref)
        dot = do_ref[...].astype(F32).T
        odo = o_ref[...].astype(F32).T * dot
        dl_all = jnp.concatenate([jnp.sum(_head(odo, h), axis=0, keepdims=True) for h in range(heads)], axis=0)
        m_all, inv_all = stat_ref[0], stat_ref[1]
        row = lambda t, first: jnp.concatenate([t[first + g:first + g + 1] for g in range(gs)], axis=1)
        lane = lax.broadcasted_iota(jnp.int32, (8, 128), 1)
        dsink = jnp.zeros((8, 128), F32)
        firsts = [(j, first) for j in range(N_KV_HEADS) for first in range(j * group, (j + 1) * group, gs)]
        ks = [_rope(_head(kt, j), tab2).astype(BF16) for j in range(N_KV_HEADS)]
        vs = [_head(vt, j).astype(BF16) for j in range(N_KV_HEADS)]
        qs = [_group_heads(qt, first, gs, tab) for _, first in firsts]
        dos = [_group_heads(dot, first, gs) for _, first in firsts]
        scores = [_dot_tn(ks[j], q) for q, (j, _) in zip(qs, firsts)]
        dps = [_dot_tn(vs[j], do) for do, (j, _) in zip(dos, firsts)]
        ps, dscs = [], []
        for s, dp, (j, first) in zip(scores, dps, firsts):
            m, inv, dl = row(m_all, first), row(inv_all, first), row(dl_all, first)
            e = jnp.exp(jnp.where(mask, s.astype(BF16), NEG) - m.astype(BF16))
            p = e * inv.astype(BF16)
            dscs.append(p * (dp.astype(BF16) - dl.astype(BF16)))
            ps.append(p)
            weight = jnp.exp(_sink_row(s_ref, first, gs) - m) * inv * dl
            for g in range(gs):
                dsink = dsink - jnp.where(lane == first + g, jnp.sum(weight[:, g * BLOCK:(g + 1) * BLOCK]), 0.0)
        dqs = []
        dks = [jnp.zeros((HEAD_DIM, 2 * BLOCK), F32) for _ in range(N_KV_HEADS)]
        dvs = [jnp.zeros((HEAD_DIM, 2 * BLOCK), F32) for _ in range(N_KV_HEADS)]
        for p, dsc, q, do, (j, _) in zip(ps, dscs, qs, dos, firsts):
            dq = _dot(ks[j], dsc) * SCORE_SCALE
            dqs += [_rope_t(dq[:, g * BLOCK:(g + 1) * BLOCK], tab) for g in range(gs)]
            dks[j] = dks[j] + _dot_nt(q, dsc)
            dvs[j] = dvs[j] + _dot_nt(do, p)
        dks = [_rope_t(dk, tab2) for dk in dks]
        dq_ref[...] = jnp.concatenate(dqs, axis=0).T.astype(BF16)
        dkv = jnp.concatenate(dks + dvs, axis=0)
        dkv_ref[...] = (carry[...] + dkv[:, :BLOCK].T).astype(BF16)
        carry[...] = dkv[:, BLOCK:].T
        _accumulate(ds_ref, n == 0, dsink)

    cur = lambda n: jnp.minimum(n, nb - 1)
    blk = lambda w: pl.BlockSpec((BLOCK, w), lambda n: (cur(n), 0))
    return _call(
        body, name=name, grid=(nb + 1,),
        in_specs=_attn_specs(D, kvd, nb) + [blk(D), blk(D), pl.BlockSpec((2, heads, BLOCK), lambda n: (0, 0, cur(n)))],
        out_specs=[blk(D), pl.BlockSpec((BLOCK, 2 * kvd), lambda n: (jnp.maximum(n - 1, 0), 0)),
                   pl.BlockSpec((8, 128), lambda n: (0, 0))],
        out_shape=[SDS((T, D), BF16), SDS((T, 2 * kvd), BF16), SDS((8, 128), F32)],
        scratch_shapes=[pltpu.VMEM((BLOCK, 2 * kvd), F32)],
        semantics=("arbitrary",), args=(q, kv, kv, kv, kv, tabs, tabs, sinks, do, o, stats), ride=ride)


def matmul_nt_normbwd(da, w, h_in, g, dh_out, *, name, ride=None, tm=ROW_TILE):
    T, D = h_in.shape
    S, _, K = da.shape

    def body(*refs):
        da_refs, w_refs = refs[:S], refs[S:2 * S]
        h_ref, g_ref, dh_ref, o_ref, dg_ref = refs[2 * S:]
        subs = _sub_tiles(tm)
        dns = []
        for rows in subs:
            dn = _dot_nt(da_refs[0][rows, :], w_refs[0][...])
            for s in range(1, S):
                dn = dn + _dot_nt(da_refs[s][rows, :], w_refs[s][...])
            dns.append(dn)
        dg = jnp.zeros((1, D), F32)
        for rows, dn in zip(subs, dns):
            dx, hh = _rmsnorm_bwd(h_ref[rows, :].astype(F32), g_ref[...], dn)
            o_ref[rows, :] = dh_ref[rows, :] + dx
            dg = dg + jnp.sum(dn * hh, axis=0, keepdims=True)
        _accumulate(dg_ref, pl.program_id(0) == 0, dg)

    row = pl.BlockSpec((tm, D), lambda i: (i, 0))
    vec = pl.BlockSpec((1, D), lambda i: (0, 0))
    part = lambda s: pl.BlockSpec((None, tm, K), lambda i: (s, i, 0))
    cols = lambda s: pl.BlockSpec((D, K), lambda i: (0, s), pipeline_mode=pl.Buffered(1))
    return _call(
        body, name=name, grid=(T // tm,),
        in_specs=[part(s) for s in range(S)] + [cols(s) for s in range(S)] + [row, vec, row],
        out_specs=[row, vec],
        out_shape=[SDS((T, D), F32), SDS((1, D), F32)],
        semantics=("arbitrary",), args=[da] * S + [w] * S + [h_in, g, dh_out], ride=ride)


def matmuls_nt_normbwd(das, ws, h_in, gs, dh_out, then, *, name, ride=None, tm=ROW_TILE):
    T, D = h_in.shape
    tm = min(tm, T)
    n = len(das)

    def body(*refs):
        da_refs, w_refs, g_refs = refs[:n], refs[n:2 * n], refs[2 * n:3 * n]
        h_ref, dh_ref, z_ref, g2_ref, w2_ref, o_ref = refs[3 * n:3 * n + 6]
        dg_refs, (dz_ref, dg2_ref, da_ref) = refs[3 * n + 6:4 * n + 6], refs[4 * n + 6:]
        first = pl.program_id(0) == 0
        subs = _sub_tiles(tm)
        dns = [[_dot_nt(da_ref_[rows, :], w_ref[...]) for da_ref_, w_ref in zip(da_refs, w_refs)] for rows in subs]
        dgs, dg2 = [jnp.zeros((1, D), F32) for _ in range(n)], jnp.zeros((1, D), F32)
        for rows, dn_sub in zip(subs, dns):
            hf = h_ref[rows, :].astype(F32)
            r = _rms_r(hf)
            hh = hf * r
            total = dh_ref[rows, :].astype(F32)
            for b, (dn, g_ref) in enumerate(zip(dn_sub, g_refs)):
                gd = g_ref[...] * dn
                total = total + r * (gd - hh * jnp.mean(hh * gd, axis=-1, keepdims=True))
                dgs[b] = dgs[b] + jnp.sum(dn * hh, axis=0, keepdims=True)
            o_ref[rows, :] = total.astype(STREAM)
            dz, zh = _rmsnorm_bwd(z_ref[rows, :].astype(F32), g2_ref[...], total)
            dz = dz.astype(BF16)
            dz_ref[rows, :] = dz
            dg2 = dg2 + jnp.sum(total * zh, axis=0, keepdims=True)
            da_ref[rows, :] = _dot_nt(dz, w2_ref[...]).astype(BF16)
        for dg_ref, dg in zip(dg_refs + (dg2_ref,), dgs + [dg2]):
            _accumulate(dg_ref, first, dg)

    row = pl.BlockSpec((tm, D), lambda i: (i, 0))
    vec = pl.BlockSpec((1, D), lambda i: (0, 0))
    then_in, then_out, then_shape = _then_specs(then, tm, T, D)
    return _call(
        body, name=name, grid=(T // tm,),
        in_specs=[pl.BlockSpec((tm, da.shape[1]), lambda i: (i, 0)) for da in das]
        + [pl.BlockSpec(w.shape, lambda i: (0, 0)) for w in ws] + [vec] * n + [row, row] + then_in,
        out_specs=[row] + [vec] * n + then_out,
        out_shape=[SDS((T, D), STREAM)] + [SDS((1, D), F32)] * n + then_shape,
        semantics=("arbitrary",), args=list(das) + list(ws) + list(gs) + [h_in, dh_out] + list(then), ride=ride)


def matmul_tn(a, b, *, tb, name, ride=None, ta=MXU_WIDTH):
    T, Ka = a.shape
    S, _, Nb = b.shape
    per = Nb // tb

    def body(a_ref, b_ref, o_ref):
        o_ref[...] = _dot_tn(a_ref[...], b_ref[...]).astype(BF16)

    out = _call(
        body, name=name, grid=(S * per, Ka // ta),
        in_specs=[pl.BlockSpec((T, ta), lambda j, i: (0, i)),
                  pl.BlockSpec((None, T, tb), lambda j, i: (j // per, 0, j % per))],
        out_specs=[pl.BlockSpec((ta, tb), lambda j, i: (i, j))],
        out_shape=[SDS((Ka, S * Nb), BF16)],
        semantics=("parallel", "parallel"), args=(a, b), ride=ride)
    return out[0] if ride is None else (out[0][0], out[1])


def conv_bwd(dy, bcx, conv_w, *, name, ride=None, tm=ROW_TILE):
    T, D = dy.shape
    nt = T // tm
    hb = tm // BF16_ROWS
    last = T // BF16_ROWS - 1

    def body(dy_ref, dyn_ref, b_ref, bn_ref, c_ref, u_ref, cp_ref, up_ref, cw_ref, o_ref, dw_ref):
        i = pl.program_id(0)
        c, u = c_ref[...].astype(F32), u_ref[...].astype(F32)
        cu = c * u
        cup = jnp.where(i == 0, 0.0, cp_ref[...].astype(F32) * up_ref[...].astype(F32))
        cu1, cu2 = _shift_down(cup, cu, 1), _shift_down(cup, cu, 2)
        w0, w1, w2 = cw_ref[0:1, :], cw_ref[1:2, :], cw_ref[2:3, :]
        dyf = dy_ref[...].astype(F32)
        o_ref[:, 0:D] = (dyf * (w0 * cu2 + w1 * cu1 + w2 * cu)).astype(BF16)
        dcv = dyf * b_ref[...].astype(F32)
        dcvn = jnp.where(i == nt - 1, 0.0, dyn_ref[...].astype(F32) * bn_ref[...].astype(F32))
        dcu = w2 * dcv + w1 * _shift_up(dcv, dcvn, 1) + w0 * _shift_up(dcv, dcvn, 2)
        o_ref[:, D:2 * D] = (dcu * u).astype(BF16)
        o_ref[:, 2 * D:3 * D] = (dcu * c).astype(BF16)
        row = lax.broadcasted_iota(jnp.int32, (8, D), 0)
        dw = jnp.zeros((8, D), F32)
        for tap, t in enumerate((cu2, cu1, cu)):
            dw = jnp.where(row == tap, jnp.sum(dcv * t, axis=0, keepdims=True), dw)
        _accumulate(dw_ref, i == 0, dw)

    tile = lambda col: pl.BlockSpec((tm, D), lambda i: (i, col))
    prev = lambda col: pl.BlockSpec((BF16_ROWS, D), lambda i: (jnp.maximum(i * hb - 1, 0), col))
    nxt = lambda col: pl.BlockSpec((BF16_ROWS, D), lambda i: (jnp.minimum((i + 1) * hb, last), col))
    return _call(
        body, name=name, grid=(nt,),
        in_specs=[tile(0), nxt(0), tile(0), nxt(0), tile(1), tile(2), prev(1), prev(2),
                  pl.BlockSpec((3, D), lambda i: (0, 0))],
        out_specs=[pl.BlockSpec((tm, 3 * D), lambda i: (i, 0)), pl.BlockSpec((8, D), lambda i: (0, 0))],
        out_shape=[SDS((T, 3 * D), BF16), SDS((8, D), F32)],
        semantics=("arbitrary",), args=(dy, dy, bcx, bcx, bcx, bcx, bcx, bcx, conv_w), ride=ride)


class NoTraffic:
    def ride(self, kernel_name):
        return None

    def landed(self, kernel_name, results, wts):
        pass

    def grad(self, key, value):
        pass


def local_step(x, target, wts, vec, traffic):
    T, D = x.shape
    tabs = rope_tables(T)
    small = {}

    def run(builder, *args, name, **kw):
        ride = traffic.ride(name)
        if ride is None:
            return builder(*args, name=name, **kw)
        out, extra = builder(*args, name=name, ride=ride, **kw)
        traffic.landed(name, extra, wts)
        return out

    bcx, xn1 = run(norm_matmul, x, vec["a_pre"], wts["w_in"], tn=3 * D, split=1, name="a_in")
    bcx = bcx[0]
    h1, z0, y0 = run(conv_mix_out, bcx, vec["conv_w"], wts["w_out"], vec["a_post"], x, name="a_out")
    gu0, act0, xt2 = run(norm_swiglu_in, h1, vec["ffn_pre0"], wts["gu0"], name="ffn0_in")
    h2, z1 = run(plain_mix_out, act0, wts["wd0"], vec["ffn_post0"], h1, name="ffn0_out")
    kvp, xkv, qp, xq = norm2_matmul(h2, [vec["kv_norm"], vec["b_pre"]], [wts["w_kv"], wts["w_q"]], name="kvq_in")
    attn, attn_stats = run(attention_fwd, qp, kvp, tabs, vec["sinks"], name="attn_fwd")
    h3, z2 = plain_mix_out(attn, wts["w_o"], vec["b_post"], h2, name="attn_out", tm=BIG_ROW_TILE)
    gu1, act1, xt3 = run(norm_swiglu_in, h3, vec["ffn_pre1"], wts["gu1"], name="ffn1_in")
    dy, dz3, small["ffn_post1"], dact1, loss = plain_mix_out(act1, wts["wd1"], vec["ffn_post1"], h3, name="ffn1_out",
                                                             target=target)

    def ffn_bwd(layer, dz, dact, gu, act, xt, h_in, dh, then, gu_first):
        tag = "ffn%d" % layer
        dwd = lambda: traffic.grad("wd%d" % layer, run(matmul_tn, act, dz[None], tb=D, name=tag + "_dwd"))
        dwgu = lambda: traffic.grad("gu%d" % layer, run(swiglu_bwd_tn, xt, dact, gu, name=tag + "_dwgu"))
        for step in ((dwgu, dwd) if gu_first else (dwd, dwgu)):
            step()
        dh_in, small["ffn_pre%d" % layer], dz_, dg_, da_ = run(
            swiglu_bwd_in, dact, gu, wts["gu%d" % layer], h_in, vec["ffn_pre%d" % layer], dh, then,
            name=tag + "_in_bwd")
        return dh_in, dz_, dg_, da_

    dh3, dz2, small["b_post"], dattn = ffn_bwd(1, dz3, dact1, gu1, act1, xt3, h3, dy,
                                               (z2, vec["b_post"], wts["w_o"]), gu_first=False)
    traffic.grad("w_o", matmul_tn(attn, dz2[None], tb=D, name="attn_dwo"))
    dq, dkv, small["sinks"] = run(attention_bwd, qp, kvp, tabs, vec["sinks"], dattn, attn, attn_stats,
                                  name="attn_bwd")
    traffic.grad("w_q", matmul_tn(xq, dq[None], tb=D, name="attn_dwq"))
    traffic.grad("w_kv", matmul_tn(xkv, dkv[None], tb=dkv.shape[1], name="attn_dwkv"))
    dh2, small["b_pre"], small["kv_norm"], dz1, small["ffn_post0"], dact0 = run(
        matmuls_nt_normbwd, [dq, dkv], [wts["w_q"], wts["w_kv"]], h2, [vec["b_pre"], vec["kv_norm"]], dh3,
        (z1, vec["ffn_post0"], wts["wd0"]), name="qkv_in_bwd")
    dh1, dz0, small["a_post"], dyc = ffn_bwd(0, dz1, dact0, gu0, act0, xt2, h1, dh2,
                                             (z0, vec["a_post"], wts["w_out"]), gu_first=True)
    traffic.grad("w_out", run(matmul_tn, y0, dz0[None], tb=D, name="a_dwout"))
    dbcx, small["conv_w"] = run(conv_bwd, dyc, bcx, vec["conv_w"], name="a_conv_bwd")
    traffic.grad("w_in", matmul_tn(xn1, dbcx[None], tb=3 * D // 2, name="a_dwin"))
    dx, small["a_pre"] = run(matmul_nt_normbwd, dbcx[None], wts["w_in"], x, vec["a_pre"], dh1, name="a_in_bwd")
    return loss, dx, small


SMALL_ROWS = 16
LOSS_ROW = 13

WHOLE = None
GATHER_PLAN = {"cast_rest": [("w_in", WHOLE)],
               "a_in": [("w_out", WHOLE), ("gu0", (0, 18))],
               "a_out": [("gu0", (18, 14))],
               "ffn0_in": [("wd0", WHOLE), ("w_kv", WHOLE), ("w_q", WHOLE), ("w_o", WHOLE)],
               "ffn0_out": [("gu1", (0, 16))],
               "attn_fwd": [("gu1", (16, 16))],
               "ffn1_in": [("wd1", WHOLE)]}
PAIR_PLAN = {"ffn1_dwgu": ["wd1"], "ffn1_in_bwd": ["gu1"], "attn_bwd": ["w_o"], "qkv_in_bwd": ["w_q", "w_kv"],
             "ffn0_dwd": ["gu0"], "ffn0_in_bwd": ["wd0"], "a_conv_bwd": ["w_out"]}
PAIR_ALONE = ["w_in"]
CHIP_PLAN = {"ffn1_in_bwd": [("wd1", WHOLE)], "attn_bwd": [("gu1", WHOLE)],
             "ffn0_dwgu": [("w_o", WHOLE), ("w_q", WHOLE), ("w_kv", WHOLE)],
             "ffn0_in_bwd": [("gu0", WHOLE)], "a_dwout": [("wd0", (0, 8))], "a_conv_bwd": [("wd0", (8, 14))],
             "a_in_bwd": [("w_out", WHOLE), ("w_in", WHOLE)]}
HALF_PLAN = {"a_in_bwd": ["gu0", "gu1", "wd0", "wd1", "w_kv", "w_q", "w_o"]}
GRAD_KIND = dict(KIND, gu0="split", gu1="split")


class Traffic:
    def __init__(self, wholes, quarter, c_arr, pc_arr):
        self.wholes, self.quarter, self.c_arr, self.pc_arr = wholes, quarter, c_arr, pc_arr
        self.views, self.sums, self.got = {}, {}, {}
        self.reduced = {}
        self.stages = {}

    def reduce(self, keys, name):
        return chip_reduce([self.sums[k] for k in keys], [self.got[k] for k in keys], [GRAD_KIND[k] for k in keys],
                           self.pc_arr, name=name)

    def ride(self, name, small=None):
        rides, stages = [], []
        if name in GATHER_PLAN:
            plan = GATHER_PLAN[name]
            rides.append(gather_ride([self.wholes[k] for k, _ in plan],
                                     [(KIND[k], self.quarter[k], part) for k, part in plan], small))
            stages.append(("gather", [k for k, _ in plan]))
        if name in CHIP_PLAN:
            plan = CHIP_PLAN[name]
            rides.append(chip_ride([self.sums[k] for k, _ in plan],
                                   [(GRAD_KIND[k], self.quarter[k], part) for k, part in plan],
                                   earlier=[self.got.get(k) for k, _ in plan]))
            stages.append(("chip", [k for k, _ in plan]))
        if name in PAIR_PLAN:
            keys = PAIR_PLAN[name]
            rides.append(pair_ride([self.views[k] for k in keys]))
            stages.append(("pair", keys))
        if name in HALF_PLAN:
            keys = HALF_PLAN[name]
            rides.append(half_ride(self.reduce(keys, "chip_reduce_early")))
            stages.append(("half", keys))
        self.stages[name] = stages
        return join(rides)

    def landed(self, name, results, wts):
        results = list(results)
        for stage, keys in self.stages[name]:
            mine, results = results[:len(keys)], results[len(keys):]
            if stage == "gather":
                for k, whole in zip(keys, mine):
                    self.wholes[k] = wts[k] = whole
            elif stage == "chip":
                self.got.update(zip(keys, mine))
            elif stage == "half":
                self.reduced.update(zip(keys, mine))
            else:
                for k, got in zip(keys, mine):
                    self.sums[k] = pair_add(self.views[k], got, self.c_arr, name="pair_add_" + k)

    def grad(self, key, value):
        r, ws = self.quarter[key]
        view = {"row": (N_CHIPS, 2, r // 2, ws), "col": (1, 2, r // 2, N_CHIPS * ws), "split": (2, 2, r // 2, 2 * ws)}
        self.views[key] = value.reshape(view[GRAD_KIND[key]])
        if key in PAIR_ALONE:
            (got,) = alone(pair_ride([self.views[key]]), name="pair_exchange_" + key)
            self.sums[key] = pair_add(self.views[key], got, self.c_arr, name="pair_add_" + key)


def kernel(x, a_pre_norm, a_w_in, a_conv_w, a_w_out, a_post_norm, ffn_pre_norm, ffn_w_gate_up, ffn_w_down, ffn_post_norm, kv_norm, w_kv, b_pre_norm, b_w_q, b_sinks, b_w_o, b_post_norm, loss_target, m_a_pre_norm, m_a_w_in, m_a_conv_w, m_a_w_out, m_a_post_norm, m_ffn_pre_norm, m_ffn_w_gate_up, m_ffn_w_down, m_ffn_post_norm, m_kv_norm, m_w_kv, m_b_pre_norm, m_b_w_q, m_b_sinks, m_b_w_o, m_b_post_norm, v_a_pre_norm, v_a_w_in, v_a_conv_w, v_a_w_out, v_a_post_norm, v_ffn_pre_norm, v_ffn_w_gate_up, v_ffn_w_down, v_ffn_post_norm, v_kv_norm, v_w_kv, v_b_pre_norm, v_b_w_q, v_b_sinks, v_b_w_o, v_b_post_norm):
    T, D = x.shape[1], x.shape[2]
    xi, yi, ci = _place()
    p = 2 * xi + yi
    p_arr = jnp.reshape(p, (1,)).astype(jnp.int32)
    c_arr = jnp.reshape(ci, (1,)).astype(jnp.int32)
    pc_arr = jnp.stack([p, ci]).astype(jnp.int32)
    me_arr = jnp.reshape(4 * xi + 2 * yi + ci, (1,)).astype(jnp.int32)
    qd = D // N_CHIPS

    big = {"w_in": (a_w_in, 0), "w_out": (a_w_out, 0), "gu0": (ffn_w_gate_up, 0), "gu1": (ffn_w_gate_up, 1),
           "wd0": (ffn_w_down, 0), "wd1": (ffn_w_down, 1), "w_kv": (w_kv[None], 0), "w_q": (b_w_q, 0),
           "w_o": (b_w_o, 0)}
    names = list(big)
    quarter = {k: w.shape[1:] for k, (w, _) in big.items()}
    source = lambda k: big[k] + (KIND[k],)
    traffic = Traffic(dict(zip(names[:1], cast_quarters([source(names[0])], p_arr, name="cast_first"))), quarter,
                      c_arr, pc_arr)
    small_shard = jnp.concatenate([a_pre_norm, a_post_norm, a_conv_w[0], jnp.zeros((3, qd), F32)], axis=0)
    wts = {}
    rest, (*landed, small_full) = cast_quarters([source(k) for k in names[1:]], p_arr, name="cast_rest",
                                                ride=traffic.ride("cast_rest", small_shard))
    traffic.wholes.update(zip(names[1:], rest))
    traffic.landed("cast_rest", landed, wts)
    rows = lambda k: jnp.transpose(small_full[:, k], (1, 0, 2)).reshape(-1, D)
    vec = {"a_pre": rows(slice(0, 1)), "a_post": rows(slice(1, 2)), "conv_w": rows(slice(2, 5)),
           "ffn_pre0": ffn_pre_norm[0:1], "ffn_pre1": ffn_pre_norm[1:2],
           "ffn_post0": ffn_post_norm[0:1], "ffn_post1": ffn_post_norm[1:2],
           "kv_norm": kv_norm[None], "b_pre": b_pre_norm, "b_post": b_post_norm, "sinks": b_sinks}

    loss, dx, small = local_step(x[0], loss_target[0], wts, vec, traffic)

    pad = lambda a: jnp.pad(a, ((0, 0), (0, D - a.shape[1])))
    small_block = jnp.concatenate(
        [small["a_pre"], small["a_post"], small["conv_w"][0:3], small["ffn_pre0"], small["ffn_pre1"],
         small["ffn_post0"], small["ffn_post1"], small["kv_norm"], small["b_pre"], small["b_post"],
         pad(small["sinks"][0:1]), pad(loss[0:1]), jnp.zeros((SMALL_ROWS - LOSS_ROW - 1, D), F32)], axis=0)
    late = [k for k in names if k not in traffic.reduced]
    *swapped, small_blocks = alone(join([half_ride(traffic.reduce(late, "chip_reduce_late")),
                                         chip_ride([], [], small_block)]), name="last_exchange")
    traffic.reduced.update(zip(late, swapped))
    grad = {k: traffic.reduced[k].reshape(quarter[k]) for k in names}
    small_sum = small_reduce(small_blocks, me_arr)

    out = {}
    out["a_w_in"] = adamw(a_w_in, [grad["w_in"]], m_a_w_in, v_a_w_in, name="adamw_a_w_in")
    out["a_w_out"] = adamw(a_w_out, [grad["w_out"]], m_a_w_out, v_a_w_out, name="adamw_a_w_out")
    out["ffn_w_gate_up"] = adamw(ffn_w_gate_up, [grad["gu0"], grad["gu1"]], m_ffn_w_gate_up, v_ffn_w_gate_up,
                                 name="adamw_ffn_w_gate_up")
    out["ffn_w_down"] = adamw(ffn_w_down, [grad["wd0"], grad["wd1"]], m_ffn_w_down, v_ffn_w_down,
                              name="adamw_ffn_w_down")
    out["w_kv"] = [o[0] for o in adamw(w_kv[None], [grad["w_kv"]], m_w_kv[None], v_w_kv[None], name="adamw_w_kv")]
    out["b_w_q"] = adamw(b_w_q, [grad["w_q"]], m_b_w_q, v_b_w_q, name="adamw_b_w_q")
    out["b_w_o"] = adamw(b_w_o, [grad["w_o"]], m_b_w_o, v_b_w_o, name="adamw_b_w_o")

    def pack(a_pre, a_post, conv, ffn_pre, ffn_post, kvn, b_pre, b_post, sinks):
        return jnp.concatenate([pad(a_pre), pad(a_post), pad(conv[0]), ffn_pre, ffn_post, kvn[None], b_pre, b_post,
                                pad(sinks), jnp.zeros((SMALL_ROWS - 13, D), F32)], axis=0)

    g_small = jnp.concatenate([pad(lax.dynamic_slice(small_sum, (0, p * qd), (5, qd))), small_sum[5:]], axis=0)
    w_small = pack(a_pre_norm, a_post_norm, a_conv_w, ffn_pre_norm, ffn_post_norm, kv_norm, b_pre_norm, b_post_norm,
                   b_sinks)
    m_small = pack(m_a_pre_norm, m_a_post_norm, m_a_conv_w, m_ffn_pre_norm, m_ffn_post_norm, m_kv_norm,
                   m_b_pre_norm, m_b_post_norm, m_b_sinks)
    v_small = pack(v_a_pre_norm, v_a_post_norm, v_a_conv_w, v_ffn_pre_norm, v_ffn_post_norm, v_kv_norm,
                   v_b_pre_norm, v_b_post_norm, v_b_sinks)
    packed = adamw(w_small[None], [g_small], m_small[None], v_small[None], name="adamw_small")
    ns = b_sinks.shape[1]
    unpack = lambda a: {"a_pre_norm": a[0:1, :qd], "a_post_norm": a[1:2, :qd], "a_conv_w": a[None, 2:5, :qd],
                        "ffn_pre_norm": a[5:7], "ffn_post_norm": a[7:9], "kv_norm": a[9], "b_pre_norm": a[10:11],
                        "b_post_norm": a[11:12], "b_sinks": a[12:13, :ns]}
    unpacked = [unpack(a[0]) for a in packed]
    for k in unpacked[0]:
        out[k] = [u[k] for u in unpacked]

    order = ["a_pre_norm", "a_w_in", "a_conv_w", "a_w_out", "a_post_norm", "ffn_pre_norm", "ffn_w_gate_up",
             "ffn_w_down", "ffn_post_norm", "kv_norm", "w_kv", "b_pre_norm", "b_w_q", "b_sinks", "b_w_o",
             "b_post_norm"]
    return (small_sum[LOSS_ROW, 0], dx[None], *[out[k][0] for k in order], *[out[k][1] for k in order],
            *[out[k][2] for k in order], *[out[k][3] for k in order])
```

```python
import math

import jax
import jax.numpy as jnp
from jax import lax
from jax.experimental import pallas as pl
from jax.experimental.pallas import tpu as pltpu

F32 = jnp.float32
BF16 = jnp.bfloat16
SDS = jax.ShapeDtypeStruct
MESH = pl.DeviceIdType.MESH
DMA = pltpu.SemaphoreType.DMA
HBM_SPEC = pl.BlockSpec(memory_space=pltpu.HBM)

EPS = 1e-6
NEG = -1e30
HEAD_DIM = 64
N_KV_HEADS = 4
BLOCK = 128
ROT_DIM = HEAD_DIM // 4
ROPE_THETA = 500000.0
N_CHIPS = 4

ADAM_LR = 0.001
ADAM_B1 = 0.9
ADAM_B2 = 0.999
ADAM_EPS = 1e-08
ADAM_WD = 0.01
ADAM_STEP = 10

VMEM_LIMIT_BYTES = 52 * 1024 * 1024
ROW_TILE = 512
BF16_ROWS = 16
STREAM = BF16
MXU_WIDTH = 256

KIND = {"w_in": "col", "gu0": "col", "gu1": "col", "w_out": "row", "wd0": "row", "wd1": "row", "w_kv": "row",
        "w_q": "row", "w_o": "row"}


def _params(*semantics):
    return pltpu.CompilerParams(dimension_semantics=semantics, vmem_limit_bytes=VMEM_LIMIT_BYTES)


def _row_tile(rows, limit, step=8):
    return max(t for t in range(step, limit + 1, step) if rows % t == 0)


def _place():
    return lax.axis_index("x"), lax.axis_index("y"), lax.axis_index("c")


def _other_chips(x, y):
    return [(1 - x, y), (x, 1 - y), (1 - x, 1 - y)]


def _remote(src, dst, send_sem, recv_sem, to):
    return pltpu.make_async_remote_copy(src_ref=src, dst_ref=dst, send_sem=send_sem, recv_sem=recv_sem,
                                        device_id=to, device_id_type=MESH)


def _full_shape(kind, quarter):
    r, ws = quarter
    return (N_CHIPS * r, ws) if kind == "row" else (r, N_CHIPS * ws)


def _rows_of(h, part):
    lo, n = (0, h) if part is None else (part[0] * BF16_ROWS, part[1] * BF16_ROWS)
    assert lo + n <= h, (h, part)
    return lo, n


def _half_of_quarter(ref, kind, quarter, part, q, half):
    r, ws = quarter
    h = r // 2
    lo, n = _rows_of(h, part)
    if kind == "row":
        return ref.at[pl.ds(pl.multiple_of(q * r + half * h + lo, BF16_ROWS), n)]
    return ref.at[pl.ds(pl.multiple_of(half * h + lo, BF16_ROWS), n), pl.ds(pl.multiple_of(q * ws, 128), ws)]


class Ride:
    def __init__(self, operands, out_shape, aliases, sems, make):
        self.operands, self.out_shape, self.aliases, self.sems, self.make = operands, out_shape, aliases, sems, make


def join(rides):
    rides = [r for r in rides if r is not None]
    if len(rides) < 2:
        return rides[0] if rides else None
    aliases, at = {}, [0, 0, 0]
    cuts = []
    for r in rides:
        aliases.update({at[0] + i: at[1] + o for i, o in r.aliases.items()})
        cuts.append(tuple(at))
        at = [at[0] + len(r.operands), at[1] + len(r.out_shape), at[2] + len(r.sems)]
    cuts.append(tuple(at))

    def make(ins, outs, sem):
        made = [r.make(ins[lo[0]:hi[0]], outs[lo[1]:hi[1]], sem[lo[2]:hi[2]]) for r, lo, hi in zip(rides, cuts, cuts[1:])]

        def start():
            for s, _ in made:
                s()

        def finish():
            for _, f in made:
                f()

        return start, finish

    return Ride(sum((list(r.operands) for r in rides), []), sum((list(r.out_shape) for r in rides), []), aliases,
                sum((list(r.sems) for r in rides), []), make)


def _call(body, *, name, grid, in_specs, out_specs, out_shape, args, scratch_shapes=(), semantics=None, ride=None,
          prefetch=None):
    pre = 0 if prefetch is None else 1
    n_in, n_out, n_scr = len(in_specs), len(out_specs), len(scratch_shapes)
    r_in, r_out = (len(ride.operands), len(ride.out_shape)) if ride is not None else (0, 0)
    a, b = pre + n_in, pre + n_in + r_in
    c, d = b + n_out, b + n_out + r_out
    e = d + n_scr

    def riding(*refs):
        start, finish = ride.make(refs[a:b], refs[c:d], refs[e:])
        ids = [pl.program_id(k) for k in range(len(grid))]
        first, last = ids[0] == 0, ids[0] == grid[0] - 1
        for k in range(1, len(grid)):
            first, last = first & (ids[k] == 0), last & (ids[k] == grid[k] - 1)
        pl.when(first)(start)
        body(*refs[:a], *refs[b:c], *refs[d:e])
        pl.when(last)(finish)

    if ride is None:
        kernel_body, extra_in, extra_out, extra_shape, extra_scr, aliases = body, [], [], [], [], {}
        params = _params(*semantics)
    else:
        kernel_body, extra_in, extra_out = riding, [HBM_SPEC] * r_in, [HBM_SPEC] * r_out
        extra_shape, extra_scr = list(ride.out_shape), list(ride.sems)
        aliases = {pre + n_in + i: n_out + o for i, o in ride.aliases.items()}
        params = _params(*(("arbitrary",) * len(grid)))
    specs = dict(grid=grid, in_specs=list(in_specs) + extra_in, out_specs=list(out_specs) + extra_out,
                 scratch_shapes=list(scratch_shapes) + extra_scr)
    if prefetch is not None:
        specs = dict(grid_spec=pltpu.PrefetchScalarGridSpec(num_scalar_prefetch=1, **specs))
        args = (prefetch,) + tuple(args)
    outs = pl.pallas_call(kernel_body, name=name, out_shape=list(out_shape) + extra_shape,
                          input_output_aliases=aliases, compiler_params=params, **specs,
                          )(*args, *(ride.operands if ride is not None else ()))
    return outs if ride is None else (outs[:n_out], outs[n_out:])


def alone(ride, *, name):
    def body(*refs):
        n = len(ride.operands)
        start, finish = ride.make(refs[:n], refs[n:n + len(ride.out_shape)], refs[n + len(ride.out_shape):])
        start()
        finish()

    return pl.pallas_call(
        body, name=name, in_specs=[HBM_SPEC] * len(ride.operands), out_specs=[HBM_SPEC] * len(ride.out_shape),
        out_shape=list(ride.out_shape), input_output_aliases=dict(ride.aliases), scratch_shapes=list(ride.sems),
    )(*ride.operands)


def gather_ride(wholes, metas, small=None):
    n = len(wholes)
    operands, out_shape = list(wholes), [SDS(s.shape, s.dtype) for s in wholes]
    sems = [DMA((n, 3)), DMA((n, 3)), DMA((n, 3)), DMA((n, 3))]
    if small is not None:
        operands.append(small)
        out_shape.append(SDS((N_CHIPS,) + small.shape, small.dtype))
        sems += [DMA((3,)), DMA((3,)), DMA(())]

    def make(ins, outs, sem):
        send1, recv1, send2, recv2 = sem[:4]
        x, y, c = _place()
        p = 2 * x + y
        chips = _other_chips(x, y)
        me, sibling = (x, y, c), (x, y, 1 - c)
        part = lambda t, q, half: _half_of_quarter(outs[t], *metas[t], q, half)
        first = []
        for j, (qx, qy) in enumerate(chips):
            if small is not None:
                first.append(_remote(ins[n], outs[n].at[p], sem[4].at[j], sem[5].at[j], (qx, qy, c)))
            for t in range(n):
                first.append(_remote(part(t, p, c), part(t, p, c), send1.at[t, j], recv1.at[t, j], (qx, qy, c)))
        local = [] if small is None else [pltpu.make_async_copy(ins[n], outs[n].at[p], sem[6])]

        def start():
            for cp in local + first:
                cp.start()

        def finish():
            passed = []
            for j, (qx, qy) in enumerate(chips):
                q = 2 * qx + qy
                for t in range(n):
                    landed = part(t, q, c)
                    _remote(landed, landed, send1.at[t, j], recv1.at[t, j], me).wait_recv()
                    cp = _remote(landed, landed, send2.at[t, j], recv2.at[t, j], sibling)
                    cp.start()
                    passed.append(cp)
            for j, (qx, qy) in enumerate(chips):
                q = 2 * qx + qy
                if small is not None:
                    _remote(outs[n].at[q], outs[n].at[q], sem[4].at[j], sem[5].at[j], me).wait_recv()
                for t in range(n):
                    theirs = part(t, q, 1 - c)
                    _remote(theirs, theirs, send2.at[t, j], recv2.at[t, j], me).wait_recv()
            for cp in first + passed:
                cp.wait_send()
            for cp in local:
                cp.wait()

        return start, finish

    return Ride(operands, out_shape, {t: t for t in range(n)}, sems, make)


def chip_ride(sums, metas, small=None, earlier=None):
    n = len(sums)
    operands = list(sums)
    out_shape = [SDS((3, s.shape[1], quarter[1]), s.dtype) for s, (_, quarter, _) in zip(sums, metas)]
    sems = [DMA((n, 3)), DMA((n, 3))] if n else []
    if small is not None:
        operands.append(small)
        out_shape.append(SDS((8,) + small.shape, small.dtype))
        sems += [DMA((7,)), DMA((7,)), DMA(())]
    aliases = {}
    for t, buffer in enumerate(earlier or [None] * n):
        if buffer is not None:
            aliases[len(operands)] = t
            operands.append(buffer)

    def make(ins, outs, sem):
        x, y, c = _place()
        cps = []
        for j, (qx, qy) in enumerate(_other_chips(x, y)):
            q = 2 * qx + qy
            for t in range(n):
                kind, (_, ws), part = metas[t]
                rows = pl.ds(*_rows_of(ins[t].shape[1], part))
                if kind == "row":
                    src = ins[t].at[q, rows]
                elif kind == "col":
                    src = ins[t].at[0, rows, pl.ds(pl.multiple_of(q * ws, 128), ws)]
                else:
                    src = ins[t].at[q // 2, rows, pl.ds(pl.multiple_of((q % 2) * ws, 128), ws)]
                cps.append(_remote(src, outs[t].at[j, rows], sem[0].at[t, j], sem[1].at[t, j], (qx, qy, c)))
        local = []
        if small is not None:
            ssend, srecv, lsem = sem[2 * bool(n):2 * bool(n) + 3]
            local.append(pltpu.make_async_copy(ins[n], outs[n].at[0], lsem))
            for k in range(1, 8):
                peer = (x ^ (k >> 2 & 1), y ^ (k >> 1 & 1), c ^ (k & 1))
                cps.append(_remote(ins[n], outs[n].at[k], ssend.at[k - 1], srecv.at[k - 1], peer))

        def start():
            for cp in local + cps:
                cp.start()

        def finish():
            for cp in cps + local:
                cp.wait()

        return start, finish

    return Ride(operands, out_shape, aliases, sems, make)


def pair_ride(grads):
    n = len(grads)

    def make(ins, outs, sem):
        x, y, c = _place()
        cps = [_remote(ins[t].at[:, 1 - c], outs[t], sem[0].at[t], sem[1].at[t], (x, y, 1 - c)) for t in range(n)]

        def start():
            for cp in cps:
                cp.start()

        def finish():
            for cp in cps:
                cp.wait()

        return start, finish

    return Ride(list(grads), [SDS((g.shape[0],) + g.shape[2:], g.dtype) for g in grads], {}, [DMA((n,)), DMA((n,))],
                make)


def half_ride(quarters):
    n = len(quarters)

    def make(ins, outs, sem):
        x, y, c = _place()
        sends = [_remote(outs[t].at[c], outs[t].at[c], sem[0].at[t], sem[1].at[t], (x, y, 1 - c)) for t in range(n)]

        def start():
            for cp in sends:
                cp.start()

        def finish():
            for t in range(n):
                theirs = outs[t].at[1 - c]
                _remote(theirs, theirs, sem[0].at[t], sem[1].at[t], (x, y, c)).wait_recv()
            for cp in sends:
                cp.wait_send()

        return start, finish

    return Ride(list(quarters), [SDS(q.shape, q.dtype) for q in quarters], {t: t for t in range(n)},
                [DMA((n,)), DMA((n,))], make)


CAST_STEPS = 4


def cast_quarters(sources, p_arr, *, name, ride=None):
    n = len(sources)
    in_specs, out_specs, out_shape = [], [], []
    for w, layer, kind in sources:
        _, r, ws = w.shape
        tr = r // CAST_STEPS
        assert tr % BF16_ROWS == 0, w.shape
        in_specs.append(pl.BlockSpec((None, tr, ws), lambda i, p_ref, layer=layer: (layer, i, 0)))
        out_specs.append(pl.BlockSpec((tr, ws), (lambda i, p_ref: (p_ref[0] * CAST_STEPS + i, 0)) if kind == "row"
                                      else (lambda i, p_ref: (i, p_ref[0]))))
        out_shape.append(SDS(_full_shape(kind, (r, ws)), BF16))

    def body(p_ref, *refs):
        for w_ref, o_ref in zip(refs[:n], refs[n:]):
            o_ref[...] = w_ref[...].astype(BF16)

    return _call(body, name=name, grid=(CAST_STEPS,), in_specs=in_specs, out_specs=out_specs, out_shape=out_shape,
                 semantics=("parallel",), args=[w for w, _, _ in sources], ride=ride, prefetch=p_arr)


def pair_add(own, got, c_arr, *, name):
    A, _, h, W = own.shape
    th = _row_tile(h, max(BF16_ROWS, (3 << 19) // W), BF16_ROWS)

    def body(c_ref, a_ref, b_ref, o_ref):
        o_ref[...] = (a_ref[...].astype(F32) + b_ref[...].astype(F32)).astype(BF16)

    return pl.pallas_call(
        body, name=name,
        grid_spec=pltpu.PrefetchScalarGridSpec(
            num_scalar_prefetch=1, grid=(A, h // th),
            in_specs=[pl.BlockSpec((None, None, th, W), lambda q, i, c_ref: (q, c_ref[0], i, 0)),
                      pl.BlockSpec((None, th, W), lambda q, i, c_ref: (q, i, 0))],
            out_specs=pl.BlockSpec((None, th, W), lambda q, i, c_ref: (q, i, 0))),
        out_shape=SDS((A, h, W), BF16),
        compiler_params=_params("parallel", "parallel"),
    )(c_arr, own, got)


REDUCE_STEPS = 2


def chip_reduce(sums, got, kinds, pc_arr, *, name):
    n = len(sums)
    mine = {"row": lambda i, pc_ref: (pc_ref[0], i, 0), "col": lambda i, pc_ref: (0, i, pc_ref[0]),
            "split": lambda i, pc_ref: (pc_ref[0] // 2, i, pc_ref[0] % 2)}
    a_specs, b_specs, o_specs, out_shape = [], [], [], []
    for g, kind in zip(got, kinds):
        _, h, ws = g.shape
        th = h // REDUCE_STEPS
        assert th % BF16_ROWS == 0, g.shape
        a_specs.append(pl.BlockSpec((None, th, ws), mine[kind]))
        b_specs.append(pl.BlockSpec((3, th, ws), lambda i, pc_ref: (0, i, 0)))
        o_specs.append(pl.BlockSpec((None, th, ws), lambda i, pc_ref: (pc_ref[1], i, 0)))
        out_shape.append(SDS((2, h, ws), F32))

    def body(pc_ref, *refs):
        for a_ref, b_ref, o_ref in zip(refs[:n], refs[n:2 * n], refs[2 * n:]):
            o_ref[...] = ((a_ref[...].astype(F32) + b_ref[0].astype(F32)) + b_ref[1].astype(F32)) + b_ref[2].astype(F32)

    return _call(body, name=name, grid=(REDUCE_STEPS,), in_specs=a_specs + b_specs, out_specs=o_specs,
                 out_shape=out_shape, semantics=("parallel",), args=list(sums) + list(got), prefetch=pc_arr)


def small_reduce(blocks, me_arr):
    _, rows, D = blocks.shape

    def body(me_ref, b_ref, o_ref):
        me = me_ref[0]
        total = b_ref[me]
        for d in range(1, 8):
            total = total + b_ref[d ^ me]
        o_ref[...] = total

    return pl.pallas_call(
        body, name="small_reduce",
        grid_spec=pltpu.PrefetchScalarGridSpec(
            num_scalar_prefetch=1, grid=(1,),
            in_specs=[pl.BlockSpec((8, rows, D), lambda i, me_ref: (0, 0, 0))],
            out_specs=pl.BlockSpec((rows, D), lambda i, me_ref: (0, 0))),
        out_shape=SDS((rows, D), F32),
        compiler_params=_params("arbitrary"),
    )(me_arr, blocks)


def adamw(w, gs, m, v, *, name):
    L, r, cols = w.shape
    tr = _row_tile(r, 256)
    nt = r // tr

    def body(*refs):
        w_ref, m_ref, v_ref = refs[:3]
        g_refs = refs[3:3 + L]
        g_out, d_out, m_out, v_out = refs[3 + L:]
        layer = pl.program_id(0)
        g = g_refs[0][...]
        for l in range(1, L):
            g = jnp.where(layer == l, g_refs[l][...], g)
        m_new = ADAM_B1 * m_ref[...] + (1.0 - ADAM_B1) * g
        v_new = ADAM_B2 * v_ref[...] + (1.0 - ADAM_B2) * (g * g)
        m_hat = m_new / (1.0 - ADAM_B1 ** ADAM_STEP)
        v_hat = v_new / (1.0 - ADAM_B2 ** ADAM_STEP)
        g_out[...] = g
        m_out[...] = m_new
        v_out[...] = v_new
        d_out[...] = -ADAM_LR * (m_hat / (jnp.sqrt(v_hat) + ADAM_EPS) + ADAM_WD * w_ref[...])

    full = pl.BlockSpec((None, tr, cols), lambda l, i: (l, i, 0))
    g_spec = lambda l0: pl.BlockSpec((tr, cols), lambda l, i: (jnp.where(l == l0, i, jnp.where(l < l0, 0, nt - 1)), 0))
    return pl.pallas_call(
        body, name=name, grid=(L, nt),
        in_specs=[full, full, full] + [g_spec(l0) for l0 in range(L)],
        out_specs=[full] * 4,
        out_shape=[SDS(w.shape, F32)] * 4,
        compiler_params=_params("arbitrary", "arbitrary"),
    )(w, m, v, *gs)


def _rms_r(xf):
    return lax.rsqrt(jnp.mean(xf * xf, axis=-1, keepdims=True) + EPS)


def _rmsnorm_bwd(xf, g, dy):
    r = _rms_r(xf)
    xh = xf * r
    gd = g * dy
    return r * (gd - xh * jnp.mean(xh * gd, axis=-1, keepdims=True)), xh


def _dot(a, b):
    return jnp.dot(a, b, preferred_element_type=F32)


def _dot_nt(a, b):
    return lax.dot_general(a, b, (((1,), (1,)), ((), ())), preferred_element_type=F32)


def _dot_tn(a, b):
    return lax.dot_general(a, b, (((0,), (0,)), ((), ())), preferred_element_type=F32)


def _accumulate(ref, first, value):
    @pl.when(first)
    def _():
        ref[...] = value

    @pl.when(jnp.logical_not(first))
    def _():
        ref[...] += value


def norm_matmul(x, g, w, *, tn, split, name, ride=None, tm=ROW_TILE):
    T, D = x.shape
    N = w.shape[1]
    per = N // split // tn

    def body(x_ref, g_ref, w_ref, o_ref, xn_ref):
        @pl.when(pl.program_id(1) == 0)
        def _():
            xf = x_ref[...].astype(F32)
            xn_ref[...] = (xf * _rms_r(xf) * g_ref[...]).astype(BF16)

        o_ref[...] = _dot(xn_ref[...], w_ref[...]).astype(BF16)

    return _call(
        body, name=name, grid=(T // tm, N // tn),
        in_specs=[pl.BlockSpec((tm, D), lambda i, j: (i, 0)),
                  pl.BlockSpec((1, D), lambda i, j: (0, 0)),
                  pl.BlockSpec((D, tn), lambda i, j: (0, j))],
        out_specs=[pl.BlockSpec((None, tm, tn), lambda i, j: (j // per, i, j % per)),
                   pl.BlockSpec((tm, D), lambda i, j: (i, 0))],
        out_shape=[SDS((split, T, N // split), BF16), SDS((T, D), BF16)],
        semantics=("parallel", "arbitrary"), args=(x, g, w), ride=ride)


BIG_ROW_TILE = 1024


def norm2_matmul(x, gains, weights, *, name, tm=BIG_ROW_TILE):
    T, D = x.shape
    tm = min(tm, T)
    n = len(gains)

    def body(x_ref, *refs):
        subs = _sub_tiles(tm)
        xhs = []
        for rows in subs:
            xf = x_ref[rows, :].astype(F32)
            xhs.append(xf * _rms_r(xf))
        for g_ref, w_ref, o_ref, xn_ref in zip(refs[:n], refs[n:2 * n], refs[2 * n::2], refs[2 * n + 1::2]):
            for rows, xh in zip(subs, xhs):
                xn = (xh * g_ref[...]).astype(BF16)
                xn_ref[rows, :] = xn
                o_ref[rows, :] = _dot(xn, w_ref[...]).astype(BF16)

    row = pl.BlockSpec((tm, D), lambda i: (i, 0))
    vec = pl.BlockSpec((1, D), lambda i: (0, 0))
    out_specs, out_shape = [], []
    for w in weights:
        out_specs += [pl.BlockSpec((tm, w.shape[1]), lambda i: (i, 0)), row]
        out_shape += [SDS((T, w.shape[1]), BF16), SDS((T, D), BF16)]
    return _call(
        body, name=name, grid=(T // tm,),
        in_specs=[row] + [vec] * n + [pl.BlockSpec(w.shape, lambda i: (0, 0)) for w in weights],
        out_specs=out_specs, out_shape=out_shape, semantics=("parallel",), args=[x] + list(gains) + list(weights))


def _shift_down(prev, cur, by):
    big = jnp.concatenate([prev, cur], axis=0)
    return pltpu.roll(big, by, 0)[prev.shape[0]:]


def _shift_up(cur, nxt, by):
    big = jnp.concatenate([cur, nxt], axis=0)
    return pltpu.roll(big, big.shape[0] - by, 0)[:cur.shape[0]]


def conv_mix_out(bcx, conv_w, w_out, g_post, res, *, name, ride=None, tm=ROW_TILE):
    T, D = res.shape
    hb = tm // BF16_ROWS

    def body(b_ref, c_ref, u_ref, cp_ref, up_ref, cw_ref, w_ref, g_ref, r_ref, h_ref, z_ref, y_ref):
        i = pl.program_id(0)
        cu = c_ref[...].astype(F32) * u_ref[...].astype(F32)
        cup = cp_ref[...].astype(F32) * up_ref[...].astype(F32)
        cup = jnp.where(i == 0, 0.0, cup)
        cv = (cw_ref[0:1, :] * _shift_down(cup, cu, 2) + cw_ref[1:2, :] * _shift_down(cup, cu, 1)
              + cw_ref[2:3, :] * cu)
        y = (b_ref[...].astype(F32) * cv).astype(BF16)
        y_ref[...] = y
        z = _dot(y, w_ref[...])
        z_ref[...] = z.astype(BF16)
        h_ref[...] = (r_ref[...] + z * _rms_r(z) * g_ref[...]).astype(STREAM)

    tile = lambda col: pl.BlockSpec((tm, D), lambda i: (i, col))
    halo = lambda col: pl.BlockSpec((BF16_ROWS, D), lambda i: (jnp.maximum(i * hb - 1, 0), col))
    row = pl.BlockSpec((tm, D), lambda i: (i, 0))
    return _call(
        body, name=name, grid=(T // tm,),
        in_specs=[tile(0), tile(1), tile(2), halo(1), halo(2),
                  pl.BlockSpec((3, D), lambda i: (0, 0)),
                  pl.BlockSpec((D, D), lambda i: (0, 0)),
                  pl.BlockSpec((1, D), lambda i: (0, 0)), row],
        out_specs=[row, row, row],
        out_shape=[SDS((T, D), STREAM), SDS((T, D), BF16), SDS((T, D), BF16)],
        semantics=("parallel",), args=(bcx, bcx, bcx, bcx, bcx, conv_w, w_out, g_post, res), ride=ride)


def _normbwd_then_nt(dh, zf, g_ref, w_ref, dz_ref, dg_ref, o_ref, first):
    dz, zh = _rmsnorm_bwd(zf, g_ref[...], dh)
    dz = dz.astype(BF16)
    dz_ref[...] = dz
    _accumulate(dg_ref, first, jnp.sum(dh * zh, axis=0, keepdims=True))
    o_ref[...] = _dot_nt(dz, w_ref[...]).astype(BF16)


def _then_specs(then, tm, T, D):
    z, g, w = then
    K = w.shape[0]
    row = pl.BlockSpec((tm, D), lambda i: (i, 0))
    vec = pl.BlockSpec((1, D), lambda i: (0, 0))
    in_specs = [row, vec, pl.BlockSpec((K, D), lambda i: (0, 0), pipeline_mode=pl.Buffered(1))]
    out_specs = [row, vec, pl.BlockSpec((tm, K), lambda i: (i, 0))]
    out_shape = [SDS((T, D), BF16), SDS((1, D), F32), SDS((T, K), BF16)]
    return in_specs, out_specs, out_shape


def plain_mix_out(a, w, g_post, res, *, name, target=None, ride=None, tm=ROW_TILE):
    T, D = res.shape
    tm = min(tm, T)
    K = a.shape[1]
    with_loss = target is not None

    def body(a_ref, w_ref, g_ref, r_ref, *rest):
        subs = _sub_tiles(tm)
        zs = [_dot(a_ref[rows, :], w_ref[...]) for rows in subs]
        if not with_loss:
            h_ref, z_ref = rest
            for rows, z in zip(subs, zs):
                h_ref[rows, :] = (r_ref[rows, :].astype(F32) + z * _rms_r(z) * g_ref[...]).astype(STREAM)
                z_ref[rows, :] = z.astype(BF16)
            return
        t_ref, h_ref, dz_ref, dg_ref, da_ref, loss_ref = rest
        first = pl.program_id(0) == 0
        loss, dg = jnp.zeros((), F32), jnp.zeros((1, D), F32)
        for rows, z in zip(subs, zs):
            diff = r_ref[rows, :].astype(F32) + z * _rms_r(z) * g_ref[...] - t_ref[rows, :]
            dh = diff * (1.0 / D)
            h_ref[rows, :] = dh.astype(STREAM)
            loss = loss + jnp.sum(diff * diff)
            dz, zh = _rmsnorm_bwd(z, g_ref[...], dh)
            dz = dz.astype(BF16)
            dz_ref[rows, :] = dz
            dg = dg + jnp.sum(dh * zh, axis=0, keepdims=True)
            da_ref[rows, :] = _dot_nt(dz, w_ref[...]).astype(BF16)
        _accumulate(loss_ref, first, jnp.full(loss_ref.shape, 0.5 / D, F32) * loss)
        _accumulate(dg_ref, first, dg)

    row = pl.BlockSpec((tm, D), lambda i: (i, 0))
    vec = pl.BlockSpec((1, D), lambda i: (0, 0))
    in_specs = [pl.BlockSpec((tm, K), lambda i: (i, 0)), pl.BlockSpec((K, D), lambda i: (0, 0)), vec, row]
    if with_loss:
        in_specs.append(row)
        out_specs = [row, row, vec, pl.BlockSpec((tm, K), lambda i: (i, 0)), pl.BlockSpec((8, 128), lambda i: (0, 0))]
        out_shape = [SDS((T, D), STREAM), SDS((T, D), BF16), SDS((1, D), F32), SDS((T, K), BF16), SDS((8, 128), F32)]
    else:
        out_specs, out_shape = [row, row], [SDS((T, D), STREAM), SDS((T, D), BF16)]
    return _call(
        body, name=name, grid=(T // tm,), in_specs=in_specs, out_specs=out_specs, out_shape=out_shape,
        semantics=("arbitrary",), args=(a, w, g_post, res) + ((target,) if with_loss else ()), ride=ride)


def _silu_grads(d, g, u):
    sg = jax.nn.sigmoid(g)
    return d * u * (sg * (1.0 + g * (1.0 - sg))), d * (g * sg)


def _sub_tiles(tm):
    return [pl.ds(k, min(MXU_WIDTH, tm)) for k in range(0, tm, MXU_WIDTH)]


def norm_swiglu_in(x, g, w, *, name, ride=None, tm=ROW_TILE):
    T, D = x.shape
    F = w.shape[1] // 2

    def body(x_ref, g_ref, wg_ref, wu_ref, gu_ref, a_ref, xt_ref):
        subs = _sub_tiles(tm)
        xns = []
        for rows in subs:
            xf = x_ref[rows, :].astype(F32)
            xns.append(xf * _rms_r(xf) * g_ref[...])
        xbs = [xn.astype(BF16) for xn in xns]
        gates = [_dot(xb, wg_ref[...]).astype(BF16) for xb in xbs]
        ups = [_dot(xb, wu_ref[...]).astype(BF16) for xb in xbs]
        for rows, gate, up in zip(subs, gates, ups):
            gu_ref[0, rows, :] = gate
            gu_ref[1, rows, :] = up
            a_ref[rows, :] = gate * jax.nn.sigmoid(gate) * up
        for rows, xn in zip(subs, xns):
            xt_ref[:, rows] = xn.T.astype(BF16)

    half = lambda s: pl.BlockSpec((D, F), lambda i: (0, s), pipeline_mode=pl.Buffered(1))
    return _call(
        body, name=name, grid=(T // tm,),
        in_specs=[pl.BlockSpec((tm, D), lambda i: (i, 0)), pl.BlockSpec((1, D), lambda i: (0, 0)), half(0), half(1)],
        out_specs=[pl.BlockSpec((2, tm, F), lambda i: (0, i, 0)), pl.BlockSpec((tm, F), lambda i: (i, 0)),
                   pl.BlockSpec((D, tm), lambda i: (0, i))],
        out_shape=[SDS((2, T, F), BF16), SDS((T, F), BF16), SDS((D, T), BF16)],
        semantics=("parallel",), args=(x, g, w, w), ride=ride)


def swiglu_bwd_tn(xt, dact, gu, *, name, ride=None, tb=MXU_WIDTH):
    D, T = xt.shape
    F = dact.shape[1]

    def body(xt_ref, d_ref, g_ref, u_ref, o_ref):
        dg, du = _silu_grads(d_ref[...], g_ref[...], u_ref[...])
        o_ref[0] = _dot(xt_ref[...], dg).astype(BF16)
        o_ref[1] = _dot(xt_ref[...], du).astype(BF16)

    col = lambda s: pl.BlockSpec((None, T, tb), lambda j: (s, 0, j))
    out = _call(
        body, name=name, grid=(F // tb,),
        in_specs=[pl.BlockSpec((D, T), lambda j: (0, 0), pipeline_mode=pl.Buffered(1)),
                  pl.BlockSpec((T, tb), lambda j: (0, j)), col(0), col(1)],
        out_specs=[pl.BlockSpec((2, D, tb), lambda j: (0, 0, j))],
        out_shape=[SDS((2, D, F), BF16)],
        semantics=("parallel",), args=(xt, dact, gu, gu), ride=ride)
    return out[0] if ride is None else (out[0][0], out[1])


def swiglu_bwd_in(dact, gu, w, h_in, g, dh_out, then, *, name, ride=None, tm=ROW_TILE):
    T, D = h_in.shape
    F = dact.shape[1]

    def body(d_ref, gg_ref, uu_ref, wg_ref, wu_ref, h_ref, g_ref, dh_ref, z_ref, g2_ref, w2_ref,
             o_ref, dg_ref, dz_ref, dg2_ref, da_ref):
        first = pl.program_id(0) == 0
        subs = _sub_tiles(tm)
        dns = []
        for rows in subs:
            dgate, dup = _silu_grads(d_ref[rows, :], gg_ref[rows, :], uu_ref[rows, :])
            dns.append(_dot_nt(dgate, wg_ref[...]) + _dot_nt(dup, wu_ref[...]))
        dg, dg2 = jnp.zeros((1, D), F32), jnp.zeros((1, D), F32)
        for rows, dn in zip(subs, dns):
            dx, hh = _rmsnorm_bwd(h_ref[rows, :].astype(F32), g_ref[...], dn)
            dh_in = dh_ref[rows, :] + dx
            o_ref[rows, :] = dh_in.astype(STREAM)
            dg = dg + jnp.sum(dn * hh, axis=0, keepdims=True)
            dz, zh = _rmsnorm_bwd(z_ref[rows, :].astype(F32), g2_ref[...], dh_in)
            dz = dz.astype(BF16)
            dz_ref[rows, :] = dz
            dg2 = dg2 + jnp.sum(dh_in * zh, axis=0, keepdims=True)
            da_ref[rows, :] = _dot_nt(dz, w2_ref[...]).astype(BF16)
        _accumulate(dg_ref, first, dg)
        _accumulate(dg2_ref, first, dg2)

    row = pl.BlockSpec((tm, D), lambda i: (i, 0))
    vec = pl.BlockSpec((1, D), lambda i: (0, 0))
    part = lambda s: pl.BlockSpec((None, tm, F), lambda i: (s, i, 0))
    half = lambda s: pl.BlockSpec((D, F), lambda i: (0, s), pipeline_mode=pl.Buffered(1))
    then_in, then_out, then_shape = _then_specs(then, tm, T, D)
    return _call(
        body, name=name, grid=(T // tm,),
        in_specs=[pl.BlockSpec((tm, F), lambda i: (i, 0)), part(0), part(1), half(0), half(1), row, vec, row] + then_in,
        out_specs=[row, vec] + then_out,
        out_shape=[SDS((T, D), STREAM), SDS((1, D), F32)] + then_shape,
        semantics=("arbitrary",), args=(dact, gu, gu, w, w, h_in, g, dh_out) + tuple(then), ride=ride)


def rope_tables(T):
    half = ROT_DIM // 2
    inv_freq = ROPE_THETA ** (-jnp.arange(0, ROT_DIM, 2, dtype=F32) / ROT_DIM)
    ang = (jnp.arange(T, dtype=F32)[:, None] * inv_freq[None, :]).T
    cos, sin = jnp.cos(ang), jnp.sin(ang)
    rest = HEAD_DIM - ROT_DIM
    one, zero = jnp.ones((rest, T), F32), jnp.zeros((rest, T), F32)
    zh = jnp.zeros((half, T), F32)
    fac = jnp.concatenate([cos, cos, one], axis=0)
    up = jnp.concatenate([-sin, zh, zero], axis=0)
    down = jnp.concatenate([zh, sin, zero], axis=0)
    return jnp.stack([fac, up, down])


def _rope(t, tab):
    half = ROT_DIM // 2
    return t * tab[0] + pltpu.roll(t, HEAD_DIM - half, 0) * tab[1] + pltpu.roll(t, half, 0) * tab[2]


def _rope_t(d, tab):
    half = ROT_DIM // 2
    return d * tab[0] + pltpu.roll(d * tab[1], half, 0) + pltpu.roll(d * tab[2], HEAD_DIM - half, 0)


def _head(t, h):
    return t[h * HEAD_DIM:(h + 1) * HEAD_DIM]


def _band(n, group):
    kj = lax.broadcasted_iota(jnp.int32, (2 * BLOCK, BLOCK), 0)
    qi = lax.broadcasted_iota(jnp.int32, (2 * BLOCK, BLOCK), 1)
    mask = (kj > qi) & (kj <= qi + BLOCK) & ((n > 0) | (kj >= BLOCK))
    return jnp.tile(mask, (1, group))


def _attn_specs(D, kvd, nb):
    cur = lambda n: jnp.minimum(n, nb - 1)
    prev = lambda n: jnp.maximum(cur(n) - 1, 0)
    return [pl.BlockSpec((BLOCK, D), lambda n: (cur(n), 0)),
            pl.BlockSpec((BLOCK, kvd), lambda n: (prev(n), 0)),
            pl.BlockSpec((BLOCK, kvd), lambda n: (cur(n), 0)),
            pl.BlockSpec((BLOCK, kvd), lambda n: (prev(n), 1)),
            pl.BlockSpec((BLOCK, kvd), lambda n: (cur(n), 1)),
            pl.BlockSpec((3, HEAD_DIM, BLOCK), lambda n: (0, 0, prev(n))),
            pl.BlockSpec((3, HEAD_DIM, BLOCK), lambda n: (0, 0, cur(n))),
            pl.BlockSpec(memory_space=pltpu.SMEM)]


def _attn_operands(q_ref, kp_ref, k_ref, vp_ref, v_ref, tp_ref, t_ref):
    flip = lambda ref: ref[...].astype(F32).T
    tab = t_ref[...]
    kt = jnp.concatenate([flip(kp_ref), flip(k_ref)], axis=1)
    vt = jnp.concatenate([flip(vp_ref), flip(v_ref)], axis=1)
    return flip(q_ref), kt, vt, tab, jnp.concatenate([tp_ref[...], tab], axis=2)


SCORE_SCALE = 1.0 / math.sqrt(HEAD_DIM)
HEADS_TOGETHER = 4


def _group_heads(t, first, count, tab=None):
    heads = [_head(t, first + g) for g in range(count)]
    if tab is not None:
        heads = [_rope(h, tab) * SCORE_SCALE for h in heads]
    return jnp.concatenate(heads, axis=1).astype(BF16)


def _sink_row(s_ref, first, count):
    which = lax.broadcasted_iota(jnp.int32, (1, count * BLOCK), 1) // BLOCK
    row = jnp.zeros((1, count * BLOCK), F32)
    for g in range(count):
        row = jnp.where(which == g, s_ref[0, first + g], row)
    return row


def _sum_keys(t):
    return _dot(jnp.ones((8, t.shape[0]), BF16), t)[0:1]


def _softmax(scores, sink, mask):
    s = jnp.where(mask, scores.astype(BF16), NEG)
    m = jnp.maximum(jnp.max(s, axis=0, keepdims=True).astype(F32), sink).astype(BF16)
    e = jnp.exp(s - m)
    m = m.astype(F32)
    return e, m, 1.0 / (_sum_keys(e) + jnp.exp(sink - m))


def _per_head(row, count):
    return [row[:, g * BLOCK:(g + 1) * BLOCK] for g in range(count)]


def attention_fwd(q, kv, tabs, sinks, *, name, ride=None):
    T, D = q.shape
    kvd = kv.shape[1] // 2
    heads = D // HEAD_DIM
    group = heads // N_KV_HEADS

    def body(q_ref, kp_ref, k_ref, vp_ref, v_ref, tp_ref, t_ref, s_ref, o_ref, stat_ref):
        gs = HEADS_TOGETHER
        mask = _band(pl.program_id(0), gs)
        qt, kt, vt, tab, tab2 = _attn_operands(q_ref, kp_ref, k_ref, vp_ref, v_ref, tp_ref, t_ref)
        firsts = [(j, first) for j in range(N_KV_HEADS) for first in range(j * group, (j + 1) * group, gs)]
        ks = [_rope(_head(kt, j), tab2).astype(BF16) for j in range(N_KV_HEADS)]
        scores = [_dot_tn(ks[j], _group_heads(qt, first, gs, tab)) for j, first in firsts]
        soft = [_softmax(s, _sink_row(s_ref, first, gs), mask) for s, (j, first) in zip(scores, firsts)]
        outs, ms, invs = [], [], []
        for (e, m, inv), (j, first) in zip(soft, firsts):
            o = _dot(_head(vt, j).astype(BF16), e) * inv
            outs += [o[:, g * BLOCK:(g + 1) * BLOCK] for g in range(gs)]
            ms += _per_head(m, gs)
            invs += _per_head(inv, gs)
        o_ref[...] = jnp.concatenate(outs, axis=0).T.astype(BF16)
        stat_ref[0] = jnp.concatenate(ms, axis=0)
        stat_ref[1] = jnp.concatenate(invs, axis=0)

    return _call(
        body, name=name, grid=(T // BLOCK,),
        in_specs=_attn_specs(D, kvd, T // BLOCK),
        out_specs=[pl.BlockSpec((BLOCK, D), lambda n: (n, 0)), pl.BlockSpec((2, heads, BLOCK), lambda n: (0, 0, n))],
        out_shape=[SDS((T, D), BF16), SDS((2, heads, T), F32)],
        semantics=("parallel",), args=(q, kv, kv, kv, kv, tabs, tabs, sinks), ride=ride)


def attention_bwd(q, kv, tabs, sinks, do, o, stats, *, name, ride=None):
    T, D = q.shape
    kvd = kv.shape[1] // 2
    heads = D // HEAD_DIM
    group = heads // N_KV_HEADS
    nb = T // BLOCK

    def body(q_ref, kp_ref, k_ref, vp_ref, v_ref, tp_ref, t_ref, s_ref, do_ref, o_ref, stat_ref,
             dq_ref, dkv_ref, ds_ref, carry):
        n = pl.program_id(0)

        @pl.when(n == 0)
        def _():
            carry[...] = jnp.zeros_like(carry)

        @pl.when(n < nb)
        def _():
            block(n, q_ref, kp_ref, k_ref, vp_ref, v_ref, tp_ref, t_ref, s_ref, do_ref, o_ref, stat_ref,
                  dq_ref, dkv_ref, ds_ref, carry)

        @pl.when(n == nb)
        def _():
            dkv_ref[...] = carry[...].astype(BF16)

    def block(n, q_ref, kp_ref, k_ref, vp_ref, v_ref, tp_ref, t_ref, s_ref, do_ref, o_ref, stat_ref,
              dq_ref, dkv_ref, ds_ref, carry):
        gs = HEADS_TOGETHER
        mask = _band(n, gs)
        qt, kt, vt, tab, tab2 = _attn_operands(q_ref, kp_ref, k_ref, vp_ref, v_ref, tp_ref, t_ref)
        dot = do_ref[...].astype(F32).T
        odo = o_ref[...].astype(F32).T * dot
        dl_all = jnp.concatenate([jnp.sum(_head(odo, h), axis=0, keepdims=True) for h in range(heads)], axis=0)
        m_all, inv_all = stat_ref[0], stat_ref[1]
        row = lambda t, first: jnp.concatenate([t[first + g:first + g + 1] for g in range(gs)], axis=1)
        lane = lax.broadcasted_iota(jnp.int32, (8, 128), 1)
        dsink = jnp.zeros((8, 128), F32)
        firsts = [(j, first) for j in range(N_KV_HEADS) for first in range(j * group, (j + 1) * group, gs)]
        ks = [_rope(_head(kt, j), tab2).astype(BF16) for j in range(N_KV_HEADS)]
        vs = [_head(vt, j).astype(BF16) for j in range(N_KV_HEADS)]
        qs = [_group_heads(qt, first, gs, tab) for _, first in firsts]
        dos = [_group_heads(dot, first, gs) for _, first in firsts]
        scores = [_dot_tn(ks[j], q) for q, (j, _) in zip(qs, firsts)]
        dps = [_dot_tn(vs[j], do) for do, (j, _) in zip(dos, firsts)]
        ps, dscs = [], []
        for s, dp, (j, first) in zip(scores, dps, firsts):
            m, inv, dl = row(m_all, first), row(inv_all, first), row(dl_all, first)
            e = jnp.exp(jnp.where(mask, s.astype(BF16), NEG) - m.astype(BF16))
            p = e * inv.astype(BF16)
            dscs.append(p * (dp.astype(BF16) - dl.astype(BF16)))
            ps.append(p)
            weight = jnp.exp(_sink_row(s_ref, first, gs) - m) * inv * dl
            for g in range(gs):
                dsink = dsink - jnp.where(lane == first + g, jnp.sum(weight[:, g * BLOCK:(g + 1) * BLOCK]), 0.0)
        dqs = []
        dks = [jnp.zeros((HEAD_DIM, 2 * BLOCK), F32) for _ in range(N_KV_HEADS)]
        dvs = [jnp.zeros((HEAD_DIM, 2 * BLOCK), F32) for _ in range(N_KV_HEADS)]
        for p, dsc, q, do, (j, _) in zip(ps, dscs, qs, dos, firsts):
            dq = _dot(ks[j], dsc) * SCORE_SCALE
            dqs += [_rope_t(dq[:, g * BLOCK:(g + 1) * BLOCK], tab) for g in range(gs)]
            dks[j] = dks[j] + _dot_nt(q, dsc)
            dvs[j] = dvs[j] + _dot_nt(do, p)
        dks = [_rope_t(dk, tab2) for dk in dks]
        dq_ref[...] = jnp.concatenate(dqs, axis=0).T.astype(BF16)
        dkv = jnp.concatenate(dks + dvs, axis=0)
        dkv_ref[...] = (carry[...] + dkv[:, :BLOCK].T).astype(BF16)
        carry[...] = dkv[:, BLOCK:].T
        _accumulate(ds_ref, n == 0, dsink)

    cur = lambda n: jnp.minimum(n, nb - 1)
    blk = lambda w: pl.BlockSpec((BLOCK, w), lambda n: (cur(n), 0))
    return _call(
        body, name=name, grid=(nb + 1,),
        in_specs=_attn_specs(D, kvd, nb) + [blk(D), blk(D), pl.BlockSpec((2, heads, BLOCK), lambda n: (0, 0, cur(n)))],
        out_specs=[blk(D), pl.BlockSpec((BLOCK, 2 * kvd), lambda n: (jnp.maximum(n - 1, 0), 0)),
                   pl.BlockSpec((8, 128), lambda n: (0, 0))],
        out_shape=[SDS((T, D), BF16), SDS((T, 2 * kvd), BF16), SDS((8, 128), F32)],
        scratch_shapes=[pltpu.VMEM((BLOCK, 2 * kvd), F32)],
        semantics=("arbitrary",), args=(q, kv, kv, kv, kv, tabs, tabs, sinks, do, o, stats), ride=ride)


def matmul_nt_normbwd(da, w, h_in, g, dh_out, *, name, ride=None, tm=ROW_TILE):
    T, D = h_in.shape
    S, _, K = da.shape

    def body(*refs):
        da_refs, w_refs = refs[:S], refs[S:2 * S]
        h_ref, g_ref, dh_ref, o_ref, dg_ref = refs[2 * S:]
        subs = _sub_tiles(tm)
        dns = []
        for rows in subs:
            dn = _dot_nt(da_refs[0][rows, :], w_refs[0][...])
            for s in range(1, S):
                dn = dn + _dot_nt(da_refs[s][rows, :], w_refs[s][...])
            dns.append(dn)
        dg = jnp.zeros((1, D), F32)
        for rows, dn in zip(subs, dns):
            dx, hh = _rmsnorm_bwd(h_ref[rows, :].astype(F32), g_ref[...], dn)
            o_ref[rows, :] = dh_ref[rows, :] + dx
            dg = dg + jnp.sum(dn * hh, axis=0, keepdims=True)
        _accumulate(dg_ref, pl.program_id(0) == 0, dg)

    row = pl.BlockSpec((tm, D), lambda i: (i, 0))
    vec = pl.BlockSpec((1, D), lambda i: (0, 0))
    part = lambda s: pl.BlockSpec((None, tm, K), lambda i: (s, i, 0))
    cols = lambda s: pl.BlockSpec((D, K), lambda i: (0, s), pipeline_mode=pl.Buffered(1))
    return _call(
        body, name=name, grid=(T // tm,),
        in_specs=[part(s) for s in range(S)] + [cols(s) for s in range(S)] + [row, vec, row],
        out_specs=[row, vec],
        out_shape=[SDS((T, D), F32), SDS((1, D), F32)],
        semantics=("arbitrary",), args=[da] * S + [w] * S + [h_in, g, dh_out], ride=ride)


def matmuls_nt_normbwd(das, ws, h_in, gs, dh_out, then, *, name, ride=None, tm=ROW_TILE):
    T, D = h_in.shape
    tm = min(tm, T)
    n = len(das)

    def body(*refs):
        da_refs, w_refs, g_refs = refs[:n], refs[n:2 * n], refs[2 * n:3 * n]
        h_ref, dh_ref, z_ref, g2_ref, w2_ref, o_ref = refs[3 * n:3 * n + 6]
        dg_refs, (dz_ref, dg2_ref, da_ref) = refs[3 * n + 6:4 * n + 6], refs[4 * n + 6:]
        first = pl.program_id(0) == 0
        subs = _sub_tiles(tm)
        dns = [[_dot_nt(da_ref_[rows, :], w_ref[...]) for da_ref_, w_ref in zip(da_refs, w_refs)] for rows in subs]
        dgs, dg2 = [jnp.zeros((1, D), F32) for _ in range(n)], jnp.zeros((1, D), F32)
        for rows, dn_sub in zip(subs, dns):
            hf = h_ref[rows, :].astype(F32)
            r = _rms_r(hf)
            hh = hf * r
            total = dh_ref[rows, :].astype(F32)
            for b, (dn, g_ref) in enumerate(zip(dn_sub, g_refs)):
                gd = g_ref[...] * dn
                total = total + r * (gd - hh * jnp.mean(hh * gd, axis=-1, keepdims=True))
                dgs[b] = dgs[b] + jnp.sum(dn * hh, axis=0, keepdims=True)
            o_ref[rows, :] = total.astype(STREAM)
            dz, zh = _rmsnorm_bwd(z_ref[rows, :].astype(F32), g2_ref[...], total)
            dz = dz.astype(BF16)
            dz_ref[rows, :] = dz
            dg2 = dg2 + jnp.sum(total * zh, axis=0, keepdims=True)
            da_ref[rows, :] = _dot_nt(dz, w2_ref[...]).astype(BF16)
        for dg_ref, dg in zip(dg_refs + (dg2_ref,), dgs + [dg2]):
            _accumulate(dg_ref, first, dg)

    row = pl.BlockSpec((tm, D), lambda i: (i, 0))
    vec = pl.BlockSpec((1, D), lambda i: (0, 0))
    then_in, then_out, then_shape = _then_specs(then, tm, T, D)
    return _call(
        body, name=name, grid=(T // tm,),
        in_specs=[pl.BlockSpec((tm, da.shape[1]), lambda i: (i, 0)) for da in das]
        + [pl.BlockSpec(w.shape, lambda i: (0, 0)) for w in ws] + [vec] * n + [row, row] + then_in,
        out_specs=[row] + [vec] * n + then_out,
        out_shape=[SDS((T, D), STREAM)] + [SDS((1, D), F32)] * n + then_shape,
        semantics=("arbitrary",), args=list(das) + list(ws) + list(gs) + [h_in, dh_out] + list(then), ride=ride)


def matmul_tn(a, b, *, tb, name, ride=None, ta=MXU_WIDTH):
    T, Ka = a.shape
    S, _, Nb = b.shape
    per = Nb // tb

    def body(a_ref, b_ref, o_ref):
        o_ref[...] = _dot_tn(a_ref[...], b_ref[...]).astype(BF16)

    out = _call(
        body, name=name, grid=(S * per, Ka // ta),
        in_specs=[pl.BlockSpec((T, ta), lambda j, i: (0, i)),
                  pl.BlockSpec((None, T, tb), lambda j, i: (j // per, 0, j % per))],
        out_specs=[pl.BlockSpec((ta, tb), lambda j, i: (i, j))],
        out_shape=[SDS((Ka, S * Nb), BF16)],
        semantics=("parallel", "parallel"), args=(a, b), ride=ride)
    return out[0] if ride is None else (out[0][0], out[1])


def conv_bwd(dy, bcx, conv_w, *, name, ride=None, tm=ROW_TILE):
    T, D = dy.shape
    nt = T // tm
    hb = tm // BF16_ROWS
    last = T // BF16_ROWS - 1

    def body(dy_ref, dyn_ref, b_ref, bn_ref, c_ref, u_ref, cp_ref, up_ref, cw_ref, o_ref, dw_ref):
        i = pl.program_id(0)
        c, u = c_ref[...].astype(F32), u_ref[...].astype(F32)
        cu = c * u
        cup = jnp.where(i == 0, 0.0, cp_ref[...].astype(F32) * up_ref[...].astype(F32))
        cu1, cu2 = _shift_down(cup, cu, 1), _shift_down(cup, cu, 2)
        w0, w1, w2 = cw_ref[0:1, :], cw_ref[1:2, :], cw_ref[2:3, :]
        dyf = dy_ref[...].astype(F32)
        o_ref[:, 0:D] = (dyf * (w0 * cu2 + w1 * cu1 + w2 * cu)).astype(BF16)
        dcv = dyf * b_ref[...].astype(F32)
        dcvn = jnp.where(i == nt - 1, 0.0, dyn_ref[...].astype(F32) * bn_ref[...].astype(F32))
        dcu = w2 * dcv + w1 * _shift_up(dcv, dcvn, 1) + w0 * _shift_up(dcv, dcvn, 2)
        o_ref[:, D:2 * D] = (dcu * u).astype(BF16)
        o_ref[:, 2 * D:3 * D] = (dcu * c).astype(BF16)
        row = lax.broadcasted_iota(jnp.int32, (8, D), 0)
        dw = jnp.zeros((8, D), F32)
        for tap, t in enumerate((cu2, cu1, cu)):
            dw = jnp.where(row == tap, jnp.sum(dcv * t, axis=0, keepdims=True), dw)
        _accumulate(dw_ref, i == 0, dw)

    tile = lambda col: pl.BlockSpec((tm, D), lambda i: (i, col))
    prev = lambda col: pl.BlockSpec((BF16_ROWS, D), lambda i: (jnp.maximum(i * hb - 1, 0), col))
    nxt = lambda col: pl.BlockSpec((BF16_ROWS, D), lambda i: (jnp.minimum((i + 1) * hb, last), col))
    return _call(
        body, name=name, grid=(nt,),
        in_specs=[tile(0), nxt(0), tile(0), nxt(0), tile(1), tile(2), prev(1), prev(2),
                  pl.BlockSpec((3, D), lambda i: (0, 0))],
        out_specs=[pl.BlockSpec((tm, 3 * D), lambda i: (i, 0)), pl.BlockSpec((8, D), lambda i: (0, 0))],
        out_shape=[SDS((T, 3 * D), BF16), SDS((8, D), F32)],
        semantics=("arbitrary",), args=(dy, dy, bcx, bcx, bcx, bcx, bcx, bcx, conv_w), ride=ride)


class NoTraffic:
    def ride(self, kernel_name):
        return None

    def landed(self, kernel_name, results, wts):
        pass

    def grad(self, key, value):
        pass


def local_step(x, target, wts, vec, traffic):
    T, D = x.shape
    tabs = rope_tables(T)
    small = {}

    def run(builder, *args, name, **kw):
        ride = traffic.ride(name)
        if ride is None:
            return builder(*args, name=name, **kw)
        out, extra = builder(*args, name=name, ride=ride, **kw)
        traffic.landed(name, extra, wts)
        return out

    bcx, xn1 = run(norm_matmul, x, vec["a_pre"], wts["w_in"], tn=3 * D, split=1, name="a_in")
    bcx = bcx[0]
    h1, z0, y0 = run(conv_mix_out, bcx, vec["conv_w"], wts["w_out"], vec["a_post"], x, name="a_out")
    gu0, act0, xt2 = run(norm_swiglu_in, h1, vec["ffn_pre0"], wts["gu0"], name="ffn0_in")
    h2, z1 = run(plain_mix_out, act0, wts["wd0"], vec["ffn_post0"], h1, name="ffn0_out")
    kvp, xkv, qp, xq = norm2_matmul(h2, [vec["kv_norm"], vec["b_pre"]], [wts["w_kv"], wts["w_q"]], name="kvq_in")
    attn, attn_stats = run(attention_fwd, qp, kvp, tabs, vec["sinks"], name="attn_fwd")
    h3, z2 = plain_mix_out(attn, wts["w_o"], vec["b_post"], h2, name="attn_out", tm=BIG_ROW_TILE)
    gu1, act1, xt3 = run(norm_swiglu_in, h3, vec["ffn_pre1"], wts["gu1"], name="ffn1_in")
    dy, dz3, small["ffn_post1"], dact1, loss = plain_mix_out(act1, wts["wd1"], vec["ffn_post1"], h3, name="ffn1_out",
                                                             target=target)

    def ffn_bwd(layer, dz, dact, gu, act, xt, h_in, dh, then, gu_first):
        tag = "ffn%d" % layer
        dwd = lambda: traffic.grad("wd%d" % layer, run(matmul_tn, act, dz[None], tb=D, name=tag + "_dwd"))
        dwgu = lambda: traffic.grad("gu%d" % layer, run(swiglu_bwd_tn, xt, dact, gu, name=tag + "_dwgu"))
        for step in ((dwgu, dwd) if gu_first else (dwd, dwgu)):
            step()
        dh_in, small["ffn_pre%d" % layer], dz_, dg_, da_ = run(
            swiglu_bwd_in, dact, gu, wts["gu%d" % layer], h_in, vec["ffn_pre%d" % layer], dh, then,
            name=tag + "_in_bwd")
        return dh_in, dz_, dg_, da_

    dh3, dz2, small["b_post"], dattn = ffn_bwd(1, dz3, dact1, gu1, act1, xt3, h3, dy,
                                               (z2, vec["b_post"], wts["w_o"]), gu_first=False)
    traffic.grad("w_o", matmul_tn(attn, dz2[None], tb=D, name="attn_dwo"))
    dq, dkv, small["sinks"] = run(attention_bwd, qp, kvp, tabs, vec["sinks"], dattn, attn, attn_stats,
                                  name="attn_bwd")
    traffic.grad("w_q", matmul_tn(xq, dq[None], tb=D, name="attn_dwq"))
    traffic.grad("w_kv", matmul_tn(xkv, dkv[None], tb=dkv.shape[1], name="attn_dwkv"))
    dh2, small["b_pre"], small["kv_norm"], dz1, small["ffn_post0"], dact0 = run(
        matmuls_nt_normbwd, [dq, dkv], [wts["w_q"], wts["w_kv"]], h2, [vec["b_pre"], vec["kv_norm"]], dh3,
        (z1, vec["ffn_post0"], wts["wd0"]), name="qkv_in_bwd")
    dh1, dz0, small["a_post"], dyc = ffn_bwd(0, dz1, dact0, gu0, act0, xt2, h1, dh2,
                                             (z0, vec["a_post"], wts["w_out"]), gu_first=True)
    traffic.grad("w_out", run(matmul_tn, y0, dz0[None], tb=D, name="a_dwout"))
    dbcx, small["conv_w"] = run(conv_bwd, dyc, bcx, vec["conv_w"], name="a_conv_bwd")
    traffic.grad("w_in", run(matmul_tn, xn1, dbcx[None], tb=3 * D // 2, name="a_dwin"))
    dx, small["a_pre"] = run(matmul_nt_normbwd, dbcx[None], wts["w_in"], x, vec["a_pre"], dh1, name="a_in_bwd")
    return loss, dx, small


SMALL_ROWS = 16
LOSS_ROW = 13

WHOLE = None
GATHER_PLAN = {"cast_rest": [("w_in", WHOLE)],
               "a_in": [("w_out", WHOLE), ("gu0", (0, 18))],
               "a_out": [("gu0", (18, 14))],
               "ffn0_in": [("wd0", WHOLE), ("w_kv", WHOLE), ("w_q", WHOLE), ("w_o", WHOLE)],
               "ffn0_out": [("gu1", (0, 16))],
               "attn_fwd": [("gu1", (16, 16))],
               "ffn1_in": [("wd1", WHOLE)]}
PAIR_PLAN = {"ffn1_dwgu": ["wd1"], "ffn1_in_bwd": ["gu1"], "attn_bwd": ["w_o"], "qkv_in_bwd": ["w_q", "w_kv"],
             "ffn0_dwd": ["gu0"], "ffn0_in_bwd": ["wd0"], "a_conv_bwd": ["w_out"]}
PAIR_ALONE = ["w_in"]
CHIP_PLAN = {"ffn1_in_bwd": [("wd1", WHOLE)], "attn_bwd": [("gu1", WHOLE)],
             "ffn0_dwgu": [("w_o", WHOLE), ("w_q", WHOLE), ("w_kv", WHOLE)],
             "ffn0_in_bwd": [("gu0", WHOLE)], "a_dwout": [("wd0", (0, 4))], "a_conv_bwd": [("wd0", (4, 10))],
             "a_dwin": [("wd0", (14, 8)), ("w_out", WHOLE)], "a_in_bwd": [("w_in", WHOLE)]}
HALF_PLAN = {"a_in_bwd": ["gu0", "gu1", "wd0", "wd1", "w_kv", "w_q", "w_o", "w_out"]}
GRAD_KIND = dict(KIND, gu0="split", gu1="split")


class Traffic:
    def __init__(self, wholes, quarter, c_arr, pc_arr):
        self.wholes, self.quarter, self.c_arr, self.pc_arr = wholes, quarter, c_arr, pc_arr
        self.views, self.sums, self.got = {}, {}, {}
        self.reduced = {}
        self.stages = {}

    def reduce(self, keys, name):
        return chip_reduce([self.sums[k] for k in keys], [self.got[k] for k in keys], [GRAD_KIND[k] for k in keys],
                           self.pc_arr, name=name)

    def ride(self, name, small=None):
        rides, stages = [], []
        if name in GATHER_PLAN:
            plan = GATHER_PLAN[name]
            rides.append(gather_ride([self.wholes[k] for k, _ in plan],
                                     [(KIND[k], self.quarter[k], part) for k, part in plan], small))
            stages.append(("gather", [k for k, _ in plan]))
        if name in CHIP_PLAN:
            plan = CHIP_PLAN[name]
            rides.append(chip_ride([self.sums[k] for k, _ in plan],
                                   [(GRAD_KIND[k], self.quarter[k], part) for k, part in plan],
                                   earlier=[self.got.get(k) for k, _ in plan]))
            stages.append(("chip", [k for k, _ in plan]))
        if name in PAIR_PLAN:
            keys = PAIR_PLAN[name]
            rides.append(pair_ride([self.views[k] for k in keys]))
            stages.append(("pair", keys))
        if name in HALF_PLAN:
            keys = HALF_PLAN[name]
            rides.append(half_ride(self.reduce(keys, "chip_reduce_early")))
            stages.append(("half", keys))
        self.stages[name] = stages
        return join(rides)

    def landed(self, name, results, wts):
        results = list(results)
        for stage, keys in self.stages[name]:
            mine, results = results[:len(keys)], results[len(keys):]
            if stage == "gather":
                for k, whole in zip(keys, mine):
                    self.wholes[k] = wts[k] = whole
            elif stage == "chip":
                self.got.update(zip(keys, mine))
            elif stage == "half":
                self.reduced.update(zip(keys, mine))
            else:
                for k, got in zip(keys, mine):
                    self.sums[k] = pair_add(self.views[k], got, self.c_arr, name="pair_add_" + k)

    def grad(self, key, value):
        r, ws = self.quarter[key]
        view = {"row": (N_CHIPS, 2, r // 2, ws), "col": (1, 2, r // 2, N_CHIPS * ws), "split": (2, 2, r // 2, 2 * ws)}
        self.views[key] = value.reshape(view[GRAD_KIND[key]])
        if key in PAIR_ALONE:
            (got,) = alone(pair_ride([self.views[key]]), name="pair_exchange_" + key)
            self.sums[key] = pair_add(self.views[key], got, self.c_arr, name="pair_add_" + key)


def kernel(x, a_pre_norm, a_w_in, a_conv_w, a_w_out, a_post_norm, ffn_pre_norm, ffn_w_gate_up, ffn_w_down, ffn_post_norm, kv_norm, w_kv, b_pre_norm, b_w_q, b_sinks, b_w_o, b_post_norm, loss_target, m_a_pre_norm, m_a_w_in, m_a_conv_w, m_a_w_out, m_a_post_norm, m_ffn_pre_norm, m_ffn_w_gate_up, m_ffn_w_down, m_ffn_post_norm, m_kv_norm, m_w_kv, m_b_pre_norm, m_b_w_q, m_b_sinks, m_b_w_o, m_b_post_norm, v_a_pre_norm, v_a_w_in, v_a_conv_w, v_a_w_out, v_a_post_norm, v_ffn_pre_norm, v_ffn_w_gate_up, v_ffn_w_down, v_ffn_post_norm, v_kv_norm, v_w_kv, v_b_pre_norm, v_b_w_q, v_b_sinks, v_b_w_o, v_b_post_norm):
    T, D = x.shape[1], x.shape[2]
    xi, yi, ci = _place()
    p = 2 * xi + yi
    p_arr = jnp.reshape(p, (1,)).astype(jnp.int32)
    c_arr = jnp.reshape(ci, (1,)).astype(jnp.int32)
    pc_arr = jnp.stack([p, ci]).astype(jnp.int32)
    me_arr = jnp.reshape(4 * xi + 2 * yi + ci, (1,)).astype(jnp.int32)
    qd = D // N_CHIPS

    big = {"w_in": (a_w_in, 0), "w_out": (a_w_out, 0), "gu0": (ffn_w_gate_up, 0), "gu1": (ffn_w_gate_up, 1),
           "wd0": (ffn_w_down, 0), "wd1": (ffn_w_down, 1), "w_kv": (w_kv[None], 0), "w_q": (b_w_q, 0),
           "w_o": (b_w_o, 0)}
    names = list(big)
    quarter = {k: w.shape[1:] for k, (w, _) in big.items()}
    source = lambda k: big[k] + (KIND[k],)
    traffic = Traffic(dict(zip(names[:1], cast_quarters([source(names[0])], p_arr, name="cast_first"))), quarter,
                      c_arr, pc_arr)
    small_shard = jnp.concatenate([a_pre_norm, a_post_norm, a_conv_w[0], jnp.zeros((3, qd), F32)], axis=0)
    wts = {}
    rest, (*landed, small_full) = cast_quarters([source(k) for k in names[1:]], p_arr, name="cast_rest",
                                                ride=traffic.ride("cast_rest", small_shard))
    traffic.wholes.update(zip(names[1:], rest))
    traffic.landed("cast_rest", landed, wts)
    rows = lambda k: jnp.transpose(small_full[:, k], (1, 0, 2)).reshape(-1, D)
    vec = {"a_pre": rows(slice(0, 1)), "a_post": rows(slice(1, 2)), "conv_w": rows(slice(2, 5)),
           "ffn_pre0": ffn_pre_norm[0:1], "ffn_pre1": ffn_pre_norm[1:2],
           "ffn_post0": ffn_post_norm[0:1], "ffn_post1": ffn_post_norm[1:2],
           "kv_norm": kv_norm[None], "b_pre": b_pre_norm, "b_post": b_post_norm, "sinks": b_sinks}

    loss, dx, small = local_step(x[0], loss_target[0], wts, vec, traffic)

    pad = lambda a: jnp.pad(a, ((0, 0), (0, D - a.shape[1])))
    small_block = jnp.concatenate(
        [small["a_pre"], small["a_post"], small["conv_w"][0:3], small["ffn_pre0"], small["ffn_pre1"],
         small["ffn_post0"], small["ffn_post1"], small["kv_norm"], small["b_pre"], small["b_post"],
         pad(small["sinks"][0:1]), pad(loss[0:1]), jnp.zeros((SMALL_ROWS - LOSS_ROW - 1, D), F32)], axis=0)
    late = [k for k in names if k not in traffic.reduced]
    *swapped, small_blocks = alone(join([half_ride(traffic.reduce(late, "chip_reduce_late")),
                                         chip_ride([], [], small_block)]), name="last_exchange")
    traffic.reduced.update(zip(late, swapped))
    grad = {k: traffic.reduced[k].reshape(quarter[k]) for k in names}
    small_sum = small_reduce(small_blocks, me_arr)

    out = {}
    out["a_w_in"] = adamw(a_w_in, [grad["w_in"]], m_a_w_in, v_a_w_in, name="adamw_a_w_in")
    out["a_w_out"] = adamw(a_w_out, [grad["w_out"]], m_a_w_out, v_a_w_out, name="adamw_a_w_out")
    out["ffn_w_gate_up"] = adamw(ffn_w_gate_up, [grad["gu0"], grad["gu1"]], m_ffn_w_gate_up, v_ffn_w_gate_up,
                                 name="adamw_ffn_w_gate_up")
    out["ffn_w_down"] = adamw(ffn_w_down, [grad["wd0"], grad["wd1"]], m_ffn_w_down, v_ffn_w_down,
                              name="adamw_ffn_w_down")
    out["w_kv"] = [o[0] for o in adamw(w_kv[None], [grad["w_kv"]], m_w_kv[None], v_w_kv[None], name="adamw_w_kv")]
    out["b_w_q"] = adamw(b_w_q, [grad["w_q"]], m_b_w_q, v_b_w_q, name="adamw_b_w_q")
    out["b_w_o"] = adamw(b_w_o, [grad["w_o"]], m_b_w_o, v_b_w_o, name="adamw_b_w_o")

    def pack(a_pre, a_post, conv, ffn_pre, ffn_post, kvn, b_pre, b_post, sinks):
        return jnp.concatenate([pad(a_pre), pad(a_post), pad(conv[0]), ffn_pre, ffn_post, kvn[None], b_pre, b_post,
                                pad(sinks), jnp.zeros((SMALL_ROWS - 13, D), F32)], axis=0)

    g_small = jnp.concatenate([pad(lax.dynamic_slice(small_sum, (0, p * qd), (5, qd))), small_sum[5:]], axis=0)
    w_small = pack(a_pre_norm, a_post_norm, a_conv_w, ffn_pre_norm, ffn_post_norm, kv_norm, b_pre_norm, b_post_norm,
                   b_sinks)
    m_small = pack(m_a_pre_norm, m_a_post_norm, m_a_conv_w, m_ffn_pre_norm, m_ffn_post_norm, m_kv_norm,
                   m_b_pre_norm, m_b_post_norm, m_b_sinks)
    v_small = pack(v_a_pre_norm, v_a_post_norm, v_a_conv_w, v_ffn_pre_norm, v_ffn_post_norm, v_kv_norm,
                   v_b_pre_norm, v_b_post_norm, v_b_sinks)
    packed = adamw(w_small[None], [g_small], m_small[None], v_small[None], name="adamw_small")
    ns = b_sinks.shape[1]
    unpack = lambda a: {"a_pre_norm": a[0:1, :qd], "a_post_norm": a[1:2, :qd], "a_conv_w": a[None, 2:5, :qd],
                        "ffn_pre_norm": a[5:7], "ffn_post_norm": a[7:9], "kv_norm": a[9], "b_pre_norm": a[10:11],
                        "b_post_norm": a[11:12], "b_sinks": a[12:13, :ns]}
    unpacked = [unpack(a[0]) for a in packed]
    for k in unpacked[0]:
        out[k] = [u[k] for u in unpacked]

    order = ["a_pre_norm", "a_w_in", "a_conv_w", "a_w_out", "a_post_norm", "ffn_pre_norm", "ffn_w_gate_up",
             "ffn_w_down", "ffn_post_norm", "kv_norm", "w_kv", "b_pre_norm", "b_w_q", "b_sinks", "b_w_o",
             "b_post_norm"]
    return (small_sum[LOSS_ROW, 0], dx[None], *[out[k][0] for k in order], *[out[k][1] for k in order],
            *[out[k][2] for k in order], *[out[k][3] for k in order])
```

```python
import math

import jax
import jax.numpy as jnp
from jax import lax
from jax.experimental import pallas as pl
from jax.experimental.pallas import tpu as pltpu

F32 = jnp.float32
BF16 = jnp.bfloat16
SDS = jax.ShapeDtypeStruct
MESH = pl.DeviceIdType.MESH
DMA = pltpu.SemaphoreType.DMA
HBM_SPEC = pl.BlockSpec(memory_space=pltpu.HBM)

EPS = 1e-6
NEG = -1e30
HEAD_DIM = 64
N_KV_HEADS = 4
BLOCK = 128
ROT_DIM = HEAD_DIM // 4
ROPE_THETA = 500000.0
N_CHIPS = 4

ADAM_LR = 0.001
ADAM_B1 = 0.9
ADAM_B2 = 0.999
ADAM_EPS = 1e-08
ADAM_WD = 0.01
ADAM_STEP = 10

VMEM_LIMIT_BYTES = 52 * 1024 * 1024
ROW_TILE = 512
BF16_ROWS = 16
STREAM = BF16
MXU_WIDTH = 256

KIND = {"w_in": "col", "gu0": "col", "gu1": "col", "w_out": "row", "wd0": "row", "wd1": "row", "w_kv": "row",
        "w_q": "row", "w_o": "row"}


def _params(*semantics):
    return pltpu.CompilerParams(dimension_semantics=semantics, vmem_limit_bytes=VMEM_LIMIT_BYTES)


def _row_tile(rows, limit, step=8):
    return max(t for t in range(step, limit + 1, step) if rows % t == 0)


def _place():
    return lax.axis_index("x"), lax.axis_index("y"), lax.axis_index("c")


def _other_chips(x, y):
    return [(1 - x, y), (x, 1 - y), (1 - x, 1 - y)]


def _remote(src, dst, send_sem, recv_sem, to):
    return pltpu.make_async_remote_copy(src_ref=src, dst_ref=dst, send_sem=send_sem, recv_sem=recv_sem,
                                        device_id=to, device_id_type=MESH)


def _full_shape(kind, quarter):
    r, ws = quarter
    return (N_CHIPS * r, ws) if kind == "row" else (r, N_CHIPS * ws)


def _rows_of(h, part):
    lo, n = (0, h) if part is None else (part[0] * BF16_ROWS, part[1] * BF16_ROWS)
    assert lo + n <= h, (h, part)
    return lo, n


def _half_of_quarter(ref, kind, quarter, part, q, half):
    r, ws = quarter
    h = r // 2
    lo, n = _rows_of(h, part)
    if kind == "row":
        return ref.at[pl.ds(pl.multiple_of(q * r + half * h + lo, BF16_ROWS), n)]
    return ref.at[pl.ds(pl.multiple_of(half * h + lo, BF16_ROWS), n), pl.ds(pl.multiple_of(q * ws, 128), ws)]


class Ride:
    def __init__(self, operands, out_shape, aliases, sems, make):
        self.operands, self.out_shape, self.aliases, self.sems, self.make = operands, out_shape, aliases, sems, make

    def stages(self, ins, outs, sems):
        made = self.make(ins, outs, sems)
        return made if len(made) == 3 else (made[0], None, made[1])


def join(rides):
    rides = [r for r in rides if r is not None]
    if len(rides) < 2:
        return rides[0] if rides else None
    aliases, at = {}, [0, 0, 0]
    cuts = []
    for r in rides:
        aliases.update({at[0] + i: at[1] + o for i, o in r.aliases.items()})
        cuts.append(tuple(at))
        at = [at[0] + len(r.operands), at[1] + len(r.out_shape), at[2] + len(r.sems)]
    cuts.append(tuple(at))

    def make(ins, outs, sem):
        made = [r.stages(ins[lo[0]:hi[0]], outs[lo[1]:hi[1]], sem[lo[2]:hi[2]]) for r, lo, hi in zip(rides, cuts, cuts[1:])]
        relays = [m[1] for m in made if m[1] is not None]

        def start():
            for m in made:
                m[0]()

        def relay():
            for r in relays:
                r()

        def finish():
            for m in made:
                m[2]()

        return (start, relay, finish) if relays else (start, finish)

    return Ride(sum((list(r.operands) for r in rides), []), sum((list(r.out_shape) for r in rides), []), aliases,
                sum((list(r.sems) for r in rides), []), make)


def _call(body, *, name, grid, in_specs, out_specs, out_shape, args, scratch_shapes=(), semantics=None, ride=None,
          prefetch=None):
    pre = 0 if prefetch is None else 1
    n_in, n_out, n_scr = len(in_specs), len(out_specs), len(scratch_shapes)
    r_in, r_out = (len(ride.operands), len(ride.out_shape)) if ride is not None else (0, 0)
    a, b = pre + n_in, pre + n_in + r_in
    c, d = b + n_out, b + n_out + r_out
    e = d + n_scr

    def riding(*refs):
        start, relay, finish = ride.stages(refs[a:b], refs[c:d], refs[e:])
        ids = [pl.program_id(k) for k in range(len(grid))]
        first, last = ids[0] == 0, ids[0] == grid[0] - 1
        for k in range(1, len(grid)):
            first, last = first & (ids[k] == 0), last & (ids[k] == grid[k] - 1)
        pl.when(first)(start)
        if relay is not None:
            pl.when(last)(relay)
        body(*refs[:a], *refs[b:c], *refs[d:e])
        pl.when(last)(finish)

    if ride is None:
        kernel_body, extra_in, extra_out, extra_shape, extra_scr, aliases = body, [], [], [], [], {}
        params = _params(*semantics)
    else:
        kernel_body, extra_in, extra_out = riding, [HBM_SPEC] * r_in, [HBM_SPEC] * r_out
        extra_shape, extra_scr = list(ride.out_shape), list(ride.sems)
        aliases = {pre + n_in + i: n_out + o for i, o in ride.aliases.items()}
        params = _params(*(("arbitrary",) * len(grid)))
    specs = dict(grid=grid, in_specs=list(in_specs) + extra_in, out_specs=list(out_specs) + extra_out,
                 scratch_shapes=list(scratch_shapes) + extra_scr)
    if prefetch is not None:
        specs = dict(grid_spec=pltpu.PrefetchScalarGridSpec(num_scalar_prefetch=1, **specs))
        args = (prefetch,) + tuple(args)
    outs = pl.pallas_call(kernel_body, name=name, out_shape=list(out_shape) + extra_shape,
                          input_output_aliases=aliases, compiler_params=params, **specs,
                          )(*args, *(ride.operands if ride is not None else ()))
    return outs if ride is None else (outs[:n_out], outs[n_out:])


def alone(ride, *, name):
    def body(*refs):
        n = len(ride.operands)
        stages = ride.stages(refs[:n], refs[n:n + len(ride.out_shape)], refs[n + len(ride.out_shape):])
        for stage in stages:
            if stage is not None:
                stage()

    return pl.pallas_call(
        body, name=name, in_specs=[HBM_SPEC] * len(ride.operands), out_specs=[HBM_SPEC] * len(ride.out_shape),
        out_shape=list(ride.out_shape), input_output_aliases=dict(ride.aliases), scratch_shapes=list(ride.sems),
    )(*ride.operands)


def gather_ride(wholes, metas, small=None):
    n = len(wholes)
    operands, out_shape = list(wholes), [SDS(s.shape, s.dtype) for s in wholes]
    sems = [DMA((n, 3)), DMA((n, 3)), DMA((n, 3)), DMA((n, 3))]
    if small is not None:
        operands.append(small)
        out_shape.append(SDS((N_CHIPS,) + small.shape, small.dtype))
        sems += [DMA((3,)), DMA((3,)), DMA(())]

    def make(ins, outs, sem):
        send1, recv1, send2, recv2 = sem[:4]
        x, y, c = _place()
        p = 2 * x + y
        chips = _other_chips(x, y)
        me, sibling = (x, y, c), (x, y, 1 - c)
        part = lambda t, q, half: _half_of_quarter(outs[t], *metas[t], q, half)
        first, landing, passing, arriving = [], [], [], []
        for j, (qx, qy) in enumerate(chips):
            q = 2 * qx + qy
            if small is not None:
                first.append(_remote(ins[n], outs[n].at[p], sem[4].at[j], sem[5].at[j], (qx, qy, c)))
                arriving.append(_remote(outs[n].at[q], outs[n].at[q], sem[4].at[j], sem[5].at[j], me))
            for t in range(n):
                first.append(_remote(part(t, p, c), part(t, p, c), send1.at[t, j], recv1.at[t, j], (qx, qy, c)))
                landed, theirs = part(t, q, c), part(t, q, 1 - c)
                landing.append(_remote(landed, landed, send1.at[t, j], recv1.at[t, j], me))
                passing.append(_remote(landed, landed, send2.at[t, j], recv2.at[t, j], sibling))
                arriving.append(_remote(theirs, theirs, send2.at[t, j], recv2.at[t, j], me))
        local = [] if small is None else [pltpu.make_async_copy(ins[n], outs[n].at[p], sem[6])]

        def start():
            for cp in local + first:
                cp.start()

        def relay():
            for got, cp in zip(landing, passing):
                got.wait_recv()
                cp.start()

        def finish():
            for cp in arriving:
                cp.wait_recv()
            for cp in first + passing:
                cp.wait_send()
            for cp in local:
                cp.wait()

        return start, relay, finish

    return Ride(operands, out_shape, {t: t for t in range(n)}, sems, make)


def chip_ride(sums, metas, small=None, earlier=None):
    n = len(sums)
    operands = list(sums)
    out_shape = [SDS((3, s.shape[1], quarter[1]), s.dtype) for s, (_, quarter, _) in zip(sums, metas)]
    sems = [DMA((n, 3)), DMA((n, 3))] if n else []
    if small is not None:
        operands.append(small)
        out_shape.append(SDS((8,) + small.shape, small.dtype))
        sems += [DMA((7,)), DMA((7,)), DMA(())]
    aliases = {}
    for t, buffer in enumerate(earlier or [None] * n):
        if buffer is not None:
            aliases[len(operands)] = t
            operands.append(buffer)

    def make(ins, outs, sem):
        x, y, c = _place()
        cps = []
        for j, (qx, qy) in enumerate(_other_chips(x, y)):
            q = 2 * qx + qy
            for t in range(n):
                kind, (_, ws), part = metas[t]
                rows = pl.ds(*_rows_of(ins[t].shape[1], part))
                if kind == "row":
                    src = ins[t].at[q, rows]
                elif kind == "col":
                    src = ins[t].at[0, rows, pl.ds(pl.multiple_of(q * ws, 128), ws)]
                else:
                    src = ins[t].at[q // 2, rows, pl.ds(pl.multiple_of((q % 2) * ws, 128), ws)]
                cps.append(_remote(src, outs[t].at[j, rows], sem[0].at[t, j], sem[1].at[t, j], (qx, qy, c)))
        local = []
        if small is not None:
            ssend, srecv, lsem = sem[2 * bool(n):2 * bool(n) + 3]
            local.append(pltpu.make_async_copy(ins[n], outs[n].at[0], lsem))
            for k in range(1, 8):
                peer = (x ^ (k >> 2 & 1), y ^ (k >> 1 & 1), c ^ (k & 1))
                cps.append(_remote(ins[n], outs[n].at[k], ssend.at[k - 1], srecv.at[k - 1], peer))

        def start():
            for cp in local + cps:
                cp.start()

        def finish():
            for cp in cps + local:
                cp.wait()

        return start, finish

    return Ride(operands, out_shape, aliases, sems, make)


def pair_ride(grads):
    n = len(grads)

    def make(ins, outs, sem):
        x, y, c = _place()
        cps = [_remote(ins[t].at[:, 1 - c], outs[t], sem[0].at[t], sem[1].at[t], (x, y, 1 - c)) for t in range(n)]

        def start():
            for cp in cps:
                cp.start()

        def finish():
            for cp in cps:
                cp.wait()

        return start, finish

    return Ride(list(grads), [SDS((g.shape[0],) + g.shape[2:], g.dtype) for g in grads], {}, [DMA((n,)), DMA((n,))],
                make)


def half_ride(quarters):
    n = len(quarters)

    def make(ins, outs, sem):
        x, y, c = _place()
        sends = [_remote(outs[t].at[c], outs[t].at[c], sem[0].at[t], sem[1].at[t], (x, y, 1 - c)) for t in range(n)]

        def start():
            for cp in sends:
                cp.start()

        def finish():
            for t in range(n):
                theirs = outs[t].at[1 - c]
                _remote(theirs, theirs, sem[0].at[t], sem[1].at[t], (x, y, c)).wait_recv()
            for cp in sends:
                cp.wait_send()

        return start, finish

    return Ride(list(quarters), [SDS(q.shape, q.dtype) for q in quarters], {t: t for t in range(n)},
                [DMA((n,)), DMA((n,))], make)


CAST_STEPS = 4


def cast_quarters(sources, p_arr, *, name, ride=None):
    n = len(sources)
    in_specs, out_specs, out_shape = [], [], []
    for w, layer, kind in sources:
        _, r, ws = w.shape
        tr = r // CAST_STEPS
        assert tr % BF16_ROWS == 0, w.shape
        in_specs.append(pl.BlockSpec((None, tr, ws), lambda i, p_ref, layer=layer: (layer, i, 0)))
        out_specs.append(pl.BlockSpec((tr, ws), (lambda i, p_ref: (p_ref[0] * CAST_STEPS + i, 0)) if kind == "row"
                                      else (lambda i, p_ref: (i, p_ref[0]))))
        out_shape.append(SDS(_full_shape(kind, (r, ws)), BF16))

    def body(p_ref, *refs):
        for w_ref, o_ref in zip(refs[:n], refs[n:]):
            o_ref[...] = w_ref[...].astype(BF16)

    return _call(body, name=name, grid=(CAST_STEPS,), in_specs=in_specs, out_specs=out_specs, out_shape=out_shape,
                 semantics=("parallel",), args=[w for w, _, _ in sources], ride=ride, prefetch=p_arr)


def pair_add(own, got, c_arr, *, name):
    A, _, h, W = own.shape
    th = _row_tile(h, max(BF16_ROWS, (3 << 19) // W), BF16_ROWS)

    def body(c_ref, a_ref, b_ref, o_ref):
        o_ref[...] = (a_ref[...].astype(F32) + b_ref[...].astype(F32)).astype(BF16)

    return pl.pallas_call(
        body, name=name,
        grid_spec=pltpu.PrefetchScalarGridSpec(
            num_scalar_prefetch=1, grid=(A, h // th),
            in_specs=[pl.BlockSpec((None, None, th, W), lambda q, i, c_ref: (q, c_ref[0], i, 0)),
                      pl.BlockSpec((None, th, W), lambda q, i, c_ref: (q, i, 0))],
            out_specs=pl.BlockSpec((None, th, W), lambda q, i, c_ref: (q, i, 0))),
        out_shape=SDS((A, h, W), BF16),
        compiler_params=_params("parallel", "parallel"),
    )(c_arr, own, got)


REDUCE_STEPS = 2


def chip_reduce(sums, got, kinds, pc_arr, *, name):
    n = len(sums)
    mine = {"row": lambda i, pc_ref: (pc_ref[0], i, 0), "col": lambda i, pc_ref: (0, i, pc_ref[0]),
            "split": lambda i, pc_ref: (pc_ref[0] // 2, i, pc_ref[0] % 2)}
    a_specs, b_specs, o_specs, out_shape = [], [], [], []
    for g, kind in zip(got, kinds):
        _, h, ws = g.shape
        th = h // REDUCE_STEPS
        assert th % BF16_ROWS == 0, g.shape
        a_specs.append(pl.BlockSpec((None, th, ws), mine[kind]))
        b_specs.append(pl.BlockSpec((3, th, ws), lambda i, pc_ref: (0, i, 0)))
        o_specs.append(pl.BlockSpec((None, th, ws), lambda i, pc_ref: (pc_ref[1], i, 0)))
        out_shape.append(SDS((2, h, ws), F32))

    def body(pc_ref, *refs):
        for a_ref, b_ref, o_ref in zip(refs[:n], refs[n:2 * n], refs[2 * n:]):
            o_ref[...] = ((a_ref[...].astype(F32) + b_ref[0].astype(F32)) + b_ref[1].astype(F32)) + b_ref[2].astype(F32)

    return _call(body, name=name, grid=(REDUCE_STEPS,), in_specs=a_specs + b_specs, out_specs=o_specs,
                 out_shape=out_shape, semantics=("parallel",), args=list(sums) + list(got), prefetch=pc_arr)


def small_reduce(blocks, me_arr):
    _, rows, D = blocks.shape

    def body(me_ref, b_ref, o_ref):
        me = me_ref[0]
        total = b_ref[me]
        for d in range(1, 8):
            total = total + b_ref[d ^ me]
        o_ref[...] = total

    return pl.pallas_call(
        body, name="small_reduce",
        grid_spec=pltpu.PrefetchScalarGridSpec(
            num_scalar_prefetch=1, grid=(1,),
            in_specs=[pl.BlockSpec((8, rows, D), lambda i, me_ref: (0, 0, 0))],
            out_specs=pl.BlockSpec((rows, D), lambda i, me_ref: (0, 0))),
        out_shape=SDS((rows, D), F32),
        compiler_params=_params("arbitrary"),
    )(me_arr, blocks)


def adamw(w, gs, m, v, *, name):
    L, r, cols = w.shape
    tr = _row_tile(r, 256)
    nt = r // tr

    def body(*refs):
        w_ref, m_ref, v_ref = refs[:3]
        g_refs = refs[3:3 + L]
        g_out, d_out, m_out, v_out = refs[3 + L:]
        layer = pl.program_id(0)
        g = g_refs[0][...]
        for l in range(1, L):
            g = jnp.where(layer == l, g_refs[l][...], g)
        m_new = ADAM_B1 * m_ref[...] + (1.0 - ADAM_B1) * g
        v_new = ADAM_B2 * v_ref[...] + (1.0 - ADAM_B2) * (g * g)
        m_hat = m_new / (1.0 - ADAM_B1 ** ADAM_STEP)
        v_hat = v_new / (1.0 - ADAM_B2 ** ADAM_STEP)
        g_out[...] = g
        m_out[...] = m_new
        v_out[...] = v_new
        d_out[...] = -ADAM_LR * (m_hat / (jnp.sqrt(v_hat) + ADAM_EPS) + ADAM_WD * w_ref[...])

    full = pl.BlockSpec((None, tr, cols), lambda l, i: (l, i, 0))
    g_spec = lambda l0: pl.BlockSpec((tr, cols), lambda l, i: (jnp.where(l == l0, i, jnp.where(l < l0, 0, nt - 1)), 0))
    return pl.pallas_call(
        body, name=name, grid=(L, nt),
        in_specs=[full, full, full] + [g_spec(l0) for l0 in range(L)],
        out_specs=[full] * 4,
        out_shape=[SDS(w.shape, F32)] * 4,
        compiler_params=_params("arbitrary", "arbitrary"),
    )(w, m, v, *gs)


def _rms_r(xf):
    return lax.rsqrt(jnp.mean(xf * xf, axis=-1, keepdims=True) + EPS)


def _rmsnorm_bwd(xf, g, dy):
    r = _rms_r(xf)
    xh = xf * r
    gd = g * dy
    return r * (gd - xh * jnp.mean(xh * gd, axis=-1, keepdims=True)), xh


def _dot(a, b):
    return jnp.dot(a, b, preferred_element_type=F32)


def _dot_nt(a, b):
    return lax.dot_general(a, b, (((1,), (1,)), ((), ())), preferred_element_type=F32)


def _dot_tn(a, b):
    return lax.dot_general(a, b, (((0,), (0,)), ((), ())), preferred_element_type=F32)


def _accumulate(ref, first, value):
    @pl.when(first)
    def _():
        ref[...] = value

    @pl.when(jnp.logical_not(first))
    def _():
        ref[...] += value


def norm_matmul(x, g, w, *, tn, split, name, ride=None, tm=ROW_TILE):
    T, D = x.shape
    N = w.shape[1]
    per = N // split // tn

    def body(x_ref, g_ref, w_ref, o_ref, xn_ref):
        @pl.when(pl.program_id(1) == 0)
        def _():
            xf = x_ref[...].astype(F32)
            xn_ref[...] = (xf * _rms_r(xf) * g_ref[...]).astype(BF16)

        o_ref[...] = _dot(xn_ref[...], w_ref[...]).astype(BF16)

    return _call(
        body, name=name, grid=(T // tm, N // tn),
        in_specs=[pl.BlockSpec((tm, D), lambda i, j: (i, 0)),
                  pl.BlockSpec((1, D), lambda i, j: (0, 0)),
                  pl.BlockSpec((D, tn), lambda i, j: (0, j))],
        out_specs=[pl.BlockSpec((None, tm, tn), lambda i, j: (j // per, i, j % per)),
                   pl.BlockSpec((tm, D), lambda i, j: (i, 0))],
        out_shape=[SDS((split, T, N // split), BF16), SDS((T, D), BF16)],
        semantics=("parallel", "arbitrary"), args=(x, g, w), ride=ride)


BIG_ROW_TILE = 1024


def norm2_matmul(x, gains, weights, *, name, tm=BIG_ROW_TILE):
    T, D = x.shape
    tm = min(tm, T)
    n = len(gains)

    def body(x_ref, *refs):
        subs = _sub_tiles(tm)
        xhs = []
        for rows in subs:
            xf = x_ref[rows, :].astype(F32)
            xhs.append(xf * _rms_r(xf))
        for g_ref, w_ref, o_ref, xn_ref in zip(refs[:n], refs[n:2 * n], refs[2 * n::2], refs[2 * n + 1::2]):
            for rows, xh in zip(subs, xhs):
                xn = (xh * g_ref[...]).astype(BF16)
                xn_ref[rows, :] = xn
                o_ref[rows, :] = _dot(xn, w_ref[...]).astype(BF16)

    row = pl.BlockSpec((tm, D), lambda i: (i, 0))
    vec = pl.BlockSpec((1, D), lambda i: (0, 0))
    out_specs, out_shape = [], []
    for w in weights:
        out_specs += [pl.BlockSpec((tm, w.shape[1]), lambda i: (i, 0)), row]
        out_shape += [SDS((T, w.shape[1]), BF16), SDS((T, D), BF16)]
    return _call(
        body, name=name, grid=(T // tm,),
        in_specs=[row] + [vec] * n + [pl.BlockSpec(w.shape, lambda i: (0, 0)) for w in weights],
        out_specs=out_specs, out_shape=out_shape, semantics=("parallel",), args=[x] + list(gains) + list(weights))


def _shift_down(prev, cur, by):
    big = jnp.concatenate([prev, cur], axis=0)
    return pltpu.roll(big, by, 0)[prev.shape[0]:]


def _shift_up(cur, nxt, by):
    big = jnp.concatenate([cur, nxt], axis=0)
    return pltpu.roll(big, big.shape[0] - by, 0)[:cur.shape[0]]


def conv_mix_out(bcx, conv_w, w_out, g_post, res, *, name, ride=None, tm=ROW_TILE):
    T, D = res.shape
    hb = tm // BF16_ROWS

    def body(b_ref, c_ref, u_ref, cp_ref, up_ref, cw_ref, w_ref, g_ref, r_ref, h_ref, z_ref, y_ref):
        i = pl.program_id(0)
        cu = c_ref[...].astype(F32) * u_ref[...].astype(F32)
        cup = cp_ref[...].astype(F32) * up_ref[...].astype(F32)
        cup = jnp.where(i == 0, 0.0, cup)
        cv = (cw_ref[0:1, :] * _shift_down(cup, cu, 2) + cw_ref[1:2, :] * _shift_down(cup, cu, 1)
              + cw_ref[2:3, :] * cu)
        y = (b_ref[...].astype(F32) * cv).astype(BF16)
        y_ref[...] = y
        z = _dot(y, w_ref[...])
        z_ref[...] = z.astype(BF16)
        h_ref[...] = (r_ref[...] + z * _rms_r(z) * g_ref[...]).astype(STREAM)

    tile = lambda col: pl.BlockSpec((tm, D), lambda i: (i, col))
    halo = lambda col: pl.BlockSpec((BF16_ROWS, D), lambda i: (jnp.maximum(i * hb - 1, 0), col))
    row = pl.BlockSpec((tm, D), lambda i: (i, 0))
    return _call(
        body, name=name, grid=(T // tm,),
        in_specs=[tile(0), tile(1), tile(2), halo(1), halo(2),
                  pl.BlockSpec((3, D), lambda i: (0, 0)),
                  pl.BlockSpec((D, D), lambda i: (0, 0)),
                  pl.BlockSpec((1, D), lambda i: (0, 0)), row],
        out_specs=[row, row, row],
        out_shape=[SDS((T, D), STREAM), SDS((T, D), BF16), SDS((T, D), BF16)],
        semantics=("parallel",), args=(bcx, bcx, bcx, bcx, bcx, conv_w, w_out, g_post, res), ride=ride)


def _normbwd_then_nt(dh, zf, g_ref, w_ref, dz_ref, dg_ref, o_ref, first):
    dz, zh = _rmsnorm_bwd(zf, g_ref[...], dh)
    dz = dz.astype(BF16)
    dz_ref[...] = dz
    _accumulate(dg_ref, first, jnp.sum(dh * zh, axis=0, keepdims=True))
    o_ref[...] = _dot_nt(dz, w_ref[...]).astype(BF16)


def _then_specs(then, tm, T, D):
    z, g, w = then
    K = w.shape[0]
    row = pl.BlockSpec((tm, D), lambda i: (i, 0))
    vec = pl.BlockSpec((1, D), lambda i: (0, 0))
    in_specs = [row, vec, pl.BlockSpec((K, D), lambda i: (0, 0), pipeline_mode=pl.Buffered(1))]
    out_specs = [row, vec, pl.BlockSpec((tm, K), lambda i: (i, 0))]
    out_shape = [SDS((T, D), BF16), SDS((1, D), F32), SDS((T, K), BF16)]
    return in_specs, out_specs, out_shape


def plain_mix_out(a, w, g_post, res, *, name, target=None, ride=None, tm=ROW_TILE):
    T, D = res.shape
    tm = min(tm, T)
    K = a.shape[1]
    with_loss = target is not None

    def body(a_ref, w_ref, g_ref, r_ref, *rest):
        subs = _sub_tiles(tm)
        zs = [_dot(a_ref[rows, :], w_ref[...]) for rows in subs]
        if not with_loss:
            h_ref, z_ref = rest
            for rows, z in zip(subs, zs):
                h_ref[rows, :] = (r_ref[rows, :].astype(F32) + z * _rms_r(z) * g_ref[...]).astype(STREAM)
                z_ref[rows, :] = z.astype(BF16)
            return
        t_ref, h_ref, dz_ref, dg_ref, da_ref, loss_ref = rest
        first = pl.program_id(0) == 0
        loss, dg = jnp.zeros((), F32), jnp.zeros((1, D), F32)
        for rows, z in zip(subs, zs):
            diff = r_ref[rows, :].astype(F32) + z * _rms_r(z) * g_ref[...] - t_ref[rows, :]
            dh = diff * (1.0 / D)
            h_ref[rows, :] = dh.astype(STREAM)
            loss = loss + jnp.sum(diff * diff)
            dz, zh = _rmsnorm_bwd(z, g_ref[...], dh)
            dz = dz.astype(BF16)
            dz_ref[rows, :] = dz
            dg = dg + jnp.sum(dh * zh, axis=0, keepdims=True)
            da_ref[rows, :] = _dot_nt(dz, w_ref[...]).astype(BF16)
        _accumulate(loss_ref, first, jnp.full(loss_ref.shape, 0.5 / D, F32) * loss)
        _accumulate(dg_ref, first, dg)

    row = pl.BlockSpec((tm, D), lambda i: (i, 0))
    vec = pl.BlockSpec((1, D), lambda i: (0, 0))
    in_specs = [pl.BlockSpec((tm, K), lambda i: (i, 0)), pl.BlockSpec((K, D), lambda i: (0, 0)), vec, row]
    if with_loss:
        in_specs.append(row)
        out_specs = [row, row, vec, pl.BlockSpec((tm, K), lambda i: (i, 0)), pl.BlockSpec((8, 128), lambda i: (0, 0))]
        out_shape = [SDS((T, D), STREAM), SDS((T, D), BF16), SDS((1, D), F32), SDS((T, K), BF16), SDS((8, 128), F32)]
    else:
        out_specs, out_shape = [row, row], [SDS((T, D), STREAM), SDS((T, D), BF16)]
    return _call(
        body, name=name, grid=(T // tm,), in_specs=in_specs, out_specs=out_specs, out_shape=out_shape,
        semantics=("arbitrary",), args=(a, w, g_post, res) + ((target,) if with_loss else ()), ride=ride)


def _silu_grads(d, g, u):
    sg = jax.nn.sigmoid(g)
    return d * u * (sg * (1.0 + g * (1.0 - sg))), d * (g * sg)


def _sub_tiles(tm):
    return [pl.ds(k, min(MXU_WIDTH, tm)) for k in range(0, tm, MXU_WIDTH)]


def norm_swiglu_in(x, g, w, *, name, ride=None, tm=ROW_TILE):
    T, D = x.shape
    F = w.shape[1] // 2

    def body(x_ref, g_ref, wg_ref, wu_ref, gu_ref, a_ref, xt_ref):
        subs = _sub_tiles(tm)
        xns = []
        for rows in subs:
            xf = x_ref[rows, :].astype(F32)
            xns.append(xf * _rms_r(xf) * g_ref[...])
        xbs = [xn.astype(BF16) for xn in xns]
        gates = [_dot(xb, wg_ref[...]).astype(BF16) for xb in xbs]
        ups = [_dot(xb, wu_ref[...]).astype(BF16) for xb in xbs]
        for rows, gate, up in zip(subs, gates, ups):
            gu_ref[0, rows, :] = gate
            gu_ref[1, rows, :] = up
            a_ref[rows, :] = gate * jax.nn.sigmoid(gate) * up
        for rows, xn in zip(subs, xns):
            xt_ref[:, rows] = xn.T.astype(BF16)

    half = lambda s: pl.BlockSpec((D, F), lambda i: (0, s), pipeline_mode=pl.Buffered(1))
    return _call(
        body, name=name, grid=(T // tm,),
        in_specs=[pl.BlockSpec((tm, D), lambda i: (i, 0)), pl.BlockSpec((1, D), lambda i: (0, 0)), half(0), half(1)],
        out_specs=[pl.BlockSpec((2, tm, F), lambda i: (0, i, 0)), pl.BlockSpec((tm, F), lambda i: (i, 0)),
                   pl.BlockSpec((D, tm), lambda i: (0, i))],
        out_shape=[SDS((2, T, F), BF16), SDS((T, F), BF16), SDS((D, T), BF16)],
        semantics=("parallel",), args=(x, g, w, w), ride=ride)


def swiglu_bwd_tn(xt, dact, gu, *, name, ride=None, tb=MXU_WIDTH):
    D, T = xt.shape
    F = dact.shape[1]

    def body(xt_ref, d_ref, g_ref, u_ref, o_ref):
        dg, du = _silu_grads(d_ref[...], g_ref[...], u_ref[...])
        o_ref[0] = _dot(xt_ref[...], dg).astype(BF16)
        o_ref[1] = _dot(xt_ref[...], du).astype(BF16)

    col = lambda s: pl.BlockSpec((None, T, tb), lambda j: (s, 0, j))
    out = _call(
        body, name=name, grid=(F // tb,),
        in_specs=[pl.BlockSpec((D, T), lambda j: (0, 0), pipeline_mode=pl.Buffered(1)),
                  pl.BlockSpec((T, tb), lambda j: (0, j)), col(0), col(1)],
        out_specs=[pl.BlockSpec((2, D, tb), lambda j: (0, 0, j))],
        out_shape=[SDS((2, D, F), BF16)],
        semantics=("parallel",), args=(xt, dact, gu, gu), ride=ride)
    return out[0] if ride is None else (out[0][0], out[1])


def swiglu_bwd_in(dact, gu, w, h_in, g, dh_out, then, *, name, ride=None, tm=ROW_TILE):
    T, D = h_in.shape
    F = dact.shape[1]

    def body(d_ref, gg_ref, uu_ref, wg_ref, wu_ref, h_ref, g_ref, dh_ref, z_ref, g2_ref, w2_ref,
             o_ref, dg_ref, dz_ref, dg2_ref, da_ref):
        first = pl.program_id(0) == 0
        subs = _sub_tiles(tm)
        dns = []
        for rows in subs:
            dgate, dup = _silu_grads(d_ref[rows, :], gg_ref[rows, :], uu_ref[rows, :])
            dns.append(_dot_nt(dgate, wg_ref[...]) + _dot_nt(dup, wu_ref[...]))
        dg, dg2 = jnp.zeros((1, D), F32), jnp.zeros((1, D), F32)
        for rows, dn in zip(subs, dns):
            dx, hh = _rmsnorm_bwd(h_ref[rows, :].astype(F32), g_ref[...], dn)
            dh_in = dh_ref[rows, :] + dx
            o_ref[rows, :] = dh_in.astype(STREAM)
            dg = dg + jnp.sum(dn * hh, axis=0, keepdims=True)
            dz, zh = _rmsnorm_bwd(z_ref[rows, :].astype(F32), g2_ref[...], dh_in)
            dz = dz.astype(BF16)
            dz_ref[rows, :] = dz
            dg2 = dg2 + jnp.sum(dh_in * zh, axis=0, keepdims=True)
            da_ref[rows, :] = _dot_nt(dz, w2_ref[...]).astype(BF16)
        _accumulate(dg_ref, first, dg)
        _accumulate(dg2_ref, first, dg2)

    row = pl.BlockSpec((tm, D), lambda i: (i, 0))
    vec = pl.BlockSpec((1, D), lambda i: (0, 0))
    part = lambda s: pl.BlockSpec((None, tm, F), lambda i: (s, i, 0))
    half = lambda s: pl.BlockSpec((D, F), lambda i: (0, s), pipeline_mode=pl.Buffered(1))
    then_in, then_out, then_shape = _then_specs(then, tm, T, D)
    return _call(
        body, name=name, grid=(T // tm,),
        in_specs=[pl.BlockSpec((tm, F), lambda i: (i, 0)), part(0), part(1), half(0), half(1), row, vec, row] + then_in,
        out_specs=[row, vec] + then_out,
        out_shape=[SDS((T, D), STREAM), SDS((1, D), F32)] + then_shape,
        semantics=("arbitrary",), args=(dact, gu, gu, w, w, h_in, g, dh_out) + tuple(then), ride=ride)


def rope_tables(T):
    half = ROT_DIM // 2
    inv_freq = ROPE_THETA ** (-jnp.arange(0, ROT_DIM, 2, dtype=F32) / ROT_DIM)
    ang = (jnp.arange(T, dtype=F32)[:, None] * inv_freq[None, :]).T
    cos, sin = jnp.cos(ang), jnp.sin(ang)
    rest = HEAD_DIM - ROT_DIM
    one, zero = jnp.ones((rest, T), F32), jnp.zeros((rest, T), F32)
    zh = jnp.zeros((half, T), F32)
    fac = jnp.concatenate([cos, cos, one], axis=0)
    up = jnp.concatenate([-sin, zh, zero], axis=0)
    down = jnp.concatenate([zh, sin, zero], axis=0)
    return jnp.stack([fac, up, down])


def _rope(t, tab):
    half = ROT_DIM // 2
    return t * tab[0] + pltpu.roll(t, HEAD_DIM - half, 0) * tab[1] + pltpu.roll(t, half, 0) * tab[2]


def _rope_t(d, tab):
    half = ROT_DIM // 2
    return d * tab[0] + pltpu.roll(d * tab[1], half, 0) + pltpu.roll(d * tab[2], HEAD_DIM - half, 0)


def _head(t, h):
    return t[h * HEAD_DIM:(h + 1) * HEAD_DIM]


def _band(n, group):
    kj = lax.broadcasted_iota(jnp.int32, (2 * BLOCK, BLOCK), 0)
    qi = lax.broadcasted_iota(jnp.int32, (2 * BLOCK, BLOCK), 1)
    mask = (kj > qi) & (kj <= qi + BLOCK) & ((n > 0) | (kj >= BLOCK))
    return jnp.tile(mask, (1, group))


def _attn_specs(D, kvd, nb):
    cur = lambda n: jnp.minimum(n, nb - 1)
    prev = lambda n: jnp.maximum(cur(n) - 1, 0)
    return [pl.BlockSpec((BLOCK, D), lambda n: (cur(n), 0)),
            pl.BlockSpec((BLOCK, kvd), lambda n: (prev(n), 0)),
            pl.BlockSpec((BLOCK, kvd), lambda n: (cur(n), 0)),
            pl.BlockSpec((BLOCK, kvd), lambda n: (prev(n), 1)),
            pl.BlockSpec((BLOCK, kvd), lambda n: (cur(n), 1)),
            pl.BlockSpec((3, HEAD_DIM, BLOCK), lambda n: (0, 0, prev(n))),
            pl.BlockSpec((3, HEAD_DIM, BLOCK), lambda n: (0, 0, cur(n))),
            pl.BlockSpec(memory_space=pltpu.SMEM)]


def _attn_operands(q_ref, kp_ref, k_ref, vp_ref, v_ref, tp_ref, t_ref):
    flip = lambda ref: ref[...].astype(F32).T
    tab = t_ref[...]
    kt = jnp.concatenate([flip(kp_ref), flip(k_ref)], axis=1)
    vt = jnp.concatenate([flip(vp_ref), flip(v_ref)], axis=1)
    return flip(q_ref), kt, vt, tab, jnp.concatenate([tp_ref[...], tab], axis=2)


SCORE_SCALE = 1.0 / math.sqrt(HEAD_DIM)
HEADS_TOGETHER = 4


def _group_heads(t, first, count, tab=None):
    heads = [_head(t, first + g) for g in range(count)]
    if tab is not None:
        heads = [_rope(h, tab) * SCORE_SCALE for h in heads]
    return jnp.concatenate(heads, axis=1).astype(BF16)


def _sink_row(s_ref, first, count):
    which = lax.broadcasted_iota(jnp.int32, (1, count * BLOCK), 1) // BLOCK
    row = jnp.zeros((1, count * BLOCK), F32)
    for g in range(count):
        row = jnp.where(which == g, s_ref[0, first + g], row)
    return row


def _sum_keys(t):
    return _dot(jnp.ones((8, t.shape[0]), BF16), t)[0:1]


def _softmax(scores, sink, mask):
    s = jnp.where(mask, scores.astype(BF16), NEG)
    m = jnp.maximum(jnp.max(s, axis=0, keepdims=True).astype(F32), sink).astype(BF16)
    e = jnp.exp(s - m)
    m = m.astype(F32)
    return e, m, 1.0 / (_sum_keys(e) + jnp.exp(sink - m))


def _per_head(row, count):
    return [row[:, g * BLOCK:(g + 1) * BLOCK] for g in range(count)]


def attention_fwd(q, kv, tabs, sinks, *, name, ride=None):
    T, D = q.shape
    kvd = kv.shape[1] // 2
    heads = D // HEAD_DIM
    group = heads // N_KV_HEADS

    def body(q_ref, kp_ref, k_ref, vp_ref, v_ref, tp_ref, t_ref, s_ref, o_ref, stat_ref):
        gs = HEADS_TOGETHER
        mask = _band(pl.program_id(0), gs)
        qt, kt, vt, tab, tab2 = _attn_operands(q_ref, kp_ref, k_ref, vp_ref, v_ref, tp_ref, t_ref)
        firsts = [(j, first) for j in range(N_KV_HEADS) for first in range(j * group, (j + 1) * group, gs)]
        ks = [_rope(_head(kt, j), tab2).astype(BF16) for j in range(N_KV_HEADS)]
        scores = [_dot_tn(ks[j], _group_heads(qt, first, gs, tab)) for j, first in firsts]
        soft = [_softmax(s, _sink_row(s_ref, first, gs), mask) for s, (j, first) in zip(scores, firsts)]
        outs, ms, invs = [], [], []
        for (e, m, inv), (j, first) in zip(soft, firsts):
            o = _dot(_head(vt, j).astype(BF16), e) * inv
            outs += [o[:, g * BLOCK:(g + 1) * BLOCK] for g in range(gs)]
            ms += _per_head(m, gs)
            invs += _per_head(inv, gs)
        o_ref[...] = jnp.concatenate(outs, axis=0).T.astype(BF16)
        stat_ref[0] = jnp.concatenate(ms, axis=0)
        stat_ref[1] = jnp.concatenate(invs, axis=0)

    return _call(
        body, name=name, grid=(T // BLOCK,),
        in_specs=_attn_specs(D, kvd, T // BLOCK),
        out_specs=[pl.BlockSpec((BLOCK, D), lambda n: (n, 0)), pl.BlockSpec((2, heads, BLOCK), lambda n: (0, 0, n))],
        out_shape=[SDS((T, D), BF16), SDS((2, heads, T), F32)],
        semantics=("parallel",), args=(q, kv, kv, kv, kv, tabs, tabs, sinks), ride=ride)


def attention_bwd(q, kv, tabs, sinks, do, o, stats, *, name, ride=None):
    T, D = q.shape
    kvd = kv.shape[1] // 2
    heads = D // HEAD_DIM
    group = heads // N_KV_HEADS
    nb = T // BLOCK

    def body(q_ref, kp_ref, k_ref, vp_ref, v_ref, tp_ref, t_ref, s_ref, do_ref, o_ref, stat_ref,
             dq_ref, dkv_ref, ds_ref, carry):
        n = pl.program_id(0)

        @pl.when(n == 0)
        def _():
            carry[...] = jnp.zeros_like(carry)

        @pl.when(n < nb)
        def _():
            block(n, q_ref, kp_ref, k_ref, vp_ref, v_ref, tp_ref, t_ref, s_ref, do_ref, o_ref, stat_ref,
                  dq_ref, dkv_ref, ds_ref, carry)

        @pl.when(n == nb)
        def _():
            dkv_ref[...] = carry[...].astype(BF16)

    def block(n, q_ref, kp_ref, k_ref, vp_ref, v_ref, tp_ref, t_ref, s_ref, do_ref, o_ref, stat_ref,
              dq_ref, dkv_ref, ds_ref, carry):
        gs = HEADS_TOGETHER
        mask = _band(n, gs)
        qt, kt, vt, tab, tab2 = _attn_operands(q_ref, kp_ref, k_ref, vp_ref, v_ref, tp_ref, t_ref)
        dot = do_ref[...].astype(F32).T
        odo = o_ref[...].astype(F32).T * dot
        dl_all = jnp.concatenate([jnp.sum(_head(odo, h), axis=0, keepdims=True) for h in range(heads)], axis=0)
        m_all, inv_all = stat_ref[0], stat_ref[1]
        row = lambda t, first: jnp.concatenate([t[first + g:first + g + 1] for g in range(gs)], axis=1)
        lane = lax.broadcasted_iota(jnp.int32, (8, 128), 1)
        dsink = jnp.zeros((8, 128), F32)
        firsts = [(j, first) for j in range(N_KV_HEADS) for first in range(j * group, (j + 1) * group, gs)]
        ks = [_rope(_head(kt, j), tab2).astype(BF16) for j in range(N_KV_HEADS)]
        vs = [_head(vt, j).astype(BF16) for j in range(N_KV_HEADS)]
        qs = [_group_heads(qt, first, gs, tab) for _, first in firsts]
        dos = [_group_heads(dot, first, gs) for _, first in firsts]
        scores = [_dot_tn(ks[j], q) for q, (j, _) in zip(qs, firsts)]
        dps = [_dot_tn(vs[j], do) for do, (j, _) in zip(dos, firsts)]
        ps, dscs = [], []
        for s, dp, (j, first) in zip(scores, dps, firsts):
            m, inv, dl = row(m_all, first), row(inv_all, first), row(dl_all, first)
            e = jnp.exp(jnp.where(mask, s.astype(BF16), NEG) - m.astype(BF16))
            p = e * inv.astype(BF16)
            dscs.append(p * (dp.astype(BF16) - dl.astype(BF16)))
            ps.append(p)
            weight = jnp.exp(_sink_row(s_ref, first, gs) - m) * inv * dl
            for g in range(gs):
                dsink = dsink - jnp.where(lane == first + g, jnp.sum(weight[:, g * BLOCK:(g + 1) * BLOCK]), 0.0)
        dqs = []
        dks = [jnp.zeros((HEAD_DIM, 2 * BLOCK), F32) for _ in range(N_KV_HEADS)]
        dvs = [jnp.zeros((HEAD_DIM, 2 * BLOCK), F32) for _ in range(N_KV_HEADS)]
        for p, dsc, q, do, (j, _) in zip(ps, dscs, qs, dos, firsts):
            dq = _dot(ks[j], dsc) * SCORE_SCALE
            dqs += [_rope_t(dq[:, g * BLOCK:(g + 1) * BLOCK], tab) for g in range(gs)]
            dks[j] = dks[j] + _dot_nt(q, dsc)
            dvs[j] = dvs[j] + _dot_nt(do, p)
        dks = [_rope_t(dk, tab2) for dk in dks]
        dq_ref[...] = jnp.concatenate(dqs, axis=0).T.astype(BF16)
        dkv = jnp.concatenate(dks + dvs, axis=0)
        dkv_ref[...] = (carry[...] + dkv[:, :BLOCK].T).astype(BF16)
        carry[...] = dkv[:, BLOCK:].T
        _accumulate(ds_ref, n == 0, dsink)

    cur = lambda n: jnp.minimum(n, nb - 1)
    blk = lambda w: pl.BlockSpec((BLOCK, w), lambda n: (cur(n), 0))
    return _call(
        body, name=name, grid=(nb + 1,),
        in_specs=_attn_specs(D, kvd, nb) + [blk(D), blk(D), pl.BlockSpec((2, heads, BLOCK), lambda n: (0, 0, cur(n)))],
        out_specs=[blk(D), pl.BlockSpec((BLOCK, 2 * kvd), lambda n: (jnp.maximum(n - 1, 0), 0)),
                   pl.BlockSpec((8, 128), lambda n: (0, 0))],
        out_shape=[SDS((T, D), BF16), SDS((T, 2 * kvd), BF16), SDS((8, 128), F32)],
        scratch_shapes=[pltpu.VMEM((BLOCK, 2 * kvd), F32)],
        semantics=("arbitrary",), args=(q, kv, kv, kv, kv, tabs, tabs, sinks, do, o, stats), ride=ride)


def matmul_nt_normbwd(da, w, h_in, g, dh_out, *, name, ride=None, tm=ROW_TILE):
    T, D = h_in.shape
    S, _, K = da.shape

    def body(*refs):
        da_refs, w_refs = refs[:S], refs[S:2 * S]
        h_ref, g_ref, dh_ref, o_ref, dg_ref = refs[2 * S:]
        subs = _sub_tiles(tm)
        dns = []
        for rows in subs:
            dn = _dot_nt(da_refs[0][rows, :], w_refs[0][...])
            for s in range(1, S):
                dn = dn + _dot_nt(da_refs[s][rows, :], w_refs[s][...])
            dns.append(dn)
        dg = jnp.zeros((1, D), F32)
        for rows, dn in zip(subs, dns):
            dx, hh = _rmsnorm_bwd(h_ref[rows, :].astype(F32), g_ref[...], dn)
            o_ref[rows, :] = dh_ref[rows, :] + dx
            dg = dg + jnp.sum(dn * hh, axis=0, keepdims=True)
        _accumulate(dg_ref, pl.program_id(0) == 0, dg)

    row = pl.BlockSpec((tm, D), lambda i: (i, 0))
    vec = pl.BlockSpec((1, D), lambda i: (0, 0))
    part = lambda s: pl.BlockSpec((None, tm, K), lambda i: (s, i, 0))
    cols = lambda s: pl.BlockSpec((D, K), lambda i: (0, s), pipeline_mode=pl.Buffered(1))
    return _call(
        body, name=name, grid=(T // tm,),
        in_specs=[part(s) for s in range(S)] + [cols(s) for s in range(S)] + [row, vec, row],
        out_specs=[row, vec],
        out_shape=[SDS((T, D), F32), SDS((1, D), F32)],
        semantics=("arbitrary",), args=[da] * S + [w] * S + [h_in, g, dh_out], ride=ride)


def matmuls_nt_normbwd(das, ws, h_in, gs, dh_out, then, *, name, ride=None, tm=ROW_TILE):
    T, D = h_in.shape
    tm = min(tm, T)
    n = len(das)

    def body(*refs):
        da_refs, w_refs, g_refs = refs[:n], refs[n:2 * n], refs[2 * n:3 * n]
        h_ref, dh_ref, z_ref, g2_ref, w2_ref, o_ref = refs[3 * n:3 * n + 6]
        dg_refs, (dz_ref, dg2_ref, da_ref) = refs[3 * n + 6:4 * n + 6], refs[4 * n + 6:]
        first = pl.program_id(0) == 0
        subs = _sub_tiles(tm)
        dns = [[_dot_nt(da_ref_[rows, :], w_ref[...]) for da_ref_, w_ref in zip(da_refs, w_refs)] for rows in subs]
        dgs, dg2 = [jnp.zeros((1, D), F32) for _ in range(n)], jnp.zeros((1, D), F32)
        for rows, dn_sub in zip(subs, dns):
            hf = h_ref[rows, :].astype(F32)
            r = _rms_r(hf)
            hh = hf * r
            total = dh_ref[rows, :].astype(F32)
            for b, (dn, g_ref) in enumerate(zip(dn_sub, g_refs)):
                gd = g_ref[...] * dn
                total = total + r * (gd - hh * jnp.mean(hh * gd, axis=-1, keepdims=True))
                dgs[b] = dgs[b] + jnp.sum(dn * hh, axis=0, keepdims=True)
            o_ref[rows, :] = total.astype(STREAM)
            dz, zh = _rmsnorm_bwd(z_ref[rows, :].astype(F32), g2_ref[...], total)
            dz = dz.astype(BF16)
            dz_ref[rows, :] = dz
            dg2 = dg2 + jnp.sum(total * zh, axis=0, keepdims=True)
            da_ref[rows, :] = _dot_nt(dz, w2_ref[...]).astype(BF16)
        for dg_ref, dg in zip(dg_refs + (dg2_ref,), dgs + [dg2]):
            _accumulate(dg_ref, first, dg)

    row = pl.BlockSpec((tm, D), lambda i: (i, 0))
    vec = pl.BlockSpec((1, D), lambda i: (0, 0))
    then_in, then_out, then_shape = _then_specs(then, tm, T, D)
    return _call(
        body, name=name, grid=(T // tm,),
        in_specs=[pl.BlockSpec((tm, da.shape[1]), lambda i: (i, 0)) for da in das]
        + [pl.BlockSpec(w.shape, lambda i: (0, 0)) for w in ws] + [vec] * n + [row, row] + then_in,
        out_specs=[row] + [vec] * n + then_out,
        out_shape=[SDS((T, D), STREAM)] + [SDS((1, D), F32)] * n + then_shape,
        semantics=("arbitrary",), args=list(das) + list(ws) + list(gs) + [h_in, dh_out] + list(then), ride=ride)


def matmul_tn(a, b, *, tb, name, ride=None, ta=MXU_WIDTH):
    T, Ka = a.shape
    S, _, Nb = b.shape
    per = Nb // tb

    def body(a_ref, b_ref, o_ref):
        o_ref[...] = _dot_tn(a_ref[...], b_ref[...]).astype(BF16)

    out = _call(
        body, name=name, grid=(S * per, Ka // ta),
        in_specs=[pl.BlockSpec((T, ta), lambda j, i: (0, i)),
                  pl.BlockSpec((None, T, tb), lambda j, i: (j // per, 0, j % per))],
        out_specs=[pl.BlockSpec((ta, tb), lambda j, i: (i, j))],
        out_shape=[SDS((Ka, S * Nb), BF16)],
        semantics=("parallel", "parallel"), args=(a, b), ride=ride)
    return out[0] if ride is None else (out[0][0], out[1])


def conv_bwd(dy, bcx, conv_w, *, name, ride=None, tm=ROW_TILE):
    T, D = dy.shape
    nt = T // tm
    hb = tm // BF16_ROWS
    last = T // BF16_ROWS - 1

    def body(dy_ref, dyn_ref, b_ref, bn_ref, c_ref, u_ref, cp_ref, up_ref, cw_ref, o_ref, dw_ref):
        i = pl.program_id(0)
        c, u = c_ref[...].astype(F32), u_ref[...].astype(F32)
        cu = c * u
        cup = jnp.where(i == 0, 0.0, cp_ref[...].astype(F32) * up_ref[...].astype(F32))
        cu1, cu2 = _shift_down(cup, cu, 1), _shift_down(cup, cu, 2)
        w0, w1, w2 = cw_ref[0:1, :], cw_ref[1:2, :], cw_ref[2:3, :]
        dyf = dy_ref[...].astype(F32)
        o_ref[:, 0:D] = (dyf * (w0 * cu2 + w1 * cu1 + w2 * cu)).astype(BF16)
        dcv = dyf * b_ref[...].astype(F32)
        dcvn = jnp.where(i == nt - 1, 0.0, dyn_ref[...].astype(F32) * bn_ref[...].astype(F32))
        dcu = w2 * dcv + w1 * _shift_up(dcv, dcvn, 1) + w0 * _shift_up(dcv, dcvn, 2)
        o_ref[:, D:2 * D] = (dcu * u).astype(BF16)
        o_ref[:, 2 * D:3 * D] = (dcu * c).astype(BF16)
        row = lax.broadcasted_iota(jnp.int32, (8, D), 0)
        dw = jnp.zeros((8, D), F32)
        for tap, t in enumerate((cu2, cu1, cu)):
            dw = jnp.where(row == tap, jnp.sum(dcv * t, axis=0, keepdims=True), dw)
        _accumulate(dw_ref, i == 0, dw)

    tile = lambda col: pl.BlockSpec((tm, D), lambda i: (i, col))
    prev = lambda col: pl.BlockSpec((BF16_ROWS, D), lambda i: (jnp.maximum(i * hb - 1, 0), col))
    nxt = lambda col: pl.BlockSpec((BF16_ROWS, D), lambda i: (jnp.minimum((i + 1) * hb, last), col))
    return _call(
        body, name=name, grid=(nt,),
        in_specs=[tile(0), nxt(0), tile(0), nxt(0), tile(1), tile(2), prev(1), prev(2),
                  pl.BlockSpec((3, D), lambda i: (0, 0))],
        out_specs=[pl.BlockSpec((tm, 3 * D), lambda i: (i, 0)), pl.BlockSpec((8, D), lambda i: (0, 0))],
        out_shape=[SDS((T, 3 * D), BF16), SDS((8, D), F32)],
        semantics=("arbitrary",), args=(dy, dy, bcx, bcx, bcx, bcx, bcx, bcx, conv_w), ride=ride)


class NoTraffic:
    def ride(self, kernel_name):
        return None

    def landed(self, kernel_name, results, wts):
        pass

    def grad(self, key, value):
        pass


def local_step(x, target, wts, vec, traffic):
    T, D = x.shape
    tabs = rope_tables(T)
    small = {}

    def run(builder, *args, name, **kw):
        ride = traffic.ride(name)
        if ride is None:
            return builder(*args, name=name, **kw)
        out, extra = builder(*args, name=name, ride=ride, **kw)
        traffic.landed(name, extra, wts)
        return out

    bcx, xn1 = run(norm_matmul, x, vec["a_pre"], wts["w_in"], tn=3 * D, split=1, name="a_in")
    bcx = bcx[0]
    h1, z0, y0 = run(conv_mix_out, bcx, vec["conv_w"], wts["w_out"], vec["a_post"], x, name="a_out")
    gu0, act0, xt2 = run(norm_swiglu_in, h1, vec["ffn_pre0"], wts["gu0"], name="ffn0_in")
    h2, z1 = run(plain_mix_out, act0, wts["wd0"], vec["ffn_post0"], h1, name="ffn0_out")
    kvp, xkv, qp, xq = norm2_matmul(h2, [vec["kv_norm"], vec["b_pre"]], [wts["w_kv"], wts["w_q"]], name="kvq_in")
    attn, attn_stats = run(attention_fwd, qp, kvp, tabs, vec["sinks"], name="attn_fwd")
    h3, z2 = plain_mix_out(attn, wts["w_o"], vec["b_post"], h2, name="attn_out", tm=BIG_ROW_TILE)
    gu1, act1, xt3 = run(norm_swiglu_in, h3, vec["ffn_pre1"], wts["gu1"], name="ffn1_in")
    dy, dz3, small["ffn_post1"], dact1, loss = plain_mix_out(act1, wts["wd1"], vec["ffn_post1"], h3, name="ffn1_out",
                                                             target=target)

    def ffn_bwd(layer, dz, dact, gu, act, xt, h_in, dh, then, gu_first):
        tag = "ffn%d" % layer
        dwd = lambda: traffic.grad("wd%d" % layer, run(matmul_tn, act, dz[None], tb=D, name=tag + "_dwd"))
        dwgu = lambda: traffic.grad("gu%d" % layer, run(swiglu_bwd_tn, xt, dact, gu, name=tag + "_dwgu"))
        for step in ((dwgu, dwd) if gu_first else (dwd, dwgu)):
            step()
        dh_in, small["ffn_pre%d" % layer], dz_, dg_, da_ = run(
            swiglu_bwd_in, dact, gu, wts["gu%d" % layer], h_in, vec["ffn_pre%d" % layer], dh, then,
            name=tag + "_in_bwd")
        return dh_in, dz_, dg_, da_

    dh3, dz2, small["b_post"], dattn = ffn_bwd(1, dz3, dact1, gu1, act1, xt3, h3, dy,
                                               (z2, vec["b_post"], wts["w_o"]), gu_first=False)
    traffic.grad("w_o", matmul_tn(attn, dz2[None], tb=D, name="attn_dwo"))
    dq, dkv, small["sinks"] = run(attention_bwd, qp, kvp, tabs, vec["sinks"], dattn, attn, attn_stats,
                                  name="attn_bwd")
    traffic.grad("w_q", matmul_tn(xq, dq[None], tb=D, name="attn_dwq"))
    traffic.grad("w_kv", matmul_tn(xkv, dkv[None], tb=dkv.shape[1], name="attn_dwkv"))
    dh2, small["b_pre"], small["kv_norm"], dz1, small["ffn_post0"], dact0 = run(
        matmuls_nt_normbwd, [dq, dkv], [wts["w_q"], wts["w_kv"]], h2, [vec["b_pre"], vec["kv_norm"]], dh3,
        (z1, vec["ffn_post0"], wts["wd0"]), name="qkv_in_bwd")
    dh1, dz0, small["a_post"], dyc = ffn_bwd(0, dz1, dact0, gu0, act0, xt2, h1, dh2,
                                             (z0, vec["a_post"], wts["w_out"]), gu_first=True)
    traffic.grad("w_out", run(matmul_tn, y0, dz0[None], tb=D, name="a_dwout"))
    dbcx, small["conv_w"] = run(conv_bwd, dyc, bcx, vec["conv_w"], name="a_conv_bwd")
    traffic.grad("w_in", run(matmul_tn, xn1, dbcx[None], tb=3 * D // 2, name="a_dwin"))
    dx, small["a_pre"] = run(matmul_nt_normbwd, dbcx[None], wts["w_in"], x, vec["a_pre"], dh1, name="a_in_bwd")
    return loss, dx, small


SMALL_ROWS = 16
LOSS_ROW = 13

WHOLE = None
GATHER_PLAN = {"cast_rest": [("w_in", WHOLE)],
               "a_in": [("w_out", WHOLE), ("gu0", (0, 18))],
               "a_out": [("gu0", (18, 14))],
               "ffn0_in": [("wd0", WHOLE), ("w_kv", WHOLE), ("w_q", WHOLE), ("w_o", WHOLE)],
               "ffn0_out": [("gu1", (0, 16))],
               "attn_fwd": [("gu1", (16, 16))],
               "ffn1_in": [("wd1", WHOLE)]}
PAIR_PLAN = {"ffn1_dwgu": ["wd1"], "ffn1_in_bwd": ["gu1"], "attn_bwd": ["w_o"], "qkv_in_bwd": ["w_q", "w_kv"],
             "ffn0_dwd": ["gu0"], "ffn0_in_bwd": ["wd0"], "a_conv_bwd": ["w_out"]}
PAIR_ALONE = ["w_in"]
CHIP_PLAN = {"ffn1_in_bwd": [("wd1", WHOLE)], "attn_bwd": [("gu1", WHOLE)],
             "ffn0_dwgu": [("w_o", WHOLE), ("w_q", WHOLE), ("w_kv", WHOLE)],
             "ffn0_in_bwd": [("gu0", WHOLE)], "a_conv_bwd": [("wd0", (0, 12))],
             "a_dwin": [("wd0", (12, 10)), ("w_out", WHOLE)], "a_in_bwd": [("w_in", WHOLE)]}
HALF_PLAN = {"a_in_bwd": ["gu0", "gu1", "wd0", "wd1", "w_kv", "w_q", "w_o", "w_out"]}
GRAD_KIND = dict(KIND, gu0="split", gu1="split")


class Traffic:
    def __init__(self, wholes, quarter, c_arr, pc_arr):
        self.wholes, self.quarter, self.c_arr, self.pc_arr = wholes, quarter, c_arr, pc_arr
        self.views, self.sums, self.got = {}, {}, {}
        self.reduced = {}
        self.stages = {}

    def reduce(self, keys, name):
        return chip_reduce([self.sums[k] for k in keys], [self.got[k] for k in keys], [GRAD_KIND[k] for k in keys],
                           self.pc_arr, name=name)

    def ride(self, name, small=None):
        rides, stages = [], []
        if name in GATHER_PLAN:
            plan = GATHER_PLAN[name]
            rides.append(gather_ride([self.wholes[k] for k, _ in plan],
                                     [(KIND[k], self.quarter[k], part) for k, part in plan], small))
            stages.append(("gather", [k for k, _ in plan]))
        if name in CHIP_PLAN:
            plan = CHIP_PLAN[name]
            rides.append(chip_ride([self.sums[k] for k, _ in plan],
                                   [(GRAD_KIND[k], self.quarter[k], part) for k, part in plan],
                                   earlier=[self.got.get(k) for k, _ in plan]))
            stages.append(("chip", [k for k, _ in plan]))
        if name in PAIR_PLAN:
            keys = PAIR_PLAN[name]
            rides.append(pair_ride([self.views[k] for k in keys]))
            stages.append(("pair", keys))
        if name in HALF_PLAN:
            keys = HALF_PLAN[name]
            rides.append(half_ride(self.reduce(keys, "chip_reduce_early")))
            stages.append(("half", keys))
        self.stages[name] = stages
        return join(rides)

    def landed(self, name, results, wts):
        results = list(results)
        for stage, keys in self.stages[name]:
            mine, results = results[:len(keys)], results[len(keys):]
            if stage == "gather":
                for k, whole in zip(keys, mine):
                    self.wholes[k] = wts[k] = whole
            elif stage == "chip":
                self.got.update(zip(keys, mine))
            elif stage == "half":
                self.reduced.update(zip(keys, mine))
            else:
                for k, got in zip(keys, mine):
                    self.sums[k] = pair_add(self.views[k], got, self.c_arr, name="pair_add_" + k)

    def grad(self, key, value):
        r, ws = self.quarter[key]
        view = {"row": (N_CHIPS, 2, r // 2, ws), "col": (1, 2, r // 2, N_CHIPS * ws), "split": (2, 2, r // 2, 2 * ws)}
        self.views[key] = value.reshape(view[GRAD_KIND[key]])
        if key in PAIR_ALONE:
            (got,) = alone(pair_ride([self.views[key]]), name="pair_exchange_" + key)
            self.sums[key] = pair_add(self.views[key], got, self.c_arr, name="pair_add_" + key)


def kernel(x, a_pre_norm, a_w_in, a_conv_w, a_w_out, a_post_norm, ffn_pre_norm, ffn_w_gate_up, ffn_w_down, ffn_post_norm, kv_norm, w_kv, b_pre_norm, b_w_q, b_sinks, b_w_o, b_post_norm, loss_target, m_a_pre_norm, m_a_w_in, m_a_conv_w, m_a_w_out, m_a_post_norm, m_ffn_pre_norm, m_ffn_w_gate_up, m_ffn_w_down, m_ffn_post_norm, m_kv_norm, m_w_kv, m_b_pre_norm, m_b_w_q, m_b_sinks, m_b_w_o, m_b_post_norm, v_a_pre_norm, v_a_w_in, v_a_conv_w, v_a_w_out, v_a_post_norm, v_ffn_pre_norm, v_ffn_w_gate_up, v_ffn_w_down, v_ffn_post_norm, v_kv_norm, v_w_kv, v_b_pre_norm, v_b_w_q, v_b_sinks, v_b_w_o, v_b_post_norm):
    T, D = x.shape[1], x.shape[2]
    xi, yi, ci = _place()
    p = 2 * xi + yi
    p_arr = jnp.reshape(p, (1,)).astype(jnp.int32)
    c_arr = jnp.reshape(ci, (1,)).astype(jnp.int32)
    pc_arr = jnp.stack([p, ci]).astype(jnp.int32)
    me_arr = jnp.reshape(4 * xi + 2 * yi + ci, (1,)).astype(jnp.int32)
    qd = D // N_CHIPS

    big = {"w_in": (a_w_in, 0), "w_out": (a_w_out, 0), "gu0": (ffn_w_gate_up, 0), "gu1": (ffn_w_gate_up, 1),
           "wd0": (ffn_w_down, 0), "wd1": (ffn_w_down, 1), "w_kv": (w_kv[None], 0), "w_q": (b_w_q, 0),
           "w_o": (b_w_o, 0)}
    names = list(big)
    quarter = {k: w.shape[1:] for k, (w, _) in big.items()}
    source = lambda k: big[k] + (KIND[k],)
    traffic = Traffic(dict(zip(names[:1], cast_quarters([source(names[0])], p_arr, name="cast_first"))), quarter,
                      c_arr, pc_arr)
    small_shard = jnp.concatenate([a_pre_norm, a_post_norm, a_conv_w[0], jnp.zeros((3, qd), F32)], axis=0)
    wts = {}
    rest, (*landed, small_full) = cast_quarters([source(k) for k in names[1:]], p_arr, name="cast_rest",
                                                ride=traffic.ride("cast_rest", small_shard))
    traffic.wholes.update(zip(names[1:], rest))
    traffic.landed("cast_rest", landed, wts)
    rows = lambda k: jnp.transpose(small_full[:, k], (1, 0, 2)).reshape(-1, D)
    vec = {"a_pre": rows(slice(0, 1)), "a_post": rows(slice(1, 2)), "conv_w": rows(slice(2, 5)),
           "ffn_pre0": ffn_pre_norm[0:1], "ffn_pre1": ffn_pre_norm[1:2],
           "ffn_post0": ffn_post_norm[0:1], "ffn_post1": ffn_post_norm[1:2],
           "kv_norm": kv_norm[None], "b_pre": b_pre_norm, "b_post": b_post_norm, "sinks": b_sinks}

    loss, dx, small = local_step(x[0], loss_target[0], wts, vec, traffic)

    pad = lambda a: jnp.pad(a, ((0, 0), (0, D - a.shape[1])))
    small_block = jnp.concatenate(
        [small["a_pre"], small["a_post"], small["conv_w"][0:3], small["ffn_pre0"], small["ffn_pre1"],
         small["ffn_post0"], small["ffn_post1"], small["kv_norm"], small["b_pre"], small["b_post"],
         pad(small["sinks"][0:1]), pad(loss[0:1]), jnp.zeros((SMALL_ROWS - LOSS_ROW - 1, D), F32)], axis=0)
    late = [k for k in names if k not in traffic.reduced]
    *swapped, small_blocks = alone(join([half_ride(traffic.reduce(late, "chip_reduce_late")),
                                         chip_ride([], [], small_block)]), name="last_exchange")
    traffic.reduced.update(zip(late, swapped))
    grad = {k: traffic.reduced[k].reshape(quarter[k]) for k in names}
    small_sum = small_reduce(small_blocks, me_arr)

    out = {}
    out["a_w_in"] = adamw(a_w_in, [grad["w_in"]], m_a_w_in, v_a_w_in, name="adamw_a_w_in")
    out["a_w_out"] = adamw(a_w_out, [grad["w_out"]], m_a_w_out, v_a_w_out, name="adamw_a_w_out")
    out["ffn_w_gate_up"] = adamw(ffn_w_gate_up, [grad["gu0"], grad["gu1"]], m_ffn_w_gate_up, v_ffn_w_gate_up,
                                 name="adamw_ffn_w_gate_up")
    out["ffn_w_down"] = adamw(ffn_w_down, [grad["wd0"], grad["wd1"]], m_ffn_w_down, v_ffn_w_down,
                              name="adamw_ffn_w_down")
    out["w_kv"] = [o[0] for o in adamw(w_kv[None], [grad["w_kv"]], m_w_kv[None], v_w_kv[None], name="adamw_w_kv")]
    out["b_w_q"] = adamw(b_w_q, [grad["w_q"]], m_b_w_q, v_b_w_q, name="adamw_b_w_q")
    out["b_w_o"] = adamw(b_w_o, [grad["w_o"]], m_b_w_o, v_b_w_o, name="adamw_b_w_o")

    def pack(a_pre, a_post, conv, ffn_pre, ffn_post, kvn, b_pre, b_post, sinks):
        return jnp.concatenate([pad(a_pre), pad(a_post), pad(conv[0]), ffn_pre, ffn_post, kvn[None], b_pre, b_post,
                                pad(sinks), jnp.zeros((SMALL_ROWS - 13, D), F32)], axis=0)

    g_small = jnp.concatenate([pad(lax.dynamic_slice(small_sum, (0, p * qd), (5, qd))), small_sum[5:]], axis=0)
    w_small = pack(a_pre_norm, a_post_norm, a_conv_w, ffn_pre_norm, ffn_post_norm, kv_norm, b_pre_norm, b_post_norm,
                   b_sinks)
    m_small = pack(m_a_pre_norm, m_a_post_norm, m_a_conv_w, m_ffn_pre_norm, m_ffn_post_norm, m_kv_norm,
                   m_b_pre_norm, m_b_post_norm, m_b_sinks)
    v_small = pack(v_a_pre_norm, v_a_post_norm, v_a_conv_w, v_ffn_pre_norm, v_ffn_post_norm, v_kv_norm,
                   v_b_pre_norm, v_b_post_norm, v_b_sinks)
    packed = adamw(w_small[None], [g_small], m_small[None], v_small[None], name="adamw_small")
    ns = b_sinks.shape[1]
    unpack = lambda a: {"a_pre_norm": a[0:1, :qd], "a_post_norm": a[1:2, :qd], "a_conv_w": a[None, 2:5, :qd],
                        "ffn_pre_norm": a[5:7], "ffn_post_norm": a[7:9], "kv_norm": a[9], "b_pre_norm": a[10:11],
                        "b_post_norm": a[11:12], "b_sinks": a[12:13, :ns]}
    unpacked = [unpack(a[0]) for a in packed]
    for k in unpacked[0]:
        out[k] = [u[k] for u in unpacked]

    order = ["a_pre_norm", "a_w_in", "a_conv_w", "a_w_out", "a_post_norm", "ffn_pre_norm", "ffn_w_gate_up",
             "ffn_w_down", "ffn_post_norm", "kv_norm", "w_kv", "b_pre_norm", "b_w_q", "b_sinks", "b_w_o",
             "b_post_norm"]
    return (small_sum[LOSS_ROW, 0], dx[None], *[out[k][0] for k in order], *[out[k][1] for k in order],
            *[out[k][2] for k in order], *[out[k][3] for k in order])
```

```python
import math

import jax
import jax.numpy as jnp
from jax import lax
from jax.experimental import pallas as pl
from jax.experimental.pallas import tpu as pltpu

F32 = jnp.float32
BF16 = jnp.bfloat16
SDS = jax.ShapeDtypeStruct
MESH = pl.DeviceIdType.MESH
DMA = pltpu.SemaphoreType.DMA
HBM_SPEC = pl.BlockSpec(memory_space=pltpu.HBM)

EPS = 1e-6
NEG = -1e30
HEAD_DIM = 64
N_KV_HEADS = 4
BLOCK = 128
ROT_DIM = HEAD_DIM // 4
ROPE_THETA = 500000.0
N_CHIPS = 4

ADAM_LR = 0.001
ADAM_B1 = 0.9
ADAM_B2 = 0.999
ADAM_EPS = 1e-08
ADAM_WD = 0.01
ADAM_STEP = 10

VMEM_LIMIT_BYTES = 52 * 1024 * 1024
ROW_TILE = 512
BF16_ROWS = 16
STREAM = BF16
MXU_WIDTH = 256

KIND = {"w_in": "col", "gu0": "col", "gu1": "col", "w_out": "row", "wd0": "row", "wd1": "row", "w_kv": "row",
        "w_q": "row", "w_o": "row"}


def _params(*semantics):
    return pltpu.CompilerParams(dimension_semantics=semantics, vmem_limit_bytes=VMEM_LIMIT_BYTES)


def _row_tile(rows, limit, step=8):
    return max(t for t in range(step, limit + 1, step) if rows % t == 0)


def _place():
    return lax.axis_index("x"), lax.axis_index("y"), lax.axis_index("c")


def _other_chips(x, y):
    return [(1 - x, y), (x, 1 - y), (1 - x, 1 - y)]


def _remote(src, dst, send_sem, recv_sem, to):
    return pltpu.make_async_remote_copy(src_ref=src, dst_ref=dst, send_sem=send_sem, recv_sem=recv_sem,
                                        device_id=to, device_id_type=MESH)


def _full_shape(kind, quarter):
    r, ws = quarter
    return (N_CHIPS * r, ws) if kind == "row" else (r, N_CHIPS * ws)


def _rows_of(h, part):
    lo, n = (0, h) if part is None else (part[0] * BF16_ROWS, part[1] * BF16_ROWS)
    assert lo + n <= h, (h, part)
    return lo, n


def _half_of_quarter(ref, kind, quarter, part, q, half):
    r, ws = quarter
    h = r // 2
    lo, n = _rows_of(h, part)
    if kind == "row":
        return ref.at[pl.ds(pl.multiple_of(q * r + half * h + lo, BF16_ROWS), n)]
    return ref.at[pl.ds(pl.multiple_of(half * h + lo, BF16_ROWS), n), pl.ds(pl.multiple_of(q * ws, 128), ws)]


class Ride:
    def __init__(self, operands, out_shape, aliases, sems, make):
        self.operands, self.out_shape, self.aliases, self.sems, self.make = operands, out_shape, aliases, sems, make

    def stages(self, ins, outs, sems):
        made = self.make(ins, outs, sems)
        return made if len(made) == 3 else (made[0], None, made[1])


def join(rides):
    rides = [r for r in rides if r is not None]
    if len(rides) < 2:
        return rides[0] if rides else None
    aliases, at = {}, [0, 0, 0]
    cuts = []
    for r in rides:
        aliases.update({at[0] + i: at[1] + o for i, o in r.aliases.items()})
        cuts.append(tuple(at))
        at = [at[0] + len(r.operands), at[1] + len(r.out_shape), at[2] + len(r.sems)]
    cuts.append(tuple(at))

    def make(ins, outs, sem):
        made = [r.stages(ins[lo[0]:hi[0]], outs[lo[1]:hi[1]], sem[lo[2]:hi[2]]) for r, lo, hi in zip(rides, cuts, cuts[1:])]
        relays = [m[1] for m in made if m[1] is not None]

        def start():
            for m in made:
                m[0]()

        def relay():
            for r in relays:
                r()

        def finish():
            for m in made:
                m[2]()

        return (start, relay, finish) if relays else (start, finish)

    return Ride(sum((list(r.operands) for r in rides), []), sum((list(r.out_shape) for r in rides), []), aliases,
                sum((list(r.sems) for r in rides), []), make)


def _call(body, *, name, grid, in_specs, out_specs, out_shape, args, scratch_shapes=(), semantics=None, ride=None,
          prefetch=None):
    pre = 0 if prefetch is None else 1
    n_in, n_out, n_scr = len(in_specs), len(out_specs), len(scratch_shapes)
    r_in, r_out = (len(ride.operands), len(ride.out_shape)) if ride is not None else (0, 0)
    a, b = pre + n_in, pre + n_in + r_in
    c, d = b + n_out, b + n_out + r_out
    e = d + n_scr

    def riding(*refs):
        start, relay, finish = ride.stages(refs[a:b], refs[c:d], refs[e:])
        ids = [pl.program_id(k) for k in range(len(grid))]
        first, last = ids[0] == 0, ids[0] == grid[0] - 1
        for k in range(1, len(grid)):
            first, last = first & (ids[k] == 0), last & (ids[k] == grid[k] - 1)
        pl.when(first)(start)
        if relay is not None:
            pl.when(last)(relay)
        body(*refs[:a], *refs[b:c], *refs[d:e])
        pl.when(last)(finish)

    if ride is None:
        kernel_body, extra_in, extra_out, extra_shape, extra_scr, aliases = body, [], [], [], [], {}
        params = _params(*semantics)
    else:
        kernel_body, extra_in, extra_out = riding, [HBM_SPEC] * r_in, [HBM_SPEC] * r_out
        extra_shape, extra_scr = list(ride.out_shape), list(ride.sems)
        aliases = {pre + n_in + i: n_out + o for i, o in ride.aliases.items()}
        params = _params(*(("arbitrary",) * len(grid)))
    specs = dict(grid=grid, in_specs=list(in_specs) + extra_in, out_specs=list(out_specs) + extra_out,
                 scratch_shapes=list(scratch_shapes) + extra_scr)
    if prefetch is not None:
        specs = dict(grid_spec=pltpu.PrefetchScalarGridSpec(num_scalar_prefetch=1, **specs))
        args = (prefetch,) + tuple(args)
    outs = pl.pallas_call(kernel_body, name=name, out_shape=list(out_shape) + extra_shape,
                          input_output_aliases=aliases, compiler_params=params, **specs,
                          )(*args, *(ride.operands if ride is not None else ()))
    return outs if ride is None else (outs[:n_out], outs[n_out:])


def alone(ride, *, name):
    def body(*refs):
        n = len(ride.operands)
        stages = ride.stages(refs[:n], refs[n:n + len(ride.out_shape)], refs[n + len(ride.out_shape):])
        for stage in stages:
            if stage is not None:
                stage()

    return pl.pallas_call(
        body, name=name, in_specs=[HBM_SPEC] * len(ride.operands), out_specs=[HBM_SPEC] * len(ride.out_shape),
        out_shape=list(ride.out_shape), input_output_aliases=dict(ride.aliases), scratch_shapes=list(ride.sems),
    )(*ride.operands)


def gather_ride(wholes, metas, small=None):
    n = len(wholes)
    operands, out_shape = list(wholes), [SDS(s.shape, s.dtype) for s in wholes]
    sems = [DMA((n, 3)), DMA((n, 3)), DMA((n, 3)), DMA((n, 3))]
    if small is not None:
        operands.append(small)
        out_shape.append(SDS((N_CHIPS,) + small.shape, small.dtype))
        sems += [DMA((3,)), DMA((3,)), DMA(())]

    def make(ins, outs, sem):
        send1, recv1, send2, recv2 = sem[:4]
        x, y, c = _place()
        p = 2 * x + y
        chips = _other_chips(x, y)
        me, sibling = (x, y, c), (x, y, 1 - c)
        part = lambda t, q, half: _half_of_quarter(outs[t], *metas[t], q, half)
        first, landing, passing, arriving = [], [], [], []
        for j, (qx, qy) in enumerate(chips):
            q = 2 * qx + qy
            if small is not None:
                first.append(_remote(ins[n], outs[n].at[p], sem[4].at[j], sem[5].at[j], (qx, qy, c)))
                arriving.append(_remote(outs[n].at[q], outs[n].at[q], sem[4].at[j], sem[5].at[j], me))
            for t in range(n):
                first.append(_remote(part(t, p, c), part(t, p, c), send1.at[t, j], recv1.at[t, j], (qx, qy, c)))
                landed, theirs = part(t, q, c), part(t, q, 1 - c)
                landing.append(_remote(landed, landed, send1.at[t, j], recv1.at[t, j], me))
                passing.append(_remote(landed, landed, send2.at[t, j], recv2.at[t, j], sibling))
                arriving.append(_remote(theirs, theirs, send2.at[t, j], recv2.at[t, j], me))
        local = [] if small is None else [pltpu.make_async_copy(ins[n], outs[n].at[p], sem[6])]

        def start():
            for cp in local + first:
                cp.start()

        def relay():
            for got, cp in zip(landing, passing):
                got.wait_recv()
                cp.start()

        def finish():
            for cp in arriving:
                cp.wait_recv()
            for cp in first + passing:
                cp.wait_send()
            for cp in local:
                cp.wait()

        return start, relay, finish

    return Ride(operands, out_shape, {t: t for t in range(n)}, sems, make)


def chip_ride(sums, metas, small=None, earlier=None):
    n = len(sums)
    operands = list(sums)
    out_shape = [SDS((3, s.shape[1], quarter[1]), s.dtype) for s, (_, quarter, _) in zip(sums, metas)]
    sems = [DMA((n, 3)), DMA((n, 3))] if n else []
    if small is not None:
        operands.append(small)
        out_shape.append(SDS((8,) + small.shape, small.dtype))
        sems += [DMA((7,)), DMA((7,)), DMA(())]
    aliases = {}
    for t, buffer in enumerate(earlier or [None] * n):
        if buffer is not None:
            aliases[len(operands)] = t
            operands.append(buffer)

    def make(ins, outs, sem):
        x, y, c = _place()
        cps = []
        for j, (qx, qy) in enumerate(_other_chips(x, y)):
            q = 2 * qx + qy
            for t in range(n):
                kind, (_, ws), part = metas[t]
                rows = pl.ds(*_rows_of(ins[t].shape[1], part))
                if kind == "row":
                    src = ins[t].at[q, rows]
                elif kind == "col":
                    src = ins[t].at[0, rows, pl.ds(pl.multiple_of(q * ws, 128), ws)]
                else:
                    src = ins[t].at[q // 2, rows, pl.ds(pl.multiple_of((q % 2) * ws, 128), ws)]
                cps.append(_remote(src, outs[t].at[j, rows], sem[0].at[t, j], sem[1].at[t, j], (qx, qy, c)))
        local = []
        if small is not None:
            ssend, srecv, lsem = sem[2 * bool(n):2 * bool(n) + 3]
            local.append(pltpu.make_async_copy(ins[n], outs[n].at[0], lsem))
            for k in range(1, 8):
                peer = (x ^ (k >> 2 & 1), y ^ (k >> 1 & 1), c ^ (k & 1))
                cps.append(_remote(ins[n], outs[n].at[k], ssend.at[k - 1], srecv.at[k - 1], peer))

        def start():
            for cp in local + cps:
                cp.start()

        def finish():
            for cp in cps + local:
                cp.wait()

        return start, finish

    return Ride(operands, out_shape, aliases, sems, make)


def pair_ride(grads):
    n = len(grads)

    def make(ins, outs, sem):
        x, y, c = _place()
        cps = [_remote(ins[t].at[:, 1 - c], outs[t], sem[0].at[t], sem[1].at[t], (x, y, 1 - c)) for t in range(n)]

        def start():
            for cp in cps:
                cp.start()

        def finish():
            for cp in cps:
                cp.wait()

        return start, finish

    return Ride(list(grads), [SDS((g.shape[0],) + g.shape[2:], g.dtype) for g in grads], {}, [DMA((n,)), DMA((n,))],
                make)


def half_ride(quarters):
    n = len(quarters)

    def make(ins, outs, sem):
        x, y, c = _place()
        sends = [_remote(outs[t].at[c], outs[t].at[c], sem[0].at[t], sem[1].at[t], (x, y, 1 - c)) for t in range(n)]

        def start():
            for cp in sends:
                cp.start()

        def finish():
            for t in range(n):
                theirs = outs[t].at[1 - c]
                _remote(theirs, theirs, sem[0].at[t], sem[1].at[t], (x, y, c)).wait_recv()
            for cp in sends:
                cp.wait_send()

        return start, finish

    return Ride(list(quarters), [SDS(q.shape, q.dtype) for q in quarters], {t: t for t in range(n)},
                [DMA((n,)), DMA((n,))], make)


CAST_STEPS = 4


def cast_quarters(sources, p_arr, *, name, ride=None):
    n = len(sources)
    in_specs, out_specs, out_shape = [], [], []
    for w, layer, kind in sources:
        _, r, ws = w.shape
        tr = r // CAST_STEPS
        assert tr % BF16_ROWS == 0, w.shape
        in_specs.append(pl.BlockSpec((None, tr, ws), lambda i, p_ref, layer=layer: (layer, i, 0)))
        out_specs.append(pl.BlockSpec((tr, ws), (lambda i, p_ref: (p_ref[0] * CAST_STEPS + i, 0)) if kind == "row"
                                      else (lambda i, p_ref: (i, p_ref[0]))))
        out_shape.append(SDS(_full_shape(kind, (r, ws)), BF16))

    def body(p_ref, *refs):
        for w_ref, o_ref in zip(refs[:n], refs[n:]):
            o_ref[...] = w_ref[...].astype(BF16)

    return _call(body, name=name, grid=(CAST_STEPS,), in_specs=in_specs, out_specs=out_specs, out_shape=out_shape,
                 semantics=("parallel",), args=[w for w, _, _ in sources], ride=ride, prefetch=p_arr)


def pair_add(own, got, c_arr, *, name):
    A, _, h, W = own.shape
    th = _row_tile(h, max(BF16_ROWS, (3 << 19) // W), BF16_ROWS)

    def body(c_ref, a_ref, b_ref, o_ref):
        o_ref[...] = (a_ref[...].astype(F32) + b_ref[...].astype(F32)).astype(BF16)

    return pl.pallas_call(
        body, name=name,
        grid_spec=pltpu.PrefetchScalarGridSpec(
            num_scalar_prefetch=1, grid=(A, h // th),
            in_specs=[pl.BlockSpec((None, None, th, W), lambda q, i, c_ref: (q, c_ref[0], i, 0)),
                      pl.BlockSpec((None, th, W), lambda q, i, c_ref: (q, i, 0))],
            out_specs=pl.BlockSpec((None, th, W), lambda q, i, c_ref: (q, i, 0))),
        out_shape=SDS((A, h, W), BF16),
        compiler_params=_params("parallel", "parallel"),
    )(c_arr, own, got)


REDUCE_STEPS = 2


def chip_reduce(sums, got, kinds, pc_arr, *, name):
    n = len(sums)
    mine = {"row": lambda i, pc_ref: (pc_ref[0], i, 0), "col": lambda i, pc_ref: (0, i, pc_ref[0]),
            "split": lambda i, pc_ref: (pc_ref[0] // 2, i, pc_ref[0] % 2)}
    a_specs, b_specs, o_specs, out_shape = [], [], [], []
    for g, kind in zip(got, kinds):
        _, h, ws = g.shape
        th = h // REDUCE_STEPS
        assert th % BF16_ROWS == 0, g.shape
        a_specs.append(pl.BlockSpec((None, th, ws), mine[kind]))
        b_specs.append(pl.BlockSpec((3, th, ws), lambda i, pc_ref: (0, i, 0)))
        o_specs.append(pl.BlockSpec((None, th, ws), lambda i, pc_ref: (pc_ref[1], i, 0)))
        out_shape.append(SDS((2, h, ws), F32))

    def body(pc_ref, *refs):
        for a_ref, b_ref, o_ref in zip(refs[:n], refs[n:2 * n], refs[2 * n:]):
            o_ref[...] = ((a_ref[...].astype(F32) + b_ref[0].astype(F32)) + b_ref[1].astype(F32)) + b_ref[2].astype(F32)

    return _call(body, name=name, grid=(REDUCE_STEPS,), in_specs=a_specs + b_specs, out_specs=o_specs,
                 out_shape=out_shape, semantics=("parallel",), args=list(sums) + list(got), prefetch=pc_arr)


def small_reduce(blocks, me_arr):
    _, rows, D = blocks.shape

    def body(me_ref, b_ref, o_ref):
        me = me_ref[0]
        total = b_ref[me]
        for d in range(1, 8):
            total = total + b_ref[d ^ me]
        o_ref[...] = total

    return pl.pallas_call(
        body, name="small_reduce",
        grid_spec=pltpu.PrefetchScalarGridSpec(
            num_scalar_prefetch=1, grid=(1,),
            in_specs=[pl.BlockSpec((8, rows, D), lambda i, me_ref: (0, 0, 0))],
            out_specs=pl.BlockSpec((rows, D), lambda i, me_ref: (0, 0))),
        out_shape=SDS((rows, D), F32),
        compiler_params=_params("arbitrary"),
    )(me_arr, blocks)


def adamw(w, gs, m, v, *, name):
    L, r, cols = w.shape
    tr = _row_tile(r, 256)
    nt = r // tr

    def body(*refs):
        w_ref, m_ref, v_ref = refs[:3]
        g_refs = refs[3:3 + L]
        g_out, d_out, m_out, v_out = refs[3 + L:]
        layer = pl.program_id(0)
        g = g_refs[0][...]
        for l in range(1, L):
            g = jnp.where(layer == l, g_refs[l][...], g)
        m_new = ADAM_B1 * m_ref[...] + (1.0 - ADAM_B1) * g
        v_new = ADAM_B2 * v_ref[...] + (1.0 - ADAM_B2) * (g * g)
        m_hat = m_new / (1.0 - ADAM_B1 ** ADAM_STEP)
        v_hat = v_new / (1.0 - ADAM_B2 ** ADAM_STEP)
        g_out[...] = g
        m_out[...] = m_new
        v_out[...] = v_new
        d_out[...] = -ADAM_LR * (m_hat / (jnp.sqrt(v_hat) + ADAM_EPS) + ADAM_WD * w_ref[...])

    full = pl.BlockSpec((None, tr, cols), lambda l, i: (l, i, 0))
    g_spec = lambda l0: pl.BlockSpec((tr, cols), lambda l, i: (jnp.where(l == l0, i, jnp.where(l < l0, 0, nt - 1)), 0))
    return pl.pallas_call(
        body, name=name, grid=(L, nt),
        in_specs=[full, full, full] + [g_spec(l0) for l0 in range(L)],
        out_specs=[full] * 4,
        out_shape=[SDS(w.shape, F32)] * 4,
        compiler_params=_params("arbitrary", "arbitrary"),
    )(w, m, v, *gs)


def _rms_r(xf):
    return lax.rsqrt(jnp.mean(xf * xf, axis=-1, keepdims=True) + EPS)


def _rmsnorm_bwd(xf, g, dy):
    r = _rms_r(xf)
    xh = xf * r
    gd = g * dy
    return r * (gd - xh * jnp.mean(xh * gd, axis=-1, keepdims=True)), xh


def _dot(a, b):
    return jnp.dot(a, b, preferred_element_type=F32)


def _dot_nt(a, b):
    return lax.dot_general(a, b, (((1,), (1,)), ((), ())), preferred_element_type=F32)


def _dot_tn(a, b):
    return lax.dot_general(a, b, (((0,), (0,)), ((), ())), preferred_element_type=F32)


def _accumulate(ref, first, value):
    @pl.when(first)
    def _():
        ref[...] = value

    @pl.when(jnp.logical_not(first))
    def _():
        ref[...] += value


def norm_matmul(x, g, w, *, tn, split, name, ride=None, tm=ROW_TILE):
    T, D = x.shape
    N = w.shape[1]
    per = N // split // tn

    def body(x_ref, g_ref, w_ref, o_ref, xn_ref):
        @pl.when(pl.program_id(1) == 0)
        def _():
            xf = x_ref[...].astype(F32)
            xn_ref[...] = (xf * _rms_r(xf) * g_ref[...]).astype(BF16)

        o_ref[...] = _dot(xn_ref[...], w_ref[...]).astype(BF16)

    return _call(
        body, name=name, grid=(T // tm, N // tn),
        in_specs=[pl.BlockSpec((tm, D), lambda i, j: (i, 0)),
                  pl.BlockSpec((1, D), lambda i, j: (0, 0)),
                  pl.BlockSpec((D, tn), lambda i, j: (0, j))],
        out_specs=[pl.BlockSpec((None, tm, tn), lambda i, j: (j // per, i, j % per)),
                   pl.BlockSpec((tm, D), lambda i, j: (i, 0))],
        out_shape=[SDS((split, T, N // split), BF16), SDS((T, D), BF16)],
        semantics=("parallel", "arbitrary"), args=(x, g, w), ride=ride)


BIG_ROW_TILE = 1024


def norm2_matmul(x, gains, weights, *, name, ride=None, tm=BIG_ROW_TILE):
    T, D = x.shape
    tm = min(tm, T)
    n = len(gains)

    def body(x_ref, *refs):
        subs = _sub_tiles(tm)
        xhs = []
        for rows in subs:
            xf = x_ref[rows, :].astype(F32)
            xhs.append(xf * _rms_r(xf))
        for g_ref, w_ref, o_ref, xn_ref in zip(refs[:n], refs[n:2 * n], refs[2 * n::2], refs[2 * n + 1::2]):
            for rows, xh in zip(subs, xhs):
                xn = (xh * g_ref[...]).astype(BF16)
                xn_ref[rows, :] = xn
                o_ref[rows, :] = _dot(xn, w_ref[...]).astype(BF16)

    row = pl.BlockSpec((tm, D), lambda i: (i, 0))
    vec = pl.BlockSpec((1, D), lambda i: (0, 0))
    out_specs, out_shape = [], []
    for w in weights:
        out_specs += [pl.BlockSpec((tm, w.shape[1]), lambda i: (i, 0)), row]
        out_shape += [SDS((T, w.shape[1]), BF16), SDS((T, D), BF16)]
    return _call(
        body, name=name, grid=(T // tm,),
        in_specs=[row] + [vec] * n + [pl.BlockSpec(w.shape, lambda i: (0, 0)) for w in weights],
        out_specs=out_specs, out_shape=out_shape, semantics=("parallel",), args=[x] + list(gains) + list(weights),
        ride=ride)


def _shift_down(prev, cur, by):
    big = jnp.concatenate([prev, cur], axis=0)
    return pltpu.roll(big, by, 0)[prev.shape[0]:]


def _shift_up(cur, nxt, by):
    big = jnp.concatenate([cur, nxt], axis=0)
    return pltpu.roll(big, big.shape[0] - by, 0)[:cur.shape[0]]


def conv_mix_out(bcx, conv_w, w_out, g_post, res, *, name, ride=None, tm=ROW_TILE):
    T, D = res.shape
    hb = tm // BF16_ROWS

    def body(b_ref, c_ref, u_ref, cp_ref, up_ref, cw_ref, w_ref, g_ref, r_ref, h_ref, z_ref, y_ref):
        i = pl.program_id(0)
        cu = c_ref[...].astype(F32) * u_ref[...].astype(F32)
        cup = cp_ref[...].astype(F32) * up_ref[...].astype(F32)
        cup = jnp.where(i == 0, 0.0, cup)
        cv = (cw_ref[0:1, :] * _shift_down(cup, cu, 2) + cw_ref[1:2, :] * _shift_down(cup, cu, 1)
              + cw_ref[2:3, :] * cu)
        y = (b_ref[...].astype(F32) * cv).astype(BF16)
        y_ref[...] = y
        z = _dot(y, w_ref[...])
        z_ref[...] = z.astype(BF16)
        h_ref[...] = (r_ref[...] + z * _rms_r(z) * g_ref[...]).astype(STREAM)

    tile = lambda col: pl.BlockSpec((tm, D), lambda i: (i, col))
    halo = lambda col: pl.BlockSpec((BF16_ROWS, D), lambda i: (jnp.maximum(i * hb - 1, 0), col))
    row = pl.BlockSpec((tm, D), lambda i: (i, 0))
    return _call(
        body, name=name, grid=(T // tm,),
        in_specs=[tile(0), tile(1), tile(2), halo(1), halo(2),
                  pl.BlockSpec((3, D), lambda i: (0, 0)),
                  pl.BlockSpec((D, D), lambda i: (0, 0)),
                  pl.BlockSpec((1, D), lambda i: (0, 0)), row],
        out_specs=[row, row, row],
        out_shape=[SDS((T, D), STREAM), SDS((T, D), BF16), SDS((T, D), BF16)],
        semantics=("parallel",), args=(bcx, bcx, bcx, bcx, bcx, conv_w, w_out, g_post, res), ride=ride)


def _normbwd_then_nt(dh, zf, g_ref, w_ref, dz_ref, dg_ref, o_ref, first):
    dz, zh = _rmsnorm_bwd(zf, g_ref[...], dh)
    dz = dz.astype(BF16)
    dz_ref[...] = dz
    _accumulate(dg_ref, first, jnp.sum(dh * zh, axis=0, keepdims=True))
    o_ref[...] = _dot_nt(dz, w_ref[...]).astype(BF16)


def _then_specs(then, tm, T, D):
    z, g, w = then
    K = w.shape[0]
    row = pl.BlockSpec((tm, D), lambda i: (i, 0))
    vec = pl.BlockSpec((1, D), lambda i: (0, 0))
    in_specs = [row, vec, pl.BlockSpec((K, D), lambda i: (0, 0), pipeline_mode=pl.Buffered(1))]
    out_specs = [row, vec, pl.BlockSpec((tm, K), lambda i: (i, 0))]
    out_shape = [SDS((T, D), BF16), SDS((1, D), F32), SDS((T, K), BF16)]
    return in_specs, out_specs, out_shape


def plain_mix_out(a, w, g_post, res, *, name, target=None, ride=None, tm=ROW_TILE):
    T, D = res.shape
    tm = min(tm, T)
    K = a.shape[1]
    with_loss = target is not None

    def body(a_ref, w_ref, g_ref, r_ref, *rest):
        subs = _sub_tiles(tm)
        zs = [_dot(a_ref[rows, :], w_ref[...]) for rows in subs]
        if not with_loss:
            h_ref, z_ref = rest
            for rows, z in zip(subs, zs):
                h_ref[rows, :] = (r_ref[rows, :].astype(F32) + z * _rms_r(z) * g_ref[...]).astype(STREAM)
                z_ref[rows, :] = z.astype(BF16)
            return
        t_ref, h_ref, dz_ref, dg_ref, da_ref, loss_ref = rest
        first = pl.program_id(0) == 0
        loss, dg = jnp.zeros((), F32), jnp.zeros((1, D), F32)
        for rows, z in zip(subs, zs):
            diff = r_ref[rows, :].astype(F32) + z * _rms_r(z) * g_ref[...] - t_ref[rows, :]
            dh = diff * (1.0 / D)
            h_ref[rows, :] = dh.astype(STREAM)
            loss = loss + jnp.sum(diff * diff)
            dz, zh = _rmsnorm_bwd(z, g_ref[...], dh)
            dz = dz.astype(BF16)
            dz_ref[rows, :] = dz
            dg = dg + jnp.sum(dh * zh, axis=0, keepdims=True)
            da_ref[rows, :] = _dot_nt(dz, w_ref[...]).astype(BF16)
        _accumulate(loss_ref, first, jnp.full(loss_ref.shape, 0.5 / D, F32) * loss)
        _accumulate(dg_ref, first, dg)

    row = pl.BlockSpec((tm, D), lambda i: (i, 0))
    vec = pl.BlockSpec((1, D), lambda i: (0, 0))
    in_specs = [pl.BlockSpec((tm, K), lambda i: (i, 0)), pl.BlockSpec((K, D), lambda i: (0, 0)), vec, row]
    if with_loss:
        in_specs.append(row)
        out_specs = [row, row, vec, pl.BlockSpec((tm, K), lambda i: (i, 0)), pl.BlockSpec((8, 128), lambda i: (0, 0))]
        out_shape = [SDS((T, D), STREAM), SDS((T, D), BF16), SDS((1, D), F32), SDS((T, K), BF16), SDS((8, 128), F32)]
    else:
        out_specs, out_shape = [row, row], [SDS((T, D), STREAM), SDS((T, D), BF16)]
    return _call(
        body, name=name, grid=(T // tm,), in_specs=in_specs, out_specs=out_specs, out_shape=out_shape,
        semantics=("arbitrary",), args=(a, w, g_post, res) + ((target,) if with_loss else ()), ride=ride)


def _silu_grads(d, g, u):
    sg = jax.nn.sigmoid(g)
    return d * u * (sg * (1.0 + g * (1.0 - sg))), d * (g * sg)


def _sub_tiles(tm):
    return [pl.ds(k, min(MXU_WIDTH, tm)) for k in range(0, tm, MXU_WIDTH)]


def norm_swiglu_in(x, g, w, *, name, ride=None, tm=ROW_TILE):
    T, D = x.shape
    F = w.shape[1] // 2

    def body(x_ref, g_ref, wg_ref, wu_ref, gu_ref, a_ref, xt_ref):
        subs = _sub_tiles(tm)
        xns = []
        for rows in subs:
            xf = x_ref[rows, :].astype(F32)
            xns.append(xf * _rms_r(xf) * g_ref[...])
        xbs = [xn.astype(BF16) for xn in xns]
        gates = [_dot(xb, wg_ref[...]).astype(BF16) for xb in xbs]
        ups = [_dot(xb, wu_ref[...]).astype(BF16) for xb in xbs]
        for rows, gate, up in zip(subs, gates, ups):
            gu_ref[0, rows, :] = gate
            gu_ref[1, rows, :] = up
            a_ref[rows, :] = gate * jax.nn.sigmoid(gate) * up
        for rows, xn in zip(subs, xns):
            xt_ref[:, rows] = xn.T.astype(BF16)

    half = lambda s: pl.BlockSpec((D, F), lambda i: (0, s), pipeline_mode=pl.Buffered(1))
    return _call(
        body, name=name, grid=(T // tm,),
        in_specs=[pl.BlockSpec((tm, D), lambda i: (i, 0)), pl.BlockSpec((1, D), lambda i: (0, 0)), half(0), half(1)],
        out_specs=[pl.BlockSpec((2, tm, F), lambda i: (0, i, 0)), pl.BlockSpec((tm, F), lambda i: (i, 0)),
                   pl.BlockSpec((D, tm), lambda i: (0, i))],
        out_shape=[SDS((2, T, F), BF16), SDS((T, F), BF16), SDS((D, T), BF16)],
        semantics=("parallel",), args=(x, g, w, w), ride=ride)


def swiglu_bwd_tn(xt, dact, gu, *, name, ride=None, tb=MXU_WIDTH):
    D, T = xt.shape
    F = dact.shape[1]

    def body(xt_ref, d_ref, g_ref, u_ref, o_ref):
        dg, du = _silu_grads(d_ref[...], g_ref[...], u_ref[...])
        o_ref[0] = _dot(xt_ref[...], dg).astype(BF16)
        o_ref[1] = _dot(xt_ref[...], du).astype(BF16)

    col = lambda s: pl.BlockSpec((None, T, tb), lambda j: (s, 0, j))
    out = _call(
        body, name=name, grid=(F // tb,),
        in_specs=[pl.BlockSpec((D, T), lambda j: (0, 0), pipeline_mode=pl.Buffered(1)),
                  pl.BlockSpec((T, tb), lambda j: (0, j)), col(0), col(1)],
        out_specs=[pl.BlockSpec((2, D, tb), lambda j: (0, 0, j))],
        out_shape=[SDS((2, D, F), BF16)],
        semantics=("parallel",), args=(xt, dact, gu, gu), ride=ride)
    return out[0] if ride is None else (out[0][0], out[1])


def swiglu_bwd_in(dact, gu, w, h_in, g, dh_out, then, *, name, ride=None, tm=ROW_TILE):
    T, D = h_in.shape
    F = dact.shape[1]

    def body(d_ref, gg_ref, uu_ref, wg_ref, wu_ref, h_ref, g_ref, dh_ref, z_ref, g2_ref, w2_ref,
             o_ref, dg_ref, dz_ref, dg2_ref, da_ref):
        first = pl.program_id(0) == 0
        subs = _sub_tiles(tm)
        dns = []
        for rows in subs:
            dgate, dup = _silu_grads(d_ref[rows, :], gg_ref[rows, :], uu_ref[rows, :])
            dns.append(_dot_nt(dgate, wg_ref[...]) + _dot_nt(dup, wu_ref[...]))
        dg, dg2 = jnp.zeros((1, D), F32), jnp.zeros((1, D), F32)
        for rows, dn in zip(subs, dns):
            dx, hh = _rmsnorm_bwd(h_ref[rows, :].astype(F32), g_ref[...], dn)
            dh_in = dh_ref[rows, :] + dx
            o_ref[rows, :] = dh_in.astype(STREAM)
            dg = dg + jnp.sum(dn * hh, axis=0, keepdims=True)
            dz, zh = _rmsnorm_bwd(z_ref[rows, :].astype(F32), g2_ref[...], dh_in)
            dz = dz.astype(BF16)
            dz_ref[rows, :] = dz
            dg2 = dg2 + jnp.sum(dh_in * zh, axis=0, keepdims=True)
            da_ref[rows, :] = _dot_nt(dz, w2_ref[...]).astype(BF16)
        _accumulate(dg_ref, first, dg)
        _accumulate(dg2_ref, first, dg2)

    row = pl.BlockSpec((tm, D), lambda i: (i, 0))
    vec = pl.BlockSpec((1, D), lambda i: (0, 0))
    part = lambda s: pl.BlockSpec((None, tm, F), lambda i: (s, i, 0))
    half = lambda s: pl.BlockSpec((D, F), lambda i: (0, s), pipeline_mode=pl.Buffered(1))
    then_in, then_out, then_shape = _then_specs(then, tm, T, D)
    return _call(
        body, name=name, grid=(T // tm,),
        in_specs=[pl.BlockSpec((tm, F), lambda i: (i, 0)), part(0), part(1), half(0), half(1), row, vec, row] + then_in,
        out_specs=[row, vec] + then_out,
        out_shape=[SDS((T, D), STREAM), SDS((1, D), F32)] + then_shape,
        semantics=("arbitrary",), args=(dact, gu, gu, w, w, h_in, g, dh_out) + tuple(then), ride=ride)


def rope_tables(T):
    half = ROT_DIM // 2
    inv_freq = ROPE_THETA ** (-jnp.arange(0, ROT_DIM, 2, dtype=F32) / ROT_DIM)
    ang = (jnp.arange(T, dtype=F32)[:, None] * inv_freq[None, :]).T
    cos, sin = jnp.cos(ang), jnp.sin(ang)
    rest = HEAD_DIM - ROT_DIM
    one, zero = jnp.ones((rest, T), F32), jnp.zeros((rest, T), F32)
    zh = jnp.zeros((half, T), F32)
    fac = jnp.concatenate([cos, cos, one], axis=0)
    up = jnp.concatenate([-sin, zh, zero], axis=0)
    down = jnp.concatenate([zh, sin, zero], axis=0)
    return jnp.stack([fac, up, down])


def _rope(t, tab):
    half = ROT_DIM // 2
    return t * tab[0] + pltpu.roll(t, HEAD_DIM - half, 0) * tab[1] + pltpu.roll(t, half, 0) * tab[2]


def _rope_t(d, tab):
    half = ROT_DIM // 2
    return d * tab[0] + pltpu.roll(d * tab[1], half, 0) + pltpu.roll(d * tab[2], HEAD_DIM - half, 0)


def _head(t, h):
    return t[h * HEAD_DIM:(h + 1) * HEAD_DIM]


def _band(n, group):
    kj = lax.broadcasted_iota(jnp.int32, (2 * BLOCK, BLOCK), 0)
    qi = lax.broadcasted_iota(jnp.int32, (2 * BLOCK, BLOCK), 1)
    mask = (kj > qi) & (kj <= qi + BLOCK) & ((n > 0) | (kj >= BLOCK))
    return jnp.tile(mask, (1, group))


def _attn_specs(D, kvd, nb):
    cur = lambda n: jnp.minimum(n, nb - 1)
    prev = lambda n: jnp.maximum(cur(n) - 1, 0)
    return [pl.BlockSpec((BLOCK, D), lambda n: (cur(n), 0)),
            pl.BlockSpec((BLOCK, kvd), lambda n: (prev(n), 0)),
            pl.BlockSpec((BLOCK, kvd), lambda n: (cur(n), 0)),
            pl.BlockSpec((BLOCK, kvd), lambda n: (prev(n), 1)),
            pl.BlockSpec((BLOCK, kvd), lambda n: (cur(n), 1)),
            pl.BlockSpec((3, HEAD_DIM, BLOCK), lambda n: (0, 0, prev(n))),
            pl.BlockSpec((3, HEAD_DIM, BLOCK), lambda n: (0, 0, cur(n))),
            pl.BlockSpec(memory_space=pltpu.SMEM)]


def _attn_operands(q_ref, kp_ref, k_ref, vp_ref, v_ref, tp_ref, t_ref):
    flip = lambda ref: ref[...].astype(F32).T
    tab = t_ref[...]
    kt = jnp.concatenate([flip(kp_ref), flip(k_ref)], axis=1)
    vt = jnp.concatenate([flip(vp_ref), flip(v_ref)], axis=1)
    return flip(q_ref), kt, vt, tab, jnp.concatenate([tp_ref[...], tab], axis=2)


SCORE_SCALE = 1.0 / math.sqrt(HEAD_DIM)
HEADS_TOGETHER = 4


def _group_heads(t, first, count, tab=None):
    heads = [_head(t, first + g) for g in range(count)]
    if tab is not None:
        heads = [_rope(h, tab) * SCORE_SCALE for h in heads]
    return jnp.concatenate(heads, axis=1).astype(BF16)


def _sink_row(s_ref, first, count):
    which = lax.broadcasted_iota(jnp.int32, (1, count * BLOCK), 1) // BLOCK
    row = jnp.zeros((1, count * BLOCK), F32)
    for g in range(count):
        row = jnp.where(which == g, s_ref[0, first + g], row)
    return row


def _sum_keys(t):
    return _dot(jnp.ones((8, t.shape[0]), BF16), t)[0:1]


def _softmax(scores, sink, mask):
    s = jnp.where(mask, scores.astype(BF16), NEG)
    m = jnp.maximum(jnp.max(s, axis=0, keepdims=True).astype(F32), sink).astype(BF16)
    e = jnp.exp(s - m)
    m = m.astype(F32)
    return e, m, 1.0 / (_sum_keys(e) + jnp.exp(sink - m))


def _per_head(row, count):
    return [row[:, g * BLOCK:(g + 1) * BLOCK] for g in range(count)]


def attention_fwd(q, kv, tabs, sinks, *, name, ride=None):
    T, D = q.shape
    kvd = kv.shape[1] // 2
    heads = D // HEAD_DIM
    group = heads // N_KV_HEADS

    def body(q_ref, kp_ref, k_ref, vp_ref, v_ref, tp_ref, t_ref, s_ref, o_ref, stat_ref):
        gs = HEADS_TOGETHER
        mask = _band(pl.program_id(0), gs)
        qt, kt, vt, tab, tab2 = _attn_operands(q_ref, kp_ref, k_ref, vp_ref, v_ref, tp_ref, t_ref)
        firsts = [(j, first) for j in range(N_KV_HEADS) for first in range(j * group, (j + 1) * group, gs)]
        ks = [_rope(_head(kt, j), tab2).astype(BF16) for j in range(N_KV_HEADS)]
        scores = [_dot_tn(ks[j], _group_heads(qt, first, gs, tab)) for j, first in firsts]
        soft = [_softmax(s, _sink_row(s_ref, first, gs), mask) for s, (j, first) in zip(scores, firsts)]
        outs, ms, invs = [], [], []
        for (e, m, inv), (j, first) in zip(soft, firsts):
            o = _dot(_head(vt, j).astype(BF16), e) * inv
            outs += [o[:, g * BLOCK:(g + 1) * BLOCK] for g in range(gs)]
            ms += _per_head(m, gs)
            invs += _per_head(inv, gs)
        o_ref[...] = jnp.concatenate(outs, axis=0).T.astype(BF16)
        stat_ref[0] = jnp.concatenate(ms, axis=0)
        stat_ref[1] = jnp.concatenate(invs, axis=0)

    return _call(
        body, name=name, grid=(T // BLOCK,),
        in_specs=_attn_specs(D, kvd, T // BLOCK),
        out_specs=[pl.BlockSpec((BLOCK, D), lambda n: (n, 0)), pl.BlockSpec((2, heads, BLOCK), lambda n: (0, 0, n))],
        out_shape=[SDS((T, D), BF16), SDS((2, heads, T), F32)],
        semantics=("parallel",), args=(q, kv, kv, kv, kv, tabs, tabs, sinks), ride=ride)


def attention_bwd(q, kv, tabs, sinks, do, o, stats, *, name, ride=None):
    T, D = q.shape
    kvd = kv.shape[1] // 2
    heads = D // HEAD_DIM
    group = heads // N_KV_HEADS
    nb = T // BLOCK

    def body(q_ref, kp_ref, k_ref, vp_ref, v_ref, tp_ref, t_ref, s_ref, do_ref, o_ref, stat_ref,
             dq_ref, dkv_ref, ds_ref, carry):
        n = pl.program_id(0)

        @pl.when(n == 0)
        def _():
            carry[...] = jnp.zeros_like(carry)

        @pl.when(n < nb)
        def _():
            block(n, q_ref, kp_ref, k_ref, vp_ref, v_ref, tp_ref, t_ref, s_ref, do_ref, o_ref, stat_ref,
                  dq_ref, dkv_ref, ds_ref, carry)

        @pl.when(n == nb)
        def _():
            dkv_ref[...] = carry[...].astype(BF16)

    def block(n, q_ref, kp_ref, k_ref, vp_ref, v_ref, tp_ref, t_ref, s_ref, do_ref, o_ref, stat_ref,
              dq_ref, dkv_ref, ds_ref, carry):
        gs = HEADS_TOGETHER
        mask = _band(n, gs)
        qt, kt, vt, tab, tab2 = _attn_operands(q_ref, kp_ref, k_ref, vp_ref, v_ref, tp_ref, t_ref)
        dot = do_ref[...].astype(F32).T
        odo = o_ref[...].astype(F32).T * dot
        dl_all = jnp.concatenate([jnp.sum(_head(odo, h), axis=0, keepdims=True) for h in range(heads)], axis=0)
        m_all, inv_all = stat_ref[0], stat_ref[1]
        row = lambda t, first: jnp.concatenate([t[first + g:first + g + 1] for g in range(gs)], axis=1)
        lane = lax.broadcasted_iota(jnp.int32, (8, 128), 1)
        dsink = jnp.zeros((8, 128), F32)
        firsts = [(j, first) for j in range(N_KV_HEADS) for first in range(j * group, (j + 1) * group, gs)]
        ks = [_rope(_head(kt, j), tab2).astype(BF16) for j in range(N_KV_HEADS)]
        vs = [_head(vt, j).astype(BF16) for j in range(N_KV_HEADS)]
        qs = [_group_heads(qt, first, gs, tab) for _, first in firsts]
        dos = [_group_heads(dot, first, gs) for _, first in firsts]
        scores = [_dot_tn(ks[j], q) for q, (j, _) in zip(qs, firsts)]
        dps = [_dot_tn(vs[j], do) for do, (j, _) in zip(dos, firsts)]
        ps, dscs = [], []
        for s, dp, (j, first) in zip(scores, dps, firsts):
            m, inv, dl = row(m_all, first), row(inv_all, first), row(dl_all, first)
            e = jnp.exp(jnp.where(mask, s.astype(BF16), NEG) - m.astype(BF16))
            p = e * inv.astype(BF16)
            dscs.append(p * (dp.astype(BF16) - dl.astype(BF16)))
            ps.append(p)
            weight = jnp.exp(_sink_row(s_ref, first, gs) - m) * inv * dl
            for g in range(gs):
                dsink = dsink - jnp.where(lane == first + g, jnp.sum(weight[:, g * BLOCK:(g + 1) * BLOCK]), 0.0)
        dqs = []
        dks = [jnp.zeros((HEAD_DIM, 2 * BLOCK), F32) for _ in range(N_KV_HEADS)]
        dvs = [jnp.zeros((HEAD_DIM, 2 * BLOCK), F32) for _ in range(N_KV_HEADS)]
        for p, dsc, q, do, (j, _) in zip(ps, dscs, qs, dos, firsts):
            dq = _dot(ks[j], dsc) * SCORE_SCALE
            dqs += [_rope_t(dq[:, g * BLOCK:(g + 1) * BLOCK], tab) for g in range(gs)]
            dks[j] = dks[j] + _dot_nt(q, dsc)
            dvs[j] = dvs[j] + _dot_nt(do, p)
        dks = [_rope_t(dk, tab2) for dk in dks]
        dq_ref[...] = jnp.concatenate(dqs, axis=0).T.astype(BF16)
        dkv = jnp.concatenate(dks + dvs, axis=0)
        dkv_ref[...] = (carry[...] + dkv[:, :BLOCK].T).astype(BF16)
        carry[...] = dkv[:, BLOCK:].T
        _accumulate(ds_ref, n == 0, dsink)

    cur = lambda n: jnp.minimum(n, nb - 1)
    blk = lambda w: pl.BlockSpec((BLOCK, w), lambda n: (cur(n), 0))
    return _call(
        body, name=name, grid=(nb + 1,),
        in_specs=_attn_specs(D, kvd, nb) + [blk(D), blk(D), pl.BlockSpec((2, heads, BLOCK), lambda n: (0, 0, cur(n)))],
        out_specs=[blk(D), pl.BlockSpec((BLOCK, 2 * kvd), lambda n: (jnp.maximum(n - 1, 0), 0)),
                   pl.BlockSpec((8, 128), lambda n: (0, 0))],
        out_shape=[SDS((T, D), BF16), SDS((T, 2 * kvd), BF16), SDS((8, 128), F32)],
        scratch_shapes=[pltpu.VMEM((BLOCK, 2 * kvd), F32)],
        semantics=("arbitrary",), args=(q, kv, kv, kv, kv, tabs, tabs, sinks, do, o, stats), ride=ride)


def matmul_nt_normbwd(da, w, h_in, g, dh_out, *, name, ride=None, tm=ROW_TILE):
    T, D = h_in.shape
    S, _, K = da.shape

    def body(*refs):
        da_refs, w_refs = refs[:S], refs[S:2 * S]
        h_ref, g_ref, dh_ref, o_ref, dg_ref = refs[2 * S:]
        subs = _sub_tiles(tm)
        dns = []
        for rows in subs:
            dn = _dot_nt(da_refs[0][rows, :], w_refs[0][...])
            for s in range(1, S):
                dn = dn + _dot_nt(da_refs[s][rows, :], w_refs[s][...])
            dns.append(dn)
        dg = jnp.zeros((1, D), F32)
        for rows, dn in zip(subs, dns):
            dx, hh = _rmsnorm_bwd(h_ref[rows, :].astype(F32), g_ref[...], dn)
            o_ref[rows, :] = dh_ref[rows, :] + dx
            dg = dg + jnp.sum(dn * hh, axis=0, keepdims=True)
        _accumulate(dg_ref, pl.program_id(0) == 0, dg)

    row = pl.BlockSpec((tm, D), lambda i: (i, 0))
    vec = pl.BlockSpec((1, D), lambda i: (0, 0))
    part = lambda s: pl.BlockSpec((None, tm, K), lambda i: (s, i, 0))
    cols = lambda s: pl.BlockSpec((D, K), lambda i: (0, s), pipeline_mode=pl.Buffered(1))
    return _call(
        body, name=name, grid=(T // tm,),
        in_specs=[part(s) for s in range(S)] + [cols(s) for s in range(S)] + [row, vec, row],
        out_specs=[row, vec],
        out_shape=[SDS((T, D), F32), SDS((1, D), F32)],
        semantics=("arbitrary",), args=[da] * S + [w] * S + [h_in, g, dh_out], ride=ride)


def matmuls_nt_normbwd(das, ws, h_in, gs, dh_out, then, *, name, ride=None, tm=ROW_TILE):
    T, D = h_in.shape
    tm = min(tm, T)
    n = len(das)

    def body(*refs):
        da_refs, w_refs, g_refs = refs[:n], refs[n:2 * n], refs[2 * n:3 * n]
        h_ref, dh_ref, z_ref, g2_ref, w2_ref, o_ref = refs[3 * n:3 * n + 6]
        dg_refs, (dz_ref, dg2_ref, da_ref) = refs[3 * n + 6:4 * n + 6], refs[4 * n + 6:]
        first = pl.program_id(0) == 0
        subs = _sub_tiles(tm)
        dns = [[_dot_nt(da_ref_[rows, :], w_ref[...]) for da_ref_, w_ref in zip(da_refs, w_refs)] for rows in subs]
        dgs, dg2 = [jnp.zeros((1, D), F32) for _ in range(n)], jnp.zeros((1, D), F32)
        for rows, dn_sub in zip(subs, dns):
            hf = h_ref[rows, :].astype(F32)
            r = _rms_r(hf)
            hh = hf * r
            total = dh_ref[rows, :].astype(F32)
            for b, (dn, g_ref) in enumerate(zip(dn_sub, g_refs)):
                gd = g_ref[...] * dn
                total = total + r * (gd - hh * jnp.mean(hh * gd, axis=-1, keepdims=True))
                dgs[b] = dgs[b] + jnp.sum(dn * hh, axis=0, keepdims=True)
            o_ref[rows, :] = total.astype(STREAM)
            dz, zh = _rmsnorm_bwd(z_ref[rows, :].astype(F32), g2_ref[...], total)
            dz = dz.astype(BF16)
            dz_ref[rows, :] = dz
            dg2 = dg2 + jnp.sum(total * zh, axis=0, keepdims=True)
            da_ref[rows, :] = _dot_nt(dz, w2_ref[...]).astype(BF16)
        for dg_ref, dg in zip(dg_refs + (dg2_ref,), dgs + [dg2]):
            _accumulate(dg_ref, first, dg)

    row = pl.BlockSpec((tm, D), lambda i: (i, 0))
    vec = pl.BlockSpec((1, D), lambda i: (0, 0))
    then_in, then_out, then_shape = _then_specs(then, tm, T, D)
    return _call(
        body, name=name, grid=(T // tm,),
        in_specs=[pl.BlockSpec((tm, da.shape[1]), lambda i: (i, 0)) for da in das]
        + [pl.BlockSpec(w.shape, lambda i: (0, 0)) for w in ws] + [vec] * n + [row, row] + then_in,
        out_specs=[row] + [vec] * n + then_out,
        out_shape=[SDS((T, D), STREAM)] + [SDS((1, D), F32)] * n + then_shape,
        semantics=("arbitrary",), args=list(das) + list(ws) + list(gs) + [h_in, dh_out] + list(then), ride=ride)


def matmul_tn(a, b, *, tb, name, ride=None, ta=MXU_WIDTH):
    T, Ka = a.shape
    S, _, Nb = b.shape
    per = Nb // tb

    def body(a_ref, b_ref, o_ref):
        o_ref[...] = _dot_tn(a_ref[...], b_ref[...]).astype(BF16)

    out = _call(
        body, name=name, grid=(S * per, Ka // ta),
        in_specs=[pl.BlockSpec((T, ta), lambda j, i: (0, i)),
                  pl.BlockSpec((None, T, tb), lambda j, i: (j // per, 0, j % per))],
        out_specs=[pl.BlockSpec((ta, tb), lambda j, i: (i, j))],
        out_shape=[SDS((Ka, S * Nb), BF16)],
        semantics=("parallel", "parallel"), args=(a, b), ride=ride)
    return out[0] if ride is None else (out[0][0], out[1])


def conv_bwd(dy, bcx, conv_w, *, name, ride=None, tm=ROW_TILE):
    T, D = dy.shape
    nt = T // tm
    hb = tm // BF16_ROWS
    last = T // BF16_ROWS - 1

    def body(dy_ref, dyn_ref, b_ref, bn_ref, c_ref, u_ref, cp_ref, up_ref, cw_ref, o_ref, dw_ref):
        i = pl.program_id(0)
        c, u = c_ref[...].astype(F32), u_ref[...].astype(F32)
        cu = c * u
        cup = jnp.where(i == 0, 0.0, cp_ref[...].astype(F32) * up_ref[...].astype(F32))
        cu1, cu2 = _shift_down(cup, cu, 1), _shift_down(cup, cu, 2)
        w0, w1, w2 = cw_ref[0:1, :], cw_ref[1:2, :], cw_ref[2:3, :]
        dyf = dy_ref[...].astype(F32)
        o_ref[:, 0:D] = (dyf * (w0 * cu2 + w1 * cu1 + w2 * cu)).astype(BF16)
        dcv = dyf * b_ref[...].astype(F32)
        dcvn = jnp.where(i == nt - 1, 0.0, dyn_ref[...].astype(F32) * bn_ref[...].astype(F32))
        dcu = w2 * dcv + w1 * _shift_up(dcv, dcvn, 1) + w0 * _shift_up(dcv, dcvn, 2)
        o_ref[:, D:2 * D] = (dcu * u).astype(BF16)
        o_ref[:, 2 * D:3 * D] = (dcu * c).astype(BF16)
        row = lax.broadcasted_iota(jnp.int32, (8, D), 0)
        dw = jnp.zeros((8, D), F32)
        for tap, t in enumerate((cu2, cu1, cu)):
            dw = jnp.where(row == tap, jnp.sum(dcv * t, axis=0, keepdims=True), dw)
        _accumulate(dw_ref, i == 0, dw)

    tile = lambda col: pl.BlockSpec((tm, D), lambda i: (i, col))
    prev = lambda col: pl.BlockSpec((BF16_ROWS, D), lambda i: (jnp.maximum(i * hb - 1, 0), col))
    nxt = lambda col: pl.BlockSpec((BF16_ROWS, D), lambda i: (jnp.minimum((i + 1) * hb, last), col))
    return _call(
        body, name=name, grid=(nt,),
        in_specs=[tile(0), nxt(0), tile(0), nxt(0), tile(1), tile(2), prev(1), prev(2),
                  pl.BlockSpec((3, D), lambda i: (0, 0))],
        out_specs=[pl.BlockSpec((tm, 3 * D), lambda i: (i, 0)), pl.BlockSpec((8, D), lambda i: (0, 0))],
        out_shape=[SDS((T, 3 * D), BF16), SDS((8, D), F32)],
        semantics=("arbitrary",), args=(dy, dy, bcx, bcx, bcx, bcx, bcx, bcx, conv_w), ride=ride)


class NoTraffic:
    def ride(self, kernel_name):
        return None

    def landed(self, kernel_name, results, wts):
        pass

    def grad(self, key, value):
        pass


def local_step(x, target, wts, vec, traffic):
    T, D = x.shape
    tabs = rope_tables(T)
    small = {}

    def run(builder, *args, name, **kw):
        ride = traffic.ride(name)
        if ride is None:
            return builder(*args, name=name, **kw)
        out, extra = builder(*args, name=name, ride=ride, **kw)
        traffic.landed(name, extra, wts)
        return out

    bcx, xn1 = run(norm_matmul, x, vec["a_pre"], wts["w_in"], tn=3 * D, split=1, name="a_in")
    bcx = bcx[0]
    h1, z0, y0 = run(conv_mix_out, bcx, vec["conv_w"], wts["w_out"], vec["a_post"], x, name="a_out")
    gu0, act0, xt2 = run(norm_swiglu_in, h1, vec["ffn_pre0"], wts["gu0"], name="ffn0_in")
    h2, z1 = run(plain_mix_out, act0, wts["wd0"], vec["ffn_post0"], h1, name="ffn0_out")
    kvp, xkv, qp, xq = run(norm2_matmul, h2, [vec["kv_norm"], vec["b_pre"]], [wts["w_kv"], wts["w_q"]],
                           name="kvq_in")
    attn, attn_stats = run(attention_fwd, qp, kvp, tabs, vec["sinks"], name="attn_fwd")
    h3, z2 = plain_mix_out(attn, wts["w_o"], vec["b_post"], h2, name="attn_out", tm=BIG_ROW_TILE)
    gu1, act1, xt3 = run(norm_swiglu_in, h3, vec["ffn_pre1"], wts["gu1"], name="ffn1_in")
    dy, dz3, small["ffn_post1"], dact1, loss = plain_mix_out(act1, wts["wd1"], vec["ffn_post1"], h3, name="ffn1_out",
                                                             target=target)

    def ffn_bwd(layer, dz, dact, gu, act, xt, h_in, dh, then, gu_first):
        tag = "ffn%d" % layer
        dwd = lambda: traffic.grad("wd%d" % layer, run(matmul_tn, act, dz[None], tb=D, name=tag + "_dwd"))
        dwgu = lambda: traffic.grad("gu%d" % layer, run(swiglu_bwd_tn, xt, dact, gu, name=tag + "_dwgu"))
        for step in ((dwgu, dwd) if gu_first else (dwd, dwgu)):
            step()
        dh_in, small["ffn_pre%d" % layer], dz_, dg_, da_ = run(
            swiglu_bwd_in, dact, gu, wts["gu%d" % layer], h_in, vec["ffn_pre%d" % layer], dh, then,
            name=tag + "_in_bwd")
        return dh_in, dz_, dg_, da_

    dh3, dz2, small["b_post"], dattn = ffn_bwd(1, dz3, dact1, gu1, act1, xt3, h3, dy,
                                               (z2, vec["b_post"], wts["w_o"]), gu_first=False)
    traffic.grad("w_o", matmul_tn(attn, dz2[None], tb=D, name="attn_dwo"))
    dq, dkv, small["sinks"] = run(attention_bwd, qp, kvp, tabs, vec["sinks"], dattn, attn, attn_stats,
                                  name="attn_bwd")
    traffic.grad("w_q", matmul_tn(xq, dq[None], tb=D, name="attn_dwq"))
    traffic.grad("w_kv", matmul_tn(xkv, dkv[None], tb=dkv.shape[1], name="attn_dwkv"))
    dh2, small["b_pre"], small["kv_norm"], dz1, small["ffn_post0"], dact0 = run(
        matmuls_nt_normbwd, [dq, dkv], [wts["w_q"], wts["w_kv"]], h2, [vec["b_pre"], vec["kv_norm"]], dh3,
        (z1, vec["ffn_post0"], wts["wd0"]), name="qkv_in_bwd")
    dh1, dz0, small["a_post"], dyc = ffn_bwd(0, dz1, dact0, gu0, act0, xt2, h1, dh2,
                                             (z0, vec["a_post"], wts["w_out"]), gu_first=True)
    traffic.grad("w_out", run(matmul_tn, y0, dz0[None], tb=D, name="a_dwout"))
    dbcx, small["conv_w"] = run(conv_bwd, dyc, bcx, vec["conv_w"], name="a_conv_bwd")
    traffic.grad("w_in", run(matmul_tn, xn1, dbcx[None], tb=3 * D // 2, name="a_dwin"))
    dx, small["a_pre"] = run(matmul_nt_normbwd, dbcx[None], wts["w_in"], x, vec["a_pre"], dh1, name="a_in_bwd")
    return loss, dx, small


SMALL_ROWS = 16
LOSS_ROW = 13

WHOLE = None
GATHER_PLAN = {"cast_rest": [("w_in", WHOLE)],
               "a_in": [("w_out", WHOLE), ("gu0", (0, 18))],
               "a_out": [("gu0", (18, 14))],
               "ffn0_in": [("wd0", WHOLE), ("w_kv", WHOLE), ("w_q", WHOLE)],
               "ffn0_out": [("w_o", WHOLE), ("gu1", (0, 8))],
               "kvq_in": [("gu1", (8, 4))],
               "attn_fwd": [("gu1", (12, 20))],
               "ffn1_in": [("wd1", WHOLE)]}
PAIR_PLAN = {"ffn1_dwgu": ["wd1"], "ffn1_in_bwd": ["gu1"], "attn_bwd": ["w_o"], "qkv_in_bwd": ["w_q", "w_kv"],
             "ffn0_dwd": ["gu0"], "ffn0_in_bwd": ["wd0"], "a_conv_bwd": ["w_out"]}
PAIR_ALONE = ["w_in"]
CHIP_PLAN = {"ffn1_in_bwd": [("wd1", WHOLE)], "attn_bwd": [("gu1", WHOLE)],
             "ffn0_dwgu": [("w_o", WHOLE), ("w_q", WHOLE), ("w_kv", WHOLE)],
             "ffn0_in_bwd": [("gu0", WHOLE)], "a_conv_bwd": [("wd0", (0, 12))],
             "a_dwin": [("wd0", (12, 10)), ("w_out", WHOLE)], "a_in_bwd": [("w_in", WHOLE)]}
HALF_PLAN = {"a_in_bwd": ["gu0", "gu1", "wd0", "wd1", "w_kv", "w_q", "w_o", "w_out"]}
GRAD_KIND = dict(KIND, gu0="split", gu1="split")


class Traffic:
    def __init__(self, wholes, quarter, c_arr, pc_arr):
        self.wholes, self.quarter, self.c_arr, self.pc_arr = wholes, quarter, c_arr, pc_arr
        self.views, self.sums, self.got = {}, {}, {}
        self.reduced = {}
        self.stages = {}

    def reduce(self, keys, name):
        return chip_reduce([self.sums[k] for k in keys], [self.got[k] for k in keys], [GRAD_KIND[k] for k in keys],
                           self.pc_arr, name=name)

    def ride(self, name, small=None):
        rides, stages = [], []
        if name in GATHER_PLAN:
            plan = GATHER_PLAN[name]
            rides.append(gather_ride([self.wholes[k] for k, _ in plan],
                                     [(KIND[k], self.quarter[k], part) for k, part in plan], small))
            stages.append(("gather", [k for k, _ in plan]))
        if name in CHIP_PLAN:
            plan = CHIP_PLAN[name]
            rides.append(chip_ride([self.sums[k] for k, _ in plan],
                                   [(GRAD_KIND[k], self.quarter[k], part) for k, part in plan],
                                   earlier=[self.got.get(k) for k, _ in plan]))
            stages.append(("chip", [k for k, _ in plan]))
        if name in PAIR_PLAN:
            keys = PAIR_PLAN[name]
            rides.append(pair_ride([self.views[k] for k in keys]))
            stages.append(("pair", keys))
        if name in HALF_PLAN:
            keys = HALF_PLAN[name]
            rides.append(half_ride(self.reduce(keys, "chip_reduce_early")))
            stages.append(("half", keys))
        self.stages[name] = stages
        return join(rides)

    def landed(self, name, results, wts):
        results = list(results)
        for stage, keys in self.stages[name]:
            mine, results = results[:len(keys)], results[len(keys):]
            if stage == "gather":
                for k, whole in zip(keys, mine):
                    self.wholes[k] = wts[k] = whole
            elif stage == "chip":
                self.got.update(zip(keys, mine))
            elif stage == "half":
                self.reduced.update(zip(keys, mine))
            else:
                for k, got in zip(keys, mine):
                    self.sums[k] = pair_add(self.views[k], got, self.c_arr, name="pair_add_" + k)

    def grad(self, key, value):
        r, ws = self.quarter[key]
        view = {"row": (N_CHIPS, 2, r // 2, ws), "col": (1, 2, r // 2, N_CHIPS * ws), "split": (2, 2, r // 2, 2 * ws)}
        self.views[key] = value.reshape(view[GRAD_KIND[key]])
        if key in PAIR_ALONE:
            (got,) = alone(pair_ride([self.views[key]]), name="pair_exchange_" + key)
            self.sums[key] = pair_add(self.views[key], got, self.c_arr, name="pair_add_" + key)


def kernel(x, a_pre_norm, a_w_in, a_conv_w, a_w_out, a_post_norm, ffn_pre_norm, ffn_w_gate_up, ffn_w_down, ffn_post_norm, kv_norm, w_kv, b_pre_norm, b_w_q, b_sinks, b_w_o, b_post_norm, loss_target, m_a_pre_norm, m_a_w_in, m_a_conv_w, m_a_w_out, m_a_post_norm, m_ffn_pre_norm, m_ffn_w_gate_up, m_ffn_w_down, m_ffn_post_norm, m_kv_norm, m_w_kv, m_b_pre_norm, m_b_w_q, m_b_sinks, m_b_w_o, m_b_post_norm, v_a_pre_norm, v_a_w_in, v_a_conv_w, v_a_w_out, v_a_post_norm, v_ffn_pre_norm, v_ffn_w_gate_up, v_ffn_w_down, v_ffn_post_norm, v_kv_norm, v_w_kv, v_b_pre_norm, v_b_w_q, v_b_sinks, v_b_w_o, v_b_post_norm):
    T, D = x.shape[1], x.shape[2]
    xi, yi, ci = _place()
    p = 2 * xi + yi
    p_arr = jnp.reshape(p, (1,)).astype(jnp.int32)
    c_arr = jnp.reshape(ci, (1,)).astype(jnp.int32)
    pc_arr = jnp.stack([p, ci]).astype(jnp.int32)
    me_arr = jnp.reshape(4 * xi + 2 * yi + ci, (1,)).astype(jnp.int32)
    qd = D // N_CHIPS

    big = {"w_in": (a_w_in, 0), "w_out": (a_w_out, 0), "gu0": (ffn_w_gate_up, 0), "gu1": (ffn_w_gate_up, 1),
           "wd0": (ffn_w_down, 0), "wd1": (ffn_w_down, 1), "w_kv": (w_kv[None], 0), "w_q": (b_w_q, 0),
           "w_o": (b_w_o, 0)}
    names = list(big)
    quarter = {k: w.shape[1:] for k, (w, _) in big.items()}
    source = lambda k: big[k] + (KIND[k],)
    traffic = Traffic(dict(zip(names[:1], cast_quarters([source(names[0])], p_arr, name="cast_first"))), quarter,
                      c_arr, pc_arr)
    small_shard = jnp.concatenate([a_pre_norm, a_post_norm, a_conv_w[0], jnp.zeros((3, qd), F32)], axis=0)
    wts = {}
    rest, (*landed, small_full) = cast_quarters([source(k) for k in names[1:]], p_arr, name="cast_rest",
                                                ride=traffic.ride("cast_rest", small_shard))
    traffic.wholes.update(zip(names[1:], rest))
    traffic.landed("cast_rest", landed, wts)
    rows = lambda k: jnp.transpose(small_full[:, k], (1, 0, 2)).reshape(-1, D)
    vec = {"a_pre": rows(slice(0, 1)), "a_post": rows(slice(1, 2)), "conv_w": rows(slice(2, 5)),
           "ffn_pre0": ffn_pre_norm[0:1], "ffn_pre1": ffn_pre_norm[1:2],
           "ffn_post0": ffn_post_norm[0:1], "ffn_post1": ffn_post_norm[1:2],
           "kv_norm": kv_norm[None], "b_pre": b_pre_norm, "b_post": b_post_norm, "sinks": b_sinks}

    loss, dx, small = local_step(x[0], loss_target[0], wts, vec, traffic)

    pad = lambda a: jnp.pad(a, ((0, 0), (0, D - a.shape[1])))
    small_block = jnp.concatenate(
        [small["a_pre"], small["a_post"], small["conv_w"][0:3], small["ffn_pre0"], small["ffn_pre1"],
         small["ffn_post0"], small["ffn_post1"], small["kv_norm"], small["b_pre"], small["b_post"],
         pad(small["sinks"][0:1]), pad(loss[0:1]), jnp.zeros((SMALL_ROWS - LOSS_ROW - 1, D), F32)], axis=0)
    late = [k for k in names if k not in traffic.reduced]
    *swapped, small_blocks = alone(join([half_ride(traffic.reduce(late, "chip_reduce_late")),
                                         chip_ride([], [], small_block)]), name="last_exchange")
    traffic.reduced.update(zip(late, swapped))
    grad = {k: traffic.reduced[k].reshape(quarter[k]) for k in names}
    small_sum = small_reduce(small_blocks, me_arr)

    out = {}
    out["a_w_in"] = adamw(a_w_in, [grad["w_in"]], m_a_w_in, v_a_w_in, name="adamw_a_w_in")
    out["a_w_out"] = adamw(a_w_out, [grad["w_out"]], m_a_w_out, v_a_w_out, name="adamw_a_w_out")
    out["ffn_w_gate_up"] = adamw(ffn_w_gate_up, [grad["gu0"], grad["gu1"]], m_ffn_w_gate_up, v_ffn_w_gate_up,
                                 name="adamw_ffn_w_gate_up")
    out["ffn_w_down"] = adamw(ffn_w_down, [grad["wd0"], grad["wd1"]], m_ffn_w_down, v_ffn_w_down,
                              name="adamw_ffn_w_down")
    out["w_kv"] = [o[0] for o in adamw(w_kv[None], [grad["w_kv"]], m_w_kv[None], v_w_kv[None], name="adamw_w_kv")]
    out["b_w_q"] = adamw(b_w_q, [grad["w_q"]], m_b_w_q, v_b_w_q, name="adamw_b_w_q")
    out["b_w_o"] = adamw(b_w_o, [grad["w_o"]], m_b_w_o, v_b_w_o, name="adamw_b_w_o")

    def pack(a_pre, a_post, conv, ffn_pre, ffn_post, kvn, b_pre, b_post, sinks):
        return jnp.concatenate([pad(a_pre), pad(a_post), pad(conv[0]), ffn_pre, ffn_post, kvn[None], b_pre, b_post,
                                pad(sinks), jnp.zeros((SMALL_ROWS - 13, D), F32)], axis=0)

    g_small = jnp.concatenate([pad(lax.dynamic_slice(small_sum, (0, p * qd), (5, qd))), small_sum[5:]], axis=0)
    w_small = pack(a_pre_norm, a_post_norm, a_conv_w, ffn_pre_norm, ffn_post_norm, kv_norm, b_pre_norm, b_post_norm,
                   b_sinks)
    m_small = pack(m_a_pre_norm, m_a_post_norm, m_a_conv_w, m_ffn_pre_norm, m_ffn_post_norm, m_kv_norm,
                   m_b_pre_norm, m_b_post_norm, m_b_sinks)
    v_small = pack(v_a_pre_norm, v_a_post_norm, v_a_conv_w, v_ffn_pre_norm, v_ffn_post_norm, v_kv_norm,
                   v_b_pre_norm, v_b_post_norm, v_b_sinks)
    packed = adamw(w_small[None], [g_small], m_small[None], v_small[None], name="adamw_small")
    ns = b_sinks.shape[1]
    unpack = lambda a: {"a_pre_norm": a[0:1, :qd], "a_post_norm": a[1:2, :qd], "a_conv_w": a[None, 2:5, :qd],
                        "ffn_pre_norm": a[5:7], "ffn_post_norm": a[7:9], "kv_norm": a[9], "b_pre_norm": a[10:11],
                        "b_post_norm": a[11:12], "b_sinks": a[12:13, :ns]}
    unpacked = [unpack(a[0]) for a in packed]
    for k in unpacked[0]:
        out[k] = [u[k] for u in unpacked]

    order = ["a_pre_norm", "a_w_in", "a_conv_w", "a_w_out", "a_post_norm", "ffn_pre_norm", "ffn_w_gate_up",
             "ffn_w_down", "ffn_post_norm", "kv_norm", "w_kv", "b_pre_norm", "b_w_q", "b_sinks", "b_w_o",
             "b_post_norm"]
    return (small_sum[LOSS_ROW, 0], dx[None], *[out[k][0] for k in order], *[out[k][1] for k in order],
            *[out[k][2] for k in order], *[out[k][3] for k in order])
```

```python
import math

import jax
import jax.numpy as jnp
from jax import lax
from jax.experimental import pallas as pl
from jax.experimental.pallas import tpu as pltpu

F32 = jnp.float32
BF16 = jnp.bfloat16
SDS = jax.ShapeDtypeStruct
MESH = pl.DeviceIdType.MESH
DMA = pltpu.SemaphoreType.DMA
HBM_SPEC = pl.BlockSpec(memory_space=pltpu.HBM)

EPS = 1e-6
NEG = -1e30
HEAD_DIM = 64
N_KV_HEADS = 4
BLOCK = 128
ROT_DIM = HEAD_DIM // 4
ROPE_THETA = 500000.0
N_CHIPS = 4

ADAM_LR = 0.001
ADAM_B1 = 0.9
ADAM_B2 = 0.999
ADAM_EPS = 1e-08
ADAM_WD = 0.01
ADAM_STEP = 10

VMEM_LIMIT_BYTES = 52 * 1024 * 1024
ROW_TILE = 512
BF16_ROWS = 16
STREAM = BF16
MXU_WIDTH = 256

KIND = {"w_in": "col", "gu0": "col", "gu1": "col", "w_out": "row", "wd0": "row", "wd1": "row", "w_kv": "row",
        "w_q": "row", "w_o": "row"}


def _params(*semantics):
    return pltpu.CompilerParams(dimension_semantics=semantics, vmem_limit_bytes=VMEM_LIMIT_BYTES)


def _row_tile(rows, limit, step=8):
    return max(t for t in range(step, limit + 1, step) if rows % t == 0)


def _place():
    return lax.axis_index("x"), lax.axis_index("y"), lax.axis_index("c")


def _other_chips(x, y):
    return [(1 - x, y), (x, 1 - y), (1 - x, 1 - y)]


def _remote(src, dst, send_sem, recv_sem, to):
    return pltpu.make_async_remote_copy(src_ref=src, dst_ref=dst, send_sem=send_sem, recv_sem=recv_sem,
                                        device_id=to, device_id_type=MESH)


def _full_shape(kind, quarter):
    r, ws = quarter
    return (N_CHIPS * r, ws) if kind == "row" else (r, N_CHIPS * ws)


def _rows_of(h, part):
    lo, n = (0, h) if part is None else (part[0] * BF16_ROWS, part[1] * BF16_ROWS)
    assert lo + n <= h, (h, part)
    return lo, n


def _half_of_quarter(ref, kind, quarter, part, q, half):
    r, ws = quarter
    h = r // 2
    lo, n = _rows_of(h, part)
    if kind == "row":
        return ref.at[pl.ds(pl.multiple_of(q * r + half * h + lo, BF16_ROWS), n)]
    return ref.at[pl.ds(pl.multiple_of(half * h + lo, BF16_ROWS), n), pl.ds(pl.multiple_of(q * ws, 128), ws)]


class Ride:
    def __init__(self, operands, out_shape, aliases, sems, make):
        self.operands, self.out_shape, self.aliases, self.sems, self.make = operands, out_shape, aliases, sems, make

    def stages(self, ins, outs, sems):
        made = self.make(ins, outs, sems)
        return made if len(made) == 3 else (made[0], None, made[1])


def join(rides):
    rides = [r for r in rides if r is not None]
    if len(rides) < 2:
        return rides[0] if rides else None
    aliases, at = {}, [0, 0, 0]
    cuts = []
    for r in rides:
        aliases.update({at[0] + i: at[1] + o for i, o in r.aliases.items()})
        cuts.append(tuple(at))
        at = [at[0] + len(r.operands), at[1] + len(r.out_shape), at[2] + len(r.sems)]
    cuts.append(tuple(at))

    def make(ins, outs, sem):
        made = [r.stages(ins[lo[0]:hi[0]], outs[lo[1]:hi[1]], sem[lo[2]:hi[2]]) for r, lo, hi in zip(rides, cuts, cuts[1:])]
        relays = [m[1] for m in made if m[1] is not None]

        def start():
            for m in made:
                m[0]()

        def relay():
            for r in relays:
                r()

        def finish():
            for m in made:
                m[2]()

        return (start, relay, finish) if relays else (start, finish)

    return Ride(sum((list(r.operands) for r in rides), []), sum((list(r.out_shape) for r in rides), []), aliases,
                sum((list(r.sems) for r in rides), []), make)


def _call(body, *, name, grid, in_specs, out_specs, out_shape, args, scratch_shapes=(), semantics=None, ride=None,
          prefetch=None):
    pre = 0 if prefetch is None else 1
    n_in, n_out, n_scr = len(in_specs), len(out_specs), len(scratch_shapes)
    r_in, r_out = (len(ride.operands), len(ride.out_shape)) if ride is not None else (0, 0)
    a, b = pre + n_in, pre + n_in + r_in
    c, d = b + n_out, b + n_out + r_out
    e = d + n_scr

    def riding(*refs):
        start, relay, finish = ride.stages(refs[a:b], refs[c:d], refs[e:])
        ids = [pl.program_id(k) for k in range(len(grid))]
        first, last = ids[0] == 0, ids[0] == grid[0] - 1
        for k in range(1, len(grid)):
            first, last = first & (ids[k] == 0), last & (ids[k] == grid[k] - 1)
        pl.when(first)(start)
        if relay is not None:
            pl.when(last)(relay)
        body(*refs[:a], *refs[b:c], *refs[d:e])
        pl.when(last)(finish)

    if ride is None:
        kernel_body, extra_in, extra_out, extra_shape, extra_scr, aliases = body, [], [], [], [], {}
        params = _params(*semantics)
    else:
        kernel_body, extra_in, extra_out = riding, [HBM_SPEC] * r_in, [HBM_SPEC] * r_out
        extra_shape, extra_scr = list(ride.out_shape), list(ride.sems)
        aliases = {pre + n_in + i: n_out + o for i, o in ride.aliases.items()}
        params = _params(*(("arbitrary",) * len(grid)))
    specs = dict(grid=grid, in_specs=list(in_specs) + extra_in, out_specs=list(out_specs) + extra_out,
                 scratch_shapes=list(scratch_shapes) + extra_scr)
    if prefetch is not None:
        specs = dict(grid_spec=pltpu.PrefetchScalarGridSpec(num_scalar_prefetch=1, **specs))
        args = (prefetch,) + tuple(args)
    outs = pl.pallas_call(kernel_body, name=name, out_shape=list(out_shape) + extra_shape,
                          input_output_aliases=aliases, compiler_params=params, **specs,
                          )(*args, *(ride.operands if ride is not None else ()))
    return outs if ride is None else (outs[:n_out], outs[n_out:])


def alone(ride, *, name):
    def body(*refs):
        n = len(ride.operands)
        stages = ride.stages(refs[:n], refs[n:n + len(ride.out_shape)], refs[n + len(ride.out_shape):])
        for stage in stages:
            if stage is not None:
                stage()

    return pl.pallas_call(
        body, name=name, in_specs=[HBM_SPEC] * len(ride.operands), out_specs=[HBM_SPEC] * len(ride.out_shape),
        out_shape=list(ride.out_shape), input_output_aliases=dict(ride.aliases), scratch_shapes=list(ride.sems),
    )(*ride.operands)


def gather_ride(wholes, metas, small=None):
    n = len(wholes)
    operands, out_shape = list(wholes), [SDS(s.shape, s.dtype) for s in wholes]
    sems = [DMA((n, 3)), DMA((n, 3)), DMA((n, 3)), DMA((n, 3))]
    if small is not None:
        operands.append(small)
        out_shape.append(SDS((N_CHIPS,) + small.shape, small.dtype))
        sems += [DMA((3,)), DMA((3,)), DMA(())]

    def make(ins, outs, sem):
        send1, recv1, send2, recv2 = sem[:4]
        x, y, c = _place()
        p = 2 * x + y
        chips = _other_chips(x, y)
        me, sibling = (x, y, c), (x, y, 1 - c)
        part = lambda t, q, half: _half_of_quarter(outs[t], *metas[t], q, half)
        first, landing, passing, arriving = [], [], [], []
        for j, (qx, qy) in enumerate(chips):
            q = 2 * qx + qy
            if small is not None:
                first.append(_remote(ins[n], outs[n].at[p], sem[4].at[j], sem[5].at[j], (qx, qy, c)))
                arriving.append(_remote(outs[n].at[q], outs[n].at[q], sem[4].at[j], sem[5].at[j], me))
            for t in range(n):
                first.append(_remote(part(t, p, c), part(t, p, c), send1.at[t, j], recv1.at[t, j], (qx, qy, c)))
                landed, theirs = part(t, q, c), part(t, q, 1 - c)
                landing.append(_remote(landed, landed, send1.at[t, j], recv1.at[t, j], me))
                passing.append(_remote(landed, landed, send2.at[t, j], recv2.at[t, j], sibling))
                arriving.append(_remote(theirs, theirs, send2.at[t, j], recv2.at[t, j], me))
        local = [] if small is None else [pltpu.make_async_copy(ins[n], outs[n].at[p], sem[6])]

        def start():
            for cp in local + first:
                cp.start()

        def relay():
            for got, cp in zip(landing, passing):
                got.wait_recv()
                cp.start()

        def finish():
            for cp in arriving:
                cp.wait_recv()
            for cp in first + passing:
                cp.wait_send()
            for cp in local:
                cp.wait()

        return start, relay, finish

    return Ride(operands, out_shape, {t: t for t in range(n)}, sems, make)


def chip_ride(sums, metas, small=None, earlier=None):
    n = len(sums)
    operands = list(sums)
    out_shape = [SDS((3, s.shape[1], quarter[1]), s.dtype) for s, (_, quarter, _) in zip(sums, metas)]
    sems = [DMA((n, 3)), DMA((n, 3))] if n else []
    if small is not None:
        operands.append(small)
        out_shape.append(SDS((8,) + small.shape, small.dtype))
        sems += [DMA((7,)), DMA((7,)), DMA(())]
    aliases = {}
    for t, buffer in enumerate(earlier or [None] * n):
        if buffer is not None:
            aliases[len(operands)] = t
            operands.append(buffer)

    def make(ins, outs, sem):
        x, y, c = _place()
        cps = []
        for j, (qx, qy) in enumerate(_other_chips(x, y)):
            q = 2 * qx + qy
            for t in range(n):
                kind, (_, ws), part = metas[t]
                rows = pl.ds(*_rows_of(ins[t].shape[1], part))
                if kind == "row":
                    src = ins[t].at[q, rows]
                elif kind == "col":
                    src = ins[t].at[0, rows, pl.ds(pl.multiple_of(q * ws, 128), ws)]
                else:
                    src = ins[t].at[q // 2, rows, pl.ds(pl.multiple_of((q % 2) * ws, 128), ws)]
                cps.append(_remote(src, outs[t].at[j, rows], sem[0].at[t, j], sem[1].at[t, j], (qx, qy, c)))
        local = []
        if small is not None:
            ssend, srecv, lsem = sem[2 * bool(n):2 * bool(n) + 3]
            local.append(pltpu.make_async_copy(ins[n], outs[n].at[0], lsem))
            for k in range(1, 8):
                peer = (x ^ (k >> 2 & 1), y ^ (k >> 1 & 1), c ^ (k & 1))
                cps.append(_remote(ins[n], outs[n].at[k], ssend.at[k - 1], srecv.at[k - 1], peer))

        def start():
            for cp in local + cps:
                cp.start()

        def finish():
            for cp in cps + local:
                cp.wait()

        return start, finish

    return Ride(operands, out_shape, aliases, sems, make)


def pair_ride(grads):
    n = len(grads)

    def make(ins, outs, sem):
        x, y, c = _place()
        cps = [_remote(ins[t].at[:, 1 - c], outs[t], sem[0].at[t], sem[1].at[t], (x, y, 1 - c)) for t in range(n)]

        def start():
            for cp in cps:
                cp.start()

        def finish():
            for cp in cps:
                cp.wait()

        return start, finish

    return Ride(list(grads), [SDS((g.shape[0],) + g.shape[2:], g.dtype) for g in grads], {}, [DMA((n,)), DMA((n,))],
                make)


def half_ride(quarters):
    n = len(quarters)

    def make(ins, outs, sem):
        x, y, c = _place()
        sends = [_remote(outs[t].at[c], outs[t].at[c], sem[0].at[t], sem[1].at[t], (x, y, 1 - c)) for t in range(n)]

        def start():
            for cp in sends:
                cp.start()

        def finish():
            for t in range(n):
                theirs = outs[t].at[1 - c]
                _remote(theirs, theirs, sem[0].at[t], sem[1].at[t], (x, y, c)).wait_recv()
            for cp in sends:
                cp.wait_send()

        return start, finish

    return Ride(list(quarters), [SDS(q.shape, q.dtype) for q in quarters], {t: t for t in range(n)},
                [DMA((n,)), DMA((n,))], make)


CAST_STEPS = 4


def cast_quarters(sources, p_arr, *, name, ride=None):
    n = len(sources)
    in_specs, out_specs, out_shape = [], [], []
    for w, layer, kind in sources:
        _, r, ws = w.shape
        tr = r // CAST_STEPS
        assert tr % BF16_ROWS == 0, w.shape
        in_specs.append(pl.BlockSpec((None, tr, ws), lambda i, p_ref, layer=layer: (layer, i, 0)))
        out_specs.append(pl.BlockSpec((tr, ws), (lambda i, p_ref: (p_ref[0] * CAST_STEPS + i, 0)) if kind == "row"
                                      else (lambda i, p_ref: (i, p_ref[0]))))
        out_shape.append(SDS(_full_shape(kind, (r, ws)), BF16))

    def body(p_ref, *refs):
        for w_ref, o_ref in zip(refs[:n], refs[n:]):
            o_ref[...] = w_ref[...].astype(BF16)

    return _call(body, name=name, grid=(CAST_STEPS,), in_specs=in_specs, out_specs=out_specs, out_shape=out_shape,
                 semantics=("parallel",), args=[w for w, _, _ in sources], ride=ride, prefetch=p_arr)


def pair_add(own, got, c_arr, *, name):
    A, _, h, W = own.shape
    th = _row_tile(h, max(BF16_ROWS, (3 << 19) // W), BF16_ROWS)

    def body(c_ref, a_ref, b_ref, o_ref):
        o_ref[...] = (a_ref[...].astype(F32) + b_ref[...].astype(F32)).astype(BF16)

    return pl.pallas_call(
        body, name=name,
        grid_spec=pltpu.PrefetchScalarGridSpec(
            num_scalar_prefetch=1, grid=(A, h // th),
            in_specs=[pl.BlockSpec((None, None, th, W), lambda q, i, c_ref: (q, c_ref[0], i, 0)),
                      pl.BlockSpec((None, th, W), lambda q, i, c_ref: (q, i, 0))],
            out_specs=pl.BlockSpec((None, th, W), lambda q, i, c_ref: (q, i, 0))),
        out_shape=SDS((A, h, W), BF16),
        compiler_params=_params("parallel", "parallel"),
    )(c_arr, own, got)


REDUCE_STEPS = 2


def chip_reduce(sums, got, kinds, pc_arr, *, name, ride=None):
    n = len(sums)
    mine = {"row": lambda i, pc_ref: (pc_ref[0], i, 0), "col": lambda i, pc_ref: (0, i, pc_ref[0]),
            "split": lambda i, pc_ref: (pc_ref[0] // 2, i, pc_ref[0] % 2)}
    a_specs, b_specs, o_specs, out_shape = [], [], [], []
    for g, kind in zip(got, kinds):
        _, h, ws = g.shape
        th = h // REDUCE_STEPS
        assert th % BF16_ROWS == 0, g.shape
        a_specs.append(pl.BlockSpec((None, th, ws), mine[kind]))
        b_specs.append(pl.BlockSpec((3, th, ws), lambda i, pc_ref: (0, i, 0)))
        o_specs.append(pl.BlockSpec((None, th, ws), lambda i, pc_ref: (pc_ref[1], i, 0)))
        out_shape.append(SDS((2, h, ws), F32))

    def body(pc_ref, *refs):
        for a_ref, b_ref, o_ref in zip(refs[:n], refs[n:2 * n], refs[2 * n:]):
            o_ref[...] = ((a_ref[...].astype(F32) + b_ref[0].astype(F32)) + b_ref[1].astype(F32)) + b_ref[2].astype(F32)

    return _call(body, name=name, grid=(REDUCE_STEPS,), in_specs=a_specs + b_specs, out_specs=o_specs,
                 out_shape=out_shape, semantics=("parallel",), args=list(sums) + list(got), prefetch=pc_arr, ride=ride)


def small_reduce(blocks, me_arr):
    _, rows, D = blocks.shape

    def body(me_ref, b_ref, o_ref):
        me = me_ref[0]
        total = b_ref[me]
        for d in range(1, 8):
            total = total + b_ref[d ^ me]
        o_ref[...] = total

    return pl.pallas_call(
        body, name="small_reduce",
        grid_spec=pltpu.PrefetchScalarGridSpec(
            num_scalar_prefetch=1, grid=(1,),
            in_specs=[pl.BlockSpec((8, rows, D), lambda i, me_ref: (0, 0, 0))],
            out_specs=pl.BlockSpec((rows, D), lambda i, me_ref: (0, 0))),
        out_shape=SDS((rows, D), F32),
        compiler_params=_params("arbitrary"),
    )(me_arr, blocks)


def adamw(w, gs, m, v, *, name):
    L, r, cols = w.shape
    tr = _row_tile(r, 256)
    nt = r // tr

    def body(*refs):
        w_ref, m_ref, v_ref = refs[:3]
        g_refs = refs[3:3 + L]
        g_out, d_out, m_out, v_out = refs[3 + L:]
        layer = pl.program_id(0)
        g = g_refs[0][...]
        for l in range(1, L):
            g = jnp.where(layer == l, g_refs[l][...], g)
        m_new = ADAM_B1 * m_ref[...] + (1.0 - ADAM_B1) * g
        v_new = ADAM_B2 * v_ref[...] + (1.0 - ADAM_B2) * (g * g)
        m_hat = m_new / (1.0 - ADAM_B1 ** ADAM_STEP)
        v_hat = v_new / (1.0 - ADAM_B2 ** ADAM_STEP)
        g_out[...] = g
        m_out[...] = m_new
        v_out[...] = v_new
        d_out[...] = -ADAM_LR * (m_hat / (jnp.sqrt(v_hat) + ADAM_EPS) + ADAM_WD * w_ref[...])

    full = pl.BlockSpec((None, tr, cols), lambda l, i: (l, i, 0))
    g_spec = lambda l0: pl.BlockSpec((tr, cols), lambda l, i: (jnp.where(l == l0, i, jnp.where(l < l0, 0, nt - 1)), 0))
    return pl.pallas_call(
        body, name=name, grid=(L, nt),
        in_specs=[full, full, full] + [g_spec(l0) for l0 in range(L)],
        out_specs=[full] * 4,
        out_shape=[SDS(w.shape, F32)] * 4,
        compiler_params=_params("arbitrary", "arbitrary"),
    )(w, m, v, *gs)


def _rms_r(xf):
    return lax.rsqrt(jnp.mean(xf * xf, axis=-1, keepdims=True) + EPS)


def _rmsnorm_bwd(xf, g, dy):
    r = _rms_r(xf)
    xh = xf * r
    gd = g * dy
    return r * (gd - xh * jnp.mean(xh * gd, axis=-1, keepdims=True)), xh


def _dot(a, b):
    return jnp.dot(a, b, preferred_element_type=F32)


def _dot_nt(a, b):
    return lax.dot_general(a, b, (((1,), (1,)), ((), ())), preferred_element_type=F32)


def _dot_tn(a, b):
    return lax.dot_general(a, b, (((0,), (0,)), ((), ())), preferred_element_type=F32)


def _accumulate(ref, first, value):
    @pl.when(first)
    def _():
        ref[...] = value

    @pl.when(jnp.logical_not(first))
    def _():
        ref[...] += value


def norm_matmul(x, g, w, *, tn, split, name, ride=None, tm=ROW_TILE):
    T, D = x.shape
    N = w.shape[1]
    per = N // split // tn

    def body(x_ref, g_ref, w_ref, o_ref, xn_ref):
        @pl.when(pl.program_id(1) == 0)
        def _():
            xf = x_ref[...].astype(F32)
            xn_ref[...] = (xf * _rms_r(xf) * g_ref[...]).astype(BF16)

        o_ref[...] = _dot(xn_ref[...], w_ref[...]).astype(BF16)

    return _call(
        body, name=name, grid=(T // tm, N // tn),
        in_specs=[pl.BlockSpec((tm, D), lambda i, j: (i, 0)),
                  pl.BlockSpec((1, D), lambda i, j: (0, 0)),
                  pl.BlockSpec((D, tn), lambda i, j: (0, j))],
        out_specs=[pl.BlockSpec((None, tm, tn), lambda i, j: (j // per, i, j % per)),
                   pl.BlockSpec((tm, D), lambda i, j: (i, 0))],
        out_shape=[SDS((split, T, N // split), BF16), SDS((T, D), BF16)],
        semantics=("parallel", "arbitrary"), args=(x, g, w), ride=ride)


BIG_ROW_TILE = 1024


def norm2_matmul(x, gains, weights, *, name, ride=None, tm=BIG_ROW_TILE):
    T, D = x.shape
    tm = min(tm, T)
    n = len(gains)

    def body(x_ref, *refs):
        subs = _sub_tiles(tm)
        xhs = []
        for rows in subs:
            xf = x_ref[rows, :].astype(F32)
            xhs.append(xf * _rms_r(xf))
        for g_ref, w_ref, o_ref, xn_ref in zip(refs[:n], refs[n:2 * n], refs[2 * n::2], refs[2 * n + 1::2]):
            for rows, xh in zip(subs, xhs):
                xn = (xh * g_ref[...]).astype(BF16)
                xn_ref[rows, :] = xn
                o_ref[rows, :] = _dot(xn, w_ref[...]).astype(BF16)

    row = pl.BlockSpec((tm, D), lambda i: (i, 0))
    vec = pl.BlockSpec((1, D), lambda i: (0, 0))
    out_specs, out_shape = [], []
    for w in weights:
        out_specs += [pl.BlockSpec((tm, w.shape[1]), lambda i: (i, 0)), row]
        out_shape += [SDS((T, w.shape[1]), BF16), SDS((T, D), BF16)]
    return _call(
        body, name=name, grid=(T // tm,),
        in_specs=[row] + [vec] * n + [pl.BlockSpec(w.shape, lambda i: (0, 0)) for w in weights],
        out_specs=out_specs, out_shape=out_shape, semantics=("parallel",), args=[x] + list(gains) + list(weights),
        ride=ride)


def _shift_down(prev, cur, by):
    big = jnp.concatenate([prev, cur], axis=0)
    return pltpu.roll(big, by, 0)[prev.shape[0]:]


def _shift_up(cur, nxt, by):
    big = jnp.concatenate([cur, nxt], axis=0)
    return pltpu.roll(big, big.shape[0] - by, 0)[:cur.shape[0]]


def conv_mix_out(bcx, conv_w, w_out, g_post, res, *, name, ride=None, tm=ROW_TILE):
    T, D = res.shape
    hb = tm // BF16_ROWS

    def body(b_ref, c_ref, u_ref, cp_ref, up_ref, cw_ref, w_ref, g_ref, r_ref, h_ref, z_ref, y_ref):
        i = pl.program_id(0)
        cu = c_ref[...].astype(F32) * u_ref[...].astype(F32)
        cup = cp_ref[...].astype(F32) * up_ref[...].astype(F32)
        cup = jnp.where(i == 0, 0.0, cup)
        cv = (cw_ref[0:1, :] * _shift_down(cup, cu, 2) + cw_ref[1:2, :] * _shift_down(cup, cu, 1)
              + cw_ref[2:3, :] * cu)
        y = (b_ref[...].astype(F32) * cv).astype(BF16)
        y_ref[...] = y
        z = _dot(y, w_ref[...])
        z_ref[...] = z.astype(BF16)
        h_ref[...] = (r_ref[...] + z * _rms_r(z) * g_ref[...]).astype(STREAM)

    tile = lambda col: pl.BlockSpec((tm, D), lambda i: (i, col))
    halo = lambda col: pl.BlockSpec((BF16_ROWS, D), lambda i: (jnp.maximum(i * hb - 1, 0), col))
    row = pl.BlockSpec((tm, D), lambda i: (i, 0))
    return _call(
        body, name=name, grid=(T // tm,),
        in_specs=[tile(0), tile(1), tile(2), halo(1), halo(2),
                  pl.BlockSpec((3, D), lambda i: (0, 0)),
                  pl.BlockSpec((D, D), lambda i: (0, 0)),
                  pl.BlockSpec((1, D), lambda i: (0, 0)), row],
        out_specs=[row, row, row],
        out_shape=[SDS((T, D), STREAM), SDS((T, D), BF16), SDS((T, D), BF16)],
        semantics=("parallel",), args=(bcx, bcx, bcx, bcx, bcx, conv_w, w_out, g_post, res), ride=ride)


def _normbwd_then_nt(dh, zf, g_ref, w_ref, dz_ref, dg_ref, o_ref, first):
    dz, zh = _rmsnorm_bwd(zf, g_ref[...], dh)
    dz = dz.astype(BF16)
    dz_ref[...] = dz
    _accumulate(dg_ref, first, jnp.sum(dh * zh, axis=0, keepdims=True))
    o_ref[...] = _dot_nt(dz, w_ref[...]).astype(BF16)


def _then_specs(then, tm, T, D):
    z, g, w = then
    K = w.shape[0]
    row = pl.BlockSpec((tm, D), lambda i: (i, 0))
    vec = pl.BlockSpec((1, D), lambda i: (0, 0))
    in_specs = [row, vec, pl.BlockSpec((K, D), lambda i: (0, 0), pipeline_mode=pl.Buffered(1))]
    out_specs = [row, vec, pl.BlockSpec((tm, K), lambda i: (i, 0))]
    out_shape = [SDS((T, D), BF16), SDS((1, D), F32), SDS((T, K), BF16)]
    return in_specs, out_specs, out_shape


def plain_mix_out(a, w, g_post, res, *, name, target=None, ride=None, tm=ROW_TILE):
    T, D = res.shape
    tm = min(tm, T)
    K = a.shape[1]
    with_loss = target is not None

    def body(a_ref, w_ref, g_ref, r_ref, *rest):
        subs = _sub_tiles(tm)
        zs = [_dot(a_ref[rows, :], w_ref[...]) for rows in subs]
        if not with_loss:
            h_ref, z_ref = rest
            for rows, z in zip(subs, zs):
                h_ref[rows, :] = (r_ref[rows, :].astype(F32) + z * _rms_r(z) * g_ref[...]).astype(STREAM)
                z_ref[rows, :] = z.astype(BF16)
            return
        t_ref, h_ref, dz_ref, dg_ref, da_ref, loss_ref = rest
        first = pl.program_id(0) == 0
        loss, dg = jnp.zeros((), F32), jnp.zeros((1, D), F32)
        for rows, z in zip(subs, zs):
            diff = r_ref[rows, :].astype(F32) + z * _rms_r(z) * g_ref[...] - t_ref[rows, :]
            dh = diff * (1.0 / D)
            h_ref[rows, :] = dh.astype(STREAM)
            loss = loss + jnp.sum(diff * diff)
            dz, zh = _rmsnorm_bwd(z, g_ref[...], dh)
            dz = dz.astype(BF16)
            dz_ref[rows, :] = dz
            dg = dg + jnp.sum(dh * zh, axis=0, keepdims=True)
            da_ref[rows, :] = _dot_nt(dz, w_ref[...]).astype(BF16)
        _accumulate(loss_ref, first, jnp.full(loss_ref.shape, 0.5 / D, F32) * loss)
        _accumulate(dg_ref, first, dg)

    row = pl.BlockSpec((tm, D), lambda i: (i, 0))
    vec = pl.BlockSpec((1, D), lambda i: (0, 0))
    in_specs = [pl.BlockSpec((tm, K), lambda i: (i, 0)), pl.BlockSpec((K, D), lambda i: (0, 0)), vec, row]
    if with_loss:
        in_specs.append(row)
        out_specs = [row, row, vec, pl.BlockSpec((tm, K), lambda i: (i, 0)), pl.BlockSpec((8, 128), lambda i: (0, 0))]
        out_shape = [SDS((T, D), STREAM), SDS((T, D), BF16), SDS((1, D), F32), SDS((T, K), BF16), SDS((8, 128), F32)]
    else:
        out_specs, out_shape = [row, row], [SDS((T, D), STREAM), SDS((T, D), BF16)]
    return _call(
        body, name=name, grid=(T // tm,), in_specs=in_specs, out_specs=out_specs, out_shape=out_shape,
        semantics=("arbitrary",), args=(a, w, g_post, res) + ((target,) if with_loss else ()), ride=ride)


def _silu_grads(d, g, u):
    sg = jax.nn.sigmoid(g)
    return d * u * (sg * (1.0 + g * (1.0 - sg))), d * (g * sg)


def _sub_tiles(tm):
    return [pl.ds(k, min(MXU_WIDTH, tm)) for k in range(0, tm, MXU_WIDTH)]


def norm_swiglu_in(x, g, w, *, name, ride=None, tm=ROW_TILE):
    T, D = x.shape
    F = w.shape[1] // 2

    def body(x_ref, g_ref, wg_ref, wu_ref, gu_ref, a_ref, xt_ref):
        subs = _sub_tiles(tm)
        xns = []
        for rows in subs:
            xf = x_ref[rows, :].astype(F32)
            xns.append(xf * _rms_r(xf) * g_ref[...])
        xbs = [xn.astype(BF16) for xn in xns]
        gates = [_dot(xb, wg_ref[...]).astype(BF16) for xb in xbs]
        ups = [_dot(xb, wu_ref[...]).astype(BF16) for xb in xbs]
        for rows, gate, up in zip(subs, gates, ups):
            gu_ref[0, rows, :] = gate
            gu_ref[1, rows, :] = up
            a_ref[rows, :] = gate * jax.nn.sigmoid(gate) * up
        for rows, xn in zip(subs, xns):
            xt_ref[:, rows] = xn.T.astype(BF16)

    half = lambda s: pl.BlockSpec((D, F), lambda i: (0, s), pipeline_mode=pl.Buffered(1))
    return _call(
        body, name=name, grid=(T // tm,),
        in_specs=[pl.BlockSpec((tm, D), lambda i: (i, 0)), pl.BlockSpec((1, D), lambda i: (0, 0)), half(0), half(1)],
        out_specs=[pl.BlockSpec((2, tm, F), lambda i: (0, i, 0)), pl.BlockSpec((tm, F), lambda i: (i, 0)),
                   pl.BlockSpec((D, tm), lambda i: (0, i))],
        out_shape=[SDS((2, T, F), BF16), SDS((T, F), BF16), SDS((D, T), BF16)],
        semantics=("parallel",), args=(x, g, w, w), ride=ride)


def swiglu_bwd_tn(xt, dact, gu, *, name, ride=None, tb=MXU_WIDTH):
    D, T = xt.shape
    F = dact.shape[1]

    def body(xt_ref, d_ref, g_ref, u_ref, o_ref):
        dg, du = _silu_grads(d_ref[...], g_ref[...], u_ref[...])
        o_ref[0] = _dot(xt_ref[...], dg).astype(BF16)
        o_ref[1] = _dot(xt_ref[...], du).astype(BF16)

    col = lambda s: pl.BlockSpec((None, T, tb), lambda j: (s, 0, j))
    out = _call(
        body, name=name, grid=(F // tb,),
        in_specs=[pl.BlockSpec((D, T), lambda j: (0, 0), pipeline_mode=pl.Buffered(1)),
                  pl.BlockSpec((T, tb), lambda j: (0, j)), col(0), col(1)],
        out_specs=[pl.BlockSpec((2, D, tb), lambda j: (0, 0, j))],
        out_shape=[SDS((2, D, F), BF16)],
        semantics=("parallel",), args=(xt, dact, gu, gu), ride=ride)
    return out[0] if ride is None else (out[0][0], out[1])


def swiglu_bwd_in(dact, gu, w, h_in, g, dh_out, then, *, name, ride=None, tm=ROW_TILE):
    T, D = h_in.shape
    F = dact.shape[1]

    def body(d_ref, gg_ref, uu_ref, wg_ref, wu_ref, h_ref, g_ref, dh_ref, z_ref, g2_ref, w2_ref,
             o_ref, dg_ref, dz_ref, dg2_ref, da_ref):
        first = pl.program_id(0) == 0
        subs = _sub_tiles(tm)
        dns = []
        for rows in subs:
            dgate, dup = _silu_grads(d_ref[rows, :], gg_ref[rows, :], uu_ref[rows, :])
            dns.append(_dot_nt(dgate, wg_ref[...]) + _dot_nt(dup, wu_ref[...]))
        dg, dg2 = jnp.zeros((1, D), F32), jnp.zeros((1, D), F32)
        for rows, dn in zip(subs, dns):
            dx, hh = _rmsnorm_bwd(h_ref[rows, :].astype(F32), g_ref[...], dn)
            dh_in = dh_ref[rows, :] + dx
            o_ref[rows, :] = dh_in.astype(STREAM)
            dg = dg + jnp.sum(dn * hh, axis=0, keepdims=True)
            dz, zh = _rmsnorm_bwd(z_ref[rows, :].astype(F32), g2_ref[...], dh_in)
            dz = dz.astype(BF16)
            dz_ref[rows, :] = dz
            dg2 = dg2 + jnp.sum(dh_in * zh, axis=0, keepdims=True)
            da_ref[rows, :] = _dot_nt(dz, w2_ref[...]).astype(BF16)
        _accumulate(dg_ref, first, dg)
        _accumulate(dg2_ref, first, dg2)

    row = pl.BlockSpec((tm, D), lambda i: (i, 0))
    vec = pl.BlockSpec((1, D), lambda i: (0, 0))
    part = lambda s: pl.BlockSpec((None, tm, F), lambda i: (s, i, 0))
    half = lambda s: pl.BlockSpec((D, F), lambda i: (0, s), pipeline_mode=pl.Buffered(1))
    then_in, then_out, then_shape = _then_specs(then, tm, T, D)
    return _call(
        body, name=name, grid=(T // tm,),
        in_specs=[pl.BlockSpec((tm, F), lambda i: (i, 0)), part(0), part(1), half(0), half(1), row, vec, row] + then_in,
        out_specs=[row, vec] + then_out,
        out_shape=[SDS((T, D), STREAM), SDS((1, D), F32)] + then_shape,
        semantics=("arbitrary",), args=(dact, gu, gu, w, w, h_in, g, dh_out) + tuple(then), ride=ride)


def rope_tables(T):
    half = ROT_DIM // 2
    inv_freq = ROPE_THETA ** (-jnp.arange(0, ROT_DIM, 2, dtype=F32) / ROT_DIM)
    ang = (jnp.arange(T, dtype=F32)[:, None] * inv_freq[None, :]).T
    cos, sin = jnp.cos(ang), jnp.sin(ang)
    rest = HEAD_DIM - ROT_DIM
    one, zero = jnp.ones((rest, T), F32), jnp.zeros((rest, T), F32)
    zh = jnp.zeros((half, T), F32)
    fac = jnp.concatenate([cos, cos, one], axis=0)
    up = jnp.concatenate([-sin, zh, zero], axis=0)
    down = jnp.concatenate([zh, sin, zero], axis=0)
    return jnp.stack([fac, up, down])


def _rope(t, tab):
    half = ROT_DIM // 2
    return t * tab[0] + pltpu.roll(t, HEAD_DIM - half, 0) * tab[1] + pltpu.roll(t, half, 0) * tab[2]


def _rope_t(d, tab):
    half = ROT_DIM // 2
    return d * tab[0] + pltpu.roll(d * tab[1], half, 0) + pltpu.roll(d * tab[2], HEAD_DIM - half, 0)


def _head(t, h):
    return t[h * HEAD_DIM:(h + 1) * HEAD_DIM]


def _band(n, group):
    kj = lax.broadcasted_iota(jnp.int32, (2 * BLOCK, BLOCK), 0)
    qi = lax.broadcasted_iota(jnp.int32, (2 * BLOCK, BLOCK), 1)
    mask = (kj > qi) & (kj <= qi + BLOCK) & ((n > 0) | (kj >= BLOCK))
    return jnp.tile(mask, (1, group))


def _attn_specs(D, kvd, nb):
    cur = lambda n: jnp.minimum(n, nb - 1)
    prev = lambda n: jnp.maximum(cur(n) - 1, 0)
    return [pl.BlockSpec((BLOCK, D), lambda n: (cur(n), 0)),
            pl.BlockSpec((BLOCK, kvd), lambda n: (prev(n), 0)),
            pl.BlockSpec((BLOCK, kvd), lambda n: (cur(n), 0)),
            pl.BlockSpec((BLOCK, kvd), lambda n: (prev(n), 1)),
            pl.BlockSpec((BLOCK, kvd), lambda n: (cur(n), 1)),
            pl.BlockSpec((3, HEAD_DIM, BLOCK), lambda n: (0, 0, prev(n))),
            pl.BlockSpec((3, HEAD_DIM, BLOCK), lambda n: (0, 0, cur(n))),
            pl.BlockSpec(memory_space=pltpu.SMEM)]


def _attn_operands(q_ref, kp_ref, k_ref, vp_ref, v_ref, tp_ref, t_ref):
    flip = lambda ref: ref[...].astype(F32).T
    tab = t_ref[...]
    kt = jnp.concatenate([flip(kp_ref), flip(k_ref)], axis=1)
    vt = jnp.concatenate([flip(vp_ref), flip(v_ref)], axis=1)
    return flip(q_ref), kt, vt, tab, jnp.concatenate([tp_ref[...], tab], axis=2)


SCORE_SCALE = 1.0 / math.sqrt(HEAD_DIM)
HEADS_TOGETHER = 4


def _group_heads(t, first, count, tab=None):
    heads = [_head(t, first + g) for g in range(count)]
    if tab is not None:
        heads = [_rope(h, tab) * SCORE_SCALE for h in heads]
    return jnp.concatenate(heads, axis=1).astype(BF16)


def _sink_row(s_ref, first, count):
    which = lax.broadcasted_iota(jnp.int32, (1, count * BLOCK), 1) // BLOCK
    row = jnp.zeros((1, count * BLOCK), F32)
    for g in range(count):
        row = jnp.where(which == g, s_ref[0, first + g], row)
    return row


def _sum_keys(t):
    return _dot(jnp.ones((8, t.shape[0]), BF16), t)[0:1]


def _softmax(scores, sink, mask):
    s = jnp.where(mask, scores.astype(BF16), NEG)
    m = jnp.maximum(jnp.max(s, axis=0, keepdims=True).astype(F32), sink).astype(BF16)
    e = jnp.exp(s - m)
    m = m.astype(F32)
    return e, m, 1.0 / (_sum_keys(e) + jnp.exp(sink - m))


def _per_head(row, count):
    return [row[:, g * BLOCK:(g + 1) * BLOCK] for g in range(count)]


def attention_fwd(q, kv, tabs, sinks, *, name, ride=None):
    T, D = q.shape
    kvd = kv.shape[1] // 2
    heads = D // HEAD_DIM
    group = heads // N_KV_HEADS

    def body(q_ref, kp_ref, k_ref, vp_ref, v_ref, tp_ref, t_ref, s_ref, o_ref, stat_ref):
        gs = HEADS_TOGETHER
        mask = _band(pl.program_id(0), gs)
        qt, kt, vt, tab, tab2 = _attn_operands(q_ref, kp_ref, k_ref, vp_ref, v_ref, tp_ref, t_ref)
        firsts = [(j, first) for j in range(N_KV_HEADS) for first in range(j * group, (j + 1) * group, gs)]
        ks = [_rope(_head(kt, j), tab2).astype(BF16) for j in range(N_KV_HEADS)]
        scores = [_dot_tn(ks[j], _group_heads(qt, first, gs, tab)) for j, first in firsts]
        soft = [_softmax(s, _sink_row(s_ref, first, gs), mask) for s, (j, first) in zip(scores, firsts)]
        outs, ms, invs = [], [], []
        for (e, m, inv), (j, first) in zip(soft, firsts):
            o = _dot(_head(vt, j).astype(BF16), e) * inv
            outs += [o[:, g * BLOCK:(g + 1) * BLOCK] for g in range(gs)]
            ms += _per_head(m, gs)
            invs += _per_head(inv, gs)
        o_ref[...] = jnp.concatenate(outs, axis=0).T.astype(BF16)
        stat_ref[0] = jnp.concatenate(ms, axis=0)
        stat_ref[1] = jnp.concatenate(invs, axis=0)

    return _call(
        body, name=name, grid=(T // BLOCK,),
        in_specs=_attn_specs(D, kvd, T // BLOCK),
        out_specs=[pl.BlockSpec((BLOCK, D), lambda n: (n, 0)), pl.BlockSpec((2, heads, BLOCK), lambda n: (0, 0, n))],
        out_shape=[SDS((T, D), BF16), SDS((2, heads, T), F32)],
        semantics=("parallel",), args=(q, kv, kv, kv, kv, tabs, tabs, sinks), ride=ride)


def attention_bwd(q, kv, tabs, sinks, do, o, stats, *, name, ride=None):
    T, D = q.shape
    kvd = kv.shape[1] // 2
    heads = D // HEAD_DIM
    group = heads // N_KV_HEADS
    nb = T // BLOCK

    def body(q_ref, kp_ref, k_ref, vp_ref, v_ref, tp_ref, t_ref, s_ref, do_ref, o_ref, stat_ref,
             dq_ref, dkv_ref, ds_ref, carry):
        n = pl.program_id(0)

        @pl.when(n == 0)
        def _():
            carry[...] = jnp.zeros_like(carry)

        @pl.when(n < nb)
        def _():
            block(n, q_ref, kp_ref, k_ref, vp_ref, v_ref, tp_ref, t_ref, s_ref, do_ref, o_ref, stat_ref,
                  dq_ref, dkv_ref, ds_ref, carry)

        @pl.when(n == nb)
        def _():
            dkv_ref[...] = carry[...].astype(BF16)

    def block(n, q_ref, kp_ref, k_ref, vp_ref, v_ref, tp_ref, t_ref, s_ref, do_ref, o_ref, stat_ref,
              dq_ref, dkv_ref, ds_ref, carry):
        gs = HEADS_TOGETHER
        mask = _band(n, gs)
        qt, kt, vt, tab, tab2 = _attn_operands(q_ref, kp_ref, k_ref, vp_ref, v_ref, tp_ref, t_ref)
        dot = do_ref[...].astype(F32).T
        odo = o_ref[...].astype(F32).T * dot
        dl_all = jnp.concatenate([jnp.sum(_head(odo, h), axis=0, keepdims=True) for h in range(heads)], axis=0)
        m_all, inv_all = stat_ref[0], stat_ref[1]
        row = lambda t, first: jnp.concatenate([t[first + g:first + g + 1] for g in range(gs)], axis=1)
        lane = lax.broadcasted_iota(jnp.int32, (8, 128), 1)
        dsink = jnp.zeros((8, 128), F32)
        firsts = [(j, first) for j in range(N_KV_HEADS) for first in range(j * group, (j + 1) * group, gs)]
        ks = [_rope(_head(kt, j), tab2).astype(BF16) for j in range(N_KV_HEADS)]
        vs = [_head(vt, j).astype(BF16) for j in range(N_KV_HEADS)]
        qs = [_group_heads(qt, first, gs, tab) for _, first in firsts]
        dos = [_group_heads(dot, first, gs) for _, first in firsts]
        scores = [_dot_tn(ks[j], q) for q, (j, _) in zip(qs, firsts)]
        dps = [_dot_tn(vs[j], do) for do, (j, _) in zip(dos, firsts)]
        ps, dscs = [], []
        for s, dp, (j, first) in zip(scores, dps, firsts):
            m, inv, dl = row(m_all, first), row(inv_all, first), row(dl_all, first)
            e = jnp.exp(jnp.where(mask, s.astype(BF16), NEG) - m.astype(BF16))
            p = e * inv.astype(BF16)
            dscs.append(p * (dp.astype(BF16) - dl.astype(BF16)))
            ps.append(p)
            weight = jnp.exp(_sink_row(s_ref, first, gs) - m) * inv * dl
            for g in range(gs):
                dsink = dsink - jnp.where(lane == first + g, jnp.sum(weight[:, g * BLOCK:(g + 1) * BLOCK]), 0.0)
        dqs = []
        dks = [jnp.zeros((HEAD_DIM, 2 * BLOCK), F32) for _ in range(N_KV_HEADS)]
        dvs = [jnp.zeros((HEAD_DIM, 2 * BLOCK), F32) for _ in range(N_KV_HEADS)]
        for p, dsc, q, do, (j, _) in zip(ps, dscs, qs, dos, firsts):
            dq = _dot(ks[j], dsc) * SCORE_SCALE
            dqs += [_rope_t(dq[:, g * BLOCK:(g + 1) * BLOCK], tab) for g in range(gs)]
            dks[j] = dks[j] + _dot_nt(q, dsc)
            dvs[j] = dvs[j] + _dot_nt(do, p)
        dks = [_rope_t(dk, tab2) for dk in dks]
        dq_ref[...] = jnp.concatenate(dqs, axis=0).T.astype(BF16)
        dkv = jnp.concatenate(dks + dvs, axis=0)
        dkv_ref[...] = (carry[...] + dkv[:, :BLOCK].T).astype(BF16)
        carry[...] = dkv[:, BLOCK:].T
        _accumulate(ds_ref, n == 0, dsink)

    cur = lambda n: jnp.minimum(n, nb - 1)
    blk = lambda w: pl.BlockSpec((BLOCK, w), lambda n: (cur(n), 0))
    return _call(
        body, name=name, grid=(nb + 1,),
        in_specs=_attn_specs(D, kvd, nb) + [blk(D), blk(D), pl.BlockSpec((2, heads, BLOCK), lambda n: (0, 0, cur(n)))],
        out_specs=[blk(D), pl.BlockSpec((BLOCK, 2 * kvd), lambda n: (jnp.maximum(n - 1, 0), 0)),
                   pl.BlockSpec((8, 128), lambda n: (0, 0))],
        out_shape=[SDS((T, D), BF16), SDS((T, 2 * kvd), BF16), SDS((8, 128), F32)],
        scratch_shapes=[pltpu.VMEM((BLOCK, 2 * kvd), F32)],
        semantics=("arbitrary",), args=(q, kv, kv, kv, kv, tabs, tabs, sinks, do, o, stats), ride=ride)


def matmul_nt_normbwd(da, w, h_in, g, dh_out, *, name, ride=None, tm=ROW_TILE):
    T, D = h_in.shape
    S, _, K = da.shape

    def body(*refs):
        da_refs, w_refs = refs[:S], refs[S:2 * S]
        h_ref, g_ref, dh_ref, o_ref, dg_ref = refs[2 * S:]
        subs = _sub_tiles(tm)
        dns = []
        for rows in subs:
            dn = _dot_nt(da_refs[0][rows, :], w_refs[0][...])
            for s in range(1, S):
                dn = dn + _dot_nt(da_refs[s][rows, :], w_refs[s][...])
            dns.append(dn)
        dg = jnp.zeros((1, D), F32)
        for rows, dn in zip(subs, dns):
            dx, hh = _rmsnorm_bwd(h_ref[rows, :].astype(F32), g_ref[...], dn)
            o_ref[rows, :] = dh_ref[rows, :] + dx
            dg = dg + jnp.sum(dn * hh, axis=0, keepdims=True)
        _accumulate(dg_ref, pl.program_id(0) == 0, dg)

    row = pl.BlockSpec((tm, D), lambda i: (i, 0))
    vec = pl.BlockSpec((1, D), lambda i: (0, 0))
    part = lambda s: pl.BlockSpec((None, tm, K), lambda i: (s, i, 0))
    cols = lambda s: pl.BlockSpec((D, K), lambda i: (0, s), pipeline_mode=pl.Buffered(1))
    return _call(
        body, name=name, grid=(T // tm,),
        in_specs=[part(s) for s in range(S)] + [cols(s) for s in range(S)] + [row, vec, row],
        out_specs=[row, vec],
        out_shape=[SDS((T, D), F32), SDS((1, D), F32)],
        semantics=("arbitrary",), args=[da] * S + [w] * S + [h_in, g, dh_out], ride=ride)


def matmuls_nt_normbwd(das, ws, h_in, gs, dh_out, then, *, name, ride=None, tm=ROW_TILE):
    T, D = h_in.shape
    tm = min(tm, T)
    n = len(das)

    def body(*refs):
        da_refs, w_refs, g_refs = refs[:n], refs[n:2 * n], refs[2 * n:3 * n]
        h_ref, dh_ref, z_ref, g2_ref, w2_ref, o_ref = refs[3 * n:3 * n + 6]
        dg_refs, (dz_ref, dg2_ref, da_ref) = refs[3 * n + 6:4 * n + 6], refs[4 * n + 6:]
        first = pl.program_id(0) == 0
        subs = _sub_tiles(tm)
        dns = [[_dot_nt(da_ref_[rows, :], w_ref[...]) for da_ref_, w_ref in zip(da_refs, w_refs)] for rows in subs]
        dgs, dg2 = [jnp.zeros((1, D), F32) for _ in range(n)], jnp.zeros((1, D), F32)
        for rows, dn_sub in zip(subs, dns):
            hf = h_ref[rows, :].astype(F32)
            r = _rms_r(hf)
            hh = hf * r
            total = dh_ref[rows, :].astype(F32)
            for b, (dn, g_ref) in enumerate(zip(dn_sub, g_refs)):
                gd = g_ref[...] * dn
                total = total + r * (gd - hh * jnp.mean(hh * gd, axis=-1, keepdims=True))
                dgs[b] = dgs[b] + jnp.sum(dn * hh, axis=0, keepdims=True)
            o_ref[rows, :] = total.astype(STREAM)
            dz, zh = _rmsnorm_bwd(z_ref[rows, :].astype(F32), g2_ref[...], total)
            dz = dz.astype(BF16)
            dz_ref[rows, :] = dz
            dg2 = dg2 + jnp.sum(total * zh, axis=0, keepdims=True)
            da_ref[rows, :] = _dot_nt(dz, w2_ref[...]).astype(BF16)
        for dg_ref, dg in zip(dg_refs + (dg2_ref,), dgs + [dg2]):
            _accumulate(dg_ref, first, dg)

    row = pl.BlockSpec((tm, D), lambda i: (i, 0))
    vec = pl.BlockSpec((1, D), lambda i: (0, 0))
    then_in, then_out, then_shape = _then_specs(then, tm, T, D)
    return _call(
        body, name=name, grid=(T // tm,),
        in_specs=[pl.BlockSpec((tm, da.shape[1]), lambda i: (i, 0)) for da in das]
        + [pl.BlockSpec(w.shape, lambda i: (0, 0)) for w in ws] + [vec] * n + [row, row] + then_in,
        out_specs=[row] + [vec] * n + then_out,
        out_shape=[SDS((T, D), STREAM)] + [SDS((1, D), F32)] * n + then_shape,
        semantics=("arbitrary",), args=list(das) + list(ws) + list(gs) + [h_in, dh_out] + list(then), ride=ride)


def matmul_tn(a, b, *, tb, name, ride=None, ta=MXU_WIDTH):
    T, Ka = a.shape
    S, _, Nb = b.shape
    per = Nb // tb

    def body(a_ref, b_ref, o_ref):
        o_ref[...] = _dot_tn(a_ref[...], b_ref[...]).astype(BF16)

    out = _call(
        body, name=name, grid=(S * per, Ka // ta),
        in_specs=[pl.BlockSpec((T, ta), lambda j, i: (0, i)),
                  pl.BlockSpec((None, T, tb), lambda j, i: (j // per, 0, j % per))],
        out_specs=[pl.BlockSpec((ta, tb), lambda j, i: (i, j))],
        out_shape=[SDS((Ka, S * Nb), BF16)],
        semantics=("parallel", "parallel"), args=(a, b), ride=ride)
    return out[0] if ride is None else (out[0][0], out[1])


def conv_bwd(dy, bcx, conv_w, *, name, ride=None, tm=ROW_TILE):
    T, D = dy.shape
    nt = T // tm
    hb = tm // BF16_ROWS
    last = T // BF16_ROWS - 1

    def body(dy_ref, dyn_ref, b_ref, bn_ref, c_ref, u_ref, cp_ref, up_ref, cw_ref, o_ref, dw_ref):
        i = pl.program_id(0)
        c, u = c_ref[...].astype(F32), u_ref[...].astype(F32)
        cu = c * u
        cup = jnp.where(i == 0, 0.0, cp_ref[...].astype(F32) * up_ref[...].astype(F32))
        cu1, cu2 = _shift_down(cup, cu, 1), _shift_down(cup, cu, 2)
        w0, w1, w2 = cw_ref[0:1, :], cw_ref[1:2, :], cw_ref[2:3, :]
        dyf = dy_ref[...].astype(F32)
        o_ref[:, 0:D] = (dyf * (w0 * cu2 + w1 * cu1 + w2 * cu)).astype(BF16)
        dcv = dyf * b_ref[...].astype(F32)
        dcvn = jnp.where(i == nt - 1, 0.0, dyn_ref[...].astype(F32) * bn_ref[...].astype(F32))
        dcu = w2 * dcv + w1 * _shift_up(dcv, dcvn, 1) + w0 * _shift_up(dcv, dcvn, 2)
        o_ref[:, D:2 * D] = (dcu * u).astype(BF16)
        o_ref[:, 2 * D:3 * D] = (dcu * c).astype(BF16)
        row = lax.broadcasted_iota(jnp.int32, (8, D), 0)
        dw = jnp.zeros((8, D), F32)
        for tap, t in enumerate((cu2, cu1, cu)):
            dw = jnp.where(row == tap, jnp.sum(dcv * t, axis=0, keepdims=True), dw)
        _accumulate(dw_ref, i == 0, dw)

    tile = lambda col: pl.BlockSpec((tm, D), lambda i: (i, col))
    prev = lambda col: pl.BlockSpec((BF16_ROWS, D), lambda i: (jnp.maximum(i * hb - 1, 0), col))
    nxt = lambda col: pl.BlockSpec((BF16_ROWS, D), lambda i: (jnp.minimum((i + 1) * hb, last), col))
    return _call(
        body, name=name, grid=(nt,),
        in_specs=[tile(0), nxt(0), tile(0), nxt(0), tile(1), tile(2), prev(1), prev(2),
                  pl.BlockSpec((3, D), lambda i: (0, 0))],
        out_specs=[pl.BlockSpec((tm, 3 * D), lambda i: (i, 0)), pl.BlockSpec((8, D), lambda i: (0, 0))],
        out_shape=[SDS((T, 3 * D), BF16), SDS((8, D), F32)],
        semantics=("arbitrary",), args=(dy, dy, bcx, bcx, bcx, bcx, bcx, bcx, conv_w), ride=ride)


class NoTraffic:
    def ride(self, kernel_name):
        return None

    def landed(self, kernel_name, results, wts):
        pass

    def grad(self, key, value):
        pass


def local_step(x, target, wts, vec, traffic):
    T, D = x.shape
    tabs = rope_tables(T)
    small = {}

    def run(builder, *args, name, **kw):
        ride = traffic.ride(name)
        if ride is None:
            return builder(*args, name=name, **kw)
        out, extra = builder(*args, name=name, ride=ride, **kw)
        traffic.landed(name, extra, wts)
        return out

    bcx, xn1 = run(norm_matmul, x, vec["a_pre"], wts["w_in"], tn=3 * D, split=1, name="a_in")
    bcx = bcx[0]
    h1, z0, y0 = run(conv_mix_out, bcx, vec["conv_w"], wts["w_out"], vec["a_post"], x, name="a_out")
    gu0, act0, xt2 = run(norm_swiglu_in, h1, vec["ffn_pre0"], wts["gu0"], name="ffn0_in")
    h2, z1 = run(plain_mix_out, act0, wts["wd0"], vec["ffn_post0"], h1, name="ffn0_out")
    kvp, xkv, qp, xq = run(norm2_matmul, h2, [vec["kv_norm"], vec["b_pre"]], [wts["w_kv"], wts["w_q"]],
                           name="kvq_in")
    attn, attn_stats = run(attention_fwd, qp, kvp, tabs, vec["sinks"], name="attn_fwd")
    h3, z2 = plain_mix_out(attn, wts["w_o"], vec["b_post"], h2, name="attn_out", tm=BIG_ROW_TILE)
    gu1, act1, xt3 = run(norm_swiglu_in, h3, vec["ffn_pre1"], wts["gu1"], name="ffn1_in")
    dy, dz3, small["ffn_post1"], dact1, loss = plain_mix_out(act1, wts["wd1"], vec["ffn_post1"], h3, name="ffn1_out",
                                                             target=target)

    def ffn_bwd(layer, dz, dact, gu, act, xt, h_in, dh, then, gu_first):
        tag = "ffn%d" % layer
        dwd = lambda: traffic.grad("wd%d" % layer, run(matmul_tn, act, dz[None], tb=D, name=tag + "_dwd"))
        dwgu = lambda: traffic.grad("gu%d" % layer, run(swiglu_bwd_tn, xt, dact, gu, name=tag + "_dwgu"))
        for step in ((dwgu, dwd) if gu_first else (dwd, dwgu)):
            step()
        dh_in, small["ffn_pre%d" % layer], dz_, dg_, da_ = run(
            swiglu_bwd_in, dact, gu, wts["gu%d" % layer], h_in, vec["ffn_pre%d" % layer], dh, then,
            name=tag + "_in_bwd")
        return dh_in, dz_, dg_, da_

    dh3, dz2, small["b_post"], dattn = ffn_bwd(1, dz3, dact1, gu1, act1, xt3, h3, dy,
                                               (z2, vec["b_post"], wts["w_o"]), gu_first=False)
    traffic.grad("w_o", matmul_tn(attn, dz2[None], tb=D, name="attn_dwo"))
    dq, dkv, small["sinks"] = run(attention_bwd, qp, kvp, tabs, vec["sinks"], dattn, attn, attn_stats,
                                  name="attn_bwd")
    traffic.grad("w_q", matmul_tn(xq, dq[None], tb=D, name="attn_dwq"))
    traffic.grad("w_kv", matmul_tn(xkv, dkv[None], tb=dkv.shape[1], name="attn_dwkv"))
    dh2, small["b_pre"], small["kv_norm"], dz1, small["ffn_post0"], dact0 = run(
        matmuls_nt_normbwd, [dq, dkv], [wts["w_q"], wts["w_kv"]], h2, [vec["b_pre"], vec["kv_norm"]], dh3,
        (z1, vec["ffn_post0"], wts["wd0"]), name="qkv_in_bwd")
    dh1, dz0, small["a_post"], dyc = ffn_bwd(0, dz1, dact0, gu0, act0, xt2, h1, dh2,
                                             (z0, vec["a_post"], wts["w_out"]), gu_first=True)
    traffic.grad("w_out", run(matmul_tn, y0, dz0[None], tb=D, name="a_dwout"))
    dbcx, small["conv_w"] = run(conv_bwd, dyc, bcx, vec["conv_w"], name="a_conv_bwd")
    traffic.grad("w_in", run(matmul_tn, xn1, dbcx[None], tb=3 * D // 2, name="a_dwin"))
    dx, small["a_pre"] = run(matmul_nt_normbwd, dbcx[None], wts["w_in"], x, vec["a_pre"], dh1, name="a_in_bwd")
    return loss, dx, small


SMALL_ROWS = 16
LOSS_ROW = 13

WHOLE = None
GATHER_PLAN = {"cast_rest": [("w_in", WHOLE)],
               "a_in": [("w_out", WHOLE), ("gu0", (0, 18))],
               "a_out": [("gu0", (18, 14))],
               "ffn0_in": [("wd0", WHOLE), ("w_kv", WHOLE), ("w_q", WHOLE)],
               "ffn0_out": [("w_o", WHOLE), ("gu1", (0, 8))],
               "kvq_in": [("gu1", (8, 4))],
               "attn_fwd": [("gu1", (12, 20))],
               "ffn1_in": [("wd1", WHOLE)]}
PAIR_PLAN = {"ffn1_dwgu": ["wd1"], "ffn1_in_bwd": ["gu1"], "attn_bwd": ["w_o"], "qkv_in_bwd": ["w_q", "w_kv"],
             "ffn0_dwd": ["gu0"], "ffn0_in_bwd": ["wd0"], "a_conv_bwd": ["w_out"], "chip_reduce_early": ["w_in"]}
CHIP_PLAN = {"ffn1_in_bwd": [("wd1", WHOLE)], "attn_bwd": [("gu1", WHOLE)],
             "ffn0_dwgu": [("w_o", WHOLE), ("w_q", WHOLE), ("w_kv", WHOLE)],
             "ffn0_in_bwd": [("gu0", WHOLE)], "a_conv_bwd": [("wd0", (0, 12))],
             "a_dwin": [("wd0", (12, 10)), ("w_out", WHOLE)], "a_in_bwd": [("w_in", WHOLE)]}
HALF_PLAN = {"a_in_bwd": ["gu0", "gu1", "wd0", "wd1", "w_kv", "w_q", "w_o", "w_out"]}
GRAD_KIND = dict(KIND, gu0="split", gu1="split")


class Traffic:
    def __init__(self, wholes, quarter, c_arr, pc_arr):
        self.wholes, self.quarter, self.c_arr, self.pc_arr = wholes, quarter, c_arr, pc_arr
        self.views, self.sums, self.got = {}, {}, {}
        self.reduced = {}
        self.stages = {}

    def reduce(self, keys, name):
        args = ([self.sums[k] for k in keys], [self.got[k] for k in keys], [GRAD_KIND[k] for k in keys], self.pc_arr)
        if name not in PAIR_PLAN:
            return chip_reduce(*args, name=name)
        pairs = PAIR_PLAN[name]
        out, got = chip_reduce(*args, name=name, ride=pair_ride([self.views[k] for k in pairs]))
        self.pair_sums(pairs, got)
        return out

    def pair_sums(self, keys, got):
        for k, theirs in zip(keys, got):
            self.sums[k] = pair_add(self.views[k], theirs, self.c_arr, name="pair_add_" + k)

    def ride(self, name, small=None):
        rides, stages = [], []
        if name in GATHER_PLAN:
            plan = GATHER_PLAN[name]
            rides.append(gather_ride([self.wholes[k] for k, _ in plan],
                                     [(KIND[k], self.quarter[k], part) for k, part in plan], small))
            stages.append(("gather", [k for k, _ in plan]))
        if name in HALF_PLAN:
            keys = HALF_PLAN[name]
            rides.append(half_ride(self.reduce(keys, "chip_reduce_early")))
            stages.append(("half", keys))
        if name in CHIP_PLAN:
            plan = CHIP_PLAN[name]
            rides.append(chip_ride([self.sums[k] for k, _ in plan],
                                   [(GRAD_KIND[k], self.quarter[k], part) for k, part in plan],
                                   earlier=[self.got.get(k) for k, _ in plan]))
            stages.append(("chip", [k for k, _ in plan]))
        if name in PAIR_PLAN:
            keys = PAIR_PLAN[name]
            rides.append(pair_ride([self.views[k] for k in keys]))
            stages.append(("pair", keys))
        self.stages[name] = stages
        return join(rides)

    def landed(self, name, results, wts):
        results = list(results)
        for stage, keys in self.stages[name]:
            mine, results = results[:len(keys)], results[len(keys):]
            if stage == "gather":
                for k, whole in zip(keys, mine):
                    self.wholes[k] = wts[k] = whole
            elif stage == "chip":
                self.got.update(zip(keys, mine))
            elif stage == "half":
                self.reduced.update(zip(keys, mine))
            else:
                self.pair_sums(keys, mine)

    def grad(self, key, value):
        r, ws = self.quarter[key]
        view = {"row": (N_CHIPS, 2, r // 2, ws), "col": (1, 2, r // 2, N_CHIPS * ws), "split": (2, 2, r // 2, 2 * ws)}
        self.views[key] = value.reshape(view[GRAD_KIND[key]])


def kernel(x, a_pre_norm, a_w_in, a_conv_w, a_w_out, a_post_norm, ffn_pre_norm, ffn_w_gate_up, ffn_w_down, ffn_post_norm, kv_norm, w_kv, b_pre_norm, b_w_q, b_sinks, b_w_o, b_post_norm, loss_target, m_a_pre_norm, m_a_w_in, m_a_conv_w, m_a_w_out, m_a_post_norm, m_ffn_pre_norm, m_ffn_w_gate_up, m_ffn_w_down, m_ffn_post_norm, m_kv_norm, m_w_kv, m_b_pre_norm, m_b_w_q, m_b_sinks, m_b_w_o, m_b_post_norm, v_a_pre_norm, v_a_w_in, v_a_conv_w, v_a_w_out, v_a_post_norm, v_ffn_pre_norm, v_ffn_w_gate_up, v_ffn_w_down, v_ffn_post_norm, v_kv_norm, v_w_kv, v_b_pre_norm, v_b_w_q, v_b_sinks, v_b_w_o, v_b_post_norm):
    T, D = x.shape[1], x.shape[2]
    xi, yi, ci = _place()
    p = 2 * xi + yi
    p_arr = jnp.reshape(p, (1,)).astype(jnp.int32)
    c_arr = jnp.reshape(ci, (1,)).astype(jnp.int32)
    pc_arr = jnp.stack([p, ci]).astype(jnp.int32)
    me_arr = jnp.reshape(4 * xi + 2 * yi + ci, (1,)).astype(jnp.int32)
    qd = D // N_CHIPS

    big = {"w_in": (a_w_in, 0), "w_out": (a_w_out, 0), "gu0": (ffn_w_gate_up, 0), "gu1": (ffn_w_gate_up, 1),
           "wd0": (ffn_w_down, 0), "wd1": (ffn_w_down, 1), "w_kv": (w_kv[None], 0), "w_q": (b_w_q, 0),
           "w_o": (b_w_o, 0)}
    names = list(big)
    quarter = {k: w.shape[1:] for k, (w, _) in big.items()}
    source = lambda k: big[k] + (KIND[k],)
    traffic = Traffic(dict(zip(names[:1], cast_quarters([source(names[0])], p_arr, name="cast_first"))), quarter,
                      c_arr, pc_arr)
    small_shard = jnp.concatenate([a_pre_norm, a_post_norm, a_conv_w[0], jnp.zeros((3, qd), F32)], axis=0)
    wts = {}
    rest, (*landed, small_full) = cast_quarters([source(k) for k in names[1:]], p_arr, name="cast_rest",
                                                ride=traffic.ride("cast_rest", small_shard))
    traffic.wholes.update(zip(names[1:], rest))
    traffic.landed("cast_rest", landed, wts)
    rows = lambda k: jnp.transpose(small_full[:, k], (1, 0, 2)).reshape(-1, D)
    vec = {"a_pre": rows(slice(0, 1)), "a_post": rows(slice(1, 2)), "conv_w": rows(slice(2, 5)),
           "ffn_pre0": ffn_pre_norm[0:1], "ffn_pre1": ffn_pre_norm[1:2],
           "ffn_post0": ffn_post_norm[0:1], "ffn_post1": ffn_post_norm[1:2],
           "kv_norm": kv_norm[None], "b_pre": b_pre_norm, "b_post": b_post_norm, "sinks": b_sinks}

    loss, dx, small = local_step(x[0], loss_target[0], wts, vec, traffic)

    pad = lambda a: jnp.pad(a, ((0, 0), (0, D - a.shape[1])))
    small_block = jnp.concatenate(
        [small["a_pre"], small["a_post"], small["conv_w"][0:3], small["ffn_pre0"], small["ffn_pre1"],
         small["ffn_post0"], small["ffn_post1"], small["kv_norm"], small["b_pre"], small["b_post"],
         pad(small["sinks"][0:1]), pad(loss[0:1]), jnp.zeros((SMALL_ROWS - LOSS_ROW - 1, D), F32)], axis=0)
    late = [k for k in names if k not in traffic.reduced]
    *swapped, small_blocks = alone(join([half_ride(traffic.reduce(late, "chip_reduce_late")),
                                         chip_ride([], [], small_block)]), name="last_exchange")
    traffic.reduced.update(zip(late, swapped))
    grad = {k: traffic.reduced[k].reshape(quarter[k]) for k in names}
    small_sum = small_reduce(small_blocks, me_arr)

    out = {}
    out["a_w_in"] = adamw(a_w_in, [grad["w_in"]], m_a_w_in, v_a_w_in, name="adamw_a_w_in")
    out["a_w_out"] = adamw(a_w_out, [grad["w_out"]], m_a_w_out, v_a_w_out, name="adamw_a_w_out")
    out["ffn_w_gate_up"] = adamw(ffn_w_gate_up, [grad["gu0"], grad["gu1"]], m_ffn_w_gate_up, v_ffn_w_gate_up,
                                 name="adamw_ffn_w_gate_up")
    out["ffn_w_down"] = adamw(ffn_w_down, [grad["wd0"], grad["wd1"]], m_ffn_w_down, v_ffn_w_down,
                              name="adamw_ffn_w_down")
    out["w_kv"] = [o[0] for o in adamw(w_kv[None], [grad["w_kv"]], m_w_kv[None], v_w_kv[None], name="adamw_w_kv")]
    out["b_w_q"] = adamw(b_w_q, [grad["w_q"]], m_b_w_q, v_b_w_q, name="adamw_b_w_q")
    out["b_w_o"] = adamw(b_w_o, [grad["w_o"]], m_b_w_o, v_b_w_o, name="adamw_b_w_o")

    def pack(a_pre, a_post, conv, ffn_pre, ffn_post, kvn, b_pre, b_post, sinks):
        return jnp.concatenate([pad(a_pre), pad(a_post), pad(conv[0]), ffn_pre, ffn_post, kvn[None], b_pre, b_post,
                                pad(sinks), jnp.zeros((SMALL_ROWS - 13, D), F32)], axis=0)

    g_small = jnp.concatenate([pad(lax.dynamic_slice(small_sum, (0, p * qd), (5, qd))), small_sum[5:]], axis=0)
    w_small = pack(a_pre_norm, a_post_norm, a_conv_w, ffn_pre_norm, ffn_post_norm, kv_norm, b_pre_norm, b_post_norm,
                   b_sinks)
    m_small = pack(m_a_pre_norm, m_a_post_norm, m_a_conv_w, m_ffn_pre_norm, m_ffn_post_norm, m_kv_norm,
                   m_b_pre_norm, m_b_post_norm, m_b_sinks)
    v_small = pack(v_a_pre_norm, v_a_post_norm, v_a_conv_w, v_ffn_pre_norm, v_ffn_post_norm, v_kv_norm,
                   v_b_pre_norm, v_b_post_norm, v_b_sinks)
    packed = adamw(w_small[None], [g_small], m_small[None], v_small[None], name="adamw_small")
    ns = b_sinks.shape[1]
    unpack = lambda a: {"a_pre_norm": a[0:1, :qd], "a_post_norm": a[1:2, :qd], "a_conv_w": a[None, 2:5, :qd],
                        "ffn_pre_norm": a[5:7], "ffn_post_norm": a[7:9], "kv_norm": a[9], "b_pre_norm": a[10:11],
                        "b_post_norm": a[11:12], "b_sinks": a[12:13, :ns]}
    unpacked = [unpack(a[0]) for a in packed]
    for k in unpacked[0]:
        out[k] = [u[k] for u in unpacked]

    order = ["a_pre_norm", "a_w_in", "a_conv_w", "a_w_out", "a_post_norm", "ffn_pre_norm", "ffn_w_gate_up",
             "ffn_w_down", "ffn_post_norm", "kv_norm", "w_kv", "b_pre_norm", "b_w_q", "b_sinks", "b_w_o",
             "b_post_norm"]
    return (small_sum[LOSS_ROW, 0], dx[None], *[out[k][0] for k in order], *[out[k][1] for k in order],
            *[out[k][2] for k in order], *[out[k][3] for k in order])
```

```python
import math

import jax
import jax.numpy as jnp
from jax import lax
from jax.experimental import pallas as pl
from jax.experimental.pallas import tpu as pltpu

F32 = jnp.float32
BF16 = jnp.bfloat16
SDS = jax.ShapeDtypeStruct
MESH = pl.DeviceIdType.MESH
DMA = pltpu.SemaphoreType.DMA
HBM_SPEC = pl.BlockSpec(memory_space=pltpu.HBM)

EPS = 1e-6
NEG = -1e30
HEAD_DIM = 64
N_KV_HEADS = 4
BLOCK = 128
ROT_DIM = HEAD_DIM // 4
ROPE_THETA = 500000.0
N_CHIPS = 4

ADAM_LR = 0.001
ADAM_B1 = 0.9
ADAM_B2 = 0.999
ADAM_EPS = 1e-08
ADAM_WD = 0.01
ADAM_STEP = 10

VMEM_LIMIT_BYTES = 52 * 1024 * 1024
ROW_TILE = 512
BF16_ROWS = 16
STREAM = BF16
MXU_WIDTH = 256

KIND = {"w_in": "col", "gu0": "col", "gu1": "col", "w_out": "row", "wd0": "row", "wd1": "row", "w_kv": "row",
        "w_q": "row", "w_o": "row"}


def _params(*semantics):
    return pltpu.CompilerParams(dimension_semantics=semantics, vmem_limit_bytes=VMEM_LIMIT_BYTES)


def _row_tile(rows, limit, step=8):
    return max(t for t in range(step, limit + 1, step) if rows % t == 0)


def _place():
    return lax.axis_index("x"), lax.axis_index("y"), lax.axis_index("c")


def _other_chips(x, y):
    return [(1 - x, y), (x, 1 - y), (1 - x, 1 - y)]


def _remote(src, dst, send_sem, recv_sem, to):
    return pltpu.make_async_remote_copy(src_ref=src, dst_ref=dst, send_sem=send_sem, recv_sem=recv_sem,
                                        device_id=to, device_id_type=MESH)


def _full_shape(kind, quarter):
    r, ws = quarter
    return (N_CHIPS * r, ws) if kind == "row" else (r, N_CHIPS * ws)


def _rows_of(h, part):
    lo, n = (0, h) if part is None else (part[0] * BF16_ROWS, part[1] * BF16_ROWS)
    assert lo + n <= h, (h, part)
    return lo, n


def _half_of_quarter(ref, kind, quarter, part, q, half):
    r, ws = quarter
    h = r // 2
    lo, n = _rows_of(h, part)
    if kind == "row":
        return ref.at[pl.ds(pl.multiple_of(q * r + half * h + lo, BF16_ROWS), n)]
    return ref.at[pl.ds(pl.multiple_of(half * h + lo, BF16_ROWS), n), pl.ds(pl.multiple_of(q * ws, 128), ws)]


class Ride:
    def __init__(self, operands, out_shape, aliases, sems, make):
        self.operands, self.out_shape, self.aliases, self.sems, self.make = operands, out_shape, aliases, sems, make

    def stages(self, ins, outs, sems):
        made = self.make(ins, outs, sems)
        return made if len(made) == 4 else (made[0], None, None, made[1])


def join(rides):
    rides = [r for r in rides if r is not None]
    if len(rides) < 2:
        return rides[0] if rides else None
    aliases, at = {}, [0, 0, 0]
    cuts = []
    for r in rides:
        aliases.update({at[0] + i: at[1] + o for i, o in r.aliases.items()})
        cuts.append(tuple(at))
        at = [at[0] + len(r.operands), at[1] + len(r.out_shape), at[2] + len(r.sems)]
    cuts.append(tuple(at))

    def make(ins, outs, sem):
        made = [r.stages(ins[lo[0]:hi[0]], outs[lo[1]:hi[1]], sem[lo[2]:hi[2]]) for r, lo, hi in zip(rides, cuts, cuts[1:])]
        def all_of(k):
            def stage():
                for m in made:
                    if m[k] is not None:
                        m[k]()
            return stage

        if all(m[1] is None for m in made):
            return all_of(0), all_of(3)
        return all_of(0), all_of(1), all_of(2), all_of(3)

    return Ride(sum((list(r.operands) for r in rides), []), sum((list(r.out_shape) for r in rides), []), aliases,
                sum((list(r.sems) for r in rides), []), make)


def _call(body, *, name, grid, in_specs, out_specs, out_shape, args, scratch_shapes=(), semantics=None, ride=None,
          prefetch=None):
    pre = 0 if prefetch is None else 1
    n_in, n_out, n_scr = len(in_specs), len(out_specs), len(scratch_shapes)
    r_in, r_out = (len(ride.operands), len(ride.out_shape)) if ride is not None else (0, 0)
    a, b = pre + n_in, pre + n_in + r_in
    c, d = b + n_out, b + n_out + r_out
    e = d + n_scr

    def riding(*refs):
        start, relay, relay_again, finish = ride.stages(refs[a:b], refs[c:d], refs[e:])
        step, steps = pl.program_id(0), 1
        for k, extent in enumerate(grid):
            step = pl.program_id(k) if k == 0 else step * extent + pl.program_id(k)
            steps *= extent
        pl.when(step == 0)(start)
        if relay is not None:
            pl.when(step == steps // 2)(relay)
            pl.when(step == steps - 1)(relay_again)
        body(*refs[:a], *refs[b:c], *refs[d:e])
        pl.when(step == steps - 1)(finish)

    if ride is None:
        kernel_body, extra_in, extra_out, extra_shape, extra_scr, aliases = body, [], [], [], [], {}
        params = _params(*semantics)
    else:
        kernel_body, extra_in, extra_out = riding, [HBM_SPEC] * r_in, [HBM_SPEC] * r_out
        extra_shape, extra_scr = list(ride.out_shape), list(ride.sems)
        aliases = {pre + n_in + i: n_out + o for i, o in ride.aliases.items()}
        params = _params(*(("arbitrary",) * len(grid)))
    specs = dict(grid=grid, in_specs=list(in_specs) + extra_in, out_specs=list(out_specs) + extra_out,
                 scratch_shapes=list(scratch_shapes) + extra_scr)
    if prefetch is not None:
        specs = dict(grid_spec=pltpu.PrefetchScalarGridSpec(num_scalar_prefetch=1, **specs))
        args = (prefetch,) + tuple(args)
    outs = pl.pallas_call(kernel_body, name=name, out_shape=list(out_shape) + extra_shape,
                          input_output_aliases=aliases, compiler_params=params, **specs,
                          )(*args, *(ride.operands if ride is not None else ()))
    return outs if ride is None else (outs[:n_out], outs[n_out:])


def alone(ride, *, name):
    def body(*refs):
        n = len(ride.operands)
        stages = ride.stages(refs[:n], refs[n:n + len(ride.out_shape)], refs[n + len(ride.out_shape):])
        for stage in stages:
            if stage is not None:
                stage()

    return pl.pallas_call(
        body, name=name, in_specs=[HBM_SPEC] * len(ride.operands), out_specs=[HBM_SPEC] * len(ride.out_shape),
        out_shape=list(ride.out_shape), input_output_aliases=dict(ride.aliases), scratch_shapes=list(ride.sems),
    )(*ride.operands)


def _two_pieces(h, part):
    lo, n = (0, h // BF16_ROWS) if part is None else part
    assert n >= 2, (h, part)
    return (lo, n // 2), (lo + n // 2, n - n // 2)


def gather_ride(wholes, metas, small=None):
    n = len(wholes)
    operands, out_shape = list(wholes), [SDS(s.shape, s.dtype) for s in wholes]
    sems = [DMA((n, 4)), DMA((n, 4)), DMA((n, 4)), DMA((n, 4))]
    if small is not None:
        operands.append(small)
        out_shape.append(SDS((N_CHIPS,) + small.shape, small.dtype))
        sems += [DMA((3,)), DMA((3,)), DMA(())]

    def make(ins, outs, sem):
        send1, recv1, send2, recv2 = sem[:4]
        x, y, c = _place()
        p = 2 * x + y
        chips = _other_chips(x, y)
        across_x, across_y, across_both = [2 * qx + qy for qx, qy in chips]
        me, sibling = (x, y, c), (x, y, 1 - c)

        def region(t, q, half, piece=None):
            kind, quarter, part = metas[t]
            if piece is not None:
                part = _two_pieces(quarter[0] // 2, part)[piece]
            return _half_of_quarter(outs[t], kind, quarter, part, q, half)

        first, second, arriving = [], [], []
        landing, passing = [[], [], [], []], [[], [], [], []]
        for j, (qx, qy) in enumerate(chips):
            if small is not None:
                q = 2 * qx + qy
                first.append(_remote(ins[n], outs[n].at[p], sem[4].at[j], sem[5].at[j], (qx, qy, c)))
                arriving.append(_remote(outs[n].at[q], outs[n].at[q], sem[4].at[j], sem[5].at[j], me))
        for t in range(n):
            mine = region(t, p, c)
            for j in range(2):
                first.append(_remote(mine, mine, send1.at[t, j], recv1.at[t, j], chips[j] + (c,)))
            lands = [(across_x, None), (across_y, None), (across_both, 0), (across_both, 1)]
            for k, (q, piece) in enumerate(lands):
                landed, theirs = region(t, q, c, piece), region(t, q, 1 - c, piece)
                landing[k].append(_remote(landed, landed, send1.at[t, k], recv1.at[t, k], me))
                passing[k].append(_remote(landed, landed, send2.at[t, k], recv2.at[t, k], sibling))
                arriving.append(_remote(theirs, theirs, send2.at[t, k], recv2.at[t, k], me))
            onward = region(t, across_x, c, 0)
            second.append(_remote(onward, onward, send1.at[t, 2], recv1.at[t, 2], chips[1] + (c,)))
            onward = region(t, across_y, c, 1)
            second.append(_remote(onward, onward, send1.at[t, 3], recv1.at[t, 3], chips[0] + (c,)))
        local = [] if small is None else [pltpu.make_async_copy(ins[n], outs[n].at[p], sem[6])]

        def start():
            for cp in local + first:
                cp.start()

        def relay():
            for k in range(2):
                for t in range(n):
                    landing[k][t].wait_recv()
                    second[2 * t + k].start()
                    passing[k][t].start()

        def relay_again():
            for k in range(2, 4):
                for t in range(n):
                    landing[k][t].wait_recv()
                    passing[k][t].start()

        def finish():
            for cp in arriving:
                cp.wait_recv()
            for cp in first + second + sum(passing, []):
                cp.wait_send()
            for cp in local:
                cp.wait()

        return start, relay, relay_again, finish

    return Ride(operands, out_shape, {t: t for t in range(n)}, sems, make)


def chip_ride(sums, metas, small=None, earlier=None):
    n = len(sums)
    operands = list(sums)
    out_shape = [SDS((3, s.shape[1], quarter[1]), s.dtype) for s, (_, quarter, _) in zip(sums, metas)]
    sems = [DMA((n, 3)), DMA((n, 3))] if n else []
    if small is not None:
        operands.append(small)
        out_shape.append(SDS((8,) + small.shape, small.dtype))
        sems += [DMA((7,)), DMA((7,)), DMA(())]
    aliases = {}
    for t, buffer in enumerate(earlier or [None] * n):
        if buffer is not None:
            aliases[len(operands)] = t
            operands.append(buffer)

    def make(ins, outs, sem):
        x, y, c = _place()
        cps = []
        for j, (qx, qy) in enumerate(_other_chips(x, y)):
            q = 2 * qx + qy
            for t in range(n):
                kind, (_, ws), part = metas[t]
                rows = pl.ds(*_rows_of(ins[t].shape[1], part))
                if kind == "row":
                    src = ins[t].at[q, rows]
                elif kind == "col":
                    src = ins[t].at[0, rows, pl.ds(pl.multiple_of(q * ws, 128), ws)]
                else:
                    src = ins[t].at[q // 2, rows, pl.ds(pl.multiple_of((q % 2) * ws, 128), ws)]
                cps.append(_remote(src, outs[t].at[j, rows], sem[0].at[t, j], sem[1].at[t, j], (qx, qy, c)))
        local = []
        if small is not None:
            ssend, srecv, lsem = sem[2 * bool(n):2 * bool(n) + 3]
            local.append(pltpu.make_async_copy(ins[n], outs[n].at[0], lsem))
            for k in range(1, 8):
                peer = (x ^ (k >> 2 & 1), y ^ (k >> 1 & 1), c ^ (k & 1))
                cps.append(_remote(ins[n], outs[n].at[k], ssend.at[k - 1], srecv.at[k - 1], peer))

        def start():
            for cp in local + cps:
                cp.start()

        def finish():
            for cp in cps + local:
                cp.wait()

        return start, finish

    return Ride(operands, out_shape, aliases, sems, make)


def pair_ride(grads):
    n = len(grads)

    def make(ins, outs, sem):
        x, y, c = _place()
        cps = [_remote(ins[t].at[:, 1 - c], outs[t], sem[0].at[t], sem[1].at[t], (x, y, 1 - c)) for t in range(n)]

        def start():
            for cp in cps:
                cp.start()

        def finish():
            for cp in cps:
                cp.wait()

        return start, finish

    return Ride(list(grads), [SDS((g.shape[0],) + g.shape[2:], g.dtype) for g in grads], {}, [DMA((n,)), DMA((n,))],
                make)


def half_ride(quarters):
    n = len(quarters)

    def make(ins, outs, sem):
        x, y, c = _place()
        sends = [_remote(outs[t].at[c], outs[t].at[c], sem[0].at[t], sem[1].at[t], (x, y, 1 - c)) for t in range(n)]

        def start():
            for cp in sends:
                cp.start()

        def finish():
            for t in range(n):
                theirs = outs[t].at[1 - c]
                _remote(theirs, theirs, sem[0].at[t], sem[1].at[t], (x, y, c)).wait_recv()
            for cp in sends:
                cp.wait_send()

        return start, finish

    return Ride(list(quarters), [SDS(q.shape, q.dtype) for q in quarters], {t: t for t in range(n)},
                [DMA((n,)), DMA((n,))], make)


CAST_STEPS = 4


def cast_quarters(sources, p_arr, *, name, ride=None):
    n = len(sources)
    in_specs, out_specs, out_shape = [], [], []
    for w, layer, kind in sources:
        _, r, ws = w.shape
        tr = r // CAST_STEPS
        assert tr % BF16_ROWS == 0, w.shape
        in_specs.append(pl.BlockSpec((None, tr, ws), lambda i, p_ref, layer=layer: (layer, i, 0)))
        out_specs.append(pl.BlockSpec((tr, ws), (lambda i, p_ref: (p_ref[0] * CAST_STEPS + i, 0)) if kind == "row"
                                      else (lambda i, p_ref: (i, p_ref[0]))))
        out_shape.append(SDS(_full_shape(kind, (r, ws)), BF16))

    def body(p_ref, *refs):
        for w_ref, o_ref in zip(refs[:n], refs[n:]):
            o_ref[...] = w_ref[...].astype(BF16)

    return _call(body, name=name, grid=(CAST_STEPS,), in_specs=in_specs, out_specs=out_specs, out_shape=out_shape,
                 semantics=("parallel",), args=[w for w, _, _ in sources], ride=ride, prefetch=p_arr)


def pair_add(own, got, c_arr, *, name):
    A, _, h, W = own.shape
    th = _row_tile(h, max(BF16_ROWS, (3 << 19) // W), BF16_ROWS)

    def body(c_ref, a_ref, b_ref, o_ref):
        o_ref[...] = (a_ref[...].astype(F32) + b_ref[...].astype(F32)).astype(BF16)

    return pl.pallas_call(
        body, name=name,
        grid_spec=pltpu.PrefetchScalarGridSpec(
            num_scalar_prefetch=1, grid=(A, h // th),
            in_specs=[pl.BlockSpec((None, None, th, W), lambda q, i, c_ref: (q, c_ref[0], i, 0)),
                      pl.BlockSpec((None, th, W), lambda q, i, c_ref: (q, i, 0))],
            out_specs=pl.BlockSpec((None, th, W), lambda q, i, c_ref: (q, i, 0))),
        out_shape=SDS((A, h, W), BF16),
        compiler_params=_params("parallel", "parallel"),
    )(c_arr, own, got)


REDUCE_STEPS = 2


def chip_reduce(sums, got, kinds, pc_arr, *, name, ride=None):
    n = len(sums)
    mine = {"row": lambda i, pc_ref: (pc_ref[0], i, 0), "col": lambda i, pc_ref: (0, i, pc_ref[0]),
            "split": lambda i, pc_ref: (pc_ref[0] // 2, i, pc_ref[0] % 2)}
    a_specs, b_specs, o_specs, out_shape = [], [], [], []
    for g, kind in zip(got, kinds):
        _, h, ws = g.shape
        th = h // REDUCE_STEPS
        assert th % BF16_ROWS == 0, g.shape
        a_specs.append(pl.BlockSpec((None, th, ws), mine[kind]))
        b_specs.append(pl.BlockSpec((3, th, ws), lambda i, pc_ref: (0, i, 0)))
        o_specs.append(pl.BlockSpec((None, th, ws), lambda i, pc_ref: (pc_ref[1], i, 0)))
        out_shape.append(SDS((2, h, ws), F32))

    def body(pc_ref, *refs):
        for a_ref, b_ref, o_ref in zip(refs[:n], refs[n:2 * n], refs[2 * n:]):
            o_ref[...] = ((a_ref[...].astype(F32) + b_ref[0].astype(F32)) + b_ref[1].astype(F32)) + b_ref[2].astype(F32)

    return _call(body, name=name, grid=(REDUCE_STEPS,), in_specs=a_specs + b_specs, out_specs=o_specs,
                 out_shape=out_shape, semantics=("parallel",), args=list(sums) + list(got), prefetch=pc_arr, ride=ride)


def small_reduce(blocks, me_arr):
    _, rows, D = blocks.shape

    def body(me_ref, b_ref, o_ref):
        me = me_ref[0]
        total = b_ref[me]
        for d in range(1, 8):
            total = total + b_ref[d ^ me]
        o_ref[...] = total

    return pl.pallas_call(
        body, name="small_reduce",
        grid_spec=pltpu.PrefetchScalarGridSpec(
            num_scalar_prefetch=1, grid=(1,),
            in_specs=[pl.BlockSpec((8, rows, D), lambda i, me_ref: (0, 0, 0))],
            out_specs=pl.BlockSpec((rows, D), lambda i, me_ref: (0, 0))),
        out_shape=SDS((rows, D), F32),
        compiler_params=_params("arbitrary"),
    )(me_arr, blocks)


def adamw(w, gs, m, v, *, name):
    L, r, cols = w.shape
    tr = _row_tile(r, 256)
    nt = r // tr

    def body(*refs):
        w_ref, m_ref, v_ref = refs[:3]
        g_refs = refs[3:3 + L]
        g_out, d_out, m_out, v_out = refs[3 + L:]
        layer = pl.program_id(0)
        g = g_refs[0][...]
        for l in range(1, L):
            g = jnp.where(layer == l, g_refs[l][...], g)
        m_new = ADAM_B1 * m_ref[...] + (1.0 - ADAM_B1) * g
        v_new = ADAM_B2 * v_ref[...] + (1.0 - ADAM_B2) * (g * g)
        m_hat = m_new / (1.0 - ADAM_B1 ** ADAM_STEP)
        v_hat = v_new / (1.0 - ADAM_B2 ** ADAM_STEP)
        g_out[...] = g
        m_out[...] = m_new
        v_out[...] = v_new
        d_out[...] = -ADAM_LR * (m_hat / (jnp.sqrt(v_hat) + ADAM_EPS) + ADAM_WD * w_ref[...])

    full = pl.BlockSpec((None, tr, cols), lambda l, i: (l, i, 0))
    g_spec = lambda l0: pl.BlockSpec((tr, cols), lambda l, i: (jnp.where(l == l0, i, jnp.where(l < l0, 0, nt - 1)), 0))
    return pl.pallas_call(
        body, name=name, grid=(L, nt),
        in_specs=[full, full, full] + [g_spec(l0) for l0 in range(L)],
        out_specs=[full] * 4,
        out_shape=[SDS(w.shape, F32)] * 4,
        compiler_params=_params("arbitrary", "arbitrary"),
    )(w, m, v, *gs)


def _rms_r(xf):
    return lax.rsqrt(jnp.mean(xf * xf, axis=-1, keepdims=True) + EPS)


def _rmsnorm_bwd(xf, g, dy):
    r = _rms_r(xf)
    xh = xf * r
    gd = g * dy
    return r * (gd - xh * jnp.mean(xh * gd, axis=-1, keepdims=True)), xh


def _dot(a, b):
    return jnp.dot(a, b, preferred_element_type=F32)


def _dot_nt(a, b):
    return lax.dot_general(a, b, (((1,), (1,)), ((), ())), preferred_element_type=F32)


def _dot_tn(a, b):
    return lax.dot_general(a, b, (((0,), (0,)), ((), ())), preferred_element_type=F32)


def _accumulate(ref, first, value):
    @pl.when(first)
    def _():
        ref[...] = value

    @pl.when(jnp.logical_not(first))
    def _():
        ref[...] += value


def norm_matmul(x, g, w, *, tn, split, name, ride=None, tm=ROW_TILE):
    T, D = x.shape
    N = w.shape[1]
    per = N // split // tn

    def body(x_ref, g_ref, w_ref, o_ref, xn_ref):
        @pl.when(pl.program_id(1) == 0)
        def _():
            xf = x_ref[...].astype(F32)
            xn_ref[...] = (xf * _rms_r(xf) * g_ref[...]).astype(BF16)

        o_ref[...] = _dot(xn_ref[...], w_ref[...]).astype(BF16)

    return _call(
        body, name=name, grid=(T // tm, N // tn),
        in_specs=[pl.BlockSpec((tm, D), lambda i, j: (i, 0)),
                  pl.BlockSpec((1, D), lambda i, j: (0, 0)),
                  pl.BlockSpec((D, tn), lambda i, j: (0, j))],
        out_specs=[pl.BlockSpec((None, tm, tn), lambda i, j: (j // per, i, j % per)),
                   pl.BlockSpec((tm, D), lambda i, j: (i, 0))],
        out_shape=[SDS((split, T, N // split), BF16), SDS((T, D), BF16)],
        semantics=("parallel", "arbitrary"), args=(x, g, w), ride=ride)


BIG_ROW_TILE = 1024


def norm2_matmul(x, gains, weights, *, name, ride=None, tm=BIG_ROW_TILE):
    T, D = x.shape
    tm = min(tm, T)
    n = len(gains)

    def body(x_ref, *refs):
        subs = _sub_tiles(tm)
        xhs = []
        for rows in subs:
            xf = x_ref[rows, :].astype(F32)
            xhs.append(xf * _rms_r(xf))
        for g_ref, w_ref, o_ref, xn_ref in zip(refs[:n], refs[n:2 * n], refs[2 * n::2], refs[2 * n + 1::2]):
            for rows, xh in zip(subs, xhs):
                xn = (xh * g_ref[...]).astype(BF16)
                xn_ref[rows, :] = xn
                o_ref[rows, :] = _dot(xn, w_ref[...]).astype(BF16)

    row = pl.BlockSpec((tm, D), lambda i: (i, 0))
    vec = pl.BlockSpec((1, D), lambda i: (0, 0))
    out_specs, out_shape = [], []
    for w in weights:
        out_specs += [pl.BlockSpec((tm, w.shape[1]), lambda i: (i, 0)), row]
        out_shape += [SDS((T, w.shape[1]), BF16), SDS((T, D), BF16)]
    return _call(
        body, name=name, grid=(T // tm,),
        in_specs=[row] + [vec] * n + [pl.BlockSpec(w.shape, lambda i: (0, 0)) for w in weights],
        out_specs=out_specs, out_shape=out_shape, semantics=("parallel",), args=[x] + list(gains) + list(weights),
        ride=ride)


def _shift_down(prev, cur, by):
    big = jnp.concatenate([prev, cur], axis=0)
    return pltpu.roll(big, by, 0)[prev.shape[0]:]


def _shift_up(cur, nxt, by):
    big = jnp.concatenate([cur, nxt], axis=0)
    return pltpu.roll(big, big.shape[0] - by, 0)[:cur.shape[0]]


def conv_mix_out(bcx, conv_w, w_out, g_post, res, *, name, ride=None, tm=ROW_TILE):
    T, D = res.shape
    hb = tm // BF16_ROWS

    def body(b_ref, c_ref, u_ref, cp_ref, up_ref, cw_ref, w_ref, g_ref, r_ref, h_ref, z_ref, y_ref):
        i = pl.program_id(0)
        cu = c_ref[...].astype(F32) * u_ref[...].astype(F32)
        cup = cp_ref[...].astype(F32) * up_ref[...].astype(F32)
        cup = jnp.where(i == 0, 0.0, cup)
        cv = (cw_ref[0:1, :] * _shift_down(cup, cu, 2) + cw_ref[1:2, :] * _shift_down(cup, cu, 1)
              + cw_ref[2:3, :] * cu)
        y = (b_ref[...].astype(F32) * cv).astype(BF16)
        y_ref[...] = y
        z = _dot(y, w_ref[...])
        z_ref[...] = z.astype(BF16)
        h_ref[...] = (r_ref[...] + z * _rms_r(z) * g_ref[...]).astype(STREAM)

    tile = lambda col: pl.BlockSpec((tm, D), lambda i: (i, col))
    halo = lambda col: pl.BlockSpec((BF16_ROWS, D), lambda i: (jnp.maximum(i * hb - 1, 0), col))
    row = pl.BlockSpec((tm, D), lambda i: (i, 0))
    return _call(
        body, name=name, grid=(T // tm,),
        in_specs=[tile(0), tile(1), tile(2), halo(1), halo(2),
                  pl.BlockSpec((3, D), lambda i: (0, 0)),
                  pl.BlockSpec((D, D), lambda i: (0, 0)),
                  pl.BlockSpec((1, D), lambda i: (0, 0)), row],
        out_specs=[row, row, row],
        out_shape=[SDS((T, D), STREAM), SDS((T, D), BF16), SDS((T, D), BF16)],
        semantics=("parallel",), args=(bcx, bcx, bcx, bcx, bcx, conv_w, w_out, g_post, res), ride=ride)


def _normbwd_then_nt(dh, zf, g_ref, w_ref, dz_ref, dg_ref, o_ref, first):
    dz, zh = _rmsnorm_bwd(zf, g_ref[...], dh)
    dz = dz.astype(BF16)
    dz_ref[...] = dz
    _accumulate(dg_ref, first, jnp.sum(dh * zh, axis=0, keepdims=True))
    o_ref[...] = _dot_nt(dz, w_ref[...]).astype(BF16)


def _then_specs(then, tm, T, D):
    z, g, w = then
    K = w.shape[0]
    row = pl.BlockSpec((tm, D), lambda i: (i, 0))
    vec = pl.BlockSpec((1, D), lambda i: (0, 0))
    in_specs = [row, vec, pl.BlockSpec((K, D), lambda i: (0, 0), pipeline_mode=pl.Buffered(1))]
    out_specs = [row, vec, pl.BlockSpec((tm, K), lambda i: (i, 0))]
    out_shape = [SDS((T, D), BF16), SDS((1, D), F32), SDS((T, K), BF16)]
    return in_specs, out_specs, out_shape


def plain_mix_out(a, w, g_post, res, *, name, target=None, ride=None, tm=ROW_TILE):
    T, D = res.shape
    tm = min(tm, T)
    K = a.shape[1]
    with_loss = target is not None

    def body(a_ref, w_ref, g_ref, r_ref, *rest):
        subs = _sub_tiles(tm)
        zs = [_dot(a_ref[rows, :], w_ref[...]) for rows in subs]
        if not with_loss:
            h_ref, z_ref = rest
            for rows, z in zip(subs, zs):
                h_ref[rows, :] = (r_ref[rows, :].astype(F32) + z * _rms_r(z) * g_ref[...]).astype(STREAM)
                z_ref[rows, :] = z.astype(BF16)
            return
        t_ref, h_ref, dz_ref, dg_ref, da_ref, loss_ref = rest
        first = pl.program_id(0) == 0
        loss, dg = jnp.zeros((), F32), jnp.zeros((1, D), F32)
        for rows, z in zip(subs, zs):
            diff = r_ref[rows, :].astype(F32) + z * _rms_r(z) * g_ref[...] - t_ref[rows, :]
            dh = diff * (1.0 / D)
            h_ref[rows, :] = dh.astype(STREAM)
            loss = loss + jnp.sum(diff * diff)
            dz, zh = _rmsnorm_bwd(z, g_ref[...], dh)
            dz = dz.astype(BF16)
            dz_ref[rows, :] = dz
            dg = dg + jnp.sum(dh * zh, axis=0, keepdims=True)
            da_ref[rows, :] = _dot_nt(dz, w_ref[...]).astype(BF16)
        _accumulate(loss_ref, first, jnp.full(loss_ref.shape, 0.5 / D, F32) * loss)
        _accumulate(dg_ref, first, dg)

    row = pl.BlockSpec((tm, D), lambda i: (i, 0))
    vec = pl.BlockSpec((1, D), lambda i: (0, 0))
    in_specs = [pl.BlockSpec((tm, K), lambda i: (i, 0)), pl.BlockSpec((K, D), lambda i: (0, 0)), vec, row]
    if with_loss:
        in_specs.append(row)
        out_specs = [row, row, vec, pl.BlockSpec((tm, K), lambda i: (i, 0)), pl.BlockSpec((8, 128), lambda i: (0, 0))]
        out_shape = [SDS((T, D), STREAM), SDS((T, D), BF16), SDS((1, D), F32), SDS((T, K), BF16), SDS((8, 128), F32)]
    else:
        out_specs, out_shape = [row, row], [SDS((T, D), STREAM), SDS((T, D), BF16)]
    return _call(
        body, name=name, grid=(T // tm,), in_specs=in_specs, out_specs=out_specs, out_shape=out_shape,
        semantics=("arbitrary",), args=(a, w, g_post, res) + ((target,) if with_loss else ()), ride=ride)


def _silu_grads(d, g, u):
    sg = jax.nn.sigmoid(g)
    return d * u * (sg * (1.0 + g * (1.0 - sg))), d * (g * sg)


def _sub_tiles(tm):
    return [pl.ds(k, min(MXU_WIDTH, tm)) for k in range(0, tm, MXU_WIDTH)]


def norm_swiglu_in(x, g, w, *, name, ride=None, tm=ROW_TILE):
    T, D = x.shape
    F = w.shape[1] // 2

    def body(x_ref, g_ref, wg_ref, wu_ref, gu_ref, a_ref, xt_ref):
        subs = _sub_tiles(tm)
        xns = []
        for rows in subs:
            xf = x_ref[rows, :].astype(F32)
            xns.append(xf * _rms_r(xf) * g_ref[...])
        xbs = [xn.astype(BF16) for xn in xns]
        gates = [_dot(xb, wg_ref[...]).astype(BF16) for xb in xbs]
        ups = [_dot(xb, wu_ref[...]).astype(BF16) for xb in xbs]
        for rows, gate, up in zip(subs, gates, ups):
            gu_ref[0, rows, :] = gate
            gu_ref[1, rows, :] = up
            a_ref[rows, :] = gate * jax.nn.sigmoid(gate) * up
        for rows, xn in zip(subs, xns):
            xt_ref[:, rows] = xn.T.astype(BF16)

    half = lambda s: pl.BlockSpec((D, F), lambda i: (0, s), pipeline_mode=pl.Buffered(1))
    return _call(
        body, name=name, grid=(T // tm,),
        in_specs=[pl.BlockSpec((tm, D), lambda i: (i, 0)), pl.BlockSpec((1, D), lambda i: (0, 0)), half(0), half(1)],
        out_specs=[pl.BlockSpec((2, tm, F), lambda i: (0, i, 0)), pl.BlockSpec((tm, F), lambda i: (i, 0)),
                   pl.BlockSpec((D, tm), lambda i: (0, i))],
        out_shape=[SDS((2, T, F), BF16), SDS((T, F), BF16), SDS((D, T), BF16)],
        semantics=("parallel",), args=(x, g, w, w), ride=ride)


def swiglu_bwd_tn(xt, dact, gu, *, name, ride=None, tb=MXU_WIDTH):
    D, T = xt.shape
    F = dact.shape[1]

    def body(xt_ref, d_ref, g_ref, u_ref, o_ref):
        dg, du = _silu_grads(d_ref[...], g_ref[...], u_ref[...])
        o_ref[0] = _dot(xt_ref[...], dg).astype(BF16)
        o_ref[1] = _dot(xt_ref[...], du).astype(BF16)

    col = lambda s: pl.BlockSpec((None, T, tb), lambda j: (s, 0, j))
    out = _call(
        body, name=name, grid=(F // tb,),
        in_specs=[pl.BlockSpec((D, T), lambda j: (0, 0), pipeline_mode=pl.Buffered(1)),
                  pl.BlockSpec((T, tb), lambda j: (0, j)), col(0), col(1)],
        out_specs=[pl.BlockSpec((2, D, tb), lambda j: (0, 0, j))],
        out_shape=[SDS((2, D, F), BF16)],
        semantics=("parallel",), args=(xt, dact, gu, gu), ride=ride)
    return out[0] if ride is None else (out[0][0], out[1])


def swiglu_bwd_in(dact, gu, w, h_in, g, dh_out, then, *, name, ride=None, tm=ROW_TILE):
    T, D = h_in.shape
    F = dact.shape[1]

    def body(d_ref, gg_ref, uu_ref, wg_ref, wu_ref, h_ref, g_ref, dh_ref, z_ref, g2_ref, w2_ref,
             o_ref, dg_ref, dz_ref, dg2_ref, da_ref):
        first = pl.program_id(0) == 0
        subs = _sub_tiles(tm)
        dns = []
        for rows in subs:
            dgate, dup = _silu_grads(d_ref[rows, :], gg_ref[rows, :], uu_ref[rows, :])
            dns.append(_dot_nt(dgate, wg_ref[...]) + _dot_nt(dup, wu_ref[...]))
        dg, dg2 = jnp.zeros((1, D), F32), jnp.zeros((1, D), F32)
        for rows, dn in zip(subs, dns):
            dx, hh = _rmsnorm_bwd(h_ref[rows, :].astype(F32), g_ref[...], dn)
            dh_in = dh_ref[rows, :] + dx
            o_ref[rows, :] = dh_in.astype(STREAM)
            dg = dg + jnp.sum(dn * hh, axis=0, keepdims=True)
            dz, zh = _rmsnorm_bwd(z_ref[rows, :].astype(F32), g2_ref[...], dh_in)
            dz = dz.astype(BF16)
            dz_ref[rows, :] = dz
            dg2 = dg2 + jnp.sum(dh_in * zh, axis=0, keepdims=True)
            da_ref[rows, :] = _dot_nt(dz, w2_ref[...]).astype(BF16)
        _accumulate(dg_ref, first, dg)
        _accumulate(dg2_ref, first, dg2)

    row = pl.BlockSpec((tm, D), lambda i: (i, 0))
    vec = pl.BlockSpec((1, D), lambda i: (0, 0))
    part = lambda s: pl.BlockSpec((None, tm, F), lambda i: (s, i, 0))
    half = lambda s: pl.BlockSpec((D, F), lambda i: (0, s), pipeline_mode=pl.Buffered(1))
    then_in, then_out, then_shape = _then_specs(then, tm, T, D)
    return _call(
        body, name=name, grid=(T // tm,),
        in_specs=[pl.BlockSpec((tm, F), lambda i: (i, 0)), part(0), part(1), half(0), half(1), row, vec, row] + then_in,
        out_specs=[row, vec] + then_out,
        out_shape=[SDS((T, D), STREAM), SDS((1, D), F32)] + then_shape,
        semantics=("arbitrary",), args=(dact, gu, gu, w, w, h_in, g, dh_out) + tuple(then), ride=ride)


def rope_tables(T):
    half = ROT_DIM // 2
    inv_freq = ROPE_THETA ** (-jnp.arange(0, ROT_DIM, 2, dtype=F32) / ROT_DIM)
    ang = (jnp.arange(T, dtype=F32)[:, None] * inv_freq[None, :]).T
    cos, sin = jnp.cos(ang), jnp.sin(ang)
    rest = HEAD_DIM - ROT_DIM
    one, zero = jnp.ones((rest, T), F32), jnp.zeros((rest, T), F32)
    zh = jnp.zeros((half, T), F32)
    fac = jnp.concatenate([cos, cos, one], axis=0)
    up = jnp.concatenate([-sin, zh, zero], axis=0)
    down = jnp.concatenate([zh, sin, zero], axis=0)
    return jnp.stack([fac, up, down])


def _rope(t, tab):
    half = ROT_DIM // 2
    return t * tab[0] + pltpu.roll(t, HEAD_DIM - half, 0) * tab[1] + pltpu.roll(t, half, 0) * tab[2]


def _rope_t(d, tab):
    half = ROT_DIM // 2
    return d * tab[0] + pltpu.roll(d * tab[1], half, 0) + pltpu.roll(d * tab[2], HEAD_DIM - half, 0)


def _head(t, h):
    return t[h * HEAD_DIM:(h + 1) * HEAD_DIM]


def _band(n, group):
    kj = lax.broadcasted_iota(jnp.int32, (2 * BLOCK, BLOCK), 0)
    qi = lax.broadcasted_iota(jnp.int32, (2 * BLOCK, BLOCK), 1)
    mask = (kj > qi) & (kj <= qi + BLOCK) & ((n > 0) | (kj >= BLOCK))
    return jnp.tile(mask, (1, group))


def _attn_specs(D, kvd, nb):
    cur = lambda n: jnp.minimum(n, nb - 1)
    prev = lambda n: jnp.maximum(cur(n) - 1, 0)
    return [pl.BlockSpec((BLOCK, D), lambda n: (cur(n), 0)),
            pl.BlockSpec((BLOCK, kvd), lambda n: (prev(n), 0)),
            pl.BlockSpec((BLOCK, kvd), lambda n: (cur(n), 0)),
            pl.BlockSpec((BLOCK, kvd), lambda n: (prev(n), 1)),
            pl.BlockSpec((BLOCK, kvd), lambda n: (cur(n), 1)),
            pl.BlockSpec((3, HEAD_DIM, BLOCK), lambda n: (0, 0, prev(n))),
            pl.BlockSpec((3, HEAD_DIM, BLOCK), lambda n: (0, 0, cur(n))),
            pl.BlockSpec(memory_space=pltpu.SMEM)]


def _attn_operands(q_ref, kp_ref, k_ref, vp_ref, v_ref, tp_ref, t_ref):
    flip = lambda ref: ref[...].astype(F32).T
    tab = t_ref[...]
    kt = jnp.concatenate([flip(kp_ref), flip(k_ref)], axis=1)
    vt = jnp.concatenate([flip(vp_ref), flip(v_ref)], axis=1)
    return flip(q_ref), kt, vt, tab, jnp.concatenate([tp_ref[...], tab], axis=2)


SCORE_SCALE = 1.0 / math.sqrt(HEAD_DIM)
HEADS_TOGETHER = 4


def _group_heads(t, first, count, tab=None):
    heads = [_head(t, first + g) for g in range(count)]
    if tab is not None:
        heads = [_rope(h, tab) * SCORE_SCALE for h in heads]
    return jnp.concatenate(heads, axis=1).astype(BF16)


def _sink_row(s_ref, first, count):
    which = lax.broadcasted_iota(jnp.int32, (1, count * BLOCK), 1) // BLOCK
    row = jnp.zeros((1, count * BLOCK), F32)
    for g in range(count):
        row = jnp.where(which == g, s_ref[0, first + g], row)
    return row


def _sum_keys(t):
    return _dot(jnp.ones((8, t.shape[0]), BF16), t)[0:1]


def _softmax(scores, sink, mask):
    s = jnp.where(mask, scores.astype(BF16), NEG)
    m = jnp.maximum(jnp.max(s, axis=0, keepdims=True).astype(F32), sink).astype(BF16)
    e = jnp.exp(s - m)
    m = m.astype(F32)
    return e, m, 1.0 / (_sum_keys(e) + jnp.exp(sink - m))


def _per_head(row, count):
    return [row[:, g * BLOCK:(g + 1) * BLOCK] for g in range(count)]


def attention_fwd(q, kv, tabs, sinks, *, name, ride=None):
    T, D = q.shape
    kvd = kv.shape[1] // 2
    heads = D // HEAD_DIM
    group = heads // N_KV_HEADS

    def body(q_ref, kp_ref, k_ref, vp_ref, v_ref, tp_ref, t_ref, s_ref, o_ref, stat_ref):
        gs = HEADS_TOGETHER
        mask = _band(pl.program_id(0), gs)
        qt, kt, vt, tab, tab2 = _attn_operands(q_ref, kp_ref, k_ref, vp_ref, v_ref, tp_ref, t_ref)
        firsts = [(j, first) for j in range(N_KV_HEADS) for first in range(j * group, (j + 1) * group, gs)]
        ks = [_rope(_head(kt, j), tab2).astype(BF16) for j in range(N_KV_HEADS)]
        scores = [_dot_tn(ks[j], _group_heads(qt, first, gs, tab)) for j, first in firsts]
        soft = [_softmax(s, _sink_row(s_ref, first, gs), mask) for s, (j, first) in zip(scores, firsts)]
        outs, ms, invs = [], [], []
        for (e, m, inv), (j, first) in zip(soft, firsts):
            o = _dot(_head(vt, j).astype(BF16), e) * inv
            outs += [o[:, g * BLOCK:(g + 1) * BLOCK] for g in range(gs)]
            ms += _per_head(m, gs)
            invs += _per_head(inv, gs)
        o_ref[...] = jnp.concatenate(outs, axis=0).T.astype(BF16)
        stat_ref[0] = jnp.concatenate(ms, axis=0)
        stat_ref[1] = jnp.concatenate(invs, axis=0)

    return _call(
        body, name=name, grid=(T // BLOCK,),
        in_specs=_attn_specs(D, kvd, T // BLOCK),
        out_specs=[pl.BlockSpec((BLOCK, D), lambda n: (n, 0)), pl.BlockSpec((2, heads, BLOCK), lambda n: (0, 0, n))],
        out_shape=[SDS((T, D), BF16), SDS((2, heads, T), F32)],
        semantics=("parallel",), args=(q, kv, kv, kv, kv, tabs, tabs, sinks), ride=ride)


def attention_bwd(q, kv, tabs, sinks, do, o, stats, *, name, ride=None):
    T, D = q.shape
    kvd = kv.shape[1] // 2
    heads = D // HEAD_DIM
    group = heads // N_KV_HEADS
    nb = T // BLOCK

    def body(q_ref, kp_ref, k_ref, vp_ref, v_ref, tp_ref, t_ref, s_ref, do_ref, o_ref, stat_ref,
             dq_ref, dkv_ref, ds_ref, carry):
        n = pl.program_id(0)

        @pl.when(n == 0)
        def _():
            carry[...] = jnp.zeros_like(carry)

        @pl.when(n < nb)
        def _():
            block(n, q_ref, kp_ref, k_ref, vp_ref, v_ref, tp_ref, t_ref, s_ref, do_ref, o_ref, stat_ref,
                  dq_ref, dkv_ref, ds_ref, carry)

        @pl.when(n == nb)
        def _():
            dkv_ref[...] = carry[...].astype(BF16)

    def block(n, q_ref, kp_ref, k_ref, vp_ref, v_ref, tp_ref, t_ref, s_ref, do_ref, o_ref, stat_ref,
              dq_ref, dkv_ref, ds_ref, carry):
        gs = HEADS_TOGETHER
        mask = _band(n, gs)
        qt, kt, vt, tab, tab2 = _attn_operands(q_ref, kp_ref, k_ref, vp_ref, v_ref, tp_ref, t_ref)
        dot = do_ref[...].astype(F32).T
        odo = o_ref[...].astype(F32).T * dot
        dl_all = jnp.concatenate([jnp.sum(_head(odo, h), axis=0, keepdims=True) for h in range(heads)], axis=0)
        m_all, inv_all = stat_ref[0], stat_ref[1]
        row = lambda t, first: jnp.concatenate([t[first + g:first + g + 1] for g in range(gs)], axis=1)
        lane = lax.broadcasted_iota(jnp.int32, (8, 128), 1)
        dsink = jnp.zeros((8, 128), F32)
        firsts = [(j, first) for j in range(N_KV_HEADS) for first in range(j * group, (j + 1) * group, gs)]
        ks = [_rope(_head(kt, j), tab2).astype(BF16) for j in range(N_KV_HEADS)]
        vs = [_head(vt, j).astype(BF16) for j in range(N_KV_HEADS)]
        qs = [_group_heads(qt, first, gs, tab) for _, first in firsts]
        dos = [_group_heads(dot, first, gs) for _, first in firsts]
        scores = [_dot_tn(ks[j], q) for q, (j, _) in zip(qs, firsts)]
        dps = [_dot_tn(vs[j], do) for do, (j, _) in zip(dos, firsts)]
        ps, dscs = [], []
        for s, dp, (j, first) in zip(scores, dps, firsts):
            m, inv, dl = row(m_all, first), row(inv_all, first), row(dl_all, first)
            e = jnp.exp(jnp.where(mask, s.astype(BF16), NEG) - m.astype(BF16))
            p = e * inv.astype(BF16)
            dscs.append(p * (dp.astype(BF16) - dl.astype(BF16)))
            ps.append(p)
            weight = jnp.exp(_sink_row(s_ref, first, gs) - m) * inv * dl
            for g in range(gs):
                dsink = dsink - jnp.where(lane == first + g, jnp.sum(weight[:, g * BLOCK:(g + 1) * BLOCK]), 0.0)
        dqs = []
        dks = [jnp.zeros((HEAD_DIM, 2 * BLOCK), F32) for _ in range(N_KV_HEADS)]
        dvs = [jnp.zeros((HEAD_DIM, 2 * BLOCK), F32) for _ in range(N_KV_HEADS)]
        for p, dsc, q, do, (j, _) in zip(ps, dscs, qs, dos, firsts):
            dq = _dot(ks[j], dsc) * SCORE_SCALE
            dqs += [_rope_t(dq[:, g * BLOCK:(g + 1) * BLOCK], tab) for g in range(gs)]
            dks[j] = dks[j] + _dot_nt(q, dsc)
            dvs[j] = dvs[j] + _dot_nt(do, p)
        dks = [_rope_t(dk, tab2) for dk in dks]
        dq_ref[...] = jnp.concatenate(dqs, axis=0).T.astype(BF16)
        dkv = jnp.concatenate(dks + dvs, axis=0)
        dkv_ref[...] = (carry[...] + dkv[:, :BLOCK].T).astype(BF16)
        carry[...] = dkv[:, BLOCK:].T
        _accumulate(ds_ref, n == 0, dsink)

    cur = lambda n: jnp.minimum(n, nb - 1)
    blk = lambda w: pl.BlockSpec((BLOCK, w), lambda n: (cur(n), 0))
    return _call(
        body, name=name, grid=(nb + 1,),
        in_specs=_attn_specs(D, kvd, nb) + [blk(D), blk(D), pl.BlockSpec((2, heads, BLOCK), lambda n: (0, 0, cur(n)))],
        out_specs=[blk(D), pl.BlockSpec((BLOCK, 2 * kvd), lambda n: (jnp.maximum(n - 1, 0), 0)),
                   pl.BlockSpec((8, 128), lambda n: (0, 0))],
        out_shape=[SDS((T, D), BF16), SDS((T, 2 * kvd), BF16), SDS((8, 128), F32)],
        scratch_shapes=[pltpu.VMEM((BLOCK, 2 * kvd), F32)],
        semantics=("arbitrary",), args=(q, kv, kv, kv, kv, tabs, tabs, sinks, do, o, stats), ride=ride)


def matmul_nt_normbwd(da, w, h_in, g, dh_out, *, name, ride=None, tm=ROW_TILE):
    T, D = h_in.shape
    S, _, K = da.shape

    def body(*refs):
        da_refs, w_refs = refs[:S], refs[S:2 * S]
        h_ref, g_ref, dh_ref, o_ref, dg_ref = refs[2 * S:]
        subs = _sub_tiles(tm)
        dns = []
        for rows in subs:
            dn = _dot_nt(da_refs[0][rows, :], w_refs[0][...])
            for s in range(1, S):
                dn = dn + _dot_nt(da_refs[s][rows, :], w_refs[s][...])
            dns.append(dn)
        dg = jnp.zeros((1, D), F32)
        for rows, dn in zip(subs, dns):
            dx, hh = _rmsnorm_bwd(h_ref[rows, :].astype(F32), g_ref[...], dn)
            o_ref[rows, :] = dh_ref[rows, :] + dx
            dg = dg + jnp.sum(dn * hh, axis=0, keepdims=True)
        _accumulate(dg_ref, pl.program_id(0) == 0, dg)

    row = pl.BlockSpec((tm, D), lambda i: (i, 0))
    vec = pl.BlockSpec((1, D), lambda i: (0, 0))
    part = lambda s: pl.BlockSpec((None, tm, K), lambda i: (s, i, 0))
    cols = lambda s: pl.BlockSpec((D, K), lambda i: (0, s), pipeline_mode=pl.Buffered(1))
    return _call(
        body, name=name, grid=(T // tm,),
        in_specs=[part(s) for s in range(S)] + [cols(s) for s in range(S)] + [row, vec, row],
        out_specs=[row, vec],
        out_shape=[SDS((T, D), F32), SDS((1, D), F32)],
        semantics=("arbitrary",), args=[da] * S + [w] * S + [h_in, g, dh_out], ride=ride)


def matmuls_nt_normbwd(das, ws, h_in, gs, dh_out, then, *, name, ride=None, tm=ROW_TILE):
    T, D = h_in.shape
    tm = min(tm, T)
    n = len(das)

    def body(*refs):
        da_refs, w_refs, g_refs = refs[:n], refs[n:2 * n], refs[2 * n:3 * n]
        h_ref, dh_ref, z_ref, g2_ref, w2_ref, o_ref = refs[3 * n:3 * n + 6]
        dg_refs, (dz_ref, dg2_ref, da_ref) = refs[3 * n + 6:4 * n + 6], refs[4 * n + 6:]
        first = pl.program_id(0) == 0
        subs = _sub_tiles(tm)
        dns = [[_dot_nt(da_ref_[rows, :], w_ref[...]) for da_ref_, w_ref in zip(da_refs, w_refs)] for rows in subs]
        dgs, dg2 = [jnp.zeros((1, D), F32) for _ in range(n)], jnp.zeros((1, D), F32)
        for rows, dn_sub in zip(subs, dns):
            hf = h_ref[rows, :].astype(F32)
            r = _rms_r(hf)
            hh = hf * r
            total = dh_ref[rows, :].astype(F32)
            for b, (dn, g_ref) in enumerate(zip(dn_sub, g_refs)):
                gd = g_ref[...] * dn
                total = total + r * (gd - hh * jnp.mean(hh * gd, axis=-1, keepdims=True))
                dgs[b] = dgs[b] + jnp.sum(dn * hh, axis=0, keepdims=True)
            o_ref[rows, :] = total.astype(STREAM)
            dz, zh = _rmsnorm_bwd(z_ref[rows, :].astype(F32), g2_ref[...], total)
            dz = dz.astype(BF16)
            dz_ref[rows, :] = dz
            dg2 = dg2 + jnp.sum(total * zh, axis=0, keepdims=True)
            da_ref[rows, :] = _dot_nt(dz, w2_ref[...]).astype(BF16)
        for dg_ref, dg in zip(dg_refs + (dg2_ref,), dgs + [dg2]):
            _accumulate(dg_ref, first, dg)

    row = pl.BlockSpec((tm, D), lambda i: (i, 0))
    vec = pl.BlockSpec((1, D), lambda i: (0, 0))
    then_in, then_out, then_shape = _then_specs(then, tm, T, D)
    return _call(
        body, name=name, grid=(T // tm,),
        in_specs=[pl.BlockSpec((tm, da.shape[1]), lambda i: (i, 0)) for da in das]
        + [pl.BlockSpec(w.shape, lambda i: (0, 0)) for w in ws] + [vec] * n + [row, row] + then_in,
        out_specs=[row] + [vec] * n + then_out,
        out_shape=[SDS((T, D), STREAM)] + [SDS((1, D), F32)] * n + then_shape,
        semantics=("arbitrary",), args=list(das) + list(ws) + list(gs) + [h_in, dh_out] + list(then), ride=ride)


def matmul_tn(a, b, *, tb, name, ride=None, ta=MXU_WIDTH):
    T, Ka = a.shape
    S, _, Nb = b.shape
    per = Nb // tb

    def body(a_ref, b_ref, o_ref):
        o_ref[...] = _dot_tn(a_ref[...], b_ref[...]).astype(BF16)

    out = _call(
        body, name=name, grid=(S * per, Ka // ta),
        in_specs=[pl.BlockSpec((T, ta), lambda j, i: (0, i)),
                  pl.BlockSpec((None, T, tb), lambda j, i: (j // per, 0, j % per))],
        out_specs=[pl.BlockSpec((ta, tb), lambda j, i: (i, j))],
        out_shape=[SDS((Ka, S * Nb), BF16)],
        semantics=("parallel", "parallel"), args=(a, b), ride=ride)
    return out[0] if ride is None else (out[0][0], out[1])


def conv_bwd(dy, bcx, conv_w, *, name, ride=None, tm=ROW_TILE):
    T, D = dy.shape
    nt = T // tm
    hb = tm // BF16_ROWS
    last = T // BF16_ROWS - 1

    def body(dy_ref, dyn_ref, b_ref, bn_ref, c_ref, u_ref, cp_ref, up_ref, cw_ref, o_ref, dw_ref):
        i = pl.program_id(0)
        c, u = c_ref[...].astype(F32), u_ref[...].astype(F32)
        cu = c * u
        cup = jnp.where(i == 0, 0.0, cp_ref[...].astype(F32) * up_ref[...].astype(F32))
        cu1, cu2 = _shift_down(cup, cu, 1), _shift_down(cup, cu, 2)
        w0, w1, w2 = cw_ref[0:1, :], cw_ref[1:2, :], cw_ref[2:3, :]
        dyf = dy_ref[...].astype(F32)
        o_ref[:, 0:D] = (dyf * (w0 * cu2 + w1 * cu1 + w2 * cu)).astype(BF16)
        dcv = dyf * b_ref[...].astype(F32)
        dcvn = jnp.where(i == nt - 1, 0.0, dyn_ref[...].astype(F32) * bn_ref[...].astype(F32))
        dcu = w2 * dcv + w1 * _shift_up(dcv, dcvn, 1) + w0 * _shift_up(dcv, dcvn, 2)
        o_ref[:, D:2 * D] = (dcu * u).astype(BF16)
        o_ref[:, 2 * D:3 * D] = (dcu * c).astype(BF16)
        row = lax.broadcasted_iota(jnp.int32, (8, D), 0)
        dw = jnp.zeros((8, D), F32)
        for tap, t in enumerate((cu2, cu1, cu)):
            dw = jnp.where(row == tap, jnp.sum(dcv * t, axis=0, keepdims=True), dw)
        _accumulate(dw_ref, i == 0, dw)

    tile = lambda col: pl.BlockSpec((tm, D), lambda i: (i, col))
    prev = lambda col: pl.BlockSpec((BF16_ROWS, D), lambda i: (jnp.maximum(i * hb - 1, 0), col))
    nxt = lambda col: pl.BlockSpec((BF16_ROWS, D), lambda i: (jnp.minimum((i + 1) * hb, last), col))
    return _call(
        body, name=name, grid=(nt,),
        in_specs=[tile(0), nxt(0), tile(0), nxt(0), tile(1), tile(2), prev(1), prev(2),
                  pl.BlockSpec((3, D), lambda i: (0, 0))],
        out_specs=[pl.BlockSpec((tm, 3 * D), lambda i: (i, 0)), pl.BlockSpec((8, D), lambda i: (0, 0))],
        out_shape=[SDS((T, 3 * D), BF16), SDS((8, D), F32)],
        semantics=("arbitrary",), args=(dy, dy, bcx, bcx, bcx, bcx, bcx, bcx, conv_w), ride=ride)


class NoTraffic:
    def ride(self, kernel_name):
        return None

    def landed(self, kernel_name, results, wts):
        pass

    def grad(self, key, value):
        pass


def local_step(x, target, wts, vec, traffic):
    T, D = x.shape
    tabs = rope_tables(T)
    small = {}

    def run(builder, *args, name, **kw):
        ride = traffic.ride(name)
        if ride is None:
            return builder(*args, name=name, **kw)
        out, extra = builder(*args, name=name, ride=ride, **kw)
        traffic.landed(name, extra, wts)
        return out

    bcx, xn1 = run(norm_matmul, x, vec["a_pre"], wts["w_in"], tn=3 * D, split=1, name="a_in")
    bcx = bcx[0]
    h1, z0, y0 = run(conv_mix_out, bcx, vec["conv_w"], wts["w_out"], vec["a_post"], x, name="a_out")
    gu0, act0, xt2 = run(norm_swiglu_in, h1, vec["ffn_pre0"], wts["gu0"], name="ffn0_in")
    h2, z1 = run(plain_mix_out, act0, wts["wd0"], vec["ffn_post0"], h1, name="ffn0_out")
    kvp, xkv, qp, xq = run(norm2_matmul, h2, [vec["kv_norm"], vec["b_pre"]], [wts["w_kv"], wts["w_q"]],
                           name="kvq_in")
    attn, attn_stats = run(attention_fwd, qp, kvp, tabs, vec["sinks"], name="attn_fwd")
    h3, z2 = plain_mix_out(attn, wts["w_o"], vec["b_post"], h2, name="attn_out", tm=BIG_ROW_TILE)
    gu1, act1, xt3 = run(norm_swiglu_in, h3, vec["ffn_pre1"], wts["gu1"], name="ffn1_in")
    dy, dz3, small["ffn_post1"], dact1, loss = plain_mix_out(act1, wts["wd1"], vec["ffn_post1"], h3, name="ffn1_out",
                                                             target=target)

    def ffn_bwd(layer, dz, dact, gu, act, xt, h_in, dh, then, gu_first):
        tag = "ffn%d" % layer
        dwd = lambda: traffic.grad("wd%d" % layer, run(matmul_tn, act, dz[None], tb=D, name=tag + "_dwd"))
        dwgu = lambda: traffic.grad("gu%d" % layer, run(swiglu_bwd_tn, xt, dact, gu, name=tag + "_dwgu"))
        for step in ((dwgu, dwd) if gu_first else (dwd, dwgu)):
            step()
        dh_in, small["ffn_pre%d" % layer], dz_, dg_, da_ = run(
            swiglu_bwd_in, dact, gu, wts["gu%d" % layer], h_in, vec["ffn_pre%d" % layer], dh, then,
            name=tag + "_in_bwd")
        return dh_in, dz_, dg_, da_

    dh3, dz2, small["b_post"], dattn = ffn_bwd(1, dz3, dact1, gu1, act1, xt3, h3, dy,
                                               (z2, vec["b_post"], wts["w_o"]), gu_first=False)
    traffic.grad("w_o", matmul_tn(attn, dz2[None], tb=D, name="attn_dwo"))
    dq, dkv, small["sinks"] = run(attention_bwd, qp, kvp, tabs, vec["sinks"], dattn, attn, attn_stats,
                                  name="attn_bwd")
    traffic.grad("w_q", matmul_tn(xq, dq[None], tb=D, name="attn_dwq"))
    traffic.grad("w_kv", matmul_tn(xkv, dkv[None], tb=dkv.shape[1], name="attn_dwkv"))
    dh2, small["b_pre"], small["kv_norm"], dz1, small["ffn_post0"], dact0 = run(
        matmuls_nt_normbwd, [dq, dkv], [wts["w_q"], wts["w_kv"]], h2, [vec["b_pre"], vec["kv_norm"]], dh3,
        (z1, vec["ffn_post0"], wts["wd0"]), name="qkv_in_bwd")
    dh1, dz0, small["a_post"], dyc = ffn_bwd(0, dz1, dact0, gu0, act0, xt2, h1, dh2,
                                             (z0, vec["a_post"], wts["w_out"]), gu_first=True)
    traffic.grad("w_out", run(matmul_tn, y0, dz0[None], tb=D, name="a_dwout"))
    dbcx, small["conv_w"] = run(conv_bwd, dyc, bcx, vec["conv_w"], name="a_conv_bwd")
    traffic.grad("w_in", run(matmul_tn, xn1, dbcx[None], tb=3 * D // 2, name="a_dwin"))
    dx, small["a_pre"] = run(matmul_nt_normbwd, dbcx[None], wts["w_in"], x, vec["a_pre"], dh1, name="a_in_bwd")
    return loss, dx, small


SMALL_ROWS = 16
LOSS_ROW = 13

WHOLE = None
GATHER_PLAN = {"cast_rest": [("w_in", WHOLE)],
               "a_in": [("w_out", WHOLE), ("gu0", (0, 18))],
               "a_out": [("gu0", (18, 14))],
               "ffn0_in": [("wd0", WHOLE), ("w_kv", WHOLE), ("w_q", WHOLE)],
               "ffn0_out": [("w_o", WHOLE), ("gu1", (0, 8))],
               "kvq_in": [("gu1", (8, 4))],
               "attn_fwd": [("gu1", (12, 20))],
               "ffn1_in": [("wd1", WHOLE)]}
PAIR_PLAN = {"ffn1_dwgu": ["wd1"], "ffn1_in_bwd": ["gu1"], "attn_bwd": ["w_o"], "qkv_in_bwd": ["w_q", "w_kv"],
             "ffn0_dwd": ["gu0"], "ffn0_in_bwd": ["wd0"], "a_conv_bwd": ["w_out"], "chip_reduce_early": ["w_in"]}
CHIP_PLAN = {"ffn1_in_bwd": [("wd1", WHOLE)], "attn_bwd": [("gu1", WHOLE)],
             "ffn0_dwgu": [("w_o", WHOLE), ("w_q", WHOLE), ("w_kv", WHOLE)],
             "ffn0_in_bwd": [("gu0", WHOLE)], "a_conv_bwd": [("wd0", (0, 12))],
             "a_dwin": [("wd0", (12, 10)), ("w_out", WHOLE)], "a_in_bwd": [("w_in", WHOLE)]}
HALF_PLAN = {"a_in_bwd": ["gu0", "gu1", "wd0", "wd1", "w_kv", "w_q", "w_o", "w_out"]}
GRAD_KIND = dict(KIND, gu0="split", gu1="split")


class Traffic:
    def __init__(self, wholes, quarter, c_arr, pc_arr):
        self.wholes, self.quarter, self.c_arr, self.pc_arr = wholes, quarter, c_arr, pc_arr
        self.views, self.sums, self.got = {}, {}, {}
        self.reduced = {}
        self.stages = {}

    def reduce(self, keys, name):
        args = ([self.sums[k] for k in keys], [self.got[k] for k in keys], [GRAD_KIND[k] for k in keys], self.pc_arr)
        if name not in PAIR_PLAN:
            return chip_reduce(*args, name=name)
        pairs = PAIR_PLAN[name]
        out, got = chip_reduce(*args, name=name, ride=pair_ride([self.views[k] for k in pairs]))
        self.pair_sums(pairs, got)
        return out

    def pair_sums(self, keys, got):
        for k, theirs in zip(keys, got):
            self.sums[k] = pair_add(self.views[k], theirs, self.c_arr, name="pair_add_" + k)

    def ride(self, name, small=None):
        rides, stages = [], []
        if name in GATHER_PLAN:
            plan = GATHER_PLAN[name]
            rides.append(gather_ride([self.wholes[k] for k, _ in plan],
                                     [(KIND[k], self.quarter[k], part) for k, part in plan], small))
            stages.append(("gather", [k for k, _ in plan]))
        if name in HALF_PLAN:
            keys = HALF_PLAN[name]
            rides.append(half_ride(self.reduce(keys, "chip_reduce_early")))
            stages.append(("half", keys))
        if name in CHIP_PLAN:
            plan = CHIP_PLAN[name]
            rides.append(chip_ride([self.sums[k] for k, _ in plan],
                                   [(GRAD_KIND[k], self.quarter[k], part) for k, part in plan],
                                   earlier=[self.got.get(k) for k, _ in plan]))
            stages.append(("chip", [k for k, _ in plan]))
        if name in PAIR_PLAN:
            keys = PAIR_PLAN[name]
            rides.append(pair_ride([self.views[k] for k in keys]))
            stages.append(("pair", keys))
        self.stages[name] = stages
        return join(rides)

    def landed(self, name, results, wts):
        results = list(results)
        for stage, keys in self.stages[name]:
            mine, results = results[:len(keys)], results[len(keys):]
            if stage == "gather":
                for k, whole in zip(keys, mine):
                    self.wholes[k] = wts[k] = whole
            elif stage == "chip":
                self.got.update(zip(keys, mine))
            elif stage == "half":
                self.reduced.update(zip(keys, mine))
            else:
                self.pair_sums(keys, mine)

    def grad(self, key, value):
        r, ws = self.quarter[key]
        view = {"row": (N_CHIPS, 2, r // 2, ws), "col": (1, 2, r // 2, N_CHIPS * ws), "split": (2, 2, r // 2, 2 * ws)}
        self.views[key] = value.reshape(view[GRAD_KIND[key]])


def kernel(x, a_pre_norm, a_w_in, a_conv_w, a_w_out, a_post_norm, ffn_pre_norm, ffn_w_gate_up, ffn_w_down, ffn_post_norm, kv_norm, w_kv, b_pre_norm, b_w_q, b_sinks, b_w_o, b_post_norm, loss_target, m_a_pre_norm, m_a_w_in, m_a_conv_w, m_a_w_out, m_a_post_norm, m_ffn_pre_norm, m_ffn_w_gate_up, m_ffn_w_down, m_ffn_post_norm, m_kv_norm, m_w_kv, m_b_pre_norm, m_b_w_q, m_b_sinks, m_b_w_o, m_b_post_norm, v_a_pre_norm, v_a_w_in, v_a_conv_w, v_a_w_out, v_a_post_norm, v_ffn_pre_norm, v_ffn_w_gate_up, v_ffn_w_down, v_ffn_post_norm, v_kv_norm, v_w_kv, v_b_pre_norm, v_b_w_q, v_b_sinks, v_b_w_o, v_b_post_norm):
    T, D = x.shape[1], x.shape[2]
    xi, yi, ci = _place()
    p = 2 * xi + yi
    p_arr = jnp.reshape(p, (1,)).astype(jnp.int32)
    c_arr = jnp.reshape(ci, (1,)).astype(jnp.int32)
    pc_arr = jnp.stack([p, ci]).astype(jnp.int32)
    me_arr = jnp.reshape(4 * xi + 2 * yi + ci, (1,)).astype(jnp.int32)
    qd = D // N_CHIPS

    big = {"w_in": (a_w_in, 0), "w_out": (a_w_out, 0), "gu0": (ffn_w_gate_up, 0), "gu1": (ffn_w_gate_up, 1),
           "wd0": (ffn_w_down, 0), "wd1": (ffn_w_down, 1), "w_kv": (w_kv[None], 0), "w_q": (b_w_q, 0),
           "w_o": (b_w_o, 0)}
    names = list(big)
    quarter = {k: w.shape[1:] for k, (w, _) in big.items()}
    source = lambda k: big[k] + (KIND[k],)
    traffic = Traffic(dict(zip(names[:1], cast_quarters([source(names[0])], p_arr, name="cast_first"))), quarter,
                      c_arr, pc_arr)
    small_shard = jnp.concatenate([a_pre_norm, a_post_norm, a_conv_w[0], jnp.zeros((3, qd), F32)], axis=0)
    wts = {}
    rest, (*landed, small_full) = cast_quarters([source(k) for k in names[1:]], p_arr, name="cast_rest",
                                                ride=traffic.ride("cast_rest", small_shard))
    traffic.wholes.update(zip(names[1:], rest))
    traffic.landed("cast_rest", landed, wts)
    rows = lambda k: jnp.transpose(small_full[:, k], (1, 0, 2)).reshape(-1, D)
    vec = {"a_pre": rows(slice(0, 1)), "a_post": rows(slice(1, 2)), "conv_w": rows(slice(2, 5)),
           "ffn_pre0": ffn_pre_norm[0:1], "ffn_pre1": ffn_pre_norm[1:2],
           "ffn_post0": ffn_post_norm[0:1], "ffn_post1": ffn_post_norm[1:2],
           "kv_norm": kv_norm[None], "b_pre": b_pre_norm, "b_post": b_post_norm, "sinks": b_sinks}

    loss, dx, small = local_step(x[0], loss_target[0], wts, vec, traffic)

    pad = lambda a: jnp.pad(a, ((0, 0), (0, D - a.shape[1])))
    small_block = jnp.concatenate(
        [small["a_pre"], small["a_post"], small["conv_w"][0:3], small["ffn_pre0"], small["ffn_pre1"],
         small["ffn_post0"], small["ffn_post1"], small["kv_norm"], small["b_pre"], small["b_post"],
         pad(small["sinks"][0:1]), pad(loss[0:1]), jnp.zeros((SMALL_ROWS - LOSS_ROW - 1, D), F32)], axis=0)
    late = [k for k in names if k not in traffic.reduced]
    *swapped, small_blocks = alone(join([half_ride(traffic.reduce(late, "chip_reduce_late")),
                                         chip_ride([], [], small_block)]), name="last_exchange")
    traffic.reduced.update(zip(late, swapped))
    grad = {k: traffic.reduced[k].reshape(quarter[k]) for k in names}
    small_sum = small_reduce(small_blocks, me_arr)

    out = {}
    out["a_w_in"] = adamw(a_w_in, [grad["w_in"]], m_a_w_in, v_a_w_in, name="adamw_a_w_in")
    out["a_w_out"] = adamw(a_w_out, [grad["w_out"]], m_a_w_out, v_a_w_out, name="adamw_a_w_out")
    out["ffn_w_gate_up"] = adamw(ffn_w_gate_up, [grad["gu0"], grad["gu1"]], m_ffn_w_gate_up, v_ffn_w_gate_up,
                                 name="adamw_ffn_w_gate_up")
    out["ffn_w_down"] = adamw(ffn_w_down, [grad["wd0"], grad["wd1"]], m_ffn_w_down, v_ffn_w_down,
                              name="adamw_ffn_w_down")
    out["w_kv"] = [o[0] for o in adamw(w_kv[None], [grad["w_kv"]], m_w_kv[None], v_w_kv[None], name="adamw_w_kv")]
    out["b_w_q"] = adamw(b_w_q, [grad["w_q"]], m_b_w_q, v_b_w_q, name="adamw_b_w_q")
    out["b_w_o"] = adamw(b_w_o, [grad["w_o"]], m_b_w_o, v_b_w_o, name="adamw_b_w_o")

    def pack(a_pre, a_post, conv, ffn_pre, ffn_post, kvn, b_pre, b_post, sinks):
        return jnp.concatenate([pad(a_pre), pad(a_post), pad(conv[0]), ffn_pre, ffn_post, kvn[None], b_pre, b_post,
                                pad(sinks), jnp.zeros((SMALL_ROWS - 13, D), F32)], axis=0)

    g_small = jnp.concatenate([pad(lax.dynamic_slice(small_sum, (0, p * qd), (5, qd))), small_sum[5:]], axis=0)
    w_small = pack(a_pre_norm, a_post_norm, a_conv_w, ffn_pre_norm, ffn_post_norm, kv_norm, b_pre_norm, b_post_norm,
                   b_sinks)
    m_small = pack(m_a_pre_norm, m_a_post_norm, m_a_conv_w, m_ffn_pre_norm, m_ffn_post_norm, m_kv_norm,
                   m_b_pre_norm, m_b_post_norm, m_b_sinks)
    v_small = pack(v_a_pre_norm, v_a_post_norm, v_a_conv_w, v_ffn_pre_norm, v_ffn_post_norm, v_kv_norm,
                   v_b_pre_norm, v_b_post_norm, v_b_sinks)
    packed = adamw(w_small[None], [g_small], m_small[None], v_small[None], name="adamw_small")
    ns = b_sinks.shape[1]
    unpack = lambda a: {"a_pre_norm": a[0:1, :qd], "a_post_norm": a[1:2, :qd], "a_conv_w": a[None, 2:5, :qd],
                        "ffn_pre_norm": a[5:7], "ffn_post_norm": a[7:9], "kv_norm": a[9], "b_pre_norm": a[10:11],
                        "b_post_norm": a[11:12], "b_sinks": a[12:13, :ns]}
    unpacked = [unpack(a[0]) for a in packed]
    for k in unpacked[0]:
        out[k] = [u[k] for u in unpacked]

    order = ["a_pre_norm", "a_w_in", "a_conv_w", "a_w_out", "a_post_norm", "ffn_pre_norm", "ffn_w_gate_up",
             "ffn_w_down", "ffn_post_norm", "kv_norm", "w_kv", "b_pre_norm", "b_w_q", "b_sinks", "b_w_o",
             "b_post_norm"]
    return (small_sum[LOSS_ROW, 0], dx[None], *[out[k][0] for k in order], *[out[k][1] for k in order],
            *[out[k][2] for k in order], *[out[k][3] for k in order])
```

```python
import math

import jax
import jax.numpy as jnp
from jax import lax
from jax.experimental import pallas as pl
from jax.experimental.pallas import tpu as pltpu

F32 = jnp.float32
BF16 = jnp.bfloat16
SDS = jax.ShapeDtypeStruct
MESH = pl.DeviceIdType.MESH
DMA = pltpu.SemaphoreType.DMA
HBM_SPEC = pl.BlockSpec(memory_space=pltpu.HBM)

EPS = 1e-6
NEG = -1e30
HEAD_DIM = 64
N_KV_HEADS = 4
BLOCK = 128
ROT_DIM = HEAD_DIM // 4
ROPE_THETA = 500000.0
N_CHIPS = 4

ADAM_LR = 0.001
ADAM_B1 = 0.9
ADAM_B2 = 0.999
ADAM_EPS = 1e-08
ADAM_WD = 0.01
ADAM_STEP = 10

VMEM_LIMIT_BYTES = 52 * 1024 * 1024
ROW_TILE = 512
BF16_ROWS = 16
STREAM = BF16
MXU_WIDTH = 256

KIND = {"w_in": "col", "gu0": "col", "gu1": "col", "w_out": "row", "wd0": "row", "wd1": "row", "w_kv": "row",
        "w_q": "row", "w_o": "row"}


def _params(*semantics):
    return pltpu.CompilerParams(dimension_semantics=semantics, vmem_limit_bytes=VMEM_LIMIT_BYTES)


def _row_tile(rows, limit, step=8):
    return max(t for t in range(step, limit + 1, step) if rows % t == 0)


def _place():
    return lax.axis_index("x"), lax.axis_index("y"), lax.axis_index("c")


def _other_chips(x, y):
    return [(1 - x, y), (x, 1 - y), (1 - x, 1 - y)]


def _remote(src, dst, send_sem, recv_sem, to):
    return pltpu.make_async_remote_copy(src_ref=src, dst_ref=dst, send_sem=send_sem, recv_sem=recv_sem,
                                        device_id=to, device_id_type=MESH)


def _full_shape(kind, quarter):
    r, ws = quarter
    return (N_CHIPS * r, ws) if kind == "row" else (r, N_CHIPS * ws)


def _rows_of(h, part):
    lo, n = (0, h) if part is None else (part[0] * BF16_ROWS, part[1] * BF16_ROWS)
    assert lo + n <= h, (h, part)
    return lo, n


def _half_of_quarter(ref, kind, quarter, part, q, half):
    r, ws = quarter
    h = r // 2
    lo, n = _rows_of(h, part)
    if kind == "row":
        return ref.at[pl.ds(pl.multiple_of(q * r + half * h + lo, BF16_ROWS), n)]
    return ref.at[pl.ds(pl.multiple_of(half * h + lo, BF16_ROWS), n), pl.ds(pl.multiple_of(q * ws, 128), ws)]


class Ride:
    def __init__(self, operands, out_shape, aliases, sems, make, late=False):
        self.operands, self.out_shape, self.aliases, self.sems, self.make = operands, out_shape, aliases, sems, make
        self.late = late

    def stages(self, ins, outs, sems):
        made = self.make(ins, outs, sems)
        return made if len(made) == 4 else (made[0], None, None, made[1])


def join(rides):
    rides = [r for r in rides if r is not None]
    if len(rides) < 2:
        return rides[0] if rides else None
    aliases, at = {}, [0, 0, 0]
    cuts = []
    for r in rides:
        aliases.update({at[0] + i: at[1] + o for i, o in r.aliases.items()})
        cuts.append(tuple(at))
        at = [at[0] + len(r.operands), at[1] + len(r.out_shape), at[2] + len(r.sems)]
    cuts.append(tuple(at))

    def make(ins, outs, sem):
        made = [r.stages(ins[lo[0]:hi[0]], outs[lo[1]:hi[1]], sem[lo[2]:hi[2]]) for r, lo, hi in zip(rides, cuts, cuts[1:])]
        def all_of(k):
            def stage():
                for m in made:
                    if m[k] is not None:
                        m[k]()
            return stage

        if all(m[1] is None for m in made):
            return all_of(0), all_of(3)
        return all_of(0), all_of(1), all_of(2), all_of(3)

    return Ride(sum((list(r.operands) for r in rides), []), sum((list(r.out_shape) for r in rides), []), aliases,
                sum((list(r.sems) for r in rides), []), make, late=any(r.late for r in rides))


def _call(body, *, name, grid, in_specs, out_specs, out_shape, args, scratch_shapes=(), semantics=None, ride=None,
          prefetch=None):
    pre = 0 if prefetch is None else 1
    n_in, n_out, n_scr = len(in_specs), len(out_specs), len(scratch_shapes)
    r_in, r_out = (len(ride.operands), len(ride.out_shape)) if ride is not None else (0, 0)
    a, b = pre + n_in, pre + n_in + r_in
    c, d = b + n_out, b + n_out + r_out
    e = d + n_scr

    def riding(*refs):
        start, relay, relay_again, finish = ride.stages(refs[a:b], refs[c:d], refs[e:])
        step, steps = pl.program_id(0), 1
        for k, extent in enumerate(grid):
            step = pl.program_id(k) if k == 0 else step * extent + pl.program_id(k)
            steps *= extent
        pl.when(step == 0)(start)
        if relay is not None:
            pl.when(step == (steps - 1 if ride.late else steps // 2))(relay)
            pl.when(step == steps - 1)(relay_again)
        body(*refs[:a], *refs[b:c], *refs[d:e])
        pl.when(step == steps - 1)(finish)

    if ride is None:
        kernel_body, extra_in, extra_out, extra_shape, extra_scr, aliases = body, [], [], [], [], {}
        params = _params(*semantics)
    else:
        kernel_body, extra_in, extra_out = riding, [HBM_SPEC] * r_in, [HBM_SPEC] * r_out
        extra_shape, extra_scr = list(ride.out_shape), list(ride.sems)
        aliases = {pre + n_in + i: n_out + o for i, o in ride.aliases.items()}
        params = _params(*(("arbitrary",) * len(grid)))
    specs = dict(grid=grid, in_specs=list(in_specs) + extra_in, out_specs=list(out_specs) + extra_out,
                 scratch_shapes=list(scratch_shapes) + extra_scr)
    if prefetch is not None:
        specs = dict(grid_spec=pltpu.PrefetchScalarGridSpec(num_scalar_prefetch=1, **specs))
        args = (prefetch,) + tuple(args)
    outs = pl.pallas_call(kernel_body, name=name, out_shape=list(out_shape) + extra_shape,
                          input_output_aliases=aliases, compiler_params=params, **specs,
                          )(*args, *(ride.operands if ride is not None else ()))
    return outs if ride is None else (outs[:n_out], outs[n_out:])


def alone(ride, *, name):
    def body(*refs):
        n = len(ride.operands)
        stages = ride.stages(refs[:n], refs[n:n + len(ride.out_shape)], refs[n + len(ride.out_shape):])
        for stage in stages:
            if stage is not None:
                stage()

    return pl.pallas_call(
        body, name=name, in_specs=[HBM_SPEC] * len(ride.operands), out_specs=[HBM_SPEC] * len(ride.out_shape),
        out_shape=list(ride.out_shape), input_output_aliases=dict(ride.aliases), scratch_shapes=list(ride.sems),
    )(*ride.operands)


def _two_pieces(h, part):
    lo, n = (0, h // BF16_ROWS) if part is None else part
    assert n >= 2, (h, part)
    return (lo, n // 2), (lo + n // 2, n - n // 2)


def gather_ride(wholes, metas, small=None, late=False):
    n = len(wholes)
    operands, out_shape = list(wholes), [SDS(s.shape, s.dtype) for s in wholes]
    sems = [DMA((n, 4)), DMA((n, 4)), DMA((n, 4)), DMA((n, 4))]
    if small is not None:
        operands.append(small)
        out_shape.append(SDS((N_CHIPS,) + small.shape, small.dtype))
        sems += [DMA((3,)), DMA((3,)), DMA(())]

    def make(ins, outs, sem):
        send1, recv1, send2, recv2 = sem[:4]
        x, y, c = _place()
        p = 2 * x + y
        chips = _other_chips(x, y)
        across_x, across_y, across_both = [2 * qx + qy for qx, qy in chips]
        me, sibling = (x, y, c), (x, y, 1 - c)

        def region(t, q, half, piece=None):
            kind, quarter, part = metas[t]
            if piece is not None:
                part = _two_pieces(quarter[0] // 2, part)[piece]
            return _half_of_quarter(outs[t], kind, quarter, part, q, half)

        first, second, arriving = [], [], []
        landing, passing = [[], [], [], []], [[], [], [], []]
        for j, (qx, qy) in enumerate(chips):
            if small is not None:
                q = 2 * qx + qy
                first.append(_remote(ins[n], outs[n].at[p], sem[4].at[j], sem[5].at[j], (qx, qy, c)))
                arriving.append(_remote(outs[n].at[q], outs[n].at[q], sem[4].at[j], sem[5].at[j], me))
        for t in range(n):
            mine = region(t, p, c)
            for j in range(2):
                first.append(_remote(mine, mine, send1.at[t, j], recv1.at[t, j], chips[j] + (c,)))
            lands = [(across_x, None), (across_y, None), (across_both, 0), (across_both, 1)]
            for k, (q, piece) in enumerate(lands):
                landed, theirs = region(t, q, c, piece), region(t, q, 1 - c, piece)
                landing[k].append(_remote(landed, landed, send1.at[t, k], recv1.at[t, k], me))
                passing[k].append(_remote(landed, landed, send2.at[t, k], recv2.at[t, k], sibling))
                arriving.append(_remote(theirs, theirs, send2.at[t, k], recv2.at[t, k], me))
            onward = region(t, across_x, c, 0)
            second.append(_remote(onward, onward, send1.at[t, 2], recv1.at[t, 2], chips[1] + (c,)))
            onward = region(t, across_y, c, 1)
            second.append(_remote(onward, onward, send1.at[t, 3], recv1.at[t, 3], chips[0] + (c,)))
        local = [] if small is None else [pltpu.make_async_copy(ins[n], outs[n].at[p], sem[6])]

        def start():
            for cp in local + first:
                cp.start()

        def relay():
            for k in range(2):
                for t in range(n):
                    landing[k][t].wait_recv()
                    second[2 * t + k].start()
                    passing[k][t].start()

        def relay_again():
            for k in range(2, 4):
                for t in range(n):
                    landing[k][t].wait_recv()
                    passing[k][t].start()

        def finish():
            for cp in arriving:
                cp.wait_recv()
            for cp in first + second + sum(passing, []):
                cp.wait_send()
            for cp in local:
                cp.wait()

        return start, relay, relay_again, finish

    return Ride(operands, out_shape, {t: t for t in range(n)}, sems, make, late=late)


def chip_ride(sums, metas, small=None, earlier=None):
    n = len(sums)
    operands = list(sums)
    out_shape = [SDS((3, s.shape[1], quarter[1]), s.dtype) for s, (_, quarter, _) in zip(sums, metas)]
    sems = [DMA((n, 3)), DMA((n, 3))] if n else []
    if small is not None:
        operands.append(small)
        out_shape.append(SDS((8,) + small.shape, small.dtype))
        sems += [DMA((7,)), DMA((7,)), DMA(())]
    aliases = {}
    for t, buffer in enumerate(earlier or [None] * n):
        if buffer is not None:
            aliases[len(operands)] = t
            operands.append(buffer)

    def make(ins, outs, sem):
        x, y, c = _place()
        cps = []
        for j, (qx, qy) in enumerate(_other_chips(x, y)):
            q = 2 * qx + qy
            for t in range(n):
                kind, (_, ws), part = metas[t]
                rows = pl.ds(*_rows_of(ins[t].shape[1], part))
                if kind == "row":
                    src = ins[t].at[q, rows]
                elif kind == "col":
                    src = ins[t].at[0, rows, pl.ds(pl.multiple_of(q * ws, 128), ws)]
                else:
                    src = ins[t].at[q // 2, rows, pl.ds(pl.multiple_of((q % 2) * ws, 128), ws)]
                cps.append(_remote(src, outs[t].at[j, rows], sem[0].at[t, j], sem[1].at[t, j], (qx, qy, c)))
        local = []
        if small is not None:
            ssend, srecv, lsem = sem[2 * bool(n):2 * bool(n) + 3]
            local.append(pltpu.make_async_copy(ins[n], outs[n].at[0], lsem))
            for k in range(1, 8):
                peer = (x ^ (k >> 2 & 1), y ^ (k >> 1 & 1), c ^ (k & 1))
                cps.append(_remote(ins[n], outs[n].at[k], ssend.at[k - 1], srecv.at[k - 1], peer))

        def start():
            for cp in local + cps:
                cp.start()

        def finish():
            for cp in cps + local:
                cp.wait()

        return start, finish

    return Ride(operands, out_shape, aliases, sems, make)


def pair_ride(grads):
    n = len(grads)

    def make(ins, outs, sem):
        x, y, c = _place()
        cps = [_remote(ins[t].at[:, 1 - c], outs[t], sem[0].at[t], sem[1].at[t], (x, y, 1 - c)) for t in range(n)]

        def start():
            for cp in cps:
                cp.start()

        def finish():
            for cp in cps:
                cp.wait()

        return start, finish

    return Ride(list(grads), [SDS((g.shape[0],) + g.shape[2:], g.dtype) for g in grads], {}, [DMA((n,)), DMA((n,))],
                make)


def half_ride(quarters):
    n = len(quarters)

    def make(ins, outs, sem):
        x, y, c = _place()
        sends = [_remote(outs[t].at[c], outs[t].at[c], sem[0].at[t], sem[1].at[t], (x, y, 1 - c)) for t in range(n)]

        def start():
            for cp in sends:
                cp.start()

        def finish():
            for t in range(n):
                theirs = outs[t].at[1 - c]
                _remote(theirs, theirs, sem[0].at[t], sem[1].at[t], (x, y, c)).wait_recv()
            for cp in sends:
                cp.wait_send()

        return start, finish

    return Ride(list(quarters), [SDS(q.shape, q.dtype) for q in quarters], {t: t for t in range(n)},
                [DMA((n,)), DMA((n,))], make)


CAST_STEPS = 4


def cast_quarters(sources, p_arr, *, name, ride=None):
    n = len(sources)
    in_specs, out_specs, out_shape = [], [], []
    for w, layer, kind in sources:
        _, r, ws = w.shape
        tr = r // CAST_STEPS
        assert tr % BF16_ROWS == 0, w.shape
        in_specs.append(pl.BlockSpec((None, tr, ws), lambda i, p_ref, layer=layer: (layer, i, 0)))
        out_specs.append(pl.BlockSpec((tr, ws), (lambda i, p_ref: (p_ref[0] * CAST_STEPS + i, 0)) if kind == "row"
                                      else (lambda i, p_ref: (i, p_ref[0]))))
        out_shape.append(SDS(_full_shape(kind, (r, ws)), BF16))

    def body(p_ref, *refs):
        for w_ref, o_ref in zip(refs[:n], refs[n:]):
            o_ref[...] = w_ref[...].astype(BF16)

    return _call(body, name=name, grid=(CAST_STEPS,), in_specs=in_specs, out_specs=out_specs, out_shape=out_shape,
                 semantics=("parallel",), args=[w for w, _, _ in sources], ride=ride, prefetch=p_arr)


def pair_add(own, got, c_arr, *, name):
    A, _, h, W = own.shape
    th = _row_tile(h, max(BF16_ROWS, (3 << 19) // W), BF16_ROWS)

    def body(c_ref, a_ref, b_ref, o_ref):
        o_ref[...] = (a_ref[...].astype(F32) + b_ref[...].astype(F32)).astype(BF16)

    return pl.pallas_call(
        body, name=name,
        grid_spec=pltpu.PrefetchScalarGridSpec(
            num_scalar_prefetch=1, grid=(A, h // th),
            in_specs=[pl.BlockSpec((None, None, th, W), lambda q, i, c_ref: (q, c_ref[0], i, 0)),
                      pl.BlockSpec((None, th, W), lambda q, i, c_ref: (q, i, 0))],
            out_specs=pl.BlockSpec((None, th, W), lambda q, i, c_ref: (q, i, 0))),
        out_shape=SDS((A, h, W), BF16),
        compiler_params=_params("parallel", "parallel"),
    )(c_arr, own, got)


REDUCE_STEPS = 2


def chip_reduce(sums, got, kinds, pc_arr, *, name, ride=None):
    n = len(sums)
    mine = {"row": lambda i, pc_ref: (pc_ref[0], i, 0), "col": lambda i, pc_ref: (0, i, pc_ref[0]),
            "split": lambda i, pc_ref: (pc_ref[0] // 2, i, pc_ref[0] % 2)}
    a_specs, b_specs, o_specs, out_shape = [], [], [], []
    for g, kind in zip(got, kinds):
        _, h, ws = g.shape
        th = h // REDUCE_STEPS
        assert th % BF16_ROWS == 0, g.shape
        a_specs.append(pl.BlockSpec((None, th, ws), mine[kind]))
        b_specs.append(pl.BlockSpec((3, th, ws), lambda i, pc_ref: (0, i, 0)))
        o_specs.append(pl.BlockSpec((None, th, ws), lambda i, pc_ref: (pc_ref[1], i, 0)))
        out_shape.append(SDS((2, h, ws), F32))

    def body(pc_ref, *refs):
        for a_ref, b_ref, o_ref in zip(refs[:n], refs[n:2 * n], refs[2 * n:]):
            o_ref[...] = ((a_ref[...].astype(F32) + b_ref[0].astype(F32)) + b_ref[1].astype(F32)) + b_ref[2].astype(F32)

    return _call(body, name=name, grid=(REDUCE_STEPS,), in_specs=a_specs + b_specs, out_specs=o_specs,
                 out_shape=out_shape, semantics=("parallel",), args=list(sums) + list(got), prefetch=pc_arr, ride=ride)


def small_reduce(blocks, me_arr):
    _, rows, D = blocks.shape

    def body(me_ref, b_ref, o_ref):
        me = me_ref[0]
        total = b_ref[me]
        for d in range(1, 8):
            total = total + b_ref[d ^ me]
        o_ref[...] = total

    return pl.pallas_call(
        body, name="small_reduce",
        grid_spec=pltpu.PrefetchScalarGridSpec(
            num_scalar_prefetch=1, grid=(1,),
            in_specs=[pl.BlockSpec((8, rows, D), lambda i, me_ref: (0, 0, 0))],
            out_specs=pl.BlockSpec((rows, D), lambda i, me_ref: (0, 0))),
        out_shape=SDS((rows, D), F32),
        compiler_params=_params("arbitrary"),
    )(me_arr, blocks)


def adamw(w, gs, m, v, *, name):
    L, r, cols = w.shape
    tr = _row_tile(r, 256)
    nt = r // tr

    def body(*refs):
        w_ref, m_ref, v_ref = refs[:3]
        g_refs = refs[3:3 + L]
        g_out, d_out, m_out, v_out = refs[3 + L:]
        layer = pl.program_id(0)
        g = g_refs[0][...]
        for l in range(1, L):
            g = jnp.where(layer == l, g_refs[l][...], g)
        m_new = ADAM_B1 * m_ref[...] + (1.0 - ADAM_B1) * g
        v_new = ADAM_B2 * v_ref[...] + (1.0 - ADAM_B2) * (g * g)
        m_hat = m_new / (1.0 - ADAM_B1 ** ADAM_STEP)
        v_hat = v_new / (1.0 - ADAM_B2 ** ADAM_STEP)
        g_out[...] = g
        m_out[...] = m_new
        v_out[...] = v_new
        d_out[...] = -ADAM_LR * (m_hat / (jnp.sqrt(v_hat) + ADAM_EPS) + ADAM_WD * w_ref[...])

    full = pl.BlockSpec((None, tr, cols), lambda l, i: (l, i, 0))
    g_spec = lambda l0: pl.BlockSpec((tr, cols), lambda l, i: (jnp.where(l == l0, i, jnp.where(l < l0, 0, nt - 1)), 0))
    return pl.pallas_call(
        body, name=name, grid=(L, nt),
        in_specs=[full, full, full] + [g_spec(l0) for l0 in range(L)],
        out_specs=[full] * 4,
        out_shape=[SDS(w.shape, F32)] * 4,
        compiler_params=_params("arbitrary", "arbitrary"),
    )(w, m, v, *gs)


def _rms_r(xf):
    return lax.rsqrt(jnp.mean(xf * xf, axis=-1, keepdims=True) + EPS)


def _rmsnorm_bwd(xf, g, dy):
    r = _rms_r(xf)
    xh = xf * r
    gd = g * dy
    return r * (gd - xh * jnp.mean(xh * gd, axis=-1, keepdims=True)), xh


def _dot(a, b):
    return jnp.dot(a, b, preferred_element_type=F32)


def _dot_nt(a, b):
    return lax.dot_general(a, b, (((1,), (1,)), ((), ())), preferred_element_type=F32)


def _dot_tn(a, b):
    return lax.dot_general(a, b, (((0,), (0,)), ((), ())), preferred_element_type=F32)


def _accumulate(ref, first, value):
    @pl.when(first)
    def _():
        ref[...] = value

    @pl.when(jnp.logical_not(first))
    def _():
        ref[...] += value


def norm_matmul(x, g, w, *, tn, split, name, ride=None, tm=ROW_TILE):
    T, D = x.shape
    N = w.shape[1]
    per = N // split // tn

    def body(x_ref, g_ref, w_ref, o_ref, xn_ref):
        @pl.when(pl.program_id(1) == 0)
        def _():
            xf = x_ref[...].astype(F32)
            xn_ref[...] = (xf * _rms_r(xf) * g_ref[...]).astype(BF16)

        o_ref[...] = _dot(xn_ref[...], w_ref[...]).astype(BF16)

    return _call(
        body, name=name, grid=(T // tm, N // tn),
        in_specs=[pl.BlockSpec((tm, D), lambda i, j: (i, 0)),
                  pl.BlockSpec((1, D), lambda i, j: (0, 0)),
                  pl.BlockSpec((D, tn), lambda i, j: (0, j))],
        out_specs=[pl.BlockSpec((None, tm, tn), lambda i, j: (j // per, i, j % per)),
                   pl.BlockSpec((tm, D), lambda i, j: (i, 0))],
        out_shape=[SDS((split, T, N // split), BF16), SDS((T, D), BF16)],
        semantics=("parallel", "arbitrary"), args=(x, g, w), ride=ride)


BIG_ROW_TILE = 1024


def norm2_matmul(x, gains, weights, *, name, ride=None, tm=BIG_ROW_TILE):
    T, D = x.shape
    tm = min(tm, T)
    n = len(gains)

    def body(x_ref, *refs):
        subs = _sub_tiles(tm)
        xhs = []
        for rows in subs:
            xf = x_ref[rows, :].astype(F32)
            xhs.append(xf * _rms_r(xf))
        for g_ref, w_ref, o_ref, xn_ref in zip(refs[:n], refs[n:2 * n], refs[2 * n::2], refs[2 * n + 1::2]):
            for rows, xh in zip(subs, xhs):
                xn = (xh * g_ref[...]).astype(BF16)
                xn_ref[rows, :] = xn
                o_ref[rows, :] = _dot(xn, w_ref[...]).astype(BF16)

    row = pl.BlockSpec((tm, D), lambda i: (i, 0))
    vec = pl.BlockSpec((1, D), lambda i: (0, 0))
    out_specs, out_shape = [], []
    for w in weights:
        out_specs += [pl.BlockSpec((tm, w.shape[1]), lambda i: (i, 0)), row]
        out_shape += [SDS((T, w.shape[1]), BF16), SDS((T, D), BF16)]
    return _call(
        body, name=name, grid=(T // tm,),
        in_specs=[row] + [vec] * n + [pl.BlockSpec(w.shape, lambda i: (0, 0)) for w in weights],
        out_specs=out_specs, out_shape=out_shape, semantics=("parallel",), args=[x] + list(gains) + list(weights),
        ride=ride)


def _shift_down(prev, cur, by):
    big = jnp.concatenate([prev, cur], axis=0)
    return pltpu.roll(big, by, 0)[prev.shape[0]:]


def _shift_up(cur, nxt, by):
    big = jnp.concatenate([cur, nxt], axis=0)
    return pltpu.roll(big, big.shape[0] - by, 0)[:cur.shape[0]]


def conv_mix_out(bcx, conv_w, w_out, g_post, res, *, name, ride=None, tm=ROW_TILE):
    T, D = res.shape
    hb = tm // BF16_ROWS

    def body(b_ref, c_ref, u_ref, cp_ref, up_ref, cw_ref, w_ref, g_ref, r_ref, h_ref, z_ref, y_ref):
        i = pl.program_id(0)
        cu = c_ref[...].astype(F32) * u_ref[...].astype(F32)
        cup = cp_ref[...].astype(F32) * up_ref[...].astype(F32)
        cup = jnp.where(i == 0, 0.0, cup)
        cv = (cw_ref[0:1, :] * _shift_down(cup, cu, 2) + cw_ref[1:2, :] * _shift_down(cup, cu, 1)
              + cw_ref[2:3, :] * cu)
        y = (b_ref[...].astype(F32) * cv).astype(BF16)
        y_ref[...] = y
        z = _dot(y, w_ref[...])
        z_ref[...] = z.astype(BF16)
        h_ref[...] = (r_ref[...] + z * _rms_r(z) * g_ref[...]).astype(STREAM)

    tile = lambda col: pl.BlockSpec((tm, D), lambda i: (i, col))
    halo = lambda col: pl.BlockSpec((BF16_ROWS, D), lambda i: (jnp.maximum(i * hb - 1, 0), col))
    row = pl.BlockSpec((tm, D), lambda i: (i, 0))
    return _call(
        body, name=name, grid=(T // tm,),
        in_specs=[tile(0), tile(1), tile(2), halo(1), halo(2),
                  pl.BlockSpec((3, D), lambda i: (0, 0)),
                  pl.BlockSpec((D, D), lambda i: (0, 0)),
                  pl.BlockSpec((1, D), lambda i: (0, 0)), row],
        out_specs=[row, row, row],
        out_shape=[SDS((T, D), STREAM), SDS((T, D), BF16), SDS((T, D), BF16)],
        semantics=("parallel",), args=(bcx, bcx, bcx, bcx, bcx, conv_w, w_out, g_post, res), ride=ride)


def _normbwd_then_nt(dh, zf, g_ref, w_ref, dz_ref, dg_ref, o_ref, first):
    dz, zh = _rmsnorm_bwd(zf, g_ref[...], dh)
    dz = dz.astype(BF16)
    dz_ref[...] = dz
    _accumulate(dg_ref, first, jnp.sum(dh * zh, axis=0, keepdims=True))
    o_ref[...] = _dot_nt(dz, w_ref[...]).astype(BF16)


def _then_specs(then, tm, T, D):
    z, g, w = then
    K = w.shape[0]
    row = pl.BlockSpec((tm, D), lambda i: (i, 0))
    vec = pl.BlockSpec((1, D), lambda i: (0, 0))
    in_specs = [row, vec, pl.BlockSpec((K, D), lambda i: (0, 0), pipeline_mode=pl.Buffered(1))]
    out_specs = [row, vec, pl.BlockSpec((tm, K), lambda i: (i, 0))]
    out_shape = [SDS((T, D), BF16), SDS((1, D), F32), SDS((T, K), BF16)]
    return in_specs, out_specs, out_shape


def plain_mix_out(a, w, g_post, res, *, name, target=None, ride=None, tm=ROW_TILE):
    T, D = res.shape
    tm = min(tm, T)
    K = a.shape[1]
    with_loss = target is not None

    def body(a_ref, w_ref, g_ref, r_ref, *rest):
        subs = _sub_tiles(tm)
        zs = [_dot(a_ref[rows, :], w_ref[...]) for rows in subs]
        if not with_loss:
            h_ref, z_ref = rest
            for rows, z in zip(subs, zs):
                h_ref[rows, :] = (r_ref[rows, :].astype(F32) + z * _rms_r(z) * g_ref[...]).astype(STREAM)
                z_ref[rows, :] = z.astype(BF16)
            return
        t_ref, h_ref, dz_ref, dg_ref, da_ref, loss_ref = rest
        first = pl.program_id(0) == 0
        loss, dg = jnp.zeros((), F32), jnp.zeros((1, D), F32)
        for rows, z in zip(subs, zs):
            diff = r_ref[rows, :].astype(F32) + z * _rms_r(z) * g_ref[...] - t_ref[rows, :]
            dh = diff * (1.0 / D)
            h_ref[rows, :] = dh.astype(STREAM)
            loss = loss + jnp.sum(diff * diff)
            dz, zh = _rmsnorm_bwd(z, g_ref[...], dh)
            dz = dz.astype(BF16)
            dz_ref[rows, :] = dz
            dg = dg + jnp.sum(dh * zh, axis=0, keepdims=True)
            da_ref[rows, :] = _dot_nt(dz, w_ref[...]).astype(BF16)
        _accumulate(loss_ref, first, jnp.full(loss_ref.shape, 0.5 / D, F32) * loss)
        _accumulate(dg_ref, first, dg)

    row = pl.BlockSpec((tm, D), lambda i: (i, 0))
    vec = pl.BlockSpec((1, D), lambda i: (0, 0))
    in_specs = [pl.BlockSpec((tm, K), lambda i: (i, 0)), pl.BlockSpec((K, D), lambda i: (0, 0)), vec, row]
    if with_loss:
        in_specs.append(row)
        out_specs = [row, row, vec, pl.BlockSpec((tm, K), lambda i: (i, 0)), pl.BlockSpec((8, 128), lambda i: (0, 0))]
        out_shape = [SDS((T, D), STREAM), SDS((T, D), BF16), SDS((1, D), F32), SDS((T, K), BF16), SDS((8, 128), F32)]
    else:
        out_specs, out_shape = [row, row], [SDS((T, D), STREAM), SDS((T, D), BF16)]
    return _call(
        body, name=name, grid=(T // tm,), in_specs=in_specs, out_specs=out_specs, out_shape=out_shape,
        semantics=("arbitrary",), args=(a, w, g_post, res) + ((target,) if with_loss else ()), ride=ride)


def _silu_grads(d, g, u):
    sg = jax.nn.sigmoid(g)
    return d * u * (sg * (1.0 + g * (1.0 - sg))), d * (g * sg)


def _sub_tiles(tm):
    return [pl.ds(k, min(MXU_WIDTH, tm)) for k in range(0, tm, MXU_WIDTH)]


def norm_swiglu_in(x, g, w, *, name, ride=None, tm=ROW_TILE):
    T, D = x.shape
    F = w.shape[1] // 2

    def body(x_ref, g_ref, wg_ref, wu_ref, gu_ref, a_ref, xt_ref):
        subs = _sub_tiles(tm)
        xns = []
        for rows in subs:
            xf = x_ref[rows, :].astype(F32)
            xns.append(xf * _rms_r(xf) * g_ref[...])
        xbs = [xn.astype(BF16) for xn in xns]
        gates = [_dot(xb, wg_ref[...]).astype(BF16) for xb in xbs]
        ups = [_dot(xb, wu_ref[...]).astype(BF16) for xb in xbs]
        for rows, gate, up in zip(subs, gates, ups):
            gu_ref[0, rows, :] = gate
            gu_ref[1, rows, :] = up
            a_ref[rows, :] = gate * jax.nn.sigmoid(gate) * up
        for rows, xn in zip(subs, xns):
            xt_ref[:, rows] = xn.T.astype(BF16)

    half = lambda s: pl.BlockSpec((D, F), lambda i: (0, s), pipeline_mode=pl.Buffered(1))
    return _call(
        body, name=name, grid=(T // tm,),
        in_specs=[pl.BlockSpec((tm, D), lambda i: (i, 0)), pl.BlockSpec((1, D), lambda i: (0, 0)), half(0), half(1)],
        out_specs=[pl.BlockSpec((2, tm, F), lambda i: (0, i, 0)), pl.BlockSpec((tm, F), lambda i: (i, 0)),
                   pl.BlockSpec((D, tm), lambda i: (0, i))],
        out_shape=[SDS((2, T, F), BF16), SDS((T, F), BF16), SDS((D, T), BF16)],
        semantics=("parallel",), args=(x, g, w, w), ride=ride)


def swiglu_bwd_tn(xt, dact, gu, *, name, ride=None, tb=MXU_WIDTH):
    D, T = xt.shape
    F = dact.shape[1]

    def body(xt_ref, d_ref, g_ref, u_ref, o_ref):
        dg, du = _silu_grads(d_ref[...], g_ref[...], u_ref[...])
        o_ref[0] = _dot(xt_ref[...], dg).astype(BF16)
        o_ref[1] = _dot(xt_ref[...], du).astype(BF16)

    col = lambda s: pl.BlockSpec((None, T, tb), lambda j: (s, 0, j))
    out = _call(
        body, name=name, grid=(F // tb,),
        in_specs=[pl.BlockSpec((D, T), lambda j: (0, 0), pipeline_mode=pl.Buffered(1)),
                  pl.BlockSpec((T, tb), lambda j: (0, j)), col(0), col(1)],
        out_specs=[pl.BlockSpec((2, D, tb), lambda j: (0, 0, j))],
        out_shape=[SDS((2, D, F), BF16)],
        semantics=("parallel",), args=(xt, dact, gu, gu), ride=ride)
    return out[0] if ride is None else (out[0][0], out[1])


def swiglu_bwd_in(dact, gu, w, h_in, g, dh_out, then, *, name, ride=None, tm=ROW_TILE):
    T, D = h_in.shape
    F = dact.shape[1]

    def body(d_ref, gg_ref, uu_ref, wg_ref, wu_ref, h_ref, g_ref, dh_ref, z_ref, g2_ref, w2_ref,
             o_ref, dg_ref, dz_ref, dg2_ref, da_ref):
        first = pl.program_id(0) == 0
        subs = _sub_tiles(tm)
        dns = []
        for rows in subs:
            dgate, dup = _silu_grads(d_ref[rows, :], gg_ref[rows, :], uu_ref[rows, :])
            dns.append(_dot_nt(dgate, wg_ref[...]) + _dot_nt(dup, wu_ref[...]))
        dg, dg2 = jnp.zeros((1, D), F32), jnp.zeros((1, D), F32)
        for rows, dn in zip(subs, dns):
            dx, hh = _rmsnorm_bwd(h_ref[rows, :].astype(F32), g_ref[...], dn)
            dh_in = dh_ref[rows, :] + dx
            o_ref[rows, :] = dh_in.astype(STREAM)
            dg = dg + jnp.sum(dn * hh, axis=0, keepdims=True)
            dz, zh = _rmsnorm_bwd(z_ref[rows, :].astype(F32), g2_ref[...], dh_in)
            dz = dz.astype(BF16)
            dz_ref[rows, :] = dz
            dg2 = dg2 + jnp.sum(dh_in * zh, axis=0, keepdims=True)
            da_ref[rows, :] = _dot_nt(dz, w2_ref[...]).astype(BF16)
        _accumulate(dg_ref, first, dg)
        _accumulate(dg2_ref, first, dg2)

    row = pl.BlockSpec((tm, D), lambda i: (i, 0))
    vec = pl.BlockSpec((1, D), lambda i: (0, 0))
    part = lambda s: pl.BlockSpec((None, tm, F), lambda i: (s, i, 0))
    half = lambda s: pl.BlockSpec((D, F), lambda i: (0, s), pipeline_mode=pl.Buffered(1))
    then_in, then_out, then_shape = _then_specs(then, tm, T, D)
    return _call(
        body, name=name, grid=(T // tm,),
        in_specs=[pl.BlockSpec((tm, F), lambda i: (i, 0)), part(0), part(1), half(0), half(1), row, vec, row] + then_in,
        out_specs=[row, vec] + then_out,
        out_shape=[SDS((T, D), STREAM), SDS((1, D), F32)] + then_shape,
        semantics=("arbitrary",), args=(dact, gu, gu, w, w, h_in, g, dh_out) + tuple(then), ride=ride)


def rope_tables(T):
    half = ROT_DIM // 2
    inv_freq = ROPE_THETA ** (-jnp.arange(0, ROT_DIM, 2, dtype=F32) / ROT_DIM)
    ang = (jnp.arange(T, dtype=F32)[:, None] * inv_freq[None, :]).T
    cos, sin = jnp.cos(ang), jnp.sin(ang)
    rest = HEAD_DIM - ROT_DIM
    one, zero = jnp.ones((rest, T), F32), jnp.zeros((rest, T), F32)
    zh = jnp.zeros((half, T), F32)
    fac = jnp.concatenate([cos, cos, one], axis=0)
    up = jnp.concatenate([-sin, zh, zero], axis=0)
    down = jnp.concatenate([zh, sin, zero], axis=0)
    return jnp.stack([fac, up, down])


def _rope(t, tab):
    half = ROT_DIM // 2
    return t * tab[0] + pltpu.roll(t, HEAD_DIM - half, 0) * tab[1] + pltpu.roll(t, half, 0) * tab[2]


def _rope_t(d, tab):
    half = ROT_DIM // 2
    return d * tab[0] + pltpu.roll(d * tab[1], half, 0) + pltpu.roll(d * tab[2], HEAD_DIM - half, 0)


def _head(t, h):
    return t[h * HEAD_DIM:(h + 1) * HEAD_DIM]


def _band(n, group):
    kj = lax.broadcasted_iota(jnp.int32, (2 * BLOCK, BLOCK), 0)
    qi = lax.broadcasted_iota(jnp.int32, (2 * BLOCK, BLOCK), 1)
    mask = (kj > qi) & (kj <= qi + BLOCK) & ((n > 0) | (kj >= BLOCK))
    return jnp.tile(mask, (1, group))


def _attn_specs(D, kvd, nb):
    cur = lambda n: jnp.minimum(n, nb - 1)
    prev = lambda n: jnp.maximum(cur(n) - 1, 0)
    return [pl.BlockSpec((BLOCK, D), lambda n: (cur(n), 0)),
            pl.BlockSpec((BLOCK, kvd), lambda n: (prev(n), 0)),
            pl.BlockSpec((BLOCK, kvd), lambda n: (cur(n), 0)),
            pl.BlockSpec((BLOCK, kvd), lambda n: (prev(n), 1)),
            pl.BlockSpec((BLOCK, kvd), lambda n: (cur(n), 1)),
            pl.BlockSpec((3, HEAD_DIM, BLOCK), lambda n: (0, 0, prev(n))),
            pl.BlockSpec((3, HEAD_DIM, BLOCK), lambda n: (0, 0, cur(n))),
            pl.BlockSpec(memory_space=pltpu.SMEM)]


def _attn_operands(q_ref, kp_ref, k_ref, vp_ref, v_ref, tp_ref, t_ref):
    flip = lambda ref: ref[...].astype(F32).T
    tab = t_ref[...]
    kt = jnp.concatenate([flip(kp_ref), flip(k_ref)], axis=1)
    vt = jnp.concatenate([flip(vp_ref), flip(v_ref)], axis=1)
    return flip(q_ref), kt, vt, tab, jnp.concatenate([tp_ref[...], tab], axis=2)


SCORE_SCALE = 1.0 / math.sqrt(HEAD_DIM)
HEADS_TOGETHER = 4


def _group_heads(t, first, count, tab=None):
    heads = [_head(t, first + g) for g in range(count)]
    if tab is not None:
        heads = [_rope(h, tab) * SCORE_SCALE for h in heads]
    return jnp.concatenate(heads, axis=1).astype(BF16)


def _sink_row(s_ref, first, count):
    which = lax.broadcasted_iota(jnp.int32, (1, count * BLOCK), 1) // BLOCK
    row = jnp.zeros((1, count * BLOCK), F32)
    for g in range(count):
        row = jnp.where(which == g, s_ref[0, first + g], row)
    return row


def _sum_keys(t):
    return _dot(jnp.ones((8, t.shape[0]), BF16), t)[0:1]


def _softmax(scores, sink, mask):
    s = jnp.where(mask, scores.astype(BF16), NEG)
    m = jnp.maximum(jnp.max(s, axis=0, keepdims=True).astype(F32), sink).astype(BF16)
    e = jnp.exp(s - m)
    m = m.astype(F32)
    return e, m, 1.0 / (_sum_keys(e) + jnp.exp(sink - m))


def _per_head(row, count):
    return [row[:, g * BLOCK:(g + 1) * BLOCK] for g in range(count)]


def attention_fwd(q, kv, tabs, sinks, *, name, ride=None):
    T, D = q.shape
    kvd = kv.shape[1] // 2
    heads = D // HEAD_DIM
    group = heads // N_KV_HEADS

    def body(q_ref, kp_ref, k_ref, vp_ref, v_ref, tp_ref, t_ref, s_ref, o_ref, stat_ref):
        gs = HEADS_TOGETHER
        mask = _band(pl.program_id(0), gs)
        qt, kt, vt, tab, tab2 = _attn_operands(q_ref, kp_ref, k_ref, vp_ref, v_ref, tp_ref, t_ref)
        firsts = [(j, first) for j in range(N_KV_HEADS) for first in range(j * group, (j + 1) * group, gs)]
        ks = [_rope(_head(kt, j), tab2).astype(BF16) for j in range(N_KV_HEADS)]
        scores = [_dot_tn(ks[j], _group_heads(qt, first, gs, tab)) for j, first in firsts]
        soft = [_softmax(s, _sink_row(s_ref, first, gs), mask) for s, (j, first) in zip(scores, firsts)]
        outs, ms, invs = [], [], []
        for (e, m, inv), (j, first) in zip(soft, firsts):
            o = _dot(_head(vt, j).astype(BF16), e) * inv
            outs += [o[:, g * BLOCK:(g + 1) * BLOCK] for g in range(gs)]
            ms += _per_head(m, gs)
            invs += _per_head(inv, gs)
        o_ref[...] = jnp.concatenate(outs, axis=0).T.astype(BF16)
        stat_ref[0] = jnp.concatenate(ms, axis=0)
        stat_ref[1] = jnp.concatenate(invs, axis=0)

    return _call(
        body, name=name, grid=(T // BLOCK,),
        in_specs=_attn_specs(D, kvd, T // BLOCK),
        out_specs=[pl.BlockSpec((BLOCK, D), lambda n: (n, 0)), pl.BlockSpec((2, heads, BLOCK), lambda n: (0, 0, n))],
        out_shape=[SDS((T, D), BF16), SDS((2, heads, T), F32)],
        semantics=("parallel",), args=(q, kv, kv, kv, kv, tabs, tabs, sinks), ride=ride)


def attention_bwd(q, kv, tabs, sinks, do, o, stats, *, name, ride=None):
    T, D = q.shape
    kvd = kv.shape[1] // 2
    heads = D // HEAD_DIM
    group = heads // N_KV_HEADS
    nb = T // BLOCK

    def body(q_ref, kp_ref, k_ref, vp_ref, v_ref, tp_ref, t_ref, s_ref, do_ref, o_ref, stat_ref,
             dq_ref, dkv_ref, ds_ref, carry):
        n = pl.program_id(0)

        @pl.when(n == 0)
        def _():
            carry[...] = jnp.zeros_like(carry)

        @pl.when(n < nb)
        def _():
            block(n, q_ref, kp_ref, k_ref, vp_ref, v_ref, tp_ref, t_ref, s_ref, do_ref, o_ref, stat_ref,
                  dq_ref, dkv_ref, ds_ref, carry)

        @pl.when(n == nb)
        def _():
            dkv_ref[...] = carry[...].astype(BF16)

    def block(n, q_ref, kp_ref, k_ref, vp_ref, v_ref, tp_ref, t_ref, s_ref, do_ref, o_ref, stat_ref,
              dq_ref, dkv_ref, ds_ref, carry):
        gs = HEADS_TOGETHER
        mask = _band(n, gs)
        qt, kt, vt, tab, tab2 = _attn_operands(q_ref, kp_ref, k_ref, vp_ref, v_ref, tp_ref, t_ref)
        dot = do_ref[...].astype(F32).T
        odo = o_ref[...].astype(F32).T * dot
        dl_all = jnp.concatenate([jnp.sum(_head(odo, h), axis=0, keepdims=True) for h in range(heads)], axis=0)
        m_all, inv_all = stat_ref[0], stat_ref[1]
        row = lambda t, first: jnp.concatenate([t[first + g:first + g + 1] for g in range(gs)], axis=1)
        lane = lax.broadcasted_iota(jnp.int32, (8, 128), 1)
        dsink = jnp.zeros((8, 128), F32)
        firsts = [(j, first) for j in range(N_KV_HEADS) for first in range(j * group, (j + 1) * group, gs)]
        ks = [_rope(_head(kt, j), tab2).astype(BF16) for j in range(N_KV_HEADS)]
        vs = [_head(vt, j).astype(BF16) for j in range(N_KV_HEADS)]
        qs = [_group_heads(qt, first, gs, tab) for _, first in firsts]
        dos = [_group_heads(dot, first, gs) for _, first in firsts]
        scores = [_dot_tn(ks[j], q) for q, (j, _) in zip(qs, firsts)]
        dps = [_dot_tn(vs[j], do) for do, (j, _) in zip(dos, firsts)]
        ps, dscs = [], []
        for s, dp, (j, first) in zip(scores, dps, firsts):
            m, inv, dl = row(m_all, first), row(inv_all, first), row(dl_all, first)
            e = jnp.exp(jnp.where(mask, s.astype(BF16), NEG) - m.astype(BF16))
            p = e * inv.astype(BF16)
            dscs.append(p * (dp.astype(BF16) - dl.astype(BF16)))
            ps.append(p)
            weight = jnp.exp(_sink_row(s_ref, first, gs) - m) * inv * dl
            for g in range(gs):
                dsink = dsink - jnp.where(lane == first + g, jnp.sum(weight[:, g * BLOCK:(g + 1) * BLOCK]), 0.0)
        dqs = []
        dks = [jnp.zeros((HEAD_DIM, 2 * BLOCK), F32) for _ in range(N_KV_HEADS)]
        dvs = [jnp.zeros((HEAD_DIM, 2 * BLOCK), F32) for _ in range(N_KV_HEADS)]
        for p, dsc, q, do, (j, _) in zip(ps, dscs, qs, dos, firsts):
            dq = _dot(ks[j], dsc) * SCORE_SCALE
            dqs += [_rope_t(dq[:, g * BLOCK:(g + 1) * BLOCK], tab) for g in range(gs)]
            dks[j] = dks[j] + _dot_nt(q, dsc)
            dvs[j] = dvs[j] + _dot_nt(do, p)
        dks = [_rope_t(dk, tab2) for dk in dks]
        dq_ref[...] = jnp.concatenate(dqs, axis=0).T.astype(BF16)
        dkv = jnp.concatenate(dks + dvs, axis=0)
        dkv_ref[...] = (carry[...] + dkv[:, :BLOCK].T).astype(BF16)
        carry[...] = dkv[:, BLOCK:].T
        _accumulate(ds_ref, n == 0, dsink)

    cur = lambda n: jnp.minimum(n, nb - 1)
    blk = lambda w: pl.BlockSpec((BLOCK, w), lambda n: (cur(n), 0))
    return _call(
        body, name=name, grid=(nb + 1,),
        in_specs=_attn_specs(D, kvd, nb) + [blk(D), blk(D), pl.BlockSpec((2, heads, BLOCK), lambda n: (0, 0, cur(n)))],
        out_specs=[blk(D), pl.BlockSpec((BLOCK, 2 * kvd), lambda n: (jnp.maximum(n - 1, 0), 0)),
                   pl.BlockSpec((8, 128), lambda n: (0, 0))],
        out_shape=[SDS((T, D), BF16), SDS((T, 2 * kvd), BF16), SDS((8, 128), F32)],
        scratch_shapes=[pltpu.VMEM((BLOCK, 2 * kvd), F32)],
        semantics=("arbitrary",), args=(q, kv, kv, kv, kv, tabs, tabs, sinks, do, o, stats), ride=ride)


def matmul_nt_normbwd(da, w, h_in, g, dh_out, *, name, ride=None, tm=ROW_TILE):
    T, D = h_in.shape
    S, _, K = da.shape

    def body(*refs):
        da_refs, w_refs = refs[:S], refs[S:2 * S]
        h_ref, g_ref, dh_ref, o_ref, dg_ref = refs[2 * S:]
        subs = _sub_tiles(tm)
        dns = []
        for rows in subs:
            dn = _dot_nt(da_refs[0][rows, :], w_refs[0][...])
            for s in range(1, S):
                dn = dn + _dot_nt(da_refs[s][rows, :], w_refs[s][...])
            dns.append(dn)
        dg = jnp.zeros((1, D), F32)
        for rows, dn in zip(subs, dns):
            dx, hh = _rmsnorm_bwd(h_ref[rows, :].astype(F32), g_ref[...], dn)
            o_ref[rows, :] = dh_ref[rows, :] + dx
            dg = dg + jnp.sum(dn * hh, axis=0, keepdims=True)
        _accumulate(dg_ref, pl.program_id(0) == 0, dg)

    row = pl.BlockSpec((tm, D), lambda i: (i, 0))
    vec = pl.BlockSpec((1, D), lambda i: (0, 0))
    part = lambda s: pl.BlockSpec((None, tm, K), lambda i: (s, i, 0))
    cols = lambda s: pl.BlockSpec((D, K), lambda i: (0, s), pipeline_mode=pl.Buffered(1))
    return _call(
        body, name=name, grid=(T // tm,),
        in_specs=[part(s) for s in range(S)] + [cols(s) for s in range(S)] + [row, vec, row],
        out_specs=[row, vec],
        out_shape=[SDS((T, D), F32), SDS((1, D), F32)],
        semantics=("arbitrary",), args=[da] * S + [w] * S + [h_in, g, dh_out], ride=ride)


def matmuls_nt_normbwd(das, ws, h_in, gs, dh_out, then, *, name, ride=None, tm=ROW_TILE):
    T, D = h_in.shape
    tm = min(tm, T)
    n = len(das)

    def body(*refs):
        da_refs, w_refs, g_refs = refs[:n], refs[n:2 * n], refs[2 * n:3 * n]
        h_ref, dh_ref, z_ref, g2_ref, w2_ref, o_ref = refs[3 * n:3 * n + 6]
        dg_refs, (dz_ref, dg2_ref, da_ref) = refs[3 * n + 6:4 * n + 6], refs[4 * n + 6:]
        first = pl.program_id(0) == 0
        subs = _sub_tiles(tm)
        dns = [[_dot_nt(da_ref_[rows, :], w_ref[...]) for da_ref_, w_ref in zip(da_refs, w_refs)] for rows in subs]
        dgs, dg2 = [jnp.zeros((1, D), F32) for _ in range(n)], jnp.zeros((1, D), F32)
        for rows, dn_sub in zip(subs, dns):
            hf = h_ref[rows, :].astype(F32)
            r = _rms_r(hf)
            hh = hf * r
            total = dh_ref[rows, :].astype(F32)
            for b, (dn, g_ref) in enumerate(zip(dn_sub, g_refs)):
                gd = g_ref[...] * dn
                total = total + r * (gd - hh * jnp.mean(hh * gd, axis=-1, keepdims=True))
                dgs[b] = dgs[b] + jnp.sum(dn * hh, axis=0, keepdims=True)
            o_ref[rows, :] = total.astype(STREAM)
            dz, zh = _rmsnorm_bwd(z_ref[rows, :].astype(F32), g2_ref[...], total)
            dz = dz.astype(BF16)
            dz_ref[rows, :] = dz
            dg2 = dg2 + jnp.sum(total * zh, axis=0, keepdims=True)
            da_ref[rows, :] = _dot_nt(dz, w2_ref[...]).astype(BF16)
        for dg_ref, dg in zip(dg_refs + (dg2_ref,), dgs + [dg2]):
            _accumulate(dg_ref, first, dg)

    row = pl.BlockSpec((tm, D), lambda i: (i, 0))
    vec = pl.BlockSpec((1, D), lambda i: (0, 0))
    then_in, then_out, then_shape = _then_specs(then, tm, T, D)
    return _call(
        body, name=name, grid=(T // tm,),
        in_specs=[pl.BlockSpec((tm, da.shape[1]), lambda i: (i, 0)) for da in das]
        + [pl.BlockSpec(w.shape, lambda i: (0, 0)) for w in ws] + [vec] * n + [row, row] + then_in,
        out_specs=[row] + [vec] * n + then_out,
        out_shape=[SDS((T, D), STREAM)] + [SDS((1, D), F32)] * n + then_shape,
        semantics=("arbitrary",), args=list(das) + list(ws) + list(gs) + [h_in, dh_out] + list(then), ride=ride)


def matmul_tn(a, b, *, tb, name, ride=None, ta=MXU_WIDTH):
    T, Ka = a.shape
    S, _, Nb = b.shape
    per = Nb // tb

    def body(a_ref, b_ref, o_ref):
        o_ref[...] = _dot_tn(a_ref[...], b_ref[...]).astype(BF16)

    out = _call(
        body, name=name, grid=(S * per, Ka // ta),
        in_specs=[pl.BlockSpec((T, ta), lambda j, i: (0, i)),
                  pl.BlockSpec((None, T, tb), lambda j, i: (j // per, 0, j % per))],
        out_specs=[pl.BlockSpec((ta, tb), lambda j, i: (i, j))],
        out_shape=[SDS((Ka, S * Nb), BF16)],
        semantics=("parallel", "parallel"), args=(a, b), ride=ride)
    return out[0] if ride is None else (out[0][0], out[1])


def conv_bwd(dy, bcx, conv_w, *, name, ride=None, tm=ROW_TILE):
    T, D = dy.shape
    nt = T // tm
    hb = tm // BF16_ROWS
    last = T // BF16_ROWS - 1

    def body(dy_ref, dyn_ref, b_ref, bn_ref, c_ref, u_ref, cp_ref, up_ref, cw_ref, o_ref, dw_ref):
        i = pl.program_id(0)
        c, u = c_ref[...].astype(F32), u_ref[...].astype(F32)
        cu = c * u
        cup = jnp.where(i == 0, 0.0, cp_ref[...].astype(F32) * up_ref[...].astype(F32))
        cu1, cu2 = _shift_down(cup, cu, 1), _shift_down(cup, cu, 2)
        w0, w1, w2 = cw_ref[0:1, :], cw_ref[1:2, :], cw_ref[2:3, :]
        dyf = dy_ref[...].astype(F32)
        o_ref[:, 0:D] = (dyf * (w0 * cu2 + w1 * cu1 + w2 * cu)).astype(BF16)
        dcv = dyf * b_ref[...].astype(F32)
        dcvn = jnp.where(i == nt - 1, 0.0, dyn_ref[...].astype(F32) * bn_ref[...].astype(F32))
        dcu = w2 * dcv + w1 * _shift_up(dcv, dcvn, 1) + w0 * _shift_up(dcv, dcvn, 2)
        o_ref[:, D:2 * D] = (dcu * u).astype(BF16)
        o_ref[:, 2 * D:3 * D] = (dcu * c).astype(BF16)
        row = lax.broadcasted_iota(jnp.int32, (8, D), 0)
        dw = jnp.zeros((8, D), F32)
        for tap, t in enumerate((cu2, cu1, cu)):
            dw = jnp.where(row == tap, jnp.sum(dcv * t, axis=0, keepdims=True), dw)
        _accumulate(dw_ref, i == 0, dw)

    tile = lambda col: pl.BlockSpec((tm, D), lambda i: (i, col))
    prev = lambda col: pl.BlockSpec((BF16_ROWS, D), lambda i: (jnp.maximum(i * hb - 1, 0), col))
    nxt = lambda col: pl.BlockSpec((BF16_ROWS, D), lambda i: (jnp.minimum((i + 1) * hb, last), col))
    return _call(
        body, name=name, grid=(nt,),
        in_specs=[tile(0), nxt(0), tile(0), nxt(0), tile(1), tile(2), prev(1), prev(2),
                  pl.BlockSpec((3, D), lambda i: (0, 0))],
        out_specs=[pl.BlockSpec((tm, 3 * D), lambda i: (i, 0)), pl.BlockSpec((8, D), lambda i: (0, 0))],
        out_shape=[SDS((T, 3 * D), BF16), SDS((8, D), F32)],
        semantics=("arbitrary",), args=(dy, dy, bcx, bcx, bcx, bcx, bcx, bcx, conv_w), ride=ride)


class NoTraffic:
    def ride(self, kernel_name):
        return None

    def landed(self, kernel_name, results, wts):
        pass

    def grad(self, key, value):
        pass


def local_step(x, target, wts, vec, traffic):
    T, D = x.shape
    tabs = rope_tables(T)
    small = {}

    def run(builder, *args, name, **kw):
        ride = traffic.ride(name)
        if ride is None:
            return builder(*args, name=name, **kw)
        out, extra = builder(*args, name=name, ride=ride, **kw)
        traffic.landed(name, extra, wts)
        return out

    bcx, xn1 = run(norm_matmul, x, vec["a_pre"], wts["w_in"], tn=3 * D, split=1, name="a_in")
    bcx = bcx[0]
    h1, z0, y0 = run(conv_mix_out, bcx, vec["conv_w"], wts["w_out"], vec["a_post"], x, name="a_out")
    gu0, act0, xt2 = run(norm_swiglu_in, h1, vec["ffn_pre0"], wts["gu0"], name="ffn0_in")
    h2, z1 = run(plain_mix_out, act0, wts["wd0"], vec["ffn_post0"], h1, name="ffn0_out")
    kvp, xkv, qp, xq = run(norm2_matmul, h2, [vec["kv_norm"], vec["b_pre"]], [wts["w_kv"], wts["w_q"]],
                           name="kvq_in")
    attn, attn_stats = run(attention_fwd, qp, kvp, tabs, vec["sinks"], name="attn_fwd")
    h3, z2 = plain_mix_out(attn, wts["w_o"], vec["b_post"], h2, name="attn_out", tm=BIG_ROW_TILE)
    gu1, act1, xt3 = run(norm_swiglu_in, h3, vec["ffn_pre1"], wts["gu1"], name="ffn1_in")
    dy, dz3, small["ffn_post1"], dact1, loss = plain_mix_out(act1, wts["wd1"], vec["ffn_post1"], h3, name="ffn1_out",
                                                             target=target)

    def ffn_bwd(layer, dz, dact, gu, act, xt, h_in, dh, then, gu_first):
        tag = "ffn%d" % layer
        dwd = lambda: traffic.grad("wd%d" % layer, run(matmul_tn, act, dz[None], tb=D, name=tag + "_dwd"))
        dwgu = lambda: traffic.grad("gu%d" % layer, run(swiglu_bwd_tn, xt, dact, gu, name=tag + "_dwgu"))
        for step in ((dwgu, dwd) if gu_first else (dwd, dwgu)):
            step()
        dh_in, small["ffn_pre%d" % layer], dz_, dg_, da_ = run(
            swiglu_bwd_in, dact, gu, wts["gu%d" % layer], h_in, vec["ffn_pre%d" % layer], dh, then,
            name=tag + "_in_bwd")
        return dh_in, dz_, dg_, da_

    dh3, dz2, small["b_post"], dattn = ffn_bwd(1, dz3, dact1, gu1, act1, xt3, h3, dy,
                                               (z2, vec["b_post"], wts["w_o"]), gu_first=False)
    traffic.grad("w_o", matmul_tn(attn, dz2[None], tb=D, name="attn_dwo"))
    dq, dkv, small["sinks"] = run(attention_bwd, qp, kvp, tabs, vec["sinks"], dattn, attn, attn_stats,
                                  name="attn_bwd")
    traffic.grad("w_q", matmul_tn(xq, dq[None], tb=D, name="attn_dwq"))
    traffic.grad("w_kv", matmul_tn(xkv, dkv[None], tb=dkv.shape[1], name="attn_dwkv"))
    dh2, small["b_pre"], small["kv_norm"], dz1, small["ffn_post0"], dact0 = run(
        matmuls_nt_normbwd, [dq, dkv], [wts["w_q"], wts["w_kv"]], h2, [vec["b_pre"], vec["kv_norm"]], dh3,
        (z1, vec["ffn_post0"], wts["wd0"]), name="qkv_in_bwd")
    dh1, dz0, small["a_post"], dyc = ffn_bwd(0, dz1, dact0, gu0, act0, xt2, h1, dh2,
                                             (z0, vec["a_post"], wts["w_out"]), gu_first=True)
    traffic.grad("w_out", run(matmul_tn, y0, dz0[None], tb=D, name="a_dwout"))
    dbcx, small["conv_w"] = run(conv_bwd, dyc, bcx, vec["conv_w"], name="a_conv_bwd")
    traffic.grad("w_in", run(matmul_tn, xn1, dbcx[None], tb=3 * D // 2, name="a_dwin"))
    dx, small["a_pre"] = run(matmul_nt_normbwd, dbcx[None], wts["w_in"], x, vec["a_pre"], dh1, name="a_in_bwd")
    return loss, dx, small


SMALL_ROWS = 16
LOSS_ROW = 13

WHOLE = None
GATHER_PLAN = {"cast_rest": [("w_in", WHOLE)],
               "a_in": [("w_out", WHOLE), ("gu0", (0, 18))],
               "a_out": [("gu0", (18, 14))],
               "ffn0_in": [("wd0", WHOLE), ("w_kv", WHOLE), ("w_q", WHOLE), ("gu1", (0, 4))],
               "ffn0_out": [("w_o", WHOLE), ("gu1", (4, 8))],
               "attn_fwd": [("gu1", (12, 20))],
               "ffn1_in": [("wd1", WHOLE)]}
GATHER_LATE = ["cast_rest", "a_in"]
PAIR_PLAN = {"ffn1_dwgu": ["wd1"], "ffn1_in_bwd": ["gu1"], "attn_bwd": ["w_o"], "qkv_in_bwd": ["w_q", "w_kv"],
             "ffn0_dwd": ["gu0"], "ffn0_in_bwd": ["wd0"], "a_conv_bwd": ["w_out"], "chip_reduce_early": ["w_in"]}
CHIP_PLAN = {"ffn1_in_bwd": [("wd1", WHOLE)], "attn_bwd": [("gu1", WHOLE)],
             "ffn0_dwgu": [("w_o", WHOLE), ("w_q", WHOLE), ("w_kv", WHOLE)],
             "ffn0_in_bwd": [("gu0", WHOLE)], "a_conv_bwd": [("wd0", (0, 12))],
             "a_dwin": [("wd0", (12, 10)), ("w_out", WHOLE)], "a_in_bwd": [("w_in", WHOLE)]}
HALF_PLAN = {"a_in_bwd": ["gu0", "gu1", "wd0", "wd1", "w_kv", "w_q", "w_o", "w_out"]}
GRAD_KIND = dict(KIND, gu0="split", gu1="split")


class Traffic:
    def __init__(self, wholes, quarter, c_arr, pc_arr):
        self.wholes, self.quarter, self.c_arr, self.pc_arr = wholes, quarter, c_arr, pc_arr
        self.views, self.sums, self.got = {}, {}, {}
        self.reduced = {}
        self.stages = {}

    def reduce(self, keys, name):
        args = ([self.sums[k] for k in keys], [self.got[k] for k in keys], [GRAD_KIND[k] for k in keys], self.pc_arr)
        if name not in PAIR_PLAN:
            return chip_reduce(*args, name=name)
        pairs = PAIR_PLAN[name]
        out, got = chip_reduce(*args, name=name, ride=pair_ride([self.views[k] for k in pairs]))
        self.pair_sums(pairs, got)
        return out

    def pair_sums(self, keys, got):
        for k, theirs in zip(keys, got):
            self.sums[k] = pair_add(self.views[k], theirs, self.c_arr, name="pair_add_" + k)

    def ride(self, name, small=None):
        rides, stages = [], []
        if name in GATHER_PLAN:
            plan = GATHER_PLAN[name]
            rides.append(gather_ride([self.wholes[k] for k, _ in plan],
                                     [(KIND[k], self.quarter[k], part) for k, part in plan], small,
                                     late=name in GATHER_LATE))
            stages.append(("gather", [k for k, _ in plan]))
        if name in HALF_PLAN:
            keys = HALF_PLAN[name]
            rides.append(half_ride(self.reduce(keys, "chip_reduce_early")))
            stages.append(("half", keys))
        if name in CHIP_PLAN:
            plan = CHIP_PLAN[name]
            rides.append(chip_ride([self.sums[k] for k, _ in plan],
                                   [(GRAD_KIND[k], self.quarter[k], part) for k, part in plan],
                                   earlier=[self.got.get(k) for k, _ in plan]))
            stages.append(("chip", [k for k, _ in plan]))
        if name in PAIR_PLAN:
            keys = PAIR_PLAN[name]
            rides.append(pair_ride([self.views[k] for k in keys]))
            stages.append(("pair", keys))
        self.stages[name] = stages
        return join(rides)

    def landed(self, name, results, wts):
        results = list(results)
        for stage, keys in self.stages[name]:
            mine, results = results[:len(keys)], results[len(keys):]
            if stage == "gather":
                for k, whole in zip(keys, mine):
                    self.wholes[k] = wts[k] = whole
            elif stage == "chip":
                self.got.update(zip(keys, mine))
            elif stage == "half":
                self.reduced.update(zip(keys, mine))
            else:
                self.pair_sums(keys, mine)

    def grad(self, key, value):
        r, ws = self.quarter[key]
        view = {"row": (N_CHIPS, 2, r // 2, ws), "col": (1, 2, r // 2, N_CHIPS * ws), "split": (2, 2, r // 2, 2 * ws)}
        self.views[key] = value.reshape(view[GRAD_KIND[key]])


def kernel(x, a_pre_norm, a_w_in, a_conv_w, a_w_out, a_post_norm, ffn_pre_norm, ffn_w_gate_up, ffn_w_down, ffn_post_norm, kv_norm, w_kv, b_pre_norm, b_w_q, b_sinks, b_w_o, b_post_norm, loss_target, m_a_pre_norm, m_a_w_in, m_a_conv_w, m_a_w_out, m_a_post_norm, m_ffn_pre_norm, m_ffn_w_gate_up, m_ffn_w_down, m_ffn_post_norm, m_kv_norm, m_w_kv, m_b_pre_norm, m_b_w_q, m_b_sinks, m_b_w_o, m_b_post_norm, v_a_pre_norm, v_a_w_in, v_a_conv_w, v_a_w_out, v_a_post_norm, v_ffn_pre_norm, v_ffn_w_gate_up, v_ffn_w_down, v_ffn_post_norm, v_kv_norm, v_w_kv, v_b_pre_norm, v_b_w_q, v_b_sinks, v_b_w_o, v_b_post_norm):
    T, D = x.shape[1], x.shape[2]
    xi, yi, ci = _place()
    p = 2 * xi + yi
    p_arr = jnp.reshape(p, (1,)).astype(jnp.int32)
    c_arr = jnp.reshape(ci, (1,)).astype(jnp.int32)
    pc_arr = jnp.stack([p, ci]).astype(jnp.int32)
    me_arr = jnp.reshape(4 * xi + 2 * yi + ci, (1,)).astype(jnp.int32)
    qd = D // N_CHIPS

    big = {"w_in": (a_w_in, 0), "w_out": (a_w_out, 0), "gu0": (ffn_w_gate_up, 0), "gu1": (ffn_w_gate_up, 1),
           "wd0": (ffn_w_down, 0), "wd1": (ffn_w_down, 1), "w_kv": (w_kv[None], 0), "w_q": (b_w_q, 0),
           "w_o": (b_w_o, 0)}
    names = list(big)
    quarter = {k: w.shape[1:] for k, (w, _) in big.items()}
    source = lambda k: big[k] + (KIND[k],)
    traffic = Traffic(dict(zip(names[:1], cast_quarters([source(names[0])], p_arr, name="cast_first"))), quarter,
                      c_arr, pc_arr)
    small_shard = jnp.concatenate([a_pre_norm, a_post_norm, a_conv_w[0], jnp.zeros((3, qd), F32)], axis=0)
    wts = {}
    rest, (*landed, small_full) = cast_quarters([source(k) for k in names[1:]], p_arr, name="cast_rest",
                                                ride=traffic.ride("cast_rest", small_shard))
    traffic.wholes.update(zip(names[1:], rest))
    traffic.landed("cast_rest", landed, wts)
    rows = lambda k: jnp.transpose(small_full[:, k], (1, 0, 2)).reshape(-1, D)
    vec = {"a_pre": rows(slice(0, 1)), "a_post": rows(slice(1, 2)), "conv_w": rows(slice(2, 5)),
           "ffn_pre0": ffn_pre_norm[0:1], "ffn_pre1": ffn_pre_norm[1:2],
           "ffn_post0": ffn_post_norm[0:1], "ffn_post1": ffn_post_norm[1:2],
           "kv_norm": kv_norm[None], "b_pre": b_pre_norm, "b_post": b_post_norm, "sinks": b_sinks}

    loss, dx, small = local_step(x[0], loss_target[0], wts, vec, traffic)

    pad = lambda a: jnp.pad(a, ((0, 0), (0, D - a.shape[1])))
    small_block = jnp.concatenate(
        [small["a_pre"], small["a_post"], small["conv_w"][0:3], small["ffn_pre0"], small["ffn_pre1"],
         small["ffn_post0"], small["ffn_post1"], small["kv_norm"], small["b_pre"], small["b_post"],
         pad(small["sinks"][0:1]), pad(loss[0:1]), jnp.zeros((SMALL_ROWS - LOSS_ROW - 1, D), F32)], axis=0)
    late = [k for k in names if k not in traffic.reduced]
    *swapped, small_blocks = alone(join([half_ride(traffic.reduce(late, "chip_reduce_late")),
                                         chip_ride([], [], small_block)]), name="last_exchange")
    traffic.reduced.update(zip(late, swapped))
    grad = {k: traffic.reduced[k].reshape(quarter[k]) for k in names}
    small_sum = small_reduce(small_blocks, me_arr)

    out = {}
    out["a_w_in"] = adamw(a_w_in, [grad["w_in"]], m_a_w_in, v_a_w_in, name="adamw_a_w_in")
    out["a_w_out"] = adamw(a_w_out, [grad["w_out"]], m_a_w_out, v_a_w_out, name="adamw_a_w_out")
    out["ffn_w_gate_up"] = adamw(ffn_w_gate_up, [grad["gu0"], grad["gu1"]], m_ffn_w_gate_up, v_ffn_w_gate_up,
                                 name="adamw_ffn_w_gate_up")
    out["ffn_w_down"] = adamw(ffn_w_down, [grad["wd0"], grad["wd1"]], m_ffn_w_down, v_ffn_w_down,
                              name="adamw_ffn_w_down")
    out["w_kv"] = [o[0] for o in adamw(w_kv[None], [grad["w_kv"]], m_w_kv[None], v_w_kv[None], name="adamw_w_kv")]
    out["b_w_q"] = adamw(b_w_q, [grad["w_q"]], m_b_w_q, v_b_w_q, name="adamw_b_w_q")
    out["b_w_o"] = adamw(b_w_o, [grad["w_o"]], m_b_w_o, v_b_w_o, name="adamw_b_w_o")

    def pack(a_pre, a_post, conv, ffn_pre, ffn_post, kvn, b_pre, b_post, sinks):
        return jnp.concatenate([pad(a_pre), pad(a_post), pad(conv[0]), ffn_pre, ffn_post, kvn[None], b_pre, b_post,
                                pad(sinks), jnp.zeros((SMALL_ROWS - 13, D), F32)], axis=0)

    g_small = jnp.concatenate([pad(lax.dynamic_slice(small_sum, (0, p * qd), (5, qd))), small_sum[5:]], axis=0)
    w_small = pack(a_pre_norm, a_post_norm, a_conv_w, ffn_pre_norm, ffn_post_norm, kv_norm, b_pre_norm, b_post_norm,
                   b_sinks)
    m_small = pack(m_a_pre_norm, m_a_post_norm, m_a_conv_w, m_ffn_pre_norm, m_ffn_post_norm, m_kv_norm,
                   m_b_pre_norm, m_b_post_norm, m_b_sinks)
    v_small = pack(v_a_pre_norm, v_a_post_norm, v_a_conv_w, v_ffn_pre_norm, v_ffn_post_norm, v_kv_norm,
                   v_b_pre_norm, v_b_post_norm, v_b_sinks)
    packed = adamw(w_small[None], [g_small], m_small[None], v_small[None], name="adamw_small")
    ns = b_sinks.shape[1]
    unpack = lambda a: {"a_pre_norm": a[0:1, :qd], "a_post_norm": a[1:2, :qd], "a_conv_w": a[None, 2:5, :qd],
                        "ffn_pre_norm": a[5:7], "ffn_post_norm": a[7:9], "kv_norm": a[9], "b_pre_norm": a[10:11],
                        "b_post_norm": a[11:12], "b_sinks": a[12:13, :ns]}
    unpacked = [unpack(a[0]) for a in packed]
    for k in unpacked[0]:
        out[k] = [u[k] for u in unpacked]

    order = ["a_pre_norm", "a_w_in", "a_conv_w", "a_w_out", "a_post_norm", "ffn_pre_norm", "ffn_w_gate_up",
             "ffn_w_down", "ffn_post_norm", "kv_norm", "w_kv", "b_pre_norm", "b_w_q", "b_sinks", "b_w_o",
             "b_post_norm"]
    return (small_sum[LOSS_ROW, 0], dx[None], *[out[k][0] for k in order], *[out[k][1] for k in order],
            *[out[k][2] for k in order], *[out[k][3] for k in order])
```

```python
import math

import jax
import jax.numpy as jnp
from jax import lax
from jax.experimental import pallas as pl
from jax.experimental.pallas import tpu as pltpu

F32 = jnp.float32
BF16 = jnp.bfloat16
SDS = jax.ShapeDtypeStruct
MESH = pl.DeviceIdType.MESH
DMA = pltpu.SemaphoreType.DMA
HBM_SPEC = pl.BlockSpec(memory_space=pltpu.HBM)

EPS = 1e-6
NEG = -1e30
HEAD_DIM = 64
N_KV_HEADS = 4
BLOCK = 128
ROT_DIM = HEAD_DIM // 4
ROPE_THETA = 500000.0
N_CHIPS = 4

ADAM_LR = 0.001
ADAM_B1 = 0.9
ADAM_B2 = 0.999
ADAM_EPS = 1e-08
ADAM_WD = 0.01
ADAM_STEP = 10

VMEM_LIMIT_BYTES = 52 * 1024 * 1024
ROW_TILE = 512
BF16_ROWS = 16
STREAM = BF16
MXU_WIDTH = 256

KIND = {"w_in": "col", "gu0": "col", "gu1": "col", "w_out": "row", "wd0": "row", "wd1": "row", "w_kv": "row",
        "w_q": "row", "w_o": "row"}


def _params(*semantics):
    return pltpu.CompilerParams(dimension_semantics=semantics, vmem_limit_bytes=VMEM_LIMIT_BYTES)


def _row_tile(rows, limit, step=8):
    return max(t for t in range(step, limit + 1, step) if rows % t == 0)


def _place():
    return lax.axis_index("x"), lax.axis_index("y"), lax.axis_index("c")


def _other_chips(x, y):
    return [(1 - x, y), (x, 1 - y), (1 - x, 1 - y)]


def _remote(src, dst, send_sem, recv_sem, to):
    return pltpu.make_async_remote_copy(src_ref=src, dst_ref=dst, send_sem=send_sem, recv_sem=recv_sem,
                                        device_id=to, device_id_type=MESH)


def _full_shape(kind, quarter):
    r, ws = quarter
    return (N_CHIPS * r, ws) if kind == "row" else (r, N_CHIPS * ws)


def _rows_of(h, part):
    lo, n = (0, h) if part is None else (part[0] * BF16_ROWS, part[1] * BF16_ROWS)
    assert lo + n <= h, (h, part)
    return lo, n


def _half_of_quarter(ref, kind, quarter, part, q, half):
    r, ws = quarter
    h = r // 2
    lo, n = _rows_of(h, part)
    if kind == "row":
        return ref.at[pl.ds(pl.multiple_of(q * r + half * h + lo, BF16_ROWS), n)]
    return ref.at[pl.ds(pl.multiple_of(half * h + lo, BF16_ROWS), n), pl.ds(pl.multiple_of(q * ws, 128), ws)]


class Ride:
    def __init__(self, operands, out_shape, aliases, sems, make):
        self.operands, self.out_shape, self.aliases, self.sems, self.make = operands, out_shape, aliases, sems, make

    def stages(self, ins, outs, sems):
        made = self.make(ins, outs, sems)
        return made if len(made) == 4 else (made[0], None, None, made[1])


def join(rides):
    rides = [r for r in rides if r is not None]
    if len(rides) < 2:
        return rides[0] if rides else None
    aliases, at = {}, [0, 0, 0]
    cuts = []
    for r in rides:
        aliases.update({at[0] + i: at[1] + o for i, o in r.aliases.items()})
        cuts.append(tuple(at))
        at = [at[0] + len(r.operands), at[1] + len(r.out_shape), at[2] + len(r.sems)]
    cuts.append(tuple(at))

    def make(ins, outs, sem):
        made = [r.stages(ins[lo[0]:hi[0]], outs[lo[1]:hi[1]], sem[lo[2]:hi[2]]) for r, lo, hi in zip(rides, cuts, cuts[1:])]
        def all_of(k):
            def stage():
                for m in made:
                    if m[k] is not None:
                        m[k]()
            return stage

        if all(m[1] is None for m in made):
            return all_of(0), all_of(3)
        return all_of(0), all_of(1), all_of(2), all_of(3)

    return Ride(sum((list(r.operands) for r in rides), []), sum((list(r.out_shape) for r in rides), []), aliases,
                sum((list(r.sems) for r in rides), []), make)


def _call(body, *, name, grid, in_specs, out_specs, out_shape, args, scratch_shapes=(), semantics=None, ride=None,
          prefetch=None):
    pre = 0 if prefetch is None else 1
    n_in, n_out, n_scr = len(in_specs), len(out_specs), len(scratch_shapes)
    r_in, r_out = (len(ride.operands), len(ride.out_shape)) if ride is not None else (0, 0)
    a, b = pre + n_in, pre + n_in + r_in
    c, d = b + n_out, b + n_out + r_out
    e = d + n_scr

    def riding(*refs):
        start, relay, relay_again, finish = ride.stages(refs[a:b], refs[c:d], refs[e:])
        step, steps = pl.program_id(0), 1
        for k, extent in enumerate(grid):
            step = pl.program_id(k) if k == 0 else step * extent + pl.program_id(k)
            steps *= extent
        pl.when(step == 0)(start)
        if relay is not None:
            pl.when(step == steps // 2)(relay)
            pl.when(step == steps - 1)(relay_again)
        body(*refs[:a], *refs[b:c], *refs[d:e])
        pl.when(step == steps - 1)(finish)

    if ride is None:
        kernel_body, extra_in, extra_out, extra_shape, extra_scr, aliases = body, [], [], [], [], {}
        params = _params(*semantics)
    else:
        kernel_body, extra_in, extra_out = riding, [HBM_SPEC] * r_in, [HBM_SPEC] * r_out
        extra_shape, extra_scr = list(ride.out_shape), list(ride.sems)
        aliases = {pre + n_in + i: n_out + o for i, o in ride.aliases.items()}
        params = _params(*(("arbitrary",) * len(grid)))
    specs = dict(grid=grid, in_specs=list(in_specs) + extra_in, out_specs=list(out_specs) + extra_out,
                 scratch_shapes=list(scratch_shapes) + extra_scr)
    if prefetch is not None:
        specs = dict(grid_spec=pltpu.PrefetchScalarGridSpec(num_scalar_prefetch=1, **specs))
        args = (prefetch,) + tuple(args)
    outs = pl.pallas_call(kernel_body, name=name, out_shape=list(out_shape) + extra_shape,
                          input_output_aliases=aliases, compiler_params=params, **specs,
                          )(*args, *(ride.operands if ride is not None else ()))
    return outs if ride is None else (outs[:n_out], outs[n_out:])


def alone(ride, *, name):
    def body(*refs):
        n = len(ride.operands)
        stages = ride.stages(refs[:n], refs[n:n + len(ride.out_shape)], refs[n + len(ride.out_shape):])
        for stage in stages:
            if stage is not None:
                stage()

    return pl.pallas_call(
        body, name=name, in_specs=[HBM_SPEC] * len(ride.operands), out_specs=[HBM_SPEC] * len(ride.out_shape),
        out_shape=list(ride.out_shape), input_output_aliases=dict(ride.aliases), scratch_shapes=list(ride.sems),
    )(*ride.operands)


def _two_pieces(h, part):
    lo, n = (0, h // BF16_ROWS) if part is None else part
    assert n >= 2, (h, part)
    return (lo, n // 2), (lo + n // 2, n - n // 2)


def gather_ride(wholes, metas, small=None):
    n = len(wholes)
    operands, out_shape = list(wholes), [SDS(s.shape, s.dtype) for s in wholes]
    sems = [DMA((n, 4)), DMA((n, 4)), DMA((n, 4)), DMA((n, 4))]
    if small is not None:
        operands.append(small)
        out_shape.append(SDS((N_CHIPS,) + small.shape, small.dtype))
        sems += [DMA((3,)), DMA((3,)), DMA(())]

    def make(ins, outs, sem):
        send1, recv1, send2, recv2 = sem[:4]
        x, y, c = _place()
        p = 2 * x + y
        chips = _other_chips(x, y)
        across_x, across_y, across_both = [2 * qx + qy for qx, qy in chips]
        me, sibling = (x, y, c), (x, y, 1 - c)

        def region(t, q, half, piece=None):
            kind, quarter, part = metas[t]
            if piece is not None:
                part = _two_pieces(quarter[0] // 2, part)[piece]
            return _half_of_quarter(outs[t], kind, quarter, part, q, half)

        first, second, arriving = [], [], []
        landing, passing = [[], [], [], []], [[], [], [], []]
        for j, (qx, qy) in enumerate(chips):
            if small is not None:
                q = 2 * qx + qy
                first.append(_remote(ins[n], outs[n].at[p], sem[4].at[j], sem[5].at[j], (qx, qy, c)))
                arriving.append(_remote(outs[n].at[q], outs[n].at[q], sem[4].at[j], sem[5].at[j], me))
        for t in range(n):
            mine = region(t, p, c)
            for j in range(2):
                first.append(_remote(mine, mine, send1.at[t, j], recv1.at[t, j], chips[j] + (c,)))
            lands = [(across_x, None), (across_y, None), (across_both, 0), (across_both, 1)]
            for k, (q, piece) in enumerate(lands):
                landed, theirs = region(t, q, c, piece), region(t, q, 1 - c, piece)
                landing[k].append(_remote(landed, landed, send1.at[t, k], recv1.at[t, k], me))
                passing[k].append(_remote(landed, landed, send2.at[t, k], recv2.at[t, k], sibling))
                arriving.append(_remote(theirs, theirs, send2.at[t, k], recv2.at[t, k], me))
            onward = region(t, across_x, c, 0)
            second.append(_remote(onward, onward, send1.at[t, 2], recv1.at[t, 2], chips[1] + (c,)))
            onward = region(t, across_y, c, 1)
            second.append(_remote(onward, onward, send1.at[t, 3], recv1.at[t, 3], chips[0] + (c,)))
        local = [] if small is None else [pltpu.make_async_copy(ins[n], outs[n].at[p], sem[6])]

        def start():
            for cp in local + first:
                cp.start()

        def relay():
            for k in range(2):
                for t in range(n):
                    landing[k][t].wait_recv()
                    second[2 * t + k].start()
                    passing[k][t].start()

        def relay_again():
            for k in range(2, 4):
                for t in range(n):
                    landing[k][t].wait_recv()
                    passing[k][t].start()

        def finish():
            for cp in arriving:
                cp.wait_recv()
            for cp in first + second + sum(passing, []):
                cp.wait_send()
            for cp in local:
                cp.wait()

        return start, relay, relay_again, finish

    return Ride(operands, out_shape, {t: t for t in range(n)}, sems, make)


def chip_ride(sums, metas, small=None, earlier=None):
    n = len(sums)
    operands = list(sums)
    out_shape = [SDS((3, s.shape[1], quarter[1]), s.dtype) for s, (_, quarter, _) in zip(sums, metas)]
    sems = [DMA((n, 3)), DMA((n, 3))] if n else []
    if small is not None:
        operands.append(small)
        out_shape.append(SDS((8,) + small.shape, small.dtype))
        sems += [DMA((7,)), DMA((7,)), DMA(())]
    aliases = {}
    for t, buffer in enumerate(earlier or [None] * n):
        if buffer is not None:
            aliases[len(operands)] = t
            operands.append(buffer)

    def make(ins, outs, sem):
        x, y, c = _place()
        cps = []
        for j, (qx, qy) in enumerate(_other_chips(x, y)):
            q = 2 * qx + qy
            for t in range(n):
                kind, (_, ws), part = metas[t]
                rows = pl.ds(*_rows_of(ins[t].shape[1], part))
                if kind == "row":
                    src = ins[t].at[q, rows]
                elif kind == "col":
                    src = ins[t].at[0, rows, pl.ds(pl.multiple_of(q * ws, 128), ws)]
                else:
                    src = ins[t].at[q // 2, rows, pl.ds(pl.multiple_of((q % 2) * ws, 128), ws)]
                cps.append(_remote(src, outs[t].at[j, rows], sem[0].at[t, j], sem[1].at[t, j], (qx, qy, c)))
        local = []
        if small is not None:
            ssend, srecv, lsem = sem[2 * bool(n):2 * bool(n) + 3]
            local.append(pltpu.make_async_copy(ins[n], outs[n].at[0], lsem))
            for k in range(1, 8):
                peer = (x ^ (k >> 2 & 1), y ^ (k >> 1 & 1), c ^ (k & 1))
                cps.append(_remote(ins[n], outs[n].at[k], ssend.at[k - 1], srecv.at[k - 1], peer))

        def start():
            for cp in local + cps:
                cp.start()

        def finish():
            for cp in cps + local:
                cp.wait()

        return start, finish

    return Ride(operands, out_shape, aliases, sems, make)


def pair_ride(grads):
    n = len(grads)

    def make(ins, outs, sem):
        x, y, c = _place()
        cps = [_remote(ins[t].at[:, 1 - c], outs[t], sem[0].at[t], sem[1].at[t], (x, y, 1 - c)) for t in range(n)]

        def start():
            for cp in cps:
                cp.start()

        def finish():
            for cp in cps:
                cp.wait()

        return start, finish

    return Ride(list(grads), [SDS((g.shape[0],) + g.shape[2:], g.dtype) for g in grads], {}, [DMA((n,)), DMA((n,))],
                make)


def half_ride(quarters):
    n = len(quarters)

    def make(ins, outs, sem):
        x, y, c = _place()
        sends = [_remote(outs[t].at[c], outs[t].at[c], sem[0].at[t], sem[1].at[t], (x, y, 1 - c)) for t in range(n)]

        def start():
            for cp in sends:
                cp.start()

        def finish():
            for t in range(n):
                theirs = outs[t].at[1 - c]
                _remote(theirs, theirs, sem[0].at[t], sem[1].at[t], (x, y, c)).wait_recv()
            for cp in sends:
                cp.wait_send()

        return start, finish

    return Ride(list(quarters), [SDS(q.shape, q.dtype) for q in quarters], {t: t for t in range(n)},
                [DMA((n,)), DMA((n,))], make)


CAST_STEPS = 4


def cast_quarters(sources, p_arr, *, name, ride=None):
    n = len(sources)
    in_specs, out_specs, out_shape = [], [], []
    for w, layer, kind in sources:
        _, r, ws = w.shape
        tr = r // CAST_STEPS
        assert tr % BF16_ROWS == 0, w.shape
        in_specs.append(pl.BlockSpec((None, tr, ws), lambda i, p_ref, layer=layer: (layer, i, 0)))
        out_specs.append(pl.BlockSpec((tr, ws), (lambda i, p_ref: (p_ref[0] * CAST_STEPS + i, 0)) if kind == "row"
                                      else (lambda i, p_ref: (i, p_ref[0]))))
        out_shape.append(SDS(_full_shape(kind, (r, ws)), BF16))

    def body(p_ref, *refs):
        for w_ref, o_ref in zip(refs[:n], refs[n:]):
            o_ref[...] = w_ref[...].astype(BF16)

    return _call(body, name=name, grid=(CAST_STEPS,), in_specs=in_specs, out_specs=out_specs, out_shape=out_shape,
                 semantics=("parallel",), args=[w for w, _, _ in sources], ride=ride, prefetch=p_arr)


def pair_add(own, got, c_arr, *, name):
    A, _, h, W = own.shape
    th = _row_tile(h, max(BF16_ROWS, (3 << 19) // W), BF16_ROWS)

    def body(c_ref, a_ref, b_ref, o_ref):
        o_ref[...] = (a_ref[...].astype(F32) + b_ref[...].astype(F32)).astype(BF16)

    return pl.pallas_call(
        body, name=name,
        grid_spec=pltpu.PrefetchScalarGridSpec(
            num_scalar_prefetch=1, grid=(A, h // th),
            in_specs=[pl.BlockSpec((None, None, th, W), lambda q, i, c_ref: (q, c_ref[0], i, 0)),
                      pl.BlockSpec((None, th, W), lambda q, i, c_ref: (q, i, 0))],
            out_specs=pl.BlockSpec((None, th, W), lambda q, i, c_ref: (q, i, 0))),
        out_shape=SDS((A, h, W), BF16),
        compiler_params=_params("parallel", "parallel"),
    )(c_arr, own, got)


REDUCE_STEPS = 2


def chip_reduce(sums, got, kinds, pc_arr, *, name, ride=None):
    n = len(sums)
    mine = {"row": lambda i, pc_ref: (pc_ref[0], i, 0), "col": lambda i, pc_ref: (0, i, pc_ref[0]),
            "split": lambda i, pc_ref: (pc_ref[0] // 2, i, pc_ref[0] % 2)}
    a_specs, b_specs, o_specs, out_shape = [], [], [], []
    for g, kind in zip(got, kinds):
        _, h, ws = g.shape
        th = h // REDUCE_STEPS
        assert th % BF16_ROWS == 0, g.shape
        a_specs.append(pl.BlockSpec((None, th, ws), mine[kind]))
        b_specs.append(pl.BlockSpec((3, th, ws), lambda i, pc_ref: (0, i, 0)))
        o_specs.append(pl.BlockSpec((None, th, ws), lambda i, pc_ref: (pc_ref[1], i, 0)))
        out_shape.append(SDS((2, h, ws), F32))

    def body(pc_ref, *refs):
        for a_ref, b_ref, o_ref in zip(refs[:n], refs[n:2 * n], refs[2 * n:]):
            o_ref[...] = ((a_ref[...].astype(F32) + b_ref[0].astype(F32)) + b_ref[1].astype(F32)) + b_ref[2].astype(F32)

    return _call(body, name=name, grid=(REDUCE_STEPS,), in_specs=a_specs + b_specs, out_specs=o_specs,
                 out_shape=out_shape, semantics=("parallel",), args=list(sums) + list(got), prefetch=pc_arr, ride=ride)


def small_reduce(blocks, me_arr):
    _, rows, D = blocks.shape

    def body(me_ref, b_ref, o_ref):
        me = me_ref[0]
        total = b_ref[me]
        for d in range(1, 8):
            total = total + b_ref[d ^ me]
        o_ref[...] = total

    return pl.pallas_call(
        body, name="small_reduce",
        grid_spec=pltpu.PrefetchScalarGridSpec(
            num_scalar_prefetch=1, grid=(1,),
            in_specs=[pl.BlockSpec((8, rows, D), lambda i, me_ref: (0, 0, 0))],
            out_specs=pl.BlockSpec((rows, D), lambda i, me_ref: (0, 0))),
        out_shape=SDS((rows, D), F32),
        compiler_params=_params("arbitrary"),
    )(me_arr, blocks)


def _adam(w, g, m, v):
    m_new = ADAM_B1 * m + (1.0 - ADAM_B1) * g
    v_new = ADAM_B2 * v + (1.0 - ADAM_B2) * (g * g)
    m_hat = m_new / (1.0 - ADAM_B1 ** ADAM_STEP)
    v_hat = v_new / (1.0 - ADAM_B2 ** ADAM_STEP)
    return -ADAM_LR * (m_hat / (jnp.sqrt(v_hat) + ADAM_EPS) + ADAM_WD * w), m_new, v_new


def adamw_rows(block, p_arr, leaves, *, name):
    L = len(leaves)

    def body(p_ref, b_ref, *refs):
        outs = refs[3 * L:]
        for i, (w, _, _, row, sharded) in enumerate(leaves):
            n, width = w.shape[-2:]
            cols = pl.ds(pl.multiple_of(p_ref[0] * width, 128), width) if sharded else slice(0, width)
            g = b_ref[row:row + n, cols].reshape(w.shape)
            results = (g,) + _adam(refs[i][...], g, refs[L + i][...], refs[2 * L + i][...])
            for o_ref, value in zip(outs[4 * i:4 * i + 4], results):
                o_ref[...] = value

    whole = lambda a: pl.BlockSpec(a.shape, lambda i, p_ref, nd=len(a.shape): (0,) * nd)
    arrays = [leaf[k] for k in range(3) for leaf in leaves]
    shapes = [SDS(leaf[0].shape, F32) for leaf in leaves for _ in range(4)]
    outs = pl.pallas_call(
        body, name=name,
        grid_spec=pltpu.PrefetchScalarGridSpec(
            num_scalar_prefetch=1, grid=(1,), in_specs=[whole(block)] + [whole(a) for a in arrays],
            out_specs=[whole(s) for s in shapes]),
        out_shape=shapes, compiler_params=_params("arbitrary"),
    )(p_arr, block, *arrays)
    return [outs[4 * i:4 * i + 4] for i in range(L)]


def adamw(w, gs, m, v, *, name):
    L, r, cols = w.shape
    tr = _row_tile(r, 256)
    nt = r // tr

    def body(*refs):
        w_ref, m_ref, v_ref = refs[:3]
        g_refs = refs[3:3 + L]
        g_out, d_out, m_out, v_out = refs[3 + L:]
        layer = pl.program_id(0)
        g = g_refs[0][...]
        for l in range(1, L):
            g = jnp.where(layer == l, g_refs[l][...], g)
        g_out[...] = g
        d_out[...], m_out[...], v_out[...] = _adam(w_ref[...], g, m_ref[...], v_ref[...])

    full = pl.BlockSpec((None, tr, cols), lambda l, i: (l, i, 0))
    g_spec = lambda l0: pl.BlockSpec((tr, cols), lambda l, i: (jnp.where(l == l0, i, jnp.where(l < l0, 0, nt - 1)), 0))
    return pl.pallas_call(
        body, name=name, grid=(L, nt),
        in_specs=[full, full, full] + [g_spec(l0) for l0 in range(L)],
        out_specs=[full] * 4,
        out_shape=[SDS(w.shape, F32)] * 4,
        compiler_params=_params("arbitrary", "arbitrary"),
    )(w, m, v, *gs)


def _rms_r(xf):
    return lax.rsqrt(jnp.mean(xf * xf, axis=-1, keepdims=True) + EPS)


def _rmsnorm_bwd(xf, g, dy):
    r = _rms_r(xf)
    xh = xf * r
    gd = g * dy
    return r * (gd - xh * jnp.mean(xh * gd, axis=-1, keepdims=True)), xh


def _dot(a, b):
    return jnp.dot(a, b, preferred_element_type=F32)


def _dot_nt(a, b):
    return lax.dot_general(a, b, (((1,), (1,)), ((), ())), preferred_element_type=F32)


def _dot_tn(a, b):
    return lax.dot_general(a, b, (((0,), (0,)), ((), ())), preferred_element_type=F32)


def _accumulate(ref, first, value):
    @pl.when(first)
    def _():
        ref[...] = value

    @pl.when(jnp.logical_not(first))
    def _():
        ref[...] += value


def norm_matmul(x, g, w, *, tn, split, name, ride=None, tm=ROW_TILE):
    T, D = x.shape
    N = w.shape[1]
    per = N // split // tn

    def body(x_ref, g_ref, w_ref, o_ref, xn_ref):
        @pl.when(pl.program_id(1) == 0)
        def _():
            xf = x_ref[...].astype(F32)
            xn_ref[...] = (xf * _rms_r(xf) * g_ref[...]).astype(BF16)

        o_ref[...] = _dot(xn_ref[...], w_ref[...]).astype(BF16)

    return _call(
        body, name=name, grid=(T // tm, N // tn),
        in_specs=[pl.BlockSpec((tm, D), lambda i, j: (i, 0)),
                  pl.BlockSpec((1, D), lambda i, j: (0, 0)),
                  pl.BlockSpec((D, tn), lambda i, j: (0, j))],
        out_specs=[pl.BlockSpec((None, tm, tn), lambda i, j: (j // per, i, j % per)),
                   pl.BlockSpec((tm, D), lambda i, j: (i, 0))],
        out_shape=[SDS((split, T, N // split), BF16), SDS((T, D), BF16)],
        semantics=("parallel", "arbitrary"), args=(x, g, w), ride=ride)


BIG_ROW_TILE = 1024


def norm2_matmul(x, gains, weights, *, name, ride=None, tm=BIG_ROW_TILE):
    T, D = x.shape
    tm = min(tm, T)
    n = len(gains)

    def body(x_ref, *refs):
        subs = _sub_tiles(tm)
        xhs = []
        for rows in subs:
            xf = x_ref[rows, :].astype(F32)
            xhs.append(xf * _rms_r(xf))
        for g_ref, w_ref, o_ref, xn_ref in zip(refs[:n], refs[n:2 * n], refs[2 * n::2], refs[2 * n + 1::2]):
            for rows, xh in zip(subs, xhs):
                xn = (xh * g_ref[...]).astype(BF16)
                xn_ref[rows, :] = xn
                o_ref[rows, :] = _dot(xn, w_ref[...]).astype(BF16)

    row = pl.BlockSpec((tm, D), lambda i: (i, 0))
    vec = pl.BlockSpec((1, D), lambda i: (0, 0))
    out_specs, out_shape = [], []
    for w in weights:
        out_specs += [pl.BlockSpec((tm, w.shape[1]), lambda i: (i, 0)), row]
        out_shape += [SDS((T, w.shape[1]), BF16), SDS((T, D), BF16)]
    return _call(
        body, name=name, grid=(T // tm,),
        in_specs=[row] + [vec] * n + [pl.BlockSpec(w.shape, lambda i: (0, 0)) for w in weights],
        out_specs=out_specs, out_shape=out_shape, semantics=("parallel",), args=[x] + list(gains) + list(weights),
        ride=ride)


def _shift_down(prev, cur, by):
    big = jnp.concatenate([prev, cur], axis=0)
    return pltpu.roll(big, by, 0)[prev.shape[0]:]


def _shift_up(cur, nxt, by):
    big = jnp.concatenate([cur, nxt], axis=0)
    return pltpu.roll(big, big.shape[0] - by, 0)[:cur.shape[0]]


def conv_mix_out(bcx, conv_w, w_out, g_post, res, *, name, ride=None, tm=ROW_TILE):
    T, D = res.shape
    hb = tm // BF16_ROWS

    def body(b_ref, c_ref, u_ref, cp_ref, up_ref, cw_ref, w_ref, g_ref, r_ref, h_ref, z_ref, y_ref):
        i = pl.program_id(0)
        cu = c_ref[...].astype(F32) * u_ref[...].astype(F32)
        cup = cp_ref[...].astype(F32) * up_ref[...].astype(F32)
        cup = jnp.where(i == 0, 0.0, cup)
        cv = (cw_ref[0:1, :] * _shift_down(cup, cu, 2) + cw_ref[1:2, :] * _shift_down(cup, cu, 1)
              + cw_ref[2:3, :] * cu)
        y = (b_ref[...].astype(F32) * cv).astype(BF16)
        y_ref[...] = y
        z = _dot(y, w_ref[...])
        z_ref[...] = z.astype(BF16)
        h_ref[...] = (r_ref[...] + z * _rms_r(z) * g_ref[...]).astype(STREAM)

    tile = lambda col: pl.BlockSpec((tm, D), lambda i: (i, col))
    halo = lambda col: pl.BlockSpec((BF16_ROWS, D), lambda i: (jnp.maximum(i * hb - 1, 0), col))
    row = pl.BlockSpec((tm, D), lambda i: (i, 0))
    return _call(
        body, name=name, grid=(T // tm,),
        in_specs=[tile(0), tile(1), tile(2), halo(1), halo(2),
                  pl.BlockSpec((3, D), lambda i: (0, 0)),
                  pl.BlockSpec((D, D), lambda i: (0, 0)),
                  pl.BlockSpec((1, D), lambda i: (0, 0)), row],
        out_specs=[row, row, row],
        out_shape=[SDS((T, D), STREAM), SDS((T, D), BF16), SDS((T, D), BF16)],
        semantics=("parallel",), args=(bcx, bcx, bcx, bcx, bcx, conv_w, w_out, g_post, res), ride=ride)


def _normbwd_then_nt(dh, zf, g_ref, w_ref, dz_ref, dg_ref, o_ref, first):
    dz, zh = _rmsnorm_bwd(zf, g_ref[...], dh)
    dz = dz.astype(BF16)
    dz_ref[...] = dz
    _accumulate(dg_ref, first, jnp.sum(dh * zh, axis=0, keepdims=True))
    o_ref[...] = _dot_nt(dz, w_ref[...]).astype(BF16)


def _then_specs(then, tm, T, D):
    z, g, w = then
    K = w.shape[0]
    row = pl.BlockSpec((tm, D), lambda i: (i, 0))
    vec = pl.BlockSpec((1, D), lambda i: (0, 0))
    in_specs = [row, vec, pl.BlockSpec((K, D), lambda i: (0, 0), pipeline_mode=pl.Buffered(1))]
    out_specs = [row, vec, pl.BlockSpec((tm, K), lambda i: (i, 0))]
    out_shape = [SDS((T, D), BF16), SDS((1, D), F32), SDS((T, K), BF16)]
    return in_specs, out_specs, out_shape


def plain_mix_out(a, w, g_post, res, *, name, target=None, ride=None, tm=ROW_TILE):
    T, D = res.shape
    tm = min(tm, T)
    K = a.shape[1]
    with_loss = target is not None

    def body(a_ref, w_ref, g_ref, r_ref, *rest):
        subs = _sub_tiles(tm)
        zs = [_dot(a_ref[rows, :], w_ref[...]) for rows in subs]
        if not with_loss:
            h_ref, z_ref = rest
            for rows, z in zip(subs, zs):
                h_ref[rows, :] = (r_ref[rows, :].astype(F32) + z * _rms_r(z) * g_ref[...]).astype(STREAM)
                z_ref[rows, :] = z.astype(BF16)
            return
        t_ref, h_ref, dz_ref, dg_ref, da_ref, loss_ref = rest
        first = pl.program_id(0) == 0
        loss, dg = jnp.zeros((), F32), jnp.zeros((1, D), F32)
        for rows, z in zip(subs, zs):
            diff = r_ref[rows, :].astype(F32) + z * _rms_r(z) * g_ref[...] - t_ref[rows, :]
            dh = diff * (1.0 / D)
            h_ref[rows, :] = dh.astype(STREAM)
            loss = loss + jnp.sum(diff * diff)
            dz, zh = _rmsnorm_bwd(z, g_ref[...], dh)
            dz = dz.astype(BF16)
            dz_ref[rows, :] = dz
            dg = dg + jnp.sum(dh * zh, axis=0, keepdims=True)
            da_ref[rows, :] = _dot_nt(dz, w_ref[...]).astype(BF16)
        _accumulate(loss_ref, first, jnp.full(loss_ref.shape, 0.5 / D, F32) * loss)
        _accumulate(dg_ref, first, dg)

    row = pl.BlockSpec((tm, D), lambda i: (i, 0))
    vec = pl.BlockSpec((1, D), lambda i: (0, 0))
    in_specs = [pl.BlockSpec((tm, K), lambda i: (i, 0)), pl.BlockSpec((K, D), lambda i: (0, 0)), vec, row]
    if with_loss:
        in_specs.append(row)
        out_specs = [row, row, vec, pl.BlockSpec((tm, K), lambda i: (i, 0)), pl.BlockSpec((8, 128), lambda i: (0, 0))]
        out_shape = [SDS((T, D), STREAM), SDS((T, D), BF16), SDS((1, D), F32), SDS((T, K), BF16), SDS((8, 128), F32)]
    else:
        out_specs, out_shape = [row, row], [SDS((T, D), STREAM), SDS((T, D), BF16)]
    return _call(
        body, name=name, grid=(T // tm,), in_specs=in_specs, out_specs=out_specs, out_shape=out_shape,
        semantics=("arbitrary",), args=(a, w, g_post, res) + ((target,) if with_loss else ()), ride=ride)


def _silu_grads(d, g, u):
    sg = jax.nn.sigmoid(g)
    return d * u * (sg * (1.0 + g * (1.0 - sg))), d * (g * sg)


def _sub_tiles(tm):
    return [pl.ds(k, min(MXU_WIDTH, tm)) for k in range(0, tm, MXU_WIDTH)]


def norm_swiglu_in(x, g, w, *, name, ride=None, tm=ROW_TILE):
    T, D = x.shape
    F = w.shape[1] // 2

    def body(x_ref, g_ref, wg_ref, wu_ref, gu_ref, a_ref, xt_ref):
        subs = _sub_tiles(tm)
        xns = []
        for rows in subs:
            xf = x_ref[rows, :].astype(F32)
            xns.append(xf * _rms_r(xf) * g_ref[...])
        xbs = [xn.astype(BF16) for xn in xns]
        gates = [_dot(xb, wg_ref[...]).astype(BF16) for xb in xbs]
        ups = [_dot(xb, wu_ref[...]).astype(BF16) for xb in xbs]
        for rows, gate, up in zip(subs, gates, ups):
            gu_ref[0, rows, :] = gate
            gu_ref[1, rows, :] = up
            a_ref[rows, :] = gate * jax.nn.sigmoid(gate) * up
        for rows, xn in zip(subs, xns):
            xt_ref[:, rows] = xn.T.astype(BF16)

    half = lambda s: pl.BlockSpec((D, F), lambda i: (0, s), pipeline_mode=pl.Buffered(1))
    return _call(
        body, name=name, grid=(T // tm,),
        in_specs=[pl.BlockSpec((tm, D), lambda i: (i, 0)), pl.BlockSpec((1, D), lambda i: (0, 0)), half(0), half(1)],
        out_specs=[pl.BlockSpec((2, tm, F), lambda i: (0, i, 0)), pl.BlockSpec((tm, F), lambda i: (i, 0)),
                   pl.BlockSpec((D, tm), lambda i: (0, i))],
        out_shape=[SDS((2, T, F), BF16), SDS((T, F), BF16), SDS((D, T), BF16)],
        semantics=("parallel",), args=(x, g, w, w), ride=ride)


def swiglu_bwd_tn(xt, dact, gu, *, name, ride=None, tb=MXU_WIDTH):
    D, T = xt.shape
    F = dact.shape[1]

    def body(xt_ref, d_ref, g_ref, u_ref, o_ref):
        dg, du = _silu_grads(d_ref[...], g_ref[...], u_ref[...])
        o_ref[0] = _dot(xt_ref[...], dg).astype(BF16)
        o_ref[1] = _dot(xt_ref[...], du).astype(BF16)

    col = lambda s: pl.BlockSpec((None, T, tb), lambda j: (s, 0, j))
    out = _call(
        body, name=name, grid=(F // tb,),
        in_specs=[pl.BlockSpec((D, T), lambda j: (0, 0), pipeline_mode=pl.Buffered(1)),
                  pl.BlockSpec((T, tb), lambda j: (0, j)), col(0), col(1)],
        out_specs=[pl.BlockSpec((2, D, tb), lambda j: (0, 0, j))],
        out_shape=[SDS((2, D, F), BF16)],
        semantics=("parallel",), args=(xt, dact, gu, gu), ride=ride)
    return out[0] if ride is None else (out[0][0], out[1])


def swiglu_bwd_in(dact, gu, w, h_in, g, dh_out, then, *, name, ride=None, tm=ROW_TILE):
    T, D = h_in.shape
    F = dact.shape[1]

    def body(d_ref, gg_ref, uu_ref, wg_ref, wu_ref, h_ref, g_ref, dh_ref, z_ref, g2_ref, w2_ref,
             o_ref, dg_ref, dz_ref, dg2_ref, da_ref):
        first = pl.program_id(0) == 0
        subs = _sub_tiles(tm)
        dns = []
        for rows in subs:
            dgate, dup = _silu_grads(d_ref[rows, :], gg_ref[rows, :], uu_ref[rows, :])
            dns.append(_dot_nt(dgate, wg_ref[...]) + _dot_nt(dup, wu_ref[...]))
        dg, dg2 = jnp.zeros((1, D), F32), jnp.zeros((1, D), F32)
        for rows, dn in zip(subs, dns):
            dx, hh = _rmsnorm_bwd(h_ref[rows, :].astype(F32), g_ref[...], dn)
            dh_in = dh_ref[rows, :] + dx
            o_ref[rows, :] = dh_in.astype(STREAM)
            dg = dg + jnp.sum(dn * hh, axis=0, keepdims=True)
            dz, zh = _rmsnorm_bwd(z_ref[rows, :].astype(F32), g2_ref[...], dh_in)
            dz = dz.astype(BF16)
            dz_ref[rows, :] = dz
            dg2 = dg2 + jnp.sum(dh_in * zh, axis=0, keepdims=True)
            da_ref[rows, :] = _dot_nt(dz, w2_ref[...]).astype(BF16)
        _accumulate(dg_ref, first, dg)
        _accumulate(dg2_ref, first, dg2)

    row = pl.BlockSpec((tm, D), lambda i: (i, 0))
    vec = pl.BlockSpec((1, D), lambda i: (0, 0))
    part = lambda s: pl.BlockSpec((None, tm, F), lambda i: (s, i, 0))
    half = lambda s: pl.BlockSpec((D, F), lambda i: (0, s), pipeline_mode=pl.Buffered(1))
    then_in, then_out, then_shape = _then_specs(then, tm, T, D)
    return _call(
        body, name=name, grid=(T // tm,),
        in_specs=[pl.BlockSpec((tm, F), lambda i: (i, 0)), part(0), part(1), half(0), half(1), row, vec, row] + then_in,
        out_specs=[row, vec] + then_out,
        out_shape=[SDS((T, D), STREAM), SDS((1, D), F32)] + then_shape,
        semantics=("arbitrary",), args=(dact, gu, gu, w, w, h_in, g, dh_out) + tuple(then), ride=ride)


def rope_tables(T):
    half = ROT_DIM // 2
    inv_freq = ROPE_THETA ** (-jnp.arange(0, ROT_DIM, 2, dtype=F32) / ROT_DIM)
    ang = (jnp.arange(T, dtype=F32)[:, None] * inv_freq[None, :]).T
    cos, sin = jnp.cos(ang), jnp.sin(ang)
    rest = HEAD_DIM - ROT_DIM
    one, zero = jnp.ones((rest, T), F32), jnp.zeros((rest, T), F32)
    zh = jnp.zeros((half, T), F32)
    fac = jnp.concatenate([cos, cos, one], axis=0)
    up = jnp.concatenate([-sin, zh, zero], axis=0)
    down = jnp.concatenate([zh, sin, zero], axis=0)
    return jnp.stack([fac, up, down])


def _rope(t, tab):
    half = ROT_DIM // 2
    return t * tab[0] + pltpu.roll(t, HEAD_DIM - half, 0) * tab[1] + pltpu.roll(t, half, 0) * tab[2]


def _rope_t(d, tab):
    half = ROT_DIM // 2
    return d * tab[0] + pltpu.roll(d * tab[1], half, 0) + pltpu.roll(d * tab[2], HEAD_DIM - half, 0)


def _head(t, h):
    return t[h * HEAD_DIM:(h + 1) * HEAD_DIM]


def _band(n, group):
    kj = lax.broadcasted_iota(jnp.int32, (2 * BLOCK, BLOCK), 0)
    qi = lax.broadcasted_iota(jnp.int32, (2 * BLOCK, BLOCK), 1)
    mask = (kj > qi) & (kj <= qi + BLOCK) & ((n > 0) | (kj >= BLOCK))
    return jnp.tile(mask, (1, group))


def _attn_specs(D, kvd, nb):
    cur = lambda n: jnp.minimum(n, nb - 1)
    prev = lambda n: jnp.maximum(cur(n) - 1, 0)
    return [pl.BlockSpec((BLOCK, D), lambda n: (cur(n), 0)),
            pl.BlockSpec((BLOCK, kvd), lambda n: (prev(n), 0)),
            pl.BlockSpec((BLOCK, kvd), lambda n: (cur(n), 0)),
            pl.BlockSpec((BLOCK, kvd), lambda n: (prev(n), 1)),
            pl.BlockSpec((BLOCK, kvd), lambda n: (cur(n), 1)),
            pl.BlockSpec((3, HEAD_DIM, BLOCK), lambda n: (0, 0, prev(n))),
            pl.BlockSpec((3, HEAD_DIM, BLOCK), lambda n: (0, 0, cur(n))),
            pl.BlockSpec(memory_space=pltpu.SMEM)]


def _attn_operands(q_ref, kp_ref, k_ref, vp_ref, v_ref, tp_ref, t_ref):
    flip = lambda ref: ref[...].astype(F32).T
    tab = t_ref[...]
    kt = jnp.concatenate([flip(kp_ref), flip(k_ref)], axis=1)
    vt = jnp.concatenate([flip(vp_ref), flip(v_ref)], axis=1)
    return flip(q_ref), kt, vt, tab, jnp.concatenate([tp_ref[...], tab], axis=2)


SCORE_SCALE = 1.0 / math.sqrt(HEAD_DIM)
HEADS_TOGETHER = 4


def _group_heads(t, first, count, tab=None):
    heads = [_head(t, first + g) for g in range(count)]
    if tab is not None:
        heads = [_rope(h, tab) * SCORE_SCALE for h in heads]
    return jnp.concatenate(heads, axis=1).astype(BF16)


def _sink_row(s_ref, first, count):
    which = lax.broadcasted_iota(jnp.int32, (1, count * BLOCK), 1) // BLOCK
    row = jnp.zeros((1, count * BLOCK), F32)
    for g in range(count):
        row = jnp.where(which == g, s_ref[0, first + g], row)
    return row


def _sum_keys(t):
    return _dot(jnp.ones((8, t.shape[0]), BF16), t)[0:1]


def _softmax(scores, sink, mask):
    s = jnp.where(mask, scores.astype(BF16), NEG)
    m = jnp.maximum(jnp.max(s, axis=0, keepdims=True).astype(F32), sink).astype(BF16)
    e = jnp.exp(s - m)
    m = m.astype(F32)
    return e, m, 1.0 / (_sum_keys(e) + jnp.exp(sink - m))


def _per_head(row, count):
    return [row[:, g * BLOCK:(g + 1) * BLOCK] for g in range(count)]


def attention_fwd(q, kv, tabs, sinks, *, name, ride=None):
    T, D = q.shape
    kvd = kv.shape[1] // 2
    heads = D // HEAD_DIM
    group = heads // N_KV_HEADS

    def body(q_ref, kp_ref, k_ref, vp_ref, v_ref, tp_ref, t_ref, s_ref, o_ref, stat_ref):
        gs = HEADS_TOGETHER
        mask = _band(pl.program_id(0), gs)
        qt, kt, vt, tab, tab2 = _attn_operands(q_ref, kp_ref, k_ref, vp_ref, v_ref, tp_ref, t_ref)
        firsts = [(j, first) for j in range(N_KV_HEADS) for first in range(j * group, (j + 1) * group, gs)]
        ks = [_rope(_head(kt, j), tab2).astype(BF16) for j in range(N_KV_HEADS)]
        scores = [_dot_tn(ks[j], _group_heads(qt, first, gs, tab)) for j, first in firsts]
        soft = [_softmax(s, _sink_row(s_ref, first, gs), mask) for s, (j, first) in zip(scores, firsts)]
        outs, ms, invs = [], [], []
        for (e, m, inv), (j, first) in zip(soft, firsts):
            o = _dot(_head(vt, j).astype(BF16), e) * inv
            outs += [o[:, g * BLOCK:(g + 1) * BLOCK] for g in range(gs)]
            ms += _per_head(m, gs)
            invs += _per_head(inv, gs)
        o_ref[...] = jnp.concatenate(outs, axis=0).T.astype(BF16)
        stat_ref[0] = jnp.concatenate(ms, axis=0)
        stat_ref[1] = jnp.concatenate(invs, axis=0)

    return _call(
        body, name=name, grid=(T // BLOCK,),
        in_specs=_attn_specs(D, kvd, T // BLOCK),
        out_specs=[pl.BlockSpec((BLOCK, D), lambda n: (n, 0)), pl.BlockSpec((2, heads, BLOCK), lambda n: (0, 0, n))],
        out_shape=[SDS((T, D), BF16), SDS((2, heads, T), F32)],
        semantics=("parallel",), args=(q, kv, kv, kv, kv, tabs, tabs, sinks), ride=ride)


def attention_bwd(q, kv, tabs, sinks, do, o, stats, *, name, ride=None):
    T, D = q.shape
    kvd = kv.shape[1] // 2
    heads = D // HEAD_DIM
    group = heads // N_KV_HEADS
    nb = T // BLOCK

    def body(q_ref, kp_ref, k_ref, vp_ref, v_ref, tp_ref, t_ref, s_ref, do_ref, o_ref, stat_ref,
             dq_ref, dkv_ref, ds_ref, carry):
        n = pl.program_id(0)

        @pl.when(n == 0)
        def _():
            carry[...] = jnp.zeros_like(carry)

        @pl.when(n < nb)
        def _():
            block(n, q_ref, kp_ref, k_ref, vp_ref, v_ref, tp_ref, t_ref, s_ref, do_ref, o_ref, stat_ref,
                  dq_ref, dkv_ref, ds_ref, carry)

        @pl.when(n == nb)
        def _():
            dkv_ref[...] = carry[...].astype(BF16)

    def block(n, q_ref, kp_ref, k_ref, vp_ref, v_ref, tp_ref, t_ref, s_ref, do_ref, o_ref, stat_ref,
              dq_ref, dkv_ref, ds_ref, carry):
        gs = HEADS_TOGETHER
        mask = _band(n, gs)
        qt, kt, vt, tab, tab2 = _attn_operands(q_ref, kp_ref, k_ref, vp_ref, v_ref, tp_ref, t_ref)
        dot = do_ref[...].astype(F32).T
        odo = o_ref[...].astype(F32).T * dot
        dl_all = jnp.concatenate([jnp.sum(_head(odo, h), axis=0, keepdims=True) for h in range(heads)], axis=0)
        m_all, inv_all = stat_ref[0], stat_ref[1]
        row = lambda t, first: jnp.concatenate([t[first + g:first + g + 1] for g in range(gs)], axis=1)
        lane = lax.broadcasted_iota(jnp.int32, (8, 128), 1)
        dsink = jnp.zeros((8, 128), F32)
        firsts = [(j, first) for j in range(N_KV_HEADS) for first in range(j * group, (j + 1) * group, gs)]
        ks = [_rope(_head(kt, j), tab2).astype(BF16) for j in range(N_KV_HEADS)]
        vs = [_head(vt, j).astype(BF16) for j in range(N_KV_HEADS)]
        qs = [_group_heads(qt, first, gs, tab) for _, first in firsts]
        dos = [_group_heads(dot, first, gs) for _, first in firsts]
        scores = [_dot_tn(ks[j], q) for q, (j, _) in zip(qs, firsts)]
        dps = [_dot_tn(vs[j], do) for do, (j, _) in zip(dos, firsts)]
        ps, dscs = [], []
        for s, dp, (j, first) in zip(scores, dps, firsts):
            m, inv, dl = row(m_all, first), row(inv_all, first), row(dl_all, first)
            e = jnp.exp(jnp.where(mask, s.astype(BF16), NEG) - m.astype(BF16))
            p = e * inv.astype(BF16)
            dscs.append(p * (dp.astype(BF16) - dl.astype(BF16)))
            ps.append(p)
            weight = jnp.exp(_sink_row(s_ref, first, gs) - m) * inv * dl
            for g in range(gs):
                dsink = dsink - jnp.where(lane == first + g, jnp.sum(weight[:, g * BLOCK:(g + 1) * BLOCK]), 0.0)
        dqs = []
        dks = [jnp.zeros((HEAD_DIM, 2 * BLOCK), F32) for _ in range(N_KV_HEADS)]
        dvs = [jnp.zeros((HEAD_DIM, 2 * BLOCK), F32) for _ in range(N_KV_HEADS)]
        for p, dsc, q, do, (j, _) in zip(ps, dscs, qs, dos, firsts):
            dq = _dot(ks[j], dsc) * SCORE_SCALE
            dqs += [_rope_t(dq[:, g * BLOCK:(g + 1) * BLOCK], tab) for g in range(gs)]
            dks[j] = dks[j] + _dot_nt(q, dsc)
            dvs[j] = dvs[j] + _dot_nt(do, p)
        dks = [_rope_t(dk, tab2) for dk in dks]
        dq_ref[...] = jnp.concatenate(dqs, axis=0).T.astype(BF16)
        dkv = jnp.concatenate(dks + dvs, axis=0)
        dkv_ref[...] = (carry[...] + dkv[:, :BLOCK].T).astype(BF16)
        carry[...] = dkv[:, BLOCK:].T
        _accumulate(ds_ref, n == 0, dsink)

    cur = lambda n: jnp.minimum(n, nb - 1)
    blk = lambda w: pl.BlockSpec((BLOCK, w), lambda n: (cur(n), 0))
    return _call(
        body, name=name, grid=(nb + 1,),
        in_specs=_attn_specs(D, kvd, nb) + [blk(D), blk(D), pl.BlockSpec((2, heads, BLOCK), lambda n: (0, 0, cur(n)))],
        out_specs=[blk(D), pl.BlockSpec((BLOCK, 2 * kvd), lambda n: (jnp.maximum(n - 1, 0), 0)),
                   pl.BlockSpec((8, 128), lambda n: (0, 0))],
        out_shape=[SDS((T, D), BF16), SDS((T, 2 * kvd), BF16), SDS((8, 128), F32)],
        scratch_shapes=[pltpu.VMEM((BLOCK, 2 * kvd), F32)],
        semantics=("arbitrary",), args=(q, kv, kv, kv, kv, tabs, tabs, sinks, do, o, stats), ride=ride)


def matmul_nt_normbwd(da, w, h_in, g, dh_out, *, name, ride=None, tm=ROW_TILE):
    T, D = h_in.shape
    S, _, K = da.shape

    def body(*refs):
        da_refs, w_refs = refs[:S], refs[S:2 * S]
        h_ref, g_ref, dh_ref, o_ref, dg_ref = refs[2 * S:]
        subs = _sub_tiles(tm)
        dns = []
        for rows in subs:
            dn = _dot_nt(da_refs[0][rows, :], w_refs[0][...])
            for s in range(1, S):
                dn = dn + _dot_nt(da_refs[s][rows, :], w_refs[s][...])
            dns.append(dn)
        dg = jnp.zeros((1, D), F32)
        for rows, dn in zip(subs, dns):
            dx, hh = _rmsnorm_bwd(h_ref[rows, :].astype(F32), g_ref[...], dn)
            o_ref[rows, :] = dh_ref[rows, :] + dx
            dg = dg + jnp.sum(dn * hh, axis=0, keepdims=True)
        _accumulate(dg_ref, pl.program_id(0) == 0, dg)

    row = pl.BlockSpec((tm, D), lambda i: (i, 0))
    vec = pl.BlockSpec((1, D), lambda i: (0, 0))
    part = lambda s: pl.BlockSpec((None, tm, K), lambda i: (s, i, 0))
    cols = lambda s: pl.BlockSpec((D, K), lambda i: (0, s), pipeline_mode=pl.Buffered(1))
    return _call(
        body, name=name, grid=(T // tm,),
        in_specs=[part(s) for s in range(S)] + [cols(s) for s in range(S)] + [row, vec, row],
        out_specs=[row, vec],
        out_shape=[SDS((T, D), F32), SDS((1, D), F32)],
        semantics=("arbitrary",), args=[da] * S + [w] * S + [h_in, g, dh_out], ride=ride)


def matmuls_nt_normbwd(das, ws, h_in, gs, dh_out, then, *, name, ride=None, tm=ROW_TILE):
    T, D = h_in.shape
    tm = min(tm, T)
    n = len(das)

    def body(*refs):
        da_refs, w_refs, g_refs = refs[:n], refs[n:2 * n], refs[2 * n:3 * n]
        h_ref, dh_ref, z_ref, g2_ref, w2_ref, o_ref = refs[3 * n:3 * n + 6]
        dg_refs, (dz_ref, dg2_ref, da_ref) = refs[3 * n + 6:4 * n + 6], refs[4 * n + 6:]
        first = pl.program_id(0) == 0
        subs = _sub_tiles(tm)
        dns = [[_dot_nt(da_ref_[rows, :], w_ref[...]) for da_ref_, w_ref in zip(da_refs, w_refs)] for rows in subs]
        dgs, dg2 = [jnp.zeros((1, D), F32) for _ in range(n)], jnp.zeros((1, D), F32)
        for rows, dn_sub in zip(subs, dns):
            hf = h_ref[rows, :].astype(F32)
            r = _rms_r(hf)
            hh = hf * r
            total = dh_ref[rows, :].astype(F32)
            for b, (dn, g_ref) in enumerate(zip(dn_sub, g_refs)):
                gd = g_ref[...] * dn
                total = total + r * (gd - hh * jnp.mean(hh * gd, axis=-1, keepdims=True))
                dgs[b] = dgs[b] + jnp.sum(dn * hh, axis=0, keepdims=True)
            o_ref[rows, :] = total.astype(STREAM)
            dz, zh = _rmsnorm_bwd(z_ref[rows, :].astype(F32), g2_ref[...], total)
            dz = dz.astype(BF16)
            dz_ref[rows, :] = dz
            dg2 = dg2 + jnp.sum(total * zh, axis=0, keepdims=True)
            da_ref[rows, :] = _dot_nt(dz, w2_ref[...]).astype(BF16)
        for dg_ref, dg in zip(dg_refs + (dg2_ref,), dgs + [dg2]):
            _accumulate(dg_ref, first, dg)

    row = pl.BlockSpec((tm, D), lambda i: (i, 0))
    vec = pl.BlockSpec((1, D), lambda i: (0, 0))
    then_in, then_out, then_shape = _then_specs(then, tm, T, D)
    return _call(
        body, name=name, grid=(T // tm,),
        in_specs=[pl.BlockSpec((tm, da.shape[1]), lambda i: (i, 0)) for da in das]
        + [pl.BlockSpec(w.shape, lambda i: (0, 0)) for w in ws] + [vec] * n + [row, row] + then_in,
        out_specs=[row] + [vec] * n + then_out,
        out_shape=[SDS((T, D), STREAM)] + [SDS((1, D), F32)] * n + then_shape,
        semantics=("arbitrary",), args=list(das) + list(ws) + list(gs) + [h_in, dh_out] + list(then), ride=ride)


def matmul_tn(a, b, *, tb, name, ride=None, ta=MXU_WIDTH):
    T, Ka = a.shape
    S, _, Nb = b.shape
    per = Nb // tb

    def body(a_ref, b_ref, o_ref):
        o_ref[...] = _dot_tn(a_ref[...], b_ref[...]).astype(BF16)

    out = _call(
        body, name=name, grid=(S * per, Ka // ta),
        in_specs=[pl.BlockSpec((T, ta), lambda j, i: (0, i)),
                  pl.BlockSpec((None, T, tb), lambda j, i: (j // per, 0, j % per))],
        out_specs=[pl.BlockSpec((ta, tb), lambda j, i: (i, j))],
        out_shape=[SDS((Ka, S * Nb), BF16)],
        semantics=("parallel", "parallel"), args=(a, b), ride=ride)
    return out[0] if ride is None else (out[0][0], out[1])


def conv_bwd(dy, bcx, conv_w, *, name, ride=None, tm=ROW_TILE):
    T, D = dy.shape
    nt = T // tm
    hb = tm // BF16_ROWS
    last = T // BF16_ROWS - 1

    def body(dy_ref, dyn_ref, b_ref, bn_ref, c_ref, u_ref, cp_ref, up_ref, cw_ref, o_ref, dw_ref):
        i = pl.program_id(0)
        c, u = c_ref[...].astype(F32), u_ref[...].astype(F32)
        cu = c * u
        cup = jnp.where(i == 0, 0.0, cp_ref[...].astype(F32) * up_ref[...].astype(F32))
        cu1, cu2 = _shift_down(cup, cu, 1), _shift_down(cup, cu, 2)
        w0, w1, w2 = cw_ref[0:1, :], cw_ref[1:2, :], cw_ref[2:3, :]
        dyf = dy_ref[...].astype(F32)
        o_ref[:, 0:D] = (dyf * (w0 * cu2 + w1 * cu1 + w2 * cu)).astype(BF16)
        dcv = dyf * b_ref[...].astype(F32)
        dcvn = jnp.where(i == nt - 1, 0.0, dyn_ref[...].astype(F32) * bn_ref[...].astype(F32))
        dcu = w2 * dcv + w1 * _shift_up(dcv, dcvn, 1) + w0 * _shift_up(dcv, dcvn, 2)
        o_ref[:, D:2 * D] = (dcu * u).astype(BF16)
        o_ref[:, 2 * D:3 * D] = (dcu * c).astype(BF16)
        row = lax.broadcasted_iota(jnp.int32, (8, D), 0)
        dw = jnp.zeros((8, D), F32)
        for tap, t in enumerate((cu2, cu1, cu)):
            dw = jnp.where(row == tap, jnp.sum(dcv * t, axis=0, keepdims=True), dw)
        _accumulate(dw_ref, i == 0, dw)

    tile = lambda col: pl.BlockSpec((tm, D), lambda i: (i, col))
    prev = lambda col: pl.BlockSpec((BF16_ROWS, D), lambda i: (jnp.maximum(i * hb - 1, 0), col))
    nxt = lambda col: pl.BlockSpec((BF16_ROWS, D), lambda i: (jnp.minimum((i + 1) * hb, last), col))
    return _call(
        body, name=name, grid=(nt,),
        in_specs=[tile(0), nxt(0), tile(0), nxt(0), tile(1), tile(2), prev(1), prev(2),
                  pl.BlockSpec((3, D), lambda i: (0, 0))],
        out_specs=[pl.BlockSpec((tm, 3 * D), lambda i: (i, 0)), pl.BlockSpec((8, D), lambda i: (0, 0))],
        out_shape=[SDS((T, 3 * D), BF16), SDS((8, D), F32)],
        semantics=("arbitrary",), args=(dy, dy, bcx, bcx, bcx, bcx, bcx, bcx, conv_w), ride=ride)


class NoTraffic:
    def ride(self, kernel_name):
        return None

    def landed(self, kernel_name, results, wts):
        pass

    def grad(self, key, value):
        pass


def local_step(x, target, wts, vec, traffic):
    T, D = x.shape
    tabs = rope_tables(T)
    small = {}

    def run(builder, *args, name, **kw):
        ride = traffic.ride(name)
        if ride is None:
            return builder(*args, name=name, **kw)
        out, extra = builder(*args, name=name, ride=ride, **kw)
        traffic.landed(name, extra, wts)
        return out

    bcx, xn1 = run(norm_matmul, x, vec["a_pre"], wts["w_in"], tn=3 * D, split=1, name="a_in")
    bcx = bcx[0]
    h1, z0, y0 = run(conv_mix_out, bcx, vec["conv_w"], wts["w_out"], vec["a_post"], x, name="a_out")
    gu0, act0, xt2 = run(norm_swiglu_in, h1, vec["ffn_pre0"], wts["gu0"], name="ffn0_in")
    h2, z1 = run(plain_mix_out, act0, wts["wd0"], vec["ffn_post0"], h1, name="ffn0_out")
    kvp, xkv, qp, xq = run(norm2_matmul, h2, [vec["kv_norm"], vec["b_pre"]], [wts["w_kv"], wts["w_q"]],
                           name="kvq_in")
    attn, attn_stats = run(attention_fwd, qp, kvp, tabs, vec["sinks"], name="attn_fwd")
    h3, z2 = plain_mix_out(attn, wts["w_o"], vec["b_post"], h2, name="attn_out", tm=BIG_ROW_TILE)
    gu1, act1, xt3 = run(norm_swiglu_in, h3, vec["ffn_pre1"], wts["gu1"], name="ffn1_in")
    dy, dz3, small["ffn_post1"], dact1, loss = plain_mix_out(act1, wts["wd1"], vec["ffn_post1"], h3, name="ffn1_out",
                                                             target=target)

    def ffn_bwd(layer, dz, dact, gu, act, xt, h_in, dh, then, gu_first):
        tag = "ffn%d" % layer
        dwd = lambda: traffic.grad("wd%d" % layer, run(matmul_tn, act, dz[None], tb=D, name=tag + "_dwd"))
        dwgu = lambda: traffic.grad("gu%d" % layer, run(swiglu_bwd_tn, xt, dact, gu, name=tag + "_dwgu"))
        for step in ((dwgu, dwd) if gu_first else (dwd, dwgu)):
            step()
        dh_in, small["ffn_pre%d" % layer], dz_, dg_, da_ = run(
            swiglu_bwd_in, dact, gu, wts["gu%d" % layer], h_in, vec["ffn_pre%d" % layer], dh, then,
            name=tag + "_in_bwd")
        return dh_in, dz_, dg_, da_

    dh3, dz2, small["b_post"], dattn = ffn_bwd(1, dz3, dact1, gu1, act1, xt3, h3, dy,
                                               (z2, vec["b_post"], wts["w_o"]), gu_first=False)
    traffic.grad("w_o", matmul_tn(attn, dz2[None], tb=D, name="attn_dwo"))
    dq, dkv, small["sinks"] = run(attention_bwd, qp, kvp, tabs, vec["sinks"], dattn, attn, attn_stats,
                                  name="attn_bwd")
    traffic.grad("w_q", matmul_tn(xq, dq[None], tb=D, name="attn_dwq"))
    traffic.grad("w_kv", matmul_tn(xkv, dkv[None], tb=dkv.shape[1], name="attn_dwkv"))
    dh2, small["b_pre"], small["kv_norm"], dz1, small["ffn_post0"], dact0 = run(
        matmuls_nt_normbwd, [dq, dkv], [wts["w_q"], wts["w_kv"]], h2, [vec["b_pre"], vec["kv_norm"]], dh3,
        (z1, vec["ffn_post0"], wts["wd0"]), name="qkv_in_bwd")
    dh1, dz0, small["a_post"], dyc = ffn_bwd(0, dz1, dact0, gu0, act0, xt2, h1, dh2,
                                             (z0, vec["a_post"], wts["w_out"]), gu_first=True)
    traffic.grad("w_out", run(matmul_tn, y0, dz0[None], tb=D, name="a_dwout"))
    dbcx, small["conv_w"] = run(conv_bwd, dyc, bcx, vec["conv_w"], name="a_conv_bwd")
    traffic.grad("w_in", run(matmul_tn, xn1, dbcx[None], tb=3 * D // 2, name="a_dwin"))
    dx, small["a_pre"] = run(matmul_nt_normbwd, dbcx[None], wts["w_in"], x, vec["a_pre"], dh1, name="a_in_bwd")
    return loss, dx, small


SMALL_ROWS = 16
LOSS_ROW = 13

WHOLE = None
GATHER_PLAN = {"cast_rest": [("w_in", WHOLE)],
               "a_in": [("w_out", WHOLE), ("gu0", (0, 18))],
               "a_out": [("gu0", (18, 14))],
               "ffn0_in": [("wd0", WHOLE), ("w_kv", WHOLE), ("w_q", WHOLE), ("gu1", (0, 4))],
               "ffn0_out": [("w_o", WHOLE), ("gu1", (4, 8))],
               "attn_fwd": [("gu1", (12, 20))],
               "ffn1_in": [("wd1", WHOLE)]}
PAIR_PLAN = {"ffn1_dwgu": ["wd1"], "ffn1_in_bwd": ["gu1"], "attn_bwd": ["w_o"], "qkv_in_bwd": ["w_q", "w_kv"],
             "ffn0_dwd": ["gu0"], "ffn0_in_bwd": ["wd0"], "a_conv_bwd": ["w_out"], "chip_reduce_early": ["w_in"]}
CHIP_PLAN = {"ffn1_in_bwd": [("wd1", WHOLE)], "attn_bwd": [("gu1", WHOLE)],
             "ffn0_dwgu": [("w_o", WHOLE), ("w_q", WHOLE), ("w_kv", WHOLE)],
             "ffn0_in_bwd": [("gu0", WHOLE)], "a_conv_bwd": [("wd0", (0, 12))],
             "a_dwin": [("wd0", (12, 10)), ("w_out", WHOLE)], "a_in_bwd": [("w_in", WHOLE)]}
HALF_PLAN = {"a_in_bwd": ["gu0", "gu1", "wd0", "wd1", "w_kv", "w_q", "w_o", "w_out"]}
GRAD_KIND = dict(KIND, gu0="split", gu1="split")


class Traffic:
    def __init__(self, wholes, quarter, c_arr, pc_arr):
        self.wholes, self.quarter, self.c_arr, self.pc_arr = wholes, quarter, c_arr, pc_arr
        self.views, self.sums, self.got = {}, {}, {}
        self.reduced = {}
        self.stages = {}

    def reduce(self, keys, name):
        args = ([self.sums[k] for k in keys], [self.got[k] for k in keys], [GRAD_KIND[k] for k in keys], self.pc_arr)
        if name not in PAIR_PLAN:
            return chip_reduce(*args, name=name)
        pairs = PAIR_PLAN[name]
        out, got = chip_reduce(*args, name=name, ride=pair_ride([self.views[k] for k in pairs]))
        self.pair_sums(pairs, got)
        return out

    def pair_sums(self, keys, got):
        for k, theirs in zip(keys, got):
            self.sums[k] = pair_add(self.views[k], theirs, self.c_arr, name="pair_add_" + k)

    def ride(self, name, small=None):
        rides, stages = [], []
        if name in GATHER_PLAN:
            plan = GATHER_PLAN[name]
            rides.append(gather_ride([self.wholes[k] for k, _ in plan],
                                     [(KIND[k], self.quarter[k], part) for k, part in plan], small))
            stages.append(("gather", [k for k, _ in plan]))
        if name in HALF_PLAN:
            keys = HALF_PLAN[name]
            rides.append(half_ride(self.reduce(keys, "chip_reduce_early")))
            stages.append(("half", keys))
        if name in CHIP_PLAN:
            plan = CHIP_PLAN[name]
            rides.append(chip_ride([self.sums[k] for k, _ in plan],
                                   [(GRAD_KIND[k], self.quarter[k], part) for k, part in plan],
                                   earlier=[self.got.get(k) for k, _ in plan]))
            stages.append(("chip", [k for k, _ in plan]))
        if name in PAIR_PLAN:
            keys = PAIR_PLAN[name]
            rides.append(pair_ride([self.views[k] for k in keys]))
            stages.append(("pair", keys))
        self.stages[name] = stages
        return join(rides)

    def landed(self, name, results, wts):
        results = list(results)
        for stage, keys in self.stages[name]:
            mine, results = results[:len(keys)], results[len(keys):]
            if stage == "gather":
                for k, whole in zip(keys, mine):
                    self.wholes[k] = wts[k] = whole
            elif stage == "chip":
                self.got.update(zip(keys, mine))
            elif stage == "half":
                self.reduced.update(zip(keys, mine))
            else:
                self.pair_sums(keys, mine)

    def grad(self, key, value):
        r, ws = self.quarter[key]
        view = {"row": (N_CHIPS, 2, r // 2, ws), "col": (1, 2, r // 2, N_CHIPS * ws), "split": (2, 2, r // 2, 2 * ws)}
        self.views[key] = value.reshape(view[GRAD_KIND[key]])


def kernel(x, a_pre_norm, a_w_in, a_conv_w, a_w_out, a_post_norm, ffn_pre_norm, ffn_w_gate_up, ffn_w_down, ffn_post_norm, kv_norm, w_kv, b_pre_norm, b_w_q, b_sinks, b_w_o, b_post_norm, loss_target, m_a_pre_norm, m_a_w_in, m_a_conv_w, m_a_w_out, m_a_post_norm, m_ffn_pre_norm, m_ffn_w_gate_up, m_ffn_w_down, m_ffn_post_norm, m_kv_norm, m_w_kv, m_b_pre_norm, m_b_w_q, m_b_sinks, m_b_w_o, m_b_post_norm, v_a_pre_norm, v_a_w_in, v_a_conv_w, v_a_w_out, v_a_post_norm, v_ffn_pre_norm, v_ffn_w_gate_up, v_ffn_w_down, v_ffn_post_norm, v_kv_norm, v_w_kv, v_b_pre_norm, v_b_w_q, v_b_sinks, v_b_w_o, v_b_post_norm):
    T, D = x.shape[1], x.shape[2]
    xi, yi, ci = _place()
    p = 2 * xi + yi
    p_arr = jnp.reshape(p, (1,)).astype(jnp.int32)
    c_arr = jnp.reshape(ci, (1,)).astype(jnp.int32)
    pc_arr = jnp.stack([p, ci]).astype(jnp.int32)
    me_arr = jnp.reshape(4 * xi + 2 * yi + ci, (1,)).astype(jnp.int32)
    qd = D // N_CHIPS

    big = {"w_in": (a_w_in, 0), "w_out": (a_w_out, 0), "gu0": (ffn_w_gate_up, 0), "gu1": (ffn_w_gate_up, 1),
           "wd0": (ffn_w_down, 0), "wd1": (ffn_w_down, 1), "w_kv": (w_kv[None], 0), "w_q": (b_w_q, 0),
           "w_o": (b_w_o, 0)}
    names = list(big)
    quarter = {k: w.shape[1:] for k, (w, _) in big.items()}
    source = lambda k: big[k] + (KIND[k],)
    traffic = Traffic(dict(zip(names[:1], cast_quarters([source(names[0])], p_arr, name="cast_first"))), quarter,
                      c_arr, pc_arr)
    small_shard = jnp.concatenate([a_pre_norm, a_post_norm, a_conv_w[0], jnp.zeros((3, qd), F32)], axis=0)
    wts = {}
    rest, (*landed, small_full) = cast_quarters([source(k) for k in names[1:]], p_arr, name="cast_rest",
                                                ride=traffic.ride("cast_rest", small_shard))
    traffic.wholes.update(zip(names[1:], rest))
    traffic.landed("cast_rest", landed, wts)
    rows = lambda k: jnp.transpose(small_full[:, k], (1, 0, 2)).reshape(-1, D)
    vec = {"a_pre": rows(slice(0, 1)), "a_post": rows(slice(1, 2)), "conv_w": rows(slice(2, 5)),
           "ffn_pre0": ffn_pre_norm[0:1], "ffn_pre1": ffn_pre_norm[1:2],
           "ffn_post0": ffn_post_norm[0:1], "ffn_post1": ffn_post_norm[1:2],
           "kv_norm": kv_norm[None], "b_pre": b_pre_norm, "b_post": b_post_norm, "sinks": b_sinks}

    loss, dx, small = local_step(x[0], loss_target[0], wts, vec, traffic)

    pad = lambda a: jnp.pad(a, ((0, 0), (0, D - a.shape[1])))
    small_block = jnp.concatenate(
        [small["a_pre"], small["a_post"], small["conv_w"][0:3], small["ffn_pre0"], small["ffn_pre1"],
         small["ffn_post0"], small["ffn_post1"], small["kv_norm"], small["b_pre"], small["b_post"],
         pad(small["sinks"][0:1]), pad(loss[0:1]), jnp.zeros((SMALL_ROWS - LOSS_ROW - 1, D), F32)], axis=0)
    late = [k for k in names if k not in traffic.reduced]
    *swapped, small_blocks = alone(join([half_ride(traffic.reduce(late, "chip_reduce_late")),
                                         chip_ride([], [], small_block)]), name="last_exchange")
    traffic.reduced.update(zip(late, swapped))
    grad = {k: traffic.reduced[k].reshape(quarter[k]) for k in names}
    small_sum = small_reduce(small_blocks, me_arr)

    out = {}
    out["a_w_in"] = adamw(a_w_in, [grad["w_in"]], m_a_w_in, v_a_w_in, name="adamw_a_w_in")
    out["a_w_out"] = adamw(a_w_out, [grad["w_out"]], m_a_w_out, v_a_w_out, name="adamw_a_w_out")
    out["ffn_w_gate_up"] = adamw(ffn_w_gate_up, [grad["gu0"], grad["gu1"]], m_ffn_w_gate_up, v_ffn_w_gate_up,
                                 name="adamw_ffn_w_gate_up")
    out["ffn_w_down"] = adamw(ffn_w_down, [grad["wd0"], grad["wd1"]], m_ffn_w_down, v_ffn_w_down,
                              name="adamw_ffn_w_down")
    out["w_kv"] = [o[0] for o in adamw(w_kv[None], [grad["w_kv"]], m_w_kv[None], v_w_kv[None], name="adamw_w_kv")]
    out["b_w_q"] = adamw(b_w_q, [grad["w_q"]], m_b_w_q, v_b_w_q, name="adamw_b_w_q")
    out["b_w_o"] = adamw(b_w_o, [grad["w_o"]], m_b_w_o, v_b_w_o, name="adamw_b_w_o")

    leaves = {"a_pre_norm": (a_pre_norm, m_a_pre_norm, v_a_pre_norm, 0, True),
              "a_post_norm": (a_post_norm, m_a_post_norm, v_a_post_norm, 1, True),
              "a_conv_w": (a_conv_w, m_a_conv_w, v_a_conv_w, 2, True),
              "ffn_pre_norm": (ffn_pre_norm, m_ffn_pre_norm, v_ffn_pre_norm, 5, False),
              "ffn_post_norm": (ffn_post_norm, m_ffn_post_norm, v_ffn_post_norm, 7, False),
              "kv_norm": (kv_norm[None], m_kv_norm[None], v_kv_norm[None], 9, False),
              "b_pre_norm": (b_pre_norm, m_b_pre_norm, v_b_pre_norm, 10, False),
              "b_post_norm": (b_post_norm, m_b_post_norm, v_b_post_norm, 11, False),
              "b_sinks": (b_sinks, m_b_sinks, v_b_sinks, 12, False)}
    for k, results in zip(leaves, adamw_rows(small_sum, p_arr, list(leaves.values()), name="adamw_small")):
        out[k] = [r[0] for r in results] if k == "kv_norm" else results

    order = ["a_pre_norm", "a_w_in", "a_conv_w", "a_w_out", "a_post_norm", "ffn_pre_norm", "ffn_w_gate_up",
             "ffn_w_down", "ffn_post_norm", "kv_norm", "w_kv", "b_pre_norm", "b_w_q", "b_sinks", "b_w_o",
             "b_post_norm"]
    return (small_sum[LOSS_ROW, 0], dx[None], *[out[k][0] for k in order], *[out[k][1] for k in order],
            *[out[k][2] for k in order], *[out[k][3] for k in order])
```

```python
import math

import jax
import jax.numpy as jnp
from jax import lax
from jax.experimental import pallas as pl
from jax.experimental.pallas import tpu as pltpu

F32 = jnp.float32
BF16 = jnp.bfloat16
SDS = jax.ShapeDtypeStruct
MESH = pl.DeviceIdType.MESH
DMA = pltpu.SemaphoreType.DMA
HBM_SPEC = pl.BlockSpec(memory_space=pltpu.HBM)

EPS = 1e-6
NEG = -1e30
HEAD_DIM = 64
N_KV_HEADS = 4
BLOCK = 128
ROT_DIM = HEAD_DIM // 4
ROPE_THETA = 500000.0
N_CHIPS = 4

ADAM_LR = 0.001
ADAM_B1 = 0.9
ADAM_B2 = 0.999
ADAM_EPS = 1e-08
ADAM_WD = 0.01
ADAM_STEP = 10

VMEM_LIMIT_BYTES = 52 * 1024 * 1024
ROW_TILE = 512
BF16_ROWS = 16
STREAM = BF16
MXU_WIDTH = 256

KIND = {"w_in": "col", "gu0": "col", "gu1": "col", "w_out": "row", "wd0": "row", "wd1": "row", "w_kv": "row",
        "w_q": "row", "w_o": "row"}


def _params(*semantics):
    return pltpu.CompilerParams(dimension_semantics=semantics, vmem_limit_bytes=VMEM_LIMIT_BYTES)


def _row_tile(rows, limit, step=8):
    return max(t for t in range(step, limit + 1, step) if rows % t == 0)


def _place():
    return lax.axis_index("x"), lax.axis_index("y"), lax.axis_index("c")


def _other_chips(x, y):
    return [(1 - x, y), (x, 1 - y), (1 - x, 1 - y)]


def _remote(src, dst, send_sem, recv_sem, to):
    return pltpu.make_async_remote_copy(src_ref=src, dst_ref=dst, send_sem=send_sem, recv_sem=recv_sem,
                                        device_id=to, device_id_type=MESH)


def _full_shape(kind, quarter):
    r, ws = quarter
    return (N_CHIPS * r, ws) if kind == "row" else (r, N_CHIPS * ws)


def _rows_of(h, part):
    lo, n = (0, h) if part is None else (part[0] * BF16_ROWS, part[1] * BF16_ROWS)
    assert lo + n <= h, (h, part)
    return lo, n


def _half_of_quarter(ref, kind, quarter, part, q, half):
    r, ws = quarter
    h = r // 2
    lo, n = _rows_of(h, part)
    if kind == "row":
        return ref.at[pl.ds(pl.multiple_of(q * r + half * h + lo, BF16_ROWS), n)]
    return ref.at[pl.ds(pl.multiple_of(half * h + lo, BF16_ROWS), n), pl.ds(pl.multiple_of(q * ws, 128), ws)]


class Ride:
    def __init__(self, operands, out_shape, aliases, sems, make):
        self.operands, self.out_shape, self.aliases, self.sems, self.make = operands, out_shape, aliases, sems, make

    def stages(self, ins, outs, sems):
        made = self.make(ins, outs, sems)
        return made if len(made) == 4 else (made[0], None, None, made[1])


def join(rides):
    rides = [r for r in rides if r is not None]
    if len(rides) < 2:
        return rides[0] if rides else None
    aliases, at = {}, [0, 0, 0]
    cuts = []
    for r in rides:
        aliases.update({at[0] + i: at[1] + o for i, o in r.aliases.items()})
        cuts.append(tuple(at))
        at = [at[0] + len(r.operands), at[1] + len(r.out_shape), at[2] + len(r.sems)]
    cuts.append(tuple(at))

    def make(ins, outs, sem):
        made = [r.stages(ins[lo[0]:hi[0]], outs[lo[1]:hi[1]], sem[lo[2]:hi[2]]) for r, lo, hi in zip(rides, cuts, cuts[1:])]
        def all_of(k):
            def stage():
                for m in made:
                    if m[k] is not None:
                        m[k]()
            return stage

        if all(m[1] is None for m in made):
            return all_of(0), all_of(3)
        return all_of(0), all_of(1), all_of(2), all_of(3)

    return Ride(sum((list(r.operands) for r in rides), []), sum((list(r.out_shape) for r in rides), []), aliases,
                sum((list(r.sems) for r in rides), []), make)


def _call(body, *, name, grid, in_specs, out_specs, out_shape, args, scratch_shapes=(), semantics=None, ride=None,
          prefetch=None):
    pre = 0 if prefetch is None else 1
    n_in, n_out, n_scr = len(in_specs), len(out_specs), len(scratch_shapes)
    r_in, r_out = (len(ride.operands), len(ride.out_shape)) if ride is not None else (0, 0)
    a, b = pre + n_in, pre + n_in + r_in
    c, d = b + n_out, b + n_out + r_out
    e = d + n_scr

    def riding(*refs):
        start, relay, relay_again, finish = ride.stages(refs[a:b], refs[c:d], refs[e:])
        step, steps = pl.program_id(0), 1
        for k, extent in enumerate(grid):
            step = pl.program_id(k) if k == 0 else step * extent + pl.program_id(k)
            steps *= extent
        pl.when(step == 0)(start)
        if relay is not None:
            pl.when(step == steps // 2)(relay)
            pl.when(step == steps - 1)(relay_again)
        body(*refs[:a], *refs[b:c], *refs[d:e])
        pl.when(step == steps - 1)(finish)

    if ride is None:
        kernel_body, extra_in, extra_out, extra_shape, extra_scr, aliases = body, [], [], [], [], {}
        params = _params(*semantics)
    else:
        kernel_body, extra_in, extra_out = riding, [HBM_SPEC] * r_in, [HBM_SPEC] * r_out
        extra_shape, extra_scr = list(ride.out_shape), list(ride.sems)
        aliases = {pre + n_in + i: n_out + o for i, o in ride.aliases.items()}
        params = _params(*(("arbitrary",) * len(grid)))
    specs = dict(grid=grid, in_specs=list(in_specs) + extra_in, out_specs=list(out_specs) + extra_out,
                 scratch_shapes=list(scratch_shapes) + extra_scr)
    if prefetch is not None:
        specs = dict(grid_spec=pltpu.PrefetchScalarGridSpec(num_scalar_prefetch=1, **specs))
        args = (prefetch,) + tuple(args)
    outs = pl.pallas_call(kernel_body, name=name, out_shape=list(out_shape) + extra_shape,
                          input_output_aliases=aliases, compiler_params=params, **specs,
                          )(*args, *(ride.operands if ride is not None else ()))
    return outs if ride is None else (outs[:n_out], outs[n_out:])


def alone(ride, *, name):
    def body(*refs):
        n = len(ride.operands)
        stages = ride.stages(refs[:n], refs[n:n + len(ride.out_shape)], refs[n + len(ride.out_shape):])
        for stage in stages:
            if stage is not None:
                stage()

    return pl.pallas_call(
        body, name=name, in_specs=[HBM_SPEC] * len(ride.operands), out_specs=[HBM_SPEC] * len(ride.out_shape),
        out_shape=list(ride.out_shape), input_output_aliases=dict(ride.aliases), scratch_shapes=list(ride.sems),
    )(*ride.operands)


def _two_pieces(h, part):
    lo, n = (0, h // BF16_ROWS) if part is None else part
    assert n >= 2, (h, part)
    return (lo, n // 2), (lo + n // 2, n - n // 2)


def gather_ride(wholes, metas, small=None):
    n = len(wholes)
    operands, out_shape = list(wholes), [SDS(s.shape, s.dtype) for s in wholes]
    sems = [DMA((n, 4)), DMA((n, 4)), DMA((n, 4)), DMA((n, 4))]
    if small is not None:
        operands.append(small)
        out_shape.append(SDS((N_CHIPS,) + small.shape, small.dtype))
        sems += [DMA((3,)), DMA((3,)), DMA(())]

    def make(ins, outs, sem):
        send1, recv1, send2, recv2 = sem[:4]
        x, y, c = _place()
        p = 2 * x + y
        chips = _other_chips(x, y)
        across_x, across_y, across_both = [2 * qx + qy for qx, qy in chips]
        me, sibling = (x, y, c), (x, y, 1 - c)

        def region(t, q, half, piece=None):
            kind, quarter, part = metas[t]
            if piece is not None:
                part = _two_pieces(quarter[0] // 2, part)[piece]
            return _half_of_quarter(outs[t], kind, quarter, part, q, half)

        first, second, arriving = [], [], []
        landing, passing = [[], [], [], []], [[], [], [], []]
        for j, (qx, qy) in enumerate(chips):
            if small is not None:
                q = 2 * qx + qy
                first.append(_remote(ins[n], outs[n].at[p], sem[4].at[j], sem[5].at[j], (qx, qy, c)))
                arriving.append(_remote(outs[n].at[q], outs[n].at[q], sem[4].at[j], sem[5].at[j], me))
        for t in range(n):
            mine = region(t, p, c)
            for j in range(2):
                first.append(_remote(mine, mine, send1.at[t, j], recv1.at[t, j], chips[j] + (c,)))
            lands = [(across_x, None), (across_y, None), (across_both, 0), (across_both, 1)]
            for k, (q, piece) in enumerate(lands):
                landed, theirs = region(t, q, c, piece), region(t, q, 1 - c, piece)
                landing[k].append(_remote(landed, landed, send1.at[t, k], recv1.at[t, k], me))
                passing[k].append(_remote(landed, landed, send2.at[t, k], recv2.at[t, k], sibling))
                arriving.append(_remote(theirs, theirs, send2.at[t, k], recv2.at[t, k], me))
            onward = region(t, across_x, c, 0)
            second.append(_remote(onward, onward, send1.at[t, 2], recv1.at[t, 2], chips[1] + (c,)))
            onward = region(t, across_y, c, 1)
            second.append(_remote(onward, onward, send1.at[t, 3], recv1.at[t, 3], chips[0] + (c,)))
        local = [] if small is None else [pltpu.make_async_copy(ins[n], outs[n].at[p], sem[6])]

        def start():
            for cp in local + first:
                cp.start()

        def relay():
            for k in range(2):
                for t in range(n):
                    landing[k][t].wait_recv()
                    second[2 * t + k].start()
                    passing[k][t].start()

        def relay_again():
            for k in range(2, 4):
                for t in range(n):
                    landing[k][t].wait_recv()
                    passing[k][t].start()

        def finish():
            for cp in arriving:
                cp.wait_recv()
            for cp in first + second + sum(passing, []):
                cp.wait_send()
            for cp in local:
                cp.wait()

        return start, relay, relay_again, finish

    return Ride(operands, out_shape, {t: t for t in range(n)}, sems, make)


def chip_ride(sums, metas, small=None, earlier=None):
    n = len(sums)
    operands = list(sums)
    out_shape = [SDS((3, s.shape[1], quarter[1]), s.dtype) for s, (_, quarter, _) in zip(sums, metas)]
    sems = [DMA((n, 3)), DMA((n, 3))] if n else []
    if small is not None:
        operands.append(small)
        out_shape.append(SDS((8,) + small.shape, small.dtype))
        sems += [DMA((7,)), DMA((7,)), DMA(())]
    aliases = {}
    for t, buffer in enumerate(earlier or [None] * n):
        if buffer is not None:
            aliases[len(operands)] = t
            operands.append(buffer)

    def make(ins, outs, sem):
        x, y, c = _place()
        cps = []
        for j, (qx, qy) in enumerate(_other_chips(x, y)):
            q = 2 * qx + qy
            for t in range(n):
                kind, (_, ws), part = metas[t]
                rows = pl.ds(*_rows_of(ins[t].shape[1], part))
                if kind == "row":
                    src = ins[t].at[q, rows]
                elif kind == "col":
                    src = ins[t].at[0, rows, pl.ds(pl.multiple_of(q * ws, 128), ws)]
                else:
                    src = ins[t].at[q // 2, rows, pl.ds(pl.multiple_of((q % 2) * ws, 128), ws)]
                cps.append(_remote(src, outs[t].at[j, rows], sem[0].at[t, j], sem[1].at[t, j], (qx, qy, c)))
        local = []
        if small is not None:
            ssend, srecv, lsem = sem[2 * bool(n):2 * bool(n) + 3]
            local.append(pltpu.make_async_copy(ins[n], outs[n].at[0], lsem))
            for k in range(1, 8):
                peer = (x ^ (k >> 2 & 1), y ^ (k >> 1 & 1), c ^ (k & 1))
                cps.append(_remote(ins[n], outs[n].at[k], ssend.at[k - 1], srecv.at[k - 1], peer))

        def start():
            for cp in local + cps:
                cp.start()

        def finish():
            for cp in cps + local:
                cp.wait()

        return start, finish

    return Ride(operands, out_shape, aliases, sems, make)


def pair_ride(grads):
    n = len(grads)

    def make(ins, outs, sem):
        x, y, c = _place()
        cps = [_remote(ins[t].at[:, 1 - c], outs[t], sem[0].at[t], sem[1].at[t], (x, y, 1 - c)) for t in range(n)]

        def start():
            for cp in cps:
                cp.start()

        def finish():
            for cp in cps:
                cp.wait()

        return start, finish

    return Ride(list(grads), [SDS((g.shape[0],) + g.shape[2:], g.dtype) for g in grads], {}, [DMA((n,)), DMA((n,))],
                make)


def half_ride(quarters):
    n = len(quarters)

    def make(ins, outs, sem):
        x, y, c = _place()
        sends = [_remote(outs[t].at[c], outs[t].at[c], sem[0].at[t], sem[1].at[t], (x, y, 1 - c)) for t in range(n)]

        def start():
            for cp in sends:
                cp.start()

        def finish():
            for t in range(n):
                theirs = outs[t].at[1 - c]
                _remote(theirs, theirs, sem[0].at[t], sem[1].at[t], (x, y, c)).wait_recv()
            for cp in sends:
                cp.wait_send()

        return start, finish

    return Ride(list(quarters), [SDS(q.shape, q.dtype) for q in quarters], {t: t for t in range(n)},
                [DMA((n,)), DMA((n,))], make)


CAST_STEPS = 4


def cast_quarters(sources, p_arr, *, name, ride=None):
    n = len(sources)
    in_specs, out_specs, out_shape = [], [], []
    for w, layer, kind in sources:
        _, r, ws = w.shape
        tr = r // CAST_STEPS
        assert tr % BF16_ROWS == 0, w.shape
        in_specs.append(pl.BlockSpec((None, tr, ws), lambda i, p_ref, layer=layer: (layer, i, 0)))
        out_specs.append(pl.BlockSpec((tr, ws), (lambda i, p_ref: (p_ref[0] * CAST_STEPS + i, 0)) if kind == "row"
                                      else (lambda i, p_ref: (i, p_ref[0]))))
        out_shape.append(SDS(_full_shape(kind, (r, ws)), BF16))

    def body(p_ref, *refs):
        for w_ref, o_ref in zip(refs[:n], refs[n:]):
            o_ref[...] = w_ref[...].astype(BF16)

    return _call(body, name=name, grid=(CAST_STEPS,), in_specs=in_specs, out_specs=out_specs, out_shape=out_shape,
                 semantics=("parallel",), args=[w for w, _, _ in sources], ride=ride, prefetch=p_arr)


def pair_add(own, got, c_arr, *, name):
    A, _, h, W = own.shape
    th = _row_tile(h, max(BF16_ROWS, (3 << 19) // W), BF16_ROWS)

    def body(c_ref, a_ref, b_ref, o_ref):
        o_ref[...] = (a_ref[...].astype(F32) + b_ref[...].astype(F32)).astype(BF16)

    return pl.pallas_call(
        body, name=name,
        grid_spec=pltpu.PrefetchScalarGridSpec(
            num_scalar_prefetch=1, grid=(A, h // th),
            in_specs=[pl.BlockSpec((None, None, th, W), lambda q, i, c_ref: (q, c_ref[0], i, 0)),
                      pl.BlockSpec((None, th, W), lambda q, i, c_ref: (q, i, 0))],
            out_specs=pl.BlockSpec((None, th, W), lambda q, i, c_ref: (q, i, 0))),
        out_shape=SDS((A, h, W), BF16),
        compiler_params=_params("parallel", "parallel"),
    )(c_arr, own, got)


REDUCE_STEPS = 2


def chip_reduce(sums, got, kinds, pc_arr, *, name, ride=None):
    n = len(sums)
    mine = {"row": lambda i, pc_ref: (pc_ref[0], i, 0), "col": lambda i, pc_ref: (0, i, pc_ref[0]),
            "split": lambda i, pc_ref: (pc_ref[0] // 2, i, pc_ref[0] % 2)}
    a_specs, b_specs, o_specs, out_shape = [], [], [], []
    for g, kind in zip(got, kinds):
        _, h, ws = g.shape
        th = h // REDUCE_STEPS
        assert th % BF16_ROWS == 0, g.shape
        a_specs.append(pl.BlockSpec((None, th, ws), mine[kind]))
        b_specs.append(pl.BlockSpec((3, th, ws), lambda i, pc_ref: (0, i, 0)))
        o_specs.append(pl.BlockSpec((None, th, ws), lambda i, pc_ref: (pc_ref[1], i, 0)))
        out_shape.append(SDS((2, h, ws), F32))

    def body(pc_ref, *refs):
        for a_ref, b_ref, o_ref in zip(refs[:n], refs[n:2 * n], refs[2 * n:]):
            o_ref[...] = ((a_ref[...].astype(F32) + b_ref[0].astype(F32)) + b_ref[1].astype(F32)) + b_ref[2].astype(F32)

    return _call(body, name=name, grid=(REDUCE_STEPS,), in_specs=a_specs + b_specs, out_specs=o_specs,
                 out_shape=out_shape, semantics=("parallel",), args=list(sums) + list(got), prefetch=pc_arr, ride=ride)


def small_reduce(blocks, me_arr):
    _, rows, D = blocks.shape

    def body(me_ref, b_ref, o_ref):
        me = me_ref[0]
        total = b_ref[me]
        for d in range(1, 8):
            total = total + b_ref[d ^ me]
        o_ref[...] = total

    return pl.pallas_call(
        body, name="small_reduce",
        grid_spec=pltpu.PrefetchScalarGridSpec(
            num_scalar_prefetch=1, grid=(1,),
            in_specs=[pl.BlockSpec((8, rows, D), lambda i, me_ref: (0, 0, 0))],
            out_specs=pl.BlockSpec((rows, D), lambda i, me_ref: (0, 0))),
        out_shape=SDS((rows, D), F32),
        compiler_params=_params("arbitrary"),
    )(me_arr, blocks)


def _adam(w, g, m, v):
    m_new = ADAM_B1 * m + (1.0 - ADAM_B1) * g
    v_new = ADAM_B2 * v + (1.0 - ADAM_B2) * (g * g)
    m_hat = m_new / (1.0 - ADAM_B1 ** ADAM_STEP)
    v_hat = v_new / (1.0 - ADAM_B2 ** ADAM_STEP)
    return -ADAM_LR * (m_hat / (jnp.sqrt(v_hat) + ADAM_EPS) + ADAM_WD * w), m_new, v_new


def adamw_rows(block, p_arr, leaves, *, name):
    L = len(leaves)

    def body(p_ref, b_ref, *refs):
        outs = refs[3 * L:]
        for i, (w, _, _, row, sharded) in enumerate(leaves):
            n, width = w.shape[-2:]
            cols = pl.ds(pl.multiple_of(p_ref[0] * width, 128), width) if sharded else slice(0, width)
            g = b_ref[row:row + n, cols].reshape(w.shape)
            results = (g,) + _adam(refs[i][...], g, refs[L + i][...], refs[2 * L + i][...])
            for o_ref, value in zip(outs[4 * i:4 * i + 4], results):
                o_ref[...] = value

    whole = lambda a: pl.BlockSpec(a.shape, lambda i, p_ref, nd=len(a.shape): (0,) * nd)
    arrays = [leaf[k] for k in range(3) for leaf in leaves]
    shapes = [SDS(leaf[0].shape, F32) for leaf in leaves for _ in range(4)]
    outs = pl.pallas_call(
        body, name=name,
        grid_spec=pltpu.PrefetchScalarGridSpec(
            num_scalar_prefetch=1, grid=(1,), in_specs=[whole(block)] + [whole(a) for a in arrays],
            out_specs=[whole(s) for s in shapes]),
        out_shape=shapes, compiler_params=_params("arbitrary"),
    )(p_arr, block, *arrays)
    return [outs[4 * i:4 * i + 4] for i in range(L)]


def adamw(w, gs, m, v, *, name):
    L, r, cols = w.shape
    tr = _row_tile(r, 256)
    nt = r // tr

    def body(*refs):
        w_ref, m_ref, v_ref = refs[:3]
        g_refs = refs[3:3 + L]
        g_out, d_out, m_out, v_out = refs[3 + L:]
        layer = pl.program_id(0)
        g = g_refs[0][...]
        for l in range(1, L):
            g = jnp.where(layer == l, g_refs[l][...], g)
        g_out[...] = g
        d_out[...], m_out[...], v_out[...] = _adam(w_ref[...], g, m_ref[...], v_ref[...])

    full = pl.BlockSpec((None, tr, cols), lambda l, i: (l, i, 0))
    g_spec = lambda l0: pl.BlockSpec((tr, cols), lambda l, i: (jnp.where(l == l0, i, jnp.where(l < l0, 0, nt - 1)), 0))
    return pl.pallas_call(
        body, name=name, grid=(L, nt),
        in_specs=[full, full, full] + [g_spec(l0) for l0 in range(L)],
        out_specs=[full] * 4,
        out_shape=[SDS(w.shape, F32)] * 4,
        compiler_params=_params("arbitrary", "arbitrary"),
    )(w, m, v, *gs)


def _rms_r(xf):
    return lax.rsqrt(jnp.mean(xf * xf, axis=-1, keepdims=True) + EPS)


def _rmsnorm_bwd(xf, g, dy):
    r = _rms_r(xf)
    xh = xf * r
    gd = g * dy
    return r * (gd - xh * jnp.mean(xh * gd, axis=-1, keepdims=True)), xh


def _dot(a, b):
    return jnp.dot(a, b, preferred_element_type=F32)


def _dot_nt(a, b):
    return lax.dot_general(a, b, (((1,), (1,)), ((), ())), preferred_element_type=F32)


def _dot_tn(a, b):
    return lax.dot_general(a, b, (((0,), (0,)), ((), ())), preferred_element_type=F32)


def _accumulate(ref, first, value):
    @pl.when(first)
    def _():
        ref[...] = value

    @pl.when(jnp.logical_not(first))
    def _():
        ref[...] += value


def norm_matmul(x, g, w, *, tn, split, name, ride=None, tm=ROW_TILE):
    T, D = x.shape
    N = w.shape[1]
    per = N // split // tn

    def body(x_ref, g_ref, w_ref, o_ref, xn_ref):
        @pl.when(pl.program_id(1) == 0)
        def _():
            xf = x_ref[...].astype(F32)
            xn_ref[...] = (xf * _rms_r(xf) * g_ref[...]).astype(BF16)

        o_ref[...] = _dot(xn_ref[...], w_ref[...]).astype(BF16)

    return _call(
        body, name=name, grid=(T // tm, N // tn),
        in_specs=[pl.BlockSpec((tm, D), lambda i, j: (i, 0)),
                  pl.BlockSpec((1, D), lambda i, j: (0, 0)),
                  pl.BlockSpec((D, tn), lambda i, j: (0, j))],
        out_specs=[pl.BlockSpec((None, tm, tn), lambda i, j: (j // per, i, j % per)),
                   pl.BlockSpec((tm, D), lambda i, j: (i, 0))],
        out_shape=[SDS((split, T, N // split), BF16), SDS((T, D), BF16)],
        semantics=("parallel", "arbitrary"), args=(x, g, w), ride=ride)


BIG_ROW_TILE = 1024


def norm2_matmul(x, gains, weights, *, name, ride=None, tm=BIG_ROW_TILE):
    T, D = x.shape
    tm = min(tm, T)
    n = len(gains)

    def body(x_ref, *refs):
        subs = _sub_tiles(tm)
        xhs = []
        for rows in subs:
            xf = x_ref[rows, :].astype(F32)
            xhs.append(xf * _rms_r(xf))
        for g_ref, w_ref, o_ref, xn_ref in zip(refs[:n], refs[n:2 * n], refs[2 * n::2], refs[2 * n + 1::2]):
            for rows, xh in zip(subs, xhs):
                xn = (xh * g_ref[...]).astype(BF16)
                xn_ref[rows, :] = xn
                o_ref[rows, :] = _dot(xn, w_ref[...]).astype(BF16)

    row = pl.BlockSpec((tm, D), lambda i: (i, 0))
    vec = pl.BlockSpec((1, D), lambda i: (0, 0))
    out_specs, out_shape = [], []
    for w in weights:
        out_specs += [pl.BlockSpec((tm, w.shape[1]), lambda i: (i, 0)), row]
        out_shape += [SDS((T, w.shape[1]), BF16), SDS((T, D), BF16)]
    return _call(
        body, name=name, grid=(T // tm,),
        in_specs=[row] + [vec] * n + [pl.BlockSpec(w.shape, lambda i: (0, 0)) for w in weights],
        out_specs=out_specs, out_shape=out_shape, semantics=("parallel",), args=[x] + list(gains) + list(weights),
        ride=ride)


def _shift_down(prev, cur, by):
    big = jnp.concatenate([prev, cur], axis=0)
    return pltpu.roll(big, by, 0)[prev.shape[0]:]


def _shift_up(cur, nxt, by):
    big = jnp.concatenate([cur, nxt], axis=0)
    return pltpu.roll(big, big.shape[0] - by, 0)[:cur.shape[0]]


def conv_mix_out(bcx, conv_w, w_out, g_post, res, *, name, ride=None, tm=ROW_TILE):
    T, D = res.shape
    hb = tm // BF16_ROWS

    def body(b_ref, c_ref, u_ref, cp_ref, up_ref, cw_ref, w_ref, g_ref, r_ref, h_ref, z_ref, y_ref):
        i = pl.program_id(0)
        cu = c_ref[...].astype(F32) * u_ref[...].astype(F32)
        cup = cp_ref[...].astype(F32) * up_ref[...].astype(F32)
        cup = jnp.where(i == 0, 0.0, cup)
        cv = (cw_ref[0:1, :] * _shift_down(cup, cu, 2) + cw_ref[1:2, :] * _shift_down(cup, cu, 1)
              + cw_ref[2:3, :] * cu)
        y = (b_ref[...].astype(F32) * cv).astype(BF16)
        y_ref[...] = y
        z = _dot(y, w_ref[...])
        z_ref[...] = z.astype(BF16)
        h_ref[...] = (r_ref[...] + z * _rms_r(z) * g_ref[...]).astype(STREAM)

    tile = lambda col: pl.BlockSpec((tm, D), lambda i: (i, col))
    halo = lambda col: pl.BlockSpec((BF16_ROWS, D), lambda i: (jnp.maximum(i * hb - 1, 0), col))
    row = pl.BlockSpec((tm, D), lambda i: (i, 0))
    return _call(
        body, name=name, grid=(T // tm,),
        in_specs=[tile(0), tile(1), tile(2), halo(1), halo(2),
                  pl.BlockSpec((3, D), lambda i: (0, 0)),
                  pl.BlockSpec((D, D), lambda i: (0, 0)),
                  pl.BlockSpec((1, D), lambda i: (0, 0)), row],
        out_specs=[row, row, row],
        out_shape=[SDS((T, D), STREAM), SDS((T, D), BF16), SDS((T, D), BF16)],
        semantics=("parallel",), args=(bcx, bcx, bcx, bcx, bcx, conv_w, w_out, g_post, res), ride=ride)


def _normbwd_then_nt(dh, zf, g_ref, w_ref, dz_ref, dg_ref, o_ref, first):
    dz, zh = _rmsnorm_bwd(zf, g_ref[...], dh)
    dz = dz.astype(BF16)
    dz_ref[...] = dz
    _accumulate(dg_ref, first, jnp.sum(dh * zh, axis=0, keepdims=True))
    o_ref[...] = _dot_nt(dz, w_ref[...]).astype(BF16)


def _then_specs(then, tm, T, D):
    z, g, w = then
    K = w.shape[0]
    row = pl.BlockSpec((tm, D), lambda i: (i, 0))
    vec = pl.BlockSpec((1, D), lambda i: (0, 0))
    in_specs = [row, vec, pl.BlockSpec((K, D), lambda i: (0, 0), pipeline_mode=pl.Buffered(1))]
    out_specs = [row, vec, pl.BlockSpec((tm, K), lambda i: (i, 0))]
    out_shape = [SDS((T, D), BF16), SDS((1, D), F32), SDS((T, K), BF16)]
    return in_specs, out_specs, out_shape


def plain_mix_out(a, w, g_post, res, *, name, target=None, ride=None, tm=ROW_TILE):
    T, D = res.shape
    tm = min(tm, T)
    K = a.shape[1]
    with_loss = target is not None

    def body(a_ref, w_ref, g_ref, r_ref, *rest):
        subs = _sub_tiles(tm)
        zs = [_dot(a_ref[rows, :], w_ref[...]) for rows in subs]
        if not with_loss:
            h_ref, z_ref = rest
            for rows, z in zip(subs, zs):
                h_ref[rows, :] = (r_ref[rows, :].astype(F32) + z * _rms_r(z) * g_ref[...]).astype(STREAM)
                z_ref[rows, :] = z.astype(BF16)
            return
        t_ref, h_ref, dz_ref, dg_ref, da_ref, loss_ref = rest
        first = pl.program_id(0) == 0
        loss, dg = jnp.zeros((), F32), jnp.zeros((1, D), F32)
        for rows, z in zip(subs, zs):
            diff = r_ref[rows, :].astype(F32) + z * _rms_r(z) * g_ref[...] - t_ref[rows, :]
            dh = diff * (1.0 / D)
            h_ref[rows, :] = dh.astype(STREAM)
            loss = loss + jnp.sum(diff * diff)
            dz, zh = _rmsnorm_bwd(z, g_ref[...], dh)
            dz = dz.astype(BF16)
            dz_ref[rows, :] = dz
            dg = dg + jnp.sum(dh * zh, axis=0, keepdims=True)
            da_ref[rows, :] = _dot_nt(dz, w_ref[...]).astype(BF16)
        _accumulate(loss_ref, first, jnp.full(loss_ref.shape, 0.5 / D, F32) * loss)
        _accumulate(dg_ref, first, dg)

    row = pl.BlockSpec((tm, D), lambda i: (i, 0))
    vec = pl.BlockSpec((1, D), lambda i: (0, 0))
    in_specs = [pl.BlockSpec((tm, K), lambda i: (i, 0)), pl.BlockSpec((K, D), lambda i: (0, 0)), vec, row]
    if with_loss:
        in_specs.append(row)
        out_specs = [row, row, vec, pl.BlockSpec((tm, K), lambda i: (i, 0)), pl.BlockSpec((8, 128), lambda i: (0, 0))]
        out_shape = [SDS((T, D), STREAM), SDS((T, D), BF16), SDS((1, D), F32), SDS((T, K), BF16), SDS((8, 128), F32)]
    else:
        out_specs, out_shape = [row, row], [SDS((T, D), STREAM), SDS((T, D), BF16)]
    return _call(
        body, name=name, grid=(T // tm,), in_specs=in_specs, out_specs=out_specs, out_shape=out_shape,
        semantics=("arbitrary",), args=(a, w, g_post, res) + ((target,) if with_loss else ()), ride=ride)


def _silu_grads(d, g, u):
    sg = jax.nn.sigmoid(g)
    return d * u * (sg * (1.0 + g * (1.0 - sg))), d * (g * sg)


def _sub_tiles(tm):
    return [pl.ds(k, min(MXU_WIDTH, tm)) for k in range(0, tm, MXU_WIDTH)]


def norm_swiglu_in(x, g, w, *, name, ride=None, tm=ROW_TILE):
    T, D = x.shape
    F = w.shape[1] // 2

    def body(x_ref, g_ref, wg_ref, wu_ref, gu_ref, a_ref, xt_ref):
        subs = _sub_tiles(tm)
        xns = []
        for rows in subs:
            xf = x_ref[rows, :].astype(F32)
            xns.append(xf * _rms_r(xf) * g_ref[...])
        xbs = [xn.astype(BF16) for xn in xns]
        gates = [_dot(xb, wg_ref[...]).astype(BF16) for xb in xbs]
        ups = [_dot(xb, wu_ref[...]).astype(BF16) for xb in xbs]
        for rows, gate, up in zip(subs, gates, ups):
            gu_ref[0, rows, :] = gate
            gu_ref[1, rows, :] = up
            a_ref[rows, :] = gate * jax.nn.sigmoid(gate) * up
        for rows, xn in zip(subs, xns):
            xt_ref[:, rows] = xn.T.astype(BF16)

    half = lambda s: pl.BlockSpec((D, F), lambda i: (0, s), pipeline_mode=pl.Buffered(1))
    return _call(
        body, name=name, grid=(T // tm,),
        in_specs=[pl.BlockSpec((tm, D), lambda i: (i, 0)), pl.BlockSpec((1, D), lambda i: (0, 0)), half(0), half(1)],
        out_specs=[pl.BlockSpec((2, tm, F), lambda i: (0, i, 0)), pl.BlockSpec((tm, F), lambda i: (i, 0)),
                   pl.BlockSpec((D, tm), lambda i: (0, i))],
        out_shape=[SDS((2, T, F), BF16), SDS((T, F), BF16), SDS((D, T), BF16)],
        semantics=("parallel",), args=(x, g, w, w), ride=ride)


def swiglu_bwd_tn(xt, dact, gu, *, name, ride=None, tb=MXU_WIDTH):
    D, T = xt.shape
    F = dact.shape[1]

    def body(xt_ref, d_ref, g_ref, u_ref, o_ref):
        dg, du = _silu_grads(d_ref[...], g_ref[...], u_ref[...])
        o_ref[0] = _dot(xt_ref[...], dg).astype(BF16)
        o_ref[1] = _dot(xt_ref[...], du).astype(BF16)

    col = lambda s: pl.BlockSpec((None, T, tb), lambda j: (s, 0, j))
    out = _call(
        body, name=name, grid=(F // tb,),
        in_specs=[pl.BlockSpec((D, T), lambda j: (0, 0), pipeline_mode=pl.Buffered(1)),
                  pl.BlockSpec((T, tb), lambda j: (0, j)), col(0), col(1)],
        out_specs=[pl.BlockSpec((2, D, tb), lambda j: (0, 0, j))],
        out_shape=[SDS((2, D, F), BF16)],
        semantics=("parallel",), args=(xt, dact, gu, gu), ride=ride)
    return out[0] if ride is None else (out[0][0], out[1])


def swiglu_bwd_in(dact, gu, w, h_in, g, dh_out, then, *, name, ride=None, tm=ROW_TILE):
    T, D = h_in.shape
    F = dact.shape[1]

    def body(d_ref, gg_ref, uu_ref, wg_ref, wu_ref, h_ref, g_ref, dh_ref, z_ref, g2_ref, w2_ref,
             o_ref, dg_ref, dz_ref, dg2_ref, da_ref):
        first = pl.program_id(0) == 0
        subs = _sub_tiles(tm)
        dns = []
        for rows in subs:
            dgate, dup = _silu_grads(d_ref[rows, :], gg_ref[rows, :], uu_ref[rows, :])
            dns.append(_dot_nt(dgate, wg_ref[...]) + _dot_nt(dup, wu_ref[...]))
        dg, dg2 = jnp.zeros((1, D), F32), jnp.zeros((1, D), F32)
        for rows, dn in zip(subs, dns):
            dx, hh = _rmsnorm_bwd(h_ref[rows, :].astype(F32), g_ref[...], dn)
            dh_in = dh_ref[rows, :] + dx
            o_ref[rows, :] = dh_in.astype(STREAM)
            dg = dg + jnp.sum(dn * hh, axis=0, keepdims=True)
            dz, zh = _rmsnorm_bwd(z_ref[rows, :].astype(F32), g2_ref[...], dh_in)
            dz = dz.astype(BF16)
            dz_ref[rows, :] = dz
            dg2 = dg2 + jnp.sum(dh_in * zh, axis=0, keepdims=True)
            da_ref[rows, :] = _dot_nt(dz, w2_ref[...]).astype(BF16)
        _accumulate(dg_ref, first, dg)
        _accumulate(dg2_ref, first, dg2)

    row = pl.BlockSpec((tm, D), lambda i: (i, 0))
    vec = pl.BlockSpec((1, D), lambda i: (0, 0))
    part = lambda s: pl.BlockSpec((None, tm, F), lambda i: (s, i, 0))
    half = lambda s: pl.BlockSpec((D, F), lambda i: (0, s), pipeline_mode=pl.Buffered(1))
    then_in, then_out, then_shape = _then_specs(then, tm, T, D)
    return _call(
        body, name=name, grid=(T // tm,),
        in_specs=[pl.BlockSpec((tm, F), lambda i: (i, 0)), part(0), part(1), half(0), half(1), row, vec, row] + then_in,
        out_specs=[row, vec] + then_out,
        out_shape=[SDS((T, D), STREAM), SDS((1, D), F32)] + then_shape,
        semantics=("arbitrary",), args=(dact, gu, gu, w, w, h_in, g, dh_out) + tuple(then), ride=ride)


def rope_tables(T):
    half = ROT_DIM // 2
    inv_freq = ROPE_THETA ** (-jnp.arange(0, ROT_DIM, 2, dtype=F32) / ROT_DIM)
    ang = (jnp.arange(T, dtype=F32)[:, None] * inv_freq[None, :]).T
    cos, sin = jnp.cos(ang), jnp.sin(ang)
    rest = HEAD_DIM - ROT_DIM
    one, zero = jnp.ones((rest, T), F32), jnp.zeros((rest, T), F32)
    zh = jnp.zeros((half, T), F32)
    fac = jnp.concatenate([cos, cos, one], axis=0)
    up = jnp.concatenate([-sin, zh, zero], axis=0)
    down = jnp.concatenate([zh, sin, zero], axis=0)
    return jnp.stack([fac, up, down])


def _rope(t, tab):
    half = ROT_DIM // 2
    return t * tab[0] + pltpu.roll(t, HEAD_DIM - half, 0) * tab[1] + pltpu.roll(t, half, 0) * tab[2]


def _rope_t(d, tab):
    half = ROT_DIM // 2
    return d * tab[0] + pltpu.roll(d * tab[1], half, 0) + pltpu.roll(d * tab[2], HEAD_DIM - half, 0)


def _head(t, h):
    return t[h * HEAD_DIM:(h + 1) * HEAD_DIM]


def _band(n, group):
    kj = lax.broadcasted_iota(jnp.int32, (2 * BLOCK, BLOCK), 0)
    qi = lax.broadcasted_iota(jnp.int32, (2 * BLOCK, BLOCK), 1)
    mask = (kj > qi) & (kj <= qi + BLOCK) & ((n > 0) | (kj >= BLOCK))
    return jnp.tile(mask, (1, group))


def _attn_specs(D, kvd, nb):
    cur = lambda n: jnp.minimum(n, nb - 1)
    prev = lambda n: jnp.maximum(cur(n) - 1, 0)
    return [pl.BlockSpec((BLOCK, D), lambda n: (cur(n), 0)),
            pl.BlockSpec((BLOCK, kvd), lambda n: (prev(n), 0)),
            pl.BlockSpec((BLOCK, kvd), lambda n: (cur(n), 0)),
            pl.BlockSpec((BLOCK, kvd), lambda n: (prev(n), 1)),
            pl.BlockSpec((BLOCK, kvd), lambda n: (cur(n), 1)),
            pl.BlockSpec((3, HEAD_DIM, BLOCK), lambda n: (0, 0, prev(n))),
            pl.BlockSpec((3, HEAD_DIM, BLOCK), lambda n: (0, 0, cur(n))),
            pl.BlockSpec(memory_space=pltpu.SMEM)]


def _attn_operands(q_ref, kp_ref, k_ref, vp_ref, v_ref, tp_ref, t_ref):
    flip = lambda ref: ref[...].astype(F32).T
    tab = t_ref[...]
    kt = jnp.concatenate([flip(kp_ref), flip(k_ref)], axis=1)
    vt = jnp.concatenate([flip(vp_ref), flip(v_ref)], axis=1)
    return flip(q_ref), kt, vt, tab, jnp.concatenate([tp_ref[...], tab], axis=2)


SCORE_SCALE = 1.0 / math.sqrt(HEAD_DIM)
HEADS_TOGETHER = 4


def _group_heads(t, first, count, tab=None):
    heads = [_head(t, first + g) for g in range(count)]
    if tab is not None:
        heads = [_rope(h, tab) * SCORE_SCALE for h in heads]
    return jnp.concatenate(heads, axis=1).astype(BF16)


def _sink_row(s_ref, first, count):
    which = lax.broadcasted_iota(jnp.int32, (1, count * BLOCK), 1) // BLOCK
    row = jnp.zeros((1, count * BLOCK), F32)
    for g in range(count):
        row = jnp.where(which == g, s_ref[0, first + g], row)
    return row


def _sum_keys(t):
    return _dot(jnp.ones((8, t.shape[0]), BF16), t)[0:1]


def _softmax(scores, sink, mask):
    s = jnp.where(mask, scores.astype(BF16), NEG)
    m = jnp.maximum(jnp.max(s, axis=0, keepdims=True).astype(F32), sink).astype(BF16)
    e = jnp.exp(s - m)
    m = m.astype(F32)
    return e, m, 1.0 / (_sum_keys(e) + jnp.exp(sink - m))


def _per_head(row, count):
    return [row[:, g * BLOCK:(g + 1) * BLOCK] for g in range(count)]


def attention_fwd(q, kv, tabs, sinks, *, name, ride=None):
    T, D = q.shape
    kvd = kv.shape[1] // 2
    heads = D // HEAD_DIM
    group = heads // N_KV_HEADS

    def body(q_ref, kp_ref, k_ref, vp_ref, v_ref, tp_ref, t_ref, s_ref, o_ref, stat_ref):
        gs = HEADS_TOGETHER
        mask = _band(pl.program_id(0), gs)
        qt, kt, vt, tab, tab2 = _attn_operands(q_ref, kp_ref, k_ref, vp_ref, v_ref, tp_ref, t_ref)
        firsts = [(j, first) for j in range(N_KV_HEADS) for first in range(j * group, (j + 1) * group, gs)]
        ks = [_rope(_head(kt, j), tab2).astype(BF16) for j in range(N_KV_HEADS)]
        scores = [_dot_tn(ks[j], _group_heads(qt, first, gs, tab)) for j, first in firsts]
        soft = [_softmax(s, _sink_row(s_ref, first, gs), mask) for s, (j, first) in zip(scores, firsts)]
        outs, ms, invs = [], [], []
        for (e, m, inv), (j, first) in zip(soft, firsts):
            o = _dot(_head(vt, j).astype(BF16), e) * inv
            outs += [o[:, g * BLOCK:(g + 1) * BLOCK] for g in range(gs)]
            ms += _per_head(m, gs)
            invs += _per_head(inv, gs)
        o_ref[...] = jnp.concatenate(outs, axis=0).T.astype(BF16)
        stat_ref[0] = jnp.concatenate(ms, axis=0)
        stat_ref[1] = jnp.concatenate(invs, axis=0)

    return _call(
        body, name=name, grid=(T // BLOCK,),
        in_specs=_attn_specs(D, kvd, T // BLOCK),
        out_specs=[pl.BlockSpec((BLOCK, D), lambda n: (n, 0)), pl.BlockSpec((2, heads, BLOCK), lambda n: (0, 0, n))],
        out_shape=[SDS((T, D), BF16), SDS((2, heads, T), F32)],
        semantics=("parallel",), args=(q, kv, kv, kv, kv, tabs, tabs, sinks), ride=ride)


def attention_bwd(q, kv, tabs, sinks, do, o, stats, *, name, ride=None):
    T, D = q.shape
    kvd = kv.shape[1] // 2
    heads = D // HEAD_DIM
    group = heads // N_KV_HEADS
    nb = T // BLOCK

    def body(q_ref, kp_ref, k_ref, vp_ref, v_ref, tp_ref, t_ref, s_ref, do_ref, o_ref, stat_ref,
             dq_ref, dkv_ref, ds_ref, carry):
        n = pl.program_id(0)

        @pl.when(n == 0)
        def _():
            carry[...] = jnp.zeros_like(carry)

        @pl.when(n < nb)
        def _():
            block(n, q_ref, kp_ref, k_ref, vp_ref, v_ref, tp_ref, t_ref, s_ref, do_ref, o_ref, stat_ref,
                  dq_ref, dkv_ref, ds_ref, carry)

        @pl.when(n == nb)
        def _():
            dkv_ref[...] = carry[...].astype(BF16)

    def block(n, q_ref, kp_ref, k_ref, vp_ref, v_ref, tp_ref, t_ref, s_ref, do_ref, o_ref, stat_ref,
              dq_ref, dkv_ref, ds_ref, carry):
        gs = HEADS_TOGETHER
        mask = _band(n, gs)
        qt, kt, vt, tab, tab2 = _attn_operands(q_ref, kp_ref, k_ref, vp_ref, v_ref, tp_ref, t_ref)
        dot = do_ref[...].astype(F32).T
        odo = o_ref[...].astype(F32).T * dot
        dl_all = jnp.concatenate([jnp.sum(_head(odo, h), axis=0, keepdims=True) for h in range(heads)], axis=0)
        m_all, inv_all = stat_ref[0], stat_ref[1]
        row = lambda t, first: jnp.concatenate([t[first + g:first + g + 1] for g in range(gs)], axis=1)
        lane = lax.broadcasted_iota(jnp.int32, (8, 128), 1)
        dsink = jnp.zeros((8, 128), F32)
        firsts = [(j, first) for j in range(N_KV_HEADS) for first in range(j * group, (j + 1) * group, gs)]
        ks = [_rope(_head(kt, j), tab2).astype(BF16) for j in range(N_KV_HEADS)]
        vs = [_head(vt, j).astype(BF16) for j in range(N_KV_HEADS)]
        qs = [_group_heads(qt, first, gs, tab) for _, first in firsts]
        dos = [_group_heads(dot, first, gs) for _, first in firsts]
        scores = [_dot_tn(ks[j], q) for q, (j, _) in zip(qs, firsts)]
        dps = [_dot_tn(vs[j], do) for do, (j, _) in zip(dos, firsts)]
        ps, dscs = [], []
        for s, dp, (j, first) in zip(scores, dps, firsts):
            m, inv, dl = row(m_all, first), row(inv_all, first), row(dl_all, first)
            e = jnp.exp(jnp.where(mask, s.astype(BF16), NEG) - m.astype(BF16))
            p = e * inv.astype(BF16)
            dscs.append(p * (dp.astype(BF16) - dl.astype(BF16)))
            ps.append(p)
            weight = jnp.exp(_sink_row(s_ref, first, gs) - m) * inv * dl
            for g in range(gs):
                dsink = dsink - jnp.where(lane == first + g, jnp.sum(weight[:, g * BLOCK:(g + 1) * BLOCK]), 0.0)
        dqs = []
        dks = [jnp.zeros((HEAD_DIM, 2 * BLOCK), F32) for _ in range(N_KV_HEADS)]
        dvs = [jnp.zeros((HEAD_DIM, 2 * BLOCK), F32) for _ in range(N_KV_HEADS)]
        for p, dsc, q, do, (j, _) in zip(ps, dscs, qs, dos, firsts):
            dq = _dot(ks[j], dsc) * SCORE_SCALE
            dqs += [_rope_t(dq[:, g * BLOCK:(g + 1) * BLOCK], tab) for g in range(gs)]
            dks[j] = dks[j] + _dot_nt(q, dsc)
            dvs[j] = dvs[j] + _dot_nt(do, p)
        dks = [_rope_t(dk, tab2) for dk in dks]
        dq_ref[...] = jnp.concatenate(dqs, axis=0).T.astype(BF16)
        dkv = jnp.concatenate(dks + dvs, axis=0)
        dkv_ref[...] = (carry[...] + dkv[:, :BLOCK].T).astype(BF16)
        carry[...] = dkv[:, BLOCK:].T
        _accumulate(ds_ref, n == 0, dsink)

    cur = lambda n: jnp.minimum(n, nb - 1)
    blk = lambda w: pl.BlockSpec((BLOCK, w), lambda n: (cur(n), 0))
    return _call(
        body, name=name, grid=(nb + 1,),
        in_specs=_attn_specs(D, kvd, nb) + [blk(D), blk(D), pl.BlockSpec((2, heads, BLOCK), lambda n: (0, 0, cur(n)))],
        out_specs=[blk(D), pl.BlockSpec((BLOCK, 2 * kvd), lambda n: (jnp.maximum(n - 1, 0), 0)),
                   pl.BlockSpec((8, 128), lambda n: (0, 0))],
        out_shape=[SDS((T, D), BF16), SDS((T, 2 * kvd), BF16), SDS((8, 128), F32)],
        scratch_shapes=[pltpu.VMEM((BLOCK, 2 * kvd), F32)],
        semantics=("arbitrary",), args=(q, kv, kv, kv, kv, tabs, tabs, sinks, do, o, stats), ride=ride)


def matmul_nt_normbwd(da, w, h_in, g, dh_out, *, name, ride=None, tm=ROW_TILE):
    T, D = h_in.shape
    S, _, K = da.shape

    def body(*refs):
        da_refs, w_refs = refs[:S], refs[S:2 * S]
        h_ref, g_ref, dh_ref, o_ref, dg_ref = refs[2 * S:]
        subs = _sub_tiles(tm)
        dns = []
        for rows in subs:
            dn = _dot_nt(da_refs[0][rows, :], w_refs[0][...])
            for s in range(1, S):
                dn = dn + _dot_nt(da_refs[s][rows, :], w_refs[s][...])
            dns.append(dn)
        dg = jnp.zeros((1, D), F32)
        for rows, dn in zip(subs, dns):
            dx, hh = _rmsnorm_bwd(h_ref[rows, :].astype(F32), g_ref[...], dn)
            o_ref[rows, :] = dh_ref[rows, :] + dx
            dg = dg + jnp.sum(dn * hh, axis=0, keepdims=True)
        _accumulate(dg_ref, pl.program_id(0) == 0, dg)

    row = pl.BlockSpec((tm, D), lambda i: (i, 0))
    vec = pl.BlockSpec((1, D), lambda i: (0, 0))
    part = lambda s: pl.BlockSpec((None, tm, K), lambda i: (s, i, 0))
    cols = lambda s: pl.BlockSpec((D, K), lambda i: (0, s), pipeline_mode=pl.Buffered(1))
    return _call(
        body, name=name, grid=(T // tm,),
        in_specs=[part(s) for s in range(S)] + [cols(s) for s in range(S)] + [row, vec, row],
        out_specs=[row, vec],
        out_shape=[SDS((T, D), F32), SDS((1, D), F32)],
        semantics=("arbitrary",), args=[da] * S + [w] * S + [h_in, g, dh_out], ride=ride)


def matmuls_nt_normbwd(das, ws, h_in, gs, dh_out, then, *, name, ride=None, tm=ROW_TILE):
    T, D = h_in.shape
    tm = min(tm, T)
    n = len(das)

    def body(*refs):
        da_refs, w_refs, g_refs = refs[:n], refs[n:2 * n], refs[2 * n:3 * n]
        h_ref, dh_ref, z_ref, g2_ref, w2_ref, o_ref = refs[3 * n:3 * n + 6]
        dg_refs, (dz_ref, dg2_ref, da_ref) = refs[3 * n + 6:4 * n + 6], refs[4 * n + 6:]
        first = pl.program_id(0) == 0
        subs = _sub_tiles(tm)
        dns = [[_dot_nt(da_ref_[rows, :], w_ref[...]) for da_ref_, w_ref in zip(da_refs, w_refs)] for rows in subs]
        dgs, dg2 = [jnp.zeros((1, D), F32) for _ in range(n)], jnp.zeros((1, D), F32)
        for rows, dn_sub in zip(subs, dns):
            hf = h_ref[rows, :].astype(F32)
            r = _rms_r(hf)
            hh = hf * r
            total = dh_ref[rows, :].astype(F32)
            for b, (dn, g_ref) in enumerate(zip(dn_sub, g_refs)):
                gd = g_ref[...] * dn
                total = total + r * (gd - hh * jnp.mean(hh * gd, axis=-1, keepdims=True))
                dgs[b] = dgs[b] + jnp.sum(dn * hh, axis=0, keepdims=True)
            o_ref[rows, :] = total.astype(STREAM)
            dz, zh = _rmsnorm_bwd(z_ref[rows, :].astype(F32), g2_ref[...], total)
            dz = dz.astype(BF16)
            dz_ref[rows, :] = dz
            dg2 = dg2 + jnp.sum(total * zh, axis=0, keepdims=True)
            da_ref[rows, :] = _dot_nt(dz, w2_ref[...]).astype(BF16)
        for dg_ref, dg in zip(dg_refs + (dg2_ref,), dgs + [dg2]):
            _accumulate(dg_ref, first, dg)

    row = pl.BlockSpec((tm, D), lambda i: (i, 0))
    vec = pl.BlockSpec((1, D), lambda i: (0, 0))
    then_in, then_out, then_shape = _then_specs(then, tm, T, D)
    return _call(
        body, name=name, grid=(T // tm,),
        in_specs=[pl.BlockSpec((tm, da.shape[1]), lambda i: (i, 0)) for da in das]
        + [pl.BlockSpec(w.shape, lambda i: (0, 0)) for w in ws] + [vec] * n + [row, row] + then_in,
        out_specs=[row] + [vec] * n + then_out,
        out_shape=[SDS((T, D), STREAM)] + [SDS((1, D), F32)] * n + then_shape,
        semantics=("arbitrary",), args=list(das) + list(ws) + list(gs) + [h_in, dh_out] + list(then), ride=ride)


def matmul_tn(a, b, *, tb, name, ride=None, ta=MXU_WIDTH):
    T, Ka = a.shape
    S, _, Nb = b.shape
    per = Nb // tb

    def body(a_ref, b_ref, o_ref):
        o_ref[...] = _dot_tn(a_ref[...], b_ref[...]).astype(BF16)

    out = _call(
        body, name=name, grid=(S * per, Ka // ta),
        in_specs=[pl.BlockSpec((T, ta), lambda j, i: (0, i)),
                  pl.BlockSpec((None, T, tb), lambda j, i: (j // per, 0, j % per))],
        out_specs=[pl.BlockSpec((ta, tb), lambda j, i: (i, j))],
        out_shape=[SDS((Ka, S * Nb), BF16)],
        semantics=("parallel", "parallel"), args=(a, b), ride=ride)
    return out[0] if ride is None else (out[0][0], out[1])


def conv_bwd(dy, bcx, conv_w, *, name, ride=None, tm=ROW_TILE):
    T, D = dy.shape
    nt = T // tm
    hb = tm // BF16_ROWS
    last = T // BF16_ROWS - 1

    def body(dy_ref, dyn_ref, b_ref, bn_ref, c_ref, u_ref, cp_ref, up_ref, cw_ref, o_ref, dw_ref):
        i = pl.program_id(0)
        c, u = c_ref[...].astype(F32), u_ref[...].astype(F32)
        cu = c * u
        cup = jnp.where(i == 0, 0.0, cp_ref[...].astype(F32) * up_ref[...].astype(F32))
        cu1, cu2 = _shift_down(cup, cu, 1), _shift_down(cup, cu, 2)
        w0, w1, w2 = cw_ref[0:1, :], cw_ref[1:2, :], cw_ref[2:3, :]
        dyf = dy_ref[...].astype(F32)
        o_ref[:, 0:D] = (dyf * (w0 * cu2 + w1 * cu1 + w2 * cu)).astype(BF16)
        dcv = dyf * b_ref[...].astype(F32)
        dcvn = jnp.where(i == nt - 1, 0.0, dyn_ref[...].astype(F32) * bn_ref[...].astype(F32))
        dcu = w2 * dcv + w1 * _shift_up(dcv, dcvn, 1) + w0 * _shift_up(dcv, dcvn, 2)
        o_ref[:, D:2 * D] = (dcu * u).astype(BF16)
        o_ref[:, 2 * D:3 * D] = (dcu * c).astype(BF16)
        row = lax.broadcasted_iota(jnp.int32, (8, D), 0)
        dw = jnp.zeros((8, D), F32)
        for tap, t in enumerate((cu2, cu1, cu)):
            dw = jnp.where(row == tap, jnp.sum(dcv * t, axis=0, keepdims=True), dw)
        _accumulate(dw_ref, i == 0, dw)

    tile = lambda col: pl.BlockSpec((tm, D), lambda i: (i, col))
    prev = lambda col: pl.BlockSpec((BF16_ROWS, D), lambda i: (jnp.maximum(i * hb - 1, 0), col))
    nxt = lambda col: pl.BlockSpec((BF16_ROWS, D), lambda i: (jnp.minimum((i + 1) * hb, last), col))
    return _call(
        body, name=name, grid=(nt,),
        in_specs=[tile(0), nxt(0), tile(0), nxt(0), tile(1), tile(2), prev(1), prev(2),
                  pl.BlockSpec((3, D), lambda i: (0, 0))],
        out_specs=[pl.BlockSpec((tm, 3 * D), lambda i: (i, 0)), pl.BlockSpec((8, D), lambda i: (0, 0))],
        out_shape=[SDS((T, 3 * D), BF16), SDS((8, D), F32)],
        semantics=("arbitrary",), args=(dy, dy, bcx, bcx, bcx, bcx, bcx, bcx, conv_w), ride=ride)


class NoTraffic:
    def ride(self, kernel_name):
        return None

    def landed(self, kernel_name, results, wts):
        pass

    def grad(self, key, value):
        pass


def local_step(x, target, wts, vec, traffic):
    T, D = x.shape
    tabs = rope_tables(T)
    small = {}

    def run(builder, *args, name, **kw):
        ride = traffic.ride(name)
        if ride is None:
            return builder(*args, name=name, **kw)
        out, extra = builder(*args, name=name, ride=ride, **kw)
        traffic.landed(name, extra, wts)
        return out

    bcx, xn1 = run(norm_matmul, x, vec["a_pre"], wts["w_in"], tn=3 * D, split=1, name="a_in")
    bcx = bcx[0]
    h1, z0, y0 = run(conv_mix_out, bcx, vec["conv_w"], wts["w_out"], vec["a_post"], x, name="a_out")
    gu0, act0, xt2 = run(norm_swiglu_in, h1, vec["ffn_pre0"], wts["gu0"], name="ffn0_in")
    h2, z1 = run(plain_mix_out, act0, wts["wd0"], vec["ffn_post0"], h1, name="ffn0_out")
    kvp, xkv, qp, xq = run(norm2_matmul, h2, [vec["kv_norm"], vec["b_pre"]], [wts["w_kv"], wts["w_q"]],
                           name="kvq_in")
    attn, attn_stats = run(attention_fwd, qp, kvp, tabs, vec["sinks"], name="attn_fwd")
    h3, z2 = plain_mix_out(attn, wts["w_o"], vec["b_post"], h2, name="attn_out", tm=BIG_ROW_TILE)
    gu1, act1, xt3 = run(norm_swiglu_in, h3, vec["ffn_pre1"], wts["gu1"], name="ffn1_in")
    dy, dz3, small["ffn_post1"], dact1, loss = plain_mix_out(act1, wts["wd1"], vec["ffn_post1"], h3, name="ffn1_out",
                                                             target=target)

    def ffn_bwd(layer, dz, dact, gu, act, xt, h_in, dh, then, gu_first):
        tag = "ffn%d" % layer
        dwd = lambda: traffic.grad("wd%d" % layer, run(matmul_tn, act, dz[None], tb=D, name=tag + "_dwd"))
        dwgu = lambda: traffic.grad("gu%d" % layer, run(swiglu_bwd_tn, xt, dact, gu, name=tag + "_dwgu"))
        for step in ((dwgu, dwd) if gu_first else (dwd, dwgu)):
            step()
        dh_in, small["ffn_pre%d" % layer], dz_, dg_, da_ = run(
            swiglu_bwd_in, dact, gu, wts["gu%d" % layer], h_in, vec["ffn_pre%d" % layer], dh, then,
            name=tag + "_in_bwd")
        return dh_in, dz_, dg_, da_

    dh3, dz2, small["b_post"], dattn = ffn_bwd(1, dz3, dact1, gu1, act1, xt3, h3, dy,
                                               (z2, vec["b_post"], wts["w_o"]), gu_first=False)
    traffic.grad("w_o", matmul_tn(attn, dz2[None], tb=D, name="attn_dwo"))
    dq, dkv, small["sinks"] = run(attention_bwd, qp, kvp, tabs, vec["sinks"], dattn, attn, attn_stats,
                                  name="attn_bwd")
    traffic.grad("w_q", matmul_tn(xq, dq[None], tb=D, name="attn_dwq"))
    traffic.grad("w_kv", matmul_tn(xkv, dkv[None], tb=dkv.shape[1], name="attn_dwkv"))
    dh2, small["b_pre"], small["kv_norm"], dz1, small["ffn_post0"], dact0 = run(
        matmuls_nt_normbwd, [dq, dkv], [wts["w_q"], wts["w_kv"]], h2, [vec["b_pre"], vec["kv_norm"]], dh3,
        (z1, vec["ffn_post0"], wts["wd0"]), name="qkv_in_bwd")
    dh1, dz0, small["a_post"], dyc = ffn_bwd(0, dz1, dact0, gu0, act0, xt2, h1, dh2,
                                             (z0, vec["a_post"], wts["w_out"]), gu_first=True)
    traffic.grad("w_out", run(matmul_tn, y0, dz0[None], tb=D, name="a_dwout"))
    dbcx, small["conv_w"] = run(conv_bwd, dyc, bcx, vec["conv_w"], name="a_conv_bwd")
    traffic.grad("w_in", run(matmul_tn, xn1, dbcx[None], tb=3 * D // 2, name="a_dwin"))
    dx, small["a_pre"] = run(matmul_nt_normbwd, dbcx[None], wts["w_in"], x, vec["a_pre"], dh1, name="a_in_bwd")
    return loss, dx, small


SMALL_ROWS = 16
LOSS_ROW = 13

WHOLE = None
GATHER_PLAN = {"cast_rest": [("w_in", WHOLE)],
               "a_in": [("w_out", WHOLE), ("gu0", (0, 18))],
               "a_out": [("gu0", (18, 14))],
               "ffn0_in": [("wd0", WHOLE), ("w_kv", WHOLE), ("w_q", WHOLE)],
               "ffn0_out": [("w_o", WHOLE), ("gu1", (0, 10))],
               "attn_fwd": [("gu1", (10, 22))],
               "ffn1_in": [("wd1", WHOLE)]}
PAIR_PLAN = {"ffn1_dwgu": ["wd1"], "ffn1_in_bwd": ["gu1"], "attn_bwd": ["w_o"], "qkv_in_bwd": ["w_q", "w_kv"],
             "ffn0_dwd": ["gu0"], "ffn0_in_bwd": ["wd0"], "a_conv_bwd": ["w_out"], "chip_reduce_early": ["w_in"]}
CHIP_PLAN = {"ffn1_in_bwd": [("wd1", WHOLE)], "attn_bwd": [("gu1", WHOLE)],
             "ffn0_dwgu": [("w_o", WHOLE), ("w_q", WHOLE), ("w_kv", WHOLE)],
             "ffn0_in_bwd": [("gu0", WHOLE)], "a_conv_bwd": [("wd0", (0, 12))],
             "a_dwin": [("wd0", (12, 10)), ("w_out", WHOLE)], "a_in_bwd": [("w_in", WHOLE)]}
HALF_PLAN = {"a_in_bwd": ["gu0", "gu1", "wd0", "wd1", "w_kv", "w_q", "w_o", "w_out"]}
GRAD_KIND = dict(KIND, gu0="split", gu1="split")


class Traffic:
    def __init__(self, wholes, quarter, c_arr, pc_arr):
        self.wholes, self.quarter, self.c_arr, self.pc_arr = wholes, quarter, c_arr, pc_arr
        self.views, self.sums, self.got = {}, {}, {}
        self.reduced = {}
        self.stages = {}

    def reduce(self, keys, name):
        args = ([self.sums[k] for k in keys], [self.got[k] for k in keys], [GRAD_KIND[k] for k in keys], self.pc_arr)
        if name not in PAIR_PLAN:
            return chip_reduce(*args, name=name)
        pairs = PAIR_PLAN[name]
        out, got = chip_reduce(*args, name=name, ride=pair_ride([self.views[k] for k in pairs]))
        self.pair_sums(pairs, got)
        return out

    def pair_sums(self, keys, got):
        for k, theirs in zip(keys, got):
            self.sums[k] = pair_add(self.views[k], theirs, self.c_arr, name="pair_add_" + k)

    def ride(self, name, small=None):
        rides, stages = [], []
        if name in GATHER_PLAN:
            plan = GATHER_PLAN[name]
            rides.append(gather_ride([self.wholes[k] for k, _ in plan],
                                     [(KIND[k], self.quarter[k], part) for k, part in plan], small))
            stages.append(("gather", [k for k, _ in plan]))
        if name in HALF_PLAN:
            keys = HALF_PLAN[name]
            rides.append(half_ride(self.reduce(keys, "chip_reduce_early")))
            stages.append(("half", keys))
        if name in CHIP_PLAN:
            plan = CHIP_PLAN[name]
            rides.append(chip_ride([self.sums[k] for k, _ in plan],
                                   [(GRAD_KIND[k], self.quarter[k], part) for k, part in plan],
                                   earlier=[self.got.get(k) for k, _ in plan]))
            stages.append(("chip", [k for k, _ in plan]))
        if name in PAIR_PLAN:
            keys = PAIR_PLAN[name]
            rides.append(pair_ride([self.views[k] for k in keys]))
            stages.append(("pair", keys))
        self.stages[name] = stages
        return join(rides)

    def landed(self, name, results, wts):
        results = list(results)
        for stage, keys in self.stages[name]:
            mine, results = results[:len(keys)], results[len(keys):]
            if stage == "gather":
                for k, whole in zip(keys, mine):
                    self.wholes[k] = wts[k] = whole
            elif stage == "chip":
                self.got.update(zip(keys, mine))
            elif stage == "half":
                self.reduced.update(zip(keys, mine))
            else:
                self.pair_sums(keys, mine)

    def grad(self, key, value):
        r, ws = self.quarter[key]
        view = {"row": (N_CHIPS, 2, r // 2, ws), "col": (1, 2, r // 2, N_CHIPS * ws), "split": (2, 2, r // 2, 2 * ws)}
        self.views[key] = value.reshape(view[GRAD_KIND[key]])


def kernel(x, a_pre_norm, a_w_in, a_conv_w, a_w_out, a_post_norm, ffn_pre_norm, ffn_w_gate_up, ffn_w_down, ffn_post_norm, kv_norm, w_kv, b_pre_norm, b_w_q, b_sinks, b_w_o, b_post_norm, loss_target, m_a_pre_norm, m_a_w_in, m_a_conv_w, m_a_w_out, m_a_post_norm, m_ffn_pre_norm, m_ffn_w_gate_up, m_ffn_w_down, m_ffn_post_norm, m_kv_norm, m_w_kv, m_b_pre_norm, m_b_w_q, m_b_sinks, m_b_w_o, m_b_post_norm, v_a_pre_norm, v_a_w_in, v_a_conv_w, v_a_w_out, v_a_post_norm, v_ffn_pre_norm, v_ffn_w_gate_up, v_ffn_w_down, v_ffn_post_norm, v_kv_norm, v_w_kv, v_b_pre_norm, v_b_w_q, v_b_sinks, v_b_w_o, v_b_post_norm):
    T, D = x.shape[1], x.shape[2]
    xi, yi, ci = _place()
    p = 2 * xi + yi
    p_arr = jnp.reshape(p, (1,)).astype(jnp.int32)
    c_arr = jnp.reshape(ci, (1,)).astype(jnp.int32)
    pc_arr = jnp.stack([p, ci]).astype(jnp.int32)
    me_arr = jnp.reshape(4 * xi + 2 * yi + ci, (1,)).astype(jnp.int32)
    qd = D // N_CHIPS

    big = {"w_in": (a_w_in, 0), "w_out": (a_w_out, 0), "gu0": (ffn_w_gate_up, 0), "gu1": (ffn_w_gate_up, 1),
           "wd0": (ffn_w_down, 0), "wd1": (ffn_w_down, 1), "w_kv": (w_kv[None], 0), "w_q": (b_w_q, 0),
           "w_o": (b_w_o, 0)}
    names = list(big)
    quarter = {k: w.shape[1:] for k, (w, _) in big.items()}
    source = lambda k: big[k] + (KIND[k],)
    traffic = Traffic(dict(zip(names[:1], cast_quarters([source(names[0])], p_arr, name="cast_first"))), quarter,
                      c_arr, pc_arr)
    small_shard = jnp.concatenate([a_pre_norm, a_post_norm, a_conv_w[0], jnp.zeros((3, qd), F32)], axis=0)
    wts = {}
    rest, (*landed, small_full) = cast_quarters([source(k) for k in names[1:]], p_arr, name="cast_rest",
                                                ride=traffic.ride("cast_rest", small_shard))
    traffic.wholes.update(zip(names[1:], rest))
    traffic.landed("cast_rest", landed, wts)
    rows = lambda k: jnp.transpose(small_full[:, k], (1, 0, 2)).reshape(-1, D)
    vec = {"a_pre": rows(slice(0, 1)), "a_post": rows(slice(1, 2)), "conv_w": rows(slice(2, 5)),
           "ffn_pre0": ffn_pre_norm[0:1], "ffn_pre1": ffn_pre_norm[1:2],
           "ffn_post0": ffn_post_norm[0:1], "ffn_post1": ffn_post_norm[1:2],
           "kv_norm": kv_norm[None], "b_pre": b_pre_norm, "b_post": b_post_norm, "sinks": b_sinks}

    loss, dx, small = local_step(x[0], loss_target[0], wts, vec, traffic)

    pad = lambda a: jnp.pad(a, ((0, 0), (0, D - a.shape[1])))
    small_block = jnp.concatenate(
        [small["a_pre"], small["a_post"], small["conv_w"][0:3], small["ffn_pre0"], small["ffn_pre1"],
         small["ffn_post0"], small["ffn_post1"], small["kv_norm"], small["b_pre"], small["b_post"],
         pad(small["sinks"][0:1]), pad(loss[0:1]), jnp.zeros((SMALL_ROWS - LOSS_ROW - 1, D), F32)], axis=0)
    late = [k for k in names if k not in traffic.reduced]
    *swapped, small_blocks = alone(join([half_ride(traffic.reduce(late, "chip_reduce_late")),
                                         chip_ride([], [], small_block)]), name="last_exchange")
    traffic.reduced.update(zip(late, swapped))
    grad = {k: traffic.reduced[k].reshape(quarter[k]) for k in names}
    small_sum = small_reduce(small_blocks, me_arr)

    out = {}
    out["a_w_in"] = adamw(a_w_in, [grad["w_in"]], m_a_w_in, v_a_w_in, name="adamw_a_w_in")
    out["a_w_out"] = adamw(a_w_out, [grad["w_out"]], m_a_w_out, v_a_w_out, name="adamw_a_w_out")
    out["ffn_w_gate_up"] = adamw(ffn_w_gate_up, [grad["gu0"], grad["gu1"]], m_ffn_w_gate_up, v_ffn_w_gate_up,
                                 name="adamw_ffn_w_gate_up")
    out["ffn_w_down"] = adamw(ffn_w_down, [grad["wd0"], grad["wd1"]], m_ffn_w_down, v_ffn_w_down,
                              name="adamw_ffn_w_down")
    out["w_kv"] = [o[0] for o in adamw(w_kv[None], [grad["w_kv"]], m_w_kv[None], v_w_kv[None], name="adamw_w_kv")]
    out["b_w_q"] = adamw(b_w_q, [grad["w_q"]], m_b_w_q, v_b_w_q, name="adamw_b_w_q")
    out["b_w_o"] = adamw(b_w_o, [grad["w_o"]], m_b_w_o, v_b_w_o, name="adamw_b_w_o")

    leaves = {"a_pre_norm": (a_pre_norm, m_a_pre_norm, v_a_pre_norm, 0, True),
              "a_post_norm": (a_post_norm, m_a_post_norm, v_a_post_norm, 1, True),
              "a_conv_w": (a_conv_w, m_a_conv_w, v_a_conv_w, 2, True),
              "ffn_pre_norm": (ffn_pre_norm, m_ffn_pre_norm, v_ffn_pre_norm, 5, False),
              "ffn_post_norm": (ffn_post_norm, m_ffn_post_norm, v_ffn_post_norm, 7, False),
              "kv_norm": (kv_norm[None], m_kv_norm[None], v_kv_norm[None], 9, False),
              "b_pre_norm": (b_pre_norm, m_b_pre_norm, v_b_pre_norm, 10, False),
              "b_post_norm": (b_post_norm, m_b_post_norm, v_b_post_norm, 11, False),
              "b_sinks": (b_sinks, m_b_sinks, v_b_sinks, 12, False)}
    for k, results in zip(leaves, adamw_rows(small_sum, p_arr, list(leaves.values()), name="adamw_small")):
        out[k] = [r[0] for r in results] if k == "kv_norm" else results

    order = ["a_pre_norm", "a_w_in", "a_conv_w", "a_w_out", "a_post_norm", "ffn_pre_norm", "ffn_w_gate_up",
             "ffn_w_down", "ffn_post_norm", "kv_norm", "w_kv", "b_pre_norm", "b_w_q", "b_sinks", "b_w_o",
             "b_post_norm"]
    return (small_sum[LOSS_ROW, 0], dx[None], *[out[k][0] for k in order], *[out[k][1] for k in order],
            *[out[k][2] for k in order], *[out[k][3] for k in order])
```

```python
import math

import jax
import jax.numpy as jnp
from jax import lax
from jax.experimental import pallas as pl
from jax.experimental.pallas import tpu as pltpu

F32 = jnp.float32
BF16 = jnp.bfloat16
SDS = jax.ShapeDtypeStruct
MESH = pl.DeviceIdType.MESH
DMA = pltpu.SemaphoreType.DMA
HBM_SPEC = pl.BlockSpec(memory_space=pltpu.HBM)

EPS = 1e-6
NEG = -1e30
HEAD_DIM = 64
N_KV_HEADS = 4
BLOCK = 128
ROT_DIM = HEAD_DIM // 4
ROPE_THETA = 500000.0
N_CHIPS = 4

ADAM_LR = 0.001
ADAM_B1 = 0.9
ADAM_B2 = 0.999
ADAM_EPS = 1e-08
ADAM_WD = 0.01
ADAM_STEP = 10

VMEM_LIMIT_BYTES = 52 * 1024 * 1024
ROW_TILE = 512
BF16_ROWS = 16
STREAM = BF16
MXU_WIDTH = 256

KIND = {"w_in": "col", "gu0": "col", "gu1": "col", "w_out": "row", "wd0": "row", "wd1": "row", "w_kv": "row",
        "w_q": "row", "w_o": "row"}


def _params(*semantics):
    return pltpu.CompilerParams(dimension_semantics=semantics, vmem_limit_bytes=VMEM_LIMIT_BYTES)


def _row_tile(rows, limit, step=8):
    return max(t for t in range(step, limit + 1, step) if rows % t == 0)


def _place():
    return lax.axis_index("x"), lax.axis_index("y"), lax.axis_index("c")


def _other_chips(x, y):
    return [(1 - x, y), (x, 1 - y), (1 - x, 1 - y)]


def _remote(src, dst, send_sem, recv_sem, to):
    return pltpu.make_async_remote_copy(src_ref=src, dst_ref=dst, send_sem=send_sem, recv_sem=recv_sem,
                                        device_id=to, device_id_type=MESH)


def _full_shape(kind, quarter):
    r, ws = quarter
    return (N_CHIPS * r, ws) if kind == "row" else (r, N_CHIPS * ws)


def _rows_of(h, part):
    lo, n = (0, h) if part is None else (part[0] * BF16_ROWS, part[1] * BF16_ROWS)
    assert lo + n <= h, (h, part)
    return lo, n


def _half_of_quarter(ref, kind, quarter, part, q, half):
    r, ws = quarter
    h = r // 2
    lo, n = _rows_of(h, part)
    if kind == "row":
        return ref.at[pl.ds(pl.multiple_of(q * r + half * h + lo, BF16_ROWS), n)]
    return ref.at[pl.ds(pl.multiple_of(half * h + lo, BF16_ROWS), n), pl.ds(pl.multiple_of(q * ws, 128), ws)]


class Ride:
    def __init__(self, operands, out_shape, aliases, sems, make):
        self.operands, self.out_shape, self.aliases, self.sems, self.make = operands, out_shape, aliases, sems, make

    def stages(self, ins, outs, sems):
        made = self.make(ins, outs, sems)
        return made if len(made) == 4 else (made[0], None, None, made[1])


def join(rides):
    rides = [r for r in rides if r is not None]
    if len(rides) < 2:
        return rides[0] if rides else None
    aliases, at = {}, [0, 0, 0]
    cuts = []
    for r in rides:
        aliases.update({at[0] + i: at[1] + o for i, o in r.aliases.items()})
        cuts.append(tuple(at))
        at = [at[0] + len(r.operands), at[1] + len(r.out_shape), at[2] + len(r.sems)]
    cuts.append(tuple(at))

    def make(ins, outs, sem):
        made = [r.stages(ins[lo[0]:hi[0]], outs[lo[1]:hi[1]], sem[lo[2]:hi[2]]) for r, lo, hi in zip(rides, cuts, cuts[1:])]
        def all_of(k):
            def stage():
                for m in made:
                    if m[k] is not None:
                        m[k]()
            return stage

        if all(m[1] is None for m in made):
            return all_of(0), all_of(3)
        return all_of(0), all_of(1), all_of(2), all_of(3)

    return Ride(sum((list(r.operands) for r in rides), []), sum((list(r.out_shape) for r in rides), []), aliases,
                sum((list(r.sems) for r in rides), []), make)


def _call(body, *, name, grid, in_specs, out_specs, out_shape, args, scratch_shapes=(), semantics=None, ride=None,
          prefetch=None, alias=None):
    pre = 0 if prefetch is None else 1
    n_in, n_out, n_scr = len(in_specs), len(out_specs), len(scratch_shapes)
    r_in, r_out = (len(ride.operands), len(ride.out_shape)) if ride is not None else (0, 0)
    a, b = pre + n_in, pre + n_in + r_in
    c, d = b + n_out, b + n_out + r_out
    e = d + n_scr

    def riding(*refs):
        start, relay, relay_again, finish = ride.stages(refs[a:b], refs[c:d], refs[e:])
        step, steps = pl.program_id(0), 1
        for k, extent in enumerate(grid):
            step = pl.program_id(k) if k == 0 else step * extent + pl.program_id(k)
            steps *= extent
        pl.when(step == 0)(start)
        if relay is not None:
            pl.when(step == steps // 2)(relay)
            pl.when(step == steps - 1)(relay_again)
        body(*refs[:a], *refs[b:c], *refs[d:e])
        pl.when(step == steps - 1)(finish)

    if ride is None:
        kernel_body, extra_in, extra_out, extra_shape, extra_scr, aliases = body, [], [], [], [], {}
        params = _params(*semantics)
    else:
        kernel_body, extra_in, extra_out = riding, [HBM_SPEC] * r_in, [HBM_SPEC] * r_out
        extra_shape, extra_scr = list(ride.out_shape), list(ride.sems)
        aliases = {pre + n_in + i: n_out + o for i, o in ride.aliases.items()}
        params = _params(*(("arbitrary",) * len(grid)))
    aliases.update({pre + i: o for i, o in (alias or {}).items()})
    specs = dict(grid=grid, in_specs=list(in_specs) + extra_in, out_specs=list(out_specs) + extra_out,
                 scratch_shapes=list(scratch_shapes) + extra_scr)
    if prefetch is not None:
        specs = dict(grid_spec=pltpu.PrefetchScalarGridSpec(num_scalar_prefetch=1, **specs))
        args = (prefetch,) + tuple(args)
    outs = pl.pallas_call(kernel_body, name=name, out_shape=list(out_shape) + extra_shape,
                          input_output_aliases=aliases, compiler_params=params, **specs,
                          )(*args, *(ride.operands if ride is not None else ()))
    return outs if ride is None else (outs[:n_out], outs[n_out:])


def alone(ride, *, name):
    def body(*refs):
        n = len(ride.operands)
        stages = ride.stages(refs[:n], refs[n:n + len(ride.out_shape)], refs[n + len(ride.out_shape):])
        for stage in stages:
            if stage is not None:
                stage()

    return pl.pallas_call(
        body, name=name, in_specs=[HBM_SPEC] * len(ride.operands), out_specs=[HBM_SPEC] * len(ride.out_shape),
        out_shape=list(ride.out_shape), input_output_aliases=dict(ride.aliases), scratch_shapes=list(ride.sems),
    )(*ride.operands)


def _two_pieces(h, part):
    lo, n = (0, h // BF16_ROWS) if part is None else part
    assert n >= 2, (h, part)
    return (lo, n // 2), (lo + n // 2, n - n // 2)


def gather_ride(wholes, metas, small=None):
    n = len(wholes)
    operands, out_shape = list(wholes), [SDS(s.shape, s.dtype) for s in wholes]
    sems = [DMA((n, 4)), DMA((n, 4)), DMA((n, 4)), DMA((n, 4))]
    if small is not None:
        operands.append(small)
        out_shape.append(SDS((N_CHIPS,) + small.shape, small.dtype))
        sems += [DMA((3,)), DMA((3,)), DMA(())]

    def make(ins, outs, sem):
        send1, recv1, send2, recv2 = sem[:4]
        x, y, c = _place()
        p = 2 * x + y
        chips = _other_chips(x, y)
        across_x, across_y, across_both = [2 * qx + qy for qx, qy in chips]
        me, sibling = (x, y, c), (x, y, 1 - c)

        def region(t, q, half, piece=None):
            kind, quarter, part, _ = metas[t]
            if piece is not None:
                part = _two_pieces(quarter[0] // 2, part)[piece]
            return _half_of_quarter(outs[t], kind, quarter, part, q, half)

        first, arriving = [], []
        second, landing, passing = [{}, {}], [{}, {}, {}, {}], [{}, {}, {}, {}]
        for j, (qx, qy) in enumerate(chips):
            if small is not None:
                q = 2 * qx + qy
                first.append(_remote(ins[n], outs[n].at[p], sem[4].at[j], sem[5].at[j], (qx, qy, c)))
                arriving.append(_remote(outs[n].at[q], outs[n].at[q], sem[4].at[j], sem[5].at[j], me))
        for t in range(n):
            leg = metas[t][3]
            lands = [(across_x, None), (across_y, None), (across_both, 0), (across_both, 1)]
            for k, (q, piece) in enumerate(lands):
                if leg == ("second" if k < 2 else "first"):
                    continue
                landed, theirs = region(t, q, c, piece), region(t, q, 1 - c, piece)
                landing[k][t] = _remote(landed, landed, send1.at[t, k], recv1.at[t, k], me)
                passing[k][t] = _remote(landed, landed, send2.at[t, k], recv2.at[t, k], sibling)
                arriving.append(_remote(theirs, theirs, send2.at[t, k], recv2.at[t, k], me))
            if leg != "second":
                mine = region(t, p, c)
                for j in range(2):
                    first.append(_remote(mine, mine, send1.at[t, j], recv1.at[t, j], chips[j] + (c,)))
            if leg != "first":
                onward = region(t, across_x, c, 0)
                second[0][t] = _remote(onward, onward, send1.at[t, 2], recv1.at[t, 2], chips[1] + (c,))
                onward = region(t, across_y, c, 1)
                second[1][t] = _remote(onward, onward, send1.at[t, 3], recv1.at[t, 3], chips[0] + (c,))
        local = [] if small is None else [pltpu.make_async_copy(ins[n], outs[n].at[p], sem[6])]
        at_once = [cp for k in range(2) for t, cp in second[k].items() if metas[t][3] == "second"]

        def start():
            for cp in local + first + at_once:
                cp.start()

        def relay():
            for k in range(2):
                for t, got in landing[k].items():
                    got.wait_recv()
                    if t in second[k]:
                        second[k][t].start()
                    passing[k][t].start()

        def relay_again():
            for k in range(2, 4):
                for t, got in landing[k].items():
                    got.wait_recv()
                    passing[k][t].start()

        def finish():
            for cp in arriving:
                cp.wait_recv()
            for cp in first + [cp for group in second + passing for cp in group.values()]:
                cp.wait_send()
            for cp in local:
                cp.wait()

        return start, relay, relay_again, finish

    return Ride(operands, out_shape, {t: t for t in range(n)}, sems, make)


def chip_ride(sums, metas, small=None, earlier=None):
    n = len(sums)
    operands = list(sums)
    out_shape = [SDS((3, s.shape[1], quarter[1]), s.dtype) for s, (_, quarter, _) in zip(sums, metas)]
    sems = [DMA((n, 3)), DMA((n, 3))] if n else []
    if small is not None:
        operands.append(small)
        out_shape.append(SDS((8,) + small.shape, small.dtype))
        sems += [DMA((7,)), DMA((7,)), DMA(())]
    aliases = {}
    for t, buffer in enumerate(earlier or [None] * n):
        if buffer is not None:
            aliases[len(operands)] = t
            operands.append(buffer)

    def make(ins, outs, sem):
        x, y, c = _place()
        cps = []
        for j, (qx, qy) in enumerate(_other_chips(x, y)):
            q = 2 * qx + qy
            for t in range(n):
                kind, (_, ws), part = metas[t]
                rows = pl.ds(*_rows_of(ins[t].shape[1], part))
                if kind == "row":
                    src = ins[t].at[q, rows]
                elif kind == "col":
                    src = ins[t].at[0, rows, pl.ds(pl.multiple_of(q * ws, 128), ws)]
                else:
                    src = ins[t].at[q // 2, rows, pl.ds(pl.multiple_of((q % 2) * ws, 128), ws)]
                cps.append(_remote(src, outs[t].at[j, rows], sem[0].at[t, j], sem[1].at[t, j], (qx, qy, c)))
        local = []
        if small is not None:
            ssend, srecv, lsem = sem[2 * bool(n):2 * bool(n) + 3]
            local.append(pltpu.make_async_copy(ins[n], outs[n].at[0], lsem))
            for k in range(1, 8):
                peer = (x ^ (k >> 2 & 1), y ^ (k >> 1 & 1), c ^ (k & 1))
                cps.append(_remote(ins[n], outs[n].at[k], ssend.at[k - 1], srecv.at[k - 1], peer))

        def start():
            for cp in local + cps:
                cp.start()

        def finish():
            for cp in cps + local:
                cp.wait()

        return start, finish

    return Ride(operands, out_shape, aliases, sems, make)


def pair_ride(grads):
    n = len(grads)

    def make(ins, outs, sem):
        x, y, c = _place()
        cps = [_remote(ins[t].at[:, 1 - c], outs[t], sem[0].at[t], sem[1].at[t], (x, y, 1 - c)) for t in range(n)]

        def start():
            for cp in cps:
                cp.start()

        def finish():
            for cp in cps:
                cp.wait()

        return start, finish

    return Ride(list(grads), [SDS((g.shape[0],) + g.shape[2:], g.dtype) for g in grads], {}, [DMA((n,)), DMA((n,))],
                make)


def half_ride(quarters):
    n = len(quarters)

    def make(ins, outs, sem):
        x, y, c = _place()
        sends = [_remote(outs[t].at[c], outs[t].at[c], sem[0].at[t], sem[1].at[t], (x, y, 1 - c)) for t in range(n)]

        def start():
            for cp in sends:
                cp.start()

        def finish():
            for t in range(n):
                theirs = outs[t].at[1 - c]
                _remote(theirs, theirs, sem[0].at[t], sem[1].at[t], (x, y, c)).wait_recv()
            for cp in sends:
                cp.wait_send()

        return start, finish

    return Ride(list(quarters), [SDS(q.shape, q.dtype) for q in quarters], {t: t for t in range(n)},
                [DMA((n,)), DMA((n,))], make)


CAST_STEPS = 4


def cast_quarters(sources, p_arr, *, name, ride=None):
    n = len(sources)
    in_specs, out_specs, out_shape = [], [], []
    for w, layer, kind in sources:
        _, r, ws = w.shape
        tr = r // CAST_STEPS
        assert tr % BF16_ROWS == 0, w.shape
        in_specs.append(pl.BlockSpec((None, tr, ws), lambda i, p_ref, layer=layer: (layer, i, 0)))
        out_specs.append(pl.BlockSpec((tr, ws), (lambda i, p_ref: (p_ref[0] * CAST_STEPS + i, 0)) if kind == "row"
                                      else (lambda i, p_ref: (i, p_ref[0]))))
        out_shape.append(SDS(_full_shape(kind, (r, ws)), BF16))

    def body(p_ref, *refs):
        for w_ref, o_ref in zip(refs[:n], refs[n:]):
            o_ref[...] = w_ref[...].astype(BF16)

    return _call(body, name=name, grid=(CAST_STEPS,), in_specs=in_specs, out_specs=out_specs, out_shape=out_shape,
                 semantics=("parallel",), args=[w for w, _, _ in sources], ride=ride, prefetch=p_arr)


def pair_add(own, got, c_arr, *, name):
    A, _, h, W = own.shape
    th = _row_tile(h, max(BF16_ROWS, (3 << 19) // W), BF16_ROWS)

    def body(c_ref, a_ref, b_ref, o_ref):
        o_ref[...] = (a_ref[...].astype(F32) + b_ref[...].astype(F32)).astype(BF16)

    return pl.pallas_call(
        body, name=name,
        grid_spec=pltpu.PrefetchScalarGridSpec(
            num_scalar_prefetch=1, grid=(A, h // th),
            in_specs=[pl.BlockSpec((None, None, th, W), lambda q, i, c_ref: (q, c_ref[0], i, 0)),
                      pl.BlockSpec((None, th, W), lambda q, i, c_ref: (q, i, 0))],
            out_specs=pl.BlockSpec((None, th, W), lambda q, i, c_ref: (q, i, 0))),
        out_shape=SDS((A, h, W), BF16),
        compiler_params=_params("parallel", "parallel"),
    )(c_arr, own, got)


REDUCE_STEPS = 2


def chip_reduce(sums, got, kinds, pc_arr, *, name, ride=None):
    n = len(sums)
    mine = {"row": lambda i, pc_ref: (pc_ref[0], i, 0), "col": lambda i, pc_ref: (0, i, pc_ref[0]),
            "split": lambda i, pc_ref: (pc_ref[0] // 2, i, pc_ref[0] % 2)}
    a_specs, b_specs, o_specs, out_shape = [], [], [], []
    for g, kind in zip(got, kinds):
        _, h, ws = g.shape
        th = h // REDUCE_STEPS
        assert th % BF16_ROWS == 0, g.shape
        a_specs.append(pl.BlockSpec((None, th, ws), mine[kind]))
        b_specs.append(pl.BlockSpec((3, th, ws), lambda i, pc_ref: (0, i, 0)))
        o_specs.append(pl.BlockSpec((None, th, ws), lambda i, pc_ref: (pc_ref[1], i, 0)))
        out_shape.append(SDS((2, h, ws), F32))

    def body(pc_ref, *refs):
        for a_ref, b_ref, o_ref in zip(refs[:n], refs[n:2 * n], refs[2 * n:]):
            o_ref[...] = ((a_ref[...].astype(F32) + b_ref[0].astype(F32)) + b_ref[1].astype(F32)) + b_ref[2].astype(F32)

    return _call(body, name=name, grid=(REDUCE_STEPS,), in_specs=a_specs + b_specs, out_specs=o_specs,
                 out_shape=out_shape, semantics=("parallel",), args=list(sums) + list(got), prefetch=pc_arr, ride=ride)


def small_reduce(blocks, me_arr):
    _, rows, D = blocks.shape

    def body(me_ref, b_ref, o_ref):
        me = me_ref[0]
        total = b_ref[me]
        for d in range(1, 8):
            total = total + b_ref[d ^ me]
        o_ref[...] = total

    return pl.pallas_call(
        body, name="small_reduce",
        grid_spec=pltpu.PrefetchScalarGridSpec(
            num_scalar_prefetch=1, grid=(1,),
            in_specs=[pl.BlockSpec((8, rows, D), lambda i, me_ref: (0, 0, 0))],
            out_specs=pl.BlockSpec((rows, D), lambda i, me_ref: (0, 0))),
        out_shape=SDS((rows, D), F32),
        compiler_params=_params("arbitrary"),
    )(me_arr, blocks)


def _adam(w, g, m, v):
    m_new = ADAM_B1 * m + (1.0 - ADAM_B1) * g
    v_new = ADAM_B2 * v + (1.0 - ADAM_B2) * (g * g)
    m_hat = m_new / (1.0 - ADAM_B1 ** ADAM_STEP)
    v_hat = v_new / (1.0 - ADAM_B2 ** ADAM_STEP)
    return -ADAM_LR * (m_hat / (jnp.sqrt(v_hat) + ADAM_EPS) + ADAM_WD * w), m_new, v_new


def adamw_rows(block, p_arr, leaves, *, name):
    L = len(leaves)

    def body(p_ref, b_ref, *refs):
        outs = refs[3 * L:]
        for i, (w, _, _, row, sharded) in enumerate(leaves):
            n, width = w.shape[-2:]
            cols = pl.ds(pl.multiple_of(p_ref[0] * width, 128), width) if sharded else slice(0, width)
            g = b_ref[row:row + n, cols].reshape(w.shape)
            results = (g,) + _adam(refs[i][...], g, refs[L + i][...], refs[2 * L + i][...])
            for o_ref, value in zip(outs[4 * i:4 * i + 4], results):
                o_ref[...] = value

    whole = lambda a: pl.BlockSpec(a.shape, lambda i, p_ref, nd=len(a.shape): (0,) * nd)
    arrays = [leaf[k] for k in range(3) for leaf in leaves]
    shapes = [SDS(leaf[0].shape, F32) for leaf in leaves for _ in range(4)]
    outs = pl.pallas_call(
        body, name=name,
        grid_spec=pltpu.PrefetchScalarGridSpec(
            num_scalar_prefetch=1, grid=(1,), in_specs=[whole(block)] + [whole(a) for a in arrays],
            out_specs=[whole(s) for s in shapes]),
        out_shape=shapes, compiler_params=_params("arbitrary"),
    )(p_arr, block, *arrays)
    return [outs[4 * i:4 * i + 4] for i in range(L)]


def adamw(w, gs, m, v, *, name):
    L, r, cols = w.shape
    tr = _row_tile(r, 256)
    nt = r // tr

    def body(*refs):
        w_ref, m_ref, v_ref = refs[:3]
        g_refs = refs[3:3 + L]
        g_out, d_out, m_out, v_out = refs[3 + L:]
        layer = pl.program_id(0)
        g = g_refs[0][...]
        for l in range(1, L):
            g = jnp.where(layer == l, g_refs[l][...], g)
        g_out[...] = g
        d_out[...], m_out[...], v_out[...] = _adam(w_ref[...], g, m_ref[...], v_ref[...])

    full = pl.BlockSpec((None, tr, cols), lambda l, i: (l, i, 0))
    g_spec = lambda l0: pl.BlockSpec((tr, cols), lambda l, i: (jnp.where(l == l0, i, jnp.where(l < l0, 0, nt - 1)), 0))
    return pl.pallas_call(
        body, name=name, grid=(L, nt),
        in_specs=[full, full, full] + [g_spec(l0) for l0 in range(L)],
        out_specs=[full] * 4,
        out_shape=[SDS(w.shape, F32)] * 4,
        compiler_params=_params("arbitrary", "arbitrary"),
    )(w, m, v, *gs)


def _rms_r(xf):
    return lax.rsqrt(jnp.mean(xf * xf, axis=-1, keepdims=True) + EPS)


def _rmsnorm_bwd(xf, g, dy):
    r = _rms_r(xf)
    xh = xf * r
    gd = g * dy
    return r * (gd - xh * jnp.mean(xh * gd, axis=-1, keepdims=True)), xh


def _dot(a, b):
    return jnp.dot(a, b, preferred_element_type=F32)


def _dot_nt(a, b):
    return lax.dot_general(a, b, (((1,), (1,)), ((), ())), preferred_element_type=F32)


def _dot_tn(a, b):
    return lax.dot_general(a, b, (((0,), (0,)), ((), ())), preferred_element_type=F32)


def _accumulate(ref, first, value):
    @pl.when(first)
    def _():
        ref[...] = value

    @pl.when(jnp.logical_not(first))
    def _():
        ref[...] += value


def norm_matmul(x, g, w, *, tn, split, name, ride=None, tm=ROW_TILE):
    T, D = x.shape
    N = w.shape[1]
    per = N // split // tn

    def body(x_ref, g_ref, w_ref, o_ref, xn_ref):
        @pl.when(pl.program_id(1) == 0)
        def _():
            xf = x_ref[...].astype(F32)
            xn_ref[...] = (xf * _rms_r(xf) * g_ref[...]).astype(BF16)

        o_ref[...] = _dot(xn_ref[...], w_ref[...]).astype(BF16)

    return _call(
        body, name=name, grid=(T // tm, N // tn),
        in_specs=[pl.BlockSpec((tm, D), lambda i, j: (i, 0)),
                  pl.BlockSpec((1, D), lambda i, j: (0, 0)),
                  pl.BlockSpec((D, tn), lambda i, j: (0, j))],
        out_specs=[pl.BlockSpec((None, tm, tn), lambda i, j: (j // per, i, j % per)),
                   pl.BlockSpec((tm, D), lambda i, j: (i, 0))],
        out_shape=[SDS((split, T, N // split), BF16), SDS((T, D), BF16)],
        semantics=("parallel", "arbitrary"), args=(x, g, w), ride=ride)


BIG_ROW_TILE = 1024


def norm2_matmul(x, gains, weights, *, name, ride=None, tm=BIG_ROW_TILE):
    T, D = x.shape
    tm = min(tm, T)
    n = len(gains)

    def body(x_ref, *refs):
        subs = _sub_tiles(tm)
        xhs = []
        for rows in subs:
            xf = x_ref[rows, :].astype(F32)
            xhs.append(xf * _rms_r(xf))
        for g_ref, w_ref, o_ref, xn_ref in zip(refs[:n], refs[n:2 * n], refs[2 * n::2], refs[2 * n + 1::2]):
            for rows, xh in zip(subs, xhs):
                xn = (xh * g_ref[...]).astype(BF16)
                xn_ref[rows, :] = xn
                o_ref[rows, :] = _dot(xn, w_ref[...]).astype(BF16)

    row = pl.BlockSpec((tm, D), lambda i: (i, 0))
    vec = pl.BlockSpec((1, D), lambda i: (0, 0))
    out_specs, out_shape = [], []
    for w in weights:
        out_specs += [pl.BlockSpec((tm, w.shape[1]), lambda i: (i, 0)), row]
        out_shape += [SDS((T, w.shape[1]), BF16), SDS((T, D), BF16)]
    return _call(
        body, name=name, grid=(T // tm,),
        in_specs=[row] + [vec] * n + [pl.BlockSpec(w.shape, lambda i: (0, 0)) for w in weights],
        out_specs=out_specs, out_shape=out_shape, semantics=("parallel",), args=[x] + list(gains) + list(weights),
        ride=ride)


def _shift_down(prev, cur, by):
    big = jnp.concatenate([prev, cur], axis=0)
    return pltpu.roll(big, by, 0)[prev.shape[0]:]


def _shift_up(cur, nxt, by):
    big = jnp.concatenate([cur, nxt], axis=0)
    return pltpu.roll(big, big.shape[0] - by, 0)[:cur.shape[0]]


def conv_mix_out(bcx, conv_w, w_out, g_post, res, *, name, ride=None, tm=ROW_TILE):
    T, D = res.shape
    hb = tm // BF16_ROWS

    def body(b_ref, c_ref, u_ref, cp_ref, up_ref, cw_ref, w_ref, g_ref, r_ref, h_ref, z_ref, y_ref):
        i = pl.program_id(0)
        cu = c_ref[...].astype(F32) * u_ref[...].astype(F32)
        cup = cp_ref[...].astype(F32) * up_ref[...].astype(F32)
        cup = jnp.where(i == 0, 0.0, cup)
        cv = (cw_ref[0:1, :] * _shift_down(cup, cu, 2) + cw_ref[1:2, :] * _shift_down(cup, cu, 1)
              + cw_ref[2:3, :] * cu)
        y = (b_ref[...].astype(F32) * cv).astype(BF16)
        y_ref[...] = y
        z = _dot(y, w_ref[...])
        z_ref[...] = z.astype(BF16)
        h_ref[...] = (r_ref[...] + z * _rms_r(z) * g_ref[...]).astype(STREAM)

    tile = lambda col: pl.BlockSpec((tm, D), lambda i: (i, col))
    halo = lambda col: pl.BlockSpec((BF16_ROWS, D), lambda i: (jnp.maximum(i * hb - 1, 0), col))
    row = pl.BlockSpec((tm, D), lambda i: (i, 0))
    return _call(
        body, name=name, grid=(T // tm,),
        in_specs=[tile(0), tile(1), tile(2), halo(1), halo(2),
                  pl.BlockSpec((3, D), lambda i: (0, 0)),
                  pl.BlockSpec((D, D), lambda i: (0, 0)),
                  pl.BlockSpec((1, D), lambda i: (0, 0)), row],
        out_specs=[row, row, row],
        out_shape=[SDS((T, D), STREAM), SDS((T, D), BF16), SDS((T, D), BF16)],
        semantics=("parallel",), args=(bcx, bcx, bcx, bcx, bcx, conv_w, w_out, g_post, res), ride=ride)


def _normbwd_then_nt(dh, zf, g_ref, w_ref, dz_ref, dg_ref, o_ref, first):
    dz, zh = _rmsnorm_bwd(zf, g_ref[...], dh)
    dz = dz.astype(BF16)
    dz_ref[...] = dz
    _accumulate(dg_ref, first, jnp.sum(dh * zh, axis=0, keepdims=True))
    o_ref[...] = _dot_nt(dz, w_ref[...]).astype(BF16)


def _then_specs(then, tm, T, D):
    z, g, w = then
    K = w.shape[0]
    row = pl.BlockSpec((tm, D), lambda i: (i, 0))
    vec = pl.BlockSpec((1, D), lambda i: (0, 0))
    in_specs = [row, vec, pl.BlockSpec((K, D), lambda i: (0, 0), pipeline_mode=pl.Buffered(1))]
    out_specs = [row, vec, pl.BlockSpec((tm, K), lambda i: (i, 0))]
    out_shape = [SDS((T, D), BF16), SDS((1, D), F32), SDS((T, K), BF16)]
    return in_specs, out_specs, out_shape


def plain_mix_out(a, w, g_post, res, *, name, target=None, ride=None, tm=ROW_TILE):
    T, D = res.shape
    tm = min(tm, T)
    K = a.shape[1]
    with_loss = target is not None

    def body(a_ref, w_ref, g_ref, r_ref, *rest):
        subs = _sub_tiles(tm)
        zs = [_dot(a_ref[rows, :], w_ref[...]) for rows in subs]
        if not with_loss:
            h_ref, z_ref = rest
            for rows, z in zip(subs, zs):
                h_ref[rows, :] = (r_ref[rows, :].astype(F32) + z * _rms_r(z) * g_ref[...]).astype(STREAM)
                z_ref[rows, :] = z.astype(BF16)
            return
        t_ref, h_ref, dz_ref, dg_ref, da_ref, loss_ref = rest
        first = pl.program_id(0) == 0
        loss, dg = jnp.zeros((), F32), jnp.zeros((1, D), F32)
        for rows, z in zip(subs, zs):
            diff = r_ref[rows, :].astype(F32) + z * _rms_r(z) * g_ref[...] - t_ref[rows, :]
            dh = diff * (1.0 / D)
            h_ref[rows, :] = dh.astype(STREAM)
            loss = loss + jnp.sum(diff * diff)
            dz, zh = _rmsnorm_bwd(z, g_ref[...], dh)
            dz = dz.astype(BF16)
            dz_ref[rows, :] = dz
            dg = dg + jnp.sum(dh * zh, axis=0, keepdims=True)
            da_ref[rows, :] = _dot_nt(dz, w_ref[...]).astype(BF16)
        _accumulate(loss_ref, first, jnp.full(loss_ref.shape, 0.5 / D, F32) * loss)
        _accumulate(dg_ref, first, dg)

    row = pl.BlockSpec((tm, D), lambda i: (i, 0))
    vec = pl.BlockSpec((1, D), lambda i: (0, 0))
    in_specs = [pl.BlockSpec((tm, K), lambda i: (i, 0)), pl.BlockSpec((K, D), lambda i: (0, 0)), vec, row]
    if with_loss:
        in_specs.append(row)
        out_specs = [row, row, vec, pl.BlockSpec((tm, K), lambda i: (i, 0)), pl.BlockSpec((8, 128), lambda i: (0, 0))]
        out_shape = [SDS((T, D), STREAM), SDS((T, D), BF16), SDS((1, D), F32), SDS((T, K), BF16), SDS((8, 128), F32)]
    else:
        out_specs, out_shape = [row, row], [SDS((T, D), STREAM), SDS((T, D), BF16)]
    return _call(
        body, name=name, grid=(T // tm,), in_specs=in_specs, out_specs=out_specs, out_shape=out_shape,
        semantics=("arbitrary",), args=(a, w, g_post, res) + ((target,) if with_loss else ()), ride=ride)


def _silu_grads(d, g, u):
    sg = jax.nn.sigmoid(g)
    return d * u * (sg * (1.0 + g * (1.0 - sg))), d * (g * sg)


def _sub_tiles(tm):
    return [pl.ds(k, min(MXU_WIDTH, tm)) for k in range(0, tm, MXU_WIDTH)]


def norm_swiglu_in(x, g, w, *, name, ride=None, tm=ROW_TILE, cols=None, into=None):
    T, D = x.shape
    F = w.shape[1] // 2
    Fc = F if cols is None else F // 2
    shared = [] if into is None else list(into)

    def body(*refs):
        refs = refs[(cols is not None):]
        x_ref, g_ref, wg_ref, wu_ref = refs[:4]
        gu_ref, a_ref = refs[4 + len(shared):6 + len(shared)]
        subs = _sub_tiles(tm)
        xns = []
        for rows in subs:
            xf = x_ref[rows, :].astype(F32)
            xns.append(xf * _rms_r(xf) * g_ref[...])
        xbs = [xn.astype(BF16) for xn in xns]
        gates = [_dot(xb, wg_ref[...]).astype(BF16) for xb in xbs]
        ups = [_dot(xb, wu_ref[...]).astype(BF16) for xb in xbs]
        for rows, gate, up in zip(subs, gates, ups):
            gu_ref[0, rows, :] = gate
            gu_ref[1, rows, :] = up
            a_ref[rows, :] = gate * jax.nn.sigmoid(gate) * up
        if into is None:
            for rows, xn in zip(subs, xns):
                refs[-1][:, rows] = xn.T.astype(BF16)

    which = lambda h: h[0][0] if h else 0
    weight = lambda s: pl.BlockSpec((D, Fc), lambda i, *h: (0, s * (F // Fc) + which(h)), pipeline_mode=pl.Buffered(1))
    out_specs = [pl.BlockSpec((2, tm, Fc), lambda i, *h: (0, i, which(h))),
                 pl.BlockSpec((tm, Fc), lambda i, *h: (i, which(h)))]
    out_shape = [SDS((2, T, F), BF16), SDS((T, F), BF16)]
    if into is None:
        out_specs.append(pl.BlockSpec((D, tm), lambda i, *h: (0, i)))
        out_shape.append(SDS((D, T), BF16))
    return _call(
        body, name=name, grid=(T // tm,),
        in_specs=[pl.BlockSpec((tm, D), lambda i, *h: (i, 0)), pl.BlockSpec((1, D), lambda i, *h: (0, 0)), weight(0),
                  weight(1)] + [HBM_SPEC] * len(shared),
        out_specs=out_specs, out_shape=out_shape, semantics=("parallel",), args=(x, g, w, w) + tuple(shared),
        ride=ride, prefetch=cols, alias={4 + k: k for k in range(len(shared))})


def swiglu_bwd_tn(xt, dact, gu, *, name, ride=None, tb=MXU_WIDTH):
    D, T = xt.shape
    F = dact.shape[1]

    def body(xt_ref, d_ref, g_ref, u_ref, o_ref):
        dg, du = _silu_grads(d_ref[...], g_ref[...], u_ref[...])
        o_ref[0] = _dot(xt_ref[...], dg).astype(BF16)
        o_ref[1] = _dot(xt_ref[...], du).astype(BF16)

    col = lambda s: pl.BlockSpec((None, T, tb), lambda j: (s, 0, j))
    out = _call(
        body, name=name, grid=(F // tb,),
        in_specs=[pl.BlockSpec((D, T), lambda j: (0, 0), pipeline_mode=pl.Buffered(1)),
                  pl.BlockSpec((T, tb), lambda j: (0, j)), col(0), col(1)],
        out_specs=[pl.BlockSpec((2, D, tb), lambda j: (0, 0, j))],
        out_shape=[SDS((2, D, F), BF16)],
        semantics=("parallel",), args=(xt, dact, gu, gu), ride=ride)
    return out[0] if ride is None else (out[0][0], out[1])


def swiglu_bwd_in(dact, gu, w, h_in, g, dh_out, then, *, name, ride=None, tm=ROW_TILE):
    T, D = h_in.shape
    F = dact.shape[1]

    def body(d_ref, gg_ref, uu_ref, wg_ref, wu_ref, h_ref, g_ref, dh_ref, z_ref, g2_ref, w2_ref,
             o_ref, dg_ref, dz_ref, dg2_ref, da_ref):
        first = pl.program_id(0) == 0
        subs = _sub_tiles(tm)
        dns = []
        for rows in subs:
            dgate, dup = _silu_grads(d_ref[rows, :], gg_ref[rows, :], uu_ref[rows, :])
            dns.append(_dot_nt(dgate, wg_ref[...]) + _dot_nt(dup, wu_ref[...]))
        dg, dg2 = jnp.zeros((1, D), F32), jnp.zeros((1, D), F32)
        for rows, dn in zip(subs, dns):
            dx, hh = _rmsnorm_bwd(h_ref[rows, :].astype(F32), g_ref[...], dn)
            dh_in = dh_ref[rows, :] + dx
            o_ref[rows, :] = dh_in.astype(STREAM)
            dg = dg + jnp.sum(dn * hh, axis=0, keepdims=True)
            dz, zh = _rmsnorm_bwd(z_ref[rows, :].astype(F32), g2_ref[...], dh_in)
            dz = dz.astype(BF16)
            dz_ref[rows, :] = dz
            dg2 = dg2 + jnp.sum(dh_in * zh, axis=0, keepdims=True)
            da_ref[rows, :] = _dot_nt(dz, w2_ref[...]).astype(BF16)
        _accumulate(dg_ref, first, dg)
        _accumulate(dg2_ref, first, dg2)

    row = pl.BlockSpec((tm, D), lambda i: (i, 0))
    vec = pl.BlockSpec((1, D), lambda i: (0, 0))
    part = lambda s: pl.BlockSpec((None, tm, F), lambda i: (s, i, 0))
    half = lambda s: pl.BlockSpec((D, F), lambda i: (0, s), pipeline_mode=pl.Buffered(1))
    then_in, then_out, then_shape = _then_specs(then, tm, T, D)
    return _call(
        body, name=name, grid=(T // tm,),
        in_specs=[pl.BlockSpec((tm, F), lambda i: (i, 0)), part(0), part(1), half(0), half(1), row, vec, row] + then_in,
        out_specs=[row, vec] + then_out,
        out_shape=[SDS((T, D), STREAM), SDS((1, D), F32)] + then_shape,
        semantics=("arbitrary",), args=(dact, gu, gu, w, w, h_in, g, dh_out) + tuple(then), ride=ride)


def rope_tables(T):
    half = ROT_DIM // 2
    inv_freq = ROPE_THETA ** (-jnp.arange(0, ROT_DIM, 2, dtype=F32) / ROT_DIM)
    ang = (jnp.arange(T, dtype=F32)[:, None] * inv_freq[None, :]).T
    cos, sin = jnp.cos(ang), jnp.sin(ang)
    rest = HEAD_DIM - ROT_DIM
    one, zero = jnp.ones((rest, T), F32), jnp.zeros((rest, T), F32)
    zh = jnp.zeros((half, T), F32)
    fac = jnp.concatenate([cos, cos, one], axis=0)
    up = jnp.concatenate([-sin, zh, zero], axis=0)
    down = jnp.concatenate([zh, sin, zero], axis=0)
    return jnp.stack([fac, up, down])


def _rope(t, tab):
    half = ROT_DIM // 2
    return t * tab[0] + pltpu.roll(t, HEAD_DIM - half, 0) * tab[1] + pltpu.roll(t, half, 0) * tab[2]


def _rope_t(d, tab):
    half = ROT_DIM // 2
    return d * tab[0] + pltpu.roll(d * tab[1], half, 0) + pltpu.roll(d * tab[2], HEAD_DIM - half, 0)


def _head(t, h):
    return t[h * HEAD_DIM:(h + 1) * HEAD_DIM]


def _band(n, group):
    kj = lax.broadcasted_iota(jnp.int32, (2 * BLOCK, BLOCK), 0)
    qi = lax.broadcasted_iota(jnp.int32, (2 * BLOCK, BLOCK), 1)
    mask = (kj > qi) & (kj <= qi + BLOCK) & ((n > 0) | (kj >= BLOCK))
    return jnp.tile(mask, (1, group))


def _attn_specs(D, kvd, nb):
    cur = lambda n: jnp.minimum(n, nb - 1)
    prev = lambda n: jnp.maximum(cur(n) - 1, 0)
    return [pl.BlockSpec((BLOCK, D), lambda n: (cur(n), 0)),
            pl.BlockSpec((BLOCK, kvd), lambda n: (prev(n), 0)),
            pl.BlockSpec((BLOCK, kvd), lambda n: (cur(n), 0)),
            pl.BlockSpec((BLOCK, kvd), lambda n: (prev(n), 1)),
            pl.BlockSpec((BLOCK, kvd), lambda n: (cur(n), 1)),
            pl.BlockSpec((3, HEAD_DIM, BLOCK), lambda n: (0, 0, prev(n))),
            pl.BlockSpec((3, HEAD_DIM, BLOCK), lambda n: (0, 0, cur(n))),
            pl.BlockSpec(memory_space=pltpu.SMEM)]


def _attn_operands(q_ref, kp_ref, k_ref, vp_ref, v_ref, tp_ref, t_ref):
    flip = lambda ref: ref[...].astype(F32).T
    tab = t_ref[...]
    kt = jnp.concatenate([flip(kp_ref), flip(k_ref)], axis=1)
    vt = jnp.concatenate([flip(vp_ref), flip(v_ref)], axis=1)
    return flip(q_ref), kt, vt, tab, jnp.concatenate([tp_ref[...], tab], axis=2)


SCORE_SCALE = 1.0 / math.sqrt(HEAD_DIM)
HEADS_TOGETHER = 4


def _group_heads(t, first, count, tab=None):
    heads = [_head(t, first + g) for g in range(count)]
    if tab is not None:
        heads = [_rope(h, tab) * SCORE_SCALE for h in heads]
    return jnp.concatenate(heads, axis=1).astype(BF16)


def _sink_row(s_ref, first, count):
    which = lax.broadcasted_iota(jnp.int32, (1, count * BLOCK), 1) // BLOCK
    row = jnp.zeros((1, count * BLOCK), F32)
    for g in range(count):
        row = jnp.where(which == g, s_ref[0, first + g], row)
    return row


def _sum_keys(t):
    return _dot(jnp.ones((8, t.shape[0]), BF16), t)[0:1]


def _softmax(scores, sink, mask):
    s = jnp.where(mask, scores.astype(BF16), NEG)
    m = jnp.maximum(jnp.max(s, axis=0, keepdims=True).astype(F32), sink).astype(BF16)
    e = jnp.exp(s - m)
    m = m.astype(F32)
    return e, m, 1.0 / (_sum_keys(e) + jnp.exp(sink - m))


def _per_head(row, count):
    return [row[:, g * BLOCK:(g + 1) * BLOCK] for g in range(count)]


def attention_fwd(q, kv, tabs, sinks, *, name, ride=None):
    T, D = q.shape
    kvd = kv.shape[1] // 2
    heads = D // HEAD_DIM
    group = heads // N_KV_HEADS

    def body(q_ref, kp_ref, k_ref, vp_ref, v_ref, tp_ref, t_ref, s_ref, o_ref, stat_ref):
        gs = HEADS_TOGETHER
        mask = _band(pl.program_id(0), gs)
        qt, kt, vt, tab, tab2 = _attn_operands(q_ref, kp_ref, k_ref, vp_ref, v_ref, tp_ref, t_ref)
        firsts = [(j, first) for j in range(N_KV_HEADS) for first in range(j * group, (j + 1) * group, gs)]
        ks = [_rope(_head(kt, j), tab2).astype(BF16) for j in range(N_KV_HEADS)]
        scores = [_dot_tn(ks[j], _group_heads(qt, first, gs, tab)) for j, first in firsts]
        soft = [_softmax(s, _sink_row(s_ref, first, gs), mask) for s, (j, first) in zip(scores, firsts)]
        outs, ms, invs = [], [], []
        for (e, m, inv), (j, first) in zip(soft, firsts):
            o = _dot(_head(vt, j).astype(BF16), e) * inv
            outs += [o[:, g * BLOCK:(g + 1) * BLOCK] for g in range(gs)]
            ms += _per_head(m, gs)
            invs += _per_head(inv, gs)
        o_ref[...] = jnp.concatenate(outs, axis=0).T.astype(BF16)
        stat_ref[0] = jnp.concatenate(ms, axis=0)
        stat_ref[1] = jnp.concatenate(invs, axis=0)

    return _call(
        body, name=name, grid=(T // BLOCK,),
        in_specs=_attn_specs(D, kvd, T // BLOCK),
        out_specs=[pl.BlockSpec((BLOCK, D), lambda n: (n, 0)), pl.BlockSpec((2, heads, BLOCK), lambda n: (0, 0, n))],
        out_shape=[SDS((T, D), BF16), SDS((2, heads, T), F32)],
        semantics=("parallel",), args=(q, kv, kv, kv, kv, tabs, tabs, sinks), ride=ride)


def attention_bwd(q, kv, tabs, sinks, do, o, stats, *, name, ride=None):
    T, D = q.shape
    kvd = kv.shape[1] // 2
    heads = D // HEAD_DIM
    group = heads // N_KV_HEADS
    nb = T // BLOCK

    def body(q_ref, kp_ref, k_ref, vp_ref, v_ref, tp_ref, t_ref, s_ref, do_ref, o_ref, stat_ref,
             dq_ref, dkv_ref, ds_ref, carry):
        n = pl.program_id(0)

        @pl.when(n == 0)
        def _():
            carry[...] = jnp.zeros_like(carry)

        @pl.when(n < nb)
        def _():
            block(n, q_ref, kp_ref, k_ref, vp_ref, v_ref, tp_ref, t_ref, s_ref, do_ref, o_ref, stat_ref,
                  dq_ref, dkv_ref, ds_ref, carry)

        @pl.when(n == nb)
        def _():
            dkv_ref[...] = carry[...].astype(BF16)

    def block(n, q_ref, kp_ref, k_ref, vp_ref, v_ref, tp_ref, t_ref, s_ref, do_ref, o_ref, stat_ref,
              dq_ref, dkv_ref, ds_ref, carry):
        gs = HEADS_TOGETHER
        mask = _band(n, gs)
        qt, kt, vt, tab, tab2 = _attn_operands(q_ref, kp_ref, k_ref, vp_ref, v_ref, tp_ref, t_ref)
        dot = do_ref[...].astype(F32).T
        odo = o_ref[...].astype(F32).T * dot
        dl_all = jnp.concatenate([jnp.sum(_head(odo, h), axis=0, keepdims=True) for h in range(heads)], axis=0)
        m_all, inv_all = stat_ref[0], stat_ref[1]
        row = lambda t, first: jnp.concatenate([t[first + g:first + g + 1] for g in range(gs)], axis=1)
        lane = lax.broadcasted_iota(jnp.int32, (8, 128), 1)
        dsink = jnp.zeros((8, 128), F32)
        firsts = [(j, first) for j in range(N_KV_HEADS) for first in range(j * group, (j + 1) * group, gs)]
        ks = [_rope(_head(kt, j), tab2).astype(BF16) for j in range(N_KV_HEADS)]
        vs = [_head(vt, j).astype(BF16) for j in range(N_KV_HEADS)]
        qs = [_group_heads(qt, first, gs, tab) for _, first in firsts]
        dos = [_group_heads(dot, first, gs) for _, first in firsts]
        scores = [_dot_tn(ks[j], q) for q, (j, _) in zip(qs, firsts)]
        dps = [_dot_tn(vs[j], do) for do, (j, _) in zip(dos, firsts)]
        ps, dscs = [], []
        for s, dp, (j, first) in zip(scores, dps, firsts):
            m, inv, dl = row(m_all, first), row(inv_all, first), row(dl_all, first)
            e = jnp.exp(jnp.where(mask, s.astype(BF16), NEG) - m.astype(BF16))
            p = e * inv.astype(BF16)
            dscs.append(p * (dp.astype(BF16) - dl.astype(BF16)))
            ps.append(p)
            weight = jnp.exp(_sink_row(s_ref, first, gs) - m) * inv * dl
            for g in range(gs):
                dsink = dsink - jnp.where(lane == first + g, jnp.sum(weight[:, g * BLOCK:(g + 1) * BLOCK]), 0.0)
        dqs = []
        dks = [jnp.zeros((HEAD_DIM, 2 * BLOCK), F32) for _ in range(N_KV_HEADS)]
        dvs = [jnp.zeros((HEAD_DIM, 2 * BLOCK), F32) for _ in range(N_KV_HEADS)]
        for p, dsc, q, do, (j, _) in zip(ps, dscs, qs, dos, firsts):
            dq = _dot(ks[j], dsc) * SCORE_SCALE
            dqs += [_rope_t(dq[:, g * BLOCK:(g + 1) * BLOCK], tab) for g in range(gs)]
            dks[j] = dks[j] + _dot_nt(q, dsc)
            dvs[j] = dvs[j] + _dot_nt(do, p)
        dks = [_rope_t(dk, tab2) for dk in dks]
        dq_ref[...] = jnp.concatenate(dqs, axis=0).T.astype(BF16)
        dkv = jnp.concatenate(dks + dvs, axis=0)
        dkv_ref[...] = (carry[...] + dkv[:, :BLOCK].T).astype(BF16)
        carry[...] = dkv[:, BLOCK:].T
        _accumulate(ds_ref, n == 0, dsink)

    cur = lambda n: jnp.minimum(n, nb - 1)
    blk = lambda w: pl.BlockSpec((BLOCK, w), lambda n: (cur(n), 0))
    return _call(
        body, name=name, grid=(nb + 1,),
        in_specs=_attn_specs(D, kvd, nb) + [blk(D), blk(D), pl.BlockSpec((2, heads, BLOCK), lambda n: (0, 0, cur(n)))],
        out_specs=[blk(D), pl.BlockSpec((BLOCK, 2 * kvd), lambda n: (jnp.maximum(n - 1, 0), 0)),
                   pl.BlockSpec((8, 128), lambda n: (0, 0))],
        out_shape=[SDS((T, D), BF16), SDS((T, 2 * kvd), BF16), SDS((8, 128), F32)],
        scratch_shapes=[pltpu.VMEM((BLOCK, 2 * kvd), F32)],
        semantics=("arbitrary",), args=(q, kv, kv, kv, kv, tabs, tabs, sinks, do, o, stats), ride=ride)


def matmul_nt_normbwd(da, w, h_in, g, dh_out, *, name, ride=None, tm=ROW_TILE):
    T, D = h_in.shape
    S, _, K = da.shape

    def body(*refs):
        da_refs, w_refs = refs[:S], refs[S:2 * S]
        h_ref, g_ref, dh_ref, o_ref, dg_ref = refs[2 * S:]
        subs = _sub_tiles(tm)
        dns = []
        for rows in subs:
            dn = _dot_nt(da_refs[0][rows, :], w_refs[0][...])
            for s in range(1, S):
                dn = dn + _dot_nt(da_refs[s][rows, :], w_refs[s][...])
            dns.append(dn)
        dg = jnp.zeros((1, D), F32)
        for rows, dn in zip(subs, dns):
            dx, hh = _rmsnorm_bwd(h_ref[rows, :].astype(F32), g_ref[...], dn)
            o_ref[rows, :] = dh_ref[rows, :] + dx
            dg = dg + jnp.sum(dn * hh, axis=0, keepdims=True)
        _accumulate(dg_ref, pl.program_id(0) == 0, dg)

    row = pl.BlockSpec((tm, D), lambda i: (i, 0))
    vec = pl.BlockSpec((1, D), lambda i: (0, 0))
    part = lambda s: pl.BlockSpec((None, tm, K), lambda i: (s, i, 0))
    cols = lambda s: pl.BlockSpec((D, K), lambda i: (0, s), pipeline_mode=pl.Buffered(1))
    return _call(
        body, name=name, grid=(T // tm,),
        in_specs=[part(s) for s in range(S)] + [cols(s) for s in range(S)] + [row, vec, row],
        out_specs=[row, vec],
        out_shape=[SDS((T, D), F32), SDS((1, D), F32)],
        semantics=("arbitrary",), args=[da] * S + [w] * S + [h_in, g, dh_out], ride=ride)


def matmuls_nt_normbwd(das, ws, h_in, gs, dh_out, then, *, name, ride=None, tm=ROW_TILE):
    T, D = h_in.shape
    tm = min(tm, T)
    n = len(das)

    def body(*refs):
        da_refs, w_refs, g_refs = refs[:n], refs[n:2 * n], refs[2 * n:3 * n]
        h_ref, dh_ref, z_ref, g2_ref, w2_ref, o_ref = refs[3 * n:3 * n + 6]
        dg_refs, (dz_ref, dg2_ref, da_ref) = refs[3 * n + 6:4 * n + 6], refs[4 * n + 6:]
        first = pl.program_id(0) == 0
        subs = _sub_tiles(tm)
        dns = [[_dot_nt(da_ref_[rows, :], w_ref[...]) for da_ref_, w_ref in zip(da_refs, w_refs)] for rows in subs]
        dgs, dg2 = [jnp.zeros((1, D), F32) for _ in range(n)], jnp.zeros((1, D), F32)
        for rows, dn_sub in zip(subs, dns):
            hf = h_ref[rows, :].astype(F32)
            r = _rms_r(hf)
            hh = hf * r
            total = dh_ref[rows, :].astype(F32)
            for b, (dn, g_ref) in enumerate(zip(dn_sub, g_refs)):
                gd = g_ref[...] * dn
                total = total + r * (gd - hh * jnp.mean(hh * gd, axis=-1, keepdims=True))
                dgs[b] = dgs[b] + jnp.sum(dn * hh, axis=0, keepdims=True)
            o_ref[rows, :] = total.astype(STREAM)
            dz, zh = _rmsnorm_bwd(z_ref[rows, :].astype(F32), g2_ref[...], total)
            dz = dz.astype(BF16)
            dz_ref[rows, :] = dz
            dg2 = dg2 + jnp.sum(total * zh, axis=0, keepdims=True)
            da_ref[rows, :] = _dot_nt(dz, w2_ref[...]).astype(BF16)
        for dg_ref, dg in zip(dg_refs + (dg2_ref,), dgs + [dg2]):
            _accumulate(dg_ref, first, dg)

    row = pl.BlockSpec((tm, D), lambda i: (i, 0))
    vec = pl.BlockSpec((1, D), lambda i: (0, 0))
    then_in, then_out, then_shape = _then_specs(then, tm, T, D)
    return _call(
        body, name=name, grid=(T // tm,),
        in_specs=[pl.BlockSpec((tm, da.shape[1]), lambda i: (i, 0)) for da in das]
        + [pl.BlockSpec(w.shape, lambda i: (0, 0)) for w in ws] + [vec] * n + [row, row] + then_in,
        out_specs=[row] + [vec] * n + then_out,
        out_shape=[SDS((T, D), STREAM)] + [SDS((1, D), F32)] * n + then_shape,
        semantics=("arbitrary",), args=list(das) + list(ws) + list(gs) + [h_in, dh_out] + list(then), ride=ride)


def matmul_tn(a, b, *, tb, name, ride=None, ta=MXU_WIDTH):
    T, Ka = a.shape
    S, _, Nb = b.shape
    per = Nb // tb

    def body(a_ref, b_ref, o_ref):
        o_ref[...] = _dot_tn(a_ref[...], b_ref[...]).astype(BF16)

    out = _call(
        body, name=name, grid=(S * per, Ka // ta),
        in_specs=[pl.BlockSpec((T, ta), lambda j, i: (0, i)),
                  pl.BlockSpec((None, T, tb), lambda j, i: (j // per, 0, j % per))],
        out_specs=[pl.BlockSpec((ta, tb), lambda j, i: (i, j))],
        out_shape=[SDS((Ka, S * Nb), BF16)],
        semantics=("parallel", "parallel"), args=(a, b), ride=ride)
    return out[0] if ride is None else (out[0][0], out[1])


def conv_bwd(dy, bcx, conv_w, *, name, ride=None, tm=ROW_TILE):
    T, D = dy.shape
    nt = T // tm
    hb = tm // BF16_ROWS
    last = T // BF16_ROWS - 1

    def body(dy_ref, dyn_ref, b_ref, bn_ref, c_ref, u_ref, cp_ref, up_ref, cw_ref, o_ref, dw_ref):
        i = pl.program_id(0)
        c, u = c_ref[...].astype(F32), u_ref[...].astype(F32)
        cu = c * u
        cup = jnp.where(i == 0, 0.0, cp_ref[...].astype(F32) * up_ref[...].astype(F32))
        cu1, cu2 = _shift_down(cup, cu, 1), _shift_down(cup, cu, 2)
        w0, w1, w2 = cw_ref[0:1, :], cw_ref[1:2, :], cw_ref[2:3, :]
        dyf = dy_ref[...].astype(F32)
        o_ref[:, 0:D] = (dyf * (w0 * cu2 + w1 * cu1 + w2 * cu)).astype(BF16)
        dcv = dyf * b_ref[...].astype(F32)
        dcvn = jnp.where(i == nt - 1, 0.0, dyn_ref[...].astype(F32) * bn_ref[...].astype(F32))
        dcu = w2 * dcv + w1 * _shift_up(dcv, dcvn, 1) + w0 * _shift_up(dcv, dcvn, 2)
        o_ref[:, D:2 * D] = (dcu * u).astype(BF16)
        o_ref[:, 2 * D:3 * D] = (dcu * c).astype(BF16)
        row = lax.broadcasted_iota(jnp.int32, (8, D), 0)
        dw = jnp.zeros((8, D), F32)
        for tap, t in enumerate((cu2, cu1, cu)):
            dw = jnp.where(row == tap, jnp.sum(dcv * t, axis=0, keepdims=True), dw)
        _accumulate(dw_ref, i == 0, dw)

    tile = lambda col: pl.BlockSpec((tm, D), lambda i: (i, col))
    prev = lambda col: pl.BlockSpec((BF16_ROWS, D), lambda i: (jnp.maximum(i * hb - 1, 0), col))
    nxt = lambda col: pl.BlockSpec((BF16_ROWS, D), lambda i: (jnp.minimum((i + 1) * hb, last), col))
    return _call(
        body, name=name, grid=(nt,),
        in_specs=[tile(0), nxt(0), tile(0), nxt(0), tile(1), tile(2), prev(1), prev(2),
                  pl.BlockSpec((3, D), lambda i: (0, 0))],
        out_specs=[pl.BlockSpec((tm, 3 * D), lambda i: (i, 0)), pl.BlockSpec((8, D), lambda i: (0, 0))],
        out_shape=[SDS((T, 3 * D), BF16), SDS((8, D), F32)],
        semantics=("arbitrary",), args=(dy, dy, bcx, bcx, bcx, bcx, bcx, bcx, conv_w), ride=ride)


class NoTraffic:
    def ride(self, kernel_name):
        return None

    def column_halves(self):
        return None

    def landed(self, kernel_name, results, wts):
        pass

    def grad(self, key, value):
        pass


def local_step(x, target, wts, vec, traffic):
    T, D = x.shape
    tabs = rope_tables(T)
    small = {}

    def run(builder, *args, name, **kw):
        ride = traffic.ride(name)
        if ride is None:
            return builder(*args, name=name, **kw)
        out, extra = builder(*args, name=name, ride=ride, **kw)
        traffic.landed(name, extra, wts)
        return out

    bcx, xn1 = run(norm_matmul, x, vec["a_pre"], wts["w_in"], tn=3 * D, split=1, name="a_in")
    bcx = bcx[0]
    h1, z0, y0 = run(conv_mix_out, bcx, vec["conv_w"], wts["w_out"], vec["a_post"], x, name="a_out")
    halves = traffic.column_halves()
    if halves is None:
        gu0, act0, xt2 = run(norm_swiglu_in, h1, vec["ffn_pre0"], wts["gu0"], name="ffn0_in")
    else:
        gu0, act0, xt2 = run(norm_swiglu_in, h1, vec["ffn_pre0"], wts["gu0"], name="ffn0_in", cols=halves[0])
        gu0, act0 = run(norm_swiglu_in, h1, vec["ffn_pre0"], wts["gu0"], name="ffn0_in_rest", cols=halves[1],
                        into=(gu0, act0))
    h2, z1 = run(plain_mix_out, act0, wts["wd0"], vec["ffn_post0"], h1, name="ffn0_out")
    kvp, xkv, qp, xq = run(norm2_matmul, h2, [vec["kv_norm"], vec["b_pre"]], [wts["w_kv"], wts["w_q"]],
                           name="kvq_in")
    attn, attn_stats = run(attention_fwd, qp, kvp, tabs, vec["sinks"], name="attn_fwd")
    h3, z2 = plain_mix_out(attn, wts["w_o"], vec["b_post"], h2, name="attn_out", tm=BIG_ROW_TILE)
    gu1, act1, xt3 = run(norm_swiglu_in, h3, vec["ffn_pre1"], wts["gu1"], name="ffn1_in")
    dy, dz3, small["ffn_post1"], dact1, loss = plain_mix_out(act1, wts["wd1"], vec["ffn_post1"], h3, name="ffn1_out",
                                                             target=target)

    def ffn_bwd(layer, dz, dact, gu, act, xt, h_in, dh, then, gu_first):
        tag = "ffn%d" % layer
        dwd = lambda: traffic.grad("wd%d" % layer, run(matmul_tn, act, dz[None], tb=D, name=tag + "_dwd"))
        dwgu = lambda: traffic.grad("gu%d" % layer, run(swiglu_bwd_tn, xt, dact, gu, name=tag + "_dwgu"))
        for step in ((dwgu, dwd) if gu_first else (dwd, dwgu)):
            step()
        dh_in, small["ffn_pre%d" % layer], dz_, dg_, da_ = run(
            swiglu_bwd_in, dact, gu, wts["gu%d" % layer], h_in, vec["ffn_pre%d" % layer], dh, then,
            name=tag + "_in_bwd")
        return dh_in, dz_, dg_, da_

    dh3, dz2, small["b_post"], dattn = ffn_bwd(1, dz3, dact1, gu1, act1, xt3, h3, dy,
                                               (z2, vec["b_post"], wts["w_o"]), gu_first=False)
    traffic.grad("w_o", matmul_tn(attn, dz2[None], tb=D, name="attn_dwo"))
    dq, dkv, small["sinks"] = run(attention_bwd, qp, kvp, tabs, vec["sinks"], dattn, attn, attn_stats,
                                  name="attn_bwd")
    traffic.grad("w_q", matmul_tn(xq, dq[None], tb=D, name="attn_dwq"))
    traffic.grad("w_kv", matmul_tn(xkv, dkv[None], tb=dkv.shape[1], name="attn_dwkv"))
    dh2, small["b_pre"], small["kv_norm"], dz1, small["ffn_post0"], dact0 = run(
        matmuls_nt_normbwd, [dq, dkv], [wts["w_q"], wts["w_kv"]], h2, [vec["b_pre"], vec["kv_norm"]], dh3,
        (z1, vec["ffn_post0"], wts["wd0"]), name="qkv_in_bwd")
    dh1, dz0, small["a_post"], dyc = ffn_bwd(0, dz1, dact0, gu0, act0, xt2, h1, dh2,
                                             (z0, vec["a_post"], wts["w_out"]), gu_first=True)
    traffic.grad("w_out", run(matmul_tn, y0, dz0[None], tb=D, name="a_dwout"))
    dbcx, small["conv_w"] = run(conv_bwd, dyc, bcx, vec["conv_w"], name="a_conv_bwd")
    traffic.grad("w_in", run(matmul_tn, xn1, dbcx[None], tb=3 * D // 2, name="a_dwin"))
    dx, small["a_pre"] = run(matmul_nt_normbwd, dbcx[None], wts["w_in"], x, vec["a_pre"], dh1, name="a_in_bwd")
    return loss, dx, small


SMALL_ROWS = 16
LOSS_ROW = 13

WHOLE = None
GATHER_PLAN = {"cast_rest": [("w_in", WHOLE)],
               "a_in": [("w_out", WHOLE), ("gu0", (0, 16), "first")],
               "a_out": [("gu0", (16, 16), "first")],
               "ffn0_in": [("gu0", WHOLE, "second"), ("w_kv", WHOLE), ("w_q", WHOLE)],
               "ffn0_in_rest": [("wd0", WHOLE), ("gu1", (0, 4))],
               "ffn0_out": [("w_o", WHOLE), ("gu1", (4, 8))],
               "attn_fwd": [("gu1", (12, 20))],
               "ffn1_in": [("wd1", WHOLE)]}
PAIR_PLAN = {"ffn1_dwgu": ["wd1"], "ffn1_in_bwd": ["gu1"], "attn_bwd": ["w_o"], "qkv_in_bwd": ["w_q", "w_kv"],
             "ffn0_dwd": ["gu0"], "ffn0_in_bwd": ["wd0"], "a_conv_bwd": ["w_out"], "chip_reduce_early": ["w_in"]}
CHIP_PLAN = {"ffn1_in_bwd": [("wd1", WHOLE)], "attn_bwd": [("gu1", WHOLE)],
             "ffn0_dwgu": [("w_o", WHOLE), ("w_q", WHOLE), ("w_kv", WHOLE)],
             "ffn0_in_bwd": [("gu0", WHOLE)], "a_conv_bwd": [("wd0", (0, 12))],
             "a_dwin": [("wd0", (12, 10)), ("w_out", WHOLE)], "a_in_bwd": [("w_in", WHOLE)]}
HALF_PLAN = {"a_in_bwd": ["gu0", "gu1", "wd0", "wd1", "w_kv", "w_q", "w_o", "w_out"]}
GRAD_KIND = dict(KIND, gu0="split", gu1="split")


class Traffic:
    def __init__(self, wholes, quarter, c_arr, pc_arr):
        self.wholes, self.quarter, self.c_arr, self.pc_arr = wholes, quarter, c_arr, pc_arr
        self.views, self.sums, self.got = {}, {}, {}
        self.reduced = {}
        self.stages = {}

    def reduce(self, keys, name):
        args = ([self.sums[k] for k in keys], [self.got[k] for k in keys], [GRAD_KIND[k] for k in keys], self.pc_arr)
        if name not in PAIR_PLAN:
            return chip_reduce(*args, name=name)
        pairs = PAIR_PLAN[name]
        out, got = chip_reduce(*args, name=name, ride=pair_ride([self.views[k] for k in pairs]))
        self.pair_sums(pairs, got)
        return out

    def column_halves(self):
        mine = self.pc_arr[0:1] % 2
        return mine, 1 - mine

    def pair_sums(self, keys, got):
        for k, theirs in zip(keys, got):
            self.sums[k] = pair_add(self.views[k], theirs, self.c_arr, name="pair_add_" + k)

    def ride(self, name, small=None):
        rides, stages = [], []
        if name in GATHER_PLAN:
            plan = [(entry + (None,))[:3] for entry in GATHER_PLAN[name]]
            rides.append(gather_ride([self.wholes[k] for k, _, _ in plan],
                                     [(KIND[k], self.quarter[k], part, leg) for k, part, leg in plan], small))
            stages.append(("gather", [k for k, _, _ in plan]))
        if name in HALF_PLAN:
            keys = HALF_PLAN[name]
            rides.append(half_ride(self.reduce(keys, "chip_reduce_early")))
            stages.append(("half", keys))
        if name in CHIP_PLAN:
            plan = CHIP_PLAN[name]
            rides.append(chip_ride([self.sums[k] for k, _ in plan],
                                   [(GRAD_KIND[k], self.quarter[k], part) for k, part in plan],
                                   earlier=[self.got.get(k) for k, _ in plan]))
            stages.append(("chip", [k for k, _ in plan]))
        if name in PAIR_PLAN:
            keys = PAIR_PLAN[name]
            rides.append(pair_ride([self.views[k] for k in keys]))
            stages.append(("pair", keys))
        self.stages[name] = stages
        return join(rides)

    def landed(self, name, results, wts):
        results = list(results)
        for stage, keys in self.stages[name]:
            mine, results = results[:len(keys)], results[len(keys):]
            if stage == "gather":
                for k, whole in zip(keys, mine):
                    self.wholes[k] = wts[k] = whole
            elif stage == "chip":
                self.got.update(zip(keys, mine))
            elif stage == "half":
                self.reduced.update(zip(keys, mine))
            else:
                self.pair_sums(keys, mine)

    def grad(self, key, value):
        r, ws = self.quarter[key]
        view = {"row": (N_CHIPS, 2, r // 2, ws), "col": (1, 2, r // 2, N_CHIPS * ws), "split": (2, 2, r // 2, 2 * ws)}
        self.views[key] = value.reshape(view[GRAD_KIND[key]])


def kernel(x, a_pre_norm, a_w_in, a_conv_w, a_w_out, a_post_norm, ffn_pre_norm, ffn_w_gate_up, ffn_w_down, ffn_post_norm, kv_norm, w_kv, b_pre_norm, b_w_q, b_sinks, b_w_o, b_post_norm, loss_target, m_a_pre_norm, m_a_w_in, m_a_conv_w, m_a_w_out, m_a_post_norm, m_ffn_pre_norm, m_ffn_w_gate_up, m_ffn_w_down, m_ffn_post_norm, m_kv_norm, m_w_kv, m_b_pre_norm, m_b_w_q, m_b_sinks, m_b_w_o, m_b_post_norm, v_a_pre_norm, v_a_w_in, v_a_conv_w, v_a_w_out, v_a_post_norm, v_ffn_pre_norm, v_ffn_w_gate_up, v_ffn_w_down, v_ffn_post_norm, v_kv_norm, v_w_kv, v_b_pre_norm, v_b_w_q, v_b_sinks, v_b_w_o, v_b_post_norm):
    T, D = x.shape[1], x.shape[2]
    xi, yi, ci = _place()
    p = 2 * xi + yi
    p_arr = jnp.reshape(p, (1,)).astype(jnp.int32)
    c_arr = jnp.reshape(ci, (1,)).astype(jnp.int32)
    pc_arr = jnp.stack([p, ci]).astype(jnp.int32)
    me_arr = jnp.reshape(4 * xi + 2 * yi + ci, (1,)).astype(jnp.int32)
    qd = D // N_CHIPS

    big = {"w_in": (a_w_in, 0), "w_out": (a_w_out, 0), "gu0": (ffn_w_gate_up, 0), "gu1": (ffn_w_gate_up, 1),
           "wd0": (ffn_w_down, 0), "wd1": (ffn_w_down, 1), "w_kv": (w_kv[None], 0), "w_q": (b_w_q, 0),
           "w_o": (b_w_o, 0)}
    names = list(big)
    quarter = {k: w.shape[1:] for k, (w, _) in big.items()}
    source = lambda k: big[k] + (KIND[k],)
    traffic = Traffic(dict(zip(names[:1], cast_quarters([source(names[0])], p_arr, name="cast_first"))), quarter,
                      c_arr, pc_arr)
    small_shard = jnp.concatenate([a_pre_norm, a_post_norm, a_conv_w[0], jnp.zeros((3, qd), F32)], axis=0)
    wts = {}
    rest, (*landed, small_full) = cast_quarters([source(k) for k in names[1:]], p_arr, name="cast_rest",
                                                ride=traffic.ride("cast_rest", small_shard))
    traffic.wholes.update(zip(names[1:], rest))
    traffic.landed("cast_rest", landed, wts)
    rows = lambda k: jnp.transpose(small_full[:, k], (1, 0, 2)).reshape(-1, D)
    vec = {"a_pre": rows(slice(0, 1)), "a_post": rows(slice(1, 2)), "conv_w": rows(slice(2, 5)),
           "ffn_pre0": ffn_pre_norm[0:1], "ffn_pre1": ffn_pre_norm[1:2],
           "ffn_post0": ffn_post_norm[0:1], "ffn_post1": ffn_post_norm[1:2],
           "kv_norm": kv_norm[None], "b_pre": b_pre_norm, "b_post": b_post_norm, "sinks": b_sinks}

    loss, dx, small = local_step(x[0], loss_target[0], wts, vec, traffic)

    pad = lambda a: jnp.pad(a, ((0, 0), (0, D - a.shape[1])))
    small_block = jnp.concatenate(
        [small["a_pre"], small["a_post"], small["conv_w"][0:3], small["ffn_pre0"], small["ffn_pre1"],
         small["ffn_post0"], small["ffn_post1"], small["kv_norm"], small["b_pre"], small["b_post"],
         pad(small["sinks"][0:1]), pad(loss[0:1]), jnp.zeros((SMALL_ROWS - LOSS_ROW - 1, D), F32)], axis=0)
    late = [k for k in names if k not in traffic.reduced]
    *swapped, small_blocks = alone(join([half_ride(traffic.reduce(late, "chip_reduce_late")),
                                         chip_ride([], [], small_block)]), name="last_exchange")
    traffic.reduced.update(zip(late, swapped))
    grad = {k: traffic.reduced[k].reshape(quarter[k]) for k in names}
    small_sum = small_reduce(small_blocks, me_arr)

    out = {}
    out["a_w_in"] = adamw(a_w_in, [grad["w_in"]], m_a_w_in, v_a_w_in, name="adamw_a_w_in")
    out["a_w_out"] = adamw(a_w_out, [grad["w_out"]], m_a_w_out, v_a_w_out, name="adamw_a_w_out")
    out["ffn_w_gate_up"] = adamw(ffn_w_gate_up, [grad["gu0"], grad["gu1"]], m_ffn_w_gate_up, v_ffn_w_gate_up,
                                 name="adamw_ffn_w_gate_up")
    out["ffn_w_down"] = adamw(ffn_w_down, [grad["wd0"], grad["wd1"]], m_ffn_w_down, v_ffn_w_down,
                              name="adamw_ffn_w_down")
    out["w_kv"] = [o[0] for o in adamw(w_kv[None], [grad["w_kv"]], m_w_kv[None], v_w_kv[None], name="adamw_w_kv")]
    out["b_w_q"] = adamw(b_w_q, [grad["w_q"]], m_b_w_q, v_b_w_q, name="adamw_b_w_q")
    out["b_w_o"] = adamw(b_w_o, [grad["w_o"]], m_b_w_o, v_b_w_o, name="adamw_b_w_o")

    leaves = {"a_pre_norm": (a_pre_norm, m_a_pre_norm, v_a_pre_norm, 0, True),
              "a_post_norm": (a_post_norm, m_a_post_norm, v_a_post_norm, 1, True),
              "a_conv_w": (a_conv_w, m_a_conv_w, v_a_conv_w, 2, True),
              "ffn_pre_norm": (ffn_pre_norm, m_ffn_pre_norm, v_ffn_pre_norm, 5, False),
              "ffn_post_norm": (ffn_post_norm, m_ffn_post_norm, v_ffn_post_norm, 7, False),
              "kv_norm": (kv_norm[None], m_kv_norm[None], v_kv_norm[None], 9, False),
              "b_pre_norm": (b_pre_norm, m_b_pre_norm, v_b_pre_norm, 10, False),
              "b_post_norm": (b_post_norm, m_b_post_norm, v_b_post_norm, 11, False),
              "b_sinks": (b_sinks, m_b_sinks, v_b_sinks, 12, False)}
    for k, results in zip(leaves, adamw_rows(small_sum, p_arr, list(leaves.values()), name="adamw_small")):
        out[k] = [r[0] for r in results] if k == "kv_norm" else results

    order = ["a_pre_norm", "a_w_in", "a_conv_w", "a_w_out", "a_post_norm", "ffn_pre_norm", "ffn_w_gate_up",
             "ffn_w_down", "ffn_post_norm", "kv_norm", "w_kv", "b_pre_norm", "b_w_q", "b_sinks", "b_w_o",
             "b_post_norm"]
    return (small_sum[LOSS_ROW, 0], dx[None], *[out[k][0] for k in order], *[out[k][1] for k in order],
            *[out[k][2] for k in order], *[out[k][3] for k in order])
```

```python
import math

import jax
import jax.numpy as jnp
from jax import lax
from jax.experimental import pallas as pl
from jax.experimental.pallas import tpu as pltpu

F32 = jnp.float32
BF16 = jnp.bfloat16
SDS = jax.ShapeDtypeStruct
MESH = pl.DeviceIdType.MESH
DMA = pltpu.SemaphoreType.DMA
HBM_SPEC = pl.BlockSpec(memory_space=pltpu.HBM)

EPS = 1e-6
NEG = -1e30
HEAD_DIM = 64
N_KV_HEADS = 4
BLOCK = 128
ROT_DIM = HEAD_DIM // 4
ROPE_THETA = 500000.0
N_CHIPS = 4

ADAM_LR = 0.001
ADAM_B1 = 0.9
ADAM_B2 = 0.999
ADAM_EPS = 1e-08
ADAM_WD = 0.01
ADAM_STEP = 10

VMEM_LIMIT_BYTES = 52 * 1024 * 1024
ROW_TILE = 512
BF16_ROWS = 16
STREAM = BF16
MXU_WIDTH = 256

KIND = {"w_in": "col", "gu0": "col", "gu1": "col", "w_out": "row", "wd0": "row", "wd1": "row", "w_kv": "row",
        "w_q": "row", "w_o": "row"}


def _params(*semantics):
    return pltpu.CompilerParams(dimension_semantics=semantics, vmem_limit_bytes=VMEM_LIMIT_BYTES)


def _row_tile(rows, limit, step=8):
    return max(t for t in range(step, limit + 1, step) if rows % t == 0)


def _place():
    return lax.axis_index("x"), lax.axis_index("y"), lax.axis_index("c")


def _other_chips(x, y):
    return [(1 - x, y), (x, 1 - y), (1 - x, 1 - y)]


def _remote(src, dst, send_sem, recv_sem, to):
    return pltpu.make_async_remote_copy(src_ref=src, dst_ref=dst, send_sem=send_sem, recv_sem=recv_sem,
                                        device_id=to, device_id_type=MESH)


def _full_shape(kind, quarter):
    r, ws = quarter
    return (N_CHIPS * r, ws) if kind == "row" else (r, N_CHIPS * ws)


def _rows_of(h, part):
    lo, n = (0, h) if part is None else (part[0] * BF16_ROWS, part[1] * BF16_ROWS)
    assert lo + n <= h, (h, part)
    return lo, n


def _half_of_quarter(ref, kind, quarter, part, q, half):
    r, ws = quarter
    h = r // 2
    lo, n = _rows_of(h, part)
    if kind == "row":
        return ref.at[pl.ds(pl.multiple_of(q * r + half * h + lo, BF16_ROWS), n)]
    return ref.at[pl.ds(pl.multiple_of(half * h + lo, BF16_ROWS), n), pl.ds(pl.multiple_of(q * ws, 128), ws)]


class Ride:
    def __init__(self, operands, out_shape, aliases, sems, make):
        self.operands, self.out_shape, self.aliases, self.sems, self.make = operands, out_shape, aliases, sems, make

    def stages(self, ins, outs, sems):
        made = self.make(ins, outs, sems)
        return made if len(made) == 4 else (made[0], None, None, made[1])


def join(rides):
    rides = [r for r in rides if r is not None]
    if len(rides) < 2:
        return rides[0] if rides else None
    aliases, at = {}, [0, 0, 0]
    cuts = []
    for r in rides:
        aliases.update({at[0] + i: at[1] + o for i, o in r.aliases.items()})
        cuts.append(tuple(at))
        at = [at[0] + len(r.operands), at[1] + len(r.out_shape), at[2] + len(r.sems)]
    cuts.append(tuple(at))

    def make(ins, outs, sem):
        made = [r.stages(ins[lo[0]:hi[0]], outs[lo[1]:hi[1]], sem[lo[2]:hi[2]]) for r, lo, hi in zip(rides, cuts, cuts[1:])]
        def all_of(k):
            def stage():
                for m in made:
                    if m[k] is not None:
                        m[k]()
            return stage

        if all(m[1] is None for m in made):
            return all_of(0), all_of(3)
        return all_of(0), all_of(1), all_of(2), all_of(3)

    return Ride(sum((list(r.operands) for r in rides), []), sum((list(r.out_shape) for r in rides), []), aliases,
                sum((list(r.sems) for r in rides), []), make)


def _call(body, *, name, grid, in_specs, out_specs, out_shape, args, scratch_shapes=(), semantics=None, ride=None,
          prefetch=None):
    pre = 0 if prefetch is None else 1
    n_in, n_out, n_scr = len(in_specs), len(out_specs), len(scratch_shapes)
    r_in, r_out = (len(ride.operands), len(ride.out_shape)) if ride is not None else (0, 0)
    a, b = pre + n_in, pre + n_in + r_in
    c, d = b + n_out, b + n_out + r_out
    e = d + n_scr

    def riding(*refs):
        start, relay, relay_again, finish = ride.stages(refs[a:b], refs[c:d], refs[e:])
        step, steps = pl.program_id(0), 1
        for k, extent in enumerate(grid):
            step = pl.program_id(k) if k == 0 else step * extent + pl.program_id(k)
            steps *= extent
        pl.when(step == 0)(start)
        if relay is not None:
            pl.when(step == steps // 2)(relay)
            pl.when(step == steps - 1)(relay_again)
        body(*refs[:a], *refs[b:c], *refs[d:e])
        pl.when(step == steps - 1)(finish)

    if ride is None:
        kernel_body, extra_in, extra_out, extra_shape, extra_scr, aliases = body, [], [], [], [], {}
        params = _params(*semantics)
    else:
        kernel_body, extra_in, extra_out = riding, [HBM_SPEC] * r_in, [HBM_SPEC] * r_out
        extra_shape, extra_scr = list(ride.out_shape), list(ride.sems)
        aliases = {pre + n_in + i: n_out + o for i, o in ride.aliases.items()}
        params = _params(*(("arbitrary",) * len(grid)))
    specs = dict(grid=grid, in_specs=list(in_specs) + extra_in, out_specs=list(out_specs) + extra_out,
                 scratch_shapes=list(scratch_shapes) + extra_scr)
    if prefetch is not None:
        specs = dict(grid_spec=pltpu.PrefetchScalarGridSpec(num_scalar_prefetch=1, **specs))
        args = (prefetch,) + tuple(args)
    outs = pl.pallas_call(kernel_body, name=name, out_shape=list(out_shape) + extra_shape,
                          input_output_aliases=aliases, compiler_params=params, **specs,
                          )(*args, *(ride.operands if ride is not None else ()))
    return outs if ride is None else (outs[:n_out], outs[n_out:])


def alone(ride, *, name):
    def body(*refs):
        n = len(ride.operands)
        stages = ride.stages(refs[:n], refs[n:n + len(ride.out_shape)], refs[n + len(ride.out_shape):])
        for stage in stages:
            if stage is not None:
                stage()

    return pl.pallas_call(
        body, name=name, in_specs=[HBM_SPEC] * len(ride.operands), out_specs=[HBM_SPEC] * len(ride.out_shape),
        out_shape=list(ride.out_shape), input_output_aliases=dict(ride.aliases), scratch_shapes=list(ride.sems),
    )(*ride.operands)


def _two_pieces(h, part):
    lo, n = (0, h // BF16_ROWS) if part is None else part
    assert n >= 2, (h, part)
    return (lo, n // 2), (lo + n // 2, n - n // 2)


def gather_ride(wholes, metas, small=None):
    n = len(wholes)
    operands, out_shape = list(wholes), [SDS(s.shape, s.dtype) for s in wholes]
    sems = [DMA((n, 4)), DMA((n, 4)), DMA((n, 4)), DMA((n, 4))]
    if small is not None:
        operands.append(small)
        out_shape.append(SDS((N_CHIPS,) + small.shape, small.dtype))
        sems += [DMA((3,)), DMA((3,)), DMA(())]

    def make(ins, outs, sem):
        send1, recv1, send2, recv2 = sem[:4]
        x, y, c = _place()
        p = 2 * x + y
        chips = _other_chips(x, y)
        across_x, across_y, across_both = [2 * qx + qy for qx, qy in chips]
        me, sibling = (x, y, c), (x, y, 1 - c)

        def region(t, q, half, piece=None):
            kind, quarter, part = metas[t]
            if piece is not None:
                part = _two_pieces(quarter[0] // 2, part)[piece]
            return _half_of_quarter(outs[t], kind, quarter, part, q, half)

        first, second, arriving = [], [], []
        landing, passing = [[], [], [], []], [[], [], [], []]
        for j, (qx, qy) in enumerate(chips):
            if small is not None:
                q = 2 * qx + qy
                first.append(_remote(ins[n], outs[n].at[p], sem[4].at[j], sem[5].at[j], (qx, qy, c)))
                arriving.append(_remote(outs[n].at[q], outs[n].at[q], sem[4].at[j], sem[5].at[j], me))
        for t in range(n):
            mine = region(t, p, c)
            for j in range(2):
                first.append(_remote(mine, mine, send1.at[t, j], recv1.at[t, j], chips[j] + (c,)))
            lands = [(across_x, None), (across_y, None), (across_both, 0), (across_both, 1)]
            for k, (q, piece) in enumerate(lands):
                landed, theirs = region(t, q, c, piece), region(t, q, 1 - c, piece)
                landing[k].append(_remote(landed, landed, send1.at[t, k], recv1.at[t, k], me))
                passing[k].append(_remote(landed, landed, send2.at[t, k], recv2.at[t, k], sibling))
                arriving.append(_remote(theirs, theirs, send2.at[t, k], recv2.at[t, k], me))
            onward = region(t, across_x, c, 0)
            second.append(_remote(onward, onward, send1.at[t, 2], recv1.at[t, 2], chips[1] + (c,)))
            onward = region(t, across_y, c, 1)
            second.append(_remote(onward, onward, send1.at[t, 3], recv1.at[t, 3], chips[0] + (c,)))
        local = [] if small is None else [pltpu.make_async_copy(ins[n], outs[n].at[p], sem[6])]

        def start():
            for cp in local + first:
                cp.start()

        def relay():
            for k in range(2):
                for t in range(n):
                    landing[k][t].wait_recv()
                    second[2 * t + k].start()
                    passing[k][t].start()

        def relay_again():
            for k in range(2, 4):
                for t in range(n):
                    landing[k][t].wait_recv()
                    passing[k][t].start()

        def finish():
            for cp in arriving:
                cp.wait_recv()
            for cp in first + second + sum(passing, []):
                cp.wait_send()
            for cp in local:
                cp.wait()

        return start, relay, relay_again, finish

    return Ride(operands, out_shape, {t: t for t in range(n)}, sems, make)


def chip_ride(sums, metas, small=None, earlier=None):
    n = len(sums)
    operands = list(sums)
    out_shape = [SDS((3, s.shape[1], quarter[1]), s.dtype) for s, (_, quarter, _) in zip(sums, metas)]
    sems = [DMA((n, 3)), DMA((n, 3))] if n else []
    if small is not None:
        operands.append(small)
        out_shape.append(SDS((8,) + small.shape, small.dtype))
        sems += [DMA((7,)), DMA((7,)), DMA(())]
    aliases = {}
    for t, buffer in enumerate(earlier or [None] * n):
        if buffer is not None:
            aliases[len(operands)] = t
            operands.append(buffer)

    def make(ins, outs, sem):
        x, y, c = _place()
        cps = []
        for j, (qx, qy) in enumerate(_other_chips(x, y)):
            q = 2 * qx + qy
            for t in range(n):
                kind, (_, ws), part = metas[t]
                rows = pl.ds(*_rows_of(ins[t].shape[1], part))
                if kind == "row":
                    src = ins[t].at[q, rows]
                elif kind == "col":
                    src = ins[t].at[0, rows, pl.ds(pl.multiple_of(q * ws, 128), ws)]
                else:
                    src = ins[t].at[q // 2, rows, pl.ds(pl.multiple_of((q % 2) * ws, 128), ws)]
                cps.append(_remote(src, outs[t].at[j, rows], sem[0].at[t, j], sem[1].at[t, j], (qx, qy, c)))
        local = []
        if small is not None:
            ssend, srecv, lsem = sem[2 * bool(n):2 * bool(n) + 3]
            local.append(pltpu.make_async_copy(ins[n], outs[n].at[0], lsem))
            for k in range(1, 8):
                peer = (x ^ (k >> 2 & 1), y ^ (k >> 1 & 1), c ^ (k & 1))
                cps.append(_remote(ins[n], outs[n].at[k], ssend.at[k - 1], srecv.at[k - 1], peer))

        def start():
            for cp in local + cps:
                cp.start()

        def finish():
            for cp in cps + local:
                cp.wait()

        return start, finish

    return Ride(operands, out_shape, aliases, sems, make)


def pair_ride(grads):
    n = len(grads)

    def make(ins, outs, sem):
        x, y, c = _place()
        cps = [_remote(ins[t].at[:, 1 - c], outs[t], sem[0].at[t], sem[1].at[t], (x, y, 1 - c)) for t in range(n)]

        def start():
            for cp in cps:
                cp.start()

        def finish():
            for cp in cps:
                cp.wait()

        return start, finish

    return Ride(list(grads), [SDS((g.shape[0],) + g.shape[2:], g.dtype) for g in grads], {}, [DMA((n,)), DMA((n,))],
                make)


def half_ride(quarters):
    n = len(quarters)

    def make(ins, outs, sem):
        x, y, c = _place()
        sends = [_remote(outs[t].at[c], outs[t].at[c], sem[0].at[t], sem[1].at[t], (x, y, 1 - c)) for t in range(n)]

        def start():
            for cp in sends:
                cp.start()

        def finish():
            for t in range(n):
                theirs = outs[t].at[1 - c]
                _remote(theirs, theirs, sem[0].at[t], sem[1].at[t], (x, y, c)).wait_recv()
            for cp in sends:
                cp.wait_send()

        return start, finish

    return Ride(list(quarters), [SDS(q.shape, q.dtype) for q in quarters], {t: t for t in range(n)},
                [DMA((n,)), DMA((n,))], make)


CAST_STEPS = 4


def cast_quarters(sources, p_arr, *, name, ride=None):
    n = len(sources)
    in_specs, out_specs, out_shape = [], [], []
    for w, layer, kind in sources:
        _, r, ws = w.shape
        tr = r // CAST_STEPS
        assert tr % BF16_ROWS == 0, w.shape
        in_specs.append(pl.BlockSpec((None, tr, ws), lambda i, p_ref, layer=layer: (layer, i, 0)))
        out_specs.append(pl.BlockSpec((tr, ws), (lambda i, p_ref: (p_ref[0] * CAST_STEPS + i, 0)) if kind == "row"
                                      else (lambda i, p_ref: (i, p_ref[0]))))
        out_shape.append(SDS(_full_shape(kind, (r, ws)), BF16))

    def body(p_ref, *refs):
        for w_ref, o_ref in zip(refs[:n], refs[n:]):
            o_ref[...] = w_ref[...].astype(BF16)

    return _call(body, name=name, grid=(CAST_STEPS,), in_specs=in_specs, out_specs=out_specs, out_shape=out_shape,
                 semantics=("parallel",), args=[w for w, _, _ in sources], ride=ride, prefetch=p_arr)


def _chip_part(kind, W):
    if kind == "row":
        return W, lambda q: (q, 0)
    if kind == "col":
        return W // N_CHIPS, lambda q: (0, q)
    return W // 2, lambda q: (q // 2, q % 2)


def pair_add(own, got, kind, pc_arr, *, name):
    A, _, h, W = own.shape
    wb, where = _chip_part(kind, W)
    th = _row_tile(h, max(BF16_ROWS, (3 << 19) // wb), BF16_ROWS)

    def body(pc_ref, a_ref, b_ref, o_ref):
        o_ref[...] = (a_ref[...].astype(F32) + b_ref[...].astype(F32)).astype(BF16)

    def other(k, pc_ref):
        return where(k + (k >= pc_ref[0]).astype(jnp.int32))

    def mine(k, i, pc_ref):
        entry, block = other(k, pc_ref)
        return entry, pc_ref[1], i, block

    def theirs(k, i, pc_ref):
        entry, block = other(k, pc_ref)
        return entry, i, block

    return pl.pallas_call(
        body, name=name,
        grid_spec=pltpu.PrefetchScalarGridSpec(
            num_scalar_prefetch=1, grid=(N_CHIPS - 1, h // th),
            in_specs=[pl.BlockSpec((None, None, th, wb), mine), pl.BlockSpec((None, th, wb), theirs)],
            out_specs=pl.BlockSpec((None, th, wb), theirs)),
        out_shape=SDS((A, h, W), BF16),
        compiler_params=_params("parallel", "parallel"),
    )(pc_arr, own, got)


REDUCE_STEPS = 2


def chip_reduce(owns, theirs, got, kinds, pc_arr, *, name, ride=None):
    n = len(owns)
    a_specs, t_specs, b_specs, o_specs, out_shape = [], [], [], [], []
    for own, g, kind in zip(owns, got, kinds):
        _, h, ws = g.shape
        wb, where = _chip_part(kind, own.shape[3])
        th = h // REDUCE_STEPS
        assert th % BF16_ROWS == 0 and wb == ws, (g.shape, own.shape)
        a_specs.append(pl.BlockSpec((None, None, th, ws), lambda i, pc_ref, where=where: (
            where(pc_ref[0])[0], pc_ref[1], i, where(pc_ref[0])[1])))
        t_specs.append(pl.BlockSpec((None, th, ws), lambda i, pc_ref, where=where: (
            where(pc_ref[0])[0], i, where(pc_ref[0])[1])))
        b_specs.append(pl.BlockSpec((3, th, ws), lambda i, pc_ref: (0, i, 0)))
        o_specs.append(pl.BlockSpec((None, th, ws), lambda i, pc_ref: (pc_ref[1], i, 0)))
        out_shape.append(SDS((2, h, ws), F32))

    def body(pc_ref, *refs):
        for a_ref, t_ref, b_ref, o_ref in zip(refs[:n], refs[n:2 * n], refs[2 * n:3 * n], refs[3 * n:]):
            pair = a_ref[...].astype(F32) + t_ref[...].astype(F32)
            o_ref[...] = ((pair + b_ref[0].astype(F32)) + b_ref[1].astype(F32)) + b_ref[2].astype(F32)

    return _call(body, name=name, grid=(REDUCE_STEPS,), in_specs=a_specs + t_specs + b_specs, out_specs=o_specs,
                 out_shape=out_shape, semantics=("parallel",), args=list(owns) + list(theirs) + list(got),
                 prefetch=pc_arr, ride=ride)


def small_reduce(blocks, me_arr):
    _, rows, D = blocks.shape

    def body(me_ref, b_ref, o_ref):
        me = me_ref[0]
        total = b_ref[me]
        for d in range(1, 8):
            total = total + b_ref[d ^ me]
        o_ref[...] = total

    return pl.pallas_call(
        body, name="small_reduce",
        grid_spec=pltpu.PrefetchScalarGridSpec(
            num_scalar_prefetch=1, grid=(1,),
            in_specs=[pl.BlockSpec((8, rows, D), lambda i, me_ref: (0, 0, 0))],
            out_specs=pl.BlockSpec((rows, D), lambda i, me_ref: (0, 0))),
        out_shape=SDS((rows, D), F32),
        compiler_params=_params("arbitrary"),
    )(me_arr, blocks)


def _adam(w, g, m, v):
    m_new = ADAM_B1 * m + (1.0 - ADAM_B1) * g
    v_new = ADAM_B2 * v + (1.0 - ADAM_B2) * (g * g)
    m_hat = m_new / (1.0 - ADAM_B1 ** ADAM_STEP)
    v_hat = v_new / (1.0 - ADAM_B2 ** ADAM_STEP)
    return -ADAM_LR * (m_hat / (jnp.sqrt(v_hat) + ADAM_EPS) + ADAM_WD * w), m_new, v_new


def adamw_rows(block, p_arr, leaves, *, name):
    L = len(leaves)

    def body(p_ref, b_ref, *refs):
        outs = refs[3 * L:]
        for i, (w, _, _, row, sharded) in enumerate(leaves):
            n, width = w.shape[-2:]
            cols = pl.ds(pl.multiple_of(p_ref[0] * width, 128), width) if sharded else slice(0, width)
            g = b_ref[row:row + n, cols].reshape(w.shape)
            results = (g,) + _adam(refs[i][...], g, refs[L + i][...], refs[2 * L + i][...])
            for o_ref, value in zip(outs[4 * i:4 * i + 4], results):
                o_ref[...] = value

    whole = lambda a: pl.BlockSpec(a.shape, lambda i, p_ref, nd=len(a.shape): (0,) * nd)
    arrays = [leaf[k] for k in range(3) for leaf in leaves]
    shapes = [SDS(leaf[0].shape, F32) for leaf in leaves for _ in range(4)]
    outs = pl.pallas_call(
        body, name=name,
        grid_spec=pltpu.PrefetchScalarGridSpec(
            num_scalar_prefetch=1, grid=(1,), in_specs=[whole(block)] + [whole(a) for a in arrays],
            out_specs=[whole(s) for s in shapes]),
        out_shape=shapes, compiler_params=_params("arbitrary"),
    )(p_arr, block, *arrays)
    return [outs[4 * i:4 * i + 4] for i in range(L)]


def adamw(w, gs, m, v, *, name):
    L, r, cols = w.shape
    tr = _row_tile(r, 256)
    nt = r // tr

    def body(*refs):
        w_ref, m_ref, v_ref = refs[:3]
        g_refs = refs[3:3 + L]
        g_out, d_out, m_out, v_out = refs[3 + L:]
        layer = pl.program_id(0)
        g = g_refs[0][...]
        for l in range(1, L):
            g = jnp.where(layer == l, g_refs[l][...], g)
        g_out[...] = g
        d_out[...], m_out[...], v_out[...] = _adam(w_ref[...], g, m_ref[...], v_ref[...])

    full = pl.BlockSpec((None, tr, cols), lambda l, i: (l, i, 0))
    g_spec = lambda l0: pl.BlockSpec((tr, cols), lambda l, i: (jnp.where(l == l0, i, jnp.where(l < l0, 0, nt - 1)), 0))
    return pl.pallas_call(
        body, name=name, grid=(L, nt),
        in_specs=[full, full, full] + [g_spec(l0) for l0 in range(L)],
        out_specs=[full] * 4,
        out_shape=[SDS(w.shape, F32)] * 4,
        compiler_params=_params("arbitrary", "arbitrary"),
    )(w, m, v, *gs)


def _rms_r(xf):
    return lax.rsqrt(jnp.mean(xf * xf, axis=-1, keepdims=True) + EPS)


def _rmsnorm_bwd(xf, g, dy):
    r = _rms_r(xf)
    xh = xf * r
    gd = g * dy
    return r * (gd - xh * jnp.mean(xh * gd, axis=-1, keepdims=True)), xh


def _dot(a, b):
    return jnp.dot(a, b, preferred_element_type=F32)


def _dot_nt(a, b):
    return lax.dot_general(a, b, (((1,), (1,)), ((), ())), preferred_element_type=F32)


def _dot_tn(a, b):
    return lax.dot_general(a, b, (((0,), (0,)), ((), ())), preferred_element_type=F32)


def _accumulate(ref, first, value):
    @pl.when(first)
    def _():
        ref[...] = value

    @pl.when(jnp.logical_not(first))
    def _():
        ref[...] += value


def norm_matmul(x, g, w, *, tn, split, name, ride=None, tm=ROW_TILE):
    T, D = x.shape
    N = w.shape[1]
    per = N // split // tn

    def body(x_ref, g_ref, w_ref, o_ref, xn_ref):
        @pl.when(pl.program_id(1) == 0)
        def _():
            xf = x_ref[...].astype(F32)
            xn_ref[...] = (xf * _rms_r(xf) * g_ref[...]).astype(BF16)

        o_ref[...] = _dot(xn_ref[...], w_ref[...]).astype(BF16)

    return _call(
        body, name=name, grid=(T // tm, N // tn),
        in_specs=[pl.BlockSpec((tm, D), lambda i, j: (i, 0)),
                  pl.BlockSpec((1, D), lambda i, j: (0, 0)),
                  pl.BlockSpec((D, tn), lambda i, j: (0, j))],
        out_specs=[pl.BlockSpec((None, tm, tn), lambda i, j: (j // per, i, j % per)),
                   pl.BlockSpec((tm, D), lambda i, j: (i, 0))],
        out_shape=[SDS((split, T, N // split), BF16), SDS((T, D), BF16)],
        semantics=("parallel", "arbitrary"), args=(x, g, w), ride=ride)


BIG_ROW_TILE = 1024


def norm2_matmul(x, gains, weights, *, name, ride=None, tm=BIG_ROW_TILE):
    T, D = x.shape
    tm = min(tm, T)
    n = len(gains)

    def body(x_ref, *refs):
        subs = _sub_tiles(tm)
        xhs = []
        for rows in subs:
            xf = x_ref[rows, :].astype(F32)
            xhs.append(xf * _rms_r(xf))
        for g_ref, w_ref, o_ref, xn_ref in zip(refs[:n], refs[n:2 * n], refs[2 * n::2], refs[2 * n + 1::2]):
            for rows, xh in zip(subs, xhs):
                xn = (xh * g_ref[...]).astype(BF16)
                xn_ref[rows, :] = xn
                o_ref[rows, :] = _dot(xn, w_ref[...]).astype(BF16)

    row = pl.BlockSpec((tm, D), lambda i: (i, 0))
    vec = pl.BlockSpec((1, D), lambda i: (0, 0))
    out_specs, out_shape = [], []
    for w in weights:
        out_specs += [pl.BlockSpec((tm, w.shape[1]), lambda i: (i, 0)), row]
        out_shape += [SDS((T, w.shape[1]), BF16), SDS((T, D), BF16)]
    return _call(
        body, name=name, grid=(T // tm,),
        in_specs=[row] + [vec] * n + [pl.BlockSpec(w.shape, lambda i: (0, 0)) for w in weights],
        out_specs=out_specs, out_shape=out_shape, semantics=("parallel",), args=[x] + list(gains) + list(weights),
        ride=ride)


def _shift_down(prev, cur, by):
    big = jnp.concatenate([prev, cur], axis=0)
    return pltpu.roll(big, by, 0)[prev.shape[0]:]


def _shift_up(cur, nxt, by):
    big = jnp.concatenate([cur, nxt], axis=0)
    return pltpu.roll(big, big.shape[0] - by, 0)[:cur.shape[0]]


def conv_mix_out(bcx, conv_w, w_out, g_post, res, *, name, ride=None, tm=ROW_TILE):
    T, D = res.shape
    hb = tm // BF16_ROWS

    def body(b_ref, c_ref, u_ref, cp_ref, up_ref, cw_ref, w_ref, g_ref, r_ref, h_ref, z_ref, y_ref):
        i = pl.program_id(0)
        cu = c_ref[...].astype(F32) * u_ref[...].astype(F32)
        cup = cp_ref[...].astype(F32) * up_ref[...].astype(F32)
        cup = jnp.where(i == 0, 0.0, cup)
        cv = (cw_ref[0:1, :] * _shift_down(cup, cu, 2) + cw_ref[1:2, :] * _shift_down(cup, cu, 1)
              + cw_ref[2:3, :] * cu)
        y = (b_ref[...].astype(F32) * cv).astype(BF16)
        y_ref[...] = y
        z = _dot(y, w_ref[...])
        z_ref[...] = z.astype(BF16)
        h_ref[...] = (r_ref[...] + z * _rms_r(z) * g_ref[...]).astype(STREAM)

    tile = lambda col: pl.BlockSpec((tm, D), lambda i: (i, col))
    halo = lambda col: pl.BlockSpec((BF16_ROWS, D), lambda i: (jnp.maximum(i * hb - 1, 0), col))
    row = pl.BlockSpec((tm, D), lambda i: (i, 0))
    return _call(
        body, name=name, grid=(T // tm,),
        in_specs=[tile(0), tile(1), tile(2), halo(1), halo(2),
                  pl.BlockSpec((3, D), lambda i: (0, 0)),
                  pl.BlockSpec((D, D), lambda i: (0, 0)),
                  pl.BlockSpec((1, D), lambda i: (0, 0)), row],
        out_specs=[row, row, row],
        out_shape=[SDS((T, D), STREAM), SDS((T, D), BF16), SDS((T, D), BF16)],
        semantics=("parallel",), args=(bcx, bcx, bcx, bcx, bcx, conv_w, w_out, g_post, res), ride=ride)


def _normbwd_then_nt(dh, zf, g_ref, w_ref, dz_ref, dg_ref, o_ref, first):
    dz, zh = _rmsnorm_bwd(zf, g_ref[...], dh)
    dz = dz.astype(BF16)
    dz_ref[...] = dz
    _accumulate(dg_ref, first, jnp.sum(dh * zh, axis=0, keepdims=True))
    o_ref[...] = _dot_nt(dz, w_ref[...]).astype(BF16)


def _then_specs(then, tm, T, D):
    z, g, w = then
    K = w.shape[0]
    row = pl.BlockSpec((tm, D), lambda i: (i, 0))
    vec = pl.BlockSpec((1, D), lambda i: (0, 0))
    in_specs = [row, vec, pl.BlockSpec((K, D), lambda i: (0, 0), pipeline_mode=pl.Buffered(1))]
    out_specs = [row, vec, pl.BlockSpec((tm, K), lambda i: (i, 0))]
    out_shape = [SDS((T, D), BF16), SDS((1, D), F32), SDS((T, K), BF16)]
    return in_specs, out_specs, out_shape


def plain_mix_out(a, w, g_post, res, *, name, target=None, ride=None, tm=ROW_TILE):
    T, D = res.shape
    tm = min(tm, T)
    K = a.shape[1]
    with_loss = target is not None

    def body(a_ref, w_ref, g_ref, r_ref, *rest):
        subs = _sub_tiles(tm)
        zs = [_dot(a_ref[rows, :], w_ref[...]) for rows in subs]
        if not with_loss:
            h_ref, z_ref = rest
            for rows, z in zip(subs, zs):
                h_ref[rows, :] = (r_ref[rows, :].astype(F32) + z * _rms_r(z) * g_ref[...]).astype(STREAM)
                z_ref[rows, :] = z.astype(BF16)
            return
        t_ref, h_ref, dz_ref, dg_ref, da_ref, loss_ref = rest
        first = pl.program_id(0) == 0
        loss, dg = jnp.zeros((), F32), jnp.zeros((1, D), F32)
        for rows, z in zip(subs, zs):
            diff = r_ref[rows, :].astype(F32) + z * _rms_r(z) * g_ref[...] - t_ref[rows, :]
            dh = diff * (1.0 / D)
            h_ref[rows, :] = dh.astype(STREAM)
            loss = loss + jnp.sum(diff * diff)
            dz, zh = _rmsnorm_bwd(z, g_ref[...], dh)
            dz = dz.astype(BF16)
            dz_ref[rows, :] = dz
            dg = dg + jnp.sum(dh * zh, axis=0, keepdims=True)
            da_ref[rows, :] = _dot_nt(dz, w_ref[...]).astype(BF16)
        _accumulate(loss_ref, first, jnp.full(loss_ref.shape, 0.5 / D, F32) * loss)
        _accumulate(dg_ref, first, dg)

    row = pl.BlockSpec((tm, D), lambda i: (i, 0))
    vec = pl.BlockSpec((1, D), lambda i: (0, 0))
    in_specs = [pl.BlockSpec((tm, K), lambda i: (i, 0)), pl.BlockSpec((K, D), lambda i: (0, 0)), vec, row]
    if with_loss:
        in_specs.append(row)
        out_specs = [row, row, vec, pl.BlockSpec((tm, K), lambda i: (i, 0)), pl.BlockSpec((8, 128), lambda i: (0, 0))]
        out_shape = [SDS((T, D), STREAM), SDS((T, D), BF16), SDS((1, D), F32), SDS((T, K), BF16), SDS((8, 128), F32)]
    else:
        out_specs, out_shape = [row, row], [SDS((T, D), STREAM), SDS((T, D), BF16)]
    return _call(
        body, name=name, grid=(T // tm,), in_specs=in_specs, out_specs=out_specs, out_shape=out_shape,
        semantics=("arbitrary",), args=(a, w, g_post, res) + ((target,) if with_loss else ()), ride=ride)


def _silu_grads(d, g, u):
    sg = jax.nn.sigmoid(g)
    return d * u * (sg * (1.0 + g * (1.0 - sg))), d * (g * sg)


def _sub_tiles(tm):
    return [pl.ds(k, min(MXU_WIDTH, tm)) for k in range(0, tm, MXU_WIDTH)]


def norm_swiglu_in(x, g, w, *, name, ride=None, tm=ROW_TILE):
    T, D = x.shape
    F = w.shape[1] // 2

    def body(x_ref, g_ref, wg_ref, wu_ref, gu_ref, a_ref, xt_ref):
        subs = _sub_tiles(tm)
        xns = []
        for rows in subs:
            xf = x_ref[rows, :].astype(F32)
            xns.append(xf * _rms_r(xf) * g_ref[...])
        xbs = [xn.astype(BF16) for xn in xns]
        gates = [_dot(xb, wg_ref[...]).astype(BF16) for xb in xbs]
        ups = [_dot(xb, wu_ref[...]).astype(BF16) for xb in xbs]
        for rows, gate, up in zip(subs, gates, ups):
            gu_ref[0, rows, :] = gate
            gu_ref[1, rows, :] = up
            a_ref[rows, :] = gate * jax.nn.sigmoid(gate) * up
        for rows, xn in zip(subs, xns):
            xt_ref[:, rows] = xn.T.astype(BF16)

    half = lambda s: pl.BlockSpec((D, F), lambda i: (0, s), pipeline_mode=pl.Buffered(1))
    return _call(
        body, name=name, grid=(T // tm,),
        in_specs=[pl.BlockSpec((tm, D), lambda i: (i, 0)), pl.BlockSpec((1, D), lambda i: (0, 0)), half(0), half(1)],
        out_specs=[pl.BlockSpec((2, tm, F), lambda i: (0, i, 0)), pl.BlockSpec((tm, F), lambda i: (i, 0)),
                   pl.BlockSpec((D, tm), lambda i: (0, i))],
        out_shape=[SDS((2, T, F), BF16), SDS((T, F), BF16), SDS((D, T), BF16)],
        semantics=("parallel",), args=(x, g, w, w), ride=ride)


def swiglu_bwd_tn(xt, dact, gu, *, name, ride=None, tb=MXU_WIDTH):
    D, T = xt.shape
    F = dact.shape[1]

    def body(xt_ref, d_ref, g_ref, u_ref, o_ref):
        dg, du = _silu_grads(d_ref[...], g_ref[...], u_ref[...])
        o_ref[0] = _dot(xt_ref[...], dg).astype(BF16)
        o_ref[1] = _dot(xt_ref[...], du).astype(BF16)

    col = lambda s: pl.BlockSpec((None, T, tb), lambda j: (s, 0, j))
    out = _call(
        body, name=name, grid=(F // tb,),
        in_specs=[pl.BlockSpec((D, T), lambda j: (0, 0), pipeline_mode=pl.Buffered(1)),
                  pl.BlockSpec((T, tb), lambda j: (0, j)), col(0), col(1)],
        out_specs=[pl.BlockSpec((2, D, tb), lambda j: (0, 0, j))],
        out_shape=[SDS((2, D, F), BF16)],
        semantics=("parallel",), args=(xt, dact, gu, gu), ride=ride)
    return out[0] if ride is None else (out[0][0], out[1])


def swiglu_bwd_in(dact, gu, w, h_in, g, dh_out, then, *, name, ride=None, tm=ROW_TILE):
    T, D = h_in.shape
    F = dact.shape[1]

    def body(d_ref, gg_ref, uu_ref, wg_ref, wu_ref, h_ref, g_ref, dh_ref, z_ref, g2_ref, w2_ref,
             o_ref, dg_ref, dz_ref, dg2_ref, da_ref):
        first = pl.program_id(0) == 0
        subs = _sub_tiles(tm)
        dns = []
        for rows in subs:
            dgate, dup = _silu_grads(d_ref[rows, :], gg_ref[rows, :], uu_ref[rows, :])
            dns.append(_dot_nt(dgate, wg_ref[...]) + _dot_nt(dup, wu_ref[...]))
        dg, dg2 = jnp.zeros((1, D), F32), jnp.zeros((1, D), F32)
        for rows, dn in zip(subs, dns):
            dx, hh = _rmsnorm_bwd(h_ref[rows, :].astype(F32), g_ref[...], dn)
            dh_in = dh_ref[rows, :] + dx
            o_ref[rows, :] = dh_in.astype(STREAM)
            dg = dg + jnp.sum(dn * hh, axis=0, keepdims=True)
            dz, zh = _rmsnorm_bwd(z_ref[rows, :].astype(F32), g2_ref[...], dh_in)
            dz = dz.astype(BF16)
            dz_ref[rows, :] = dz
            dg2 = dg2 + jnp.sum(dh_in * zh, axis=0, keepdims=True)
            da_ref[rows, :] = _dot_nt(dz, w2_ref[...]).astype(BF16)
        _accumulate(dg_ref, first, dg)
        _accumulate(dg2_ref, first, dg2)

    row = pl.BlockSpec((tm, D), lambda i: (i, 0))
    vec = pl.BlockSpec((1, D), lambda i: (0, 0))
    part = lambda s: pl.BlockSpec((None, tm, F), lambda i: (s, i, 0))
    half = lambda s: pl.BlockSpec((D, F), lambda i: (0, s), pipeline_mode=pl.Buffered(1))
    then_in, then_out, then_shape = _then_specs(then, tm, T, D)
    return _call(
        body, name=name, grid=(T // tm,),
        in_specs=[pl.BlockSpec((tm, F), lambda i: (i, 0)), part(0), part(1), half(0), half(1), row, vec, row] + then_in,
        out_specs=[row, vec] + then_out,
        out_shape=[SDS((T, D), STREAM), SDS((1, D), F32)] + then_shape,
        semantics=("arbitrary",), args=(dact, gu, gu, w, w, h_in, g, dh_out) + tuple(then), ride=ride)


def rope_tables(T):
    half = ROT_DIM // 2
    inv_freq = ROPE_THETA ** (-jnp.arange(0, ROT_DIM, 2, dtype=F32) / ROT_DIM)
    ang = (jnp.arange(T, dtype=F32)[:, None] * inv_freq[None, :]).T
    cos, sin = jnp.cos(ang), jnp.sin(ang)
    rest = HEAD_DIM - ROT_DIM
    one, zero = jnp.ones((rest, T), F32), jnp.zeros((rest, T), F32)
    zh = jnp.zeros((half, T), F32)
    fac = jnp.concatenate([cos, cos, one], axis=0)
    up = jnp.concatenate([-sin, zh, zero], axis=0)
    down = jnp.concatenate([zh, sin, zero], axis=0)
    return jnp.stack([fac, up, down])


def _rope(t, tab):
    half = ROT_DIM // 2
    return t * tab[0] + pltpu.roll(t, HEAD_DIM - half, 0) * tab[1] + pltpu.roll(t, half, 0) * tab[2]


def _rope_t(d, tab):
    half = ROT_DIM // 2
    return d * tab[0] + pltpu.roll(d * tab[1], half, 0) + pltpu.roll(d * tab[2], HEAD_DIM - half, 0)


def _head(t, h):
    return t[h * HEAD_DIM:(h + 1) * HEAD_DIM]


def _band(n, group):
    kj = lax.broadcasted_iota(jnp.int32, (2 * BLOCK, BLOCK), 0)
    qi = lax.broadcasted_iota(jnp.int32, (2 * BLOCK, BLOCK), 1)
    mask = (kj > qi) & (kj <= qi + BLOCK) & ((n > 0) | (kj >= BLOCK))
    return jnp.tile(mask, (1, group))


def _attn_specs(D, kvd, nb):
    cur = lambda n: jnp.minimum(n, nb - 1)
    prev = lambda n: jnp.maximum(cur(n) - 1, 0)
    return [pl.BlockSpec((BLOCK, D), lambda n: (cur(n), 0)),
            pl.BlockSpec((BLOCK, kvd), lambda n: (prev(n), 0)),
            pl.BlockSpec((BLOCK, kvd), lambda n: (cur(n), 0)),
            pl.BlockSpec((BLOCK, kvd), lambda n: (prev(n), 1)),
            pl.BlockSpec((BLOCK, kvd), lambda n: (cur(n), 1)),
            pl.BlockSpec((3, HEAD_DIM, BLOCK), lambda n: (0, 0, prev(n))),
            pl.BlockSpec((3, HEAD_DIM, BLOCK), lambda n: (0, 0, cur(n))),
            pl.BlockSpec(memory_space=pltpu.SMEM)]


def _attn_operands(q_ref, kp_ref, k_ref, vp_ref, v_ref, tp_ref, t_ref):
    flip = lambda ref: ref[...].astype(F32).T
    tab = t_ref[...]
    kt = jnp.concatenate([flip(kp_ref), flip(k_ref)], axis=1)
    vt = jnp.concatenate([flip(vp_ref), flip(v_ref)], axis=1)
    return flip(q_ref), kt, vt, tab, jnp.concatenate([tp_ref[...], tab], axis=2)


SCORE_SCALE = 1.0 / math.sqrt(HEAD_DIM)
HEADS_TOGETHER = 4


def _group_heads(t, first, count, tab=None):
    heads = [_head(t, first + g) for g in range(count)]
    if tab is not None:
        heads = [_rope(h, tab) * SCORE_SCALE for h in heads]
    return jnp.concatenate(heads, axis=1).astype(BF16)


def _sink_row(s_ref, first, count):
    which = lax.broadcasted_iota(jnp.int32, (1, count * BLOCK), 1) // BLOCK
    row = jnp.zeros((1, count * BLOCK), F32)
    for g in range(count):
        row = jnp.where(which == g, s_ref[0, first + g], row)
    return row


def _sum_keys(t):
    return _dot(jnp.ones((8, t.shape[0]), BF16), t)[0:1]


def _softmax(scores, sink, mask):
    s = jnp.where(mask, scores.astype(BF16), NEG)
    m = jnp.maximum(jnp.max(s, axis=0, keepdims=True).astype(F32), sink).astype(BF16)
    e = jnp.exp(s - m)
    m = m.astype(F32)
    return e, m, 1.0 / (_sum_keys(e) + jnp.exp(sink - m))


def _per_head(row, count):
    return [row[:, g * BLOCK:(g + 1) * BLOCK] for g in range(count)]


def attention_fwd(q, kv, tabs, sinks, *, name, ride=None):
    T, D = q.shape
    kvd = kv.shape[1] // 2
    heads = D // HEAD_DIM
    group = heads // N_KV_HEADS

    def body(q_ref, kp_ref, k_ref, vp_ref, v_ref, tp_ref, t_ref, s_ref, o_ref, stat_ref):
        gs = HEADS_TOGETHER
        mask = _band(pl.program_id(0), gs)
        qt, kt, vt, tab, tab2 = _attn_operands(q_ref, kp_ref, k_ref, vp_ref, v_ref, tp_ref, t_ref)
        firsts = [(j, first) for j in range(N_KV_HEADS) for first in range(j * group, (j + 1) * group, gs)]
        ks = [_rope(_head(kt, j), tab2).astype(BF16) for j in range(N_KV_HEADS)]
        scores = [_dot_tn(ks[j], _group_heads(qt, first, gs, tab)) for j, first in firsts]
        soft = [_softmax(s, _sink_row(s_ref, first, gs), mask) for s, (j, first) in zip(scores, firsts)]
        outs, ms, invs = [], [], []
        for (e, m, inv), (j, first) in zip(soft, firsts):
            o = _dot(_head(vt, j).astype(BF16), e) * inv
            outs += [o[:, g * BLOCK:(g + 1) * BLOCK] for g in range(gs)]
            ms += _per_head(m, gs)
            invs += _per_head(inv, gs)
        o_ref[...] = jnp.concatenate(outs, axis=0).T.astype(BF16)
        stat_ref[0] = jnp.concatenate(ms, axis=0)
        stat_ref[1] = jnp.concatenate(invs, axis=0)

    return _call(
        body, name=name, grid=(T // BLOCK,),
        in_specs=_attn_specs(D, kvd, T // BLOCK),
        out_specs=[pl.BlockSpec((BLOCK, D), lambda n: (n, 0)), pl.BlockSpec((2, heads, BLOCK), lambda n: (0, 0, n))],
        out_shape=[SDS((T, D), BF16), SDS((2, heads, T), F32)],
        semantics=("parallel",), args=(q, kv, kv, kv, kv, tabs, tabs, sinks), ride=ride)


def attention_bwd(q, kv, tabs, sinks, do, o, stats, *, name, ride=None):
    T, D = q.shape
    kvd = kv.shape[1] // 2
    heads = D // HEAD_DIM
    group = heads // N_KV_HEADS
    nb = T // BLOCK

    def body(q_ref, kp_ref, k_ref, vp_ref, v_ref, tp_ref, t_ref, s_ref, do_ref, o_ref, stat_ref,
             dq_ref, dkv_ref, ds_ref, carry):
        n = pl.program_id(0)

        @pl.when(n == 0)
        def _():
            carry[...] = jnp.zeros_like(carry)

        @pl.when(n < nb)
        def _():
            block(n, q_ref, kp_ref, k_ref, vp_ref, v_ref, tp_ref, t_ref, s_ref, do_ref, o_ref, stat_ref,
                  dq_ref, dkv_ref, ds_ref, carry)

        @pl.when(n == nb)
        def _():
            dkv_ref[...] = carry[...].astype(BF16)

    def block(n, q_ref, kp_ref, k_ref, vp_ref, v_ref, tp_ref, t_ref, s_ref, do_ref, o_ref, stat_ref,
              dq_ref, dkv_ref, ds_ref, carry):
        gs = HEADS_TOGETHER
        mask = _band(n, gs)
        qt, kt, vt, tab, tab2 = _attn_operands(q_ref, kp_ref, k_ref, vp_ref, v_ref, tp_ref, t_ref)
        dot = do_ref[...].astype(F32).T
        odo = o_ref[...].astype(F32).T * dot
        dl_all = jnp.concatenate([jnp.sum(_head(odo, h), axis=0, keepdims=True) for h in range(heads)], axis=0)
        m_all, inv_all = stat_ref[0], stat_ref[1]
        row = lambda t, first: jnp.concatenate([t[first + g:first + g + 1] for g in range(gs)], axis=1)
        lane = lax.broadcasted_iota(jnp.int32, (8, 128), 1)
        dsink = jnp.zeros((8, 128), F32)
        firsts = [(j, first) for j in range(N_KV_HEADS) for first in range(j * group, (j + 1) * group, gs)]
        ks = [_rope(_head(kt, j), tab2).astype(BF16) for j in range(N_KV_HEADS)]
        vs = [_head(vt, j).astype(BF16) for j in range(N_KV_HEADS)]
        qs = [_group_heads(qt, first, gs, tab) for _, first in firsts]
        dos = [_group_heads(dot, first, gs) for _, first in firsts]
        scores = [_dot_tn(ks[j], q) for q, (j, _) in zip(qs, firsts)]
        dps = [_dot_tn(vs[j], do) for do, (j, _) in zip(dos, firsts)]
        ps, dscs = [], []
        for s, dp, (j, first) in zip(scores, dps, firsts):
            m, inv, dl = row(m_all, first), row(inv_all, first), row(dl_all, first)
            e = jnp.exp(jnp.where(mask, s.astype(BF16), NEG) - m.astype(BF16))
            p = e * inv.astype(BF16)
            dscs.append(p * (dp.astype(BF16) - dl.astype(BF16)))
            ps.append(p)
            weight = jnp.exp(_sink_row(s_ref, first, gs) - m) * inv * dl
            for g in range(gs):
                dsink = dsink - jnp.where(lane == first + g, jnp.sum(weight[:, g * BLOCK:(g + 1) * BLOCK]), 0.0)
        dqs = []
        dks = [jnp.zeros((HEAD_DIM, 2 * BLOCK), F32) for _ in range(N_KV_HEADS)]
        dvs = [jnp.zeros((HEAD_DIM, 2 * BLOCK), F32) for _ in range(N_KV_HEADS)]
        for p, dsc, q, do, (j, _) in zip(ps, dscs, qs, dos, firsts):
            dq = _dot(ks[j], dsc) * SCORE_SCALE
            dqs += [_rope_t(dq[:, g * BLOCK:(g + 1) * BLOCK], tab) for g in range(gs)]
            dks[j] = dks[j] + _dot_nt(q, dsc)
            dvs[j] = dvs[j] + _dot_nt(do, p)
        dks = [_rope_t(dk, tab2) for dk in dks]
        dq_ref[...] = jnp.concatenate(dqs, axis=0).T.astype(BF16)
        dkv = jnp.concatenate(dks + dvs, axis=0)
        dkv_ref[...] = (carry[...] + dkv[:, :BLOCK].T).astype(BF16)
        carry[...] = dkv[:, BLOCK:].T
        _accumulate(ds_ref, n == 0, dsink)

    cur = lambda n: jnp.minimum(n, nb - 1)
    blk = lambda w: pl.BlockSpec((BLOCK, w), lambda n: (cur(n), 0))
    return _call(
        body, name=name, grid=(nb + 1,),
        in_specs=_attn_specs(D, kvd, nb) + [blk(D), blk(D), pl.BlockSpec((2, heads, BLOCK), lambda n: (0, 0, cur(n)))],
        out_specs=[blk(D), pl.BlockSpec((BLOCK, 2 * kvd), lambda n: (jnp.maximum(n - 1, 0), 0)),
                   pl.BlockSpec((8, 128), lambda n: (0, 0))],
        out_shape=[SDS((T, D), BF16), SDS((T, 2 * kvd), BF16), SDS((8, 128), F32)],
        scratch_shapes=[pltpu.VMEM((BLOCK, 2 * kvd), F32)],
        semantics=("arbitrary",), args=(q, kv, kv, kv, kv, tabs, tabs, sinks, do, o, stats), ride=ride)


def matmul_nt_normbwd(da, w, h_in, g, dh_out, *, name, ride=None, tm=ROW_TILE):
    T, D = h_in.shape
    S, _, K = da.shape

    def body(*refs):
        da_refs, w_refs = refs[:S], refs[S:2 * S]
        h_ref, g_ref, dh_ref, o_ref, dg_ref = refs[2 * S:]
        subs = _sub_tiles(tm)
        dns = []
        for rows in subs:
            dn = _dot_nt(da_refs[0][rows, :], w_refs[0][...])
            for s in range(1, S):
                dn = dn + _dot_nt(da_refs[s][rows, :], w_refs[s][...])
            dns.append(dn)
        dg = jnp.zeros((1, D), F32)
        for rows, dn in zip(subs, dns):
            dx, hh = _rmsnorm_bwd(h_ref[rows, :].astype(F32), g_ref[...], dn)
            o_ref[rows, :] = dh_ref[rows, :] + dx
            dg = dg + jnp.sum(dn * hh, axis=0, keepdims=True)
        _accumulate(dg_ref, pl.program_id(0) == 0, dg)

    row = pl.BlockSpec((tm, D), lambda i: (i, 0))
    vec = pl.BlockSpec((1, D), lambda i: (0, 0))
    part = lambda s: pl.BlockSpec((None, tm, K), lambda i: (s, i, 0))
    cols = lambda s: pl.BlockSpec((D, K), lambda i: (0, s), pipeline_mode=pl.Buffered(1))
    return _call(
        body, name=name, grid=(T // tm,),
        in_specs=[part(s) for s in range(S)] + [cols(s) for s in range(S)] + [row, vec, row],
        out_specs=[row, vec],
        out_shape=[SDS((T, D), F32), SDS((1, D), F32)],
        semantics=("arbitrary",), args=[da] * S + [w] * S + [h_in, g, dh_out], ride=ride)


def matmuls_nt_normbwd(das, ws, h_in, gs, dh_out, then, *, name, ride=None, tm=ROW_TILE):
    T, D = h_in.shape
    tm = min(tm, T)
    n = len(das)

    def body(*refs):
        da_refs, w_refs, g_refs = refs[:n], refs[n:2 * n], refs[2 * n:3 * n]
        h_ref, dh_ref, z_ref, g2_ref, w2_ref, o_ref = refs[3 * n:3 * n + 6]
        dg_refs, (dz_ref, dg2_ref, da_ref) = refs[3 * n + 6:4 * n + 6], refs[4 * n + 6:]
        first = pl.program_id(0) == 0
        subs = _sub_tiles(tm)
        dns = [[_dot_nt(da_ref_[rows, :], w_ref[...]) for da_ref_, w_ref in zip(da_refs, w_refs)] for rows in subs]
        dgs, dg2 = [jnp.zeros((1, D), F32) for _ in range(n)], jnp.zeros((1, D), F32)
        for rows, dn_sub in zip(subs, dns):
            hf = h_ref[rows, :].astype(F32)
            r = _rms_r(hf)
            hh = hf * r
            total = dh_ref[rows, :].astype(F32)
            for b, (dn, g_ref) in enumerate(zip(dn_sub, g_refs)):
                gd = g_ref[...] * dn
                total = total + r * (gd - hh * jnp.mean(hh * gd, axis=-1, keepdims=True))
                dgs[b] = dgs[b] + jnp.sum(dn * hh, axis=0, keepdims=True)
            o_ref[rows, :] = total.astype(STREAM)
            dz, zh = _rmsnorm_bwd(z_ref[rows, :].astype(F32), g2_ref[...], total)
            dz = dz.astype(BF16)
            dz_ref[rows, :] = dz
            dg2 = dg2 + jnp.sum(total * zh, axis=0, keepdims=True)
            da_ref[rows, :] = _dot_nt(dz, w2_ref[...]).astype(BF16)
        for dg_ref, dg in zip(dg_refs + (dg2_ref,), dgs + [dg2]):
            _accumulate(dg_ref, first, dg)

    row = pl.BlockSpec((tm, D), lambda i: (i, 0))
    vec = pl.BlockSpec((1, D), lambda i: (0, 0))
    then_in, then_out, then_shape = _then_specs(then, tm, T, D)
    return _call(
        body, name=name, grid=(T // tm,),
        in_specs=[pl.BlockSpec((tm, da.shape[1]), lambda i: (i, 0)) for da in das]
        + [pl.BlockSpec(w.shape, lambda i: (0, 0)) for w in ws] + [vec] * n + [row, row] + then_in,
        out_specs=[row] + [vec] * n + then_out,
        out_shape=[SDS((T, D), STREAM)] + [SDS((1, D), F32)] * n + then_shape,
        semantics=("arbitrary",), args=list(das) + list(ws) + list(gs) + [h_in, dh_out] + list(then), ride=ride)


def matmul_tn(a, b, *, tb, name, ride=None, ta=MXU_WIDTH):
    T, Ka = a.shape
    S, _, Nb = b.shape
    per = Nb // tb

    def body(a_ref, b_ref, o_ref):
        o_ref[...] = _dot_tn(a_ref[...], b_ref[...]).astype(BF16)

    out = _call(
        body, name=name, grid=(S * per, Ka // ta),
        in_specs=[pl.BlockSpec((T, ta), lambda j, i: (0, i)),
                  pl.BlockSpec((None, T, tb), lambda j, i: (j // per, 0, j % per))],
        out_specs=[pl.BlockSpec((ta, tb), lambda j, i: (i, j))],
        out_shape=[SDS((Ka, S * Nb), BF16)],
        semantics=("parallel", "parallel"), args=(a, b), ride=ride)
    return out[0] if ride is None else (out[0][0], out[1])


def conv_bwd(dy, bcx, conv_w, *, name, ride=None, tm=ROW_TILE):
    T, D = dy.shape
    nt = T // tm
    hb = tm // BF16_ROWS
    last = T // BF16_ROWS - 1

    def body(dy_ref, dyn_ref, b_ref, bn_ref, c_ref, u_ref, cp_ref, up_ref, cw_ref, o_ref, dw_ref):
        i = pl.program_id(0)
        c, u = c_ref[...].astype(F32), u_ref[...].astype(F32)
        cu = c * u
        cup = jnp.where(i == 0, 0.0, cp_ref[...].astype(F32) * up_ref[...].astype(F32))
        cu1, cu2 = _shift_down(cup, cu, 1), _shift_down(cup, cu, 2)
        w0, w1, w2 = cw_ref[0:1, :], cw_ref[1:2, :], cw_ref[2:3, :]
        dyf = dy_ref[...].astype(F32)
        o_ref[:, 0:D] = (dyf * (w0 * cu2 + w1 * cu1 + w2 * cu)).astype(BF16)
        dcv = dyf * b_ref[...].astype(F32)
        dcvn = jnp.where(i == nt - 1, 0.0, dyn_ref[...].astype(F32) * bn_ref[...].astype(F32))
        dcu = w2 * dcv + w1 * _shift_up(dcv, dcvn, 1) + w0 * _shift_up(dcv, dcvn, 2)
        o_ref[:, D:2 * D] = (dcu * u).astype(BF16)
        o_ref[:, 2 * D:3 * D] = (dcu * c).astype(BF16)
        row = lax.broadcasted_iota(jnp.int32, (8, D), 0)
        dw = jnp.zeros((8, D), F32)
        for tap, t in enumerate((cu2, cu1, cu)):
            dw = jnp.where(row == tap, jnp.sum(dcv * t, axis=0, keepdims=True), dw)
        _accumulate(dw_ref, i == 0, dw)

    tile = lambda col: pl.BlockSpec((tm, D), lambda i: (i, col))
    prev = lambda col: pl.BlockSpec((BF16_ROWS, D), lambda i: (jnp.maximum(i * hb - 1, 0), col))
    nxt = lambda col: pl.BlockSpec((BF16_ROWS, D), lambda i: (jnp.minimum((i + 1) * hb, last), col))
    return _call(
        body, name=name, grid=(nt,),
        in_specs=[tile(0), nxt(0), tile(0), nxt(0), tile(1), tile(2), prev(1), prev(2),
                  pl.BlockSpec((3, D), lambda i: (0, 0))],
        out_specs=[pl.BlockSpec((tm, 3 * D), lambda i: (i, 0)), pl.BlockSpec((8, D), lambda i: (0, 0))],
        out_shape=[SDS((T, 3 * D), BF16), SDS((8, D), F32)],
        semantics=("arbitrary",), args=(dy, dy, bcx, bcx, bcx, bcx, bcx, bcx, conv_w), ride=ride)


class NoTraffic:
    def ride(self, kernel_name):
        return None

    def landed(self, kernel_name, results, wts):
        pass

    def grad(self, key, value):
        pass


def local_step(x, target, wts, vec, traffic):
    T, D = x.shape
    tabs = rope_tables(T)
    small = {}

    def run(builder, *args, name, **kw):
        ride = traffic.ride(name)
        if ride is None:
            return builder(*args, name=name, **kw)
        out, extra = builder(*args, name=name, ride=ride, **kw)
        traffic.landed(name, extra, wts)
        return out

    bcx, xn1 = run(norm_matmul, x, vec["a_pre"], wts["w_in"], tn=3 * D, split=1, name="a_in")
    bcx = bcx[0]
    h1, z0, y0 = run(conv_mix_out, bcx, vec["conv_w"], wts["w_out"], vec["a_post"], x, name="a_out")
    gu0, act0, xt2 = run(norm_swiglu_in, h1, vec["ffn_pre0"], wts["gu0"], name="ffn0_in")
    h2, z1 = run(plain_mix_out, act0, wts["wd0"], vec["ffn_post0"], h1, name="ffn0_out")
    kvp, xkv, qp, xq = run(norm2_matmul, h2, [vec["kv_norm"], vec["b_pre"]], [wts["w_kv"], wts["w_q"]],
                           name="kvq_in")
    attn, attn_stats = run(attention_fwd, qp, kvp, tabs, vec["sinks"], name="attn_fwd")
    h3, z2 = plain_mix_out(attn, wts["w_o"], vec["b_post"], h2, name="attn_out", tm=BIG_ROW_TILE)
    gu1, act1, xt3 = run(norm_swiglu_in, h3, vec["ffn_pre1"], wts["gu1"], name="ffn1_in")
    dy, dz3, small["ffn_post1"], dact1, loss = plain_mix_out(act1, wts["wd1"], vec["ffn_post1"], h3, name="ffn1_out",
                                                             target=target)

    def ffn_bwd(layer, dz, dact, gu, act, xt, h_in, dh, then, gu_first):
        tag = "ffn%d" % layer
        dwd = lambda: traffic.grad("wd%d" % layer, run(matmul_tn, act, dz[None], tb=D, name=tag + "_dwd"))
        dwgu = lambda: traffic.grad("gu%d" % layer, run(swiglu_bwd_tn, xt, dact, gu, name=tag + "_dwgu"))
        for step in ((dwgu, dwd) if gu_first else (dwd, dwgu)):
            step()
        dh_in, small["ffn_pre%d" % layer], dz_, dg_, da_ = run(
            swiglu_bwd_in, dact, gu, wts["gu%d" % layer], h_in, vec["ffn_pre%d" % layer], dh, then,
            name=tag + "_in_bwd")
        return dh_in, dz_, dg_, da_

    dh3, dz2, small["b_post"], dattn = ffn_bwd(1, dz3, dact1, gu1, act1, xt3, h3, dy,
                                               (z2, vec["b_post"], wts["w_o"]), gu_first=False)
    traffic.grad("w_o", matmul_tn(attn, dz2[None], tb=D, name="attn_dwo"))
    dq, dkv, small["sinks"] = run(attention_bwd, qp, kvp, tabs, vec["sinks"], dattn, attn, attn_stats,
                                  name="attn_bwd")
    traffic.grad("w_q", matmul_tn(xq, dq[None], tb=D, name="attn_dwq"))
    traffic.grad("w_kv", matmul_tn(xkv, dkv[None], tb=dkv.shape[1], name="attn_dwkv"))
    dh2, small["b_pre"], small["kv_norm"], dz1, small["ffn_post0"], dact0 = run(
        matmuls_nt_normbwd, [dq, dkv], [wts["w_q"], wts["w_kv"]], h2, [vec["b_pre"], vec["kv_norm"]], dh3,
        (z1, vec["ffn_post0"], wts["wd0"]), name="qkv_in_bwd")
    dh1, dz0, small["a_post"], dyc = ffn_bwd(0, dz1, dact0, gu0, act0, xt2, h1, dh2,
                                             (z0, vec["a_post"], wts["w_out"]), gu_first=True)
    traffic.grad("w_out", run(matmul_tn, y0, dz0[None], tb=D, name="a_dwout"))
    dbcx, small["conv_w"] = run(conv_bwd, dyc, bcx, vec["conv_w"], name="a_conv_bwd")
    traffic.grad("w_in", run(matmul_tn, xn1, dbcx[None], tb=3 * D // 2, name="a_dwin"))
    dx, small["a_pre"] = run(matmul_nt_normbwd, dbcx[None], wts["w_in"], x, vec["a_pre"], dh1, name="a_in_bwd")
    return loss, dx, small


SMALL_ROWS = 16
LOSS_ROW = 13

WHOLE = None
GATHER_PLAN = {"cast_rest": [("w_in", WHOLE)],
               "a_in": [("w_out", WHOLE), ("gu0", (0, 18))],
               "a_out": [("gu0", (18, 14))],
               "ffn0_in": [("wd0", WHOLE), ("w_kv", WHOLE), ("w_q", WHOLE), ("gu1", (0, 4))],
               "ffn0_out": [("w_o", WHOLE), ("gu1", (4, 8))],
               "attn_fwd": [("gu1", (12, 20))],
               "ffn1_in": [("wd1", WHOLE)]}
PAIR_PLAN = {"ffn1_dwgu": ["wd1"], "ffn1_in_bwd": ["gu1"], "attn_bwd": ["w_o"], "qkv_in_bwd": ["w_q", "w_kv"],
             "ffn0_dwd": ["gu0"], "ffn0_in_bwd": ["wd0"], "a_conv_bwd": ["w_out"], "chip_reduce_early": ["w_in"]}
CHIP_PLAN = {"ffn1_in_bwd": [("wd1", WHOLE)], "attn_bwd": [("gu1", WHOLE)],
             "ffn0_dwgu": [("w_o", WHOLE), ("w_q", WHOLE), ("w_kv", WHOLE)],
             "ffn0_in_bwd": [("gu0", WHOLE)], "a_conv_bwd": [("wd0", (0, 12))],
             "a_dwin": [("wd0", (12, 10)), ("w_out", WHOLE)], "a_in_bwd": [("w_in", WHOLE)]}
HALF_PLAN = {"a_in_bwd": ["gu0", "gu1", "wd0", "wd1", "w_kv", "w_q", "w_o", "w_out"]}
GRAD_KIND = dict(KIND, gu0="split", gu1="split")


class Traffic:
    def __init__(self, wholes, quarter, c_arr, pc_arr):
        self.wholes, self.quarter, self.c_arr, self.pc_arr = wholes, quarter, c_arr, pc_arr
        self.views, self.theirs = {}, {}
        self.sums, self.got = {}, {}
        self.reduced = {}
        self.stages = {}

    def reduce(self, keys, name):
        args = ([self.views[k] for k in keys], [self.theirs[k] for k in keys], [self.got[k] for k in keys],
                [GRAD_KIND[k] for k in keys], self.pc_arr)
        if name not in PAIR_PLAN:
            return chip_reduce(*args, name=name)
        pairs = PAIR_PLAN[name]
        out, got = chip_reduce(*args, name=name, ride=pair_ride([self.views[k] for k in pairs]))
        self.pair_sums(pairs, got)
        return out

    def pair_sums(self, keys, got):
        for k, theirs in zip(keys, got):
            self.theirs[k] = theirs
            self.sums[k] = pair_add(self.views[k], theirs, GRAD_KIND[k], self.pc_arr, name="pair_add_" + k)

    def ride(self, name, small=None):
        rides, stages = [], []
        if name in GATHER_PLAN:
            plan = GATHER_PLAN[name]
            rides.append(gather_ride([self.wholes[k] for k, _ in plan],
                                     [(KIND[k], self.quarter[k], part) for k, part in plan], small))
            stages.append(("gather", [k for k, _ in plan]))
        if name in HALF_PLAN:
            keys = HALF_PLAN[name]
            rides.append(half_ride(self.reduce(keys, "chip_reduce_early")))
            stages.append(("half", keys))
        if name in CHIP_PLAN:
            plan = CHIP_PLAN[name]
            rides.append(chip_ride([self.sums[k] for k, _ in plan],
                                   [(GRAD_KIND[k], self.quarter[k], part) for k, part in plan],
                                   earlier=[self.got.get(k) for k, _ in plan]))
            stages.append(("chip", [k for k, _ in plan]))
        if name in PAIR_PLAN:
            keys = PAIR_PLAN[name]
            rides.append(pair_ride([self.views[k] for k in keys]))
            stages.append(("pair", keys))
        self.stages[name] = stages
        return join(rides)

    def landed(self, name, results, wts):
        results = list(results)
        for stage, keys in self.stages[name]:
            mine, results = results[:len(keys)], results[len(keys):]
            if stage == "gather":
                for k, whole in zip(keys, mine):
                    self.wholes[k] = wts[k] = whole
            elif stage == "chip":
                self.got.update(zip(keys, mine))
            elif stage == "half":
                self.reduced.update(zip(keys, mine))
            else:
                self.pair_sums(keys, mine)

    def grad(self, key, value):
        r, ws = self.quarter[key]
        view = {"row": (N_CHIPS, 2, r // 2, ws), "col": (1, 2, r // 2, N_CHIPS * ws), "split": (2, 2, r // 2, 2 * ws)}
        self.views[key] = value.reshape(view[GRAD_KIND[key]])


def kernel(x, a_pre_norm, a_w_in, a_conv_w, a_w_out, a_post_norm, ffn_pre_norm, ffn_w_gate_up, ffn_w_down, ffn_post_norm, kv_norm, w_kv, b_pre_norm, b_w_q, b_sinks, b_w_o, b_post_norm, loss_target, m_a_pre_norm, m_a_w_in, m_a_conv_w, m_a_w_out, m_a_post_norm, m_ffn_pre_norm, m_ffn_w_gate_up, m_ffn_w_down, m_ffn_post_norm, m_kv_norm, m_w_kv, m_b_pre_norm, m_b_w_q, m_b_sinks, m_b_w_o, m_b_post_norm, v_a_pre_norm, v_a_w_in, v_a_conv_w, v_a_w_out, v_a_post_norm, v_ffn_pre_norm, v_ffn_w_gate_up, v_ffn_w_down, v_ffn_post_norm, v_kv_norm, v_w_kv, v_b_pre_norm, v_b_w_q, v_b_sinks, v_b_w_o, v_b_post_norm):
    T, D = x.shape[1], x.shape[2]
    xi, yi, ci = _place()
    p = 2 * xi + yi
    p_arr = jnp.reshape(p, (1,)).astype(jnp.int32)
    c_arr = jnp.reshape(ci, (1,)).astype(jnp.int32)
    pc_arr = jnp.stack([p, ci]).astype(jnp.int32)
    me_arr = jnp.reshape(4 * xi + 2 * yi + ci, (1,)).astype(jnp.int32)
    qd = D // N_CHIPS

    big = {"w_in": (a_w_in, 0), "w_out": (a_w_out, 0), "gu0": (ffn_w_gate_up, 0), "gu1": (ffn_w_gate_up, 1),
           "wd0": (ffn_w_down, 0), "wd1": (ffn_w_down, 1), "w_kv": (w_kv[None], 0), "w_q": (b_w_q, 0),
           "w_o": (b_w_o, 0)}
    names = list(big)
    quarter = {k: w.shape[1:] for k, (w, _) in big.items()}
    source = lambda k: big[k] + (KIND[k],)
    traffic = Traffic(dict(zip(names[:1], cast_quarters([source(names[0])], p_arr, name="cast_first"))), quarter,
                      c_arr, pc_arr)
    small_shard = jnp.concatenate([a_pre_norm, a_post_norm, a_conv_w[0], jnp.zeros((3, qd), F32)], axis=0)
    wts = {}
    rest, (*landed, small_full) = cast_quarters([source(k) for k in names[1:]], p_arr, name="cast_rest",
                                                ride=traffic.ride("cast_rest", small_shard))
    traffic.wholes.update(zip(names[1:], rest))
    traffic.landed("cast_rest", landed, wts)
    rows = lambda k: jnp.transpose(small_full[:, k], (1, 0, 2)).reshape(-1, D)
    vec = {"a_pre": rows(slice(0, 1)), "a_post": rows(slice(1, 2)), "conv_w": rows(slice(2, 5)),
           "ffn_pre0": ffn_pre_norm[0:1], "ffn_pre1": ffn_pre_norm[1:2],
           "ffn_post0": ffn_post_norm[0:1], "ffn_post1": ffn_post_norm[1:2],
           "kv_norm": kv_norm[None], "b_pre": b_pre_norm, "b_post": b_post_norm, "sinks": b_sinks}

    loss, dx, small = local_step(x[0], loss_target[0], wts, vec, traffic)

    pad = lambda a: jnp.pad(a, ((0, 0), (0, D - a.shape[1])))
    small_block = jnp.concatenate(
        [small["a_pre"], small["a_post"], small["conv_w"][0:3], small["ffn_pre0"], small["ffn_pre1"],
         small["ffn_post0"], small["ffn_post1"], small["kv_norm"], small["b_pre"], small["b_post"],
         pad(small["sinks"][0:1]), pad(loss[0:1]), jnp.zeros((SMALL_ROWS - LOSS_ROW - 1, D), F32)], axis=0)
    late = [k for k in names if k not in traffic.reduced]
    *swapped, small_blocks = alone(join([half_ride(traffic.reduce(late, "chip_reduce_late")),
                                         chip_ride([], [], small_block)]), name="last_exchange")
    traffic.reduced.update(zip(late, swapped))
    grad = {k: traffic.reduced[k].reshape(quarter[k]) for k in names}
    small_sum = small_reduce(small_blocks, me_arr)

    out = {}
    out["a_w_in"] = adamw(a_w_in, [grad["w_in"]], m_a_w_in, v_a_w_in, name="adamw_a_w_in")
    out["a_w_out"] = adamw(a_w_out, [grad["w_out"]], m_a_w_out, v_a_w_out, name="adamw_a_w_out")
    out["ffn_w_gate_up"] = adamw(ffn_w_gate_up, [grad["gu0"], grad["gu1"]], m_ffn_w_gate_up, v_ffn_w_gate_up,
                                 name="adamw_ffn_w_gate_up")
    out["ffn_w_down"] = adamw(ffn_w_down, [grad["wd0"], grad["wd1"]], m_ffn_w_down, v_ffn_w_down,
                              name="adamw_ffn_w_down")
    out["w_kv"] = [o[0] for o in adamw(w_kv[None], [grad["w_kv"]], m_w_kv[None], v_w_kv[None], name="adamw_w_kv")]
    out["b_w_q"] = adamw(b_w_q, [grad["w_q"]], m_b_w_q, v_b_w_q, name="adamw_b_w_q")
    out["b_w_o"] = adamw(b_w_o, [grad["w_o"]], m_b_w_o, v_b_w_o, name="adamw_b_w_o")

    leaves = {"a_pre_norm": (a_pre_norm, m_a_pre_norm, v_a_pre_norm, 0, True),
              "a_post_norm": (a_post_norm, m_a_post_norm, v_a_post_norm, 1, True),
              "a_conv_w": (a_conv_w, m_a_conv_w, v_a_conv_w, 2, True),
              "ffn_pre_norm": (ffn_pre_norm, m_ffn_pre_norm, v_ffn_pre_norm, 5, False),
              "ffn_post_norm": (ffn_post_norm, m_ffn_post_norm, v_ffn_post_norm, 7, False),
              "kv_norm": (kv_norm[None], m_kv_norm[None], v_kv_norm[None], 9, False),
              "b_pre_norm": (b_pre_norm, m_b_pre_norm, v_b_pre_norm, 10, False),
              "b_post_norm": (b_post_norm, m_b_post_norm, v_b_post_norm, 11, False),
              "b_sinks": (b_sinks, m_b_sinks, v_b_sinks, 12, False)}
    for k, results in zip(leaves, adamw_rows(small_sum, p_arr, list(leaves.values()), name="adamw_small")):
        out[k] = [r[0] for r in results] if k == "kv_norm" else results

    order = ["a_pre_norm", "a_w_in", "a_conv_w", "a_w_out", "a_post_norm", "ffn_pre_norm", "ffn_w_gate_up",
             "ffn_w_down", "ffn_post_norm", "kv_norm", "w_kv", "b_pre_norm", "b_w_q", "b_sinks", "b_w_o",
             "b_post_norm"]
    return (small_sum[LOSS_ROW, 0], dx[None], *[out[k][0] for k in order], *[out[k][1] for k in order],
            *[out[k][2] for k in order], *[out[k][3] for k in order])
```

```python
import math

import jax
import jax.numpy as jnp
from jax import lax
from jax.experimental import pallas as pl
from jax.experimental.pallas import tpu as pltpu

F32 = jnp.float32
BF16 = jnp.bfloat16
SDS = jax.ShapeDtypeStruct
MESH = pl.DeviceIdType.MESH
DMA = pltpu.SemaphoreType.DMA
HBM_SPEC = pl.BlockSpec(memory_space=pltpu.HBM)

EPS = 1e-6
NEG = -1e30
HEAD_DIM = 64
N_KV_HEADS = 4
BLOCK = 128
ROT_DIM = HEAD_DIM // 4
ROPE_THETA = 500000.0
N_CHIPS = 4

ADAM_LR = 0.001
ADAM_B1 = 0.9
ADAM_B2 = 0.999
ADAM_EPS = 1e-08
ADAM_WD = 0.01
ADAM_STEP = 10

VMEM_LIMIT_BYTES = 52 * 1024 * 1024
ROW_TILE = 512
BF16_ROWS = 16
STREAM = BF16
MXU_WIDTH = 256

KIND = {"w_in": "col", "gu0": "col", "gu1": "col", "w_out": "row", "wd0": "row", "wd1": "row", "w_kv": "row",
        "w_q": "row", "w_o": "row"}


def _params(*semantics):
    return pltpu.CompilerParams(dimension_semantics=semantics, vmem_limit_bytes=VMEM_LIMIT_BYTES)


def _row_tile(rows, limit, step=8):
    return max(t for t in range(step, limit + 1, step) if rows % t == 0)


def _place():
    return lax.axis_index("x"), lax.axis_index("y"), lax.axis_index("c")


def _other_chips(x, y):
    return [(1 - x, y), (x, 1 - y), (1 - x, 1 - y)]


def _remote(src, dst, send_sem, recv_sem, to):
    return pltpu.make_async_remote_copy(src_ref=src, dst_ref=dst, send_sem=send_sem, recv_sem=recv_sem,
                                        device_id=to, device_id_type=MESH)


def _full_shape(kind, quarter):
    r, ws = quarter
    return (N_CHIPS * r, ws) if kind == "row" else (r, N_CHIPS * ws)


def _rows_of(h, part):
    lo, n = (0, h) if part is None else (part[0] * BF16_ROWS, part[1] * BF16_ROWS)
    assert lo + n <= h, (h, part)
    return lo, n


def _half_of_quarter(ref, kind, quarter, part, q, half):
    r, ws = quarter
    h = r // 2
    lo, n = _rows_of(h, part)
    if kind == "row":
        return ref.at[pl.ds(pl.multiple_of(q * r + half * h + lo, BF16_ROWS), n)]
    return ref.at[pl.ds(pl.multiple_of(half * h + lo, BF16_ROWS), n), pl.ds(pl.multiple_of(q * ws, 128), ws)]


class Ride:
    def __init__(self, operands, out_shape, aliases, sems, make):
        self.operands, self.out_shape, self.aliases, self.sems, self.make = operands, out_shape, aliases, sems, make

    def stages(self, ins, outs, sems):
        made = self.make(ins, outs, sems)
        return made if len(made) == 4 else (made[0], None, None, made[1])


def join(rides):
    rides = [r for r in rides if r is not None]
    if len(rides) < 2:
        return rides[0] if rides else None
    aliases, at = {}, [0, 0, 0]
    cuts = []
    for r in rides:
        aliases.update({at[0] + i: at[1] + o for i, o in r.aliases.items()})
        cuts.append(tuple(at))
        at = [at[0] + len(r.operands), at[1] + len(r.out_shape), at[2] + len(r.sems)]
    cuts.append(tuple(at))

    def make(ins, outs, sem):
        made = [r.stages(ins[lo[0]:hi[0]], outs[lo[1]:hi[1]], sem[lo[2]:hi[2]]) for r, lo, hi in zip(rides, cuts, cuts[1:])]
        def all_of(k):
            def stage():
                for m in made:
                    if m[k] is not None:
                        m[k]()
            return stage

        if all(m[1] is None for m in made):
            return all_of(0), all_of(3)
        return all_of(0), all_of(1), all_of(2), all_of(3)

    return Ride(sum((list(r.operands) for r in rides), []), sum((list(r.out_shape) for r in rides), []), aliases,
                sum((list(r.sems) for r in rides), []), make)


def _call(body, *, name, grid, in_specs, out_specs, out_shape, args, scratch_shapes=(), semantics=None, ride=None,
          prefetch=None):
    pre = 0 if prefetch is None else 1
    n_in, n_out, n_scr = len(in_specs), len(out_specs), len(scratch_shapes)
    r_in, r_out = (len(ride.operands), len(ride.out_shape)) if ride is not None else (0, 0)
    a, b = pre + n_in, pre + n_in + r_in
    c, d = b + n_out, b + n_out + r_out
    e = d + n_scr

    def riding(*refs):
        start, relay, relay_again, finish = ride.stages(refs[a:b], refs[c:d], refs[e:])
        step, steps = pl.program_id(0), 1
        for k, extent in enumerate(grid):
            step = pl.program_id(k) if k == 0 else step * extent + pl.program_id(k)
            steps *= extent
        pl.when(step == 0)(start)
        if relay is not None:
            pl.when(step == steps // 2)(relay)
            pl.when(step == steps - 1)(relay_again)
        body(*refs[:a], *refs[b:c], *refs[d:e])
        pl.when(step == steps - 1)(finish)

    if ride is None:
        kernel_body, extra_in, extra_out, extra_shape, extra_scr, aliases = body, [], [], [], [], {}
        params = _params(*semantics)
    else:
        kernel_body, extra_in, extra_out = riding, [HBM_SPEC] * r_in, [HBM_SPEC] * r_out
        extra_shape, extra_scr = list(ride.out_shape), list(ride.sems)
        aliases = {pre + n_in + i: n_out + o for i, o in ride.aliases.items()}
        params = _params(*(("arbitrary",) * len(grid)))
    specs = dict(grid=grid, in_specs=list(in_specs) + extra_in, out_specs=list(out_specs) + extra_out,
                 scratch_shapes=list(scratch_shapes) + extra_scr)
    if prefetch is not None:
        specs = dict(grid_spec=pltpu.PrefetchScalarGridSpec(num_scalar_prefetch=1, **specs))
        args = (prefetch,) + tuple(args)
    outs = pl.pallas_call(kernel_body, name=name, out_shape=list(out_shape) + extra_shape,
                          input_output_aliases=aliases, compiler_params=params, **specs,
                          )(*args, *(ride.operands if ride is not None else ()))
    return outs if ride is None else (outs[:n_out], outs[n_out:])


def alone(ride, *, name):
    def body(*refs):
        n = len(ride.operands)
        stages = ride.stages(refs[:n], refs[n:n + len(ride.out_shape)], refs[n + len(ride.out_shape):])
        for stage in stages:
            if stage is not None:
                stage()

    return pl.pallas_call(
        body, name=name, in_specs=[HBM_SPEC] * len(ride.operands), out_specs=[HBM_SPEC] * len(ride.out_shape),
        out_shape=list(ride.out_shape), input_output_aliases=dict(ride.aliases), scratch_shapes=list(ride.sems),
    )(*ride.operands)


def _two_pieces(h, part):
    lo, n = (0, h // BF16_ROWS) if part is None else part
    assert n >= 2, (h, part)
    return (lo, n // 2), (lo + n // 2, n - n // 2)


def gather_ride(wholes, metas, small=None):
    n = len(wholes)
    operands, out_shape = list(wholes), [SDS(s.shape, s.dtype) for s in wholes]
    sems = [DMA((n, 4)), DMA((n, 4)), DMA((n, 4)), DMA((n, 4))]
    if small is not None:
        operands.append(small)
        out_shape.append(SDS((N_CHIPS,) + small.shape, small.dtype))
        sems += [DMA((3,)), DMA((3,)), DMA(())]

    def make(ins, outs, sem):
        send1, recv1, send2, recv2 = sem[:4]
        x, y, c = _place()
        p = 2 * x + y
        chips = _other_chips(x, y)
        across_x, across_y, across_both = [2 * qx + qy for qx, qy in chips]
        me, sibling = (x, y, c), (x, y, 1 - c)

        def region(t, q, half, piece=None):
            kind, quarter, part = metas[t]
            if piece is not None:
                part = _two_pieces(quarter[0] // 2, part)[piece]
            return _half_of_quarter(outs[t], kind, quarter, part, q, half)

        first, second, arriving = [], [], []
        landing, passing = [[], [], [], []], [[], [], [], []]
        for j, (qx, qy) in enumerate(chips):
            if small is not None:
                q = 2 * qx + qy
                first.append(_remote(ins[n], outs[n].at[p], sem[4].at[j], sem[5].at[j], (qx, qy, c)))
                arriving.append(_remote(outs[n].at[q], outs[n].at[q], sem[4].at[j], sem[5].at[j], me))
        for t in range(n):
            mine = region(t, p, c)
            for j in range(2):
                first.append(_remote(mine, mine, send1.at[t, j], recv1.at[t, j], chips[j] + (c,)))
            lands = [(across_x, None), (across_y, None), (across_both, 0), (across_both, 1)]
            for k, (q, piece) in enumerate(lands):
                landed, theirs = region(t, q, c, piece), region(t, q, 1 - c, piece)
                landing[k].append(_remote(landed, landed, send1.at[t, k], recv1.at[t, k], me))
                passing[k].append(_remote(landed, landed, send2.at[t, k], recv2.at[t, k], sibling))
                arriving.append(_remote(theirs, theirs, send2.at[t, k], recv2.at[t, k], me))
            onward = region(t, across_x, c, 0)
            second.append(_remote(onward, onward, send1.at[t, 2], recv1.at[t, 2], chips[1] + (c,)))
            onward = region(t, across_y, c, 1)
            second.append(_remote(onward, onward, send1.at[t, 3], recv1.at[t, 3], chips[0] + (c,)))
        local = [] if small is None else [pltpu.make_async_copy(ins[n], outs[n].at[p], sem[6])]

        def start():
            for cp in local + first:
                cp.start()

        def relay():
            for k in range(2):
                for t in range(n):
                    landing[k][t].wait_recv()
                    second[2 * t + k].start()
                    passing[k][t].start()

        def relay_again():
            for k in range(2, 4):
                for t in range(n):
                    landing[k][t].wait_recv()
                    passing[k][t].start()

        def finish():
            for cp in arriving:
                cp.wait_recv()
            for cp in first + second + sum(passing, []):
                cp.wait_send()
            for cp in local:
                cp.wait()

        return start, relay, relay_again, finish

    return Ride(operands, out_shape, {t: t for t in range(n)}, sems, make)


def chip_ride(sums, metas, small=None, earlier=None):
    n = len(sums)
    operands = list(sums)
    out_shape = [SDS((3, s.shape[1], quarter[1]), s.dtype) for s, (_, quarter, _) in zip(sums, metas)]
    sems = [DMA((n, 3)), DMA((n, 3))] if n else []
    if small is not None:
        operands.append(small)
        out_shape.append(SDS((8,) + small.shape, small.dtype))
        sems += [DMA((7,)), DMA((7,)), DMA(())]
    aliases = {}
    for t, buffer in enumerate(earlier or [None] * n):
        if buffer is not None:
            aliases[len(operands)] = t
            operands.append(buffer)

    def make(ins, outs, sem):
        x, y, c = _place()
        cps = []
        for j, (qx, qy) in enumerate(_other_chips(x, y)):
            q = 2 * qx + qy
            for t in range(n):
                kind, (_, ws), part = metas[t]
                rows = pl.ds(*_rows_of(ins[t].shape[1], part))
                if kind == "row":
                    src = ins[t].at[q, rows]
                elif kind == "col":
                    src = ins[t].at[0, rows, pl.ds(pl.multiple_of(q * ws, 128), ws)]
                else:
                    src = ins[t].at[q // 2, rows, pl.ds(pl.multiple_of((q % 2) * ws, 128), ws)]
                cps.append(_remote(src, outs[t].at[j, rows], sem[0].at[t, j], sem[1].at[t, j], (qx, qy, c)))
        local = []
        if small is not None:
            ssend, srecv, lsem = sem[2 * bool(n):2 * bool(n) + 3]
            local.append(pltpu.make_async_copy(ins[n], outs[n].at[0], lsem))
            for k in range(1, 8):
                peer = (x ^ (k >> 2 & 1), y ^ (k >> 1 & 1), c ^ (k & 1))
                cps.append(_remote(ins[n], outs[n].at[k], ssend.at[k - 1], srecv.at[k - 1], peer))

        def start():
            for cp in local + cps:
                cp.start()

        def finish():
            for cp in cps + local:
                cp.wait()

        return start, finish

    return Ride(operands, out_shape, aliases, sems, make)


def pair_ride(grads):
    n = len(grads)

    def make(ins, outs, sem):
        x, y, c = _place()
        cps = [_remote(ins[t].at[:, 1 - c], outs[t], sem[0].at[t], sem[1].at[t], (x, y, 1 - c)) for t in range(n)]

        def start():
            for cp in cps:
                cp.start()

        def finish():
            for cp in cps:
                cp.wait()

        return start, finish

    return Ride(list(grads), [SDS((g.shape[0],) + g.shape[2:], g.dtype) for g in grads], {}, [DMA((n,)), DMA((n,))],
                make)


def half_ride(quarters):
    n = len(quarters)

    def make(ins, outs, sem):
        x, y, c = _place()
        sends = [_remote(outs[t].at[c], outs[t].at[c], sem[0].at[t], sem[1].at[t], (x, y, 1 - c)) for t in range(n)]

        def start():
            for cp in sends:
                cp.start()

        def finish():
            for t in range(n):
                theirs = outs[t].at[1 - c]
                _remote(theirs, theirs, sem[0].at[t], sem[1].at[t], (x, y, c)).wait_recv()
            for cp in sends:
                cp.wait_send()

        return start, finish

    return Ride(list(quarters), [SDS(q.shape, q.dtype) for q in quarters], {t: t for t in range(n)},
                [DMA((n,)), DMA((n,))], make)


CAST_STEPS = 4


def cast_quarters(sources, p_arr, *, name, ride=None):
    n = len(sources)
    in_specs, out_specs, out_shape = [], [], []
    for w, layer, kind in sources:
        _, r, ws = w.shape
        tr = r // CAST_STEPS
        assert tr % BF16_ROWS == 0, w.shape
        in_specs.append(pl.BlockSpec((None, tr, ws), lambda i, p_ref, layer=layer: (layer, i, 0)))
        out_specs.append(pl.BlockSpec((tr, ws), (lambda i, p_ref: (p_ref[0] * CAST_STEPS + i, 0)) if kind == "row"
                                      else (lambda i, p_ref: (i, p_ref[0]))))
        out_shape.append(SDS(_full_shape(kind, (r, ws)), BF16))

    def body(p_ref, *refs):
        for w_ref, o_ref in zip(refs[:n], refs[n:]):
            o_ref[...] = w_ref[...].astype(BF16)

    return _call(body, name=name, grid=(CAST_STEPS,), in_specs=in_specs, out_specs=out_specs, out_shape=out_shape,
                 semantics=("parallel",), args=[w for w, _, _ in sources], ride=ride, prefetch=p_arr)


def pair_add(owns, gots, c_arr, *, name):
    n = len(owns)
    A, _, h, _ = owns[0].shape
    assert all(own.shape[:3] == (A, 2, h) for own in owns), [own.shape for own in owns]
    th = _row_tile(h, max(BF16_ROWS, (3 << 19) // sum(own.shape[3] for own in owns)), BF16_ROWS)

    def body(c_ref, *refs):
        for a_ref, b_ref, o_ref in zip(refs[:n], refs[n:2 * n], refs[2 * n:]):
            o_ref[...] = (a_ref[...].astype(F32) + b_ref[...].astype(F32)).astype(BF16)

    mine = lambda W: pl.BlockSpec((None, None, th, W), lambda q, i, c_ref: (q, c_ref[0], i, 0))
    theirs = lambda W: pl.BlockSpec((None, th, W), lambda q, i, c_ref: (q, i, 0))
    widths = [own.shape[3] for own in owns]
    return pl.pallas_call(
        body, name=name,
        grid_spec=pltpu.PrefetchScalarGridSpec(
            num_scalar_prefetch=1, grid=(A, h // th),
            in_specs=[mine(W) for W in widths] + [theirs(W) for W in widths],
            out_specs=[theirs(W) for W in widths]),
        out_shape=[SDS((A, h, W), BF16) for W in widths],
        compiler_params=_params("parallel", "parallel"),
    )(c_arr, *owns, *gots)


REDUCE_STEPS = 2


def chip_reduce(sums, got, kinds, pc_arr, *, name, ride=None):
    n = len(sums)
    mine = {"row": lambda i, pc_ref: (pc_ref[0], i, 0), "col": lambda i, pc_ref: (0, i, pc_ref[0]),
            "split": lambda i, pc_ref: (pc_ref[0] // 2, i, pc_ref[0] % 2)}
    a_specs, b_specs, o_specs, out_shape = [], [], [], []
    for g, kind in zip(got, kinds):
        _, h, ws = g.shape
        th = h // REDUCE_STEPS
        assert th % BF16_ROWS == 0, g.shape
        a_specs.append(pl.BlockSpec((None, th, ws), mine[kind]))
        b_specs.append(pl.BlockSpec((3, th, ws), lambda i, pc_ref: (0, i, 0)))
        o_specs.append(pl.BlockSpec((None, th, ws), lambda i, pc_ref: (pc_ref[1], i, 0)))
        out_shape.append(SDS((2, h, ws), F32))

    def body(pc_ref, *refs):
        for a_ref, b_ref, o_ref in zip(refs[:n], refs[n:2 * n], refs[2 * n:]):
            o_ref[...] = ((a_ref[...].astype(F32) + b_ref[0].astype(F32)) + b_ref[1].astype(F32)) + b_ref[2].astype(F32)

    return _call(body, name=name, grid=(REDUCE_STEPS,), in_specs=a_specs + b_specs, out_specs=o_specs,
                 out_shape=out_shape, semantics=("parallel",), args=list(sums) + list(got), prefetch=pc_arr, ride=ride)


def small_reduce(blocks, me_arr):
    _, rows, D = blocks.shape

    def body(me_ref, b_ref, o_ref):
        me = me_ref[0]
        total = b_ref[me]
        for d in range(1, 8):
            total = total + b_ref[d ^ me]
        o_ref[...] = total

    return pl.pallas_call(
        body, name="small_reduce",
        grid_spec=pltpu.PrefetchScalarGridSpec(
            num_scalar_prefetch=1, grid=(1,),
            in_specs=[pl.BlockSpec((8, rows, D), lambda i, me_ref: (0, 0, 0))],
            out_specs=pl.BlockSpec((rows, D), lambda i, me_ref: (0, 0))),
        out_shape=SDS((rows, D), F32),
        compiler_params=_params("arbitrary"),
    )(me_arr, blocks)


def _adam(w, g, m, v):
    m_new = ADAM_B1 * m + (1.0 - ADAM_B1) * g
    v_new = ADAM_B2 * v + (1.0 - ADAM_B2) * (g * g)
    m_hat = m_new / (1.0 - ADAM_B1 ** ADAM_STEP)
    v_hat = v_new / (1.0 - ADAM_B2 ** ADAM_STEP)
    return -ADAM_LR * (m_hat / (jnp.sqrt(v_hat) + ADAM_EPS) + ADAM_WD * w), m_new, v_new


def adamw_rows(block, p_arr, leaves, *, name):
    L = len(leaves)

    def body(p_ref, b_ref, *refs):
        outs = refs[3 * L:]
        for i, (w, _, _, row, sharded) in enumerate(leaves):
            n, width = w.shape[-2:]
            cols = pl.ds(pl.multiple_of(p_ref[0] * width, 128), width) if sharded else slice(0, width)
            g = b_ref[row:row + n, cols].reshape(w.shape)
            results = (g,) + _adam(refs[i][...], g, refs[L + i][...], refs[2 * L + i][...])
            for o_ref, value in zip(outs[4 * i:4 * i + 4], results):
                o_ref[...] = value

    whole = lambda a: pl.BlockSpec(a.shape, lambda i, p_ref, nd=len(a.shape): (0,) * nd)
    arrays = [leaf[k] for k in range(3) for leaf in leaves]
    shapes = [SDS(leaf[0].shape, F32) for leaf in leaves for _ in range(4)]
    outs = pl.pallas_call(
        body, name=name,
        grid_spec=pltpu.PrefetchScalarGridSpec(
            num_scalar_prefetch=1, grid=(1,), in_specs=[whole(block)] + [whole(a) for a in arrays],
            out_specs=[whole(s) for s in shapes]),
        out_shape=shapes, compiler_params=_params("arbitrary"),
    )(p_arr, block, *arrays)
    return [outs[4 * i:4 * i + 4] for i in range(L)]


def adamw(w, gs, m, v, *, name):
    L, r, cols = w.shape
    tr = _row_tile(r, 256)
    nt = r // tr

    def body(*refs):
        w_ref, m_ref, v_ref = refs[:3]
        g_refs = refs[3:3 + L]
        g_out, d_out, m_out, v_out = refs[3 + L:]
        layer = pl.program_id(0)
        g = g_refs[0][...]
        for l in range(1, L):
            g = jnp.where(layer == l, g_refs[l][...], g)
        g_out[...] = g
        d_out[...], m_out[...], v_out[...] = _adam(w_ref[...], g, m_ref[...], v_ref[...])

    full = pl.BlockSpec((None, tr, cols), lambda l, i: (l, i, 0))
    g_spec = lambda l0: pl.BlockSpec((tr, cols), lambda l, i: (jnp.where(l == l0, i, jnp.where(l < l0, 0, nt - 1)), 0))
    return pl.pallas_call(
        body, name=name, grid=(L, nt),
        in_specs=[full, full, full] + [g_spec(l0) for l0 in range(L)],
        out_specs=[full] * 4,
        out_shape=[SDS(w.shape, F32)] * 4,
        compiler_params=_params("arbitrary", "arbitrary"),
    )(w, m, v, *gs)


def _rms_r(xf):
    return lax.rsqrt(jnp.mean(xf * xf, axis=-1, keepdims=True) + EPS)


def _rmsnorm_bwd(xf, g, dy):
    r = _rms_r(xf)
    xh = xf * r
    gd = g * dy
    return r * (gd - xh * jnp.mean(xh * gd, axis=-1, keepdims=True)), xh


def _dot(a, b):
    return jnp.dot(a, b, preferred_element_type=F32)


def _dot_nt(a, b):
    return lax.dot_general(a, b, (((1,), (1,)), ((), ())), preferred_element_type=F32)


def _dot_tn(a, b):
    return lax.dot_general(a, b, (((0,), (0,)), ((), ())), preferred_element_type=F32)


def _accumulate(ref, first, value):
    @pl.when(first)
    def _():
        ref[...] = value

    @pl.when(jnp.logical_not(first))
    def _():
        ref[...] += value


def norm_matmul(x, g, w, *, tn, split, name, ride=None, tm=ROW_TILE):
    T, D = x.shape
    N = w.shape[1]
    per = N // split // tn

    def body(x_ref, g_ref, w_ref, o_ref, xn_ref):
        @pl.when(pl.program_id(1) == 0)
        def _():
            xf = x_ref[...].astype(F32)
            xn_ref[...] = (xf * _rms_r(xf) * g_ref[...]).astype(BF16)

        o_ref[...] = _dot(xn_ref[...], w_ref[...]).astype(BF16)

    return _call(
        body, name=name, grid=(T // tm, N // tn),
        in_specs=[pl.BlockSpec((tm, D), lambda i, j: (i, 0)),
                  pl.BlockSpec((1, D), lambda i, j: (0, 0)),
                  pl.BlockSpec((D, tn), lambda i, j: (0, j))],
        out_specs=[pl.BlockSpec((None, tm, tn), lambda i, j: (j // per, i, j % per)),
                   pl.BlockSpec((tm, D), lambda i, j: (i, 0))],
        out_shape=[SDS((split, T, N // split), BF16), SDS((T, D), BF16)],
        semantics=("parallel", "arbitrary"), args=(x, g, w), ride=ride)


BIG_ROW_TILE = 1024


def norm2_matmul(x, gains, weights, *, name, ride=None, tm=BIG_ROW_TILE):
    T, D = x.shape
    tm = min(tm, T)
    n = len(gains)

    def body(x_ref, *refs):
        subs = _sub_tiles(tm)
        xhs = []
        for rows in subs:
            xf = x_ref[rows, :].astype(F32)
            xhs.append(xf * _rms_r(xf))
        for g_ref, w_ref, o_ref, xn_ref in zip(refs[:n], refs[n:2 * n], refs[2 * n::2], refs[2 * n + 1::2]):
            for rows, xh in zip(subs, xhs):
                xn = (xh * g_ref[...]).astype(BF16)
                xn_ref[rows, :] = xn
                o_ref[rows, :] = _dot(xn, w_ref[...]).astype(BF16)

    row = pl.BlockSpec((tm, D), lambda i: (i, 0))
    vec = pl.BlockSpec((1, D), lambda i: (0, 0))
    out_specs, out_shape = [], []
    for w in weights:
        out_specs += [pl.BlockSpec((tm, w.shape[1]), lambda i: (i, 0)), row]
        out_shape += [SDS((T, w.shape[1]), BF16), SDS((T, D), BF16)]
    return _call(
        body, name=name, grid=(T // tm,),
        in_specs=[row] + [vec] * n + [pl.BlockSpec(w.shape, lambda i: (0, 0)) for w in weights],
        out_specs=out_specs, out_shape=out_shape, semantics=("parallel",), args=[x] + list(gains) + list(weights),
        ride=ride)


def _shift_down(prev, cur, by):
    big = jnp.concatenate([prev, cur], axis=0)
    return pltpu.roll(big, by, 0)[prev.shape[0]:]


def _shift_up(cur, nxt, by):
    big = jnp.concatenate([cur, nxt], axis=0)
    return pltpu.roll(big, big.shape[0] - by, 0)[:cur.shape[0]]


def conv_mix_out(bcx, conv_w, w_out, g_post, res, *, name, ride=None, tm=ROW_TILE):
    T, D = res.shape
    hb = tm // BF16_ROWS

    def body(b_ref, c_ref, u_ref, cp_ref, up_ref, cw_ref, w_ref, g_ref, r_ref, h_ref, z_ref, y_ref):
        i = pl.program_id(0)
        cu = c_ref[...].astype(F32) * u_ref[...].astype(F32)
        cup = cp_ref[...].astype(F32) * up_ref[...].astype(F32)
        cup = jnp.where(i == 0, 0.0, cup)
        cv = (cw_ref[0:1, :] * _shift_down(cup, cu, 2) + cw_ref[1:2, :] * _shift_down(cup, cu, 1)
              + cw_ref[2:3, :] * cu)
        y = (b_ref[...].astype(F32) * cv).astype(BF16)
        y_ref[...] = y
        z = _dot(y, w_ref[...])
        z_ref[...] = z.astype(BF16)
        h_ref[...] = (r_ref[...] + z * _rms_r(z) * g_ref[...]).astype(STREAM)

    tile = lambda col: pl.BlockSpec((tm, D), lambda i: (i, col))
    halo = lambda col: pl.BlockSpec((BF16_ROWS, D), lambda i: (jnp.maximum(i * hb - 1, 0), col))
    row = pl.BlockSpec((tm, D), lambda i: (i, 0))
    return _call(
        body, name=name, grid=(T // tm,),
        in_specs=[tile(0), tile(1), tile(2), halo(1), halo(2),
                  pl.BlockSpec((3, D), lambda i: (0, 0)),
                  pl.BlockSpec((D, D), lambda i: (0, 0)),
                  pl.BlockSpec((1, D), lambda i: (0, 0)), row],
        out_specs=[row, row, row],
        out_shape=[SDS((T, D), STREAM), SDS((T, D), BF16), SDS((T, D), BF16)],
        semantics=("parallel",), args=(bcx, bcx, bcx, bcx, bcx, conv_w, w_out, g_post, res), ride=ride)


def _normbwd_then_nt(dh, zf, g_ref, w_ref, dz_ref, dg_ref, o_ref, first):
    dz, zh = _rmsnorm_bwd(zf, g_ref[...], dh)
    dz = dz.astype(BF16)
    dz_ref[...] = dz
    _accumulate(dg_ref, first, jnp.sum(dh * zh, axis=0, keepdims=True))
    o_ref[...] = _dot_nt(dz, w_ref[...]).astype(BF16)


def _then_specs(then, tm, T, D):
    z, g, w = then
    K = w.shape[0]
    row = pl.BlockSpec((tm, D), lambda i: (i, 0))
    vec = pl.BlockSpec((1, D), lambda i: (0, 0))
    in_specs = [row, vec, pl.BlockSpec((K, D), lambda i: (0, 0), pipeline_mode=pl.Buffered(1))]
    out_specs = [row, vec, pl.BlockSpec((tm, K), lambda i: (i, 0))]
    out_shape = [SDS((T, D), BF16), SDS((1, D), F32), SDS((T, K), BF16)]
    return in_specs, out_specs, out_shape


def plain_mix_out(a, w, g_post, res, *, name, target=None, ride=None, tm=ROW_TILE):
    T, D = res.shape
    tm = min(tm, T)
    K = a.shape[1]
    with_loss = target is not None

    def body(a_ref, w_ref, g_ref, r_ref, *rest):
        subs = _sub_tiles(tm)
        zs = [_dot(a_ref[rows, :], w_ref[...]) for rows in subs]
        if not with_loss:
            h_ref, z_ref = rest
            for rows, z in zip(subs, zs):
                h_ref[rows, :] = (r_ref[rows, :].astype(F32) + z * _rms_r(z) * g_ref[...]).astype(STREAM)
                z_ref[rows, :] = z.astype(BF16)
            return
        t_ref, h_ref, dz_ref, dg_ref, da_ref, loss_ref = rest
        first = pl.program_id(0) == 0
        loss, dg = jnp.zeros((), F32), jnp.zeros((1, D), F32)
        for rows, z in zip(subs, zs):
            diff = r_ref[rows, :].astype(F32) + z * _rms_r(z) * g_ref[...] - t_ref[rows, :]
            dh = diff * (1.0 / D)
            h_ref[rows, :] = dh.astype(STREAM)
            loss = loss + jnp.sum(diff * diff)
            dz, zh = _rmsnorm_bwd(z, g_ref[...], dh)
            dz = dz.astype(BF16)
            dz_ref[rows, :] = dz
            dg = dg + jnp.sum(dh * zh, axis=0, keepdims=True)
            da_ref[rows, :] = _dot_nt(dz, w_ref[...]).astype(BF16)
        _accumulate(loss_ref, first, jnp.full(loss_ref.shape, 0.5 / D, F32) * loss)
        _accumulate(dg_ref, first, dg)

    row = pl.BlockSpec((tm, D), lambda i: (i, 0))
    vec = pl.BlockSpec((1, D), lambda i: (0, 0))
    in_specs = [pl.BlockSpec((tm, K), lambda i: (i, 0)), pl.BlockSpec((K, D), lambda i: (0, 0)), vec, row]
    if with_loss:
        in_specs.append(row)
        out_specs = [row, row, vec, pl.BlockSpec((tm, K), lambda i: (i, 0)), pl.BlockSpec((8, 128), lambda i: (0, 0))]
        out_shape = [SDS((T, D), STREAM), SDS((T, D), BF16), SDS((1, D), F32), SDS((T, K), BF16), SDS((8, 128), F32)]
    else:
        out_specs, out_shape = [row, row], [SDS((T, D), STREAM), SDS((T, D), BF16)]
    return _call(
        body, name=name, grid=(T // tm,), in_specs=in_specs, out_specs=out_specs, out_shape=out_shape,
        semantics=("arbitrary",), args=(a, w, g_post, res) + ((target,) if with_loss else ()), ride=ride)


def _silu_grads(d, g, u):
    sg = jax.nn.sigmoid(g)
    return d * u * (sg * (1.0 + g * (1.0 - sg))), d * (g * sg)


def _sub_tiles(tm):
    return [pl.ds(k, min(MXU_WIDTH, tm)) for k in range(0, tm, MXU_WIDTH)]


def norm_swiglu_in(x, g, w, *, name, ride=None, tm=ROW_TILE):
    T, D = x.shape
    F = w.shape[1] // 2

    def body(x_ref, g_ref, wg_ref, wu_ref, gu_ref, a_ref, xt_ref):
        subs = _sub_tiles(tm)
        xns = []
        for rows in subs:
            xf = x_ref[rows, :].astype(F32)
            xns.append(xf * _rms_r(xf) * g_ref[...])
        xbs = [xn.astype(BF16) for xn in xns]
        gates = [_dot(xb, wg_ref[...]).astype(BF16) for xb in xbs]
        ups = [_dot(xb, wu_ref[...]).astype(BF16) for xb in xbs]
        for rows, gate, up in zip(subs, gates, ups):
            gu_ref[0, rows, :] = gate
            gu_ref[1, rows, :] = up
            a_ref[rows, :] = gate * jax.nn.sigmoid(gate) * up
        for rows, xn in zip(subs, xns):
            xt_ref[:, rows] = xn.T.astype(BF16)

    half = lambda s: pl.BlockSpec((D, F), lambda i: (0, s), pipeline_mode=pl.Buffered(1))
    return _call(
        body, name=name, grid=(T // tm,),
        in_specs=[pl.BlockSpec((tm, D), lambda i: (i, 0)), pl.BlockSpec((1, D), lambda i: (0, 0)), half(0), half(1)],
        out_specs=[pl.BlockSpec((2, tm, F), lambda i: (0, i, 0)), pl.BlockSpec((tm, F), lambda i: (i, 0)),
                   pl.BlockSpec((D, tm), lambda i: (0, i))],
        out_shape=[SDS((2, T, F), BF16), SDS((T, F), BF16), SDS((D, T), BF16)],
        semantics=("parallel",), args=(x, g, w, w), ride=ride)


def swiglu_bwd_tn(xt, dact, gu, *, name, ride=None, tb=MXU_WIDTH):
    D, T = xt.shape
    F = dact.shape[1]

    def body(xt_ref, d_ref, g_ref, u_ref, o_ref):
        dg, du = _silu_grads(d_ref[...], g_ref[...], u_ref[...])
        o_ref[0] = _dot(xt_ref[...], dg).astype(BF16)
        o_ref[1] = _dot(xt_ref[...], du).astype(BF16)

    col = lambda s: pl.BlockSpec((None, T, tb), lambda j: (s, 0, j))
    out = _call(
        body, name=name, grid=(F // tb,),
        in_specs=[pl.BlockSpec((D, T), lambda j: (0, 0), pipeline_mode=pl.Buffered(1)),
                  pl.BlockSpec((T, tb), lambda j: (0, j)), col(0), col(1)],
        out_specs=[pl.BlockSpec((2, D, tb), lambda j: (0, 0, j))],
        out_shape=[SDS((2, D, F), BF16)],
        semantics=("parallel",), args=(xt, dact, gu, gu), ride=ride)
    return out[0] if ride is None else (out[0][0], out[1])


def swiglu_bwd_in(dact, gu, w, h_in, g, dh_out, then, *, name, ride=None, tm=ROW_TILE):
    T, D = h_in.shape
    F = dact.shape[1]

    def body(d_ref, gg_ref, uu_ref, wg_ref, wu_ref, h_ref, g_ref, dh_ref, z_ref, g2_ref, w2_ref,
             o_ref, dg_ref, dz_ref, dg2_ref, da_ref):
        first = pl.program_id(0) == 0
        subs = _sub_tiles(tm)
        dns = []
        for rows in subs:
            dgate, dup = _silu_grads(d_ref[rows, :], gg_ref[rows, :], uu_ref[rows, :])
            dns.append(_dot_nt(dgate, wg_ref[...]) + _dot_nt(dup, wu_ref[...]))
        dg, dg2 = jnp.zeros((1, D), F32), jnp.zeros((1, D), F32)
        for rows, dn in zip(subs, dns):
            dx, hh = _rmsnorm_bwd(h_ref[rows, :].astype(F32), g_ref[...], dn)
            dh_in = dh_ref[rows, :] + dx
            o_ref[rows, :] = dh_in.astype(STREAM)
            dg = dg + jnp.sum(dn * hh, axis=0, keepdims=True)
            dz, zh = _rmsnorm_bwd(z_ref[rows, :].astype(F32), g2_ref[...], dh_in)
            dz = dz.astype(BF16)
            dz_ref[rows, :] = dz
            dg2 = dg2 + jnp.sum(dh_in * zh, axis=0, keepdims=True)
            da_ref[rows, :] = _dot_nt(dz, w2_ref[...]).astype(BF16)
        _accumulate(dg_ref, first, dg)
        _accumulate(dg2_ref, first, dg2)

    row = pl.BlockSpec((tm, D), lambda i: (i, 0))
    vec = pl.BlockSpec((1, D), lambda i: (0, 0))
    part = lambda s: pl.BlockSpec((None, tm, F), lambda i: (s, i, 0))
    half = lambda s: pl.BlockSpec((D, F), lambda i: (0, s), pipeline_mode=pl.Buffered(1))
    then_in, then_out, then_shape = _then_specs(then, tm, T, D)
    return _call(
        body, name=name, grid=(T // tm,),
        in_specs=[pl.BlockSpec((tm, F), lambda i: (i, 0)), part(0), part(1), half(0), half(1), row, vec, row] + then_in,
        out_specs=[row, vec] + then_out,
        out_shape=[SDS((T, D), STREAM), SDS((1, D), F32)] + then_shape,
        semantics=("arbitrary",), args=(dact, gu, gu, w, w, h_in, g, dh_out) + tuple(then), ride=ride)


def rope_tables(T):
    half = ROT_DIM // 2
    inv_freq = ROPE_THETA ** (-jnp.arange(0, ROT_DIM, 2, dtype=F32) / ROT_DIM)
    ang = (jnp.arange(T, dtype=F32)[:, None] * inv_freq[None, :]).T
    cos, sin = jnp.cos(ang), jnp.sin(ang)
    rest = HEAD_DIM - ROT_DIM
    one, zero = jnp.ones((rest, T), F32), jnp.zeros((rest, T), F32)
    zh = jnp.zeros((half, T), F32)
    fac = jnp.concatenate([cos, cos, one], axis=0)
    up = jnp.concatenate([-sin, zh, zero], axis=0)
    down = jnp.concatenate([zh, sin, zero], axis=0)
    return jnp.stack([fac, up, down])


def _rope(t, tab):
    half = ROT_DIM // 2
    return t * tab[0] + pltpu.roll(t, HEAD_DIM - half, 0) * tab[1] + pltpu.roll(t, half, 0) * tab[2]


def _rope_t(d, tab):
    half = ROT_DIM // 2
    return d * tab[0] + pltpu.roll(d * tab[1], half, 0) + pltpu.roll(d * tab[2], HEAD_DIM - half, 0)


def _head(t, h):
    return t[h * HEAD_DIM:(h + 1) * HEAD_DIM]


def _band(n, group):
    kj = lax.broadcasted_iota(jnp.int32, (2 * BLOCK, BLOCK), 0)
    qi = lax.broadcasted_iota(jnp.int32, (2 * BLOCK, BLOCK), 1)
    mask = (kj > qi) & (kj <= qi + BLOCK) & ((n > 0) | (kj >= BLOCK))
    return jnp.tile(mask, (1, group))


def _attn_specs(D, kvd, nb):
    cur = lambda n: jnp.minimum(n, nb - 1)
    prev = lambda n: jnp.maximum(cur(n) - 1, 0)
    return [pl.BlockSpec((BLOCK, D), lambda n: (cur(n), 0)),
            pl.BlockSpec((BLOCK, kvd), lambda n: (prev(n), 0)),
            pl.BlockSpec((BLOCK, kvd), lambda n: (cur(n), 0)),
            pl.BlockSpec((BLOCK, kvd), lambda n: (prev(n), 1)),
            pl.BlockSpec((BLOCK, kvd), lambda n: (cur(n), 1)),
            pl.BlockSpec((3, HEAD_DIM, BLOCK), lambda n: (0, 0, prev(n))),
            pl.BlockSpec((3, HEAD_DIM, BLOCK), lambda n: (0, 0, cur(n))),
            pl.BlockSpec(memory_space=pltpu.SMEM)]


def _attn_operands(q_ref, kp_ref, k_ref, vp_ref, v_ref, tp_ref, t_ref):
    flip = lambda ref: ref[...].astype(F32).T
    tab = t_ref[...]
    kt = jnp.concatenate([flip(kp_ref), flip(k_ref)], axis=1)
    vt = jnp.concatenate([flip(vp_ref), flip(v_ref)], axis=1)
    return flip(q_ref), kt, vt, tab, jnp.concatenate([tp_ref[...], tab], axis=2)


SCORE_SCALE = 1.0 / math.sqrt(HEAD_DIM)
HEADS_TOGETHER = 4


def _group_heads(t, first, count, tab=None):
    heads = [_head(t, first + g) for g in range(count)]
    if tab is not None:
        heads = [_rope(h, tab) * SCORE_SCALE for h in heads]
    return jnp.concatenate(heads, axis=1).astype(BF16)


def _sink_row(s_ref, first, count):
    which = lax.broadcasted_iota(jnp.int32, (1, count * BLOCK), 1) // BLOCK
    row = jnp.zeros((1, count * BLOCK), F32)
    for g in range(count):
        row = jnp.where(which == g, s_ref[0, first + g], row)
    return row


def _sum_keys(t):
    return _dot(jnp.ones((8, t.shape[0]), BF16), t)[0:1]


def _softmax(scores, sink, mask):
    s = jnp.where(mask, scores.astype(BF16), NEG)
    m = jnp.maximum(jnp.max(s, axis=0, keepdims=True).astype(F32), sink).astype(BF16)
    e = jnp.exp(s - m)
    m = m.astype(F32)
    return e, m, 1.0 / (_sum_keys(e) + jnp.exp(sink - m))


def _per_head(row, count):
    return [row[:, g * BLOCK:(g + 1) * BLOCK] for g in range(count)]


def attention_fwd(q, kv, tabs, sinks, *, name, ride=None):
    T, D = q.shape
    kvd = kv.shape[1] // 2
    heads = D // HEAD_DIM
    group = heads // N_KV_HEADS

    def body(q_ref, kp_ref, k_ref, vp_ref, v_ref, tp_ref, t_ref, s_ref, o_ref, stat_ref):
        gs = HEADS_TOGETHER
        mask = _band(pl.program_id(0), gs)
        qt, kt, vt, tab, tab2 = _attn_operands(q_ref, kp_ref, k_ref, vp_ref, v_ref, tp_ref, t_ref)
        firsts = [(j, first) for j in range(N_KV_HEADS) for first in range(j * group, (j + 1) * group, gs)]
        ks = [_rope(_head(kt, j), tab2).astype(BF16) for j in range(N_KV_HEADS)]
        scores = [_dot_tn(ks[j], _group_heads(qt, first, gs, tab)) for j, first in firsts]
        soft = [_softmax(s, _sink_row(s_ref, first, gs), mask) for s, (j, first) in zip(scores, firsts)]
        outs, ms, invs = [], [], []
        for (e, m, inv), (j, first) in zip(soft, firsts):
            o = _dot(_head(vt, j).astype(BF16), e) * inv
            outs += [o[:, g * BLOCK:(g + 1) * BLOCK] for g in range(gs)]
            ms += _per_head(m, gs)
            invs += _per_head(inv, gs)
        o_ref[...] = jnp.concatenate(outs, axis=0).T.astype(BF16)
        stat_ref[0] = jnp.concatenate(ms, axis=0)
        stat_ref[1] = jnp.concatenate(invs, axis=0)

    return _call(
        body, name=name, grid=(T // BLOCK,),
        in_specs=_attn_specs(D, kvd, T // BLOCK),
        out_specs=[pl.BlockSpec((BLOCK, D), lambda n: (n, 0)), pl.BlockSpec((2, heads, BLOCK), lambda n: (0, 0, n))],
        out_shape=[SDS((T, D), BF16), SDS((2, heads, T), F32)],
        semantics=("parallel",), args=(q, kv, kv, kv, kv, tabs, tabs, sinks), ride=ride)


def attention_bwd(q, kv, tabs, sinks, do, o, stats, *, name, ride=None):
    T, D = q.shape
    kvd = kv.shape[1] // 2
    heads = D // HEAD_DIM
    group = heads // N_KV_HEADS
    nb = T // BLOCK

    def body(q_ref, kp_ref, k_ref, vp_ref, v_ref, tp_ref, t_ref, s_ref, do_ref, o_ref, stat_ref,
             dq_ref, dkv_ref, ds_ref, carry):
        n = pl.program_id(0)

        @pl.when(n == 0)
        def _():
            carry[...] = jnp.zeros_like(carry)

        @pl.when(n < nb)
        def _():
            block(n, q_ref, kp_ref, k_ref, vp_ref, v_ref, tp_ref, t_ref, s_ref, do_ref, o_ref, stat_ref,
                  dq_ref, dkv_ref, ds_ref, carry)

        @pl.when(n == nb)
        def _():
            dkv_ref[...] = carry[...].astype(BF16)

    def block(n, q_ref, kp_ref, k_ref, vp_ref, v_ref, tp_ref, t_ref, s_ref, do_ref, o_ref, stat_ref,
              dq_ref, dkv_ref, ds_ref, carry):
        gs = HEADS_TOGETHER
        mask = _band(n, gs)
        qt, kt, vt, tab, tab2 = _attn_operands(q_ref, kp_ref, k_ref, vp_ref, v_ref, tp_ref, t_ref)
        dot = do_ref[...].astype(F32).T
        odo = o_ref[...].astype(F32).T * dot
        dl_all = jnp.concatenate([jnp.sum(_head(odo, h), axis=0, keepdims=True) for h in range(heads)], axis=0)
        m_all, inv_all = stat_ref[0], stat_ref[1]
        row = lambda t, first: jnp.concatenate([t[first + g:first + g + 1] for g in range(gs)], axis=1)
        lane = lax.broadcasted_iota(jnp.int32, (8, 128), 1)
        dsink = jnp.zeros((8, 128), F32)
        firsts = [(j, first) for j in range(N_KV_HEADS) for first in range(j * group, (j + 1) * group, gs)]
        ks = [_rope(_head(kt, j), tab2).astype(BF16) for j in range(N_KV_HEADS)]
        vs = [_head(vt, j).astype(BF16) for j in range(N_KV_HEADS)]
        qs = [_group_heads(qt, first, gs, tab) for _, first in firsts]
        dos = [_group_heads(dot, first, gs) for _, first in firsts]
        scores = [_dot_tn(ks[j], q) for q, (j, _) in zip(qs, firsts)]
        dps = [_dot_tn(vs[j], do) for do, (j, _) in zip(dos, firsts)]
        ps, dscs = [], []
        for s, dp, (j, first) in zip(scores, dps, firsts):
            m, inv, dl = row(m_all, first), row(inv_all, first), row(dl_all, first)
            e = jnp.exp(jnp.where(mask, s.astype(BF16), NEG) - m.astype(BF16))
            p = e * inv.astype(BF16)
            dscs.append(p * (dp.astype(BF16) - dl.astype(BF16)))
            ps.append(p)
            weight = jnp.exp(_sink_row(s_ref, first, gs) - m) * inv * dl
            for g in range(gs):
                dsink = dsink - jnp.where(lane == first + g, jnp.sum(weight[:, g * BLOCK:(g + 1) * BLOCK]), 0.0)
        dqs = []
        dks = [jnp.zeros((HEAD_DIM, 2 * BLOCK), F32) for _ in range(N_KV_HEADS)]
        dvs = [jnp.zeros((HEAD_DIM, 2 * BLOCK), F32) for _ in range(N_KV_HEADS)]
        for p, dsc, q, do, (j, _) in zip(ps, dscs, qs, dos, firsts):
            dq = _dot(ks[j], dsc) * SCORE_SCALE
            dqs += [_rope_t(dq[:, g * BLOCK:(g + 1) * BLOCK], tab) for g in range(gs)]
            dks[j] = dks[j] + _dot_nt(q, dsc)
            dvs[j] = dvs[j] + _dot_nt(do, p)
        dks = [_rope_t(dk, tab2) for dk in dks]
        dq_ref[...] = jnp.concatenate(dqs, axis=0).T.astype(BF16)
        dkv = jnp.concatenate(dks + dvs, axis=0)
        dkv_ref[...] = (carry[...] + dkv[:, :BLOCK].T).astype(BF16)
        carry[...] = dkv[:, BLOCK:].T
        _accumulate(ds_ref, n == 0, dsink)

    cur = lambda n: jnp.minimum(n, nb - 1)
    blk = lambda w: pl.BlockSpec((BLOCK, w), lambda n: (cur(n), 0))
    return _call(
        body, name=name, grid=(nb + 1,),
        in_specs=_attn_specs(D, kvd, nb) + [blk(D), blk(D), pl.BlockSpec((2, heads, BLOCK), lambda n: (0, 0, cur(n)))],
        out_specs=[blk(D), pl.BlockSpec((BLOCK, 2 * kvd), lambda n: (jnp.maximum(n - 1, 0), 0)),
                   pl.BlockSpec((8, 128), lambda n: (0, 0))],
        out_shape=[SDS((T, D), BF16), SDS((T, 2 * kvd), BF16), SDS((8, 128), F32)],
        scratch_shapes=[pltpu.VMEM((BLOCK, 2 * kvd), F32)],
        semantics=("arbitrary",), args=(q, kv, kv, kv, kv, tabs, tabs, sinks, do, o, stats), ride=ride)


def matmul_nt_normbwd(da, w, h_in, g, dh_out, *, name, ride=None, tm=ROW_TILE):
    T, D = h_in.shape
    S, _, K = da.shape

    def body(*refs):
        da_refs, w_refs = refs[:S], refs[S:2 * S]
        h_ref, g_ref, dh_ref, o_ref, dg_ref = refs[2 * S:]
        subs = _sub_tiles(tm)
        dns = []
        for rows in subs:
            dn = _dot_nt(da_refs[0][rows, :], w_refs[0][...])
            for s in range(1, S):
                dn = dn + _dot_nt(da_refs[s][rows, :], w_refs[s][...])
            dns.append(dn)
        dg = jnp.zeros((1, D), F32)
        for rows, dn in zip(subs, dns):
            dx, hh = _rmsnorm_bwd(h_ref[rows, :].astype(F32), g_ref[...], dn)
            o_ref[rows, :] = dh_ref[rows, :] + dx
            dg = dg + jnp.sum(dn * hh, axis=0, keepdims=True)
        _accumulate(dg_ref, pl.program_id(0) == 0, dg)

    row = pl.BlockSpec((tm, D), lambda i: (i, 0))
    vec = pl.BlockSpec((1, D), lambda i: (0, 0))
    part = lambda s: pl.BlockSpec((None, tm, K), lambda i: (s, i, 0))
    cols = lambda s: pl.BlockSpec((D, K), lambda i: (0, s), pipeline_mode=pl.Buffered(1))
    return _call(
        body, name=name, grid=(T // tm,),
        in_specs=[part(s) for s in range(S)] + [cols(s) for s in range(S)] + [row, vec, row],
        out_specs=[row, vec],
        out_shape=[SDS((T, D), F32), SDS((1, D), F32)],
        semantics=("arbitrary",), args=[da] * S + [w] * S + [h_in, g, dh_out], ride=ride)


def matmuls_nt_normbwd(das, ws, h_in, gs, dh_out, then, *, name, ride=None, tm=ROW_TILE):
    T, D = h_in.shape
    tm = min(tm, T)
    n = len(das)

    def body(*refs):
        da_refs, w_refs, g_refs = refs[:n], refs[n:2 * n], refs[2 * n:3 * n]
        h_ref, dh_ref, z_ref, g2_ref, w2_ref, o_ref = refs[3 * n:3 * n + 6]
        dg_refs, (dz_ref, dg2_ref, da_ref) = refs[3 * n + 6:4 * n + 6], refs[4 * n + 6:]
        first = pl.program_id(0) == 0
        subs = _sub_tiles(tm)
        dns = [[_dot_nt(da_ref_[rows, :], w_ref[...]) for da_ref_, w_ref in zip(da_refs, w_refs)] for rows in subs]
        dgs, dg2 = [jnp.zeros((1, D), F32) for _ in range(n)], jnp.zeros((1, D), F32)
        for rows, dn_sub in zip(subs, dns):
            hf = h_ref[rows, :].astype(F32)
            r = _rms_r(hf)
            hh = hf * r
            total = dh_ref[rows, :].astype(F32)
            for b, (dn, g_ref) in enumerate(zip(dn_sub, g_refs)):
                gd = g_ref[...] * dn
                total = total + r * (gd - hh * jnp.mean(hh * gd, axis=-1, keepdims=True))
                dgs[b] = dgs[b] + jnp.sum(dn * hh, axis=0, keepdims=True)
            o_ref[rows, :] = total.astype(STREAM)
            dz, zh = _rmsnorm_bwd(z_ref[rows, :].astype(F32), g2_ref[...], total)
            dz = dz.astype(BF16)
            dz_ref[rows, :] = dz
            dg2 = dg2 + jnp.sum(total * zh, axis=0, keepdims=True)
            da_ref[rows, :] = _dot_nt(dz, w2_ref[...]).astype(BF16)
        for dg_ref, dg in zip(dg_refs + (dg2_ref,), dgs + [dg2]):
            _accumulate(dg_ref, first, dg)

    row = pl.BlockSpec((tm, D), lambda i: (i, 0))
    vec = pl.BlockSpec((1, D), lambda i: (0, 0))
    then_in, then_out, then_shape = _then_specs(then, tm, T, D)
    return _call(
        body, name=name, grid=(T // tm,),
        in_specs=[pl.BlockSpec((tm, da.shape[1]), lambda i: (i, 0)) for da in das]
        + [pl.BlockSpec(w.shape, lambda i: (0, 0)) for w in ws] + [vec] * n + [row, row] + then_in,
        out_specs=[row] + [vec] * n + then_out,
        out_shape=[SDS((T, D), STREAM)] + [SDS((1, D), F32)] * n + then_shape,
        semantics=("arbitrary",), args=list(das) + list(ws) + list(gs) + [h_in, dh_out] + list(then), ride=ride)


def matmul_tn(a, b, *, tb, name, ride=None, ta=MXU_WIDTH):
    T, Ka = a.shape
    S, _, Nb = b.shape
    per = Nb // tb

    def body(a_ref, b_ref, o_ref):
        o_ref[...] = _dot_tn(a_ref[...], b_ref[...]).astype(BF16)

    out = _call(
        body, name=name, grid=(S * per, Ka // ta),
        in_specs=[pl.BlockSpec((T, ta), lambda j, i: (0, i)),
                  pl.BlockSpec((None, T, tb), lambda j, i: (j // per, 0, j % per))],
        out_specs=[pl.BlockSpec((ta, tb), lambda j, i: (i, j))],
        out_shape=[SDS((Ka, S * Nb), BF16)],
        semantics=("parallel", "parallel"), args=(a, b), ride=ride)
    return out[0] if ride is None else (out[0][0], out[1])


def conv_bwd(dy, bcx, conv_w, *, name, ride=None, tm=ROW_TILE):
    T, D = dy.shape
    nt = T // tm
    hb = tm // BF16_ROWS
    last = T // BF16_ROWS - 1

    def body(dy_ref, dyn_ref, b_ref, bn_ref, c_ref, u_ref, cp_ref, up_ref, cw_ref, o_ref, dw_ref):
        i = pl.program_id(0)
        c, u = c_ref[...].astype(F32), u_ref[...].astype(F32)
        cu = c * u
        cup = jnp.where(i == 0, 0.0, cp_ref[...].astype(F32) * up_ref[...].astype(F32))
        cu1, cu2 = _shift_down(cup, cu, 1), _shift_down(cup, cu, 2)
        w0, w1, w2 = cw_ref[0:1, :], cw_ref[1:2, :], cw_ref[2:3, :]
        dyf = dy_ref[...].astype(F32)
        o_ref[:, 0:D] = (dyf * (w0 * cu2 + w1 * cu1 + w2 * cu)).astype(BF16)
        dcv = dyf * b_ref[...].astype(F32)
        dcvn = jnp.where(i == nt - 1, 0.0, dyn_ref[...].astype(F32) * bn_ref[...].astype(F32))
        dcu = w2 * dcv + w1 * _shift_up(dcv, dcvn, 1) + w0 * _shift_up(dcv, dcvn, 2)
        o_ref[:, D:2 * D] = (dcu * u).astype(BF16)
        o_ref[:, 2 * D:3 * D] = (dcu * c).astype(BF16)
        row = lax.broadcasted_iota(jnp.int32, (8, D), 0)
        dw = jnp.zeros((8, D), F32)
        for tap, t in enumerate((cu2, cu1, cu)):
            dw = jnp.where(row == tap, jnp.sum(dcv * t, axis=0, keepdims=True), dw)
        _accumulate(dw_ref, i == 0, dw)

    tile = lambda col: pl.BlockSpec((tm, D), lambda i: (i, col))
    prev = lambda col: pl.BlockSpec((BF16_ROWS, D), lambda i: (jnp.maximum(i * hb - 1, 0), col))
    nxt = lambda col: pl.BlockSpec((BF16_ROWS, D), lambda i: (jnp.minimum((i + 1) * hb, last), col))
    return _call(
        body, name=name, grid=(nt,),
        in_specs=[tile(0), nxt(0), tile(0), nxt(0), tile(1), tile(2), prev(1), prev(2),
                  pl.BlockSpec((3, D), lambda i: (0, 0))],
        out_specs=[pl.BlockSpec((tm, 3 * D), lambda i: (i, 0)), pl.BlockSpec((8, D), lambda i: (0, 0))],
        out_shape=[SDS((T, 3 * D), BF16), SDS((8, D), F32)],
        semantics=("arbitrary",), args=(dy, dy, bcx, bcx, bcx, bcx, bcx, bcx, conv_w), ride=ride)


class NoTraffic:
    def ride(self, kernel_name):
        return None

    def landed(self, kernel_name, results, wts):
        pass

    def grad(self, key, value):
        pass


def local_step(x, target, wts, vec, traffic):
    T, D = x.shape
    tabs = rope_tables(T)
    small = {}

    def run(builder, *args, name, **kw):
        ride = traffic.ride(name)
        if ride is None:
            return builder(*args, name=name, **kw)
        out, extra = builder(*args, name=name, ride=ride, **kw)
        traffic.landed(name, extra, wts)
        return out

    bcx, xn1 = run(norm_matmul, x, vec["a_pre"], wts["w_in"], tn=3 * D, split=1, name="a_in")
    bcx = bcx[0]
    h1, z0, y0 = run(conv_mix_out, bcx, vec["conv_w"], wts["w_out"], vec["a_post"], x, name="a_out")
    gu0, act0, xt2 = run(norm_swiglu_in, h1, vec["ffn_pre0"], wts["gu0"], name="ffn0_in")
    h2, z1 = run(plain_mix_out, act0, wts["wd0"], vec["ffn_post0"], h1, name="ffn0_out")
    kvp, xkv, qp, xq = run(norm2_matmul, h2, [vec["kv_norm"], vec["b_pre"]], [wts["w_kv"], wts["w_q"]],
                           name="kvq_in")
    attn, attn_stats = run(attention_fwd, qp, kvp, tabs, vec["sinks"], name="attn_fwd")
    h3, z2 = plain_mix_out(attn, wts["w_o"], vec["b_post"], h2, name="attn_out", tm=BIG_ROW_TILE)
    gu1, act1, xt3 = run(norm_swiglu_in, h3, vec["ffn_pre1"], wts["gu1"], name="ffn1_in")
    dy, dz3, small["ffn_post1"], dact1, loss = plain_mix_out(act1, wts["wd1"], vec["ffn_post1"], h3, name="ffn1_out",
                                                             target=target)

    def ffn_bwd(layer, dz, dact, gu, act, xt, h_in, dh, then, gu_first):
        tag = "ffn%d" % layer
        dwd = lambda: traffic.grad("wd%d" % layer, run(matmul_tn, act, dz[None], tb=D, name=tag + "_dwd"))
        dwgu = lambda: traffic.grad("gu%d" % layer, run(swiglu_bwd_tn, xt, dact, gu, name=tag + "_dwgu"))
        for step in ((dwgu, dwd) if gu_first else (dwd, dwgu)):
            step()
        dh_in, small["ffn_pre%d" % layer], dz_, dg_, da_ = run(
            swiglu_bwd_in, dact, gu, wts["gu%d" % layer], h_in, vec["ffn_pre%d" % layer], dh, then,
            name=tag + "_in_bwd")
        return dh_in, dz_, dg_, da_

    dh3, dz2, small["b_post"], dattn = ffn_bwd(1, dz3, dact1, gu1, act1, xt3, h3, dy,
                                               (z2, vec["b_post"], wts["w_o"]), gu_first=False)
    traffic.grad("w_o", matmul_tn(attn, dz2[None], tb=D, name="attn_dwo"))
    dq, dkv, small["sinks"] = run(attention_bwd, qp, kvp, tabs, vec["sinks"], dattn, attn, attn_stats,
                                  name="attn_bwd")
    traffic.grad("w_q", matmul_tn(xq, dq[None], tb=D, name="attn_dwq"))
    traffic.grad("w_kv", matmul_tn(xkv, dkv[None], tb=dkv.shape[1], name="attn_dwkv"))
    dh2, small["b_pre"], small["kv_norm"], dz1, small["ffn_post0"], dact0 = run(
        matmuls_nt_normbwd, [dq, dkv], [wts["w_q"], wts["w_kv"]], h2, [vec["b_pre"], vec["kv_norm"]], dh3,
        (z1, vec["ffn_post0"], wts["wd0"]), name="qkv_in_bwd")
    dh1, dz0, small["a_post"], dyc = ffn_bwd(0, dz1, dact0, gu0, act0, xt2, h1, dh2,
                                             (z0, vec["a_post"], wts["w_out"]), gu_first=True)
    traffic.grad("w_out", run(matmul_tn, y0, dz0[None], tb=D, name="a_dwout"))
    dbcx, small["conv_w"] = run(conv_bwd, dyc, bcx, vec["conv_w"], name="a_conv_bwd")
    traffic.grad("w_in", run(matmul_tn, xn1, dbcx[None], tb=3 * D // 2, name="a_dwin"))
    dx, small["a_pre"] = run(matmul_nt_normbwd, dbcx[None], wts["w_in"], x, vec["a_pre"], dh1, name="a_in_bwd")
    return loss, dx, small


SMALL_ROWS = 16
LOSS_ROW = 13

WHOLE = None
GATHER_PLAN = {"cast_rest": [("w_in", WHOLE)],
               "a_in": [("w_out", WHOLE), ("gu0", (0, 18))],
               "a_out": [("gu0", (18, 14))],
               "ffn0_in": [("wd0", WHOLE), ("w_kv", WHOLE), ("w_q", WHOLE), ("gu1", (0, 4))],
               "ffn0_out": [("w_o", WHOLE), ("gu1", (4, 8))],
               "attn_fwd": [("gu1", (12, 20))],
               "ffn1_in": [("wd1", WHOLE)]}
PAIR_PLAN = {"ffn1_dwgu": ["wd1"], "ffn1_in_bwd": ["gu1"], "qkv_in_bwd": ["w_o", "w_q", "w_kv"],
             "ffn0_dwd": ["gu0"], "ffn0_in_bwd": ["wd0"], "a_conv_bwd": ["w_out"], "chip_reduce_early": ["w_in"]}
CHIP_PLAN = {"ffn1_in_bwd": [("wd1", WHOLE)], "attn_bwd": [("gu1", WHOLE)],
             "ffn0_dwgu": [("w_o", WHOLE), ("w_q", WHOLE), ("w_kv", WHOLE)],
             "ffn0_in_bwd": [("gu0", WHOLE)], "a_conv_bwd": [("wd0", (0, 12))],
             "a_dwin": [("wd0", (12, 10)), ("w_out", WHOLE)], "a_in_bwd": [("w_in", WHOLE)]}
HALF_PLAN = {"a_in_bwd": ["gu0", "gu1", "wd0", "wd1", "w_kv", "w_q", "w_o", "w_out"]}
GRAD_KIND = dict(KIND, gu0="split", gu1="split")


class Traffic:
    def __init__(self, wholes, quarter, c_arr, pc_arr):
        self.wholes, self.quarter, self.c_arr, self.pc_arr = wholes, quarter, c_arr, pc_arr
        self.views, self.sums, self.got = {}, {}, {}
        self.reduced = {}
        self.stages = {}

    def reduce(self, keys, name):
        args = ([self.sums[k] for k in keys], [self.got[k] for k in keys], [GRAD_KIND[k] for k in keys], self.pc_arr)
        if name not in PAIR_PLAN:
            return chip_reduce(*args, name=name)
        pairs = PAIR_PLAN[name]
        out, got = chip_reduce(*args, name=name, ride=pair_ride([self.views[k] for k in pairs]))
        self.pair_sums(pairs, got)
        return out

    def pair_sums(self, keys, got):
        groups = {}
        for k, theirs in zip(keys, got):
            groups.setdefault(self.views[k].shape[:3], []).append((k, theirs))
        for members in groups.values():
            names = [k for k, _ in members]
            sums = pair_add([self.views[k] for k in names], [theirs for _, theirs in members], self.c_arr,
                            name="pair_add_" + "_".join(names))
            self.sums.update(zip(names, sums))

    def ride(self, name, small=None):
        rides, stages = [], []
        if name in GATHER_PLAN:
            plan = GATHER_PLAN[name]
            rides.append(gather_ride([self.wholes[k] for k, _ in plan],
                                     [(KIND[k], self.quarter[k], part) for k, part in plan], small))
            stages.append(("gather", [k for k, _ in plan]))
        if name in HALF_PLAN:
            keys = HALF_PLAN[name]
            rides.append(half_ride(self.reduce(keys, "chip_reduce_early")))
            stages.append(("half", keys))
        if name in CHIP_PLAN:
            plan = CHIP_PLAN[name]
            rides.append(chip_ride([self.sums[k] for k, _ in plan],
                                   [(GRAD_KIND[k], self.quarter[k], part) for k, part in plan],
                                   earlier=[self.got.get(k) for k, _ in plan]))
            stages.append(("chip", [k for k, _ in plan]))
        if name in PAIR_PLAN:
            keys = PAIR_PLAN[name]
            rides.append(pair_ride([self.views[k] for k in keys]))
            stages.append(("pair", keys))
        self.stages[name] = stages
        return join(rides)

    def landed(self, name, results, wts):
        results = list(results)
        for stage, keys in self.stages[name]:
            mine, results = results[:len(keys)], results[len(keys):]
            if stage == "gather":
                for k, whole in zip(keys, mine):
                    self.wholes[k] = wts[k] = whole
            elif stage == "chip":
                self.got.update(zip(keys, mine))
            elif stage == "half":
                self.reduced.update(zip(keys, mine))
            else:
                self.pair_sums(keys, mine)

    def grad(self, key, value):
        r, ws = self.quarter[key]
        view = {"row": (N_CHIPS, 2, r // 2, ws), "col": (1, 2, r // 2, N_CHIPS * ws), "split": (2, 2, r // 2, 2 * ws)}
        self.views[key] = value.reshape(view[GRAD_KIND[key]])


def kernel(x, a_pre_norm, a_w_in, a_conv_w, a_w_out, a_post_norm, ffn_pre_norm, ffn_w_gate_up, ffn_w_down, ffn_post_norm, kv_norm, w_kv, b_pre_norm, b_w_q, b_sinks, b_w_o, b_post_norm, loss_target, m_a_pre_norm, m_a_w_in, m_a_conv_w, m_a_w_out, m_a_post_norm, m_ffn_pre_norm, m_ffn_w_gate_up, m_ffn_w_down, m_ffn_post_norm, m_kv_norm, m_w_kv, m_b_pre_norm, m_b_w_q, m_b_sinks, m_b_w_o, m_b_post_norm, v_a_pre_norm, v_a_w_in, v_a_conv_w, v_a_w_out, v_a_post_norm, v_ffn_pre_norm, v_ffn_w_gate_up, v_ffn_w_down, v_ffn_post_norm, v_kv_norm, v_w_kv, v_b_pre_norm, v_b_w_q, v_b_sinks, v_b_w_o, v_b_post_norm):
    T, D = x.shape[1], x.shape[2]
    xi, yi, ci = _place()
    p = 2 * xi + yi
    p_arr = jnp.reshape(p, (1,)).astype(jnp.int32)
    c_arr = jnp.reshape(ci, (1,)).astype(jnp.int32)
    pc_arr = jnp.stack([p, ci]).astype(jnp.int32)
    me_arr = jnp.reshape(4 * xi + 2 * yi + ci, (1,)).astype(jnp.int32)
    qd = D // N_CHIPS

    big = {"w_in": (a_w_in, 0), "w_out": (a_w_out, 0), "gu0": (ffn_w_gate_up, 0), "gu1": (ffn_w_gate_up, 1),
           "wd0": (ffn_w_down, 0), "wd1": (ffn_w_down, 1), "w_kv": (w_kv[None], 0), "w_q": (b_w_q, 0),
           "w_o": (b_w_o, 0)}
    names = list(big)
    quarter = {k: w.shape[1:] for k, (w, _) in big.items()}
    source = lambda k: big[k] + (KIND[k],)
    traffic = Traffic(dict(zip(names[:1], cast_quarters([source(names[0])], p_arr, name="cast_first"))), quarter,
                      c_arr, pc_arr)
    small_shard = jnp.concatenate([a_pre_norm, a_post_norm, a_conv_w[0], jnp.zeros((3, qd), F32)], axis=0)
    wts = {}
    rest, (*landed, small_full) = cast_quarters([source(k) for k in names[1:]], p_arr, name="cast_rest",
                                                ride=traffic.ride("cast_rest", small_shard))
    traffic.wholes.update(zip(names[1:], rest))
    traffic.landed("cast_rest", landed, wts)
    rows = lambda k: jnp.transpose(small_full[:, k], (1, 0, 2)).reshape(-1, D)
    vec = {"a_pre": rows(slice(0, 1)), "a_post": rows(slice(1, 2)), "conv_w": rows(slice(2, 5)),
           "ffn_pre0": ffn_pre_norm[0:1], "ffn_pre1": ffn_pre_norm[1:2],
           "ffn_post0": ffn_post_norm[0:1], "ffn_post1": ffn_post_norm[1:2],
           "kv_norm": kv_norm[None], "b_pre": b_pre_norm, "b_post": b_post_norm, "sinks": b_sinks}

    loss, dx, small = local_step(x[0], loss_target[0], wts, vec, traffic)

    pad = lambda a: jnp.pad(a, ((0, 0), (0, D - a.shape[1])))
    small_block = jnp.concatenate(
        [small["a_pre"], small["a_post"], small["conv_w"][0:3], small["ffn_pre0"], small["ffn_pre1"],
         small["ffn_post0"], small["ffn_post1"], small["kv_norm"], small["b_pre"], small["b_post"],
         pad(small["sinks"][0:1]), pad(loss[0:1]), jnp.zeros((SMALL_ROWS - LOSS_ROW - 1, D), F32)], axis=0)
    late = [k for k in names if k not in traffic.reduced]
    *swapped, small_blocks = alone(join([half_ride(traffic.reduce(late, "chip_reduce_late")),
                                         chip_ride([], [], small_block)]), name="last_exchange")
    traffic.reduced.update(zip(late, swapped))
    grad = {k: traffic.reduced[k].reshape(quarter[k]) for k in names}
    small_sum = small_reduce(small_blocks, me_arr)

    out = {}
    out["a_w_in"] = adamw(a_w_in, [grad["w_in"]], m_a_w_in, v_a_w_in, name="adamw_a_w_in")
    out["a_w_out"] = adamw(a_w_out, [grad["w_out"]], m_a_w_out, v_a_w_out, name="adamw_a_w_out")
    out["ffn_w_gate_up"] = adamw(ffn_w_gate_up, [grad["gu0"], grad["gu1"]], m_ffn_w_gate_up, v_ffn_w_gate_up,
                                 name="adamw_ffn_w_gate_up")
    out["ffn_w_down"] = adamw(ffn_w_down, [grad["wd0"], grad["wd1"]], m_ffn_w_down, v_ffn_w_down,
                              name="adamw_ffn_w_down")
    out["w_kv"] = [o[0] for o in adamw(w_kv[None], [grad["w_kv"]], m_w_kv[None], v_w_kv[None], name="adamw_w_kv")]
    out["b_w_q"] = adamw(b_w_q, [grad["w_q"]], m_b_w_q, v_b_w_q, name="adamw_b_w_q")
    out["b_w_o"] = adamw(b_w_o, [grad["w_o"]], m_b_w_o, v_b_w_o, name="adamw_b_w_o")

    leaves = {"a_pre_norm": (a_pre_norm, m_a_pre_norm, v_a_pre_norm, 0, True),
              "a_post_norm": (a_post_norm, m_a_post_norm, v_a_post_norm, 1, True),
              "a_conv_w": (a_conv_w, m_a_conv_w, v_a_conv_w, 2, True),
              "ffn_pre_norm": (ffn_pre_norm, m_ffn_pre_norm, v_ffn_pre_norm, 5, False),
              "ffn_post_norm": (ffn_post_norm, m_ffn_post_norm, v_ffn_post_norm, 7, False),
              "kv_norm": (kv_norm[None], m_kv_norm[None], v_kv_norm[None], 9, False),
              "b_pre_norm": (b_pre_norm, m_b_pre_norm, v_b_pre_norm, 10, False),
              "b_post_norm": (b_post_norm, m_b_post_norm, v_b_post_norm, 11, False),
              "b_sinks": (b_sinks, m_b_sinks, v_b_sinks, 12, False)}
    for k, results in zip(leaves, adamw_rows(small_sum, p_arr, list(leaves.values()), name="adamw_small")):
        out[k] = [r[0] for r in results] if k == "kv_norm" else results

    order = ["a_pre_norm", "a_w_in", "a_conv_w", "a_w_out", "a_post_norm", "ffn_pre_norm", "ffn_w_gate_up",
             "ffn_w_down", "ffn_post_norm", "kv_norm", "w_kv", "b_pre_norm", "b_w_q", "b_sinks", "b_w_o",
             "b_post_norm"]
    return (small_sum[LOSS_ROW, 0], dx[None], *[out[k][0] for k in order], *[out[k][1] for k in order],
            *[out[k][2] for k in order], *[out[k][3] for k in order])
```

```python
import math

import jax
import jax.numpy as jnp
from jax import lax
from jax.experimental import pallas as pl
from jax.experimental.pallas import tpu as pltpu

F32 = jnp.float32
BF16 = jnp.bfloat16
SDS = jax.ShapeDtypeStruct
MESH = pl.DeviceIdType.MESH
DMA = pltpu.SemaphoreType.DMA
HBM_SPEC = pl.BlockSpec(memory_space=pltpu.HBM)

EPS = 1e-6
NEG = -1e30
HEAD_DIM = 64
N_KV_HEADS = 4
BLOCK = 128
ROT_DIM = HEAD_DIM // 4
ROPE_THETA = 500000.0
N_CHIPS = 4

ADAM_LR = 0.001
ADAM_B1 = 0.9
ADAM_B2 = 0.999
ADAM_EPS = 1e-08
ADAM_WD = 0.01
ADAM_STEP = 10

VMEM_LIMIT_BYTES = 52 * 1024 * 1024
ROW_TILE = 512
BF16_ROWS = 16
STREAM = BF16
MXU_WIDTH = 256

KIND = {"w_in": "col", "gu0": "col", "gu1": "col", "w_out": "row", "wd0": "row", "wd1": "row", "w_kv": "row",
        "w_q": "row", "w_o": "row"}


def _params(*semantics):
    return pltpu.CompilerParams(dimension_semantics=semantics, vmem_limit_bytes=VMEM_LIMIT_BYTES)


def _row_tile(rows, limit, step=8):
    return max(t for t in range(step, limit + 1, step) if rows % t == 0)


def _place():
    return lax.axis_index("x"), lax.axis_index("y"), lax.axis_index("c")


def _other_chips(x, y):
    return [(1 - x, y), (x, 1 - y), (1 - x, 1 - y)]


def _remote(src, dst, send_sem, recv_sem, to):
    return pltpu.make_async_remote_copy(src_ref=src, dst_ref=dst, send_sem=send_sem, recv_sem=recv_sem,
                                        device_id=to, device_id_type=MESH)


def _full_shape(kind, quarter):
    r, ws = quarter
    return (N_CHIPS * r, ws) if kind == "row" else (r, N_CHIPS * ws)


def _rows_of(h, part):
    lo, n = (0, h) if part is None else (part[0] * BF16_ROWS, part[1] * BF16_ROWS)
    assert lo + n <= h, (h, part)
    return lo, n


def _half_of_quarter(ref, kind, quarter, part, q, half):
    r, ws = quarter
    h = r // 2
    lo, n = _rows_of(h, part)
    if kind == "row":
        return ref.at[pl.ds(pl.multiple_of(q * r + half * h + lo, BF16_ROWS), n)]
    return ref.at[pl.ds(pl.multiple_of(half * h + lo, BF16_ROWS), n), pl.ds(pl.multiple_of(q * ws, 128), ws)]


class Ride:
    def __init__(self, operands, out_shape, aliases, sems, make):
        self.operands, self.out_shape, self.aliases, self.sems, self.make = operands, out_shape, aliases, sems, make

    def stages(self, ins, outs, sems):
        made = self.make(ins, outs, sems)
        return made if len(made) == 4 else (made[0], None, None, made[1])


def join(rides):
    rides = [r for r in rides if r is not None]
    if len(rides) < 2:
        return rides[0] if rides else None
    aliases, at = {}, [0, 0, 0]
    cuts = []
    for r in rides:
        aliases.update({at[0] + i: at[1] + o for i, o in r.aliases.items()})
        cuts.append(tuple(at))
        at = [at[0] + len(r.operands), at[1] + len(r.out_shape), at[2] + len(r.sems)]
    cuts.append(tuple(at))

    def make(ins, outs, sem):
        made = [r.stages(ins[lo[0]:hi[0]], outs[lo[1]:hi[1]], sem[lo[2]:hi[2]]) for r, lo, hi in zip(rides, cuts, cuts[1:])]
        def all_of(k):
            def stage():
                for m in made:
                    if m[k] is not None:
                        m[k]()
            return stage

        if all(m[1] is None for m in made):
            return all_of(0), all_of(3)
        return all_of(0), all_of(1), all_of(2), all_of(3)

    return Ride(sum((list(r.operands) for r in rides), []), sum((list(r.out_shape) for r in rides), []), aliases,
                sum((list(r.sems) for r in rides), []), make)


def _call(body, *, name, grid, in_specs, out_specs, out_shape, args, scratch_shapes=(), semantics=None, ride=None,
          prefetch=None):
    pre = 0 if prefetch is None else 1
    n_in, n_out, n_scr = len(in_specs), len(out_specs), len(scratch_shapes)
    r_in, r_out = (len(ride.operands), len(ride.out_shape)) if ride is not None else (0, 0)
    a, b = pre + n_in, pre + n_in + r_in
    c, d = b + n_out, b + n_out + r_out
    e = d + n_scr

    def riding(*refs):
        start, relay, relay_again, finish = ride.stages(refs[a:b], refs[c:d], refs[e:])
        step, steps = pl.program_id(0), 1
        for k, extent in enumerate(grid):
            step = pl.program_id(k) if k == 0 else step * extent + pl.program_id(k)
            steps *= extent
        pl.when(step == 0)(start)
        if relay is not None:
            pl.when(step == steps // 2)(relay)
            pl.when(step == steps - 1)(relay_again)
        body(*refs[:a], *refs[b:c], *refs[d:e])
        pl.when(step == steps - 1)(finish)

    if ride is None:
        kernel_body, extra_in, extra_out, extra_shape, extra_scr, aliases = body, [], [], [], [], {}
        params = _params(*semantics)
    else:
        kernel_body, extra_in, extra_out = riding, [HBM_SPEC] * r_in, [HBM_SPEC] * r_out
        extra_shape, extra_scr = list(ride.out_shape), list(ride.sems)
        aliases = {pre + n_in + i: n_out + o for i, o in ride.aliases.items()}
        params = _params(*(("arbitrary",) * len(grid)))
    specs = dict(grid=grid, in_specs=list(in_specs) + extra_in, out_specs=list(out_specs) + extra_out,
                 scratch_shapes=list(scratch_shapes) + extra_scr)
    if prefetch is not None:
        specs = dict(grid_spec=pltpu.PrefetchScalarGridSpec(num_scalar_prefetch=1, **specs))
        args = (prefetch,) + tuple(args)
    outs = pl.pallas_call(kernel_body, name=name, out_shape=list(out_shape) + extra_shape,
                          input_output_aliases=aliases, compiler_params=params, **specs,
                          )(*args, *(ride.operands if ride is not None else ()))
    return outs if ride is None else (outs[:n_out], outs[n_out:])


def alone(ride, *, name):
    def body(*refs):
        n = len(ride.operands)
        stages = ride.stages(refs[:n], refs[n:n + len(ride.out_shape)], refs[n + len(ride.out_shape):])
        for stage in stages:
            if stage is not None:
                stage()

    return pl.pallas_call(
        body, name=name, in_specs=[HBM_SPEC] * len(ride.operands), out_specs=[HBM_SPEC] * len(ride.out_shape),
        out_shape=list(ride.out_shape), input_output_aliases=dict(ride.aliases), scratch_shapes=list(ride.sems),
    )(*ride.operands)


def _two_pieces(h, part):
    lo, n = (0, h // BF16_ROWS) if part is None else part
    assert n >= 2, (h, part)
    return (lo, n // 2), (lo + n // 2, n - n // 2)


def gather_ride(wholes, metas, small=None):
    n = len(wholes)
    operands, out_shape = list(wholes), [SDS(s.shape, s.dtype) for s in wholes]
    sems = [DMA((n, 4)), DMA((n, 4)), DMA((n, 4)), DMA((n, 4))]
    if small is not None:
        operands.append(small)
        out_shape.append(SDS((N_CHIPS,) + small.shape, small.dtype))
        sems += [DMA((3,)), DMA((3,)), DMA(())]

    def make(ins, outs, sem):
        send1, recv1, send2, recv2 = sem[:4]
        x, y, c = _place()
        p = 2 * x + y
        chips = _other_chips(x, y)
        across_x, across_y, across_both = [2 * qx + qy for qx, qy in chips]
        me, sibling = (x, y, c), (x, y, 1 - c)

        def region(t, q, half, piece=None):
            kind, quarter, part = metas[t]
            if piece is not None:
                part = _two_pieces(quarter[0] // 2, part)[piece]
            return _half_of_quarter(outs[t], kind, quarter, part, q, half)

        first, second, arriving = [], [], []
        landing, passing = [[], [], [], []], [[], [], [], []]
        for j, (qx, qy) in enumerate(chips):
            if small is not None:
                q = 2 * qx + qy
                first.append(_remote(ins[n], outs[n].at[p], sem[4].at[j], sem[5].at[j], (qx, qy, c)))
                arriving.append(_remote(outs[n].at[q], outs[n].at[q], sem[4].at[j], sem[5].at[j], me))
        for t in range(n):
            mine = region(t, p, c)
            for j in range(2):
                first.append(_remote(mine, mine, send1.at[t, j], recv1.at[t, j], chips[j] + (c,)))
            lands = [(across_x, None), (across_y, None), (across_both, 0), (across_both, 1)]
            for k, (q, piece) in enumerate(lands):
                landed, theirs = region(t, q, c, piece), region(t, q, 1 - c, piece)
                landing[k].append(_remote(landed, landed, send1.at[t, k], recv1.at[t, k], me))
                passing[k].append(_remote(landed, landed, send2.at[t, k], recv2.at[t, k], sibling))
                arriving.append(_remote(theirs, theirs, send2.at[t, k], recv2.at[t, k], me))
            onward = region(t, across_x, c, 0)
            second.append(_remote(onward, onward, send1.at[t, 2], recv1.at[t, 2], chips[1] + (c,)))
            onward = region(t, across_y, c, 1)
            second.append(_remote(onward, onward, send1.at[t, 3], recv1.at[t, 3], chips[0] + (c,)))
        local = [] if small is None else [pltpu.make_async_copy(ins[n], outs[n].at[p], sem[6])]

        def start():
            for cp in local + first:
                cp.start()

        def relay():
            for k in range(2):
                for t in range(n):
                    landing[k][t].wait_recv()
                    second[2 * t + k].start()
                    passing[k][t].start()

        def relay_again():
            for k in range(2, 4):
                for t in range(n):
                    landing[k][t].wait_recv()
                    passing[k][t].start()

        def finish():
            for cp in arriving:
                cp.wait_recv()
            for cp in first + second + sum(passing, []):
                cp.wait_send()
            for cp in local:
                cp.wait()

        return start, relay, relay_again, finish

    return Ride(operands, out_shape, {t: t for t in range(n)}, sems, make)


def chip_ride(sums, metas, small=None, earlier=None):
    n = len(sums)
    operands = list(sums)
    out_shape = [SDS((3, s.shape[1], quarter[1]), s.dtype) for s, (_, quarter, _) in zip(sums, metas)]
    sems = [DMA((n, 3)), DMA((n, 3))] if n else []
    if small is not None:
        operands.append(small)
        out_shape.append(SDS((8,) + small.shape, small.dtype))
        sems += [DMA((7,)), DMA((7,)), DMA(())]
    aliases = {}
    for t, buffer in enumerate(earlier or [None] * n):
        if buffer is not None:
            aliases[len(operands)] = t
            operands.append(buffer)

    def make(ins, outs, sem):
        x, y, c = _place()
        cps = []
        for j, (qx, qy) in enumerate(_other_chips(x, y)):
            q = 2 * qx + qy
            for t in range(n):
                kind, (_, ws), part = metas[t]
                rows = pl.ds(*_rows_of(ins[t].shape[1], part))
                if kind == "row":
                    src = ins[t].at[q, rows]
                elif kind == "col":
                    src = ins[t].at[0, rows, pl.ds(pl.multiple_of(q * ws, 128), ws)]
                else:
                    src = ins[t].at[q // 2, rows, pl.ds(pl.multiple_of((q % 2) * ws, 128), ws)]
                cps.append(_remote(src, outs[t].at[j, rows], sem[0].at[t, j], sem[1].at[t, j], (qx, qy, c)))
        local = []
        if small is not None:
            ssend, srecv, lsem = sem[2 * bool(n):2 * bool(n) + 3]
            local.append(pltpu.make_async_copy(ins[n], outs[n].at[0], lsem))
            for k in range(1, 8):
                peer = (x ^ (k >> 2 & 1), y ^ (k >> 1 & 1), c ^ (k & 1))
                cps.append(_remote(ins[n], outs[n].at[k], ssend.at[k - 1], srecv.at[k - 1], peer))

        def start():
            for cp in local + cps:
                cp.start()

        def finish():
            for cp in cps + local:
                cp.wait()

        return start, finish

    return Ride(operands, out_shape, aliases, sems, make)


def pair_ride(grads):
    n = len(grads)

    def make(ins, outs, sem):
        x, y, c = _place()
        cps = [_remote(ins[t].at[:, 1 - c], outs[t], sem[0].at[t], sem[1].at[t], (x, y, 1 - c)) for t in range(n)]

        def start():
            for cp in cps:
                cp.start()

        def finish():
            for cp in cps:
                cp.wait()

        return start, finish

    return Ride(list(grads), [SDS((g.shape[0],) + g.shape[2:], g.dtype) for g in grads], {}, [DMA((n,)), DMA((n,))],
                make)


def half_ride(quarters):
    n = len(quarters)

    def make(ins, outs, sem):
        x, y, c = _place()
        sends = [_remote(outs[t].at[c], outs[t].at[c], sem[0].at[t], sem[1].at[t], (x, y, 1 - c)) for t in range(n)]

        def start():
            for cp in sends:
                cp.start()

        def finish():
            for t in range(n):
                theirs = outs[t].at[1 - c]
                _remote(theirs, theirs, sem[0].at[t], sem[1].at[t], (x, y, c)).wait_recv()
            for cp in sends:
                cp.wait_send()

        return start, finish

    return Ride(list(quarters), [SDS(q.shape, q.dtype) for q in quarters], {t: t for t in range(n)},
                [DMA((n,)), DMA((n,))], make)


CAST_STEPS = 4


def cast_quarters(sources, p_arr, *, name, ride=None):
    n = len(sources)
    in_specs, out_specs, out_shape = [], [], []
    for w, layer, kind in sources:
        _, r, ws = w.shape
        tr = r // CAST_STEPS
        assert tr % BF16_ROWS == 0, w.shape
        in_specs.append(pl.BlockSpec((None, tr, ws), lambda i, p_ref, layer=layer: (layer, i, 0)))
        out_specs.append(pl.BlockSpec((tr, ws), (lambda i, p_ref: (p_ref[0] * CAST_STEPS + i, 0)) if kind == "row"
                                      else (lambda i, p_ref: (i, p_ref[0]))))
        out_shape.append(SDS(_full_shape(kind, (r, ws)), BF16))

    def body(p_ref, *refs):
        for w_ref, o_ref in zip(refs[:n], refs[n:]):
            o_ref[...] = w_ref[...].astype(BF16)

    return _call(body, name=name, grid=(CAST_STEPS,), in_specs=in_specs, out_specs=out_specs, out_shape=out_shape,
                 semantics=("parallel",), args=[w for w, _, _ in sources], ride=ride, prefetch=p_arr)


def pair_add(owns, gots, c_arr, *, name):
    n = len(owns)
    A, _, h, _ = owns[0].shape
    assert all(own.shape[:3] == (A, 2, h) for own in owns), [own.shape for own in owns]
    th = _row_tile(h, max(BF16_ROWS, (3 << 19) // sum(own.shape[3] for own in owns)), BF16_ROWS)

    def body(c_ref, *refs):
        for a_ref, b_ref, o_ref in zip(refs[:n], refs[n:2 * n], refs[2 * n:]):
            o_ref[...] = (a_ref[...].astype(F32) + b_ref[...].astype(F32)).astype(BF16)

    mine = lambda W: pl.BlockSpec((None, None, th, W), lambda q, i, c_ref: (q, c_ref[0], i, 0))
    theirs = lambda W: pl.BlockSpec((None, th, W), lambda q, i, c_ref: (q, i, 0))
    widths = [own.shape[3] for own in owns]
    return pl.pallas_call(
        body, name=name,
        grid_spec=pltpu.PrefetchScalarGridSpec(
            num_scalar_prefetch=1, grid=(A, h // th),
            in_specs=[mine(W) for W in widths] + [theirs(W) for W in widths],
            out_specs=[theirs(W) for W in widths]),
        out_shape=[SDS((A, h, W), BF16) for W in widths],
        compiler_params=_params("parallel", "parallel"),
    )(c_arr, *owns, *gots)


REDUCE_STEPS = 2


def chip_reduce(sums, got, kinds, pc_arr, *, name, ride=None):
    n = len(sums)
    mine = {"row": lambda i, pc_ref: (pc_ref[0], i, 0), "col": lambda i, pc_ref: (0, i, pc_ref[0]),
            "split": lambda i, pc_ref: (pc_ref[0] // 2, i, pc_ref[0] % 2)}
    a_specs, b_specs, o_specs, out_shape = [], [], [], []
    for g, kind in zip(got, kinds):
        _, h, ws = g.shape
        th = h // REDUCE_STEPS
        assert th % BF16_ROWS == 0, g.shape
        a_specs.append(pl.BlockSpec((None, th, ws), mine[kind]))
        b_specs.append(pl.BlockSpec((3, th, ws), lambda i, pc_ref: (0, i, 0)))
        o_specs.append(pl.BlockSpec((None, th, ws), lambda i, pc_ref: (pc_ref[1], i, 0)))
        out_shape.append(SDS((2, h, ws), F32))

    def body(pc_ref, *refs):
        for a_ref, b_ref, o_ref in zip(refs[:n], refs[n:2 * n], refs[2 * n:]):
            o_ref[...] = ((a_ref[...].astype(F32) + b_ref[0].astype(F32)) + b_ref[1].astype(F32)) + b_ref[2].astype(F32)

    return _call(body, name=name, grid=(REDUCE_STEPS,), in_specs=a_specs + b_specs, out_specs=o_specs,
                 out_shape=out_shape, semantics=("parallel",), args=list(sums) + list(got), prefetch=pc_arr, ride=ride)


def small_reduce(blocks, me_arr):
    _, rows, D = blocks.shape

    def body(me_ref, b_ref, o_ref):
        me = me_ref[0]
        total = b_ref[me]
        for d in range(1, 8):
            total = total + b_ref[d ^ me]
        o_ref[...] = total

    return pl.pallas_call(
        body, name="small_reduce",
        grid_spec=pltpu.PrefetchScalarGridSpec(
            num_scalar_prefetch=1, grid=(1,),
            in_specs=[pl.BlockSpec((8, rows, D), lambda i, me_ref: (0, 0, 0))],
            out_specs=pl.BlockSpec((rows, D), lambda i, me_ref: (0, 0))),
        out_shape=SDS((rows, D), F32),
        compiler_params=_params("arbitrary"),
    )(me_arr, blocks)


def _adam(w, g, m, v):
    m_new = ADAM_B1 * m + (1.0 - ADAM_B1) * g
    v_new = ADAM_B2 * v + (1.0 - ADAM_B2) * (g * g)
    m_hat = m_new / (1.0 - ADAM_B1 ** ADAM_STEP)
    v_hat = v_new / (1.0 - ADAM_B2 ** ADAM_STEP)
    return -ADAM_LR * (m_hat / (jnp.sqrt(v_hat) + ADAM_EPS) + ADAM_WD * w), m_new, v_new


def adamw_rows(block, p_arr, leaves, *, name):
    L = len(leaves)

    def body(p_ref, b_ref, *refs):
        outs = refs[3 * L:]
        for i, (w, _, _, row, sharded) in enumerate(leaves):
            n, width = w.shape[-2:]
            cols = pl.ds(pl.multiple_of(p_ref[0] * width, 128), width) if sharded else slice(0, width)
            g = b_ref[row:row + n, cols].reshape(w.shape)
            results = (g,) + _adam(refs[i][...], g, refs[L + i][...], refs[2 * L + i][...])
            for o_ref, value in zip(outs[4 * i:4 * i + 4], results):
                o_ref[...] = value

    whole = lambda a: pl.BlockSpec(a.shape, lambda i, p_ref, nd=len(a.shape): (0,) * nd)
    arrays = [leaf[k] for k in range(3) for leaf in leaves]
    shapes = [SDS(leaf[0].shape, F32) for leaf in leaves for _ in range(4)]
    outs = pl.pallas_call(
        body, name=name,
        grid_spec=pltpu.PrefetchScalarGridSpec(
            num_scalar_prefetch=1, grid=(1,), in_specs=[whole(block)] + [whole(a) for a in arrays],
            out_specs=[whole(s) for s in shapes]),
        out_shape=shapes, compiler_params=_params("arbitrary"),
    )(p_arr, block, *arrays)
    return [outs[4 * i:4 * i + 4] for i in range(L)]


def adamw_like(ws, gs, ms, vs, *, name):
    n = len(ws)
    r = ws[0].shape[1]
    assert all(w.shape[:2] == (1, r) for w in ws), [w.shape for w in ws]
    tr = _row_tile(r, 128)
    cols = [w.shape[2] for w in ws]

    def body(*refs):
        for i in range(n):
            g = refs[3 * n + i][...]
            g_out, d_out, m_out, v_out = refs[4 * n + 4 * i:4 * n + 4 * i + 4]
            g_out[...] = g
            d_out[...], m_out[...], v_out[...] = _adam(refs[i][...], g, refs[n + i][...], refs[2 * n + i][...])

    full = lambda c: pl.BlockSpec((None, tr, c), lambda i: (0, i, 0))
    outs = pl.pallas_call(
        body, name=name, grid=(r // tr,),
        in_specs=[full(c) for c in cols] * 3 + [pl.BlockSpec((tr, c), lambda i: (i, 0)) for c in cols],
        out_specs=[full(c) for c in cols for _ in range(4)],
        out_shape=[SDS(w.shape, F32) for w in ws for _ in range(4)],
        compiler_params=_params("arbitrary"),
    )(*ws, *ms, *vs, *gs)
    return [outs[4 * i:4 * i + 4] for i in range(n)]


def adamw(w, gs, m, v, *, name):
    L, r, cols = w.shape
    tr = _row_tile(r, 256)
    nt = r // tr

    def body(*refs):
        w_ref, m_ref, v_ref = refs[:3]
        g_refs = refs[3:3 + L]
        g_out, d_out, m_out, v_out = refs[3 + L:]
        layer = pl.program_id(0)
        g = g_refs[0][...]
        for l in range(1, L):
            g = jnp.where(layer == l, g_refs[l][...], g)
        g_out[...] = g
        d_out[...], m_out[...], v_out[...] = _adam(w_ref[...], g, m_ref[...], v_ref[...])

    full = pl.BlockSpec((None, tr, cols), lambda l, i: (l, i, 0))
    g_spec = lambda l0: pl.BlockSpec((tr, cols), lambda l, i: (jnp.where(l == l0, i, jnp.where(l < l0, 0, nt - 1)), 0))
    return pl.pallas_call(
        body, name=name, grid=(L, nt),
        in_specs=[full, full, full] + [g_spec(l0) for l0 in range(L)],
        out_specs=[full] * 4,
        out_shape=[SDS(w.shape, F32)] * 4,
        compiler_params=_params("arbitrary", "arbitrary"),
    )(w, m, v, *gs)


def _rms_r(xf):
    return lax.rsqrt(jnp.mean(xf * xf, axis=-1, keepdims=True) + EPS)


def _rmsnorm_bwd(xf, g, dy):
    r = _rms_r(xf)
    xh = xf * r
    gd = g * dy
    return r * (gd - xh * jnp.mean(xh * gd, axis=-1, keepdims=True)), xh


def _dot(a, b):
    return jnp.dot(a, b, preferred_element_type=F32)


def _dot_nt(a, b):
    return lax.dot_general(a, b, (((1,), (1,)), ((), ())), preferred_element_type=F32)


def _dot_tn(a, b):
    return lax.dot_general(a, b, (((0,), (0,)), ((), ())), preferred_element_type=F32)


def _accumulate(ref, first, value):
    @pl.when(first)
    def _():
        ref[...] = value

    @pl.when(jnp.logical_not(first))
    def _():
        ref[...] += value


def norm_matmul(x, g, w, *, tn, split, name, ride=None, tm=ROW_TILE):
    T, D = x.shape
    N = w.shape[1]
    per = N // split // tn

    def body(x_ref, g_ref, w_ref, o_ref, xn_ref):
        @pl.when(pl.program_id(1) == 0)
        def _():
            xf = x_ref[...].astype(F32)
            xn_ref[...] = (xf * _rms_r(xf) * g_ref[...]).astype(BF16)

        o_ref[...] = _dot(xn_ref[...], w_ref[...]).astype(BF16)

    return _call(
        body, name=name, grid=(T // tm, N // tn),
        in_specs=[pl.BlockSpec((tm, D), lambda i, j: (i, 0)),
                  pl.BlockSpec((1, D), lambda i, j: (0, 0)),
                  pl.BlockSpec((D, tn), lambda i, j: (0, j))],
        out_specs=[pl.BlockSpec((None, tm, tn), lambda i, j: (j // per, i, j % per)),
                   pl.BlockSpec((tm, D), lambda i, j: (i, 0))],
        out_shape=[SDS((split, T, N // split), BF16), SDS((T, D), BF16)],
        semantics=("parallel", "arbitrary"), args=(x, g, w), ride=ride)


BIG_ROW_TILE = 1024


def norm2_matmul(x, gains, weights, *, name, ride=None, tm=BIG_ROW_TILE):
    T, D = x.shape
    tm = min(tm, T)
    n = len(gains)

    def body(x_ref, *refs):
        subs = _sub_tiles(tm)
        xhs = []
        for rows in subs:
            xf = x_ref[rows, :].astype(F32)
            xhs.append(xf * _rms_r(xf))
        for g_ref, w_ref, o_ref, xn_ref in zip(refs[:n], refs[n:2 * n], refs[2 * n::2], refs[2 * n + 1::2]):
            for rows, xh in zip(subs, xhs):
                xn = (xh * g_ref[...]).astype(BF16)
                xn_ref[rows, :] = xn
                o_ref[rows, :] = _dot(xn, w_ref[...]).astype(BF16)

    row = pl.BlockSpec((tm, D), lambda i: (i, 0))
    vec = pl.BlockSpec((1, D), lambda i: (0, 0))
    out_specs, out_shape = [], []
    for w in weights:
        out_specs += [pl.BlockSpec((tm, w.shape[1]), lambda i: (i, 0)), row]
        out_shape += [SDS((T, w.shape[1]), BF16), SDS((T, D), BF16)]
    return _call(
        body, name=name, grid=(T // tm,),
        in_specs=[row] + [vec] * n + [pl.BlockSpec(w.shape, lambda i: (0, 0)) for w in weights],
        out_specs=out_specs, out_shape=out_shape, semantics=("parallel",), args=[x] + list(gains) + list(weights),
        ride=ride)


def _shift_down(prev, cur, by):
    big = jnp.concatenate([prev, cur], axis=0)
    return pltpu.roll(big, by, 0)[prev.shape[0]:]


def _shift_up(cur, nxt, by):
    big = jnp.concatenate([cur, nxt], axis=0)
    return pltpu.roll(big, big.shape[0] - by, 0)[:cur.shape[0]]


def conv_mix_out(bcx, conv_w, w_out, g_post, res, *, name, ride=None, tm=ROW_TILE):
    T, D = res.shape
    hb = tm // BF16_ROWS

    def body(b_ref, c_ref, u_ref, cp_ref, up_ref, cw_ref, w_ref, g_ref, r_ref, h_ref, z_ref, y_ref):
        i = pl.program_id(0)
        cu = c_ref[...].astype(F32) * u_ref[...].astype(F32)
        cup = cp_ref[...].astype(F32) * up_ref[...].astype(F32)
        cup = jnp.where(i == 0, 0.0, cup)
        cv = (cw_ref[0:1, :] * _shift_down(cup, cu, 2) + cw_ref[1:2, :] * _shift_down(cup, cu, 1)
              + cw_ref[2:3, :] * cu)
        y = (b_ref[...].astype(F32) * cv).astype(BF16)
        y_ref[...] = y
        z = _dot(y, w_ref[...])
        z_ref[...] = z.astype(BF16)
        h_ref[...] = (r_ref[...] + z * _rms_r(z) * g_ref[...]).astype(STREAM)

    tile = lambda col: pl.BlockSpec((tm, D), lambda i: (i, col))
    halo = lambda col: pl.BlockSpec((BF16_ROWS, D), lambda i: (jnp.maximum(i * hb - 1, 0), col))
    row = pl.BlockSpec((tm, D), lambda i: (i, 0))
    return _call(
        body, name=name, grid=(T // tm,),
        in_specs=[tile(0), tile(1), tile(2), halo(1), halo(2),
                  pl.BlockSpec((3, D), lambda i: (0, 0)),
                  pl.BlockSpec((D, D), lambda i: (0, 0)),
                  pl.BlockSpec((1, D), lambda i: (0, 0)), row],
        out_specs=[row, row, row],
        out_shape=[SDS((T, D), STREAM), SDS((T, D), BF16), SDS((T, D), BF16)],
        semantics=("parallel",), args=(bcx, bcx, bcx, bcx, bcx, conv_w, w_out, g_post, res), ride=ride)


def _normbwd_then_nt(dh, zf, g_ref, w_ref, dz_ref, dg_ref, o_ref, first):
    dz, zh = _rmsnorm_bwd(zf, g_ref[...], dh)
    dz = dz.astype(BF16)
    dz_ref[...] = dz
    _accumulate(dg_ref, first, jnp.sum(dh * zh, axis=0, keepdims=True))
    o_ref[...] = _dot_nt(dz, w_ref[...]).astype(BF16)


def _then_specs(then, tm, T, D):
    z, g, w = then
    K = w.shape[0]
    row = pl.BlockSpec((tm, D), lambda i: (i, 0))
    vec = pl.BlockSpec((1, D), lambda i: (0, 0))
    in_specs = [row, vec, pl.BlockSpec((K, D), lambda i: (0, 0), pipeline_mode=pl.Buffered(1))]
    out_specs = [row, vec, pl.BlockSpec((tm, K), lambda i: (i, 0))]
    out_shape = [SDS((T, D), BF16), SDS((1, D), F32), SDS((T, K), BF16)]
    return in_specs, out_specs, out_shape


def plain_mix_out(a, w, g_post, res, *, name, target=None, ride=None, tm=ROW_TILE):
    T, D = res.shape
    tm = min(tm, T)
    K = a.shape[1]
    with_loss = target is not None

    def body(a_ref, w_ref, g_ref, r_ref, *rest):
        subs = _sub_tiles(tm)
        zs = [_dot(a_ref[rows, :], w_ref[...]) for rows in subs]
        if not with_loss:
            h_ref, z_ref = rest
            for rows, z in zip(subs, zs):
                h_ref[rows, :] = (r_ref[rows, :].astype(F32) + z * _rms_r(z) * g_ref[...]).astype(STREAM)
                z_ref[rows, :] = z.astype(BF16)
            return
        t_ref, h_ref, dz_ref, dg_ref, da_ref, loss_ref = rest
        first = pl.program_id(0) == 0
        loss, dg = jnp.zeros((), F32), jnp.zeros((1, D), F32)
        for rows, z in zip(subs, zs):
            diff = r_ref[rows, :].astype(F32) + z * _rms_r(z) * g_ref[...] - t_ref[rows, :]
            dh = diff * (1.0 / D)
            h_ref[rows, :] = dh.astype(STREAM)
            loss = loss + jnp.sum(diff * diff)
            dz, zh = _rmsnorm_bwd(z, g_ref[...], dh)
            dz = dz.astype(BF16)
            dz_ref[rows, :] = dz
            dg = dg + jnp.sum(dh * zh, axis=0, keepdims=True)
            da_ref[rows, :] = _dot_nt(dz, w_ref[...]).astype(BF16)
        _accumulate(loss_ref, first, jnp.full(loss_ref.shape, 0.5 / D, F32) * loss)
        _accumulate(dg_ref, first, dg)

    row = pl.BlockSpec((tm, D), lambda i: (i, 0))
    vec = pl.BlockSpec((1, D), lambda i: (0, 0))
    in_specs = [pl.BlockSpec((tm, K), lambda i: (i, 0)), pl.BlockSpec((K, D), lambda i: (0, 0)), vec, row]
    if with_loss:
        in_specs.append(row)
        out_specs = [row, row, vec, pl.BlockSpec((tm, K), lambda i: (i, 0)), pl.BlockSpec((8, 128), lambda i: (0, 0))]
        out_shape = [SDS((T, D), STREAM), SDS((T, D), BF16), SDS((1, D), F32), SDS((T, K), BF16), SDS((8, 128), F32)]
    else:
        out_specs, out_shape = [row, row], [SDS((T, D), STREAM), SDS((T, D), BF16)]
    return _call(
        body, name=name, grid=(T // tm,), in_specs=in_specs, out_specs=out_specs, out_shape=out_shape,
        semantics=("arbitrary",), args=(a, w, g_post, res) + ((target,) if with_loss else ()), ride=ride)


def _silu_grads(d, g, u):
    sg = jax.nn.sigmoid(g)
    return d * u * (sg * (1.0 + g * (1.0 - sg))), d * (g * sg)


def _sub_tiles(tm):
    return [pl.ds(k, min(MXU_WIDTH, tm)) for k in range(0, tm, MXU_WIDTH)]


def norm_swiglu_in(x, g, w, *, name, ride=None, tm=ROW_TILE):
    T, D = x.shape
    F = w.shape[1] // 2

    def body(x_ref, g_ref, wg_ref, wu_ref, gu_ref, a_ref, xt_ref):
        subs = _sub_tiles(tm)
        xns = []
        for rows in subs:
            xf = x_ref[rows, :].astype(F32)
            xns.append(xf * _rms_r(xf) * g_ref[...])
        xbs = [xn.astype(BF16) for xn in xns]
        gates = [_dot(xb, wg_ref[...]).astype(BF16) for xb in xbs]
        ups = [_dot(xb, wu_ref[...]).astype(BF16) for xb in xbs]
        for rows, gate, up in zip(subs, gates, ups):
            gu_ref[0, rows, :] = gate
            gu_ref[1, rows, :] = up
            a_ref[rows, :] = gate * jax.nn.sigmoid(gate) * up
        for rows, xn in zip(subs, xns):
            xt_ref[:, rows] = xn.T.astype(BF16)

    half = lambda s: pl.BlockSpec((D, F), lambda i: (0, s), pipeline_mode=pl.Buffered(1))
    return _call(
        body, name=name, grid=(T // tm,),
        in_specs=[pl.BlockSpec((tm, D), lambda i: (i, 0)), pl.BlockSpec((1, D), lambda i: (0, 0)), half(0), half(1)],
        out_specs=[pl.BlockSpec((2, tm, F), lambda i: (0, i, 0)), pl.BlockSpec((tm, F), lambda i: (i, 0)),
                   pl.BlockSpec((D, tm), lambda i: (0, i))],
        out_shape=[SDS((2, T, F), BF16), SDS((T, F), BF16), SDS((D, T), BF16)],
        semantics=("parallel",), args=(x, g, w, w), ride=ride)


def swiglu_bwd_tn(xt, dact, gu, *, name, ride=None, tb=MXU_WIDTH):
    D, T = xt.shape
    F = dact.shape[1]

    def body(xt_ref, d_ref, g_ref, u_ref, o_ref):
        dg, du = _silu_grads(d_ref[...], g_ref[...], u_ref[...])
        o_ref[0] = _dot(xt_ref[...], dg).astype(BF16)
        o_ref[1] = _dot(xt_ref[...], du).astype(BF16)

    col = lambda s: pl.BlockSpec((None, T, tb), lambda j: (s, 0, j))
    out = _call(
        body, name=name, grid=(F // tb,),
        in_specs=[pl.BlockSpec((D, T), lambda j: (0, 0), pipeline_mode=pl.Buffered(1)),
                  pl.BlockSpec((T, tb), lambda j: (0, j)), col(0), col(1)],
        out_specs=[pl.BlockSpec((2, D, tb), lambda j: (0, 0, j))],
        out_shape=[SDS((2, D, F), BF16)],
        semantics=("parallel",), args=(xt, dact, gu, gu), ride=ride)
    return out[0] if ride is None else (out[0][0], out[1])


def swiglu_bwd_in(dact, gu, w, h_in, g, dh_out, then, *, name, ride=None, tm=ROW_TILE):
    T, D = h_in.shape
    F = dact.shape[1]

    def body(d_ref, gg_ref, uu_ref, wg_ref, wu_ref, h_ref, g_ref, dh_ref, z_ref, g2_ref, w2_ref,
             o_ref, dg_ref, dz_ref, dg2_ref, da_ref):
        first = pl.program_id(0) == 0
        subs = _sub_tiles(tm)
        dns = []
        for rows in subs:
            dgate, dup = _silu_grads(d_ref[rows, :], gg_ref[rows, :], uu_ref[rows, :])
            dns.append(_dot_nt(dgate, wg_ref[...]) + _dot_nt(dup, wu_ref[...]))
        dg, dg2 = jnp.zeros((1, D), F32), jnp.zeros((1, D), F32)
        for rows, dn in zip(subs, dns):
            dx, hh = _rmsnorm_bwd(h_ref[rows, :].astype(F32), g_ref[...], dn)
            dh_in = dh_ref[rows, :] + dx
            o_ref[rows, :] = dh_in.astype(STREAM)
            dg = dg + jnp.sum(dn * hh, axis=0, keepdims=True)
            dz, zh = _rmsnorm_bwd(z_ref[rows, :].astype(F32), g2_ref[...], dh_in)
            dz = dz.astype(BF16)
            dz_ref[rows, :] = dz
            dg2 = dg2 + jnp.sum(dh_in * zh, axis=0, keepdims=True)
            da_ref[rows, :] = _dot_nt(dz, w2_ref[...]).astype(BF16)
        _accumulate(dg_ref, first, dg)
        _accumulate(dg2_ref, first, dg2)

    row = pl.BlockSpec((tm, D), lambda i: (i, 0))
    vec = pl.BlockSpec((1, D), lambda i: (0, 0))
    part = lambda s: pl.BlockSpec((None, tm, F), lambda i: (s, i, 0))
    half = lambda s: pl.BlockSpec((D, F), lambda i: (0, s), pipeline_mode=pl.Buffered(1))
    then_in, then_out, then_shape = _then_specs(then, tm, T, D)
    return _call(
        body, name=name, grid=(T // tm,),
        in_specs=[pl.BlockSpec((tm, F), lambda i: (i, 0)), part(0), part(1), half(0), half(1), row, vec, row] + then_in,
        out_specs=[row, vec] + then_out,
        out_shape=[SDS((T, D), STREAM), SDS((1, D), F32)] + then_shape,
        semantics=("arbitrary",), args=(dact, gu, gu, w, w, h_in, g, dh_out) + tuple(then), ride=ride)


def rope_tables(T):
    half = ROT_DIM // 2
    inv_freq = ROPE_THETA ** (-jnp.arange(0, ROT_DIM, 2, dtype=F32) / ROT_DIM)
    ang = (jnp.arange(T, dtype=F32)[:, None] * inv_freq[None, :]).T
    cos, sin = jnp.cos(ang), jnp.sin(ang)
    rest = HEAD_DIM - ROT_DIM
    one, zero = jnp.ones((rest, T), F32), jnp.zeros((rest, T), F32)
    zh = jnp.zeros((half, T), F32)
    fac = jnp.concatenate([cos, cos, one], axis=0)
    up = jnp.concatenate([-sin, zh, zero], axis=0)
    down = jnp.concatenate([zh, sin, zero], axis=0)
    return jnp.stack([fac, up, down])


def _rope(t, tab):
    half = ROT_DIM // 2
    return t * tab[0] + pltpu.roll(t, HEAD_DIM - half, 0) * tab[1] + pltpu.roll(t, half, 0) * tab[2]


def _rope_t(d, tab):
    half = ROT_DIM // 2
    return d * tab[0] + pltpu.roll(d * tab[1], half, 0) + pltpu.roll(d * tab[2], HEAD_DIM - half, 0)


def _head(t, h):
    return t[h * HEAD_DIM:(h + 1) * HEAD_DIM]


def _band(n, group):
    kj = lax.broadcasted_iota(jnp.int32, (2 * BLOCK, BLOCK), 0)
    qi = lax.broadcasted_iota(jnp.int32, (2 * BLOCK, BLOCK), 1)
    mask = (kj > qi) & (kj <= qi + BLOCK) & ((n > 0) | (kj >= BLOCK))
    return jnp.tile(mask, (1, group))


def _attn_specs(D, kvd, nb):
    cur = lambda n: jnp.minimum(n, nb - 1)
    prev = lambda n: jnp.maximum(cur(n) - 1, 0)
    return [pl.BlockSpec((BLOCK, D), lambda n: (cur(n), 0)),
            pl.BlockSpec((BLOCK, kvd), lambda n: (prev(n), 0)),
            pl.BlockSpec((BLOCK, kvd), lambda n: (cur(n), 0)),
            pl.BlockSpec((BLOCK, kvd), lambda n: (prev(n), 1)),
            pl.BlockSpec((BLOCK, kvd), lambda n: (cur(n), 1)),
            pl.BlockSpec((3, HEAD_DIM, BLOCK), lambda n: (0, 0, prev(n))),
            pl.BlockSpec((3, HEAD_DIM, BLOCK), lambda n: (0, 0, cur(n))),
            pl.BlockSpec(memory_space=pltpu.SMEM)]


def _attn_operands(q_ref, kp_ref, k_ref, vp_ref, v_ref, tp_ref, t_ref):
    flip = lambda ref: ref[...].astype(F32).T
    tab = t_ref[...]
    kt = jnp.concatenate([flip(kp_ref), flip(k_ref)], axis=1)
    vt = jnp.concatenate([flip(vp_ref), flip(v_ref)], axis=1)
    return flip(q_ref), kt, vt, tab, jnp.concatenate([tp_ref[...], tab], axis=2)


SCORE_SCALE = 1.0 / math.sqrt(HEAD_DIM)
HEADS_TOGETHER = 4


def _group_heads(t, first, count, tab=None):
    heads = [_head(t, first + g) for g in range(count)]
    if tab is not None:
        heads = [_rope(h, tab) * SCORE_SCALE for h in heads]
    return jnp.concatenate(heads, axis=1).astype(BF16)


def _sink_row(s_ref, first, count):
    which = lax.broadcasted_iota(jnp.int32, (1, count * BLOCK), 1) // BLOCK
    row = jnp.zeros((1, count * BLOCK), F32)
    for g in range(count):
        row = jnp.where(which == g, s_ref[0, first + g], row)
    return row


def _sum_keys(t):
    return _dot(jnp.ones((8, t.shape[0]), BF16), t)[0:1]


def _softmax(scores, sink, mask):
    s = jnp.where(mask, scores.astype(BF16), NEG)
    m = jnp.maximum(jnp.max(s, axis=0, keepdims=True).astype(F32), sink).astype(BF16)
    e = jnp.exp(s - m)
    m = m.astype(F32)
    return e, m, 1.0 / (_sum_keys(e) + jnp.exp(sink - m))


def _per_head(row, count):
    return [row[:, g * BLOCK:(g + 1) * BLOCK] for g in range(count)]


def attention_fwd(q, kv, tabs, sinks, *, name, ride=None):
    T, D = q.shape
    kvd = kv.shape[1] // 2
    heads = D // HEAD_DIM
    group = heads // N_KV_HEADS

    def body(q_ref, kp_ref, k_ref, vp_ref, v_ref, tp_ref, t_ref, s_ref, o_ref, stat_ref):
        gs = HEADS_TOGETHER
        mask = _band(pl.program_id(0), gs)
        qt, kt, vt, tab, tab2 = _attn_operands(q_ref, kp_ref, k_ref, vp_ref, v_ref, tp_ref, t_ref)
        firsts = [(j, first) for j in range(N_KV_HEADS) for first in range(j * group, (j + 1) * group, gs)]
        ks = [_rope(_head(kt, j), tab2).astype(BF16) for j in range(N_KV_HEADS)]
        scores = [_dot_tn(ks[j], _group_heads(qt, first, gs, tab)) for j, first in firsts]
        soft = [_softmax(s, _sink_row(s_ref, first, gs), mask) for s, (j, first) in zip(scores, firsts)]
        outs, ms, invs = [], [], []
        for (e, m, inv), (j, first) in zip(soft, firsts):
            o = _dot(_head(vt, j).astype(BF16), e) * inv
            outs += [o[:, g * BLOCK:(g + 1) * BLOCK] for g in range(gs)]
            ms += _per_head(m, gs)
            invs += _per_head(inv, gs)
        o_ref[...] = jnp.concatenate(outs, axis=0).T.astype(BF16)
        stat_ref[0] = jnp.concatenate(ms, axis=0)
        stat_ref[1] = jnp.concatenate(invs, axis=0)

    return _call(
        body, name=name, grid=(T // BLOCK,),
        in_specs=_attn_specs(D, kvd, T // BLOCK),
        out_specs=[pl.BlockSpec((BLOCK, D), lambda n: (n, 0)), pl.BlockSpec((2, heads, BLOCK), lambda n: (0, 0, n))],
        out_shape=[SDS((T, D), BF16), SDS((2, heads, T), F32)],
        semantics=("parallel",), args=(q, kv, kv, kv, kv, tabs, tabs, sinks), ride=ride)


def attention_bwd(q, kv, tabs, sinks, do, o, stats, *, name, ride=None):
    T, D = q.shape
    kvd = kv.shape[1] // 2
    heads = D // HEAD_DIM
    group = heads // N_KV_HEADS
    nb = T // BLOCK

    def body(q_ref, kp_ref, k_ref, vp_ref, v_ref, tp_ref, t_ref, s_ref, do_ref, o_ref, stat_ref,
             dq_ref, dkv_ref, ds_ref, carry):
        n = pl.program_id(0)

        @pl.when(n == 0)
        def _():
            carry[...] = jnp.zeros_like(carry)

        @pl.when(n < nb)
        def _():
            block(n, q_ref, kp_ref, k_ref, vp_ref, v_ref, tp_ref, t_ref, s_ref, do_ref, o_ref, stat_ref,
                  dq_ref, dkv_ref, ds_ref, carry)

        @pl.when(n == nb)
        def _():
            dkv_ref[...] = carry[...].astype(BF16)

    def block(n, q_ref, kp_ref, k_ref, vp_ref, v_ref, tp_ref, t_ref, s_ref, do_ref, o_ref, stat_ref,
              dq_ref, dkv_ref, ds_ref, carry):
        gs = HEADS_TOGETHER
        mask = _band(n, gs)
        qt, kt, vt, tab, tab2 = _attn_operands(q_ref, kp_ref, k_ref, vp_ref, v_ref, tp_ref, t_ref)
        dot = do_ref[...].astype(F32).T
        odo = o_ref[...].astype(F32).T * dot
        dl_all = jnp.concatenate([jnp.sum(_head(odo, h), axis=0, keepdims=True) for h in range(heads)], axis=0)
        m_all, inv_all = stat_ref[0], stat_ref[1]
        row = lambda t, first: jnp.concatenate([t[first + g:first + g + 1] for g in range(gs)], axis=1)
        lane = lax.broadcasted_iota(jnp.int32, (8, 128), 1)
        dsink = jnp.zeros((8, 128), F32)
        firsts = [(j, first) for j in range(N_KV_HEADS) for first in range(j * group, (j + 1) * group, gs)]
        ks = [_rope(_head(kt, j), tab2).astype(BF16) for j in range(N_KV_HEADS)]
        vs = [_head(vt, j).astype(BF16) for j in range(N_KV_HEADS)]
        qs = [_group_heads(qt, first, gs, tab) for _, first in firsts]
        dos = [_group_heads(dot, first, gs) for _, first in firsts]
        scores = [_dot_tn(ks[j], q) for q, (j, _) in zip(qs, firsts)]
        dps = [_dot_tn(vs[j], do) for do, (j, _) in zip(dos, firsts)]
        ps, dscs = [], []
        for s, dp, (j, first) in zip(scores, dps, firsts):
            m, inv, dl = row(m_all, first), row(inv_all, first), row(dl_all, first)
            e = jnp.exp(jnp.where(mask, s.astype(BF16), NEG) - m.astype(BF16))
            p = e * inv.astype(BF16)
            dscs.append(p * (dp.astype(BF16) - dl.astype(BF16)))
            ps.append(p)
            weight = jnp.exp(_sink_row(s_ref, first, gs) - m) * inv * dl
            for g in range(gs):
                dsink = dsink - jnp.where(lane == first + g, jnp.sum(weight[:, g * BLOCK:(g + 1) * BLOCK]), 0.0)
        dqs = []
        dks = [jnp.zeros((HEAD_DIM, 2 * BLOCK), F32) for _ in range(N_KV_HEADS)]
        dvs = [jnp.zeros((HEAD_DIM, 2 * BLOCK), F32) for _ in range(N_KV_HEADS)]
        for p, dsc, q, do, (j, _) in zip(ps, dscs, qs, dos, firsts):
            dq = _dot(ks[j], dsc) * SCORE_SCALE
            dqs += [_rope_t(dq[:, g * BLOCK:(g + 1) * BLOCK], tab) for g in range(gs)]
            dks[j] = dks[j] + _dot_nt(q, dsc)
            dvs[j] = dvs[j] + _dot_nt(do, p)
        dks = [_rope_t(dk, tab2) for dk in dks]
        dq_ref[...] = jnp.concatenate(dqs, axis=0).T.astype(BF16)
        dkv = jnp.concatenate(dks + dvs, axis=0)
        dkv_ref[...] = (carry[...] + dkv[:, :BLOCK].T).astype(BF16)
        carry[...] = dkv[:, BLOCK:].T
        _accumulate(ds_ref, n == 0, dsink)

    cur = lambda n: jnp.minimum(n, nb - 1)
    blk = lambda w: pl.BlockSpec((BLOCK, w), lambda n: (cur(n), 0))
    return _call(
        body, name=name, grid=(nb + 1,),
        in_specs=_attn_specs(D, kvd, nb) + [blk(D), blk(D), pl.BlockSpec((2, heads, BLOCK), lambda n: (0, 0, cur(n)))],
        out_specs=[blk(D), pl.BlockSpec((BLOCK, 2 * kvd), lambda n: (jnp.maximum(n - 1, 0), 0)),
                   pl.BlockSpec((8, 128), lambda n: (0, 0))],
        out_shape=[SDS((T, D), BF16), SDS((T, 2 * kvd), BF16), SDS((8, 128), F32)],
        scratch_shapes=[pltpu.VMEM((BLOCK, 2 * kvd), F32)],
        semantics=("arbitrary",), args=(q, kv, kv, kv, kv, tabs, tabs, sinks, do, o, stats), ride=ride)


def matmul_nt_normbwd(da, w, h_in, g, dh_out, *, name, ride=None, tm=ROW_TILE):
    T, D = h_in.shape
    S, _, K = da.shape

    def body(*refs):
        da_refs, w_refs = refs[:S], refs[S:2 * S]
        h_ref, g_ref, dh_ref, o_ref, dg_ref = refs[2 * S:]
        subs = _sub_tiles(tm)
        dns = []
        for rows in subs:
            dn = _dot_nt(da_refs[0][rows, :], w_refs[0][...])
            for s in range(1, S):
                dn = dn + _dot_nt(da_refs[s][rows, :], w_refs[s][...])
            dns.append(dn)
        dg = jnp.zeros((1, D), F32)
        for rows, dn in zip(subs, dns):
            dx, hh = _rmsnorm_bwd(h_ref[rows, :].astype(F32), g_ref[...], dn)
            o_ref[rows, :] = dh_ref[rows, :] + dx
            dg = dg + jnp.sum(dn * hh, axis=0, keepdims=True)
        _accumulate(dg_ref, pl.program_id(0) == 0, dg)

    row = pl.BlockSpec((tm, D), lambda i: (i, 0))
    vec = pl.BlockSpec((1, D), lambda i: (0, 0))
    part = lambda s: pl.BlockSpec((None, tm, K), lambda i: (s, i, 0))
    cols = lambda s: pl.BlockSpec((D, K), lambda i: (0, s), pipeline_mode=pl.Buffered(1))
    return _call(
        body, name=name, grid=(T // tm,),
        in_specs=[part(s) for s in range(S)] + [cols(s) for s in range(S)] + [row, vec, row],
        out_specs=[row, vec],
        out_shape=[SDS((T, D), F32), SDS((1, D), F32)],
        semantics=("arbitrary",), args=[da] * S + [w] * S + [h_in, g, dh_out], ride=ride)


def matmuls_nt_normbwd(das, ws, h_in, gs, dh_out, then, *, name, ride=None, tm=ROW_TILE):
    T, D = h_in.shape
    tm = min(tm, T)
    n = len(das)

    def body(*refs):
        da_refs, w_refs, g_refs = refs[:n], refs[n:2 * n], refs[2 * n:3 * n]
        h_ref, dh_ref, z_ref, g2_ref, w2_ref, o_ref = refs[3 * n:3 * n + 6]
        dg_refs, (dz_ref, dg2_ref, da_ref) = refs[3 * n + 6:4 * n + 6], refs[4 * n + 6:]
        first = pl.program_id(0) == 0
        subs = _sub_tiles(tm)
        dns = [[_dot_nt(da_ref_[rows, :], w_ref[...]) for da_ref_, w_ref in zip(da_refs, w_refs)] for rows in subs]
        dgs, dg2 = [jnp.zeros((1, D), F32) for _ in range(n)], jnp.zeros((1, D), F32)
        for rows, dn_sub in zip(subs, dns):
            hf = h_ref[rows, :].astype(F32)
            r = _rms_r(hf)
            hh = hf * r
            total = dh_ref[rows, :].astype(F32)
            for b, (dn, g_ref) in enumerate(zip(dn_sub, g_refs)):
                gd = g_ref[...] * dn
                total = total + r * (gd - hh * jnp.mean(hh * gd, axis=-1, keepdims=True))
                dgs[b] = dgs[b] + jnp.sum(dn * hh, axis=0, keepdims=True)
            o_ref[rows, :] = total.astype(STREAM)
            dz, zh = _rmsnorm_bwd(z_ref[rows, :].astype(F32), g2_ref[...], total)
            dz = dz.astype(BF16)
            dz_ref[rows, :] = dz
            dg2 = dg2 + jnp.sum(total * zh, axis=0, keepdims=True)
            da_ref[rows, :] = _dot_nt(dz, w2_ref[...]).astype(BF16)
        for dg_ref, dg in zip(dg_refs + (dg2_ref,), dgs + [dg2]):
            _accumulate(dg_ref, first, dg)

    row = pl.BlockSpec((tm, D), lambda i: (i, 0))
    vec = pl.BlockSpec((1, D), lambda i: (0, 0))
    then_in, then_out, then_shape = _then_specs(then, tm, T, D)
    return _call(
        body, name=name, grid=(T // tm,),
        in_specs=[pl.BlockSpec((tm, da.shape[1]), lambda i: (i, 0)) for da in das]
        + [pl.BlockSpec(w.shape, lambda i: (0, 0)) for w in ws] + [vec] * n + [row, row] + then_in,
        out_specs=[row] + [vec] * n + then_out,
        out_shape=[SDS((T, D), STREAM)] + [SDS((1, D), F32)] * n + then_shape,
        semantics=("arbitrary",), args=list(das) + list(ws) + list(gs) + [h_in, dh_out] + list(then), ride=ride)


def matmul_tn(a, b, *, tb, name, ride=None, ta=MXU_WIDTH):
    T, Ka = a.shape
    S, _, Nb = b.shape
    per = Nb // tb

    def body(a_ref, b_ref, o_ref):
        o_ref[...] = _dot_tn(a_ref[...], b_ref[...]).astype(BF16)

    out = _call(
        body, name=name, grid=(S * per, Ka // ta),
        in_specs=[pl.BlockSpec((T, ta), lambda j, i: (0, i)),
                  pl.BlockSpec((None, T, tb), lambda j, i: (j // per, 0, j % per))],
        out_specs=[pl.BlockSpec((ta, tb), lambda j, i: (i, j))],
        out_shape=[SDS((Ka, S * Nb), BF16)],
        semantics=("parallel", "parallel"), args=(a, b), ride=ride)
    return out[0] if ride is None else (out[0][0], out[1])


def conv_bwd(dy, bcx, conv_w, *, name, ride=None, tm=ROW_TILE):
    T, D = dy.shape
    nt = T // tm
    hb = tm // BF16_ROWS
    last = T // BF16_ROWS - 1

    def body(dy_ref, dyn_ref, b_ref, bn_ref, c_ref, u_ref, cp_ref, up_ref, cw_ref, o_ref, dw_ref):
        i = pl.program_id(0)
        c, u = c_ref[...].astype(F32), u_ref[...].astype(F32)
        cu = c * u
        cup = jnp.where(i == 0, 0.0, cp_ref[...].astype(F32) * up_ref[...].astype(F32))
        cu1, cu2 = _shift_down(cup, cu, 1), _shift_down(cup, cu, 2)
        w0, w1, w2 = cw_ref[0:1, :], cw_ref[1:2, :], cw_ref[2:3, :]
        dyf = dy_ref[...].astype(F32)
        o_ref[:, 0:D] = (dyf * (w0 * cu2 + w1 * cu1 + w2 * cu)).astype(BF16)
        dcv = dyf * b_ref[...].astype(F32)
        dcvn = jnp.where(i == nt - 1, 0.0, dyn_ref[...].astype(F32) * bn_ref[...].astype(F32))
        dcu = w2 * dcv + w1 * _shift_up(dcv, dcvn, 1) + w0 * _shift_up(dcv, dcvn, 2)
        o_ref[:, D:2 * D] = (dcu * u).astype(BF16)
        o_ref[:, 2 * D:3 * D] = (dcu * c).astype(BF16)
        row = lax.broadcasted_iota(jnp.int32, (8, D), 0)
        dw = jnp.zeros((8, D), F32)
        for tap, t in enumerate((cu2, cu1, cu)):
            dw = jnp.where(row == tap, jnp.sum(dcv * t, axis=0, keepdims=True), dw)
        _accumulate(dw_ref, i == 0, dw)

    tile = lambda col: pl.BlockSpec((tm, D), lambda i: (i, col))
    prev = lambda col: pl.BlockSpec((BF16_ROWS, D), lambda i: (jnp.maximum(i * hb - 1, 0), col))
    nxt = lambda col: pl.BlockSpec((BF16_ROWS, D), lambda i: (jnp.minimum((i + 1) * hb, last), col))
    return _call(
        body, name=name, grid=(nt,),
        in_specs=[tile(0), nxt(0), tile(0), nxt(0), tile(1), tile(2), prev(1), prev(2),
                  pl.BlockSpec((3, D), lambda i: (0, 0))],
        out_specs=[pl.BlockSpec((tm, 3 * D), lambda i: (i, 0)), pl.BlockSpec((8, D), lambda i: (0, 0))],
        out_shape=[SDS((T, 3 * D), BF16), SDS((8, D), F32)],
        semantics=("arbitrary",), args=(dy, dy, bcx, bcx, bcx, bcx, bcx, bcx, conv_w), ride=ride)


class NoTraffic:
    def ride(self, kernel_name):
        return None

    def landed(self, kernel_name, results, wts):
        pass

    def grad(self, key, value):
        pass


def local_step(x, target, wts, vec, traffic):
    T, D = x.shape
    tabs = rope_tables(T)
    small = {}

    def run(builder, *args, name, **kw):
        ride = traffic.ride(name)
        if ride is None:
            return builder(*args, name=name, **kw)
        out, extra = builder(*args, name=name, ride=ride, **kw)
        traffic.landed(name, extra, wts)
        return out

    bcx, xn1 = run(norm_matmul, x, vec["a_pre"], wts["w_in"], tn=3 * D, split=1, name="a_in")
    bcx = bcx[0]
    h1, z0, y0 = run(conv_mix_out, bcx, vec["conv_w"], wts["w_out"], vec["a_post"], x, name="a_out")
    gu0, act0, xt2 = run(norm_swiglu_in, h1, vec["ffn_pre0"], wts["gu0"], name="ffn0_in")
    h2, z1 = run(plain_mix_out, act0, wts["wd0"], vec["ffn_post0"], h1, name="ffn0_out")
    kvp, xkv, qp, xq = run(norm2_matmul, h2, [vec["kv_norm"], vec["b_pre"]], [wts["w_kv"], wts["w_q"]],
                           name="kvq_in")
    attn, attn_stats = run(attention_fwd, qp, kvp, tabs, vec["sinks"], name="attn_fwd")
    h3, z2 = plain_mix_out(attn, wts["w_o"], vec["b_post"], h2, name="attn_out", tm=BIG_ROW_TILE)
    gu1, act1, xt3 = run(norm_swiglu_in, h3, vec["ffn_pre1"], wts["gu1"], name="ffn1_in")
    dy, dz3, small["ffn_post1"], dact1, loss = plain_mix_out(act1, wts["wd1"], vec["ffn_post1"], h3, name="ffn1_out",
                                                             target=target)

    def ffn_bwd(layer, dz, dact, gu, act, xt, h_in, dh, then, gu_first):
        tag = "ffn%d" % layer
        dwd = lambda: traffic.grad("wd%d" % layer, run(matmul_tn, act, dz[None], tb=D, name=tag + "_dwd"))
        dwgu = lambda: traffic.grad("gu%d" % layer, run(swiglu_bwd_tn, xt, dact, gu, name=tag + "_dwgu"))
        for step in ((dwgu, dwd) if gu_first else (dwd, dwgu)):
            step()
        dh_in, small["ffn_pre%d" % layer], dz_, dg_, da_ = run(
            swiglu_bwd_in, dact, gu, wts["gu%d" % layer], h_in, vec["ffn_pre%d" % layer], dh, then,
            name=tag + "_in_bwd")
        return dh_in, dz_, dg_, da_

    dh3, dz2, small["b_post"], dattn = ffn_bwd(1, dz3, dact1, gu1, act1, xt3, h3, dy,
                                               (z2, vec["b_post"], wts["w_o"]), gu_first=False)
    traffic.grad("w_o", matmul_tn(attn, dz2[None], tb=D, name="attn_dwo"))
    dq, dkv, small["sinks"] = run(attention_bwd, qp, kvp, tabs, vec["sinks"], dattn, attn, attn_stats,
                                  name="attn_bwd")
    traffic.grad("w_q", matmul_tn(xq, dq[None], tb=D, name="attn_dwq"))
    traffic.grad("w_kv", matmul_tn(xkv, dkv[None], tb=dkv.shape[1], name="attn_dwkv"))
    dh2, small["b_pre"], small["kv_norm"], dz1, small["ffn_post0"], dact0 = run(
        matmuls_nt_normbwd, [dq, dkv], [wts["w_q"], wts["w_kv"]], h2, [vec["b_pre"], vec["kv_norm"]], dh3,
        (z1, vec["ffn_post0"], wts["wd0"]), name="qkv_in_bwd")
    dh1, dz0, small["a_post"], dyc = ffn_bwd(0, dz1, dact0, gu0, act0, xt2, h1, dh2,
                                             (z0, vec["a_post"], wts["w_out"]), gu_first=True)
    traffic.grad("w_out", run(matmul_tn, y0, dz0[None], tb=D, name="a_dwout"))
    dbcx, small["conv_w"] = run(conv_bwd, dyc, bcx, vec["conv_w"], name="a_conv_bwd")
    traffic.grad("w_in", run(matmul_tn, xn1, dbcx[None], tb=3 * D // 2, name="a_dwin"))
    dx, small["a_pre"] = run(matmul_nt_normbwd, dbcx[None], wts["w_in"], x, vec["a_pre"], dh1, name="a_in_bwd")
    return loss, dx, small


SMALL_ROWS = 16
LOSS_ROW = 13

WHOLE = None
GATHER_PLAN = {"cast_rest": [("w_in", WHOLE)],
               "a_in": [("w_out", WHOLE), ("gu0", (0, 18))],
               "a_out": [("gu0", (18, 14))],
               "ffn0_in": [("wd0", WHOLE), ("w_kv", WHOLE), ("w_q", WHOLE), ("gu1", (0, 4))],
               "ffn0_out": [("w_o", WHOLE), ("gu1", (4, 8))],
               "attn_fwd": [("gu1", (12, 20))],
               "ffn1_in": [("wd1", WHOLE)]}
PAIR_PLAN = {"ffn1_dwgu": ["wd1"], "ffn1_in_bwd": ["gu1"], "qkv_in_bwd": ["w_o", "w_q", "w_kv"],
             "ffn0_dwd": ["gu0"], "ffn0_in_bwd": ["wd0"], "a_conv_bwd": ["w_out"], "chip_reduce_early": ["w_in"]}
CHIP_PLAN = {"ffn1_in_bwd": [("wd1", WHOLE)], "attn_bwd": [("gu1", WHOLE)],
             "ffn0_dwgu": [("w_o", WHOLE), ("w_q", WHOLE), ("w_kv", WHOLE)],
             "ffn0_in_bwd": [("gu0", WHOLE)], "a_conv_bwd": [("wd0", (0, 12))],
             "a_dwin": [("wd0", (12, 10)), ("w_out", WHOLE)], "a_in_bwd": [("w_in", WHOLE)]}
HALF_PLAN = {"a_in_bwd": ["gu0", "gu1", "wd0", "wd1", "w_kv", "w_q", "w_o", "w_out"]}
GRAD_KIND = dict(KIND, gu0="split", gu1="split")


class Traffic:
    def __init__(self, wholes, quarter, c_arr, pc_arr):
        self.wholes, self.quarter, self.c_arr, self.pc_arr = wholes, quarter, c_arr, pc_arr
        self.views, self.sums, self.got = {}, {}, {}
        self.reduced = {}
        self.stages = {}

    def reduce(self, keys, name):
        args = ([self.sums[k] for k in keys], [self.got[k] for k in keys], [GRAD_KIND[k] for k in keys], self.pc_arr)
        if name not in PAIR_PLAN:
            return chip_reduce(*args, name=name)
        pairs = PAIR_PLAN[name]
        out, got = chip_reduce(*args, name=name, ride=pair_ride([self.views[k] for k in pairs]))
        self.pair_sums(pairs, got)
        return out

    def pair_sums(self, keys, got):
        groups = {}
        for k, theirs in zip(keys, got):
            groups.setdefault(self.views[k].shape[:3], []).append((k, theirs))
        for members in groups.values():
            names = [k for k, _ in members]
            sums = pair_add([self.views[k] for k in names], [theirs for _, theirs in members], self.c_arr,
                            name="pair_add_" + "_".join(names))
            self.sums.update(zip(names, sums))

    def ride(self, name, small=None):
        rides, stages = [], []
        if name in GATHER_PLAN:
            plan = GATHER_PLAN[name]
            rides.append(gather_ride([self.wholes[k] for k, _ in plan],
                                     [(KIND[k], self.quarter[k], part) for k, part in plan], small))
            stages.append(("gather", [k for k, _ in plan]))
        if name in HALF_PLAN:
            keys = HALF_PLAN[name]
            rides.append(half_ride(self.reduce(keys, "chip_reduce_early")))
            stages.append(("half", keys))
        if name in CHIP_PLAN:
            plan = CHIP_PLAN[name]
            rides.append(chip_ride([self.sums[k] for k, _ in plan],
                                   [(GRAD_KIND[k], self.quarter[k], part) for k, part in plan],
                                   earlier=[self.got.get(k) for k, _ in plan]))
            stages.append(("chip", [k for k, _ in plan]))
        if name in PAIR_PLAN:
            keys = PAIR_PLAN[name]
            rides.append(pair_ride([self.views[k] for k in keys]))
            stages.append(("pair", keys))
        self.stages[name] = stages
        return join(rides)

    def landed(self, name, results, wts):
        results = list(results)
        for stage, keys in self.stages[name]:
            mine, results = results[:len(keys)], results[len(keys):]
            if stage == "gather":
                for k, whole in zip(keys, mine):
                    self.wholes[k] = wts[k] = whole
            elif stage == "chip":
                self.got.update(zip(keys, mine))
            elif stage == "half":
                self.reduced.update(zip(keys, mine))
            else:
                self.pair_sums(keys, mine)

    def grad(self, key, value):
        r, ws = self.quarter[key]
        view = {"row": (N_CHIPS, 2, r // 2, ws), "col": (1, 2, r // 2, N_CHIPS * ws), "split": (2, 2, r // 2, 2 * ws)}
        self.views[key] = value.reshape(view[GRAD_KIND[key]])


def kernel(x, a_pre_norm, a_w_in, a_conv_w, a_w_out, a_post_norm, ffn_pre_norm, ffn_w_gate_up, ffn_w_down, ffn_post_norm, kv_norm, w_kv, b_pre_norm, b_w_q, b_sinks, b_w_o, b_post_norm, loss_target, m_a_pre_norm, m_a_w_in, m_a_conv_w, m_a_w_out, m_a_post_norm, m_ffn_pre_norm, m_ffn_w_gate_up, m_ffn_w_down, m_ffn_post_norm, m_kv_norm, m_w_kv, m_b_pre_norm, m_b_w_q, m_b_sinks, m_b_w_o, m_b_post_norm, v_a_pre_norm, v_a_w_in, v_a_conv_w, v_a_w_out, v_a_post_norm, v_ffn_pre_norm, v_ffn_w_gate_up, v_ffn_w_down, v_ffn_post_norm, v_kv_norm, v_w_kv, v_b_pre_norm, v_b_w_q, v_b_sinks, v_b_w_o, v_b_post_norm):
    T, D = x.shape[1], x.shape[2]
    xi, yi, ci = _place()
    p = 2 * xi + yi
    p_arr = jnp.reshape(p, (1,)).astype(jnp.int32)
    c_arr = jnp.reshape(ci, (1,)).astype(jnp.int32)
    pc_arr = jnp.stack([p, ci]).astype(jnp.int32)
    me_arr = jnp.reshape(4 * xi + 2 * yi + ci, (1,)).astype(jnp.int32)
    qd = D // N_CHIPS

    big = {"w_in": (a_w_in, 0), "w_out": (a_w_out, 0), "gu0": (ffn_w_gate_up, 0), "gu1": (ffn_w_gate_up, 1),
           "wd0": (ffn_w_down, 0), "wd1": (ffn_w_down, 1), "w_kv": (w_kv[None], 0), "w_q": (b_w_q, 0),
           "w_o": (b_w_o, 0)}
    names = list(big)
    quarter = {k: w.shape[1:] for k, (w, _) in big.items()}
    source = lambda k: big[k] + (KIND[k],)
    traffic = Traffic(dict(zip(names[:1], cast_quarters([source(names[0])], p_arr, name="cast_first"))), quarter,
                      c_arr, pc_arr)
    small_shard = jnp.concatenate([a_pre_norm, a_post_norm, a_conv_w[0], jnp.zeros((3, qd), F32)], axis=0)
    wts = {}
    rest, (*landed, small_full) = cast_quarters([source(k) for k in names[1:]], p_arr, name="cast_rest",
                                                ride=traffic.ride("cast_rest", small_shard))
    traffic.wholes.update(zip(names[1:], rest))
    traffic.landed("cast_rest", landed, wts)
    rows = lambda k: jnp.transpose(small_full[:, k], (1, 0, 2)).reshape(-1, D)
    vec = {"a_pre": rows(slice(0, 1)), "a_post": rows(slice(1, 2)), "conv_w": rows(slice(2, 5)),
           "ffn_pre0": ffn_pre_norm[0:1], "ffn_pre1": ffn_pre_norm[1:2],
           "ffn_post0": ffn_post_norm[0:1], "ffn_post1": ffn_post_norm[1:2],
           "kv_norm": kv_norm[None], "b_pre": b_pre_norm, "b_post": b_post_norm, "sinks": b_sinks}

    loss, dx, small = local_step(x[0], loss_target[0], wts, vec, traffic)

    pad = lambda a: jnp.pad(a, ((0, 0), (0, D - a.shape[1])))
    small_block = jnp.concatenate(
        [small["a_pre"], small["a_post"], small["conv_w"][0:3], small["ffn_pre0"], small["ffn_pre1"],
         small["ffn_post0"], small["ffn_post1"], small["kv_norm"], small["b_pre"], small["b_post"],
         pad(small["sinks"][0:1]), pad(loss[0:1]), jnp.zeros((SMALL_ROWS - LOSS_ROW - 1, D), F32)], axis=0)
    late = [k for k in names if k not in traffic.reduced]
    *swapped, small_blocks = alone(join([half_ride(traffic.reduce(late, "chip_reduce_late")),
                                         chip_ride([], [], small_block)]), name="last_exchange")
    traffic.reduced.update(zip(late, swapped))
    grad = {k: traffic.reduced[k].reshape(quarter[k]) for k in names}
    small_sum = small_reduce(small_blocks, me_arr)

    out = {}
    out["a_w_in"] = adamw(a_w_in, [grad["w_in"]], m_a_w_in, v_a_w_in, name="adamw_a_w_in")
    out["ffn_w_gate_up"] = adamw(ffn_w_gate_up, [grad["gu0"], grad["gu1"]], m_ffn_w_gate_up, v_ffn_w_gate_up,
                                 name="adamw_ffn_w_gate_up")
    out["ffn_w_down"] = adamw(ffn_w_down, [grad["wd0"], grad["wd1"]], m_ffn_w_down, v_ffn_w_down,
                              name="adamw_ffn_w_down")
    out["a_w_out"], out["b_w_q"], out["b_w_o"], kv = adamw_like(
        [a_w_out, b_w_q, b_w_o, w_kv[None]], [grad["w_out"], grad["w_q"], grad["w_o"], grad["w_kv"]],
        [m_a_w_out, m_b_w_q, m_b_w_o, m_w_kv[None]], [v_a_w_out, v_b_w_q, v_b_w_o, v_w_kv[None]],
        name="adamw_256_rows")
    out["w_kv"] = [o[0] for o in kv]

    leaves = {"a_pre_norm": (a_pre_norm, m_a_pre_norm, v_a_pre_norm, 0, True),
              "a_post_norm": (a_post_norm, m_a_post_norm, v_a_post_norm, 1, True),
              "a_conv_w": (a_conv_w, m_a_conv_w, v_a_conv_w, 2, True),
              "ffn_pre_norm": (ffn_pre_norm, m_ffn_pre_norm, v_ffn_pre_norm, 5, False),
              "ffn_post_norm": (ffn_post_norm, m_ffn_post_norm, v_ffn_post_norm, 7, False),
              "kv_norm": (kv_norm[None], m_kv_norm[None], v_kv_norm[None], 9, False),
              "b_pre_norm": (b_pre_norm, m_b_pre_norm, v_b_pre_norm, 10, False),
              "b_post_norm": (b_post_norm, m_b_post_norm, v_b_post_norm, 11, False),
              "b_sinks": (b_sinks, m_b_sinks, v_b_sinks, 12, False)}
    for k, results in zip(leaves, adamw_rows(small_sum, p_arr, list(leaves.values()), name="adamw_small")):
        out[k] = [r[0] for r in results] if k == "kv_norm" else results

    order = ["a_pre_norm", "a_w_in", "a_conv_w", "a_w_out", "a_post_norm", "ffn_pre_norm", "ffn_w_gate_up",
             "ffn_w_down", "ffn_post_norm", "kv_norm", "w_kv", "b_pre_norm", "b_w_q", "b_sinks", "b_w_o",
             "b_post_norm"]
    return (small_sum[LOSS_ROW, 0], dx[None], *[out[k][0] for k in order], *[out[k][1] for k in order],
            *[out[k][2] for k in order], *[out[k][3] for k in order])
```
